```python
import math
import jax, jax.numpy as jnp
from jax import lax
import numpy as np

D_MODEL = 1024
BATCH = 8
SEQ = 8192
DEPTH = 2

HEAD_DIM = 64
ROT_DIM = HEAD_DIM // 4
ROPE_THETA = 500000.0
BLK = 128
NEG_INF = -1e30
EPS = 1e-6
A_HEADS = 8
A_CONFIGS = ((128, 1), (512, 4), (2048, 16))
B_Q_HEADS = 8
B_KV_HEADS = 2
B_GROUP = B_Q_HEADS // B_KV_HEADS
B_WINDOW = 128
C_QK_HEADS = 4
C_V_HEADS = 8
C_DK = 128
C_DV = 128
C_CONV = 4
C_CHUNK = 64
D_FF = 4 * D_MODEL
A_W = A_HEADS * HEAD_DIM
B_QW = B_Q_HEADS * HEAD_DIM
B_KVW = B_KV_HEADS * HEAD_DIM
C_QKW = C_QK_HEADS * C_DK
C_VW = C_V_HEADS * C_DV
IN_WIDTHS = (A_W, A_W, A_W, B_QW, B_KVW, B_KVW, C_QKW, C_QKW, C_VW, C_VW,
             C_V_HEADS, C_V_HEADS, D_MODEL, D_MODEL, D_MODEL)
D_IN = sum(IN_WIDTHS)
MAX_POS_OFFSET = 4096

kernel_name = "hybrid_gated_dilated_swa_deltanet_block"


def rmsnorm(x, gain):
    xf = x.astype(jnp.float32)
    y = xf * lax.rsqrt(jnp.mean(xf * xf, axis=-1, keepdims=True) + EPS)
    return (y * gain.astype(jnp.float32)).astype(x.dtype)


def l2norm(t):
    tf = t.astype(jnp.float32)
    return tf * lax.rsqrt(jnp.sum(tf * tf, axis=-1, keepdims=True) + EPS)


def rope_tables(positions, dtype):
    inv_freq = jnp.power(ROPE_THETA, -jnp.arange(0, ROT_DIM, 2, dtype=jnp.float32) / ROT_DIM)
    ang = positions.astype(jnp.float32)[..., None] * inv_freq
    return jnp.cos(ang)[:, :, None, :].astype(dtype), jnp.sin(ang)[:, :, None, :].astype(dtype)


def apply_rope(x, cos, sin):
    half = ROT_DIM // 2
    x1, x2 = x[..., :half], x[..., half:ROT_DIM]
    return jnp.concatenate([x1 * cos - x2 * sin, x2 * cos + x1 * sin, x[..., ROT_DIM:]], axis=-1)


def split_columns(u):
    outs, start = [], 0
    for width in IN_WIDTHS:
        outs.append(u[..., start:start + width])
        start += width
    return outs


def banded_attention(q, k, v, max_dist, sink=None):
    b, l, hkv, g, d = q.shape
    nb = l // BLK
    qb = q.reshape(b, nb, BLK, hkv, g, d)

    def with_prev(t):
        tb = t.reshape(b, nb, BLK, hkv, d)
        prev = jnp.concatenate([jnp.zeros_like(tb[:, :1]), tb[:, :-1]], axis=1)
        return jnp.concatenate([prev, tb], axis=2)

    kk, vv = with_prev(k), with_prev(v)
    s = jnp.einsum('bnqhgd,bnkhd->bnhgqk', qb, kk,
                   preferred_element_type=jnp.float32) * (d ** -0.5)
    qi = jnp.arange(BLK)[:, None]
    kj = jnp.arange(2 * BLK)[None, :]
    dist = BLK + qi - kj
    band = (dist >= 0) & (dist <= max_dist)
    not_pad = (jnp.arange(nb) > 0)[:, None, None] | (kj >= BLK)[None]
    valid = band[None] & not_pad
    s = jnp.where(valid[None, :, None, None], s, NEG_INF)
    m = jnp.max(s, axis=-1, keepdims=True)
    if sink is not None:
        sk = sink.astype(jnp.float32)[None, None, :, :, None, None]
        m = jnp.maximum(m, sk)
    p = jnp.exp(s - m)
    den = jnp.sum(p, axis=-1, keepdims=True)
    if sink is not None:
        den = den + jnp.exp(sk - m)
    o = jnp.einsum('bnhgqk,bnkhd->bnqhgd', p.astype(v.dtype), vv,
                   preferred_element_type=jnp.float32)
    den_t = jnp.transpose(den[..., 0], (0, 1, 4, 2, 3))
    lse_t = jnp.transpose((m + jnp.log(den))[..., 0], (0, 1, 4, 2, 3))
    o = (o / den_t[..., None]).reshape(b, l, hkv, g, d)
    return o.astype(q.dtype), lse_t.reshape(b, l, hkv, g)


def dilated_attention(q, k, v):
    b, s, h, d = q.shape
    outs, lses = [], []
    for window, dil in A_CONFIGS:
        steps = window // dil
        l = s // dil
        lp = -(-l // BLK) * BLK

        def by_stride(t):
            t = t.reshape(b, l, dil, h, d).transpose(0, 2, 1, 3, 4).reshape(b * dil, l, h, d)
            return jnp.pad(t, ((0, 0), (0, lp - l), (0, 0), (0, 0)))

        o, lse = banded_attention(by_stride(q)[:, :, :, None], by_stride(k), by_stride(v), steps)
        o = o[:, :l, :, 0].reshape(b, dil, l, h, d).transpose(0, 2, 1, 3, 4).reshape(b, s, h, d)
        lse = lse[:, :l, :, 0].reshape(b, dil, l, h).transpose(0, 2, 1, 3).reshape(b, s, h)
        outs.append(o)
        lses.append(lse)
    w = jax.nn.softmax(jnp.stack(lses, axis=0), axis=0)
    o = jnp.sum(w[..., None] * jnp.stack(outs, axis=0).astype(jnp.float32), axis=0)
    return o.astype(q.dtype)


def short_conv(x, w):
    s = x.shape[1]
    xp = jnp.pad(x, ((0, 0), (C_CONV - 1, 0), (0, 0)))
    y = xp[:, 0:s] * w[0]
    for j in range(1, C_CONV):
        y = y + xp[:, j:j + s] * w[j]
    return jax.nn.silu(y)


def gated_delta_rule(q, k, v, g, beta):
    b, s, h, dk = k.shape
    dv = v.shape[-1]
    nc = s // C_CHUNK

    def chunks(t):
        t = t.astype(jnp.float32).reshape((b, nc, C_CHUNK) + t.shape[2:])
        return jnp.moveaxis(t, 3, 1)

    qc = chunks(q) * (dk ** -0.5)
    kc, vc, bc = chunks(k), chunks(v), chunks(beta)
    gc = jnp.cumsum(chunks(g), axis=-1)
    tril = jnp.tril(jnp.ones((C_CHUNK, C_CHUNK), dtype=bool))
    strict = jnp.tril(jnp.ones((C_CHUNK, C_CHUNK), dtype=bool), -1)
    diff = gc[..., :, None] - gc[..., None, :]
    decay = jnp.where(tril, jnp.exp(jnp.where(tril, diff, 0.0)), 0.0)
    kkt = jnp.einsum('bhncd,bhnjd->bhncj', kc, kc)
    a_mat = jnp.where(strict, kkt * decay * bc[..., None], 0.0)
    eye = jnp.eye(C_CHUNK, dtype=jnp.float32)
    rhs = jnp.concatenate([vc * bc[..., None], kc * (bc * jnp.exp(gc))[..., None]], axis=-1)
    sol = lax.linalg.triangular_solve(a_mat + eye, rhs, left_side=True, lower=True,
                                      unit_diagonal=True)
    u, w = sol[..., :dv], sol[..., dv:]
    attn = jnp.where(tril, jnp.einsum('bhncd,bhnjd->bhncj', qc, kc) * decay, 0.0)
    q_dec = qc * jnp.exp(gc)[..., None]
    k_dec = kc * jnp.exp(gc[..., -1:] - gc)[..., None]
    g_last = jnp.exp(gc[..., -1])

    def step(state, xs):
        u_i, w_i, attn_i, qd_i, kd_i, gl_i = xs
        v_new = u_i - jnp.einsum('bhcd,bhde->bhce', w_i, state)
        o_i = (jnp.einsum('bhcd,bhde->bhce', qd_i, state)
               + jnp.einsum('bhcj,bhje->bhce', attn_i, v_new))
        state = state * gl_i[..., None, None] + jnp.einsum('bhcd,bhce->bhde', kd_i, v_new)
        return state, o_i

    xs = tuple(jnp.moveaxis(t, 2, 0) for t in (u, w, attn, q_dec, k_dec, g_last))
    state0 = jnp.zeros((b, h, dk, dv), jnp.float32)
    _, o = lax.scan(step, state0, xs)
    return jnp.transpose(o, (1, 0, 3, 2, 4)).reshape(b, s, h, dv)


def mixer_block(h, cos, sin, w_in, b_in, conv_w, a_log, dt_bias, sinks, c_norm,
                w_branch_a, w_branch_b, w_branch_c, w_out):
    b, s, _ = h.shape
    u = h @ w_in + b_in
    (a_q, a_k, a_v, b_q, b_k, b_v, c_q, c_k, c_v, c_z, c_a, c_b,
     gate_a, gate_b, gate_c) = split_columns(u)

    def heads(t, n):
        return t.reshape(b, s, n, -1)

    ya = dilated_attention(apply_rope(heads(a_q, A_HEADS), cos, sin),
                           apply_rope(heads(a_k, A_HEADS), cos, sin),
                           heads(a_v, A_HEADS)).reshape(b, s, A_W)

    qb = apply_rope(heads(b_q, B_Q_HEADS), cos, sin).reshape(b, s, B_KV_HEADS, B_GROUP, HEAD_DIM)
    kb = apply_rope(heads(b_k, B_KV_HEADS), cos, sin)
    vb = heads(b_v, B_KV_HEADS)
    yb, _ = banded_attention(qb, kb, vb, B_WINDOW - 1, sink=sinks.reshape(B_KV_HEADS, B_GROUP))
    yb = yb.reshape(b, s, B_QW)

    qkv = short_conv(jnp.concatenate([c_q, c_k, c_v], axis=-1), conv_w)
    rep = C_V_HEADS // C_QK_HEADS
    cq = jnp.repeat(l2norm(qkv[..., :C_QKW].reshape(b, s, C_QK_HEADS, C_DK)), rep, axis=2)
    ck = jnp.repeat(l2norm(qkv[..., C_QKW:2 * C_QKW].reshape(b, s, C_QK_HEADS, C_DK)), rep, axis=2)
    cv = qkv[..., 2 * C_QKW:].reshape(b, s, C_V_HEADS, C_DV)
    beta = jax.nn.sigmoid(c_b.astype(jnp.float32))
    g = -jnp.exp(a_log.astype(jnp.float32)) * jax.nn.softplus(
        c_a.astype(jnp.float32) + dt_bias.astype(jnp.float32))
    yc = gated_delta_rule(cq, ck, cv, g, beta)
    yc = rmsnorm(yc, c_norm) * jax.nn.silu(c_z.reshape(b, s, C_V_HEADS, C_DV).astype(jnp.float32))
    yc = yc.astype(h.dtype).reshape(b, s, C_VW)

    merged = (jax.nn.sigmoid(gate_a) * (ya @ w_branch_a)
              + jax.nn.sigmoid(gate_b) * (yb @ w_branch_b)
              + jax.nn.sigmoid(gate_c) * (yc @ w_branch_c))
    return merged @ w_out


def _fwd_setup_inputs(seed: int = 0) -> dict:
    key = jax.random.key(seed)
    ks = jax.random.split(key, 20)
    f32 = jnp.float32

    def dense(k, shape, fan_in, scale=1.0):
        return jax.random.normal(k, shape, f32) * (scale * fan_in ** -0.5)

    def gain(k, shape):
        return 1.0 + 0.05 * jax.random.normal(k, shape, f32)

    res_scale = (2 * DEPTH) ** -0.5
    x = jax.random.normal(ks[0], (BATCH, SEQ, D_MODEL), f32)
    positions = (jax.random.randint(ks[1], (BATCH, 1), 0, MAX_POS_OFFSET, dtype=jnp.int32)
                 + jnp.arange(SEQ, dtype=jnp.int32)[None, :])
    norm_mix = gain(ks[2], (DEPTH, D_MODEL))
    w_in = dense(ks[3], (DEPTH, D_MODEL, D_IN), D_MODEL)
    b_in = 0.02 * jax.random.normal(ks[4], (DEPTH, D_IN), f32)
    conv_w = dense(ks[5], (DEPTH, C_CONV, 2 * C_QKW + C_VW), C_CONV)
    a_log = jnp.log(jax.random.uniform(ks[6], (DEPTH, C_V_HEADS), f32, 1.0, 16.0))
    dt = jnp.exp(jax.random.uniform(ks[7], (DEPTH, C_V_HEADS), f32, math.log(1e-3), math.log(1e-1)))
    dt_bias = dt + jnp.log(-jnp.expm1(-dt))
    sinks = 0.5 * jax.random.normal(ks[8], (DEPTH, B_Q_HEADS), f32)
    c_norm = gain(ks[9], (DEPTH, C_DV))
    w_branch_a = dense(ks[10], (DEPTH, A_W, D_MODEL), A_W)
    w_branch_b = dense(ks[11], (DEPTH, B_QW, D_MODEL), B_QW)
    w_branch_c = dense(ks[12], (DEPTH, C_VW, D_MODEL), C_VW)
    w_out = dense(ks[13], (DEPTH, D_MODEL, D_MODEL), D_MODEL, res_scale)
    norm_ffn = gain(ks[14], (DEPTH, D_MODEL))
    w_ff1 = dense(ks[15], (DEPTH, D_MODEL, D_FF), D_MODEL)
    w_ff2 = dense(ks[16], (DEPTH, D_FF, D_MODEL), D_FF, res_scale)
    norm_final = gain(ks[17], (D_MODEL,))
    return {"x": x, "positions": positions, "norm_mix": norm_mix, "w_in": w_in, "b_in": b_in,
            "conv_w": conv_w, "a_log": a_log, "dt_bias": dt_bias, "sinks": sinks, "c_norm": c_norm,
            "w_branch_a": w_branch_a, "w_branch_b": w_branch_b, "w_branch_c": w_branch_c,
            "w_out": w_out, "norm_ffn": norm_ffn, "w_ff1": w_ff1, "w_ff2": w_ff2,
            "norm_final": norm_final}


def _fwd_reference(x, positions, norm_mix, w_in, b_in, conv_w, a_log, dt_bias, sinks, c_norm,
              w_branch_a, w_branch_b, w_branch_c, w_out, norm_ffn, w_ff1, w_ff2, norm_final):
    cos, sin = rope_tables(positions, x.dtype)
    for layer in range(DEPTH):
        h = rmsnorm(x, norm_mix[layer])
        x = x + mixer_block(h, cos, sin, w_in[layer], b_in[layer], conv_w[layer], a_log[layer],
                            dt_bias[layer], sinks[layer], c_norm[layer], w_branch_a[layer],
                            w_branch_b[layer], w_branch_c[layer], w_out[layer])
        h = rmsnorm(x, norm_ffn[layer])
        x = x + jnp.square(jax.nn.relu(h @ w_ff1[layer])) @ w_ff2[layer]
    return rmsnorm(x, norm_final)


import jax as _jax
import jax.numpy as _jnp

TWIN_FORMAT = 'train_step'
FWD_PARAMS = ['x', 'positions', 'norm_mix', 'w_in', 'b_in', 'conv_w', 'a_log', 'dt_bias', 'sinks', 'c_norm', 'w_branch_a', 'w_branch_b', 'w_branch_c', 'w_out', 'norm_ffn', 'w_ff1', 'w_ff2', 'norm_final']
TWIN_WEIGHTS = ['norm_mix', 'w_in', 'b_in', 'conv_w', 'a_log', 'dt_bias', 'sinks', 'c_norm', 'w_branch_a', 'w_branch_b', 'w_branch_c', 'w_out', 'norm_ffn', 'w_ff1', 'w_ff2', 'norm_final']
TWIN_DIFF_INPUT = 'x'
TWIN_INPUTS = ['x', 'positions', 'norm_mix', 'w_in', 'b_in', 'conv_w', 'a_log', 'dt_bias', 'sinks', 'c_norm', 'w_branch_a', 'w_branch_b', 'w_branch_c', 'w_out', 'norm_ffn', 'w_ff1', 'w_ff2', 'norm_final', 'loss_target', 'm_norm_mix', 'm_w_in', 'm_b_in', 'm_conv_w', 'm_a_log', 'm_dt_bias', 'm_sinks', 'm_c_norm', 'm_w_branch_a', 'm_w_branch_b', 'm_w_branch_c', 'm_w_out', 'm_norm_ffn', 'm_w_ff1', 'm_w_ff2', 'm_norm_final', 'v_norm_mix', 'v_w_in', 'v_b_in', 'v_conv_w', 'v_a_log', 'v_dt_bias', 'v_sinks', 'v_c_norm', 'v_w_branch_a', 'v_w_branch_b', 'v_w_branch_c', 'v_w_out', 'v_norm_ffn', 'v_w_ff1', 'v_w_ff2', 'v_norm_final']
TWIN_OUTPUTS = ['loss', 'grad_x', 'grad_norm_mix', 'grad_w_in', 'grad_b_in', 'grad_conv_w', 'grad_a_log', 'grad_dt_bias', 'grad_sinks', 'grad_c_norm', 'grad_w_branch_a', 'grad_w_branch_b', 'grad_w_branch_c', 'grad_w_out', 'grad_norm_ffn', 'grad_w_ff1', 'grad_w_ff2', 'grad_norm_final', 'delta_norm_mix', 'delta_w_in', 'delta_b_in', 'delta_conv_w', 'delta_a_log', 'delta_dt_bias', 'delta_sinks', 'delta_c_norm', 'delta_w_branch_a', 'delta_w_branch_b', 'delta_w_branch_c', 'delta_w_out', 'delta_norm_ffn', 'delta_w_ff1', 'delta_w_ff2', 'delta_norm_final', 'new_m_norm_mix', 'new_m_w_in', 'new_m_b_in', 'new_m_conv_w', 'new_m_a_log', 'new_m_dt_bias', 'new_m_sinks', 'new_m_c_norm', 'new_m_w_branch_a', 'new_m_w_branch_b', 'new_m_w_branch_c', 'new_m_w_out', 'new_m_norm_ffn', 'new_m_w_ff1', 'new_m_w_ff2', 'new_m_norm_final', 'new_v_norm_mix', 'new_v_w_in', 'new_v_b_in', 'new_v_conv_w', 'new_v_a_log', 'new_v_dt_bias', 'new_v_sinks', 'new_v_c_norm', 'new_v_w_branch_a', 'new_v_w_branch_b', 'new_v_w_branch_c', 'new_v_w_out', 'new_v_norm_ffn', 'new_v_w_ff1', 'new_v_w_ff2', 'new_v_norm_final']
TWIN_LEAF_KINDS = {'loss': 'loss', 'grad_x': 'grad_x', 'grad_norm_mix': 'grad_w', 'grad_w_in': 'grad_w', 'grad_b_in': 'grad_w', 'grad_conv_w': 'grad_w', 'grad_a_log': 'grad_w', 'grad_dt_bias': 'grad_w', 'grad_sinks': 'grad_w', 'grad_c_norm': 'grad_w', 'grad_w_branch_a': 'grad_w', 'grad_w_branch_b': 'grad_w', 'grad_w_branch_c': 'grad_w', 'grad_w_out': 'grad_w', 'grad_norm_ffn': 'grad_w', 'grad_w_ff1': 'grad_w', 'grad_w_ff2': 'grad_w', 'grad_norm_final': 'grad_w', 'delta_norm_mix': 'delta_w', 'delta_w_in': 'delta_w', 'delta_b_in': 'delta_w', 'delta_conv_w': 'delta_w', 'delta_a_log': 'delta_w', 'delta_dt_bias': 'delta_w', 'delta_sinks': 'delta_w', 'delta_c_norm': 'delta_w', 'delta_w_branch_a': 'delta_w', 'delta_w_branch_b': 'delta_w', 'delta_w_branch_c': 'delta_w', 'delta_w_out': 'delta_w', 'delta_norm_ffn': 'delta_w', 'delta_w_ff1': 'delta_w', 'delta_w_ff2': 'delta_w', 'delta_norm_final': 'delta_w', 'new_m_norm_mix': 'new_m', 'new_m_w_in': 'new_m', 'new_m_b_in': 'new_m', 'new_m_conv_w': 'new_m', 'new_m_a_log': 'new_m', 'new_m_dt_bias': 'new_m', 'new_m_sinks': 'new_m', 'new_m_c_norm': 'new_m', 'new_m_w_branch_a': 'new_m', 'new_m_w_branch_b': 'new_m', 'new_m_w_branch_c': 'new_m', 'new_m_w_out': 'new_m', 'new_m_norm_ffn': 'new_m', 'new_m_w_ff1': 'new_m', 'new_m_w_ff2': 'new_m', 'new_m_norm_final': 'new_m', 'new_v_norm_mix': 'new_v', 'new_v_w_in': 'new_v', 'new_v_b_in': 'new_v', 'new_v_conv_w': 'new_v', 'new_v_a_log': 'new_v', 'new_v_dt_bias': 'new_v', 'new_v_sinks': 'new_v', 'new_v_c_norm': 'new_v', 'new_v_w_branch_a': 'new_v', 'new_v_w_branch_b': 'new_v', 'new_v_w_branch_c': 'new_v', 'new_v_w_out': 'new_v', 'new_v_norm_ffn': 'new_v', 'new_v_w_ff1': 'new_v', 'new_v_w_ff2': 'new_v', 'new_v_norm_final': 'new_v'}


def _forward(args):
    return _fwd_reference(*[args[k] for k in FWD_PARAMS])


def _output_shape():
    def fwd():
        inp = _fwd_setup_inputs(0)
        return _fwd_reference(*[inp[k] for k in FWD_PARAMS])
    out = _jax.eval_shape(fwd)
    return out.shape, out.dtype

N_MICROBATCH = 1
ADAM_LR = 0.001
ADAM_B1 = 0.9
ADAM_B2 = 0.999
ADAM_EPS = 1e-08
ADAM_WD = 0.01
ADAM_STEP = 10
PER_EXAMPLE_BATCH_AXIS = {'x': 0, 'positions': 0, 'loss_target': 0}
SHARED_INPUTS = []
_WEIGHT_DTYPES = {'norm_mix': _jnp.float32, 'w_in': _jnp.float32, 'b_in': _jnp.float32, 'conv_w': _jnp.float32, 'a_log': _jnp.float32, 'dt_bias': _jnp.float32, 'sinks': _jnp.float32, 'c_norm': _jnp.float32, 'w_branch_a': _jnp.float32, 'w_branch_b': _jnp.float32, 'w_branch_c': _jnp.float32, 'w_out': _jnp.float32, 'norm_ffn': _jnp.float32, 'w_ff1': _jnp.float32, 'w_ff2': _jnp.float32, 'norm_final': _jnp.float32}
MOMENT_SCALE = {'norm_mix': 8.138908e-02, 'w_in': 2.859853e-02, 'b_in': 1.599169e-01, 'conv_w': 3.984497e-02, 'a_log': 3.324754e-01, 'dt_bias': 3.260788e-01, 'sinks': 1.637296e-02, 'c_norm': 1.176220e-01, 'w_branch_a': 3.113155e-02, 'w_branch_b': 3.310832e-02, 'w_branch_c': 4.999228e-02, 'w_out': 1.304288e-01, 'norm_ffn': 1.629794e-01, 'w_ff1': 7.857216e-02, 'w_ff2': 5.380221e-01, 'norm_final': 6.453172e+01}


def _to_microbatches(a, axis):
    t = _jnp.moveaxis(a, axis, 0)
    t = t.reshape((N_MICROBATCH, t.shape[0] // N_MICROBATCH) + t.shape[1:])
    return _jnp.moveaxis(t, 1, axis + 1)


def setup_inputs(seed: int = 0) -> dict:
    inp = _fwd_setup_inputs(seed)
    key = _jax.random.fold_in(_jax.random.key(seed), 7919)
    shape, _ = _output_shape()
    out = dict(inp)
    out["loss_target"] = _jax.random.normal(_jax.random.fold_in(key, 0), shape, _jnp.float32)
    for i, name in enumerate(TWIN_WEIGHTS):
        w = inp[name].astype(_jnp.float32)
        if MOMENT_SCALE is None:
            s = _jnp.sqrt(_jnp.mean(_jnp.square(w)) + 1e-30)
        else:
            s = MOMENT_SCALE[name]
        km, kv = _jax.random.split(_jax.random.fold_in(key, i + 1))
        out[name] = w
        out["m_" + name] = s * _jax.random.normal(km, w.shape, _jnp.float32)
        out["v_" + name] = (s * s) * _jax.random.uniform(kv, w.shape, _jnp.float32, 0.5, 1.5)
    if N_MICROBATCH > 1:
        for name, axis in PER_EXAMPLE_BATCH_AXIS.items():
            out[name] = _to_microbatches(out[name], axis)
    return {'x': out['x'], 'positions': out['positions'], 'norm_mix': out['norm_mix'], 'w_in': out['w_in'], 'b_in': out['b_in'], 'conv_w': out['conv_w'], 'a_log': out['a_log'], 'dt_bias': out['dt_bias'], 'sinks': out['sinks'], 'c_norm': out['c_norm'], 'w_branch_a': out['w_branch_a'], 'w_branch_b': out['w_branch_b'], 'w_branch_c': out['w_branch_c'], 'w_out': out['w_out'], 'norm_ffn': out['norm_ffn'], 'w_ff1': out['w_ff1'], 'w_ff2': out['w_ff2'], 'norm_final': out['norm_final'], 'loss_target': out['loss_target'], 'm_norm_mix': out['m_norm_mix'], 'm_w_in': out['m_w_in'], 'm_b_in': out['m_b_in'], 'm_conv_w': out['m_conv_w'], 'm_a_log': out['m_a_log'], 'm_dt_bias': out['m_dt_bias'], 'm_sinks': out['m_sinks'], 'm_c_norm': out['m_c_norm'], 'm_w_branch_a': out['m_w_branch_a'], 'm_w_branch_b': out['m_w_branch_b'], 'm_w_branch_c': out['m_w_branch_c'], 'm_w_out': out['m_w_out'], 'm_norm_ffn': out['m_norm_ffn'], 'm_w_ff1': out['m_w_ff1'], 'm_w_ff2': out['m_w_ff2'], 'm_norm_final': out['m_norm_final'], 'v_norm_mix': out['v_norm_mix'], 'v_w_in': out['v_w_in'], 'v_b_in': out['v_b_in'], 'v_conv_w': out['v_conv_w'], 'v_a_log': out['v_a_log'], 'v_dt_bias': out['v_dt_bias'], 'v_sinks': out['v_sinks'], 'v_c_norm': out['v_c_norm'], 'v_w_branch_a': out['v_w_branch_a'], 'v_w_branch_b': out['v_w_branch_b'], 'v_w_branch_c': out['v_w_branch_c'], 'v_w_out': out['v_w_out'], 'v_norm_ffn': out['v_norm_ffn'], 'v_w_ff1': out['v_w_ff1'], 'v_w_ff2': out['v_w_ff2'], 'v_norm_final': out['v_norm_final']}


def _loss(weights, diff, rest, loss_target):
    with _jax.named_scope("forward"):
        args = {**rest, TWIN_DIFF_INPUT: diff, **{k: w.astype(_WEIGHT_DTYPES[k]) for k, w in weights.items()}}
        y = _forward(args)
    with _jax.named_scope("loss_head"):
        err = _jnp.square(y.astype(_jnp.float32) - loss_target)
        return 0.5 * _jnp.sum(_jnp.mean(err, axis=-1)) if err.ndim else 0.5 * err


def _adamw(w, g, m, v):
    m = ADAM_B1 * m + (1.0 - ADAM_B1) * g
    v = ADAM_B2 * v + (1.0 - ADAM_B2) * _jnp.square(g)
    m_hat = m / (1.0 - ADAM_B1 ** ADAM_STEP)
    v_hat = v / (1.0 - ADAM_B2 ** ADAM_STEP)
    delta = -ADAM_LR * (m_hat / (_jnp.sqrt(v_hat) + ADAM_EPS) + ADAM_WD * w)
    return delta, m, v


def reference(x, positions, norm_mix, w_in, b_in, conv_w, a_log, dt_bias, sinks, c_norm, w_branch_a, w_branch_b, w_branch_c, w_out, norm_ffn, w_ff1, w_ff2, norm_final, loss_target, m_norm_mix, m_w_in, m_b_in, m_conv_w, m_a_log, m_dt_bias, m_sinks, m_c_norm, m_w_branch_a, m_w_branch_b, m_w_branch_c, m_w_out, m_norm_ffn, m_w_ff1, m_w_ff2, m_norm_final, v_norm_mix, v_w_in, v_b_in, v_conv_w, v_a_log, v_dt_bias, v_sinks, v_c_norm, v_w_branch_a, v_w_branch_b, v_w_branch_c, v_w_out, v_norm_ffn, v_w_ff1, v_w_ff2, v_norm_final):
    given = dict(x=x, positions=positions, norm_mix=norm_mix, w_in=w_in, b_in=b_in, conv_w=conv_w, a_log=a_log, dt_bias=dt_bias, sinks=sinks, c_norm=c_norm, w_branch_a=w_branch_a, w_branch_b=w_branch_b, w_branch_c=w_branch_c, w_out=w_out, norm_ffn=norm_ffn, w_ff1=w_ff1, w_ff2=w_ff2, norm_final=norm_final, loss_target=loss_target, m_norm_mix=m_norm_mix, m_w_in=m_w_in, m_b_in=m_b_in, m_conv_w=m_conv_w, m_a_log=m_a_log, m_dt_bias=m_dt_bias, m_sinks=m_sinks, m_c_norm=m_c_norm, m_w_branch_a=m_w_branch_a, m_w_branch_b=m_w_branch_b, m_w_branch_c=m_w_branch_c, m_w_out=m_w_out, m_norm_ffn=m_norm_ffn, m_w_ff1=m_w_ff1, m_w_ff2=m_w_ff2, m_norm_final=m_norm_final, v_norm_mix=v_norm_mix, v_w_in=v_w_in, v_b_in=v_b_in, v_conv_w=v_conv_w, v_a_log=v_a_log, v_dt_bias=v_dt_bias, v_sinks=v_sinks, v_c_norm=v_c_norm, v_w_branch_a=v_w_branch_a, v_w_branch_b=v_w_branch_b, v_w_branch_c=v_w_branch_c, v_w_out=v_w_out, v_norm_ffn=v_norm_ffn, v_w_ff1=v_w_ff1, v_w_ff2=v_w_ff2, v_norm_final=v_norm_final)
    weights = {n: given[n] for n in TWIN_WEIGHTS}
    shared = {n: given[n] for n in SHARED_INPUTS}
    per_example = {n: given[n] for n in ['x', 'positions']}
    grad_fn = _jax.value_and_grad(_loss, argnums=(0, 1))

    def one_microbatch(ex, loss_target):
        ex = dict(ex)
        diff = ex.pop(TWIN_DIFF_INPUT)
        return grad_fn(weights, diff, {**shared, **ex}, loss_target)

    if N_MICROBATCH == 1:
        loss, (grad_w, grad_x) = one_microbatch(per_example, given["loss_target"])
    else:
        def body(carry, xs):
            loss_sum, grad_sum = carry
            l_k, (gw_k, gx_k) = one_microbatch(xs[0], xs[1])
            with _jax.named_scope("update"):
                return (loss_sum + l_k, _jax.tree.map(_jnp.add, grad_sum, gw_k)), gx_k

        init = (_jnp.zeros((), _jnp.float32), _jax.tree.map(_jnp.zeros_like, weights))
        (loss, grad_w), grad_x = _jax.lax.scan(body, init, (per_example, given["loss_target"]))
    with _jax.named_scope("update"):
        delta_w, new_m, new_v = {}, {}, {}
        for n in TWIN_WEIGHTS:
            delta_w[n], new_m[n], new_v[n] = _adamw(weights[n], grad_w[n], given["m_" + n], given["v_" + n])
    return (loss, grad_x, *[grad_w[n] for n in TWIN_WEIGHTS], *[delta_w[n] for n in TWIN_WEIGHTS],
            *[new_m[n] for n in TWIN_WEIGHTS], *[new_v[n] for n in TWIN_WEIGHTS])
```

```python
import functools
import math

import jax
import jax.numpy as jnp
from jax import lax
from jax.experimental import pallas as pl
from jax.experimental.pallas import tpu as pltpu

F32 = jnp.float32
BF16 = jnp.bfloat16
HIGHEST = lax.Precision.HIGHEST

N_DEV = 8
D_MODEL = 1024
DEPTH = 2
HEAD_DIM = 64
ROT_DIM = 16
ROPE_THETA = 500000.0
BLK = 128
NEG_INF = -1e30
EPS = 1e-6
A_CONFIGS = ((128, 1), (512, 4), (2048, 16))
B_GROUP = 4
C_QK_HEADS = 4
C_V_HEADS = 8
C_DK = 128
C_CONV = 4
CHUNK = 64
D_FF = 4096
D_IN = 8464
ADAM_LR = 0.001
ADAM_B1 = 0.9
ADAM_B2 = 0.999
ADAM_EPS = 1e-08
ADAM_WD = 0.01
ADAM_STEP = 10

IN_LAYOUT = (
    ("a_q", 0, 512, 0), ("a_k", 512, 512, 512), ("a_v", 1024, 512, 1024), ("b_q", 1536, 512, 1536),
    ("c_qkv", 2048, 2048, 2304), ("c_z", 4096, 1024, 4352),
    ("gate_a", 5120, 1024, 5392), ("gate_b", 6144, 1024, 6416), ("gate_c", 7168, 1024, 7440),
    ("b_k", 8192, 128, 2048), ("b_v", 8320, 128, 2176), ("c_ab", 8448, 16, 5376),
)
D_IN_PAD = 8704
LANES = 128
VMEM_LIMIT = 56 * 1024 * 1024


def _cparams(sem=None):
    return pltpu.CompilerParams(dimension_semantics=sem, vmem_limit_bytes=VMEM_LIMIT)


def _relu2(t):
    return jnp.square(jnp.maximum(t, 0.0))


def _mm(a, b, *, ta=False, tb=False, bias=None, a_fn=None, mul_drelu2=None, add=None,
        tm=512, tn=512, tk=2048, name):
    if ta:
        kdim, m = a.shape
    else:
        m, kdim = a.shape
    n = b.shape[0] if tb else b.shape[1]
    tm, tn, tk = min(tm, m), min(tn, n), min(tk, kdim)
    assert m % tm == 0 and n % tn == 0 and kdim % tk == 0, (a.shape, b.shape, tm, tn, tk)
    nk = kdim // tk
    dims = (((0 if ta else 1,), (1 if tb else 0,)), ((), ()))
    extras = [e for e in (bias, mul_drelu2, add) if e is not None]

    def body(*refs):
        a_ref, b_ref = refs[0], refs[1]
        pos = 2
        bias_ref = pre_ref = add_ref = None
        if bias is not None:
            bias_ref = refs[pos]; pos += 1
        if mul_drelu2 is not None:
            pre_ref = refs[pos]; pos += 1
        if add is not None:
            add_ref = refs[pos]; pos += 1
        o_ref = refs[pos]
        acc_ref = refs[pos + 1] if nk > 1 else None

        av = a_ref[...]
        if a_fn is not None:
            av = a_fn(av)
        part = lax.dot_general(av.astype(BF16), b_ref[...].astype(BF16), dims,
                               preferred_element_type=F32)

        def finish(acc):
            if bias_ref is not None:
                acc = acc + bias_ref[...]
            if pre_ref is not None:
                acc = acc * (2.0 * jnp.maximum(pre_ref[...], 0.0))
            if add_ref is not None:
                acc = acc + add_ref[...]
            o_ref[...] = acc

        if nk == 1:
            finish(part)
        else:
            k = pl.program_id(2)

            @pl.when(k == 0)
            def _():
                acc_ref[...] = part

            @pl.when(k > 0)
            def _():
                acc_ref[...] += part

            @pl.when(k == nk - 1)
            def _():
                finish(acc_ref[...])

    a_spec = (pl.BlockSpec((tk, tm), lambda i, j, k: (k, i)) if ta
              else pl.BlockSpec((tm, tk), lambda i, j, k: (i, k)))
    b_spec = (pl.BlockSpec((tn, tk), lambda i, j, k: (j, k)) if tb
              else pl.BlockSpec((tk, tn), lambda i, j, k: (k, j)))
    in_specs = [a_spec, b_spec]
    if bias is not None:
        in_specs.append(pl.BlockSpec((1, tn), lambda i, j, k: (0, j)))
    for _ in extras[(1 if bias is not None else 0):]:
        in_specs.append(pl.BlockSpec((tm, tn), lambda i, j, k: (i, j)))
    return pl.pallas_call(
        body, name=name,
        grid=(m // tm, n // tn, nk),
        in_specs=in_specs,
        out_specs=pl.BlockSpec((tm, tn), lambda i, j, k: (i, j)),
        out_shape=jax.ShapeDtypeStruct((m, n), F32),
        scratch_shapes=[pltpu.VMEM((tm, tn), F32)] if nk > 1 else [],
        compiler_params=_cparams(("parallel", "parallel", "arbitrary")),
    )(a, b, *extras)


def _colsum(x, *, name, tk=512, tn=1024):
    t, n = x.shape
    tn = min(tn, n)
    assert t % tk == 0 and n % tn == 0

    def body(x_ref, o_ref, acc_ref):
        k = pl.program_id(1)
        part = jnp.sum(x_ref[...].reshape(tk // 8, 8, tn), axis=0)

        @pl.when(k == 0)
        def _():
            acc_ref[...] = part

        @pl.when(k > 0)
        def _():
            acc_ref[...] += part

        @pl.when(k == t // tk - 1)
        def _():
            o_ref[...] = jnp.sum(acc_ref[...], axis=0, keepdims=True)

    return pl.pallas_call(
        body, name=name, grid=(n // tn, t // tk),
        in_specs=[pl.BlockSpec((tk, tn), lambda j, k: (k, j))],
        out_specs=pl.BlockSpec((1, tn), lambda j, k: (0, j)),
        out_shape=jax.ShapeDtypeStruct((1, n), F32),
        scratch_shapes=[pltpu.VMEM((8, tn), F32)],
        compiler_params=_cparams(("parallel", "arbitrary")),
    )(x)


@jax.custom_vjp
def linear(a, w):
    return _mm(a, w, name="linear_fwd")


def _linear_fwd(a, w):
    return _mm(a, w, name="linear_fwd"), (a, w)


def _linear_bwd(res, dy):
    a, w = res
    da = _mm(dy, w, tb=True, name="linear_da")
    dw = _mm(a, dy, ta=True, tm=512, tn=512, tk=1024, name="linear_dw")
    return da, dw


linear.defvjp(_linear_fwd, _linear_bwd)


@jax.custom_vjp
def linear_res(a, w, res):
    return _mm(a, w, add=res, name="linear_res_fwd")


def _linear_res_fwd(a, w, res):
    return _mm(a, w, add=res, name="linear_res_fwd"), (a, w)


def _linear_res_bwd(saved, dy):
    a, w = saved
    da = _mm(dy, w, tb=True, name="linear_res_da")
    dw = _mm(a, dy, ta=True, tk=1024, name="linear_res_dw")
    return da, dw, dy


linear_res.defvjp(_linear_res_fwd, _linear_res_bwd)


@jax.custom_vjp
def linear_bias(a, w, b):
    return _mm(a, w, bias=b, tm=1024, name="in_proj_fwd")


def _linear_bias_fwd(a, w, b):
    return _mm(a, w, bias=b, tm=1024, name="in_proj_fwd"), (a, w)


def _linear_bias_bwd(saved, dy):
    a, w = saved
    da = _mm(dy, w, tb=True, tk=2176, name="in_proj_da")
    dw = _mm(a, dy, ta=True, tk=1024, name="in_proj_dw")
    db = _colsum(dy, name="in_proj_db", tn=512)
    return da, dw, db


linear_bias.defvjp(_linear_bias_fwd, _linear_bias_bwd)


@jax.custom_vjp
def ffn(h, w1, w2, res):
    pre = _mm(h, w1, name="ffn_up")
    return _mm(pre, w2, a_fn=_relu2, add=res, name="ffn_down")


def _ffn_fwd(h, w1, w2, res):
    pre = _mm(h, w1, name="ffn_up")
    return _mm(pre, w2, a_fn=_relu2, add=res, name="ffn_down"), (h, w1, w2, pre)


def _ffn_bwd(saved, dy):
    h, w1, w2, pre = saved
    dpre = _mm(dy, w2, tb=True, mul_drelu2=pre, name="ffn_dpre")
    dw2 = _mm(pre, dy, ta=True, a_fn=_relu2, tk=1024, name="ffn_dw2")
    dw1 = _mm(h, dpre, ta=True, tk=1024, name="ffn_dw1")
    dh = _mm(dpre, w1, tb=True, name="ffn_dh")
    return dh, dw1, dw2, dy


ffn.defvjp(_ffn_fwd, _ffn_bwd)


def _rms_fwd_call(x, g, *, name, tq=512):
    t, d = x.shape

    def body(x_ref, g_ref, y_ref):
        xv = x_ref[...]
        r = lax.rsqrt(jnp.mean(xv * xv, axis=-1, keepdims=True) + EPS)
        y_ref[...] = xv * r * g_ref[...]

    return pl.pallas_call(
        body, name=name, grid=(t // tq,),
        in_specs=[pl.BlockSpec((tq, d), lambda i: (i, 0)), pl.BlockSpec((1, d), lambda i: (0, 0))],
        out_specs=pl.BlockSpec((tq, d), lambda i: (i, 0)),
        out_shape=jax.ShapeDtypeStruct((t, d), F32),
        compiler_params=_cparams(("parallel",)),
    )(x, g)


def _rms_bwd_call(x, g, dy, *, name, tq=512):
    t, d = x.shape
    nt = t // tq

    def body(x_ref, g_ref, dy_ref, dx_ref, dg_ref, acc_ref):
        i = pl.program_id(0)
        xv = x_ref[...]
        r = lax.rsqrt(jnp.mean(xv * xv, axis=-1, keepdims=True) + EPS)
        xh = xv * r
        dyv = dy_ref[...]
        dxh = dyv * g_ref[...]
        dx_ref[...] = r * (dxh - xh * jnp.mean(dxh * xh, axis=-1, keepdims=True))
        part = jnp.sum((dyv * xh).reshape(tq // 8, 8, d), axis=0)

        @pl.when(i == 0)
        def _():
            acc_ref[...] = part

        @pl.when(i > 0)
        def _():
            acc_ref[...] += part

        @pl.when(i == nt - 1)
        def _():
            dg_ref[...] = jnp.sum(acc_ref[...], axis=0, keepdims=True)

    return pl.pallas_call(
        body, name=name, grid=(nt,),
        in_specs=[pl.BlockSpec((tq, d), lambda i: (i, 0)), pl.BlockSpec((1, d), lambda i: (0, 0)),
                  pl.BlockSpec((tq, d), lambda i: (i, 0))],
        out_specs=[pl.BlockSpec((tq, d), lambda i: (i, 0)), pl.BlockSpec((1, d), lambda i: (0, 0))],
        out_shape=[jax.ShapeDtypeStruct((t, d), F32), jax.ShapeDtypeStruct((1, d), F32)],
        scratch_shapes=[pltpu.VMEM((8, d), F32)],
        compiler_params=_cparams(("arbitrary",)),
    )(x, g, dy)


@jax.custom_vjp
def rmsnorm(x, g):
    return _rms_fwd_call(x, g, name="rms_fwd")


def _rmsnorm_fwd(x, g):
    return _rms_fwd_call(x, g, name="rms_fwd"), (x, g)


def _rmsnorm_bwd(saved, dy):
    x, g = saved
    dx, dg = _rms_bwd_call(x, g, dy, name="rms_bwd")
    return dx, dg


rmsnorm.defvjp(_rmsnorm_fwd, _rmsnorm_bwd)


def _loss_fwd_call(x, g, tgt, *, tq=512):
    t, d = x.shape
    nt = t // tq

    def body(x_ref, g_ref, t_ref, o_ref, acc_ref):
        i = pl.program_id(0)
        xv = x_ref[...]
        r = lax.rsqrt(jnp.mean(xv * xv, axis=-1, keepdims=True) + EPS)
        err = xv * r * g_ref[...] - t_ref[...]
        part = jnp.sum((err * err).reshape(tq // 8, 8, d), axis=0)

        @pl.when(i == 0)
        def _():
            acc_ref[...] = part

        @pl.when(i > 0)
        def _():
            acc_ref[...] += part

        @pl.when(i == nt - 1)
        def _():
            tot = jnp.sum(jnp.sum(acc_ref[...], axis=0, keepdims=True), axis=1, keepdims=True)
            o_ref[...] = jnp.broadcast_to(tot * (0.5 / d), (8, LANES))

    out = pl.pallas_call(
        body, name="loss_fwd", grid=(nt,),
        in_specs=[pl.BlockSpec((tq, d), lambda i: (i, 0)), pl.BlockSpec((1, d), lambda i: (0, 0)),
                  pl.BlockSpec((tq, d), lambda i: (i, 0))],
        out_specs=pl.BlockSpec((8, LANES), lambda i: (0, 0)),
        out_shape=jax.ShapeDtypeStruct((8, LANES), F32),
        scratch_shapes=[pltpu.VMEM((8, d), F32)],
        compiler_params=_cparams(("arbitrary",)),
    )(x, g, tgt)
    return out[0, 0]


def _loss_bwd_call(x, g, tgt, scale, *, tq=512):
    t, d = x.shape
    nt = t // tq

    def body(x_ref, g_ref, t_ref, s_ref, dx_ref, dg_ref, acc_ref):
        i = pl.program_id(0)
        xv = x_ref[...]
        r = lax.rsqrt(jnp.mean(xv * xv, axis=-1, keepdims=True) + EPS)
        xh = xv * r
        gv = g_ref[...]
        dyv = (xh * gv - t_ref[...]) * (s_ref[...] * (1.0 / d))
        dxh = dyv * gv
        dx_ref[...] = r * (dxh - xh * jnp.mean(dxh * xh, axis=-1, keepdims=True))
        part = jnp.sum((dyv * xh).reshape(tq // 8, 8, d), axis=0)

        @pl.when(i == 0)
        def _():
            acc_ref[...] = part

        @pl.when(i > 0)
        def _():
            acc_ref[...] += part

        @pl.when(i == nt - 1)
        def _():
            dg_ref[...] = jnp.sum(acc_ref[...], axis=0, keepdims=True)

    return pl.pallas_call(
        body, name="loss_bwd", grid=(nt,),
        in_specs=[pl.BlockSpec((tq, d), lambda i: (i, 0)), pl.BlockSpec((1, d), lambda i: (0, 0)),
                  pl.BlockSpec((tq, d), lambda i: (i, 0)), pl.BlockSpec((1, 1), lambda i: (0, 0))],
        out_specs=[pl.BlockSpec((tq, d), lambda i: (i, 0)), pl.BlockSpec((1, d), lambda i: (0, 0))],
        out_shape=[jax.ShapeDtypeStruct((t, d), F32), jax.ShapeDtypeStruct((1, d), F32)],
        scratch_shapes=[pltpu.VMEM((8, d), F32)],
        compiler_params=_cparams(("arbitrary",)),
    )(x, g, tgt, scale)


@jax.custom_vjp
def norm_loss(x, g, tgt):
    return _loss_fwd_call(x, g, tgt)


def _norm_loss_fwd(x, g, tgt):
    return _loss_fwd_call(x, g, tgt), (x, g, tgt)


def _norm_loss_bwd(saved, dl):
    x, g, tgt = saved
    dx, dg = _loss_bwd_call(x, g, tgt, jnp.reshape(dl, (1, 1)).astype(F32))
    return dx, dg, jnp.zeros_like(tgt)


norm_loss.defvjp(_norm_loss_fwd, _norm_loss_bwd)


def rope_tables(positions):
    inv_freq = jnp.power(ROPE_THETA, -jnp.arange(0, ROT_DIM, 2, dtype=F32) / ROT_DIM)
    ang = positions.astype(F32)[:, None] * inv_freq
    cos, sin = jnp.cos(ang), jnp.sin(ang)
    t = positions.shape[0]
    one = jnp.ones((t, HEAD_DIM - ROT_DIM), F32)
    zero8 = jnp.zeros((t, ROT_DIM // 2), F32)
    zero = jnp.zeros((t, HEAD_DIM - ROT_DIM), F32)
    a = jnp.concatenate([cos, cos, one], axis=1)
    b = jnp.concatenate([zero8, sin, zero], axis=1)
    c = jnp.concatenate([-sin, zero8, zero], axis=1)
    return tuple(jnp.concatenate([m, m], axis=1) for m in (a, b, c))


def _rope_call(x, tabs, *, transpose, name, tq=512):
    t, w = x.shape
    half = ROT_DIM // 2

    def body(x_ref, a_ref, b_ref, c_ref, o_ref):
        a, b, c = a_ref[...], b_ref[...], c_ref[...]
        for j in range(w // LANES):
            xs = x_ref[:, j * LANES:(j + 1) * LANES]
            if transpose:
                out = (xs * a + pltpu.roll(xs * b, LANES - half, 1) + pltpu.roll(xs * c, half, 1))
            else:
                out = (xs * a + pltpu.roll(xs, half, 1) * b + pltpu.roll(xs, LANES - half, 1) * c)
            o_ref[:, j * LANES:(j + 1) * LANES] = out

    tab_spec = pl.BlockSpec((tq, LANES), lambda i: (i, 0))
    return pl.pallas_call(
        body, name=name, grid=(t // tq,),
        in_specs=[pl.BlockSpec((tq, w), lambda i: (i, 0)), tab_spec, tab_spec, tab_spec],
        out_specs=pl.BlockSpec((tq, w), lambda i: (i, 0)),
        out_shape=jax.ShapeDtypeStruct((t, w), F32),
        compiler_params=_cparams(("parallel",)),
    )(x, *tabs)


@jax.custom_vjp
def rope(x, tabs):
    return _rope_call(x, tabs, transpose=False, name="rope_fwd")


def _rope_fwd(x, tabs):
    return _rope_call(x, tabs, transpose=False, name="rope_fwd"), tabs


def _rope_bwd(tabs, dy):
    return _rope_call(dy, tabs, transpose=True, name="rope_bwd"), tuple(jnp.zeros_like(m) for m in tabs)


rope.defvjp(_rope_fwd, _rope_bwd)


def _band_masks(first_block, max_dist):
    qi = lax.broadcasted_iota(jnp.int32, (BLK, BLK), 0)
    kj = lax.broadcasted_iota(jnp.int32, (BLK, BLK), 1)
    valid_prev = jnp.logical_and(kj >= qi + (BLK - max_dist), jnp.logical_not(first_block))
    valid_cur = kj <= qi
    return valid_prev, valid_cur


def _attn_specs(dil, qw, kw):
    q_spec = pl.BlockSpec((BLK, qw), lambda r, i: (i, r))
    kp_spec = pl.BlockSpec((BLK, kw), lambda r, i: (jnp.maximum(i - 1, 0), r))
    kc_spec = pl.BlockSpec((BLK, kw), lambda r, i: (i, r))
    lse_spec = pl.BlockSpec((BLK, LANES), lambda r, i: (i, r))
    return q_spec, kp_spec, kc_spec, lse_spec


_NT = (((1,), (1,)), ((), ()))
_TN = (((0,), (0,)), ((), ()))


def _attn_fwd_call(q, k, v, sink, *, dil, group, max_dist, name):
    t, qw = q.shape
    kw = k.shape[1]
    nh = qw // HEAD_DIM
    l = t // dil
    nb = l // BLK
    scale = HEAD_DIM ** -0.5
    use_sink = sink is not None

    def body(*refs):
        if use_sink:
            sink_ref, refs = refs[0], refs[1:]
        q_ref, kp_ref, kc_ref, vp_ref, vc_ref, o_ref, lse_ref = refs
        valid_prev, valid_cur = _band_masks(pl.program_id(1) == 0, max_dist)
        lane = lax.broadcasted_iota(jnp.int32, (BLK, LANES), 1)
        lse_tile = jnp.zeros((BLK, LANES), F32)
        for h in range(nh):
            kh = h // group
            ks = slice(kh * HEAD_DIM, (kh + 1) * HEAD_DIM)
            qh = q_ref[:, h * HEAD_DIM:(h + 1) * HEAD_DIM].astype(BF16)
            kp, kc = kp_ref[:, ks].astype(BF16), kc_ref[:, ks].astype(BF16)
            vp, vc = vp_ref[:, ks].astype(BF16), vc_ref[:, ks].astype(BF16)
            sp = lax.dot_general(qh, kp, _NT, preferred_element_type=F32) * scale
            sc = lax.dot_general(qh, kc, _NT, preferred_element_type=F32) * scale
            sp = jnp.where(valid_prev, sp, NEG_INF)
            sc = jnp.where(valid_cur, sc, NEG_INF)
            m = jnp.maximum(jnp.max(sp, axis=1, keepdims=True), jnp.max(sc, axis=1, keepdims=True))
            if use_sink:
                m = jnp.maximum(m, sink_ref[h])
            pp, pc = jnp.exp(sp - m), jnp.exp(sc - m)
            den = jnp.sum(pp, axis=1, keepdims=True) + jnp.sum(pc, axis=1, keepdims=True)
            if use_sink:
                den = den + jnp.exp(sink_ref[h] - m)
            o = (jnp.dot(pp.astype(BF16), vp, preferred_element_type=F32)
                 + jnp.dot(pc.astype(BF16), vc, preferred_element_type=F32))
            o_ref[:, h * HEAD_DIM:(h + 1) * HEAD_DIM] = o / den
            lse_tile = jnp.where(lane == h, m + jnp.log(den), lse_tile)
        lse_ref[...] = lse_tile

    q_spec, kp_spec, kc_spec, lse_spec = _attn_specs(dil, qw, kw)
    in_specs = [q_spec, kp_spec, kc_spec, kp_spec, kc_spec]
    args = [q.reshape(l, dil * qw), k.reshape(l, dil * kw), k.reshape(l, dil * kw),
            v.reshape(l, dil * kw), v.reshape(l, dil * kw)]
    if use_sink:
        in_specs = [pl.BlockSpec(memory_space=pltpu.SMEM)] + in_specs
        args = [sink] + args
    o, lse = pl.pallas_call(
        body, name=name, grid=(dil, nb),
        in_specs=in_specs,
        out_specs=[q_spec, lse_spec],
        out_shape=[jax.ShapeDtypeStruct((l, dil * qw), F32), jax.ShapeDtypeStruct((l, dil * LANES), F32)],
        compiler_params=_cparams(("parallel", "parallel")),
    )(*args)
    return o.reshape(t, qw), lse.reshape(t, LANES)


def _attn_bwd_call(q, k, v, sink, o, lse, do, dlse, *, dil, group, max_dist, name):
    t, qw = q.shape
    kw = k.shape[1]
    nh = qw // HEAD_DIM
    l = t // dil
    nb = l // BLK
    scale = HEAD_DIM ** -0.5
    use_sink = sink is not None

    def body(*refs):
        if use_sink:
            sink_ref, refs = refs[0], refs[1:]
        (q_ref, kp_ref, kc_ref, vp_ref, vc_ref, o_ref, lse_ref, do_ref, dlse_ref,
         dq_ref, dkc_ref, dkp_ref, dvc_ref, dvp_ref, dsink_ref) = refs
        first = jnp.logical_and(pl.program_id(0) == 0, pl.program_id(1) == 0)
        valid_prev, valid_cur = _band_masks(pl.program_id(1) == 0, max_dist)
        row = lax.broadcasted_iota(jnp.int32, (8, LANES), 0)
        lanes8 = lax.broadcasted_iota(jnp.int32, (8, LANES), 1)
        ds_tile = jnp.zeros((8, LANES), F32)
        acc = {}
        for h in range(nh):
            kh = h // group
            ks = slice(kh * HEAD_DIM, (kh + 1) * HEAD_DIM)
            hs = slice(h * HEAD_DIM, (h + 1) * HEAD_DIM)
            qh = q_ref[:, hs].astype(BF16)
            kp, kc = kp_ref[:, ks].astype(BF16), kc_ref[:, ks].astype(BF16)
            vp, vc = vp_ref[:, ks].astype(BF16), vc_ref[:, ks].astype(BF16)
            doh = do_ref[:, hs]
            dob = doh.astype(BF16)
            lse_h = lse_ref[:, h:h + 1]
            sp = lax.dot_general(qh, kp, _NT, preferred_element_type=F32) * scale
            sc = lax.dot_general(qh, kc, _NT, preferred_element_type=F32) * scale
            pp = jnp.where(valid_prev, jnp.exp(jnp.where(valid_prev, sp, NEG_INF) - lse_h), 0.0)
            pc = jnp.where(valid_cur, jnp.exp(jnp.where(valid_cur, sc, NEG_INF) - lse_h), 0.0)
            delta = jnp.sum(doh * o_ref[:, hs], axis=1, keepdims=True)
            corr = dlse_ref[:, h:h + 1] - delta
            dsp = pp * (lax.dot_general(dob, vp, _NT, preferred_element_type=F32) + corr)
            dsc = pc * (lax.dot_general(dob, vc, _NT, preferred_element_type=F32) + corr)
            dspb, dscb = dsp.astype(BF16), dsc.astype(BF16)
            dq_ref[:, hs] = (jnp.dot(dspb, kp, preferred_element_type=F32)
                             + jnp.dot(dscb, kc, preferred_element_type=F32)) * scale
            parts = (lax.dot_general(dscb, qh, _TN, preferred_element_type=F32) * scale,
                     lax.dot_general(dspb, qh, _TN, preferred_element_type=F32) * scale,
                     lax.dot_general(pc.astype(BF16), dob, _TN, preferred_element_type=F32),
                     lax.dot_general(pp.astype(BF16), dob, _TN, preferred_element_type=F32))
            if kh in acc:
                acc[kh] = tuple(x + y for x, y in zip(acc[kh], parts))
            else:
                acc[kh] = parts
            if h % group == group - 1:
                for ref, val in zip((dkc_ref, dkp_ref, dvc_ref, dvp_ref), acc[kh]):
                    ref[:, ks] = val
            if use_sink:
                ps = jnp.exp(sink_ref[h] - lse_h)
                val = -jnp.sum(ps * delta, axis=0, keepdims=True)
                ds_tile = jnp.where(jnp.logical_and(row == 0, lanes8 == h), val, ds_tile)

        @pl.when(first)
        def _():
            dsink_ref[...] = ds_tile

        @pl.when(jnp.logical_not(first))
        def _():
            dsink_ref[...] += ds_tile

    q_spec, kp_spec, kc_spec, lse_spec = _attn_specs(dil, qw, kw)
    in_specs = [q_spec, kp_spec, kc_spec, kp_spec, kc_spec, q_spec, lse_spec, q_spec, lse_spec]
    k2, v2 = k.reshape(l, dil * kw), v.reshape(l, dil * kw)
    args = [q.reshape(l, dil * qw), k2, k2, v2, v2, o.reshape(l, dil * qw), lse.reshape(l, dil * LANES),
            do.reshape(l, dil * qw), dlse.reshape(l, dil * LANES)]
    if use_sink:
        in_specs = [pl.BlockSpec(memory_space=pltpu.SMEM)] + in_specs
        args = [sink] + args
    kv_shape = jax.ShapeDtypeStruct((l, dil * kw), F32)
    outs = pl.pallas_call(
        body, name=name, grid=(dil, nb),
        in_specs=in_specs,
        out_specs=[q_spec, kc_spec, kc_spec, kc_spec, kc_spec, pl.BlockSpec((8, LANES), lambda r, i: (0, 0))],
        out_shape=[jax.ShapeDtypeStruct((l, dil * qw), F32), kv_shape, kv_shape, kv_shape, kv_shape,
                   jax.ShapeDtypeStruct((8, LANES), F32)],
        compiler_params=_cparams(("arbitrary", "arbitrary")),
    )(*args)
    return outs


def _shift_add_call(cur, prev, *, dil, kw, name):
    l = cur.shape[0]
    nb = l // BLK

    def body(c_ref, p_ref, o_ref):
        last = pl.program_id(1) == nb - 1
        o_ref[...] = c_ref[...] + jnp.where(last, 0.0, p_ref[...])

    spec = pl.BlockSpec((BLK, kw), lambda r, i: (i, r))
    nxt = pl.BlockSpec((BLK, kw), lambda r, i: (jnp.minimum(i + 1, nb - 1), r))
    return pl.pallas_call(
        body, name=name, grid=(dil, nb), in_specs=[spec, nxt], out_specs=spec,
        out_shape=jax.ShapeDtypeStruct(cur.shape, F32),
        compiler_params=_cparams(("parallel", "parallel")),
    )(cur, prev)


def _make_attention(dil, group, max_dist, use_sink, tag):
    kwargs = dict(dil=dil, group=group, max_dist=max_dist)

    @jax.custom_vjp
    def attn(q, k, v, sink):
        return _attn_fwd_call(q, k, v, sink if use_sink else None, name=tag + "_fwd", **kwargs)

    def fwd(q, k, v, sink):
        o, lse = _attn_fwd_call(q, k, v, sink if use_sink else None, name=tag + "_fwd", **kwargs)
        return (o, lse), (q, k, v, sink, o, lse)

    def bwd(saved, cts):
        q, k, v, sink, o, lse = saved
        do, dlse = cts
        t, kw = k.shape
        dq, dkc, dkp, dvc, dvp, dsink = _attn_bwd_call(
            q, k, v, sink if use_sink else None, o, lse, do, dlse, name=tag + "_bwd", **kwargs)
        dk = _shift_add_call(dkc, dkp, dil=dil, kw=kw, name=tag + "_dk").reshape(t, kw)
        dv = _shift_add_call(dvc, dvp, dil=dil, kw=kw, name=tag + "_dv").reshape(t, kw)
        return dq.reshape(q.shape), dk, dv, dsink[0, :sink.shape[0]]

    attn.defvjp(fwd, bwd)
    return attn


ATTN_A = tuple(_make_attention(dil, 1, window // dil, False, "attn_a%d" % dil) for window, dil in A_CONFIGS)
ATTN_B = _make_attention(1, B_GROUP, BLK - 1, True, "attn_b")


def _head_expand():
    r = lax.broadcasted_iota(jnp.int32, (LANES, 8 * HEAD_DIM), 0)
    c = lax.broadcasted_iota(jnp.int32, (LANES, 8 * HEAD_DIM), 1)
    return (c // HEAD_DIM == r).astype(F32)


def _combine_weights(l0, l1, l2):
    m = jnp.maximum(jnp.maximum(l0, l1), l2)
    e0, e1, e2 = jnp.exp(l0 - m), jnp.exp(l1 - m), jnp.exp(l2 - m)
    inv = 1.0 / (e0 + e1 + e2)
    return e0 * inv, e1 * inv, e2 * inv


def _combine_fwd_call(os_, lses, *, tq=256):
    t, w = os_[0].shape

    def body(o0, o1, o2, l0, l1, l2, y_ref):
        ws = _combine_weights(l0[...], l1[...], l2[...])
        e = _head_expand()
        y = jnp.zeros((tq, w), F32)
        for o_ref, wt in zip((o0, o1, o2), ws):
            y = y + jnp.dot(wt, e, precision=HIGHEST, preferred_element_type=F32) * o_ref[...]
        y_ref[...] = y

    o_spec = pl.BlockSpec((tq, w), lambda i: (i, 0))
    l_spec = pl.BlockSpec((tq, LANES), lambda i: (i, 0))
    return pl.pallas_call(
        body, name="combine_fwd", grid=(t // tq,),
        in_specs=[o_spec] * 3 + [l_spec] * 3, out_specs=o_spec,
        out_shape=jax.ShapeDtypeStruct((t, w), F32),
        compiler_params=_cparams(("parallel",)),
    )(*os_, *lses)


def _combine_bwd_call(os_, lses, dy, *, tq=256):
    t, w = dy.shape

    def body(o0, o1, o2, l0, l1, l2, dy_ref, do0, do1, do2, dl0, dl1, dl2):
        ws = _combine_weights(l0[...], l1[...], l2[...])
        e = _head_expand()
        dyv = dy_ref[...]
        dws = []
        for o_ref, do_ref, wt in zip((o0, o1, o2), (do0, do1, do2), ws):
            do_ref[...] = jnp.dot(wt, e, precision=HIGHEST, preferred_element_type=F32) * dyv
            dws.append(lax.dot_general(dyv * o_ref[...], e, _NT, precision=HIGHEST,
                                       preferred_element_type=F32))
        mean = ws[0] * dws[0] + ws[1] * dws[1] + ws[2] * dws[2]
        for dl_ref, wt, dw in zip((dl0, dl1, dl2), ws, dws):
            dl_ref[...] = wt * (dw - mean)

    o_spec = pl.BlockSpec((tq, w), lambda i: (i, 0))
    l_spec = pl.BlockSpec((tq, LANES), lambda i: (i, 0))
    o_shape = jax.ShapeDtypeStruct((t, w), F32)
    l_shape = jax.ShapeDtypeStruct((t, LANES), F32)
    return pl.pallas_call(
        body, name="combine_bwd", grid=(t // tq,),
        in_specs=[o_spec] * 3 + [l_spec] * 3 + [o_spec], out_specs=[o_spec] * 3 + [l_spec] * 3,
        out_shape=[o_shape] * 3 + [l_shape] * 3,
        compiler_params=_cparams(("parallel",)),
    )(*os_, *lses, dy)


@jax.custom_vjp
def combine(os_, lses):
    return _combine_fwd_call(os_, lses)


def _combine_fwd(os_, lses):
    return _combine_fwd_call(os_, lses), (os_, lses)


def _combine_bwd(saved, dy):
    os_, lses = saved
    outs = _combine_bwd_call(os_, lses, dy)
    return tuple(outs[:3]), tuple(outs[3:])


combine.defvjp(_combine_fwd, _combine_bwd)


C_QKW = C_QK_HEADS * C_DK
C_CONV_W = 2 * C_QKW + C_V_HEADS * C_DK
HALO = 8


def _silu_parts(z):
    sig = jax.nn.sigmoid(z)
    return z * sig, sig * (1.0 + z * (1.0 - sig))


def _conv_prep_fwd_call(x, w, *, tq=256):
    t, c = x.shape
    nqk = 2 * C_QK_HEADS

    def body(x_ref, halo_ref, w_ref, z_ref, qk_ref, v_ref):
        i = pl.program_id(0)
        halo = jnp.where(i == 0, 0.0, halo_ref[...])
        xc = jnp.concatenate([halo, x_ref[...]], axis=0)
        wv = w_ref[...]
        z = xc[HALO - 3:HALO - 3 + tq] * wv[0:1]
        for j in range(1, C_CONV):
            z = z + xc[HALO - 3 + j:HALO - 3 + j + tq] * wv[j:j + 1]
        z_ref[...] = z
        act, _ = _silu_parts(z)
        for h in range(nqk):
            a = act[:, h * C_DK:(h + 1) * C_DK]
            qk_ref[:, h * C_DK:(h + 1) * C_DK] = a * lax.rsqrt(jnp.sum(a * a, axis=1, keepdims=True) + EPS)
        v_ref[...] = act[:, nqk * C_DK:]

    return pl.pallas_call(
        body, name="conv_prep_fwd", grid=(t // tq,),
        in_specs=[pl.BlockSpec((tq, c), lambda i: (i, 0)),
                  pl.BlockSpec((HALO, c), lambda i: (jnp.maximum(i * (tq // HALO) - 1, 0), 0)),
                  pl.BlockSpec((C_CONV, c), lambda i: (0, 0))],
        out_specs=[pl.BlockSpec((tq, c), lambda i: (i, 0)),
                   pl.BlockSpec((tq, 2 * C_QKW), lambda i: (i, 0)),
                   pl.BlockSpec((tq, c - 2 * C_QKW), lambda i: (i, 0))],
        out_shape=[jax.ShapeDtypeStruct((t, c), F32), jax.ShapeDtypeStruct((t, 2 * C_QKW), F32),
                   jax.ShapeDtypeStruct((t, c - 2 * C_QKW), F32)],
        compiler_params=_cparams(("parallel",)),
    )(x, x, w)


def _conv_prep_dz_call(z, dqk, dv, *, tq=256):
    t, c = z.shape
    nqk = 2 * C_QK_HEADS

    def body(z_ref, dqk_ref, dv_ref, dz_ref):
        zv = z_ref[...]
        act, dact = _silu_parts(zv)
        for h in range(nqk):
            hs = slice(h * C_DK, (h + 1) * C_DK)
            a = act[:, hs]
            r = lax.rsqrt(jnp.sum(a * a, axis=1, keepdims=True) + EPS)
            nrm = a * r
            dn = dqk_ref[:, hs]
            da = r * (dn - nrm * jnp.sum(dn * nrm, axis=1, keepdims=True))
            dz_ref[:, hs] = da * dact[:, hs]
        dz_ref[:, nqk * C_DK:] = dv_ref[...] * dact[:, nqk * C_DK:]

    return pl.pallas_call(
        body, name="conv_prep_dz", grid=(t // tq,),
        in_specs=[pl.BlockSpec((tq, c), lambda i: (i, 0)),
                  pl.BlockSpec((tq, 2 * C_QKW), lambda i: (i, 0)),
                  pl.BlockSpec((tq, c - 2 * C_QKW), lambda i: (i, 0))],
        out_specs=pl.BlockSpec((tq, c), lambda i: (i, 0)),
        out_shape=jax.ShapeDtypeStruct((t, c), F32),
        compiler_params=_cparams(("parallel",)),
    )(z, dqk, dv)


def _conv_bwd_call(x, dz, w, *, tq=256):
    t, c = x.shape
    nt = t // tq

    def body(x_ref, xh_ref, dz_ref, dzh_ref, w_ref, dx_ref, dw_ref):
        i = pl.program_id(0)
        xc = jnp.concatenate([jnp.where(i == 0, 0.0, xh_ref[...]), x_ref[...]], axis=0)
        dzv = dz_ref[...]
        dzc = jnp.concatenate([dzv, jnp.where(i == nt - 1, 0.0, dzh_ref[...])], axis=0)
        wv = w_ref[...]
        dx = dzv * wv[3:4]
        for s in range(1, C_CONV):
            dx = dx + dzc[s:s + tq] * wv[3 - s:4 - s]
        dx_ref[...] = dx
        row = lax.broadcasted_iota(jnp.int32, (8, c), 0)
        dw = jnp.zeros((8, c), F32)
        for j in range(C_CONV):
            prod = dzv * xc[HALO - 3 + j:HALO - 3 + j + tq]
            col = jnp.sum(jnp.sum(prod.reshape(tq // 8, 8, c), axis=0), axis=0, keepdims=True)
            dw = jnp.where(row == j, col, dw)

        @pl.when(i == 0)
        def _():
            dw_ref[...] = dw

        @pl.when(i > 0)
        def _():
            dw_ref[...] += dw

    blk = pl.BlockSpec((tq, c), lambda i: (i, 0))
    return pl.pallas_call(
        body, name="conv_bwd", grid=(nt,),
        in_specs=[blk, pl.BlockSpec((HALO, c), lambda i: (jnp.maximum(i * (tq // HALO) - 1, 0), 0)),
                  blk, pl.BlockSpec((HALO, c), lambda i: (jnp.minimum((i + 1) * (tq // HALO), t // HALO - 1), 0)),
                  pl.BlockSpec((C_CONV, c), lambda i: (0, 0))],
        out_specs=[blk, pl.BlockSpec((8, c), lambda i: (0, 0))],
        out_shape=[jax.ShapeDtypeStruct((t, c), F32), jax.ShapeDtypeStruct((8, c), F32)],
        compiler_params=_cparams(("arbitrary",)),
    )(x, x, dz, dz, w)


@jax.custom_vjp
def conv_prep(x, w):
    _, qk, v = _conv_prep_fwd_call(x, w)
    return qk, v


def _conv_prep_fwd(x, w):
    z, qk, v = _conv_prep_fwd_call(x, w)
    return (qk, v), (x, w, z)


def _conv_prep_bwd(saved, cts):
    x, w, z = saved
    dz = _conv_prep_dz_call(z, cts[0], cts[1])
    dx, dw = _conv_bwd_call(x, dz, w)
    return dx, dw[:C_CONV]


conv_prep.defvjp(_conv_prep_fwd, _conv_prep_bwd)


C_VW = C_V_HEADS * C_DK


def _softplus(x):
    return jnp.maximum(x, 0.0) + jnp.log(1.0 + jnp.exp(-jnp.abs(x)))


def _tri_masks():
    r = lax.broadcasted_iota(jnp.int32, (CHUNK, CHUNK), 0)
    c = lax.broadcasted_iota(jnp.int32, (CHUNK, CHUNK), 1)
    return r >= c, r > c


def _dot_hi(a, b, dims=None):
    if dims is None:
        return jnp.dot(a, b, precision=HIGHEST, preferred_element_type=F32)
    return lax.dot_general(a, b, dims, precision=HIGHEST, preferred_element_type=F32)


def _unit_lower_inverse(a):
    r = lax.broadcasted_iota(jnp.int32, (CHUNK, CHUNK), 0)
    c = lax.broadcasted_iota(jnp.int32, (CHUNK, CHUNK), 1)
    x = (r == c).astype(F32) - a
    p = _dot_hi(a, a)
    steps = int(math.log2(CHUNK)) - 1
    for s in range(steps):
        x = x + _dot_hi(x, p)
        if s < steps - 1:
            p = _dot_hi(p, p)
    return x


def _gate_tiles(cab, alog, dtb):
    pre = cab + dtb
    g = -jnp.exp(alog) * _softplus(pre)
    beta = jax.nn.sigmoid(pltpu.roll(cab, LANES - C_V_HEADS, 1))
    return g, beta, pre


def _chunk_common(kk, qk, gc, gct, beta, h, tri, strict):
    gcol, grow, bcol = gc[:, h:h + 1], gct[h:h + 1, :], beta[:, h:h + 1]
    decay = jnp.where(tri, jnp.exp(jnp.where(tri, gcol - grow, 0.0)), 0.0)
    kkd = jnp.where(strict, kk * decay, 0.0)
    attn = jnp.where(tri, qk * decay, 0.0)
    glast = gc[CHUNK - 1:CHUNK, h:h + 1]
    return gcol, bcol, decay, kkd, attn, glast


def _delta_prep_call(qk, v, cab, alog, dtb):
    t = qk.shape[0]
    nc = t // CHUNK
    scale = C_DK ** -0.5

    def body(q_ref, k_ref, v_ref, cab_ref, alog_ref, dtb_ref,
             u_ref, w_ref, qd_ref, kd_ref, attn_ref, tmat_ref, gc_ref, beta_ref):
        tri, strict = _tri_masks()
        g, beta, _ = _gate_tiles(cab_ref[...], alog_ref[...], dtb_ref[...])
        gc = _dot_hi(tri.astype(F32), g)
        gct = gc.T
        gc_ref[...] = gc
        beta_ref[...] = beta
        for j in range(C_QK_HEADS):
            js = slice(j * C_DK, (j + 1) * C_DK)
            kf, qf = k_ref[:, js], q_ref[:, js] * scale
            kb, qb = kf.astype(BF16), qf.astype(BF16)
            kk = lax.dot_general(kb, kb, _NT, preferred_element_type=F32)
            qk = lax.dot_general(qb, kb, _NT, preferred_element_type=F32)
            for h in (2 * j, 2 * j + 1):
                hs = slice(h * C_DK, (h + 1) * C_DK)
                gcol, bcol, decay, kkd, attn, glast = _chunk_common(kk, qk, gc, gct, beta, h, tri, strict)
                tmat = _unit_lower_inverse(kkd * bcol)
                gexp = jnp.exp(gcol)
                u_ref[:, hs] = _dot_hi(tmat, v_ref[:, hs] * bcol)
                w_ref[:, hs] = _dot_hi(tmat, kf * (bcol * gexp))
                qd_ref[:, hs] = qf * gexp
                kd_ref[:, hs] = kf * jnp.exp(glast - gcol)
                attn_ref[:, h * CHUNK:(h + 1) * CHUNK] = attn
                tmat_ref[:, h * CHUNK:(h + 1) * CHUNK] = tmat

    def blk(w):
        return pl.BlockSpec((CHUNK, w), lambda n: (n, 0))

    row = pl.BlockSpec((1, LANES), lambda n: (0, 0))
    big = jax.ShapeDtypeStruct((t, C_VW), F32)
    sq = jax.ShapeDtypeStruct((t, C_V_HEADS * CHUNK), F32)
    tile = jax.ShapeDtypeStruct((t, LANES), F32)
    return pl.pallas_call(
        body, name="delta_prep", grid=(nc,),
        in_specs=[blk(C_QKW), pl.BlockSpec((CHUNK, C_QKW), lambda n: (n, 1)), blk(C_VW), blk(LANES), row, row],
        out_specs=[blk(C_VW)] * 4 + [blk(C_V_HEADS * CHUNK)] * 2 + [blk(LANES)] * 2,
        out_shape=[big] * 4 + [sq] * 2 + [tile] * 2,
        compiler_params=_cparams(("parallel",)),
    )(qk, qk, v, cab, alog, dtb)


def _delta_scan_call(u, w, qd, kd, attn, gc):
    t = u.shape[0]
    nc = t // CHUNK

    def body(u_ref, w_ref, qd_ref, kd_ref, attn_ref, gc_ref, o_ref, vn_ref, st_ref, s_ref):
        @pl.when(pl.program_id(0) == 0)
        def _():
            s_ref[...] = jnp.zeros_like(s_ref)

        for h in range(C_V_HEADS):
            hs = slice(h * C_DK, (h + 1) * C_DK)
            s = s_ref[hs, :]
            st_ref[0, hs, :] = s
            sb = s.astype(BF16)
            vn = u_ref[:, hs] - jnp.dot(w_ref[:, hs].astype(BF16), sb, preferred_element_type=F32)
            vnb = vn.astype(BF16)
            vn_ref[:, hs] = vn
            o_ref[:, hs] = (jnp.dot(qd_ref[:, hs].astype(BF16), sb, preferred_element_type=F32)
                            + jnp.dot(attn_ref[:, h * CHUNK:(h + 1) * CHUNK].astype(BF16), vnb,
                                      preferred_element_type=F32))
            glast = jnp.exp(gc_ref[CHUNK - 1:CHUNK, h:h + 1])
            s_ref[hs, :] = s * glast + lax.dot_general(kd_ref[:, hs].astype(BF16), vnb, _TN,
                                                       preferred_element_type=F32)

    def blk(wd):
        return pl.BlockSpec((CHUNK, wd), lambda n: (n, 0))

    big = jax.ShapeDtypeStruct((t, C_VW), F32)
    return pl.pallas_call(
        body, name="delta_scan", grid=(nc,),
        in_specs=[blk(C_VW)] * 4 + [blk(C_V_HEADS * CHUNK), blk(LANES)],
        out_specs=[blk(C_VW), blk(C_VW), pl.BlockSpec((1, C_VW, C_DK), lambda n: (n, 0, 0))],
        out_shape=[big, big, jax.ShapeDtypeStruct((nc, C_VW, C_DK), F32)],
        scratch_shapes=[pltpu.VMEM((C_VW, C_DK), F32)],
        compiler_params=_cparams(("arbitrary",)),
    )(u, w, qd, kd, attn, gc)


def _delta_scan_bwd_call(do, w, qd, kd, attn, gc, vn, st):
    t = do.shape[0]
    nc = t // CHUNK

    def body(do_ref, w_ref, qd_ref, kd_ref, attn_ref, gc_ref, vn_ref, st_ref,
             du_ref, dw_ref, dqd_ref, dkd_ref, dattn_ref, dgl_ref, ds_ref):
        @pl.when(pl.program_id(0) == 0)
        def _():
            ds_ref[...] = jnp.zeros_like(ds_ref)

        tri, _ = _tri_masks()
        row = lax.broadcasted_iota(jnp.int32, (8, LANES), 0)
        lane = lax.broadcasted_iota(jnp.int32, (8, LANES), 1)
        dgl = jnp.zeros((8, LANES), F32)
        for h in range(C_V_HEADS):
            hs = slice(h * C_DK, (h + 1) * C_DK)
            cs = slice(h * CHUNK, (h + 1) * CHUNK)
            s = st_ref[0, hs, :]
            dsp = ds_ref[hs, :]
            sb, dspb = s.astype(BF16), dsp.astype(BF16)
            dob = do_ref[:, hs].astype(BF16)
            vnb = vn_ref[:, hs].astype(BF16)
            dvn = (lax.dot_general(attn_ref[:, cs].astype(BF16), dob, _TN, preferred_element_type=F32)
                   + jnp.dot(kd_ref[:, hs].astype(BF16), dspb, preferred_element_type=F32))
            dvnb = dvn.astype(BF16)
            du_ref[:, hs] = dvn
            dw_ref[:, hs] = -lax.dot_general(dvnb, sb, _NT, preferred_element_type=F32)
            dqd_ref[:, hs] = lax.dot_general(dob, sb, _NT, preferred_element_type=F32)
            dkd_ref[:, hs] = lax.dot_general(vnb, dspb, _NT, preferred_element_type=F32)
            dattn_ref[:, cs] = jnp.where(tri, lax.dot_general(dob, vnb, _NT, preferred_element_type=F32), 0.0)
            tot = jnp.sum(jnp.sum(dsp * s, axis=0, keepdims=True), axis=1, keepdims=True)
            dgl = jnp.where(jnp.logical_and(row == 0, lane == h), tot, dgl)
            glast = jnp.exp(gc_ref[CHUNK - 1:CHUNK, h:h + 1])
            ds_ref[hs, :] = (lax.dot_general(qd_ref[:, hs].astype(BF16), dob, _TN, preferred_element_type=F32)
                             + glast * dsp
                             - lax.dot_general(w_ref[:, hs].astype(BF16), dvnb, _TN, preferred_element_type=F32))
        dgl_ref[...] = dgl

    def blk(wd):
        return pl.BlockSpec((CHUNK, wd), lambda n: (nc - 1 - n, 0))

    big = jax.ShapeDtypeStruct((t, C_VW), F32)
    return pl.pallas_call(
        body, name="delta_scan_bwd", grid=(nc,),
        in_specs=[blk(C_VW)] * 4 + [blk(C_V_HEADS * CHUNK), blk(LANES), blk(C_VW),
                                    pl.BlockSpec((1, C_VW, C_DK), lambda n: (nc - 1 - n, 0, 0))],
        out_specs=[blk(C_VW)] * 4 + [blk(C_V_HEADS * CHUNK), pl.BlockSpec((8, LANES), lambda n: (nc - 1 - n, 0))],
        out_shape=[big] * 4 + [jax.ShapeDtypeStruct((t, C_V_HEADS * CHUNK), F32),
                               jax.ShapeDtypeStruct((nc * 8, LANES), F32)],
        scratch_shapes=[pltpu.VMEM((C_VW, C_DK), F32)],
        compiler_params=_cparams(("arbitrary",)),
    )(do, w, qd, kd, attn, gc, vn, st)


def _delta_prep_bwd_call(qk, v, cab, alog, dtb, tmat, u, w, gc, beta, du, dw, dqd, dkd, dattn, dgl):
    t = qk.shape[0]
    nc = t // CHUNK
    scale = C_DK ** -0.5

    def body(q_ref, k_ref, v_ref, cab_ref, alog_ref, dtb_ref, tmat_ref, u_ref, w_ref, gc_ref, beta_ref,
             du_ref, dw_ref, dqd_ref, dkd_ref, dattn_ref, dgl_ref,
             dqk_ref, dv_ref, dcab_ref, dpar_ref):
        tri, strict = _tri_masks()
        gc, beta = gc_ref[...], beta_ref[...]
        gct = gc.T
        ones = jnp.ones((CHUNK, LANES), F32)
        lane = lax.broadcasted_iota(jnp.int32, (CHUNK, LANES), 1)
        rowi = lax.broadcasted_iota(jnp.int32, (CHUNK, 1), 0)
        dgc_tile = jnp.zeros((CHUNK, LANES), F32)
        db_tile = jnp.zeros((CHUNK, LANES), F32)
        for j in range(C_QK_HEADS):
            js = slice(j * C_DK, (j + 1) * C_DK)
            kf, qf = k_ref[:, js], q_ref[:, js] * scale
            kb, qb = kf.astype(BF16), qf.astype(BF16)
            kk = lax.dot_general(kb, kb, _NT, preferred_element_type=F32)
            qk = lax.dot_general(qb, kb, _NT, preferred_element_type=F32)
            dq_acc = jnp.zeros((CHUNK, C_DK), F32)
            dk_acc = jnp.zeros((CHUNK, C_DK), F32)
            for h in (2 * j, 2 * j + 1):
                hs = slice(h * C_DK, (h + 1) * C_DK)
                cs = slice(h * CHUNK, (h + 1) * CHUNK)
                gcol, bcol, decay, kkd, attn, glast = _chunk_common(kk, qk, gc, gct, beta, h, tri, strict)
                gexp = jnp.exp(gcol)
                fdec = jnp.exp(glast - gcol)
                tmat = tmat_ref[:, cs]
                uu, ww = u_ref[:, hs], w_ref[:, hs]
                dvb = _dot_hi(tmat, du_ref[:, hs], _TN)
                dkb = _dot_hi(tmat, dw_ref[:, hs], _TN)
                da = -jnp.where(strict, _dot_hi(dvb, uu, _NT) + _dot_hi(dkb, ww, _NT), 0.0)
                dattn_h = dattn_ref[:, cs]
                dkk = (da * decay * bcol).astype(BF16)
                dqk = (dattn_h * decay).astype(BF16)
                e = da * kkd * bcol + dattn_h * attn
                dgc = jnp.sum(e, axis=1, keepdims=True) - _dot_hi(e, ones, _TN)[:, :1]
                dk_acc = (dk_acc + jnp.dot(dkk, kb, preferred_element_type=F32)
                          + lax.dot_general(dkk, kb, _TN, preferred_element_type=F32)
                          + lax.dot_general(dqk, qb, _TN, preferred_element_type=F32)
                          + dkb * (bcol * gexp) + dkd_ref[:, hs] * fdec)
                dq_acc = dq_acc + jnp.dot(dqk, kb, preferred_element_type=F32) + dqd_ref[:, hs] * gexp
                dv_ref[:, hs] = dvb * bcol
                s_kb = jnp.sum(dkb * kf, axis=1, keepdims=True)
                db = (jnp.sum(da * kkd, axis=1, keepdims=True) + jnp.sum(dvb * v_ref[:, hs], axis=1, keepdims=True)
                      + s_kb * gexp)
                rho = jnp.sum(dkd_ref[:, hs] * kf, axis=1, keepdims=True) * fdec
                dgc = (dgc + s_kb * bcol * gexp + jnp.sum(dqd_ref[:, hs] * qf, axis=1, keepdims=True) * gexp - rho)
                last = jnp.sum(rho, axis=0, keepdims=True) + dgl_ref[0:1, h:h + 1] * jnp.exp(glast)
                dgc = dgc + jnp.where(rowi == CHUNK - 1, last, 0.0)
                dgc_tile = jnp.where(lane == h, dgc, dgc_tile)
                db_tile = jnp.where(lane == h, db, db_tile)
            dqk_ref[:, js] = dq_acc * scale
            dqk_ref[:, C_QKW + j * C_DK:C_QKW + (j + 1) * C_DK] = dk_acc
        dg = _dot_hi(jnp.logical_not(strict).astype(F32), dgc_tile)
        alog = alog_ref[...]
        g, _, pre = _gate_tiles(cab_ref[...], alog, dtb_ref[...])
        dca = dg * (-jnp.exp(alog)) * jax.nn.sigmoid(pre)
        dcab_ref[...] = dca + pltpu.roll(db_tile * beta * (1.0 - beta), C_V_HEADS, 1)
        row8 = lax.broadcasted_iota(jnp.int32, (8, LANES), 0)
        par = jnp.where(row8 == 0, jnp.sum(dg * g, axis=0, keepdims=True),
                        jnp.where(row8 == 1, jnp.sum(dca, axis=0, keepdims=True), 0.0))

        @pl.when(pl.program_id(0) == 0)
        def _():
            dpar_ref[...] = par

        @pl.when(pl.program_id(0) > 0)
        def _():
            dpar_ref[...] += par

    def blk(wd):
        return pl.BlockSpec((CHUNK, wd), lambda n: (n, 0))

    row = pl.BlockSpec((1, LANES), lambda n: (0, 0))
    sq = blk(C_V_HEADS * CHUNK)
    return pl.pallas_call(
        body, name="delta_prep_bwd", grid=(nc,),
        in_specs=[blk(C_QKW), pl.BlockSpec((CHUNK, C_QKW), lambda n: (n, 1)), blk(C_VW), blk(LANES), row, row, sq,
                  blk(C_VW), blk(C_VW),
                  blk(LANES), blk(LANES), blk(C_VW), blk(C_VW), blk(C_VW), blk(C_VW), sq,
                  pl.BlockSpec((8, LANES), lambda n: (n, 0))],
        out_specs=[blk(2 * C_QKW), blk(C_VW), blk(LANES), pl.BlockSpec((8, LANES), lambda n: (0, 0))],
        out_shape=[jax.ShapeDtypeStruct((t, 2 * C_QKW), F32),
                   jax.ShapeDtypeStruct((t, C_VW), F32), jax.ShapeDtypeStruct((t, LANES), F32),
                   jax.ShapeDtypeStruct((8, LANES), F32)],
        compiler_params=_cparams(("arbitrary",)),
    )(qk, qk, v, cab, alog, dtb, tmat, u, w, gc, beta, du, dw, dqd, dkd, dattn, dgl)


@jax.custom_vjp
def delta_rule(qk, v, cab, alog, dtb):
    u, w, qd, kd, attn, _, gc, _ = _delta_prep_call(qk, v, cab, alog, dtb)
    return _delta_scan_call(u, w, qd, kd, attn, gc)[0]


def _delta_rule_fwd(qk, v, cab, alog, dtb):
    u, w, qd, kd, attn, tmat, gc, beta = _delta_prep_call(qk, v, cab, alog, dtb)
    o, vn, st = _delta_scan_call(u, w, qd, kd, attn, gc)
    return o, (qk, v, cab, alog, dtb, u, w, qd, kd, attn, tmat, gc, beta, vn, st)


def _delta_rule_bwd(saved, do):
    qk, v, cab, alog, dtb, u, w, qd, kd, attn, tmat, gc, beta, vn, st = saved
    du, dw, dqd, dkd, dattn, dgl = _delta_scan_bwd_call(do, w, qd, kd, attn, gc, vn, st)
    dqk, dv, dcab, dpar = _delta_prep_bwd_call(qk, v, cab, alog, dtb, tmat, u, w, gc, beta,
                                               du, dw, dqd, dkd, dattn, dgl)
    return dqk, dv, dcab, dpar[0:1], dpar[1:2]


delta_rule.defvjp(_delta_rule_fwd, _delta_rule_bwd)


def _gated_norm_fwd_call(o, z, gain, *, tq=256):
    t, w = o.shape

    def body(o_ref, z_ref, g_ref, y_ref):
        act, _ = _silu_parts(z_ref[...])
        gv = g_ref[...]
        for h in range(C_V_HEADS):
            hs = slice(h * C_DK, (h + 1) * C_DK)
            ov = o_ref[:, hs]
            r = lax.rsqrt(jnp.mean(ov * ov, axis=1, keepdims=True) + EPS)
            y_ref[:, hs] = ov * r * gv * act[:, hs]

    blk = pl.BlockSpec((tq, w), lambda i: (i, 0))
    return pl.pallas_call(
        body, name="gated_norm_fwd", grid=(t // tq,),
        in_specs=[blk, blk, pl.BlockSpec((1, C_DK), lambda i: (0, 0))], out_specs=blk,
        out_shape=jax.ShapeDtypeStruct((t, w), F32),
        compiler_params=_cparams(("parallel",)),
    )(o, z, gain)


def _gated_norm_bwd_call(o, z, gain, dy, *, tq=256):
    t, w = o.shape
    nt = t // tq

    def body(o_ref, z_ref, g_ref, dy_ref, do_ref, dz_ref, dg_ref, acc_ref):
        i = pl.program_id(0)
        act, dact = _silu_parts(z_ref[...])
        gv = g_ref[...]
        part = jnp.zeros((8, C_DK), F32)
        for h in range(C_V_HEADS):
            hs = slice(h * C_DK, (h + 1) * C_DK)
            ov = o_ref[:, hs]
            r = lax.rsqrt(jnp.mean(ov * ov, axis=1, keepdims=True) + EPS)
            xh = ov * r
            dyv = dy_ref[:, hs]
            dn = dyv * act[:, hs]
            dz_ref[:, hs] = dyv * xh * gv * dact[:, hs]
            dxh = dn * gv
            do_ref[:, hs] = r * (dxh - xh * jnp.mean(dxh * xh, axis=1, keepdims=True))
            part = part + jnp.sum((dn * xh).reshape(tq // 8, 8, C_DK), axis=0)

        @pl.when(i == 0)
        def _():
            acc_ref[...] = part

        @pl.when(i > 0)
        def _():
            acc_ref[...] += part

        @pl.when(i == nt - 1)
        def _():
            dg_ref[...] = jnp.sum(acc_ref[...], axis=0, keepdims=True)

    blk = pl.BlockSpec((tq, w), lambda i: (i, 0))
    grow = pl.BlockSpec((1, C_DK), lambda i: (0, 0))
    big = jax.ShapeDtypeStruct((t, w), F32)
    return pl.pallas_call(
        body, name="gated_norm_bwd", grid=(nt,),
        in_specs=[blk, blk, grow, blk], out_specs=[blk, blk, grow],
        out_shape=[big, big, jax.ShapeDtypeStruct((1, C_DK), F32)],
        scratch_shapes=[pltpu.VMEM((8, C_DK), F32)],
        compiler_params=_cparams(("arbitrary",)),
    )(o, z, gain, dy)


@jax.custom_vjp
def gated_norm(o, z, gain):
    return _gated_norm_fwd_call(o, z, gain)


def _gated_norm_fwd(o, z, gain):
    return _gated_norm_fwd_call(o, z, gain), (o, z, gain)


def _gated_norm_bwd(saved, dy):
    return tuple(_gated_norm_bwd_call(*saved, dy))


gated_norm.defvjp(_gated_norm_fwd, _gated_norm_bwd)


def _merge_fwd_call(ps, gs, *, tq=256):
    t, w = ps[0].shape

    def body(p0, p1, p2, g0, g1, g2, y_ref):
        y_ref[...] = (jax.nn.sigmoid(g0[...]) * p0[...] + jax.nn.sigmoid(g1[...]) * p1[...]
                      + jax.nn.sigmoid(g2[...]) * p2[...])

    blk = pl.BlockSpec((tq, w), lambda i: (i, 0))
    return pl.pallas_call(
        body, name="merge_fwd", grid=(t // tq,), in_specs=[blk] * 6, out_specs=blk,
        out_shape=jax.ShapeDtypeStruct((t, w), F32),
        compiler_params=_cparams(("parallel",)),
    )(*ps, *gs)


def _merge_bwd_call(ps, gs, dy, *, tq=256):
    t, w = dy.shape

    def body(p0, p1, p2, g0, g1, g2, dy_ref, dp0, dp1, dp2, dg0, dg1, dg2):
        dyv = dy_ref[...]
        for p, g, dp, dg in ((p0, g0, dp0, dg0), (p1, g1, dp1, dg1), (p2, g2, dp2, dg2)):
            sig = jax.nn.sigmoid(g[...])
            dp[...] = dyv * sig
            dg[...] = dyv * p[...] * sig * (1.0 - sig)

    blk = pl.BlockSpec((tq, w), lambda i: (i, 0))
    big = jax.ShapeDtypeStruct((t, w), F32)
    return pl.pallas_call(
        body, name="merge_bwd", grid=(t // tq,), in_specs=[blk] * 7, out_specs=[blk] * 6,
        out_shape=[big] * 6,
        compiler_params=_cparams(("parallel",)),
    )(*ps, *gs, dy)


@jax.custom_vjp
def gate_merge(ps, gs):
    return _merge_fwd_call(ps, gs)


def _gate_merge_fwd(ps, gs):
    return _merge_fwd_call(ps, gs), (ps, gs)


def _gate_merge_bwd(saved, dy):
    ps, gs = saved
    outs = _merge_bwd_call(ps, gs, dy)
    return tuple(outs[:3]), tuple(outs[3:])


gate_merge.defvjp(_gate_merge_fwd, _gate_merge_bwd)


_SPLITS = tuple((name, start, max(width, LANES)) for name, start, width, _ in IN_LAYOUT)


@jax.custom_vjp
def split_cols(u):
    return tuple(u[:, s:s + w] for _, s, w in _SPLITS)


def _split_cols_fwd(u):
    return split_cols(u), u.shape[0]


def _split_cols_bwd(t, cts):
    used = _SPLITS[-1][1] + _SPLITS[-1][2]
    return (jnp.concatenate(list(cts) + [jnp.zeros((t, D_IN_PAD - used), F32)], axis=1),)


split_cols.defvjp(_split_cols_fwd, _split_cols_bwd)


def _layer(x, tabs, p):
    h = rmsnorm(x, p["norm_mix"])
    u = linear_bias(h, p["w_in"], p["b_in"])
    a_q, a_k, a_v, b_q, c_qkv, c_z, gate_a, gate_b, gate_c, b_k, b_v, c_ab = split_cols(u)
    q, k = rope(a_q, tabs), rope(a_k, tabs)
    no_sink = jnp.zeros((8,), F32)
    outs = [f(q, k, a_v, no_sink) for f in ATTN_A]
    ya = combine(tuple(o for o, _ in outs), tuple(l for _, l in outs))
    yb, _ = ATTN_B(rope(b_q, tabs), rope(b_k, tabs), b_v, p["sinks"])
    qk, v = conv_prep(c_qkv, p["conv_w"])
    o = delta_rule(qk, v, c_ab, p["a_log"], p["dt_bias"])
    yc = gated_norm(o, c_z, p["c_norm"])
    merged = gate_merge((linear(ya, p["w_branch_a"]), linear(yb, p["w_branch_b"]), linear(yc, p["w_branch_c"])),
                        (gate_a, gate_b, gate_c))
    x = linear_res(merged, p["w_out"], x)
    return ffn(rmsnorm(x, p["norm_ffn"]), p["w_ff1"], p["w_ff2"], x)


def _local_loss(x, params, tabs, tgt):
    for layer in range(DEPTH):
        x = _layer(x, tabs, {n: w[layer] for n, w in params.items() if n != "norm_final"})
    return norm_loss(x, params["norm_final"], tgt)


def _in_cols_to_kernel(w):
    lead = w.shape[:-1]
    parts, pos = [], 0
    for _, start, width, ref_start in IN_LAYOUT:
        if start > pos:
            parts.append(jnp.zeros(lead + (start - pos,), w.dtype))
        parts.append(w[..., ref_start:ref_start + width])
        pos = start + width
    parts.append(jnp.zeros(lead + (D_IN_PAD - pos,), w.dtype))
    return jnp.concatenate(parts, axis=-1)


def _in_cols_to_reference(w):
    by_ref = sorted(IN_LAYOUT, key=lambda e: e[3])
    return jnp.concatenate([w[..., start:start + width] for _, start, width, _ in by_ref], axis=-1)


def _pad_lanes(v):
    return jnp.pad(v, ((0, 0), (0, LANES - v.shape[1])))[:, None, :]


BIG = (("w_in", 2), ("conv_w", 2), ("w_branch_a", 2), ("w_branch_b", 2), ("w_branch_c", 1), ("w_out", 1),
       ("w_ff1", 2), ("w_ff2", 1))
SMALL = ("norm_mix", "b_in", "a_log", "dt_bias", "sinks", "c_norm", "norm_ffn", "norm_final")
WEIGHTS = ("norm_mix", "w_in", "b_in", "conv_w", "a_log", "dt_bias", "sinks", "c_norm", "w_branch_a",
           "w_branch_b", "w_branch_c", "w_out", "norm_ffn", "w_ff1", "w_ff2", "norm_final")
PACK_ROWS = 1024


def _pack(arrays):
    flat = jnp.concatenate([a.reshape(-1) for a in arrays])
    n = flat.shape[0]
    rows = -(-n // (PACK_ROWS * LANES)) * PACK_ROWS
    return jnp.pad(flat, (0, rows * LANES - n)).reshape(rows, LANES)


def _pack_blocks(arrays):
    flat = jnp.concatenate([a.reshape(N_DEV, -1) for a in arrays], axis=1)
    n = flat.shape[1]
    rows = -(-n // (PACK_ROWS * LANES)) * PACK_ROWS
    return jnp.pad(flat, ((0, 0), (0, rows * LANES - n))).reshape(N_DEV, rows, LANES)


def _unpack(buf, shapes):
    lead = buf.shape[:-2]
    flat = buf.reshape(lead + (-1,))
    out, pos = [], 0
    for shp in shapes:
        n = math.prod(shp)
        out.append(flat[..., pos:pos + n].reshape(lead + tuple(shp)))
        pos += n
    return out


def _shards_to_full(blocks, axis):
    moved = jnp.moveaxis(blocks, 0, axis)
    shp = list(blocks.shape[1:])
    shp[axis] = shp[axis] * N_DEV
    return moved.reshape(shp)


def _full_to_shards(full, axis):
    shp = list(full.shape)
    shp[axis:axis + 1] = [N_DEV, shp[axis] // N_DEV]
    return jnp.moveaxis(full.reshape(shp), axis, 0)


MESH_ID = pl.DeviceIdType.MESH
HBM_SPEC = pl.BlockSpec(memory_space=pl.ANY)


def _my_place():
    return lax.axis_index("x"), lax.axis_index("y"), lax.axis_index("c")


def _slot(x, y, c):
    return 4 * x + 2 * y + c


def _all_gather(block, *, name):
    rows = block.shape[0]

    def body(x_ref, out_ref, send_sems, recv_sems, local_sem):
        x, y, c = _my_place()
        me, sibling = (x, y, c), (x, y, 1 - c)
        chips = [(1 - x, y), (x, 1 - y), (1 - x, 1 - y)]

        def copy(k, blk, to, src=None):
            dst = out_ref.at[_slot(*blk)]
            return pltpu.make_async_remote_copy(
                src_ref=dst if src is None else src, dst_ref=dst,
                send_sem=send_sems.at[k], recv_sem=recv_sems.at[k], device_id=to, device_id_type=MESH_ID)

        mine = pltpu.make_async_copy(x_ref, out_ref.at[_slot(*me)], local_sem)
        mine.start()
        first = [copy(0, me, sibling, src=x_ref)]
        first += [copy(1 + j, me, (*chip, c), src=x_ref) for j, chip in enumerate(chips)]
        for cp in first:
            cp.start()
        passed = [copy(4 + j, (*chip, c), sibling) for j, chip in enumerate(chips)]
        for j, chip in enumerate(chips):
            copy(1 + j, (*chip, c), me).wait_recv()
            passed[j].start()
        copy(0, sibling, me).wait_recv()
        for j, chip in enumerate(chips):
            copy(4 + j, (*chip, 1 - c), me).wait_recv()
        for cp in first + passed:
            cp.wait_send()
        mine.wait()

    return pl.pallas_call(
        body, name=name,
        out_shape=jax.ShapeDtypeStruct((N_DEV, rows, LANES), block.dtype),
        in_specs=[HBM_SPEC], out_specs=HBM_SPEC,
        scratch_shapes=[pltpu.SemaphoreType.DMA((7,)), pltpu.SemaphoreType.DMA((7,)), pltpu.SemaphoreType.DMA],
    )(block)


def _all_to_all(blocks, *, name):
    def body(g_ref, out_ref, send_sems, recv_sems, local_sem):
        x, y, c = _my_place()
        mine_slot = _slot(x, y, c)
        local = pltpu.make_async_copy(g_ref.at[mine_slot], out_ref.at[mine_slot], local_sem)
        local.start()
        copies = []
        for k in range(1, N_DEV):
            px, py, pc = x ^ (k >> 2), y ^ ((k >> 1) & 1), c ^ (k & 1)
            copies.append(pltpu.make_async_remote_copy(
                src_ref=g_ref.at[_slot(px, py, pc)], dst_ref=out_ref.at[mine_slot],
                send_sem=send_sems.at[k - 1], recv_sem=recv_sems.at[k - 1],
                device_id=(px, py, pc), device_id_type=MESH_ID))
        for cp in copies:
            cp.start()
        for cp in copies:
            cp.wait_recv()
        for cp in copies:
            cp.wait_send()
        local.wait()

    return pl.pallas_call(
        body, name=name,
        out_shape=jax.ShapeDtypeStruct(blocks.shape, blocks.dtype),
        in_specs=[HBM_SPEC], out_specs=HBM_SPEC,
        scratch_shapes=[pltpu.SemaphoreType.DMA((7,)), pltpu.SemaphoreType.DMA((7,)), pltpu.SemaphoreType.DMA],
    )(blocks)


def _adamw_call(parts, w, m, v, *, name):
    rows = w.shape[0]
    tr = min(PACK_ROWS, rows)
    assert rows % tr == 0

    def body(p_ref, w_ref, m_ref, v_ref, g_ref, d_ref, nm_ref, nv_ref):
        g = p_ref[0]
        for s in range(1, N_DEV):
            g = g + p_ref[s]
        nm = ADAM_B1 * m_ref[...] + (1.0 - ADAM_B1) * g
        nv = ADAM_B2 * v_ref[...] + (1.0 - ADAM_B2) * jnp.square(g)
        m_hat = nm / (1.0 - ADAM_B1 ** ADAM_STEP)
        v_hat = nv / (1.0 - ADAM_B2 ** ADAM_STEP)
        g_ref[...] = g
        nm_ref[...] = nm
        nv_ref[...] = nv
        d_ref[...] = -ADAM_LR * (m_hat / (jnp.sqrt(v_hat) + ADAM_EPS) + ADAM_WD * w_ref[...])

    blk = pl.BlockSpec((tr, LANES), lambda i: (i, 0))
    shape = jax.ShapeDtypeStruct((rows, LANES), F32)
    return pl.pallas_call(
        body, name=name, grid=(rows // tr,),
        in_specs=[pl.BlockSpec((N_DEV, tr, LANES), lambda i: (0, i, 0)), blk, blk, blk],
        out_specs=[blk] * 4, out_shape=[shape] * 4,
        compiler_params=_cparams(("parallel",)),
    )(parts, w, m, v)


def _kernel_params(full):
    return {
        "norm_mix": full["norm_mix"][:, None, :],
        "w_in": _in_cols_to_kernel(full["w_in"]),
        "b_in": _in_cols_to_kernel(full["b_in"])[:, None, :],
        "conv_w": full["conv_w"],
        "a_log": _pad_lanes(full["a_log"]),
        "dt_bias": _pad_lanes(full["dt_bias"]),
        "sinks": full["sinks"],
        "c_norm": full["c_norm"][:, None, :],
        "w_branch_a": full["w_branch_a"], "w_branch_b": full["w_branch_b"], "w_branch_c": full["w_branch_c"],
        "w_out": full["w_out"],
        "norm_ffn": full["norm_ffn"][:, None, :],
        "w_ff1": full["w_ff1"], "w_ff2": full["w_ff2"],
        "norm_final": full["norm_final"][None, :],
    }


def _reference_grads(g):
    return {
        "norm_mix": g["norm_mix"][:, 0, :],
        "w_in": _in_cols_to_reference(g["w_in"]),
        "b_in": _in_cols_to_reference(g["b_in"][:, 0, :]),
        "conv_w": g["conv_w"],
        "a_log": g["a_log"][:, 0, :C_V_HEADS],
        "dt_bias": g["dt_bias"][:, 0, :C_V_HEADS],
        "sinks": g["sinks"],
        "c_norm": g["c_norm"][:, 0, :],
        "w_branch_a": g["w_branch_a"], "w_branch_b": g["w_branch_b"], "w_branch_c": g["w_branch_c"],
        "w_out": g["w_out"],
        "norm_ffn": g["norm_ffn"][:, 0, :],
        "w_ff1": g["w_ff1"], "w_ff2": g["w_ff2"],
        "norm_final": g["norm_final"][0],
    }


def kernel(x, positions, norm_mix, w_in, b_in, conv_w, a_log, dt_bias, sinks, c_norm, w_branch_a, w_branch_b, w_branch_c, w_out, norm_ffn, w_ff1, w_ff2, norm_final, loss_target, m_norm_mix, m_w_in, m_b_in, m_conv_w, m_a_log, m_dt_bias, m_sinks, m_c_norm, m_w_branch_a, m_w_branch_b, m_w_branch_c, m_w_out, m_norm_ffn, m_w_ff1, m_w_ff2, m_norm_final, v_norm_mix, v_w_in, v_b_in, v_conv_w, v_a_log, v_dt_bias, v_sinks, v_c_norm, v_w_branch_a, v_w_branch_b, v_w_branch_c, v_w_out, v_norm_ffn, v_w_ff1, v_w_ff2, v_norm_final):
    env = dict(locals())
    weights = {n: env[n] for n in WEIGHTS}
    moments_m = {n: env["m_" + n] for n in WEIGHTS}
    moments_v = {n: env["v_" + n] for n in WEIGHTS}

    big_shapes = [weights[n].shape for n, _ in BIG]
    gathered = _all_gather(_pack([weights[n].astype(BF16) for n, _ in BIG]), name="gather_weights")
    blocks = _unpack(gathered, big_shapes)
    full = {n: weights[n] for n in SMALL}
    for (n, axis), blk in zip(BIG, blocks):
        full[n] = _shards_to_full(blk, axis).astype(F32)

    tabs = rope_tables(positions[0])
    loss, (dx, dparams) = jax.value_and_grad(_local_loss, argnums=(0, 1))(
        x[0], _kernel_params(full), tabs, loss_target[0])
    grads = _reference_grads(dparams)
    loss = lax.psum(loss, ("x", "y", "c"))

    big_parts = _all_to_all(_pack_blocks([_full_to_shards(grads[n], axis) for n, axis in BIG]),
                            name="scatter_grads")
    small_parts = _all_gather(_pack([grads[n] for n in SMALL]), name="gather_small_grads")

    out = {}
    for names, parts in (([n for n, _ in BIG], big_parts), (list(SMALL), small_parts)):
        shapes = [weights[n].shape for n in names]
        packed = [_pack([d[n] for n in names]) for d in (weights, moments_m, moments_v)]
        results = _adamw_call(parts, *packed, name="adamw_" + names[0])
        for kind, buf in zip(("grad", "delta", "new_m", "new_v"), results):
            for n, arr in zip(names, _unpack(buf, shapes)):
                out[kind, n] = arr
    return (loss, dx[None], *[out[kind, n] for kind in ("grad", "delta", "new_m", "new_v") for n in WEIGHTS])
```

```python
import functools
import math

import jax
import jax.numpy as jnp
from jax import lax
from jax.experimental import pallas as pl
from jax.experimental.pallas import tpu as pltpu

F32 = jnp.float32
BF16 = jnp.bfloat16
HIGHEST = lax.Precision.HIGHEST

N_DEV = 8
D_MODEL = 1024
DEPTH = 2
HEAD_DIM = 64
ROT_DIM = 16
ROPE_THETA = 500000.0
BLK = 128
NEG_INF = -1e30
EPS = 1e-6
A_CONFIGS = ((128, 1), (512, 4), (2048, 16))
B_GROUP = 4
C_QK_HEADS = 4
C_V_HEADS = 8
C_DK = 128
C_CONV = 4
CHUNK = 64
D_FF = 4096
D_IN = 8464
ADAM_LR = 0.001
ADAM_B1 = 0.9
ADAM_B2 = 0.999
ADAM_EPS = 1e-08
ADAM_WD = 0.01
ADAM_STEP = 10

IN_LAYOUT = (
    ("a_q", 0, 512, 0), ("a_k", 512, 512, 512), ("a_v", 1024, 512, 1024), ("b_q", 1536, 512, 1536),
    ("c_qkv", 2048, 2048, 2304), ("c_z", 4096, 1024, 4352),
    ("gate_a", 5120, 1024, 5392), ("gate_b", 6144, 1024, 6416), ("gate_c", 7168, 1024, 7440),
    ("b_k", 8192, 128, 2048), ("b_v", 8320, 128, 2176), ("c_ab", 8448, 16, 5376),
)
D_IN_PAD = 8704
LANES = 128
VMEM_LIMIT = 56 * 1024 * 1024


def _cparams(sem=None):
    return pltpu.CompilerParams(dimension_semantics=sem, vmem_limit_bytes=VMEM_LIMIT)


def _relu2(t):
    return jnp.square(jnp.maximum(t, 0.0))


def _mm(a, b, *, ta=False, tb=False, bias=None, a_fn=None, mul_drelu2=None, add=None,
        out_dtype=F32, relu2_out=False, tm=1024, tn=1024, tk=2048, name):
    if ta:
        kdim, m = a.shape
    else:
        m, kdim = a.shape
    n = b.shape[0] if tb else b.shape[1]
    tm, tn, tk = min(tm, m), min(tn, n), min(tk, kdim)
    assert m % tm == 0 and n % tn == 0 and kdim % tk == 0, (a.shape, b.shape, tm, tn, tk)
    nk = kdim // tk
    dims = (((0 if ta else 1,), (1 if tb else 0,)), ((), ()))
    extras = [e for e in (bias, mul_drelu2, add) if e is not None]

    def body(*refs):
        a_ref, b_ref = refs[0], refs[1]
        pos = 2
        bias_ref = pre_ref = add_ref = None
        if bias is not None:
            bias_ref = refs[pos]; pos += 1
        if mul_drelu2 is not None:
            pre_ref = refs[pos]; pos += 1
        if add is not None:
            add_ref = refs[pos]; pos += 1
        o_ref = refs[pos]
        pos += 1
        r_ref = None
        if relu2_out:
            r_ref = refs[pos]; pos += 1
        acc_ref = refs[pos] if nk > 1 else None

        av = a_ref[...]
        if a_fn is not None:
            av = a_fn(av)
        part = lax.dot_general(av.astype(BF16), b_ref[...].astype(BF16), dims,
                               preferred_element_type=F32)

        def finish(acc):
            if bias_ref is not None:
                acc = acc + bias_ref[...]
            if pre_ref is not None:
                acc = acc * (2.0 * jnp.maximum(pre_ref[...], 0.0))
            if add_ref is not None:
                acc = acc + add_ref[...]
            o_ref[...] = acc.astype(out_dtype)
            if r_ref is not None:
                r_ref[...] = _relu2(acc).astype(BF16)

        if nk == 1:
            finish(part)
        else:
            k = pl.program_id(2)

            @pl.when(k == 0)
            def _():
                acc_ref[...] = part

            @pl.when(k > 0)
            def _():
                acc_ref[...] += part

            @pl.when(k == nk - 1)
            def _():
                finish(acc_ref[...])

    a_spec = (pl.BlockSpec((tk, tm), lambda i, j, k: (k, i)) if ta
              else pl.BlockSpec((tm, tk), lambda i, j, k: (i, k)))
    b_spec = (pl.BlockSpec((tn, tk), lambda i, j, k: (j, k)) if tb
              else pl.BlockSpec((tk, tn), lambda i, j, k: (k, j)))
    in_specs = [a_spec, b_spec]
    if bias is not None:
        in_specs.append(pl.BlockSpec((1, tn), lambda i, j, k: (0, j)))
    for _ in extras[(1 if bias is not None else 0):]:
        in_specs.append(pl.BlockSpec((tm, tn), lambda i, j, k: (i, j)))
    o_spec = pl.BlockSpec((tm, tn), lambda i, j, k: (i, j))
    o_shape = jax.ShapeDtypeStruct((m, n), out_dtype)
    return pl.pallas_call(
        body, name=name,
        grid=(m // tm, n // tn, nk),
        in_specs=in_specs,
        out_specs=[o_spec, o_spec] if relu2_out else o_spec,
        out_shape=[o_shape, jax.ShapeDtypeStruct((m, n), BF16)] if relu2_out else o_shape,
        scratch_shapes=[pltpu.VMEM((tm, tn), F32)] if nk > 1 else [],
        compiler_params=_cparams(("parallel", "parallel", "arbitrary")),
    )(a, b, *extras)


def _colsum(x, *, name, tk=512, tn=1024):
    t, n = x.shape
    tn = min(tn, n)
    assert t % tk == 0 and n % tn == 0

    def body(x_ref, o_ref, acc_ref):
        k = pl.program_id(1)
        part = jnp.sum(x_ref[...].astype(F32).reshape(tk // 8, 8, tn), axis=0)

        @pl.when(k == 0)
        def _():
            acc_ref[...] = part

        @pl.when(k > 0)
        def _():
            acc_ref[...] += part

        @pl.when(k == t // tk - 1)
        def _():
            o_ref[...] = jnp.sum(acc_ref[...], axis=0, keepdims=True)

    return pl.pallas_call(
        body, name=name, grid=(n // tn, t // tk),
        in_specs=[pl.BlockSpec((tk, tn), lambda j, k: (k, j))],
        out_specs=pl.BlockSpec((1, tn), lambda j, k: (0, j)),
        out_shape=jax.ShapeDtypeStruct((1, n), F32),
        scratch_shapes=[pltpu.VMEM((8, tn), F32)],
        compiler_params=_cparams(("parallel", "arbitrary")),
    )(x)


@jax.custom_vjp
def linear(a, w, wb):
    return _mm(a, wb, name="linear_fwd")


def _linear_fwd(a, w, wb):
    return _mm(a, wb, name="linear_fwd"), (a, wb)


def _linear_bwd(res, dy):
    a, wb = res
    da = _mm(dy, wb, tb=True, name="linear_da")
    dw = _mm(a, dy, ta=True, tk=1024, name="linear_dw")
    return da, dw, jnp.zeros_like(wb)


linear.defvjp(_linear_fwd, _linear_bwd)


@jax.custom_vjp
def linear_res(a, w, wb, res):
    return _mm(a, wb, add=res, name="linear_res_fwd")


def _linear_res_fwd(a, w, wb, res):
    return _mm(a, wb, add=res, name="linear_res_fwd"), (a, wb)


def _linear_res_bwd(saved, dy):
    a, wb = saved
    da = _mm(dy, wb, tb=True, name="linear_res_da")
    dw = _mm(a, dy, ta=True, tk=1024, name="linear_res_dw")
    return da, dw, jnp.zeros_like(wb), dy


linear_res.defvjp(_linear_res_fwd, _linear_res_bwd)


IN_TN = D_IN_PAD // 4


def _norm_in_proj_run(x, g, wb, b):
    h = _rms_fwd_call(x, g, name="rms_mix_fwd", out_dtype=BF16)
    return _mm(h, wb, bias=b, tn=IN_TN, name="in_proj_fwd"), h


@jax.custom_vjp
def norm_in_proj(x, g, w, wb, b):
    return _norm_in_proj_run(x, g, wb, b)[0]


def _norm_in_proj_fwd(x, g, w, wb, b):
    u, h = _norm_in_proj_run(x, g, wb, b)
    return u, (x, g, wb, h)


def _norm_in_proj_bwd(saved, du):
    x, g, wb, h = saved
    dub = du.astype(BF16)
    dh = _mm(dub, wb, tb=True, tk=IN_TN, name="in_proj_da")
    dw = _mm(h, dub, ta=True, tn=512, tk=1024, name="in_proj_dw")
    db = _colsum(dub, name="in_proj_db", tn=IN_TN)
    dx, dg = _rms_bwd_call(x, g, dh, name="rms_mix_bwd")
    return dx, dg, dw, jnp.zeros_like(wb), db


norm_in_proj.defvjp(_norm_in_proj_fwd, _norm_in_proj_bwd)


def _norm_ffn_run(x, g, w1b, w2b):
    h = _rms_fwd_call(x, g, name="rms_ffn_fwd", out_dtype=BF16)
    pre, act = _mm(h, w1b, relu2_out=True, name="ffn_up")
    return _mm(act, w2b, add=x, name="ffn_down"), h, pre, act


@jax.custom_vjp
def norm_ffn(x, g, w1, w2, w1b, w2b):
    return _norm_ffn_run(x, g, w1b, w2b)[0]


def _norm_ffn_fwd(x, g, w1, w2, w1b, w2b):
    y, h, pre, act = _norm_ffn_run(x, g, w1b, w2b)
    return y, (x, g, w1b, w2b, h, pre, act)


def _norm_ffn_bwd(saved, dy):
    x, g, w1b, w2b, h, pre, act = saved
    dpre = _mm(dy, w2b, tb=True, mul_drelu2=pre, out_dtype=BF16, name="ffn_dpre")
    dw2 = _mm(act, dy, ta=True, tk=1024, name="ffn_dw2")
    dw1 = _mm(h, dpre, ta=True, tk=1024, name="ffn_dw1")
    dh = _mm(dpre, w1b, tb=True, name="ffn_dh")
    dx, dg = _rms_bwd_call(x, g, dh, add=dy, name="rms_ffn_bwd")
    return dx, dg, dw1, dw2, jnp.zeros_like(w1b), jnp.zeros_like(w2b)


norm_ffn.defvjp(_norm_ffn_fwd, _norm_ffn_bwd)


def _rms_fwd_call(x, g, *, name, out_dtype=F32, tq=512):
    t, d = x.shape

    def body(x_ref, g_ref, y_ref):
        xv = x_ref[...]
        r = lax.rsqrt(jnp.mean(xv * xv, axis=-1, keepdims=True) + EPS)
        y_ref[...] = (xv * r * g_ref[...]).astype(out_dtype)

    return pl.pallas_call(
        body, name=name, grid=(t // tq,),
        in_specs=[pl.BlockSpec((tq, d), lambda i: (i, 0)), pl.BlockSpec((1, d), lambda i: (0, 0))],
        out_specs=pl.BlockSpec((tq, d), lambda i: (i, 0)),
        out_shape=jax.ShapeDtypeStruct((t, d), out_dtype),
        compiler_params=_cparams(("parallel",)),
    )(x, g)


def _rms_bwd_call(x, g, dy, *, name, add=None, tq=512):
    t, d = x.shape
    nt = t // tq

    def body(*refs):
        if add is None:
            x_ref, g_ref, dy_ref, dx_ref, dg_ref, acc_ref = refs
        else:
            x_ref, g_ref, dy_ref, add_ref, dx_ref, dg_ref, acc_ref = refs
        i = pl.program_id(0)
        xv = x_ref[...]
        r = lax.rsqrt(jnp.mean(xv * xv, axis=-1, keepdims=True) + EPS)
        xh = xv * r
        dyv = dy_ref[...]
        dxh = dyv * g_ref[...]
        dx = r * (dxh - xh * jnp.mean(dxh * xh, axis=-1, keepdims=True))
        dx_ref[...] = dx if add is None else dx + add_ref[...]
        part = jnp.sum((dyv * xh).reshape(tq // 8, 8, d), axis=0)

        @pl.when(i == 0)
        def _():
            acc_ref[...] = part

        @pl.when(i > 0)
        def _():
            acc_ref[...] += part

        @pl.when(i == nt - 1)
        def _():
            dg_ref[...] = jnp.sum(acc_ref[...], axis=0, keepdims=True)

    blk = pl.BlockSpec((tq, d), lambda i: (i, 0))
    row = pl.BlockSpec((1, d), lambda i: (0, 0))
    extra = [] if add is None else [add]
    return pl.pallas_call(
        body, name=name, grid=(nt,),
        in_specs=[blk, row, blk] + [blk] * len(extra),
        out_specs=[blk, row],
        out_shape=[jax.ShapeDtypeStruct((t, d), F32), jax.ShapeDtypeStruct((1, d), F32)],
        scratch_shapes=[pltpu.VMEM((8, d), F32)],
        compiler_params=_cparams(("arbitrary",)),
    )(x, g, dy, *extra)


def _loss_fwd_call(x, g, tgt, *, tq=512):
    t, d = x.shape
    nt = t // tq

    def body(x_ref, g_ref, t_ref, o_ref, acc_ref):
        i = pl.program_id(0)
        xv = x_ref[...]
        r = lax.rsqrt(jnp.mean(xv * xv, axis=-1, keepdims=True) + EPS)
        err = xv * r * g_ref[...] - t_ref[...]
        part = jnp.sum((err * err).reshape(tq // 8, 8, d), axis=0)

        @pl.when(i == 0)
        def _():
            acc_ref[...] = part

        @pl.when(i > 0)
        def _():
            acc_ref[...] += part

        @pl.when(i == nt - 1)
        def _():
            tot = jnp.sum(jnp.sum(acc_ref[...], axis=0, keepdims=True), axis=1, keepdims=True)
            o_ref[...] = jnp.broadcast_to(tot * (0.5 / d), (8, LANES))

    out = pl.pallas_call(
        body, name="loss_fwd", grid=(nt,),
        in_specs=[pl.BlockSpec((tq, d), lambda i: (i, 0)), pl.BlockSpec((1, d), lambda i: (0, 0)),
                  pl.BlockSpec((tq, d), lambda i: (i, 0))],
        out_specs=pl.BlockSpec((8, LANES), lambda i: (0, 0)),
        out_shape=jax.ShapeDtypeStruct((8, LANES), F32),
        scratch_shapes=[pltpu.VMEM((8, d), F32)],
        compiler_params=_cparams(("arbitrary",)),
    )(x, g, tgt)
    return out[0, 0]


def _loss_bwd_call(x, g, tgt, scale, *, tq=512):
    t, d = x.shape
    nt = t // tq

    def body(x_ref, g_ref, t_ref, s_ref, dx_ref, dg_ref, acc_ref):
        i = pl.program_id(0)
        xv = x_ref[...]
        r = lax.rsqrt(jnp.mean(xv * xv, axis=-1, keepdims=True) + EPS)
        xh = xv * r
        gv = g_ref[...]
        dyv = (xh * gv - t_ref[...]) * (s_ref[...] * (1.0 / d))
        dxh = dyv * gv
        dx_ref[...] = r * (dxh - xh * jnp.mean(dxh * xh, axis=-1, keepdims=True))
        part = jnp.sum((dyv * xh).reshape(tq // 8, 8, d), axis=0)

        @pl.when(i == 0)
        def _():
            acc_ref[...] = part

        @pl.when(i > 0)
        def _():
            acc_ref[...] += part

        @pl.when(i == nt - 1)
        def _():
            dg_ref[...] = jnp.sum(acc_ref[...], axis=0, keepdims=True)

    return pl.pallas_call(
        body, name="loss_bwd", grid=(nt,),
        in_specs=[pl.BlockSpec((tq, d), lambda i: (i, 0)), pl.BlockSpec((1, d), lambda i: (0, 0)),
                  pl.BlockSpec((tq, d), lambda i: (i, 0)), pl.BlockSpec((1, 1), lambda i: (0, 0))],
        out_specs=[pl.BlockSpec((tq, d), lambda i: (i, 0)), pl.BlockSpec((1, d), lambda i: (0, 0))],
        out_shape=[jax.ShapeDtypeStruct((t, d), F32), jax.ShapeDtypeStruct((1, d), F32)],
        scratch_shapes=[pltpu.VMEM((8, d), F32)],
        compiler_params=_cparams(("arbitrary",)),
    )(x, g, tgt, scale)


@jax.custom_vjp
def norm_loss(x, g, tgt):
    return _loss_fwd_call(x, g, tgt)


def _norm_loss_fwd(x, g, tgt):
    return _loss_fwd_call(x, g, tgt), (x, g, tgt)


def _norm_loss_bwd(saved, dl):
    x, g, tgt = saved
    dx, dg = _loss_bwd_call(x, g, tgt, jnp.reshape(dl, (1, 1)).astype(F32))
    return dx, dg, jnp.zeros_like(tgt)


norm_loss.defvjp(_norm_loss_fwd, _norm_loss_bwd)


def rope_tables(positions):
    inv_freq = jnp.power(ROPE_THETA, -jnp.arange(0, ROT_DIM, 2, dtype=F32) / ROT_DIM)
    ang = positions.astype(F32)[:, None] * inv_freq
    cos, sin = jnp.cos(ang), jnp.sin(ang)
    t = positions.shape[0]
    one = jnp.ones((t, HEAD_DIM - ROT_DIM), F32)
    zero8 = jnp.zeros((t, ROT_DIM // 2), F32)
    zero = jnp.zeros((t, HEAD_DIM - ROT_DIM), F32)
    a = jnp.concatenate([cos, cos, one], axis=1)
    b = jnp.concatenate([zero8, sin, zero], axis=1)
    c = jnp.concatenate([-sin, zero8, zero], axis=1)
    return tuple(jnp.concatenate([m, m], axis=1) for m in (a, b, c))


def _rope_call(x, tabs, *, transpose, name, tq=512):
    t, w = x.shape
    half = ROT_DIM // 2

    def body(x_ref, a_ref, b_ref, c_ref, o_ref):
        a, b, c = a_ref[...], b_ref[...], c_ref[...]
        for j in range(w // LANES):
            xs = x_ref[:, j * LANES:(j + 1) * LANES]
            if transpose:
                out = (xs * a + pltpu.roll(xs * b, LANES - half, 1) + pltpu.roll(xs * c, half, 1))
            else:
                out = (xs * a + pltpu.roll(xs, half, 1) * b + pltpu.roll(xs, LANES - half, 1) * c)
            o_ref[:, j * LANES:(j + 1) * LANES] = out

    tab_spec = pl.BlockSpec((tq, LANES), lambda i: (i, 0))
    return pl.pallas_call(
        body, name=name, grid=(t // tq,),
        in_specs=[pl.BlockSpec((tq, w), lambda i: (i, 0)), tab_spec, tab_spec, tab_spec],
        out_specs=pl.BlockSpec((tq, w), lambda i: (i, 0)),
        out_shape=jax.ShapeDtypeStruct((t, w), F32),
        compiler_params=_cparams(("parallel",)),
    )(x, *tabs)


@jax.custom_vjp
def rope(x, tabs):
    return _rope_call(x, tabs, transpose=False, name="rope_fwd")


def _rope_fwd(x, tabs):
    return _rope_call(x, tabs, transpose=False, name="rope_fwd"), tabs


def _rope_bwd(tabs, dy):
    return _rope_call(dy, tabs, transpose=True, name="rope_bwd"), tuple(jnp.zeros_like(m) for m in tabs)


rope.defvjp(_rope_fwd, _rope_bwd)


def _band_masks(first_block, max_dist):
    qi = lax.broadcasted_iota(jnp.int32, (BLK, BLK), 0)
    kj = lax.broadcasted_iota(jnp.int32, (BLK, BLK), 1)
    valid_prev = jnp.logical_and(kj >= qi + (BLK - max_dist), jnp.logical_not(first_block))
    valid_cur = kj <= qi
    return valid_prev, valid_cur


def _attn_specs(dil, qw, kw):
    q_spec = pl.BlockSpec((BLK, qw), lambda r, i: (i, r))
    kp_spec = pl.BlockSpec((BLK, kw), lambda r, i: (jnp.maximum(i - 1, 0), r))
    kc_spec = pl.BlockSpec((BLK, kw), lambda r, i: (i, r))
    lse_spec = pl.BlockSpec((BLK, LANES), lambda r, i: (i, r))
    return q_spec, kp_spec, kc_spec, lse_spec


_NT = (((1,), (1,)), ((), ()))
_TN = (((0,), (0,)), ((), ()))


def _attn_fwd_call(q, k, v, sink, *, dil, group, max_dist, name):
    t, qw = q.shape
    kw = k.shape[1]
    nh = qw // HEAD_DIM
    l = t // dil
    nb = l // BLK
    scale = HEAD_DIM ** -0.5
    use_sink = sink is not None

    def body(*refs):
        if use_sink:
            sink_ref, refs = refs[0], refs[1:]
        q_ref, kp_ref, kc_ref, vp_ref, vc_ref, o_ref, lse_ref = refs
        valid_prev, valid_cur = _band_masks(pl.program_id(1) == 0, max_dist)
        lane = lax.broadcasted_iota(jnp.int32, (BLK, LANES), 1)
        lse_tile = jnp.zeros((BLK, LANES), F32)
        for h in range(nh):
            kh = h // group
            ks = slice(kh * HEAD_DIM, (kh + 1) * HEAD_DIM)
            qh = q_ref[:, h * HEAD_DIM:(h + 1) * HEAD_DIM].astype(BF16)
            kp, kc = kp_ref[:, ks].astype(BF16), kc_ref[:, ks].astype(BF16)
            vp, vc = vp_ref[:, ks].astype(BF16), vc_ref[:, ks].astype(BF16)
            sp = lax.dot_general(qh, kp, _NT, preferred_element_type=F32) * scale
            sc = lax.dot_general(qh, kc, _NT, preferred_element_type=F32) * scale
            sp = jnp.where(valid_prev, sp, NEG_INF)
            sc = jnp.where(valid_cur, sc, NEG_INF)
            m = jnp.maximum(jnp.max(sp, axis=1, keepdims=True), jnp.max(sc, axis=1, keepdims=True))
            if use_sink:
                m = jnp.maximum(m, sink_ref[h])
            pp, pc = jnp.exp(sp - m), jnp.exp(sc - m)
            den = jnp.sum(pp, axis=1, keepdims=True) + jnp.sum(pc, axis=1, keepdims=True)
            if use_sink:
                den = den + jnp.exp(sink_ref[h] - m)
            o = (jnp.dot(pp.astype(BF16), vp, preferred_element_type=F32)
                 + jnp.dot(pc.astype(BF16), vc, preferred_element_type=F32))
            o_ref[:, h * HEAD_DIM:(h + 1) * HEAD_DIM] = o / den
            lse_tile = jnp.where(lane == h, m + jnp.log(den), lse_tile)
        lse_ref[...] = lse_tile

    q_spec, kp_spec, kc_spec, lse_spec = _attn_specs(dil, qw, kw)
    in_specs = [q_spec, kp_spec, kc_spec, kp_spec, kc_spec]
    args = [q.reshape(l, dil * qw), k.reshape(l, dil * kw), k.reshape(l, dil * kw),
            v.reshape(l, dil * kw), v.reshape(l, dil * kw)]
    if use_sink:
        in_specs = [pl.BlockSpec(memory_space=pltpu.SMEM)] + in_specs
        args = [sink] + args
    o, lse = pl.pallas_call(
        body, name=name, grid=(dil, nb),
        in_specs=in_specs,
        out_specs=[q_spec, lse_spec],
        out_shape=[jax.ShapeDtypeStruct((l, dil * qw), F32), jax.ShapeDtypeStruct((l, dil * LANES), F32)],
        compiler_params=_cparams(("parallel", "parallel")),
    )(*args)
    return o.reshape(t, qw), lse.reshape(t, LANES)


def _attn_bwd_call(q, k, v, sink, o, lse, do, dlse, *, dil, group, max_dist, name):
    t, qw = q.shape
    kw = k.shape[1]
    nh = qw // HEAD_DIM
    l = t // dil
    nb = l // BLK
    scale = HEAD_DIM ** -0.5
    use_sink = sink is not None

    def body(*refs):
        if use_sink:
            sink_ref, refs = refs[0], refs[1:]
        (q_ref, kp_ref, kc_ref, vp_ref, vc_ref, o_ref, lse_ref, do_ref, dlse_ref,
         dq_ref, dkc_ref, dkp_ref, dvc_ref, dvp_ref, dsink_ref) = refs
        first = jnp.logical_and(pl.program_id(0) == 0, pl.program_id(1) == 0)
        valid_prev, valid_cur = _band_masks(pl.program_id(1) == 0, max_dist)
        row = lax.broadcasted_iota(jnp.int32, (8, LANES), 0)
        lanes8 = lax.broadcasted_iota(jnp.int32, (8, LANES), 1)
        ds_tile = jnp.zeros((8, LANES), F32)
        acc = {}
        for h in range(nh):
            kh = h // group
            ks = slice(kh * HEAD_DIM, (kh + 1) * HEAD_DIM)
            hs = slice(h * HEAD_DIM, (h + 1) * HEAD_DIM)
            qh = q_ref[:, hs].astype(BF16)
            kp, kc = kp_ref[:, ks].astype(BF16), kc_ref[:, ks].astype(BF16)
            vp, vc = vp_ref[:, ks].astype(BF16), vc_ref[:, ks].astype(BF16)
            doh = do_ref[:, hs]
            dob = doh.astype(BF16)
            lse_h = lse_ref[:, h:h + 1]
            sp = lax.dot_general(qh, kp, _NT, preferred_element_type=F32) * scale
            sc = lax.dot_general(qh, kc, _NT, preferred_element_type=F32) * scale
            pp = jnp.where(valid_prev, jnp.exp(jnp.where(valid_prev, sp, NEG_INF) - lse_h), 0.0)
            pc = jnp.where(valid_cur, jnp.exp(jnp.where(valid_cur, sc, NEG_INF) - lse_h), 0.0)
            delta = jnp.sum(doh * o_ref[:, hs], axis=1, keepdims=True)
            corr = dlse_ref[:, h:h + 1] - delta
            dsp = pp * (lax.dot_general(dob, vp, _NT, preferred_element_type=F32) + corr)
            dsc = pc * (lax.dot_general(dob, vc, _NT, preferred_element_type=F32) + corr)
            dspb, dscb = dsp.astype(BF16), dsc.astype(BF16)
            dq_ref[:, hs] = (jnp.dot(dspb, kp, preferred_element_type=F32)
                             + jnp.dot(dscb, kc, preferred_element_type=F32)) * scale
            parts = (lax.dot_general(dscb, qh, _TN, preferred_element_type=F32) * scale,
                     lax.dot_general(dspb, qh, _TN, preferred_element_type=F32) * scale,
                     lax.dot_general(pc.astype(BF16), dob, _TN, preferred_element_type=F32),
                     lax.dot_general(pp.astype(BF16), dob, _TN, preferred_element_type=F32))
            if kh in acc:
                acc[kh] = tuple(x + y for x, y in zip(acc[kh], parts))
            else:
                acc[kh] = parts
            if h % group == group - 1:
                for ref, val in zip((dkc_ref, dkp_ref, dvc_ref, dvp_ref), acc[kh]):
                    ref[:, ks] = val
            if use_sink:
                ps = jnp.exp(sink_ref[h] - lse_h)
                val = -jnp.sum(ps * delta, axis=0, keepdims=True)
                ds_tile = jnp.where(jnp.logical_and(row == 0, lanes8 == h), val, ds_tile)

        @pl.when(first)
        def _():
            dsink_ref[...] = ds_tile

        @pl.when(jnp.logical_not(first))
        def _():
            dsink_ref[...] += ds_tile

    q_spec, kp_spec, kc_spec, lse_spec = _attn_specs(dil, qw, kw)
    in_specs = [q_spec, kp_spec, kc_spec, kp_spec, kc_spec, q_spec, lse_spec, q_spec, lse_spec]
    k2, v2 = k.reshape(l, dil * kw), v.reshape(l, dil * kw)
    args = [q.reshape(l, dil * qw), k2, k2, v2, v2, o.reshape(l, dil * qw), lse.reshape(l, dil * LANES),
            do.reshape(l, dil * qw), dlse.reshape(l, dil * LANES)]
    if use_sink:
        in_specs = [pl.BlockSpec(memory_space=pltpu.SMEM)] + in_specs
        args = [sink] + args
    kv_shape = jax.ShapeDtypeStruct((l, dil * kw), F32)
    outs = pl.pallas_call(
        body, name=name, grid=(dil, nb),
        in_specs=in_specs,
        out_specs=[q_spec, kc_spec, kc_spec, kc_spec, kc_spec, pl.BlockSpec((8, LANES), lambda r, i: (0, 0))],
        out_shape=[jax.ShapeDtypeStruct((l, dil * qw), F32), kv_shape, kv_shape, kv_shape, kv_shape,
                   jax.ShapeDtypeStruct((8, LANES), F32)],
        compiler_params=_cparams(("arbitrary", "arbitrary")),
    )(*args)
    return outs


def _shift_add_call(cur, prev, *, dil, kw, name):
    l = cur.shape[0]
    nb = l // BLK

    def body(c_ref, p_ref, o_ref):
        last = pl.program_id(1) == nb - 1
        o_ref[...] = c_ref[...] + jnp.where(last, 0.0, p_ref[...])

    spec = pl.BlockSpec((BLK, kw), lambda r, i: (i, r))
    nxt = pl.BlockSpec((BLK, kw), lambda r, i: (jnp.minimum(i + 1, nb - 1), r))
    return pl.pallas_call(
        body, name=name, grid=(dil, nb), in_specs=[spec, nxt], out_specs=spec,
        out_shape=jax.ShapeDtypeStruct(cur.shape, F32),
        compiler_params=_cparams(("parallel", "parallel")),
    )(cur, prev)


def _make_attention(dil, group, max_dist, use_sink, tag):
    kwargs = dict(dil=dil, group=group, max_dist=max_dist)

    @jax.custom_vjp
    def attn(q, k, v, sink):
        return _attn_fwd_call(q, k, v, sink if use_sink else None, name=tag + "_fwd", **kwargs)

    def fwd(q, k, v, sink):
        o, lse = _attn_fwd_call(q, k, v, sink if use_sink else None, name=tag + "_fwd", **kwargs)
        return (o, lse), (q, k, v, sink, o, lse)

    def bwd(saved, cts):
        q, k, v, sink, o, lse = saved
        do, dlse = cts
        t, kw = k.shape
        dq, dkc, dkp, dvc, dvp, dsink = _attn_bwd_call(
            q, k, v, sink if use_sink else None, o, lse, do, dlse, name=tag + "_bwd", **kwargs)
        dk = _shift_add_call(dkc, dkp, dil=dil, kw=kw, name=tag + "_dk").reshape(t, kw)
        dv = _shift_add_call(dvc, dvp, dil=dil, kw=kw, name=tag + "_dv").reshape(t, kw)
        return dq.reshape(q.shape), dk, dv, dsink[0, :sink.shape[0]]

    attn.defvjp(fwd, bwd)
    return attn


ATTN_A = tuple(_make_attention(dil, 1, window // dil, False, "attn_a%d" % dil) for window, dil in A_CONFIGS)
ATTN_B = _make_attention(1, B_GROUP, BLK - 1, True, "attn_b")


def _head_expand():
    r = lax.broadcasted_iota(jnp.int32, (LANES, 8 * HEAD_DIM), 0)
    c = lax.broadcasted_iota(jnp.int32, (LANES, 8 * HEAD_DIM), 1)
    return (c // HEAD_DIM == r).astype(F32)


def _combine_weights(l0, l1, l2):
    m = jnp.maximum(jnp.maximum(l0, l1), l2)
    e0, e1, e2 = jnp.exp(l0 - m), jnp.exp(l1 - m), jnp.exp(l2 - m)
    inv = 1.0 / (e0 + e1 + e2)
    return e0 * inv, e1 * inv, e2 * inv


def _combine_fwd_call(os_, lses, *, tq=256):
    t, w = os_[0].shape

    def body(o0, o1, o2, l0, l1, l2, y_ref):
        ws = _combine_weights(l0[...], l1[...], l2[...])
        e = _head_expand()
        y = jnp.zeros((tq, w), F32)
        for o_ref, wt in zip((o0, o1, o2), ws):
            y = y + _dot_mask(e, wt, mask_left=False) * o_ref[...]
        y_ref[...] = y

    o_spec = pl.BlockSpec((tq, w), lambda i: (i, 0))
    l_spec = pl.BlockSpec((tq, LANES), lambda i: (i, 0))
    return pl.pallas_call(
        body, name="combine_fwd", grid=(t // tq,),
        in_specs=[o_spec] * 3 + [l_spec] * 3, out_specs=o_spec,
        out_shape=jax.ShapeDtypeStruct((t, w), F32),
        compiler_params=_cparams(("parallel",)),
    )(*os_, *lses)


def _combine_bwd_call(os_, lses, dy, *, tq=256):
    t, w = dy.shape

    def body(o0, o1, o2, l0, l1, l2, dy_ref, do0, do1, do2, dl0, dl1, dl2):
        ws = _combine_weights(l0[...], l1[...], l2[...])
        e = _head_expand()
        dyv = dy_ref[...]
        dws = []
        for o_ref, do_ref, wt in zip((o0, o1, o2), (do0, do1, do2), ws):
            do_ref[...] = _dot_mask(e, wt, mask_left=False) * dyv
            dws.append(_dot_mask(e, dyv * o_ref[...], _NT, mask_left=False))
        mean = ws[0] * dws[0] + ws[1] * dws[1] + ws[2] * dws[2]
        for dl_ref, wt, dw in zip((dl0, dl1, dl2), ws, dws):
            dl_ref[...] = wt * (dw - mean)

    o_spec = pl.BlockSpec((tq, w), lambda i: (i, 0))
    l_spec = pl.BlockSpec((tq, LANES), lambda i: (i, 0))
    o_shape = jax.ShapeDtypeStruct((t, w), F32)
    l_shape = jax.ShapeDtypeStruct((t, LANES), F32)
    return pl.pallas_call(
        body, name="combine_bwd", grid=(t // tq,),
        in_specs=[o_spec] * 3 + [l_spec] * 3 + [o_spec], out_specs=[o_spec] * 3 + [l_spec] * 3,
        out_shape=[o_shape] * 3 + [l_shape] * 3,
        compiler_params=_cparams(("parallel",)),
    )(*os_, *lses, dy)


@jax.custom_vjp
def combine(os_, lses):
    return _combine_fwd_call(os_, lses)


def _combine_fwd(os_, lses):
    return _combine_fwd_call(os_, lses), (os_, lses)


def _combine_bwd(saved, dy):
    os_, lses = saved
    outs = _combine_bwd_call(os_, lses, dy)
    return tuple(outs[:3]), tuple(outs[3:])


combine.defvjp(_combine_fwd, _combine_bwd)


C_QKW = C_QK_HEADS * C_DK
C_CONV_W = 2 * C_QKW + C_V_HEADS * C_DK
HALO = 8


def _silu_parts(z):
    sig = jax.nn.sigmoid(z)
    return z * sig, sig * (1.0 + z * (1.0 - sig))


def _conv_prep_fwd_call(x, w, *, tq=256):
    t, c = x.shape
    nqk = 2 * C_QK_HEADS

    def body(x_ref, halo_ref, w_ref, z_ref, qk_ref, v_ref):
        i = pl.program_id(0)
        halo = jnp.where(i == 0, 0.0, halo_ref[...])
        xc = jnp.concatenate([halo, x_ref[...]], axis=0)
        wv = w_ref[...]
        z = xc[HALO - 3:HALO - 3 + tq] * wv[0:1]
        for j in range(1, C_CONV):
            z = z + xc[HALO - 3 + j:HALO - 3 + j + tq] * wv[j:j + 1]
        z_ref[...] = z
        act, _ = _silu_parts(z)
        for h in range(nqk):
            a = act[:, h * C_DK:(h + 1) * C_DK]
            qk_ref[:, h * C_DK:(h + 1) * C_DK] = a * lax.rsqrt(jnp.sum(a * a, axis=1, keepdims=True) + EPS)
        v_ref[...] = act[:, nqk * C_DK:]

    return pl.pallas_call(
        body, name="conv_prep_fwd", grid=(t // tq,),
        in_specs=[pl.BlockSpec((tq, c), lambda i: (i, 0)),
                  pl.BlockSpec((HALO, c), lambda i: (jnp.maximum(i * (tq // HALO) - 1, 0), 0)),
                  pl.BlockSpec((C_CONV, c), lambda i: (0, 0))],
        out_specs=[pl.BlockSpec((tq, c), lambda i: (i, 0)),
                   pl.BlockSpec((tq, 2 * C_QKW), lambda i: (i, 0)),
                   pl.BlockSpec((tq, c - 2 * C_QKW), lambda i: (i, 0))],
        out_shape=[jax.ShapeDtypeStruct((t, c), F32), jax.ShapeDtypeStruct((t, 2 * C_QKW), F32),
                   jax.ShapeDtypeStruct((t, c - 2 * C_QKW), F32)],
        compiler_params=_cparams(("parallel",)),
    )(x, x, w)


def _conv_prep_dz_call(z, dqk, dv, *, tq=256):
    t, c = z.shape
    nqk = 2 * C_QK_HEADS

    def body(z_ref, dqk_ref, dv_ref, dz_ref):
        zv = z_ref[...]
        act, dact = _silu_parts(zv)
        for h in range(nqk):
            hs = slice(h * C_DK, (h + 1) * C_DK)
            a = act[:, hs]
            r = lax.rsqrt(jnp.sum(a * a, axis=1, keepdims=True) + EPS)
            nrm = a * r
            dn = dqk_ref[:, hs]
            da = r * (dn - nrm * jnp.sum(dn * nrm, axis=1, keepdims=True))
            dz_ref[:, hs] = da * dact[:, hs]
        dz_ref[:, nqk * C_DK:] = dv_ref[...] * dact[:, nqk * C_DK:]

    return pl.pallas_call(
        body, name="conv_prep_dz", grid=(t // tq,),
        in_specs=[pl.BlockSpec((tq, c), lambda i: (i, 0)),
                  pl.BlockSpec((tq, 2 * C_QKW), lambda i: (i, 0)),
                  pl.BlockSpec((tq, c - 2 * C_QKW), lambda i: (i, 0))],
        out_specs=pl.BlockSpec((tq, c), lambda i: (i, 0)),
        out_shape=jax.ShapeDtypeStruct((t, c), F32),
        compiler_params=_cparams(("parallel",)),
    )(z, dqk, dv)


def _conv_bwd_call(x, dz, w, *, tq=256):
    t, c = x.shape
    nt = t // tq

    def body(x_ref, xh_ref, dz_ref, dzh_ref, w_ref, dx_ref, dw_ref):
        i = pl.program_id(0)
        xc = jnp.concatenate([jnp.where(i == 0, 0.0, xh_ref[...]), x_ref[...]], axis=0)
        dzv = dz_ref[...]
        dzc = jnp.concatenate([dzv, jnp.where(i == nt - 1, 0.0, dzh_ref[...])], axis=0)
        wv = w_ref[...]
        dx = dzv * wv[3:4]
        for s in range(1, C_CONV):
            dx = dx + dzc[s:s + tq] * wv[3 - s:4 - s]
        dx_ref[...] = dx
        row = lax.broadcasted_iota(jnp.int32, (8, c), 0)
        dw = jnp.zeros((8, c), F32)
        for j in range(C_CONV):
            prod = dzv * xc[HALO - 3 + j:HALO - 3 + j + tq]
            col = jnp.sum(jnp.sum(prod.reshape(tq // 8, 8, c), axis=0), axis=0, keepdims=True)
            dw = jnp.where(row == j, col, dw)

        @pl.when(i == 0)
        def _():
            dw_ref[...] = dw

        @pl.when(i > 0)
        def _():
            dw_ref[...] += dw

    blk = pl.BlockSpec((tq, c), lambda i: (i, 0))
    return pl.pallas_call(
        body, name="conv_bwd", grid=(nt,),
        in_specs=[blk, pl.BlockSpec((HALO, c), lambda i: (jnp.maximum(i * (tq // HALO) - 1, 0), 0)),
                  blk, pl.BlockSpec((HALO, c), lambda i: (jnp.minimum((i + 1) * (tq // HALO), t // HALO - 1), 0)),
                  pl.BlockSpec((C_CONV, c), lambda i: (0, 0))],
        out_specs=[blk, pl.BlockSpec((8, c), lambda i: (0, 0))],
        out_shape=[jax.ShapeDtypeStruct((t, c), F32), jax.ShapeDtypeStruct((8, c), F32)],
        compiler_params=_cparams(("arbitrary",)),
    )(x, x, dz, dz, w)


@jax.custom_vjp
def conv_prep(x, w):
    _, qk, v = _conv_prep_fwd_call(x, w)
    return qk, v


def _conv_prep_fwd(x, w):
    z, qk, v = _conv_prep_fwd_call(x, w)
    return (qk, v), (x, w, z)


def _conv_prep_bwd(saved, cts):
    x, w, z = saved
    dz = _conv_prep_dz_call(z, cts[0], cts[1])
    dx, dw = _conv_bwd_call(x, dz, w)
    return dx, dw[:C_CONV]


conv_prep.defvjp(_conv_prep_fwd, _conv_prep_bwd)


C_VW = C_V_HEADS * C_DK


def _softplus(x):
    return jnp.maximum(x, 0.0) + jnp.log(1.0 + jnp.exp(-jnp.abs(x)))


def _tri_masks():
    r = lax.broadcasted_iota(jnp.int32, (CHUNK, CHUNK), 0)
    c = lax.broadcasted_iota(jnp.int32, (CHUNK, CHUNK), 1)
    return r >= c, r > c


_NN = (((1,), (0,)), ((), ()))


def _split_bf16(a):
    hi = a.astype(BF16)
    return hi, (a - hi.astype(F32)).astype(BF16)


def _dot_hi(a, b, dims=None):
    dims = _NN if dims is None else dims
    ah, al = _split_bf16(a)
    bh, bl = _split_bf16(b)

    def d(x, y):
        return lax.dot_general(x, y, dims, preferred_element_type=F32)

    return d(ah, bh) + (d(ah, bl) + d(al, bh))


def _dot_mask(mask, b, dims=None, mask_left=True):
    dims = _NN if dims is None else dims
    mb = mask.astype(BF16)
    b1 = b.astype(BF16)
    rest = b - b1.astype(F32)
    b2 = rest.astype(BF16)
    b3 = (rest - b2.astype(F32)).astype(BF16)
    out = None
    for p in (b1, b2, b3):
        term = (lax.dot_general(mb, p, dims, preferred_element_type=F32) if mask_left
                else lax.dot_general(p, mb, dims, preferred_element_type=F32))
        out = term if out is None else out + term
    return out


def _unit_lower_inverses(mats):
    r = lax.broadcasted_iota(jnp.int32, (CHUNK, CHUNK), 0)
    c = lax.broadcasted_iota(jnp.int32, (CHUNK, CHUNK), 1)
    eye = (r == c).astype(F32)
    xs = [eye - a for a in mats]
    ps = [_dot_hi(a, a) for a in mats]
    steps = int(math.log2(CHUNK)) - 1
    for s in range(steps):
        xs = [x + _dot_hi(x, p) for x, p in zip(xs, ps)]
        if s < steps - 1:
            ps = [_dot_hi(p, p) for p in ps]
    return xs


def _gate_tiles(cab, alog, dtb):
    pre = cab + dtb
    g = -jnp.exp(alog) * _softplus(pre)
    beta = jax.nn.sigmoid(pltpu.roll(cab, LANES - C_V_HEADS, 1))
    return g, beta, pre


def _chunk_common(kk, qk, gc, gct, beta, h, tri, strict):
    gcol, grow, bcol = gc[:, h:h + 1], gct[h:h + 1, :], beta[:, h:h + 1]
    decay = jnp.where(tri, jnp.exp(jnp.where(tri, gcol - grow, 0.0)), 0.0)
    kkd = jnp.where(strict, kk * decay, 0.0)
    attn = jnp.where(tri, qk * decay, 0.0)
    glast = gc[CHUNK - 1:CHUNK, h:h + 1]
    return gcol, bcol, decay, kkd, attn, glast


def _delta_prep_call(qk, v, cab, alog, dtb):
    t = qk.shape[0]
    nc = t // CHUNK
    scale = C_DK ** -0.5

    def body(q_ref, k_ref, v_ref, cab_ref, alog_ref, dtb_ref,
             u_ref, w_ref, qd_ref, kd_ref, attn_ref, tmat_ref, gc_ref, beta_ref):
        tri, strict = _tri_masks()
        g, beta, _ = _gate_tiles(cab_ref[...], alog_ref[...], dtb_ref[...])
        gc = _dot_mask(tri, g)
        gct = gc.T
        gc_ref[...] = gc
        beta_ref[...] = beta
        mats, rhs = [], []
        for j in range(C_QK_HEADS):
            js = slice(j * C_DK, (j + 1) * C_DK)
            kf, qf = k_ref[:, js], q_ref[:, js] * scale
            kb, qb = kf.astype(BF16), qf.astype(BF16)
            kk = lax.dot_general(kb, kb, _NT, preferred_element_type=F32)
            qk = lax.dot_general(qb, kb, _NT, preferred_element_type=F32)
            for h in (2 * j, 2 * j + 1):
                hs = slice(h * C_DK, (h + 1) * C_DK)
                gcol, bcol, decay, kkd, attn, glast = _chunk_common(kk, qk, gc, gct, beta, h, tri, strict)
                gexp = jnp.exp(gcol)
                mats.append(kkd * bcol)
                rhs.append(jnp.concatenate([v_ref[:, hs] * bcol, kf * (bcol * gexp)], axis=1))
                qd_ref[:, hs] = qf * gexp
                kd_ref[:, hs] = kf * jnp.exp(glast - gcol)
                attn_ref[:, h * CHUNK:(h + 1) * CHUNK] = attn
        for h, (tmat, r) in enumerate(zip(_unit_lower_inverses(mats), rhs)):
            hs = slice(h * C_DK, (h + 1) * C_DK)
            uw = _dot_hi(tmat, r)
            u_ref[:, hs] = uw[:, :C_DK]
            w_ref[:, hs] = uw[:, C_DK:]
            tmat_ref[:, h * CHUNK:(h + 1) * CHUNK] = tmat

    def blk(w):
        return pl.BlockSpec((CHUNK, w), lambda n: (n, 0))

    row = pl.BlockSpec((1, LANES), lambda n: (0, 0))
    big = jax.ShapeDtypeStruct((t, C_VW), F32)
    sq = jax.ShapeDtypeStruct((t, C_V_HEADS * CHUNK), F32)
    tile = jax.ShapeDtypeStruct((t, LANES), F32)
    return pl.pallas_call(
        body, name="delta_prep", grid=(nc,),
        in_specs=[blk(C_QKW), pl.BlockSpec((CHUNK, C_QKW), lambda n: (n, 1)), blk(C_VW), blk(LANES), row, row],
        out_specs=[blk(C_VW)] * 4 + [blk(C_V_HEADS * CHUNK)] * 2 + [blk(LANES)] * 2,
        out_shape=[big] * 4 + [sq] * 2 + [tile] * 2,
        compiler_params=_cparams(("parallel",)),
    )(qk, qk, v, cab, alog, dtb)


def _delta_scan_call(u, w, qd, kd, attn, gc):
    t = u.shape[0]
    nc = t // CHUNK

    def body(u_ref, w_ref, qd_ref, kd_ref, attn_ref, gc_ref, o_ref, vn_ref, st_ref, s_ref):
        @pl.when(pl.program_id(0) == 0)
        def _():
            s_ref[...] = jnp.zeros_like(s_ref)

        hss = [slice(h * C_DK, (h + 1) * C_DK) for h in range(C_V_HEADS)]
        states = [s_ref[hs, :] for hs in hss]
        for hs, s in zip(hss, states):
            st_ref[0, hs, :] = s
        sbs = [s.astype(BF16) for s in states]
        vns = [u_ref[:, hs] - jnp.dot(w_ref[:, hs].astype(BF16), sb, preferred_element_type=F32)
               for hs, sb in zip(hss, sbs)]
        qss = [jnp.dot(qd_ref[:, hs].astype(BF16), sb, preferred_element_type=F32) for hs, sb in zip(hss, sbs)]
        vnbs = [vn.astype(BF16) for vn in vns]
        for h, hs in enumerate(hss):
            vn_ref[:, hs] = vns[h]
            o_ref[:, hs] = qss[h] + jnp.dot(attn_ref[:, h * CHUNK:(h + 1) * CHUNK].astype(BF16), vnbs[h],
                                            preferred_element_type=F32)
        for h, hs in enumerate(hss):
            glast = jnp.exp(gc_ref[CHUNK - 1:CHUNK, h:h + 1])
            s_ref[hs, :] = states[h] * glast + lax.dot_general(kd_ref[:, hs].astype(BF16), vnbs[h], _TN,
                                                               preferred_element_type=F32)

    def blk(wd):
        return pl.BlockSpec((CHUNK, wd), lambda n: (n, 0))

    big = jax.ShapeDtypeStruct((t, C_VW), F32)
    return pl.pallas_call(
        body, name="delta_scan", grid=(nc,),
        in_specs=[blk(C_VW)] * 4 + [blk(C_V_HEADS * CHUNK), blk(LANES)],
        out_specs=[blk(C_VW), blk(C_VW), pl.BlockSpec((1, C_VW, C_DK), lambda n: (n, 0, 0))],
        out_shape=[big, big, jax.ShapeDtypeStruct((nc, C_VW, C_DK), F32)],
        scratch_shapes=[pltpu.VMEM((C_VW, C_DK), F32)],
        compiler_params=_cparams(("arbitrary",)),
    )(u, w, qd, kd, attn, gc)


def _delta_scan_bwd_call(do, w, qd, kd, attn, gc, vn, st):
    t = do.shape[0]
    nc = t // CHUNK

    def body(do_ref, w_ref, qd_ref, kd_ref, attn_ref, gc_ref, vn_ref, st_ref,
             du_ref, dw_ref, dqd_ref, dkd_ref, dattn_ref, dgl_ref, ds_ref):
        @pl.when(pl.program_id(0) == 0)
        def _():
            ds_ref[...] = jnp.zeros_like(ds_ref)

        tri, _ = _tri_masks()
        row = lax.broadcasted_iota(jnp.int32, (8, LANES), 0)
        lane = lax.broadcasted_iota(jnp.int32, (8, LANES), 1)
        dgl = jnp.zeros((8, LANES), F32)
        hss = [slice(h * C_DK, (h + 1) * C_DK) for h in range(C_V_HEADS)]
        css = [slice(h * CHUNK, (h + 1) * CHUNK) for h in range(C_V_HEADS)]

        def dg(a, b, dims):
            return lax.dot_general(a, b, dims, preferred_element_type=F32)

        ss = [st_ref[0, hs, :] for hs in hss]
        dsps = [ds_ref[hs, :] for hs in hss]
        sbs = [s.astype(BF16) for s in ss]
        dspbs = [d.astype(BF16) for d in dsps]
        dobs = [do_ref[:, hs].astype(BF16) for hs in hss]
        vnbs = [vn_ref[:, hs].astype(BF16) for hs in hss]
        dvns = [dg(attn_ref[:, cs].astype(BF16), dob, _TN) + dg(kd_ref[:, hs].astype(BF16), dspb, _NN)
                for hs, cs, dob, dspb in zip(hss, css, dobs, dspbs)]
        for h, hs in enumerate(hss):
            dqd_ref[:, hs] = dg(dobs[h], sbs[h], _NT)
            dkd_ref[:, hs] = dg(vnbs[h], dspbs[h], _NT)
            dattn_ref[:, css[h]] = jnp.where(tri, dg(dobs[h], vnbs[h], _NT), 0.0)
        dvnbs = [d.astype(BF16) for d in dvns]
        for h, hs in enumerate(hss):
            du_ref[:, hs] = dvns[h]
            dw_ref[:, hs] = -dg(dvnbs[h], sbs[h], _NT)
            tot = jnp.sum(jnp.sum(dsps[h] * ss[h], axis=0, keepdims=True), axis=1, keepdims=True)
            dgl = jnp.where(jnp.logical_and(row == 0, lane == h), tot, dgl)
        for h, hs in enumerate(hss):
            glast = jnp.exp(gc_ref[CHUNK - 1:CHUNK, h:h + 1])
            ds_ref[hs, :] = (dg(qd_ref[:, hs].astype(BF16), dobs[h], _TN) + glast * dsps[h]
                             - dg(w_ref[:, hs].astype(BF16), dvnbs[h], _TN))
        dgl_ref[...] = dgl

    def blk(wd):
        return pl.BlockSpec((CHUNK, wd), lambda n: (nc - 1 - n, 0))

    big = jax.ShapeDtypeStruct((t, C_VW), F32)
    return pl.pallas_call(
        body, name="delta_scan_bwd", grid=(nc,),
        in_specs=[blk(C_VW)] * 4 + [blk(C_V_HEADS * CHUNK), blk(LANES), blk(C_VW),
                                    pl.BlockSpec((1, C_VW, C_DK), lambda n: (nc - 1 - n, 0, 0))],
        out_specs=[blk(C_VW)] * 4 + [blk(C_V_HEADS * CHUNK), pl.BlockSpec((8, LANES), lambda n: (nc - 1 - n, 0))],
        out_shape=[big] * 4 + [jax.ShapeDtypeStruct((t, C_V_HEADS * CHUNK), F32),
                               jax.ShapeDtypeStruct((nc * 8, LANES), F32)],
        scratch_shapes=[pltpu.VMEM((C_VW, C_DK), F32)],
        compiler_params=_cparams(("arbitrary",)),
    )(do, w, qd, kd, attn, gc, vn, st)


def _delta_prep_bwd_call(qk, v, cab, alog, dtb, tmat, u, w, gc, beta, du, dw, dqd, dkd, dattn, dgl):
    t = qk.shape[0]
    nc = t // CHUNK
    scale = C_DK ** -0.5

    def body(q_ref, k_ref, v_ref, cab_ref, alog_ref, dtb_ref, tmat_ref, u_ref, w_ref, gc_ref, beta_ref,
             du_ref, dw_ref, dqd_ref, dkd_ref, dattn_ref, dgl_ref,
             dqk_ref, dv_ref, dcab_ref, dpar_ref):
        tri, strict = _tri_masks()
        gc, beta = gc_ref[...], beta_ref[...]
        gct = gc.T
        ones = jnp.ones((CHUNK, LANES), F32)
        lane = lax.broadcasted_iota(jnp.int32, (CHUNK, LANES), 1)
        rowi = lax.broadcasted_iota(jnp.int32, (CHUNK, 1), 0)
        dgc_tile = jnp.zeros((CHUNK, LANES), F32)
        db_tile = jnp.zeros((CHUNK, LANES), F32)
        heads = []
        for j in range(C_QK_HEADS):
            js = slice(j * C_DK, (j + 1) * C_DK)
            kf, qf = k_ref[:, js], q_ref[:, js] * scale
            kb, qb = kf.astype(BF16), qf.astype(BF16)
            kk = lax.dot_general(kb, kb, _NT, preferred_element_type=F32)
            qk = lax.dot_general(qb, kb, _NT, preferred_element_type=F32)
            for h in (2 * j, 2 * j + 1):
                heads.append((h, kf, qf, kb, qb) + _chunk_common(kk, qk, gc, gct, beta, h, tri, strict))
        dvks = [_dot_hi(tmat_ref[:, h * CHUNK:(h + 1) * CHUNK],
                        jnp.concatenate([du_ref[:, h * C_DK:(h + 1) * C_DK], dw_ref[:, h * C_DK:(h + 1) * C_DK]], axis=1),
                        _TN) for h in range(C_V_HEADS)]
        das = [-jnp.where(strict, _dot_hi(dvk, jnp.concatenate([u_ref[:, h * C_DK:(h + 1) * C_DK],
                                                                w_ref[:, h * C_DK:(h + 1) * C_DK]], axis=1), _NT), 0.0)
               for h, dvk in enumerate(dvks)]
        dq_parts, dk_parts = [], []
        for (h, kf, qf, kb, qb, gcol, bcol, decay, kkd, attn, glast), dvk, da in zip(heads, dvks, das):
            hs = slice(h * C_DK, (h + 1) * C_DK)
            gexp = jnp.exp(gcol)
            fdec = jnp.exp(glast - gcol)
            dvb, dkb = dvk[:, :C_DK], dvk[:, C_DK:]
            dattn_h = dattn_ref[:, h * CHUNK:(h + 1) * CHUNK]
            dkk = (da * decay * bcol).astype(BF16)
            dqk = (dattn_h * decay).astype(BF16)
            e = da * kkd * bcol + dattn_h * attn
            dgc = jnp.sum(e, axis=1, keepdims=True) - _dot_mask(ones, e, _TN, mask_left=False)[:, :1]
            dk_parts.append(jnp.dot(dkk, kb, preferred_element_type=F32)
                            + lax.dot_general(dkk, kb, _TN, preferred_element_type=F32)
                            + lax.dot_general(dqk, qb, _TN, preferred_element_type=F32)
                            + dkb * (bcol * gexp) + dkd_ref[:, hs] * fdec)
            dq_parts.append(jnp.dot(dqk, kb, preferred_element_type=F32) + dqd_ref[:, hs] * gexp)
            dv_ref[:, hs] = dvb * bcol
            s_kb = jnp.sum(dkb * kf, axis=1, keepdims=True)
            db = (jnp.sum(da * kkd, axis=1, keepdims=True) + jnp.sum(dvb * v_ref[:, hs], axis=1, keepdims=True)
                  + s_kb * gexp)
            rho = jnp.sum(dkd_ref[:, hs] * kf, axis=1, keepdims=True) * fdec
            dgc = (dgc + s_kb * bcol * gexp + jnp.sum(dqd_ref[:, hs] * qf, axis=1, keepdims=True) * gexp - rho)
            last = jnp.sum(rho, axis=0, keepdims=True) + dgl_ref[0:1, h:h + 1] * jnp.exp(glast)
            dgc = dgc + jnp.where(rowi == CHUNK - 1, last, 0.0)
            dgc_tile = jnp.where(lane == h, dgc, dgc_tile)
            db_tile = jnp.where(lane == h, db, db_tile)
        for j in range(C_QK_HEADS):
            dqk_ref[:, j * C_DK:(j + 1) * C_DK] = (dq_parts[2 * j] + dq_parts[2 * j + 1]) * scale
            dqk_ref[:, C_QKW + j * C_DK:C_QKW + (j + 1) * C_DK] = dk_parts[2 * j] + dk_parts[2 * j + 1]
        dg = _dot_mask(jnp.logical_not(strict), dgc_tile)
        alog = alog_ref[...]
        g, _, pre = _gate_tiles(cab_ref[...], alog, dtb_ref[...])
        dca = dg * (-jnp.exp(alog)) * jax.nn.sigmoid(pre)
        dcab_ref[...] = dca + pltpu.roll(db_tile * beta * (1.0 - beta), C_V_HEADS, 1)
        row8 = lax.broadcasted_iota(jnp.int32, (8, LANES), 0)
        par = jnp.where(row8 == 0, jnp.sum(dg * g, axis=0, keepdims=True),
                        jnp.where(row8 == 1, jnp.sum(dca, axis=0, keepdims=True), 0.0))

        @pl.when(pl.program_id(0) == 0)
        def _():
            dpar_ref[...] = par

        @pl.when(pl.program_id(0) > 0)
        def _():
            dpar_ref[...] += par

    def blk(wd):
        return pl.BlockSpec((CHUNK, wd), lambda n: (n, 0))

    row = pl.BlockSpec((1, LANES), lambda n: (0, 0))
    sq = blk(C_V_HEADS * CHUNK)
    return pl.pallas_call(
        body, name="delta_prep_bwd", grid=(nc,),
        in_specs=[blk(C_QKW), pl.BlockSpec((CHUNK, C_QKW), lambda n: (n, 1)), blk(C_VW), blk(LANES), row, row, sq,
                  blk(C_VW), blk(C_VW),
                  blk(LANES), blk(LANES), blk(C_VW), blk(C_VW), blk(C_VW), blk(C_VW), sq,
                  pl.BlockSpec((8, LANES), lambda n: (n, 0))],
        out_specs=[blk(2 * C_QKW), blk(C_VW), blk(LANES), pl.BlockSpec((8, LANES), lambda n: (0, 0))],
        out_shape=[jax.ShapeDtypeStruct((t, 2 * C_QKW), F32),
                   jax.ShapeDtypeStruct((t, C_VW), F32), jax.ShapeDtypeStruct((t, LANES), F32),
                   jax.ShapeDtypeStruct((8, LANES), F32)],
        compiler_params=_cparams(("arbitrary",)),
    )(qk, qk, v, cab, alog, dtb, tmat, u, w, gc, beta, du, dw, dqd, dkd, dattn, dgl)


@jax.custom_vjp
def delta_rule(qk, v, cab, alog, dtb):
    u, w, qd, kd, attn, _, gc, _ = _delta_prep_call(qk, v, cab, alog, dtb)
    return _delta_scan_call(u, w, qd, kd, attn, gc)[0]


def _delta_rule_fwd(qk, v, cab, alog, dtb):
    u, w, qd, kd, attn, tmat, gc, beta = _delta_prep_call(qk, v, cab, alog, dtb)
    o, vn, st = _delta_scan_call(u, w, qd, kd, attn, gc)
    return o, (qk, v, cab, alog, dtb, u, w, qd, kd, attn, tmat, gc, beta, vn, st)


def _delta_rule_bwd(saved, do):
    qk, v, cab, alog, dtb, u, w, qd, kd, attn, tmat, gc, beta, vn, st = saved
    du, dw, dqd, dkd, dattn, dgl = _delta_scan_bwd_call(do, w, qd, kd, attn, gc, vn, st)
    dqk, dv, dcab, dpar = _delta_prep_bwd_call(qk, v, cab, alog, dtb, tmat, u, w, gc, beta,
                                               du, dw, dqd, dkd, dattn, dgl)
    return dqk, dv, dcab, dpar[0:1], dpar[1:2]


delta_rule.defvjp(_delta_rule_fwd, _delta_rule_bwd)


def _gated_norm_fwd_call(o, z, gain, *, tq=256):
    t, w = o.shape

    def body(o_ref, z_ref, g_ref, y_ref):
        act, _ = _silu_parts(z_ref[...])
        gv = g_ref[...]
        for h in range(C_V_HEADS):
            hs = slice(h * C_DK, (h + 1) * C_DK)
            ov = o_ref[:, hs]
            r = lax.rsqrt(jnp.mean(ov * ov, axis=1, keepdims=True) + EPS)
            y_ref[:, hs] = ov * r * gv * act[:, hs]

    blk = pl.BlockSpec((tq, w), lambda i: (i, 0))
    return pl.pallas_call(
        body, name="gated_norm_fwd", grid=(t // tq,),
        in_specs=[blk, blk, pl.BlockSpec((1, C_DK), lambda i: (0, 0))], out_specs=blk,
        out_shape=jax.ShapeDtypeStruct((t, w), F32),
        compiler_params=_cparams(("parallel",)),
    )(o, z, gain)


def _gated_norm_bwd_call(o, z, gain, dy, *, tq=256):
    t, w = o.shape
    nt = t // tq

    def body(o_ref, z_ref, g_ref, dy_ref, do_ref, dz_ref, dg_ref, acc_ref):
        i = pl.program_id(0)
        act, dact = _silu_parts(z_ref[...])
        gv = g_ref[...]
        part = jnp.zeros((8, C_DK), F32)
        for h in range(C_V_HEADS):
            hs = slice(h * C_DK, (h + 1) * C_DK)
            ov = o_ref[:, hs]
            r = lax.rsqrt(jnp.mean(ov * ov, axis=1, keepdims=True) + EPS)
            xh = ov * r
            dyv = dy_ref[:, hs]
            dn = dyv * act[:, hs]
            dz_ref[:, hs] = dyv * xh * gv * dact[:, hs]
            dxh = dn * gv
            do_ref[:, hs] = r * (dxh - xh * jnp.mean(dxh * xh, axis=1, keepdims=True))
            part = part + jnp.sum((dn * xh).reshape(tq // 8, 8, C_DK), axis=0)

        @pl.when(i == 0)
        def _():
            acc_ref[...] = part

        @pl.when(i > 0)
        def _():
            acc_ref[...] += part

        @pl.when(i == nt - 1)
        def _():
            dg_ref[...] = jnp.sum(acc_ref[...], axis=0, keepdims=True)

    blk = pl.BlockSpec((tq, w), lambda i: (i, 0))
    grow = pl.BlockSpec((1, C_DK), lambda i: (0, 0))
    big = jax.ShapeDtypeStruct((t, w), F32)
    return pl.pallas_call(
        body, name="gated_norm_bwd", grid=(nt,),
        in_specs=[blk, blk, grow, blk], out_specs=[blk, blk, grow],
        out_shape=[big, big, jax.ShapeDtypeStruct((1, C_DK), F32)],
        scratch_shapes=[pltpu.VMEM((8, C_DK), F32)],
        compiler_params=_cparams(("arbitrary",)),
    )(o, z, gain, dy)


@jax.custom_vjp
def gated_norm(o, z, gain):
    return _gated_norm_fwd_call(o, z, gain)


def _gated_norm_fwd(o, z, gain):
    return _gated_norm_fwd_call(o, z, gain), (o, z, gain)


def _gated_norm_bwd(saved, dy):
    return tuple(_gated_norm_bwd_call(*saved, dy))


gated_norm.defvjp(_gated_norm_fwd, _gated_norm_bwd)


def _merge_fwd_call(ps, gs, *, tq=256):
    t, w = ps[0].shape

    def body(p0, p1, p2, g0, g1, g2, y_ref):
        y_ref[...] = (jax.nn.sigmoid(g0[...]) * p0[...] + jax.nn.sigmoid(g1[...]) * p1[...]
                      + jax.nn.sigmoid(g2[...]) * p2[...])

    blk = pl.BlockSpec((tq, w), lambda i: (i, 0))
    return pl.pallas_call(
        body, name="merge_fwd", grid=(t // tq,), in_specs=[blk] * 6, out_specs=blk,
        out_shape=jax.ShapeDtypeStruct((t, w), F32),
        compiler_params=_cparams(("parallel",)),
    )(*ps, *gs)


def _merge_bwd_call(ps, gs, dy, *, tq=256):
    t, w = dy.shape

    def body(p0, p1, p2, g0, g1, g2, dy_ref, dp0, dp1, dp2, dg0, dg1, dg2):
        dyv = dy_ref[...]
        for p, g, dp, dg in ((p0, g0, dp0, dg0), (p1, g1, dp1, dg1), (p2, g2, dp2, dg2)):
            sig = jax.nn.sigmoid(g[...])
            dp[...] = dyv * sig
            dg[...] = dyv * p[...] * sig * (1.0 - sig)

    blk = pl.BlockSpec((tq, w), lambda i: (i, 0))
    big = jax.ShapeDtypeStruct((t, w), F32)
    return pl.pallas_call(
        body, name="merge_bwd", grid=(t // tq,), in_specs=[blk] * 7, out_specs=[blk] * 6,
        out_shape=[big] * 6,
        compiler_params=_cparams(("parallel",)),
    )(*ps, *gs, dy)


@jax.custom_vjp
def gate_merge(ps, gs):
    return _merge_fwd_call(ps, gs)


def _gate_merge_fwd(ps, gs):
    return _merge_fwd_call(ps, gs), (ps, gs)


def _gate_merge_bwd(saved, dy):
    ps, gs = saved
    outs = _merge_bwd_call(ps, gs, dy)
    return tuple(outs[:3]), tuple(outs[3:])


gate_merge.defvjp(_gate_merge_fwd, _gate_merge_bwd)


_SPLITS = tuple((name, start, max(width, LANES)) for name, start, width, _ in IN_LAYOUT)


@jax.custom_vjp
def split_cols(u):
    return tuple(u[:, s:s + w] for _, s, w in _SPLITS)


def _split_cols_fwd(u):
    return split_cols(u), u.shape[0]


def _split_cols_bwd(t, cts):
    used = _SPLITS[-1][1] + _SPLITS[-1][2]
    return (jnp.concatenate(list(cts) + [jnp.zeros((t, D_IN_PAD - used), F32)], axis=1),)


split_cols.defvjp(_split_cols_fwd, _split_cols_bwd)


def _layer(x, tabs, p, wb):
    u = norm_in_proj(x, p["norm_mix"], p["w_in"], wb["w_in"], p["b_in"])
    a_q, a_k, a_v, b_q, c_qkv, c_z, gate_a, gate_b, gate_c, b_k, b_v, c_ab = split_cols(u)
    q, k = rope(a_q, tabs), rope(a_k, tabs)
    no_sink = jnp.zeros((8,), F32)
    outs = [f(q, k, a_v, no_sink) for f in ATTN_A]
    ya = combine(tuple(o for o, _ in outs), tuple(l for _, l in outs))
    yb, _ = ATTN_B(rope(b_q, tabs), rope(b_k, tabs), b_v, p["sinks"])
    qk, v = conv_prep(c_qkv, p["conv_w"])
    o = delta_rule(qk, v, c_ab, p["a_log"], p["dt_bias"])
    yc = gated_norm(o, c_z, p["c_norm"])
    merged = gate_merge(tuple(linear(y, p[n], wb[n]) for y, n in
                              ((ya, "w_branch_a"), (yb, "w_branch_b"), (yc, "w_branch_c"))),
                        (gate_a, gate_b, gate_c))
    x = linear_res(merged, p["w_out"], wb["w_out"], x)
    return norm_ffn(x, p["norm_ffn"], p["w_ff1"], p["w_ff2"], wb["w_ff1"], wb["w_ff2"])


def _local_loss(x, params, wbf, tabs, tgt):
    for layer in range(DEPTH):
        x = _layer(x, tabs, {n: w[layer] for n, w in params.items() if n != "norm_final"},
                   {n: w[layer] for n, w in wbf.items()})
    return norm_loss(x, params["norm_final"], tgt)


def _in_cols_to_kernel(w):
    lead = w.shape[:-1]
    parts, pos = [], 0
    for _, start, width, ref_start in IN_LAYOUT:
        if start > pos:
            parts.append(jnp.zeros(lead + (start - pos,), w.dtype))
        parts.append(w[..., ref_start:ref_start + width])
        pos = start + width
    parts.append(jnp.zeros(lead + (D_IN_PAD - pos,), w.dtype))
    return jnp.concatenate(parts, axis=-1)


def _in_cols_to_reference(w):
    by_ref = sorted(IN_LAYOUT, key=lambda e: e[3])
    return jnp.concatenate([w[..., start:start + width] for _, start, width, _ in by_ref], axis=-1)


def _pad_lanes(v):
    return jnp.pad(v, ((0, 0), (0, LANES - v.shape[1])))[:, None, :]


BIG = (("w_in", 2), ("conv_w", 2), ("w_branch_a", 2), ("w_branch_b", 2), ("w_branch_c", 1), ("w_out", 1),
       ("w_ff1", 2), ("w_ff2", 1))
SMALL = ("norm_mix", "b_in", "a_log", "dt_bias", "sinks", "c_norm", "norm_ffn", "norm_final")
WEIGHTS = ("norm_mix", "w_in", "b_in", "conv_w", "a_log", "dt_bias", "sinks", "c_norm", "w_branch_a",
           "w_branch_b", "w_branch_c", "w_out", "norm_ffn", "w_ff1", "w_ff2", "norm_final")
PACK_ROWS = 1024


ROW_ALIGN = 16


def _seg_rows(n):
    return -(-n // (LANES * ROW_ALIGN)) * ROW_ALIGN


def _pack(arrays, lead=0):
    parts = []
    for a in arrays:
        lead_shape = a.shape[:lead]
        n = math.prod(a.shape[lead:])
        rows = _seg_rows(n)
        if rows * LANES != n:
            a = jnp.pad(a.reshape(lead_shape + (n,)), [(0, 0)] * lead + [(0, rows * LANES - n)])
        parts.append(a.reshape(lead_shape + (rows, LANES)))
    total = sum(p.shape[lead] for p in parts)
    padded = -(-total // PACK_ROWS) * PACK_ROWS
    if padded > total:
        parts.append(jnp.zeros(parts[0].shape[:lead] + (padded - total, LANES), parts[0].dtype))
    return jnp.concatenate(parts, axis=lead)


def _unpack(buf, shapes):
    lead = buf.shape[:-2]
    out, pos = [], 0
    for shp in shapes:
        n = math.prod(shp)
        rows = _seg_rows(n)
        seg = buf[..., pos:pos + rows, :]
        if rows * LANES != n:
            seg = seg.reshape(lead + (rows * LANES,))[..., :n]
        out.append(seg.reshape(lead + tuple(shp)))
        pos += rows
    return out


def _shards_to_full(blocks, axis):
    moved = jnp.moveaxis(blocks, 0, axis)
    shp = list(blocks.shape[1:])
    shp[axis] = shp[axis] * N_DEV
    return moved.reshape(shp)


def _full_to_shards(full, axis):
    shp = list(full.shape)
    shp[axis:axis + 1] = [N_DEV, shp[axis] // N_DEV]
    return jnp.moveaxis(full.reshape(shp), axis, 0)


MESH_ID = pl.DeviceIdType.MESH
HBM_SPEC = pl.BlockSpec(memory_space=pl.ANY)


def _my_place():
    return lax.axis_index("x"), lax.axis_index("y"), lax.axis_index("c")


def _slot(x, y, c):
    return 4 * x + 2 * y + c


def _all_gather(block, *, name):
    rows = block.shape[0]

    def body(x_ref, out_ref, send_sems, recv_sems, local_sem):
        x, y, c = _my_place()
        me, sibling = (x, y, c), (x, y, 1 - c)
        chips = [(1 - x, y), (x, 1 - y), (1 - x, 1 - y)]

        def copy(k, blk, to, src=None):
            dst = out_ref.at[_slot(*blk)]
            return pltpu.make_async_remote_copy(
                src_ref=dst if src is None else src, dst_ref=dst,
                send_sem=send_sems.at[k], recv_sem=recv_sems.at[k], device_id=to, device_id_type=MESH_ID)

        mine = pltpu.make_async_copy(x_ref, out_ref.at[_slot(*me)], local_sem)
        mine.start()
        first = [copy(0, me, sibling, src=x_ref)]
        first += [copy(1 + j, me, (*chip, c), src=x_ref) for j, chip in enumerate(chips)]
        for cp in first:
            cp.start()
        passed = [copy(4 + j, (*chip, c), sibling) for j, chip in enumerate(chips)]
        for j, chip in enumerate(chips):
            copy(1 + j, (*chip, c), me).wait_recv()
            passed[j].start()
        copy(0, sibling, me).wait_recv()
        for j, chip in enumerate(chips):
            copy(4 + j, (*chip, 1 - c), me).wait_recv()
        for cp in first + passed:
            cp.wait_send()
        mine.wait()

    return pl.pallas_call(
        body, name=name,
        out_shape=jax.ShapeDtypeStruct((N_DEV, rows, LANES), block.dtype),
        in_specs=[HBM_SPEC], out_specs=HBM_SPEC,
        scratch_shapes=[pltpu.SemaphoreType.DMA((7,)), pltpu.SemaphoreType.DMA((7,)), pltpu.SemaphoreType.DMA],
    )(block)


def _all_to_all(blocks, *, name):
    def body(g_ref, out_ref, send_sems, recv_sems, local_sem):
        x, y, c = _my_place()
        mine_slot = _slot(x, y, c)
        local = pltpu.make_async_copy(g_ref.at[mine_slot], out_ref.at[mine_slot], local_sem)
        local.start()
        copies = []
        for k in range(1, N_DEV):
            px, py, pc = x ^ (k >> 2), y ^ ((k >> 1) & 1), c ^ (k & 1)
            copies.append(pltpu.make_async_remote_copy(
                src_ref=g_ref.at[_slot(px, py, pc)], dst_ref=out_ref.at[mine_slot],
                send_sem=send_sems.at[k - 1], recv_sem=recv_sems.at[k - 1],
                device_id=(px, py, pc), device_id_type=MESH_ID))
        for cp in copies:
            cp.start()
        for cp in copies:
            cp.wait_recv()
        for cp in copies:
            cp.wait_send()
        local.wait()

    return pl.pallas_call(
        body, name=name,
        out_shape=jax.ShapeDtypeStruct(blocks.shape, blocks.dtype),
        in_specs=[HBM_SPEC], out_specs=HBM_SPEC,
        scratch_shapes=[pltpu.SemaphoreType.DMA((7,)), pltpu.SemaphoreType.DMA((7,)), pltpu.SemaphoreType.DMA],
    )(blocks)


def _adamw_call(parts, w, m, v, *, name):
    rows = w.shape[0]
    tr = min(PACK_ROWS, rows)
    assert rows % tr == 0

    def body(p_ref, w_ref, m_ref, v_ref, g_ref, d_ref, nm_ref, nv_ref):
        g = p_ref[0].astype(F32)
        for s in range(1, N_DEV):
            g = g + p_ref[s].astype(F32)
        nm = ADAM_B1 * m_ref[...] + (1.0 - ADAM_B1) * g
        nv = ADAM_B2 * v_ref[...] + (1.0 - ADAM_B2) * jnp.square(g)
        m_hat = nm / (1.0 - ADAM_B1 ** ADAM_STEP)
        v_hat = nv / (1.0 - ADAM_B2 ** ADAM_STEP)
        g_ref[...] = g
        nm_ref[...] = nm
        nv_ref[...] = nv
        d_ref[...] = -ADAM_LR * (m_hat / (jnp.sqrt(v_hat) + ADAM_EPS) + ADAM_WD * w_ref[...])

    blk = pl.BlockSpec((tr, LANES), lambda i: (i, 0))
    shape = jax.ShapeDtypeStruct((rows, LANES), F32)
    return pl.pallas_call(
        body, name=name, grid=(rows // tr,),
        in_specs=[pl.BlockSpec((N_DEV, tr, LANES), lambda i: (0, i, 0)), blk, blk, blk],
        out_specs=[blk] * 4, out_shape=[shape] * 4,
        compiler_params=_cparams(("parallel",)),
    )(parts, w, m, v)


MATMUL_WEIGHTS = ("w_in", "w_branch_a", "w_branch_b", "w_branch_c", "w_out", "w_ff1", "w_ff2")


def _kernel_params(full, wbf):
    params = {
        "norm_mix": full["norm_mix"][:, None, :],
        "b_in": _in_cols_to_kernel(full["b_in"])[:, None, :],
        "conv_w": full["conv_w"],
        "a_log": _pad_lanes(full["a_log"]),
        "dt_bias": _pad_lanes(full["dt_bias"]),
        "sinks": full["sinks"],
        "c_norm": full["c_norm"][:, None, :],
        "norm_ffn": full["norm_ffn"][:, None, :],
        "norm_final": full["norm_final"][None, :],
    }
    for n in MATMUL_WEIGHTS:
        params[n] = jnp.zeros(wbf[n].shape, F32)
    return params


def _reference_grads(g):
    return {
        "norm_mix": g["norm_mix"][:, 0, :],
        "w_in": _in_cols_to_reference(g["w_in"]),
        "b_in": _in_cols_to_reference(g["b_in"][:, 0, :]),
        "conv_w": g["conv_w"],
        "a_log": g["a_log"][:, 0, :C_V_HEADS],
        "dt_bias": g["dt_bias"][:, 0, :C_V_HEADS],
        "sinks": g["sinks"],
        "c_norm": g["c_norm"][:, 0, :],
        "w_branch_a": g["w_branch_a"], "w_branch_b": g["w_branch_b"], "w_branch_c": g["w_branch_c"],
        "w_out": g["w_out"],
        "norm_ffn": g["norm_ffn"][:, 0, :],
        "w_ff1": g["w_ff1"], "w_ff2": g["w_ff2"],
        "norm_final": g["norm_final"][0],
    }


def kernel(x, positions, norm_mix, w_in, b_in, conv_w, a_log, dt_bias, sinks, c_norm, w_branch_a, w_branch_b, w_branch_c, w_out, norm_ffn, w_ff1, w_ff2, norm_final, loss_target, m_norm_mix, m_w_in, m_b_in, m_conv_w, m_a_log, m_dt_bias, m_sinks, m_c_norm, m_w_branch_a, m_w_branch_b, m_w_branch_c, m_w_out, m_norm_ffn, m_w_ff1, m_w_ff2, m_norm_final, v_norm_mix, v_w_in, v_b_in, v_conv_w, v_a_log, v_dt_bias, v_sinks, v_c_norm, v_w_branch_a, v_w_branch_b, v_w_branch_c, v_w_out, v_norm_ffn, v_w_ff1, v_w_ff2, v_norm_final):
    env = dict(locals())
    weights = {n: env[n] for n in WEIGHTS}
    moments_m = {n: env["m_" + n] for n in WEIGHTS}
    moments_v = {n: env["v_" + n] for n in WEIGHTS}

    axis_of = dict(BIG)
    cw = weights["conv_w"]
    c1 = cw.astype(BF16)
    c2 = (cw - c1.astype(F32)).astype(BF16)
    c3 = (cw - c1.astype(F32) - c2.astype(F32)).astype(BF16)
    gathered = _all_gather(_pack([weights[n].astype(BF16) for n in MATMUL_WEIGHTS] + [c1, c2, c3]),
                           name="gather_weights")
    blocks = _unpack(gathered, [weights[n].shape for n in MATMUL_WEIGHTS] + [cw.shape] * 3)
    wbf = {n: _shards_to_full(blk, axis_of[n]) for n, blk in zip(MATMUL_WEIGHTS, blocks)}
    wbf["w_in"] = _in_cols_to_kernel(wbf["w_in"])
    full = {n: weights[n] for n in SMALL}
    full["conv_w"] = _shards_to_full(sum(b.astype(F32) for b in blocks[-3:]), axis_of["conv_w"])

    tabs = rope_tables(positions[0])
    loss, (dx, dparams) = jax.value_and_grad(_local_loss, argnums=(0, 1))(
        x[0], _kernel_params(full, wbf), wbf, tabs, loss_target[0])
    grads = _reference_grads(dparams)
    loss = lax.psum(loss, ("x", "y", "c"))

    big_parts = _all_to_all(_pack([_full_to_shards(grads[n], axis).astype(BF16) for n, axis in BIG], lead=1),
                            name="scatter_grads")
    small_parts = _all_gather(_pack([grads[n] for n in SMALL]), name="gather_small_grads")

    out = {}
    for names, parts in (([n for n, _ in BIG], big_parts), (list(SMALL), small_parts)):
        shapes = [weights[n].shape for n in names]
        packed = [_pack([d[n] for n in names]) for d in (weights, moments_m, moments_v)]
        results = _adamw_call(parts, *packed, name="adamw_" + names[0])
        for kind, buf in zip(("grad", "delta", "new_m", "new_v"), results):
            for n, arr in zip(names, _unpack(buf, shapes)):
                out[kind, n] = arr
    return (loss, dx[None], *[out[kind, n] for kind in ("grad", "delta", "new_m", "new_v") for n in WEIGHTS])
```

```python
import functools
import math

import jax
import jax.numpy as jnp
from jax import lax
from jax.experimental import pallas as pl
from jax.experimental.pallas import tpu as pltpu

F32 = jnp.float32
BF16 = jnp.bfloat16
HIGHEST = lax.Precision.HIGHEST

N_DEV = 8
D_MODEL = 1024
DEPTH = 2
HEAD_DIM = 64
ROT_DIM = 16
ROPE_THETA = 500000.0
BLK = 128
NEG_INF = -1e30
EPS = 1e-6
A_CONFIGS = ((128, 1), (512, 4), (2048, 16))
B_GROUP = 4
C_QK_HEADS = 4
C_V_HEADS = 8
C_DK = 128
C_CONV = 4
CHUNK = 64
D_FF = 4096
D_IN = 8464
ADAM_LR = 0.001
ADAM_B1 = 0.9
ADAM_B2 = 0.999
ADAM_EPS = 1e-08
ADAM_WD = 0.01
ADAM_STEP = 10

IN_LAYOUT = (
    ("a_q", 0, 512, 0), ("a_k", 512, 512, 512), ("a_v", 1024, 512, 1024), ("b_q", 1536, 512, 1536),
    ("c_qkv", 2048, 2048, 2304), ("c_z", 4096, 1024, 4352),
    ("gate_a", 5120, 1024, 5392), ("gate_b", 6144, 1024, 6416), ("gate_c", 7168, 1024, 7440),
    ("b_k", 8192, 128, 2048), ("b_v", 8320, 128, 2176), ("c_ab", 8448, 16, 5376),
)
D_IN_PAD = 8704
LANES = 128
VMEM_LIMIT = 56 * 1024 * 1024


def _cparams(sem=None):
    return pltpu.CompilerParams(dimension_semantics=sem, vmem_limit_bytes=VMEM_LIMIT)


def _relu2(t):
    return jnp.square(jnp.maximum(t, 0.0))


def _mm(a, b, *, ta=False, tb=False, bias=None, a_fn=None, mul_drelu2=None, add=None,
        out_dtype=F32, relu2_out=False, tm=1024, tn=1024, tk=2048, name):
    if ta:
        kdim, m = a.shape
    else:
        m, kdim = a.shape
    n = b.shape[0] if tb else b.shape[1]
    tm, tn, tk = min(tm, m), min(tn, n), min(tk, kdim)
    assert m % tm == 0 and n % tn == 0 and kdim % tk == 0, (a.shape, b.shape, tm, tn, tk)
    nk = kdim // tk
    dims = (((0 if ta else 1,), (1 if tb else 0,)), ((), ()))
    extras = [e for e in (bias, mul_drelu2, add) if e is not None]

    def body(*refs):
        a_ref, b_ref = refs[0], refs[1]
        pos = 2
        bias_ref = pre_ref = add_ref = None
        if bias is not None:
            bias_ref = refs[pos]; pos += 1
        if mul_drelu2 is not None:
            pre_ref = refs[pos]; pos += 1
        if add is not None:
            add_ref = refs[pos]; pos += 1
        o_ref = refs[pos]
        pos += 1
        r_ref = None
        if relu2_out:
            r_ref = refs[pos]; pos += 1
        acc_ref = refs[pos] if nk > 1 else None

        av = a_ref[...]
        if a_fn is not None:
            av = a_fn(av)
        part = lax.dot_general(av.astype(BF16), b_ref[...].astype(BF16), dims,
                               preferred_element_type=F32)

        def finish(acc):
            if bias_ref is not None:
                acc = acc + bias_ref[...]
            if pre_ref is not None:
                acc = acc * (2.0 * jnp.maximum(pre_ref[...], 0.0))
            if add_ref is not None:
                acc = acc + add_ref[...]
            o_ref[...] = acc.astype(out_dtype)
            if r_ref is not None:
                r_ref[...] = _relu2(acc).astype(BF16)

        if nk == 1:
            finish(part)
        else:
            k = pl.program_id(2)

            @pl.when(k == 0)
            def _():
                acc_ref[...] = part

            @pl.when(k > 0)
            def _():
                acc_ref[...] += part

            @pl.when(k == nk - 1)
            def _():
                finish(acc_ref[...])

    a_spec = (pl.BlockSpec((tk, tm), lambda i, j, k: (k, i)) if ta
              else pl.BlockSpec((tm, tk), lambda i, j, k: (i, k)))
    b_spec = (pl.BlockSpec((tn, tk), lambda i, j, k: (j, k)) if tb
              else pl.BlockSpec((tk, tn), lambda i, j, k: (k, j)))
    in_specs = [a_spec, b_spec]
    if bias is not None:
        in_specs.append(pl.BlockSpec((1, tn), lambda i, j, k: (0, j)))
    for _ in extras[(1 if bias is not None else 0):]:
        in_specs.append(pl.BlockSpec((tm, tn), lambda i, j, k: (i, j)))
    o_spec = pl.BlockSpec((tm, tn), lambda i, j, k: (i, j))
    o_shape = jax.ShapeDtypeStruct((m, n), out_dtype)
    return pl.pallas_call(
        body, name=name,
        grid=(m // tm, n // tn, nk),
        in_specs=in_specs,
        out_specs=[o_spec, o_spec] if relu2_out else o_spec,
        out_shape=[o_shape, jax.ShapeDtypeStruct((m, n), BF16)] if relu2_out else o_shape,
        scratch_shapes=[pltpu.VMEM((tm, tn), F32)] if nk > 1 else [],
        compiler_params=_cparams(("parallel", "parallel", "arbitrary")),
    )(a, b, *extras)


def _colsum(x, *, name, tk=512, tn=1024):
    t, n = x.shape
    tn = min(tn, n)
    assert t % tk == 0 and n % tn == 0

    def body(x_ref, o_ref, acc_ref):
        k = pl.program_id(1)
        part = jnp.sum(x_ref[...].astype(F32).reshape(tk // 8, 8, tn), axis=0)

        @pl.when(k == 0)
        def _():
            acc_ref[...] = part

        @pl.when(k > 0)
        def _():
            acc_ref[...] += part

        @pl.when(k == t // tk - 1)
        def _():
            o_ref[...] = jnp.sum(acc_ref[...], axis=0, keepdims=True)

    return pl.pallas_call(
        body, name=name, grid=(n // tn, t // tk),
        in_specs=[pl.BlockSpec((tk, tn), lambda j, k: (k, j))],
        out_specs=pl.BlockSpec((1, tn), lambda j, k: (0, j)),
        out_shape=jax.ShapeDtypeStruct((1, n), F32),
        scratch_shapes=[pltpu.VMEM((8, tn), F32)],
        compiler_params=_cparams(("parallel", "arbitrary")),
    )(x)


@jax.custom_vjp
def linear(a, w, wb):
    return _mm(a, wb, name="linear_fwd")


def _linear_fwd(a, w, wb):
    return _mm(a, wb, name="linear_fwd"), (a, wb)


def _linear_bwd(res, dy):
    a, wb = res
    da = _mm(dy, wb, tb=True, name="linear_da")
    dw = _mm(a, dy, ta=True, tk=1024, name="linear_dw")
    return da, dw, jnp.zeros_like(wb)


linear.defvjp(_linear_fwd, _linear_bwd)


@jax.custom_vjp
def linear_res(a, w, wb, res):
    return _mm(a, wb, add=res, name="linear_res_fwd")


def _linear_res_fwd(a, w, wb, res):
    return _mm(a, wb, add=res, name="linear_res_fwd"), (a, wb)


def _linear_res_bwd(saved, dy):
    a, wb = saved
    da = _mm(dy, wb, tb=True, name="linear_res_da")
    dw = _mm(a, dy, ta=True, tk=1024, name="linear_res_dw")
    return da, dw, jnp.zeros_like(wb), dy


linear_res.defvjp(_linear_res_fwd, _linear_res_bwd)


IN_TN = D_IN_PAD // 4


def _norm_in_proj_run(x, g, wb, b):
    h = _rms_fwd_call(x, g, name="rms_mix_fwd", out_dtype=BF16)
    return _mm(h, wb, bias=b, tn=IN_TN, name="in_proj_fwd"), h


@jax.custom_vjp
def norm_in_proj(x, g, w, wb, b):
    return _norm_in_proj_run(x, g, wb, b)[0]


def _norm_in_proj_fwd(x, g, w, wb, b):
    u, h = _norm_in_proj_run(x, g, wb, b)
    return u, (x, g, wb, h)


def _norm_in_proj_bwd(saved, du):
    x, g, wb, h = saved
    dub = du.astype(BF16)
    dh = _mm(dub, wb, tb=True, tk=IN_TN, name="in_proj_da")
    dw = _mm(h, dub, ta=True, tn=512, tk=1024, name="in_proj_dw")
    db = _colsum(dub, name="in_proj_db", tn=IN_TN)
    dx, dg = _rms_bwd_call(x, g, dh, name="rms_mix_bwd")
    return dx, dg, dw, jnp.zeros_like(wb), db


norm_in_proj.defvjp(_norm_in_proj_fwd, _norm_in_proj_bwd)


def _norm_ffn_run(x, g, w1b, w2b):
    h = _rms_fwd_call(x, g, name="rms_ffn_fwd", out_dtype=BF16)
    pre, act = _mm(h, w1b, relu2_out=True, name="ffn_up")
    return _mm(act, w2b, add=x, name="ffn_down"), h, pre, act


@jax.custom_vjp
def norm_ffn(x, g, w1, w2, w1b, w2b):
    return _norm_ffn_run(x, g, w1b, w2b)[0]


def _norm_ffn_fwd(x, g, w1, w2, w1b, w2b):
    y, h, pre, act = _norm_ffn_run(x, g, w1b, w2b)
    return y, (x, g, w1b, w2b, h, pre, act)


def _norm_ffn_bwd(saved, dy):
    x, g, w1b, w2b, h, pre, act = saved
    dpre = _mm(dy, w2b, tb=True, mul_drelu2=pre, out_dtype=BF16, name="ffn_dpre")
    dw2 = _mm(act, dy, ta=True, tk=1024, name="ffn_dw2")
    dw1 = _mm(h, dpre, ta=True, tk=1024, name="ffn_dw1")
    dh = _mm(dpre, w1b, tb=True, name="ffn_dh")
    dx, dg = _rms_bwd_call(x, g, dh, add=dy, name="rms_ffn_bwd")
    return dx, dg, dw1, dw2, jnp.zeros_like(w1b), jnp.zeros_like(w2b)


norm_ffn.defvjp(_norm_ffn_fwd, _norm_ffn_bwd)


def _rms_fwd_call(x, g, *, name, out_dtype=F32, tq=512):
    t, d = x.shape

    def body(x_ref, g_ref, y_ref):
        xv = x_ref[...]
        r = lax.rsqrt(jnp.mean(xv * xv, axis=-1, keepdims=True) + EPS)
        y_ref[...] = (xv * r * g_ref[...]).astype(out_dtype)

    return pl.pallas_call(
        body, name=name, grid=(t // tq,),
        in_specs=[pl.BlockSpec((tq, d), lambda i: (i, 0)), pl.BlockSpec((1, d), lambda i: (0, 0))],
        out_specs=pl.BlockSpec((tq, d), lambda i: (i, 0)),
        out_shape=jax.ShapeDtypeStruct((t, d), out_dtype),
        compiler_params=_cparams(("parallel",)),
    )(x, g)


def _rms_bwd_call(x, g, dy, *, name, add=None, tq=512):
    t, d = x.shape
    nt = t // tq

    def body(*refs):
        if add is None:
            x_ref, g_ref, dy_ref, dx_ref, dg_ref, acc_ref = refs
        else:
            x_ref, g_ref, dy_ref, add_ref, dx_ref, dg_ref, acc_ref = refs
        i = pl.program_id(0)
        xv = x_ref[...]
        r = lax.rsqrt(jnp.mean(xv * xv, axis=-1, keepdims=True) + EPS)
        xh = xv * r
        dyv = dy_ref[...]
        dxh = dyv * g_ref[...]
        dx = r * (dxh - xh * jnp.mean(dxh * xh, axis=-1, keepdims=True))
        dx_ref[...] = dx if add is None else dx + add_ref[...]
        part = jnp.sum((dyv * xh).reshape(tq // 8, 8, d), axis=0)

        @pl.when(i == 0)
        def _():
            acc_ref[...] = part

        @pl.when(i > 0)
        def _():
            acc_ref[...] += part

        @pl.when(i == nt - 1)
        def _():
            dg_ref[...] = jnp.sum(acc_ref[...], axis=0, keepdims=True)

    blk = pl.BlockSpec((tq, d), lambda i: (i, 0))
    row = pl.BlockSpec((1, d), lambda i: (0, 0))
    extra = [] if add is None else [add]
    return pl.pallas_call(
        body, name=name, grid=(nt,),
        in_specs=[blk, row, blk] + [blk] * len(extra),
        out_specs=[blk, row],
        out_shape=[jax.ShapeDtypeStruct((t, d), F32), jax.ShapeDtypeStruct((1, d), F32)],
        scratch_shapes=[pltpu.VMEM((8, d), F32)],
        compiler_params=_cparams(("arbitrary",)),
    )(x, g, dy, *extra)


def _loss_fwd_call(x, g, tgt, *, tq=512):
    t, d = x.shape
    nt = t // tq

    def body(x_ref, g_ref, t_ref, o_ref, acc_ref):
        i = pl.program_id(0)
        xv = x_ref[...]
        r = lax.rsqrt(jnp.mean(xv * xv, axis=-1, keepdims=True) + EPS)
        err = xv * r * g_ref[...] - t_ref[...]
        part = jnp.sum((err * err).reshape(tq // 8, 8, d), axis=0)

        @pl.when(i == 0)
        def _():
            acc_ref[...] = part

        @pl.when(i > 0)
        def _():
            acc_ref[...] += part

        @pl.when(i == nt - 1)
        def _():
            tot = jnp.sum(jnp.sum(acc_ref[...], axis=0, keepdims=True), axis=1, keepdims=True)
            o_ref[...] = jnp.broadcast_to(tot * (0.5 / d), (8, LANES))

    out = pl.pallas_call(
        body, name="loss_fwd", grid=(nt,),
        in_specs=[pl.BlockSpec((tq, d), lambda i: (i, 0)), pl.BlockSpec((1, d), lambda i: (0, 0)),
                  pl.BlockSpec((tq, d), lambda i: (i, 0))],
        out_specs=pl.BlockSpec((8, LANES), lambda i: (0, 0)),
        out_shape=jax.ShapeDtypeStruct((8, LANES), F32),
        scratch_shapes=[pltpu.VMEM((8, d), F32)],
        compiler_params=_cparams(("arbitrary",)),
    )(x, g, tgt)
    return out[0, 0]


def _loss_bwd_call(x, g, tgt, scale, *, tq=512):
    t, d = x.shape
    nt = t // tq

    def body(x_ref, g_ref, t_ref, s_ref, dx_ref, dg_ref, acc_ref):
        i = pl.program_id(0)
        xv = x_ref[...]
        r = lax.rsqrt(jnp.mean(xv * xv, axis=-1, keepdims=True) + EPS)
        xh = xv * r
        gv = g_ref[...]
        dyv = (xh * gv - t_ref[...]) * (s_ref[...] * (1.0 / d))
        dxh = dyv * gv
        dx_ref[...] = r * (dxh - xh * jnp.mean(dxh * xh, axis=-1, keepdims=True))
        part = jnp.sum((dyv * xh).reshape(tq // 8, 8, d), axis=0)

        @pl.when(i == 0)
        def _():
            acc_ref[...] = part

        @pl.when(i > 0)
        def _():
            acc_ref[...] += part

        @pl.when(i == nt - 1)
        def _():
            dg_ref[...] = jnp.sum(acc_ref[...], axis=0, keepdims=True)

    return pl.pallas_call(
        body, name="loss_bwd", grid=(nt,),
        in_specs=[pl.BlockSpec((tq, d), lambda i: (i, 0)), pl.BlockSpec((1, d), lambda i: (0, 0)),
                  pl.BlockSpec((tq, d), lambda i: (i, 0)), pl.BlockSpec((1, 1), lambda i: (0, 0))],
        out_specs=[pl.BlockSpec((tq, d), lambda i: (i, 0)), pl.BlockSpec((1, d), lambda i: (0, 0))],
        out_shape=[jax.ShapeDtypeStruct((t, d), F32), jax.ShapeDtypeStruct((1, d), F32)],
        scratch_shapes=[pltpu.VMEM((8, d), F32)],
        compiler_params=_cparams(("arbitrary",)),
    )(x, g, tgt, scale)


@jax.custom_vjp
def norm_loss(x, g, tgt):
    return _loss_fwd_call(x, g, tgt)


def _norm_loss_fwd(x, g, tgt):
    return _loss_fwd_call(x, g, tgt), (x, g, tgt)


def _norm_loss_bwd(saved, dl):
    x, g, tgt = saved
    dx, dg = _loss_bwd_call(x, g, tgt, jnp.reshape(dl, (1, 1)).astype(F32))
    return dx, dg, jnp.zeros_like(tgt)


norm_loss.defvjp(_norm_loss_fwd, _norm_loss_bwd)


def rope_tables(positions):
    inv_freq = jnp.power(ROPE_THETA, -jnp.arange(0, ROT_DIM, 2, dtype=F32) / ROT_DIM)
    ang = positions.astype(F32)[:, None] * inv_freq
    cos, sin = jnp.cos(ang), jnp.sin(ang)
    t = positions.shape[0]
    one = jnp.ones((t, HEAD_DIM - ROT_DIM), F32)
    zero8 = jnp.zeros((t, ROT_DIM // 2), F32)
    zero = jnp.zeros((t, HEAD_DIM - ROT_DIM), F32)
    a = jnp.concatenate([cos, cos, one], axis=1)
    b = jnp.concatenate([zero8, sin, zero], axis=1)
    c = jnp.concatenate([-sin, zero8, zero], axis=1)
    return tuple(jnp.concatenate([m, m], axis=1) for m in (a, b, c))


def _rope_call(x, tabs, *, transpose, name, tq=512):
    t, w = x.shape
    half = ROT_DIM // 2

    def body(x_ref, a_ref, b_ref, c_ref, o_ref):
        a, b, c = a_ref[...], b_ref[...], c_ref[...]
        for j in range(w // LANES):
            xs = x_ref[:, j * LANES:(j + 1) * LANES]
            if transpose:
                out = (xs * a + pltpu.roll(xs * b, LANES - half, 1) + pltpu.roll(xs * c, half, 1))
            else:
                out = (xs * a + pltpu.roll(xs, half, 1) * b + pltpu.roll(xs, LANES - half, 1) * c)
            o_ref[:, j * LANES:(j + 1) * LANES] = out

    tab_spec = pl.BlockSpec((tq, LANES), lambda i: (i, 0))
    return pl.pallas_call(
        body, name=name, grid=(t // tq,),
        in_specs=[pl.BlockSpec((tq, w), lambda i: (i, 0)), tab_spec, tab_spec, tab_spec],
        out_specs=pl.BlockSpec((tq, w), lambda i: (i, 0)),
        out_shape=jax.ShapeDtypeStruct((t, w), F32),
        compiler_params=_cparams(("parallel",)),
    )(x, *tabs)


@jax.custom_vjp
def rope(x, tabs):
    return _rope_call(x, tabs, transpose=False, name="rope_fwd")


def _rope_fwd(x, tabs):
    return _rope_call(x, tabs, transpose=False, name="rope_fwd"), tabs


def _rope_bwd(tabs, dy):
    return _rope_call(dy, tabs, transpose=True, name="rope_bwd"), tuple(jnp.zeros_like(m) for m in tabs)


rope.defvjp(_rope_fwd, _rope_bwd)


def _band_masks(first_block, max_dist):
    qi = lax.broadcasted_iota(jnp.int32, (BLK, BLK), 0)
    kj = lax.broadcasted_iota(jnp.int32, (BLK, BLK), 1)
    valid_prev = jnp.logical_and(kj >= qi + (BLK - max_dist), jnp.logical_not(first_block))
    valid_cur = kj <= qi
    return valid_prev, valid_cur


_NT = (((1,), (1,)), ((), ()))
_TN = (((0,), (0,)), ((), ()))


HEAD_STAGE = 8


def _attn_index_maps(nb):
    def cur(r, i):
        return jnp.minimum(i, nb - 1), r

    def prev(r, i):
        return jnp.maximum(jnp.minimum(i, nb - 1) - 1, 0), r

    return cur, prev


def _attn_fwd_call(q2, k2, v2, sink, *, dil, group, max_dist, name):
    l = q2.shape[0]
    qw, kw = q2.shape[1] // dil, k2.shape[1] // dil
    nh = qw // HEAD_DIM
    nb = l // BLK
    scale = HEAD_DIM ** -0.5
    use_sink = sink is not None

    def body(*refs):
        if use_sink:
            sink_ref, refs = refs[0], refs[1:]
        q_ref, kp_ref, kc_ref, vp_ref, vc_ref, o_ref, lse_ref = refs
        valid_prev, valid_cur = _band_masks(pl.program_id(1) == 0, max_dist)
        lane = lax.broadcasted_iota(jnp.int32, (BLK, LANES), 1)
        lse_tile = jnp.zeros((BLK, LANES), F32)

        def dot(a, b, dims=_NN):
            return lax.dot_general(a, b, dims, preferred_element_type=F32)

        for g0 in range(0, nh, HEAD_STAGE):
            heads = list(range(g0, min(g0 + HEAD_STAGE, nh)))
            kv = {}
            for kh in sorted({h // group for h in heads}):
                ks = slice(kh * HEAD_DIM, (kh + 1) * HEAD_DIM)
                kv[kh] = tuple(ref[:, ks].astype(BF16) for ref in (kp_ref, kc_ref, vp_ref, vc_ref))
            qs = [q_ref[:, h * HEAD_DIM:(h + 1) * HEAD_DIM].astype(BF16) for h in heads]
            sps = [jnp.where(valid_prev, dot(qh, kv[h // group][0], _NT) * scale, NEG_INF) for h, qh in zip(heads, qs)]
            scs = [jnp.where(valid_cur, dot(qh, kv[h // group][1], _NT) * scale, NEG_INF) for h, qh in zip(heads, qs)]
            ms = [jnp.maximum(jnp.max(sp, axis=1, keepdims=True), jnp.max(sc, axis=1, keepdims=True))
                  for sp, sc in zip(sps, scs)]
            if use_sink:
                ms = [jnp.maximum(m, sink_ref[h]) for h, m in zip(heads, ms)]
            pps = [jnp.exp(sp - m) for sp, m in zip(sps, ms)]
            pcs = [jnp.exp(sc - m) for sc, m in zip(scs, ms)]
            dens = [jnp.sum(pp, axis=1, keepdims=True) + jnp.sum(pc, axis=1, keepdims=True)
                    for pp, pc in zip(pps, pcs)]
            if use_sink:
                dens = [den + jnp.exp(sink_ref[h] - m) for h, den, m in zip(heads, dens, ms)]
            outs = [dot(pp.astype(BF16), kv[h // group][2]) + dot(pc.astype(BF16), kv[h // group][3])
                    for h, pp, pc in zip(heads, pps, pcs)]
            for h, o, den, m in zip(heads, outs, dens, ms):
                o_ref[:, h * HEAD_DIM:(h + 1) * HEAD_DIM] = o / den
                lse_tile = jnp.where(lane == h, m + jnp.log(den), lse_tile)
        lse_ref[...] = lse_tile

    cur, prev = _attn_index_maps(nb)
    q_spec = pl.BlockSpec((BLK, qw), cur)
    kp_spec, kc_spec = pl.BlockSpec((BLK, kw), prev), pl.BlockSpec((BLK, kw), cur)
    lse_spec = pl.BlockSpec((BLK, LANES), cur)
    in_specs = [q_spec, kp_spec, kc_spec, kp_spec, kc_spec]
    args = [q2, k2, k2, v2, v2]
    if use_sink:
        in_specs = [pl.BlockSpec(memory_space=pltpu.SMEM)] + in_specs
        args = [sink] + args
    return pl.pallas_call(
        body, name=name, grid=(dil, nb),
        in_specs=in_specs,
        out_specs=[q_spec, lse_spec],
        out_shape=[jax.ShapeDtypeStruct((l, dil * qw), F32), jax.ShapeDtypeStruct((l, dil * LANES), F32)],
        compiler_params=_cparams(("parallel", "parallel")),
    )(*args)


def _attn_bwd_call(q2, k2, v2, sink, o2, lse2, do2, dlse2, *, dil, group, max_dist, name):
    l = q2.shape[0]
    qw, kw = q2.shape[1] // dil, k2.shape[1] // dil
    nh = qw // HEAD_DIM
    nb = l // BLK
    scale = HEAD_DIM ** -0.5
    use_sink = sink is not None

    def body(*refs):
        if use_sink:
            sink_ref, refs = refs[0], refs[1:]
        (q_ref, kp_ref, kc_ref, vp_ref, vc_ref, o_ref, lse_ref, do_ref, dlse_ref,
         dq_ref, dk_ref, dv_ref, dsink_ref, ck_ref, cv_ref) = refs
        step = pl.program_id(1)

        @pl.when(jnp.logical_and(pl.program_id(0) == 0, step == 0))
        def _():
            dsink_ref[...] = jnp.zeros_like(dsink_ref)

        @pl.when(step == 0)
        def _():
            ck_ref[...] = jnp.zeros_like(ck_ref)
            cv_ref[...] = jnp.zeros_like(cv_ref)

        def dot(a, b, dims=_NN):
            return lax.dot_general(a, b, dims, preferred_element_type=F32)

        @pl.when(step < nb)
        def _():
            valid_prev, valid_cur = _band_masks(step == 0, max_dist)
            row = lax.broadcasted_iota(jnp.int32, (8, LANES), 0)
            lanes8 = lax.broadcasted_iota(jnp.int32, (8, LANES), 1)
            ds_tile = jnp.zeros((8, LANES), F32)
            for g0 in range(0, nh, HEAD_STAGE):
                heads = list(range(g0, min(g0 + HEAD_STAGE, nh)))
                hss = [slice(h * HEAD_DIM, (h + 1) * HEAD_DIM) for h in heads]
                kv = {}
                for kh in sorted({h // group for h in heads}):
                    ks = slice(kh * HEAD_DIM, (kh + 1) * HEAD_DIM)
                    kv[kh] = tuple(ref[:, ks].astype(BF16) for ref in (kp_ref, kc_ref, vp_ref, vc_ref))
                qs = [q_ref[:, hs].astype(BF16) for hs in hss]
                dos = [do_ref[:, hs] for hs in hss]
                dobs = [d.astype(BF16) for d in dos]
                lses = [lse_ref[:, h:h + 1] for h in heads]
                sps = [dot(qh, kv[h // group][0], _NT) * scale for h, qh in zip(heads, qs)]
                scs = [dot(qh, kv[h // group][1], _NT) * scale for h, qh in zip(heads, qs)]
                dpps = [dot(dob, kv[h // group][2], _NT) for h, dob in zip(heads, dobs)]
                dpcs = [dot(dob, kv[h // group][3], _NT) for h, dob in zip(heads, dobs)]
                pps = [jnp.where(valid_prev, jnp.exp(jnp.where(valid_prev, sp, NEG_INF) - ls), 0.0)
                       for sp, ls in zip(sps, lses)]
                pcs = [jnp.where(valid_cur, jnp.exp(jnp.where(valid_cur, sc, NEG_INF) - ls), 0.0)
                       for sc, ls in zip(scs, lses)]
                deltas = [jnp.sum(d * o_ref[:, hs], axis=1, keepdims=True) for d, hs in zip(dos, hss)]
                corrs = [dlse_ref[:, h:h + 1] - dl for h, dl in zip(heads, deltas)]
                dsps = [(pp * (dp + c)).astype(BF16) for pp, dp, c in zip(pps, dpps, corrs)]
                dscs = [(pc * (dp + c)).astype(BF16) for pc, dp, c in zip(pcs, dpcs, corrs)]
                for h, hs, dsp, dsc in zip(heads, hss, dsps, dscs):
                    dq_ref[:, hs] = (dot(dsp, kv[h // group][0]) + dot(dsc, kv[h // group][1])) * scale
                parts = [(dot(dsc, qh, _TN) * scale, dot(dsp, qh, _TN) * scale,
                          dot(pc.astype(BF16), dob, _TN), dot(pp.astype(BF16), dob, _TN))
                         for dsc, dsp, qh, pc, pp, dob in zip(dscs, dsps, qs, pcs, pps, dobs)]
                for kh in kv:
                    ks = slice(kh * HEAD_DIM, (kh + 1) * HEAD_DIM)
                    mine = [p for h, p in zip(heads, parts) if h // group == kh]
                    dkc, dkp, dvc, dvp = (sum(p[j] for p in mine[1:]) + mine[0][j] for j in range(4))
                    dk_ref[:, ks] = ck_ref[:, ks] + dkp
                    dv_ref[:, ks] = cv_ref[:, ks] + dvp
                    ck_ref[:, ks] = dkc
                    cv_ref[:, ks] = dvc
                if use_sink:
                    for h, ls, dl in zip(heads, lses, deltas):
                        val = -jnp.sum(jnp.exp(sink_ref[h] - ls) * dl, axis=0, keepdims=True)
                        ds_tile = jnp.where(jnp.logical_and(row == 0, lanes8 == h), val, ds_tile)
            if use_sink:
                dsink_ref[...] += ds_tile

        @pl.when(step == nb)
        def _():
            dk_ref[...] = ck_ref[...]
            dv_ref[...] = cv_ref[...]

    cur, prev = _attn_index_maps(nb)
    q_spec = pl.BlockSpec((BLK, qw), cur)
    kp_spec, kc_spec = pl.BlockSpec((BLK, kw), prev), pl.BlockSpec((BLK, kw), cur)
    lse_spec = pl.BlockSpec((BLK, LANES), cur)
    lag_spec = pl.BlockSpec((BLK, kw), lambda r, i: (jnp.maximum(i - 1, 0), r))
    in_specs = [q_spec, kp_spec, kc_spec, kp_spec, kc_spec, q_spec, lse_spec, q_spec, lse_spec]
    args = [q2, k2, k2, v2, v2, o2, lse2, do2, dlse2]
    if use_sink:
        in_specs = [pl.BlockSpec(memory_space=pltpu.SMEM)] + in_specs
        args = [sink] + args
    kv_shape = jax.ShapeDtypeStruct((l, dil * kw), F32)
    return pl.pallas_call(
        body, name=name, grid=(dil, nb + 1),
        in_specs=in_specs,
        out_specs=[q_spec, lag_spec, lag_spec, pl.BlockSpec((8, LANES), lambda r, i: (0, 0))],
        out_shape=[jax.ShapeDtypeStruct((l, dil * qw), F32), kv_shape, kv_shape,
                   jax.ShapeDtypeStruct((8, LANES), F32)],
        scratch_shapes=[pltpu.VMEM((BLK, kw), F32), pltpu.VMEM((BLK, kw), F32)],
        compiler_params=_cparams(("arbitrary", "arbitrary")),
    )(*args)


def _make_attention(dil, group, max_dist, use_sink, tag):
    kwargs = dict(dil=dil, group=group, max_dist=max_dist)

    def dilated(a):
        return a.reshape(a.shape[0] // dil, dil * a.shape[1])

    def run(q, k, v, sink):
        q2, k2, v2 = dilated(q), dilated(k), dilated(v)
        o2, lse2 = _attn_fwd_call(q2, k2, v2, sink if use_sink else None, name=tag + "_fwd", **kwargs)
        return (o2.reshape(q.shape), lse2.reshape(q.shape[0], LANES)), (q2, k2, v2, sink, o2, lse2)

    @jax.custom_vjp
    def attn(q, k, v, sink):
        return run(q, k, v, sink)[0]

    def bwd(saved, cts):
        q2, k2, v2, sink, o2, lse2 = saved
        do, dlse = cts
        t = do.shape[0]
        dq2, dk2, dv2, dsink = _attn_bwd_call(
            q2, k2, v2, sink if use_sink else None, o2, lse2, dilated(do), dilated(dlse),
            name=tag + "_bwd", **kwargs)
        kw = k2.shape[1] // dil
        return dq2.reshape(do.shape), dk2.reshape(t, kw), dv2.reshape(t, kw), dsink[0, :sink.shape[0]]

    attn.defvjp(run, bwd)
    return attn


ATTN_A = tuple(_make_attention(dil, 1, window // dil, False, "attn_a%d" % dil) for window, dil in A_CONFIGS)
ATTN_B = _make_attention(1, B_GROUP, BLK - 1, True, "attn_b")


def _head_expand():
    r = lax.broadcasted_iota(jnp.int32, (LANES, 8 * HEAD_DIM), 0)
    c = lax.broadcasted_iota(jnp.int32, (LANES, 8 * HEAD_DIM), 1)
    return (c // HEAD_DIM == r).astype(F32)


def _combine_weights(l0, l1, l2):
    m = jnp.maximum(jnp.maximum(l0, l1), l2)
    e0, e1, e2 = jnp.exp(l0 - m), jnp.exp(l1 - m), jnp.exp(l2 - m)
    inv = 1.0 / (e0 + e1 + e2)
    return e0 * inv, e1 * inv, e2 * inv


def _combine_fwd_call(os_, lses, *, tq=256):
    t, w = os_[0].shape

    def body(o0, o1, o2, l0, l1, l2, y_ref):
        ws = _combine_weights(l0[...], l1[...], l2[...])
        e = _head_expand()
        y = jnp.zeros((tq, w), F32)
        for o_ref, wt in zip((o0, o1, o2), ws):
            y = y + _dot_mask(e, wt, mask_left=False) * o_ref[...]
        y_ref[...] = y

    o_spec = pl.BlockSpec((tq, w), lambda i: (i, 0))
    l_spec = pl.BlockSpec((tq, LANES), lambda i: (i, 0))
    return pl.pallas_call(
        body, name="combine_fwd", grid=(t // tq,),
        in_specs=[o_spec] * 3 + [l_spec] * 3, out_specs=o_spec,
        out_shape=jax.ShapeDtypeStruct((t, w), F32),
        compiler_params=_cparams(("parallel",)),
    )(*os_, *lses)


def _combine_bwd_call(os_, lses, dy, *, tq=256):
    t, w = dy.shape

    def body(o0, o1, o2, l0, l1, l2, dy_ref, do0, do1, do2, dl0, dl1, dl2):
        ws = _combine_weights(l0[...], l1[...], l2[...])
        e = _head_expand()
        dyv = dy_ref[...]
        dws = []
        for o_ref, do_ref, wt in zip((o0, o1, o2), (do0, do1, do2), ws):
            do_ref[...] = _dot_mask(e, wt, mask_left=False) * dyv
            dws.append(_dot_mask(e, dyv * o_ref[...], _NT, mask_left=False))
        mean = ws[0] * dws[0] + ws[1] * dws[1] + ws[2] * dws[2]
        for dl_ref, wt, dw in zip((dl0, dl1, dl2), ws, dws):
            dl_ref[...] = wt * (dw - mean)

    o_spec = pl.BlockSpec((tq, w), lambda i: (i, 0))
    l_spec = pl.BlockSpec((tq, LANES), lambda i: (i, 0))
    o_shape = jax.ShapeDtypeStruct((t, w), F32)
    l_shape = jax.ShapeDtypeStruct((t, LANES), F32)
    return pl.pallas_call(
        body, name="combine_bwd", grid=(t // tq,),
        in_specs=[o_spec] * 3 + [l_spec] * 3 + [o_spec], out_specs=[o_spec] * 3 + [l_spec] * 3,
        out_shape=[o_shape] * 3 + [l_shape] * 3,
        compiler_params=_cparams(("parallel",)),
    )(*os_, *lses, dy)


@jax.custom_vjp
def combine(os_, lses):
    return _combine_fwd_call(os_, lses)


def _combine_fwd(os_, lses):
    return _combine_fwd_call(os_, lses), (os_, lses)


def _combine_bwd(saved, dy):
    os_, lses = saved
    outs = _combine_bwd_call(os_, lses, dy)
    return tuple(outs[:3]), tuple(outs[3:])


combine.defvjp(_combine_fwd, _combine_bwd)


C_QKW = C_QK_HEADS * C_DK
C_CONV_W = 2 * C_QKW + C_V_HEADS * C_DK
HALO = 8


def _silu_parts(z):
    sig = jax.nn.sigmoid(z)
    return z * sig, sig * (1.0 + z * (1.0 - sig))


def _conv_prep_fwd_call(x, w, *, tq=256):
    t, c = x.shape
    nqk = 2 * C_QK_HEADS

    def body(x_ref, halo_ref, w_ref, z_ref, qk_ref, v_ref):
        i = pl.program_id(0)
        halo = jnp.where(i == 0, 0.0, halo_ref[...])
        xc = jnp.concatenate([halo, x_ref[...]], axis=0)
        wv = w_ref[...]
        z = xc[HALO - 3:HALO - 3 + tq] * wv[0:1]
        for j in range(1, C_CONV):
            z = z + xc[HALO - 3 + j:HALO - 3 + j + tq] * wv[j:j + 1]
        z_ref[...] = z
        act, _ = _silu_parts(z)
        for h in range(nqk):
            a = act[:, h * C_DK:(h + 1) * C_DK]
            qk_ref[:, h * C_DK:(h + 1) * C_DK] = a * lax.rsqrt(jnp.sum(a * a, axis=1, keepdims=True) + EPS)
        v_ref[...] = act[:, nqk * C_DK:]

    return pl.pallas_call(
        body, name="conv_prep_fwd", grid=(t // tq,),
        in_specs=[pl.BlockSpec((tq, c), lambda i: (i, 0)),
                  pl.BlockSpec((HALO, c), lambda i: (jnp.maximum(i * (tq // HALO) - 1, 0), 0)),
                  pl.BlockSpec((C_CONV, c), lambda i: (0, 0))],
        out_specs=[pl.BlockSpec((tq, c), lambda i: (i, 0)),
                   pl.BlockSpec((tq, 2 * C_QKW), lambda i: (i, 0)),
                   pl.BlockSpec((tq, c - 2 * C_QKW), lambda i: (i, 0))],
        out_shape=[jax.ShapeDtypeStruct((t, c), F32), jax.ShapeDtypeStruct((t, 2 * C_QKW), F32),
                   jax.ShapeDtypeStruct((t, c - 2 * C_QKW), F32)],
        compiler_params=_cparams(("parallel",)),
    )(x, x, w)


def _conv_prep_dz_call(z, dqk, dv, *, tq=256):
    t, c = z.shape
    nqk = 2 * C_QK_HEADS

    def body(z_ref, dqk_ref, dv_ref, dz_ref):
        zv = z_ref[...]
        act, dact = _silu_parts(zv)
        for h in range(nqk):
            hs = slice(h * C_DK, (h + 1) * C_DK)
            a = act[:, hs]
            r = lax.rsqrt(jnp.sum(a * a, axis=1, keepdims=True) + EPS)
            nrm = a * r
            dn = dqk_ref[:, hs]
            da = r * (dn - nrm * jnp.sum(dn * nrm, axis=1, keepdims=True))
            dz_ref[:, hs] = da * dact[:, hs]
        dz_ref[:, nqk * C_DK:] = dv_ref[...] * dact[:, nqk * C_DK:]

    return pl.pallas_call(
        body, name="conv_prep_dz", grid=(t // tq,),
        in_specs=[pl.BlockSpec((tq, c), lambda i: (i, 0)),
                  pl.BlockSpec((tq, 2 * C_QKW), lambda i: (i, 0)),
                  pl.BlockSpec((tq, c - 2 * C_QKW), lambda i: (i, 0))],
        out_specs=pl.BlockSpec((tq, c), lambda i: (i, 0)),
        out_shape=jax.ShapeDtypeStruct((t, c), F32),
        compiler_params=_cparams(("parallel",)),
    )(z, dqk, dv)


def _conv_bwd_call(x, dz, w, *, tq=256):
    t, c = x.shape
    nt = t // tq

    def body(x_ref, xh_ref, dz_ref, dzh_ref, w_ref, dx_ref, dw_ref):
        i = pl.program_id(0)
        xc = jnp.concatenate([jnp.where(i == 0, 0.0, xh_ref[...]), x_ref[...]], axis=0)
        dzv = dz_ref[...]
        dzc = jnp.concatenate([dzv, jnp.where(i == nt - 1, 0.0, dzh_ref[...])], axis=0)
        wv = w_ref[...]
        dx = dzv * wv[3:4]
        for s in range(1, C_CONV):
            dx = dx + dzc[s:s + tq] * wv[3 - s:4 - s]
        dx_ref[...] = dx
        row = lax.broadcasted_iota(jnp.int32, (8, c), 0)
        dw = jnp.zeros((8, c), F32)
        for j in range(C_CONV):
            prod = dzv * xc[HALO - 3 + j:HALO - 3 + j + tq]
            col = jnp.sum(jnp.sum(prod.reshape(tq // 8, 8, c), axis=0), axis=0, keepdims=True)
            dw = jnp.where(row == j, col, dw)

        @pl.when(i == 0)
        def _():
            dw_ref[...] = dw

        @pl.when(i > 0)
        def _():
            dw_ref[...] += dw

    blk = pl.BlockSpec((tq, c), lambda i: (i, 0))
    return pl.pallas_call(
        body, name="conv_bwd", grid=(nt,),
        in_specs=[blk, pl.BlockSpec((HALO, c), lambda i: (jnp.maximum(i * (tq // HALO) - 1, 0), 0)),
                  blk, pl.BlockSpec((HALO, c), lambda i: (jnp.minimum((i + 1) * (tq // HALO), t // HALO - 1), 0)),
                  pl.BlockSpec((C_CONV, c), lambda i: (0, 0))],
        out_specs=[blk, pl.BlockSpec((8, c), lambda i: (0, 0))],
        out_shape=[jax.ShapeDtypeStruct((t, c), F32), jax.ShapeDtypeStruct((8, c), F32)],
        compiler_params=_cparams(("arbitrary",)),
    )(x, x, dz, dz, w)


@jax.custom_vjp
def conv_prep(x, w):
    _, qk, v = _conv_prep_fwd_call(x, w)
    return qk, v


def _conv_prep_fwd(x, w):
    z, qk, v = _conv_prep_fwd_call(x, w)
    return (qk, v), (x, w, z)


def _conv_prep_bwd(saved, cts):
    x, w, z = saved
    dz = _conv_prep_dz_call(z, cts[0], cts[1])
    dx, dw = _conv_bwd_call(x, dz, w)
    return dx, dw[:C_CONV]


conv_prep.defvjp(_conv_prep_fwd, _conv_prep_bwd)


C_VW = C_V_HEADS * C_DK


def _softplus(x):
    return jnp.maximum(x, 0.0) + jnp.log(1.0 + jnp.exp(-jnp.abs(x)))


def _tri_masks():
    r = lax.broadcasted_iota(jnp.int32, (CHUNK, CHUNK), 0)
    c = lax.broadcasted_iota(jnp.int32, (CHUNK, CHUNK), 1)
    return r >= c, r > c


_NN = (((1,), (0,)), ((), ()))


def _split_bf16(a):
    hi = a.astype(BF16)
    return hi, (a - hi.astype(F32)).astype(BF16)


def _dot_hi(a, b, dims=None):
    dims = _NN if dims is None else dims
    ah, al = _split_bf16(a)
    bh, bl = _split_bf16(b)

    def d(x, y):
        return lax.dot_general(x, y, dims, preferred_element_type=F32)

    return d(ah, bh) + (d(ah, bl) + d(al, bh))


def _dot_mask(mask, b, dims=None, mask_left=True):
    dims = _NN if dims is None else dims
    mb = mask.astype(BF16)
    b1 = b.astype(BF16)
    rest = b - b1.astype(F32)
    b2 = rest.astype(BF16)
    b3 = (rest - b2.astype(F32)).astype(BF16)
    out = None
    for p in (b1, b2, b3):
        term = (lax.dot_general(mb, p, dims, preferred_element_type=F32) if mask_left
                else lax.dot_general(p, mb, dims, preferred_element_type=F32))
        out = term if out is None else out + term
    return out


def _unit_lower_inverses(mats):
    r = lax.broadcasted_iota(jnp.int32, (CHUNK, CHUNK), 0)
    c = lax.broadcasted_iota(jnp.int32, (CHUNK, CHUNK), 1)
    eye = (r == c).astype(F32)
    xs = [eye - a for a in mats]
    ps = [_dot_hi(a, a) for a in mats]
    steps = int(math.log2(CHUNK)) - 1
    for s in range(steps):
        xs = [x + _dot_hi(x, p) for x, p in zip(xs, ps)]
        if s < steps - 1:
            ps = [_dot_hi(p, p) for p in ps]
    return xs


def _gate_tiles(cab, alog, dtb):
    pre = cab + dtb
    g = -jnp.exp(alog) * _softplus(pre)
    beta = jax.nn.sigmoid(pltpu.roll(cab, LANES - C_V_HEADS, 1))
    return g, beta, pre


def _chunk_common(kk, qk, gc, gct, beta, h, tri, strict):
    gcol, grow, bcol = gc[:, h:h + 1], gct[h:h + 1, :], beta[:, h:h + 1]
    decay = jnp.where(tri, jnp.exp(jnp.where(tri, gcol - grow, 0.0)), 0.0)
    kkd = jnp.where(strict, kk * decay, 0.0)
    attn = jnp.where(tri, qk * decay, 0.0)
    glast = gc[CHUNK - 1:CHUNK, h:h + 1]
    return gcol, bcol, decay, kkd, attn, glast


def _delta_prep_call(qk, v, cab, alog, dtb):
    t = qk.shape[0]
    nc = t // CHUNK
    scale = C_DK ** -0.5

    def body(q_ref, k_ref, v_ref, cab_ref, alog_ref, dtb_ref,
             u_ref, w_ref, qd_ref, kd_ref, attn_ref, tmat_ref, gc_ref, beta_ref):
        tri, strict = _tri_masks()
        g, beta, _ = _gate_tiles(cab_ref[...], alog_ref[...], dtb_ref[...])
        gc = _dot_mask(tri, g)
        gct = gc.T
        gc_ref[...] = gc
        beta_ref[...] = beta
        mats, rhs = [], []
        for j in range(C_QK_HEADS):
            js = slice(j * C_DK, (j + 1) * C_DK)
            kf, qf = k_ref[:, js], q_ref[:, js] * scale
            kb, qb = kf.astype(BF16), qf.astype(BF16)
            kk = lax.dot_general(kb, kb, _NT, preferred_element_type=F32)
            qk = lax.dot_general(qb, kb, _NT, preferred_element_type=F32)
            for h in (2 * j, 2 * j + 1):
                hs = slice(h * C_DK, (h + 1) * C_DK)
                gcol, bcol, decay, kkd, attn, glast = _chunk_common(kk, qk, gc, gct, beta, h, tri, strict)
                gexp = jnp.exp(gcol)
                mats.append(kkd * bcol)
                rhs.append(jnp.concatenate([v_ref[:, hs] * bcol, kf * (bcol * gexp)], axis=1))
                qd_ref[:, hs] = qf * gexp
                kd_ref[:, hs] = kf * jnp.exp(glast - gcol)
                attn_ref[:, h * CHUNK:(h + 1) * CHUNK] = attn
        for h, (tmat, r) in enumerate(zip(_unit_lower_inverses(mats), rhs)):
            hs = slice(h * C_DK, (h + 1) * C_DK)
            uw = _dot_hi(tmat, r)
            u_ref[:, hs] = uw[:, :C_DK]
            w_ref[:, hs] = uw[:, C_DK:]
            tmat_ref[:, h * CHUNK:(h + 1) * CHUNK] = tmat

    def blk(w):
        return pl.BlockSpec((CHUNK, w), lambda n: (n, 0))

    row = pl.BlockSpec((1, LANES), lambda n: (0, 0))
    big = jax.ShapeDtypeStruct((t, C_VW), F32)
    sq = jax.ShapeDtypeStruct((t, C_V_HEADS * CHUNK), F32)
    tile = jax.ShapeDtypeStruct((t, LANES), F32)
    return pl.pallas_call(
        body, name="delta_prep", grid=(nc,),
        in_specs=[blk(C_QKW), pl.BlockSpec((CHUNK, C_QKW), lambda n: (n, 1)), blk(C_VW), blk(LANES), row, row],
        out_specs=[blk(C_VW)] * 4 + [blk(C_V_HEADS * CHUNK)] * 2 + [blk(LANES)] * 2,
        out_shape=[big] * 4 + [sq] * 2 + [tile] * 2,
        compiler_params=_cparams(("parallel",)),
    )(qk, qk, v, cab, alog, dtb)


def _delta_scan_call(u, w, qd, kd, attn, gc):
    t = u.shape[0]
    nc = t // CHUNK

    def body(u_ref, w_ref, qd_ref, kd_ref, attn_ref, gc_ref, o_ref, vn_ref, st_ref, s_ref):
        @pl.when(pl.program_id(0) == 0)
        def _():
            s_ref[...] = jnp.zeros_like(s_ref)

        hss = [slice(h * C_DK, (h + 1) * C_DK) for h in range(C_V_HEADS)]
        states = [s_ref[hs, :] for hs in hss]
        for hs, s in zip(hss, states):
            st_ref[0, hs, :] = s
        sbs = [s.astype(BF16) for s in states]
        vns = [u_ref[:, hs] - jnp.dot(w_ref[:, hs].astype(BF16), sb, preferred_element_type=F32)
               for hs, sb in zip(hss, sbs)]
        qss = [jnp.dot(qd_ref[:, hs].astype(BF16), sb, preferred_element_type=F32) for hs, sb in zip(hss, sbs)]
        vnbs = [vn.astype(BF16) for vn in vns]
        for h, hs in enumerate(hss):
            vn_ref[:, hs] = vns[h]
            o_ref[:, hs] = qss[h] + jnp.dot(attn_ref[:, h * CHUNK:(h + 1) * CHUNK].astype(BF16), vnbs[h],
                                            preferred_element_type=F32)
        for h, hs in enumerate(hss):
            glast = jnp.exp(gc_ref[CHUNK - 1:CHUNK, h:h + 1])
            s_ref[hs, :] = states[h] * glast + lax.dot_general(kd_ref[:, hs].astype(BF16), vnbs[h], _TN,
                                                               preferred_element_type=F32)

    def blk(wd):
        return pl.BlockSpec((CHUNK, wd), lambda n: (n, 0))

    big = jax.ShapeDtypeStruct((t, C_VW), F32)
    return pl.pallas_call(
        body, name="delta_scan", grid=(nc,),
        in_specs=[blk(C_VW)] * 4 + [blk(C_V_HEADS * CHUNK), blk(LANES)],
        out_specs=[blk(C_VW), blk(C_VW), pl.BlockSpec((1, C_VW, C_DK), lambda n: (n, 0, 0))],
        out_shape=[big, big, jax.ShapeDtypeStruct((nc, C_VW, C_DK), F32)],
        scratch_shapes=[pltpu.VMEM((C_VW, C_DK), F32)],
        compiler_params=_cparams(("arbitrary",)),
    )(u, w, qd, kd, attn, gc)


def _delta_scan_bwd_call(do, w, qd, kd, attn, gc, vn, st):
    t = do.shape[0]
    nc = t // CHUNK

    def body(do_ref, w_ref, qd_ref, kd_ref, attn_ref, gc_ref, vn_ref, st_ref,
             du_ref, dw_ref, dqd_ref, dkd_ref, dattn_ref, dgl_ref, ds_ref):
        @pl.when(pl.program_id(0) == 0)
        def _():
            ds_ref[...] = jnp.zeros_like(ds_ref)

        tri, _ = _tri_masks()
        row = lax.broadcasted_iota(jnp.int32, (8, LANES), 0)
        lane = lax.broadcasted_iota(jnp.int32, (8, LANES), 1)
        dgl = jnp.zeros((8, LANES), F32)
        hss = [slice(h * C_DK, (h + 1) * C_DK) for h in range(C_V_HEADS)]
        css = [slice(h * CHUNK, (h + 1) * CHUNK) for h in range(C_V_HEADS)]

        def dg(a, b, dims):
            return lax.dot_general(a, b, dims, preferred_element_type=F32)

        ss = [st_ref[0, hs, :] for hs in hss]
        dsps = [ds_ref[hs, :] for hs in hss]
        sbs = [s.astype(BF16) for s in ss]
        dspbs = [d.astype(BF16) for d in dsps]
        dobs = [do_ref[:, hs].astype(BF16) for hs in hss]
        vnbs = [vn_ref[:, hs].astype(BF16) for hs in hss]
        dvns = [dg(attn_ref[:, cs].astype(BF16), dob, _TN) + dg(kd_ref[:, hs].astype(BF16), dspb, _NN)
                for hs, cs, dob, dspb in zip(hss, css, dobs, dspbs)]
        for h, hs in enumerate(hss):
            dqd_ref[:, hs] = dg(dobs[h], sbs[h], _NT)
            dkd_ref[:, hs] = dg(vnbs[h], dspbs[h], _NT)
            dattn_ref[:, css[h]] = jnp.where(tri, dg(dobs[h], vnbs[h], _NT), 0.0)
        dvnbs = [d.astype(BF16) for d in dvns]
        for h, hs in enumerate(hss):
            du_ref[:, hs] = dvns[h]
            dw_ref[:, hs] = -dg(dvnbs[h], sbs[h], _NT)
            tot = jnp.sum(jnp.sum(dsps[h] * ss[h], axis=0, keepdims=True), axis=1, keepdims=True)
            dgl = jnp.where(jnp.logical_and(row == 0, lane == h), tot, dgl)
        for h, hs in enumerate(hss):
            glast = jnp.exp(gc_ref[CHUNK - 1:CHUNK, h:h + 1])
            ds_ref[hs, :] = (dg(qd_ref[:, hs].astype(BF16), dobs[h], _TN) + glast * dsps[h]
                             - dg(w_ref[:, hs].astype(BF16), dvnbs[h], _TN))
        dgl_ref[...] = dgl

    def blk(wd):
        return pl.BlockSpec((CHUNK, wd), lambda n: (nc - 1 - n, 0))

    big = jax.ShapeDtypeStruct((t, C_VW), F32)
    return pl.pallas_call(
        body, name="delta_scan_bwd", grid=(nc,),
        in_specs=[blk(C_VW)] * 4 + [blk(C_V_HEADS * CHUNK), blk(LANES), blk(C_VW),
                                    pl.BlockSpec((1, C_VW, C_DK), lambda n: (nc - 1 - n, 0, 0))],
        out_specs=[blk(C_VW)] * 4 + [blk(C_V_HEADS * CHUNK), pl.BlockSpec((8, LANES), lambda n: (nc - 1 - n, 0))],
        out_shape=[big] * 4 + [jax.ShapeDtypeStruct((t, C_V_HEADS * CHUNK), F32),
                               jax.ShapeDtypeStruct((nc * 8, LANES), F32)],
        scratch_shapes=[pltpu.VMEM((C_VW, C_DK), F32)],
        compiler_params=_cparams(("arbitrary",)),
    )(do, w, qd, kd, attn, gc, vn, st)


def _delta_prep_bwd_call(qk, v, cab, alog, dtb, tmat, u, w, gc, beta, du, dw, dqd, dkd, dattn, dgl):
    t = qk.shape[0]
    nc = t // CHUNK
    scale = C_DK ** -0.5

    def body(q_ref, k_ref, v_ref, cab_ref, alog_ref, dtb_ref, tmat_ref, u_ref, w_ref, gc_ref, beta_ref,
             du_ref, dw_ref, dqd_ref, dkd_ref, dattn_ref, dgl_ref,
             dqk_ref, dv_ref, dcab_ref, dpar_ref):
        tri, strict = _tri_masks()
        gc, beta = gc_ref[...], beta_ref[...]
        gct = gc.T
        ones = jnp.ones((CHUNK, LANES), F32)
        lane = lax.broadcasted_iota(jnp.int32, (CHUNK, LANES), 1)
        rowi = lax.broadcasted_iota(jnp.int32, (CHUNK, 1), 0)
        dgc_tile = jnp.zeros((CHUNK, LANES), F32)
        db_tile = jnp.zeros((CHUNK, LANES), F32)
        heads = []
        for j in range(C_QK_HEADS):
            js = slice(j * C_DK, (j + 1) * C_DK)
            kf, qf = k_ref[:, js], q_ref[:, js] * scale
            kb, qb = kf.astype(BF16), qf.astype(BF16)
            kk = lax.dot_general(kb, kb, _NT, preferred_element_type=F32)
            qk = lax.dot_general(qb, kb, _NT, preferred_element_type=F32)
            for h in (2 * j, 2 * j + 1):
                heads.append((h, kf, qf, kb, qb) + _chunk_common(kk, qk, gc, gct, beta, h, tri, strict))
        dvks = [_dot_hi(tmat_ref[:, h * CHUNK:(h + 1) * CHUNK],
                        jnp.concatenate([du_ref[:, h * C_DK:(h + 1) * C_DK], dw_ref[:, h * C_DK:(h + 1) * C_DK]], axis=1),
                        _TN) for h in range(C_V_HEADS)]
        das = [-jnp.where(strict, _dot_hi(dvk, jnp.concatenate([u_ref[:, h * C_DK:(h + 1) * C_DK],
                                                                w_ref[:, h * C_DK:(h + 1) * C_DK]], axis=1), _NT), 0.0)
               for h, dvk in enumerate(dvks)]
        dq_parts, dk_parts = [], []
        for (h, kf, qf, kb, qb, gcol, bcol, decay, kkd, attn, glast), dvk, da in zip(heads, dvks, das):
            hs = slice(h * C_DK, (h + 1) * C_DK)
            gexp = jnp.exp(gcol)
            fdec = jnp.exp(glast - gcol)
            dvb, dkb = dvk[:, :C_DK], dvk[:, C_DK:]
            dattn_h = dattn_ref[:, h * CHUNK:(h + 1) * CHUNK]
            dkk = (da * decay * bcol).astype(BF16)
            dqk = (dattn_h * decay).astype(BF16)
            e = da * kkd * bcol + dattn_h * attn
            dgc = jnp.sum(e, axis=1, keepdims=True) - _dot_mask(ones, e, _TN, mask_left=False)[:, :1]
            dk_parts.append(jnp.dot(dkk, kb, preferred_element_type=F32)
                            + lax.dot_general(dkk, kb, _TN, preferred_element_type=F32)
                            + lax.dot_general(dqk, qb, _TN, preferred_element_type=F32)
                            + dkb * (bcol * gexp) + dkd_ref[:, hs] * fdec)
            dq_parts.append(jnp.dot(dqk, kb, preferred_element_type=F32) + dqd_ref[:, hs] * gexp)
            dv_ref[:, hs] = dvb * bcol
            s_kb = jnp.sum(dkb * kf, axis=1, keepdims=True)
            db = (jnp.sum(da * kkd, axis=1, keepdims=True) + jnp.sum(dvb * v_ref[:, hs], axis=1, keepdims=True)
                  + s_kb * gexp)
            rho = jnp.sum(dkd_ref[:, hs] * kf, axis=1, keepdims=True) * fdec
            dgc = (dgc + s_kb * bcol * gexp + jnp.sum(dqd_ref[:, hs] * qf, axis=1, keepdims=True) * gexp - rho)
            last = jnp.sum(rho, axis=0, keepdims=True) + dgl_ref[0:1, h:h + 1] * jnp.exp(glast)
            dgc = dgc + jnp.where(rowi == CHUNK - 1, last, 0.0)
            dgc_tile = jnp.where(lane == h, dgc, dgc_tile)
            db_tile = jnp.where(lane == h, db, db_tile)
        for j in range(C_QK_HEADS):
            dqk_ref[:, j * C_DK:(j + 1) * C_DK] = (dq_parts[2 * j] + dq_parts[2 * j + 1]) * scale
            dqk_ref[:, C_QKW + j * C_DK:C_QKW + (j + 1) * C_DK] = dk_parts[2 * j] + dk_parts[2 * j + 1]
        dg = _dot_mask(jnp.logical_not(strict), dgc_tile)
        alog = alog_ref[...]
        g, _, pre = _gate_tiles(cab_ref[...], alog, dtb_ref[...])
        dca = dg * (-jnp.exp(alog)) * jax.nn.sigmoid(pre)
        dcab_ref[...] = dca + pltpu.roll(db_tile * beta * (1.0 - beta), C_V_HEADS, 1)
        row8 = lax.broadcasted_iota(jnp.int32, (8, LANES), 0)
        par = jnp.where(row8 == 0, jnp.sum(dg * g, axis=0, keepdims=True),
                        jnp.where(row8 == 1, jnp.sum(dca, axis=0, keepdims=True), 0.0))

        @pl.when(pl.program_id(0) == 0)
        def _():
            dpar_ref[...] = par

        @pl.when(pl.program_id(0) > 0)
        def _():
            dpar_ref[...] += par

    def blk(wd):
        return pl.BlockSpec((CHUNK, wd), lambda n: (n, 0))

    row = pl.BlockSpec((1, LANES), lambda n: (0, 0))
    sq = blk(C_V_HEADS * CHUNK)
    return pl.pallas_call(
        body, name="delta_prep_bwd", grid=(nc,),
        in_specs=[blk(C_QKW), pl.BlockSpec((CHUNK, C_QKW), lambda n: (n, 1)), blk(C_VW), blk(LANES), row, row, sq,
                  blk(C_VW), blk(C_VW),
                  blk(LANES), blk(LANES), blk(C_VW), blk(C_VW), blk(C_VW), blk(C_VW), sq,
                  pl.BlockSpec((8, LANES), lambda n: (n, 0))],
        out_specs=[blk(2 * C_QKW), blk(C_VW), blk(LANES), pl.BlockSpec((8, LANES), lambda n: (0, 0))],
        out_shape=[jax.ShapeDtypeStruct((t, 2 * C_QKW), F32),
                   jax.ShapeDtypeStruct((t, C_VW), F32), jax.ShapeDtypeStruct((t, LANES), F32),
                   jax.ShapeDtypeStruct((8, LANES), F32)],
        compiler_params=_cparams(("arbitrary",)),
    )(qk, qk, v, cab, alog, dtb, tmat, u, w, gc, beta, du, dw, dqd, dkd, dattn, dgl)


@jax.custom_vjp
def delta_rule(qk, v, cab, alog, dtb):
    u, w, qd, kd, attn, _, gc, _ = _delta_prep_call(qk, v, cab, alog, dtb)
    return _delta_scan_call(u, w, qd, kd, attn, gc)[0]


def _delta_rule_fwd(qk, v, cab, alog, dtb):
    u, w, qd, kd, attn, tmat, gc, beta = _delta_prep_call(qk, v, cab, alog, dtb)
    o, vn, st = _delta_scan_call(u, w, qd, kd, attn, gc)
    return o, (qk, v, cab, alog, dtb, u, w, qd, kd, attn, tmat, gc, beta, vn, st)


def _delta_rule_bwd(saved, do):
    qk, v, cab, alog, dtb, u, w, qd, kd, attn, tmat, gc, beta, vn, st = saved
    du, dw, dqd, dkd, dattn, dgl = _delta_scan_bwd_call(do, w, qd, kd, attn, gc, vn, st)
    dqk, dv, dcab, dpar = _delta_prep_bwd_call(qk, v, cab, alog, dtb, tmat, u, w, gc, beta,
                                               du, dw, dqd, dkd, dattn, dgl)
    return dqk, dv, dcab, dpar[0:1], dpar[1:2]


delta_rule.defvjp(_delta_rule_fwd, _delta_rule_bwd)


def _gated_norm_fwd_call(o, z, gain, *, tq=256):
    t, w = o.shape

    def body(o_ref, z_ref, g_ref, y_ref):
        act, _ = _silu_parts(z_ref[...])
        gv = g_ref[...]
        for h in range(C_V_HEADS):
            hs = slice(h * C_DK, (h + 1) * C_DK)
            ov = o_ref[:, hs]
            r = lax.rsqrt(jnp.mean(ov * ov, axis=1, keepdims=True) + EPS)
            y_ref[:, hs] = ov * r * gv * act[:, hs]

    blk = pl.BlockSpec((tq, w), lambda i: (i, 0))
    return pl.pallas_call(
        body, name="gated_norm_fwd", grid=(t // tq,),
        in_specs=[blk, blk, pl.BlockSpec((1, C_DK), lambda i: (0, 0))], out_specs=blk,
        out_shape=jax.ShapeDtypeStruct((t, w), F32),
        compiler_params=_cparams(("parallel",)),
    )(o, z, gain)


def _gated_norm_bwd_call(o, z, gain, dy, *, tq=256):
    t, w = o.shape
    nt = t // tq

    def body(o_ref, z_ref, g_ref, dy_ref, do_ref, dz_ref, dg_ref, acc_ref):
        i = pl.program_id(0)
        act, dact = _silu_parts(z_ref[...])
        gv = g_ref[...]
        part = jnp.zeros((8, C_DK), F32)
        for h in range(C_V_HEADS):
            hs = slice(h * C_DK, (h + 1) * C_DK)
            ov = o_ref[:, hs]
            r = lax.rsqrt(jnp.mean(ov * ov, axis=1, keepdims=True) + EPS)
            xh = ov * r
            dyv = dy_ref[:, hs]
            dn = dyv * act[:, hs]
            dz_ref[:, hs] = dyv * xh * gv * dact[:, hs]
            dxh = dn * gv
            do_ref[:, hs] = r * (dxh - xh * jnp.mean(dxh * xh, axis=1, keepdims=True))
            part = part + jnp.sum((dn * xh).reshape(tq // 8, 8, C_DK), axis=0)

        @pl.when(i == 0)
        def _():
            acc_ref[...] = part

        @pl.when(i > 0)
        def _():
            acc_ref[...] += part

        @pl.when(i == nt - 1)
        def _():
            dg_ref[...] = jnp.sum(acc_ref[...], axis=0, keepdims=True)

    blk = pl.BlockSpec((tq, w), lambda i: (i, 0))
    grow = pl.BlockSpec((1, C_DK), lambda i: (0, 0))
    big = jax.ShapeDtypeStruct((t, w), F32)
    return pl.pallas_call(
        body, name="gated_norm_bwd", grid=(nt,),
        in_specs=[blk, blk, grow, blk], out_specs=[blk, blk, grow],
        out_shape=[big, big, jax.ShapeDtypeStruct((1, C_DK), F32)],
        scratch_shapes=[pltpu.VMEM((8, C_DK), F32)],
        compiler_params=_cparams(("arbitrary",)),
    )(o, z, gain, dy)


@jax.custom_vjp
def gated_norm(o, z, gain):
    return _gated_norm_fwd_call(o, z, gain)


def _gated_norm_fwd(o, z, gain):
    return _gated_norm_fwd_call(o, z, gain), (o, z, gain)


def _gated_norm_bwd(saved, dy):
    return tuple(_gated_norm_bwd_call(*saved, dy))


gated_norm.defvjp(_gated_norm_fwd, _gated_norm_bwd)


def _merge_fwd_call(ps, gs, *, tq=256):
    t, w = ps[0].shape

    def body(p0, p1, p2, g0, g1, g2, y_ref):
        y_ref[...] = (jax.nn.sigmoid(g0[...]) * p0[...] + jax.nn.sigmoid(g1[...]) * p1[...]
                      + jax.nn.sigmoid(g2[...]) * p2[...])

    blk = pl.BlockSpec((tq, w), lambda i: (i, 0))
    return pl.pallas_call(
        body, name="merge_fwd", grid=(t // tq,), in_specs=[blk] * 6, out_specs=blk,
        out_shape=jax.ShapeDtypeStruct((t, w), F32),
        compiler_params=_cparams(("parallel",)),
    )(*ps, *gs)


def _merge_bwd_call(ps, gs, dy, *, tq=256):
    t, w = dy.shape

    def body(p0, p1, p2, g0, g1, g2, dy_ref, dp0, dp1, dp2, dg0, dg1, dg2):
        dyv = dy_ref[...]
        for p, g, dp, dg in ((p0, g0, dp0, dg0), (p1, g1, dp1, dg1), (p2, g2, dp2, dg2)):
            sig = jax.nn.sigmoid(g[...])
            dp[...] = dyv * sig
            dg[...] = dyv * p[...] * sig * (1.0 - sig)

    blk = pl.BlockSpec((tq, w), lambda i: (i, 0))
    big = jax.ShapeDtypeStruct((t, w), F32)
    return pl.pallas_call(
        body, name="merge_bwd", grid=(t // tq,), in_specs=[blk] * 7, out_specs=[blk] * 6,
        out_shape=[big] * 6,
        compiler_params=_cparams(("parallel",)),
    )(*ps, *gs, dy)


@jax.custom_vjp
def gate_merge(ps, gs):
    return _merge_fwd_call(ps, gs)


def _gate_merge_fwd(ps, gs):
    return _merge_fwd_call(ps, gs), (ps, gs)


def _gate_merge_bwd(saved, dy):
    ps, gs = saved
    outs = _merge_bwd_call(ps, gs, dy)
    return tuple(outs[:3]), tuple(outs[3:])


gate_merge.defvjp(_gate_merge_fwd, _gate_merge_bwd)


_SPLITS = tuple((name, start, max(width, LANES)) for name, start, width, _ in IN_LAYOUT)


@jax.custom_vjp
def split_cols(u):
    return tuple(u[:, s:s + w] for _, s, w in _SPLITS)


def _split_cols_fwd(u):
    return split_cols(u), u.shape[0]


def _split_cols_bwd(t, cts):
    used = _SPLITS[-1][1] + _SPLITS[-1][2]
    return (jnp.concatenate(list(cts) + [jnp.zeros((t, D_IN_PAD - used), F32)], axis=1),)


split_cols.defvjp(_split_cols_fwd, _split_cols_bwd)


def _layer(x, tabs, p, wb):
    u = norm_in_proj(x, p["norm_mix"], p["w_in"], wb["w_in"], p["b_in"])
    a_q, a_k, a_v, b_q, c_qkv, c_z, gate_a, gate_b, gate_c, b_k, b_v, c_ab = split_cols(u)
    q, k = rope(a_q, tabs), rope(a_k, tabs)
    no_sink = jnp.zeros((8,), F32)
    outs = [f(q, k, a_v, no_sink) for f in ATTN_A]
    ya = combine(tuple(o for o, _ in outs), tuple(l for _, l in outs))
    yb, _ = ATTN_B(rope(b_q, tabs), rope(b_k, tabs), b_v, p["sinks"])
    qk, v = conv_prep(c_qkv, p["conv_w"])
    o = delta_rule(qk, v, c_ab, p["a_log"], p["dt_bias"])
    yc = gated_norm(o, c_z, p["c_norm"])
    merged = gate_merge(tuple(linear(y, p[n], wb[n]) for y, n in
                              ((ya, "w_branch_a"), (yb, "w_branch_b"), (yc, "w_branch_c"))),
                        (gate_a, gate_b, gate_c))
    x = linear_res(merged, p["w_out"], wb["w_out"], x)
    return norm_ffn(x, p["norm_ffn"], p["w_ff1"], p["w_ff2"], wb["w_ff1"], wb["w_ff2"])


def _local_loss(x, params, wbf, tabs, tgt):
    for layer in range(DEPTH):
        x = _layer(x, tabs, {n: w[layer] for n, w in params.items() if n != "norm_final"},
                   {n: w[layer] for n, w in wbf.items()})
    return norm_loss(x, params["norm_final"], tgt)


def _in_cols_to_kernel(w):
    lead = w.shape[:-1]
    parts, pos = [], 0
    for _, start, width, ref_start in IN_LAYOUT:
        if start > pos:
            parts.append(jnp.zeros(lead + (start - pos,), w.dtype))
        parts.append(w[..., ref_start:ref_start + width])
        pos = start + width
    parts.append(jnp.zeros(lead + (D_IN_PAD - pos,), w.dtype))
    return jnp.concatenate(parts, axis=-1)


def _in_cols_to_reference(w):
    by_ref = sorted(IN_LAYOUT, key=lambda e: e[3])
    return jnp.concatenate([w[..., start:start + width] for _, start, width, _ in by_ref], axis=-1)


def _pad_lanes(v):
    return jnp.pad(v, ((0, 0), (0, LANES - v.shape[1])))[:, None, :]


BIG = (("w_in", 2), ("conv_w", 2), ("w_branch_a", 2), ("w_branch_b", 2), ("w_branch_c", 1), ("w_out", 1),
       ("w_ff1", 2), ("w_ff2", 1))
SMALL = ("norm_mix", "b_in", "a_log", "dt_bias", "sinks", "c_norm", "norm_ffn", "norm_final")
WEIGHTS = ("norm_mix", "w_in", "b_in", "conv_w", "a_log", "dt_bias", "sinks", "c_norm", "w_branch_a",
           "w_branch_b", "w_branch_c", "w_out", "norm_ffn", "w_ff1", "w_ff2", "norm_final")
PACK_ROWS = 1024


ROW_ALIGN = 16


def _seg_rows(n):
    return -(-n // (LANES * ROW_ALIGN)) * ROW_ALIGN


def _pack(arrays, lead=0):
    parts = []
    for a in arrays:
        lead_shape = a.shape[:lead]
        n = math.prod(a.shape[lead:])
        rows = _seg_rows(n)
        if rows * LANES != n:
            a = jnp.pad(a.reshape(lead_shape + (n,)), [(0, 0)] * lead + [(0, rows * LANES - n)])
        parts.append(a.reshape(lead_shape + (rows, LANES)))
    total = sum(p.shape[lead] for p in parts)
    padded = -(-total // PACK_ROWS) * PACK_ROWS
    if padded > total:
        parts.append(jnp.zeros(parts[0].shape[:lead] + (padded - total, LANES), parts[0].dtype))
    return jnp.concatenate(parts, axis=lead)


def _unpack(buf, shapes):
    lead = buf.shape[:-2]
    out, pos = [], 0
    for shp in shapes:
        n = math.prod(shp)
        rows = _seg_rows(n)
        seg = buf[..., pos:pos + rows, :]
        if rows * LANES != n:
            seg = seg.reshape(lead + (rows * LANES,))[..., :n]
        out.append(seg.reshape(lead + tuple(shp)))
        pos += rows
    return out


def _shards_to_full(blocks, axis):
    moved = jnp.moveaxis(blocks, 0, axis)
    shp = list(blocks.shape[1:])
    shp[axis] = shp[axis] * N_DEV
    return moved.reshape(shp)


def _full_to_shards(full, axis):
    shp = list(full.shape)
    shp[axis:axis + 1] = [N_DEV, shp[axis] // N_DEV]
    return jnp.moveaxis(full.reshape(shp), axis, 0)


MESH_ID = pl.DeviceIdType.MESH
HBM_SPEC = pl.BlockSpec(memory_space=pl.ANY)


def _my_place():
    return lax.axis_index("x"), lax.axis_index("y"), lax.axis_index("c")


def _slot(x, y, c):
    return 4 * x + 2 * y + c


def _all_gather(block, *, name):
    rows = block.shape[0]

    def body(x_ref, out_ref, send_sems, recv_sems, local_sem):
        x, y, c = _my_place()
        me, sibling = (x, y, c), (x, y, 1 - c)
        chips = [(1 - x, y), (x, 1 - y), (1 - x, 1 - y)]

        def copy(k, blk, to, src=None):
            dst = out_ref.at[_slot(*blk)]
            return pltpu.make_async_remote_copy(
                src_ref=dst if src is None else src, dst_ref=dst,
                send_sem=send_sems.at[k], recv_sem=recv_sems.at[k], device_id=to, device_id_type=MESH_ID)

        mine = pltpu.make_async_copy(x_ref, out_ref.at[_slot(*me)], local_sem)
        mine.start()
        first = [copy(0, me, sibling, src=x_ref)]
        first += [copy(1 + j, me, (*chip, c), src=x_ref) for j, chip in enumerate(chips)]
        for cp in first:
            cp.start()
        passed = [copy(4 + j, (*chip, c), sibling) for j, chip in enumerate(chips)]
        for j, chip in enumerate(chips):
            copy(1 + j, (*chip, c), me).wait_recv()
            passed[j].start()
        copy(0, sibling, me).wait_recv()
        for j, chip in enumerate(chips):
            copy(4 + j, (*chip, 1 - c), me).wait_recv()
        for cp in first + passed:
            cp.wait_send()
        mine.wait()

    return pl.pallas_call(
        body, name=name,
        out_shape=jax.ShapeDtypeStruct((N_DEV, rows, LANES), block.dtype),
        in_specs=[HBM_SPEC], out_specs=HBM_SPEC,
        scratch_shapes=[pltpu.SemaphoreType.DMA((7,)), pltpu.SemaphoreType.DMA((7,)), pltpu.SemaphoreType.DMA],
    )(block)


N_CHIP = N_DEV // 2


def _swap_with_sibling(blocks, *, name):
    def body(g_ref, out_ref, send_sem, recv_sem):
        x, y, c = _my_place()
        cp = pltpu.make_async_remote_copy(src_ref=g_ref, dst_ref=out_ref, send_sem=send_sem, recv_sem=recv_sem,
                                          device_id=(x, y, 1 - c), device_id_type=MESH_ID)
        cp.start()
        cp.wait_recv()
        cp.wait_send()

    return pl.pallas_call(
        body, name=name,
        out_shape=jax.ShapeDtypeStruct(blocks.shape, blocks.dtype),
        in_specs=[HBM_SPEC], out_specs=HBM_SPEC,
        scratch_shapes=[pltpu.SemaphoreType.DMA, pltpu.SemaphoreType.DMA],
    )(blocks)


def _chip_all_to_all(blocks, *, name):
    def body(g_ref, out_ref, send_sems, recv_sems, local_sem):
        x, y, c = _my_place()
        mine_slot = 2 * x + y
        local = pltpu.make_async_copy(g_ref.at[mine_slot], out_ref.at[mine_slot], local_sem)
        local.start()
        copies = []
        for k in range(1, N_CHIP):
            px, py = x ^ (k >> 1), y ^ (k & 1)
            copies.append(pltpu.make_async_remote_copy(
                src_ref=g_ref.at[2 * px + py], dst_ref=out_ref.at[mine_slot],
                send_sem=send_sems.at[k - 1], recv_sem=recv_sems.at[k - 1],
                device_id=(px, py, c), device_id_type=MESH_ID))
        for cp in copies:
            cp.start()
        for cp in copies:
            cp.wait_recv()
        for cp in copies:
            cp.wait_send()
        local.wait()

    return pl.pallas_call(
        body, name=name,
        out_shape=jax.ShapeDtypeStruct(blocks.shape, blocks.dtype),
        in_specs=[HBM_SPEC], out_specs=HBM_SPEC,
        scratch_shapes=[pltpu.SemaphoreType.DMA((N_CHIP - 1,)), pltpu.SemaphoreType.DMA((N_CHIP - 1,)),
                        pltpu.SemaphoreType.DMA],
    )(blocks)


def _add_bf16_call(a, b, *, name):
    n, rows, _ = a.shape
    tr = min(PACK_ROWS, rows)

    def body(a_ref, b_ref, o_ref):
        o_ref[...] = (a_ref[...].astype(F32) + b_ref[...].astype(F32)).astype(BF16)

    blk = pl.BlockSpec((n, tr, LANES), lambda i: (0, i, 0))
    return pl.pallas_call(
        body, name=name, grid=(rows // tr,), in_specs=[blk, blk], out_specs=blk,
        out_shape=jax.ShapeDtypeStruct(a.shape, BF16),
        compiler_params=_cparams(("parallel",)),
    )(a, b)


def _adamw_call(parts, w, m, v, *, name):
    rows = w.shape[0]
    tr = min(PACK_ROWS, rows)
    assert rows % tr == 0
    n_parts = parts.shape[0]

    def body(p_ref, w_ref, m_ref, v_ref, g_ref, d_ref, nm_ref, nv_ref):
        g = p_ref[0].astype(F32)
        for s in range(1, n_parts):
            g = g + p_ref[s].astype(F32)
        nm = ADAM_B1 * m_ref[...] + (1.0 - ADAM_B1) * g
        nv = ADAM_B2 * v_ref[...] + (1.0 - ADAM_B2) * jnp.square(g)
        m_hat = nm / (1.0 - ADAM_B1 ** ADAM_STEP)
        v_hat = nv / (1.0 - ADAM_B2 ** ADAM_STEP)
        g_ref[...] = g
        nm_ref[...] = nm
        nv_ref[...] = nv
        d_ref[...] = -ADAM_LR * (m_hat / (jnp.sqrt(v_hat) + ADAM_EPS) + ADAM_WD * w_ref[...])

    blk = pl.BlockSpec((tr, LANES), lambda i: (i, 0))
    shape = jax.ShapeDtypeStruct((rows, LANES), F32)
    return pl.pallas_call(
        body, name=name, grid=(rows // tr,),
        in_specs=[pl.BlockSpec((n_parts, tr, LANES), lambda i: (0, i, 0)), blk, blk, blk],
        out_specs=[blk] * 4, out_shape=[shape] * 4,
        compiler_params=_cparams(("parallel",)),
    )(parts, w, m, v)


MATMUL_WEIGHTS = ("w_in", "w_branch_a", "w_branch_b", "w_branch_c", "w_out", "w_ff1", "w_ff2")


def _kernel_params(full, wbf):
    params = {
        "norm_mix": full["norm_mix"][:, None, :],
        "b_in": _in_cols_to_kernel(full["b_in"])[:, None, :],
        "conv_w": full["conv_w"],
        "a_log": _pad_lanes(full["a_log"]),
        "dt_bias": _pad_lanes(full["dt_bias"]),
        "sinks": full["sinks"],
        "c_norm": full["c_norm"][:, None, :],
        "norm_ffn": full["norm_ffn"][:, None, :],
        "norm_final": full["norm_final"][None, :],
    }
    for n in MATMUL_WEIGHTS:
        params[n] = jnp.zeros(wbf[n].shape, F32)
    return params


def _reference_grads(g):
    return {
        "norm_mix": g["norm_mix"][:, 0, :],
        "w_in": _in_cols_to_reference(g["w_in"]),
        "b_in": _in_cols_to_reference(g["b_in"][:, 0, :]),
        "conv_w": g["conv_w"],
        "a_log": g["a_log"][:, 0, :C_V_HEADS],
        "dt_bias": g["dt_bias"][:, 0, :C_V_HEADS],
        "sinks": g["sinks"],
        "c_norm": g["c_norm"][:, 0, :],
        "w_branch_a": g["w_branch_a"], "w_branch_b": g["w_branch_b"], "w_branch_c": g["w_branch_c"],
        "w_out": g["w_out"],
        "norm_ffn": g["norm_ffn"][:, 0, :],
        "w_ff1": g["w_ff1"], "w_ff2": g["w_ff2"],
        "norm_final": g["norm_final"][0],
    }


def kernel(x, positions, norm_mix, w_in, b_in, conv_w, a_log, dt_bias, sinks, c_norm, w_branch_a, w_branch_b, w_branch_c, w_out, norm_ffn, w_ff1, w_ff2, norm_final, loss_target, m_norm_mix, m_w_in, m_b_in, m_conv_w, m_a_log, m_dt_bias, m_sinks, m_c_norm, m_w_branch_a, m_w_branch_b, m_w_branch_c, m_w_out, m_norm_ffn, m_w_ff1, m_w_ff2, m_norm_final, v_norm_mix, v_w_in, v_b_in, v_conv_w, v_a_log, v_dt_bias, v_sinks, v_c_norm, v_w_branch_a, v_w_branch_b, v_w_branch_c, v_w_out, v_norm_ffn, v_w_ff1, v_w_ff2, v_norm_final):
    env = dict(locals())
    weights = {n: env[n] for n in WEIGHTS}
    moments_m = {n: env["m_" + n] for n in WEIGHTS}
    moments_v = {n: env["v_" + n] for n in WEIGHTS}

    axis_of = dict(BIG)
    cw = weights["conv_w"]
    c1 = cw.astype(BF16)
    c2 = (cw - c1.astype(F32)).astype(BF16)
    c3 = (cw - c1.astype(F32) - c2.astype(F32)).astype(BF16)
    gathered = _all_gather(_pack([weights[n].astype(BF16) for n in MATMUL_WEIGHTS] + [c1, c2, c3]),
                           name="gather_weights")
    blocks = _unpack(gathered, [weights[n].shape for n in MATMUL_WEIGHTS] + [cw.shape] * 3)
    wbf = {n: _shards_to_full(blk, axis_of[n]) for n, blk in zip(MATMUL_WEIGHTS, blocks)}
    wbf["w_in"] = _in_cols_to_kernel(wbf["w_in"])
    full = {n: weights[n] for n in SMALL}
    full["conv_w"] = _shards_to_full(sum(b.astype(F32) for b in blocks[-3:]), axis_of["conv_w"])

    tabs = rope_tables(positions[0])
    loss, (dx, dparams) = jax.value_and_grad(_local_loss, argnums=(0, 1))(
        x[0], _kernel_params(full, wbf), wbf, tabs, loss_target[0])
    grads = _reference_grads(dparams)
    loss = lax.psum(loss, ("x", "y", "c"))

    core = lax.axis_index("c")

    def by_core(n, axis, which):
        sh = _full_to_shards(grads[n], axis)
        sh = sh.reshape((N_CHIP, 2) + sh.shape[1:])
        return lax.dynamic_index_in_dim(sh, which, axis=1, keepdims=False).astype(BF16)

    from_sibling = _swap_with_sibling(_pack([by_core(n, axis, 1 - core) for n, axis in BIG], lead=1),
                                      name="scatter_grads_d2d")
    chip_sum = _add_bf16_call(_pack([by_core(n, axis, core) for n, axis in BIG], lead=1), from_sibling,
                              name="scatter_grads_add")
    big_parts = _chip_all_to_all(chip_sum, name="scatter_grads_ici")
    small_parts = _all_gather(_pack([grads[n] for n in SMALL]), name="gather_small_grads")

    out = {}
    for names, parts in (([n for n, _ in BIG], big_parts), (list(SMALL), small_parts)):
        shapes = [weights[n].shape for n in names]
        packed = [_pack([d[n] for n in names]) for d in (weights, moments_m, moments_v)]
        results = _adamw_call(parts, *packed, name="adamw_" + names[0])
        for kind, buf in zip(("grad", "delta", "new_m", "new_v"), results):
            for n, arr in zip(names, _unpack(buf, shapes)):
                out[kind, n] = arr
    return (loss, dx[None], *[out[kind, n] for kind in ("grad", "delta", "new_m", "new_v") for n in WEIGHTS])
```

```python
import math

import jax
import jax.numpy as jnp
from jax import lax
from jax.experimental import pallas as pl
from jax.experimental.pallas import tpu as pltpu

F32 = jnp.float32
BF16 = jnp.bfloat16

N_DEV = 8
D_MODEL = 1024
DEPTH = 2
HEAD_DIM = 64
ROT_DIM = 16
ROPE_THETA = 500000.0
BLK = 128
NEG_INF = -1e30
EPS = 1e-6
A_CONFIGS = ((128, 1), (512, 4), (2048, 16))
B_GROUP = 4
C_QK_HEADS = 4
C_V_HEADS = 8
C_DK = 128
C_CONV = 4
CHUNK = 64
ADAM_LR = 0.001
ADAM_B1 = 0.9
ADAM_B2 = 0.999
ADAM_EPS = 1e-08
ADAM_WD = 0.01
ADAM_STEP = 10

IN_LAYOUT = (
    ("gate_a", 0, 1024, 5392), ("gate_b", 1024, 1024, 6416), ("gate_c", 2048, 1024, 7440),
    ("a_q", 3072, 512, 0), ("a_k", 3584, 512, 512), ("a_v", 4096, 512, 1024), ("b_q", 4608, 512, 1536),
    ("c_z", 5120, 1024, 4352), ("c_qkv", 6144, 2048, 2304),
    ("b_k", 8192, 128, 2048), ("b_v", 8320, 128, 2176), ("c_ab", 8448, 16, 5376),
)
COL = {name: start for name, start, _, _ in IN_LAYOUT}
D_IN_PAD = 8704
IN_TN = D_IN_PAD // 4
LANES = 128
VMEM_LIMIT = 56 * 1024 * 1024


def _cparams(sem=None):
    return pltpu.CompilerParams(dimension_semantics=sem, vmem_limit_bytes=VMEM_LIMIT)


def _relu2(t):
    return jnp.square(jnp.maximum(t, 0.0))


def _mm(a, b, *, ta=False, tb=False, bias=None, a_fn=None, mul_drelu2=None, add=None,
        out_dtype=F32, relu2_out=False, tm=1024, tn=1024, tk=2048, name):
    if ta:
        kdim, m = a.shape
    else:
        m, kdim = a.shape
    n = b.shape[0] if tb else b.shape[1]
    tm, tn, tk = min(tm, m), min(tn, n), min(tk, kdim)
    assert m % tm == 0 and n % tn == 0 and kdim % tk == 0, (a.shape, b.shape, tm, tn, tk)
    nk = kdim // tk
    dims = (((0 if ta else 1,), (1 if tb else 0,)), ((), ()))
    extras = [e for e in (bias, mul_drelu2, add) if e is not None]

    def body(*refs):
        a_ref, b_ref = refs[0], refs[1]
        pos = 2
        bias_ref = pre_ref = add_ref = None
        if bias is not None:
            bias_ref = refs[pos]; pos += 1
        if mul_drelu2 is not None:
            pre_ref = refs[pos]; pos += 1
        if add is not None:
            add_ref = refs[pos]; pos += 1
        o_ref = refs[pos]
        pos += 1
        r_ref = None
        if relu2_out:
            r_ref = refs[pos]; pos += 1
        acc_ref = refs[pos] if nk > 1 else None

        av = a_ref[...]
        if a_fn is not None:
            av = a_fn(av)
        part = lax.dot_general(av.astype(BF16), b_ref[...].astype(BF16), dims,
                               preferred_element_type=F32)

        def finish(acc):
            if bias_ref is not None:
                acc = acc + bias_ref[...]
            if pre_ref is not None:
                acc = acc * (2.0 * jnp.maximum(pre_ref[...], 0.0))
            if add_ref is not None:
                acc = acc + add_ref[...]
            o_ref[...] = acc.astype(out_dtype)
            if r_ref is not None:
                r_ref[...] = _relu2(acc).astype(BF16)

        if nk == 1:
            finish(part)
        else:
            k = pl.program_id(2)

            @pl.when(k == 0)
            def _():
                acc_ref[...] = part

            @pl.when(k > 0)
            def _():
                acc_ref[...] += part

            @pl.when(k == nk - 1)
            def _():
                finish(acc_ref[...])

    a_spec = (pl.BlockSpec((tk, tm), lambda i, j, k: (k, i)) if ta
              else pl.BlockSpec((tm, tk), lambda i, j, k: (i, k)))
    b_spec = (pl.BlockSpec((tn, tk), lambda i, j, k: (j, k)) if tb
              else pl.BlockSpec((tk, tn), lambda i, j, k: (k, j)))
    in_specs = [a_spec, b_spec]
    if bias is not None:
        in_specs.append(pl.BlockSpec((1, tn), lambda i, j, k: (0, j)))
    for _ in extras[(1 if bias is not None else 0):]:
        in_specs.append(pl.BlockSpec((tm, tn), lambda i, j, k: (i, j)))
    o_spec = pl.BlockSpec((tm, tn), lambda i, j, k: (i, j))
    o_shape = jax.ShapeDtypeStruct((m, n), out_dtype)
    return pl.pallas_call(
        body, name=name,
        grid=(m // tm, n // tn, nk),
        in_specs=in_specs,
        out_specs=[o_spec, o_spec] if relu2_out else o_spec,
        out_shape=[o_shape, jax.ShapeDtypeStruct((m, n), BF16)] if relu2_out else o_shape,
        scratch_shapes=[pltpu.VMEM((tm, tn), F32)] if nk > 1 else [],
        compiler_params=_cparams(("parallel", "parallel", "arbitrary")),
    )(a, b, *extras)


def _colsum(x, *, name, tk=512, tn=1024):
    t, n = x.shape
    tn = min(tn, n)
    assert t % tk == 0 and n % tn == 0

    def body(x_ref, o_ref, acc_ref):
        k = pl.program_id(1)
        part = jnp.sum(x_ref[...].astype(F32).reshape(tk // 8, 8, tn), axis=0)

        @pl.when(k == 0)
        def _():
            acc_ref[...] = part

        @pl.when(k > 0)
        def _():
            acc_ref[...] += part

        @pl.when(k == t // tk - 1)
        def _():
            o_ref[...] = jnp.sum(acc_ref[...], axis=0, keepdims=True)

    return pl.pallas_call(
        body, name=name, grid=(n // tn, t // tk),
        in_specs=[pl.BlockSpec((tk, tn), lambda j, k: (k, j))],
        out_specs=pl.BlockSpec((1, tn), lambda j, k: (0, j)),
        out_shape=jax.ShapeDtypeStruct((1, n), F32),
        scratch_shapes=[pltpu.VMEM((8, tn), F32)],
        compiler_params=_cparams(("parallel", "arbitrary")),
    )(x)


def _rms_fwd_call(x, g, *, name, out_dtype=F32, tq=512):
    t, d = x.shape

    def body(x_ref, g_ref, y_ref):
        xv = x_ref[...]
        r = lax.rsqrt(jnp.mean(xv * xv, axis=-1, keepdims=True) + EPS)
        y_ref[...] = (xv * r * g_ref[...]).astype(out_dtype)

    return pl.pallas_call(
        body, name=name, grid=(t // tq,),
        in_specs=[pl.BlockSpec((tq, d), lambda i: (i, 0)), pl.BlockSpec((1, d), lambda i: (0, 0))],
        out_specs=pl.BlockSpec((tq, d), lambda i: (i, 0)),
        out_shape=jax.ShapeDtypeStruct((t, d), out_dtype),
        compiler_params=_cparams(("parallel",)),
    )(x, g)


def _rms_bwd_call(x, g, dy, *, name, add=None, tq=512):
    t, d = x.shape
    nt = t // tq

    def body(*refs):
        if add is None:
            x_ref, g_ref, dy_ref, dx_ref, dg_ref, acc_ref = refs
        else:
            x_ref, g_ref, dy_ref, add_ref, dx_ref, dg_ref, acc_ref = refs
        i = pl.program_id(0)
        xv = x_ref[...]
        r = lax.rsqrt(jnp.mean(xv * xv, axis=-1, keepdims=True) + EPS)
        xh = xv * r
        dyv = dy_ref[...]
        dxh = dyv * g_ref[...]
        dx = r * (dxh - xh * jnp.mean(dxh * xh, axis=-1, keepdims=True))
        dx_ref[...] = dx if add is None else dx + add_ref[...]
        part = jnp.sum((dyv * xh).reshape(tq // 8, 8, d), axis=0)

        @pl.when(i == 0)
        def _():
            acc_ref[...] = part

        @pl.when(i > 0)
        def _():
            acc_ref[...] += part

        @pl.when(i == nt - 1)
        def _():
            dg_ref[...] = jnp.sum(acc_ref[...], axis=0, keepdims=True)

    blk = pl.BlockSpec((tq, d), lambda i: (i, 0))
    row = pl.BlockSpec((1, d), lambda i: (0, 0))
    extra = [] if add is None else [add]
    return pl.pallas_call(
        body, name=name, grid=(nt,),
        in_specs=[blk, row, blk] + [blk] * len(extra),
        out_specs=[blk, row],
        out_shape=[jax.ShapeDtypeStruct((t, d), F32), jax.ShapeDtypeStruct((1, d), F32)],
        scratch_shapes=[pltpu.VMEM((8, d), F32)],
        compiler_params=_cparams(("arbitrary",)),
    )(x, g, dy, *extra)


def _loss_fwd_call(x, g, tgt, *, tq=512):
    t, d = x.shape
    nt = t // tq

    def body(x_ref, g_ref, t_ref, o_ref, acc_ref):
        i = pl.program_id(0)
        xv = x_ref[...]
        r = lax.rsqrt(jnp.mean(xv * xv, axis=-1, keepdims=True) + EPS)
        err = xv * r * g_ref[...] - t_ref[...]
        part = jnp.sum((err * err).reshape(tq // 8, 8, d), axis=0)

        @pl.when(i == 0)
        def _():
            acc_ref[...] = part

        @pl.when(i > 0)
        def _():
            acc_ref[...] += part

        @pl.when(i == nt - 1)
        def _():
            tot = jnp.sum(jnp.sum(acc_ref[...], axis=0, keepdims=True), axis=1, keepdims=True)
            o_ref[...] = jnp.broadcast_to(tot * (0.5 / d), (8, LANES))

    out = pl.pallas_call(
        body, name="loss_fwd", grid=(nt,),
        in_specs=[pl.BlockSpec((tq, d), lambda i: (i, 0)), pl.BlockSpec((1, d), lambda i: (0, 0)),
                  pl.BlockSpec((tq, d), lambda i: (i, 0))],
        out_specs=pl.BlockSpec((8, LANES), lambda i: (0, 0)),
        out_shape=jax.ShapeDtypeStruct((8, LANES), F32),
        scratch_shapes=[pltpu.VMEM((8, d), F32)],
        compiler_params=_cparams(("arbitrary",)),
    )(x, g, tgt)
    return out[0, 0]


def _loss_bwd_call(x, g, tgt, scale, *, tq=512):
    t, d = x.shape
    nt = t // tq

    def body(x_ref, g_ref, t_ref, s_ref, dx_ref, dg_ref, acc_ref):
        i = pl.program_id(0)
        xv = x_ref[...]
        r = lax.rsqrt(jnp.mean(xv * xv, axis=-1, keepdims=True) + EPS)
        xh = xv * r
        gv = g_ref[...]
        dyv = (xh * gv - t_ref[...]) * (s_ref[...] * (1.0 / d))
        dxh = dyv * gv
        dx_ref[...] = r * (dxh - xh * jnp.mean(dxh * xh, axis=-1, keepdims=True))
        part = jnp.sum((dyv * xh).reshape(tq // 8, 8, d), axis=0)

        @pl.when(i == 0)
        def _():
            acc_ref[...] = part

        @pl.when(i > 0)
        def _():
            acc_ref[...] += part

        @pl.when(i == nt - 1)
        def _():
            dg_ref[...] = jnp.sum(acc_ref[...], axis=0, keepdims=True)

    return pl.pallas_call(
        body, name="loss_bwd", grid=(nt,),
        in_specs=[pl.BlockSpec((tq, d), lambda i: (i, 0)), pl.BlockSpec((1, d), lambda i: (0, 0)),
                  pl.BlockSpec((tq, d), lambda i: (i, 0)), pl.BlockSpec((1, 1), lambda i: (0, 0))],
        out_specs=[pl.BlockSpec((tq, d), lambda i: (i, 0)), pl.BlockSpec((1, d), lambda i: (0, 0))],
        out_shape=[jax.ShapeDtypeStruct((t, d), F32), jax.ShapeDtypeStruct((1, d), F32)],
        scratch_shapes=[pltpu.VMEM((8, d), F32)],
        compiler_params=_cparams(("arbitrary",)),
    )(x, g, tgt, scale)


MESH_ID = pl.DeviceIdType.MESH
HBM_SPEC = pl.BlockSpec(memory_space=pl.ANY)


def rope_tables(positions):
    inv_freq = jnp.power(ROPE_THETA, -jnp.arange(0, ROT_DIM, 2, dtype=F32) / ROT_DIM)
    ang = positions.astype(F32)[:, None] * inv_freq
    cos, sin = jnp.cos(ang), jnp.sin(ang)
    t = positions.shape[0]
    one = jnp.ones((t, HEAD_DIM - ROT_DIM), F32)
    zero8 = jnp.zeros((t, ROT_DIM // 2), F32)
    zero = jnp.zeros((t, HEAD_DIM - ROT_DIM), F32)
    a = jnp.concatenate([cos, cos, one], axis=1)
    b = jnp.concatenate([zero8, sin, zero], axis=1)
    c = jnp.concatenate([-sin, zero8, zero], axis=1)
    return tuple(jnp.concatenate([m, m], axis=1) for m in (a, b, c))


def _rope_chunk(xs, a, b, c, transpose):
    half = ROT_DIM // 2
    if transpose:
        return xs * a + pltpu.roll(xs * b, LANES - half, 1) + pltpu.roll(xs * c, half, 1)
    return xs * a + pltpu.roll(xs, half, 1) * b + pltpu.roll(xs, LANES - half, 1) * c


def _rope_gather_call(u, tabs, parts, *, name, tq=512):
    t = u.shape[0]
    total = sum(w for _, w, _ in parts)
    assert all(start % w == 0 for start, w, _ in parts)

    def body(a_ref, b_ref, c_ref, *refs):
        o_ref = refs[-1]
        a, b, c = a_ref[...], b_ref[...], c_ref[...]
        off = 0
        for x_ref, (_, w, roped) in zip(refs[:-1], parts):
            for j in range(w // LANES):
                xs = x_ref[:, j * LANES:(j + 1) * LANES]
                o_ref[:, off + j * LANES:off + (j + 1) * LANES] = _rope_chunk(xs, a, b, c, False) if roped else xs
            off += w

    tab_spec = pl.BlockSpec((tq, LANES), lambda i: (i, 0))
    return pl.pallas_call(
        body, name=name, grid=(t // tq,),
        in_specs=[tab_spec] * 3 + [pl.BlockSpec((tq, w), lambda i, cb=start // w: (i, cb)) for start, w, _ in parts],
        out_specs=pl.BlockSpec((tq, total), lambda i: (i, 0)),
        out_shape=jax.ShapeDtypeStruct((t, total), F32),
        compiler_params=_cparams(("parallel",)),
    )(*tabs, *([u] * len(parts)))


def _du_operands(du_buf, n_inputs):
    if du_buf is None:
        return [], [], {}
    return [du_buf], [HBM_SPEC], {n_inputs: 0}


def _du_shape(t):
    return jax.ShapeDtypeStruct((t, D_IN_PAD), BF16)


def _rope_scatter_call(du_buf, t, pieces, col, tabs, *, name, tq=512):
    total = sum(w for _, w, _ in pieces)
    assert col % total == 0
    arrays = [a for arrs, _, _ in pieces for a in arrs]
    extra, extra_specs, aliases = _du_operands(du_buf, 3 + len(arrays))

    def body(a_ref, b_ref, c_ref, *refs):
        o_ref = refs[len(arrays) + len(extra)]
        a, b, c = a_ref[...], b_ref[...], c_ref[...]
        k = off = 0
        for arrs, w, roped in pieces:
            mine = refs[k:k + len(arrs)]
            k += len(arrs)
            for j in range(w // LANES):
                cs = slice(j * LANES, (j + 1) * LANES)
                xs = mine[0][:, cs]
                for r in mine[1:]:
                    xs = xs + r[:, cs]
                val = _rope_chunk(xs, a, b, c, True) if roped else xs
                o_ref[:, off + j * LANES:off + (j + 1) * LANES] = val.astype(BF16)
            off += w

    tab_spec = pl.BlockSpec((tq, LANES), lambda i: (i, 0))
    in_specs = [tab_spec] * 3 + [pl.BlockSpec((tq, w), lambda i: (i, 0)) for arrs, w, _ in pieces for _ in arrs]
    return pl.pallas_call(
        body, name=name, grid=(t // tq,),
        in_specs=in_specs + extra_specs,
        out_specs=pl.BlockSpec((tq, total), lambda i: (i, col // total)),
        out_shape=_du_shape(t), input_output_aliases=aliases,
        compiler_params=_cparams(("parallel",)),
    )(*tabs, *arrays, *extra)


def _band_masks(first_block, max_dist):
    qi = lax.broadcasted_iota(jnp.int32, (BLK, BLK), 0)
    kj = lax.broadcasted_iota(jnp.int32, (BLK, BLK), 1)
    valid_prev = jnp.logical_and(kj >= qi + (BLK - max_dist), jnp.logical_not(first_block))
    valid_cur = kj <= qi
    return valid_prev, valid_cur


_NN = (((1,), (0,)), ((), ()))
_NT = (((1,), (1,)), ((), ()))
_TN = (((0,), (0,)), ((), ()))


HEAD_STAGE = 8


def _attn_row_maps(nb):
    def cur(i):
        return jnp.minimum(i, nb - 1)

    def prev(i):
        return jnp.maximum(jnp.minimum(i, nb - 1) - 1, 0)

    return cur, prev


def _dil_view(a, dil):
    t, w = a.shape
    return a.reshape(t // dil, dil * w)


def _dil_spec(w, dil, rows, seg=None, off=0):
    seg = w if seg is None else seg
    assert off % w == 0 and (dil == 1 or seg % w == 0)
    return pl.BlockSpec((BLK, w), lambda r, i: (rows(i), (r * seg + off) // w))


def _dil_shape(l, dil, w):
    return jax.ShapeDtypeStruct((l, dil * w), F32)


def _attn_fwd_call(qkv2, sink, *, dil, group, max_dist, seg, offs, qw, kw, name):
    l = qkv2.shape[0]
    nh = qw // HEAD_DIM
    nb = l // BLK
    scale = HEAD_DIM ** -0.5
    use_sink = sink is not None

    def body(*refs):
        if use_sink:
            sink_ref, refs = refs[0], refs[1:]
        q_ref, kp_ref, kc_ref, vp_ref, vc_ref, o_ref, lse_ref = refs
        valid_prev, valid_cur = _band_masks(pl.program_id(1) == 0, max_dist)
        lane = lax.broadcasted_iota(jnp.int32, (BLK, LANES), 1)
        lse_tile = jnp.zeros((BLK, LANES), F32)

        def dot(a, b, dims=_NN):
            return lax.dot_general(a, b, dims, preferred_element_type=F32)

        for g0 in range(0, nh, HEAD_STAGE):
            heads = list(range(g0, min(g0 + HEAD_STAGE, nh)))
            kv = {}
            for kh in sorted({h // group for h in heads}):
                ks = slice(kh * HEAD_DIM, (kh + 1) * HEAD_DIM)
                kv[kh] = tuple(ref[:, ks].astype(BF16) for ref in (kp_ref, kc_ref, vp_ref, vc_ref))
            qs = [q_ref[:, h * HEAD_DIM:(h + 1) * HEAD_DIM].astype(BF16) for h in heads]
            sps = [jnp.where(valid_prev, dot(qh, kv[h // group][0], _NT) * scale, NEG_INF) for h, qh in zip(heads, qs)]
            scs = [jnp.where(valid_cur, dot(qh, kv[h // group][1], _NT) * scale, NEG_INF) for h, qh in zip(heads, qs)]
            ms = [jnp.maximum(jnp.max(sp, axis=1, keepdims=True), jnp.max(sc, axis=1, keepdims=True))
                  for sp, sc in zip(sps, scs)]
            if use_sink:
                ms = [jnp.maximum(m, sink_ref[h]) for h, m in zip(heads, ms)]
            pps = [jnp.exp(sp - m) for sp, m in zip(sps, ms)]
            pcs = [jnp.exp(sc - m) for sc, m in zip(scs, ms)]
            dens = [jnp.sum(pp, axis=1, keepdims=True) + jnp.sum(pc, axis=1, keepdims=True)
                    for pp, pc in zip(pps, pcs)]
            if use_sink:
                dens = [den + jnp.exp(sink_ref[h] - m) for h, den, m in zip(heads, dens, ms)]
            outs = [dot(pp.astype(BF16), kv[h // group][2]) + dot(pc.astype(BF16), kv[h // group][3])
                    for h, pp, pc in zip(heads, pps, pcs)]
            for h, o, den, m in zip(heads, outs, dens, ms):
                o_ref[:, h * HEAD_DIM:(h + 1) * HEAD_DIM] = o / den
                lse_tile = jnp.where(lane == h, m + jnp.log(den), lse_tile)
        lse_ref[...] = lse_tile

    cur, prev = _attn_row_maps(nb)
    o_spec, lse_spec = _dil_spec(qw, dil, cur), _dil_spec(LANES, dil, cur)
    in_specs = [_dil_spec(qw, dil, cur, seg, offs[0]),
                _dil_spec(kw, dil, prev, seg, offs[1]), _dil_spec(kw, dil, cur, seg, offs[1]),
                _dil_spec(kw, dil, prev, seg, offs[2]), _dil_spec(kw, dil, cur, seg, offs[2])]
    args = [qkv2] * 5
    if use_sink:
        in_specs = [pl.BlockSpec(memory_space=pltpu.SMEM)] + in_specs
        args = [sink] + args
    return pl.pallas_call(
        body, name=name, grid=(dil, nb),
        in_specs=in_specs,
        out_specs=[o_spec, lse_spec],
        out_shape=[_dil_shape(l, dil, qw), _dil_shape(l, dil, LANES)],
        compiler_params=_cparams(("parallel", "parallel")),
    )(*args)


def _attn_bwd_call(qkv2, sink, o2, lse2, do2, dlse2, *, dil, group, max_dist, seg, offs, qw, kw, name):
    l = qkv2.shape[0]
    nh = qw // HEAD_DIM
    nb = l // BLK
    scale = HEAD_DIM ** -0.5
    use_sink = sink is not None

    def body(*refs):
        if use_sink:
            sink_ref, refs = refs[0], refs[1:]
        (q_ref, kp_ref, kc_ref, vp_ref, vc_ref, o_ref, lse_ref, do_ref, dlse_ref,
         dq_ref, dk_ref, dv_ref, dsink_ref, ck_ref, cv_ref) = refs
        step = pl.program_id(1)

        @pl.when(jnp.logical_and(pl.program_id(0) == 0, step == 0))
        def _():
            dsink_ref[...] = jnp.zeros_like(dsink_ref)

        @pl.when(step == 0)
        def _():
            ck_ref[...] = jnp.zeros_like(ck_ref)
            cv_ref[...] = jnp.zeros_like(cv_ref)

        def dot(a, b, dims=_NN):
            return lax.dot_general(a, b, dims, preferred_element_type=F32)

        @pl.when(step < nb)
        def _():
            valid_prev, valid_cur = _band_masks(step == 0, max_dist)
            row = lax.broadcasted_iota(jnp.int32, (8, LANES), 0)
            lanes8 = lax.broadcasted_iota(jnp.int32, (8, LANES), 1)
            ds_tile = jnp.zeros((8, LANES), F32)
            for g0 in range(0, nh, HEAD_STAGE):
                heads = list(range(g0, min(g0 + HEAD_STAGE, nh)))
                hss = [slice(h * HEAD_DIM, (h + 1) * HEAD_DIM) for h in heads]
                kv = {}
                for kh in sorted({h // group for h in heads}):
                    ks = slice(kh * HEAD_DIM, (kh + 1) * HEAD_DIM)
                    kv[kh] = tuple(ref[:, ks].astype(BF16) for ref in (kp_ref, kc_ref, vp_ref, vc_ref))
                qs = [q_ref[:, hs].astype(BF16) for hs in hss]
                dos = [do_ref[:, hs] for hs in hss]
                dobs = [d.astype(BF16) for d in dos]
                lses = [lse_ref[:, h:h + 1] for h in heads]
                sps = [dot(qh, kv[h // group][0], _NT) * scale for h, qh in zip(heads, qs)]
                scs = [dot(qh, kv[h // group][1], _NT) * scale for h, qh in zip(heads, qs)]
                dpps = [dot(dob, kv[h // group][2], _NT) for h, dob in zip(heads, dobs)]
                dpcs = [dot(dob, kv[h // group][3], _NT) for h, dob in zip(heads, dobs)]
                pps = [jnp.where(valid_prev, jnp.exp(jnp.where(valid_prev, sp, NEG_INF) - ls), 0.0)
                       for sp, ls in zip(sps, lses)]
                pcs = [jnp.where(valid_cur, jnp.exp(jnp.where(valid_cur, sc, NEG_INF) - ls), 0.0)
                       for sc, ls in zip(scs, lses)]
                deltas = [jnp.sum(d * o_ref[:, hs], axis=1, keepdims=True) for d, hs in zip(dos, hss)]
                corrs = [dlse_ref[:, h:h + 1] - dl for h, dl in zip(heads, deltas)]
                dsps = [(pp * (dp + c)).astype(BF16) for pp, dp, c in zip(pps, dpps, corrs)]
                dscs = [(pc * (dp + c)).astype(BF16) for pc, dp, c in zip(pcs, dpcs, corrs)]
                for h, hs, dsp, dsc in zip(heads, hss, dsps, dscs):
                    dq_ref[:, hs] = (dot(dsp, kv[h // group][0]) + dot(dsc, kv[h // group][1])) * scale
                parts = [(dot(dsc, qh, _TN) * scale, dot(dsp, qh, _TN) * scale,
                          dot(pc.astype(BF16), dob, _TN), dot(pp.astype(BF16), dob, _TN))
                         for dsc, dsp, qh, pc, pp, dob in zip(dscs, dsps, qs, pcs, pps, dobs)]
                for kh in kv:
                    ks = slice(kh * HEAD_DIM, (kh + 1) * HEAD_DIM)
                    mine = [p for h, p in zip(heads, parts) if h // group == kh]
                    dkc, dkp, dvc, dvp = (sum(p[j] for p in mine[1:]) + mine[0][j] for j in range(4))
                    dk_ref[:, ks] = ck_ref[:, ks] + dkp
                    dv_ref[:, ks] = cv_ref[:, ks] + dvp
                    ck_ref[:, ks] = dkc
                    cv_ref[:, ks] = dvc
                if use_sink:
                    for h, ls, dl in zip(heads, lses, deltas):
                        val = -jnp.sum(jnp.exp(sink_ref[h] - ls) * dl, axis=0, keepdims=True)
                        ds_tile = jnp.where(jnp.logical_and(row == 0, lanes8 == h), val, ds_tile)
            if use_sink:
                dsink_ref[...] += ds_tile

        @pl.when(step == nb)
        def _():
            dk_ref[...] = ck_ref[...]
            dv_ref[...] = cv_ref[...]

    cur, prev = _attn_row_maps(nb)
    q_spec, lse_spec = _dil_spec(qw, dil, cur), _dil_spec(LANES, dil, cur)
    lag_spec = _dil_spec(kw, dil, lambda i: jnp.maximum(i - 1, 0))
    in_specs = [_dil_spec(qw, dil, cur, seg, offs[0]),
                _dil_spec(kw, dil, prev, seg, offs[1]), _dil_spec(kw, dil, cur, seg, offs[1]),
                _dil_spec(kw, dil, prev, seg, offs[2]), _dil_spec(kw, dil, cur, seg, offs[2]),
                q_spec, lse_spec, q_spec, lse_spec]
    args = [qkv2] * 5 + [o2, lse2, do2, dlse2]
    if use_sink:
        in_specs = [pl.BlockSpec(memory_space=pltpu.SMEM)] + in_specs
        args = [sink] + args
    kv_shape = _dil_shape(l, dil, kw)
    return pl.pallas_call(
        body, name=name, grid=(dil, nb + 1),
        in_specs=in_specs,
        out_specs=[q_spec, lag_spec, lag_spec, pl.BlockSpec((8, LANES), lambda r, i: (0, 0))],
        out_shape=[_dil_shape(l, dil, qw), kv_shape, kv_shape,
                   jax.ShapeDtypeStruct((8, LANES), F32)],
        scratch_shapes=[pltpu.VMEM((BLK, kw), F32), pltpu.VMEM((BLK, kw), F32)],
        compiler_params=_cparams(("arbitrary", "arbitrary")),
    )(*args)


def _attn_config(tag, dil, group, max_dist, seg, offs, qw, kw):
    return dict(name=tag, dil=dil, group=group, max_dist=max_dist, seg=seg, offs=offs, qw=qw, kw=kw)


A_W = 8 * HEAD_DIM
ATTN_A_CFGS = tuple(_attn_config("attn_a%d" % dil, dil, 1, window // dil, 3 * A_W, (0, A_W, 2 * A_W), A_W, A_W)
                    for window, dil in A_CONFIGS)
B_KVW = 2 * HEAD_DIM
ATTN_B_CFG = _attn_config("attn_b", 1, B_GROUP, BLK - 1, A_W + 2 * B_KVW, (0, A_W, A_W + B_KVW), A_W, B_KVW)


def _attn_fwd(cfg, qkv, sink):
    t = qkv.shape[0]
    kw = {k: v for k, v in cfg.items() if k != "name"}
    qkv2 = _dil_view(qkv, cfg["dil"])
    o2, lse2 = _attn_fwd_call(qkv2, sink, name=cfg["name"] + "_fwd", **kw)
    return o2.reshape(t, cfg["qw"]), lse2.reshape(t, LANES), (qkv2, o2, lse2)


def _attn_bwd(cfg, saved, sink, do, dlse):
    qkv2, o2, lse2 = saved
    t = do.shape[0]
    kw = {k: v for k, v in cfg.items() if k != "name"}
    dq2, dk2, dv2, dsink = _attn_bwd_call(qkv2, sink, o2, lse2, _dil_view(do, cfg["dil"]),
                                          _dil_view(dlse, cfg["dil"]), name=cfg["name"] + "_bwd", **kw)
    return dq2.reshape(t, cfg["qw"]), dk2.reshape(t, cfg["kw"]), dv2.reshape(t, cfg["kw"]), dsink


def _head_expand():
    r = lax.broadcasted_iota(jnp.int32, (LANES, 8 * HEAD_DIM), 0)
    c = lax.broadcasted_iota(jnp.int32, (LANES, 8 * HEAD_DIM), 1)
    return (c // HEAD_DIM == r).astype(F32)


def _combine_weights(l0, l1, l2):
    m = jnp.maximum(jnp.maximum(l0, l1), l2)
    e0, e1, e2 = jnp.exp(l0 - m), jnp.exp(l1 - m), jnp.exp(l2 - m)
    inv = 1.0 / (e0 + e1 + e2)
    return e0 * inv, e1 * inv, e2 * inv


def _combine_fwd_call(os_, lses, *, tq=256):
    t, w = os_[0].shape

    def body(o0, o1, o2, l0, l1, l2, y_ref):
        ws = _combine_weights(l0[...], l1[...], l2[...])
        e = _head_expand()
        y = jnp.zeros((tq, w), F32)
        for o_ref, wt in zip((o0, o1, o2), ws):
            y = y + _dot_mask(e, wt, mask_left=False) * o_ref[...]
        y_ref[...] = y

    o_spec = pl.BlockSpec((tq, w), lambda i: (i, 0))
    l_spec = pl.BlockSpec((tq, LANES), lambda i: (i, 0))
    return pl.pallas_call(
        body, name="combine_fwd", grid=(t // tq,),
        in_specs=[o_spec] * 3 + [l_spec] * 3, out_specs=o_spec,
        out_shape=jax.ShapeDtypeStruct((t, w), F32),
        compiler_params=_cparams(("parallel",)),
    )(*os_, *lses)


def _combine_bwd_call(os_, lses, dy, *, tq=256):
    t, w = dy.shape

    def body(o0, o1, o2, l0, l1, l2, dy_ref, do0, do1, do2, dl0, dl1, dl2):
        ws = _combine_weights(l0[...], l1[...], l2[...])
        e = _head_expand()
        dyv = dy_ref[...]
        dws = []
        for o_ref, do_ref, wt in zip((o0, o1, o2), (do0, do1, do2), ws):
            do_ref[...] = _dot_mask(e, wt, mask_left=False) * dyv
            dws.append(_dot_mask(e, dyv * o_ref[...], _NT, mask_left=False))
        mean = ws[0] * dws[0] + ws[1] * dws[1] + ws[2] * dws[2]
        for dl_ref, wt, dw in zip((dl0, dl1, dl2), ws, dws):
            dl_ref[...] = wt * (dw - mean)

    o_spec = pl.BlockSpec((tq, w), lambda i: (i, 0))
    l_spec = pl.BlockSpec((tq, LANES), lambda i: (i, 0))
    o_shape = jax.ShapeDtypeStruct((t, w), F32)
    l_shape = jax.ShapeDtypeStruct((t, LANES), F32)
    return pl.pallas_call(
        body, name="combine_bwd", grid=(t // tq,),
        in_specs=[o_spec] * 3 + [l_spec] * 3 + [o_spec], out_specs=[o_spec] * 3 + [l_spec] * 3,
        out_shape=[o_shape] * 3 + [l_shape] * 3,
        compiler_params=_cparams(("parallel",)),
    )(*os_, *lses, dy)


C_QKW = C_QK_HEADS * C_DK
C_CONV_W = 2 * C_QKW + C_V_HEADS * C_DK
HALO = 8


def _silu_parts(z):
    sig = jax.nn.sigmoid(z)
    return z * sig, sig * (1.0 + z * (1.0 - sig))


def _conv_window_specs(tq, t):
    c = C_CONV_W
    cb = COL["c_qkv"] // c
    blk = pl.BlockSpec((tq, c), lambda i: (i, cb))
    before = pl.BlockSpec((HALO, c), lambda i: (jnp.maximum(i * (tq // HALO) - 1, 0), cb))
    return c, cb, blk, before


def _conv_prep_fwd_call(u, w, *, tq=256):
    t = u.shape[0]
    c, _, x_spec, halo_spec = _conv_window_specs(tq, t)
    nqk = 2 * C_QK_HEADS

    def body(x_ref, halo_ref, w_ref, z_ref, qk_ref, v_ref):
        i = pl.program_id(0)
        halo = jnp.where(i == 0, 0.0, halo_ref[...])
        xc = jnp.concatenate([halo, x_ref[...]], axis=0)
        wv = w_ref[...]
        z = xc[HALO - 3:HALO - 3 + tq] * wv[0:1]
        for j in range(1, C_CONV):
            z = z + xc[HALO - 3 + j:HALO - 3 + j + tq] * wv[j:j + 1]
        z_ref[...] = z
        act, _ = _silu_parts(z)
        for h in range(nqk):
            a = act[:, h * C_DK:(h + 1) * C_DK]
            qk_ref[:, h * C_DK:(h + 1) * C_DK] = a * lax.rsqrt(jnp.sum(a * a, axis=1, keepdims=True) + EPS)
        v_ref[...] = act[:, nqk * C_DK:]

    return pl.pallas_call(
        body, name="conv_prep_fwd", grid=(t // tq,),
        in_specs=[x_spec, halo_spec, pl.BlockSpec((C_CONV, c), lambda i: (0, 0))],
        out_specs=[pl.BlockSpec((tq, c), lambda i: (i, 0)),
                   pl.BlockSpec((tq, 2 * C_QKW), lambda i: (i, 0)),
                   pl.BlockSpec((tq, c - 2 * C_QKW), lambda i: (i, 0))],
        out_shape=[jax.ShapeDtypeStruct((t, c), F32), jax.ShapeDtypeStruct((t, 2 * C_QKW), F32),
                   jax.ShapeDtypeStruct((t, c - 2 * C_QKW), F32)],
        compiler_params=_cparams(("parallel",)),
    )(u, u, w)


def _conv_prep_dz_call(z, dqk, dv, *, tq=256):
    t, c = z.shape
    nqk = 2 * C_QK_HEADS

    def body(z_ref, dqk_ref, dv_ref, dz_ref):
        zv = z_ref[...]
        act, dact = _silu_parts(zv)
        for h in range(nqk):
            hs = slice(h * C_DK, (h + 1) * C_DK)
            a = act[:, hs]
            r = lax.rsqrt(jnp.sum(a * a, axis=1, keepdims=True) + EPS)
            nrm = a * r
            dn = dqk_ref[:, hs]
            da = r * (dn - nrm * jnp.sum(dn * nrm, axis=1, keepdims=True))
            dz_ref[:, hs] = da * dact[:, hs]
        dz_ref[:, nqk * C_DK:] = dv_ref[...] * dact[:, nqk * C_DK:]

    return pl.pallas_call(
        body, name="conv_prep_dz", grid=(t // tq,),
        in_specs=[pl.BlockSpec((tq, c), lambda i: (i, 0)),
                  pl.BlockSpec((tq, 2 * C_QKW), lambda i: (i, 0)),
                  pl.BlockSpec((tq, c - 2 * C_QKW), lambda i: (i, 0))],
        out_specs=pl.BlockSpec((tq, c), lambda i: (i, 0)),
        out_shape=jax.ShapeDtypeStruct((t, c), F32),
        compiler_params=_cparams(("parallel",)),
    )(z, dqk, dv)


def _conv_bwd_call(u, dz, w, du_buf, *, tq=256):
    t = u.shape[0]
    nt = t // tq
    c, cb, x_spec, halo_spec = _conv_window_specs(tq, t)
    extra, extra_specs, aliases = _du_operands(du_buf, 5)

    def body(x_ref, xh_ref, dz_ref, dzh_ref, w_ref, *refs):
        dx_ref, dw_ref = refs[len(extra):]
        i = pl.program_id(0)
        xc = jnp.concatenate([jnp.where(i == 0, 0.0, xh_ref[...]), x_ref[...]], axis=0)
        dzv = dz_ref[...]
        dzc = jnp.concatenate([dzv, jnp.where(i == nt - 1, 0.0, dzh_ref[...])], axis=0)
        wv = w_ref[...]
        dx = dzv * wv[3:4]
        for s in range(1, C_CONV):
            dx = dx + dzc[s:s + tq] * wv[3 - s:4 - s]
        dx_ref[...] = dx.astype(BF16)
        row = lax.broadcasted_iota(jnp.int32, (8, c), 0)
        dw = jnp.zeros((8, c), F32)
        for j in range(C_CONV):
            prod = dzv * xc[HALO - 3 + j:HALO - 3 + j + tq]
            col = jnp.sum(jnp.sum(prod.reshape(tq // 8, 8, c), axis=0), axis=0, keepdims=True)
            dw = jnp.where(row == j, col, dw)

        @pl.when(i == 0)
        def _():
            dw_ref[...] = dw

        @pl.when(i > 0)
        def _():
            dw_ref[...] += dw

    blk = pl.BlockSpec((tq, c), lambda i: (i, 0))
    after = pl.BlockSpec((HALO, c), lambda i: (jnp.minimum((i + 1) * (tq // HALO), t // HALO - 1), 0))
    return pl.pallas_call(
        body, name="conv_bwd", grid=(nt,),
        in_specs=[x_spec, halo_spec, blk, after, pl.BlockSpec((C_CONV, c), lambda i: (0, 0))] + extra_specs,
        out_specs=[pl.BlockSpec((tq, c), lambda i: (i, cb)), pl.BlockSpec((8, c), lambda i: (0, 0))],
        out_shape=[_du_shape(t), jax.ShapeDtypeStruct((8, c), F32)],
        input_output_aliases=aliases,
        compiler_params=_cparams(("arbitrary",)),
    )(u, u, dz, dz, w, *extra)


C_VW = C_V_HEADS * C_DK


def _softplus(x):
    return jnp.maximum(x, 0.0) + jnp.log(1.0 + jnp.exp(-jnp.abs(x)))


def _tri_masks():
    r = lax.broadcasted_iota(jnp.int32, (CHUNK, CHUNK), 0)
    c = lax.broadcasted_iota(jnp.int32, (CHUNK, CHUNK), 1)
    return r >= c, r > c


def _split_bf16(a):
    hi = a.astype(BF16)
    return hi, (a - hi.astype(F32)).astype(BF16)


def _dot_hi(a, b, dims=None):
    dims = _NN if dims is None else dims
    ah, al = _split_bf16(a)
    bh, bl = _split_bf16(b)

    def d(x, y):
        return lax.dot_general(x, y, dims, preferred_element_type=F32)

    return d(ah, bh) + (d(ah, bl) + d(al, bh))


def _dot_mask(mask, b, dims=None, mask_left=True):
    dims = _NN if dims is None else dims
    mb = mask.astype(BF16)
    b1 = b.astype(BF16)
    rest = b - b1.astype(F32)
    b2 = rest.astype(BF16)
    b3 = (rest - b2.astype(F32)).astype(BF16)
    out = None
    for p in (b1, b2, b3):
        term = (lax.dot_general(mb, p, dims, preferred_element_type=F32) if mask_left
                else lax.dot_general(p, mb, dims, preferred_element_type=F32))
        out = term if out is None else out + term
    return out


def _unit_lower_inverses(mats):
    r = lax.broadcasted_iota(jnp.int32, (CHUNK, CHUNK), 0)
    c = lax.broadcasted_iota(jnp.int32, (CHUNK, CHUNK), 1)
    eye = (r == c).astype(F32)
    xs = [eye - a for a in mats]
    ps = [_dot_hi(a, a) for a in mats]
    steps = int(math.log2(CHUNK)) - 1
    for s in range(steps):
        xs = [x + _dot_hi(x, p) for x, p in zip(xs, ps)]
        if s < steps - 1:
            ps = [_dot_hi(p, p) for p in ps]
    return xs


def _gate_tiles(cab, alog, dtb):
    pre = cab + dtb
    g = -jnp.exp(alog) * _softplus(pre)
    beta = jax.nn.sigmoid(pltpu.roll(cab, LANES - C_V_HEADS, 1))
    return g, beta, pre


def _chunk_common(kk, qk, gc, gct, beta, h, tri, strict):
    gcol, grow, bcol = gc[:, h:h + 1], gct[h:h + 1, :], beta[:, h:h + 1]
    decay = jnp.where(tri, jnp.exp(jnp.where(tri, gcol - grow, 0.0)), 0.0)
    kkd = jnp.where(strict, kk * decay, 0.0)
    attn = jnp.where(tri, qk * decay, 0.0)
    glast = gc[CHUNK - 1:CHUNK, h:h + 1]
    return gcol, bcol, decay, kkd, attn, glast


def _cab_spec():
    return pl.BlockSpec((CHUNK, LANES), lambda n: (n, COL["c_ab"] // LANES))


def _delta_prep_call(qk, v, u, alog, dtb):
    t = qk.shape[0]
    nc = t // CHUNK
    scale = C_DK ** -0.5

    def body(q_ref, k_ref, v_ref, cab_ref, alog_ref, dtb_ref,
             u_ref, w_ref, qd_ref, kd_ref, attn_ref, tmat_ref, gc_ref, beta_ref):
        tri, strict = _tri_masks()
        g, beta, _ = _gate_tiles(cab_ref[...], alog_ref[...], dtb_ref[...])
        gc = _dot_mask(tri, g)
        gct = gc.T
        gc_ref[...] = gc
        beta_ref[...] = beta
        mats, rhs = [], []
        for j in range(C_QK_HEADS):
            js = slice(j * C_DK, (j + 1) * C_DK)
            kf, qf = k_ref[:, js], q_ref[:, js] * scale
            kb, qb = kf.astype(BF16), qf.astype(BF16)
            kk = lax.dot_general(kb, kb, _NT, preferred_element_type=F32)
            qk = lax.dot_general(qb, kb, _NT, preferred_element_type=F32)
            for h in (2 * j, 2 * j + 1):
                hs = slice(h * C_DK, (h + 1) * C_DK)
                gcol, bcol, decay, kkd, attn, glast = _chunk_common(kk, qk, gc, gct, beta, h, tri, strict)
                gexp = jnp.exp(gcol)
                mats.append(kkd * bcol)
                rhs.append(jnp.concatenate([v_ref[:, hs] * bcol, kf * (bcol * gexp)], axis=1))
                qd_ref[:, hs] = (qf * gexp).astype(BF16)
                kd_ref[:, hs] = (kf * jnp.exp(glast - gcol)).astype(BF16)
                attn_ref[:, h * CHUNK:(h + 1) * CHUNK] = attn.astype(BF16)
        for h, (tmat, r) in enumerate(zip(_unit_lower_inverses(mats), rhs)):
            hs = slice(h * C_DK, (h + 1) * C_DK)
            uw = _dot_hi(tmat, r)
            u_ref[:, hs] = uw[:, :C_DK]
            w_ref[:, hs] = uw[:, C_DK:]
            tmat_ref[:, h * CHUNK:(h + 1) * CHUNK] = tmat

    def blk(w):
        return pl.BlockSpec((CHUNK, w), lambda n: (n, 0))

    row = pl.BlockSpec((1, LANES), lambda n: (0, 0))
    big = jax.ShapeDtypeStruct((t, C_VW), F32)
    sq = jax.ShapeDtypeStruct((t, C_V_HEADS * CHUNK), F32)
    tile = jax.ShapeDtypeStruct((t, LANES), F32)
    half = jax.ShapeDtypeStruct((t, C_VW), BF16)
    return pl.pallas_call(
        body, name="delta_prep", grid=(nc,),
        in_specs=[blk(C_QKW), pl.BlockSpec((CHUNK, C_QKW), lambda n: (n, 1)), blk(C_VW), _cab_spec(), row, row],
        out_specs=[blk(C_VW)] * 4 + [blk(C_V_HEADS * CHUNK)] * 2 + [blk(LANES)] * 2,
        out_shape=[big, big, half, half, jax.ShapeDtypeStruct(sq.shape, BF16), sq] + [tile] * 2,
        compiler_params=_cparams(("parallel",)),
    )(qk, qk, v, u, alog, dtb)


def _delta_scan_call(u, w, qd, kd, attn, gc):
    t = u.shape[0]
    nc = t // CHUNK

    def body(u_ref, w_ref, qd_ref, kd_ref, attn_ref, gc_ref, o_ref, vn_ref, st_ref, s_ref):
        @pl.when(pl.program_id(0) == 0)
        def _():
            s_ref[...] = jnp.zeros_like(s_ref)

        hss = [slice(h * C_DK, (h + 1) * C_DK) for h in range(C_V_HEADS)]
        states = [s_ref[hs, :] for hs in hss]
        for hs, s in zip(hss, states):
            st_ref[0, hs, :] = s
        sbs = [s.astype(BF16) for s in states]
        vns = [u_ref[:, hs] - jnp.dot(w_ref[:, hs].astype(BF16), sb, preferred_element_type=F32)
               for hs, sb in zip(hss, sbs)]
        qss = [jnp.dot(qd_ref[:, hs].astype(BF16), sb, preferred_element_type=F32) for hs, sb in zip(hss, sbs)]
        vnbs = [vn.astype(BF16) for vn in vns]
        for h, hs in enumerate(hss):
            vn_ref[:, hs] = vnbs[h]
            o_ref[:, hs] = qss[h] + jnp.dot(attn_ref[:, h * CHUNK:(h + 1) * CHUNK].astype(BF16), vnbs[h],
                                            preferred_element_type=F32)
        for h, hs in enumerate(hss):
            glast = jnp.exp(gc_ref[CHUNK - 1:CHUNK, h:h + 1])
            s_ref[hs, :] = states[h] * glast + lax.dot_general(kd_ref[:, hs].astype(BF16), vnbs[h], _TN,
                                                               preferred_element_type=F32)

    def blk(wd):
        return pl.BlockSpec((CHUNK, wd), lambda n: (n, 0))

    big = jax.ShapeDtypeStruct((t, C_VW), F32)
    return pl.pallas_call(
        body, name="delta_scan", grid=(nc,),
        in_specs=[blk(C_VW)] * 4 + [blk(C_V_HEADS * CHUNK), blk(LANES)],
        out_specs=[blk(C_VW), blk(C_VW), pl.BlockSpec((1, C_VW, C_DK), lambda n: (n, 0, 0))],
        out_shape=[big, jax.ShapeDtypeStruct((t, C_VW), BF16), jax.ShapeDtypeStruct((nc, C_VW, C_DK), F32)],
        scratch_shapes=[pltpu.VMEM((C_VW, C_DK), F32)],
        compiler_params=_cparams(("arbitrary",)),
    )(u, w, qd, kd, attn, gc)


def _delta_scan_bwd_call(do, w, qd, kd, attn, gc, vn, st):
    t = do.shape[0]
    nc = t // CHUNK

    def body(do_ref, w_ref, qd_ref, kd_ref, attn_ref, gc_ref, vn_ref, st_ref,
             du_ref, dw_ref, dqd_ref, dkd_ref, dattn_ref, dgl_ref, ds_ref):
        @pl.when(pl.program_id(0) == 0)
        def _():
            ds_ref[...] = jnp.zeros_like(ds_ref)

        tri, _ = _tri_masks()
        row = lax.broadcasted_iota(jnp.int32, (8, LANES), 0)
        lane = lax.broadcasted_iota(jnp.int32, (8, LANES), 1)
        dgl = jnp.zeros((8, LANES), F32)
        hss = [slice(h * C_DK, (h + 1) * C_DK) for h in range(C_V_HEADS)]
        css = [slice(h * CHUNK, (h + 1) * CHUNK) for h in range(C_V_HEADS)]

        def dg(a, b, dims):
            return lax.dot_general(a, b, dims, preferred_element_type=F32)

        ss = [st_ref[0, hs, :] for hs in hss]
        dsps = [ds_ref[hs, :] for hs in hss]
        sbs = [s.astype(BF16) for s in ss]
        dspbs = [d.astype(BF16) for d in dsps]
        dobs = [do_ref[:, hs].astype(BF16) for hs in hss]
        vnbs = [vn_ref[:, hs].astype(BF16) for hs in hss]
        dvns = [dg(attn_ref[:, cs].astype(BF16), dob, _TN) + dg(kd_ref[:, hs].astype(BF16), dspb, _NN)
                for hs, cs, dob, dspb in zip(hss, css, dobs, dspbs)]
        for h, hs in enumerate(hss):
            dqd_ref[:, hs] = dg(dobs[h], sbs[h], _NT)
            dkd_ref[:, hs] = dg(vnbs[h], dspbs[h], _NT)
            dattn_ref[:, css[h]] = jnp.where(tri, dg(dobs[h], vnbs[h], _NT), 0.0)
        dvnbs = [d.astype(BF16) for d in dvns]
        for h, hs in enumerate(hss):
            du_ref[:, hs] = dvns[h]
            dw_ref[:, hs] = -dg(dvnbs[h], sbs[h], _NT)
            tot = jnp.sum(jnp.sum(dsps[h] * ss[h], axis=0, keepdims=True), axis=1, keepdims=True)
            dgl = jnp.where(jnp.logical_and(row == 0, lane == h), tot, dgl)
        for h, hs in enumerate(hss):
            glast = jnp.exp(gc_ref[CHUNK - 1:CHUNK, h:h + 1])
            ds_ref[hs, :] = (dg(qd_ref[:, hs].astype(BF16), dobs[h], _TN) + glast * dsps[h]
                             - dg(w_ref[:, hs].astype(BF16), dvnbs[h], _TN))
        dgl_ref[...] = dgl

    def blk(wd):
        return pl.BlockSpec((CHUNK, wd), lambda n: (nc - 1 - n, 0))

    big = jax.ShapeDtypeStruct((t, C_VW), F32)
    return pl.pallas_call(
        body, name="delta_scan_bwd", grid=(nc,),
        in_specs=[blk(C_VW)] * 4 + [blk(C_V_HEADS * CHUNK), blk(LANES), blk(C_VW),
                                    pl.BlockSpec((1, C_VW, C_DK), lambda n: (nc - 1 - n, 0, 0))],
        out_specs=[blk(C_VW)] * 4 + [blk(C_V_HEADS * CHUNK), pl.BlockSpec((8, LANES), lambda n: (nc - 1 - n, 0))],
        out_shape=[big] * 4 + [jax.ShapeDtypeStruct((t, C_V_HEADS * CHUNK), F32),
                               jax.ShapeDtypeStruct((nc * 8, LANES), F32)],
        scratch_shapes=[pltpu.VMEM((C_VW, C_DK), F32)],
        compiler_params=_cparams(("arbitrary",)),
    )(do, w, qd, kd, attn, gc, vn, st)


def _delta_prep_bwd_call(qk, v, proj, alog, dtb, tmat, u, w, gc, beta, du, dw, dqd, dkd, dattn, dgl, du_buf):
    t = qk.shape[0]
    extra, extra_specs, aliases = _du_operands(du_buf, 17)
    nc = t // CHUNK
    scale = C_DK ** -0.5

    def body(q_ref, k_ref, v_ref, cab_ref, alog_ref, dtb_ref, tmat_ref, u_ref, w_ref, gc_ref, beta_ref,
             du_ref, dw_ref, dqd_ref, dkd_ref, dattn_ref, dgl_ref, *outs):
        dcab_ref, dqk_ref, dv_ref, dpar_ref = outs[len(extra):]
        tri, strict = _tri_masks()
        gc, beta = gc_ref[...], beta_ref[...]
        gct = gc.T
        ones = jnp.ones((CHUNK, LANES), F32)
        lane = lax.broadcasted_iota(jnp.int32, (CHUNK, LANES), 1)
        rowi = lax.broadcasted_iota(jnp.int32, (CHUNK, 1), 0)
        dgc_tile = jnp.zeros((CHUNK, LANES), F32)
        db_tile = jnp.zeros((CHUNK, LANES), F32)
        heads = []
        for j in range(C_QK_HEADS):
            js = slice(j * C_DK, (j + 1) * C_DK)
            kf, qf = k_ref[:, js], q_ref[:, js] * scale
            kb, qb = kf.astype(BF16), qf.astype(BF16)
            kk = lax.dot_general(kb, kb, _NT, preferred_element_type=F32)
            qk = lax.dot_general(qb, kb, _NT, preferred_element_type=F32)
            for h in (2 * j, 2 * j + 1):
                heads.append((h, kf, qf, kb, qb) + _chunk_common(kk, qk, gc, gct, beta, h, tri, strict))
        dvks = [_dot_hi(tmat_ref[:, h * CHUNK:(h + 1) * CHUNK],
                        jnp.concatenate([du_ref[:, h * C_DK:(h + 1) * C_DK], dw_ref[:, h * C_DK:(h + 1) * C_DK]], axis=1),
                        _TN) for h in range(C_V_HEADS)]
        das = [-jnp.where(strict, _dot_hi(dvk, jnp.concatenate([u_ref[:, h * C_DK:(h + 1) * C_DK],
                                                                w_ref[:, h * C_DK:(h + 1) * C_DK]], axis=1), _NT), 0.0)
               for h, dvk in enumerate(dvks)]

        def dot(x, y, dims=_NN):
            return lax.dot_general(x, y, dims, preferred_element_type=F32)

        pre = []
        for (h, kf, qf, kb, qb, gcol, bcol, decay, kkd, attn, glast), da in zip(heads, das):
            dattn_h = dattn_ref[:, h * CHUNK:(h + 1) * CHUNK]
            pre.append(((da * decay * bcol).astype(BF16), (dattn_h * decay).astype(BF16),
                        da * kkd * bcol + dattn_h * attn))
        mms = [(dot(dkk, hd[3]), dot(dkk, hd[3], _TN), dot(dqk, hd[4], _TN), dot(dqk, hd[3]),
                _dot_mask(ones, e, _TN, mask_left=False))
               for hd, (dkk, dqk, e) in zip(heads, pre)]
        dq_parts, dk_parts = [], []
        for (h, kf, qf, kb, qb, gcol, bcol, decay, kkd, attn, glast), dvk, da, (_, _, e), mm in zip(
                heads, dvks, das, pre, mms):
            hs = slice(h * C_DK, (h + 1) * C_DK)
            gexp = jnp.exp(gcol)
            fdec = jnp.exp(glast - gcol)
            dvb, dkb = dvk[:, :C_DK], dvk[:, C_DK:]
            dgc = jnp.sum(e, axis=1, keepdims=True) - mm[4][:, :1]
            dk_parts.append(mm[0] + mm[1] + mm[2] + dkb * (bcol * gexp) + dkd_ref[:, hs] * fdec)
            dq_parts.append(mm[3] + dqd_ref[:, hs] * gexp)
            dv_ref[:, hs] = dvb * bcol
            s_kb = jnp.sum(dkb * kf, axis=1, keepdims=True)
            db = (jnp.sum(da * kkd, axis=1, keepdims=True) + jnp.sum(dvb * v_ref[:, hs], axis=1, keepdims=True)
                  + s_kb * gexp)
            rho = jnp.sum(dkd_ref[:, hs] * kf, axis=1, keepdims=True) * fdec
            dgc = (dgc + s_kb * bcol * gexp + jnp.sum(dqd_ref[:, hs] * qf, axis=1, keepdims=True) * gexp - rho)
            last = jnp.sum(rho, axis=0, keepdims=True) + dgl_ref[0:1, h:h + 1] * jnp.exp(glast)
            dgc = dgc + jnp.where(rowi == CHUNK - 1, last, 0.0)
            dgc_tile = jnp.where(lane == h, dgc, dgc_tile)
            db_tile = jnp.where(lane == h, db, db_tile)
        for j in range(C_QK_HEADS):
            dqk_ref[:, j * C_DK:(j + 1) * C_DK] = (dq_parts[2 * j] + dq_parts[2 * j + 1]) * scale
            dqk_ref[:, C_QKW + j * C_DK:C_QKW + (j + 1) * C_DK] = dk_parts[2 * j] + dk_parts[2 * j + 1]
        dg = _dot_mask(jnp.logical_not(strict), dgc_tile)
        alog = alog_ref[...]
        g, _, gate_pre = _gate_tiles(cab_ref[...], alog, dtb_ref[...])
        dca = dg * (-jnp.exp(alog)) * jax.nn.sigmoid(gate_pre)
        dcab_ref[:, :LANES] = (dca + pltpu.roll(db_tile * beta * (1.0 - beta), C_V_HEADS, 1)).astype(BF16)
        dcab_ref[:, LANES:] = jnp.zeros((CHUNK, D_IN_PAD - COL["c_ab"] - LANES), BF16)
        row8 = lax.broadcasted_iota(jnp.int32, (8, LANES), 0)
        par = jnp.where(row8 == 0, jnp.sum(dg * g, axis=0, keepdims=True),
                        jnp.where(row8 == 1, jnp.sum(dca, axis=0, keepdims=True), 0.0))

        @pl.when(pl.program_id(0) == 0)
        def _():
            dpar_ref[...] = par

        @pl.when(pl.program_id(0) > 0)
        def _():
            dpar_ref[...] += par

    def blk(wd):
        return pl.BlockSpec((CHUNK, wd), lambda n: (n, 0))

    row = pl.BlockSpec((1, LANES), lambda n: (0, 0))
    sq = blk(C_V_HEADS * CHUNK)
    tail = D_IN_PAD - COL["c_ab"]
    assert COL["c_ab"] % tail == 0
    return pl.pallas_call(
        body, name="delta_prep_bwd", grid=(nc,),
        in_specs=[blk(C_QKW), pl.BlockSpec((CHUNK, C_QKW), lambda n: (n, 1)), blk(C_VW), _cab_spec(), row, row, sq,
                  blk(C_VW), blk(C_VW),
                  blk(LANES), blk(LANES), blk(C_VW), blk(C_VW), blk(C_VW), blk(C_VW), sq,
                  pl.BlockSpec((8, LANES), lambda n: (n, 0))] + extra_specs,
        out_specs=[pl.BlockSpec((CHUNK, tail), lambda n: (n, COL["c_ab"] // tail)),
                   blk(2 * C_QKW), blk(C_VW), pl.BlockSpec((8, LANES), lambda n: (0, 0))],
        out_shape=[_du_shape(t), jax.ShapeDtypeStruct((t, 2 * C_QKW), F32),
                   jax.ShapeDtypeStruct((t, C_VW), F32), jax.ShapeDtypeStruct((8, LANES), F32)],
        input_output_aliases=aliases,
        compiler_params=_cparams(("arbitrary",)),
    )(qk, qk, v, proj, alog, dtb, tmat, u, w, gc, beta, du, dw, dqd, dkd, dattn, dgl, *extra)


def _z_spec(tq):
    return pl.BlockSpec((tq, C_VW), lambda i: (i, COL["c_z"] // C_VW))


def _gated_norm_fwd_call(o, u, gain, *, tq=256):
    t, w = o.shape

    def body(o_ref, z_ref, g_ref, y_ref):
        act, _ = _silu_parts(z_ref[...])
        gv = g_ref[...]
        for h in range(C_V_HEADS):
            hs = slice(h * C_DK, (h + 1) * C_DK)
            ov = o_ref[:, hs]
            r = lax.rsqrt(jnp.mean(ov * ov, axis=1, keepdims=True) + EPS)
            y_ref[:, hs] = ov * r * gv * act[:, hs]

    blk = pl.BlockSpec((tq, w), lambda i: (i, 0))
    return pl.pallas_call(
        body, name="gated_norm_fwd", grid=(t // tq,),
        in_specs=[blk, _z_spec(tq), pl.BlockSpec((1, C_DK), lambda i: (0, 0))], out_specs=blk,
        out_shape=jax.ShapeDtypeStruct((t, w), F32),
        compiler_params=_cparams(("parallel",)),
    )(o, u, gain)


def _gated_norm_bwd_call(o, u, gain, dy, du_buf, *, tq=256):
    t, w = o.shape
    nt = t // tq
    extra, extra_specs, aliases = _du_operands(du_buf, 4)

    def body(o_ref, z_ref, g_ref, dy_ref, *refs):
        dz_ref, do_ref, dg_ref, acc_ref = refs[len(extra):]
        i = pl.program_id(0)
        act, dact = _silu_parts(z_ref[...])
        gv = g_ref[...]
        part = jnp.zeros((8, C_DK), F32)
        for h in range(C_V_HEADS):
            hs = slice(h * C_DK, (h + 1) * C_DK)
            ov = o_ref[:, hs]
            r = lax.rsqrt(jnp.mean(ov * ov, axis=1, keepdims=True) + EPS)
            xh = ov * r
            dyv = dy_ref[:, hs]
            dn = dyv * act[:, hs]
            dz_ref[:, hs] = (dyv * xh * gv * dact[:, hs]).astype(BF16)
            dxh = dn * gv
            do_ref[:, hs] = r * (dxh - xh * jnp.mean(dxh * xh, axis=1, keepdims=True))
            part = part + jnp.sum((dn * xh).reshape(tq // 8, 8, C_DK), axis=0)

        @pl.when(i == 0)
        def _():
            acc_ref[...] = part

        @pl.when(i > 0)
        def _():
            acc_ref[...] += part

        @pl.when(i == nt - 1)
        def _():
            dg_ref[...] = jnp.sum(acc_ref[...], axis=0, keepdims=True)

    blk = pl.BlockSpec((tq, w), lambda i: (i, 0))
    grow = pl.BlockSpec((1, C_DK), lambda i: (0, 0))
    return pl.pallas_call(
        body, name="gated_norm_bwd", grid=(nt,),
        in_specs=[blk, _z_spec(tq), grow, blk] + extra_specs, out_specs=[_z_spec(tq), blk, grow],
        out_shape=[_du_shape(t), jax.ShapeDtypeStruct((t, w), F32), jax.ShapeDtypeStruct((1, C_DK), F32)],
        scratch_shapes=[pltpu.VMEM((8, C_DK), F32)],
        input_output_aliases=aliases,
        compiler_params=_cparams(("arbitrary",)),
    )(o, u, gain, dy, *extra)


def _gate_specs(tq):
    return [pl.BlockSpec((tq, D_MODEL), lambda i, j=j: (i, j)) for j in range(3)]


def _merge_fwd_call(ps, u, *, tq=256):
    t, w = ps[0].shape

    def body(p0, p1, p2, g0, g1, g2, y_ref):
        y_ref[...] = (jax.nn.sigmoid(g0[...]) * p0[...] + jax.nn.sigmoid(g1[...]) * p1[...]
                      + jax.nn.sigmoid(g2[...]) * p2[...]).astype(BF16)

    blk = pl.BlockSpec((tq, w), lambda i: (i, 0))
    return pl.pallas_call(
        body, name="merge_fwd", grid=(t // tq,), in_specs=[blk] * 3 + _gate_specs(tq), out_specs=blk,
        out_shape=jax.ShapeDtypeStruct((t, w), BF16),
        compiler_params=_cparams(("parallel",)),
    )(*ps, u, u, u)


def _merge_bwd_call(ps, u, dy, *, tq=256):
    t, w = dy.shape

    def body(p0, p1, p2, g0, g1, g2, dy_ref, dg_ref, dp0, dp1, dp2):
        dyv = dy_ref[...]
        for j, (p, g, dp) in enumerate(((p0, g0, dp0), (p1, g1, dp1), (p2, g2, dp2))):
            sig = jax.nn.sigmoid(g[...])
            dp[...] = (dyv * sig).astype(BF16)
            dg_ref[:, j * w:(j + 1) * w] = (dyv * p[...] * sig * (1.0 - sig)).astype(BF16)

    blk = pl.BlockSpec((tq, w), lambda i: (i, 0))
    small = jax.ShapeDtypeStruct((t, w), BF16)
    return pl.pallas_call(
        body, name="merge_bwd", grid=(t // tq,), in_specs=[blk] * 3 + _gate_specs(tq) + [blk],
        out_specs=[pl.BlockSpec((tq, 3 * w), lambda i: (i, 0))] + [blk] * 3,
        out_shape=[_du_shape(t)] + [small] * 3,
        compiler_params=_cparams(("parallel",)),
    )(*ps, u, u, u, dy)


A_PARTS = ((COL["a_q"], 2 * A_W, True), (COL["a_v"], A_W, False))
B_PARTS = ((COL["b_q"], A_W, True), (COL["b_k"], B_KVW, True), (COL["b_v"], B_KVW, False))
BRANCHES = ("w_branch_a", "w_branch_b", "w_branch_c")


def _layer_fwd(x, tabs, p, wb):
    h = _rms_fwd_call(x, p["norm_mix"], name="rms_mix_fwd", out_dtype=BF16)
    u = _mm(h, wb["w_in"], bias=p["b_in"], tn=IN_TN, name="in_proj_fwd")
    qkv_a = _rope_gather_call(u, tabs, A_PARTS, name="rope_a_fwd")
    a_runs = [_attn_fwd(cfg, qkv_a, None) for cfg in ATTN_A_CFGS]
    os_, lses = tuple(r[0] for r in a_runs), tuple(r[1] for r in a_runs)
    ya = _combine_fwd_call(os_, lses)
    qkv_b = _rope_gather_call(u, tabs, B_PARTS, name="rope_b_fwd")
    yb, _, b_saved = _attn_fwd(ATTN_B_CFG, qkv_b, p["sinks"])
    zc, qk, v = _conv_prep_fwd_call(u, p["conv_w"])
    uu, ww, qd, kd, attn, tmat, gc, beta = _delta_prep_call(qk, v, u, p["a_log"], p["dt_bias"])
    o, vn, st = _delta_scan_call(uu, ww, qd, kd, attn, gc)
    yc = _gated_norm_fwd_call(o, u, p["c_norm"])
    ys = (ya, yb, yc)
    ps = tuple(_mm(y, wb[n], name="branch_fwd") for y, n in zip(ys, BRANCHES))
    merged = _merge_fwd_call(ps, u)
    x1 = _mm(merged, wb["w_out"], add=x, name="out_proj_fwd")
    h2 = _rms_fwd_call(x1, p["norm_ffn"], name="rms_ffn_fwd", out_dtype=BF16)
    pre, act = _mm(h2, wb["w_ff1"], relu2_out=True, name="ffn_up")
    x2 = _mm(act, wb["w_ff2"], add=x1, name="ffn_down")
    saved = dict(x=x, h=h, u=u, a_saved=[r[2] for r in a_runs], os_=os_, lses=lses, b_saved=b_saved,
                 zc=zc, qk=qk, v=v, delta=(tmat, uu, ww, gc, beta, qd, kd, attn, vn, st), o=o, ys=ys, ps=ps,
                 merged=merged, x1=x1, h2=h2, pre=pre, act=act)
    return x2, saved


def _layer_bwd(s, dx2, tabs, p, wb):
    g = {}
    t = dx2.shape[0]
    dpre = _mm(dx2, wb["w_ff2"], tb=True, mul_drelu2=s["pre"], out_dtype=BF16, name="ffn_dpre")
    g["w_ff2"] = _mm(s["act"], dx2, ta=True, tk=1024, name="ffn_dw2")
    g["w_ff1"] = _mm(s["h2"], dpre, ta=True, tk=1024, name="ffn_dw1")
    dh2 = _mm(dpre, wb["w_ff1"], tb=True, name="ffn_dh")
    dx1, g["norm_ffn"] = _rms_bwd_call(s["x1"], p["norm_ffn"], dh2, add=dx2, name="rms_ffn_bwd")
    dmerged = _mm(dx1, wb["w_out"], tb=True, name="out_proj_da")
    g["w_out"] = _mm(s["merged"], dx1, ta=True, tk=1024, name="out_proj_dw")
    du, *dps = _merge_bwd_call(s["ps"], s["u"], dmerged)
    dys = []
    for y, dp, n in zip(s["ys"], dps, BRANCHES):
        dys.append(_mm(dp, wb[n], tb=True, name="branch_da"))
        g[n] = _mm(y, dp, ta=True, tk=1024, name="branch_dw")
    dya, dyb, dyc = dys
    tmat, uu, ww, gc, beta, qd, kd, attn, vn, st = s["delta"]
    du, do, g["c_norm"] = _gated_norm_bwd_call(s["o"], s["u"], p["c_norm"], dyc, du)
    ddu, ddw, dqd, dkd, dattn, dgl = _delta_scan_bwd_call(do, ww, qd, kd, attn, gc, vn, st)
    du, dqk, dv, dpar = _delta_prep_bwd_call(s["qk"], s["v"], s["u"], p["a_log"], p["dt_bias"], tmat, uu, ww, gc,
                                             beta, ddu, ddw, dqd, dkd, dattn, dgl, du)
    g["a_log"], g["dt_bias"] = dpar[0:1], dpar[1:2]
    dzc = _conv_prep_dz_call(s["zc"], dqk, dv)
    du, dconv = _conv_bwd_call(s["u"], dzc, p["conv_w"], du)
    g["conv_w"] = dconv[:C_CONV]
    no_dlse = jnp.zeros((t, LANES), F32)
    dq, dk, dv_b, dsink = _attn_bwd(ATTN_B_CFG, s["b_saved"], p["sinks"], dyb, no_dlse)
    g["sinks"] = dsink[0, :p["sinks"].shape[0]]
    du = _rope_scatter_call(du, t, [([dq], A_W, True)], COL["b_q"], tabs, name="rope_bq_bwd")
    du = _rope_scatter_call(du, t, [([dk], B_KVW, True), ([dv_b], B_KVW, False)], COL["b_k"], tabs,
                            name="rope_bkv_bwd")
    *dos, dl0, dl1, dl2 = _combine_bwd_call(s["os_"], s["lses"], dya)
    grads_a = [_attn_bwd(cfg, sv, None, do_c, dl)[:3]
               for cfg, sv, do_c, dl in zip(ATTN_A_CFGS, s["a_saved"], dos, (dl0, dl1, dl2))]
    dqs, dks, dvs = zip(*grads_a)
    du = _rope_scatter_call(du, t, [(list(dqs), A_W, True), (list(dks), A_W, True), (list(dvs), A_W, False)],
                            COL["a_q"], tabs, name="rope_a_bwd")
    dh = _mm(du, wb["w_in"], tb=True, tk=IN_TN, name="in_proj_da")
    g["w_in"] = _mm(s["h"], du, ta=True, tn=IN_TN, tk=1024, name="in_proj_dw")
    g["b_in"] = _colsum(du, name="in_proj_db", tn=IN_TN)
    dx, g["norm_mix"] = _rms_bwd_call(s["x"], p["norm_mix"], dh, add=dx1, name="rms_mix_bwd")
    return dx, g


def _local_step(x, params, wbf, tabs, tgt):
    def of_layer(d, layer):
        return {n: w[layer] for n, w in d.items() if n != "norm_final"}

    saves = []
    for layer in range(DEPTH):
        x, s = _layer_fwd(x, tabs, of_layer(params, layer), of_layer(wbf, layer))
        saves.append(s)
    loss = _loss_fwd_call(x, params["norm_final"], tgt)
    dx, dfinal = _loss_bwd_call(x, params["norm_final"], tgt, jnp.ones((1, 1), F32))
    per_layer = []
    for layer in reversed(range(DEPTH)):
        dx, g = _layer_bwd(saves[layer], dx, tabs, of_layer(params, layer), of_layer(wbf, layer))
        per_layer.append(g)
    per_layer.reverse()
    grads = {n: jnp.stack([g[n] for g in per_layer]) for n in per_layer[0]}
    grads["norm_final"] = dfinal
    return loss, dx, grads


def _in_cols_to_kernel(w):
    lead = w.shape[:-1]
    parts, pos = [], 0
    for _, start, width, ref_start in IN_LAYOUT:
        if start > pos:
            parts.append(jnp.zeros(lead + (start - pos,), w.dtype))
        parts.append(w[..., ref_start:ref_start + width])
        pos = start + width
    parts.append(jnp.zeros(lead + (D_IN_PAD - pos,), w.dtype))
    return jnp.concatenate(parts, axis=-1)


def _in_cols_to_reference(w):
    by_ref = sorted(IN_LAYOUT, key=lambda e: e[3])
    return jnp.concatenate([w[..., start:start + width] for _, start, width, _ in by_ref], axis=-1)


def _pad_lanes(v):
    return jnp.pad(v, ((0, 0), (0, LANES - v.shape[1])))[:, None, :]


BIG = (("w_in", 2), ("conv_w", 2), ("w_branch_a", 2), ("w_branch_b", 2), ("w_branch_c", 1), ("w_out", 1),
       ("w_ff1", 2), ("w_ff2", 1))
SMALL = ("norm_mix", "b_in", "a_log", "dt_bias", "sinks", "c_norm", "norm_ffn", "norm_final")
WEIGHTS = ("norm_mix", "w_in", "b_in", "conv_w", "a_log", "dt_bias", "sinks", "c_norm", "w_branch_a",
           "w_branch_b", "w_branch_c", "w_out", "norm_ffn", "w_ff1", "w_ff2", "norm_final")
MATMUL_WEIGHTS = ("w_in", "w_branch_a", "w_branch_b", "w_branch_c", "w_out", "w_ff1", "w_ff2")
PACK_ROWS = 1024
ROW_ALIGN = 16


def _seg_rows(n):
    return -(-n // (LANES * ROW_ALIGN)) * ROW_ALIGN


def _pack(arrays, lead=0):
    parts = []
    for a in arrays:
        lead_shape = a.shape[:lead]
        n = math.prod(a.shape[lead:])
        rows = _seg_rows(n)
        if rows * LANES != n:
            a = jnp.pad(a.reshape(lead_shape + (n,)), [(0, 0)] * lead + [(0, rows * LANES - n)])
        parts.append(a.reshape(lead_shape + (rows, LANES)))
    total = sum(p.shape[lead] for p in parts)
    padded = -(-total // PACK_ROWS) * PACK_ROWS
    if padded > total:
        parts.append(jnp.zeros(parts[0].shape[:lead] + (padded - total, LANES), parts[0].dtype))
    return jnp.concatenate(parts, axis=lead)


def _unpack(buf, shapes):
    lead = buf.shape[:-2]
    out, pos = [], 0
    for shp in shapes:
        n = math.prod(shp)
        rows = _seg_rows(n)
        seg = buf[..., pos:pos + rows, :]
        if rows * LANES != n:
            seg = seg.reshape(lead + (rows * LANES,))[..., :n]
        out.append(seg.reshape(lead + tuple(shp)))
        pos += rows
    return out


def _shards_to_full(blocks, axis):
    moved = jnp.moveaxis(blocks, 0, axis)
    shp = list(blocks.shape[1:])
    shp[axis] = shp[axis] * N_DEV
    return moved.reshape(shp)


def _full_to_shards(full, axis):
    shp = list(full.shape)
    shp[axis:axis + 1] = [N_DEV, shp[axis] // N_DEV]
    return jnp.moveaxis(full.reshape(shp), axis, 0)


def _my_place():
    return lax.axis_index("x"), lax.axis_index("y"), lax.axis_index("c")


def _slot(x, y, c):
    return 4 * x + 2 * y + c


def _all_gather(block, *, name):
    rows = block.shape[0]

    def body(x_ref, out_ref, send_sems, recv_sems, local_sem):
        x, y, c = _my_place()
        me, sibling = (x, y, c), (x, y, 1 - c)
        chips = [(1 - x, y), (x, 1 - y), (1 - x, 1 - y)]

        def copy(k, blk, to, src=None):
            dst = out_ref.at[_slot(*blk)]
            return pltpu.make_async_remote_copy(
                src_ref=dst if src is None else src, dst_ref=dst,
                send_sem=send_sems.at[k], recv_sem=recv_sems.at[k], device_id=to, device_id_type=MESH_ID)

        mine = pltpu.make_async_copy(x_ref, out_ref.at[_slot(*me)], local_sem)
        mine.start()
        first = [copy(0, me, sibling, src=x_ref)]
        first += [copy(1 + j, me, (*chip, c), src=x_ref) for j, chip in enumerate(chips)]
        for cp in first:
            cp.start()
        passed = [copy(4 + j, (*chip, c), sibling) for j, chip in enumerate(chips)]
        for j, chip in enumerate(chips):
            copy(1 + j, (*chip, c), me).wait_recv()
            passed[j].start()
        copy(0, sibling, me).wait_recv()
        for j, chip in enumerate(chips):
            copy(4 + j, (*chip, 1 - c), me).wait_recv()
        for cp in first + passed:
            cp.wait_send()
        mine.wait()

    return pl.pallas_call(
        body, name=name,
        out_shape=jax.ShapeDtypeStruct((N_DEV, rows, LANES), block.dtype),
        in_specs=[HBM_SPEC], out_specs=HBM_SPEC,
        scratch_shapes=[pltpu.SemaphoreType.DMA((7,)), pltpu.SemaphoreType.DMA((7,)), pltpu.SemaphoreType.DMA],
    )(block)


N_CHIP = N_DEV // 2


def _swap_with_sibling(blocks, *, name):
    def body(g_ref, out_ref, send_sem, recv_sem):
        x, y, c = _my_place()
        cp = pltpu.make_async_remote_copy(src_ref=g_ref, dst_ref=out_ref, send_sem=send_sem, recv_sem=recv_sem,
                                          device_id=(x, y, 1 - c), device_id_type=MESH_ID)
        cp.start()
        cp.wait_recv()
        cp.wait_send()

    return pl.pallas_call(
        body, name=name,
        out_shape=jax.ShapeDtypeStruct(blocks.shape, blocks.dtype),
        in_specs=[HBM_SPEC], out_specs=HBM_SPEC,
        scratch_shapes=[pltpu.SemaphoreType.DMA, pltpu.SemaphoreType.DMA],
    )(blocks)


def _chip_all_to_all(blocks, *, name):
    def body(g_ref, out_ref, send_sems, recv_sems, local_sem):
        x, y, c = _my_place()
        mine_slot = 2 * x + y
        local = pltpu.make_async_copy(g_ref.at[mine_slot], out_ref.at[mine_slot], local_sem)
        local.start()
        copies = []
        for k in range(1, N_CHIP):
            px, py = x ^ (k >> 1), y ^ (k & 1)
            copies.append(pltpu.make_async_remote_copy(
                src_ref=g_ref.at[2 * px + py], dst_ref=out_ref.at[mine_slot],
                send_sem=send_sems.at[k - 1], recv_sem=recv_sems.at[k - 1],
                device_id=(px, py, c), device_id_type=MESH_ID))
        for cp in copies:
            cp.start()
        for cp in copies:
            cp.wait_recv()
        for cp in copies:
            cp.wait_send()
        local.wait()

    return pl.pallas_call(
        body, name=name,
        out_shape=jax.ShapeDtypeStruct(blocks.shape, blocks.dtype),
        in_specs=[HBM_SPEC], out_specs=HBM_SPEC,
        scratch_shapes=[pltpu.SemaphoreType.DMA((N_CHIP - 1,)), pltpu.SemaphoreType.DMA((N_CHIP - 1,)),
                        pltpu.SemaphoreType.DMA],
    )(blocks)


def _add_bf16_call(a, b, *, name):
    n, rows, _ = a.shape
    tr = min(PACK_ROWS, rows)

    def body(a_ref, b_ref, o_ref):
        o_ref[...] = (a_ref[...].astype(F32) + b_ref[...].astype(F32)).astype(BF16)

    blk = pl.BlockSpec((n, tr, LANES), lambda i: (0, i, 0))
    return pl.pallas_call(
        body, name=name, grid=(rows // tr,), in_specs=[blk, blk], out_specs=blk,
        out_shape=jax.ShapeDtypeStruct(a.shape, BF16),
        compiler_params=_cparams(("parallel",)),
    )(a, b)


def _adamw_call(parts, w, m, v, *, name):
    rows = w.shape[0]
    tr = min(PACK_ROWS, rows)
    assert rows % tr == 0
    n_parts = parts.shape[0]

    def body(p_ref, w_ref, m_ref, v_ref, g_ref, d_ref, nm_ref, nv_ref):
        g = p_ref[0].astype(F32)
        for s in range(1, n_parts):
            g = g + p_ref[s].astype(F32)
        nm = ADAM_B1 * m_ref[...] + (1.0 - ADAM_B1) * g
        nv = ADAM_B2 * v_ref[...] + (1.0 - ADAM_B2) * jnp.square(g)
        m_hat = nm / (1.0 - ADAM_B1 ** ADAM_STEP)
        v_hat = nv / (1.0 - ADAM_B2 ** ADAM_STEP)
        g_ref[...] = g
        nm_ref[...] = nm
        nv_ref[...] = nv
        d_ref[...] = -ADAM_LR * (m_hat / (jnp.sqrt(v_hat) + ADAM_EPS) + ADAM_WD * w_ref[...])

    blk = pl.BlockSpec((tr, LANES), lambda i: (i, 0))
    shape = jax.ShapeDtypeStruct((rows, LANES), F32)
    return pl.pallas_call(
        body, name=name, grid=(rows // tr,),
        in_specs=[pl.BlockSpec((n_parts, tr, LANES), lambda i: (0, i, 0)), blk, blk, blk],
        out_specs=[blk] * 4, out_shape=[shape] * 4,
        compiler_params=_cparams(("parallel",)),
    )(parts, w, m, v)


def _kernel_params(full):
    return {
        "norm_mix": full["norm_mix"][:, None, :],
        "b_in": _in_cols_to_kernel(full["b_in"])[:, None, :],
        "conv_w": full["conv_w"],
        "a_log": _pad_lanes(full["a_log"]),
        "dt_bias": _pad_lanes(full["dt_bias"]),
        "sinks": full["sinks"],
        "c_norm": full["c_norm"][:, None, :],
        "norm_ffn": full["norm_ffn"][:, None, :],
        "norm_final": full["norm_final"][None, :],
    }


def _reference_grads(g):
    return {
        "norm_mix": g["norm_mix"][:, 0, :],
        "w_in": _in_cols_to_reference(g["w_in"]),
        "b_in": _in_cols_to_reference(g["b_in"][:, 0, :]),
        "conv_w": g["conv_w"],
        "a_log": g["a_log"][:, 0, :C_V_HEADS],
        "dt_bias": g["dt_bias"][:, 0, :C_V_HEADS],
        "sinks": g["sinks"],
        "c_norm": g["c_norm"][:, 0, :],
        "w_branch_a": g["w_branch_a"], "w_branch_b": g["w_branch_b"], "w_branch_c": g["w_branch_c"],
        "w_out": g["w_out"],
        "norm_ffn": g["norm_ffn"][:, 0, :],
        "w_ff1": g["w_ff1"], "w_ff2": g["w_ff2"],
        "norm_final": g["norm_final"][0],
    }


def kernel(x, positions, norm_mix, w_in, b_in, conv_w, a_log, dt_bias, sinks, c_norm, w_branch_a, w_branch_b, w_branch_c, w_out, norm_ffn, w_ff1, w_ff2, norm_final, loss_target, m_norm_mix, m_w_in, m_b_in, m_conv_w, m_a_log, m_dt_bias, m_sinks, m_c_norm, m_w_branch_a, m_w_branch_b, m_w_branch_c, m_w_out, m_norm_ffn, m_w_ff1, m_w_ff2, m_norm_final, v_norm_mix, v_w_in, v_b_in, v_conv_w, v_a_log, v_dt_bias, v_sinks, v_c_norm, v_w_branch_a, v_w_branch_b, v_w_branch_c, v_w_out, v_norm_ffn, v_w_ff1, v_w_ff2, v_norm_final):
    env = dict(locals())
    weights = {n: env[n] for n in WEIGHTS}
    moments_m = {n: env["m_" + n] for n in WEIGHTS}
    moments_v = {n: env["v_" + n] for n in WEIGHTS}

    axis_of = dict(BIG)
    cw = weights["conv_w"]
    c1 = cw.astype(BF16)
    c2 = (cw - c1.astype(F32)).astype(BF16)
    c3 = (cw - c1.astype(F32) - c2.astype(F32)).astype(BF16)
    gathered = _all_gather(_pack([weights[n].astype(BF16) for n in MATMUL_WEIGHTS] + [c1, c2, c3]),
                           name="gather_weights")
    blocks = _unpack(gathered, [weights[n].shape for n in MATMUL_WEIGHTS] + [cw.shape] * 3)
    wbf = {n: _shards_to_full(blk, axis_of[n]) for n, blk in zip(MATMUL_WEIGHTS, blocks)}
    wbf["w_in"] = _in_cols_to_kernel(wbf["w_in"])
    full = {n: weights[n] for n in SMALL}
    full["conv_w"] = _shards_to_full(sum(b.astype(F32) for b in blocks[-3:]), axis_of["conv_w"])

    tabs = rope_tables(positions[0])
    loss, dx, dparams = _local_step(x[0], _kernel_params(full), wbf, tabs, loss_target[0])
    grads = _reference_grads(dparams)
    loss = lax.psum(loss, ("x", "y", "c"))

    core = lax.axis_index("c")

    def by_core(n, axis, which):
        sh = _full_to_shards(grads[n], axis)
        sh = sh.reshape((N_CHIP, 2) + sh.shape[1:])
        return lax.dynamic_index_in_dim(sh, which, axis=1, keepdims=False).astype(BF16)

    from_sibling = _swap_with_sibling(_pack([by_core(n, axis, 1 - core) for n, axis in BIG], lead=1),
                                      name="scatter_grads_d2d")
    chip_sum = _add_bf16_call(_pack([by_core(n, axis, core) for n, axis in BIG], lead=1), from_sibling,
                              name="scatter_grads_add")
    big_parts = _chip_all_to_all(chip_sum, name="scatter_grads_ici")
    small_parts = _all_gather(_pack([grads[n] for n in SMALL]), name="gather_small_grads")

    out = {}
    for names, parts in (([n for n, _ in BIG], big_parts), (list(SMALL), small_parts)):
        shapes = [weights[n].shape for n in names]
        packed = [_pack([d[n] for n in names]) for d in (weights, moments_m, moments_v)]
        results = _adamw_call(parts, *packed, name="adamw_" + names[0])
        for kind, buf in zip(("grad", "delta", "new_m", "new_v"), results):
            for n, arr in zip(names, _unpack(buf, shapes)):
                out[kind, n] = arr
    return (loss, dx[None], *[out[kind, n] for kind in ("grad", "delta", "new_m", "new_v") for n in WEIGHTS])
```

```python
import math

import jax
import jax.numpy as jnp
from jax import lax
from jax.experimental import pallas as pl
from jax.experimental.pallas import tpu as pltpu

F32 = jnp.float32
BF16 = jnp.bfloat16

N_DEV = 8
D_MODEL = 1024
DEPTH = 2
HEAD_DIM = 64
ROT_DIM = 16
ROPE_THETA = 500000.0
BLK = 128
NEG_INF = -1e30
EPS = 1e-6
A_CONFIGS = ((128, 1), (512, 4), (2048, 16))
B_GROUP = 4
C_QK_HEADS = 4
C_V_HEADS = 8
C_DK = 128
C_CONV = 4
CHUNK = 64
ADAM_LR = 0.001
ADAM_B1 = 0.9
ADAM_B2 = 0.999
ADAM_EPS = 1e-08
ADAM_WD = 0.01
ADAM_STEP = 10

IN_LAYOUT = (
    ("gate_a", 0, 1024, 5392), ("gate_b", 1024, 1024, 6416), ("gate_c", 2048, 1024, 7440),
    ("a_q", 3072, 512, 0), ("a_k", 3584, 512, 512), ("a_v", 4096, 512, 1024), ("b_q", 4608, 512, 1536),
    ("c_z", 5120, 1024, 4352), ("c_qkv", 6144, 2048, 2304),
    ("b_k", 8192, 128, 2048), ("b_v", 8320, 128, 2176), ("c_ab", 8448, 16, 5376),
)
COL = {name: start for name, start, _, _ in IN_LAYOUT}
D_IN_PAD = 8704
IN_TN = D_IN_PAD // 4
LANES = 128
VMEM_LIMIT = 56 * 1024 * 1024


def _cparams(sem=None):
    return pltpu.CompilerParams(dimension_semantics=sem, vmem_limit_bytes=VMEM_LIMIT)


def _relu2(t):
    return jnp.square(jnp.maximum(t, 0.0))


def _mm(a, b, *, ta=False, tb=False, bias=None, a_fn=None, mul_drelu2=None, add=None,
        out_dtype=F32, relu2_out=False, b_colsum=False, tm=1024, tn=1024, tk=2048, name):
    if ta:
        kdim, m = a.shape
    else:
        m, kdim = a.shape
    n = b.shape[0] if tb else b.shape[1]
    tm, tn, tk = min(tm, m), min(tn, n), min(tk, kdim)
    assert m % tm == 0 and n % tn == 0 and kdim % tk == 0, (a.shape, b.shape, tm, tn, tk)
    nk = kdim // tk
    assert not b_colsum or (m == tm and not tb and nk > 1)
    dims = (((0 if ta else 1,), (1 if tb else 0,)), ((), ()))
    extras = [e for e in (bias, mul_drelu2, add) if e is not None]

    def body(*refs):
        a_ref, b_ref = refs[0], refs[1]
        pos = 2
        bias_ref = pre_ref = add_ref = None
        if bias is not None:
            bias_ref = refs[pos]; pos += 1
        if mul_drelu2 is not None:
            pre_ref = refs[pos]; pos += 1
        if add is not None:
            add_ref = refs[pos]; pos += 1
        o_ref = refs[pos]
        pos += 1
        r_ref = None
        if relu2_out:
            r_ref = refs[pos]; pos += 1
        cs_ref = None
        if b_colsum:
            cs_ref = refs[pos]; pos += 1
        acc_ref = refs[pos] if nk > 1 else None
        cs_acc = refs[pos + 1] if b_colsum else None

        av = a_ref[...]
        if a_fn is not None:
            av = a_fn(av)
        bv = b_ref[...]
        part = lax.dot_general(av.astype(BF16), bv.astype(BF16), dims,
                               preferred_element_type=F32)
        if b_colsum:
            cs_part = jnp.sum(bv.astype(F32).reshape(tk // 8, 8, tn), axis=0)

        def finish(acc):
            if bias_ref is not None:
                acc = acc + bias_ref[...]
            if pre_ref is not None:
                acc = acc * (2.0 * jnp.maximum(pre_ref[...], 0.0))
            if add_ref is not None:
                acc = acc + add_ref[...]
            o_ref[...] = acc.astype(out_dtype)
            if r_ref is not None:
                r_ref[...] = _relu2(acc).astype(BF16)

        if nk == 1:
            finish(part)
        else:
            k = pl.program_id(2)

            @pl.when(k == 0)
            def _():
                acc_ref[...] = part
                if b_colsum:
                    cs_acc[...] = cs_part

            @pl.when(k > 0)
            def _():
                acc_ref[...] += part
                if b_colsum:
                    cs_acc[...] += cs_part

            @pl.when(k == nk - 1)
            def _():
                finish(acc_ref[...])
                if b_colsum:
                    cs_ref[...] = jnp.sum(cs_acc[...], axis=0, keepdims=True)

    a_spec = (pl.BlockSpec((tk, tm), lambda i, j, k: (k, i)) if ta
              else pl.BlockSpec((tm, tk), lambda i, j, k: (i, k)))
    b_spec = (pl.BlockSpec((tn, tk), lambda i, j, k: (j, k)) if tb
              else pl.BlockSpec((tk, tn), lambda i, j, k: (k, j)))
    in_specs = [a_spec, b_spec]
    if bias is not None:
        in_specs.append(pl.BlockSpec((1, tn), lambda i, j, k: (0, j)))
    for _ in extras[(1 if bias is not None else 0):]:
        in_specs.append(pl.BlockSpec((tm, tn), lambda i, j, k: (i, j)))
    o_spec = pl.BlockSpec((tm, tn), lambda i, j, k: (i, j))
    out_specs, out_shape = [o_spec], [jax.ShapeDtypeStruct((m, n), out_dtype)]
    scratch = [pltpu.VMEM((tm, tn), F32)] if nk > 1 else []
    if relu2_out:
        out_specs.append(o_spec)
        out_shape.append(jax.ShapeDtypeStruct((m, n), BF16))
    if b_colsum:
        out_specs.append(pl.BlockSpec((1, tn), lambda i, j, k: (0, j)))
        out_shape.append(jax.ShapeDtypeStruct((1, n), F32))
        scratch.append(pltpu.VMEM((8, tn), F32))
    single = len(out_specs) == 1
    return pl.pallas_call(
        body, name=name,
        grid=(m // tm, n // tn, nk),
        in_specs=in_specs,
        out_specs=out_specs[0] if single else out_specs,
        out_shape=out_shape[0] if single else out_shape,
        scratch_shapes=scratch,
        compiler_params=_cparams(("parallel", "parallel", "arbitrary")),
    )(a, b, *extras)


def _rms_fwd_call(x, g, *, name, out_dtype=F32, tq=512):
    t, d = x.shape

    def body(x_ref, g_ref, y_ref):
        xv = x_ref[...]
        r = lax.rsqrt(jnp.mean(xv * xv, axis=-1, keepdims=True) + EPS)
        y_ref[...] = (xv * r * g_ref[...]).astype(out_dtype)

    return pl.pallas_call(
        body, name=name, grid=(t // tq,),
        in_specs=[pl.BlockSpec((tq, d), lambda i: (i, 0)), pl.BlockSpec((1, d), lambda i: (0, 0))],
        out_specs=pl.BlockSpec((tq, d), lambda i: (i, 0)),
        out_shape=jax.ShapeDtypeStruct((t, d), out_dtype),
        compiler_params=_cparams(("parallel",)),
    )(x, g)


def _rms_bwd_call(x, g, dy, *, name, add=None, tq=512):
    t, d = x.shape
    nt = t // tq

    def body(*refs):
        if add is None:
            x_ref, g_ref, dy_ref, dx_ref, dg_ref, acc_ref = refs
        else:
            x_ref, g_ref, dy_ref, add_ref, dx_ref, dg_ref, acc_ref = refs
        i = pl.program_id(0)
        xv = x_ref[...]
        r = lax.rsqrt(jnp.mean(xv * xv, axis=-1, keepdims=True) + EPS)
        xh = xv * r
        dyv = dy_ref[...]
        dxh = dyv * g_ref[...]
        dx = r * (dxh - xh * jnp.mean(dxh * xh, axis=-1, keepdims=True))
        dx_ref[...] = dx if add is None else dx + add_ref[...]
        part = jnp.sum((dyv * xh).reshape(tq // 8, 8, d), axis=0)

        @pl.when(i == 0)
        def _():
            acc_ref[...] = part

        @pl.when(i > 0)
        def _():
            acc_ref[...] += part

        @pl.when(i == nt - 1)
        def _():
            dg_ref[...] = jnp.sum(acc_ref[...], axis=0, keepdims=True)

    blk = pl.BlockSpec((tq, d), lambda i: (i, 0))
    row = pl.BlockSpec((1, d), lambda i: (0, 0))
    extra = [] if add is None else [add]
    return pl.pallas_call(
        body, name=name, grid=(nt,),
        in_specs=[blk, row, blk] + [blk] * len(extra),
        out_specs=[blk, row],
        out_shape=[jax.ShapeDtypeStruct((t, d), F32), jax.ShapeDtypeStruct((1, d), F32)],
        scratch_shapes=[pltpu.VMEM((8, d), F32)],
        compiler_params=_cparams(("arbitrary",)),
    )(x, g, dy, *extra)


def _loss_fwd_call(x, g, tgt, *, tq=512):
    t, d = x.shape
    nt = t // tq

    def body(x_ref, g_ref, t_ref, o_ref, acc_ref):
        i = pl.program_id(0)
        xv = x_ref[...]
        r = lax.rsqrt(jnp.mean(xv * xv, axis=-1, keepdims=True) + EPS)
        err = xv * r * g_ref[...] - t_ref[...]
        part = jnp.sum((err * err).reshape(tq // 8, 8, d), axis=0)

        @pl.when(i == 0)
        def _():
            acc_ref[...] = part

        @pl.when(i > 0)
        def _():
            acc_ref[...] += part

        @pl.when(i == nt - 1)
        def _():
            tot = jnp.sum(jnp.sum(acc_ref[...], axis=0, keepdims=True), axis=1, keepdims=True)
            o_ref[...] = jnp.broadcast_to(tot * (0.5 / d), (8, LANES))

    out = pl.pallas_call(
        body, name="loss_fwd", grid=(nt,),
        in_specs=[pl.BlockSpec((tq, d), lambda i: (i, 0)), pl.BlockSpec((1, d), lambda i: (0, 0)),
                  pl.BlockSpec((tq, d), lambda i: (i, 0))],
        out_specs=pl.BlockSpec((8, LANES), lambda i: (0, 0)),
        out_shape=jax.ShapeDtypeStruct((8, LANES), F32),
        scratch_shapes=[pltpu.VMEM((8, d), F32)],
        compiler_params=_cparams(("arbitrary",)),
    )(x, g, tgt)
    return out[0, 0]


def _loss_bwd_call(x, g, tgt, scale, *, tq=512):
    t, d = x.shape
    nt = t // tq

    def body(x_ref, g_ref, t_ref, s_ref, dx_ref, dg_ref, acc_ref):
        i = pl.program_id(0)
        xv = x_ref[...]
        r = lax.rsqrt(jnp.mean(xv * xv, axis=-1, keepdims=True) + EPS)
        xh = xv * r
        gv = g_ref[...]
        dyv = (xh * gv - t_ref[...]) * (s_ref[...] * (1.0 / d))
        dxh = dyv * gv
        dx_ref[...] = r * (dxh - xh * jnp.mean(dxh * xh, axis=-1, keepdims=True))
        part = jnp.sum((dyv * xh).reshape(tq // 8, 8, d), axis=0)

        @pl.when(i == 0)
        def _():
            acc_ref[...] = part

        @pl.when(i > 0)
        def _():
            acc_ref[...] += part

        @pl.when(i == nt - 1)
        def _():
            dg_ref[...] = jnp.sum(acc_ref[...], axis=0, keepdims=True)

    return pl.pallas_call(
        body, name="loss_bwd", grid=(nt,),
        in_specs=[pl.BlockSpec((tq, d), lambda i: (i, 0)), pl.BlockSpec((1, d), lambda i: (0, 0)),
                  pl.BlockSpec((tq, d), lambda i: (i, 0)), pl.BlockSpec((1, 1), lambda i: (0, 0))],
        out_specs=[pl.BlockSpec((tq, d), lambda i: (i, 0)), pl.BlockSpec((1, d), lambda i: (0, 0))],
        out_shape=[jax.ShapeDtypeStruct((t, d), F32), jax.ShapeDtypeStruct((1, d), F32)],
        scratch_shapes=[pltpu.VMEM((8, d), F32)],
        compiler_params=_cparams(("arbitrary",)),
    )(x, g, tgt, scale)


MESH_ID = pl.DeviceIdType.MESH
HBM_SPEC = pl.BlockSpec(memory_space=pl.ANY)


def rope_tables(positions):
    inv_freq = jnp.power(ROPE_THETA, -jnp.arange(0, ROT_DIM, 2, dtype=F32) / ROT_DIM)
    ang = positions.astype(F32)[:, None] * inv_freq
    cos, sin = jnp.cos(ang), jnp.sin(ang)
    t = positions.shape[0]
    one = jnp.ones((t, HEAD_DIM - ROT_DIM), F32)
    zero8 = jnp.zeros((t, ROT_DIM // 2), F32)
    zero = jnp.zeros((t, HEAD_DIM - ROT_DIM), F32)
    a = jnp.concatenate([cos, cos, one], axis=1)
    b = jnp.concatenate([zero8, sin, zero], axis=1)
    c = jnp.concatenate([-sin, zero8, zero], axis=1)
    return tuple(jnp.concatenate([m, m], axis=1) for m in (a, b, c))


def _rope_chunk(xs, a, b, c, transpose):
    half = ROT_DIM // 2
    if transpose:
        return xs * a + pltpu.roll(xs * b, LANES - half, 1) + pltpu.roll(xs * c, half, 1)
    return xs * a + pltpu.roll(xs, half, 1) * b + pltpu.roll(xs, LANES - half, 1) * c


def _rope_gather_call(u, tabs, parts, *, name, tq=512):
    t = u.shape[0]
    total = sum(w for _, w, _ in parts)
    assert all(start % w == 0 for start, w, _ in parts)

    def body(a_ref, b_ref, c_ref, *refs):
        o_ref = refs[-1]
        a, b, c = a_ref[...], b_ref[...], c_ref[...]
        off = 0
        for x_ref, (_, w, roped) in zip(refs[:-1], parts):
            for j in range(w // LANES):
                xs = x_ref[:, j * LANES:(j + 1) * LANES]
                val = _rope_chunk(xs, a, b, c, False) if roped else xs
                o_ref[:, off + j * LANES:off + (j + 1) * LANES] = val.astype(BF16)
            off += w

    tab_spec = pl.BlockSpec((tq, LANES), lambda i: (i, 0))
    return pl.pallas_call(
        body, name=name, grid=(t // tq,),
        in_specs=[tab_spec] * 3 + [pl.BlockSpec((tq, w), lambda i, cb=start // w: (i, cb)) for start, w, _ in parts],
        out_specs=pl.BlockSpec((tq, total), lambda i: (i, 0)),
        out_shape=jax.ShapeDtypeStruct((t, total), BF16),
        compiler_params=_cparams(("parallel",)),
    )(*tabs, *([u] * len(parts)))


def _du_operands(du_buf, n_inputs):
    if du_buf is None:
        return [], [], {}
    return [du_buf], [HBM_SPEC], {n_inputs: 0}


def _du_shape(t):
    return jax.ShapeDtypeStruct((t, D_IN_PAD), BF16)


def _rope_scatter_call(du_buf, t, pieces, col, tabs, *, name, tq=512):
    total = sum(w for _, w, _ in pieces)
    assert col % total == 0
    arrays = [a for arrs, _, _ in pieces for a in arrs]
    extra, extra_specs, aliases = _du_operands(du_buf, 3 + len(arrays))

    def body(a_ref, b_ref, c_ref, *refs):
        o_ref = refs[len(arrays) + len(extra)]
        a, b, c = a_ref[...], b_ref[...], c_ref[...]
        k = off = 0
        for arrs, w, roped in pieces:
            mine = refs[k:k + len(arrs)]
            k += len(arrs)
            for j in range(w // LANES):
                cs = slice(j * LANES, (j + 1) * LANES)
                xs = mine[0][:, cs].astype(F32)
                for r in mine[1:]:
                    xs = xs + r[:, cs].astype(F32)
                val = _rope_chunk(xs, a, b, c, True) if roped else xs
                o_ref[:, off + j * LANES:off + (j + 1) * LANES] = val.astype(BF16)
            off += w

    tab_spec = pl.BlockSpec((tq, LANES), lambda i: (i, 0))
    in_specs = [tab_spec] * 3 + [pl.BlockSpec((tq, w), lambda i: (i, 0)) for arrs, w, _ in pieces for _ in arrs]
    return pl.pallas_call(
        body, name=name, grid=(t // tq,),
        in_specs=in_specs + extra_specs,
        out_specs=pl.BlockSpec((tq, total), lambda i: (i, col // total)),
        out_shape=_du_shape(t), input_output_aliases=aliases,
        compiler_params=_cparams(("parallel",)),
    )(*tabs, *arrays, *extra)


def _band_masks(first_block, max_dist):
    qi = lax.broadcasted_iota(jnp.int32, (BLK, BLK), 0)
    kj = lax.broadcasted_iota(jnp.int32, (BLK, BLK), 1)
    valid_prev = jnp.logical_and(kj >= qi + (BLK - max_dist), jnp.logical_not(first_block))
    valid_cur = kj <= qi
    return valid_prev, valid_cur


_NN = (((1,), (0,)), ((), ()))
_NT = (((1,), (1,)), ((), ()))
_TN = (((0,), (0,)), ((), ()))


HEAD_STAGE = 8


def _attn_row_maps(nb):
    def cur(i):
        return jnp.minimum(i, nb - 1)

    def prev(i):
        return jnp.maximum(jnp.minimum(i, nb - 1) - 1, 0)

    return cur, prev


def _dil_view(a, dil):
    t, w = a.shape
    return a.reshape(t // dil, dil * w)


def _dil_spec(w, dil, rows, seg=None, off=0):
    seg = w if seg is None else seg
    assert off % w == 0 and (dil == 1 or seg % w == 0)
    return pl.BlockSpec((BLK, w), lambda r, i: (rows(i), (r * seg + off) // w))


def _dil_shape(l, dil, w, dtype=F32):
    return jax.ShapeDtypeStruct((l, dil * w), dtype)


def _attn_fwd_call(qkv2, sink, *, dil, group, max_dist, seg, offs, qw, kw, name):
    l = qkv2.shape[0]
    nh = qw // HEAD_DIM
    nb = l // BLK
    scale = HEAD_DIM ** -0.5
    use_sink = sink is not None

    def body(*refs):
        if use_sink:
            sink_ref, refs = refs[0], refs[1:]
        q_ref, kp_ref, kc_ref, vp_ref, vc_ref, o_ref, lse_ref = refs
        valid_prev, valid_cur = _band_masks(pl.program_id(1) == 0, max_dist)
        lane = lax.broadcasted_iota(jnp.int32, (BLK, LANES), 1)
        lse_tile = jnp.zeros((BLK, LANES), F32)

        def dot(a, b, dims=_NN):
            return lax.dot_general(a, b, dims, preferred_element_type=F32)

        for g0 in range(0, nh, HEAD_STAGE):
            heads = list(range(g0, min(g0 + HEAD_STAGE, nh)))
            kv = {}
            for kh in sorted({h // group for h in heads}):
                ks = slice(kh * HEAD_DIM, (kh + 1) * HEAD_DIM)
                kv[kh] = tuple(ref[:, ks].astype(BF16) for ref in (kp_ref, kc_ref, vp_ref, vc_ref))
            qs = [q_ref[:, h * HEAD_DIM:(h + 1) * HEAD_DIM].astype(BF16) for h in heads]
            sps = [jnp.where(valid_prev, dot(qh, kv[h // group][0], _NT) * scale, NEG_INF) for h, qh in zip(heads, qs)]
            scs = [jnp.where(valid_cur, dot(qh, kv[h // group][1], _NT) * scale, NEG_INF) for h, qh in zip(heads, qs)]
            ms = [jnp.maximum(jnp.max(sp, axis=1, keepdims=True), jnp.max(sc, axis=1, keepdims=True))
                  for sp, sc in zip(sps, scs)]
            if use_sink:
                ms = [jnp.maximum(m, sink_ref[h]) for h, m in zip(heads, ms)]
            pps = [jnp.exp(sp - m) for sp, m in zip(sps, ms)]
            pcs = [jnp.exp(sc - m) for sc, m in zip(scs, ms)]
            dens = [jnp.sum(pp, axis=1, keepdims=True) + jnp.sum(pc, axis=1, keepdims=True)
                    for pp, pc in zip(pps, pcs)]
            if use_sink:
                dens = [den + jnp.exp(sink_ref[h] - m) for h, den, m in zip(heads, dens, ms)]
            outs = [dot(pp.astype(BF16), kv[h // group][2]) + dot(pc.astype(BF16), kv[h // group][3])
                    for h, pp, pc in zip(heads, pps, pcs)]
            for h, o, den, m in zip(heads, outs, dens, ms):
                o_ref[:, h * HEAD_DIM:(h + 1) * HEAD_DIM] = o / den
                lse_tile = jnp.where(lane == h, m + jnp.log(den), lse_tile)
        lse_ref[...] = lse_tile

    cur, prev = _attn_row_maps(nb)
    o_spec, lse_spec = _dil_spec(qw, dil, cur), _dil_spec(LANES, dil, cur)
    in_specs = [_dil_spec(qw, dil, cur, seg, offs[0]),
                _dil_spec(kw, dil, prev, seg, offs[1]), _dil_spec(kw, dil, cur, seg, offs[1]),
                _dil_spec(kw, dil, prev, seg, offs[2]), _dil_spec(kw, dil, cur, seg, offs[2])]
    args = [qkv2] * 5
    if use_sink:
        in_specs = [pl.BlockSpec(memory_space=pltpu.SMEM)] + in_specs
        args = [sink] + args
    return pl.pallas_call(
        body, name=name, grid=(dil, nb),
        in_specs=in_specs,
        out_specs=[o_spec, lse_spec],
        out_shape=[_dil_shape(l, dil, qw), _dil_shape(l, dil, LANES)],
        compiler_params=_cparams(("parallel", "parallel")),
    )(*args)


def _attn_bwd_call(qkv2, sink, o2, lse2, do2, dlse2, *, dil, group, max_dist, seg, offs, qw, kw, name):
    l = qkv2.shape[0]
    nh = qw // HEAD_DIM
    nb = l // BLK
    scale = HEAD_DIM ** -0.5
    use_sink = sink is not None

    def body(*refs):
        if use_sink:
            sink_ref, refs = refs[0], refs[1:]
        (q_ref, kp_ref, kc_ref, vp_ref, vc_ref, o_ref, lse_ref, do_ref, dlse_ref,
         dq_ref, dk_ref, dv_ref, dsink_ref, ck_ref, cv_ref) = refs
        step = pl.program_id(1)

        @pl.when(jnp.logical_and(pl.program_id(0) == 0, step == 0))
        def _():
            dsink_ref[...] = jnp.zeros_like(dsink_ref)

        @pl.when(step == 0)
        def _():
            ck_ref[...] = jnp.zeros_like(ck_ref)
            cv_ref[...] = jnp.zeros_like(cv_ref)

        def dot(a, b, dims=_NN):
            return lax.dot_general(a, b, dims, preferred_element_type=F32)

        @pl.when(step < nb)
        def _():
            valid_prev, valid_cur = _band_masks(step == 0, max_dist)
            row = lax.broadcasted_iota(jnp.int32, (8, LANES), 0)
            lanes8 = lax.broadcasted_iota(jnp.int32, (8, LANES), 1)
            ds_tile = jnp.zeros((8, LANES), F32)
            for g0 in range(0, nh, HEAD_STAGE):
                heads = list(range(g0, min(g0 + HEAD_STAGE, nh)))
                hss = [slice(h * HEAD_DIM, (h + 1) * HEAD_DIM) for h in heads]
                kv = {}
                for kh in sorted({h // group for h in heads}):
                    ks = slice(kh * HEAD_DIM, (kh + 1) * HEAD_DIM)
                    kv[kh] = tuple(ref[:, ks].astype(BF16) for ref in (kp_ref, kc_ref, vp_ref, vc_ref))
                qs = [q_ref[:, hs].astype(BF16) for hs in hss]
                dos = [do_ref[:, hs] for hs in hss]
                dobs = [d.astype(BF16) for d in dos]
                lses = [lse_ref[:, h:h + 1] for h in heads]
                sps = [dot(qh, kv[h // group][0], _NT) * scale for h, qh in zip(heads, qs)]
                scs = [dot(qh, kv[h // group][1], _NT) * scale for h, qh in zip(heads, qs)]
                dpps = [dot(dob, kv[h // group][2], _NT) for h, dob in zip(heads, dobs)]
                dpcs = [dot(dob, kv[h // group][3], _NT) for h, dob in zip(heads, dobs)]
                pps = [jnp.where(valid_prev, jnp.exp(jnp.where(valid_prev, sp, NEG_INF) - ls), 0.0)
                       for sp, ls in zip(sps, lses)]
                pcs = [jnp.where(valid_cur, jnp.exp(jnp.where(valid_cur, sc, NEG_INF) - ls), 0.0)
                       for sc, ls in zip(scs, lses)]
                deltas = [jnp.sum(d * o_ref[:, hs], axis=1, keepdims=True) for d, hs in zip(dos, hss)]
                corrs = [dlse_ref[:, h:h + 1] - dl for h, dl in zip(heads, deltas)]
                dsps = [(pp * (dp + c)).astype(BF16) for pp, dp, c in zip(pps, dpps, corrs)]
                dscs = [(pc * (dp + c)).astype(BF16) for pc, dp, c in zip(pcs, dpcs, corrs)]
                for h, hs, dsp, dsc in zip(heads, hss, dsps, dscs):
                    dq = (dot(dsp, kv[h // group][0]) + dot(dsc, kv[h // group][1])) * scale
                    dq_ref[:, hs] = dq.astype(BF16)
                parts = [(dot(dsc, qh, _TN) * scale, dot(dsp, qh, _TN) * scale,
                          dot(pc.astype(BF16), dob, _TN), dot(pp.astype(BF16), dob, _TN))
                         for dsc, dsp, qh, pc, pp, dob in zip(dscs, dsps, qs, pcs, pps, dobs)]
                for kh in kv:
                    ks = slice(kh * HEAD_DIM, (kh + 1) * HEAD_DIM)
                    mine = [p for h, p in zip(heads, parts) if h // group == kh]
                    dkc, dkp, dvc, dvp = (sum(p[j] for p in mine[1:]) + mine[0][j] for j in range(4))
                    dk_ref[:, ks] = (ck_ref[:, ks] + dkp).astype(BF16)
                    dv_ref[:, ks] = (cv_ref[:, ks] + dvp).astype(BF16)
                    ck_ref[:, ks] = dkc
                    cv_ref[:, ks] = dvc
                if use_sink:
                    for h, ls, dl in zip(heads, lses, deltas):
                        val = -jnp.sum(jnp.exp(sink_ref[h] - ls) * dl, axis=0, keepdims=True)
                        ds_tile = jnp.where(jnp.logical_and(row == 0, lanes8 == h), val, ds_tile)
            if use_sink:
                dsink_ref[...] += ds_tile

        @pl.when(step == nb)
        def _():
            dk_ref[...] = ck_ref[...].astype(BF16)
            dv_ref[...] = cv_ref[...].astype(BF16)

    cur, prev = _attn_row_maps(nb)
    q_spec, lse_spec = _dil_spec(qw, dil, cur), _dil_spec(LANES, dil, cur)
    lag_spec = _dil_spec(kw, dil, lambda i: jnp.maximum(i - 1, 0))
    in_specs = [_dil_spec(qw, dil, cur, seg, offs[0]),
                _dil_spec(kw, dil, prev, seg, offs[1]), _dil_spec(kw, dil, cur, seg, offs[1]),
                _dil_spec(kw, dil, prev, seg, offs[2]), _dil_spec(kw, dil, cur, seg, offs[2]),
                q_spec, lse_spec, q_spec, lse_spec]
    args = [qkv2] * 5 + [o2, lse2, do2, dlse2]
    if use_sink:
        in_specs = [pl.BlockSpec(memory_space=pltpu.SMEM)] + in_specs
        args = [sink] + args
    kv_shape = _dil_shape(l, dil, kw, BF16)
    return pl.pallas_call(
        body, name=name, grid=(dil, nb + 1),
        in_specs=in_specs,
        out_specs=[q_spec, lag_spec, lag_spec, pl.BlockSpec((8, LANES), lambda r, i: (0, 0))],
        out_shape=[_dil_shape(l, dil, qw, BF16), kv_shape, kv_shape,
                   jax.ShapeDtypeStruct((8, LANES), F32)],
        scratch_shapes=[pltpu.VMEM((BLK, kw), F32), pltpu.VMEM((BLK, kw), F32)],
        compiler_params=_cparams(("arbitrary", "arbitrary")),
    )(*args)


def _attn_config(tag, dil, group, max_dist, seg, offs, qw, kw):
    return dict(name=tag, dil=dil, group=group, max_dist=max_dist, seg=seg, offs=offs, qw=qw, kw=kw)


A_W = 8 * HEAD_DIM
ATTN_A_CFGS = tuple(_attn_config("attn_a%d" % dil, dil, 1, window // dil, 3 * A_W, (0, A_W, 2 * A_W), A_W, A_W)
                    for window, dil in A_CONFIGS)
B_KVW = 2 * HEAD_DIM
ATTN_B_CFG = _attn_config("attn_b", 1, B_GROUP, BLK - 1, A_W + 2 * B_KVW, (0, A_W, A_W + B_KVW), A_W, B_KVW)


def _attn_fwd(cfg, qkv, sink):
    t = qkv.shape[0]
    kw = {k: v for k, v in cfg.items() if k != "name"}
    qkv2 = _dil_view(qkv, cfg["dil"])
    o2, lse2 = _attn_fwd_call(qkv2, sink, name=cfg["name"] + "_fwd", **kw)
    return o2.reshape(t, cfg["qw"]), lse2.reshape(t, LANES), (qkv2, o2, lse2)


def _attn_bwd(cfg, saved, sink, do, dlse):
    qkv2, o2, lse2 = saved
    t = do.shape[0]
    kw = {k: v for k, v in cfg.items() if k != "name"}
    dq2, dk2, dv2, dsink = _attn_bwd_call(qkv2, sink, o2, lse2, _dil_view(do, cfg["dil"]),
                                          _dil_view(dlse, cfg["dil"]), name=cfg["name"] + "_bwd", **kw)
    return dq2.reshape(t, cfg["qw"]), dk2.reshape(t, cfg["kw"]), dv2.reshape(t, cfg["kw"]), dsink


def _head_expand():
    r = lax.broadcasted_iota(jnp.int32, (LANES, 8 * HEAD_DIM), 0)
    c = lax.broadcasted_iota(jnp.int32, (LANES, 8 * HEAD_DIM), 1)
    return (c // HEAD_DIM == r).astype(F32)


def _combine_weights(l0, l1, l2):
    m = jnp.maximum(jnp.maximum(l0, l1), l2)
    e0, e1, e2 = jnp.exp(l0 - m), jnp.exp(l1 - m), jnp.exp(l2 - m)
    inv = 1.0 / (e0 + e1 + e2)
    return e0 * inv, e1 * inv, e2 * inv


def _combine_fwd_call(os_, lses, *, tq=256):
    t, w = os_[0].shape

    def body(o0, o1, o2, l0, l1, l2, y_ref):
        ws = _combine_weights(l0[...], l1[...], l2[...])
        e = _head_expand()
        y = jnp.zeros((tq, w), F32)
        for o_ref, wt in zip((o0, o1, o2), ws):
            y = y + _dot_mask(e, wt, mask_left=False) * o_ref[...]
        y_ref[...] = y

    o_spec = pl.BlockSpec((tq, w), lambda i: (i, 0))
    l_spec = pl.BlockSpec((tq, LANES), lambda i: (i, 0))
    return pl.pallas_call(
        body, name="combine_fwd", grid=(t // tq,),
        in_specs=[o_spec] * 3 + [l_spec] * 3, out_specs=o_spec,
        out_shape=jax.ShapeDtypeStruct((t, w), F32),
        compiler_params=_cparams(("parallel",)),
    )(*os_, *lses)


def _combine_bwd_call(os_, lses, dy, *, tq=256):
    t, w = dy.shape

    def body(o0, o1, o2, l0, l1, l2, dy_ref, do0, do1, do2, dl0, dl1, dl2):
        ws = _combine_weights(l0[...], l1[...], l2[...])
        e = _head_expand()
        dyv = dy_ref[...]
        dws = []
        for o_ref, do_ref, wt in zip((o0, o1, o2), (do0, do1, do2), ws):
            do_ref[...] = _dot_mask(e, wt, mask_left=False) * dyv
            dws.append(_dot_mask(e, dyv * o_ref[...], _NT, mask_left=False))
        mean = ws[0] * dws[0] + ws[1] * dws[1] + ws[2] * dws[2]
        for dl_ref, wt, dw in zip((dl0, dl1, dl2), ws, dws):
            dl_ref[...] = wt * (dw - mean)

    o_spec = pl.BlockSpec((tq, w), lambda i: (i, 0))
    l_spec = pl.BlockSpec((tq, LANES), lambda i: (i, 0))
    o_shape = jax.ShapeDtypeStruct((t, w), F32)
    l_shape = jax.ShapeDtypeStruct((t, LANES), F32)
    return pl.pallas_call(
        body, name="combine_bwd", grid=(t // tq,),
        in_specs=[o_spec] * 3 + [l_spec] * 3 + [o_spec], out_specs=[o_spec] * 3 + [l_spec] * 3,
        out_shape=[o_shape] * 3 + [l_shape] * 3,
        compiler_params=_cparams(("parallel",)),
    )(*os_, *lses, dy)


C_QKW = C_QK_HEADS * C_DK
C_CONV_W = 2 * C_QKW + C_V_HEADS * C_DK
HALO = 8


def _silu_parts(z):
    sig = jax.nn.sigmoid(z)
    return z * sig, sig * (1.0 + z * (1.0 - sig))


def _conv_window_specs(tq, t):
    c = C_CONV_W
    cb = COL["c_qkv"] // c
    blk = pl.BlockSpec((tq, c), lambda i: (i, cb))
    before = pl.BlockSpec((HALO, c), lambda i: (jnp.maximum(i * (tq // HALO) - 1, 0), cb))
    return c, cb, blk, before


def _conv_prep_fwd_call(u, w, *, tq=256):
    t = u.shape[0]
    c, _, x_spec, halo_spec = _conv_window_specs(tq, t)
    nqk = 2 * C_QK_HEADS

    def body(x_ref, halo_ref, w_ref, z_ref, qk_ref, v_ref):
        i = pl.program_id(0)
        halo = jnp.where(i == 0, 0.0, halo_ref[...])
        xc = jnp.concatenate([halo, x_ref[...]], axis=0)
        wv = w_ref[...]
        z = xc[HALO - 3:HALO - 3 + tq] * wv[0:1]
        for j in range(1, C_CONV):
            z = z + xc[HALO - 3 + j:HALO - 3 + j + tq] * wv[j:j + 1]
        z_ref[...] = z
        act, _ = _silu_parts(z)
        for h in range(nqk):
            a = act[:, h * C_DK:(h + 1) * C_DK]
            qk_ref[:, h * C_DK:(h + 1) * C_DK] = a * lax.rsqrt(jnp.sum(a * a, axis=1, keepdims=True) + EPS)
        v_ref[...] = act[:, nqk * C_DK:]

    return pl.pallas_call(
        body, name="conv_prep_fwd", grid=(t // tq,),
        in_specs=[x_spec, halo_spec, pl.BlockSpec((C_CONV, c), lambda i: (0, 0))],
        out_specs=[pl.BlockSpec((tq, c), lambda i: (i, 0)),
                   pl.BlockSpec((tq, 2 * C_QKW), lambda i: (i, 0)),
                   pl.BlockSpec((tq, c - 2 * C_QKW), lambda i: (i, 0))],
        out_shape=[jax.ShapeDtypeStruct((t, c), F32), jax.ShapeDtypeStruct((t, 2 * C_QKW), F32),
                   jax.ShapeDtypeStruct((t, c - 2 * C_QKW), F32)],
        compiler_params=_cparams(("parallel",)),
    )(u, u, w)


def _conv_prep_dz_call(z, dqk, dv, *, tq=256):
    t, c = z.shape
    nqk = 2 * C_QK_HEADS

    def body(z_ref, dqk_ref, dv_ref, dz_ref):
        zv = z_ref[...]
        act, dact = _silu_parts(zv)
        for h in range(nqk):
            hs = slice(h * C_DK, (h + 1) * C_DK)
            a = act[:, hs]
            r = lax.rsqrt(jnp.sum(a * a, axis=1, keepdims=True) + EPS)
            nrm = a * r
            dn = dqk_ref[:, hs]
            da = r * (dn - nrm * jnp.sum(dn * nrm, axis=1, keepdims=True))
            dz_ref[:, hs] = da * dact[:, hs]
        dz_ref[:, nqk * C_DK:] = dv_ref[...] * dact[:, nqk * C_DK:]

    return pl.pallas_call(
        body, name="conv_prep_dz", grid=(t // tq,),
        in_specs=[pl.BlockSpec((tq, c), lambda i: (i, 0)),
                  pl.BlockSpec((tq, 2 * C_QKW), lambda i: (i, 0)),
                  pl.BlockSpec((tq, c - 2 * C_QKW), lambda i: (i, 0))],
        out_specs=pl.BlockSpec((tq, c), lambda i: (i, 0)),
        out_shape=jax.ShapeDtypeStruct((t, c), F32),
        compiler_params=_cparams(("parallel",)),
    )(z, dqk, dv)


def _conv_bwd_call(u, dz, w, du_buf, *, tq=256):
    t = u.shape[0]
    nt = t // tq
    c, cb, x_spec, halo_spec = _conv_window_specs(tq, t)
    extra, extra_specs, aliases = _du_operands(du_buf, 5)

    def body(x_ref, xh_ref, dz_ref, dzh_ref, w_ref, *refs):
        dx_ref, dw_ref = refs[len(extra):]
        i = pl.program_id(0)
        xc = jnp.concatenate([jnp.where(i == 0, 0.0, xh_ref[...]), x_ref[...]], axis=0)
        dzv = dz_ref[...]
        dzc = jnp.concatenate([dzv, jnp.where(i == nt - 1, 0.0, dzh_ref[...])], axis=0)
        wv = w_ref[...]
        dx = dzv * wv[3:4]
        for s in range(1, C_CONV):
            dx = dx + dzc[s:s + tq] * wv[3 - s:4 - s]
        dx_ref[...] = dx.astype(BF16)
        row = lax.broadcasted_iota(jnp.int32, (8, c), 0)
        dw = jnp.zeros((8, c), F32)
        for j in range(C_CONV):
            prod = dzv * xc[HALO - 3 + j:HALO - 3 + j + tq]
            col = jnp.sum(jnp.sum(prod.reshape(tq // 8, 8, c), axis=0), axis=0, keepdims=True)
            dw = jnp.where(row == j, col, dw)

        @pl.when(i == 0)
        def _():
            dw_ref[...] = dw

        @pl.when(i > 0)
        def _():
            dw_ref[...] += dw

    blk = pl.BlockSpec((tq, c), lambda i: (i, 0))
    after = pl.BlockSpec((HALO, c), lambda i: (jnp.minimum((i + 1) * (tq // HALO), t // HALO - 1), 0))
    return pl.pallas_call(
        body, name="conv_bwd", grid=(nt,),
        in_specs=[x_spec, halo_spec, blk, after, pl.BlockSpec((C_CONV, c), lambda i: (0, 0))] + extra_specs,
        out_specs=[pl.BlockSpec((tq, c), lambda i: (i, cb)), pl.BlockSpec((8, c), lambda i: (0, 0))],
        out_shape=[_du_shape(t), jax.ShapeDtypeStruct((8, c), F32)],
        input_output_aliases=aliases,
        compiler_params=_cparams(("arbitrary",)),
    )(u, u, dz, dz, w, *extra)


C_VW = C_V_HEADS * C_DK


def _softplus(x):
    return jnp.maximum(x, 0.0) + jnp.log(1.0 + jnp.exp(-jnp.abs(x)))


def _tri_masks():
    r = lax.broadcasted_iota(jnp.int32, (CHUNK, CHUNK), 0)
    c = lax.broadcasted_iota(jnp.int32, (CHUNK, CHUNK), 1)
    return r >= c, r > c


def _split_bf16(a):
    hi = a.astype(BF16)
    return hi, (a - hi.astype(F32)).astype(BF16)


def _dot_hi(a, b, dims=None):
    dims = _NN if dims is None else dims
    ah, al = _split_bf16(a)
    bh, bl = _split_bf16(b)

    def d(x, y):
        return lax.dot_general(x, y, dims, preferred_element_type=F32)

    return d(ah, bh) + (d(ah, bl) + d(al, bh))


def _dot_mask(mask, b, dims=None, mask_left=True):
    dims = _NN if dims is None else dims
    mb = mask.astype(BF16)
    b1 = b.astype(BF16)
    rest = b - b1.astype(F32)
    b2 = rest.astype(BF16)
    b3 = (rest - b2.astype(F32)).astype(BF16)
    out = None
    for p in (b1, b2, b3):
        term = (lax.dot_general(mb, p, dims, preferred_element_type=F32) if mask_left
                else lax.dot_general(p, mb, dims, preferred_element_type=F32))
        out = term if out is None else out + term
    return out


def _unit_lower_inverses(mats):
    r = lax.broadcasted_iota(jnp.int32, (CHUNK, CHUNK), 0)
    c = lax.broadcasted_iota(jnp.int32, (CHUNK, CHUNK), 1)
    eye = (r == c).astype(F32)
    xs = [eye - a for a in mats]
    ps = [_dot_hi(a, a) for a in mats]
    steps = int(math.log2(CHUNK)) - 1
    for s in range(steps):
        xs = [x + _dot_hi(x, p) for x, p in zip(xs, ps)]
        if s < steps - 1:
            ps = [_dot_hi(p, p) for p in ps]
    return xs


def _gate_tiles(cab, alog, dtb):
    pre = cab + dtb
    g = -jnp.exp(alog) * _softplus(pre)
    beta = jax.nn.sigmoid(pltpu.roll(cab, LANES - C_V_HEADS, 1))
    return g, beta, pre


def _chunk_common(kk, qk, gc, gct, beta, h, tri, strict):
    gcol, grow, bcol = gc[:, h:h + 1], gct[h:h + 1, :], beta[:, h:h + 1]
    decay = jnp.where(tri, jnp.exp(jnp.where(tri, gcol - grow, 0.0)), 0.0)
    kkd = jnp.where(strict, kk * decay, 0.0)
    attn = jnp.where(tri, qk * decay, 0.0)
    glast = gc[CHUNK - 1:CHUNK, h:h + 1]
    return gcol, bcol, decay, kkd, attn, glast


def _cab_spec():
    return pl.BlockSpec((CHUNK, LANES), lambda n: (n, COL["c_ab"] // LANES))


def _delta_prep_call(qk, v, u, alog, dtb):
    t = qk.shape[0]
    nc = t // CHUNK
    scale = C_DK ** -0.5

    def body(q_ref, k_ref, v_ref, cab_ref, alog_ref, dtb_ref,
             u_ref, w_ref, qd_ref, kd_ref, attn_ref, tmat_ref, gc_ref, beta_ref):
        tri, strict = _tri_masks()
        g, beta, _ = _gate_tiles(cab_ref[...], alog_ref[...], dtb_ref[...])
        gc = _dot_mask(tri, g)
        gct = gc.T
        gc_ref[...] = gc
        beta_ref[...] = beta
        mats, rhs = [], []
        for j in range(C_QK_HEADS):
            js = slice(j * C_DK, (j + 1) * C_DK)
            kf, qf = k_ref[:, js], q_ref[:, js] * scale
            kb, qb = kf.astype(BF16), qf.astype(BF16)
            kk = lax.dot_general(kb, kb, _NT, preferred_element_type=F32)
            qk = lax.dot_general(qb, kb, _NT, preferred_element_type=F32)
            for h in (2 * j, 2 * j + 1):
                hs = slice(h * C_DK, (h + 1) * C_DK)
                gcol, bcol, decay, kkd, attn, glast = _chunk_common(kk, qk, gc, gct, beta, h, tri, strict)
                gexp = jnp.exp(gcol)
                mats.append(kkd * bcol)
                rhs.append(jnp.concatenate([v_ref[:, hs] * bcol, kf * (bcol * gexp)], axis=1))
                qd_ref[:, hs] = (qf * gexp).astype(BF16)
                kd_ref[:, hs] = (kf * jnp.exp(glast - gcol)).astype(BF16)
                attn_ref[:, h * CHUNK:(h + 1) * CHUNK] = attn.astype(BF16)
        for h, (tmat, r) in enumerate(zip(_unit_lower_inverses(mats), rhs)):
            hs = slice(h * C_DK, (h + 1) * C_DK)
            uw = _dot_hi(tmat, r)
            u_ref[:, hs] = uw[:, :C_DK]
            w_ref[:, hs] = uw[:, C_DK:]
            tmat_ref[:, h * CHUNK:(h + 1) * CHUNK] = tmat

    def blk(w):
        return pl.BlockSpec((CHUNK, w), lambda n: (n, 0))

    row = pl.BlockSpec((1, LANES), lambda n: (0, 0))
    big = jax.ShapeDtypeStruct((t, C_VW), F32)
    sq = jax.ShapeDtypeStruct((t, C_V_HEADS * CHUNK), F32)
    tile = jax.ShapeDtypeStruct((t, LANES), F32)
    half = jax.ShapeDtypeStruct((t, C_VW), BF16)
    return pl.pallas_call(
        body, name="delta_prep", grid=(nc,),
        in_specs=[blk(C_QKW), pl.BlockSpec((CHUNK, C_QKW), lambda n: (n, 1)), blk(C_VW), _cab_spec(), row, row],
        out_specs=[blk(C_VW)] * 4 + [blk(C_V_HEADS * CHUNK)] * 2 + [blk(LANES)] * 2,
        out_shape=[big, big, half, half, jax.ShapeDtypeStruct(sq.shape, BF16), sq] + [tile] * 2,
        compiler_params=_cparams(("parallel",)),
    )(qk, qk, v, u, alog, dtb)


def _delta_scan_call(u, w, qd, kd, attn, gc):
    t = u.shape[0]
    nc = t // CHUNK

    def body(u_ref, w_ref, qd_ref, kd_ref, attn_ref, gc_ref, o_ref, vn_ref, st_ref, s_ref):
        @pl.when(pl.program_id(0) == 0)
        def _():
            s_ref[...] = jnp.zeros_like(s_ref)

        hss = [slice(h * C_DK, (h + 1) * C_DK) for h in range(C_V_HEADS)]
        states = [s_ref[hs, :] for hs in hss]
        for hs, s in zip(hss, states):
            st_ref[0, hs, :] = s
        sbs = [s.astype(BF16) for s in states]
        vns = [u_ref[:, hs] - jnp.dot(w_ref[:, hs].astype(BF16), sb, preferred_element_type=F32)
               for hs, sb in zip(hss, sbs)]
        qss = [jnp.dot(qd_ref[:, hs].astype(BF16), sb, preferred_element_type=F32) for hs, sb in zip(hss, sbs)]
        vnbs = [vn.astype(BF16) for vn in vns]
        for h, hs in enumerate(hss):
            vn_ref[:, hs] = vnbs[h]
            o_ref[:, hs] = qss[h] + jnp.dot(attn_ref[:, h * CHUNK:(h + 1) * CHUNK].astype(BF16), vnbs[h],
                                            preferred_element_type=F32)
        for h, hs in enumerate(hss):
            glast = jnp.exp(gc_ref[CHUNK - 1:CHUNK, h:h + 1])
            s_ref[hs, :] = states[h] * glast + lax.dot_general(kd_ref[:, hs].astype(BF16), vnbs[h], _TN,
                                                               preferred_element_type=F32)

    def blk(wd):
        return pl.BlockSpec((CHUNK, wd), lambda n: (n, 0))

    big = jax.ShapeDtypeStruct((t, C_VW), F32)
    return pl.pallas_call(
        body, name="delta_scan", grid=(nc,),
        in_specs=[blk(C_VW)] * 4 + [blk(C_V_HEADS * CHUNK), blk(LANES)],
        out_specs=[blk(C_VW), blk(C_VW), pl.BlockSpec((1, C_VW, C_DK), lambda n: (n, 0, 0))],
        out_shape=[big, jax.ShapeDtypeStruct((t, C_VW), BF16), jax.ShapeDtypeStruct((nc, C_VW, C_DK), F32)],
        scratch_shapes=[pltpu.VMEM((C_VW, C_DK), F32)],
        compiler_params=_cparams(("arbitrary",)),
    )(u, w, qd, kd, attn, gc)


def _delta_scan_bwd_call(do, w, qd, kd, attn, gc, vn, st):
    t = do.shape[0]
    nc = t // CHUNK

    def body(do_ref, w_ref, qd_ref, kd_ref, attn_ref, gc_ref, vn_ref, st_ref,
             du_ref, dw_ref, dqd_ref, dkd_ref, dattn_ref, dgl_ref, ds_ref):
        @pl.when(pl.program_id(0) == 0)
        def _():
            ds_ref[...] = jnp.zeros_like(ds_ref)

        tri, _ = _tri_masks()
        row = lax.broadcasted_iota(jnp.int32, (8, LANES), 0)
        lane = lax.broadcasted_iota(jnp.int32, (8, LANES), 1)
        dgl = jnp.zeros((8, LANES), F32)
        hss = [slice(h * C_DK, (h + 1) * C_DK) for h in range(C_V_HEADS)]
        css = [slice(h * CHUNK, (h + 1) * CHUNK) for h in range(C_V_HEADS)]

        def dg(a, b, dims):
            return lax.dot_general(a, b, dims, preferred_element_type=F32)

        ss = [st_ref[0, hs, :] for hs in hss]
        dsps = [ds_ref[hs, :] for hs in hss]
        sbs = [s.astype(BF16) for s in ss]
        dspbs = [d.astype(BF16) for d in dsps]
        dobs = [do_ref[:, hs].astype(BF16) for hs in hss]
        vnbs = [vn_ref[:, hs].astype(BF16) for hs in hss]
        dvns = [dg(attn_ref[:, cs].astype(BF16), dob, _TN) + dg(kd_ref[:, hs].astype(BF16), dspb, _NN)
                for hs, cs, dob, dspb in zip(hss, css, dobs, dspbs)]
        for h, hs in enumerate(hss):
            dqd_ref[:, hs] = dg(dobs[h], sbs[h], _NT)
            dkd_ref[:, hs] = dg(vnbs[h], dspbs[h], _NT)
            dattn_ref[:, css[h]] = jnp.where(tri, dg(dobs[h], vnbs[h], _NT), 0.0)
        dvnbs = [d.astype(BF16) for d in dvns]
        for h, hs in enumerate(hss):
            du_ref[:, hs] = dvns[h]
            dw_ref[:, hs] = -dg(dvnbs[h], sbs[h], _NT)
            tot = jnp.sum(jnp.sum(dsps[h] * ss[h], axis=0, keepdims=True), axis=1, keepdims=True)
            dgl = jnp.where(jnp.logical_and(row == 0, lane == h), tot, dgl)
        for h, hs in enumerate(hss):
            glast = jnp.exp(gc_ref[CHUNK - 1:CHUNK, h:h + 1])
            ds_ref[hs, :] = (dg(qd_ref[:, hs].astype(BF16), dobs[h], _TN) + glast * dsps[h]
                             - dg(w_ref[:, hs].astype(BF16), dvnbs[h], _TN))
        dgl_ref[...] = dgl

    def blk(wd):
        return pl.BlockSpec((CHUNK, wd), lambda n: (nc - 1 - n, 0))

    big = jax.ShapeDtypeStruct((t, C_VW), F32)
    return pl.pallas_call(
        body, name="delta_scan_bwd", grid=(nc,),
        in_specs=[blk(C_VW)] * 4 + [blk(C_V_HEADS * CHUNK), blk(LANES), blk(C_VW),
                                    pl.BlockSpec((1, C_VW, C_DK), lambda n: (nc - 1 - n, 0, 0))],
        out_specs=[blk(C_VW)] * 4 + [blk(C_V_HEADS * CHUNK), pl.BlockSpec((8, LANES), lambda n: (nc - 1 - n, 0))],
        out_shape=[big] * 4 + [jax.ShapeDtypeStruct((t, C_V_HEADS * CHUNK), F32),
                               jax.ShapeDtypeStruct((nc * 8, LANES), F32)],
        scratch_shapes=[pltpu.VMEM((C_VW, C_DK), F32)],
        compiler_params=_cparams(("arbitrary",)),
    )(do, w, qd, kd, attn, gc, vn, st)


def _delta_prep_bwd_call(qk, v, proj, alog, dtb, tmat, u, w, gc, beta, du, dw, dqd, dkd, dattn, dgl, du_buf):
    t = qk.shape[0]
    extra, extra_specs, aliases = _du_operands(du_buf, 17)
    nc = t // CHUNK
    scale = C_DK ** -0.5

    def body(q_ref, k_ref, v_ref, cab_ref, alog_ref, dtb_ref, tmat_ref, u_ref, w_ref, gc_ref, beta_ref,
             du_ref, dw_ref, dqd_ref, dkd_ref, dattn_ref, dgl_ref, *outs):
        dcab_ref, dqk_ref, dv_ref, dpar_ref = outs[len(extra):]
        tri, strict = _tri_masks()
        gc, beta = gc_ref[...], beta_ref[...]
        gct = gc.T
        ones = jnp.ones((CHUNK, LANES), F32)
        lane = lax.broadcasted_iota(jnp.int32, (CHUNK, LANES), 1)
        rowi = lax.broadcasted_iota(jnp.int32, (CHUNK, 1), 0)
        dgc_tile = jnp.zeros((CHUNK, LANES), F32)
        db_tile = jnp.zeros((CHUNK, LANES), F32)
        heads = []
        for j in range(C_QK_HEADS):
            js = slice(j * C_DK, (j + 1) * C_DK)
            kf, qf = k_ref[:, js], q_ref[:, js] * scale
            kb, qb = kf.astype(BF16), qf.astype(BF16)
            kk = lax.dot_general(kb, kb, _NT, preferred_element_type=F32)
            qk = lax.dot_general(qb, kb, _NT, preferred_element_type=F32)
            for h in (2 * j, 2 * j + 1):
                heads.append((h, kf, qf, kb, qb) + _chunk_common(kk, qk, gc, gct, beta, h, tri, strict))
        dvks = [_dot_hi(tmat_ref[:, h * CHUNK:(h + 1) * CHUNK],
                        jnp.concatenate([du_ref[:, h * C_DK:(h + 1) * C_DK], dw_ref[:, h * C_DK:(h + 1) * C_DK]], axis=1),
                        _TN) for h in range(C_V_HEADS)]
        das = [-jnp.where(strict, _dot_hi(dvk, jnp.concatenate([u_ref[:, h * C_DK:(h + 1) * C_DK],
                                                                w_ref[:, h * C_DK:(h + 1) * C_DK]], axis=1), _NT), 0.0)
               for h, dvk in enumerate(dvks)]

        def dot(x, y, dims=_NN):
            return lax.dot_general(x, y, dims, preferred_element_type=F32)

        pre = []
        for (h, kf, qf, kb, qb, gcol, bcol, decay, kkd, attn, glast), da in zip(heads, das):
            dattn_h = dattn_ref[:, h * CHUNK:(h + 1) * CHUNK]
            pre.append(((da * decay * bcol).astype(BF16), (dattn_h * decay).astype(BF16),
                        da * kkd * bcol + dattn_h * attn))
        mms = [(dot(dkk, hd[3]), dot(dkk, hd[3], _TN), dot(dqk, hd[4], _TN), dot(dqk, hd[3]),
                _dot_mask(ones, e, _TN, mask_left=False))
               for hd, (dkk, dqk, e) in zip(heads, pre)]
        dq_parts, dk_parts = [], []
        for (h, kf, qf, kb, qb, gcol, bcol, decay, kkd, attn, glast), dvk, da, (_, _, e), mm in zip(
                heads, dvks, das, pre, mms):
            hs = slice(h * C_DK, (h + 1) * C_DK)
            gexp = jnp.exp(gcol)
            fdec = jnp.exp(glast - gcol)
            dvb, dkb = dvk[:, :C_DK], dvk[:, C_DK:]
            dgc = jnp.sum(e, axis=1, keepdims=True) - mm[4][:, :1]
            dk_parts.append(mm[0] + mm[1] + mm[2] + dkb * (bcol * gexp) + dkd_ref[:, hs] * fdec)
            dq_parts.append(mm[3] + dqd_ref[:, hs] * gexp)
            dv_ref[:, hs] = dvb * bcol
            s_kb = jnp.sum(dkb * kf, axis=1, keepdims=True)
            db = (jnp.sum(da * kkd, axis=1, keepdims=True) + jnp.sum(dvb * v_ref[:, hs], axis=1, keepdims=True)
                  + s_kb * gexp)
            rho = jnp.sum(dkd_ref[:, hs] * kf, axis=1, keepdims=True) * fdec
            dgc = (dgc + s_kb * bcol * gexp + jnp.sum(dqd_ref[:, hs] * qf, axis=1, keepdims=True) * gexp - rho)
            last = jnp.sum(rho, axis=0, keepdims=True) + dgl_ref[0:1, h:h + 1] * jnp.exp(glast)
            dgc = dgc + jnp.where(rowi == CHUNK - 1, last, 0.0)
            dgc_tile = jnp.where(lane == h, dgc, dgc_tile)
            db_tile = jnp.where(lane == h, db, db_tile)
        for j in range(C_QK_HEADS):
            dqk_ref[:, j * C_DK:(j + 1) * C_DK] = (dq_parts[2 * j] + dq_parts[2 * j + 1]) * scale
            dqk_ref[:, C_QKW + j * C_DK:C_QKW + (j + 1) * C_DK] = dk_parts[2 * j] + dk_parts[2 * j + 1]
        dg = _dot_mask(jnp.logical_not(strict), dgc_tile)
        alog = alog_ref[...]
        g, _, gate_pre = _gate_tiles(cab_ref[...], alog, dtb_ref[...])
        dca = dg * (-jnp.exp(alog)) * jax.nn.sigmoid(gate_pre)
        dcab_ref[:, :LANES] = (dca + pltpu.roll(db_tile * beta * (1.0 - beta), C_V_HEADS, 1)).astype(BF16)
        dcab_ref[:, LANES:] = jnp.zeros((CHUNK, D_IN_PAD - COL["c_ab"] - LANES), BF16)
        row8 = lax.broadcasted_iota(jnp.int32, (8, LANES), 0)
        par = jnp.where(row8 == 0, jnp.sum(dg * g, axis=0, keepdims=True),
                        jnp.where(row8 == 1, jnp.sum(dca, axis=0, keepdims=True), 0.0))

        @pl.when(pl.program_id(0) == 0)
        def _():
            dpar_ref[...] = par

        @pl.when(pl.program_id(0) > 0)
        def _():
            dpar_ref[...] += par

    def blk(wd):
        return pl.BlockSpec((CHUNK, wd), lambda n: (n, 0))

    row = pl.BlockSpec((1, LANES), lambda n: (0, 0))
    sq = blk(C_V_HEADS * CHUNK)
    tail = D_IN_PAD - COL["c_ab"]
    assert COL["c_ab"] % tail == 0
    return pl.pallas_call(
        body, name="delta_prep_bwd", grid=(nc,),
        in_specs=[blk(C_QKW), pl.BlockSpec((CHUNK, C_QKW), lambda n: (n, 1)), blk(C_VW), _cab_spec(), row, row, sq,
                  blk(C_VW), blk(C_VW),
                  blk(LANES), blk(LANES), blk(C_VW), blk(C_VW), blk(C_VW), blk(C_VW), sq,
                  pl.BlockSpec((8, LANES), lambda n: (n, 0))] + extra_specs,
        out_specs=[pl.BlockSpec((CHUNK, tail), lambda n: (n, COL["c_ab"] // tail)),
                   blk(2 * C_QKW), blk(C_VW), pl.BlockSpec((8, LANES), lambda n: (0, 0))],
        out_shape=[_du_shape(t), jax.ShapeDtypeStruct((t, 2 * C_QKW), F32),
                   jax.ShapeDtypeStruct((t, C_VW), F32), jax.ShapeDtypeStruct((8, LANES), F32)],
        input_output_aliases=aliases,
        compiler_params=_cparams(("arbitrary",)),
    )(qk, qk, v, proj, alog, dtb, tmat, u, w, gc, beta, du, dw, dqd, dkd, dattn, dgl, *extra)


def _z_spec(tq):
    return pl.BlockSpec((tq, C_VW), lambda i: (i, COL["c_z"] // C_VW))


def _gated_norm_fwd_call(o, u, gain, *, tq=256):
    t, w = o.shape

    def body(o_ref, z_ref, g_ref, y_ref):
        act, _ = _silu_parts(z_ref[...])
        gv = g_ref[...]
        for h in range(C_V_HEADS):
            hs = slice(h * C_DK, (h + 1) * C_DK)
            ov = o_ref[:, hs]
            r = lax.rsqrt(jnp.mean(ov * ov, axis=1, keepdims=True) + EPS)
            y_ref[:, hs] = ov * r * gv * act[:, hs]

    blk = pl.BlockSpec((tq, w), lambda i: (i, 0))
    return pl.pallas_call(
        body, name="gated_norm_fwd", grid=(t // tq,),
        in_specs=[blk, _z_spec(tq), pl.BlockSpec((1, C_DK), lambda i: (0, 0))], out_specs=blk,
        out_shape=jax.ShapeDtypeStruct((t, w), F32),
        compiler_params=_cparams(("parallel",)),
    )(o, u, gain)


def _gated_norm_bwd_call(o, u, gain, dy, du_buf, *, tq=256):
    t, w = o.shape
    nt = t // tq
    extra, extra_specs, aliases = _du_operands(du_buf, 4)

    def body(o_ref, z_ref, g_ref, dy_ref, *refs):
        dz_ref, do_ref, dg_ref, acc_ref = refs[len(extra):]
        i = pl.program_id(0)
        act, dact = _silu_parts(z_ref[...])
        gv = g_ref[...]
        part = jnp.zeros((8, C_DK), F32)
        for h in range(C_V_HEADS):
            hs = slice(h * C_DK, (h + 1) * C_DK)
            ov = o_ref[:, hs]
            r = lax.rsqrt(jnp.mean(ov * ov, axis=1, keepdims=True) + EPS)
            xh = ov * r
            dyv = dy_ref[:, hs]
            dn = dyv * act[:, hs]
            dz_ref[:, hs] = (dyv * xh * gv * dact[:, hs]).astype(BF16)
            dxh = dn * gv
            do_ref[:, hs] = r * (dxh - xh * jnp.mean(dxh * xh, axis=1, keepdims=True))
            part = part + jnp.sum((dn * xh).reshape(tq // 8, 8, C_DK), axis=0)

        @pl.when(i == 0)
        def _():
            acc_ref[...] = part

        @pl.when(i > 0)
        def _():
            acc_ref[...] += part

        @pl.when(i == nt - 1)
        def _():
            dg_ref[...] = jnp.sum(acc_ref[...], axis=0, keepdims=True)

    blk = pl.BlockSpec((tq, w), lambda i: (i, 0))
    grow = pl.BlockSpec((1, C_DK), lambda i: (0, 0))
    return pl.pallas_call(
        body, name="gated_norm_bwd", grid=(nt,),
        in_specs=[blk, _z_spec(tq), grow, blk] + extra_specs, out_specs=[_z_spec(tq), blk, grow],
        out_shape=[_du_shape(t), jax.ShapeDtypeStruct((t, w), F32), jax.ShapeDtypeStruct((1, C_DK), F32)],
        scratch_shapes=[pltpu.VMEM((8, C_DK), F32)],
        input_output_aliases=aliases,
        compiler_params=_cparams(("arbitrary",)),
    )(o, u, gain, dy, *extra)


def _gate_specs(tq):
    return [pl.BlockSpec((tq, D_MODEL), lambda i, j=j: (i, j)) for j in range(3)]


def _merge_fwd_call(ps, u, *, tq=256):
    t, w = ps[0].shape

    def body(p0, p1, p2, g0, g1, g2, y_ref):
        y_ref[...] = (jax.nn.sigmoid(g0[...]) * p0[...] + jax.nn.sigmoid(g1[...]) * p1[...]
                      + jax.nn.sigmoid(g2[...]) * p2[...]).astype(BF16)

    blk = pl.BlockSpec((tq, w), lambda i: (i, 0))
    return pl.pallas_call(
        body, name="merge_fwd", grid=(t // tq,), in_specs=[blk] * 3 + _gate_specs(tq), out_specs=blk,
        out_shape=jax.ShapeDtypeStruct((t, w), BF16),
        compiler_params=_cparams(("parallel",)),
    )(*ps, u, u, u)


def _merge_bwd_call(ps, u, dy, *, tq=256):
    t, w = dy.shape

    def body(p0, p1, p2, g0, g1, g2, dy_ref, dg_ref, dp0, dp1, dp2):
        dyv = dy_ref[...]
        for j, (p, g, dp) in enumerate(((p0, g0, dp0), (p1, g1, dp1), (p2, g2, dp2))):
            sig = jax.nn.sigmoid(g[...])
            dp[...] = (dyv * sig).astype(BF16)
            dg_ref[:, j * w:(j + 1) * w] = (dyv * p[...] * sig * (1.0 - sig)).astype(BF16)

    blk = pl.BlockSpec((tq, w), lambda i: (i, 0))
    small = jax.ShapeDtypeStruct((t, w), BF16)
    return pl.pallas_call(
        body, name="merge_bwd", grid=(t // tq,), in_specs=[blk] * 3 + _gate_specs(tq) + [blk],
        out_specs=[pl.BlockSpec((tq, 3 * w), lambda i: (i, 0))] + [blk] * 3,
        out_shape=[_du_shape(t)] + [small] * 3,
        compiler_params=_cparams(("parallel",)),
    )(*ps, u, u, u, dy)


A_PARTS = ((COL["a_q"], 2 * A_W, True), (COL["a_v"], A_W, False))
B_PARTS = ((COL["b_q"], A_W, True), (COL["b_k"], B_KVW, True), (COL["b_v"], B_KVW, False))
BRANCHES = ("w_branch_a", "w_branch_b", "w_branch_c")


def _layer_fwd(x, tabs, p, wb):
    h = _rms_fwd_call(x, p["norm_mix"], name="rms_mix_fwd", out_dtype=BF16)
    u = _mm(h, wb["w_in"], bias=p["b_in"], tn=IN_TN, name="in_proj_fwd")
    qkv_a = _rope_gather_call(u, tabs, A_PARTS, name="rope_a_fwd")
    a_runs = [_attn_fwd(cfg, qkv_a, None) for cfg in ATTN_A_CFGS]
    os_, lses = tuple(r[0] for r in a_runs), tuple(r[1] for r in a_runs)
    ya = _combine_fwd_call(os_, lses)
    qkv_b = _rope_gather_call(u, tabs, B_PARTS, name="rope_b_fwd")
    yb, _, b_saved = _attn_fwd(ATTN_B_CFG, qkv_b, p["sinks"])
    zc, qk, v = _conv_prep_fwd_call(u, p["conv_w"])
    uu, ww, qd, kd, attn, tmat, gc, beta = _delta_prep_call(qk, v, u, p["a_log"], p["dt_bias"])
    o, vn, st = _delta_scan_call(uu, ww, qd, kd, attn, gc)
    yc = _gated_norm_fwd_call(o, u, p["c_norm"])
    ys = (ya, yb, yc)
    ps = tuple(_mm(y, wb[n], name="branch_fwd") for y, n in zip(ys, BRANCHES))
    merged = _merge_fwd_call(ps, u)
    x1 = _mm(merged, wb["w_out"], add=x, name="out_proj_fwd")
    h2 = _rms_fwd_call(x1, p["norm_ffn"], name="rms_ffn_fwd", out_dtype=BF16)
    pre, act = _mm(h2, wb["w_ff1"], relu2_out=True, name="ffn_up")
    x2 = _mm(act, wb["w_ff2"], add=x1, name="ffn_down")
    saved = dict(x=x, h=h, u=u, a_saved=[r[2] for r in a_runs], os_=os_, lses=lses, b_saved=b_saved,
                 zc=zc, qk=qk, v=v, delta=(tmat, uu, ww, gc, beta, qd, kd, attn, vn, st), o=o, ys=ys, ps=ps,
                 merged=merged, x1=x1, h2=h2, pre=pre, act=act)
    return x2, saved


def _layer_bwd(s, dx2, tabs, p, wb):
    g = {}
    t = dx2.shape[0]
    dpre = _mm(dx2, wb["w_ff2"], tb=True, mul_drelu2=s["pre"], out_dtype=BF16, name="ffn_dpre")
    g["w_ff2"] = _mm(s["act"], dx2, ta=True, tk=1024, name="ffn_dw2")
    g["w_ff1"] = _mm(s["h2"], dpre, ta=True, tk=1024, name="ffn_dw1")
    dh2 = _mm(dpre, wb["w_ff1"], tb=True, name="ffn_dh")
    dx1, g["norm_ffn"] = _rms_bwd_call(s["x1"], p["norm_ffn"], dh2, add=dx2, name="rms_ffn_bwd")
    dmerged = _mm(dx1, wb["w_out"], tb=True, name="out_proj_da")
    g["w_out"] = _mm(s["merged"], dx1, ta=True, tk=1024, name="out_proj_dw")
    du, *dps = _merge_bwd_call(s["ps"], s["u"], dmerged)
    dys = []
    for y, dp, n in zip(s["ys"], dps, BRANCHES):
        dys.append(_mm(dp, wb[n], tb=True, name="branch_da"))
        g[n] = _mm(y, dp, ta=True, tk=1024, name="branch_dw")
    dya, dyb, dyc = dys
    tmat, uu, ww, gc, beta, qd, kd, attn, vn, st = s["delta"]
    du, do, g["c_norm"] = _gated_norm_bwd_call(s["o"], s["u"], p["c_norm"], dyc, du)
    ddu, ddw, dqd, dkd, dattn, dgl = _delta_scan_bwd_call(do, ww, qd, kd, attn, gc, vn, st)
    du, dqk, dv, dpar = _delta_prep_bwd_call(s["qk"], s["v"], s["u"], p["a_log"], p["dt_bias"], tmat, uu, ww, gc,
                                             beta, ddu, ddw, dqd, dkd, dattn, dgl, du)
    g["a_log"], g["dt_bias"] = dpar[0:1], dpar[1:2]
    dzc = _conv_prep_dz_call(s["zc"], dqk, dv)
    du, dconv = _conv_bwd_call(s["u"], dzc, p["conv_w"], du)
    g["conv_w"] = dconv[:C_CONV]
    no_dlse = jnp.zeros((t, LANES), F32)
    dq, dk, dv_b, dsink = _attn_bwd(ATTN_B_CFG, s["b_saved"], p["sinks"], dyb, no_dlse)
    g["sinks"] = dsink[0, :p["sinks"].shape[0]]
    du = _rope_scatter_call(du, t, [([dq], A_W, True)], COL["b_q"], tabs, name="rope_bq_bwd")
    du = _rope_scatter_call(du, t, [([dk], B_KVW, True), ([dv_b], B_KVW, False)], COL["b_k"], tabs,
                            name="rope_bkv_bwd")
    *dos, dl0, dl1, dl2 = _combine_bwd_call(s["os_"], s["lses"], dya)
    grads_a = [_attn_bwd(cfg, sv, None, do_c, dl)[:3]
               for cfg, sv, do_c, dl in zip(ATTN_A_CFGS, s["a_saved"], dos, (dl0, dl1, dl2))]
    dqs, dks, dvs = zip(*grads_a)
    du = _rope_scatter_call(du, t, [(list(dqs), A_W, True), (list(dks), A_W, True), (list(dvs), A_W, False)],
                            COL["a_q"], tabs, name="rope_a_bwd")
    dh = _mm(du, wb["w_in"], tb=True, tk=IN_TN, name="in_proj_da")
    g["w_in"], g["b_in"] = _mm(s["h"], du, ta=True, b_colsum=True, tn=IN_TN, tk=1024, name="in_proj_dw")
    dx, g["norm_mix"] = _rms_bwd_call(s["x"], p["norm_mix"], dh, add=dx1, name="rms_mix_bwd")
    return dx, g


def _local_step(x, params, wbf, tabs, tgt):
    def of_layer(d, layer):
        return {n: w[layer] for n, w in d.items() if n != "norm_final"}

    saves = []
    for layer in range(DEPTH):
        x, s = _layer_fwd(x, tabs, of_layer(params, layer), of_layer(wbf, layer))
        saves.append(s)
    loss = _loss_fwd_call(x, params["norm_final"], tgt)
    dx, dfinal = _loss_bwd_call(x, params["norm_final"], tgt, jnp.ones((1, 1), F32))
    per_layer = []
    for layer in reversed(range(DEPTH)):
        dx, g = _layer_bwd(saves[layer], dx, tabs, of_layer(params, layer), of_layer(wbf, layer))
        per_layer.append(g)
    per_layer.reverse()
    grads = {n: jnp.stack([g[n] for g in per_layer]) for n in per_layer[0]}
    grads["norm_final"] = dfinal
    return loss, dx, grads


def _in_cols_to_kernel(w):
    lead = w.shape[:-1]
    parts, pos = [], 0
    for _, start, width, ref_start in IN_LAYOUT:
        if start > pos:
            parts.append(jnp.zeros(lead + (start - pos,), w.dtype))
        parts.append(w[..., ref_start:ref_start + width])
        pos = start + width
    parts.append(jnp.zeros(lead + (D_IN_PAD - pos,), w.dtype))
    return jnp.concatenate(parts, axis=-1)


def _in_cols_to_reference(w):
    by_ref = sorted(IN_LAYOUT, key=lambda e: e[3])
    return jnp.concatenate([w[..., start:start + width] for _, start, width, _ in by_ref], axis=-1)


W_IN_SHARD = 8464 // N_DEV


def _w_in_from_shards(blocks):
    lead = blocks.shape[1:-1]
    parts, pos = [], 0
    for _, start, width, ref_start in IN_LAYOUT:
        if start > pos:
            parts.append(jnp.zeros(lead + (start - pos,), blocks.dtype))
        col = ref_start
        while col < ref_start + width:
            d, l = divmod(col, W_IN_SHARD)
            n = min(W_IN_SHARD - l, ref_start + width - col)
            parts.append(blocks[d, ..., l:l + n])
            col += n
        pos = start + width
    parts.append(jnp.zeros(lead + (D_IN_PAD - pos,), blocks.dtype))
    return jnp.concatenate(parts, axis=-1)


def _w_in_to_shards(g):
    by_ref = sorted(IN_LAYOUT, key=lambda e: e[3])
    blocks = []
    for d in range(N_DEV):
        lo, hi = d * W_IN_SHARD, (d + 1) * W_IN_SHARD
        parts = []
        for _, start, width, ref_start in by_ref:
            a, b = max(lo, ref_start), min(hi, ref_start + width)
            if a < b:
                parts.append(g[..., start + a - ref_start:start + b - ref_start])
        blocks.append(jnp.concatenate(parts, axis=-1))
    return jnp.stack(blocks)


def _pad_lanes(v):
    return jnp.pad(v, ((0, 0), (0, LANES - v.shape[1])))[:, None, :]


BIG = (("w_in", 2), ("conv_w", 2), ("w_branch_a", 2), ("w_branch_b", 2), ("w_branch_c", 1), ("w_out", 1),
       ("w_ff1", 2), ("w_ff2", 1))
SMALL = ("norm_mix", "b_in", "a_log", "dt_bias", "sinks", "c_norm", "norm_ffn", "norm_final")
WEIGHTS = ("norm_mix", "w_in", "b_in", "conv_w", "a_log", "dt_bias", "sinks", "c_norm", "w_branch_a",
           "w_branch_b", "w_branch_c", "w_out", "norm_ffn", "w_ff1", "w_ff2", "norm_final")
MATMUL_WEIGHTS = ("w_in", "w_branch_a", "w_branch_b", "w_branch_c", "w_out", "w_ff1", "w_ff2")
PACK_ROWS = 1024
ROW_ALIGN = 16


def _seg_rows(n):
    return -(-n // (LANES * ROW_ALIGN)) * ROW_ALIGN


def _pack(arrays, lead=0):
    parts = []
    for a in arrays:
        lead_shape = a.shape[:lead]
        n = math.prod(a.shape[lead:])
        rows = _seg_rows(n)
        if rows * LANES != n:
            a = jnp.pad(a.reshape(lead_shape + (n,)), [(0, 0)] * lead + [(0, rows * LANES - n)])
        parts.append(a.reshape(lead_shape + (rows, LANES)))
    total = sum(p.shape[lead] for p in parts)
    padded = -(-total // PACK_ROWS) * PACK_ROWS
    if padded > total:
        parts.append(jnp.zeros(parts[0].shape[:lead] + (padded - total, LANES), parts[0].dtype))
    return jnp.concatenate(parts, axis=lead)


def _unpack(buf, shapes):
    lead = buf.shape[:-2]
    out, pos = [], 0
    for shp in shapes:
        n = math.prod(shp)
        rows = _seg_rows(n)
        seg = buf[..., pos:pos + rows, :]
        if rows * LANES != n:
            seg = seg.reshape(lead + (rows * LANES,))[..., :n]
        out.append(seg.reshape(lead + tuple(shp)))
        pos += rows
    return out


def _shards_to_full(blocks, axis):
    moved = jnp.moveaxis(blocks, 0, axis)
    shp = list(blocks.shape[1:])
    shp[axis] = shp[axis] * N_DEV
    return moved.reshape(shp)


def _full_to_shards(full, axis):
    shp = list(full.shape)
    shp[axis:axis + 1] = [N_DEV, shp[axis] // N_DEV]
    return jnp.moveaxis(full.reshape(shp), axis, 0)


def _my_place():
    return lax.axis_index("x"), lax.axis_index("y"), lax.axis_index("c")


def _slot(x, y, c):
    return 4 * x + 2 * y + c


def _all_gather(block, *, name):
    rows = block.shape[0]

    def body(x_ref, out_ref, send_sems, recv_sems, local_sem):
        x, y, c = _my_place()
        me, sibling = (x, y, c), (x, y, 1 - c)
        chips = [(1 - x, y), (x, 1 - y), (1 - x, 1 - y)]

        def copy(k, blk, to, src=None):
            dst = out_ref.at[_slot(*blk)]
            return pltpu.make_async_remote_copy(
                src_ref=dst if src is None else src, dst_ref=dst,
                send_sem=send_sems.at[k], recv_sem=recv_sems.at[k], device_id=to, device_id_type=MESH_ID)

        mine = pltpu.make_async_copy(x_ref, out_ref.at[_slot(*me)], local_sem)
        mine.start()
        first = [copy(0, me, sibling, src=x_ref)]
        first += [copy(1 + j, me, (*chip, c), src=x_ref) for j, chip in enumerate(chips)]
        for cp in first:
            cp.start()
        passed = [copy(4 + j, (*chip, c), sibling) for j, chip in enumerate(chips)]
        for j, chip in enumerate(chips):
            copy(1 + j, (*chip, c), me).wait_recv()
            passed[j].start()
        copy(0, sibling, me).wait_recv()
        for j, chip in enumerate(chips):
            copy(4 + j, (*chip, 1 - c), me).wait_recv()
        for cp in first + passed:
            cp.wait_send()
        mine.wait()

    return pl.pallas_call(
        body, name=name,
        out_shape=jax.ShapeDtypeStruct((N_DEV, rows, LANES), block.dtype),
        in_specs=[HBM_SPEC], out_specs=HBM_SPEC,
        scratch_shapes=[pltpu.SemaphoreType.DMA((7,)), pltpu.SemaphoreType.DMA((7,)), pltpu.SemaphoreType.DMA],
    )(block)


N_CHIP = N_DEV // 2


def _swap_with_sibling(blocks, *, name):
    def body(g_ref, out_ref, send_sem, recv_sem):
        x, y, c = _my_place()
        cp = pltpu.make_async_remote_copy(src_ref=g_ref, dst_ref=out_ref, send_sem=send_sem, recv_sem=recv_sem,
                                          device_id=(x, y, 1 - c), device_id_type=MESH_ID)
        cp.start()
        cp.wait_recv()
        cp.wait_send()

    return pl.pallas_call(
        body, name=name,
        out_shape=jax.ShapeDtypeStruct(blocks.shape, blocks.dtype),
        in_specs=[HBM_SPEC], out_specs=HBM_SPEC,
        scratch_shapes=[pltpu.SemaphoreType.DMA, pltpu.SemaphoreType.DMA],
    )(blocks)


def _chip_all_to_all(blocks, *, name):
    def body(g_ref, out_ref, send_sems, recv_sems, local_sem):
        x, y, c = _my_place()
        mine_slot = 2 * x + y
        local = pltpu.make_async_copy(g_ref.at[mine_slot], out_ref.at[mine_slot], local_sem)
        local.start()
        copies = []
        for k in range(1, N_CHIP):
            px, py = x ^ (k >> 1), y ^ (k & 1)
            copies.append(pltpu.make_async_remote_copy(
                src_ref=g_ref.at[2 * px + py], dst_ref=out_ref.at[mine_slot],
                send_sem=send_sems.at[k - 1], recv_sem=recv_sems.at[k - 1],
                device_id=(px, py, c), device_id_type=MESH_ID))
        for cp in copies:
            cp.start()
        for cp in copies:
            cp.wait_recv()
        for cp in copies:
            cp.wait_send()
        local.wait()

    return pl.pallas_call(
        body, name=name,
        out_shape=jax.ShapeDtypeStruct(blocks.shape, blocks.dtype),
        in_specs=[HBM_SPEC], out_specs=HBM_SPEC,
        scratch_shapes=[pltpu.SemaphoreType.DMA((N_CHIP - 1,)), pltpu.SemaphoreType.DMA((N_CHIP - 1,)),
                        pltpu.SemaphoreType.DMA],
    )(blocks)


def _add_bf16_call(a, b, *, name):
    n, rows, _ = a.shape
    tr = min(PACK_ROWS, rows)

    def body(a_ref, b_ref, o_ref):
        o_ref[...] = (a_ref[...].astype(F32) + b_ref[...].astype(F32)).astype(BF16)

    blk = pl.BlockSpec((n, tr, LANES), lambda i: (0, i, 0))
    return pl.pallas_call(
        body, name=name, grid=(rows // tr,), in_specs=[blk, blk], out_specs=blk,
        out_shape=jax.ShapeDtypeStruct(a.shape, BF16),
        compiler_params=_cparams(("parallel",)),
    )(a, b)


def _adamw_call(parts, w, m, v, *, name):
    rows = w.shape[0]
    tr = min(PACK_ROWS, rows)
    assert rows % tr == 0
    n_parts = parts.shape[0]

    def body(p_ref, w_ref, m_ref, v_ref, g_ref, d_ref, nm_ref, nv_ref):
        g = p_ref[0].astype(F32)
        for s in range(1, n_parts):
            g = g + p_ref[s].astype(F32)
        nm = ADAM_B1 * m_ref[...] + (1.0 - ADAM_B1) * g
        nv = ADAM_B2 * v_ref[...] + (1.0 - ADAM_B2) * jnp.square(g)
        m_hat = nm / (1.0 - ADAM_B1 ** ADAM_STEP)
        v_hat = nv / (1.0 - ADAM_B2 ** ADAM_STEP)
        g_ref[...] = g
        nm_ref[...] = nm
        nv_ref[...] = nv
        d_ref[...] = -ADAM_LR * (m_hat / (jnp.sqrt(v_hat) + ADAM_EPS) + ADAM_WD * w_ref[...])

    blk = pl.BlockSpec((tr, LANES), lambda i: (i, 0))
    shape = jax.ShapeDtypeStruct((rows, LANES), F32)
    return pl.pallas_call(
        body, name=name, grid=(rows // tr,),
        in_specs=[pl.BlockSpec((n_parts, tr, LANES), lambda i: (0, i, 0)), blk, blk, blk],
        out_specs=[blk] * 4, out_shape=[shape] * 4,
        compiler_params=_cparams(("parallel",)),
    )(parts, w, m, v)


def _kernel_params(full):
    return {
        "norm_mix": full["norm_mix"][:, None, :],
        "b_in": _in_cols_to_kernel(full["b_in"])[:, None, :],
        "conv_w": full["conv_w"],
        "a_log": _pad_lanes(full["a_log"]),
        "dt_bias": _pad_lanes(full["dt_bias"]),
        "sinks": full["sinks"],
        "c_norm": full["c_norm"][:, None, :],
        "norm_ffn": full["norm_ffn"][:, None, :],
        "norm_final": full["norm_final"][None, :],
    }


def _reference_grads(g):
    return {
        "norm_mix": g["norm_mix"][:, 0, :],
        "b_in": _in_cols_to_reference(g["b_in"][:, 0, :]),
        "conv_w": g["conv_w"],
        "a_log": g["a_log"][:, 0, :C_V_HEADS],
        "dt_bias": g["dt_bias"][:, 0, :C_V_HEADS],
        "sinks": g["sinks"],
        "c_norm": g["c_norm"][:, 0, :],
        "w_branch_a": g["w_branch_a"], "w_branch_b": g["w_branch_b"], "w_branch_c": g["w_branch_c"],
        "w_out": g["w_out"],
        "norm_ffn": g["norm_ffn"][:, 0, :],
        "w_ff1": g["w_ff1"], "w_ff2": g["w_ff2"],
        "norm_final": g["norm_final"][0],
    }


def kernel(x, positions, norm_mix, w_in, b_in, conv_w, a_log, dt_bias, sinks, c_norm, w_branch_a, w_branch_b, w_branch_c, w_out, norm_ffn, w_ff1, w_ff2, norm_final, loss_target, m_norm_mix, m_w_in, m_b_in, m_conv_w, m_a_log, m_dt_bias, m_sinks, m_c_norm, m_w_branch_a, m_w_branch_b, m_w_branch_c, m_w_out, m_norm_ffn, m_w_ff1, m_w_ff2, m_norm_final, v_norm_mix, v_w_in, v_b_in, v_conv_w, v_a_log, v_dt_bias, v_sinks, v_c_norm, v_w_branch_a, v_w_branch_b, v_w_branch_c, v_w_out, v_norm_ffn, v_w_ff1, v_w_ff2, v_norm_final):
    env = dict(locals())
    weights = {n: env[n] for n in WEIGHTS}
    moments_m = {n: env["m_" + n] for n in WEIGHTS}
    moments_v = {n: env["v_" + n] for n in WEIGHTS}

    axis_of = dict(BIG)
    cw = weights["conv_w"]
    c1 = cw.astype(BF16)
    c2 = (cw - c1.astype(F32)).astype(BF16)
    c3 = (cw - c1.astype(F32) - c2.astype(F32)).astype(BF16)
    gathered = _all_gather(_pack([weights[n].astype(BF16) for n in MATMUL_WEIGHTS] + [c1, c2, c3]),
                           name="gather_weights")
    blocks = _unpack(gathered, [weights[n].shape for n in MATMUL_WEIGHTS] + [cw.shape] * 3)
    wbf = {n: _w_in_from_shards(blk) if n == "w_in" else _shards_to_full(blk, axis_of[n])
           for n, blk in zip(MATMUL_WEIGHTS, blocks)}
    full = {n: weights[n] for n in SMALL}
    full["conv_w"] = _shards_to_full(sum(b.astype(F32) for b in blocks[-3:]), axis_of["conv_w"])

    tabs = rope_tables(positions[0])
    loss, dx, dparams = _local_step(x[0], _kernel_params(full), wbf, tabs, loss_target[0])
    grads = _reference_grads(dparams)
    loss = lax.psum(loss, ("x", "y", "c"))

    core = lax.axis_index("c")

    def by_core(n, axis, which):
        sh = _w_in_to_shards(dparams[n]) if n == "w_in" else _full_to_shards(grads[n], axis)
        sh = sh.reshape((N_CHIP, 2) + sh.shape[1:])
        return lax.dynamic_index_in_dim(sh, which, axis=1, keepdims=False).astype(BF16)

    from_sibling = _swap_with_sibling(_pack([by_core(n, axis, 1 - core) for n, axis in BIG], lead=1),
                                      name="scatter_grads_d2d")
    chip_sum = _add_bf16_call(_pack([by_core(n, axis, core) for n, axis in BIG], lead=1), from_sibling,
                              name="scatter_grads_add")
    big_parts = _chip_all_to_all(chip_sum, name="scatter_grads_ici")
    small_parts = _all_gather(_pack([grads[n] for n in SMALL]), name="gather_small_grads")

    out = {}
    for names, parts in (([n for n, _ in BIG], big_parts), (list(SMALL), small_parts)):
        shapes = [weights[n].shape for n in names]
        packed = [_pack([d[n] for n in names]) for d in (weights, moments_m, moments_v)]
        results = _adamw_call(parts, *packed, name="adamw_" + names[0])
        for kind, buf in zip(("grad", "delta", "new_m", "new_v"), results):
            for n, arr in zip(names, _unpack(buf, shapes)):
                out[kind, n] = arr
    return (loss, dx[None], *[out[kind, n] for kind in ("grad", "delta", "new_m", "new_v") for n in WEIGHTS])
```

```python
import math

import jax
import jax.numpy as jnp
from jax import lax
from jax.experimental import pallas as pl
from jax.experimental.pallas import tpu as pltpu

F32 = jnp.float32
BF16 = jnp.bfloat16

N_DEV = 8
D_MODEL = 1024
DEPTH = 2
HEAD_DIM = 64
ROT_DIM = 16
ROPE_THETA = 500000.0
BLK = 128
NEG_INF = -1e30
EPS = 1e-6
A_CONFIGS = ((128, 1), (512, 4), (2048, 16))
B_GROUP = 4
C_QK_HEADS = 4
C_V_HEADS = 8
C_DK = 128
C_CONV = 4
CHUNK = 64
ADAM_LR = 0.001
ADAM_B1 = 0.9
ADAM_B2 = 0.999
ADAM_EPS = 1e-08
ADAM_WD = 0.01
ADAM_STEP = 10

IN_LAYOUT = (
    ("gate_a", 0, 1024, 5392), ("gate_b", 1024, 1024, 6416), ("gate_c", 2048, 1024, 7440),
    ("a_q", 3072, 512, 0), ("a_k", 3584, 512, 512), ("a_v", 4096, 512, 1024), ("b_q", 4608, 512, 1536),
    ("c_z", 5120, 1024, 4352), ("c_qkv", 6144, 2048, 2304),
    ("b_k", 8192, 128, 2048), ("b_v", 8320, 128, 2176), ("c_ab", 8448, 16, 5376),
)
COL = {name: start for name, start, _, _ in IN_LAYOUT}
D_IN_PAD = 8704
IN_TN = D_IN_PAD // 4
LANES = 128
VMEM_LIMIT = 56 * 1024 * 1024


def _cparams(sem=None):
    return pltpu.CompilerParams(dimension_semantics=sem, vmem_limit_bytes=VMEM_LIMIT)


def _relu2(t):
    return jnp.square(jnp.maximum(t, 0.0))


def _mm(a, b, *, ta=False, tb=False, bias=None, a_fn=None, mul_drelu2=None, add=None,
        out_dtype=F32, relu2_out=False, b_colsum=False, tm=1024, tn=1024, tk=2048, name):
    if ta:
        kdim, m = a.shape
    else:
        m, kdim = a.shape
    n = b.shape[0] if tb else b.shape[1]
    tm, tn, tk = min(tm, m), min(tn, n), min(tk, kdim)
    assert m % tm == 0 and n % tn == 0 and kdim % tk == 0, (a.shape, b.shape, tm, tn, tk)
    nk = kdim // tk
    assert not b_colsum or (m == tm and not tb and nk > 1)
    dims = (((0 if ta else 1,), (1 if tb else 0,)), ((), ()))
    extras = [e for e in (bias, mul_drelu2, add) if e is not None]

    def body(*refs):
        a_ref, b_ref = refs[0], refs[1]
        pos = 2
        bias_ref = pre_ref = add_ref = None
        if bias is not None:
            bias_ref = refs[pos]; pos += 1
        if mul_drelu2 is not None:
            pre_ref = refs[pos]; pos += 1
        if add is not None:
            add_ref = refs[pos]; pos += 1
        o_ref = refs[pos]
        pos += 1
        r_ref = None
        if relu2_out:
            r_ref = refs[pos]; pos += 1
        cs_ref = None
        if b_colsum:
            cs_ref = refs[pos]; pos += 1
        acc_ref = refs[pos] if nk > 1 else None
        cs_acc = refs[pos + 1] if b_colsum else None

        av = a_ref[...]
        if a_fn is not None:
            av = a_fn(av)
        bv = b_ref[...]
        part = lax.dot_general(av.astype(BF16), bv.astype(BF16), dims,
                               preferred_element_type=F32)
        if b_colsum:
            cs_part = jnp.sum(bv.astype(F32).reshape(tk // 8, 8, tn), axis=0)

        def finish(acc):
            if bias_ref is not None:
                acc = acc + bias_ref[...]
            if pre_ref is not None:
                acc = acc * (2.0 * jnp.maximum(pre_ref[...], 0.0))
            if add_ref is not None:
                acc = acc + add_ref[...]
            o_ref[...] = acc.astype(out_dtype)
            if r_ref is not None:
                r_ref[...] = _relu2(acc).astype(BF16)

        if nk == 1:
            finish(part)
        else:
            k = pl.program_id(2)

            @pl.when(k == 0)
            def _():
                acc_ref[...] = part
                if b_colsum:
                    cs_acc[...] = cs_part

            @pl.when(k > 0)
            def _():
                acc_ref[...] += part
                if b_colsum:
                    cs_acc[...] += cs_part

            @pl.when(k == nk - 1)
            def _():
                finish(acc_ref[...])
                if b_colsum:
                    cs_ref[...] = jnp.sum(cs_acc[...], axis=0, keepdims=True)

    a_spec = (pl.BlockSpec((tk, tm), lambda i, j, k: (k, i)) if ta
              else pl.BlockSpec((tm, tk), lambda i, j, k: (i, k)))
    b_spec = (pl.BlockSpec((tn, tk), lambda i, j, k: (j, k)) if tb
              else pl.BlockSpec((tk, tn), lambda i, j, k: (k, j)))
    in_specs = [a_spec, b_spec]
    if bias is not None:
        in_specs.append(pl.BlockSpec((1, tn), lambda i, j, k: (0, j)))
    for _ in extras[(1 if bias is not None else 0):]:
        in_specs.append(pl.BlockSpec((tm, tn), lambda i, j, k: (i, j)))
    o_spec = pl.BlockSpec((tm, tn), lambda i, j, k: (i, j))
    out_specs, out_shape = [o_spec], [jax.ShapeDtypeStruct((m, n), out_dtype)]
    scratch = [pltpu.VMEM((tm, tn), F32)] if nk > 1 else []
    if relu2_out:
        out_specs.append(o_spec)
        out_shape.append(jax.ShapeDtypeStruct((m, n), BF16))
    if b_colsum:
        out_specs.append(pl.BlockSpec((1, tn), lambda i, j, k: (0, j)))
        out_shape.append(jax.ShapeDtypeStruct((1, n), F32))
        scratch.append(pltpu.VMEM((8, tn), F32))
    single = len(out_specs) == 1
    return pl.pallas_call(
        body, name=name,
        grid=(m // tm, n // tn, nk),
        in_specs=in_specs,
        out_specs=out_specs[0] if single else out_specs,
        out_shape=out_shape[0] if single else out_shape,
        scratch_shapes=scratch,
        compiler_params=_cparams(("parallel", "parallel", "arbitrary")),
    )(a, b, *extras)


def _rms_fwd_call(x, g, *, name, out_dtype=F32, tq=512):
    t, d = x.shape

    def body(x_ref, g_ref, y_ref):
        xv = x_ref[...]
        r = lax.rsqrt(jnp.mean(xv * xv, axis=-1, keepdims=True) + EPS)
        y_ref[...] = (xv * r * g_ref[...]).astype(out_dtype)

    return pl.pallas_call(
        body, name=name, grid=(t // tq,),
        in_specs=[pl.BlockSpec((tq, d), lambda i: (i, 0)), pl.BlockSpec((1, d), lambda i: (0, 0))],
        out_specs=pl.BlockSpec((tq, d), lambda i: (i, 0)),
        out_shape=jax.ShapeDtypeStruct((t, d), out_dtype),
        compiler_params=_cparams(("parallel",)),
    )(x, g)


def _rms_bwd_call(x, g, dy, *, name, add=None, tq=512):
    t, d = x.shape
    nt = t // tq

    def body(*refs):
        if add is None:
            x_ref, g_ref, dy_ref, dx_ref, dg_ref, acc_ref = refs
        else:
            x_ref, g_ref, dy_ref, add_ref, dx_ref, dg_ref, acc_ref = refs
        i = pl.program_id(0)
        xv = x_ref[...]
        r = lax.rsqrt(jnp.mean(xv * xv, axis=-1, keepdims=True) + EPS)
        xh = xv * r
        dyv = dy_ref[...]
        dxh = dyv * g_ref[...]
        dx = r * (dxh - xh * jnp.mean(dxh * xh, axis=-1, keepdims=True))
        dx_ref[...] = dx if add is None else dx + add_ref[...]
        part = jnp.sum((dyv * xh).reshape(tq // 8, 8, d), axis=0)

        @pl.when(i == 0)
        def _():
            acc_ref[...] = part

        @pl.when(i > 0)
        def _():
            acc_ref[...] += part

        @pl.when(i == nt - 1)
        def _():
            dg_ref[...] = jnp.sum(acc_ref[...], axis=0, keepdims=True)

    blk = pl.BlockSpec((tq, d), lambda i: (i, 0))
    row = pl.BlockSpec((1, d), lambda i: (0, 0))
    extra = [] if add is None else [add]
    return pl.pallas_call(
        body, name=name, grid=(nt,),
        in_specs=[blk, row, blk] + [blk] * len(extra),
        out_specs=[blk, row],
        out_shape=[jax.ShapeDtypeStruct((t, d), F32), jax.ShapeDtypeStruct((1, d), F32)],
        scratch_shapes=[pltpu.VMEM((8, d), F32)],
        compiler_params=_cparams(("arbitrary",)),
    )(x, g, dy, *extra)


def _loss_fwd_call(x, g, tgt, *, tq=512):
    t, d = x.shape
    nt = t // tq

    def body(x_ref, g_ref, t_ref, o_ref, acc_ref):
        i = pl.program_id(0)
        xv = x_ref[...]
        r = lax.rsqrt(jnp.mean(xv * xv, axis=-1, keepdims=True) + EPS)
        err = xv * r * g_ref[...] - t_ref[...]
        part = jnp.sum((err * err).reshape(tq // 8, 8, d), axis=0)

        @pl.when(i == 0)
        def _():
            acc_ref[...] = part

        @pl.when(i > 0)
        def _():
            acc_ref[...] += part

        @pl.when(i == nt - 1)
        def _():
            tot = jnp.sum(jnp.sum(acc_ref[...], axis=0, keepdims=True), axis=1, keepdims=True)
            o_ref[...] = jnp.broadcast_to(tot * (0.5 / d), (8, LANES))

    out = pl.pallas_call(
        body, name="loss_fwd", grid=(nt,),
        in_specs=[pl.BlockSpec((tq, d), lambda i: (i, 0)), pl.BlockSpec((1, d), lambda i: (0, 0)),
                  pl.BlockSpec((tq, d), lambda i: (i, 0))],
        out_specs=pl.BlockSpec((8, LANES), lambda i: (0, 0)),
        out_shape=jax.ShapeDtypeStruct((8, LANES), F32),
        scratch_shapes=[pltpu.VMEM((8, d), F32)],
        compiler_params=_cparams(("arbitrary",)),
    )(x, g, tgt)
    return out[0, 0]


def _loss_bwd_call(x, g, tgt, scale, *, tq=512):
    t, d = x.shape
    nt = t // tq

    def body(x_ref, g_ref, t_ref, s_ref, dx_ref, dg_ref, acc_ref):
        i = pl.program_id(0)
        xv = x_ref[...]
        r = lax.rsqrt(jnp.mean(xv * xv, axis=-1, keepdims=True) + EPS)
        xh = xv * r
        gv = g_ref[...]
        dyv = (xh * gv - t_ref[...]) * (s_ref[...] * (1.0 / d))
        dxh = dyv * gv
        dx_ref[...] = r * (dxh - xh * jnp.mean(dxh * xh, axis=-1, keepdims=True))
        part = jnp.sum((dyv * xh).reshape(tq // 8, 8, d), axis=0)

        @pl.when(i == 0)
        def _():
            acc_ref[...] = part

        @pl.when(i > 0)
        def _():
            acc_ref[...] += part

        @pl.when(i == nt - 1)
        def _():
            dg_ref[...] = jnp.sum(acc_ref[...], axis=0, keepdims=True)

    return pl.pallas_call(
        body, name="loss_bwd", grid=(nt,),
        in_specs=[pl.BlockSpec((tq, d), lambda i: (i, 0)), pl.BlockSpec((1, d), lambda i: (0, 0)),
                  pl.BlockSpec((tq, d), lambda i: (i, 0)), pl.BlockSpec((1, 1), lambda i: (0, 0))],
        out_specs=[pl.BlockSpec((tq, d), lambda i: (i, 0)), pl.BlockSpec((1, d), lambda i: (0, 0))],
        out_shape=[jax.ShapeDtypeStruct((t, d), F32), jax.ShapeDtypeStruct((1, d), F32)],
        scratch_shapes=[pltpu.VMEM((8, d), F32)],
        compiler_params=_cparams(("arbitrary",)),
    )(x, g, tgt, scale)


MESH_ID = pl.DeviceIdType.MESH
HBM_SPEC = pl.BlockSpec(memory_space=pl.ANY)


def rope_tables(positions):
    inv_freq = jnp.power(ROPE_THETA, -jnp.arange(0, ROT_DIM, 2, dtype=F32) / ROT_DIM)
    ang = positions.astype(F32)[:, None] * inv_freq
    cos, sin = jnp.cos(ang), jnp.sin(ang)
    t = positions.shape[0]
    one = jnp.ones((t, HEAD_DIM - ROT_DIM), F32)
    zero8 = jnp.zeros((t, ROT_DIM // 2), F32)
    zero = jnp.zeros((t, HEAD_DIM - ROT_DIM), F32)
    a = jnp.concatenate([cos, cos, one], axis=1)
    b = jnp.concatenate([zero8, sin, zero], axis=1)
    c = jnp.concatenate([-sin, zero8, zero], axis=1)
    return tuple(jnp.concatenate([m, m], axis=1) for m in (a, b, c))


def _rope_chunk(xs, a, b, c, transpose):
    half = ROT_DIM // 2
    if transpose:
        return xs * a + pltpu.roll(xs * b, LANES - half, 1) + pltpu.roll(xs * c, half, 1)
    return xs * a + pltpu.roll(xs, half, 1) * b + pltpu.roll(xs, LANES - half, 1) * c


def _rope_gather_call(u, tabs, parts, *, name, tq=512):
    t = u.shape[0]
    total = sum(w for _, w, _ in parts)
    assert all(start % w == 0 for start, w, _ in parts)

    def body(a_ref, b_ref, c_ref, *refs):
        o_ref = refs[-1]
        a, b, c = a_ref[...], b_ref[...], c_ref[...]
        off = 0
        for x_ref, (_, w, roped) in zip(refs[:-1], parts):
            for j in range(w // LANES):
                xs = x_ref[:, j * LANES:(j + 1) * LANES]
                val = _rope_chunk(xs, a, b, c, False) if roped else xs
                o_ref[:, off + j * LANES:off + (j + 1) * LANES] = val.astype(BF16)
            off += w

    tab_spec = pl.BlockSpec((tq, LANES), lambda i: (i, 0))
    return pl.pallas_call(
        body, name=name, grid=(t // tq,),
        in_specs=[tab_spec] * 3 + [pl.BlockSpec((tq, w), lambda i, cb=start // w: (i, cb)) for start, w, _ in parts],
        out_specs=pl.BlockSpec((tq, total), lambda i: (i, 0)),
        out_shape=jax.ShapeDtypeStruct((t, total), BF16),
        compiler_params=_cparams(("parallel",)),
    )(*tabs, *([u] * len(parts)))


def _du_operands(du_buf, n_inputs):
    if du_buf is None:
        return [], [], {}
    return [du_buf], [HBM_SPEC], {n_inputs: 0}


def _du_shape(t):
    return jax.ShapeDtypeStruct((t, D_IN_PAD), BF16)


def _rope_scatter_call(du_buf, t, pieces, col, tabs, *, name, tq=512):
    total = sum(w for _, w, _ in pieces)
    assert col % total == 0
    arrays = [a for arrs, _, _ in pieces for a in arrs]
    extra, extra_specs, aliases = _du_operands(du_buf, 3 + len(arrays))

    def body(a_ref, b_ref, c_ref, *refs):
        o_ref = refs[len(arrays) + len(extra)]
        a, b, c = a_ref[...], b_ref[...], c_ref[...]
        k = off = 0
        for arrs, w, roped in pieces:
            mine = refs[k:k + len(arrs)]
            k += len(arrs)
            for j in range(w // LANES):
                cs = slice(j * LANES, (j + 1) * LANES)
                xs = mine[0][:, cs].astype(F32)
                for r in mine[1:]:
                    xs = xs + r[:, cs].astype(F32)
                val = _rope_chunk(xs, a, b, c, True) if roped else xs
                o_ref[:, off + j * LANES:off + (j + 1) * LANES] = val.astype(BF16)
            off += w

    tab_spec = pl.BlockSpec((tq, LANES), lambda i: (i, 0))
    in_specs = [tab_spec] * 3 + [pl.BlockSpec((tq, w), lambda i: (i, 0)) for arrs, w, _ in pieces for _ in arrs]
    return pl.pallas_call(
        body, name=name, grid=(t // tq,),
        in_specs=in_specs + extra_specs,
        out_specs=pl.BlockSpec((tq, total), lambda i: (i, col // total)),
        out_shape=_du_shape(t), input_output_aliases=aliases,
        compiler_params=_cparams(("parallel",)),
    )(*tabs, *arrays, *extra)


def _band_masks(first_block, max_dist):
    qi = lax.broadcasted_iota(jnp.int32, (BLK, BLK), 0)
    kj = lax.broadcasted_iota(jnp.int32, (BLK, BLK), 1)
    valid_prev = jnp.logical_and(kj >= qi + (BLK - max_dist), jnp.logical_not(first_block))
    valid_cur = kj <= qi
    return valid_prev, valid_cur


_NN = (((1,), (0,)), ((), ()))
_NT = (((1,), (1,)), ((), ()))
_TN = (((0,), (0,)), ((), ()))


HEAD_STAGE = 8


def _attn_row_maps(nb):
    def cur(i):
        return jnp.minimum(i, nb - 1)

    def prev(i):
        return jnp.maximum(jnp.minimum(i, nb - 1) - 1, 0)

    return cur, prev


def _dil_view(a, dil):
    t, w = a.shape
    return a.reshape(t // dil, dil * w)


def _dil_spec(w, dil, rows, seg=None, off=0):
    seg = w if seg is None else seg
    assert off % w == 0 and (dil == 1 or seg % w == 0)
    return pl.BlockSpec((BLK, w), lambda r, i: (rows(i), (r * seg + off) // w))


def _dil_shape(l, dil, w, dtype=F32):
    return jax.ShapeDtypeStruct((l, dil * w), dtype)


def _attn_fwd_call(qkv2, sink, *, dil, group, max_dist, seg, offs, qw, kw, name):
    l = qkv2.shape[0]
    nh = qw // HEAD_DIM
    nb = l // BLK
    scale = HEAD_DIM ** -0.5
    use_sink = sink is not None

    def body(*refs):
        if use_sink:
            sink_ref, refs = refs[0], refs[1:]
        q_ref, kp_ref, kc_ref, vp_ref, vc_ref, o_ref, lse_ref = refs
        valid_prev, valid_cur = _band_masks(pl.program_id(1) == 0, max_dist)
        lane = lax.broadcasted_iota(jnp.int32, (BLK, LANES), 1)
        lse_tile = jnp.zeros((BLK, LANES), F32)

        def dot(a, b, dims=_NN):
            return lax.dot_general(a, b, dims, preferred_element_type=F32)

        for g0 in range(0, nh, HEAD_STAGE):
            heads = list(range(g0, min(g0 + HEAD_STAGE, nh)))
            kv = {}
            for kh in sorted({h // group for h in heads}):
                ks = slice(kh * HEAD_DIM, (kh + 1) * HEAD_DIM)
                kv[kh] = tuple(ref[:, ks].astype(BF16) for ref in (kp_ref, kc_ref, vp_ref, vc_ref))
            qs = [q_ref[:, h * HEAD_DIM:(h + 1) * HEAD_DIM].astype(BF16) for h in heads]
            sps = [jnp.where(valid_prev, dot(qh, kv[h // group][0], _NT) * scale, NEG_INF) for h, qh in zip(heads, qs)]
            scs = [jnp.where(valid_cur, dot(qh, kv[h // group][1], _NT) * scale, NEG_INF) for h, qh in zip(heads, qs)]
            ms = [jnp.maximum(jnp.max(sp, axis=1, keepdims=True), jnp.max(sc, axis=1, keepdims=True))
                  for sp, sc in zip(sps, scs)]
            if use_sink:
                ms = [jnp.maximum(m, sink_ref[h]) for h, m in zip(heads, ms)]
            pps = [jnp.exp(sp - m) for sp, m in zip(sps, ms)]
            pcs = [jnp.exp(sc - m) for sc, m in zip(scs, ms)]
            dens = [jnp.sum(pp, axis=1, keepdims=True) + jnp.sum(pc, axis=1, keepdims=True)
                    for pp, pc in zip(pps, pcs)]
            if use_sink:
                dens = [den + jnp.exp(sink_ref[h] - m) for h, den, m in zip(heads, dens, ms)]
            outs = [dot(pp.astype(BF16), kv[h // group][2]) + dot(pc.astype(BF16), kv[h // group][3])
                    for h, pp, pc in zip(heads, pps, pcs)]
            for h, o, den, m in zip(heads, outs, dens, ms):
                o_ref[:, h * HEAD_DIM:(h + 1) * HEAD_DIM] = o / den
                lse_tile = jnp.where(lane == h, m + jnp.log(den), lse_tile)
        lse_ref[...] = lse_tile

    cur, prev = _attn_row_maps(nb)
    o_spec, lse_spec = _dil_spec(qw, dil, cur), _dil_spec(LANES, dil, cur)
    in_specs = [_dil_spec(qw, dil, cur, seg, offs[0]),
                _dil_spec(kw, dil, prev, seg, offs[1]), _dil_spec(kw, dil, cur, seg, offs[1]),
                _dil_spec(kw, dil, prev, seg, offs[2]), _dil_spec(kw, dil, cur, seg, offs[2])]
    args = [qkv2] * 5
    if use_sink:
        in_specs = [pl.BlockSpec(memory_space=pltpu.SMEM)] + in_specs
        args = [sink] + args
    return pl.pallas_call(
        body, name=name, grid=(dil, nb),
        in_specs=in_specs,
        out_specs=[o_spec, lse_spec],
        out_shape=[_dil_shape(l, dil, qw), _dil_shape(l, dil, LANES)],
        compiler_params=_cparams(("parallel", "parallel")),
    )(*args)


def _attn_bwd_call(qkv2, sink, o2, lse2, do2, dlse2, *, dil, group, max_dist, seg, offs, qw, kw, name):
    l = qkv2.shape[0]
    nh = qw // HEAD_DIM
    nb = l // BLK
    scale = HEAD_DIM ** -0.5
    use_sink = sink is not None

    def body(*refs):
        if use_sink:
            sink_ref, refs = refs[0], refs[1:]
        (q_ref, kp_ref, kc_ref, vp_ref, vc_ref, o_ref, lse_ref, do_ref, dlse_ref,
         dq_ref, dk_ref, dv_ref, dsink_ref, ck_ref, cv_ref) = refs
        step = pl.program_id(1)

        @pl.when(jnp.logical_and(pl.program_id(0) == 0, step == 0))
        def _():
            dsink_ref[...] = jnp.zeros_like(dsink_ref)

        @pl.when(step == 0)
        def _():
            ck_ref[...] = jnp.zeros_like(ck_ref)
            cv_ref[...] = jnp.zeros_like(cv_ref)

        def dot(a, b, dims=_NN):
            return lax.dot_general(a, b, dims, preferred_element_type=F32)

        @pl.when(step < nb)
        def _():
            valid_prev, valid_cur = _band_masks(step == 0, max_dist)
            row = lax.broadcasted_iota(jnp.int32, (8, LANES), 0)
            lanes8 = lax.broadcasted_iota(jnp.int32, (8, LANES), 1)
            ds_tile = jnp.zeros((8, LANES), F32)
            for g0 in range(0, nh, HEAD_STAGE):
                heads = list(range(g0, min(g0 + HEAD_STAGE, nh)))
                hss = [slice(h * HEAD_DIM, (h + 1) * HEAD_DIM) for h in heads]
                kv = {}
                for kh in sorted({h // group for h in heads}):
                    ks = slice(kh * HEAD_DIM, (kh + 1) * HEAD_DIM)
                    kv[kh] = tuple(ref[:, ks].astype(BF16) for ref in (kp_ref, kc_ref, vp_ref, vc_ref))
                qs = [q_ref[:, hs].astype(BF16) for hs in hss]
                dos = [do_ref[:, hs] for hs in hss]
                dobs = [d.astype(BF16) for d in dos]
                lses = [lse_ref[:, h:h + 1] for h in heads]
                sps = [dot(qh, kv[h // group][0], _NT) * scale for h, qh in zip(heads, qs)]
                scs = [dot(qh, kv[h // group][1], _NT) * scale for h, qh in zip(heads, qs)]
                dpps = [dot(dob, kv[h // group][2], _NT) for h, dob in zip(heads, dobs)]
                dpcs = [dot(dob, kv[h // group][3], _NT) for h, dob in zip(heads, dobs)]
                pps = [jnp.where(valid_prev, jnp.exp(jnp.where(valid_prev, sp, NEG_INF) - ls), 0.0)
                       for sp, ls in zip(sps, lses)]
                pcs = [jnp.where(valid_cur, jnp.exp(jnp.where(valid_cur, sc, NEG_INF) - ls), 0.0)
                       for sc, ls in zip(scs, lses)]
                deltas = [jnp.sum(d * o_ref[:, hs], axis=1, keepdims=True) for d, hs in zip(dos, hss)]
                corrs = [dlse_ref[:, h:h + 1] - dl for h, dl in zip(heads, deltas)]
                dsps = [(pp * (dp + c)).astype(BF16) for pp, dp, c in zip(pps, dpps, corrs)]
                dscs = [(pc * (dp + c)).astype(BF16) for pc, dp, c in zip(pcs, dpcs, corrs)]
                for h, hs, dsp, dsc in zip(heads, hss, dsps, dscs):
                    dq = (dot(dsp, kv[h // group][0]) + dot(dsc, kv[h // group][1])) * scale
                    dq_ref[:, hs] = dq.astype(BF16)
                parts = [(dot(dsc, qh, _TN) * scale, dot(dsp, qh, _TN) * scale,
                          dot(pc.astype(BF16), dob, _TN), dot(pp.astype(BF16), dob, _TN))
                         for dsc, dsp, qh, pc, pp, dob in zip(dscs, dsps, qs, pcs, pps, dobs)]
                for kh in kv:
                    ks = slice(kh * HEAD_DIM, (kh + 1) * HEAD_DIM)
                    mine = [p for h, p in zip(heads, parts) if h // group == kh]
                    dkc, dkp, dvc, dvp = (sum(p[j] for p in mine[1:]) + mine[0][j] for j in range(4))
                    dk_ref[:, ks] = (ck_ref[:, ks] + dkp).astype(BF16)
                    dv_ref[:, ks] = (cv_ref[:, ks] + dvp).astype(BF16)
                    ck_ref[:, ks] = dkc
                    cv_ref[:, ks] = dvc
                if use_sink:
                    for h, ls, dl in zip(heads, lses, deltas):
                        val = -jnp.sum(jnp.exp(sink_ref[h] - ls) * dl, axis=0, keepdims=True)
                        ds_tile = jnp.where(jnp.logical_and(row == 0, lanes8 == h), val, ds_tile)
            if use_sink:
                dsink_ref[...] += ds_tile

        @pl.when(step == nb)
        def _():
            dk_ref[...] = ck_ref[...].astype(BF16)
            dv_ref[...] = cv_ref[...].astype(BF16)

    cur, prev = _attn_row_maps(nb)
    q_spec, lse_spec = _dil_spec(qw, dil, cur), _dil_spec(LANES, dil, cur)
    lag_spec = _dil_spec(kw, dil, lambda i: jnp.maximum(i - 1, 0))
    in_specs = [_dil_spec(qw, dil, cur, seg, offs[0]),
                _dil_spec(kw, dil, prev, seg, offs[1]), _dil_spec(kw, dil, cur, seg, offs[1]),
                _dil_spec(kw, dil, prev, seg, offs[2]), _dil_spec(kw, dil, cur, seg, offs[2]),
                q_spec, lse_spec, q_spec, lse_spec]
    args = [qkv2] * 5 + [o2, lse2, do2, dlse2]
    if use_sink:
        in_specs = [pl.BlockSpec(memory_space=pltpu.SMEM)] + in_specs
        args = [sink] + args
    kv_shape = _dil_shape(l, dil, kw, BF16)
    return pl.pallas_call(
        body, name=name, grid=(dil, nb + 1),
        in_specs=in_specs,
        out_specs=[q_spec, lag_spec, lag_spec, pl.BlockSpec((8, LANES), lambda r, i: (0, 0))],
        out_shape=[_dil_shape(l, dil, qw, BF16), kv_shape, kv_shape,
                   jax.ShapeDtypeStruct((8, LANES), F32)],
        scratch_shapes=[pltpu.VMEM((BLK, kw), F32), pltpu.VMEM((BLK, kw), F32)],
        compiler_params=_cparams(("arbitrary", "arbitrary")),
    )(*args)


def _attn_config(tag, dil, group, max_dist, seg, offs, qw, kw):
    return dict(name=tag, dil=dil, group=group, max_dist=max_dist, seg=seg, offs=offs, qw=qw, kw=kw)


A_W = 8 * HEAD_DIM
ATTN_A_CFGS = tuple(_attn_config("attn_a%d" % dil, dil, 1, window // dil, 3 * A_W, (0, A_W, 2 * A_W), A_W, A_W)
                    for window, dil in A_CONFIGS)
B_KVW = 2 * HEAD_DIM
ATTN_B_CFG = _attn_config("attn_b", 1, B_GROUP, BLK - 1, A_W + 2 * B_KVW, (0, A_W, A_W + B_KVW), A_W, B_KVW)


def _attn_fwd(cfg, qkv, sink):
    t = qkv.shape[0]
    kw = {k: v for k, v in cfg.items() if k != "name"}
    qkv2 = _dil_view(qkv, cfg["dil"])
    o2, lse2 = _attn_fwd_call(qkv2, sink, name=cfg["name"] + "_fwd", **kw)
    return o2.reshape(t, cfg["qw"]), lse2.reshape(t, LANES), (qkv2, o2, lse2)


def _attn_bwd(cfg, saved, sink, do, dlse):
    qkv2, o2, lse2 = saved
    t = do.shape[0]
    kw = {k: v for k, v in cfg.items() if k != "name"}
    dq2, dk2, dv2, dsink = _attn_bwd_call(qkv2, sink, o2, lse2, _dil_view(do, cfg["dil"]),
                                          _dil_view(dlse, cfg["dil"]), name=cfg["name"] + "_bwd", **kw)
    return dq2.reshape(t, cfg["qw"]), dk2.reshape(t, cfg["kw"]), dv2.reshape(t, cfg["kw"]), dsink


def _head_expand():
    r = lax.broadcasted_iota(jnp.int32, (LANES, 8 * HEAD_DIM), 0)
    c = lax.broadcasted_iota(jnp.int32, (LANES, 8 * HEAD_DIM), 1)
    return (c // HEAD_DIM == r).astype(F32)


def _combine_weights(l0, l1, l2):
    m = jnp.maximum(jnp.maximum(l0, l1), l2)
    e0, e1, e2 = jnp.exp(l0 - m), jnp.exp(l1 - m), jnp.exp(l2 - m)
    inv = 1.0 / (e0 + e1 + e2)
    return e0 * inv, e1 * inv, e2 * inv


def _combine_fwd_call(os_, lses, *, tq=256):
    t, w = os_[0].shape

    def body(o0, o1, o2, l0, l1, l2, y_ref):
        ws = _combine_weights(l0[...], l1[...], l2[...])
        e = _head_expand()
        y = jnp.zeros((tq, w), F32)
        for o_ref, wt in zip((o0, o1, o2), ws):
            y = y + _dot_mask(e, wt, mask_left=False) * o_ref[...]
        y_ref[...] = y

    o_spec = pl.BlockSpec((tq, w), lambda i: (i, 0))
    l_spec = pl.BlockSpec((tq, LANES), lambda i: (i, 0))
    return pl.pallas_call(
        body, name="combine_fwd", grid=(t // tq,),
        in_specs=[o_spec] * 3 + [l_spec] * 3, out_specs=o_spec,
        out_shape=jax.ShapeDtypeStruct((t, w), F32),
        compiler_params=_cparams(("parallel",)),
    )(*os_, *lses)


def _combine_bwd_call(os_, lses, dy, *, tq=256):
    t, w = dy.shape

    def body(o0, o1, o2, l0, l1, l2, dy_ref, do0, do1, do2, dl0, dl1, dl2):
        ws = _combine_weights(l0[...], l1[...], l2[...])
        e = _head_expand()
        dyv = dy_ref[...]
        dws = []
        for o_ref, do_ref, wt in zip((o0, o1, o2), (do0, do1, do2), ws):
            do_ref[...] = _dot_mask(e, wt, mask_left=False) * dyv
            dws.append(_dot_mask(e, dyv * o_ref[...], _NT, mask_left=False))
        mean = ws[0] * dws[0] + ws[1] * dws[1] + ws[2] * dws[2]
        for dl_ref, wt, dw in zip((dl0, dl1, dl2), ws, dws):
            dl_ref[...] = wt * (dw - mean)

    o_spec = pl.BlockSpec((tq, w), lambda i: (i, 0))
    l_spec = pl.BlockSpec((tq, LANES), lambda i: (i, 0))
    o_shape = jax.ShapeDtypeStruct((t, w), F32)
    l_shape = jax.ShapeDtypeStruct((t, LANES), F32)
    return pl.pallas_call(
        body, name="combine_bwd", grid=(t // tq,),
        in_specs=[o_spec] * 3 + [l_spec] * 3 + [o_spec], out_specs=[o_spec] * 3 + [l_spec] * 3,
        out_shape=[o_shape] * 3 + [l_shape] * 3,
        compiler_params=_cparams(("parallel",)),
    )(*os_, *lses, dy)


C_QKW = C_QK_HEADS * C_DK
C_CONV_W = 2 * C_QKW + C_V_HEADS * C_DK
HALO = 8


def _silu_parts(z):
    sig = jax.nn.sigmoid(z)
    return z * sig, sig * (1.0 + z * (1.0 - sig))


def _conv_window_specs(tq, t):
    c = C_CONV_W
    cb = COL["c_qkv"] // c
    blk = pl.BlockSpec((tq, c), lambda i: (i, cb))
    before = pl.BlockSpec((HALO, c), lambda i: (jnp.maximum(i * (tq // HALO) - 1, 0), cb))
    return c, cb, blk, before


def _conv_prep_fwd_call(u, w, *, tq=512):
    t = u.shape[0]
    c, _, x_spec, halo_spec = _conv_window_specs(tq, t)
    nqk = 2 * C_QK_HEADS

    def body(x_ref, halo_ref, w_ref, z_ref, qk_ref, v_ref):
        i = pl.program_id(0)
        halo = jnp.where(i == 0, 0.0, halo_ref[...])
        xc = jnp.concatenate([halo, x_ref[...]], axis=0)
        wv = w_ref[...]
        z = xc[HALO - 3:HALO - 3 + tq] * wv[0:1]
        for j in range(1, C_CONV):
            z = z + xc[HALO - 3 + j:HALO - 3 + j + tq] * wv[j:j + 1]
        z_ref[...] = z
        act, _ = _silu_parts(z)
        for h in range(nqk):
            a = act[:, h * C_DK:(h + 1) * C_DK]
            qk_ref[:, h * C_DK:(h + 1) * C_DK] = a * lax.rsqrt(jnp.sum(a * a, axis=1, keepdims=True) + EPS)
        v_ref[...] = act[:, nqk * C_DK:]

    return pl.pallas_call(
        body, name="conv_prep_fwd", grid=(t // tq,),
        in_specs=[x_spec, halo_spec, pl.BlockSpec((C_CONV, c), lambda i: (0, 0))],
        out_specs=[pl.BlockSpec((tq, c), lambda i: (i, 0)),
                   pl.BlockSpec((tq, 2 * C_QKW), lambda i: (i, 0)),
                   pl.BlockSpec((tq, c - 2 * C_QKW), lambda i: (i, 0))],
        out_shape=[jax.ShapeDtypeStruct((t, c), F32), jax.ShapeDtypeStruct((t, 2 * C_QKW), F32),
                   jax.ShapeDtypeStruct((t, c - 2 * C_QKW), F32)],
        compiler_params=_cparams(("parallel",)),
    )(u, u, w)


def _conv_prep_dz_call(z, dqk, dv, *, tq=512):
    t, c = z.shape
    nqk = 2 * C_QK_HEADS

    def body(z_ref, dqk_ref, dv_ref, dz_ref):
        zv = z_ref[...]
        act, dact = _silu_parts(zv)
        for h in range(nqk):
            hs = slice(h * C_DK, (h + 1) * C_DK)
            a = act[:, hs]
            r = lax.rsqrt(jnp.sum(a * a, axis=1, keepdims=True) + EPS)
            nrm = a * r
            dn = dqk_ref[:, hs]
            da = r * (dn - nrm * jnp.sum(dn * nrm, axis=1, keepdims=True))
            dz_ref[:, hs] = da * dact[:, hs]
        dz_ref[:, nqk * C_DK:] = dv_ref[...] * dact[:, nqk * C_DK:]

    return pl.pallas_call(
        body, name="conv_prep_dz", grid=(t // tq,),
        in_specs=[pl.BlockSpec((tq, c), lambda i: (i, 0)),
                  pl.BlockSpec((tq, 2 * C_QKW), lambda i: (i, 0)),
                  pl.BlockSpec((tq, c - 2 * C_QKW), lambda i: (i, 0))],
        out_specs=pl.BlockSpec((tq, c), lambda i: (i, 0)),
        out_shape=jax.ShapeDtypeStruct((t, c), F32),
        compiler_params=_cparams(("parallel",)),
    )(z, dqk, dv)


def _conv_bwd_call(u, dz, w, du_buf, *, tq=512):
    t = u.shape[0]
    nt = t // tq
    c, cb, x_spec, halo_spec = _conv_window_specs(tq, t)
    extra, extra_specs, aliases = _du_operands(du_buf, 5)

    def body(x_ref, xh_ref, dz_ref, dzh_ref, w_ref, *refs):
        dx_ref, dw_ref = refs[len(extra):]
        i = pl.program_id(0)
        xc = jnp.concatenate([jnp.where(i == 0, 0.0, xh_ref[...]), x_ref[...]], axis=0)
        dzv = dz_ref[...]
        dzc = jnp.concatenate([dzv, jnp.where(i == nt - 1, 0.0, dzh_ref[...])], axis=0)
        wv = w_ref[...]
        dx = dzv * wv[3:4]
        for s in range(1, C_CONV):
            dx = dx + dzc[s:s + tq] * wv[3 - s:4 - s]
        dx_ref[...] = dx.astype(BF16)
        row = lax.broadcasted_iota(jnp.int32, (8, c), 0)
        dw = jnp.zeros((8, c), F32)
        for j in range(C_CONV):
            prod = dzv * xc[HALO - 3 + j:HALO - 3 + j + tq]
            col = jnp.sum(jnp.sum(prod.reshape(tq // 8, 8, c), axis=0), axis=0, keepdims=True)
            dw = jnp.where(row == j, col, dw)

        @pl.when(i == 0)
        def _():
            dw_ref[...] = dw

        @pl.when(i > 0)
        def _():
            dw_ref[...] += dw

    blk = pl.BlockSpec((tq, c), lambda i: (i, 0))
    after = pl.BlockSpec((HALO, c), lambda i: (jnp.minimum((i + 1) * (tq // HALO), t // HALO - 1), 0))
    return pl.pallas_call(
        body, name="conv_bwd", grid=(nt,),
        in_specs=[x_spec, halo_spec, blk, after, pl.BlockSpec((C_CONV, c), lambda i: (0, 0))] + extra_specs,
        out_specs=[pl.BlockSpec((tq, c), lambda i: (i, cb)), pl.BlockSpec((8, c), lambda i: (0, 0))],
        out_shape=[_du_shape(t), jax.ShapeDtypeStruct((8, c), F32)],
        input_output_aliases=aliases,
        compiler_params=_cparams(("arbitrary",)),
    )(u, u, dz, dz, w, *extra)


C_VW = C_V_HEADS * C_DK


def _softplus(x):
    return jnp.maximum(x, 0.0) + jnp.log(1.0 + jnp.exp(-jnp.abs(x)))


def _tri_masks():
    r = lax.broadcasted_iota(jnp.int32, (CHUNK, CHUNK), 0)
    c = lax.broadcasted_iota(jnp.int32, (CHUNK, CHUNK), 1)
    return r >= c, r > c


def _split_bf16(a):
    hi = a.astype(BF16)
    return hi, (a - hi.astype(F32)).astype(BF16)


def _dot_hi(a, b, dims=None):
    dims = _NN if dims is None else dims
    ah, al = _split_bf16(a)
    bh, bl = _split_bf16(b)

    def d(x, y):
        return lax.dot_general(x, y, dims, preferred_element_type=F32)

    return d(ah, bh) + (d(ah, bl) + d(al, bh))


def _dot_mask(mask, b, dims=None, mask_left=True):
    dims = _NN if dims is None else dims
    mb = mask.astype(BF16)
    b1 = b.astype(BF16)
    rest = b - b1.astype(F32)
    b2 = rest.astype(BF16)
    b3 = (rest - b2.astype(F32)).astype(BF16)
    out = None
    for p in (b1, b2, b3):
        term = (lax.dot_general(mb, p, dims, preferred_element_type=F32) if mask_left
                else lax.dot_general(p, mb, dims, preferred_element_type=F32))
        out = term if out is None else out + term
    return out


def _unit_lower_inverses(mats):
    r = lax.broadcasted_iota(jnp.int32, (CHUNK, CHUNK), 0)
    c = lax.broadcasted_iota(jnp.int32, (CHUNK, CHUNK), 1)
    eye = (r == c).astype(F32)
    xs = [eye - a for a in mats]
    ps = [_dot_hi(a, a) for a in mats]
    steps = int(math.log2(CHUNK)) - 1
    for s in range(steps):
        xs = [x + _dot_hi(x, p) for x, p in zip(xs, ps)]
        if s < steps - 1:
            ps = [_dot_hi(p, p) for p in ps]
    return xs


def _gate_tiles(cab, alog, dtb):
    pre = cab + dtb
    g = -jnp.exp(alog) * _softplus(pre)
    beta = jax.nn.sigmoid(pltpu.roll(cab, LANES - C_V_HEADS, 1))
    return g, beta, pre


def _chunk_common(kk, qk, gc, gct, beta, h, tri, strict):
    gcol, grow, bcol = gc[:, h:h + 1], gct[h:h + 1, :], beta[:, h:h + 1]
    decay = jnp.where(tri, jnp.exp(jnp.where(tri, gcol - grow, 0.0)), 0.0)
    kkd = jnp.where(strict, kk * decay, 0.0)
    attn = jnp.where(tri, qk * decay, 0.0)
    glast = gc[CHUNK - 1:CHUNK, h:h + 1]
    return gcol, bcol, decay, kkd, attn, glast


def _cab_spec():
    return pl.BlockSpec((CHUNK, LANES), lambda n: (n, COL["c_ab"] // LANES))


def _delta_prep_call(qk, v, u, alog, dtb, gather_src=None):
    t = qk.shape[0]
    nc = t // CHUNK
    scale = C_DK ** -0.5
    riding = gather_src is not None

    def body(q_ref, k_ref, v_ref, cab_ref, alog_ref, dtb_ref, *refs):
        if riding:
            gather_refs = (refs[0], refs[9]) + tuple(refs[10:])
            refs = refs[1:9]

            @pl.when(pl.program_id(0) == 0)
            def _():
                _gather_start(*gather_refs)
        u_ref, w_ref, qd_ref, kd_ref, attn_ref, tmat_ref, gc_ref, beta_ref = refs
        tri, strict = _tri_masks()
        g, beta, _ = _gate_tiles(cab_ref[...], alog_ref[...], dtb_ref[...])
        gc = _dot_mask(tri, g)
        gct = gc.T
        gc_ref[...] = gc
        beta_ref[...] = beta
        mats, rhs = [], []
        for j in range(C_QK_HEADS):
            js = slice(j * C_DK, (j + 1) * C_DK)
            kf, qf = k_ref[:, js], q_ref[:, js] * scale
            kb, qb = kf.astype(BF16), qf.astype(BF16)
            kk = lax.dot_general(kb, kb, _NT, preferred_element_type=F32)
            qk = lax.dot_general(qb, kb, _NT, preferred_element_type=F32)
            for h in (2 * j, 2 * j + 1):
                hs = slice(h * C_DK, (h + 1) * C_DK)
                gcol, bcol, decay, kkd, attn, glast = _chunk_common(kk, qk, gc, gct, beta, h, tri, strict)
                gexp = jnp.exp(gcol)
                mats.append(kkd * bcol)
                rhs.append(jnp.concatenate([v_ref[:, hs] * bcol, kf * (bcol * gexp)], axis=1))
                qd_ref[:, hs] = (qf * gexp).astype(BF16)
                kd_ref[:, hs] = (kf * jnp.exp(glast - gcol)).astype(BF16)
                attn_ref[:, h * CHUNK:(h + 1) * CHUNK] = attn.astype(BF16)
        for h, (tmat, r) in enumerate(zip(_unit_lower_inverses(mats), rhs)):
            hs = slice(h * C_DK, (h + 1) * C_DK)
            uw = _dot_hi(tmat, r)
            u_ref[:, hs] = uw[:, :C_DK]
            w_ref[:, hs] = uw[:, C_DK:]
            tmat_ref[:, h * CHUNK:(h + 1) * CHUNK] = tmat
        if riding:
            @pl.when(pl.program_id(0) == nc - 1)
            def _():
                _gather_finish(*gather_refs)

    def blk(w):
        return pl.BlockSpec((CHUNK, w), lambda n: (n, 0))

    row = pl.BlockSpec((1, LANES), lambda n: (0, 0))
    big = jax.ShapeDtypeStruct((t, C_VW), F32)
    sq = jax.ShapeDtypeStruct((t, C_V_HEADS * CHUNK), F32)
    tile = jax.ShapeDtypeStruct((t, LANES), F32)
    half = jax.ShapeDtypeStruct((t, C_VW), BF16)
    in_specs = [blk(C_QKW), pl.BlockSpec((CHUNK, C_QKW), lambda n: (n, 1)), blk(C_VW), _cab_spec(), row, row]
    out_specs = [blk(C_VW)] * 4 + [blk(C_V_HEADS * CHUNK)] * 2 + [blk(LANES)] * 2
    out_shape = [big, big, half, half, jax.ShapeDtypeStruct(sq.shape, BF16), sq] + [tile] * 2
    args = [qk, qk, v, u, alog, dtb]
    if riding:
        in_specs.append(HBM_SPEC)
        out_specs.append(HBM_SPEC)
        out_shape.append(jax.ShapeDtypeStruct((N_DEV,) + gather_src.shape, gather_src.dtype))
        args.append(gather_src)
    return pl.pallas_call(
        body, name="delta_prep_gather" if riding else "delta_prep", grid=(nc,),
        in_specs=in_specs, out_specs=out_specs, out_shape=out_shape,
        scratch_shapes=list(GATHER_SEMS) if riding else [],
        compiler_params=_cparams(("arbitrary",) if riding else ("parallel",)),
    )(*args)


def _delta_scan_call(u, w, qd, kd, attn, gc):
    t = u.shape[0]
    nc = t // CHUNK

    def body(u_ref, w_ref, qd_ref, kd_ref, attn_ref, gc_ref, o_ref, vn_ref, st_ref, s_ref):
        @pl.when(pl.program_id(0) == 0)
        def _():
            s_ref[...] = jnp.zeros_like(s_ref)

        hss = [slice(h * C_DK, (h + 1) * C_DK) for h in range(C_V_HEADS)]
        states = [s_ref[hs, :] for hs in hss]
        for hs, s in zip(hss, states):
            st_ref[0, hs, :] = s
        sbs = [s.astype(BF16) for s in states]
        vns = [u_ref[:, hs] - jnp.dot(w_ref[:, hs].astype(BF16), sb, preferred_element_type=F32)
               for hs, sb in zip(hss, sbs)]
        qss = [jnp.dot(qd_ref[:, hs].astype(BF16), sb, preferred_element_type=F32) for hs, sb in zip(hss, sbs)]
        vnbs = [vn.astype(BF16) for vn in vns]
        for h, hs in enumerate(hss):
            vn_ref[:, hs] = vnbs[h]
            o_ref[:, hs] = qss[h] + jnp.dot(attn_ref[:, h * CHUNK:(h + 1) * CHUNK].astype(BF16), vnbs[h],
                                            preferred_element_type=F32)
        for h, hs in enumerate(hss):
            glast = jnp.exp(gc_ref[CHUNK - 1:CHUNK, h:h + 1])
            s_ref[hs, :] = states[h] * glast + lax.dot_general(kd_ref[:, hs].astype(BF16), vnbs[h], _TN,
                                                               preferred_element_type=F32)

    def blk(wd):
        return pl.BlockSpec((CHUNK, wd), lambda n: (n, 0))

    big = jax.ShapeDtypeStruct((t, C_VW), F32)
    return pl.pallas_call(
        body, name="delta_scan", grid=(nc,),
        in_specs=[blk(C_VW)] * 4 + [blk(C_V_HEADS * CHUNK), blk(LANES)],
        out_specs=[blk(C_VW), blk(C_VW), pl.BlockSpec((1, C_VW, C_DK), lambda n: (n, 0, 0))],
        out_shape=[big, jax.ShapeDtypeStruct((t, C_VW), BF16), jax.ShapeDtypeStruct((nc, C_VW, C_DK), F32)],
        scratch_shapes=[pltpu.VMEM((C_VW, C_DK), F32)],
        compiler_params=_cparams(("arbitrary",)),
    )(u, w, qd, kd, attn, gc)


def _delta_scan_bwd_call(do, w, qd, kd, attn, gc, vn, st):
    t = do.shape[0]
    nc = t // CHUNK

    def body(do_ref, w_ref, qd_ref, kd_ref, attn_ref, gc_ref, vn_ref, st_ref,
             du_ref, dw_ref, dqd_ref, dkd_ref, dattn_ref, dgl_ref, ds_ref):
        @pl.when(pl.program_id(0) == 0)
        def _():
            ds_ref[...] = jnp.zeros_like(ds_ref)

        tri, _ = _tri_masks()
        row = lax.broadcasted_iota(jnp.int32, (8, LANES), 0)
        lane = lax.broadcasted_iota(jnp.int32, (8, LANES), 1)
        dgl = jnp.zeros((8, LANES), F32)
        hss = [slice(h * C_DK, (h + 1) * C_DK) for h in range(C_V_HEADS)]
        css = [slice(h * CHUNK, (h + 1) * CHUNK) for h in range(C_V_HEADS)]

        def dg(a, b, dims):
            return lax.dot_general(a, b, dims, preferred_element_type=F32)

        ss = [st_ref[0, hs, :] for hs in hss]
        dsps = [ds_ref[hs, :] for hs in hss]
        sbs = [s.astype(BF16) for s in ss]
        dspbs = [d.astype(BF16) for d in dsps]
        dobs = [do_ref[:, hs].astype(BF16) for hs in hss]
        vnbs = [vn_ref[:, hs].astype(BF16) for hs in hss]
        dvns = [dg(attn_ref[:, cs].astype(BF16), dob, _TN) + dg(kd_ref[:, hs].astype(BF16), dspb, _NN)
                for hs, cs, dob, dspb in zip(hss, css, dobs, dspbs)]
        for h, hs in enumerate(hss):
            dqd_ref[:, hs] = dg(dobs[h], sbs[h], _NT)
            dkd_ref[:, hs] = dg(vnbs[h], dspbs[h], _NT)
            dattn_ref[:, css[h]] = jnp.where(tri, dg(dobs[h], vnbs[h], _NT), 0.0)
        dvnbs = [d.astype(BF16) for d in dvns]
        for h, hs in enumerate(hss):
            du_ref[:, hs] = dvns[h]
            dw_ref[:, hs] = -dg(dvnbs[h], sbs[h], _NT)
            tot = jnp.sum(jnp.sum(dsps[h] * ss[h], axis=0, keepdims=True), axis=1, keepdims=True)
            dgl = jnp.where(jnp.logical_and(row == 0, lane == h), tot, dgl)
        for h, hs in enumerate(hss):
            glast = jnp.exp(gc_ref[CHUNK - 1:CHUNK, h:h + 1])
            ds_ref[hs, :] = (dg(qd_ref[:, hs].astype(BF16), dobs[h], _TN) + glast * dsps[h]
                             - dg(w_ref[:, hs].astype(BF16), dvnbs[h], _TN))
        dgl_ref[...] = dgl

    def blk(wd):
        return pl.BlockSpec((CHUNK, wd), lambda n: (nc - 1 - n, 0))

    big = jax.ShapeDtypeStruct((t, C_VW), F32)
    return pl.pallas_call(
        body, name="delta_scan_bwd", grid=(nc,),
        in_specs=[blk(C_VW)] * 4 + [blk(C_V_HEADS * CHUNK), blk(LANES), blk(C_VW),
                                    pl.BlockSpec((1, C_VW, C_DK), lambda n: (nc - 1 - n, 0, 0))],
        out_specs=[blk(C_VW)] * 4 + [blk(C_V_HEADS * CHUNK), pl.BlockSpec((8, LANES), lambda n: (nc - 1 - n, 0))],
        out_shape=[big] * 4 + [jax.ShapeDtypeStruct((t, C_V_HEADS * CHUNK), F32),
                               jax.ShapeDtypeStruct((nc * 8, LANES), F32)],
        scratch_shapes=[pltpu.VMEM((C_VW, C_DK), F32)],
        compiler_params=_cparams(("arbitrary",)),
    )(do, w, qd, kd, attn, gc, vn, st)


def _delta_prep_bwd_call(qk, v, proj, alog, dtb, tmat, u, w, gc, beta, du, dw, dqd, dkd, dattn, dgl, du_buf):
    t = qk.shape[0]
    extra, extra_specs, aliases = _du_operands(du_buf, 17)
    nc = t // CHUNK
    scale = C_DK ** -0.5

    def body(q_ref, k_ref, v_ref, cab_ref, alog_ref, dtb_ref, tmat_ref, u_ref, w_ref, gc_ref, beta_ref,
             du_ref, dw_ref, dqd_ref, dkd_ref, dattn_ref, dgl_ref, *outs):
        dcab_ref, dqk_ref, dv_ref, dpar_ref = outs[len(extra):]
        tri, strict = _tri_masks()
        gc, beta = gc_ref[...], beta_ref[...]
        gct = gc.T
        ones = jnp.ones((CHUNK, LANES), F32)
        lane = lax.broadcasted_iota(jnp.int32, (CHUNK, LANES), 1)
        rowi = lax.broadcasted_iota(jnp.int32, (CHUNK, 1), 0)
        dgc_tile = jnp.zeros((CHUNK, LANES), F32)
        db_tile = jnp.zeros((CHUNK, LANES), F32)
        heads = []
        for j in range(C_QK_HEADS):
            js = slice(j * C_DK, (j + 1) * C_DK)
            kf, qf = k_ref[:, js], q_ref[:, js] * scale
            kb, qb = kf.astype(BF16), qf.astype(BF16)
            kk = lax.dot_general(kb, kb, _NT, preferred_element_type=F32)
            qk = lax.dot_general(qb, kb, _NT, preferred_element_type=F32)
            for h in (2 * j, 2 * j + 1):
                heads.append((h, kf, qf, kb, qb) + _chunk_common(kk, qk, gc, gct, beta, h, tri, strict))
        dvks = [_dot_hi(tmat_ref[:, h * CHUNK:(h + 1) * CHUNK],
                        jnp.concatenate([du_ref[:, h * C_DK:(h + 1) * C_DK], dw_ref[:, h * C_DK:(h + 1) * C_DK]], axis=1),
                        _TN) for h in range(C_V_HEADS)]
        das = [-jnp.where(strict, _dot_hi(dvk, jnp.concatenate([u_ref[:, h * C_DK:(h + 1) * C_DK],
                                                                w_ref[:, h * C_DK:(h + 1) * C_DK]], axis=1), _NT), 0.0)
               for h, dvk in enumerate(dvks)]

        def dot(x, y, dims=_NN):
            return lax.dot_general(x, y, dims, preferred_element_type=F32)

        pre = []
        for (h, kf, qf, kb, qb, gcol, bcol, decay, kkd, attn, glast), da in zip(heads, das):
            dattn_h = dattn_ref[:, h * CHUNK:(h + 1) * CHUNK]
            pre.append(((da * decay * bcol).astype(BF16), (dattn_h * decay).astype(BF16),
                        da * kkd * bcol + dattn_h * attn))
        mms = [(dot(dkk, hd[3]), dot(dkk, hd[3], _TN), dot(dqk, hd[4], _TN), dot(dqk, hd[3]),
                _dot_mask(ones, e, _TN, mask_left=False))
               for hd, (dkk, dqk, e) in zip(heads, pre)]
        dq_parts, dk_parts = [], []
        for (h, kf, qf, kb, qb, gcol, bcol, decay, kkd, attn, glast), dvk, da, (_, _, e), mm in zip(
                heads, dvks, das, pre, mms):
            hs = slice(h * C_DK, (h + 1) * C_DK)
            gexp = jnp.exp(gcol)
            fdec = jnp.exp(glast - gcol)
            dvb, dkb = dvk[:, :C_DK], dvk[:, C_DK:]
            dgc = jnp.sum(e, axis=1, keepdims=True) - mm[4][:, :1]
            dk_parts.append(mm[0] + mm[1] + mm[2] + dkb * (bcol * gexp) + dkd_ref[:, hs] * fdec)
            dq_parts.append(mm[3] + dqd_ref[:, hs] * gexp)
            dv_ref[:, hs] = dvb * bcol
            s_kb = jnp.sum(dkb * kf, axis=1, keepdims=True)
            db = (jnp.sum(da * kkd, axis=1, keepdims=True) + jnp.sum(dvb * v_ref[:, hs], axis=1, keepdims=True)
                  + s_kb * gexp)
            rho = jnp.sum(dkd_ref[:, hs] * kf, axis=1, keepdims=True) * fdec
            dgc = (dgc + s_kb * bcol * gexp + jnp.sum(dqd_ref[:, hs] * qf, axis=1, keepdims=True) * gexp - rho)
            last = jnp.sum(rho, axis=0, keepdims=True) + dgl_ref[0:1, h:h + 1] * jnp.exp(glast)
            dgc = dgc + jnp.where(rowi == CHUNK - 1, last, 0.0)
            dgc_tile = jnp.where(lane == h, dgc, dgc_tile)
            db_tile = jnp.where(lane == h, db, db_tile)
        for j in range(C_QK_HEADS):
            dqk_ref[:, j * C_DK:(j + 1) * C_DK] = (dq_parts[2 * j] + dq_parts[2 * j + 1]) * scale
            dqk_ref[:, C_QKW + j * C_DK:C_QKW + (j + 1) * C_DK] = dk_parts[2 * j] + dk_parts[2 * j + 1]
        dg = _dot_mask(jnp.logical_not(strict), dgc_tile)
        alog = alog_ref[...]
        g, _, gate_pre = _gate_tiles(cab_ref[...], alog, dtb_ref[...])
        dca = dg * (-jnp.exp(alog)) * jax.nn.sigmoid(gate_pre)
        dcab_ref[:, :LANES] = (dca + pltpu.roll(db_tile * beta * (1.0 - beta), C_V_HEADS, 1)).astype(BF16)
        dcab_ref[:, LANES:] = jnp.zeros((CHUNK, D_IN_PAD - COL["c_ab"] - LANES), BF16)
        row8 = lax.broadcasted_iota(jnp.int32, (8, LANES), 0)
        par = jnp.where(row8 == 0, jnp.sum(dg * g, axis=0, keepdims=True),
                        jnp.where(row8 == 1, jnp.sum(dca, axis=0, keepdims=True), 0.0))

        @pl.when(pl.program_id(0) == 0)
        def _():
            dpar_ref[...] = par

        @pl.when(pl.program_id(0) > 0)
        def _():
            dpar_ref[...] += par

    def blk(wd):
        return pl.BlockSpec((CHUNK, wd), lambda n: (n, 0))

    row = pl.BlockSpec((1, LANES), lambda n: (0, 0))
    sq = blk(C_V_HEADS * CHUNK)
    tail = D_IN_PAD - COL["c_ab"]
    assert COL["c_ab"] % tail == 0
    return pl.pallas_call(
        body, name="delta_prep_bwd", grid=(nc,),
        in_specs=[blk(C_QKW), pl.BlockSpec((CHUNK, C_QKW), lambda n: (n, 1)), blk(C_VW), _cab_spec(), row, row, sq,
                  blk(C_VW), blk(C_VW),
                  blk(LANES), blk(LANES), blk(C_VW), blk(C_VW), blk(C_VW), blk(C_VW), sq,
                  pl.BlockSpec((8, LANES), lambda n: (n, 0))] + extra_specs,
        out_specs=[pl.BlockSpec((CHUNK, tail), lambda n: (n, COL["c_ab"] // tail)),
                   blk(2 * C_QKW), blk(C_VW), pl.BlockSpec((8, LANES), lambda n: (0, 0))],
        out_shape=[_du_shape(t), jax.ShapeDtypeStruct((t, 2 * C_QKW), F32),
                   jax.ShapeDtypeStruct((t, C_VW), F32), jax.ShapeDtypeStruct((8, LANES), F32)],
        input_output_aliases=aliases,
        compiler_params=_cparams(("arbitrary",)),
    )(qk, qk, v, proj, alog, dtb, tmat, u, w, gc, beta, du, dw, dqd, dkd, dattn, dgl, *extra)


def _z_spec(tq):
    return pl.BlockSpec((tq, C_VW), lambda i: (i, COL["c_z"] // C_VW))


def _gated_norm_fwd_call(o, u, gain, *, tq=256):
    t, w = o.shape

    def body(o_ref, z_ref, g_ref, y_ref):
        act, _ = _silu_parts(z_ref[...])
        gv = g_ref[...]
        for h in range(C_V_HEADS):
            hs = slice(h * C_DK, (h + 1) * C_DK)
            ov = o_ref[:, hs]
            r = lax.rsqrt(jnp.mean(ov * ov, axis=1, keepdims=True) + EPS)
            y_ref[:, hs] = ov * r * gv * act[:, hs]

    blk = pl.BlockSpec((tq, w), lambda i: (i, 0))
    return pl.pallas_call(
        body, name="gated_norm_fwd", grid=(t // tq,),
        in_specs=[blk, _z_spec(tq), pl.BlockSpec((1, C_DK), lambda i: (0, 0))], out_specs=blk,
        out_shape=jax.ShapeDtypeStruct((t, w), F32),
        compiler_params=_cparams(("parallel",)),
    )(o, u, gain)


def _gated_norm_bwd_call(o, u, gain, dy, du_buf, *, tq=256):
    t, w = o.shape
    nt = t // tq
    extra, extra_specs, aliases = _du_operands(du_buf, 4)

    def body(o_ref, z_ref, g_ref, dy_ref, *refs):
        dz_ref, do_ref, dg_ref, acc_ref = refs[len(extra):]
        i = pl.program_id(0)
        act, dact = _silu_parts(z_ref[...])
        gv = g_ref[...]
        part = jnp.zeros((8, C_DK), F32)
        for h in range(C_V_HEADS):
            hs = slice(h * C_DK, (h + 1) * C_DK)
            ov = o_ref[:, hs]
            r = lax.rsqrt(jnp.mean(ov * ov, axis=1, keepdims=True) + EPS)
            xh = ov * r
            dyv = dy_ref[:, hs]
            dn = dyv * act[:, hs]
            dz_ref[:, hs] = (dyv * xh * gv * dact[:, hs]).astype(BF16)
            dxh = dn * gv
            do_ref[:, hs] = r * (dxh - xh * jnp.mean(dxh * xh, axis=1, keepdims=True))
            part = part + jnp.sum((dn * xh).reshape(tq // 8, 8, C_DK), axis=0)

        @pl.when(i == 0)
        def _():
            acc_ref[...] = part

        @pl.when(i > 0)
        def _():
            acc_ref[...] += part

        @pl.when(i == nt - 1)
        def _():
            dg_ref[...] = jnp.sum(acc_ref[...], axis=0, keepdims=True)

    blk = pl.BlockSpec((tq, w), lambda i: (i, 0))
    grow = pl.BlockSpec((1, C_DK), lambda i: (0, 0))
    return pl.pallas_call(
        body, name="gated_norm_bwd", grid=(nt,),
        in_specs=[blk, _z_spec(tq), grow, blk] + extra_specs, out_specs=[_z_spec(tq), blk, grow],
        out_shape=[_du_shape(t), jax.ShapeDtypeStruct((t, w), F32), jax.ShapeDtypeStruct((1, C_DK), F32)],
        scratch_shapes=[pltpu.VMEM((8, C_DK), F32)],
        input_output_aliases=aliases,
        compiler_params=_cparams(("arbitrary",)),
    )(o, u, gain, dy, *extra)


def _gate_specs(tq):
    return [pl.BlockSpec((tq, D_MODEL), lambda i, j=j: (i, j)) for j in range(3)]


def _merge_fwd_call(ps, u, *, tq=256):
    t, w = ps[0].shape

    def body(p0, p1, p2, g0, g1, g2, y_ref):
        y_ref[...] = (jax.nn.sigmoid(g0[...]) * p0[...] + jax.nn.sigmoid(g1[...]) * p1[...]
                      + jax.nn.sigmoid(g2[...]) * p2[...]).astype(BF16)

    blk = pl.BlockSpec((tq, w), lambda i: (i, 0))
    return pl.pallas_call(
        body, name="merge_fwd", grid=(t // tq,), in_specs=[blk] * 3 + _gate_specs(tq), out_specs=blk,
        out_shape=jax.ShapeDtypeStruct((t, w), BF16),
        compiler_params=_cparams(("parallel",)),
    )(*ps, u, u, u)


def _merge_bwd_call(ps, u, dy, *, tq=256):
    t, w = dy.shape

    def body(p0, p1, p2, g0, g1, g2, dy_ref, dg_ref, dp0, dp1, dp2):
        dyv = dy_ref[...]
        for j, (p, g, dp) in enumerate(((p0, g0, dp0), (p1, g1, dp1), (p2, g2, dp2))):
            sig = jax.nn.sigmoid(g[...])
            dp[...] = (dyv * sig).astype(BF16)
            dg_ref[:, j * w:(j + 1) * w] = (dyv * p[...] * sig * (1.0 - sig)).astype(BF16)

    blk = pl.BlockSpec((tq, w), lambda i: (i, 0))
    small = jax.ShapeDtypeStruct((t, w), BF16)
    return pl.pallas_call(
        body, name="merge_bwd", grid=(t // tq,), in_specs=[blk] * 3 + _gate_specs(tq) + [blk],
        out_specs=[pl.BlockSpec((tq, 3 * w), lambda i: (i, 0))] + [blk] * 3,
        out_shape=[_du_shape(t)] + [small] * 3,
        compiler_params=_cparams(("parallel",)),
    )(*ps, u, u, u, dy)


A_PARTS = ((COL["a_q"], 2 * A_W, True), (COL["a_v"], A_W, False))
B_PARTS = ((COL["b_q"], A_W, True), (COL["b_k"], B_KVW, True), (COL["b_v"], B_KVW, False))
BRANCHES = ("w_branch_a", "w_branch_b", "w_branch_c")


def _layer_fwd(x, tabs, p, wb, gather_src=None):
    h = _rms_fwd_call(x, p["norm_mix"], name="rms_mix_fwd", out_dtype=BF16)
    u = _mm(h, wb["w_in"], bias=p["b_in"], tn=IN_TN, name="in_proj_fwd")
    qkv_a = _rope_gather_call(u, tabs, A_PARTS, name="rope_a_fwd")
    a_runs = [_attn_fwd(cfg, qkv_a, None) for cfg in ATTN_A_CFGS]
    os_, lses = tuple(r[0] for r in a_runs), tuple(r[1] for r in a_runs)
    ya = _combine_fwd_call(os_, lses)
    qkv_b = _rope_gather_call(u, tabs, B_PARTS, name="rope_b_fwd")
    yb, _, b_saved = _attn_fwd(ATTN_B_CFG, qkv_b, p["sinks"])
    zc, qk, v = _conv_prep_fwd_call(u, p["conv_w"])
    uu, ww, qd, kd, attn, tmat, gc, beta, *gathered = _delta_prep_call(qk, v, u, p["a_log"], p["dt_bias"], gather_src)
    o, vn, st = _delta_scan_call(uu, ww, qd, kd, attn, gc)
    yc = _gated_norm_fwd_call(o, u, p["c_norm"])
    ys = (ya, yb, yc)
    ps = tuple(_mm(y, wb[n], name="branch_fwd") for y, n in zip(ys, BRANCHES))
    merged = _merge_fwd_call(ps, u)
    x1 = _mm(merged, wb["w_out"], add=x, name="out_proj_fwd")
    h2 = _rms_fwd_call(x1, p["norm_ffn"], name="rms_ffn_fwd", out_dtype=BF16)
    pre, act = _mm(h2, wb["w_ff1"], relu2_out=True, name="ffn_up")
    x2 = _mm(act, wb["w_ff2"], add=x1, name="ffn_down")
    saved = dict(x=x, h=h, u=u, a_saved=[r[2] for r in a_runs], os_=os_, lses=lses, b_saved=b_saved,
                 zc=zc, qk=qk, v=v, delta=(tmat, uu, ww, gc, beta, qd, kd, attn, vn, st), o=o, ys=ys, ps=ps,
                 merged=merged, x1=x1, h2=h2, pre=pre, act=act)
    return x2, saved, (gathered[0] if gathered else None)


def _layer_bwd(s, dx2, tabs, p, wb):
    g = {}
    t = dx2.shape[0]
    dpre = _mm(dx2, wb["w_ff2"], tb=True, mul_drelu2=s["pre"], out_dtype=BF16, name="ffn_dpre")
    g["w_ff2"] = _mm(s["act"], dx2, ta=True, tk=1024, name="ffn_dw2")
    g["w_ff1"] = _mm(s["h2"], dpre, ta=True, tk=1024, name="ffn_dw1")
    dh2 = _mm(dpre, wb["w_ff1"], tb=True, name="ffn_dh")
    dx1, g["norm_ffn"] = _rms_bwd_call(s["x1"], p["norm_ffn"], dh2, add=dx2, name="rms_ffn_bwd")
    dmerged = _mm(dx1, wb["w_out"], tb=True, name="out_proj_da")
    g["w_out"] = _mm(s["merged"], dx1, ta=True, tk=1024, name="out_proj_dw")
    du, *dps = _merge_bwd_call(s["ps"], s["u"], dmerged)
    dys = []
    for y, dp, n in zip(s["ys"], dps, BRANCHES):
        dys.append(_mm(dp, wb[n], tb=True, name="branch_da"))
        g[n] = _mm(y, dp, ta=True, tk=1024, name="branch_dw")
    dya, dyb, dyc = dys
    tmat, uu, ww, gc, beta, qd, kd, attn, vn, st = s["delta"]
    du, do, g["c_norm"] = _gated_norm_bwd_call(s["o"], s["u"], p["c_norm"], dyc, du)
    ddu, ddw, dqd, dkd, dattn, dgl = _delta_scan_bwd_call(do, ww, qd, kd, attn, gc, vn, st)
    du, dqk, dv, dpar = _delta_prep_bwd_call(s["qk"], s["v"], s["u"], p["a_log"], p["dt_bias"], tmat, uu, ww, gc,
                                             beta, ddu, ddw, dqd, dkd, dattn, dgl, du)
    g["a_log"], g["dt_bias"] = dpar[0:1], dpar[1:2]
    dzc = _conv_prep_dz_call(s["zc"], dqk, dv)
    du, dconv = _conv_bwd_call(s["u"], dzc, p["conv_w"], du)
    g["conv_w"] = dconv[:C_CONV]
    no_dlse = jnp.zeros((t, LANES), F32)
    dq, dk, dv_b, dsink = _attn_bwd(ATTN_B_CFG, s["b_saved"], p["sinks"], dyb, no_dlse)
    g["sinks"] = dsink[0, :p["sinks"].shape[0]]
    du = _rope_scatter_call(du, t, [([dq], A_W, True)], COL["b_q"], tabs, name="rope_bq_bwd")
    du = _rope_scatter_call(du, t, [([dk], B_KVW, True), ([dv_b], B_KVW, False)], COL["b_k"], tabs,
                            name="rope_bkv_bwd")
    *dos, dl0, dl1, dl2 = _combine_bwd_call(s["os_"], s["lses"], dya)
    grads_a = [_attn_bwd(cfg, sv, None, do_c, dl)[:3]
               for cfg, sv, do_c, dl in zip(ATTN_A_CFGS, s["a_saved"], dos, (dl0, dl1, dl2))]
    dqs, dks, dvs = zip(*grads_a)
    du = _rope_scatter_call(du, t, [(list(dqs), A_W, True), (list(dks), A_W, True), (list(dvs), A_W, False)],
                            COL["a_q"], tabs, name="rope_a_bwd")
    dh = _mm(du, wb["w_in"], tb=True, tk=IN_TN, name="in_proj_da")
    g["w_in"], g["b_in"] = _mm(s["h"], du, ta=True, b_colsum=True, tn=IN_TN, tk=1024, name="in_proj_dw")
    dx, g["norm_mix"] = _rms_bwd_call(s["x"], p["norm_mix"], dh, add=dx1, name="rms_mix_bwd")
    return dx, g


def _local_step(x, params, first_gathered, payload_of_layer, weights_of_layer, tabs, tgt):
    saves, gathered = [], first_gathered
    for layer in range(DEPTH):
        wb, conv_w = weights_of_layer(layer, gathered)
        p = {n: w[layer] for n, w in params.items() if n != "norm_final"}
        p["conv_w"] = conv_w
        nxt = payload_of_layer(layer + 1) if layer + 1 < DEPTH else None
        x, s, gathered = _layer_fwd(x, tabs, p, wb, nxt)
        saves.append((s, p, wb))
    loss = _loss_fwd_call(x, params["norm_final"], tgt)
    dx, dfinal = _loss_bwd_call(x, params["norm_final"], tgt, jnp.ones((1, 1), F32))
    per_layer = []
    for s, p, wb in reversed(saves):
        dx, g = _layer_bwd(s, dx, tabs, p, wb)
        per_layer.append(g)
    per_layer.reverse()
    grads = {n: jnp.stack([g[n] for g in per_layer]) for n in per_layer[0]}
    grads["norm_final"] = dfinal
    return loss, dx, grads


def _in_cols_to_kernel(w):
    lead = w.shape[:-1]
    parts, pos = [], 0
    for _, start, width, ref_start in IN_LAYOUT:
        if start > pos:
            parts.append(jnp.zeros(lead + (start - pos,), w.dtype))
        parts.append(w[..., ref_start:ref_start + width])
        pos = start + width
    parts.append(jnp.zeros(lead + (D_IN_PAD - pos,), w.dtype))
    return jnp.concatenate(parts, axis=-1)


def _in_cols_to_reference(w):
    by_ref = sorted(IN_LAYOUT, key=lambda e: e[3])
    return jnp.concatenate([w[..., start:start + width] for _, start, width, _ in by_ref], axis=-1)


W_IN_SHARD = 8464 // N_DEV


def _w_in_from_shards(blocks):
    lead = blocks.shape[1:-1]
    parts, pos = [], 0
    for _, start, width, ref_start in IN_LAYOUT:
        if start > pos:
            parts.append(jnp.zeros(lead + (start - pos,), blocks.dtype))
        col = ref_start
        while col < ref_start + width:
            d, l = divmod(col, W_IN_SHARD)
            n = min(W_IN_SHARD - l, ref_start + width - col)
            parts.append(blocks[d, ..., l:l + n])
            col += n
        pos = start + width
    parts.append(jnp.zeros(lead + (D_IN_PAD - pos,), blocks.dtype))
    return jnp.concatenate(parts, axis=-1)


def _w_in_to_shards(g):
    by_ref = sorted(IN_LAYOUT, key=lambda e: e[3])
    blocks = []
    for d in range(N_DEV):
        lo, hi = d * W_IN_SHARD, (d + 1) * W_IN_SHARD
        parts = []
        for _, start, width, ref_start in by_ref:
            a, b = max(lo, ref_start), min(hi, ref_start + width)
            if a < b:
                parts.append(g[..., start + a - ref_start:start + b - ref_start])
        blocks.append(jnp.concatenate(parts, axis=-1))
    return jnp.stack(blocks)


def _pad_lanes(v):
    return jnp.pad(v, ((0, 0), (0, LANES - v.shape[1])))[:, None, :]


BIG = (("w_in", 2), ("conv_w", 2), ("w_branch_a", 2), ("w_branch_b", 2), ("w_branch_c", 1), ("w_out", 1),
       ("w_ff1", 2), ("w_ff2", 1))
SMALL = ("norm_mix", "b_in", "a_log", "dt_bias", "sinks", "c_norm", "norm_ffn", "norm_final")
WEIGHTS = ("norm_mix", "w_in", "b_in", "conv_w", "a_log", "dt_bias", "sinks", "c_norm", "w_branch_a",
           "w_branch_b", "w_branch_c", "w_out", "norm_ffn", "w_ff1", "w_ff2", "norm_final")
MATMUL_WEIGHTS = ("w_in", "w_branch_a", "w_branch_b", "w_branch_c", "w_out", "w_ff1", "w_ff2")
PACK_ROWS = 1024
ROW_ALIGN = 16


def _seg_rows(n):
    return -(-n // (LANES * ROW_ALIGN)) * ROW_ALIGN


def _pack(arrays, lead=0):
    parts = []
    for a in arrays:
        lead_shape = a.shape[:lead]
        n = math.prod(a.shape[lead:])
        rows = _seg_rows(n)
        if rows * LANES != n:
            a = jnp.pad(a.reshape(lead_shape + (n,)), [(0, 0)] * lead + [(0, rows * LANES - n)])
        parts.append(a.reshape(lead_shape + (rows, LANES)))
    total = sum(p.shape[lead] for p in parts)
    padded = -(-total // PACK_ROWS) * PACK_ROWS
    if padded > total:
        parts.append(jnp.zeros(parts[0].shape[:lead] + (padded - total, LANES), parts[0].dtype))
    return jnp.concatenate(parts, axis=lead)


def _unpack(buf, shapes):
    lead = buf.shape[:-2]
    out, pos = [], 0
    for shp in shapes:
        n = math.prod(shp)
        rows = _seg_rows(n)
        seg = buf[..., pos:pos + rows, :]
        if rows * LANES != n:
            seg = seg.reshape(lead + (rows * LANES,))[..., :n]
        out.append(seg.reshape(lead + tuple(shp)))
        pos += rows
    return out


def _shards_to_full(blocks, axis):
    moved = jnp.moveaxis(blocks, 0, axis)
    shp = list(blocks.shape[1:])
    shp[axis] = shp[axis] * N_DEV
    return moved.reshape(shp)


def _full_to_shards(full, axis):
    shp = list(full.shape)
    shp[axis:axis + 1] = [N_DEV, shp[axis] // N_DEV]
    return jnp.moveaxis(full.reshape(shp), axis, 0)


def _my_place():
    return lax.axis_index("x"), lax.axis_index("y"), lax.axis_index("c")


def _slot(x, y, c):
    return 4 * x + 2 * y + c


def _gather_plan(x_ref, out_ref, send_sems, recv_sems, local_sem):
    x, y, c = _my_place()
    me, sibling = (x, y, c), (x, y, 1 - c)
    chips = [(1 - x, y), (x, 1 - y), (1 - x, 1 - y)]

    def copy(k, blk, to, src=None):
        dst = out_ref.at[_slot(*blk)]
        return pltpu.make_async_remote_copy(
            src_ref=dst if src is None else src, dst_ref=dst,
            send_sem=send_sems.at[k], recv_sem=recv_sems.at[k], device_id=to, device_id_type=MESH_ID)

    def own():
        mine = pltpu.make_async_copy(x_ref, out_ref.at[_slot(*me)], local_sem)
        return mine, [copy(0, me, sibling, src=x_ref)] + [copy(1 + j, me, (*chip, c), src=x_ref)
                                                          for j, chip in enumerate(chips)]

    return copy, own, me, sibling, chips, c


def _gather_start(*refs):
    _, own, *_ = _gather_plan(*refs)
    mine, first = own()
    mine.start()
    for cp in first:
        cp.start()


def _gather_finish(*refs):
    copy, own, me, sibling, chips, c = _gather_plan(*refs)
    passed = [copy(4 + j, (*chip, c), sibling) for j, chip in enumerate(chips)]
    for j, chip in enumerate(chips):
        copy(1 + j, (*chip, c), me).wait_recv()
        passed[j].start()
    copy(0, sibling, me).wait_recv()
    for j, chip in enumerate(chips):
        copy(4 + j, (*chip, 1 - c), me).wait_recv()
    mine, first = own()
    for cp in first + passed:
        cp.wait_send()
    mine.wait()


GATHER_SEMS = [pltpu.SemaphoreType.DMA((7,)), pltpu.SemaphoreType.DMA((7,)), pltpu.SemaphoreType.DMA]


def _all_gather(block, *, name):
    rows = block.shape[0]

    def body(x_ref, out_ref, send_sems, recv_sems, local_sem):
        _gather_start(x_ref, out_ref, send_sems, recv_sems, local_sem)
        _gather_finish(x_ref, out_ref, send_sems, recv_sems, local_sem)

    return pl.pallas_call(
        body, name=name,
        out_shape=jax.ShapeDtypeStruct((N_DEV, rows, LANES), block.dtype),
        in_specs=[HBM_SPEC], out_specs=HBM_SPEC,
        scratch_shapes=list(GATHER_SEMS),
    )(block)


N_CHIP = N_DEV // 2


def _swap_with_sibling(blocks, *, name):
    def body(g_ref, out_ref, send_sem, recv_sem):
        x, y, c = _my_place()
        cp = pltpu.make_async_remote_copy(src_ref=g_ref, dst_ref=out_ref, send_sem=send_sem, recv_sem=recv_sem,
                                          device_id=(x, y, 1 - c), device_id_type=MESH_ID)
        cp.start()
        cp.wait_recv()
        cp.wait_send()

    return pl.pallas_call(
        body, name=name,
        out_shape=jax.ShapeDtypeStruct(blocks.shape, blocks.dtype),
        in_specs=[HBM_SPEC], out_specs=HBM_SPEC,
        scratch_shapes=[pltpu.SemaphoreType.DMA, pltpu.SemaphoreType.DMA],
    )(blocks)


def _chip_all_to_all(blocks, *, name):
    def body(g_ref, out_ref, send_sems, recv_sems, local_sem):
        x, y, c = _my_place()
        mine_slot = 2 * x + y
        local = pltpu.make_async_copy(g_ref.at[mine_slot], out_ref.at[mine_slot], local_sem)
        local.start()
        copies = []
        for k in range(1, N_CHIP):
            px, py = x ^ (k >> 1), y ^ (k & 1)
            copies.append(pltpu.make_async_remote_copy(
                src_ref=g_ref.at[2 * px + py], dst_ref=out_ref.at[mine_slot],
                send_sem=send_sems.at[k - 1], recv_sem=recv_sems.at[k - 1],
                device_id=(px, py, c), device_id_type=MESH_ID))
        for cp in copies:
            cp.start()
        for cp in copies:
            cp.wait_recv()
        for cp in copies:
            cp.wait_send()
        local.wait()

    return pl.pallas_call(
        body, name=name,
        out_shape=jax.ShapeDtypeStruct(blocks.shape, blocks.dtype),
        in_specs=[HBM_SPEC], out_specs=HBM_SPEC,
        scratch_shapes=[pltpu.SemaphoreType.DMA((N_CHIP - 1,)), pltpu.SemaphoreType.DMA((N_CHIP - 1,)),
                        pltpu.SemaphoreType.DMA],
    )(blocks)


def _add_bf16_call(a, b, *, name):
    n, rows, _ = a.shape
    tr = min(PACK_ROWS, rows)

    def body(a_ref, b_ref, o_ref):
        o_ref[...] = (a_ref[...].astype(F32) + b_ref[...].astype(F32)).astype(BF16)

    blk = pl.BlockSpec((n, tr, LANES), lambda i: (0, i, 0))
    return pl.pallas_call(
        body, name=name, grid=(rows // tr,), in_specs=[blk, blk], out_specs=blk,
        out_shape=jax.ShapeDtypeStruct(a.shape, BF16),
        compiler_params=_cparams(("parallel",)),
    )(a, b)


def _adamw_call(parts, w, m, v, *, name):
    rows = w.shape[0]
    tr = min(PACK_ROWS, rows)
    assert rows % tr == 0
    n_parts = parts.shape[0]

    def body(p_ref, w_ref, m_ref, v_ref, g_ref, d_ref, nm_ref, nv_ref):
        g = p_ref[0].astype(F32)
        for s in range(1, n_parts):
            g = g + p_ref[s].astype(F32)
        nm = ADAM_B1 * m_ref[...] + (1.0 - ADAM_B1) * g
        nv = ADAM_B2 * v_ref[...] + (1.0 - ADAM_B2) * jnp.square(g)
        m_hat = nm / (1.0 - ADAM_B1 ** ADAM_STEP)
        v_hat = nv / (1.0 - ADAM_B2 ** ADAM_STEP)
        g_ref[...] = g
        nm_ref[...] = nm
        nv_ref[...] = nv
        d_ref[...] = -ADAM_LR * (m_hat / (jnp.sqrt(v_hat) + ADAM_EPS) + ADAM_WD * w_ref[...])

    blk = pl.BlockSpec((tr, LANES), lambda i: (i, 0))
    shape = jax.ShapeDtypeStruct((rows, LANES), F32)
    return pl.pallas_call(
        body, name=name, grid=(rows // tr,),
        in_specs=[pl.BlockSpec((n_parts, tr, LANES), lambda i: (0, i, 0)), blk, blk, blk],
        out_specs=[blk] * 4, out_shape=[shape] * 4,
        compiler_params=_cparams(("parallel",)),
    )(parts, w, m, v)


def _kernel_params(full):
    return {
        "norm_mix": full["norm_mix"][:, None, :],
        "b_in": _in_cols_to_kernel(full["b_in"])[:, None, :],
        "a_log": _pad_lanes(full["a_log"]),
        "dt_bias": _pad_lanes(full["dt_bias"]),
        "sinks": full["sinks"],
        "c_norm": full["c_norm"][:, None, :],
        "norm_ffn": full["norm_ffn"][:, None, :],
        "norm_final": full["norm_final"][None, :],
    }


def _reference_grads(g):
    return {
        "norm_mix": g["norm_mix"][:, 0, :],
        "b_in": _in_cols_to_reference(g["b_in"][:, 0, :]),
        "conv_w": g["conv_w"],
        "a_log": g["a_log"][:, 0, :C_V_HEADS],
        "dt_bias": g["dt_bias"][:, 0, :C_V_HEADS],
        "sinks": g["sinks"],
        "c_norm": g["c_norm"][:, 0, :],
        "w_branch_a": g["w_branch_a"], "w_branch_b": g["w_branch_b"], "w_branch_c": g["w_branch_c"],
        "w_out": g["w_out"],
        "norm_ffn": g["norm_ffn"][:, 0, :],
        "w_ff1": g["w_ff1"], "w_ff2": g["w_ff2"],
        "norm_final": g["norm_final"][0],
    }


def kernel(x, positions, norm_mix, w_in, b_in, conv_w, a_log, dt_bias, sinks, c_norm, w_branch_a, w_branch_b, w_branch_c, w_out, norm_ffn, w_ff1, w_ff2, norm_final, loss_target, m_norm_mix, m_w_in, m_b_in, m_conv_w, m_a_log, m_dt_bias, m_sinks, m_c_norm, m_w_branch_a, m_w_branch_b, m_w_branch_c, m_w_out, m_norm_ffn, m_w_ff1, m_w_ff2, m_norm_final, v_norm_mix, v_w_in, v_b_in, v_conv_w, v_a_log, v_dt_bias, v_sinks, v_c_norm, v_w_branch_a, v_w_branch_b, v_w_branch_c, v_w_out, v_norm_ffn, v_w_ff1, v_w_ff2, v_norm_final):
    env = dict(locals())
    weights = {n: env[n] for n in WEIGHTS}
    moments_m = {n: env["m_" + n] for n in WEIGHTS}
    moments_v = {n: env["v_" + n] for n in WEIGHTS}

    axis_of = {n: axis - 1 for n, axis in BIG}

    def payload_of_layer(layer):
        cw = weights["conv_w"][layer]
        c1 = cw.astype(BF16)
        c2 = (cw - c1.astype(F32)).astype(BF16)
        c3 = (cw - c1.astype(F32) - c2.astype(F32)).astype(BF16)
        return _pack([weights[n][layer].astype(BF16) for n in MATMUL_WEIGHTS] + [c1, c2, c3])

    def weights_of_layer(layer, gathered):
        shapes = [weights[n].shape[1:] for n in MATMUL_WEIGHTS] + [weights["conv_w"].shape[1:]] * 3
        blocks = _unpack(gathered, shapes)
        wb = {n: _w_in_from_shards(blk) if n == "w_in" else _shards_to_full(blk, axis_of[n])
              for n, blk in zip(MATMUL_WEIGHTS, blocks)}
        return wb, _shards_to_full(sum(b.astype(F32) for b in blocks[-3:]), axis_of["conv_w"])

    tabs = rope_tables(positions[0])
    first = _all_gather(payload_of_layer(0), name="gather_weights")
    loss, dx, dparams = _local_step(x[0], _kernel_params({n: weights[n] for n in SMALL}), first,
                                    payload_of_layer, weights_of_layer, tabs, loss_target[0])
    grads = _reference_grads(dparams)
    loss = lax.psum(loss, ("x", "y", "c"))

    core = lax.axis_index("c")

    def by_core(n, axis, which):
        sh = _w_in_to_shards(dparams[n]) if n == "w_in" else _full_to_shards(grads[n], axis)
        sh = sh.reshape((N_CHIP, 2) + sh.shape[1:])
        return lax.dynamic_index_in_dim(sh, which, axis=1, keepdims=False).astype(BF16)

    from_sibling = _swap_with_sibling(_pack([by_core(n, axis, 1 - core) for n, axis in BIG], lead=1),
                                      name="scatter_grads_d2d")
    chip_sum = _add_bf16_call(_pack([by_core(n, axis, core) for n, axis in BIG], lead=1), from_sibling,
                              name="scatter_grads_add")
    big_parts = _chip_all_to_all(chip_sum, name="scatter_grads_ici")
    small_parts = _all_gather(_pack([grads[n] for n in SMALL]), name="gather_small_grads")

    out = {}
    for names, parts in (([n for n, _ in BIG], big_parts), (list(SMALL), small_parts)):
        shapes = [weights[n].shape for n in names]
        packed = [_pack([d[n] for n in names]) for d in (weights, moments_m, moments_v)]
        results = _adamw_call(parts, *packed, name="adamw_" + names[0])
        for kind, buf in zip(("grad", "delta", "new_m", "new_v"), results):
            for n, arr in zip(names, _unpack(buf, shapes)):
                out[kind, n] = arr
    return (loss, dx[None], *[out[kind, n] for kind in ("grad", "delta", "new_m", "new_v") for n in WEIGHTS])
```

```python
import math

import jax
import jax.numpy as jnp
from jax import lax
from jax.experimental import pallas as pl
from jax.experimental.pallas import tpu as pltpu

F32 = jnp.float32
BF16 = jnp.bfloat16

N_DEV = 8
D_MODEL = 1024
DEPTH = 2
HEAD_DIM = 64
ROT_DIM = 16
ROPE_THETA = 500000.0
BLK = 128
NEG_INF = -1e30
EPS = 1e-6
A_CONFIGS = ((128, 1), (512, 4), (2048, 16))
B_GROUP = 4
C_QK_HEADS = 4
C_V_HEADS = 8
C_DK = 128
C_CONV = 4
CHUNK = 64
ADAM_LR = 0.001
ADAM_B1 = 0.9
ADAM_B2 = 0.999
ADAM_EPS = 1e-08
ADAM_WD = 0.01
ADAM_STEP = 10

IN_LAYOUT = (
    ("gate_a", 0, 1024, 5392), ("gate_b", 1024, 1024, 6416), ("gate_c", 2048, 1024, 7440),
    ("a_q", 3072, 512, 0), ("a_k", 3584, 512, 512), ("a_v", 4096, 512, 1024), ("b_q", 4608, 512, 1536),
    ("c_z", 5120, 1024, 4352), ("c_qkv", 6144, 2048, 2304),
    ("b_k", 8192, 128, 2048), ("b_v", 8320, 128, 2176), ("c_ab", 8448, 16, 5376),
)
COL = {name: start for name, start, _, _ in IN_LAYOUT}
D_IN_PAD = 8704
IN_TN = D_IN_PAD // 4
LANES = 128
VMEM_LIMIT = 56 * 1024 * 1024


def _cparams(sem=None):
    return pltpu.CompilerParams(dimension_semantics=sem, vmem_limit_bytes=VMEM_LIMIT)


def _relu2(t):
    return jnp.square(jnp.maximum(t, 0.0))


def _mm(a, b, *, ta=False, tb=False, bias=None, a_fn=None, mul_drelu2=None, add=None,
        out_dtype=F32, relu2_out=False, b_colsum=False, tm=1024, tn=1024, tk=2048, name):
    if ta:
        kdim, m = a.shape
    else:
        m, kdim = a.shape
    n = b.shape[0] if tb else b.shape[1]
    tm, tn, tk = min(tm, m), min(tn, n), min(tk, kdim)
    assert m % tm == 0 and n % tn == 0 and kdim % tk == 0, (a.shape, b.shape, tm, tn, tk)
    nk = kdim // tk
    assert not b_colsum or (m == tm and not tb and nk > 1)
    dims = (((0 if ta else 1,), (1 if tb else 0,)), ((), ()))
    extras = [e for e in (bias, mul_drelu2, add) if e is not None]

    def body(*refs):
        a_ref, b_ref = refs[0], refs[1]
        pos = 2
        bias_ref = pre_ref = add_ref = None
        if bias is not None:
            bias_ref = refs[pos]; pos += 1
        if mul_drelu2 is not None:
            pre_ref = refs[pos]; pos += 1
        if add is not None:
            add_ref = refs[pos]; pos += 1
        o_ref = refs[pos]
        pos += 1
        r_ref = None
        if relu2_out:
            r_ref = refs[pos]; pos += 1
        cs_ref = None
        if b_colsum:
            cs_ref = refs[pos]; pos += 1
        acc_ref = refs[pos] if nk > 1 else None
        cs_acc = refs[pos + 1] if b_colsum else None

        av = a_ref[...]
        if a_fn is not None:
            av = a_fn(av)
        bv = b_ref[...]
        part = lax.dot_general(av.astype(BF16), bv.astype(BF16), dims,
                               preferred_element_type=F32)
        if b_colsum:
            cs_part = jnp.sum(bv.astype(F32).reshape(tk // 8, 8, tn), axis=0)

        def finish(acc):
            if bias_ref is not None:
                acc = acc + bias_ref[...]
            if pre_ref is not None:
                acc = acc * (2.0 * jnp.maximum(pre_ref[...], 0.0))
            if add_ref is not None:
                acc = acc + add_ref[...]
            o_ref[...] = acc.astype(out_dtype)
            if r_ref is not None:
                r_ref[...] = _relu2(acc).astype(BF16)

        if nk == 1:
            finish(part)
        else:
            k = pl.program_id(2)

            @pl.when(k == 0)
            def _():
                acc_ref[...] = part
                if b_colsum:
                    cs_acc[...] = cs_part

            @pl.when(k > 0)
            def _():
                acc_ref[...] += part
                if b_colsum:
                    cs_acc[...] += cs_part

            @pl.when(k == nk - 1)
            def _():
                finish(acc_ref[...])
                if b_colsum:
                    cs_ref[...] = jnp.sum(cs_acc[...], axis=0, keepdims=True)

    a_spec = (pl.BlockSpec((tk, tm), lambda i, j, k: (k, i)) if ta
              else pl.BlockSpec((tm, tk), lambda i, j, k: (i, k)))
    b_spec = (pl.BlockSpec((tn, tk), lambda i, j, k: (j, k)) if tb
              else pl.BlockSpec((tk, tn), lambda i, j, k: (k, j)))
    in_specs = [a_spec, b_spec]
    if bias is not None:
        in_specs.append(pl.BlockSpec((1, tn), lambda i, j, k: (0, j)))
    for _ in extras[(1 if bias is not None else 0):]:
        in_specs.append(pl.BlockSpec((tm, tn), lambda i, j, k: (i, j)))
    o_spec = pl.BlockSpec((tm, tn), lambda i, j, k: (i, j))
    out_specs, out_shape = [o_spec], [jax.ShapeDtypeStruct((m, n), out_dtype)]
    scratch = [pltpu.VMEM((tm, tn), F32)] if nk > 1 else []
    if relu2_out:
        out_specs.append(o_spec)
        out_shape.append(jax.ShapeDtypeStruct((m, n), BF16))
    if b_colsum:
        out_specs.append(pl.BlockSpec((1, tn), lambda i, j, k: (0, j)))
        out_shape.append(jax.ShapeDtypeStruct((1, n), F32))
        scratch.append(pltpu.VMEM((8, tn), F32))
    single = len(out_specs) == 1
    return pl.pallas_call(
        body, name=name,
        grid=(m // tm, n // tn, nk),
        in_specs=in_specs,
        out_specs=out_specs[0] if single else out_specs,
        out_shape=out_shape[0] if single else out_shape,
        scratch_shapes=scratch,
        compiler_params=_cparams(("parallel", "parallel", "arbitrary")),
    )(a, b, *extras)


def _rms_fwd_call(x, g, *, name, out_dtype=F32, tq=512):
    t, d = x.shape

    def body(x_ref, g_ref, y_ref):
        xv = x_ref[...]
        r = lax.rsqrt(jnp.mean(xv * xv, axis=-1, keepdims=True) + EPS)
        y_ref[...] = (xv * r * g_ref[...]).astype(out_dtype)

    return pl.pallas_call(
        body, name=name, grid=(t // tq,),
        in_specs=[pl.BlockSpec((tq, d), lambda i: (i, 0)), pl.BlockSpec((1, d), lambda i: (0, 0))],
        out_specs=pl.BlockSpec((tq, d), lambda i: (i, 0)),
        out_shape=jax.ShapeDtypeStruct((t, d), out_dtype),
        compiler_params=_cparams(("parallel",)),
    )(x, g)


def _rms_bwd_call(x, g, dy, *, name, add=None, tq=512):
    t, d = x.shape
    nt = t // tq

    def body(*refs):
        if add is None:
            x_ref, g_ref, dy_ref, dx_ref, dg_ref, acc_ref = refs
        else:
            x_ref, g_ref, dy_ref, add_ref, dx_ref, dg_ref, acc_ref = refs
        i = pl.program_id(0)
        xv = x_ref[...]
        r = lax.rsqrt(jnp.mean(xv * xv, axis=-1, keepdims=True) + EPS)
        xh = xv * r
        dyv = dy_ref[...]
        dxh = dyv * g_ref[...]
        dx = r * (dxh - xh * jnp.mean(dxh * xh, axis=-1, keepdims=True))
        dx_ref[...] = dx if add is None else dx + add_ref[...]
        part = jnp.sum((dyv * xh).reshape(tq // 8, 8, d), axis=0)

        @pl.when(i == 0)
        def _():
            acc_ref[...] = part

        @pl.when(i > 0)
        def _():
            acc_ref[...] += part

        @pl.when(i == nt - 1)
        def _():
            dg_ref[...] = jnp.sum(acc_ref[...], axis=0, keepdims=True)

    blk = pl.BlockSpec((tq, d), lambda i: (i, 0))
    row = pl.BlockSpec((1, d), lambda i: (0, 0))
    extra = [] if add is None else [add]
    return pl.pallas_call(
        body, name=name, grid=(nt,),
        in_specs=[blk, row, blk] + [blk] * len(extra),
        out_specs=[blk, row],
        out_shape=[jax.ShapeDtypeStruct((t, d), F32), jax.ShapeDtypeStruct((1, d), F32)],
        scratch_shapes=[pltpu.VMEM((8, d), F32)],
        compiler_params=_cparams(("arbitrary",)),
    )(x, g, dy, *extra)


def _loss_fwd_call(x, g, tgt, *, tq=512):
    t, d = x.shape
    nt = t // tq

    def body(x_ref, g_ref, t_ref, o_ref, acc_ref):
        i = pl.program_id(0)
        xv = x_ref[...]
        r = lax.rsqrt(jnp.mean(xv * xv, axis=-1, keepdims=True) + EPS)
        err = xv * r * g_ref[...] - t_ref[...]
        part = jnp.sum((err * err).reshape(tq // 8, 8, d), axis=0)

        @pl.when(i == 0)
        def _():
            acc_ref[...] = part

        @pl.when(i > 0)
        def _():
            acc_ref[...] += part

        @pl.when(i == nt - 1)
        def _():
            tot = jnp.sum(jnp.sum(acc_ref[...], axis=0, keepdims=True), axis=1, keepdims=True)
            o_ref[...] = jnp.broadcast_to(tot * (0.5 / d), (8, LANES))

    out = pl.pallas_call(
        body, name="loss_fwd", grid=(nt,),
        in_specs=[pl.BlockSpec((tq, d), lambda i: (i, 0)), pl.BlockSpec((1, d), lambda i: (0, 0)),
                  pl.BlockSpec((tq, d), lambda i: (i, 0))],
        out_specs=pl.BlockSpec((8, LANES), lambda i: (0, 0)),
        out_shape=jax.ShapeDtypeStruct((8, LANES), F32),
        scratch_shapes=[pltpu.VMEM((8, d), F32)],
        compiler_params=_cparams(("arbitrary",)),
    )(x, g, tgt)
    return out[0, 0]


def _loss_bwd_call(x, g, tgt, scale, *, tq=512):
    t, d = x.shape
    nt = t // tq

    def body(x_ref, g_ref, t_ref, s_ref, dx_ref, dg_ref, acc_ref):
        i = pl.program_id(0)
        xv = x_ref[...]
        r = lax.rsqrt(jnp.mean(xv * xv, axis=-1, keepdims=True) + EPS)
        xh = xv * r
        gv = g_ref[...]
        dyv = (xh * gv - t_ref[...]) * (s_ref[...] * (1.0 / d))
        dxh = dyv * gv
        dx_ref[...] = r * (dxh - xh * jnp.mean(dxh * xh, axis=-1, keepdims=True))
        part = jnp.sum((dyv * xh).reshape(tq // 8, 8, d), axis=0)

        @pl.when(i == 0)
        def _():
            acc_ref[...] = part

        @pl.when(i > 0)
        def _():
            acc_ref[...] += part

        @pl.when(i == nt - 1)
        def _():
            dg_ref[...] = jnp.sum(acc_ref[...], axis=0, keepdims=True)

    return pl.pallas_call(
        body, name="loss_bwd", grid=(nt,),
        in_specs=[pl.BlockSpec((tq, d), lambda i: (i, 0)), pl.BlockSpec((1, d), lambda i: (0, 0)),
                  pl.BlockSpec((tq, d), lambda i: (i, 0)), pl.BlockSpec((1, 1), lambda i: (0, 0))],
        out_specs=[pl.BlockSpec((tq, d), lambda i: (i, 0)), pl.BlockSpec((1, d), lambda i: (0, 0))],
        out_shape=[jax.ShapeDtypeStruct((t, d), F32), jax.ShapeDtypeStruct((1, d), F32)],
        scratch_shapes=[pltpu.VMEM((8, d), F32)],
        compiler_params=_cparams(("arbitrary",)),
    )(x, g, tgt, scale)


MESH_ID = pl.DeviceIdType.MESH
HBM_SPEC = pl.BlockSpec(memory_space=pl.ANY)


def rope_tables(positions):
    inv_freq = jnp.power(ROPE_THETA, -jnp.arange(0, ROT_DIM, 2, dtype=F32) / ROT_DIM)
    ang = positions.astype(F32)[:, None] * inv_freq
    cos, sin = jnp.cos(ang), jnp.sin(ang)
    t = positions.shape[0]
    one = jnp.ones((t, HEAD_DIM - ROT_DIM), F32)
    zero8 = jnp.zeros((t, ROT_DIM // 2), F32)
    zero = jnp.zeros((t, HEAD_DIM - ROT_DIM), F32)
    a = jnp.concatenate([cos, cos, one], axis=1)
    b = jnp.concatenate([zero8, sin, zero], axis=1)
    c = jnp.concatenate([-sin, zero8, zero], axis=1)
    return tuple(jnp.concatenate([m, m], axis=1) for m in (a, b, c))


def _rope_chunk(xs, a, b, c, transpose):
    half = ROT_DIM // 2
    if transpose:
        return xs * a + pltpu.roll(xs * b, LANES - half, 1) + pltpu.roll(xs * c, half, 1)
    return xs * a + pltpu.roll(xs, half, 1) * b + pltpu.roll(xs, LANES - half, 1) * c


def _rope_gather_call(u, tabs, parts, *, name, tq=512):
    t = u.shape[0]
    total = sum(w for _, w, _ in parts)
    assert all(start % w == 0 for start, w, _ in parts)

    def body(a_ref, b_ref, c_ref, *refs):
        o_ref = refs[-1]
        a, b, c = a_ref[...], b_ref[...], c_ref[...]
        off = 0
        for x_ref, (_, w, roped) in zip(refs[:-1], parts):
            for j in range(w // LANES):
                xs = x_ref[:, j * LANES:(j + 1) * LANES]
                val = _rope_chunk(xs, a, b, c, False) if roped else xs
                o_ref[:, off + j * LANES:off + (j + 1) * LANES] = val.astype(BF16)
            off += w

    tab_spec = pl.BlockSpec((tq, LANES), lambda i: (i, 0))
    return pl.pallas_call(
        body, name=name, grid=(t // tq,),
        in_specs=[tab_spec] * 3 + [pl.BlockSpec((tq, w), lambda i, cb=start // w: (i, cb)) for start, w, _ in parts],
        out_specs=pl.BlockSpec((tq, total), lambda i: (i, 0)),
        out_shape=jax.ShapeDtypeStruct((t, total), BF16),
        compiler_params=_cparams(("parallel",)),
    )(*tabs, *([u] * len(parts)))


def _du_operands(du_buf, n_inputs):
    if du_buf is None:
        return [], [], {}
    return [du_buf], [HBM_SPEC], {n_inputs: 0}


def _du_shape(t):
    return jax.ShapeDtypeStruct((t, D_IN_PAD), BF16)


def _rope_scatter_call(du_buf, t, pieces, col, tabs, *, name, tq=512):
    total = sum(w for _, w, _ in pieces)
    assert col % total == 0
    arrays = [a for arrs, _, _ in pieces for a in arrs]
    extra, extra_specs, aliases = _du_operands(du_buf, 3 + len(arrays))

    def body(a_ref, b_ref, c_ref, *refs):
        o_ref = refs[len(arrays) + len(extra)]
        a, b, c = a_ref[...], b_ref[...], c_ref[...]
        k = off = 0
        for arrs, w, roped in pieces:
            mine = refs[k:k + len(arrs)]
            k += len(arrs)
            for j in range(w // LANES):
                cs = slice(j * LANES, (j + 1) * LANES)
                xs = mine[0][:, cs].astype(F32)
                for r in mine[1:]:
                    xs = xs + r[:, cs].astype(F32)
                val = _rope_chunk(xs, a, b, c, True) if roped else xs
                o_ref[:, off + j * LANES:off + (j + 1) * LANES] = val.astype(BF16)
            off += w

    tab_spec = pl.BlockSpec((tq, LANES), lambda i: (i, 0))
    in_specs = [tab_spec] * 3 + [pl.BlockSpec((tq, w), lambda i: (i, 0)) for arrs, w, _ in pieces for _ in arrs]
    return pl.pallas_call(
        body, name=name, grid=(t // tq,),
        in_specs=in_specs + extra_specs,
        out_specs=pl.BlockSpec((tq, total), lambda i: (i, col // total)),
        out_shape=_du_shape(t), input_output_aliases=aliases,
        compiler_params=_cparams(("parallel",)),
    )(*tabs, *arrays, *extra)


def _band_masks(first_block, max_dist):
    qi = lax.broadcasted_iota(jnp.int32, (BLK, BLK), 0)
    kj = lax.broadcasted_iota(jnp.int32, (BLK, BLK), 1)
    valid_prev = jnp.logical_and(kj >= qi + (BLK - max_dist), jnp.logical_not(first_block))
    valid_cur = kj <= qi
    return valid_prev, valid_cur


_NN = (((1,), (0,)), ((), ()))
_NT = (((1,), (1,)), ((), ()))
_TN = (((0,), (0,)), ((), ()))


HEAD_STAGE = 8


def _attn_row_maps(nb):
    def cur(i):
        return jnp.minimum(i, nb - 1)

    def prev(i):
        return jnp.maximum(jnp.minimum(i, nb - 1) - 1, 0)

    return cur, prev


def _dil_view(a, dil):
    t, w = a.shape
    return a.reshape(t // dil, dil * w)


def _dil_spec(w, dil, rows, seg=None, off=0):
    seg = w if seg is None else seg
    assert off % w == 0 and (dil == 1 or seg % w == 0)
    return pl.BlockSpec((BLK, w), lambda r, i: (rows(i), (r * seg + off) // w))


def _dil_shape(l, dil, w, dtype=F32):
    return jax.ShapeDtypeStruct((l, dil * w), dtype)


def _attn_fwd_call(qkv2, sink, *, dil, group, max_dist, seg, offs, qw, kw, name):
    l = qkv2.shape[0]
    nh = qw // HEAD_DIM
    nb = l // BLK
    scale = HEAD_DIM ** -0.5
    use_sink = sink is not None

    def body(*refs):
        if use_sink:
            sink_ref, refs = refs[0], refs[1:]
        q_ref, kp_ref, kc_ref, vp_ref, vc_ref, o_ref, lse_ref = refs
        valid_prev, valid_cur = _band_masks(pl.program_id(1) == 0, max_dist)
        lane = lax.broadcasted_iota(jnp.int32, (BLK, LANES), 1)
        lse_tile = jnp.zeros((BLK, LANES), F32)

        def dot(a, b, dims=_NN):
            return lax.dot_general(a, b, dims, preferred_element_type=F32)

        for g0 in range(0, nh, HEAD_STAGE):
            heads = list(range(g0, min(g0 + HEAD_STAGE, nh)))
            kv = {}
            for kh in sorted({h // group for h in heads}):
                ks = slice(kh * HEAD_DIM, (kh + 1) * HEAD_DIM)
                kv[kh] = tuple(ref[:, ks].astype(BF16) for ref in (kp_ref, kc_ref, vp_ref, vc_ref))
            qs = [q_ref[:, h * HEAD_DIM:(h + 1) * HEAD_DIM].astype(BF16) for h in heads]
            sps = [jnp.where(valid_prev, dot(qh, kv[h // group][0], _NT) * scale, NEG_INF) for h, qh in zip(heads, qs)]
            scs = [jnp.where(valid_cur, dot(qh, kv[h // group][1], _NT) * scale, NEG_INF) for h, qh in zip(heads, qs)]
            ms = [jnp.maximum(jnp.max(sp, axis=1, keepdims=True), jnp.max(sc, axis=1, keepdims=True))
                  for sp, sc in zip(sps, scs)]
            if use_sink:
                ms = [jnp.maximum(m, sink_ref[h]) for h, m in zip(heads, ms)]
            pps = [jnp.exp(sp - m) for sp, m in zip(sps, ms)]
            pcs = [jnp.exp(sc - m) for sc, m in zip(scs, ms)]
            dens = [jnp.sum(pp, axis=1, keepdims=True) + jnp.sum(pc, axis=1, keepdims=True)
                    for pp, pc in zip(pps, pcs)]
            if use_sink:
                dens = [den + jnp.exp(sink_ref[h] - m) for h, den, m in zip(heads, dens, ms)]
            outs = [dot(pp.astype(BF16), kv[h // group][2]) + dot(pc.astype(BF16), kv[h // group][3])
                    for h, pp, pc in zip(heads, pps, pcs)]
            for h, o, den, m in zip(heads, outs, dens, ms):
                o_ref[:, h * HEAD_DIM:(h + 1) * HEAD_DIM] = o / den
                lse_tile = jnp.where(lane == h, m + jnp.log(den), lse_tile)
        lse_ref[...] = lse_tile

    cur, prev = _attn_row_maps(nb)
    o_spec, lse_spec = _dil_spec(qw, dil, cur), _dil_spec(LANES, dil, cur)
    in_specs = [_dil_spec(qw, dil, cur, seg, offs[0]),
                _dil_spec(kw, dil, prev, seg, offs[1]), _dil_spec(kw, dil, cur, seg, offs[1]),
                _dil_spec(kw, dil, prev, seg, offs[2]), _dil_spec(kw, dil, cur, seg, offs[2])]
    args = [qkv2] * 5
    if use_sink:
        in_specs = [pl.BlockSpec(memory_space=pltpu.SMEM)] + in_specs
        args = [sink] + args
    return pl.pallas_call(
        body, name=name, grid=(dil, nb),
        in_specs=in_specs,
        out_specs=[o_spec, lse_spec],
        out_shape=[_dil_shape(l, dil, qw), _dil_shape(l, dil, LANES)],
        compiler_params=_cparams(("parallel", "parallel")),
    )(*args)


def _attn_bwd_call(qkv2, sink, o2, lse2, do2, dlse2, *, dil, group, max_dist, seg, offs, qw, kw, name):
    l = qkv2.shape[0]
    nh = qw // HEAD_DIM
    nb = l // BLK
    scale = HEAD_DIM ** -0.5
    use_sink = sink is not None

    def body(*refs):
        if use_sink:
            sink_ref, refs = refs[0], refs[1:]
        (q_ref, kp_ref, kc_ref, vp_ref, vc_ref, o_ref, lse_ref, do_ref, dlse_ref,
         dq_ref, dk_ref, dv_ref, dsink_ref, ck_ref, cv_ref) = refs
        step = pl.program_id(1)

        @pl.when(jnp.logical_and(pl.program_id(0) == 0, step == 0))
        def _():
            dsink_ref[...] = jnp.zeros_like(dsink_ref)

        @pl.when(step == 0)
        def _():
            ck_ref[...] = jnp.zeros_like(ck_ref)
            cv_ref[...] = jnp.zeros_like(cv_ref)

        def dot(a, b, dims=_NN):
            return lax.dot_general(a, b, dims, preferred_element_type=F32)

        @pl.when(step < nb)
        def _():
            valid_prev, valid_cur = _band_masks(step == 0, max_dist)
            row = lax.broadcasted_iota(jnp.int32, (8, LANES), 0)
            lanes8 = lax.broadcasted_iota(jnp.int32, (8, LANES), 1)
            ds_tile = jnp.zeros((8, LANES), F32)
            for g0 in range(0, nh, HEAD_STAGE):
                heads = list(range(g0, min(g0 + HEAD_STAGE, nh)))
                hss = [slice(h * HEAD_DIM, (h + 1) * HEAD_DIM) for h in heads]
                kv = {}
                for kh in sorted({h // group for h in heads}):
                    ks = slice(kh * HEAD_DIM, (kh + 1) * HEAD_DIM)
                    kv[kh] = tuple(ref[:, ks].astype(BF16) for ref in (kp_ref, kc_ref, vp_ref, vc_ref))
                qs = [q_ref[:, hs].astype(BF16) for hs in hss]
                dos = [do_ref[:, hs] for hs in hss]
                dobs = [d.astype(BF16) for d in dos]
                lses = [lse_ref[:, h:h + 1] for h in heads]
                sps = [dot(qh, kv[h // group][0], _NT) * scale for h, qh in zip(heads, qs)]
                scs = [dot(qh, kv[h // group][1], _NT) * scale for h, qh in zip(heads, qs)]
                dpps = [dot(dob, kv[h // group][2], _NT) for h, dob in zip(heads, dobs)]
                dpcs = [dot(dob, kv[h // group][3], _NT) for h, dob in zip(heads, dobs)]
                pps = [jnp.where(valid_prev, jnp.exp(jnp.where(valid_prev, sp, NEG_INF) - ls), 0.0)
                       for sp, ls in zip(sps, lses)]
                pcs = [jnp.where(valid_cur, jnp.exp(jnp.where(valid_cur, sc, NEG_INF) - ls), 0.0)
                       for sc, ls in zip(scs, lses)]
                deltas = [jnp.sum(d * o_ref[:, hs], axis=1, keepdims=True) for d, hs in zip(dos, hss)]
                corrs = [dlse_ref[:, h:h + 1] - dl for h, dl in zip(heads, deltas)]
                dsps = [(pp * (dp + c)).astype(BF16) for pp, dp, c in zip(pps, dpps, corrs)]
                dscs = [(pc * (dp + c)).astype(BF16) for pc, dp, c in zip(pcs, dpcs, corrs)]
                for h, hs, dsp, dsc in zip(heads, hss, dsps, dscs):
                    dq = (dot(dsp, kv[h // group][0]) + dot(dsc, kv[h // group][1])) * scale
                    dq_ref[:, hs] = dq.astype(BF16)
                parts = [(dot(dsc, qh, _TN) * scale, dot(dsp, qh, _TN) * scale,
                          dot(pc.astype(BF16), dob, _TN), dot(pp.astype(BF16), dob, _TN))
                         for dsc, dsp, qh, pc, pp, dob in zip(dscs, dsps, qs, pcs, pps, dobs)]
                for kh in kv:
                    ks = slice(kh * HEAD_DIM, (kh + 1) * HEAD_DIM)
                    mine = [p for h, p in zip(heads, parts) if h // group == kh]
                    dkc, dkp, dvc, dvp = (sum(p[j] for p in mine[1:]) + mine[0][j] for j in range(4))
                    dk_ref[:, ks] = (ck_ref[:, ks] + dkp).astype(BF16)
                    dv_ref[:, ks] = (cv_ref[:, ks] + dvp).astype(BF16)
                    ck_ref[:, ks] = dkc
                    cv_ref[:, ks] = dvc
                if use_sink:
                    for h, ls, dl in zip(heads, lses, deltas):
                        val = -jnp.sum(jnp.exp(sink_ref[h] - ls) * dl, axis=0, keepdims=True)
                        ds_tile = jnp.where(jnp.logical_and(row == 0, lanes8 == h), val, ds_tile)
            if use_sink:
                dsink_ref[...] += ds_tile

        @pl.when(step == nb)
        def _():
            dk_ref[...] = ck_ref[...].astype(BF16)
            dv_ref[...] = cv_ref[...].astype(BF16)

    cur, prev = _attn_row_maps(nb)
    q_spec, lse_spec = _dil_spec(qw, dil, cur), _dil_spec(LANES, dil, cur)
    lag_spec = _dil_spec(kw, dil, lambda i: jnp.maximum(i - 1, 0))
    in_specs = [_dil_spec(qw, dil, cur, seg, offs[0]),
                _dil_spec(kw, dil, prev, seg, offs[1]), _dil_spec(kw, dil, cur, seg, offs[1]),
                _dil_spec(kw, dil, prev, seg, offs[2]), _dil_spec(kw, dil, cur, seg, offs[2]),
                q_spec, lse_spec, q_spec, lse_spec]
    args = [qkv2] * 5 + [o2, lse2, do2, dlse2]
    if use_sink:
        in_specs = [pl.BlockSpec(memory_space=pltpu.SMEM)] + in_specs
        args = [sink] + args
    kv_shape = _dil_shape(l, dil, kw, BF16)
    return pl.pallas_call(
        body, name=name, grid=(dil, nb + 1),
        in_specs=in_specs,
        out_specs=[q_spec, lag_spec, lag_spec, pl.BlockSpec((8, LANES), lambda r, i: (0, 0))],
        out_shape=[_dil_shape(l, dil, qw, BF16), kv_shape, kv_shape,
                   jax.ShapeDtypeStruct((8, LANES), F32)],
        scratch_shapes=[pltpu.VMEM((BLK, kw), F32), pltpu.VMEM((BLK, kw), F32)],
        compiler_params=_cparams(("arbitrary", "arbitrary")),
    )(*args)


def _attn_config(tag, dil, group, max_dist, seg, offs, qw, kw):
    return dict(name=tag, dil=dil, group=group, max_dist=max_dist, seg=seg, offs=offs, qw=qw, kw=kw)


A_W = 8 * HEAD_DIM
ATTN_A_CFGS = tuple(_attn_config("attn_a%d" % dil, dil, 1, window // dil, 3 * A_W, (0, A_W, 2 * A_W), A_W, A_W)
                    for window, dil in A_CONFIGS)
B_KVW = 2 * HEAD_DIM
ATTN_B_CFG = _attn_config("attn_b", 1, B_GROUP, BLK - 1, A_W + 2 * B_KVW, (0, A_W, A_W + B_KVW), A_W, B_KVW)


def _attn_fwd(cfg, qkv, sink):
    t = qkv.shape[0]
    kw = {k: v for k, v in cfg.items() if k != "name"}
    qkv2 = _dil_view(qkv, cfg["dil"])
    o2, lse2 = _attn_fwd_call(qkv2, sink, name=cfg["name"] + "_fwd", **kw)
    return o2.reshape(t, cfg["qw"]), lse2.reshape(t, LANES), (qkv2, o2, lse2)


def _attn_bwd(cfg, saved, sink, do, dlse):
    qkv2, o2, lse2 = saved
    t = do.shape[0]
    kw = {k: v for k, v in cfg.items() if k != "name"}
    dq2, dk2, dv2, dsink = _attn_bwd_call(qkv2, sink, o2, lse2, _dil_view(do, cfg["dil"]),
                                          _dil_view(dlse, cfg["dil"]), name=cfg["name"] + "_bwd", **kw)
    return dq2.reshape(t, cfg["qw"]), dk2.reshape(t, cfg["kw"]), dv2.reshape(t, cfg["kw"]), dsink


def _head_expand():
    r = lax.broadcasted_iota(jnp.int32, (LANES, 8 * HEAD_DIM), 0)
    c = lax.broadcasted_iota(jnp.int32, (LANES, 8 * HEAD_DIM), 1)
    return (c // HEAD_DIM == r).astype(F32)


def _combine_weights(l0, l1, l2):
    m = jnp.maximum(jnp.maximum(l0, l1), l2)
    e0, e1, e2 = jnp.exp(l0 - m), jnp.exp(l1 - m), jnp.exp(l2 - m)
    inv = 1.0 / (e0 + e1 + e2)
    return e0 * inv, e1 * inv, e2 * inv


def _combine_fwd_call(os_, lses, *, tq=256):
    t, w = os_[0].shape

    def body(o0, o1, o2, l0, l1, l2, y_ref):
        ws = _combine_weights(l0[...], l1[...], l2[...])
        e = _head_expand()
        y = jnp.zeros((tq, w), F32)
        for o_ref, wt in zip((o0, o1, o2), ws):
            y = y + _dot_mask(e, wt, mask_left=False) * o_ref[...]
        y_ref[...] = y

    o_spec = pl.BlockSpec((tq, w), lambda i: (i, 0))
    l_spec = pl.BlockSpec((tq, LANES), lambda i: (i, 0))
    return pl.pallas_call(
        body, name="combine_fwd", grid=(t // tq,),
        in_specs=[o_spec] * 3 + [l_spec] * 3, out_specs=o_spec,
        out_shape=jax.ShapeDtypeStruct((t, w), F32),
        compiler_params=_cparams(("parallel",)),
    )(*os_, *lses)


def _combine_bwd_call(os_, lses, dy, *, tq=256):
    t, w = dy.shape

    def body(o0, o1, o2, l0, l1, l2, dy_ref, do0, do1, do2, dl0, dl1, dl2):
        ws = _combine_weights(l0[...], l1[...], l2[...])
        e = _head_expand()
        dyv = dy_ref[...]
        dws = []
        for o_ref, do_ref, wt in zip((o0, o1, o2), (do0, do1, do2), ws):
            do_ref[...] = _dot_mask(e, wt, mask_left=False) * dyv
            dws.append(_dot_mask(e, dyv * o_ref[...], _NT, mask_left=False))
        mean = ws[0] * dws[0] + ws[1] * dws[1] + ws[2] * dws[2]
        for dl_ref, wt, dw in zip((dl0, dl1, dl2), ws, dws):
            dl_ref[...] = wt * (dw - mean)

    o_spec = pl.BlockSpec((tq, w), lambda i: (i, 0))
    l_spec = pl.BlockSpec((tq, LANES), lambda i: (i, 0))
    o_shape = jax.ShapeDtypeStruct((t, w), F32)
    l_shape = jax.ShapeDtypeStruct((t, LANES), F32)
    return pl.pallas_call(
        body, name="combine_bwd", grid=(t // tq,),
        in_specs=[o_spec] * 3 + [l_spec] * 3 + [o_spec], out_specs=[o_spec] * 3 + [l_spec] * 3,
        out_shape=[o_shape] * 3 + [l_shape] * 3,
        compiler_params=_cparams(("parallel",)),
    )(*os_, *lses, dy)


C_QKW = C_QK_HEADS * C_DK
C_CONV_W = 2 * C_QKW + C_V_HEADS * C_DK
HALO = 8


def _silu_parts(z):
    sig = jax.nn.sigmoid(z)
    return z * sig, sig * (1.0 + z * (1.0 - sig))


def _conv_window_specs(tq, t):
    c = C_CONV_W
    cb = COL["c_qkv"] // c
    blk = pl.BlockSpec((tq, c), lambda i: (i, cb))
    before = pl.BlockSpec((HALO, c), lambda i: (jnp.maximum(i * (tq // HALO) - 1, 0), cb))
    return c, cb, blk, before


def _conv_prep_fwd_call(u, w, *, tq=512):
    t = u.shape[0]
    c, _, x_spec, halo_spec = _conv_window_specs(tq, t)
    nqk = 2 * C_QK_HEADS

    def body(x_ref, halo_ref, w_ref, z_ref, qk_ref, v_ref):
        i = pl.program_id(0)
        halo = jnp.where(i == 0, 0.0, halo_ref[...])
        xc = jnp.concatenate([halo, x_ref[...]], axis=0)
        wv = w_ref[...]
        z = xc[HALO - 3:HALO - 3 + tq] * wv[0:1]
        for j in range(1, C_CONV):
            z = z + xc[HALO - 3 + j:HALO - 3 + j + tq] * wv[j:j + 1]
        z_ref[...] = z
        act, _ = _silu_parts(z)
        for h in range(nqk):
            a = act[:, h * C_DK:(h + 1) * C_DK]
            qk_ref[:, h * C_DK:(h + 1) * C_DK] = a * lax.rsqrt(jnp.sum(a * a, axis=1, keepdims=True) + EPS)
        v_ref[...] = act[:, nqk * C_DK:]

    return pl.pallas_call(
        body, name="conv_prep_fwd", grid=(t // tq,),
        in_specs=[x_spec, halo_spec, pl.BlockSpec((C_CONV, c), lambda i: (0, 0))],
        out_specs=[pl.BlockSpec((tq, c), lambda i: (i, 0)),
                   pl.BlockSpec((tq, 2 * C_QKW), lambda i: (i, 0)),
                   pl.BlockSpec((tq, c - 2 * C_QKW), lambda i: (i, 0))],
        out_shape=[jax.ShapeDtypeStruct((t, c), F32), jax.ShapeDtypeStruct((t, 2 * C_QKW), F32),
                   jax.ShapeDtypeStruct((t, c - 2 * C_QKW), F32)],
        compiler_params=_cparams(("parallel",)),
    )(u, u, w)


def _conv_prep_dz_call(z, dqk, dv, *, tq=512):
    t, c = z.shape
    nqk = 2 * C_QK_HEADS

    def body(z_ref, dqk_ref, dv_ref, dz_ref):
        zv = z_ref[...]
        act, dact = _silu_parts(zv)
        for h in range(nqk):
            hs = slice(h * C_DK, (h + 1) * C_DK)
            a = act[:, hs]
            r = lax.rsqrt(jnp.sum(a * a, axis=1, keepdims=True) + EPS)
            nrm = a * r
            dn = dqk_ref[:, hs]
            da = r * (dn - nrm * jnp.sum(dn * nrm, axis=1, keepdims=True))
            dz_ref[:, hs] = da * dact[:, hs]
        dz_ref[:, nqk * C_DK:] = dv_ref[...] * dact[:, nqk * C_DK:]

    return pl.pallas_call(
        body, name="conv_prep_dz", grid=(t // tq,),
        in_specs=[pl.BlockSpec((tq, c), lambda i: (i, 0)),
                  pl.BlockSpec((tq, 2 * C_QKW), lambda i: (i, 0)),
                  pl.BlockSpec((tq, c - 2 * C_QKW), lambda i: (i, 0))],
        out_specs=pl.BlockSpec((tq, c), lambda i: (i, 0)),
        out_shape=jax.ShapeDtypeStruct((t, c), F32),
        compiler_params=_cparams(("parallel",)),
    )(z, dqk, dv)


def _conv_bwd_call(u, dz, w, du_buf, *, tq=512):
    t = u.shape[0]
    nt = t // tq
    c, cb, x_spec, halo_spec = _conv_window_specs(tq, t)
    extra, extra_specs, aliases = _du_operands(du_buf, 5)

    def body(x_ref, xh_ref, dz_ref, dzh_ref, w_ref, *refs):
        dx_ref, dw_ref = refs[len(extra):]
        i = pl.program_id(0)
        xc = jnp.concatenate([jnp.where(i == 0, 0.0, xh_ref[...]), x_ref[...]], axis=0)
        dzv = dz_ref[...]
        dzc = jnp.concatenate([dzv, jnp.where(i == nt - 1, 0.0, dzh_ref[...])], axis=0)
        wv = w_ref[...]
        dx = dzv * wv[3:4]
        for s in range(1, C_CONV):
            dx = dx + dzc[s:s + tq] * wv[3 - s:4 - s]
        dx_ref[...] = dx.astype(BF16)
        row = lax.broadcasted_iota(jnp.int32, (8, c), 0)
        dw = jnp.zeros((8, c), F32)
        for j in range(C_CONV):
            prod = dzv * xc[HALO - 3 + j:HALO - 3 + j + tq]
            col = jnp.sum(jnp.sum(prod.reshape(tq // 8, 8, c), axis=0), axis=0, keepdims=True)
            dw = jnp.where(row == j, col, dw)

        @pl.when(i == 0)
        def _():
            dw_ref[...] = dw

        @pl.when(i > 0)
        def _():
            dw_ref[...] += dw

    blk = pl.BlockSpec((tq, c), lambda i: (i, 0))
    after = pl.BlockSpec((HALO, c), lambda i: (jnp.minimum((i + 1) * (tq // HALO), t // HALO - 1), 0))
    return pl.pallas_call(
        body, name="conv_bwd", grid=(nt,),
        in_specs=[x_spec, halo_spec, blk, after, pl.BlockSpec((C_CONV, c), lambda i: (0, 0))] + extra_specs,
        out_specs=[pl.BlockSpec((tq, c), lambda i: (i, cb)), pl.BlockSpec((8, c), lambda i: (0, 0))],
        out_shape=[_du_shape(t), jax.ShapeDtypeStruct((8, c), F32)],
        input_output_aliases=aliases,
        compiler_params=_cparams(("arbitrary",)),
    )(u, u, dz, dz, w, *extra)


C_VW = C_V_HEADS * C_DK


def _softplus(x):
    return jnp.maximum(x, 0.0) + jnp.log(1.0 + jnp.exp(-jnp.abs(x)))


def _tri_masks():
    r = lax.broadcasted_iota(jnp.int32, (CHUNK, CHUNK), 0)
    c = lax.broadcasted_iota(jnp.int32, (CHUNK, CHUNK), 1)
    return r >= c, r > c


def _split_bf16(a):
    hi = a.astype(BF16)
    return hi, (a - hi.astype(F32)).astype(BF16)


def _dot_hi(a, b, dims=None):
    dims = _NN if dims is None else dims
    ah, al = _split_bf16(a)
    bh, bl = _split_bf16(b)

    def d(x, y):
        return lax.dot_general(x, y, dims, preferred_element_type=F32)

    return d(ah, bh) + (d(ah, bl) + d(al, bh))


def _dot_mask(mask, b, dims=None, mask_left=True):
    dims = _NN if dims is None else dims
    mb = mask.astype(BF16)
    b1 = b.astype(BF16)
    rest = b - b1.astype(F32)
    b2 = rest.astype(BF16)
    b3 = (rest - b2.astype(F32)).astype(BF16)
    out = None
    for p in (b1, b2, b3):
        term = (lax.dot_general(mb, p, dims, preferred_element_type=F32) if mask_left
                else lax.dot_general(p, mb, dims, preferred_element_type=F32))
        out = term if out is None else out + term
    return out


def _unit_lower_inverses(mats):
    r = lax.broadcasted_iota(jnp.int32, (CHUNK, CHUNK), 0)
    c = lax.broadcasted_iota(jnp.int32, (CHUNK, CHUNK), 1)
    eye = (r == c).astype(F32)
    xs = [eye - a for a in mats]
    ps = [_dot_hi(a, a) for a in mats]
    steps = int(math.log2(CHUNK)) - 1
    for s in range(steps):
        xs = [x + _dot_hi(x, p) for x, p in zip(xs, ps)]
        if s < steps - 1:
            ps = [_dot_hi(p, p) for p in ps]
    return xs


def _gate_tiles(cab, alog, dtb):
    pre = cab + dtb
    g = -jnp.exp(alog) * _softplus(pre)
    beta = jax.nn.sigmoid(pltpu.roll(cab, LANES - C_V_HEADS, 1))
    return g, beta, pre


def _chunk_common(kk, qk, gc, gct, beta, h, tri, strict):
    gcol, grow, bcol = gc[:, h:h + 1], gct[h:h + 1, :], beta[:, h:h + 1]
    decay = jnp.where(tri, jnp.exp(jnp.where(tri, gcol - grow, 0.0)), 0.0)
    kkd = jnp.where(strict, kk * decay, 0.0)
    attn = jnp.where(tri, qk * decay, 0.0)
    glast = gc[CHUNK - 1:CHUNK, h:h + 1]
    return gcol, bcol, decay, kkd, attn, glast


def _cab_spec():
    return pl.BlockSpec((CHUNK, LANES), lambda n: (n, COL["c_ab"] // LANES))


def _delta_prep_call(qk, v, u, alog, dtb, gather_src=None):
    t = qk.shape[0]
    nc = t // CHUNK
    scale = C_DK ** -0.5
    riding = gather_src is not None

    def body(q_ref, k_ref, v_ref, cab_ref, alog_ref, dtb_ref, *refs):
        if riding:
            gather_refs = (refs[0], refs[9]) + tuple(refs[10:])
            refs = refs[1:9]

            @pl.when(pl.program_id(0) == 0)
            def _():
                _gather_start(*gather_refs)
        u_ref, w_ref, qd_ref, kd_ref, attn_ref, tmat_ref, gc_ref, beta_ref = refs
        tri, strict = _tri_masks()
        g, beta, _ = _gate_tiles(cab_ref[...], alog_ref[...], dtb_ref[...])
        gc = _dot_mask(tri, g)
        gct = gc.T
        gc_ref[...] = gc
        beta_ref[...] = beta
        mats, rhs = [], []
        for j in range(C_QK_HEADS):
            js = slice(j * C_DK, (j + 1) * C_DK)
            kf, qf = k_ref[:, js], q_ref[:, js] * scale
            kb, qb = kf.astype(BF16), qf.astype(BF16)
            kk = lax.dot_general(kb, kb, _NT, preferred_element_type=F32)
            qk = lax.dot_general(qb, kb, _NT, preferred_element_type=F32)
            for h in (2 * j, 2 * j + 1):
                hs = slice(h * C_DK, (h + 1) * C_DK)
                gcol, bcol, decay, kkd, attn, glast = _chunk_common(kk, qk, gc, gct, beta, h, tri, strict)
                gexp = jnp.exp(gcol)
                mats.append(kkd * bcol)
                rhs.append(jnp.concatenate([v_ref[:, hs] * bcol, kf * (bcol * gexp)], axis=1))
                qd_ref[:, hs] = (qf * gexp).astype(BF16)
                kd_ref[:, hs] = (kf * jnp.exp(glast - gcol)).astype(BF16)
                attn_ref[:, h * CHUNK:(h + 1) * CHUNK] = attn.astype(BF16)
        for h, (tmat, r) in enumerate(zip(_unit_lower_inverses(mats), rhs)):
            hs = slice(h * C_DK, (h + 1) * C_DK)
            uw = _dot_hi(tmat, r)
            u_ref[:, hs] = uw[:, :C_DK]
            w_ref[:, hs] = uw[:, C_DK:]
            tmat_ref[:, h * CHUNK:(h + 1) * CHUNK] = tmat
        if riding:
            @pl.when(pl.program_id(0) == nc - 1)
            def _():
                _gather_finish(*gather_refs)

    def blk(w):
        return pl.BlockSpec((CHUNK, w), lambda n: (n, 0))

    row = pl.BlockSpec((1, LANES), lambda n: (0, 0))
    big = jax.ShapeDtypeStruct((t, C_VW), F32)
    sq = jax.ShapeDtypeStruct((t, C_V_HEADS * CHUNK), F32)
    tile = jax.ShapeDtypeStruct((t, LANES), F32)
    half = jax.ShapeDtypeStruct((t, C_VW), BF16)
    in_specs = [blk(C_QKW), pl.BlockSpec((CHUNK, C_QKW), lambda n: (n, 1)), blk(C_VW), _cab_spec(), row, row]
    out_specs = [blk(C_VW)] * 4 + [blk(C_V_HEADS * CHUNK)] * 2 + [blk(LANES)] * 2
    out_shape = [big, big, half, half, jax.ShapeDtypeStruct(sq.shape, BF16), sq] + [tile] * 2
    args = [qk, qk, v, u, alog, dtb]
    if riding:
        in_specs.append(HBM_SPEC)
        out_specs.append(HBM_SPEC)
        out_shape.append(jax.ShapeDtypeStruct((N_DEV,) + gather_src.shape, gather_src.dtype))
        args.append(gather_src)
    return pl.pallas_call(
        body, name="delta_prep_gather" if riding else "delta_prep", grid=(nc,),
        in_specs=in_specs, out_specs=out_specs, out_shape=out_shape,
        scratch_shapes=list(GATHER_SEMS) if riding else [],
        compiler_params=_cparams(("arbitrary",) if riding else ("parallel",)),
    )(*args)


SCAN_SUB = 2


def _delta_scan_call(u, w, qd, kd, attn, gc):
    t = u.shape[0]
    nc = t // CHUNK
    rows = SCAN_SUB * CHUNK

    def body(u_ref, w_ref, qd_ref, kd_ref, attn_ref, gc_ref, o_ref, vn_ref, st_ref, s_ref):
        @pl.when(pl.program_id(0) == 0)
        def _():
            s_ref[...] = jnp.zeros_like(s_ref)

        hss = [slice(h * C_DK, (h + 1) * C_DK) for h in range(C_V_HEADS)]
        states = [s_ref[hs, :] for hs in hss]
        for c in range(SCAN_SUB):
            rs = slice(c * CHUNK, (c + 1) * CHUNK)
            for hs, s in zip(hss, states):
                st_ref[c, hs, :] = s
            sbs = [s.astype(BF16) for s in states]
            vns = [u_ref[rs, hs] - jnp.dot(w_ref[rs, hs].astype(BF16), sb, preferred_element_type=F32)
                   for hs, sb in zip(hss, sbs)]
            qss = [jnp.dot(qd_ref[rs, hs].astype(BF16), sb, preferred_element_type=F32) for hs, sb in zip(hss, sbs)]
            vnbs = [vn.astype(BF16) for vn in vns]
            for h, hs in enumerate(hss):
                vn_ref[rs, hs] = vnbs[h]
                o_ref[rs, hs] = qss[h] + jnp.dot(attn_ref[rs, h * CHUNK:(h + 1) * CHUNK].astype(BF16), vnbs[h],
                                                 preferred_element_type=F32)
            last = (c + 1) * CHUNK - 1
            states = [states[h] * jnp.exp(gc_ref[last:last + 1, h:h + 1])
                      + lax.dot_general(kd_ref[rs, hs].astype(BF16), vnbs[h], _TN, preferred_element_type=F32)
                      for h, hs in enumerate(hss)]
        for hs, s in zip(hss, states):
            s_ref[hs, :] = s

    def blk(wd):
        return pl.BlockSpec((rows, wd), lambda n: (n, 0))

    big = jax.ShapeDtypeStruct((t, C_VW), F32)
    return pl.pallas_call(
        body, name="delta_scan", grid=(nc // SCAN_SUB,),
        in_specs=[blk(C_VW)] * 4 + [blk(C_V_HEADS * CHUNK), blk(LANES)],
        out_specs=[blk(C_VW), blk(C_VW), pl.BlockSpec((SCAN_SUB, C_VW, C_DK), lambda n: (n, 0, 0))],
        out_shape=[big, jax.ShapeDtypeStruct((t, C_VW), BF16), jax.ShapeDtypeStruct((nc, C_VW, C_DK), F32)],
        scratch_shapes=[pltpu.VMEM((C_VW, C_DK), F32)],
        compiler_params=_cparams(("arbitrary",)),
    )(u, w, qd, kd, attn, gc)


def _delta_scan_bwd_call(do, w, qd, kd, attn, gc, vn, st):
    t = do.shape[0]
    nc = t // CHUNK
    rows = SCAN_SUB * CHUNK
    steps = nc // SCAN_SUB

    def body(do_ref, w_ref, qd_ref, kd_ref, attn_ref, gc_ref, vn_ref, st_ref,
             du_ref, dw_ref, dqd_ref, dkd_ref, dattn_ref, dgl_ref, ds_ref):
        @pl.when(pl.program_id(0) == 0)
        def _():
            ds_ref[...] = jnp.zeros_like(ds_ref)

        tri, _ = _tri_masks()
        row = lax.broadcasted_iota(jnp.int32, (8, LANES), 0)
        lane = lax.broadcasted_iota(jnp.int32, (8, LANES), 1)
        hss = [slice(h * C_DK, (h + 1) * C_DK) for h in range(C_V_HEADS)]
        css = [slice(h * CHUNK, (h + 1) * CHUNK) for h in range(C_V_HEADS)]

        def dg(a, b, dims):
            return lax.dot_general(a, b, dims, preferred_element_type=F32)

        dsps = [ds_ref[hs, :] for hs in hss]
        for c in reversed(range(SCAN_SUB)):
            rs = slice(c * CHUNK, (c + 1) * CHUNK)
            dgl = jnp.zeros((8, LANES), F32)
            ss = [st_ref[c, hs, :] for hs in hss]
            sbs = [s.astype(BF16) for s in ss]
            dspbs = [d.astype(BF16) for d in dsps]
            dobs = [do_ref[rs, hs].astype(BF16) for hs in hss]
            vnbs = [vn_ref[rs, hs].astype(BF16) for hs in hss]
            dvns = [dg(attn_ref[rs, cs].astype(BF16), dob, _TN) + dg(kd_ref[rs, hs].astype(BF16), dspb, _NN)
                    for hs, cs, dob, dspb in zip(hss, css, dobs, dspbs)]
            for h, hs in enumerate(hss):
                dqd_ref[rs, hs] = dg(dobs[h], sbs[h], _NT)
                dkd_ref[rs, hs] = dg(vnbs[h], dspbs[h], _NT)
                dattn_ref[rs, css[h]] = jnp.where(tri, dg(dobs[h], vnbs[h], _NT), 0.0)
            dvnbs = [d.astype(BF16) for d in dvns]
            for h, hs in enumerate(hss):
                du_ref[rs, hs] = dvns[h]
                dw_ref[rs, hs] = -dg(dvnbs[h], sbs[h], _NT)
                tot = jnp.sum(jnp.sum(dsps[h] * ss[h], axis=0, keepdims=True), axis=1, keepdims=True)
                dgl = jnp.where(jnp.logical_and(row == 0, lane == h), tot, dgl)
            dgl_ref[c * 8:(c + 1) * 8, :] = dgl
            last = (c + 1) * CHUNK - 1
            dsps = [dg(qd_ref[rs, hs].astype(BF16), dobs[h], _TN) + jnp.exp(gc_ref[last:last + 1, h:h + 1]) * dsps[h]
                    - dg(w_ref[rs, hs].astype(BF16), dvnbs[h], _TN) for h, hs in enumerate(hss)]
        for hs, d in zip(hss, dsps):
            ds_ref[hs, :] = d

    def blk(wd):
        return pl.BlockSpec((rows, wd), lambda n: (steps - 1 - n, 0))

    big = jax.ShapeDtypeStruct((t, C_VW), F32)
    return pl.pallas_call(
        body, name="delta_scan_bwd", grid=(steps,),
        in_specs=[blk(C_VW)] * 4 + [blk(C_V_HEADS * CHUNK), blk(LANES), blk(C_VW),
                                    pl.BlockSpec((SCAN_SUB, C_VW, C_DK), lambda n: (steps - 1 - n, 0, 0))],
        out_specs=[blk(C_VW)] * 4 + [blk(C_V_HEADS * CHUNK),
                                     pl.BlockSpec((SCAN_SUB * 8, LANES), lambda n: (steps - 1 - n, 0))],
        out_shape=[big] * 4 + [jax.ShapeDtypeStruct((t, C_V_HEADS * CHUNK), F32),
                               jax.ShapeDtypeStruct((nc * 8, LANES), F32)],
        scratch_shapes=[pltpu.VMEM((C_VW, C_DK), F32)],
        compiler_params=_cparams(("arbitrary",)),
    )(do, w, qd, kd, attn, gc, vn, st)


PREP_SUB = 2


def _delta_prep_bwd_call(qk, v, proj, alog, dtb, tmat, u, w, gc, beta, du, dw, dqd, dkd, dattn, dgl, du_buf):
    t = qk.shape[0]
    extra, extra_specs, aliases = _du_operands(du_buf, 17)
    nc = t // CHUNK
    rows = PREP_SUB * CHUNK
    scale = C_DK ** -0.5

    def body(q_ref, k_ref, v_ref, cab_ref, alog_ref, dtb_ref, tmat_ref, u_ref, w_ref, gc_ref, beta_ref,
             du_ref, dw_ref, dqd_ref, dkd_ref, dattn_ref, dgl_ref, *outs):
        dcab_ref, dqk_ref, dv_ref, dpar_ref = outs[len(extra):]
        tri, strict = _tri_masks()
        ones = jnp.ones((CHUNK, LANES), F32)
        lane = lax.broadcasted_iota(jnp.int32, (CHUNK, LANES), 1)
        rowi = lax.broadcasted_iota(jnp.int32, (CHUNK, 1), 0)
        subs = range(PREP_SUB)
        rss = [slice(c * CHUNK, (c + 1) * CHUNK) for c in subs]
        betas = [beta_ref[rs, :] for rs in rss]

        def dot(x, y, dims=_NN):
            return lax.dot_general(x, y, dims, preferred_element_type=F32)

        heads = []
        for c, rs in zip(subs, rss):
            gc = gc_ref[rs, :]
            gct = gc.T
            for j in range(C_QK_HEADS):
                js = slice(j * C_DK, (j + 1) * C_DK)
                kf, qf = k_ref[rs, js], q_ref[rs, js] * scale
                kb, qb = kf.astype(BF16), qf.astype(BF16)
                kk = dot(kb, kb, _NT)
                qk = dot(qb, kb, _NT)
                for h in (2 * j, 2 * j + 1):
                    heads.append((c, rs, h, kf, qf, kb, qb) + _chunk_common(kk, qk, gc, gct, betas[c], h, tri, strict))

        def cols(h):
            return slice(h * C_DK, (h + 1) * C_DK)

        def sq(h):
            return slice(h * CHUNK, (h + 1) * CHUNK)

        dvks = [_dot_hi(tmat_ref[hd[1], sq(hd[2])],
                        jnp.concatenate([du_ref[hd[1], cols(hd[2])], dw_ref[hd[1], cols(hd[2])]], axis=1), _TN)
                for hd in heads]
        das = [-jnp.where(strict, _dot_hi(dvk, jnp.concatenate([u_ref[hd[1], cols(hd[2])], w_ref[hd[1], cols(hd[2])]],
                                                               axis=1), _NT), 0.0)
               for hd, dvk in zip(heads, dvks)]
        pre = []
        for (c, rs, h, kf, qf, kb, qb, gcol, bcol, decay, kkd, attn, glast), da in zip(heads, das):
            dattn_h = dattn_ref[rs, sq(h)]
            pre.append(((da * decay * bcol).astype(BF16), (dattn_h * decay).astype(BF16),
                        da * kkd * bcol + dattn_h * attn))
        mms = [(dot(dkk, hd[5]), dot(dkk, hd[5], _TN), dot(dqk, hd[6], _TN), dot(dqk, hd[5]),
                _dot_mask(ones, e, _TN, mask_left=False))
               for hd, (dkk, dqk, e) in zip(heads, pre)]
        dq_parts, dk_parts = {}, {}
        dgc_tiles = [jnp.zeros((CHUNK, LANES), F32) for _ in subs]
        db_tiles = [jnp.zeros((CHUNK, LANES), F32) for _ in subs]
        for (c, rs, h, kf, qf, kb, qb, gcol, bcol, decay, kkd, attn, glast), dvk, da, (_, _, e), mm in zip(
                heads, dvks, das, pre, mms):
            hs = cols(h)
            gexp = jnp.exp(gcol)
            fdec = jnp.exp(glast - gcol)
            dvb, dkb = dvk[:, :C_DK], dvk[:, C_DK:]
            dgc = jnp.sum(e, axis=1, keepdims=True) - mm[4][:, :1]
            dk_parts[c, h] = mm[0] + mm[1] + mm[2] + dkb * (bcol * gexp) + dkd_ref[rs, hs] * fdec
            dq_parts[c, h] = mm[3] + dqd_ref[rs, hs] * gexp
            dv_ref[rs, hs] = dvb * bcol
            s_kb = jnp.sum(dkb * kf, axis=1, keepdims=True)
            db = (jnp.sum(da * kkd, axis=1, keepdims=True) + jnp.sum(dvb * v_ref[rs, hs], axis=1, keepdims=True)
                  + s_kb * gexp)
            rho = jnp.sum(dkd_ref[rs, hs] * kf, axis=1, keepdims=True) * fdec
            dgc = (dgc + s_kb * bcol * gexp + jnp.sum(dqd_ref[rs, hs] * qf, axis=1, keepdims=True) * gexp - rho)
            last = jnp.sum(rho, axis=0, keepdims=True) + dgl_ref[c * 8:c * 8 + 1, h:h + 1] * jnp.exp(glast)
            dgc = dgc + jnp.where(rowi == CHUNK - 1, last, 0.0)
            dgc_tiles[c] = jnp.where(lane == h, dgc, dgc_tiles[c])
            db_tiles[c] = jnp.where(lane == h, db, db_tiles[c])
        alog = alog_ref[...]
        row8 = lax.broadcasted_iota(jnp.int32, (8, LANES), 0)
        par = jnp.zeros((8, LANES), F32)
        for c, rs in zip(subs, rss):
            for j in range(C_QK_HEADS):
                dqk_ref[rs, j * C_DK:(j + 1) * C_DK] = (dq_parts[c, 2 * j] + dq_parts[c, 2 * j + 1]) * scale
                dqk_ref[rs, C_QKW + j * C_DK:C_QKW + (j + 1) * C_DK] = dk_parts[c, 2 * j] + dk_parts[c, 2 * j + 1]
            dg = _dot_mask(jnp.logical_not(strict), dgc_tiles[c])
            g, _, gate_pre = _gate_tiles(cab_ref[rs, :], alog, dtb_ref[...])
            dca = dg * (-jnp.exp(alog)) * jax.nn.sigmoid(gate_pre)
            beta = betas[c]
            dcab_ref[rs, :LANES] = (dca + pltpu.roll(db_tiles[c] * beta * (1.0 - beta), C_V_HEADS, 1)).astype(BF16)
            dcab_ref[rs, LANES:] = jnp.zeros((CHUNK, D_IN_PAD - COL["c_ab"] - LANES), BF16)
            par = par + jnp.where(row8 == 0, jnp.sum(dg * g, axis=0, keepdims=True),
                                  jnp.where(row8 == 1, jnp.sum(dca, axis=0, keepdims=True), 0.0))

        @pl.when(pl.program_id(0) == 0)
        def _():
            dpar_ref[...] = par

        @pl.when(pl.program_id(0) > 0)
        def _():
            dpar_ref[...] += par

    def blk(wd):
        return pl.BlockSpec((rows, wd), lambda n: (n, 0))

    row = pl.BlockSpec((1, LANES), lambda n: (0, 0))
    sqs = blk(C_V_HEADS * CHUNK)
    tail = D_IN_PAD - COL["c_ab"]
    assert COL["c_ab"] % tail == 0 and nc % PREP_SUB == 0
    return pl.pallas_call(
        body, name="delta_prep_bwd", grid=(nc // PREP_SUB,),
        in_specs=[blk(C_QKW), pl.BlockSpec((rows, C_QKW), lambda n: (n, 1)), blk(C_VW),
                  pl.BlockSpec((rows, LANES), lambda n: (n, COL["c_ab"] // LANES)), row, row, sqs,
                  blk(C_VW), blk(C_VW),
                  blk(LANES), blk(LANES), blk(C_VW), blk(C_VW), blk(C_VW), blk(C_VW), sqs,
                  pl.BlockSpec((PREP_SUB * 8, LANES), lambda n: (n, 0))] + extra_specs,
        out_specs=[pl.BlockSpec((rows, tail), lambda n: (n, COL["c_ab"] // tail)),
                   blk(2 * C_QKW), blk(C_VW), pl.BlockSpec((8, LANES), lambda n: (0, 0))],
        out_shape=[_du_shape(t), jax.ShapeDtypeStruct((t, 2 * C_QKW), F32),
                   jax.ShapeDtypeStruct((t, C_VW), F32), jax.ShapeDtypeStruct((8, LANES), F32)],
        input_output_aliases=aliases,
        compiler_params=_cparams(("arbitrary",)),
    )(qk, qk, v, proj, alog, dtb, tmat, u, w, gc, beta, du, dw, dqd, dkd, dattn, dgl, *extra)


def _z_spec(tq):
    return pl.BlockSpec((tq, C_VW), lambda i: (i, COL["c_z"] // C_VW))


def _gated_norm_fwd_call(o, u, gain, *, tq=256):
    t, w = o.shape

    def body(o_ref, z_ref, g_ref, y_ref):
        act, _ = _silu_parts(z_ref[...])
        gv = g_ref[...]
        for h in range(C_V_HEADS):
            hs = slice(h * C_DK, (h + 1) * C_DK)
            ov = o_ref[:, hs]
            r = lax.rsqrt(jnp.mean(ov * ov, axis=1, keepdims=True) + EPS)
            y_ref[:, hs] = ov * r * gv * act[:, hs]

    blk = pl.BlockSpec((tq, w), lambda i: (i, 0))
    return pl.pallas_call(
        body, name="gated_norm_fwd", grid=(t // tq,),
        in_specs=[blk, _z_spec(tq), pl.BlockSpec((1, C_DK), lambda i: (0, 0))], out_specs=blk,
        out_shape=jax.ShapeDtypeStruct((t, w), F32),
        compiler_params=_cparams(("parallel",)),
    )(o, u, gain)


def _gated_norm_bwd_call(o, u, gain, dy, du_buf, *, tq=256):
    t, w = o.shape
    nt = t // tq
    extra, extra_specs, aliases = _du_operands(du_buf, 4)

    def body(o_ref, z_ref, g_ref, dy_ref, *refs):
        dz_ref, do_ref, dg_ref, acc_ref = refs[len(extra):]
        i = pl.program_id(0)
        act, dact = _silu_parts(z_ref[...])
        gv = g_ref[...]
        part = jnp.zeros((8, C_DK), F32)
        for h in range(C_V_HEADS):
            hs = slice(h * C_DK, (h + 1) * C_DK)
            ov = o_ref[:, hs]
            r = lax.rsqrt(jnp.mean(ov * ov, axis=1, keepdims=True) + EPS)
            xh = ov * r
            dyv = dy_ref[:, hs]
            dn = dyv * act[:, hs]
            dz_ref[:, hs] = (dyv * xh * gv * dact[:, hs]).astype(BF16)
            dxh = dn * gv
            do_ref[:, hs] = r * (dxh - xh * jnp.mean(dxh * xh, axis=1, keepdims=True))
            part = part + jnp.sum((dn * xh).reshape(tq // 8, 8, C_DK), axis=0)

        @pl.when(i == 0)
        def _():
            acc_ref[...] = part

        @pl.when(i > 0)
        def _():
            acc_ref[...] += part

        @pl.when(i == nt - 1)
        def _():
            dg_ref[...] = jnp.sum(acc_ref[...], axis=0, keepdims=True)

    blk = pl.BlockSpec((tq, w), lambda i: (i, 0))
    grow = pl.BlockSpec((1, C_DK), lambda i: (0, 0))
    return pl.pallas_call(
        body, name="gated_norm_bwd", grid=(nt,),
        in_specs=[blk, _z_spec(tq), grow, blk] + extra_specs, out_specs=[_z_spec(tq), blk, grow],
        out_shape=[_du_shape(t), jax.ShapeDtypeStruct((t, w), F32), jax.ShapeDtypeStruct((1, C_DK), F32)],
        scratch_shapes=[pltpu.VMEM((8, C_DK), F32)],
        input_output_aliases=aliases,
        compiler_params=_cparams(("arbitrary",)),
    )(o, u, gain, dy, *extra)


def _gate_specs(tq):
    return [pl.BlockSpec((tq, D_MODEL), lambda i, j=j: (i, j)) for j in range(3)]


def _merge_fwd_call(ps, u, *, tq=256):
    t, w = ps[0].shape

    def body(p0, p1, p2, g0, g1, g2, y_ref):
        y_ref[...] = (jax.nn.sigmoid(g0[...]) * p0[...] + jax.nn.sigmoid(g1[...]) * p1[...]
                      + jax.nn.sigmoid(g2[...]) * p2[...]).astype(BF16)

    blk = pl.BlockSpec((tq, w), lambda i: (i, 0))
    return pl.pallas_call(
        body, name="merge_fwd", grid=(t // tq,), in_specs=[blk] * 3 + _gate_specs(tq), out_specs=blk,
        out_shape=jax.ShapeDtypeStruct((t, w), BF16),
        compiler_params=_cparams(("parallel",)),
    )(*ps, u, u, u)


def _merge_bwd_call(ps, u, dy, *, tq=256):
    t, w = dy.shape

    def body(p0, p1, p2, g0, g1, g2, dy_ref, dg_ref, dp0, dp1, dp2):
        dyv = dy_ref[...]
        for j, (p, g, dp) in enumerate(((p0, g0, dp0), (p1, g1, dp1), (p2, g2, dp2))):
            sig = jax.nn.sigmoid(g[...])
            dp[...] = (dyv * sig).astype(BF16)
            dg_ref[:, j * w:(j + 1) * w] = (dyv * p[...] * sig * (1.0 - sig)).astype(BF16)

    blk = pl.BlockSpec((tq, w), lambda i: (i, 0))
    small = jax.ShapeDtypeStruct((t, w), BF16)
    return pl.pallas_call(
        body, name="merge_bwd", grid=(t // tq,), in_specs=[blk] * 3 + _gate_specs(tq) + [blk],
        out_specs=[pl.BlockSpec((tq, 3 * w), lambda i: (i, 0))] + [blk] * 3,
        out_shape=[_du_shape(t)] + [small] * 3,
        compiler_params=_cparams(("parallel",)),
    )(*ps, u, u, u, dy)


A_PARTS = ((COL["a_q"], 2 * A_W, True), (COL["a_v"], A_W, False))
B_PARTS = ((COL["b_q"], A_W, True), (COL["b_k"], B_KVW, True), (COL["b_v"], B_KVW, False))
BRANCHES = ("w_branch_a", "w_branch_b", "w_branch_c")


def _layer_fwd(x, tabs, p, wb, gather_src=None):
    h = _rms_fwd_call(x, p["norm_mix"], name="rms_mix_fwd", out_dtype=BF16)
    u = _mm(h, wb["w_in"], bias=p["b_in"], tn=IN_TN, name="in_proj_fwd")
    qkv_a = _rope_gather_call(u, tabs, A_PARTS, name="rope_a_fwd")
    a_runs = [_attn_fwd(cfg, qkv_a, None) for cfg in ATTN_A_CFGS]
    os_, lses = tuple(r[0] for r in a_runs), tuple(r[1] for r in a_runs)
    ya = _combine_fwd_call(os_, lses)
    qkv_b = _rope_gather_call(u, tabs, B_PARTS, name="rope_b_fwd")
    yb, _, b_saved = _attn_fwd(ATTN_B_CFG, qkv_b, p["sinks"])
    zc, qk, v = _conv_prep_fwd_call(u, p["conv_w"])
    uu, ww, qd, kd, attn, tmat, gc, beta, *gathered = _delta_prep_call(qk, v, u, p["a_log"], p["dt_bias"], gather_src)
    o, vn, st = _delta_scan_call(uu, ww, qd, kd, attn, gc)
    yc = _gated_norm_fwd_call(o, u, p["c_norm"])
    ys = (ya, yb, yc)
    ps = tuple(_mm(y, wb[n], name="branch_fwd") for y, n in zip(ys, BRANCHES))
    merged = _merge_fwd_call(ps, u)
    x1 = _mm(merged, wb["w_out"], add=x, name="out_proj_fwd")
    h2 = _rms_fwd_call(x1, p["norm_ffn"], name="rms_ffn_fwd", out_dtype=BF16)
    pre, act = _mm(h2, wb["w_ff1"], relu2_out=True, name="ffn_up")
    x2 = _mm(act, wb["w_ff2"], add=x1, name="ffn_down")
    saved = dict(x=x, h=h, u=u, a_saved=[r[2] for r in a_runs], os_=os_, lses=lses, b_saved=b_saved,
                 zc=zc, qk=qk, v=v, delta=(tmat, uu, ww, gc, beta, qd, kd, attn, vn, st), o=o, ys=ys, ps=ps,
                 merged=merged, x1=x1, h2=h2, pre=pre, act=act)
    return x2, saved, (gathered[0] if gathered else None)


def _layer_bwd(s, dx2, tabs, p, wb):
    g = {}
    t = dx2.shape[0]
    dpre = _mm(dx2, wb["w_ff2"], tb=True, mul_drelu2=s["pre"], out_dtype=BF16, name="ffn_dpre")
    g["w_ff2"] = _mm(s["act"], dx2, ta=True, tk=1024, name="ffn_dw2")
    g["w_ff1"] = _mm(s["h2"], dpre, ta=True, tk=1024, name="ffn_dw1")
    dh2 = _mm(dpre, wb["w_ff1"], tb=True, name="ffn_dh")
    dx1, g["norm_ffn"] = _rms_bwd_call(s["x1"], p["norm_ffn"], dh2, add=dx2, name="rms_ffn_bwd")
    dmerged = _mm(dx1, wb["w_out"], tb=True, name="out_proj_da")
    g["w_out"] = _mm(s["merged"], dx1, ta=True, tk=1024, name="out_proj_dw")
    du, *dps = _merge_bwd_call(s["ps"], s["u"], dmerged)
    dys = []
    for y, dp, n in zip(s["ys"], dps, BRANCHES):
        dys.append(_mm(dp, wb[n], tb=True, name="branch_da"))
        g[n] = _mm(y, dp, ta=True, tk=1024, name="branch_dw")
    dya, dyb, dyc = dys
    tmat, uu, ww, gc, beta, qd, kd, attn, vn, st = s["delta"]
    du, do, g["c_norm"] = _gated_norm_bwd_call(s["o"], s["u"], p["c_norm"], dyc, du)
    ddu, ddw, dqd, dkd, dattn, dgl = _delta_scan_bwd_call(do, ww, qd, kd, attn, gc, vn, st)
    du, dqk, dv, dpar = _delta_prep_bwd_call(s["qk"], s["v"], s["u"], p["a_log"], p["dt_bias"], tmat, uu, ww, gc,
                                             beta, ddu, ddw, dqd, dkd, dattn, dgl, du)
    g["a_log"], g["dt_bias"] = dpar[0:1], dpar[1:2]
    dzc = _conv_prep_dz_call(s["zc"], dqk, dv)
    du, dconv = _conv_bwd_call(s["u"], dzc, p["conv_w"], du)
    g["conv_w"] = dconv[:C_CONV]
    no_dlse = jnp.zeros((t, LANES), F32)
    dq, dk, dv_b, dsink = _attn_bwd(ATTN_B_CFG, s["b_saved"], p["sinks"], dyb, no_dlse)
    g["sinks"] = dsink[0, :p["sinks"].shape[0]]
    du = _rope_scatter_call(du, t, [([dq], A_W, True)], COL["b_q"], tabs, name="rope_bq_bwd")
    du = _rope_scatter_call(du, t, [([dk], B_KVW, True), ([dv_b], B_KVW, False)], COL["b_k"], tabs,
                            name="rope_bkv_bwd")
    *dos, dl0, dl1, dl2 = _combine_bwd_call(s["os_"], s["lses"], dya)
    grads_a = [_attn_bwd(cfg, sv, None, do_c, dl)[:3]
               for cfg, sv, do_c, dl in zip(ATTN_A_CFGS, s["a_saved"], dos, (dl0, dl1, dl2))]
    dqs, dks, dvs = zip(*grads_a)
    du = _rope_scatter_call(du, t, [(list(dqs), A_W, True), (list(dks), A_W, True), (list(dvs), A_W, False)],
                            COL["a_q"], tabs, name="rope_a_bwd")
    dh = _mm(du, wb["w_in"], tb=True, tk=IN_TN, name="in_proj_da")
    g["w_in"], g["b_in"] = _mm(s["h"], du, ta=True, b_colsum=True, tn=IN_TN, tk=1024, name="in_proj_dw")
    dx, g["norm_mix"] = _rms_bwd_call(s["x"], p["norm_mix"], dh, add=dx1, name="rms_mix_bwd")
    return dx, g


def _local_step(x, params, first_gathered, payload_of_layer, weights_of_layer, tabs, tgt):
    saves, gathered = [], first_gathered
    for layer in range(DEPTH):
        wb, conv_w = weights_of_layer(layer, gathered)
        p = {n: w[layer] for n, w in params.items() if n != "norm_final"}
        p["conv_w"] = conv_w
        nxt = payload_of_layer(layer + 1) if layer + 1 < DEPTH else None
        x, s, gathered = _layer_fwd(x, tabs, p, wb, nxt)
        saves.append((s, p, wb))
    loss = _loss_fwd_call(x, params["norm_final"], tgt)
    dx, dfinal = _loss_bwd_call(x, params["norm_final"], tgt, jnp.ones((1, 1), F32))
    per_layer = []
    for s, p, wb in reversed(saves):
        dx, g = _layer_bwd(s, dx, tabs, p, wb)
        per_layer.append(g)
    per_layer.reverse()
    grads = {n: jnp.stack([g[n] for g in per_layer]) for n in per_layer[0]}
    grads["norm_final"] = dfinal
    return loss, dx, grads


def _in_cols_to_kernel(w):
    lead = w.shape[:-1]
    parts, pos = [], 0
    for _, start, width, ref_start in IN_LAYOUT:
        if start > pos:
            parts.append(jnp.zeros(lead + (start - pos,), w.dtype))
        parts.append(w[..., ref_start:ref_start + width])
        pos = start + width
    parts.append(jnp.zeros(lead + (D_IN_PAD - pos,), w.dtype))
    return jnp.concatenate(parts, axis=-1)


def _in_cols_to_reference(w):
    by_ref = sorted(IN_LAYOUT, key=lambda e: e[3])
    return jnp.concatenate([w[..., start:start + width] for _, start, width, _ in by_ref], axis=-1)


W_IN_SHARD = 8464 // N_DEV


def _w_in_from_shards(blocks):
    lead = blocks.shape[1:-1]
    parts, pos = [], 0
    for _, start, width, ref_start in IN_LAYOUT:
        if start > pos:
            parts.append(jnp.zeros(lead + (start - pos,), blocks.dtype))
        col = ref_start
        while col < ref_start + width:
            d, l = divmod(col, W_IN_SHARD)
            n = min(W_IN_SHARD - l, ref_start + width - col)
            parts.append(blocks[d, ..., l:l + n])
            col += n
        pos = start + width
    parts.append(jnp.zeros(lead + (D_IN_PAD - pos,), blocks.dtype))
    return jnp.concatenate(parts, axis=-1)


def _w_in_to_shards(g):
    by_ref = sorted(IN_LAYOUT, key=lambda e: e[3])
    blocks = []
    for d in range(N_DEV):
        lo, hi = d * W_IN_SHARD, (d + 1) * W_IN_SHARD
        parts = []
        for _, start, width, ref_start in by_ref:
            a, b = max(lo, ref_start), min(hi, ref_start + width)
            if a < b:
                parts.append(g[..., start + a - ref_start:start + b - ref_start])
        blocks.append(jnp.concatenate(parts, axis=-1))
    return jnp.stack(blocks)


def _pad_lanes(v):
    return jnp.pad(v, ((0, 0), (0, LANES - v.shape[1])))[:, None, :]


BIG = (("w_in", 2), ("conv_w", 2), ("w_branch_a", 2), ("w_branch_b", 2), ("w_branch_c", 1), ("w_out", 1),
       ("w_ff1", 2), ("w_ff2", 1))
SMALL = ("norm_mix", "b_in", "a_log", "dt_bias", "sinks", "c_norm", "norm_ffn", "norm_final")
WEIGHTS = ("norm_mix", "w_in", "b_in", "conv_w", "a_log", "dt_bias", "sinks", "c_norm", "w_branch_a",
           "w_branch_b", "w_branch_c", "w_out", "norm_ffn", "w_ff1", "w_ff2", "norm_final")
MATMUL_WEIGHTS = ("w_in", "w_branch_a", "w_branch_b", "w_branch_c", "w_out", "w_ff1", "w_ff2")
PACK_ROWS = 1024
ROW_ALIGN = 16


def _seg_rows(n):
    return -(-n // (LANES * ROW_ALIGN)) * ROW_ALIGN


def _pack(arrays, lead=0):
    parts = []
    for a in arrays:
        lead_shape = a.shape[:lead]
        n = math.prod(a.shape[lead:])
        rows = _seg_rows(n)
        if rows * LANES != n:
            a = jnp.pad(a.reshape(lead_shape + (n,)), [(0, 0)] * lead + [(0, rows * LANES - n)])
        parts.append(a.reshape(lead_shape + (rows, LANES)))
    total = sum(p.shape[lead] for p in parts)
    padded = -(-total // PACK_ROWS) * PACK_ROWS
    if padded > total:
        parts.append(jnp.zeros(parts[0].shape[:lead] + (padded - total, LANES), parts[0].dtype))
    return jnp.concatenate(parts, axis=lead)


def _unpack(buf, shapes):
    lead = buf.shape[:-2]
    out, pos = [], 0
    for shp in shapes:
        n = math.prod(shp)
        rows = _seg_rows(n)
        seg = buf[..., pos:pos + rows, :]
        if rows * LANES != n:
            seg = seg.reshape(lead + (rows * LANES,))[..., :n]
        out.append(seg.reshape(lead + tuple(shp)))
        pos += rows
    return out


def _shards_to_full(blocks, axis):
    moved = jnp.moveaxis(blocks, 0, axis)
    shp = list(blocks.shape[1:])
    shp[axis] = shp[axis] * N_DEV
    return moved.reshape(shp)


def _full_to_shards(full, axis):
    shp = list(full.shape)
    shp[axis:axis + 1] = [N_DEV, shp[axis] // N_DEV]
    return jnp.moveaxis(full.reshape(shp), axis, 0)


def _my_place():
    return lax.axis_index("x"), lax.axis_index("y"), lax.axis_index("c")


def _slot(x, y, c):
    return 4 * x + 2 * y + c


def _gather_plan(x_ref, out_ref, send_sems, recv_sems, local_sem):
    x, y, c = _my_place()
    me, sibling = (x, y, c), (x, y, 1 - c)
    chips = [(1 - x, y), (x, 1 - y), (1 - x, 1 - y)]

    def copy(k, blk, to, src=None):
        dst = out_ref.at[_slot(*blk)]
        return pltpu.make_async_remote_copy(
            src_ref=dst if src is None else src, dst_ref=dst,
            send_sem=send_sems.at[k], recv_sem=recv_sems.at[k], device_id=to, device_id_type=MESH_ID)

    def own():
        mine = pltpu.make_async_copy(x_ref, out_ref.at[_slot(*me)], local_sem)
        return mine, [copy(0, me, sibling, src=x_ref)] + [copy(1 + j, me, (*chip, c), src=x_ref)
                                                          for j, chip in enumerate(chips)]

    return copy, own, me, sibling, chips, c


def _gather_start(*refs):
    _, own, *_ = _gather_plan(*refs)
    mine, first = own()
    mine.start()
    for cp in first:
        cp.start()


def _gather_finish(*refs):
    copy, own, me, sibling, chips, c = _gather_plan(*refs)
    passed = [copy(4 + j, (*chip, c), sibling) for j, chip in enumerate(chips)]
    for j, chip in enumerate(chips):
        copy(1 + j, (*chip, c), me).wait_recv()
        passed[j].start()
    copy(0, sibling, me).wait_recv()
    for j, chip in enumerate(chips):
        copy(4 + j, (*chip, 1 - c), me).wait_recv()
    mine, first = own()
    for cp in first + passed:
        cp.wait_send()
    mine.wait()


GATHER_SEMS = [pltpu.SemaphoreType.DMA((7,)), pltpu.SemaphoreType.DMA((7,)), pltpu.SemaphoreType.DMA]


def _all_gather(block, *, name):
    rows = block.shape[0]

    def body(x_ref, out_ref, send_sems, recv_sems, local_sem):
        _gather_start(x_ref, out_ref, send_sems, recv_sems, local_sem)
        _gather_finish(x_ref, out_ref, send_sems, recv_sems, local_sem)

    return pl.pallas_call(
        body, name=name,
        out_shape=jax.ShapeDtypeStruct((N_DEV, rows, LANES), block.dtype),
        in_specs=[HBM_SPEC], out_specs=HBM_SPEC,
        scratch_shapes=list(GATHER_SEMS),
    )(block)


N_CHIP = N_DEV // 2


def _swap_with_sibling(blocks, *, name):
    def body(g_ref, out_ref, send_sem, recv_sem):
        x, y, c = _my_place()
        cp = pltpu.make_async_remote_copy(src_ref=g_ref, dst_ref=out_ref, send_sem=send_sem, recv_sem=recv_sem,
                                          device_id=(x, y, 1 - c), device_id_type=MESH_ID)
        cp.start()
        cp.wait_recv()
        cp.wait_send()

    return pl.pallas_call(
        body, name=name,
        out_shape=jax.ShapeDtypeStruct(blocks.shape, blocks.dtype),
        in_specs=[HBM_SPEC], out_specs=HBM_SPEC,
        scratch_shapes=[pltpu.SemaphoreType.DMA, pltpu.SemaphoreType.DMA],
    )(blocks)


def _chip_all_to_all(blocks, *, name):
    def body(g_ref, out_ref, send_sems, recv_sems, local_sem):
        x, y, c = _my_place()
        mine_slot = 2 * x + y
        local = pltpu.make_async_copy(g_ref.at[mine_slot], out_ref.at[mine_slot], local_sem)
        local.start()
        copies = []
        for k in range(1, N_CHIP):
            px, py = x ^ (k >> 1), y ^ (k & 1)
            copies.append(pltpu.make_async_remote_copy(
                src_ref=g_ref.at[2 * px + py], dst_ref=out_ref.at[mine_slot],
                send_sem=send_sems.at[k - 1], recv_sem=recv_sems.at[k - 1],
                device_id=(px, py, c), device_id_type=MESH_ID))
        for cp in copies:
            cp.start()
        for cp in copies:
            cp.wait_recv()
        for cp in copies:
            cp.wait_send()
        local.wait()

    return pl.pallas_call(
        body, name=name,
        out_shape=jax.ShapeDtypeStruct(blocks.shape, blocks.dtype),
        in_specs=[HBM_SPEC], out_specs=HBM_SPEC,
        scratch_shapes=[pltpu.SemaphoreType.DMA((N_CHIP - 1,)), pltpu.SemaphoreType.DMA((N_CHIP - 1,)),
                        pltpu.SemaphoreType.DMA],
    )(blocks)


def _add_bf16_call(a, b, *, name):
    n, rows, _ = a.shape
    tr = min(PACK_ROWS, rows)

    def body(a_ref, b_ref, o_ref):
        o_ref[...] = (a_ref[...].astype(F32) + b_ref[...].astype(F32)).astype(BF16)

    blk = pl.BlockSpec((n, tr, LANES), lambda i: (0, i, 0))
    return pl.pallas_call(
        body, name=name, grid=(rows // tr,), in_specs=[blk, blk], out_specs=blk,
        out_shape=jax.ShapeDtypeStruct(a.shape, BF16),
        compiler_params=_cparams(("parallel",)),
    )(a, b)


def _adamw_call(parts, w, m, v, *, name):
    rows = w.shape[0]
    tr = min(PACK_ROWS, rows)
    assert rows % tr == 0
    n_parts = parts.shape[0]

    def body(p_ref, w_ref, m_ref, v_ref, g_ref, d_ref, nm_ref, nv_ref):
        g = p_ref[0].astype(F32)
        for s in range(1, n_parts):
            g = g + p_ref[s].astype(F32)
        nm = ADAM_B1 * m_ref[...] + (1.0 - ADAM_B1) * g
        nv = ADAM_B2 * v_ref[...] + (1.0 - ADAM_B2) * jnp.square(g)
        m_hat = nm / (1.0 - ADAM_B1 ** ADAM_STEP)
        v_hat = nv / (1.0 - ADAM_B2 ** ADAM_STEP)
        g_ref[...] = g
        nm_ref[...] = nm
        nv_ref[...] = nv
        d_ref[...] = -ADAM_LR * (m_hat / (jnp.sqrt(v_hat) + ADAM_EPS) + ADAM_WD * w_ref[...])

    blk = pl.BlockSpec((tr, LANES), lambda i: (i, 0))
    shape = jax.ShapeDtypeStruct((rows, LANES), F32)
    return pl.pallas_call(
        body, name=name, grid=(rows // tr,),
        in_specs=[pl.BlockSpec((n_parts, tr, LANES), lambda i: (0, i, 0)), blk, blk, blk],
        out_specs=[blk] * 4, out_shape=[shape] * 4,
        compiler_params=_cparams(("parallel",)),
    )(parts, w, m, v)


def _kernel_params(full):
    return {
        "norm_mix": full["norm_mix"][:, None, :],
        "b_in": _in_cols_to_kernel(full["b_in"])[:, None, :],
        "a_log": _pad_lanes(full["a_log"]),
        "dt_bias": _pad_lanes(full["dt_bias"]),
        "sinks": full["sinks"],
        "c_norm": full["c_norm"][:, None, :],
        "norm_ffn": full["norm_ffn"][:, None, :],
        "norm_final": full["norm_final"][None, :],
    }


def _reference_grads(g):
    return {
        "norm_mix": g["norm_mix"][:, 0, :],
        "b_in": _in_cols_to_reference(g["b_in"][:, 0, :]),
        "conv_w": g["conv_w"],
        "a_log": g["a_log"][:, 0, :C_V_HEADS],
        "dt_bias": g["dt_bias"][:, 0, :C_V_HEADS],
        "sinks": g["sinks"],
        "c_norm": g["c_norm"][:, 0, :],
        "w_branch_a": g["w_branch_a"], "w_branch_b": g["w_branch_b"], "w_branch_c": g["w_branch_c"],
        "w_out": g["w_out"],
        "norm_ffn": g["norm_ffn"][:, 0, :],
        "w_ff1": g["w_ff1"], "w_ff2": g["w_ff2"],
        "norm_final": g["norm_final"][0],
    }


def kernel(x, positions, norm_mix, w_in, b_in, conv_w, a_log, dt_bias, sinks, c_norm, w_branch_a, w_branch_b, w_branch_c, w_out, norm_ffn, w_ff1, w_ff2, norm_final, loss_target, m_norm_mix, m_w_in, m_b_in, m_conv_w, m_a_log, m_dt_bias, m_sinks, m_c_norm, m_w_branch_a, m_w_branch_b, m_w_branch_c, m_w_out, m_norm_ffn, m_w_ff1, m_w_ff2, m_norm_final, v_norm_mix, v_w_in, v_b_in, v_conv_w, v_a_log, v_dt_bias, v_sinks, v_c_norm, v_w_branch_a, v_w_branch_b, v_w_branch_c, v_w_out, v_norm_ffn, v_w_ff1, v_w_ff2, v_norm_final):
    env = dict(locals())
    weights = {n: env[n] for n in WEIGHTS}
    moments_m = {n: env["m_" + n] for n in WEIGHTS}
    moments_v = {n: env["v_" + n] for n in WEIGHTS}

    axis_of = {n: axis - 1 for n, axis in BIG}

    def payload_of_layer(layer):
        cw = weights["conv_w"][layer]
        c1 = cw.astype(BF16)
        c2 = (cw - c1.astype(F32)).astype(BF16)
        c3 = (cw - c1.astype(F32) - c2.astype(F32)).astype(BF16)
        return _pack([weights[n][layer].astype(BF16) for n in MATMUL_WEIGHTS] + [c1, c2, c3])

    def weights_of_layer(layer, gathered):
        shapes = [weights[n].shape[1:] for n in MATMUL_WEIGHTS] + [weights["conv_w"].shape[1:]] * 3
        blocks = _unpack(gathered, shapes)
        wb = {n: _w_in_from_shards(blk) if n == "w_in" else _shards_to_full(blk, axis_of[n])
              for n, blk in zip(MATMUL_WEIGHTS, blocks)}
        return wb, _shards_to_full(sum(b.astype(F32) for b in blocks[-3:]), axis_of["conv_w"])

    tabs = rope_tables(positions[0])
    first = _all_gather(payload_of_layer(0), name="gather_weights")
    loss, dx, dparams = _local_step(x[0], _kernel_params({n: weights[n] for n in SMALL}), first,
                                    payload_of_layer, weights_of_layer, tabs, loss_target[0])
    grads = _reference_grads(dparams)
    loss = lax.psum(loss, ("x", "y", "c"))

    core = lax.axis_index("c")

    def by_core(n, axis, which):
        sh = _w_in_to_shards(dparams[n]) if n == "w_in" else _full_to_shards(grads[n], axis)
        sh = sh.reshape((N_CHIP, 2) + sh.shape[1:])
        return lax.dynamic_index_in_dim(sh, which, axis=1, keepdims=False).astype(BF16)

    from_sibling = _swap_with_sibling(_pack([by_core(n, axis, 1 - core) for n, axis in BIG], lead=1),
                                      name="scatter_grads_d2d")
    chip_sum = _add_bf16_call(_pack([by_core(n, axis, core) for n, axis in BIG], lead=1), from_sibling,
                              name="scatter_grads_add")
    big_parts = _chip_all_to_all(chip_sum, name="scatter_grads_ici")
    small_parts = _all_gather(_pack([grads[n] for n in SMALL]), name="gather_small_grads")

    out = {}
    for names, parts in (([n for n, _ in BIG], big_parts), (list(SMALL), small_parts)):
        shapes = [weights[n].shape for n in names]
        packed = [_pack([d[n] for n in names]) for d in (weights, moments_m, moments_v)]
        results = _adamw_call(parts, *packed, name="adamw_" + names[0])
        for kind, buf in zip(("grad", "delta", "new_m", "new_v"), results):
            for n, arr in zip(names, _unpack(buf, shapes)):
                out[kind, n] = arr
    return (loss, dx[None], *[out[kind, n] for kind in ("grad", "delta", "new_m", "new_v") for n in WEIGHTS])
```

```python
import math

import jax
import jax.numpy as jnp
from jax import lax
from jax.experimental import pallas as pl
from jax.experimental.pallas import tpu as pltpu

F32 = jnp.float32
BF16 = jnp.bfloat16

N_DEV = 8
D_MODEL = 1024
DEPTH = 2
HEAD_DIM = 64
ROT_DIM = 16
ROPE_THETA = 500000.0
BLK = 128
NEG_INF = -1e30
EPS = 1e-6
A_CONFIGS = ((128, 1), (512, 4), (2048, 16))
B_GROUP = 4
C_QK_HEADS = 4
C_V_HEADS = 8
C_DK = 128
C_CONV = 4
CHUNK = 64
ADAM_LR = 0.001
ADAM_B1 = 0.9
ADAM_B2 = 0.999
ADAM_EPS = 1e-08
ADAM_WD = 0.01
ADAM_STEP = 10

IN_LAYOUT = (
    ("gate_a", 0, 1024, 5392), ("gate_b", 1024, 1024, 6416), ("gate_c", 2048, 1024, 7440),
    ("a_q", 3072, 512, 0), ("a_k", 3584, 512, 512), ("a_v", 4096, 512, 1024), ("b_q", 4608, 512, 1536),
    ("c_z", 5120, 1024, 4352), ("c_qkv", 6144, 2048, 2304),
    ("b_k", 8192, 128, 2048), ("b_v", 8320, 128, 2176), ("c_ab", 8448, 16, 5376),
)
COL = {name: start for name, start, _, _ in IN_LAYOUT}
D_IN_PAD = 8704
IN_TN = D_IN_PAD // 4
LANES = 128
VMEM_LIMIT = 56 * 1024 * 1024


def _cparams(sem=None):
    return pltpu.CompilerParams(dimension_semantics=sem, vmem_limit_bytes=VMEM_LIMIT)


def _relu2(t):
    return jnp.square(jnp.maximum(t, 0.0))


def _mm(a, b, *, ta=False, tb=False, bias=None, a_fn=None, mul_drelu2=None, add=None,
        out_dtype=F32, relu2_out=False, b_colsum=False, tm=1024, tn=1024, tk=2048, name):
    if ta:
        kdim, m = a.shape
    else:
        m, kdim = a.shape
    n = b.shape[0] if tb else b.shape[1]
    tm, tn, tk = min(tm, m), min(tn, n), min(tk, kdim)
    assert m % tm == 0 and n % tn == 0 and kdim % tk == 0, (a.shape, b.shape, tm, tn, tk)
    nk = kdim // tk
    assert not b_colsum or (m == tm and not tb and nk > 1)
    dims = (((0 if ta else 1,), (1 if tb else 0,)), ((), ()))
    extras = [e for e in (bias, mul_drelu2, add) if e is not None]

    def body(*refs):
        a_ref, b_ref = refs[0], refs[1]
        pos = 2
        bias_ref = pre_ref = add_ref = None
        if bias is not None:
            bias_ref = refs[pos]; pos += 1
        if mul_drelu2 is not None:
            pre_ref = refs[pos]; pos += 1
        if add is not None:
            add_ref = refs[pos]; pos += 1
        o_ref = refs[pos]
        pos += 1
        r_ref = None
        if relu2_out:
            r_ref = refs[pos]; pos += 1
        cs_ref = None
        if b_colsum:
            cs_ref = refs[pos]; pos += 1
        acc_ref = refs[pos] if nk > 1 else None
        cs_acc = refs[pos + 1] if b_colsum else None

        av = a_ref[...]
        if a_fn is not None:
            av = a_fn(av)
        bv = b_ref[...]
        part = lax.dot_general(av.astype(BF16), bv.astype(BF16), dims,
                               preferred_element_type=F32)
        if b_colsum:
            cs_part = jnp.sum(bv.astype(F32).reshape(tk // 8, 8, tn), axis=0)

        def finish(acc):
            if bias_ref is not None:
                acc = acc + bias_ref[...]
            if pre_ref is not None:
                acc = acc * (2.0 * jnp.maximum(pre_ref[...], 0.0))
            if add_ref is not None:
                acc = acc + add_ref[...]
            o_ref[...] = acc.astype(out_dtype)
            if r_ref is not None:
                r_ref[...] = _relu2(acc).astype(BF16)

        if nk == 1:
            finish(part)
        else:
            k = pl.program_id(2)

            @pl.when(k == 0)
            def _():
                acc_ref[...] = part
                if b_colsum:
                    cs_acc[...] = cs_part

            @pl.when(k > 0)
            def _():
                acc_ref[...] += part
                if b_colsum:
                    cs_acc[...] += cs_part

            @pl.when(k == nk - 1)
            def _():
                finish(acc_ref[...])
                if b_colsum:
                    cs_ref[...] = jnp.sum(cs_acc[...], axis=0, keepdims=True)

    a_spec = (pl.BlockSpec((tk, tm), lambda i, j, k: (k, i)) if ta
              else pl.BlockSpec((tm, tk), lambda i, j, k: (i, k)))
    b_spec = (pl.BlockSpec((tn, tk), lambda i, j, k: (j, k)) if tb
              else pl.BlockSpec((tk, tn), lambda i, j, k: (k, j)))
    in_specs = [a_spec, b_spec]
    if bias is not None:
        in_specs.append(pl.BlockSpec((1, tn), lambda i, j, k: (0, j)))
    for _ in extras[(1 if bias is not None else 0):]:
        in_specs.append(pl.BlockSpec((tm, tn), lambda i, j, k: (i, j)))
    o_spec = pl.BlockSpec((tm, tn), lambda i, j, k: (i, j))
    out_specs, out_shape = [o_spec], [jax.ShapeDtypeStruct((m, n), out_dtype)]
    scratch = [pltpu.VMEM((tm, tn), F32)] if nk > 1 else []
    if relu2_out:
        out_specs.append(o_spec)
        out_shape.append(jax.ShapeDtypeStruct((m, n), BF16))
    if b_colsum:
        out_specs.append(pl.BlockSpec((1, tn), lambda i, j, k: (0, j)))
        out_shape.append(jax.ShapeDtypeStruct((1, n), F32))
        scratch.append(pltpu.VMEM((8, tn), F32))
    single = len(out_specs) == 1
    return pl.pallas_call(
        body, name=name,
        grid=(m // tm, n // tn, nk),
        in_specs=in_specs,
        out_specs=out_specs[0] if single else out_specs,
        out_shape=out_shape[0] if single else out_shape,
        scratch_shapes=scratch,
        compiler_params=_cparams(("parallel", "parallel", "arbitrary")),
    )(a, b, *extras)


def _rms_fwd_call(x, g, *, name, out_dtype=F32, tq=512):
    t, d = x.shape

    def body(x_ref, g_ref, y_ref):
        xv = x_ref[...]
        r = lax.rsqrt(jnp.mean(xv * xv, axis=-1, keepdims=True) + EPS)
        y_ref[...] = (xv * r * g_ref[...]).astype(out_dtype)

    return pl.pallas_call(
        body, name=name, grid=(t // tq,),
        in_specs=[pl.BlockSpec((tq, d), lambda i: (i, 0)), pl.BlockSpec((1, d), lambda i: (0, 0))],
        out_specs=pl.BlockSpec((tq, d), lambda i: (i, 0)),
        out_shape=jax.ShapeDtypeStruct((t, d), out_dtype),
        compiler_params=_cparams(("parallel",)),
    )(x, g)


def _rms_bwd_call(x, g, dy, *, name, add=None, tq=512):
    t, d = x.shape
    nt = t // tq

    def body(*refs):
        if add is None:
            x_ref, g_ref, dy_ref, dx_ref, dg_ref, acc_ref = refs
        else:
            x_ref, g_ref, dy_ref, add_ref, dx_ref, dg_ref, acc_ref = refs
        i = pl.program_id(0)
        xv = x_ref[...]
        r = lax.rsqrt(jnp.mean(xv * xv, axis=-1, keepdims=True) + EPS)
        xh = xv * r
        dyv = dy_ref[...]
        dxh = dyv * g_ref[...]
        dx = r * (dxh - xh * jnp.mean(dxh * xh, axis=-1, keepdims=True))
        dx_ref[...] = dx if add is None else dx + add_ref[...]
        part = jnp.sum((dyv * xh).reshape(tq // 8, 8, d), axis=0)

        @pl.when(i == 0)
        def _():
            acc_ref[...] = part

        @pl.when(i > 0)
        def _():
            acc_ref[...] += part

        @pl.when(i == nt - 1)
        def _():
            dg_ref[...] = jnp.sum(acc_ref[...], axis=0, keepdims=True)

    blk = pl.BlockSpec((tq, d), lambda i: (i, 0))
    row = pl.BlockSpec((1, d), lambda i: (0, 0))
    extra = [] if add is None else [add]
    return pl.pallas_call(
        body, name=name, grid=(nt,),
        in_specs=[blk, row, blk] + [blk] * len(extra),
        out_specs=[blk, row],
        out_shape=[jax.ShapeDtypeStruct((t, d), F32), jax.ShapeDtypeStruct((1, d), F32)],
        scratch_shapes=[pltpu.VMEM((8, d), F32)],
        compiler_params=_cparams(("arbitrary",)),
    )(x, g, dy, *extra)


def _loss_fwd_call(x, g, tgt, *, tq=512):
    t, d = x.shape
    nt = t // tq

    def body(x_ref, g_ref, t_ref, o_ref, acc_ref):
        i = pl.program_id(0)
        xv = x_ref[...]
        r = lax.rsqrt(jnp.mean(xv * xv, axis=-1, keepdims=True) + EPS)
        err = xv * r * g_ref[...] - t_ref[...]
        part = jnp.sum((err * err).reshape(tq // 8, 8, d), axis=0)

        @pl.when(i == 0)
        def _():
            acc_ref[...] = part

        @pl.when(i > 0)
        def _():
            acc_ref[...] += part

        @pl.when(i == nt - 1)
        def _():
            tot = jnp.sum(jnp.sum(acc_ref[...], axis=0, keepdims=True), axis=1, keepdims=True)
            o_ref[...] = jnp.broadcast_to(tot * (0.5 / d), (8, LANES))

    out = pl.pallas_call(
        body, name="loss_fwd", grid=(nt,),
        in_specs=[pl.BlockSpec((tq, d), lambda i: (i, 0)), pl.BlockSpec((1, d), lambda i: (0, 0)),
                  pl.BlockSpec((tq, d), lambda i: (i, 0))],
        out_specs=pl.BlockSpec((8, LANES), lambda i: (0, 0)),
        out_shape=jax.ShapeDtypeStruct((8, LANES), F32),
        scratch_shapes=[pltpu.VMEM((8, d), F32)],
        compiler_params=_cparams(("arbitrary",)),
    )(x, g, tgt)
    return out[0, 0]


def _loss_bwd_call(x, g, tgt, scale, *, tq=512):
    t, d = x.shape
    nt = t // tq

    def body(x_ref, g_ref, t_ref, s_ref, dx_ref, dg_ref, acc_ref):
        i = pl.program_id(0)
        xv = x_ref[...]
        r = lax.rsqrt(jnp.mean(xv * xv, axis=-1, keepdims=True) + EPS)
        xh = xv * r
        gv = g_ref[...]
        dyv = (xh * gv - t_ref[...]) * (s_ref[...] * (1.0 / d))
        dxh = dyv * gv
        dx_ref[...] = r * (dxh - xh * jnp.mean(dxh * xh, axis=-1, keepdims=True))
        part = jnp.sum((dyv * xh).reshape(tq // 8, 8, d), axis=0)

        @pl.when(i == 0)
        def _():
            acc_ref[...] = part

        @pl.when(i > 0)
        def _():
            acc_ref[...] += part

        @pl.when(i == nt - 1)
        def _():
            dg_ref[...] = jnp.sum(acc_ref[...], axis=0, keepdims=True)

    return pl.pallas_call(
        body, name="loss_bwd", grid=(nt,),
        in_specs=[pl.BlockSpec((tq, d), lambda i: (i, 0)), pl.BlockSpec((1, d), lambda i: (0, 0)),
                  pl.BlockSpec((tq, d), lambda i: (i, 0)), pl.BlockSpec((1, 1), lambda i: (0, 0))],
        out_specs=[pl.BlockSpec((tq, d), lambda i: (i, 0)), pl.BlockSpec((1, d), lambda i: (0, 0))],
        out_shape=[jax.ShapeDtypeStruct((t, d), F32), jax.ShapeDtypeStruct((1, d), F32)],
        scratch_shapes=[pltpu.VMEM((8, d), F32)],
        compiler_params=_cparams(("arbitrary",)),
    )(x, g, tgt, scale)


MESH_ID = pl.DeviceIdType.MESH
HBM_SPEC = pl.BlockSpec(memory_space=pl.ANY)


def rope_tables(positions):
    inv_freq = jnp.power(ROPE_THETA, -jnp.arange(0, ROT_DIM, 2, dtype=F32) / ROT_DIM)
    ang = positions.astype(F32)[:, None] * inv_freq
    cos, sin = jnp.cos(ang), jnp.sin(ang)
    t = positions.shape[0]
    one = jnp.ones((t, HEAD_DIM - ROT_DIM), F32)
    zero8 = jnp.zeros((t, ROT_DIM // 2), F32)
    zero = jnp.zeros((t, HEAD_DIM - ROT_DIM), F32)
    a = jnp.concatenate([cos, cos, one], axis=1)
    b = jnp.concatenate([zero8, sin, zero], axis=1)
    c = jnp.concatenate([-sin, zero8, zero], axis=1)
    return tuple(jnp.concatenate([m, m], axis=1) for m in (a, b, c))


def _rope_chunk(xs, a, b, c, transpose):
    half = ROT_DIM // 2
    if transpose:
        return xs * a + pltpu.roll(xs * b, LANES - half, 1) + pltpu.roll(xs * c, half, 1)
    return xs * a + pltpu.roll(xs, half, 1) * b + pltpu.roll(xs, LANES - half, 1) * c


def _rope_gather_call(u, tabs, parts, *, name, tq=512):
    t = u.shape[0]
    total = sum(w for _, w, _, _ in parts)
    assert all(start % w == 0 for start, w, _, _ in parts)

    def body(a_ref, b_ref, c_ref, *refs):
        o_ref = refs[-1]
        a, b, c = a_ref[...], b_ref[...], c_ref[...]
        off = 0
        for x_ref, (_, w, roped, scale) in zip(refs[:-1], parts):
            for j in range(w // LANES):
                xs = x_ref[:, j * LANES:(j + 1) * LANES]
                val = _rope_chunk(xs, a, b, c, False) if roped else xs
                o_ref[:, off + j * LANES:off + (j + 1) * LANES] = (val * scale if scale != 1.0 else val).astype(BF16)
            off += w

    tab_spec = pl.BlockSpec((tq, LANES), lambda i: (i, 0))
    return pl.pallas_call(
        body, name=name, grid=(t // tq,),
        in_specs=[tab_spec] * 3 + [pl.BlockSpec((tq, w), lambda i, cb=start // w: (i, cb)) for start, w, _, _ in parts],
        out_specs=pl.BlockSpec((tq, total), lambda i: (i, 0)),
        out_shape=jax.ShapeDtypeStruct((t, total), BF16),
        compiler_params=_cparams(("parallel",)),
    )(*tabs, *([u] * len(parts)))


def _du_operands(du_buf, n_inputs):
    if du_buf is None:
        return [], [], {}
    return [du_buf], [HBM_SPEC], {n_inputs: 0}


def _du_shape(t):
    return jax.ShapeDtypeStruct((t, D_IN_PAD), BF16)


def _rope_scatter_call(du_buf, t, pieces, col, tabs, *, name, tq=512):
    total = sum(w for _, w, _, _ in pieces)
    assert col % total == 0
    arrays = [a for arrs, _, _, _ in pieces for a in arrs]
    extra, extra_specs, aliases = _du_operands(du_buf, 3 + len(arrays))

    def body(a_ref, b_ref, c_ref, *refs):
        o_ref = refs[len(arrays) + len(extra)]
        a, b, c = a_ref[...], b_ref[...], c_ref[...]
        k = off = 0
        for arrs, w, roped, scale in pieces:
            mine = refs[k:k + len(arrs)]
            k += len(arrs)
            for j in range(w // LANES):
                cs = slice(j * LANES, (j + 1) * LANES)
                xs = mine[0][:, cs].astype(F32)
                for r in mine[1:]:
                    xs = xs + r[:, cs].astype(F32)
                if scale != 1.0:
                    xs = xs * scale
                val = _rope_chunk(xs, a, b, c, True) if roped else xs
                o_ref[:, off + j * LANES:off + (j + 1) * LANES] = val.astype(BF16)
            off += w

    tab_spec = pl.BlockSpec((tq, LANES), lambda i: (i, 0))
    in_specs = [tab_spec] * 3 + [pl.BlockSpec((tq, w), lambda i: (i, 0)) for arrs, w, _, _ in pieces for _ in arrs]
    return pl.pallas_call(
        body, name=name, grid=(t // tq,),
        in_specs=in_specs + extra_specs,
        out_specs=pl.BlockSpec((tq, total), lambda i: (i, col // total)),
        out_shape=_du_shape(t), input_output_aliases=aliases,
        compiler_params=_cparams(("parallel",)),
    )(*tabs, *arrays, *extra)


def _band_masks(first_block, max_dist):
    qi = lax.broadcasted_iota(jnp.int32, (BLK, BLK), 0)
    kj = lax.broadcasted_iota(jnp.int32, (BLK, BLK), 1)
    valid_prev = jnp.logical_and(kj >= qi + (BLK - max_dist), jnp.logical_not(first_block))
    valid_cur = kj <= qi
    return valid_prev, valid_cur


_NN = (((1,), (0,)), ((), ()))
_NT = (((1,), (1,)), ((), ()))
_TN = (((0,), (0,)), ((), ()))


HEAD_STAGE = 8


def _attn_row_maps(nb):
    def cur(i):
        return jnp.minimum(i, nb - 1)

    def prev(i):
        return jnp.maximum(jnp.minimum(i, nb - 1) - 1, 0)

    return cur, prev


def _dil_view(a, dil):
    t, w = a.shape
    return a.reshape(t // dil, dil * w)


def _dil_spec(w, dil, rows, seg=None, off=0):
    seg = w if seg is None else seg
    assert off % w == 0 and (dil == 1 or seg % w == 0)
    return pl.BlockSpec((BLK, w), lambda r, i: (rows(i), (r * seg + off) // w))


def _dil_shape(l, dil, w, dtype=F32):
    return jax.ShapeDtypeStruct((l, dil * w), dtype)


def _attn_fwd_call(qkv2, sink, *, dil, group, max_dist, seg, offs, qw, kw, name):
    l = qkv2.shape[0]
    nh = qw // HEAD_DIM
    nb = l // BLK
    use_sink = sink is not None

    def body(*refs):
        if use_sink:
            sink_ref, refs = refs[0], refs[1:]
        q_ref, kp_ref, kc_ref, vp_ref, vc_ref, o_ref, lse_ref = refs
        valid_prev, valid_cur = _band_masks(pl.program_id(1) == 0, max_dist)
        lane = lax.broadcasted_iota(jnp.int32, (BLK, LANES), 1)
        lse_tile = jnp.zeros((BLK, LANES), F32)

        def dot(a, b, dims=_NN):
            return lax.dot_general(a, b, dims, preferred_element_type=F32)

        for g0 in range(0, nh, HEAD_STAGE):
            heads = list(range(g0, min(g0 + HEAD_STAGE, nh)))
            kv = {}
            for kh in sorted({h // group for h in heads}):
                ks = slice(kh * HEAD_DIM, (kh + 1) * HEAD_DIM)
                kv[kh] = tuple(ref[:, ks].astype(BF16) for ref in (kp_ref, kc_ref, vp_ref, vc_ref))
            qs = [q_ref[:, h * HEAD_DIM:(h + 1) * HEAD_DIM].astype(BF16) for h in heads]
            sps = [jnp.where(valid_prev, dot(qh, kv[h // group][0], _NT), NEG_INF) for h, qh in zip(heads, qs)]
            scs = [jnp.where(valid_cur, dot(qh, kv[h // group][1], _NT), NEG_INF) for h, qh in zip(heads, qs)]
            ms = [jnp.maximum(jnp.max(sp, axis=1, keepdims=True), jnp.max(sc, axis=1, keepdims=True))
                  for sp, sc in zip(sps, scs)]
            if use_sink:
                ms = [jnp.maximum(m, sink_ref[h]) for h, m in zip(heads, ms)]
            pps = [jnp.exp(sp - m) for sp, m in zip(sps, ms)]
            pcs = [jnp.exp(sc - m) for sc, m in zip(scs, ms)]
            dens = [jnp.sum(pp, axis=1, keepdims=True) + jnp.sum(pc, axis=1, keepdims=True)
                    for pp, pc in zip(pps, pcs)]
            if use_sink:
                dens = [den + jnp.exp(sink_ref[h] - m) for h, den, m in zip(heads, dens, ms)]
            outs = [dot(pp.astype(BF16), kv[h // group][2]) + dot(pc.astype(BF16), kv[h // group][3])
                    for h, pp, pc in zip(heads, pps, pcs)]
            for h, o, den, m in zip(heads, outs, dens, ms):
                o_ref[:, h * HEAD_DIM:(h + 1) * HEAD_DIM] = o / den
                lse_tile = jnp.where(lane == h, m + jnp.log(den), lse_tile)
        lse_ref[...] = lse_tile

    cur, prev = _attn_row_maps(nb)
    o_spec, lse_spec = _dil_spec(qw, dil, cur), _dil_spec(LANES, dil, cur)
    in_specs = [_dil_spec(qw, dil, cur, seg, offs[0]),
                _dil_spec(kw, dil, prev, seg, offs[1]), _dil_spec(kw, dil, cur, seg, offs[1]),
                _dil_spec(kw, dil, prev, seg, offs[2]), _dil_spec(kw, dil, cur, seg, offs[2])]
    args = [qkv2] * 5
    if use_sink:
        in_specs = [pl.BlockSpec(memory_space=pltpu.SMEM)] + in_specs
        args = [sink] + args
    return pl.pallas_call(
        body, name=name, grid=(dil, nb),
        in_specs=in_specs,
        out_specs=[o_spec, lse_spec],
        out_shape=[_dil_shape(l, dil, qw), _dil_shape(l, dil, LANES)],
        compiler_params=_cparams(("parallel", "parallel")),
    )(*args)


def _attn_bwd_call(qkv2, sink, o2, lse2, do2, dlse2, *, dil, group, max_dist, seg, offs, qw, kw, name):
    l = qkv2.shape[0]
    nh = qw // HEAD_DIM
    nb = l // BLK
    use_sink = sink is not None

    def body(*refs):
        if use_sink:
            sink_ref, refs = refs[0], refs[1:]
        (q_ref, kp_ref, kc_ref, vp_ref, vc_ref, o_ref, lse_ref, do_ref, dlse_ref,
         dq_ref, dk_ref, dv_ref, dsink_ref, ck_ref, cv_ref) = refs
        step = pl.program_id(1)

        @pl.when(jnp.logical_and(pl.program_id(0) == 0, step == 0))
        def _():
            dsink_ref[...] = jnp.zeros_like(dsink_ref)

        @pl.when(step == 0)
        def _():
            ck_ref[...] = jnp.zeros_like(ck_ref)
            cv_ref[...] = jnp.zeros_like(cv_ref)

        def dot(a, b, dims=_NN):
            return lax.dot_general(a, b, dims, preferred_element_type=F32)

        @pl.when(step < nb)
        def _():
            valid_prev, valid_cur = _band_masks(step == 0, max_dist)
            row = lax.broadcasted_iota(jnp.int32, (8, LANES), 0)
            lanes8 = lax.broadcasted_iota(jnp.int32, (8, LANES), 1)
            ds_tile = jnp.zeros((8, LANES), F32)
            for g0 in range(0, nh, HEAD_STAGE):
                heads = list(range(g0, min(g0 + HEAD_STAGE, nh)))
                hss = [slice(h * HEAD_DIM, (h + 1) * HEAD_DIM) for h in heads]
                kv = {}
                for kh in sorted({h // group for h in heads}):
                    ks = slice(kh * HEAD_DIM, (kh + 1) * HEAD_DIM)
                    kv[kh] = tuple(ref[:, ks].astype(BF16) for ref in (kp_ref, kc_ref, vp_ref, vc_ref))
                qs = [q_ref[:, hs].astype(BF16) for hs in hss]
                dos = [do_ref[:, hs] for hs in hss]
                dobs = [d.astype(BF16) for d in dos]
                lses = [lse_ref[:, h:h + 1] for h in heads]
                sps = [dot(qh, kv[h // group][0], _NT) for h, qh in zip(heads, qs)]
                scs = [dot(qh, kv[h // group][1], _NT) for h, qh in zip(heads, qs)]
                dpps = [dot(dob, kv[h // group][2], _NT) for h, dob in zip(heads, dobs)]
                dpcs = [dot(dob, kv[h // group][3], _NT) for h, dob in zip(heads, dobs)]
                pps = [jnp.where(valid_prev, jnp.exp(jnp.where(valid_prev, sp, NEG_INF) - ls), 0.0)
                       for sp, ls in zip(sps, lses)]
                pcs = [jnp.where(valid_cur, jnp.exp(jnp.where(valid_cur, sc, NEG_INF) - ls), 0.0)
                       for sc, ls in zip(scs, lses)]
                deltas = [jnp.sum(d * o_ref[:, hs], axis=1, keepdims=True) for d, hs in zip(dos, hss)]
                corrs = [dlse_ref[:, h:h + 1] - dl for h, dl in zip(heads, deltas)]
                dsps = [(pp * (dp + c)).astype(BF16) for pp, dp, c in zip(pps, dpps, corrs)]
                dscs = [(pc * (dp + c)).astype(BF16) for pc, dp, c in zip(pcs, dpcs, corrs)]
                for h, hs, dsp, dsc in zip(heads, hss, dsps, dscs):
                    dq = dot(dsp, kv[h // group][0]) + dot(dsc, kv[h // group][1])
                    dq_ref[:, hs] = dq.astype(BF16)
                parts = [(dot(dsc, qh, _TN), dot(dsp, qh, _TN),
                          dot(pc.astype(BF16), dob, _TN), dot(pp.astype(BF16), dob, _TN))
                         for dsc, dsp, qh, pc, pp, dob in zip(dscs, dsps, qs, pcs, pps, dobs)]
                for kh in kv:
                    ks = slice(kh * HEAD_DIM, (kh + 1) * HEAD_DIM)
                    mine = [p for h, p in zip(heads, parts) if h // group == kh]
                    dkc, dkp, dvc, dvp = (sum(p[j] for p in mine[1:]) + mine[0][j] for j in range(4))
                    dk_ref[:, ks] = (ck_ref[:, ks] + dkp).astype(BF16)
                    dv_ref[:, ks] = (cv_ref[:, ks] + dvp).astype(BF16)
                    ck_ref[:, ks] = dkc
                    cv_ref[:, ks] = dvc
                if use_sink:
                    for h, ls, dl in zip(heads, lses, deltas):
                        val = -jnp.sum(jnp.exp(sink_ref[h] - ls) * dl, axis=0, keepdims=True)
                        ds_tile = jnp.where(jnp.logical_and(row == 0, lanes8 == h), val, ds_tile)
            if use_sink:
                dsink_ref[...] += ds_tile

        @pl.when(step == nb)
        def _():
            dk_ref[...] = ck_ref[...].astype(BF16)
            dv_ref[...] = cv_ref[...].astype(BF16)

    cur, prev = _attn_row_maps(nb)
    q_spec, lse_spec = _dil_spec(qw, dil, cur), _dil_spec(LANES, dil, cur)
    lag_spec = _dil_spec(kw, dil, lambda i: jnp.maximum(i - 1, 0))
    in_specs = [_dil_spec(qw, dil, cur, seg, offs[0]),
                _dil_spec(kw, dil, prev, seg, offs[1]), _dil_spec(kw, dil, cur, seg, offs[1]),
                _dil_spec(kw, dil, prev, seg, offs[2]), _dil_spec(kw, dil, cur, seg, offs[2]),
                q_spec, lse_spec, q_spec, lse_spec]
    args = [qkv2] * 5 + [o2, lse2, do2, dlse2]
    if use_sink:
        in_specs = [pl.BlockSpec(memory_space=pltpu.SMEM)] + in_specs
        args = [sink] + args
    kv_shape = _dil_shape(l, dil, kw, BF16)
    return pl.pallas_call(
        body, name=name, grid=(dil, nb + 1),
        in_specs=in_specs,
        out_specs=[q_spec, lag_spec, lag_spec, pl.BlockSpec((8, LANES), lambda r, i: (0, 0))],
        out_shape=[_dil_shape(l, dil, qw, BF16), kv_shape, kv_shape,
                   jax.ShapeDtypeStruct((8, LANES), F32)],
        scratch_shapes=[pltpu.VMEM((BLK, kw), F32), pltpu.VMEM((BLK, kw), F32)],
        compiler_params=_cparams(("arbitrary", "arbitrary")),
    )(*args)


def _attn_config(tag, dil, group, max_dist, seg, offs, qw, kw):
    return dict(name=tag, dil=dil, group=group, max_dist=max_dist, seg=seg, offs=offs, qw=qw, kw=kw)


A_W = 8 * HEAD_DIM
ATTN_A_CFGS = tuple(_attn_config("attn_a%d" % dil, dil, 1, window // dil, 3 * A_W, (0, A_W, 2 * A_W), A_W, A_W)
                    for window, dil in A_CONFIGS)
B_KVW = 2 * HEAD_DIM
ATTN_B_CFG = _attn_config("attn_b", 1, B_GROUP, BLK - 1, A_W + 2 * B_KVW, (0, A_W, A_W + B_KVW), A_W, B_KVW)


def _attn_fwd(cfg, qkv, sink):
    t = qkv.shape[0]
    kw = {k: v for k, v in cfg.items() if k != "name"}
    qkv2 = _dil_view(qkv, cfg["dil"])
    o2, lse2 = _attn_fwd_call(qkv2, sink, name=cfg["name"] + "_fwd", **kw)
    return o2.reshape(t, cfg["qw"]), lse2.reshape(t, LANES), (qkv2, o2, lse2)


def _attn_bwd(cfg, saved, sink, do, dlse):
    qkv2, o2, lse2 = saved
    t = do.shape[0]
    kw = {k: v for k, v in cfg.items() if k != "name"}
    dq2, dk2, dv2, dsink = _attn_bwd_call(qkv2, sink, o2, lse2, _dil_view(do, cfg["dil"]),
                                          _dil_view(dlse, cfg["dil"]), name=cfg["name"] + "_bwd", **kw)
    return dq2.reshape(t, cfg["qw"]), dk2.reshape(t, cfg["kw"]), dv2.reshape(t, cfg["kw"]), dsink


def _head_expand():
    r = lax.broadcasted_iota(jnp.int32, (LANES, 8 * HEAD_DIM), 0)
    c = lax.broadcasted_iota(jnp.int32, (LANES, 8 * HEAD_DIM), 1)
    return (c // HEAD_DIM == r).astype(F32)


def _combine_weights(l0, l1, l2):
    m = jnp.maximum(jnp.maximum(l0, l1), l2)
    e0, e1, e2 = jnp.exp(l0 - m), jnp.exp(l1 - m), jnp.exp(l2 - m)
    inv = 1.0 / (e0 + e1 + e2)
    return e0 * inv, e1 * inv, e2 * inv


def _combine_fwd_call(os_, lses, *, tq=256):
    t, w = os_[0].shape

    def body(o0, o1, o2, l0, l1, l2, y_ref):
        ws = _combine_weights(l0[...], l1[...], l2[...])
        e = _head_expand()
        y = jnp.zeros((tq, w), F32)
        for o_ref, wt in zip((o0, o1, o2), ws):
            y = y + _dot_mask(e, wt, mask_left=False) * o_ref[...]
        y_ref[...] = y

    o_spec = pl.BlockSpec((tq, w), lambda i: (i, 0))
    l_spec = pl.BlockSpec((tq, LANES), lambda i: (i, 0))
    return pl.pallas_call(
        body, name="combine_fwd", grid=(t // tq,),
        in_specs=[o_spec] * 3 + [l_spec] * 3, out_specs=o_spec,
        out_shape=jax.ShapeDtypeStruct((t, w), F32),
        compiler_params=_cparams(("parallel",)),
    )(*os_, *lses)


def _combine_bwd_call(os_, lses, dy, *, tq=256):
    t, w = dy.shape

    def body(o0, o1, o2, l0, l1, l2, dy_ref, do0, do1, do2, dl0, dl1, dl2):
        ws = _combine_weights(l0[...], l1[...], l2[...])
        e = _head_expand()
        dyv = dy_ref[...]
        dws = []
        for o_ref, do_ref, wt in zip((o0, o1, o2), (do0, do1, do2), ws):
            do_ref[...] = _dot_mask(e, wt, mask_left=False) * dyv
            dws.append(_dot_mask(e, dyv * o_ref[...], _NT, mask_left=False))
        mean = ws[0] * dws[0] + ws[1] * dws[1] + ws[2] * dws[2]
        for dl_ref, wt, dw in zip((dl0, dl1, dl2), ws, dws):
            dl_ref[...] = wt * (dw - mean)

    o_spec = pl.BlockSpec((tq, w), lambda i: (i, 0))
    l_spec = pl.BlockSpec((tq, LANES), lambda i: (i, 0))
    o_shape = jax.ShapeDtypeStruct((t, w), F32)
    l_shape = jax.ShapeDtypeStruct((t, LANES), F32)
    return pl.pallas_call(
        body, name="combine_bwd", grid=(t // tq,),
        in_specs=[o_spec] * 3 + [l_spec] * 3 + [o_spec], out_specs=[o_spec] * 3 + [l_spec] * 3,
        out_shape=[o_shape] * 3 + [l_shape] * 3,
        compiler_params=_cparams(("parallel",)),
    )(*os_, *lses, dy)


C_QKW = C_QK_HEADS * C_DK
C_CONV_W = 2 * C_QKW + C_V_HEADS * C_DK
HALO = 8


def _silu_parts(z):
    sig = jax.nn.sigmoid(z)
    return z * sig, sig * (1.0 + z * (1.0 - sig))


def _conv_window_specs(tq, t):
    c = C_CONV_W
    cb = COL["c_qkv"] // c
    blk = pl.BlockSpec((tq, c), lambda i: (i, cb))
    before = pl.BlockSpec((HALO, c), lambda i: (jnp.maximum(i * (tq // HALO) - 1, 0), cb))
    return c, cb, blk, before


def _conv_prep_fwd_call(u, w, *, tq=512):
    t = u.shape[0]
    c, _, x_spec, halo_spec = _conv_window_specs(tq, t)
    nqk = 2 * C_QK_HEADS

    def body(x_ref, halo_ref, w_ref, z_ref, qk_ref, v_ref):
        i = pl.program_id(0)
        halo = jnp.where(i == 0, 0.0, halo_ref[...])
        xc = jnp.concatenate([halo, x_ref[...]], axis=0)
        wv = w_ref[...]
        z = xc[HALO - 3:HALO - 3 + tq] * wv[0:1]
        for j in range(1, C_CONV):
            z = z + xc[HALO - 3 + j:HALO - 3 + j + tq] * wv[j:j + 1]
        z_ref[...] = z
        act, _ = _silu_parts(z)
        for h in range(nqk):
            a = act[:, h * C_DK:(h + 1) * C_DK]
            qk_ref[:, h * C_DK:(h + 1) * C_DK] = a * lax.rsqrt(jnp.sum(a * a, axis=1, keepdims=True) + EPS)
        v_ref[...] = act[:, nqk * C_DK:]

    return pl.pallas_call(
        body, name="conv_prep_fwd", grid=(t // tq,),
        in_specs=[x_spec, halo_spec, pl.BlockSpec((C_CONV, c), lambda i: (0, 0))],
        out_specs=[pl.BlockSpec((tq, c), lambda i: (i, 0)),
                   pl.BlockSpec((tq, 2 * C_QKW), lambda i: (i, 0)),
                   pl.BlockSpec((tq, c - 2 * C_QKW), lambda i: (i, 0))],
        out_shape=[jax.ShapeDtypeStruct((t, c), F32), jax.ShapeDtypeStruct((t, 2 * C_QKW), F32),
                   jax.ShapeDtypeStruct((t, c - 2 * C_QKW), F32)],
        compiler_params=_cparams(("parallel",)),
    )(u, u, w)


def _conv_prep_dz_call(z, dqk, dv, *, tq=512):
    t, c = z.shape
    nqk = 2 * C_QK_HEADS

    def body(z_ref, dqk_ref, dv_ref, dz_ref):
        zv = z_ref[...]
        act, dact = _silu_parts(zv)
        for h in range(nqk):
            hs = slice(h * C_DK, (h + 1) * C_DK)
            a = act[:, hs]
            r = lax.rsqrt(jnp.sum(a * a, axis=1, keepdims=True) + EPS)
            nrm = a * r
            dn = dqk_ref[:, hs]
            da = r * (dn - nrm * jnp.sum(dn * nrm, axis=1, keepdims=True))
            dz_ref[:, hs] = da * dact[:, hs]
        dz_ref[:, nqk * C_DK:] = dv_ref[...] * dact[:, nqk * C_DK:]

    return pl.pallas_call(
        body, name="conv_prep_dz", grid=(t // tq,),
        in_specs=[pl.BlockSpec((tq, c), lambda i: (i, 0)),
                  pl.BlockSpec((tq, 2 * C_QKW), lambda i: (i, 0)),
                  pl.BlockSpec((tq, c - 2 * C_QKW), lambda i: (i, 0))],
        out_specs=pl.BlockSpec((tq, c), lambda i: (i, 0)),
        out_shape=jax.ShapeDtypeStruct((t, c), F32),
        compiler_params=_cparams(("parallel",)),
    )(z, dqk, dv)


def _conv_bwd_call(u, dz, w, du_buf, *, tq=512):
    t = u.shape[0]
    nt = t // tq
    c, cb, x_spec, halo_spec = _conv_window_specs(tq, t)
    extra, extra_specs, aliases = _du_operands(du_buf, 5)

    def body(x_ref, xh_ref, dz_ref, dzh_ref, w_ref, *refs):
        dx_ref, dw_ref = refs[len(extra):]
        i = pl.program_id(0)
        xc = jnp.concatenate([jnp.where(i == 0, 0.0, xh_ref[...]), x_ref[...]], axis=0)
        dzv = dz_ref[...]
        dzc = jnp.concatenate([dzv, jnp.where(i == nt - 1, 0.0, dzh_ref[...])], axis=0)
        wv = w_ref[...]
        dx = dzv * wv[3:4]
        for s in range(1, C_CONV):
            dx = dx + dzc[s:s + tq] * wv[3 - s:4 - s]
        dx_ref[...] = dx.astype(BF16)
        row = lax.broadcasted_iota(jnp.int32, (8, c), 0)
        dw = jnp.zeros((8, c), F32)
        for j in range(C_CONV):
            prod = dzv * xc[HALO - 3 + j:HALO - 3 + j + tq]
            col = jnp.sum(jnp.sum(prod.reshape(tq // 8, 8, c), axis=0), axis=0, keepdims=True)
            dw = jnp.where(row == j, col, dw)

        @pl.when(i == 0)
        def _():
            dw_ref[...] = dw

        @pl.when(i > 0)
        def _():
            dw_ref[...] += dw

    blk = pl.BlockSpec((tq, c), lambda i: (i, 0))
    after = pl.BlockSpec((HALO, c), lambda i: (jnp.minimum((i + 1) * (tq // HALO), t // HALO - 1), 0))
    return pl.pallas_call(
        body, name="conv_bwd", grid=(nt,),
        in_specs=[x_spec, halo_spec, blk, after, pl.BlockSpec((C_CONV, c), lambda i: (0, 0))] + extra_specs,
        out_specs=[pl.BlockSpec((tq, c), lambda i: (i, cb)), pl.BlockSpec((8, c), lambda i: (0, 0))],
        out_shape=[_du_shape(t), jax.ShapeDtypeStruct((8, c), F32)],
        input_output_aliases=aliases,
        compiler_params=_cparams(("arbitrary",)),
    )(u, u, dz, dz, w, *extra)


C_VW = C_V_HEADS * C_DK


def _softplus(x):
    return jnp.maximum(x, 0.0) + jnp.log(1.0 + jnp.exp(-jnp.abs(x)))


def _tri_masks():
    r = lax.broadcasted_iota(jnp.int32, (CHUNK, CHUNK), 0)
    c = lax.broadcasted_iota(jnp.int32, (CHUNK, CHUNK), 1)
    return r >= c, r > c


def _split_bf16(a):
    hi = a.astype(BF16)
    return hi, (a - hi.astype(F32)).astype(BF16)


def _dot_hi(a, b, dims=None):
    dims = _NN if dims is None else dims
    ah, al = _split_bf16(a)
    bh, bl = _split_bf16(b)

    def d(x, y):
        return lax.dot_general(x, y, dims, preferred_element_type=F32)

    return d(ah, bh) + (d(ah, bl) + d(al, bh))


def _dot_mask(mask, b, dims=None, mask_left=True):
    dims = _NN if dims is None else dims
    mb = mask.astype(BF16)
    b1 = b.astype(BF16)
    rest = b - b1.astype(F32)
    b2 = rest.astype(BF16)
    b3 = (rest - b2.astype(F32)).astype(BF16)
    out = None
    for p in (b1, b2, b3):
        term = (lax.dot_general(mb, p, dims, preferred_element_type=F32) if mask_left
                else lax.dot_general(p, mb, dims, preferred_element_type=F32))
        out = term if out is None else out + term
    return out


def _unit_lower_inverses(mats):
    r = lax.broadcasted_iota(jnp.int32, (CHUNK, CHUNK), 0)
    c = lax.broadcasted_iota(jnp.int32, (CHUNK, CHUNK), 1)
    eye = (r == c).astype(F32)
    xs = [eye - a for a in mats]
    ps = [_dot_hi(a, a) for a in mats]
    steps = int(math.log2(CHUNK)) - 1
    for s in range(steps):
        xs = [x + _dot_hi(x, p) for x, p in zip(xs, ps)]
        if s < steps - 1:
            ps = [_dot_hi(p, p) for p in ps]
    return xs


def _gate_tiles(cab, alog, dtb):
    pre = cab + dtb
    g = -jnp.exp(alog) * _softplus(pre)
    beta = jax.nn.sigmoid(pltpu.roll(cab, LANES - C_V_HEADS, 1))
    return g, beta, pre


def _chunk_common(kk, qk, gc, gct, beta, h, tri, strict):
    gcol, grow, bcol = gc[:, h:h + 1], gct[h:h + 1, :], beta[:, h:h + 1]
    decay = jnp.where(tri, jnp.exp(jnp.where(tri, gcol - grow, 0.0)), 0.0)
    kkd = jnp.where(strict, kk * decay, 0.0)
    attn = jnp.where(tri, qk * decay, 0.0)
    glast = gc[CHUNK - 1:CHUNK, h:h + 1]
    return gcol, bcol, decay, kkd, attn, glast


def _cab_spec():
    return pl.BlockSpec((CHUNK, LANES), lambda n: (n, COL["c_ab"] // LANES))


def _delta_prep_call(qk, v, u, alog, dtb, gather_src=None):
    t = qk.shape[0]
    nc = t // CHUNK
    scale = C_DK ** -0.5
    riding = gather_src is not None

    def body(q_ref, k_ref, v_ref, cab_ref, alog_ref, dtb_ref, *refs):
        if riding:
            gather_refs = (refs[0], refs[9]) + tuple(refs[10:])
            refs = refs[1:9]

            @pl.when(pl.program_id(0) == 0)
            def _():
                _gather_start(*gather_refs)
        u_ref, w_ref, qd_ref, kd_ref, attn_ref, tmat_ref, gc_ref, beta_ref = refs
        tri, strict = _tri_masks()
        g, beta, _ = _gate_tiles(cab_ref[...], alog_ref[...], dtb_ref[...])
        gc = _dot_mask(tri, g)
        gct = gc.T
        gc_ref[...] = gc
        beta_ref[...] = beta
        mats, rhs = [], []
        for j in range(C_QK_HEADS):
            js = slice(j * C_DK, (j + 1) * C_DK)
            kf, qf = k_ref[:, js], q_ref[:, js] * scale
            kb, qb = kf.astype(BF16), qf.astype(BF16)
            kk = lax.dot_general(kb, kb, _NT, preferred_element_type=F32)
            qk = lax.dot_general(qb, kb, _NT, preferred_element_type=F32)
            for h in (2 * j, 2 * j + 1):
                hs = slice(h * C_DK, (h + 1) * C_DK)
                gcol, bcol, decay, kkd, attn, glast = _chunk_common(kk, qk, gc, gct, beta, h, tri, strict)
                gexp = jnp.exp(gcol)
                mats.append(kkd * bcol)
                rhs.append(jnp.concatenate([v_ref[:, hs] * bcol, kf * (bcol * gexp)], axis=1))
                qd_ref[:, hs] = (qf * gexp).astype(BF16)
                kd_ref[:, hs] = (kf * jnp.exp(glast - gcol)).astype(BF16)
                attn_ref[:, h * CHUNK:(h + 1) * CHUNK] = attn.astype(BF16)
        for h, (tmat, r) in enumerate(zip(_unit_lower_inverses(mats), rhs)):
            hs = slice(h * C_DK, (h + 1) * C_DK)
            uw = _dot_hi(tmat, r)
            u_ref[:, hs] = uw[:, :C_DK]
            w_ref[:, hs] = uw[:, C_DK:]
            tmat_ref[:, h * CHUNK:(h + 1) * CHUNK] = tmat
        if riding:
            @pl.when(pl.program_id(0) == nc - 1)
            def _():
                _gather_finish(*gather_refs)

    def blk(w):
        return pl.BlockSpec((CHUNK, w), lambda n: (n, 0))

    row = pl.BlockSpec((1, LANES), lambda n: (0, 0))
    big = jax.ShapeDtypeStruct((t, C_VW), F32)
    sq = jax.ShapeDtypeStruct((t, C_V_HEADS * CHUNK), F32)
    tile = jax.ShapeDtypeStruct((t, LANES), F32)
    half = jax.ShapeDtypeStruct((t, C_VW), BF16)
    in_specs = [blk(C_QKW), pl.BlockSpec((CHUNK, C_QKW), lambda n: (n, 1)), blk(C_VW), _cab_spec(), row, row]
    out_specs = [blk(C_VW)] * 4 + [blk(C_V_HEADS * CHUNK)] * 2 + [blk(LANES)] * 2
    out_shape = [big, big, half, half, jax.ShapeDtypeStruct(sq.shape, BF16), sq] + [tile] * 2
    args = [qk, qk, v, u, alog, dtb]
    if riding:
        in_specs.append(HBM_SPEC)
        out_specs.append(HBM_SPEC)
        out_shape.append(jax.ShapeDtypeStruct((N_DEV,) + gather_src.shape, gather_src.dtype))
        args.append(gather_src)
    return pl.pallas_call(
        body, name="delta_prep_gather" if riding else "delta_prep", grid=(nc,),
        in_specs=in_specs, out_specs=out_specs, out_shape=out_shape,
        scratch_shapes=list(GATHER_SEMS) if riding else [],
        compiler_params=_cparams(("arbitrary",) if riding else ("parallel",)),
    )(*args)


SCAN_SUB = 4


def _delta_scan_call(u, w, qd, kd, attn, gc):
    t = u.shape[0]
    nc = t // CHUNK
    rows = SCAN_SUB * CHUNK

    def body(u_ref, w_ref, qd_ref, kd_ref, attn_ref, gc_ref, o_ref, vn_ref, st_ref, s_ref):
        @pl.when(pl.program_id(0) == 0)
        def _():
            s_ref[...] = jnp.zeros_like(s_ref)

        hss = [slice(h * C_DK, (h + 1) * C_DK) for h in range(C_V_HEADS)]
        states = [s_ref[hs, :] for hs in hss]
        for c in range(SCAN_SUB):
            rs = slice(c * CHUNK, (c + 1) * CHUNK)
            for hs, s in zip(hss, states):
                st_ref[c, hs, :] = s
            sbs = [s.astype(BF16) for s in states]
            vns = [u_ref[rs, hs] - jnp.dot(w_ref[rs, hs].astype(BF16), sb, preferred_element_type=F32)
                   for hs, sb in zip(hss, sbs)]
            qss = [jnp.dot(qd_ref[rs, hs].astype(BF16), sb, preferred_element_type=F32) for hs, sb in zip(hss, sbs)]
            vnbs = [vn.astype(BF16) for vn in vns]
            for h, hs in enumerate(hss):
                vn_ref[rs, hs] = vnbs[h]
                o_ref[rs, hs] = qss[h] + jnp.dot(attn_ref[rs, h * CHUNK:(h + 1) * CHUNK].astype(BF16), vnbs[h],
                                                 preferred_element_type=F32)
            last = (c + 1) * CHUNK - 1
            states = [states[h] * jnp.exp(gc_ref[last:last + 1, h:h + 1])
                      + lax.dot_general(kd_ref[rs, hs].astype(BF16), vnbs[h], _TN, preferred_element_type=F32)
                      for h, hs in enumerate(hss)]
        for hs, s in zip(hss, states):
            s_ref[hs, :] = s

    def blk(wd):
        return pl.BlockSpec((rows, wd), lambda n: (n, 0))

    big = jax.ShapeDtypeStruct((t, C_VW), F32)
    return pl.pallas_call(
        body, name="delta_scan", grid=(nc // SCAN_SUB,),
        in_specs=[blk(C_VW)] * 4 + [blk(C_V_HEADS * CHUNK), blk(LANES)],
        out_specs=[blk(C_VW), blk(C_VW), pl.BlockSpec((SCAN_SUB, C_VW, C_DK), lambda n: (n, 0, 0))],
        out_shape=[big, jax.ShapeDtypeStruct((t, C_VW), BF16), jax.ShapeDtypeStruct((nc, C_VW, C_DK), F32)],
        scratch_shapes=[pltpu.VMEM((C_VW, C_DK), F32)],
        compiler_params=_cparams(("arbitrary",)),
    )(u, w, qd, kd, attn, gc)


def _delta_scan_bwd_call(do, w, qd, kd, attn, gc, vn, st):
    t = do.shape[0]
    nc = t // CHUNK
    rows = SCAN_SUB * CHUNK
    steps = nc // SCAN_SUB

    def body(do_ref, w_ref, qd_ref, kd_ref, attn_ref, gc_ref, vn_ref, st_ref,
             du_ref, dw_ref, dqd_ref, dkd_ref, dattn_ref, dgl_ref, ds_ref):
        @pl.when(pl.program_id(0) == 0)
        def _():
            ds_ref[...] = jnp.zeros_like(ds_ref)

        tri, _ = _tri_masks()
        row = lax.broadcasted_iota(jnp.int32, (8, LANES), 0)
        lane = lax.broadcasted_iota(jnp.int32, (8, LANES), 1)
        hss = [slice(h * C_DK, (h + 1) * C_DK) for h in range(C_V_HEADS)]
        css = [slice(h * CHUNK, (h + 1) * CHUNK) for h in range(C_V_HEADS)]

        def dg(a, b, dims):
            return lax.dot_general(a, b, dims, preferred_element_type=F32)

        dsps = [ds_ref[hs, :] for hs in hss]
        for c in reversed(range(SCAN_SUB)):
            rs = slice(c * CHUNK, (c + 1) * CHUNK)
            dgl = jnp.zeros((8, LANES), F32)
            ss = [st_ref[c, hs, :] for hs in hss]
            sbs = [s.astype(BF16) for s in ss]
            dspbs = [d.astype(BF16) for d in dsps]
            dobs = [do_ref[rs, hs].astype(BF16) for hs in hss]
            vnbs = [vn_ref[rs, hs].astype(BF16) for hs in hss]
            dvns = [dg(attn_ref[rs, cs].astype(BF16), dob, _TN) + dg(kd_ref[rs, hs].astype(BF16), dspb, _NN)
                    for hs, cs, dob, dspb in zip(hss, css, dobs, dspbs)]
            for h, hs in enumerate(hss):
                dqd_ref[rs, hs] = dg(dobs[h], sbs[h], _NT)
                dkd_ref[rs, hs] = dg(vnbs[h], dspbs[h], _NT)
                dattn_ref[rs, css[h]] = jnp.where(tri, dg(dobs[h], vnbs[h], _NT), 0.0)
            dvnbs = [d.astype(BF16) for d in dvns]
            for h, hs in enumerate(hss):
                du_ref[rs, hs] = dvns[h]
                dw_ref[rs, hs] = -dg(dvnbs[h], sbs[h], _NT)
                tot = jnp.sum(jnp.sum(dsps[h] * ss[h], axis=0, keepdims=True), axis=1, keepdims=True)
                dgl = jnp.where(jnp.logical_and(row == 0, lane == h), tot, dgl)
            dgl_ref[c * 8:(c + 1) * 8, :] = dgl
            last = (c + 1) * CHUNK - 1
            dsps = [dg(qd_ref[rs, hs].astype(BF16), dobs[h], _TN) + jnp.exp(gc_ref[last:last + 1, h:h + 1]) * dsps[h]
                    - dg(w_ref[rs, hs].astype(BF16), dvnbs[h], _TN) for h, hs in enumerate(hss)]
        for hs, d in zip(hss, dsps):
            ds_ref[hs, :] = d

    def blk(wd):
        return pl.BlockSpec((rows, wd), lambda n: (steps - 1 - n, 0))

    big = jax.ShapeDtypeStruct((t, C_VW), F32)
    return pl.pallas_call(
        body, name="delta_scan_bwd", grid=(steps,),
        in_specs=[blk(C_VW)] * 4 + [blk(C_V_HEADS * CHUNK), blk(LANES), blk(C_VW),
                                    pl.BlockSpec((SCAN_SUB, C_VW, C_DK), lambda n: (steps - 1 - n, 0, 0))],
        out_specs=[blk(C_VW)] * 4 + [blk(C_V_HEADS * CHUNK),
                                     pl.BlockSpec((SCAN_SUB * 8, LANES), lambda n: (steps - 1 - n, 0))],
        out_shape=[big] * 4 + [jax.ShapeDtypeStruct((t, C_V_HEADS * CHUNK), F32),
                               jax.ShapeDtypeStruct((nc * 8, LANES), F32)],
        scratch_shapes=[pltpu.VMEM((C_VW, C_DK), F32)],
        compiler_params=_cparams(("arbitrary",)),
    )(do, w, qd, kd, attn, gc, vn, st)


PREP_SUB = 2


def _delta_prep_bwd_call(qk, v, proj, alog, dtb, tmat, u, w, gc, beta, du, dw, dqd, dkd, dattn, dgl, du_buf):
    t = qk.shape[0]
    extra, extra_specs, aliases = _du_operands(du_buf, 17)
    nc = t // CHUNK
    rows = PREP_SUB * CHUNK
    scale = C_DK ** -0.5

    def body(q_ref, k_ref, v_ref, cab_ref, alog_ref, dtb_ref, tmat_ref, u_ref, w_ref, gc_ref, beta_ref,
             du_ref, dw_ref, dqd_ref, dkd_ref, dattn_ref, dgl_ref, *outs):
        dcab_ref, dqk_ref, dv_ref, dpar_ref = outs[len(extra):]
        tri, strict = _tri_masks()
        ones = jnp.ones((CHUNK, LANES), F32)
        lane = lax.broadcasted_iota(jnp.int32, (CHUNK, LANES), 1)
        rowi = lax.broadcasted_iota(jnp.int32, (CHUNK, 1), 0)
        subs = range(PREP_SUB)
        rss = [slice(c * CHUNK, (c + 1) * CHUNK) for c in subs]
        betas = [beta_ref[rs, :] for rs in rss]

        def dot(x, y, dims=_NN):
            return lax.dot_general(x, y, dims, preferred_element_type=F32)

        heads = []
        for c, rs in zip(subs, rss):
            gc = gc_ref[rs, :]
            gct = gc.T
            for j in range(C_QK_HEADS):
                js = slice(j * C_DK, (j + 1) * C_DK)
                kf, qf = k_ref[rs, js], q_ref[rs, js] * scale
                kb, qb = kf.astype(BF16), qf.astype(BF16)
                kk = dot(kb, kb, _NT)
                qk = dot(qb, kb, _NT)
                for h in (2 * j, 2 * j + 1):
                    heads.append((c, rs, h, kf, qf, kb, qb) + _chunk_common(kk, qk, gc, gct, betas[c], h, tri, strict))

        def cols(h):
            return slice(h * C_DK, (h + 1) * C_DK)

        def sq(h):
            return slice(h * CHUNK, (h + 1) * CHUNK)

        dvks = [_dot_hi(tmat_ref[hd[1], sq(hd[2])],
                        jnp.concatenate([du_ref[hd[1], cols(hd[2])], dw_ref[hd[1], cols(hd[2])]], axis=1), _TN)
                for hd in heads]
        das = [-jnp.where(strict, _dot_hi(dvk, jnp.concatenate([u_ref[hd[1], cols(hd[2])], w_ref[hd[1], cols(hd[2])]],
                                                               axis=1), _NT), 0.0)
               for hd, dvk in zip(heads, dvks)]
        pre = []
        for (c, rs, h, kf, qf, kb, qb, gcol, bcol, decay, kkd, attn, glast), da in zip(heads, das):
            dattn_h = dattn_ref[rs, sq(h)]
            pre.append(((da * decay * bcol).astype(BF16), (dattn_h * decay).astype(BF16),
                        da * kkd * bcol + dattn_h * attn))
        mms = [(dot(dkk, hd[5]), dot(dkk, hd[5], _TN), dot(dqk, hd[6], _TN), dot(dqk, hd[5]),
                _dot_mask(ones, e, _TN, mask_left=False))
               for hd, (dkk, dqk, e) in zip(heads, pre)]
        dq_parts, dk_parts = {}, {}
        dgc_tiles = [jnp.zeros((CHUNK, LANES), F32) for _ in subs]
        db_tiles = [jnp.zeros((CHUNK, LANES), F32) for _ in subs]
        for (c, rs, h, kf, qf, kb, qb, gcol, bcol, decay, kkd, attn, glast), dvk, da, (_, _, e), mm in zip(
                heads, dvks, das, pre, mms):
            hs = cols(h)
            gexp = jnp.exp(gcol)
            fdec = jnp.exp(glast - gcol)
            dvb, dkb = dvk[:, :C_DK], dvk[:, C_DK:]
            dgc = jnp.sum(e, axis=1, keepdims=True) - mm[4][:, :1]
            dk_parts[c, h] = mm[0] + mm[1] + mm[2] + dkb * (bcol * gexp) + dkd_ref[rs, hs] * fdec
            dq_parts[c, h] = mm[3] + dqd_ref[rs, hs] * gexp
            dv_ref[rs, hs] = dvb * bcol
            s_kb = jnp.sum(dkb * kf, axis=1, keepdims=True)
            db = (jnp.sum(da * kkd, axis=1, keepdims=True) + jnp.sum(dvb * v_ref[rs, hs], axis=1, keepdims=True)
                  + s_kb * gexp)
            rho = jnp.sum(dkd_ref[rs, hs] * kf, axis=1, keepdims=True) * fdec
            dgc = (dgc + s_kb * bcol * gexp + jnp.sum(dqd_ref[rs, hs] * qf, axis=1, keepdims=True) * gexp - rho)
            last = jnp.sum(rho, axis=0, keepdims=True) + dgl_ref[c * 8:c * 8 + 1, h:h + 1] * jnp.exp(glast)
            dgc = dgc + jnp.where(rowi == CHUNK - 1, last, 0.0)
            dgc_tiles[c] = jnp.where(lane == h, dgc, dgc_tiles[c])
            db_tiles[c] = jnp.where(lane == h, db, db_tiles[c])
        alog = alog_ref[...]
        row8 = lax.broadcasted_iota(jnp.int32, (8, LANES), 0)
        par = jnp.zeros((8, LANES), F32)
        for c, rs in zip(subs, rss):
            for j in range(C_QK_HEADS):
                dqk_ref[rs, j * C_DK:(j + 1) * C_DK] = (dq_parts[c, 2 * j] + dq_parts[c, 2 * j + 1]) * scale
                dqk_ref[rs, C_QKW + j * C_DK:C_QKW + (j + 1) * C_DK] = dk_parts[c, 2 * j] + dk_parts[c, 2 * j + 1]
            dg = _dot_mask(jnp.logical_not(strict), dgc_tiles[c])
            g, _, gate_pre = _gate_tiles(cab_ref[rs, :], alog, dtb_ref[...])
            dca = dg * (-jnp.exp(alog)) * jax.nn.sigmoid(gate_pre)
            beta = betas[c]
            dcab_ref[rs, :LANES] = (dca + pltpu.roll(db_tiles[c] * beta * (1.0 - beta), C_V_HEADS, 1)).astype(BF16)
            dcab_ref[rs, LANES:] = jnp.zeros((CHUNK, D_IN_PAD - COL["c_ab"] - LANES), BF16)
            par = par + jnp.where(row8 == 0, jnp.sum(dg * g, axis=0, keepdims=True),
                                  jnp.where(row8 == 1, jnp.sum(dca, axis=0, keepdims=True), 0.0))

        @pl.when(pl.program_id(0) == 0)
        def _():
            dpar_ref[...] = par

        @pl.when(pl.program_id(0) > 0)
        def _():
            dpar_ref[...] += par

    def blk(wd):
        return pl.BlockSpec((rows, wd), lambda n: (n, 0))

    row = pl.BlockSpec((1, LANES), lambda n: (0, 0))
    sqs = blk(C_V_HEADS * CHUNK)
    tail = D_IN_PAD - COL["c_ab"]
    assert COL["c_ab"] % tail == 0 and nc % PREP_SUB == 0
    return pl.pallas_call(
        body, name="delta_prep_bwd", grid=(nc // PREP_SUB,),
        in_specs=[blk(C_QKW), pl.BlockSpec((rows, C_QKW), lambda n: (n, 1)), blk(C_VW),
                  pl.BlockSpec((rows, LANES), lambda n: (n, COL["c_ab"] // LANES)), row, row, sqs,
                  blk(C_VW), blk(C_VW),
                  blk(LANES), blk(LANES), blk(C_VW), blk(C_VW), blk(C_VW), blk(C_VW), sqs,
                  pl.BlockSpec((PREP_SUB * 8, LANES), lambda n: (n, 0))] + extra_specs,
        out_specs=[pl.BlockSpec((rows, tail), lambda n: (n, COL["c_ab"] // tail)),
                   blk(2 * C_QKW), blk(C_VW), pl.BlockSpec((8, LANES), lambda n: (0, 0))],
        out_shape=[_du_shape(t), jax.ShapeDtypeStruct((t, 2 * C_QKW), F32),
                   jax.ShapeDtypeStruct((t, C_VW), F32), jax.ShapeDtypeStruct((8, LANES), F32)],
        input_output_aliases=aliases,
        compiler_params=_cparams(("arbitrary",)),
    )(qk, qk, v, proj, alog, dtb, tmat, u, w, gc, beta, du, dw, dqd, dkd, dattn, dgl, *extra)


def _z_spec(tq):
    return pl.BlockSpec((tq, C_VW), lambda i: (i, COL["c_z"] // C_VW))


def _gated_norm_fwd_call(o, u, gain, *, tq=256):
    t, w = o.shape

    def body(o_ref, z_ref, g_ref, y_ref):
        act, _ = _silu_parts(z_ref[...])
        gv = g_ref[...]
        for h in range(C_V_HEADS):
            hs = slice(h * C_DK, (h + 1) * C_DK)
            ov = o_ref[:, hs]
            r = lax.rsqrt(jnp.mean(ov * ov, axis=1, keepdims=True) + EPS)
            y_ref[:, hs] = ov * r * gv * act[:, hs]

    blk = pl.BlockSpec((tq, w), lambda i: (i, 0))
    return pl.pallas_call(
        body, name="gated_norm_fwd", grid=(t // tq,),
        in_specs=[blk, _z_spec(tq), pl.BlockSpec((1, C_DK), lambda i: (0, 0))], out_specs=blk,
        out_shape=jax.ShapeDtypeStruct((t, w), F32),
        compiler_params=_cparams(("parallel",)),
    )(o, u, gain)


def _gated_norm_bwd_call(o, u, gain, dy, du_buf, *, tq=256):
    t, w = o.shape
    nt = t // tq
    extra, extra_specs, aliases = _du_operands(du_buf, 4)

    def body(o_ref, z_ref, g_ref, dy_ref, *refs):
        dz_ref, do_ref, dg_ref, acc_ref = refs[len(extra):]
        i = pl.program_id(0)
        act, dact = _silu_parts(z_ref[...])
        gv = g_ref[...]
        part = jnp.zeros((8, C_DK), F32)
        for h in range(C_V_HEADS):
            hs = slice(h * C_DK, (h + 1) * C_DK)
            ov = o_ref[:, hs]
            r = lax.rsqrt(jnp.mean(ov * ov, axis=1, keepdims=True) + EPS)
            xh = ov * r
            dyv = dy_ref[:, hs]
            dn = dyv * act[:, hs]
            dz_ref[:, hs] = (dyv * xh * gv * dact[:, hs]).astype(BF16)
            dxh = dn * gv
            do_ref[:, hs] = r * (dxh - xh * jnp.mean(dxh * xh, axis=1, keepdims=True))
            part = part + jnp.sum((dn * xh).reshape(tq // 8, 8, C_DK), axis=0)

        @pl.when(i == 0)
        def _():
            acc_ref[...] = part

        @pl.when(i > 0)
        def _():
            acc_ref[...] += part

        @pl.when(i == nt - 1)
        def _():
            dg_ref[...] = jnp.sum(acc_ref[...], axis=0, keepdims=True)

    blk = pl.BlockSpec((tq, w), lambda i: (i, 0))
    grow = pl.BlockSpec((1, C_DK), lambda i: (0, 0))
    return pl.pallas_call(
        body, name="gated_norm_bwd", grid=(nt,),
        in_specs=[blk, _z_spec(tq), grow, blk] + extra_specs, out_specs=[_z_spec(tq), blk, grow],
        out_shape=[_du_shape(t), jax.ShapeDtypeStruct((t, w), F32), jax.ShapeDtypeStruct((1, C_DK), F32)],
        scratch_shapes=[pltpu.VMEM((8, C_DK), F32)],
        input_output_aliases=aliases,
        compiler_params=_cparams(("arbitrary",)),
    )(o, u, gain, dy, *extra)


def _gate_specs(tq):
    return [pl.BlockSpec((tq, D_MODEL), lambda i, j=j: (i, j)) for j in range(3)]


def _merge_fwd_call(ps, u, *, tq=256):
    t, w = ps[0].shape

    def body(p0, p1, p2, g0, g1, g2, y_ref):
        y_ref[...] = (jax.nn.sigmoid(g0[...]) * p0[...] + jax.nn.sigmoid(g1[...]) * p1[...]
                      + jax.nn.sigmoid(g2[...]) * p2[...]).astype(BF16)

    blk = pl.BlockSpec((tq, w), lambda i: (i, 0))
    return pl.pallas_call(
        body, name="merge_fwd", grid=(t // tq,), in_specs=[blk] * 3 + _gate_specs(tq), out_specs=blk,
        out_shape=jax.ShapeDtypeStruct((t, w), BF16),
        compiler_params=_cparams(("parallel",)),
    )(*ps, u, u, u)


def _merge_bwd_call(ps, u, dy, *, tq=256):
    t, w = dy.shape

    def body(p0, p1, p2, g0, g1, g2, dy_ref, dg_ref, dp0, dp1, dp2):
        dyv = dy_ref[...]
        for j, (p, g, dp) in enumerate(((p0, g0, dp0), (p1, g1, dp1), (p2, g2, dp2))):
            sig = jax.nn.sigmoid(g[...])
            dp[...] = (dyv * sig).astype(BF16)
            dg_ref[:, j * w:(j + 1) * w] = (dyv * p[...] * sig * (1.0 - sig)).astype(BF16)

    blk = pl.BlockSpec((tq, w), lambda i: (i, 0))
    small = jax.ShapeDtypeStruct((t, w), BF16)
    return pl.pallas_call(
        body, name="merge_bwd", grid=(t // tq,), in_specs=[blk] * 3 + _gate_specs(tq) + [blk],
        out_specs=[pl.BlockSpec((tq, 3 * w), lambda i: (i, 0))] + [blk] * 3,
        out_shape=[_du_shape(t)] + [small] * 3,
        compiler_params=_cparams(("parallel",)),
    )(*ps, u, u, u, dy)


Q_SCALE = HEAD_DIM ** -0.5
A_PARTS = ((COL["a_q"], A_W, True, Q_SCALE), (COL["a_k"], A_W, True, 1.0), (COL["a_v"], A_W, False, 1.0))
B_PARTS = ((COL["b_q"], A_W, True, Q_SCALE), (COL["b_k"], B_KVW, True, 1.0), (COL["b_v"], B_KVW, False, 1.0))
BRANCHES = ("w_branch_a", "w_branch_b", "w_branch_c")


def _layer_fwd(x, tabs, p, wb, gather_src=None):
    h = _rms_fwd_call(x, p["norm_mix"], name="rms_mix_fwd", out_dtype=BF16)
    u = _mm(h, wb["w_in"], bias=p["b_in"], tn=IN_TN, name="in_proj_fwd")
    qkv_a = _rope_gather_call(u, tabs, A_PARTS, name="rope_a_fwd")
    a_runs = [_attn_fwd(cfg, qkv_a, None) for cfg in ATTN_A_CFGS]
    os_, lses = tuple(r[0] for r in a_runs), tuple(r[1] for r in a_runs)
    ya = _combine_fwd_call(os_, lses)
    qkv_b = _rope_gather_call(u, tabs, B_PARTS, name="rope_b_fwd")
    yb, _, b_saved = _attn_fwd(ATTN_B_CFG, qkv_b, p["sinks"])
    zc, qk, v = _conv_prep_fwd_call(u, p["conv_w"])
    uu, ww, qd, kd, attn, tmat, gc, beta, *gathered = _delta_prep_call(qk, v, u, p["a_log"], p["dt_bias"], gather_src)
    o, vn, st = _delta_scan_call(uu, ww, qd, kd, attn, gc)
    yc = _gated_norm_fwd_call(o, u, p["c_norm"])
    ys = (ya, yb, yc)
    ps = tuple(_mm(y, wb[n], name="branch_fwd") for y, n in zip(ys, BRANCHES))
    merged = _merge_fwd_call(ps, u)
    x1 = _mm(merged, wb["w_out"], add=x, name="out_proj_fwd")
    h2 = _rms_fwd_call(x1, p["norm_ffn"], name="rms_ffn_fwd", out_dtype=BF16)
    pre, act = _mm(h2, wb["w_ff1"], relu2_out=True, name="ffn_up")
    x2 = _mm(act, wb["w_ff2"], add=x1, name="ffn_down")
    saved = dict(x=x, h=h, u=u, a_saved=[r[2] for r in a_runs], os_=os_, lses=lses, b_saved=b_saved,
                 zc=zc, qk=qk, v=v, delta=(tmat, uu, ww, gc, beta, qd, kd, attn, vn, st), o=o, ys=ys, ps=ps,
                 merged=merged, x1=x1, h2=h2, pre=pre, act=act)
    return x2, saved, (gathered[0] if gathered else None)


def _layer_bwd(s, dx2, tabs, p, wb):
    g = {}
    t = dx2.shape[0]
    dpre = _mm(dx2, wb["w_ff2"], tb=True, mul_drelu2=s["pre"], out_dtype=BF16, name="ffn_dpre")
    g["w_ff2"] = _mm(s["act"], dx2, ta=True, tk=1024, name="ffn_dw2")
    g["w_ff1"] = _mm(s["h2"], dpre, ta=True, tk=1024, name="ffn_dw1")
    dh2 = _mm(dpre, wb["w_ff1"], tb=True, name="ffn_dh")
    dx1, g["norm_ffn"] = _rms_bwd_call(s["x1"], p["norm_ffn"], dh2, add=dx2, name="rms_ffn_bwd")
    dmerged = _mm(dx1, wb["w_out"], tb=True, name="out_proj_da")
    g["w_out"] = _mm(s["merged"], dx1, ta=True, tk=1024, name="out_proj_dw")
    du, *dps = _merge_bwd_call(s["ps"], s["u"], dmerged)
    dys = []
    for y, dp, n in zip(s["ys"], dps, BRANCHES):
        dys.append(_mm(dp, wb[n], tb=True, name="branch_da"))
        g[n] = _mm(y, dp, ta=True, tk=1024, name="branch_dw")
    dya, dyb, dyc = dys
    tmat, uu, ww, gc, beta, qd, kd, attn, vn, st = s["delta"]
    du, do, g["c_norm"] = _gated_norm_bwd_call(s["o"], s["u"], p["c_norm"], dyc, du)
    ddu, ddw, dqd, dkd, dattn, dgl = _delta_scan_bwd_call(do, ww, qd, kd, attn, gc, vn, st)
    du, dqk, dv, dpar = _delta_prep_bwd_call(s["qk"], s["v"], s["u"], p["a_log"], p["dt_bias"], tmat, uu, ww, gc,
                                             beta, ddu, ddw, dqd, dkd, dattn, dgl, du)
    g["a_log"], g["dt_bias"] = dpar[0:1], dpar[1:2]
    dzc = _conv_prep_dz_call(s["zc"], dqk, dv)
    du, dconv = _conv_bwd_call(s["u"], dzc, p["conv_w"], du)
    g["conv_w"] = dconv[:C_CONV]
    no_dlse = jnp.zeros((t, LANES), F32)
    dq, dk, dv_b, dsink = _attn_bwd(ATTN_B_CFG, s["b_saved"], p["sinks"], dyb, no_dlse)
    g["sinks"] = dsink[0, :p["sinks"].shape[0]]
    du = _rope_scatter_call(du, t, [([dq], A_W, True, Q_SCALE)], COL["b_q"], tabs, name="rope_bq_bwd")
    du = _rope_scatter_call(du, t, [([dk], B_KVW, True, 1.0), ([dv_b], B_KVW, False, 1.0)], COL["b_k"], tabs,
                            name="rope_bkv_bwd")
    *dos, dl0, dl1, dl2 = _combine_bwd_call(s["os_"], s["lses"], dya)
    grads_a = [_attn_bwd(cfg, sv, None, do_c, dl)[:3]
               for cfg, sv, do_c, dl in zip(ATTN_A_CFGS, s["a_saved"], dos, (dl0, dl1, dl2))]
    dqs, dks, dvs = zip(*grads_a)
    du = _rope_scatter_call(du, t, [(list(dqs), A_W, True, Q_SCALE), (list(dks), A_W, True, 1.0),
                                    (list(dvs), A_W, False, 1.0)],
                            COL["a_q"], tabs, name="rope_a_bwd")
    dh = _mm(du, wb["w_in"], tb=True, tk=IN_TN, name="in_proj_da")
    g["w_in"], g["b_in"] = _mm(s["h"], du, ta=True, b_colsum=True, tn=IN_TN, tk=1024, name="in_proj_dw")
    dx, g["norm_mix"] = _rms_bwd_call(s["x"], p["norm_mix"], dh, add=dx1, name="rms_mix_bwd")
    return dx, g


def _local_step(x, params, first_gathered, payload_of_layer, weights_of_layer, tabs, tgt):
    saves, gathered = [], first_gathered
    for layer in range(DEPTH):
        wb, conv_w = weights_of_layer(layer, gathered)
        p = {n: w[layer] for n, w in params.items() if n != "norm_final"}
        p["conv_w"] = conv_w
        nxt = payload_of_layer(layer + 1) if layer + 1 < DEPTH else None
        x, s, gathered = _layer_fwd(x, tabs, p, wb, nxt)
        saves.append((s, p, wb))
    loss = _loss_fwd_call(x, params["norm_final"], tgt)
    dx, dfinal = _loss_bwd_call(x, params["norm_final"], tgt, jnp.ones((1, 1), F32))
    per_layer = []
    for s, p, wb in reversed(saves):
        dx, g = _layer_bwd(s, dx, tabs, p, wb)
        per_layer.append(g)
    per_layer.reverse()
    grads = {n: jnp.stack([g[n] for g in per_layer]) for n in per_layer[0]}
    grads["norm_final"] = dfinal
    return loss, dx, grads


def _in_cols_to_kernel(w):
    lead = w.shape[:-1]
    parts, pos = [], 0
    for _, start, width, ref_start in IN_LAYOUT:
        if start > pos:
            parts.append(jnp.zeros(lead + (start - pos,), w.dtype))
        parts.append(w[..., ref_start:ref_start + width])
        pos = start + width
    parts.append(jnp.zeros(lead + (D_IN_PAD - pos,), w.dtype))
    return jnp.concatenate(parts, axis=-1)


def _in_cols_to_reference(w):
    by_ref = sorted(IN_LAYOUT, key=lambda e: e[3])
    return jnp.concatenate([w[..., start:start + width] for _, start, width, _ in by_ref], axis=-1)


W_IN_SHARD = 8464 // N_DEV


def _w_in_from_shards(blocks):
    lead = blocks.shape[1:-1]
    parts, pos = [], 0
    for _, start, width, ref_start in IN_LAYOUT:
        if start > pos:
            parts.append(jnp.zeros(lead + (start - pos,), blocks.dtype))
        col = ref_start
        while col < ref_start + width:
            d, l = divmod(col, W_IN_SHARD)
            n = min(W_IN_SHARD - l, ref_start + width - col)
            parts.append(blocks[d, ..., l:l + n])
            col += n
        pos = start + width
    parts.append(jnp.zeros(lead + (D_IN_PAD - pos,), blocks.dtype))
    return jnp.concatenate(parts, axis=-1)


def _w_in_to_shards(g):
    by_ref = sorted(IN_LAYOUT, key=lambda e: e[3])
    blocks = []
    for d in range(N_DEV):
        lo, hi = d * W_IN_SHARD, (d + 1) * W_IN_SHARD
        parts = []
        for _, start, width, ref_start in by_ref:
            a, b = max(lo, ref_start), min(hi, ref_start + width)
            if a < b:
                parts.append(g[..., start + a - ref_start:start + b - ref_start])
        blocks.append(jnp.concatenate(parts, axis=-1))
    return jnp.stack(blocks)


def _pad_lanes(v):
    return jnp.pad(v, ((0, 0), (0, LANES - v.shape[1])))[:, None, :]


BIG = (("w_in", 2), ("conv_w", 2), ("w_branch_a", 2), ("w_branch_b", 2), ("w_branch_c", 1), ("w_out", 1),
       ("w_ff1", 2), ("w_ff2", 1))
SMALL = ("norm_mix", "b_in", "a_log", "dt_bias", "sinks", "c_norm", "norm_ffn", "norm_final")
WEIGHTS = ("norm_mix", "w_in", "b_in", "conv_w", "a_log", "dt_bias", "sinks", "c_norm", "w_branch_a",
           "w_branch_b", "w_branch_c", "w_out", "norm_ffn", "w_ff1", "w_ff2", "norm_final")
MATMUL_WEIGHTS = ("w_in", "w_branch_a", "w_branch_b", "w_branch_c", "w_out", "w_ff1", "w_ff2")
PACK_ROWS = 1024
ROW_ALIGN = 16


def _seg_rows(n):
    return -(-n // (LANES * ROW_ALIGN)) * ROW_ALIGN


def _pack(arrays, lead=0):
    parts = []
    for a in arrays:
        lead_shape = a.shape[:lead]
        n = math.prod(a.shape[lead:])
        rows = _seg_rows(n)
        if rows * LANES != n:
            a = jnp.pad(a.reshape(lead_shape + (n,)), [(0, 0)] * lead + [(0, rows * LANES - n)])
        parts.append(a.reshape(lead_shape + (rows, LANES)))
    total = sum(p.shape[lead] for p in parts)
    padded = -(-total // PACK_ROWS) * PACK_ROWS
    if padded > total:
        parts.append(jnp.zeros(parts[0].shape[:lead] + (padded - total, LANES), parts[0].dtype))
    return jnp.concatenate(parts, axis=lead)


def _unpack(buf, shapes):
    lead = buf.shape[:-2]
    out, pos = [], 0
    for shp in shapes:
        n = math.prod(shp)
        rows = _seg_rows(n)
        seg = buf[..., pos:pos + rows, :]
        if rows * LANES != n:
            seg = seg.reshape(lead + (rows * LANES,))[..., :n]
        out.append(seg.reshape(lead + tuple(shp)))
        pos += rows
    return out


def _shards_to_full(blocks, axis):
    moved = jnp.moveaxis(blocks, 0, axis)
    shp = list(blocks.shape[1:])
    shp[axis] = shp[axis] * N_DEV
    return moved.reshape(shp)


def _full_to_shards(full, axis):
    shp = list(full.shape)
    shp[axis:axis + 1] = [N_DEV, shp[axis] // N_DEV]
    return jnp.moveaxis(full.reshape(shp), axis, 0)


def _my_place():
    return lax.axis_index("x"), lax.axis_index("y"), lax.axis_index("c")


def _slot(x, y, c):
    return 4 * x + 2 * y + c


def _gather_plan(x_ref, out_ref, send_sems, recv_sems, local_sem):
    x, y, c = _my_place()
    me, sibling = (x, y, c), (x, y, 1 - c)
    chips = [(1 - x, y), (x, 1 - y), (1 - x, 1 - y)]

    def copy(k, blk, to, src=None):
        dst = out_ref.at[_slot(*blk)]
        return pltpu.make_async_remote_copy(
            src_ref=dst if src is None else src, dst_ref=dst,
            send_sem=send_sems.at[k], recv_sem=recv_sems.at[k], device_id=to, device_id_type=MESH_ID)

    def own():
        mine = pltpu.make_async_copy(x_ref, out_ref.at[_slot(*me)], local_sem)
        return mine, [copy(0, me, sibling, src=x_ref)] + [copy(1 + j, me, (*chip, c), src=x_ref)
                                                          for j, chip in enumerate(chips)]

    return copy, own, me, sibling, chips, c


def _gather_start(*refs):
    _, own, *_ = _gather_plan(*refs)
    mine, first = own()
    mine.start()
    for cp in first:
        cp.start()


def _gather_finish(*refs):
    copy, own, me, sibling, chips, c = _gather_plan(*refs)
    passed = [copy(4 + j, (*chip, c), sibling) for j, chip in enumerate(chips)]
    for j, chip in enumerate(chips):
        copy(1 + j, (*chip, c), me).wait_recv()
        passed[j].start()
    copy(0, sibling, me).wait_recv()
    for j, chip in enumerate(chips):
        copy(4 + j, (*chip, 1 - c), me).wait_recv()
    mine, first = own()
    for cp in first + passed:
        cp.wait_send()
    mine.wait()


GATHER_SEMS = [pltpu.SemaphoreType.DMA((7,)), pltpu.SemaphoreType.DMA((7,)), pltpu.SemaphoreType.DMA]


def _all_gather(block, *, name):
    rows = block.shape[0]

    def body(x_ref, out_ref, send_sems, recv_sems, local_sem):
        _gather_start(x_ref, out_ref, send_sems, recv_sems, local_sem)
        _gather_finish(x_ref, out_ref, send_sems, recv_sems, local_sem)

    return pl.pallas_call(
        body, name=name,
        out_shape=jax.ShapeDtypeStruct((N_DEV, rows, LANES), block.dtype),
        in_specs=[HBM_SPEC], out_specs=HBM_SPEC,
        scratch_shapes=list(GATHER_SEMS),
    )(block)


N_CHIP = N_DEV // 2


def _swap_with_sibling(blocks, *, name):
    def body(g_ref, out_ref, send_sem, recv_sem):
        x, y, c = _my_place()
        cp = pltpu.make_async_remote_copy(src_ref=g_ref, dst_ref=out_ref, send_sem=send_sem, recv_sem=recv_sem,
                                          device_id=(x, y, 1 - c), device_id_type=MESH_ID)
        cp.start()
        cp.wait_recv()
        cp.wait_send()

    return pl.pallas_call(
        body, name=name,
        out_shape=jax.ShapeDtypeStruct(blocks.shape, blocks.dtype),
        in_specs=[HBM_SPEC], out_specs=HBM_SPEC,
        scratch_shapes=[pltpu.SemaphoreType.DMA, pltpu.SemaphoreType.DMA],
    )(blocks)


def _chip_all_to_all(blocks, *, name):
    def body(g_ref, out_ref, send_sems, recv_sems, local_sem):
        x, y, c = _my_place()
        mine_slot = 2 * x + y
        local = pltpu.make_async_copy(g_ref.at[mine_slot], out_ref.at[mine_slot], local_sem)
        local.start()
        copies = []
        for k in range(1, N_CHIP):
            px, py = x ^ (k >> 1), y ^ (k & 1)
            copies.append(pltpu.make_async_remote_copy(
                src_ref=g_ref.at[2 * px + py], dst_ref=out_ref.at[mine_slot],
                send_sem=send_sems.at[k - 1], recv_sem=recv_sems.at[k - 1],
                device_id=(px, py, c), device_id_type=MESH_ID))
        for cp in copies:
            cp.start()
        for cp in copies:
            cp.wait_recv()
        for cp in copies:
            cp.wait_send()
        local.wait()

    return pl.pallas_call(
        body, name=name,
        out_shape=jax.ShapeDtypeStruct(blocks.shape, blocks.dtype),
        in_specs=[HBM_SPEC], out_specs=HBM_SPEC,
        scratch_shapes=[pltpu.SemaphoreType.DMA((N_CHIP - 1,)), pltpu.SemaphoreType.DMA((N_CHIP - 1,)),
                        pltpu.SemaphoreType.DMA],
    )(blocks)


def _add_bf16_call(a, b, *, name):
    n, rows, _ = a.shape
    tr = min(PACK_ROWS, rows)

    def body(a_ref, b_ref, o_ref):
        o_ref[...] = (a_ref[...].astype(F32) + b_ref[...].astype(F32)).astype(BF16)

    blk = pl.BlockSpec((n, tr, LANES), lambda i: (0, i, 0))
    return pl.pallas_call(
        body, name=name, grid=(rows // tr,), in_specs=[blk, blk], out_specs=blk,
        out_shape=jax.ShapeDtypeStruct(a.shape, BF16),
        compiler_params=_cparams(("parallel",)),
    )(a, b)


def _adamw_call(parts, w, m, v, *, name):
    rows = w.shape[0]
    tr = min(PACK_ROWS, rows)
    assert rows % tr == 0
    n_parts = parts.shape[0]

    def body(p_ref, w_ref, m_ref, v_ref, g_ref, d_ref, nm_ref, nv_ref):
        g = p_ref[0].astype(F32)
        for s in range(1, n_parts):
            g = g + p_ref[s].astype(F32)
        nm = ADAM_B1 * m_ref[...] + (1.0 - ADAM_B1) * g
        nv = ADAM_B2 * v_ref[...] + (1.0 - ADAM_B2) * jnp.square(g)
        m_hat = nm / (1.0 - ADAM_B1 ** ADAM_STEP)
        v_hat = nv / (1.0 - ADAM_B2 ** ADAM_STEP)
        g_ref[...] = g
        nm_ref[...] = nm
        nv_ref[...] = nv
        d_ref[...] = -ADAM_LR * (m_hat / (jnp.sqrt(v_hat) + ADAM_EPS) + ADAM_WD * w_ref[...])

    blk = pl.BlockSpec((tr, LANES), lambda i: (i, 0))
    shape = jax.ShapeDtypeStruct((rows, LANES), F32)
    return pl.pallas_call(
        body, name=name, grid=(rows // tr,),
        in_specs=[pl.BlockSpec((n_parts, tr, LANES), lambda i: (0, i, 0)), blk, blk, blk],
        out_specs=[blk] * 4, out_shape=[shape] * 4,
        compiler_params=_cparams(("parallel",)),
    )(parts, w, m, v)


def _kernel_params(full):
    return {
        "norm_mix": full["norm_mix"][:, None, :],
        "b_in": _in_cols_to_kernel(full["b_in"])[:, None, :],
        "a_log": _pad_lanes(full["a_log"]),
        "dt_bias": _pad_lanes(full["dt_bias"]),
        "sinks": full["sinks"],
        "c_norm": full["c_norm"][:, None, :],
        "norm_ffn": full["norm_ffn"][:, None, :],
        "norm_final": full["norm_final"][None, :],
    }


def _reference_grads(g):
    return {
        "norm_mix": g["norm_mix"][:, 0, :],
        "b_in": _in_cols_to_reference(g["b_in"][:, 0, :]),
        "conv_w": g["conv_w"],
        "a_log": g["a_log"][:, 0, :C_V_HEADS],
        "dt_bias": g["dt_bias"][:, 0, :C_V_HEADS],
        "sinks": g["sinks"],
        "c_norm": g["c_norm"][:, 0, :],
        "w_branch_a": g["w_branch_a"], "w_branch_b": g["w_branch_b"], "w_branch_c": g["w_branch_c"],
        "w_out": g["w_out"],
        "norm_ffn": g["norm_ffn"][:, 0, :],
        "w_ff1": g["w_ff1"], "w_ff2": g["w_ff2"],
        "norm_final": g["norm_final"][0],
    }


def kernel(x, positions, norm_mix, w_in, b_in, conv_w, a_log, dt_bias, sinks, c_norm, w_branch_a, w_branch_b, w_branch_c, w_out, norm_ffn, w_ff1, w_ff2, norm_final, loss_target, m_norm_mix, m_w_in, m_b_in, m_conv_w, m_a_log, m_dt_bias, m_sinks, m_c_norm, m_w_branch_a, m_w_branch_b, m_w_branch_c, m_w_out, m_norm_ffn, m_w_ff1, m_w_ff2, m_norm_final, v_norm_mix, v_w_in, v_b_in, v_conv_w, v_a_log, v_dt_bias, v_sinks, v_c_norm, v_w_branch_a, v_w_branch_b, v_w_branch_c, v_w_out, v_norm_ffn, v_w_ff1, v_w_ff2, v_norm_final):
    env = dict(locals())
    weights = {n: env[n] for n in WEIGHTS}
    moments_m = {n: env["m_" + n] for n in WEIGHTS}
    moments_v = {n: env["v_" + n] for n in WEIGHTS}

    axis_of = {n: axis - 1 for n, axis in BIG}

    def payload_of_layer(layer):
        cw = weights["conv_w"][layer]
        c1 = cw.astype(BF16)
        c2 = (cw - c1.astype(F32)).astype(BF16)
        c3 = (cw - c1.astype(F32) - c2.astype(F32)).astype(BF16)
        return _pack([weights[n][layer].astype(BF16) for n in MATMUL_WEIGHTS] + [c1, c2, c3])

    def weights_of_layer(layer, gathered):
        shapes = [weights[n].shape[1:] for n in MATMUL_WEIGHTS] + [weights["conv_w"].shape[1:]] * 3
        blocks = _unpack(gathered, shapes)
        wb = {n: _w_in_from_shards(blk) if n == "w_in" else _shards_to_full(blk, axis_of[n])
              for n, blk in zip(MATMUL_WEIGHTS, blocks)}
        return wb, _shards_to_full(sum(b.astype(F32) for b in blocks[-3:]), axis_of["conv_w"])

    tabs = rope_tables(positions[0])
    first = _all_gather(payload_of_layer(0), name="gather_weights")
    loss, dx, dparams = _local_step(x[0], _kernel_params({n: weights[n] for n in SMALL}), first,
                                    payload_of_layer, weights_of_layer, tabs, loss_target[0])
    grads = _reference_grads(dparams)
    loss = lax.psum(loss, ("x", "y", "c"))

    core = lax.axis_index("c")

    def by_core(n, axis, which):
        sh = _w_in_to_shards(dparams[n]) if n == "w_in" else _full_to_shards(grads[n], axis)
        sh = sh.reshape((N_CHIP, 2) + sh.shape[1:])
        return lax.dynamic_index_in_dim(sh, which, axis=1, keepdims=False).astype(BF16)

    from_sibling = _swap_with_sibling(_pack([by_core(n, axis, 1 - core) for n, axis in BIG], lead=1),
                                      name="scatter_grads_d2d")
    chip_sum = _add_bf16_call(_pack([by_core(n, axis, core) for n, axis in BIG], lead=1), from_sibling,
                              name="scatter_grads_add")
    big_parts = _chip_all_to_all(chip_sum, name="scatter_grads_ici")
    small_parts = _all_gather(_pack([grads[n] for n in SMALL]), name="gather_small_grads")

    out = {}
    for names, parts in (([n for n, _ in BIG], big_parts), (list(SMALL), small_parts)):
        shapes = [weights[n].shape for n in names]
        packed = [_pack([d[n] for n in names]) for d in (weights, moments_m, moments_v)]
        results = _adamw_call(parts, *packed, name="adamw_" + names[0])
        for kind, buf in zip(("grad", "delta", "new_m", "new_v"), results):
            for n, arr in zip(names, _unpack(buf, shapes)):
                out[kind, n] = arr
    return (loss, dx[None], *[out[kind, n] for kind in ("grad", "delta", "new_m", "new_v") for n in WEIGHTS])
```

```python
import math

import jax
import jax.numpy as jnp
from jax import lax
from jax.experimental import pallas as pl
from jax.experimental.pallas import tpu as pltpu

F32 = jnp.float32
BF16 = jnp.bfloat16

N_DEV = 8
D_MODEL = 1024
DEPTH = 2
HEAD_DIM = 64
ROT_DIM = 16
ROPE_THETA = 500000.0
BLK = 128
NEG_INF = -1e30
EPS = 1e-6
A_CONFIGS = ((128, 1), (512, 4), (2048, 16))
B_GROUP = 4
C_QK_HEADS = 4
C_V_HEADS = 8
C_DK = 128
C_CONV = 4
CHUNK = 64
ADAM_LR = 0.001
ADAM_B1 = 0.9
ADAM_B2 = 0.999
ADAM_EPS = 1e-08
ADAM_WD = 0.01
ADAM_STEP = 10

IN_LAYOUT = (
    ("gate_a", 0, 1024, 5392), ("gate_b", 1024, 1024, 6416), ("gate_c", 2048, 1024, 7440),
    ("a_q", 3072, 512, 0), ("a_k", 3584, 512, 512), ("a_v", 4096, 512, 1024), ("b_q", 4608, 512, 1536),
    ("c_z", 5120, 1024, 4352), ("c_qkv", 6144, 2048, 2304),
    ("b_k", 8192, 128, 2048), ("b_v", 8320, 128, 2176), ("c_ab", 8448, 16, 5376),
)
COL = {name: start for name, start, _, _ in IN_LAYOUT}
D_IN_PAD = 8704
IN_TN = D_IN_PAD // 4
LANES = 128
VMEM_LIMIT = 56 * 1024 * 1024


def _cparams(sem=None):
    return pltpu.CompilerParams(dimension_semantics=sem, vmem_limit_bytes=VMEM_LIMIT)


def _relu2(t):
    return jnp.square(jnp.maximum(t, 0.0))


def _mm(a, b, *, ta=False, tb=False, bias=None, a_fn=None, mul_drelu2=None, add=None,
        out_dtype=F32, relu2_out=False, b_colsum=False, tm=1024, tn=1024, tk=2048, name):
    if ta:
        kdim, m = a.shape
    else:
        m, kdim = a.shape
    n = b.shape[0] if tb else b.shape[1]
    tm, tn, tk = min(tm, m), min(tn, n), min(tk, kdim)
    assert m % tm == 0 and n % tn == 0 and kdim % tk == 0, (a.shape, b.shape, tm, tn, tk)
    nk = kdim // tk
    assert not b_colsum or (m == tm and not tb and nk > 1)
    dims = (((0 if ta else 1,), (1 if tb else 0,)), ((), ()))
    extras = [e for e in (bias, mul_drelu2, add) if e is not None]

    def body(*refs):
        a_ref, b_ref = refs[0], refs[1]
        pos = 2
        bias_ref = pre_ref = add_ref = None
        if bias is not None:
            bias_ref = refs[pos]; pos += 1
        if mul_drelu2 is not None:
            pre_ref = refs[pos]; pos += 1
        if add is not None:
            add_ref = refs[pos]; pos += 1
        o_ref = refs[pos]
        pos += 1
        r_ref = None
        if relu2_out:
            r_ref = refs[pos]; pos += 1
        cs_ref = None
        if b_colsum:
            cs_ref = refs[pos]; pos += 1
        acc_ref = refs[pos] if nk > 1 else None
        cs_acc = refs[pos + 1] if b_colsum else None

        av = a_ref[...]
        if a_fn is not None:
            av = a_fn(av)
        bv = b_ref[...]
        part = lax.dot_general(av.astype(BF16), bv.astype(BF16), dims,
                               preferred_element_type=F32)
        if b_colsum:
            cs_part = jnp.sum(bv.astype(F32).reshape(tk // 8, 8, tn), axis=0)

        def finish(acc):
            if bias_ref is not None:
                acc = acc + bias_ref[...]
            if pre_ref is not None:
                acc = acc * (2.0 * jnp.maximum(pre_ref[...], 0.0))
            if add_ref is not None:
                acc = acc + add_ref[...]
            o_ref[...] = acc.astype(out_dtype)
            if r_ref is not None:
                r_ref[...] = _relu2(acc).astype(BF16)

        if nk == 1:
            finish(part)
        else:
            k = pl.program_id(2)

            @pl.when(k == 0)
            def _():
                acc_ref[...] = part
                if b_colsum:
                    cs_acc[...] = cs_part

            @pl.when(k > 0)
            def _():
                acc_ref[...] += part
                if b_colsum:
                    cs_acc[...] += cs_part

            @pl.when(k == nk - 1)
            def _():
                finish(acc_ref[...])
                if b_colsum:
                    cs_ref[...] = jnp.sum(cs_acc[...], axis=0, keepdims=True)

    a_spec = (pl.BlockSpec((tk, tm), lambda i, j, k: (k, i)) if ta
              else pl.BlockSpec((tm, tk), lambda i, j, k: (i, k)))
    b_spec = (pl.BlockSpec((tn, tk), lambda i, j, k: (j, k)) if tb
              else pl.BlockSpec((tk, tn), lambda i, j, k: (k, j)))
    in_specs = [a_spec, b_spec]
    if bias is not None:
        in_specs.append(pl.BlockSpec((1, tn), lambda i, j, k: (0, j)))
    for _ in extras[(1 if bias is not None else 0):]:
        in_specs.append(pl.BlockSpec((tm, tn), lambda i, j, k: (i, j)))
    o_spec = pl.BlockSpec((tm, tn), lambda i, j, k: (i, j))
    out_specs, out_shape = [o_spec], [jax.ShapeDtypeStruct((m, n), out_dtype)]
    scratch = [pltpu.VMEM((tm, tn), F32)] if nk > 1 else []
    if relu2_out:
        out_specs.append(o_spec)
        out_shape.append(jax.ShapeDtypeStruct((m, n), BF16))
    if b_colsum:
        out_specs.append(pl.BlockSpec((1, tn), lambda i, j, k: (0, j)))
        out_shape.append(jax.ShapeDtypeStruct((1, n), F32))
        scratch.append(pltpu.VMEM((8, tn), F32))
    single = len(out_specs) == 1
    return pl.pallas_call(
        body, name=name,
        grid=(m // tm, n // tn, nk),
        in_specs=in_specs,
        out_specs=out_specs[0] if single else out_specs,
        out_shape=out_shape[0] if single else out_shape,
        scratch_shapes=scratch,
        compiler_params=_cparams(("parallel", "parallel", "arbitrary")),
    )(a, b, *extras)


def _rms_fwd_call(x, g, *, name, out_dtype=F32, tq=512):
    t, d = x.shape

    def body(x_ref, g_ref, y_ref):
        xv = x_ref[...]
        r = lax.rsqrt(jnp.mean(xv * xv, axis=-1, keepdims=True) + EPS)
        y_ref[...] = (xv * r * g_ref[...]).astype(out_dtype)

    return pl.pallas_call(
        body, name=name, grid=(t // tq,),
        in_specs=[pl.BlockSpec((tq, d), lambda i: (i, 0)), pl.BlockSpec((1, d), lambda i: (0, 0))],
        out_specs=pl.BlockSpec((tq, d), lambda i: (i, 0)),
        out_shape=jax.ShapeDtypeStruct((t, d), out_dtype),
        compiler_params=_cparams(("parallel",)),
    )(x, g)


def _rms_bwd_call(x, g, dy, *, name, add=None, tq=512):
    t, d = x.shape
    nt = t // tq

    def body(*refs):
        if add is None:
            x_ref, g_ref, dy_ref, dx_ref, dg_ref, acc_ref = refs
        else:
            x_ref, g_ref, dy_ref, add_ref, dx_ref, dg_ref, acc_ref = refs
        i = pl.program_id(0)
        xv = x_ref[...]
        r = lax.rsqrt(jnp.mean(xv * xv, axis=-1, keepdims=True) + EPS)
        xh = xv * r
        dyv = dy_ref[...]
        dxh = dyv * g_ref[...]
        dx = r * (dxh - xh * jnp.mean(dxh * xh, axis=-1, keepdims=True))
        dx_ref[...] = dx if add is None else dx + add_ref[...]
        part = jnp.sum((dyv * xh).reshape(tq // 8, 8, d), axis=0)

        @pl.when(i == 0)
        def _():
            acc_ref[...] = part

        @pl.when(i > 0)
        def _():
            acc_ref[...] += part

        @pl.when(i == nt - 1)
        def _():
            dg_ref[...] = jnp.sum(acc_ref[...], axis=0, keepdims=True)

    blk = pl.BlockSpec((tq, d), lambda i: (i, 0))
    row = pl.BlockSpec((1, d), lambda i: (0, 0))
    extra = [] if add is None else [add]
    return pl.pallas_call(
        body, name=name, grid=(nt,),
        in_specs=[blk, row, blk] + [blk] * len(extra),
        out_specs=[blk, row],
        out_shape=[jax.ShapeDtypeStruct((t, d), F32), jax.ShapeDtypeStruct((1, d), F32)],
        scratch_shapes=[pltpu.VMEM((8, d), F32)],
        compiler_params=_cparams(("arbitrary",)),
    )(x, g, dy, *extra)


def _loss_fwd_call(x, g, tgt, *, tq=512):
    t, d = x.shape
    nt = t // tq

    def body(x_ref, g_ref, t_ref, o_ref, acc_ref):
        i = pl.program_id(0)
        xv = x_ref[...]
        r = lax.rsqrt(jnp.mean(xv * xv, axis=-1, keepdims=True) + EPS)
        err = xv * r * g_ref[...] - t_ref[...]
        part = jnp.sum((err * err).reshape(tq // 8, 8, d), axis=0)

        @pl.when(i == 0)
        def _():
            acc_ref[...] = part

        @pl.when(i > 0)
        def _():
            acc_ref[...] += part

        @pl.when(i == nt - 1)
        def _():
            tot = jnp.sum(jnp.sum(acc_ref[...], axis=0, keepdims=True), axis=1, keepdims=True)
            o_ref[...] = jnp.broadcast_to(tot * (0.5 / d), (8, LANES))

    out = pl.pallas_call(
        body, name="loss_fwd", grid=(nt,),
        in_specs=[pl.BlockSpec((tq, d), lambda i: (i, 0)), pl.BlockSpec((1, d), lambda i: (0, 0)),
                  pl.BlockSpec((tq, d), lambda i: (i, 0))],
        out_specs=pl.BlockSpec((8, LANES), lambda i: (0, 0)),
        out_shape=jax.ShapeDtypeStruct((8, LANES), F32),
        scratch_shapes=[pltpu.VMEM((8, d), F32)],
        compiler_params=_cparams(("arbitrary",)),
    )(x, g, tgt)
    return out[0, 0]


def _loss_bwd_call(x, g, tgt, scale, *, tq=512):
    t, d = x.shape
    nt = t // tq

    def body(x_ref, g_ref, t_ref, s_ref, dx_ref, dg_ref, acc_ref):
        i = pl.program_id(0)
        xv = x_ref[...]
        r = lax.rsqrt(jnp.mean(xv * xv, axis=-1, keepdims=True) + EPS)
        xh = xv * r
        gv = g_ref[...]
        dyv = (xh * gv - t_ref[...]) * (s_ref[...] * (1.0 / d))
        dxh = dyv * gv
        dx_ref[...] = r * (dxh - xh * jnp.mean(dxh * xh, axis=-1, keepdims=True))
        part = jnp.sum((dyv * xh).reshape(tq // 8, 8, d), axis=0)

        @pl.when(i == 0)
        def _():
            acc_ref[...] = part

        @pl.when(i > 0)
        def _():
            acc_ref[...] += part

        @pl.when(i == nt - 1)
        def _():
            dg_ref[...] = jnp.sum(acc_ref[...], axis=0, keepdims=True)

    return pl.pallas_call(
        body, name="loss_bwd", grid=(nt,),
        in_specs=[pl.BlockSpec((tq, d), lambda i: (i, 0)), pl.BlockSpec((1, d), lambda i: (0, 0)),
                  pl.BlockSpec((tq, d), lambda i: (i, 0)), pl.BlockSpec((1, 1), lambda i: (0, 0))],
        out_specs=[pl.BlockSpec((tq, d), lambda i: (i, 0)), pl.BlockSpec((1, d), lambda i: (0, 0))],
        out_shape=[jax.ShapeDtypeStruct((t, d), F32), jax.ShapeDtypeStruct((1, d), F32)],
        scratch_shapes=[pltpu.VMEM((8, d), F32)],
        compiler_params=_cparams(("arbitrary",)),
    )(x, g, tgt, scale)


MESH_ID = pl.DeviceIdType.MESH
HBM_SPEC = pl.BlockSpec(memory_space=pl.ANY)


def rope_tables(positions):
    inv_freq = jnp.power(ROPE_THETA, -jnp.arange(0, ROT_DIM, 2, dtype=F32) / ROT_DIM)
    ang = positions.astype(F32)[:, None] * inv_freq
    cos, sin = jnp.cos(ang), jnp.sin(ang)
    t = positions.shape[0]
    one = jnp.ones((t, HEAD_DIM - ROT_DIM), F32)
    zero8 = jnp.zeros((t, ROT_DIM // 2), F32)
    zero = jnp.zeros((t, HEAD_DIM - ROT_DIM), F32)
    a = jnp.concatenate([cos, cos, one], axis=1)
    b = jnp.concatenate([zero8, sin, zero], axis=1)
    c = jnp.concatenate([-sin, zero8, zero], axis=1)
    return tuple(jnp.concatenate([m, m], axis=1) for m in (a, b, c))


def _rope_chunk(xs, a, b, c, transpose):
    half = ROT_DIM // 2
    if transpose:
        return xs * a + pltpu.roll(xs * b, LANES - half, 1) + pltpu.roll(xs * c, half, 1)
    return xs * a + pltpu.roll(xs, half, 1) * b + pltpu.roll(xs, LANES - half, 1) * c


def _rope_gather_call(u, tabs, parts, *, name, tq=512):
    t = u.shape[0]
    total = sum(w for _, w, _, _ in parts)
    assert all(start % w == 0 for start, w, _, _ in parts)

    def body(a_ref, b_ref, c_ref, *refs):
        o_ref = refs[-1]
        a, b, c = a_ref[...], b_ref[...], c_ref[...]
        off = 0
        for x_ref, (_, w, roped, scale) in zip(refs[:-1], parts):
            for j in range(w // LANES):
                xs = x_ref[:, j * LANES:(j + 1) * LANES]
                val = _rope_chunk(xs, a, b, c, False) if roped else xs
                o_ref[:, off + j * LANES:off + (j + 1) * LANES] = (val * scale if scale != 1.0 else val).astype(BF16)
            off += w

    tab_spec = pl.BlockSpec((tq, LANES), lambda i: (i, 0))
    return pl.pallas_call(
        body, name=name, grid=(t // tq,),
        in_specs=[tab_spec] * 3 + [pl.BlockSpec((tq, w), lambda i, cb=start // w: (i, cb)) for start, w, _, _ in parts],
        out_specs=pl.BlockSpec((tq, total), lambda i: (i, 0)),
        out_shape=jax.ShapeDtypeStruct((t, total), BF16),
        compiler_params=_cparams(("parallel",)),
    )(*tabs, *([u] * len(parts)))


def _du_operands(du_buf, n_inputs):
    if du_buf is None:
        return [], [], {}
    return [du_buf], [HBM_SPEC], {n_inputs: 0}


def _du_shape(t):
    return jax.ShapeDtypeStruct((t, D_IN_PAD), BF16)


def _rope_scatter_call(du_buf, t, pieces, col, tabs, *, name, tq=512):
    total = sum(w for _, w, _, _ in pieces)
    assert col % total == 0
    arrays = [a for arrs, _, _, _ in pieces for a in arrs]
    extra, extra_specs, aliases = _du_operands(du_buf, 3 + len(arrays))

    def body(a_ref, b_ref, c_ref, *refs):
        o_ref = refs[len(arrays) + len(extra)]
        a, b, c = a_ref[...], b_ref[...], c_ref[...]
        k = off = 0
        for arrs, w, roped, scale in pieces:
            mine = refs[k:k + len(arrs)]
            k += len(arrs)
            for j in range(w // LANES):
                cs = slice(j * LANES, (j + 1) * LANES)
                xs = mine[0][:, cs].astype(F32)
                for r in mine[1:]:
                    xs = xs + r[:, cs].astype(F32)
                if scale != 1.0:
                    xs = xs * scale
                val = _rope_chunk(xs, a, b, c, True) if roped else xs
                o_ref[:, off + j * LANES:off + (j + 1) * LANES] = val.astype(BF16)
            off += w

    tab_spec = pl.BlockSpec((tq, LANES), lambda i: (i, 0))
    in_specs = [tab_spec] * 3 + [pl.BlockSpec((tq, w), lambda i: (i, 0)) for arrs, w, _, _ in pieces for _ in arrs]
    return pl.pallas_call(
        body, name=name, grid=(t // tq,),
        in_specs=in_specs + extra_specs,
        out_specs=pl.BlockSpec((tq, total), lambda i: (i, col // total)),
        out_shape=_du_shape(t), input_output_aliases=aliases,
        compiler_params=_cparams(("parallel",)),
    )(*tabs, *arrays, *extra)


def _band_masks(first_block, max_dist):
    qi = lax.broadcasted_iota(jnp.int32, (BLK, BLK), 0)
    kj = lax.broadcasted_iota(jnp.int32, (BLK, BLK), 1)
    valid_prev = jnp.logical_and(kj >= qi + (BLK - max_dist), jnp.logical_not(first_block))
    valid_cur = kj <= qi
    return valid_prev, valid_cur


_NN = (((1,), (0,)), ((), ()))
_NT = (((1,), (1,)), ((), ()))
_TN = (((0,), (0,)), ((), ()))


HEAD_STAGE = 8


def _attn_row_maps(nb):
    def cur(i):
        return jnp.minimum(i, nb - 1)

    def prev(i):
        return jnp.maximum(jnp.minimum(i, nb - 1) - 1, 0)

    return cur, prev


def _dil_view(a, dil):
    t, w = a.shape
    return a.reshape(t // dil, dil * w)


def _dil_spec(w, dil, rows, seg=None, off=0):
    seg = w if seg is None else seg
    assert off % w == 0 and (dil == 1 or seg % w == 0)
    return pl.BlockSpec((BLK, w), lambda r, i: (rows(i), (r * seg + off) // w))


def _dil_shape(l, dil, w, dtype=F32):
    return jax.ShapeDtypeStruct((l, dil * w), dtype)


def _attn_fwd_call(qkv2, sink, *, dil, group, max_dist, seg, offs, qw, kw, name):
    l = qkv2.shape[0]
    nh = qw // HEAD_DIM
    nb = l // BLK
    use_sink = sink is not None

    def body(*refs):
        if use_sink:
            sink_ref, refs = refs[0], refs[1:]
        q_ref, kp_ref, kc_ref, vp_ref, vc_ref, o_ref, lse_ref = refs
        valid_prev, valid_cur = _band_masks(pl.program_id(1) == 0, max_dist)
        lane = lax.broadcasted_iota(jnp.int32, (BLK, LANES), 1)
        lse_tile = jnp.zeros((BLK, LANES), F32)

        def dot(a, b, dims=_NN):
            return lax.dot_general(a, b, dims, preferred_element_type=F32)

        for g0 in range(0, nh, HEAD_STAGE):
            heads = list(range(g0, min(g0 + HEAD_STAGE, nh)))
            kv = {}
            for kh in sorted({h // group for h in heads}):
                ks = slice(kh * HEAD_DIM, (kh + 1) * HEAD_DIM)
                kv[kh] = tuple(ref[:, ks].astype(BF16) for ref in (kp_ref, kc_ref, vp_ref, vc_ref))
            qs = [q_ref[:, h * HEAD_DIM:(h + 1) * HEAD_DIM].astype(BF16) for h in heads]
            sps = [jnp.where(valid_prev, dot(qh, kv[h // group][0], _NT), NEG_INF) for h, qh in zip(heads, qs)]
            scs = [jnp.where(valid_cur, dot(qh, kv[h // group][1], _NT), NEG_INF) for h, qh in zip(heads, qs)]
            ms = [jnp.maximum(jnp.max(sp, axis=1, keepdims=True), jnp.max(sc, axis=1, keepdims=True))
                  for sp, sc in zip(sps, scs)]
            if use_sink:
                ms = [jnp.maximum(m, sink_ref[h]) for h, m in zip(heads, ms)]
            pps = [jnp.exp(sp - m) for sp, m in zip(sps, ms)]
            pcs = [jnp.exp(sc - m) for sc, m in zip(scs, ms)]
            dens = [jnp.sum(pp, axis=1, keepdims=True) + jnp.sum(pc, axis=1, keepdims=True)
                    for pp, pc in zip(pps, pcs)]
            if use_sink:
                dens = [den + jnp.exp(sink_ref[h] - m) for h, den, m in zip(heads, dens, ms)]
            outs = [dot(pp.astype(BF16), kv[h // group][2]) + dot(pc.astype(BF16), kv[h // group][3])
                    for h, pp, pc in zip(heads, pps, pcs)]
            for h, o, den, m in zip(heads, outs, dens, ms):
                o_ref[:, h * HEAD_DIM:(h + 1) * HEAD_DIM] = o / den
                lse_tile = jnp.where(lane == h, m + jnp.log(den), lse_tile)
        lse_ref[...] = lse_tile

    cur, prev = _attn_row_maps(nb)
    o_spec, lse_spec = _dil_spec(qw, dil, cur), _dil_spec(LANES, dil, cur)
    in_specs = [_dil_spec(qw, dil, cur, seg, offs[0]),
                _dil_spec(kw, dil, prev, seg, offs[1]), _dil_spec(kw, dil, cur, seg, offs[1]),
                _dil_spec(kw, dil, prev, seg, offs[2]), _dil_spec(kw, dil, cur, seg, offs[2])]
    args = [qkv2] * 5
    if use_sink:
        in_specs = [pl.BlockSpec(memory_space=pltpu.SMEM)] + in_specs
        args = [sink] + args
    return pl.pallas_call(
        body, name=name, grid=(dil, nb),
        in_specs=in_specs,
        out_specs=[o_spec, lse_spec],
        out_shape=[_dil_shape(l, dil, qw), _dil_shape(l, dil, LANES)],
        compiler_params=_cparams(("parallel", "parallel")),
    )(*args)


def _attn_bwd_call(qkv2, sink, o2, lse2, do2, dlse2, *, dil, group, max_dist, seg, offs, qw, kw, name):
    l = qkv2.shape[0]
    nh = qw // HEAD_DIM
    nb = l // BLK
    use_sink = sink is not None

    def body(*refs):
        if use_sink:
            sink_ref, refs = refs[0], refs[1:]
        (q_ref, kp_ref, kc_ref, vp_ref, vc_ref, o_ref, lse_ref, do_ref, dlse_ref,
         dq_ref, dk_ref, dv_ref, dsink_ref, ck_ref, cv_ref) = refs
        step = pl.program_id(1)

        @pl.when(jnp.logical_and(pl.program_id(0) == 0, step == 0))
        def _():
            dsink_ref[...] = jnp.zeros_like(dsink_ref)

        @pl.when(step == 0)
        def _():
            ck_ref[...] = jnp.zeros_like(ck_ref)
            cv_ref[...] = jnp.zeros_like(cv_ref)

        def dot(a, b, dims=_NN):
            return lax.dot_general(a, b, dims, preferred_element_type=F32)

        @pl.when(step < nb)
        def _():
            valid_prev, valid_cur = _band_masks(step == 0, max_dist)
            row = lax.broadcasted_iota(jnp.int32, (8, LANES), 0)
            lanes8 = lax.broadcasted_iota(jnp.int32, (8, LANES), 1)
            ds_tile = jnp.zeros((8, LANES), F32)
            for g0 in range(0, nh, HEAD_STAGE):
                heads = list(range(g0, min(g0 + HEAD_STAGE, nh)))
                hss = [slice(h * HEAD_DIM, (h + 1) * HEAD_DIM) for h in heads]
                kv = {}
                for kh in sorted({h // group for h in heads}):
                    ks = slice(kh * HEAD_DIM, (kh + 1) * HEAD_DIM)
                    kv[kh] = tuple(ref[:, ks].astype(BF16) for ref in (kp_ref, kc_ref, vp_ref, vc_ref))
                qs = [q_ref[:, hs].astype(BF16) for hs in hss]
                dos = [do_ref[:, hs] for hs in hss]
                dobs = [d.astype(BF16) for d in dos]
                lses = [lse_ref[:, h:h + 1] for h in heads]
                sps = [dot(qh, kv[h // group][0], _NT) for h, qh in zip(heads, qs)]
                scs = [dot(qh, kv[h // group][1], _NT) for h, qh in zip(heads, qs)]
                dpps = [dot(dob, kv[h // group][2], _NT) for h, dob in zip(heads, dobs)]
                dpcs = [dot(dob, kv[h // group][3], _NT) for h, dob in zip(heads, dobs)]
                pps = [jnp.where(valid_prev, jnp.exp(jnp.where(valid_prev, sp, NEG_INF) - ls), 0.0)
                       for sp, ls in zip(sps, lses)]
                pcs = [jnp.where(valid_cur, jnp.exp(jnp.where(valid_cur, sc, NEG_INF) - ls), 0.0)
                       for sc, ls in zip(scs, lses)]
                deltas = [jnp.sum(d * o_ref[:, hs], axis=1, keepdims=True) for d, hs in zip(dos, hss)]
                corrs = [dlse_ref[:, h:h + 1] - dl for h, dl in zip(heads, deltas)]
                dsps = [(pp * (dp + c)).astype(BF16) for pp, dp, c in zip(pps, dpps, corrs)]
                dscs = [(pc * (dp + c)).astype(BF16) for pc, dp, c in zip(pcs, dpcs, corrs)]
                for h, hs, dsp, dsc in zip(heads, hss, dsps, dscs):
                    dq = dot(dsp, kv[h // group][0]) + dot(dsc, kv[h // group][1])
                    dq_ref[:, hs] = dq.astype(BF16)
                parts = [(dot(dsc, qh, _TN), dot(dsp, qh, _TN),
                          dot(pc.astype(BF16), dob, _TN), dot(pp.astype(BF16), dob, _TN))
                         for dsc, dsp, qh, pc, pp, dob in zip(dscs, dsps, qs, pcs, pps, dobs)]
                for kh in kv:
                    ks = slice(kh * HEAD_DIM, (kh + 1) * HEAD_DIM)
                    mine = [p for h, p in zip(heads, parts) if h // group == kh]
                    dkc, dkp, dvc, dvp = (sum(p[j] for p in mine[1:]) + mine[0][j] for j in range(4))
                    dk_ref[:, ks] = (ck_ref[:, ks] + dkp).astype(BF16)
                    dv_ref[:, ks] = (cv_ref[:, ks] + dvp).astype(BF16)
                    ck_ref[:, ks] = dkc
                    cv_ref[:, ks] = dvc
                if use_sink:
                    for h, ls, dl in zip(heads, lses, deltas):
                        val = -jnp.sum(jnp.exp(sink_ref[h] - ls) * dl, axis=0, keepdims=True)
                        ds_tile = jnp.where(jnp.logical_and(row == 0, lanes8 == h), val, ds_tile)
            if use_sink:
                dsink_ref[...] += ds_tile

        @pl.when(step == nb)
        def _():
            dk_ref[...] = ck_ref[...].astype(BF16)
            dv_ref[...] = cv_ref[...].astype(BF16)

    cur, prev = _attn_row_maps(nb)
    q_spec, lse_spec = _dil_spec(qw, dil, cur), _dil_spec(LANES, dil, cur)
    lag_spec = _dil_spec(kw, dil, lambda i: jnp.maximum(i - 1, 0))
    in_specs = [_dil_spec(qw, dil, cur, seg, offs[0]),
                _dil_spec(kw, dil, prev, seg, offs[1]), _dil_spec(kw, dil, cur, seg, offs[1]),
                _dil_spec(kw, dil, prev, seg, offs[2]), _dil_spec(kw, dil, cur, seg, offs[2]),
                q_spec, lse_spec, q_spec, lse_spec]
    args = [qkv2] * 5 + [o2, lse2, do2, dlse2]
    if use_sink:
        in_specs = [pl.BlockSpec(memory_space=pltpu.SMEM)] + in_specs
        args = [sink] + args
    kv_shape = _dil_shape(l, dil, kw, BF16)
    return pl.pallas_call(
        body, name=name, grid=(dil, nb + 1),
        in_specs=in_specs,
        out_specs=[q_spec, lag_spec, lag_spec, pl.BlockSpec((8, LANES), lambda r, i: (0, 0))],
        out_shape=[_dil_shape(l, dil, qw, BF16), kv_shape, kv_shape,
                   jax.ShapeDtypeStruct((8, LANES), F32)],
        scratch_shapes=[pltpu.VMEM((BLK, kw), F32), pltpu.VMEM((BLK, kw), F32)],
        compiler_params=_cparams(("arbitrary", "arbitrary")),
    )(*args)


def _attn_config(tag, dil, group, max_dist, seg, offs, qw, kw):
    return dict(name=tag, dil=dil, group=group, max_dist=max_dist, seg=seg, offs=offs, qw=qw, kw=kw)


A_W = 8 * HEAD_DIM
ATTN_A_CFGS = tuple(_attn_config("attn_a%d" % dil, dil, 1, window // dil, 3 * A_W, (0, A_W, 2 * A_W), A_W, A_W)
                    for window, dil in A_CONFIGS)
B_KVW = 2 * HEAD_DIM
ATTN_B_CFG = _attn_config("attn_b", 1, B_GROUP, BLK - 1, A_W + 2 * B_KVW, (0, A_W, A_W + B_KVW), A_W, B_KVW)


def _attn_fwd(cfg, qkv, sink):
    t = qkv.shape[0]
    kw = {k: v for k, v in cfg.items() if k != "name"}
    qkv2 = _dil_view(qkv, cfg["dil"])
    o2, lse2 = _attn_fwd_call(qkv2, sink, name=cfg["name"] + "_fwd", **kw)
    return o2.reshape(t, cfg["qw"]), lse2.reshape(t, LANES), (qkv2, o2, lse2)


def _attn_bwd(cfg, saved, sink, do, dlse):
    qkv2, o2, lse2 = saved
    t = do.shape[0]
    kw = {k: v for k, v in cfg.items() if k != "name"}
    dq2, dk2, dv2, dsink = _attn_bwd_call(qkv2, sink, o2, lse2, _dil_view(do, cfg["dil"]),
                                          _dil_view(dlse, cfg["dil"]), name=cfg["name"] + "_bwd", **kw)
    return dq2.reshape(t, cfg["qw"]), dk2.reshape(t, cfg["kw"]), dv2.reshape(t, cfg["kw"]), dsink


def _head_expand():
    r = lax.broadcasted_iota(jnp.int32, (LANES, 8 * HEAD_DIM), 0)
    c = lax.broadcasted_iota(jnp.int32, (LANES, 8 * HEAD_DIM), 1)
    return (c // HEAD_DIM == r).astype(F32)


def _combine_weights(l0, l1, l2):
    m = jnp.maximum(jnp.maximum(l0, l1), l2)
    e0, e1, e2 = jnp.exp(l0 - m), jnp.exp(l1 - m), jnp.exp(l2 - m)
    inv = 1.0 / (e0 + e1 + e2)
    return e0 * inv, e1 * inv, e2 * inv


def _combine_fwd_call(os_, lses, *, tq=256):
    t, w = os_[0].shape

    def body(o0, o1, o2, l0, l1, l2, y_ref):
        ws = _combine_weights(l0[...], l1[...], l2[...])
        e = _head_expand()
        y = jnp.zeros((tq, w), F32)
        for o_ref, wt in zip((o0, o1, o2), ws):
            y = y + _dot_mask(e, wt, mask_left=False) * o_ref[...]
        y_ref[...] = y

    o_spec = pl.BlockSpec((tq, w), lambda i: (i, 0))
    l_spec = pl.BlockSpec((tq, LANES), lambda i: (i, 0))
    return pl.pallas_call(
        body, name="combine_fwd", grid=(t // tq,),
        in_specs=[o_spec] * 3 + [l_spec] * 3, out_specs=o_spec,
        out_shape=jax.ShapeDtypeStruct((t, w), F32),
        compiler_params=_cparams(("parallel",)),
    )(*os_, *lses)


def _combine_bwd_call(os_, lses, dy, *, tq=256):
    t, w = dy.shape

    def body(o0, o1, o2, l0, l1, l2, dy_ref, do0, do1, do2, dl0, dl1, dl2):
        ws = _combine_weights(l0[...], l1[...], l2[...])
        e = _head_expand()
        dyv = dy_ref[...]
        dws = []
        for o_ref, do_ref, wt in zip((o0, o1, o2), (do0, do1, do2), ws):
            do_ref[...] = _dot_mask(e, wt, mask_left=False) * dyv
            dws.append(_dot_mask(e, dyv * o_ref[...], _NT, mask_left=False))
        mean = ws[0] * dws[0] + ws[1] * dws[1] + ws[2] * dws[2]
        for dl_ref, wt, dw in zip((dl0, dl1, dl2), ws, dws):
            dl_ref[...] = wt * (dw - mean)

    o_spec = pl.BlockSpec((tq, w), lambda i: (i, 0))
    l_spec = pl.BlockSpec((tq, LANES), lambda i: (i, 0))
    o_shape = jax.ShapeDtypeStruct((t, w), F32)
    l_shape = jax.ShapeDtypeStruct((t, LANES), F32)
    return pl.pallas_call(
        body, name="combine_bwd", grid=(t // tq,),
        in_specs=[o_spec] * 3 + [l_spec] * 3 + [o_spec], out_specs=[o_spec] * 3 + [l_spec] * 3,
        out_shape=[o_shape] * 3 + [l_shape] * 3,
        compiler_params=_cparams(("parallel",)),
    )(*os_, *lses, dy)


C_QKW = C_QK_HEADS * C_DK
C_CONV_W = 2 * C_QKW + C_V_HEADS * C_DK
HALO = 8


def _silu_parts(z):
    sig = jax.nn.sigmoid(z)
    return z * sig, sig * (1.0 + z * (1.0 - sig))


def _conv_window_specs(tq, t):
    c = C_CONV_W
    cb = COL["c_qkv"] // c
    blk = pl.BlockSpec((tq, c), lambda i: (i, cb))
    before = pl.BlockSpec((HALO, c), lambda i: (jnp.maximum(i * (tq // HALO) - 1, 0), cb))
    return c, cb, blk, before


def _conv_prep_fwd_call(u, w, *, tq=512):
    t = u.shape[0]
    c, _, x_spec, halo_spec = _conv_window_specs(tq, t)
    nqk = 2 * C_QK_HEADS

    def body(x_ref, halo_ref, w_ref, z_ref, qk_ref, v_ref):
        i = pl.program_id(0)
        halo = jnp.where(i == 0, 0.0, halo_ref[...])
        xc = jnp.concatenate([halo, x_ref[...]], axis=0)
        wv = w_ref[...]
        z = xc[HALO - 3:HALO - 3 + tq] * wv[0:1]
        for j in range(1, C_CONV):
            z = z + xc[HALO - 3 + j:HALO - 3 + j + tq] * wv[j:j + 1]
        z_ref[...] = z
        act, _ = _silu_parts(z)
        for h in range(nqk):
            a = act[:, h * C_DK:(h + 1) * C_DK]
            qk_ref[:, h * C_DK:(h + 1) * C_DK] = a * lax.rsqrt(jnp.sum(a * a, axis=1, keepdims=True) + EPS)
        v_ref[...] = act[:, nqk * C_DK:]

    return pl.pallas_call(
        body, name="conv_prep_fwd", grid=(t // tq,),
        in_specs=[x_spec, halo_spec, pl.BlockSpec((C_CONV, c), lambda i: (0, 0))],
        out_specs=[pl.BlockSpec((tq, c), lambda i: (i, 0)),
                   pl.BlockSpec((tq, 2 * C_QKW), lambda i: (i, 0)),
                   pl.BlockSpec((tq, c - 2 * C_QKW), lambda i: (i, 0))],
        out_shape=[jax.ShapeDtypeStruct((t, c), F32), jax.ShapeDtypeStruct((t, 2 * C_QKW), F32),
                   jax.ShapeDtypeStruct((t, c - 2 * C_QKW), F32)],
        compiler_params=_cparams(("parallel",)),
    )(u, u, w)


def _conv_prep_dz_call(z, dqk, dv, *, tq=512):
    t, c = z.shape
    nqk = 2 * C_QK_HEADS

    def body(z_ref, dqk_ref, dv_ref, dz_ref):
        zv = z_ref[...]
        act, dact = _silu_parts(zv)
        for h in range(nqk):
            hs = slice(h * C_DK, (h + 1) * C_DK)
            a = act[:, hs]
            r = lax.rsqrt(jnp.sum(a * a, axis=1, keepdims=True) + EPS)
            nrm = a * r
            dn = dqk_ref[:, hs]
            da = r * (dn - nrm * jnp.sum(dn * nrm, axis=1, keepdims=True))
            dz_ref[:, hs] = da * dact[:, hs]
        dz_ref[:, nqk * C_DK:] = dv_ref[...] * dact[:, nqk * C_DK:]

    return pl.pallas_call(
        body, name="conv_prep_dz", grid=(t // tq,),
        in_specs=[pl.BlockSpec((tq, c), lambda i: (i, 0)),
                  pl.BlockSpec((tq, 2 * C_QKW), lambda i: (i, 0)),
                  pl.BlockSpec((tq, c - 2 * C_QKW), lambda i: (i, 0))],
        out_specs=pl.BlockSpec((tq, c), lambda i: (i, 0)),
        out_shape=jax.ShapeDtypeStruct((t, c), F32),
        compiler_params=_cparams(("parallel",)),
    )(z, dqk, dv)


def _conv_bwd_call(u, dz, w, du_buf, *, tq=512):
    t = u.shape[0]
    nt = t // tq
    c, cb, x_spec, halo_spec = _conv_window_specs(tq, t)
    extra, extra_specs, aliases = _du_operands(du_buf, 5)

    def body(x_ref, xh_ref, dz_ref, dzh_ref, w_ref, *refs):
        dx_ref, dw_ref = refs[len(extra):]
        i = pl.program_id(0)
        xc = jnp.concatenate([jnp.where(i == 0, 0.0, xh_ref[...]), x_ref[...]], axis=0)
        dzv = dz_ref[...]
        dzc = jnp.concatenate([dzv, jnp.where(i == nt - 1, 0.0, dzh_ref[...])], axis=0)
        wv = w_ref[...]
        dx = dzv * wv[3:4]
        for s in range(1, C_CONV):
            dx = dx + dzc[s:s + tq] * wv[3 - s:4 - s]
        dx_ref[...] = dx.astype(BF16)
        row = lax.broadcasted_iota(jnp.int32, (8, c), 0)
        dw = jnp.zeros((8, c), F32)
        for j in range(C_CONV):
            prod = dzv * xc[HALO - 3 + j:HALO - 3 + j + tq]
            col = jnp.sum(jnp.sum(prod.reshape(tq // 8, 8, c), axis=0), axis=0, keepdims=True)
            dw = jnp.where(row == j, col, dw)

        @pl.when(i == 0)
        def _():
            dw_ref[...] = dw

        @pl.when(i > 0)
        def _():
            dw_ref[...] += dw

    blk = pl.BlockSpec((tq, c), lambda i: (i, 0))
    after = pl.BlockSpec((HALO, c), lambda i: (jnp.minimum((i + 1) * (tq // HALO), t // HALO - 1), 0))
    return pl.pallas_call(
        body, name="conv_bwd", grid=(nt,),
        in_specs=[x_spec, halo_spec, blk, after, pl.BlockSpec((C_CONV, c), lambda i: (0, 0))] + extra_specs,
        out_specs=[pl.BlockSpec((tq, c), lambda i: (i, cb)), pl.BlockSpec((8, c), lambda i: (0, 0))],
        out_shape=[_du_shape(t), jax.ShapeDtypeStruct((8, c), F32)],
        input_output_aliases=aliases,
        compiler_params=_cparams(("arbitrary",)),
    )(u, u, dz, dz, w, *extra)


C_VW = C_V_HEADS * C_DK


def _softplus(x):
    return jnp.maximum(x, 0.0) + jnp.log(1.0 + jnp.exp(-jnp.abs(x)))


def _tri_masks():
    r = lax.broadcasted_iota(jnp.int32, (CHUNK, CHUNK), 0)
    c = lax.broadcasted_iota(jnp.int32, (CHUNK, CHUNK), 1)
    return r >= c, r > c


def _split_bf16(a):
    hi = a.astype(BF16)
    return hi, (a - hi.astype(F32)).astype(BF16)


def _dot_hi(a, b, dims=None):
    dims = _NN if dims is None else dims
    ah, al = _split_bf16(a)
    bh, bl = _split_bf16(b)

    def d(x, y):
        return lax.dot_general(x, y, dims, preferred_element_type=F32)

    return d(ah, bh) + (d(ah, bl) + d(al, bh))


def _dot_mask(mask, b, dims=None, mask_left=True):
    dims = _NN if dims is None else dims
    mb = mask.astype(BF16)
    b1 = b.astype(BF16)
    rest = b - b1.astype(F32)
    b2 = rest.astype(BF16)
    b3 = (rest - b2.astype(F32)).astype(BF16)
    out = None
    for p in (b1, b2, b3):
        term = (lax.dot_general(mb, p, dims, preferred_element_type=F32) if mask_left
                else lax.dot_general(p, mb, dims, preferred_element_type=F32))
        out = term if out is None else out + term
    return out


def _unit_lower_inverses(mats):
    r = lax.broadcasted_iota(jnp.int32, (CHUNK, CHUNK), 0)
    c = lax.broadcasted_iota(jnp.int32, (CHUNK, CHUNK), 1)
    eye = (r == c).astype(F32)
    xs = [eye - a for a in mats]
    ps = [_dot_hi(a, a) for a in mats]
    steps = int(math.log2(CHUNK)) - 1
    for s in range(steps):
        xs = [x + _dot_hi(x, p) for x, p in zip(xs, ps)]
        if s < steps - 1:
            ps = [_dot_hi(p, p) for p in ps]
    return xs


def _gate_tiles(cab, alog, dtb):
    pre = cab + dtb
    g = -jnp.exp(alog) * _softplus(pre)
    beta = jax.nn.sigmoid(pltpu.roll(cab, LANES - C_V_HEADS, 1))
    return g, beta, pre


def _chunk_common(kk, qk, gc, gct, beta, h, tri, strict):
    gcol, grow, bcol = gc[:, h:h + 1], gct[h:h + 1, :], beta[:, h:h + 1]
    decay = jnp.where(tri, jnp.exp(jnp.where(tri, gcol - grow, 0.0)), 0.0)
    kkd = jnp.where(strict, kk * decay, 0.0)
    attn = jnp.where(tri, qk * decay, 0.0)
    glast = gc[CHUNK - 1:CHUNK, h:h + 1]
    return gcol, bcol, decay, kkd, attn, glast


def _cab_spec():
    return pl.BlockSpec((CHUNK, LANES), lambda n: (n, COL["c_ab"] // LANES))


def _delta_prep_call(qk, v, u, alog, dtb, gather_src=None):
    t = qk.shape[0]
    nc = t // CHUNK
    scale = C_DK ** -0.5
    riding = gather_src is not None
    ng = len(gather_src) if riding else 0

    def body(q_ref, k_ref, v_ref, cab_ref, alog_ref, dtb_ref, *refs):
        if riding:
            gather_refs = (refs[:ng], refs[ng + 8:2 * ng + 8]) + tuple(refs[2 * ng + 8:])
            refs = refs[ng:ng + 8]

            @pl.when(pl.program_id(0) == 0)
            def _():
                _gather_start(*gather_refs)
        u_ref, w_ref, qd_ref, kd_ref, attn_ref, tmat_ref, gc_ref, beta_ref = refs
        tri, strict = _tri_masks()
        g, beta, _ = _gate_tiles(cab_ref[...], alog_ref[...], dtb_ref[...])
        gc = _dot_mask(tri, g)
        gct = gc.T
        gc_ref[...] = gc
        beta_ref[...] = beta
        mats, rhs = [], []
        for j in range(C_QK_HEADS):
            js = slice(j * C_DK, (j + 1) * C_DK)
            kf, qf = k_ref[:, js], q_ref[:, js] * scale
            kb, qb = kf.astype(BF16), qf.astype(BF16)
            kk = lax.dot_general(kb, kb, _NT, preferred_element_type=F32)
            qk = lax.dot_general(qb, kb, _NT, preferred_element_type=F32)
            for h in (2 * j, 2 * j + 1):
                hs = slice(h * C_DK, (h + 1) * C_DK)
                gcol, bcol, decay, kkd, attn, glast = _chunk_common(kk, qk, gc, gct, beta, h, tri, strict)
                gexp = jnp.exp(gcol)
                mats.append(kkd * bcol)
                rhs.append(jnp.concatenate([v_ref[:, hs] * bcol, kf * (bcol * gexp)], axis=1))
                qd_ref[:, hs] = (qf * gexp).astype(BF16)
                kd_ref[:, hs] = (kf * jnp.exp(glast - gcol)).astype(BF16)
                attn_ref[:, h * CHUNK:(h + 1) * CHUNK] = attn.astype(BF16)
        for h, (tmat, r) in enumerate(zip(_unit_lower_inverses(mats), rhs)):
            hs = slice(h * C_DK, (h + 1) * C_DK)
            uw = _dot_hi(tmat, r)
            u_ref[:, hs] = uw[:, :C_DK]
            w_ref[:, hs] = uw[:, C_DK:]
            tmat_ref[:, h * CHUNK:(h + 1) * CHUNK] = tmat
        if riding:
            @pl.when(pl.program_id(0) == nc - 1)
            def _():
                _gather_finish(*gather_refs)

    def blk(w):
        return pl.BlockSpec((CHUNK, w), lambda n: (n, 0))

    row = pl.BlockSpec((1, LANES), lambda n: (0, 0))
    big = jax.ShapeDtypeStruct((t, C_VW), F32)
    sq = jax.ShapeDtypeStruct((t, C_V_HEADS * CHUNK), F32)
    tile = jax.ShapeDtypeStruct((t, LANES), F32)
    half = jax.ShapeDtypeStruct((t, C_VW), BF16)
    in_specs = [blk(C_QKW), pl.BlockSpec((CHUNK, C_QKW), lambda n: (n, 1)), blk(C_VW), _cab_spec(), row, row]
    out_specs = [blk(C_VW)] * 4 + [blk(C_V_HEADS * CHUNK)] * 2 + [blk(LANES)] * 2
    out_shape = [big, big, half, half, jax.ShapeDtypeStruct(sq.shape, BF16), sq] + [tile] * 2
    args = [qk, qk, v, u, alog, dtb]
    if riding:
        in_specs += [HBM_SPEC] * ng
        out_specs += [HBM_SPEC] * ng
        out_shape += _gathered_shapes(gather_src)
        args += list(gather_src)
    return pl.pallas_call(
        body, name="delta_prep_gather" if riding else "delta_prep", grid=(nc,),
        in_specs=in_specs, out_specs=out_specs, out_shape=out_shape,
        scratch_shapes=_gather_sems(ng) if riding else [],
        compiler_params=_cparams(("arbitrary",) if riding else ("parallel",)),
    )(*args)


SCAN_SUB = 4


def _delta_scan_call(u, w, qd, kd, attn, gc):
    t = u.shape[0]
    nc = t // CHUNK
    rows = SCAN_SUB * CHUNK

    def body(u_ref, w_ref, qd_ref, kd_ref, attn_ref, gc_ref, o_ref, vn_ref, st_ref, s_ref):
        @pl.when(pl.program_id(0) == 0)
        def _():
            s_ref[...] = jnp.zeros_like(s_ref)

        hss = [slice(h * C_DK, (h + 1) * C_DK) for h in range(C_V_HEADS)]
        states = [s_ref[hs, :] for hs in hss]
        for c in range(SCAN_SUB):
            rs = slice(c * CHUNK, (c + 1) * CHUNK)
            for hs, s in zip(hss, states):
                st_ref[c, hs, :] = s
            sbs = [s.astype(BF16) for s in states]
            vns = [u_ref[rs, hs] - jnp.dot(w_ref[rs, hs].astype(BF16), sb, preferred_element_type=F32)
                   for hs, sb in zip(hss, sbs)]
            qss = [jnp.dot(qd_ref[rs, hs].astype(BF16), sb, preferred_element_type=F32) for hs, sb in zip(hss, sbs)]
            vnbs = [vn.astype(BF16) for vn in vns]
            for h, hs in enumerate(hss):
                vn_ref[rs, hs] = vnbs[h]
                o_ref[rs, hs] = qss[h] + jnp.dot(attn_ref[rs, h * CHUNK:(h + 1) * CHUNK].astype(BF16), vnbs[h],
                                                 preferred_element_type=F32)
            last = (c + 1) * CHUNK - 1
            states = [states[h] * jnp.exp(gc_ref[last:last + 1, h:h + 1])
                      + lax.dot_general(kd_ref[rs, hs].astype(BF16), vnbs[h], _TN, preferred_element_type=F32)
                      for h, hs in enumerate(hss)]
        for hs, s in zip(hss, states):
            s_ref[hs, :] = s

    def blk(wd):
        return pl.BlockSpec((rows, wd), lambda n: (n, 0))

    big = jax.ShapeDtypeStruct((t, C_VW), F32)
    return pl.pallas_call(
        body, name="delta_scan", grid=(nc // SCAN_SUB,),
        in_specs=[blk(C_VW)] * 4 + [blk(C_V_HEADS * CHUNK), blk(LANES)],
        out_specs=[blk(C_VW), blk(C_VW), pl.BlockSpec((SCAN_SUB, C_VW, C_DK), lambda n: (n, 0, 0))],
        out_shape=[big, jax.ShapeDtypeStruct((t, C_VW), BF16), jax.ShapeDtypeStruct((nc, C_VW, C_DK), F32)],
        scratch_shapes=[pltpu.VMEM((C_VW, C_DK), F32)],
        compiler_params=_cparams(("arbitrary",)),
    )(u, w, qd, kd, attn, gc)


def _delta_scan_bwd_call(do, w, qd, kd, attn, gc, vn, st):
    t = do.shape[0]
    nc = t // CHUNK
    rows = SCAN_SUB * CHUNK
    steps = nc // SCAN_SUB

    def body(do_ref, w_ref, qd_ref, kd_ref, attn_ref, gc_ref, vn_ref, st_ref,
             du_ref, dw_ref, dqd_ref, dkd_ref, dattn_ref, dgl_ref, ds_ref):
        @pl.when(pl.program_id(0) == 0)
        def _():
            ds_ref[...] = jnp.zeros_like(ds_ref)

        tri, _ = _tri_masks()
        row = lax.broadcasted_iota(jnp.int32, (8, LANES), 0)
        lane = lax.broadcasted_iota(jnp.int32, (8, LANES), 1)
        hss = [slice(h * C_DK, (h + 1) * C_DK) for h in range(C_V_HEADS)]
        css = [slice(h * CHUNK, (h + 1) * CHUNK) for h in range(C_V_HEADS)]

        def dg(a, b, dims):
            return lax.dot_general(a, b, dims, preferred_element_type=F32)

        dsps = [ds_ref[hs, :] for hs in hss]
        for c in reversed(range(SCAN_SUB)):
            rs = slice(c * CHUNK, (c + 1) * CHUNK)
            dgl = jnp.zeros((8, LANES), F32)
            ss = [st_ref[c, hs, :] for hs in hss]
            sbs = [s.astype(BF16) for s in ss]
            dspbs = [d.astype(BF16) for d in dsps]
            dobs = [do_ref[rs, hs].astype(BF16) for hs in hss]
            vnbs = [vn_ref[rs, hs].astype(BF16) for hs in hss]
            dvns = [dg(attn_ref[rs, cs].astype(BF16), dob, _TN) + dg(kd_ref[rs, hs].astype(BF16), dspb, _NN)
                    for hs, cs, dob, dspb in zip(hss, css, dobs, dspbs)]
            for h, hs in enumerate(hss):
                dqd_ref[rs, hs] = dg(dobs[h], sbs[h], _NT)
                dkd_ref[rs, hs] = dg(vnbs[h], dspbs[h], _NT)
                dattn_ref[rs, css[h]] = jnp.where(tri, dg(dobs[h], vnbs[h], _NT), 0.0)
            dvnbs = [d.astype(BF16) for d in dvns]
            for h, hs in enumerate(hss):
                du_ref[rs, hs] = dvns[h]
                dw_ref[rs, hs] = -dg(dvnbs[h], sbs[h], _NT)
                tot = jnp.sum(jnp.sum(dsps[h] * ss[h], axis=0, keepdims=True), axis=1, keepdims=True)
                dgl = jnp.where(jnp.logical_and(row == 0, lane == h), tot, dgl)
            dgl_ref[c * 8:(c + 1) * 8, :] = dgl
            last = (c + 1) * CHUNK - 1
            dsps = [dg(qd_ref[rs, hs].astype(BF16), dobs[h], _TN) + jnp.exp(gc_ref[last:last + 1, h:h + 1]) * dsps[h]
                    - dg(w_ref[rs, hs].astype(BF16), dvnbs[h], _TN) for h, hs in enumerate(hss)]
        for hs, d in zip(hss, dsps):
            ds_ref[hs, :] = d

    def blk(wd):
        return pl.BlockSpec((rows, wd), lambda n: (steps - 1 - n, 0))

    big = jax.ShapeDtypeStruct((t, C_VW), F32)
    return pl.pallas_call(
        body, name="delta_scan_bwd", grid=(steps,),
        in_specs=[blk(C_VW)] * 4 + [blk(C_V_HEADS * CHUNK), blk(LANES), blk(C_VW),
                                    pl.BlockSpec((SCAN_SUB, C_VW, C_DK), lambda n: (steps - 1 - n, 0, 0))],
        out_specs=[blk(C_VW)] * 4 + [blk(C_V_HEADS * CHUNK),
                                     pl.BlockSpec((SCAN_SUB * 8, LANES), lambda n: (steps - 1 - n, 0))],
        out_shape=[big] * 4 + [jax.ShapeDtypeStruct((t, C_V_HEADS * CHUNK), F32),
                               jax.ShapeDtypeStruct((nc * 8, LANES), F32)],
        scratch_shapes=[pltpu.VMEM((C_VW, C_DK), F32)],
        compiler_params=_cparams(("arbitrary",)),
    )(do, w, qd, kd, attn, gc, vn, st)


PREP_SUB = 2


def _delta_prep_bwd_call(qk, v, proj, alog, dtb, tmat, u, w, gc, beta, du, dw, dqd, dkd, dattn, dgl, du_buf):
    t = qk.shape[0]
    extra, extra_specs, aliases = _du_operands(du_buf, 17)
    nc = t // CHUNK
    rows = PREP_SUB * CHUNK
    scale = C_DK ** -0.5

    def body(q_ref, k_ref, v_ref, cab_ref, alog_ref, dtb_ref, tmat_ref, u_ref, w_ref, gc_ref, beta_ref,
             du_ref, dw_ref, dqd_ref, dkd_ref, dattn_ref, dgl_ref, *outs):
        dcab_ref, dqk_ref, dv_ref, dpar_ref = outs[len(extra):]
        tri, strict = _tri_masks()
        ones = jnp.ones((CHUNK, LANES), F32)
        lane = lax.broadcasted_iota(jnp.int32, (CHUNK, LANES), 1)
        rowi = lax.broadcasted_iota(jnp.int32, (CHUNK, 1), 0)
        subs = range(PREP_SUB)
        rss = [slice(c * CHUNK, (c + 1) * CHUNK) for c in subs]
        betas = [beta_ref[rs, :] for rs in rss]

        def dot(x, y, dims=_NN):
            return lax.dot_general(x, y, dims, preferred_element_type=F32)

        heads = []
        for c, rs in zip(subs, rss):
            gc = gc_ref[rs, :]
            gct = gc.T
            for j in range(C_QK_HEADS):
                js = slice(j * C_DK, (j + 1) * C_DK)
                kf, qf = k_ref[rs, js], q_ref[rs, js] * scale
                kb, qb = kf.astype(BF16), qf.astype(BF16)
                kk = dot(kb, kb, _NT)
                qk = dot(qb, kb, _NT)
                for h in (2 * j, 2 * j + 1):
                    heads.append((c, rs, h, kf, qf, kb, qb) + _chunk_common(kk, qk, gc, gct, betas[c], h, tri, strict))

        def cols(h):
            return slice(h * C_DK, (h + 1) * C_DK)

        def sq(h):
            return slice(h * CHUNK, (h + 1) * CHUNK)

        dvks = [_dot_hi(tmat_ref[hd[1], sq(hd[2])],
                        jnp.concatenate([du_ref[hd[1], cols(hd[2])], dw_ref[hd[1], cols(hd[2])]], axis=1), _TN)
                for hd in heads]
        das = [-jnp.where(strict, _dot_hi(dvk, jnp.concatenate([u_ref[hd[1], cols(hd[2])], w_ref[hd[1], cols(hd[2])]],
                                                               axis=1), _NT), 0.0)
               for hd, dvk in zip(heads, dvks)]
        pre = []
        for (c, rs, h, kf, qf, kb, qb, gcol, bcol, decay, kkd, attn, glast), da in zip(heads, das):
            dattn_h = dattn_ref[rs, sq(h)]
            pre.append(((da * decay * bcol).astype(BF16), (dattn_h * decay).astype(BF16),
                        da * kkd * bcol + dattn_h * attn))
        mms = [(dot(dkk, hd[5]), dot(dkk, hd[5], _TN), dot(dqk, hd[6], _TN), dot(dqk, hd[5]),
                _dot_mask(ones, e, _TN, mask_left=False))
               for hd, (dkk, dqk, e) in zip(heads, pre)]
        dq_parts, dk_parts = {}, {}
        dgc_tiles = [jnp.zeros((CHUNK, LANES), F32) for _ in subs]
        db_tiles = [jnp.zeros((CHUNK, LANES), F32) for _ in subs]
        for (c, rs, h, kf, qf, kb, qb, gcol, bcol, decay, kkd, attn, glast), dvk, da, (_, _, e), mm in zip(
                heads, dvks, das, pre, mms):
            hs = cols(h)
            gexp = jnp.exp(gcol)
            fdec = jnp.exp(glast - gcol)
            dvb, dkb = dvk[:, :C_DK], dvk[:, C_DK:]
            dgc = jnp.sum(e, axis=1, keepdims=True) - mm[4][:, :1]
            dk_parts[c, h] = mm[0] + mm[1] + mm[2] + dkb * (bcol * gexp) + dkd_ref[rs, hs] * fdec
            dq_parts[c, h] = mm[3] + dqd_ref[rs, hs] * gexp
            dv_ref[rs, hs] = dvb * bcol
            s_kb = jnp.sum(dkb * kf, axis=1, keepdims=True)
            db = (jnp.sum(da * kkd, axis=1, keepdims=True) + jnp.sum(dvb * v_ref[rs, hs], axis=1, keepdims=True)
                  + s_kb * gexp)
            rho = jnp.sum(dkd_ref[rs, hs] * kf, axis=1, keepdims=True) * fdec
            dgc = (dgc + s_kb * bcol * gexp + jnp.sum(dqd_ref[rs, hs] * qf, axis=1, keepdims=True) * gexp - rho)
            last = jnp.sum(rho, axis=0, keepdims=True) + dgl_ref[c * 8:c * 8 + 1, h:h + 1] * jnp.exp(glast)
            dgc = dgc + jnp.where(rowi == CHUNK - 1, last, 0.0)
            dgc_tiles[c] = jnp.where(lane == h, dgc, dgc_tiles[c])
            db_tiles[c] = jnp.where(lane == h, db, db_tiles[c])
        alog = alog_ref[...]
        row8 = lax.broadcasted_iota(jnp.int32, (8, LANES), 0)
        par = jnp.zeros((8, LANES), F32)
        for c, rs in zip(subs, rss):
            for j in range(C_QK_HEADS):
                dqk_ref[rs, j * C_DK:(j + 1) * C_DK] = (dq_parts[c, 2 * j] + dq_parts[c, 2 * j + 1]) * scale
                dqk_ref[rs, C_QKW + j * C_DK:C_QKW + (j + 1) * C_DK] = dk_parts[c, 2 * j] + dk_parts[c, 2 * j + 1]
            dg = _dot_mask(jnp.logical_not(strict), dgc_tiles[c])
            g, _, gate_pre = _gate_tiles(cab_ref[rs, :], alog, dtb_ref[...])
            dca = dg * (-jnp.exp(alog)) * jax.nn.sigmoid(gate_pre)
            beta = betas[c]
            dcab_ref[rs, :LANES] = (dca + pltpu.roll(db_tiles[c] * beta * (1.0 - beta), C_V_HEADS, 1)).astype(BF16)
            dcab_ref[rs, LANES:] = jnp.zeros((CHUNK, D_IN_PAD - COL["c_ab"] - LANES), BF16)
            par = par + jnp.where(row8 == 0, jnp.sum(dg * g, axis=0, keepdims=True),
                                  jnp.where(row8 == 1, jnp.sum(dca, axis=0, keepdims=True), 0.0))

        @pl.when(pl.program_id(0) == 0)
        def _():
            dpar_ref[...] = par

        @pl.when(pl.program_id(0) > 0)
        def _():
            dpar_ref[...] += par

    def blk(wd):
        return pl.BlockSpec((rows, wd), lambda n: (n, 0))

    row = pl.BlockSpec((1, LANES), lambda n: (0, 0))
    sqs = blk(C_V_HEADS * CHUNK)
    tail = D_IN_PAD - COL["c_ab"]
    assert COL["c_ab"] % tail == 0 and nc % PREP_SUB == 0
    return pl.pallas_call(
        body, name="delta_prep_bwd", grid=(nc // PREP_SUB,),
        in_specs=[blk(C_QKW), pl.BlockSpec((rows, C_QKW), lambda n: (n, 1)), blk(C_VW),
                  pl.BlockSpec((rows, LANES), lambda n: (n, COL["c_ab"] // LANES)), row, row, sqs,
                  blk(C_VW), blk(C_VW),
                  blk(LANES), blk(LANES), blk(C_VW), blk(C_VW), blk(C_VW), blk(C_VW), sqs,
                  pl.BlockSpec((PREP_SUB * 8, LANES), lambda n: (n, 0))] + extra_specs,
        out_specs=[pl.BlockSpec((rows, tail), lambda n: (n, COL["c_ab"] // tail)),
                   blk(2 * C_QKW), blk(C_VW), pl.BlockSpec((8, LANES), lambda n: (0, 0))],
        out_shape=[_du_shape(t), jax.ShapeDtypeStruct((t, 2 * C_QKW), F32),
                   jax.ShapeDtypeStruct((t, C_VW), F32), jax.ShapeDtypeStruct((8, LANES), F32)],
        input_output_aliases=aliases,
        compiler_params=_cparams(("arbitrary",)),
    )(qk, qk, v, proj, alog, dtb, tmat, u, w, gc, beta, du, dw, dqd, dkd, dattn, dgl, *extra)


def _z_spec(tq):
    return pl.BlockSpec((tq, C_VW), lambda i: (i, COL["c_z"] // C_VW))


def _gated_norm_fwd_call(o, u, gain, *, tq=256):
    t, w = o.shape

    def body(o_ref, z_ref, g_ref, y_ref):
        act, _ = _silu_parts(z_ref[...])
        gv = g_ref[...]
        for h in range(C_V_HEADS):
            hs = slice(h * C_DK, (h + 1) * C_DK)
            ov = o_ref[:, hs]
            r = lax.rsqrt(jnp.mean(ov * ov, axis=1, keepdims=True) + EPS)
            y_ref[:, hs] = ov * r * gv * act[:, hs]

    blk = pl.BlockSpec((tq, w), lambda i: (i, 0))
    return pl.pallas_call(
        body, name="gated_norm_fwd", grid=(t // tq,),
        in_specs=[blk, _z_spec(tq), pl.BlockSpec((1, C_DK), lambda i: (0, 0))], out_specs=blk,
        out_shape=jax.ShapeDtypeStruct((t, w), F32),
        compiler_params=_cparams(("parallel",)),
    )(o, u, gain)


def _gated_norm_bwd_call(o, u, gain, dy, du_buf, *, tq=256):
    t, w = o.shape
    nt = t // tq
    extra, extra_specs, aliases = _du_operands(du_buf, 4)

    def body(o_ref, z_ref, g_ref, dy_ref, *refs):
        dz_ref, do_ref, dg_ref, acc_ref = refs[len(extra):]
        i = pl.program_id(0)
        act, dact = _silu_parts(z_ref[...])
        gv = g_ref[...]
        part = jnp.zeros((8, C_DK), F32)
        for h in range(C_V_HEADS):
            hs = slice(h * C_DK, (h + 1) * C_DK)
            ov = o_ref[:, hs]
            r = lax.rsqrt(jnp.mean(ov * ov, axis=1, keepdims=True) + EPS)
            xh = ov * r
            dyv = dy_ref[:, hs]
            dn = dyv * act[:, hs]
            dz_ref[:, hs] = (dyv * xh * gv * dact[:, hs]).astype(BF16)
            dxh = dn * gv
            do_ref[:, hs] = r * (dxh - xh * jnp.mean(dxh * xh, axis=1, keepdims=True))
            part = part + jnp.sum((dn * xh).reshape(tq // 8, 8, C_DK), axis=0)

        @pl.when(i == 0)
        def _():
            acc_ref[...] = part

        @pl.when(i > 0)
        def _():
            acc_ref[...] += part

        @pl.when(i == nt - 1)
        def _():
            dg_ref[...] = jnp.sum(acc_ref[...], axis=0, keepdims=True)

    blk = pl.BlockSpec((tq, w), lambda i: (i, 0))
    grow = pl.BlockSpec((1, C_DK), lambda i: (0, 0))
    return pl.pallas_call(
        body, name="gated_norm_bwd", grid=(nt,),
        in_specs=[blk, _z_spec(tq), grow, blk] + extra_specs, out_specs=[_z_spec(tq), blk, grow],
        out_shape=[_du_shape(t), jax.ShapeDtypeStruct((t, w), F32), jax.ShapeDtypeStruct((1, C_DK), F32)],
        scratch_shapes=[pltpu.VMEM((8, C_DK), F32)],
        input_output_aliases=aliases,
        compiler_params=_cparams(("arbitrary",)),
    )(o, u, gain, dy, *extra)


def _gate_specs(tq):
    return [pl.BlockSpec((tq, D_MODEL), lambda i, j=j: (i, j)) for j in range(3)]


def _merge_fwd_call(ps, u, *, tq=256):
    t, w = ps[0].shape

    def body(p0, p1, p2, g0, g1, g2, y_ref):
        y_ref[...] = (jax.nn.sigmoid(g0[...]) * p0[...] + jax.nn.sigmoid(g1[...]) * p1[...]
                      + jax.nn.sigmoid(g2[...]) * p2[...]).astype(BF16)

    blk = pl.BlockSpec((tq, w), lambda i: (i, 0))
    return pl.pallas_call(
        body, name="merge_fwd", grid=(t // tq,), in_specs=[blk] * 3 + _gate_specs(tq), out_specs=blk,
        out_shape=jax.ShapeDtypeStruct((t, w), BF16),
        compiler_params=_cparams(("parallel",)),
    )(*ps, u, u, u)


def _merge_bwd_call(ps, u, dy, *, tq=256):
    t, w = dy.shape

    def body(p0, p1, p2, g0, g1, g2, dy_ref, dg_ref, dp0, dp1, dp2):
        dyv = dy_ref[...]
        for j, (p, g, dp) in enumerate(((p0, g0, dp0), (p1, g1, dp1), (p2, g2, dp2))):
            sig = jax.nn.sigmoid(g[...])
            dp[...] = (dyv * sig).astype(BF16)
            dg_ref[:, j * w:(j + 1) * w] = (dyv * p[...] * sig * (1.0 - sig)).astype(BF16)

    blk = pl.BlockSpec((tq, w), lambda i: (i, 0))
    small = jax.ShapeDtypeStruct((t, w), BF16)
    return pl.pallas_call(
        body, name="merge_bwd", grid=(t // tq,), in_specs=[blk] * 3 + _gate_specs(tq) + [blk],
        out_specs=[pl.BlockSpec((tq, 3 * w), lambda i: (i, 0))] + [blk] * 3,
        out_shape=[_du_shape(t)] + [small] * 3,
        compiler_params=_cparams(("parallel",)),
    )(*ps, u, u, u, dy)


Q_SCALE = HEAD_DIM ** -0.5
A_PARTS = ((COL["a_q"], A_W, True, Q_SCALE), (COL["a_k"], A_W, True, 1.0), (COL["a_v"], A_W, False, 1.0))
B_PARTS = ((COL["b_q"], A_W, True, Q_SCALE), (COL["b_k"], B_KVW, True, 1.0), (COL["b_v"], B_KVW, False, 1.0))
BRANCHES = ("w_branch_a", "w_branch_b", "w_branch_c")


def _layer_fwd(x, tabs, p, wb, gather_src=None):
    h = _rms_fwd_call(x, p["norm_mix"], name="rms_mix_fwd", out_dtype=BF16)
    u = _mm(h, wb["w_in"], bias=p["b_in"], tn=IN_TN, name="in_proj_fwd")
    qkv_a = _rope_gather_call(u, tabs, A_PARTS, name="rope_a_fwd")
    a_runs = [_attn_fwd(cfg, qkv_a, None) for cfg in ATTN_A_CFGS]
    os_, lses = tuple(r[0] for r in a_runs), tuple(r[1] for r in a_runs)
    ya = _combine_fwd_call(os_, lses)
    qkv_b = _rope_gather_call(u, tabs, B_PARTS, name="rope_b_fwd")
    yb, _, b_saved = _attn_fwd(ATTN_B_CFG, qkv_b, p["sinks"])
    zc, qk, v = _conv_prep_fwd_call(u, p["conv_w"])
    uu, ww, qd, kd, attn, tmat, gc, beta, *gathered = _delta_prep_call(qk, v, u, p["a_log"], p["dt_bias"], gather_src)
    o, vn, st = _delta_scan_call(uu, ww, qd, kd, attn, gc)
    yc = _gated_norm_fwd_call(o, u, p["c_norm"])
    ys = (ya, yb, yc)
    ps = tuple(_mm(y, wb[n], name="branch_fwd") for y, n in zip(ys, BRANCHES))
    merged = _merge_fwd_call(ps, u)
    x1 = _mm(merged, wb["w_out"], add=x, name="out_proj_fwd")
    h2 = _rms_fwd_call(x1, p["norm_ffn"], name="rms_ffn_fwd", out_dtype=BF16)
    pre, act = _mm(h2, wb["w_ff1"], relu2_out=True, name="ffn_up")
    x2 = _mm(act, wb["w_ff2"], add=x1, name="ffn_down")
    saved = dict(x=x, h=h, u=u, a_saved=[r[2] for r in a_runs], os_=os_, lses=lses, b_saved=b_saved,
                 zc=zc, qk=qk, v=v, delta=(tmat, uu, ww, gc, beta, qd, kd, attn, vn, st), o=o, ys=ys, ps=ps,
                 merged=merged, x1=x1, h2=h2, pre=pre, act=act)
    return x2, saved, (gathered if gathered else None)


def _layer_bwd(s, dx2, tabs, p, wb):
    g = {}
    t = dx2.shape[0]
    dpre = _mm(dx2, wb["w_ff2"], tb=True, mul_drelu2=s["pre"], out_dtype=BF16, name="ffn_dpre")
    g["w_ff2"] = _mm(s["act"], dx2, ta=True, tk=1024, name="ffn_dw2")
    g["w_ff1"] = _mm(s["h2"], dpre, ta=True, tk=1024, name="ffn_dw1")
    dh2 = _mm(dpre, wb["w_ff1"], tb=True, name="ffn_dh")
    dx1, g["norm_ffn"] = _rms_bwd_call(s["x1"], p["norm_ffn"], dh2, add=dx2, name="rms_ffn_bwd")
    dmerged = _mm(dx1, wb["w_out"], tb=True, name="out_proj_da")
    g["w_out"] = _mm(s["merged"], dx1, ta=True, tk=1024, name="out_proj_dw")
    du, *dps = _merge_bwd_call(s["ps"], s["u"], dmerged)
    dys = []
    for y, dp, n in zip(s["ys"], dps, BRANCHES):
        dys.append(_mm(dp, wb[n], tb=True, name="branch_da"))
        g[n] = _mm(y, dp, ta=True, tk=1024, name="branch_dw")
    dya, dyb, dyc = dys
    tmat, uu, ww, gc, beta, qd, kd, attn, vn, st = s["delta"]
    du, do, g["c_norm"] = _gated_norm_bwd_call(s["o"], s["u"], p["c_norm"], dyc, du)
    ddu, ddw, dqd, dkd, dattn, dgl = _delta_scan_bwd_call(do, ww, qd, kd, attn, gc, vn, st)
    du, dqk, dv, dpar = _delta_prep_bwd_call(s["qk"], s["v"], s["u"], p["a_log"], p["dt_bias"], tmat, uu, ww, gc,
                                             beta, ddu, ddw, dqd, dkd, dattn, dgl, du)
    g["a_log"], g["dt_bias"] = dpar[0:1], dpar[1:2]
    dzc = _conv_prep_dz_call(s["zc"], dqk, dv)
    du, dconv = _conv_bwd_call(s["u"], dzc, p["conv_w"], du)
    g["conv_w"] = dconv[:C_CONV]
    no_dlse = jnp.zeros((t, LANES), F32)
    dq, dk, dv_b, dsink = _attn_bwd(ATTN_B_CFG, s["b_saved"], p["sinks"], dyb, no_dlse)
    g["sinks"] = dsink[0, :p["sinks"].shape[0]]
    du = _rope_scatter_call(du, t, [([dq], A_W, True, Q_SCALE)], COL["b_q"], tabs, name="rope_bq_bwd")
    du = _rope_scatter_call(du, t, [([dk], B_KVW, True, 1.0), ([dv_b], B_KVW, False, 1.0)], COL["b_k"], tabs,
                            name="rope_bkv_bwd")
    *dos, dl0, dl1, dl2 = _combine_bwd_call(s["os_"], s["lses"], dya)
    grads_a = [_attn_bwd(cfg, sv, None, do_c, dl)[:3]
               for cfg, sv, do_c, dl in zip(ATTN_A_CFGS, s["a_saved"], dos, (dl0, dl1, dl2))]
    dqs, dks, dvs = zip(*grads_a)
    du = _rope_scatter_call(du, t, [(list(dqs), A_W, True, Q_SCALE), (list(dks), A_W, True, 1.0),
                                    (list(dvs), A_W, False, 1.0)],
                            COL["a_q"], tabs, name="rope_a_bwd")
    dh = _mm(du, wb["w_in"], tb=True, tk=IN_TN, name="in_proj_da")
    g["w_in"], g["b_in"] = _mm(s["h"], du, ta=True, b_colsum=True, tn=IN_TN, tk=1024, name="in_proj_dw")
    dx, g["norm_mix"] = _rms_bwd_call(s["x"], p["norm_mix"], dh, add=dx1, name="rms_mix_bwd")
    return dx, g


def _local_step(x, params, first_gathered, payload_of_layer, weights_of_layer, tabs, tgt):
    saves, gathered = [], first_gathered
    for layer in range(DEPTH):
        wb, conv_w = weights_of_layer(layer, gathered)
        p = {n: w[layer] for n, w in params.items() if n != "norm_final"}
        p["conv_w"] = conv_w
        nxt = payload_of_layer(layer + 1) if layer + 1 < DEPTH else None
        x, s, gathered = _layer_fwd(x, tabs, p, wb, nxt)
        saves.append((s, p, wb))
    loss = _loss_fwd_call(x, params["norm_final"], tgt)
    dx, dfinal = _loss_bwd_call(x, params["norm_final"], tgt, jnp.ones((1, 1), F32))
    per_layer = []
    for s, p, wb in reversed(saves):
        dx, g = _layer_bwd(s, dx, tabs, p, wb)
        per_layer.append(g)
    per_layer.reverse()
    grads = {n: jnp.stack([g[n] for g in per_layer]) for n in per_layer[0]}
    grads["norm_final"] = dfinal
    return loss, dx, grads


def _in_cols_to_kernel(w):
    lead = w.shape[:-1]
    parts, pos = [], 0
    for _, start, width, ref_start in IN_LAYOUT:
        if start > pos:
            parts.append(jnp.zeros(lead + (start - pos,), w.dtype))
        parts.append(w[..., ref_start:ref_start + width])
        pos = start + width
    parts.append(jnp.zeros(lead + (D_IN_PAD - pos,), w.dtype))
    return jnp.concatenate(parts, axis=-1)


def _in_cols_to_reference(w):
    by_ref = sorted(IN_LAYOUT, key=lambda e: e[3])
    return jnp.concatenate([w[..., start:start + width] for _, start, width, _ in by_ref], axis=-1)


W_IN_SHARD = 8464 // N_DEV


def _w_in_from_shards(blocks):
    lead = blocks.shape[1:-1]
    parts, pos = [], 0
    for _, start, width, ref_start in IN_LAYOUT:
        if start > pos:
            parts.append(jnp.zeros(lead + (start - pos,), blocks.dtype))
        col = ref_start
        while col < ref_start + width:
            d, l = divmod(col, W_IN_SHARD)
            n = min(W_IN_SHARD - l, ref_start + width - col)
            parts.append(blocks[d, ..., l:l + n])
            col += n
        pos = start + width
    parts.append(jnp.zeros(lead + (D_IN_PAD - pos,), blocks.dtype))
    return jnp.concatenate(parts, axis=-1)


def _w_in_to_shards(g):
    by_ref = sorted(IN_LAYOUT, key=lambda e: e[3])
    blocks = []
    for d in range(N_DEV):
        lo, hi = d * W_IN_SHARD, (d + 1) * W_IN_SHARD
        parts = []
        for _, start, width, ref_start in by_ref:
            a, b = max(lo, ref_start), min(hi, ref_start + width)
            if a < b:
                parts.append(g[..., start + a - ref_start:start + b - ref_start])
        blocks.append(jnp.concatenate(parts, axis=-1))
    return jnp.stack(blocks)


def _pad_lanes(v):
    return jnp.pad(v, ((0, 0), (0, LANES - v.shape[1])))[:, None, :]


BIG = (("w_in", 2), ("conv_w", 2), ("w_branch_a", 2), ("w_branch_b", 2), ("w_branch_c", 1), ("w_out", 1),
       ("w_ff1", 2), ("w_ff2", 1))
SMALL = ("norm_mix", "b_in", "a_log", "dt_bias", "sinks", "c_norm", "norm_ffn", "norm_final")
WEIGHTS = ("norm_mix", "w_in", "b_in", "conv_w", "a_log", "dt_bias", "sinks", "c_norm", "w_branch_a",
           "w_branch_b", "w_branch_c", "w_out", "norm_ffn", "w_ff1", "w_ff2", "norm_final")
MATMUL_WEIGHTS = ("w_in", "w_branch_a", "w_branch_b", "w_branch_c", "w_out", "w_ff1", "w_ff2")
PACK_ROWS = 1024
ROW_ALIGN = 16


def _seg_rows(n):
    return -(-n // (LANES * ROW_ALIGN)) * ROW_ALIGN


def _pack(arrays, lead=0):
    parts = []
    for a in arrays:
        lead_shape = a.shape[:lead]
        n = math.prod(a.shape[lead:])
        rows = _seg_rows(n)
        if rows * LANES != n:
            a = jnp.pad(a.reshape(lead_shape + (n,)), [(0, 0)] * lead + [(0, rows * LANES - n)])
        parts.append(a.reshape(lead_shape + (rows, LANES)))
    total = sum(p.shape[lead] for p in parts)
    padded = -(-total // PACK_ROWS) * PACK_ROWS
    if padded > total:
        parts.append(jnp.zeros(parts[0].shape[:lead] + (padded - total, LANES), parts[0].dtype))
    return jnp.concatenate(parts, axis=lead)


def _unpack(buf, shapes):
    lead = buf.shape[:-2]
    out, pos = [], 0
    for shp in shapes:
        n = math.prod(shp)
        rows = _seg_rows(n)
        seg = buf[..., pos:pos + rows, :]
        if rows * LANES != n:
            seg = seg.reshape(lead + (rows * LANES,))[..., :n]
        out.append(seg.reshape(lead + tuple(shp)))
        pos += rows
    return out


def _shards_to_full(blocks, axis):
    moved = jnp.moveaxis(blocks, 0, axis)
    shp = list(blocks.shape[1:])
    shp[axis] = shp[axis] * N_DEV
    return moved.reshape(shp)


def _full_to_shards(full, axis):
    shp = list(full.shape)
    shp[axis:axis + 1] = [N_DEV, shp[axis] // N_DEV]
    return jnp.moveaxis(full.reshape(shp), axis, 0)


def _my_place():
    return lax.axis_index("x"), lax.axis_index("y"), lax.axis_index("c")


def _slot(x, y, c):
    return 4 * x + 2 * y + c


GATHER_COPIES = 7


def _gather_plan(x_ref, out_ref, send_sems, recv_sems, local_sem, base):
    x, y, c = _my_place()
    me, sibling = (x, y, c), (x, y, 1 - c)
    chips = [(1 - x, y), (x, 1 - y), (1 - x, 1 - y)]

    def copy(k, blk, to, src=None):
        dst = out_ref.at[_slot(*blk)]
        return pltpu.make_async_remote_copy(
            src_ref=dst if src is None else src, dst_ref=dst,
            send_sem=send_sems.at[base + k], recv_sem=recv_sems.at[base + k], device_id=to, device_id_type=MESH_ID)

    def own():
        mine = pltpu.make_async_copy(x_ref, out_ref.at[_slot(*me)], local_sem)
        return mine, [copy(0, me, sibling, src=x_ref)] + [copy(1 + j, me, (*chip, c), src=x_ref)
                                                          for j, chip in enumerate(chips)]

    return copy, own, me, sibling, chips, c


def _gather_plans(srcs, outs, send_sems, recv_sems, local_sems):
    return [_gather_plan(x_ref, out_ref, send_sems, recv_sems, local_sems.at[i], GATHER_COPIES * i)
            for i, (x_ref, out_ref) in enumerate(zip(srcs, outs))]


def _gather_start(srcs, outs, *sems):
    for _, own, *_ in _gather_plans(srcs, outs, *sems):
        mine, first = own()
        mine.start()
        for cp in first:
            cp.start()


def _gather_finish(srcs, outs, *sems):
    plans = _gather_plans(srcs, outs, *sems)
    passed_all = []
    for copy, own, me, sibling, chips, c in plans:
        passed = [copy(4 + j, (*chip, c), sibling) for j, chip in enumerate(chips)]
        for j, chip in enumerate(chips):
            copy(1 + j, (*chip, c), me).wait_recv()
            passed[j].start()
        passed_all.append(passed)
    for (copy, own, me, sibling, chips, c), passed in zip(plans, passed_all):
        copy(0, sibling, me).wait_recv()
        for j, chip in enumerate(chips):
            copy(4 + j, (*chip, 1 - c), me).wait_recv()
        mine, first = own()
        for cp in first + passed:
            cp.wait_send()
        mine.wait()


def _gather_sems(n):
    return [pltpu.SemaphoreType.DMA((GATHER_COPIES * n,)), pltpu.SemaphoreType.DMA((GATHER_COPIES * n,)),
            pltpu.SemaphoreType.DMA((n,))]


def _gathered_shapes(blocks):
    return [jax.ShapeDtypeStruct((N_DEV,) + b.shape, b.dtype) for b in blocks]


def _all_gather(blocks, *, name):
    n = len(blocks)

    def body(*refs):
        srcs, outs, sems = refs[:n], refs[n:2 * n], refs[2 * n:]
        _gather_start(srcs, outs, *sems)
        _gather_finish(srcs, outs, *sems)

    return pl.pallas_call(
        body, name=name, out_shape=_gathered_shapes(blocks),
        in_specs=[HBM_SPEC] * n, out_specs=[HBM_SPEC] * n,
        scratch_shapes=_gather_sems(n),
    )(*blocks)


N_CHIP = N_DEV // 2


def _swap_with_sibling(blocks, *, name):
    n = len(blocks)

    def body(*refs):
        srcs, outs, send_sems, recv_sems = refs[:n], refs[n:2 * n], refs[2 * n], refs[2 * n + 1]
        x, y, c = _my_place()
        copies = [pltpu.make_async_remote_copy(src_ref=g_ref, dst_ref=out_ref, send_sem=send_sems.at[i],
                                               recv_sem=recv_sems.at[i], device_id=(x, y, 1 - c),
                                               device_id_type=MESH_ID)
                  for i, (g_ref, out_ref) in enumerate(zip(srcs, outs))]
        for cp in copies:
            cp.start()
        for cp in copies:
            cp.wait_recv()
        for cp in copies:
            cp.wait_send()

    return pl.pallas_call(
        body, name=name,
        out_shape=[jax.ShapeDtypeStruct(b.shape, b.dtype) for b in blocks],
        in_specs=[HBM_SPEC] * n, out_specs=[HBM_SPEC] * n,
        scratch_shapes=[pltpu.SemaphoreType.DMA((n,)), pltpu.SemaphoreType.DMA((n,))],
    )(*blocks)


def _chip_all_to_all(blocks, *, name):
    n = len(blocks)
    peers = N_CHIP - 1

    def body(*refs):
        srcs, outs = refs[:n], refs[n:2 * n]
        send_sems, recv_sems, local_sems = refs[2 * n:]
        x, y, c = _my_place()
        mine_slot = 2 * x + y
        locals_, copies = [], []
        for i, (g_ref, out_ref) in enumerate(zip(srcs, outs)):
            locals_.append(pltpu.make_async_copy(g_ref.at[mine_slot], out_ref.at[mine_slot], local_sems.at[i]))
            for k in range(1, N_CHIP):
                px, py = x ^ (k >> 1), y ^ (k & 1)
                copies.append(pltpu.make_async_remote_copy(
                    src_ref=g_ref.at[2 * px + py], dst_ref=out_ref.at[mine_slot],
                    send_sem=send_sems.at[peers * i + k - 1], recv_sem=recv_sems.at[peers * i + k - 1],
                    device_id=(px, py, c), device_id_type=MESH_ID))
        for cp in locals_ + copies:
            cp.start()
        for cp in copies:
            cp.wait_recv()
        for cp in copies:
            cp.wait_send()
        for cp in locals_:
            cp.wait()

    return pl.pallas_call(
        body, name=name,
        out_shape=[jax.ShapeDtypeStruct(b.shape, b.dtype) for b in blocks],
        in_specs=[HBM_SPEC] * n, out_specs=[HBM_SPEC] * n,
        scratch_shapes=[pltpu.SemaphoreType.DMA((peers * n,)), pltpu.SemaphoreType.DMA((peers * n,)),
                        pltpu.SemaphoreType.DMA((n,))],
    )(*blocks)


def _block_rows(rows, cols):
    tr = max(8, min(rows, PACK_ROWS * LANES // (-(-cols // LANES) * LANES) // 8 * 8))
    while rows % tr:
        tr -= 8
    return tr


def _add_bf16_call(a, b, *, name):
    n, rows, cols = a.shape
    tr = _block_rows(rows, cols)

    def body(a_ref, b_ref, o_ref):
        o_ref[...] = (a_ref[...].astype(F32) + b_ref[...].astype(F32)).astype(BF16)

    blk = pl.BlockSpec((n, tr, cols), lambda i: (0, i, 0))
    return pl.pallas_call(
        body, name=name, grid=(rows // tr,), in_specs=[blk, blk], out_specs=blk,
        out_shape=jax.ShapeDtypeStruct(a.shape, BF16),
        compiler_params=_cparams(("parallel",)),
    )(a, b)


def _adamw_call(parts, w, m, v, *, name):
    rows, cols = w.shape
    tr = _block_rows(rows, cols)
    n_parts = parts.shape[0]

    def body(p_ref, w_ref, m_ref, v_ref, g_ref, d_ref, nm_ref, nv_ref):
        g = p_ref[0].astype(F32)
        for s in range(1, n_parts):
            g = g + p_ref[s].astype(F32)
        nm = ADAM_B1 * m_ref[...] + (1.0 - ADAM_B1) * g
        nv = ADAM_B2 * v_ref[...] + (1.0 - ADAM_B2) * jnp.square(g)
        m_hat = nm / (1.0 - ADAM_B1 ** ADAM_STEP)
        v_hat = nv / (1.0 - ADAM_B2 ** ADAM_STEP)
        g_ref[...] = g
        nm_ref[...] = nm
        nv_ref[...] = nv
        d_ref[...] = -ADAM_LR * (m_hat / (jnp.sqrt(v_hat) + ADAM_EPS) + ADAM_WD * w_ref[...])

    blk = pl.BlockSpec((tr, cols), lambda i: (i, 0))
    shape = jax.ShapeDtypeStruct((rows, cols), F32)
    return pl.pallas_call(
        body, name=name, grid=(rows // tr,),
        in_specs=[pl.BlockSpec((n_parts, tr, cols), lambda i: (0, i, 0)), blk, blk, blk],
        out_specs=[blk] * 4, out_shape=[shape] * 4,
        compiler_params=_cparams(("parallel",)),
    )(parts, w, m, v)


def _kernel_params(full):
    return {
        "norm_mix": full["norm_mix"][:, None, :],
        "b_in": _in_cols_to_kernel(full["b_in"])[:, None, :],
        "a_log": _pad_lanes(full["a_log"]),
        "dt_bias": _pad_lanes(full["dt_bias"]),
        "sinks": full["sinks"],
        "c_norm": full["c_norm"][:, None, :],
        "norm_ffn": full["norm_ffn"][:, None, :],
        "norm_final": full["norm_final"][None, :],
    }


def _reference_grads(g):
    return {
        "norm_mix": g["norm_mix"][:, 0, :],
        "b_in": _in_cols_to_reference(g["b_in"][:, 0, :]),
        "conv_w": g["conv_w"],
        "a_log": g["a_log"][:, 0, :C_V_HEADS],
        "dt_bias": g["dt_bias"][:, 0, :C_V_HEADS],
        "sinks": g["sinks"],
        "c_norm": g["c_norm"][:, 0, :],
        "w_branch_a": g["w_branch_a"], "w_branch_b": g["w_branch_b"], "w_branch_c": g["w_branch_c"],
        "w_out": g["w_out"],
        "norm_ffn": g["norm_ffn"][:, 0, :],
        "w_ff1": g["w_ff1"], "w_ff2": g["w_ff2"],
        "norm_final": g["norm_final"][0],
    }


def kernel(x, positions, norm_mix, w_in, b_in, conv_w, a_log, dt_bias, sinks, c_norm, w_branch_a, w_branch_b, w_branch_c, w_out, norm_ffn, w_ff1, w_ff2, norm_final, loss_target, m_norm_mix, m_w_in, m_b_in, m_conv_w, m_a_log, m_dt_bias, m_sinks, m_c_norm, m_w_branch_a, m_w_branch_b, m_w_branch_c, m_w_out, m_norm_ffn, m_w_ff1, m_w_ff2, m_norm_final, v_norm_mix, v_w_in, v_b_in, v_conv_w, v_a_log, v_dt_bias, v_sinks, v_c_norm, v_w_branch_a, v_w_branch_b, v_w_branch_c, v_w_out, v_norm_ffn, v_w_ff1, v_w_ff2, v_norm_final):
    env = dict(locals())
    weights = {n: env[n] for n in WEIGHTS}
    moments_m = {n: env["m_" + n] for n in WEIGHTS}
    moments_v = {n: env["v_" + n] for n in WEIGHTS}

    axis_of = {n: axis - 1 for n, axis in BIG}

    packed_names = [n for n in MATMUL_WEIGHTS if n != "w_in"]

    def payload_of_layer(layer):
        cw = weights["conv_w"][layer]
        c1 = cw.astype(BF16)
        c2 = (cw - c1.astype(F32)).astype(BF16)
        c3 = (cw - c1.astype(F32) - c2.astype(F32)).astype(BF16)
        return [weights["w_in"][layer].astype(BF16),
                _pack([weights[n][layer].astype(BF16) for n in packed_names] + [c1, c2, c3])]

    def weights_of_layer(layer, gathered):
        w_in_blocks, packed = gathered
        shapes = [weights[n].shape[1:] for n in packed_names] + [weights["conv_w"].shape[1:]] * 3
        blocks = _unpack(packed, shapes)
        wb = {n: _shards_to_full(blk, axis_of[n]) for n, blk in zip(packed_names, blocks)}
        wb["w_in"] = _w_in_from_shards(w_in_blocks)
        return wb, _shards_to_full(sum(b.astype(F32) for b in blocks[-3:]), axis_of["conv_w"])

    tabs = rope_tables(positions[0])
    first = _all_gather(payload_of_layer(0), name="gather_weights")
    loss, dx, dparams = _local_step(x[0], _kernel_params({n: weights[n] for n in SMALL}), first,
                                    payload_of_layer, weights_of_layer, tabs, loss_target[0])
    grads = _reference_grads(dparams)
    loss = lax.psum(loss, ("x", "y", "c"))

    core = lax.axis_index("c")
    rest = [(n, axis) for n, axis in BIG if n != "w_in"]
    w_in_rows = DEPTH * D_MODEL

    def by_core(shards, which):
        sh = shards.reshape((N_CHIP, 2) + shards.shape[1:])
        return lax.dynamic_index_in_dim(sh, which, axis=1, keepdims=False).astype(BF16)

    def halves(which):
        w_in_half = by_core(_w_in_to_shards(dparams["w_in"]), which).reshape(N_CHIP, w_in_rows, W_IN_SHARD)
        return [w_in_half, _pack([by_core(_full_to_shards(grads[n], axis), which) for n, axis in rest], lead=1)]

    from_sibling = _swap_with_sibling(halves(1 - core), name="scatter_grads_d2d")
    chip_sums = [_add_bf16_call(keep, got, name="scatter_grads_add")
                 for keep, got in zip(halves(core), from_sibling)]
    w_in_parts, rest_parts = _chip_all_to_all(chip_sums, name="scatter_grads_ici")
    small_parts, = _all_gather([_pack([grads[n] for n in SMALL])], name="gather_small_grads")

    out = {}
    results = _adamw_call(w_in_parts, *[d["w_in"].reshape(w_in_rows, W_IN_SHARD)
                                        for d in (weights, moments_m, moments_v)], name="adamw_w_in")
    for kind, buf in zip(("grad", "delta", "new_m", "new_v"), results):
        out[kind, "w_in"] = buf.reshape(weights["w_in"].shape)
    for names, parts in (([n for n, _ in rest], rest_parts), (list(SMALL), small_parts)):
        shapes = [weights[n].shape for n in names]
        packed = [_pack([d[n] for n in names]) for d in (weights, moments_m, moments_v)]
        results = _adamw_call(parts, *packed, name="adamw_" + names[0])
        for kind, buf in zip(("grad", "delta", "new_m", "new_v"), results):
            for n, arr in zip(names, _unpack(buf, shapes)):
                out[kind, n] = arr
    return (loss, dx[None], *[out[kind, n] for kind in ("grad", "delta", "new_m", "new_v") for n in WEIGHTS])
```

```python
import math

import jax
import jax.numpy as jnp
from jax import lax
from jax.experimental import pallas as pl
from jax.experimental.pallas import tpu as pltpu

F32 = jnp.float32
BF16 = jnp.bfloat16

N_DEV = 8
D_MODEL = 1024
DEPTH = 2
HEAD_DIM = 64
ROT_DIM = 16
ROPE_THETA = 500000.0
BLK = 128
NEG_INF = -1e30
EPS = 1e-6
A_CONFIGS = ((128, 1), (512, 4), (2048, 16))
B_GROUP = 4
C_QK_HEADS = 4
C_V_HEADS = 8
C_DK = 128
C_CONV = 4
CHUNK = 64
ADAM_LR = 0.001
ADAM_B1 = 0.9
ADAM_B2 = 0.999
ADAM_EPS = 1e-08
ADAM_WD = 0.01
ADAM_STEP = 10

IN_LAYOUT = (
    ("gate_a", 0, 1024, 5392), ("gate_b", 1024, 1024, 6416), ("gate_c", 2048, 1024, 7440),
    ("a_q", 3072, 512, 0), ("a_k", 3584, 512, 512), ("a_v", 4096, 512, 1024), ("b_q", 4608, 512, 1536),
    ("c_z", 5120, 1024, 4352), ("c_qkv", 6144, 2048, 2304),
    ("b_k", 8192, 128, 2048), ("b_v", 8320, 128, 2176), ("c_ab", 8448, 16, 5376),
)
COL = {name: start for name, start, _, _ in IN_LAYOUT}
D_IN_PAD = 8704
IN_TN = D_IN_PAD // 4
LANES = 128
VMEM_LIMIT = 56 * 1024 * 1024


def _cparams(sem=None):
    return pltpu.CompilerParams(dimension_semantics=sem, vmem_limit_bytes=VMEM_LIMIT)


def _relu2(t):
    return jnp.square(jnp.maximum(t, 0.0))


def _mm(a, b, *, ta=False, tb=False, bias=None, a_fn=None, mul_drelu2=None, add=None,
        out_dtype=F32, relu2_out=False, b_colsum=False, tm=1024, tn=1024, tk=2048, name):
    if ta:
        kdim, m = a.shape
    else:
        m, kdim = a.shape
    n = b.shape[0] if tb else b.shape[1]
    tm, tn, tk = min(tm, m), min(tn, n), min(tk, kdim)
    assert m % tm == 0 and n % tn == 0 and kdim % tk == 0, (a.shape, b.shape, tm, tn, tk)
    nk = kdim // tk
    assert not b_colsum or (m == tm and not tb and nk > 1)
    dims = (((0 if ta else 1,), (1 if tb else 0,)), ((), ()))
    extras = [e for e in (bias, mul_drelu2, add) if e is not None]

    def body(*refs):
        a_ref, b_ref = refs[0], refs[1]
        pos = 2
        bias_ref = pre_ref = add_ref = None
        if bias is not None:
            bias_ref = refs[pos]; pos += 1
        if mul_drelu2 is not None:
            pre_ref = refs[pos]; pos += 1
        if add is not None:
            add_ref = refs[pos]; pos += 1
        o_ref = refs[pos]
        pos += 1
        r_ref = None
        if relu2_out:
            r_ref = refs[pos]; pos += 1
        cs_ref = None
        if b_colsum:
            cs_ref = refs[pos]; pos += 1
        acc_ref = refs[pos] if nk > 1 else None
        cs_acc = refs[pos + 1] if b_colsum else None

        av = a_ref[...]
        if a_fn is not None:
            av = a_fn(av)
        bv = b_ref[...]
        part = lax.dot_general(av.astype(BF16), bv.astype(BF16), dims,
                               preferred_element_type=F32)
        if b_colsum:
            cs_part = jnp.sum(bv.astype(F32).reshape(tk // 8, 8, tn), axis=0)

        def finish(acc):
            if bias_ref is not None:
                acc = acc + bias_ref[...]
            if pre_ref is not None:
                acc = acc * (2.0 * jnp.maximum(pre_ref[...], 0.0))
            if add_ref is not None:
                acc = acc + add_ref[...]
            o_ref[...] = acc.astype(out_dtype)
            if r_ref is not None:
                r_ref[...] = _relu2(acc).astype(BF16)

        if nk == 1:
            finish(part)
        else:
            k = pl.program_id(2)

            @pl.when(k == 0)
            def _():
                acc_ref[...] = part
                if b_colsum:
                    cs_acc[...] = cs_part

            @pl.when(k > 0)
            def _():
                acc_ref[...] += part
                if b_colsum:
                    cs_acc[...] += cs_part

            @pl.when(k == nk - 1)
            def _():
                finish(acc_ref[...])
                if b_colsum:
                    cs_ref[...] = jnp.sum(cs_acc[...], axis=0, keepdims=True)

    a_spec = (pl.BlockSpec((tk, tm), lambda i, j, k: (k, i)) if ta
              else pl.BlockSpec((tm, tk), lambda i, j, k: (i, k)))
    b_spec = (pl.BlockSpec((tn, tk), lambda i, j, k: (j, k)) if tb
              else pl.BlockSpec((tk, tn), lambda i, j, k: (k, j)))
    in_specs = [a_spec, b_spec]
    if bias is not None:
        in_specs.append(pl.BlockSpec((1, tn), lambda i, j, k: (0, j)))
    for _ in extras[(1 if bias is not None else 0):]:
        in_specs.append(pl.BlockSpec((tm, tn), lambda i, j, k: (i, j)))
    o_spec = pl.BlockSpec((tm, tn), lambda i, j, k: (i, j))
    out_specs, out_shape = [o_spec], [jax.ShapeDtypeStruct((m, n), out_dtype)]
    scratch = [pltpu.VMEM((tm, tn), F32)] if nk > 1 else []
    if relu2_out:
        out_specs.append(o_spec)
        out_shape.append(jax.ShapeDtypeStruct((m, n), BF16))
    if b_colsum:
        out_specs.append(pl.BlockSpec((1, tn), lambda i, j, k: (0, j)))
        out_shape.append(jax.ShapeDtypeStruct((1, n), F32))
        scratch.append(pltpu.VMEM((8, tn), F32))
    single = len(out_specs) == 1
    return pl.pallas_call(
        body, name=name,
        grid=(m // tm, n // tn, nk),
        in_specs=in_specs,
        out_specs=out_specs[0] if single else out_specs,
        out_shape=out_shape[0] if single else out_shape,
        scratch_shapes=scratch,
        compiler_params=_cparams(("parallel", "parallel", "arbitrary")),
    )(a, b, *extras)


def _rms_fwd_call(x, g, *, name, out_dtype=F32, tq=512):
    t, d = x.shape

    def body(x_ref, g_ref, y_ref):
        xv = x_ref[...]
        r = lax.rsqrt(jnp.mean(xv * xv, axis=-1, keepdims=True) + EPS)
        y_ref[...] = (xv * r * g_ref[...]).astype(out_dtype)

    return pl.pallas_call(
        body, name=name, grid=(t // tq,),
        in_specs=[pl.BlockSpec((tq, d), lambda i: (i, 0)), pl.BlockSpec((1, d), lambda i: (0, 0))],
        out_specs=pl.BlockSpec((tq, d), lambda i: (i, 0)),
        out_shape=jax.ShapeDtypeStruct((t, d), out_dtype),
        compiler_params=_cparams(("parallel",)),
    )(x, g)


def _rms_bwd_call(x, g, dy, *, name, add=None, tq=512):
    t, d = x.shape
    nt = t // tq

    def body(*refs):
        if add is None:
            x_ref, g_ref, dy_ref, dx_ref, dg_ref, acc_ref = refs
        else:
            x_ref, g_ref, dy_ref, add_ref, dx_ref, dg_ref, acc_ref = refs
        i = pl.program_id(0)
        xv = x_ref[...]
        r = lax.rsqrt(jnp.mean(xv * xv, axis=-1, keepdims=True) + EPS)
        xh = xv * r
        dyv = dy_ref[...]
        dxh = dyv * g_ref[...]
        dx = r * (dxh - xh * jnp.mean(dxh * xh, axis=-1, keepdims=True))
        dx_ref[...] = dx if add is None else dx + add_ref[...]
        part = jnp.sum((dyv * xh).reshape(tq // 8, 8, d), axis=0)

        @pl.when(i == 0)
        def _():
            acc_ref[...] = part

        @pl.when(i > 0)
        def _():
            acc_ref[...] += part

        @pl.when(i == nt - 1)
        def _():
            dg_ref[...] = jnp.sum(acc_ref[...], axis=0, keepdims=True)

    blk = pl.BlockSpec((tq, d), lambda i: (i, 0))
    row = pl.BlockSpec((1, d), lambda i: (0, 0))
    extra = [] if add is None else [add]
    return pl.pallas_call(
        body, name=name, grid=(nt,),
        in_specs=[blk, row, blk] + [blk] * len(extra),
        out_specs=[blk, row],
        out_shape=[jax.ShapeDtypeStruct((t, d), F32), jax.ShapeDtypeStruct((1, d), F32)],
        scratch_shapes=[pltpu.VMEM((8, d), F32)],
        compiler_params=_cparams(("arbitrary",)),
    )(x, g, dy, *extra)


def _loss_call(x, g, tgt, *, tq=512):
    t, d = x.shape
    nt = t // tq

    def body(x_ref, g_ref, t_ref, loss_ref, dx_ref, dg_ref, acc_ref, sq_ref):
        i = pl.program_id(0)
        xv = x_ref[...]
        r = lax.rsqrt(jnp.mean(xv * xv, axis=-1, keepdims=True) + EPS)
        xh = xv * r
        gv = g_ref[...]
        err = xh * gv - t_ref[...]
        dyv = err * (1.0 / d)
        dxh = dyv * gv
        dx_ref[...] = r * (dxh - xh * jnp.mean(dxh * xh, axis=-1, keepdims=True))
        part = jnp.sum((dyv * xh).reshape(tq // 8, 8, d), axis=0)
        sq = jnp.sum((err * err).reshape(tq // 8, 8, d), axis=0)

        @pl.when(i == 0)
        def _():
            acc_ref[...] = part
            sq_ref[...] = sq

        @pl.when(i > 0)
        def _():
            acc_ref[...] += part
            sq_ref[...] += sq

        @pl.when(i == nt - 1)
        def _():
            dg_ref[...] = jnp.sum(acc_ref[...], axis=0, keepdims=True)
            tot = jnp.sum(jnp.sum(sq_ref[...], axis=0, keepdims=True), axis=1, keepdims=True)
            loss_ref[...] = jnp.broadcast_to(tot * (0.5 / d), (8, LANES))

    blk = pl.BlockSpec((tq, d), lambda i: (i, 0))
    row = pl.BlockSpec((1, d), lambda i: (0, 0))
    loss, dx, dg = pl.pallas_call(
        body, name="loss", grid=(nt,),
        in_specs=[blk, row, blk],
        out_specs=[pl.BlockSpec((8, LANES), lambda i: (0, 0)), blk, row],
        out_shape=[jax.ShapeDtypeStruct((8, LANES), F32), jax.ShapeDtypeStruct((t, d), F32),
                   jax.ShapeDtypeStruct((1, d), F32)],
        scratch_shapes=[pltpu.VMEM((8, d), F32), pltpu.VMEM((8, d), F32)],
        compiler_params=_cparams(("arbitrary",)),
    )(x, g, tgt)
    return loss[0, 0], dx, dg


MESH_ID = pl.DeviceIdType.MESH
HBM_SPEC = pl.BlockSpec(memory_space=pl.ANY)


def rope_tables(positions):
    inv_freq = jnp.power(ROPE_THETA, -jnp.arange(0, ROT_DIM, 2, dtype=F32) / ROT_DIM)
    ang = positions.astype(F32)[:, None] * inv_freq
    cos, sin = jnp.cos(ang), jnp.sin(ang)
    t = positions.shape[0]
    one = jnp.ones((t, HEAD_DIM - ROT_DIM), F32)
    zero8 = jnp.zeros((t, ROT_DIM // 2), F32)
    zero = jnp.zeros((t, HEAD_DIM - ROT_DIM), F32)
    a = jnp.concatenate([cos, cos, one], axis=1)
    b = jnp.concatenate([zero8, sin, zero], axis=1)
    c = jnp.concatenate([-sin, zero8, zero], axis=1)
    return tuple(jnp.concatenate([m, m], axis=1) for m in (a, b, c))


def _rope_chunk(xs, a, b, c, transpose):
    half = ROT_DIM // 2
    if transpose:
        return xs * a + pltpu.roll(xs * b, LANES - half, 1) + pltpu.roll(xs * c, half, 1)
    return xs * a + pltpu.roll(xs, half, 1) * b + pltpu.roll(xs, LANES - half, 1) * c


def _rope_gather_call(u, tabs, parts, *, name, tq=512):
    t = u.shape[0]
    total = sum(w for _, w, _, _ in parts)
    assert all(start % w == 0 for start, w, _, _ in parts)

    def body(a_ref, b_ref, c_ref, *refs):
        o_ref = refs[-1]
        a, b, c = a_ref[...], b_ref[...], c_ref[...]
        off = 0
        for x_ref, (_, w, roped, scale) in zip(refs[:-1], parts):
            for j in range(w // LANES):
                xs = x_ref[:, j * LANES:(j + 1) * LANES]
                val = _rope_chunk(xs, a, b, c, False) if roped else xs
                o_ref[:, off + j * LANES:off + (j + 1) * LANES] = (val * scale if scale != 1.0 else val).astype(BF16)
            off += w

    tab_spec = pl.BlockSpec((tq, LANES), lambda i: (i, 0))
    return pl.pallas_call(
        body, name=name, grid=(t // tq,),
        in_specs=[tab_spec] * 3 + [pl.BlockSpec((tq, w), lambda i, cb=start // w: (i, cb)) for start, w, _, _ in parts],
        out_specs=pl.BlockSpec((tq, total), lambda i: (i, 0)),
        out_shape=jax.ShapeDtypeStruct((t, total), BF16),
        compiler_params=_cparams(("parallel",)),
    )(*tabs, *([u] * len(parts)))


def _du_operands(du_buf, n_inputs):
    if du_buf is None:
        return [], [], {}
    return [du_buf], [HBM_SPEC], {n_inputs: 0}


def _du_shape(t):
    return jax.ShapeDtypeStruct((t, D_IN_PAD), BF16)


def _rope_scatter_call(du_buf, t, pieces, col, tabs, *, name, tq=512):
    total = sum(w for _, w, _, _ in pieces)
    assert col % total == 0
    arrays = [a for arrs, _, _, _ in pieces for a in arrs]
    extra, extra_specs, aliases = _du_operands(du_buf, 3 + len(arrays))

    def body(a_ref, b_ref, c_ref, *refs):
        o_ref = refs[len(arrays) + len(extra)]
        a, b, c = a_ref[...], b_ref[...], c_ref[...]
        k = off = 0
        for arrs, w, roped, scale in pieces:
            mine = refs[k:k + len(arrs)]
            k += len(arrs)
            for j in range(w // LANES):
                cs = slice(j * LANES, (j + 1) * LANES)
                xs = mine[0][:, cs].astype(F32)
                for r in mine[1:]:
                    xs = xs + r[:, cs].astype(F32)
                if scale != 1.0:
                    xs = xs * scale
                val = _rope_chunk(xs, a, b, c, True) if roped else xs
                o_ref[:, off + j * LANES:off + (j + 1) * LANES] = val.astype(BF16)
            off += w

    tab_spec = pl.BlockSpec((tq, LANES), lambda i: (i, 0))
    in_specs = [tab_spec] * 3 + [pl.BlockSpec((tq, w), lambda i: (i, 0)) for arrs, w, _, _ in pieces for _ in arrs]
    return pl.pallas_call(
        body, name=name, grid=(t // tq,),
        in_specs=in_specs + extra_specs,
        out_specs=pl.BlockSpec((tq, total), lambda i: (i, col // total)),
        out_shape=_du_shape(t), input_output_aliases=aliases,
        compiler_params=_cparams(("parallel",)),
    )(*tabs, *arrays, *extra)


def _band_masks(first_block, max_dist):
    qi = lax.broadcasted_iota(jnp.int32, (BLK, BLK), 0)
    kj = lax.broadcasted_iota(jnp.int32, (BLK, BLK), 1)
    valid_prev = jnp.logical_and(kj >= qi + (BLK - max_dist), jnp.logical_not(first_block))
    valid_cur = kj <= qi
    return valid_prev, valid_cur


_NN = (((1,), (0,)), ((), ()))
_NT = (((1,), (1,)), ((), ()))
_TN = (((0,), (0,)), ((), ()))


HEAD_STAGE = 8


def _attn_row_maps(nb):
    def cur(i):
        return jnp.minimum(i, nb - 1)

    def prev(i):
        return jnp.maximum(jnp.minimum(i, nb - 1) - 1, 0)

    return cur, prev


def _dil_view(a, dil):
    t, w = a.shape
    return a.reshape(t // dil, dil * w)


def _dil_spec(w, dil, rows, seg=None, off=0):
    seg = w if seg is None else seg
    assert off % w == 0 and (dil == 1 or seg % w == 0)
    return pl.BlockSpec((BLK, w), lambda r, i: (rows(i), (r * seg + off) // w))


def _dil_shape(l, dil, w, dtype=F32):
    return jax.ShapeDtypeStruct((l, dil * w), dtype)


def _attn_fwd_call(qkv2, sink, *, dil, group, max_dist, seg, offs, qw, kw, name):
    l = qkv2.shape[0]
    nh = qw // HEAD_DIM
    nb = l // BLK
    use_sink = sink is not None

    def body(*refs):
        if use_sink:
            sink_ref, refs = refs[0], refs[1:]
        q_ref, kp_ref, kc_ref, vp_ref, vc_ref, o_ref, lse_ref = refs
        valid_prev, valid_cur = _band_masks(pl.program_id(1) == 0, max_dist)
        lane = lax.broadcasted_iota(jnp.int32, (BLK, LANES), 1)
        lse_tile = jnp.zeros((BLK, LANES), F32)

        def dot(a, b, dims=_NN):
            return lax.dot_general(a, b, dims, preferred_element_type=F32)

        for g0 in range(0, nh, HEAD_STAGE):
            heads = list(range(g0, min(g0 + HEAD_STAGE, nh)))
            kv = {}
            for kh in sorted({h // group for h in heads}):
                ks = slice(kh * HEAD_DIM, (kh + 1) * HEAD_DIM)
                kv[kh] = tuple(ref[:, ks].astype(BF16) for ref in (kp_ref, kc_ref, vp_ref, vc_ref))
            qs = [q_ref[:, h * HEAD_DIM:(h + 1) * HEAD_DIM].astype(BF16) for h in heads]
            sps = [jnp.where(valid_prev, dot(qh, kv[h // group][0], _NT), NEG_INF) for h, qh in zip(heads, qs)]
            scs = [jnp.where(valid_cur, dot(qh, kv[h // group][1], _NT), NEG_INF) for h, qh in zip(heads, qs)]
            ms = [jnp.maximum(jnp.max(sp, axis=1, keepdims=True), jnp.max(sc, axis=1, keepdims=True))
                  for sp, sc in zip(sps, scs)]
            if use_sink:
                ms = [jnp.maximum(m, sink_ref[h]) for h, m in zip(heads, ms)]
            pps = [jnp.exp(sp - m) for sp, m in zip(sps, ms)]
            pcs = [jnp.exp(sc - m) for sc, m in zip(scs, ms)]
            dens = [jnp.sum(pp, axis=1, keepdims=True) + jnp.sum(pc, axis=1, keepdims=True)
                    for pp, pc in zip(pps, pcs)]
            if use_sink:
                dens = [den + jnp.exp(sink_ref[h] - m) for h, den, m in zip(heads, dens, ms)]
            outs = [dot(pp.astype(BF16), kv[h // group][2]) + dot(pc.astype(BF16), kv[h // group][3])
                    for h, pp, pc in zip(heads, pps, pcs)]
            for h, o, den, m in zip(heads, outs, dens, ms):
                o_ref[:, h * HEAD_DIM:(h + 1) * HEAD_DIM] = o / den
                lse_tile = jnp.where(lane == h, m + jnp.log(den), lse_tile)
        lse_ref[...] = lse_tile

    cur, prev = _attn_row_maps(nb)
    o_spec, lse_spec = _dil_spec(qw, dil, cur), _dil_spec(LANES, dil, cur)
    in_specs = [_dil_spec(qw, dil, cur, seg, offs[0]),
                _dil_spec(kw, dil, prev, seg, offs[1]), _dil_spec(kw, dil, cur, seg, offs[1]),
                _dil_spec(kw, dil, prev, seg, offs[2]), _dil_spec(kw, dil, cur, seg, offs[2])]
    args = [qkv2] * 5
    if use_sink:
        in_specs = [pl.BlockSpec(memory_space=pltpu.SMEM)] + in_specs
        args = [sink] + args
    return pl.pallas_call(
        body, name=name, grid=(dil, nb),
        in_specs=in_specs,
        out_specs=[o_spec, lse_spec],
        out_shape=[_dil_shape(l, dil, qw), _dil_shape(l, dil, LANES)],
        compiler_params=_cparams(("parallel", "parallel")),
    )(*args)


def _attn_bwd_call(qkv2, sink, o2, lse2, do2, dlse2, *, dil, group, max_dist, seg, offs, qw, kw, name):
    l = qkv2.shape[0]
    nh = qw // HEAD_DIM
    nb = l // BLK
    use_sink = sink is not None

    def body(*refs):
        if use_sink:
            sink_ref, refs = refs[0], refs[1:]
        (q_ref, kp_ref, kc_ref, vp_ref, vc_ref, o_ref, lse_ref, do_ref, dlse_ref,
         dq_ref, dk_ref, dv_ref, dsink_ref, ck_ref, cv_ref) = refs
        step = pl.program_id(1)

        @pl.when(jnp.logical_and(pl.program_id(0) == 0, step == 0))
        def _():
            dsink_ref[...] = jnp.zeros_like(dsink_ref)

        @pl.when(step == 0)
        def _():
            ck_ref[...] = jnp.zeros_like(ck_ref)
            cv_ref[...] = jnp.zeros_like(cv_ref)

        def dot(a, b, dims=_NN):
            return lax.dot_general(a, b, dims, preferred_element_type=F32)

        @pl.when(step < nb)
        def _():
            valid_prev, valid_cur = _band_masks(step == 0, max_dist)
            row = lax.broadcasted_iota(jnp.int32, (8, LANES), 0)
            lanes8 = lax.broadcasted_iota(jnp.int32, (8, LANES), 1)
            ds_tile = jnp.zeros((8, LANES), F32)
            for g0 in range(0, nh, HEAD_STAGE):
                heads = list(range(g0, min(g0 + HEAD_STAGE, nh)))
                hss = [slice(h * HEAD_DIM, (h + 1) * HEAD_DIM) for h in heads]
                kv = {}
                for kh in sorted({h // group for h in heads}):
                    ks = slice(kh * HEAD_DIM, (kh + 1) * HEAD_DIM)
                    kv[kh] = tuple(ref[:, ks].astype(BF16) for ref in (kp_ref, kc_ref, vp_ref, vc_ref))
                qs = [q_ref[:, hs].astype(BF16) for hs in hss]
                dos = [do_ref[:, hs] for hs in hss]
                dobs = [d.astype(BF16) for d in dos]
                lses = [lse_ref[:, h:h + 1] for h in heads]
                sps = [dot(qh, kv[h // group][0], _NT) for h, qh in zip(heads, qs)]
                scs = [dot(qh, kv[h // group][1], _NT) for h, qh in zip(heads, qs)]
                dpps = [dot(dob, kv[h // group][2], _NT) for h, dob in zip(heads, dobs)]
                dpcs = [dot(dob, kv[h // group][3], _NT) for h, dob in zip(heads, dobs)]
                pps = [jnp.where(valid_prev, jnp.exp(jnp.where(valid_prev, sp, NEG_INF) - ls), 0.0)
                       for sp, ls in zip(sps, lses)]
                pcs = [jnp.where(valid_cur, jnp.exp(jnp.where(valid_cur, sc, NEG_INF) - ls), 0.0)
                       for sc, ls in zip(scs, lses)]
                deltas = [jnp.sum(d * o_ref[:, hs], axis=1, keepdims=True) for d, hs in zip(dos, hss)]
                corrs = [dlse_ref[:, h:h + 1] - dl for h, dl in zip(heads, deltas)]
                dsps = [(pp * (dp + c)).astype(BF16) for pp, dp, c in zip(pps, dpps, corrs)]
                dscs = [(pc * (dp + c)).astype(BF16) for pc, dp, c in zip(pcs, dpcs, corrs)]
                for h, hs, dsp, dsc in zip(heads, hss, dsps, dscs):
                    dq = dot(dsp, kv[h // group][0]) + dot(dsc, kv[h // group][1])
                    dq_ref[:, hs] = dq.astype(BF16)
                parts = [(dot(dsc, qh, _TN), dot(dsp, qh, _TN),
                          dot(pc.astype(BF16), dob, _TN), dot(pp.astype(BF16), dob, _TN))
                         for dsc, dsp, qh, pc, pp, dob in zip(dscs, dsps, qs, pcs, pps, dobs)]
                for kh in kv:
                    ks = slice(kh * HEAD_DIM, (kh + 1) * HEAD_DIM)
                    mine = [p for h, p in zip(heads, parts) if h // group == kh]
                    dkc, dkp, dvc, dvp = (sum(p[j] for p in mine[1:]) + mine[0][j] for j in range(4))
                    dk_ref[:, ks] = (ck_ref[:, ks] + dkp).astype(BF16)
                    dv_ref[:, ks] = (cv_ref[:, ks] + dvp).astype(BF16)
                    ck_ref[:, ks] = dkc
                    cv_ref[:, ks] = dvc
                if use_sink:
                    for h, ls, dl in zip(heads, lses, deltas):
                        val = -jnp.sum(jnp.exp(sink_ref[h] - ls) * dl, axis=0, keepdims=True)
                        ds_tile = jnp.where(jnp.logical_and(row == 0, lanes8 == h), val, ds_tile)
            if use_sink:
                dsink_ref[...] += ds_tile

        @pl.when(step == nb)
        def _():
            dk_ref[...] = ck_ref[...].astype(BF16)
            dv_ref[...] = cv_ref[...].astype(BF16)

    cur, prev = _attn_row_maps(nb)
    q_spec, lse_spec = _dil_spec(qw, dil, cur), _dil_spec(LANES, dil, cur)
    lag_spec = _dil_spec(kw, dil, lambda i: jnp.maximum(i - 1, 0))
    in_specs = [_dil_spec(qw, dil, cur, seg, offs[0]),
                _dil_spec(kw, dil, prev, seg, offs[1]), _dil_spec(kw, dil, cur, seg, offs[1]),
                _dil_spec(kw, dil, prev, seg, offs[2]), _dil_spec(kw, dil, cur, seg, offs[2]),
                q_spec, lse_spec, q_spec, lse_spec]
    args = [qkv2] * 5 + [o2, lse2, do2, dlse2]
    if use_sink:
        in_specs = [pl.BlockSpec(memory_space=pltpu.SMEM)] + in_specs
        args = [sink] + args
    kv_shape = _dil_shape(l, dil, kw, BF16)
    return pl.pallas_call(
        body, name=name, grid=(dil, nb + 1),
        in_specs=in_specs,
        out_specs=[q_spec, lag_spec, lag_spec, pl.BlockSpec((8, LANES), lambda r, i: (0, 0))],
        out_shape=[_dil_shape(l, dil, qw, BF16), kv_shape, kv_shape,
                   jax.ShapeDtypeStruct((8, LANES), F32)],
        scratch_shapes=[pltpu.VMEM((BLK, kw), F32), pltpu.VMEM((BLK, kw), F32)],
        compiler_params=_cparams(("arbitrary", "arbitrary")),
    )(*args)


def _attn_config(tag, dil, group, max_dist, seg, offs, qw, kw):
    return dict(name=tag, dil=dil, group=group, max_dist=max_dist, seg=seg, offs=offs, qw=qw, kw=kw)


A_W = 8 * HEAD_DIM
ATTN_A_CFGS = tuple(_attn_config("attn_a%d" % dil, dil, 1, window // dil, 3 * A_W, (0, A_W, 2 * A_W), A_W, A_W)
                    for window, dil in A_CONFIGS)
B_KVW = 2 * HEAD_DIM
ATTN_B_CFG = _attn_config("attn_b", 1, B_GROUP, BLK - 1, A_W + 2 * B_KVW, (0, A_W, A_W + B_KVW), A_W, B_KVW)


def _attn_fwd(cfg, qkv, sink):
    t = qkv.shape[0]
    kw = {k: v for k, v in cfg.items() if k != "name"}
    qkv2 = _dil_view(qkv, cfg["dil"])
    o2, lse2 = _attn_fwd_call(qkv2, sink, name=cfg["name"] + "_fwd", **kw)
    return o2.reshape(t, cfg["qw"]), lse2.reshape(t, LANES), (qkv2, o2, lse2)


def _attn_bwd(cfg, saved, sink, do, dlse):
    qkv2, o2, lse2 = saved
    t = do.shape[0]
    kw = {k: v for k, v in cfg.items() if k != "name"}
    dq2, dk2, dv2, dsink = _attn_bwd_call(qkv2, sink, o2, lse2, _dil_view(do, cfg["dil"]),
                                          _dil_view(dlse, cfg["dil"]), name=cfg["name"] + "_bwd", **kw)
    return dq2.reshape(t, cfg["qw"]), dk2.reshape(t, cfg["kw"]), dv2.reshape(t, cfg["kw"]), dsink


def _head_expand():
    r = lax.broadcasted_iota(jnp.int32, (LANES, 8 * HEAD_DIM), 0)
    c = lax.broadcasted_iota(jnp.int32, (LANES, 8 * HEAD_DIM), 1)
    return (c // HEAD_DIM == r).astype(F32)


def _combine_weights(l0, l1, l2):
    m = jnp.maximum(jnp.maximum(l0, l1), l2)
    e0, e1, e2 = jnp.exp(l0 - m), jnp.exp(l1 - m), jnp.exp(l2 - m)
    inv = 1.0 / (e0 + e1 + e2)
    return e0 * inv, e1 * inv, e2 * inv


def _combine_fwd_call(os_, lses, *, tq=512):
    t, w = os_[0].shape

    def body(o0, o1, o2, l0, l1, l2, y_ref):
        ws = _combine_weights(l0[...], l1[...], l2[...])
        e = _head_expand()
        y = jnp.zeros((tq, w), F32)
        for o_ref, wt in zip((o0, o1, o2), ws):
            y = y + _dot_mask(e, wt, mask_left=False) * o_ref[...]
        y_ref[...] = y

    o_spec = pl.BlockSpec((tq, w), lambda i: (i, 0))
    l_spec = pl.BlockSpec((tq, LANES), lambda i: (i, 0))
    return pl.pallas_call(
        body, name="combine_fwd", grid=(t // tq,),
        in_specs=[o_spec] * 3 + [l_spec] * 3, out_specs=o_spec,
        out_shape=jax.ShapeDtypeStruct((t, w), F32),
        compiler_params=_cparams(("parallel",)),
    )(*os_, *lses)


def _combine_bwd_call(os_, lses, dy, *, tq=512):
    t, w = dy.shape

    def body(o0, o1, o2, l0, l1, l2, dy_ref, do0, do1, do2, dl0, dl1, dl2):
        ws = _combine_weights(l0[...], l1[...], l2[...])
        e = _head_expand()
        dyv = dy_ref[...]
        dws = []
        for o_ref, do_ref, wt in zip((o0, o1, o2), (do0, do1, do2), ws):
            do_ref[...] = _dot_mask(e, wt, mask_left=False) * dyv
            dws.append(_dot_mask(e, dyv * o_ref[...], _NT, mask_left=False))
        mean = ws[0] * dws[0] + ws[1] * dws[1] + ws[2] * dws[2]
        for dl_ref, wt, dw in zip((dl0, dl1, dl2), ws, dws):
            dl_ref[...] = wt * (dw - mean)

    o_spec = pl.BlockSpec((tq, w), lambda i: (i, 0))
    l_spec = pl.BlockSpec((tq, LANES), lambda i: (i, 0))
    o_shape = jax.ShapeDtypeStruct((t, w), F32)
    l_shape = jax.ShapeDtypeStruct((t, LANES), F32)
    return pl.pallas_call(
        body, name="combine_bwd", grid=(t // tq,),
        in_specs=[o_spec] * 3 + [l_spec] * 3 + [o_spec], out_specs=[o_spec] * 3 + [l_spec] * 3,
        out_shape=[o_shape] * 3 + [l_shape] * 3,
        compiler_params=_cparams(("parallel",)),
    )(*os_, *lses, dy)


C_QKW = C_QK_HEADS * C_DK
C_CONV_W = 2 * C_QKW + C_V_HEADS * C_DK
HALO = 8


def _silu_parts(z):
    sig = jax.nn.sigmoid(z)
    return z * sig, sig * (1.0 + z * (1.0 - sig))


def _conv_window_specs(tq, t):
    c = C_CONV_W
    cb = COL["c_qkv"] // c
    blk = pl.BlockSpec((tq, c), lambda i: (i, cb))
    before = pl.BlockSpec((HALO, c), lambda i: (jnp.maximum(i * (tq // HALO) - 1, 0), cb))
    return c, cb, blk, before


def _conv_prep_fwd_call(u, w, *, tq=512):
    t = u.shape[0]
    c, _, x_spec, halo_spec = _conv_window_specs(tq, t)
    nqk = 2 * C_QK_HEADS

    def body(x_ref, halo_ref, w_ref, z_ref, qk_ref, v_ref):
        i = pl.program_id(0)
        halo = jnp.where(i == 0, 0.0, halo_ref[...])
        xc = jnp.concatenate([halo, x_ref[...]], axis=0)
        wv = w_ref[...]
        z = xc[HALO - 3:HALO - 3 + tq] * wv[0:1]
        for j in range(1, C_CONV):
            z = z + xc[HALO - 3 + j:HALO - 3 + j + tq] * wv[j:j + 1]
        z_ref[...] = z
        act, _ = _silu_parts(z)
        for h in range(nqk):
            a = act[:, h * C_DK:(h + 1) * C_DK]
            qk_ref[:, h * C_DK:(h + 1) * C_DK] = a * lax.rsqrt(jnp.sum(a * a, axis=1, keepdims=True) + EPS)
        v_ref[...] = act[:, nqk * C_DK:]

    return pl.pallas_call(
        body, name="conv_prep_fwd", grid=(t // tq,),
        in_specs=[x_spec, halo_spec, pl.BlockSpec((C_CONV, c), lambda i: (0, 0))],
        out_specs=[pl.BlockSpec((tq, c), lambda i: (i, 0)),
                   pl.BlockSpec((tq, 2 * C_QKW), lambda i: (i, 0)),
                   pl.BlockSpec((tq, c - 2 * C_QKW), lambda i: (i, 0))],
        out_shape=[jax.ShapeDtypeStruct((t, c), F32), jax.ShapeDtypeStruct((t, 2 * C_QKW), F32),
                   jax.ShapeDtypeStruct((t, c - 2 * C_QKW), F32)],
        compiler_params=_cparams(("parallel",)),
    )(u, u, w)


def _conv_prep_dz_call(z, dqk, dv, *, tq=512):
    t, c = z.shape
    nqk = 2 * C_QK_HEADS

    def body(z_ref, dqk_ref, dv_ref, dz_ref):
        zv = z_ref[...]
        act, dact = _silu_parts(zv)
        for h in range(nqk):
            hs = slice(h * C_DK, (h + 1) * C_DK)
            a = act[:, hs]
            r = lax.rsqrt(jnp.sum(a * a, axis=1, keepdims=True) + EPS)
            nrm = a * r
            dn = dqk_ref[:, hs]
            da = r * (dn - nrm * jnp.sum(dn * nrm, axis=1, keepdims=True))
            dz_ref[:, hs] = da * dact[:, hs]
        dz_ref[:, nqk * C_DK:] = dv_ref[...] * dact[:, nqk * C_DK:]

    return pl.pallas_call(
        body, name="conv_prep_dz", grid=(t // tq,),
        in_specs=[pl.BlockSpec((tq, c), lambda i: (i, 0)),
                  pl.BlockSpec((tq, 2 * C_QKW), lambda i: (i, 0)),
                  pl.BlockSpec((tq, c - 2 * C_QKW), lambda i: (i, 0))],
        out_specs=pl.BlockSpec((tq, c), lambda i: (i, 0)),
        out_shape=jax.ShapeDtypeStruct((t, c), F32),
        compiler_params=_cparams(("parallel",)),
    )(z, dqk, dv)


def _conv_bwd_call(u, dz, w, du_buf, *, tq=512):
    t = u.shape[0]
    nt = t // tq
    c, cb, x_spec, halo_spec = _conv_window_specs(tq, t)
    extra, extra_specs, aliases = _du_operands(du_buf, 5)

    def body(x_ref, xh_ref, dz_ref, dzh_ref, w_ref, *refs):
        dx_ref, dw_ref = refs[len(extra):]
        i = pl.program_id(0)
        xc = jnp.concatenate([jnp.where(i == 0, 0.0, xh_ref[...]), x_ref[...]], axis=0)
        dzv = dz_ref[...]
        dzc = jnp.concatenate([dzv, jnp.where(i == nt - 1, 0.0, dzh_ref[...])], axis=0)
        wv = w_ref[...]
        dx = dzv * wv[3:4]
        for s in range(1, C_CONV):
            dx = dx + dzc[s:s + tq] * wv[3 - s:4 - s]
        dx_ref[...] = dx.astype(BF16)
        row = lax.broadcasted_iota(jnp.int32, (8, c), 0)
        dw = jnp.zeros((8, c), F32)
        for j in range(C_CONV):
            prod = dzv * xc[HALO - 3 + j:HALO - 3 + j + tq]
            col = jnp.sum(jnp.sum(prod.reshape(tq // 8, 8, c), axis=0), axis=0, keepdims=True)
            dw = jnp.where(row == j, col, dw)

        @pl.when(i == 0)
        def _():
            dw_ref[...] = dw

        @pl.when(i > 0)
        def _():
            dw_ref[...] += dw

    blk = pl.BlockSpec((tq, c), lambda i: (i, 0))
    after = pl.BlockSpec((HALO, c), lambda i: (jnp.minimum((i + 1) * (tq // HALO), t // HALO - 1), 0))
    return pl.pallas_call(
        body, name="conv_bwd", grid=(nt,),
        in_specs=[x_spec, halo_spec, blk, after, pl.BlockSpec((C_CONV, c), lambda i: (0, 0))] + extra_specs,
        out_specs=[pl.BlockSpec((tq, c), lambda i: (i, cb)), pl.BlockSpec((8, c), lambda i: (0, 0))],
        out_shape=[_du_shape(t), jax.ShapeDtypeStruct((8, c), F32)],
        input_output_aliases=aliases,
        compiler_params=_cparams(("arbitrary",)),
    )(u, u, dz, dz, w, *extra)


C_VW = C_V_HEADS * C_DK


def _softplus(x):
    return jnp.maximum(x, 0.0) + jnp.log(1.0 + jnp.exp(-jnp.abs(x)))


def _tri_masks():
    r = lax.broadcasted_iota(jnp.int32, (CHUNK, CHUNK), 0)
    c = lax.broadcasted_iota(jnp.int32, (CHUNK, CHUNK), 1)
    return r >= c, r > c


def _split_bf16(a):
    hi = a.astype(BF16)
    return hi, (a - hi.astype(F32)).astype(BF16)


def _dot_hi(a, b, dims=None):
    dims = _NN if dims is None else dims
    ah, al = _split_bf16(a)
    bh, bl = _split_bf16(b)

    def d(x, y):
        return lax.dot_general(x, y, dims, preferred_element_type=F32)

    return d(ah, bh) + (d(ah, bl) + d(al, bh))


def _dot_mask(mask, b, dims=None, mask_left=True):
    dims = _NN if dims is None else dims
    mb = mask.astype(BF16)
    b1 = b.astype(BF16)
    rest = b - b1.astype(F32)
    b2 = rest.astype(BF16)
    b3 = (rest - b2.astype(F32)).astype(BF16)
    out = None
    for p in (b1, b2, b3):
        term = (lax.dot_general(mb, p, dims, preferred_element_type=F32) if mask_left
                else lax.dot_general(p, mb, dims, preferred_element_type=F32))
        out = term if out is None else out + term
    return out


def _unit_lower_inverses(mats):
    r = lax.broadcasted_iota(jnp.int32, (CHUNK, CHUNK), 0)
    c = lax.broadcasted_iota(jnp.int32, (CHUNK, CHUNK), 1)
    eye = (r == c).astype(F32)
    xs = [eye - a for a in mats]
    ps = [_dot_hi(a, a) for a in mats]
    steps = int(math.log2(CHUNK)) - 1
    for s in range(steps):
        xs = [x + _dot_hi(x, p) for x, p in zip(xs, ps)]
        if s < steps - 1:
            ps = [_dot_hi(p, p) for p in ps]
    return xs


def _gate_tiles(cab, alog, dtb):
    pre = cab + dtb
    g = -jnp.exp(alog) * _softplus(pre)
    beta = jax.nn.sigmoid(pltpu.roll(cab, LANES - C_V_HEADS, 1))
    return g, beta, pre


def _chunk_common(kk, qk, gc, gct, beta, h, tri, strict):
    gcol, grow, bcol = gc[:, h:h + 1], gct[h:h + 1, :], beta[:, h:h + 1]
    decay = jnp.where(tri, jnp.exp(jnp.where(tri, gcol - grow, 0.0)), 0.0)
    kkd = jnp.where(strict, kk * decay, 0.0)
    attn = jnp.where(tri, qk * decay, 0.0)
    glast = gc[CHUNK - 1:CHUNK, h:h + 1]
    return gcol, bcol, decay, kkd, attn, glast


def _cab_spec():
    return pl.BlockSpec((CHUNK, LANES), lambda n: (n, COL["c_ab"] // LANES))


def _delta_prep_call(qk, v, u, alog, dtb, gather_src=None):
    t = qk.shape[0]
    nc = t // CHUNK
    scale = C_DK ** -0.5
    riding = gather_src is not None
    ng = len(gather_src) if riding else 0

    def body(q_ref, k_ref, v_ref, cab_ref, alog_ref, dtb_ref, *refs):
        if riding:
            gather_refs = (refs[:ng], refs[ng + 8:2 * ng + 8]) + tuple(refs[2 * ng + 8:])
            refs = refs[ng:ng + 8]

            @pl.when(pl.program_id(0) == 0)
            def _():
                _gather_start(*gather_refs)
        u_ref, w_ref, qd_ref, kd_ref, attn_ref, tmat_ref, gc_ref, beta_ref = refs
        tri, strict = _tri_masks()
        g, beta, _ = _gate_tiles(cab_ref[...], alog_ref[...], dtb_ref[...])
        gc = _dot_mask(tri, g)
        gct = gc.T
        gc_ref[...] = gc
        beta_ref[...] = beta
        mats, rhs = [], []
        for j in range(C_QK_HEADS):
            js = slice(j * C_DK, (j + 1) * C_DK)
            kf, qf = k_ref[:, js], q_ref[:, js] * scale
            kb, qb = kf.astype(BF16), qf.astype(BF16)
            kk = lax.dot_general(kb, kb, _NT, preferred_element_type=F32)
            qk = lax.dot_general(qb, kb, _NT, preferred_element_type=F32)
            for h in (2 * j, 2 * j + 1):
                hs = slice(h * C_DK, (h + 1) * C_DK)
                gcol, bcol, decay, kkd, attn, glast = _chunk_common(kk, qk, gc, gct, beta, h, tri, strict)
                gexp = jnp.exp(gcol)
                mats.append(kkd * bcol)
                rhs.append(jnp.concatenate([v_ref[:, hs] * bcol, kf * (bcol * gexp)], axis=1))
                qd_ref[:, hs] = (qf * gexp).astype(BF16)
                kd_ref[:, hs] = (kf * jnp.exp(glast - gcol)).astype(BF16)
                attn_ref[:, h * CHUNK:(h + 1) * CHUNK] = attn.astype(BF16)
        for h, (tmat, r) in enumerate(zip(_unit_lower_inverses(mats), rhs)):
            hs = slice(h * C_DK, (h + 1) * C_DK)
            uw = _dot_hi(tmat, r)
            u_ref[:, hs] = uw[:, :C_DK]
            w_ref[:, hs] = uw[:, C_DK:]
            tmat_ref[:, h * CHUNK:(h + 1) * CHUNK] = tmat
        if riding:
            @pl.when(pl.program_id(0) == nc - 1)
            def _():
                _gather_finish(*gather_refs)

    def blk(w):
        return pl.BlockSpec((CHUNK, w), lambda n: (n, 0))

    row = pl.BlockSpec((1, LANES), lambda n: (0, 0))
    big = jax.ShapeDtypeStruct((t, C_VW), F32)
    sq = jax.ShapeDtypeStruct((t, C_V_HEADS * CHUNK), F32)
    tile = jax.ShapeDtypeStruct((t, LANES), F32)
    half = jax.ShapeDtypeStruct((t, C_VW), BF16)
    in_specs = [blk(C_QKW), pl.BlockSpec((CHUNK, C_QKW), lambda n: (n, 1)), blk(C_VW), _cab_spec(), row, row]
    out_specs = [blk(C_VW)] * 4 + [blk(C_V_HEADS * CHUNK)] * 2 + [blk(LANES)] * 2
    out_shape = [big, big, half, half, jax.ShapeDtypeStruct(sq.shape, BF16), sq] + [tile] * 2
    args = [qk, qk, v, u, alog, dtb]
    if riding:
        in_specs += [HBM_SPEC] * ng
        out_specs += [HBM_SPEC] * ng
        out_shape += _gathered_shapes(gather_src)
        args += list(gather_src)
    return pl.pallas_call(
        body, name="delta_prep_gather" if riding else "delta_prep", grid=(nc,),
        in_specs=in_specs, out_specs=out_specs, out_shape=out_shape,
        scratch_shapes=_gather_sems(ng) if riding else [],
        compiler_params=_cparams(("arbitrary",) if riding else ("parallel",)),
    )(*args)


SCAN_SUB = 4


def _delta_scan_call(u, w, qd, kd, attn, gc):
    t = u.shape[0]
    nc = t // CHUNK
    rows = SCAN_SUB * CHUNK

    def body(u_ref, w_ref, qd_ref, kd_ref, attn_ref, gc_ref, o_ref, vn_ref, st_ref, s_ref):
        @pl.when(pl.program_id(0) == 0)
        def _():
            s_ref[...] = jnp.zeros_like(s_ref)

        hss = [slice(h * C_DK, (h + 1) * C_DK) for h in range(C_V_HEADS)]
        states = [s_ref[hs, :] for hs in hss]
        for c in range(SCAN_SUB):
            rs = slice(c * CHUNK, (c + 1) * CHUNK)
            for hs, s in zip(hss, states):
                st_ref[c, hs, :] = s
            sbs = [s.astype(BF16) for s in states]
            vns = [u_ref[rs, hs] - jnp.dot(w_ref[rs, hs].astype(BF16), sb, preferred_element_type=F32)
                   for hs, sb in zip(hss, sbs)]
            qss = [jnp.dot(qd_ref[rs, hs].astype(BF16), sb, preferred_element_type=F32) for hs, sb in zip(hss, sbs)]
            vnbs = [vn.astype(BF16) for vn in vns]
            for h, hs in enumerate(hss):
                vn_ref[rs, hs] = vnbs[h]
                o_ref[rs, hs] = qss[h] + jnp.dot(attn_ref[rs, h * CHUNK:(h + 1) * CHUNK].astype(BF16), vnbs[h],
                                                 preferred_element_type=F32)
            last = (c + 1) * CHUNK - 1
            states = [states[h] * jnp.exp(gc_ref[last:last + 1, h:h + 1])
                      + lax.dot_general(kd_ref[rs, hs].astype(BF16), vnbs[h], _TN, preferred_element_type=F32)
                      for h, hs in enumerate(hss)]
        for hs, s in zip(hss, states):
            s_ref[hs, :] = s

    def blk(wd):
        return pl.BlockSpec((rows, wd), lambda n: (n, 0))

    big = jax.ShapeDtypeStruct((t, C_VW), F32)
    return pl.pallas_call(
        body, name="delta_scan", grid=(nc // SCAN_SUB,),
        in_specs=[blk(C_VW)] * 4 + [blk(C_V_HEADS * CHUNK), blk(LANES)],
        out_specs=[blk(C_VW), blk(C_VW), pl.BlockSpec((SCAN_SUB, C_VW, C_DK), lambda n: (n, 0, 0))],
        out_shape=[big, jax.ShapeDtypeStruct((t, C_VW), BF16), jax.ShapeDtypeStruct((nc, C_VW, C_DK), F32)],
        scratch_shapes=[pltpu.VMEM((C_VW, C_DK), F32)],
        compiler_params=_cparams(("arbitrary",)),
    )(u, w, qd, kd, attn, gc)


def _delta_scan_bwd_call(do, w, qd, kd, attn, gc, vn, st):
    t = do.shape[0]
    nc = t // CHUNK
    rows = SCAN_SUB * CHUNK
    steps = nc // SCAN_SUB

    def body(do_ref, w_ref, qd_ref, kd_ref, attn_ref, gc_ref, vn_ref, st_ref,
             du_ref, dw_ref, dqd_ref, dkd_ref, dattn_ref, dgl_ref, ds_ref):
        @pl.when(pl.program_id(0) == 0)
        def _():
            ds_ref[...] = jnp.zeros_like(ds_ref)

        tri, _ = _tri_masks()
        row = lax.broadcasted_iota(jnp.int32, (8, LANES), 0)
        lane = lax.broadcasted_iota(jnp.int32, (8, LANES), 1)
        hss = [slice(h * C_DK, (h + 1) * C_DK) for h in range(C_V_HEADS)]
        css = [slice(h * CHUNK, (h + 1) * CHUNK) for h in range(C_V_HEADS)]

        def dg(a, b, dims):
            return lax.dot_general(a, b, dims, preferred_element_type=F32)

        dsps = [ds_ref[hs, :] for hs in hss]
        for c in reversed(range(SCAN_SUB)):
            rs = slice(c * CHUNK, (c + 1) * CHUNK)
            dgl = jnp.zeros((8, LANES), F32)
            ss = [st_ref[c, hs, :] for hs in hss]
            sbs = [s.astype(BF16) for s in ss]
            dspbs = [d.astype(BF16) for d in dsps]
            dobs = [do_ref[rs, hs].astype(BF16) for hs in hss]
            vnbs = [vn_ref[rs, hs].astype(BF16) for hs in hss]
            dvns = [dg(attn_ref[rs, cs].astype(BF16), dob, _TN) + dg(kd_ref[rs, hs].astype(BF16), dspb, _NN)
                    for hs, cs, dob, dspb in zip(hss, css, dobs, dspbs)]
            for h, hs in enumerate(hss):
                dqd_ref[rs, hs] = dg(dobs[h], sbs[h], _NT)
                dkd_ref[rs, hs] = dg(vnbs[h], dspbs[h], _NT)
                dattn_ref[rs, css[h]] = jnp.where(tri, dg(dobs[h], vnbs[h], _NT), 0.0)
            dvnbs = [d.astype(BF16) for d in dvns]
            for h, hs in enumerate(hss):
                du_ref[rs, hs] = dvns[h]
                dw_ref[rs, hs] = -dg(dvnbs[h], sbs[h], _NT)
                tot = jnp.sum(jnp.sum(dsps[h] * ss[h], axis=0, keepdims=True), axis=1, keepdims=True)
                dgl = jnp.where(jnp.logical_and(row == 0, lane == h), tot, dgl)
            dgl_ref[c * 8:(c + 1) * 8, :] = dgl
            last = (c + 1) * CHUNK - 1
            dsps = [dg(qd_ref[rs, hs].astype(BF16), dobs[h], _TN) + jnp.exp(gc_ref[last:last + 1, h:h + 1]) * dsps[h]
                    - dg(w_ref[rs, hs].astype(BF16), dvnbs[h], _TN) for h, hs in enumerate(hss)]
        for hs, d in zip(hss, dsps):
            ds_ref[hs, :] = d

    def blk(wd):
        return pl.BlockSpec((rows, wd), lambda n: (steps - 1 - n, 0))

    big = jax.ShapeDtypeStruct((t, C_VW), F32)
    return pl.pallas_call(
        body, name="delta_scan_bwd", grid=(steps,),
        in_specs=[blk(C_VW)] * 4 + [blk(C_V_HEADS * CHUNK), blk(LANES), blk(C_VW),
                                    pl.BlockSpec((SCAN_SUB, C_VW, C_DK), lambda n: (steps - 1 - n, 0, 0))],
        out_specs=[blk(C_VW)] * 4 + [blk(C_V_HEADS * CHUNK),
                                     pl.BlockSpec((SCAN_SUB * 8, LANES), lambda n: (steps - 1 - n, 0))],
        out_shape=[big] * 4 + [jax.ShapeDtypeStruct((t, C_V_HEADS * CHUNK), F32),
                               jax.ShapeDtypeStruct((nc * 8, LANES), F32)],
        scratch_shapes=[pltpu.VMEM((C_VW, C_DK), F32)],
        compiler_params=_cparams(("arbitrary",)),
    )(do, w, qd, kd, attn, gc, vn, st)


PREP_SUB = 2


def _delta_prep_bwd_call(qk, v, proj, alog, dtb, tmat, u, w, gc, beta, du, dw, dqd, dkd, dattn, dgl, du_buf):
    t = qk.shape[0]
    extra, extra_specs, aliases = _du_operands(du_buf, 17)
    nc = t // CHUNK
    rows = PREP_SUB * CHUNK
    scale = C_DK ** -0.5

    def body(q_ref, k_ref, v_ref, cab_ref, alog_ref, dtb_ref, tmat_ref, u_ref, w_ref, gc_ref, beta_ref,
             du_ref, dw_ref, dqd_ref, dkd_ref, dattn_ref, dgl_ref, *outs):
        dcab_ref, dqk_ref, dv_ref, dpar_ref = outs[len(extra):]
        tri, strict = _tri_masks()
        ones = jnp.ones((CHUNK, LANES), F32)
        lane = lax.broadcasted_iota(jnp.int32, (CHUNK, LANES), 1)
        rowi = lax.broadcasted_iota(jnp.int32, (CHUNK, 1), 0)
        subs = range(PREP_SUB)
        rss = [slice(c * CHUNK, (c + 1) * CHUNK) for c in subs]
        betas = [beta_ref[rs, :] for rs in rss]

        def dot(x, y, dims=_NN):
            return lax.dot_general(x, y, dims, preferred_element_type=F32)

        heads = []
        for c, rs in zip(subs, rss):
            gc = gc_ref[rs, :]
            gct = gc.T
            for j in range(C_QK_HEADS):
                js = slice(j * C_DK, (j + 1) * C_DK)
                kf, qf = k_ref[rs, js], q_ref[rs, js] * scale
                kb, qb = kf.astype(BF16), qf.astype(BF16)
                kk = dot(kb, kb, _NT)
                qk = dot(qb, kb, _NT)
                for h in (2 * j, 2 * j + 1):
                    heads.append((c, rs, h, kf, qf, kb, qb) + _chunk_common(kk, qk, gc, gct, betas[c], h, tri, strict))

        def cols(h):
            return slice(h * C_DK, (h + 1) * C_DK)

        def sq(h):
            return slice(h * CHUNK, (h + 1) * CHUNK)

        dvks = [_dot_hi(tmat_ref[hd[1], sq(hd[2])],
                        jnp.concatenate([du_ref[hd[1], cols(hd[2])], dw_ref[hd[1], cols(hd[2])]], axis=1), _TN)
                for hd in heads]
        das = [-jnp.where(strict, _dot_hi(dvk, jnp.concatenate([u_ref[hd[1], cols(hd[2])], w_ref[hd[1], cols(hd[2])]],
                                                               axis=1), _NT), 0.0)
               for hd, dvk in zip(heads, dvks)]
        pre = []
        for (c, rs, h, kf, qf, kb, qb, gcol, bcol, decay, kkd, attn, glast), da in zip(heads, das):
            dattn_h = dattn_ref[rs, sq(h)]
            pre.append(((da * decay * bcol).astype(BF16), (dattn_h * decay).astype(BF16),
                        da * kkd * bcol + dattn_h * attn))
        mms = [(dot(dkk, hd[5]), dot(dkk, hd[5], _TN), dot(dqk, hd[6], _TN), dot(dqk, hd[5]),
                _dot_mask(ones, e, _TN, mask_left=False))
               for hd, (dkk, dqk, e) in zip(heads, pre)]
        dq_parts, dk_parts = {}, {}
        dgc_tiles = [jnp.zeros((CHUNK, LANES), F32) for _ in subs]
        db_tiles = [jnp.zeros((CHUNK, LANES), F32) for _ in subs]
        for (c, rs, h, kf, qf, kb, qb, gcol, bcol, decay, kkd, attn, glast), dvk, da, (_, _, e), mm in zip(
                heads, dvks, das, pre, mms):
            hs = cols(h)
            gexp = jnp.exp(gcol)
            fdec = jnp.exp(glast - gcol)
            dvb, dkb = dvk[:, :C_DK], dvk[:, C_DK:]
            dgc = jnp.sum(e, axis=1, keepdims=True) - mm[4][:, :1]
            dk_parts[c, h] = mm[0] + mm[1] + mm[2] + dkb * (bcol * gexp) + dkd_ref[rs, hs] * fdec
            dq_parts[c, h] = mm[3] + dqd_ref[rs, hs] * gexp
            dv_ref[rs, hs] = dvb * bcol
            s_kb = jnp.sum(dkb * kf, axis=1, keepdims=True)
            db = (jnp.sum(da * kkd, axis=1, keepdims=True) + jnp.sum(dvb * v_ref[rs, hs], axis=1, keepdims=True)
                  + s_kb * gexp)
            rho = jnp.sum(dkd_ref[rs, hs] * kf, axis=1, keepdims=True) * fdec
            dgc = (dgc + s_kb * bcol * gexp + jnp.sum(dqd_ref[rs, hs] * qf, axis=1, keepdims=True) * gexp - rho)
            last = jnp.sum(rho, axis=0, keepdims=True) + dgl_ref[c * 8:c * 8 + 1, h:h + 1] * jnp.exp(glast)
            dgc = dgc + jnp.where(rowi == CHUNK - 1, last, 0.0)
            dgc_tiles[c] = jnp.where(lane == h, dgc, dgc_tiles[c])
            db_tiles[c] = jnp.where(lane == h, db, db_tiles[c])
        alog = alog_ref[...]
        row8 = lax.broadcasted_iota(jnp.int32, (8, LANES), 0)
        par = jnp.zeros((8, LANES), F32)
        for c, rs in zip(subs, rss):
            for j in range(C_QK_HEADS):
                dqk_ref[rs, j * C_DK:(j + 1) * C_DK] = (dq_parts[c, 2 * j] + dq_parts[c, 2 * j + 1]) * scale
                dqk_ref[rs, C_QKW + j * C_DK:C_QKW + (j + 1) * C_DK] = dk_parts[c, 2 * j] + dk_parts[c, 2 * j + 1]
            dg = _dot_mask(jnp.logical_not(strict), dgc_tiles[c])
            g, _, gate_pre = _gate_tiles(cab_ref[rs, :], alog, dtb_ref[...])
            dca = dg * (-jnp.exp(alog)) * jax.nn.sigmoid(gate_pre)
            beta = betas[c]
            dcab_ref[rs, :LANES] = (dca + pltpu.roll(db_tiles[c] * beta * (1.0 - beta), C_V_HEADS, 1)).astype(BF16)
            dcab_ref[rs, LANES:] = jnp.zeros((CHUNK, D_IN_PAD - COL["c_ab"] - LANES), BF16)
            par = par + jnp.where(row8 == 0, jnp.sum(dg * g, axis=0, keepdims=True),
                                  jnp.where(row8 == 1, jnp.sum(dca, axis=0, keepdims=True), 0.0))

        @pl.when(pl.program_id(0) == 0)
        def _():
            dpar_ref[...] = par

        @pl.when(pl.program_id(0) > 0)
        def _():
            dpar_ref[...] += par

    def blk(wd):
        return pl.BlockSpec((rows, wd), lambda n: (n, 0))

    row = pl.BlockSpec((1, LANES), lambda n: (0, 0))
    sqs = blk(C_V_HEADS * CHUNK)
    tail = D_IN_PAD - COL["c_ab"]
    assert COL["c_ab"] % tail == 0 and nc % PREP_SUB == 0
    return pl.pallas_call(
        body, name="delta_prep_bwd", grid=(nc // PREP_SUB,),
        in_specs=[blk(C_QKW), pl.BlockSpec((rows, C_QKW), lambda n: (n, 1)), blk(C_VW),
                  pl.BlockSpec((rows, LANES), lambda n: (n, COL["c_ab"] // LANES)), row, row, sqs,
                  blk(C_VW), blk(C_VW),
                  blk(LANES), blk(LANES), blk(C_VW), blk(C_VW), blk(C_VW), blk(C_VW), sqs,
                  pl.BlockSpec((PREP_SUB * 8, LANES), lambda n: (n, 0))] + extra_specs,
        out_specs=[pl.BlockSpec((rows, tail), lambda n: (n, COL["c_ab"] // tail)),
                   blk(2 * C_QKW), blk(C_VW), pl.BlockSpec((8, LANES), lambda n: (0, 0))],
        out_shape=[_du_shape(t), jax.ShapeDtypeStruct((t, 2 * C_QKW), F32),
                   jax.ShapeDtypeStruct((t, C_VW), F32), jax.ShapeDtypeStruct((8, LANES), F32)],
        input_output_aliases=aliases,
        compiler_params=_cparams(("arbitrary",)),
    )(qk, qk, v, proj, alog, dtb, tmat, u, w, gc, beta, du, dw, dqd, dkd, dattn, dgl, *extra)


def _z_spec(tq):
    return pl.BlockSpec((tq, C_VW), lambda i: (i, COL["c_z"] // C_VW))


def _gated_norm_fwd_call(o, u, gain, *, tq=512):
    t, w = o.shape

    def body(o_ref, z_ref, g_ref, y_ref):
        act, _ = _silu_parts(z_ref[...])
        gv = g_ref[...]
        for h in range(C_V_HEADS):
            hs = slice(h * C_DK, (h + 1) * C_DK)
            ov = o_ref[:, hs]
            r = lax.rsqrt(jnp.mean(ov * ov, axis=1, keepdims=True) + EPS)
            y_ref[:, hs] = ov * r * gv * act[:, hs]

    blk = pl.BlockSpec((tq, w), lambda i: (i, 0))
    return pl.pallas_call(
        body, name="gated_norm_fwd", grid=(t // tq,),
        in_specs=[blk, _z_spec(tq), pl.BlockSpec((1, C_DK), lambda i: (0, 0))], out_specs=blk,
        out_shape=jax.ShapeDtypeStruct((t, w), F32),
        compiler_params=_cparams(("parallel",)),
    )(o, u, gain)


def _gated_norm_bwd_call(o, u, gain, dy, du_buf, *, tq=512):
    t, w = o.shape
    nt = t // tq
    extra, extra_specs, aliases = _du_operands(du_buf, 4)

    def body(o_ref, z_ref, g_ref, dy_ref, *refs):
        dz_ref, do_ref, dg_ref, acc_ref = refs[len(extra):]
        i = pl.program_id(0)
        act, dact = _silu_parts(z_ref[...])
        gv = g_ref[...]
        part = jnp.zeros((8, C_DK), F32)
        for h in range(C_V_HEADS):
            hs = slice(h * C_DK, (h + 1) * C_DK)
            ov = o_ref[:, hs]
            r = lax.rsqrt(jnp.mean(ov * ov, axis=1, keepdims=True) + EPS)
            xh = ov * r
            dyv = dy_ref[:, hs]
            dn = dyv * act[:, hs]
            dz_ref[:, hs] = (dyv * xh * gv * dact[:, hs]).astype(BF16)
            dxh = dn * gv
            do_ref[:, hs] = r * (dxh - xh * jnp.mean(dxh * xh, axis=1, keepdims=True))
            part = part + jnp.sum((dn * xh).reshape(tq // 8, 8, C_DK), axis=0)

        @pl.when(i == 0)
        def _():
            acc_ref[...] = part

        @pl.when(i > 0)
        def _():
            acc_ref[...] += part

        @pl.when(i == nt - 1)
        def _():
            dg_ref[...] = jnp.sum(acc_ref[...], axis=0, keepdims=True)

    blk = pl.BlockSpec((tq, w), lambda i: (i, 0))
    grow = pl.BlockSpec((1, C_DK), lambda i: (0, 0))
    return pl.pallas_call(
        body, name="gated_norm_bwd", grid=(nt,),
        in_specs=[blk, _z_spec(tq), grow, blk] + extra_specs, out_specs=[_z_spec(tq), blk, grow],
        out_shape=[_du_shape(t), jax.ShapeDtypeStruct((t, w), F32), jax.ShapeDtypeStruct((1, C_DK), F32)],
        scratch_shapes=[pltpu.VMEM((8, C_DK), F32)],
        input_output_aliases=aliases,
        compiler_params=_cparams(("arbitrary",)),
    )(o, u, gain, dy, *extra)


def _gate_specs(tq):
    return [pl.BlockSpec((tq, D_MODEL), lambda i, j=j: (i, j)) for j in range(3)]


def _merge_fwd_call(ps, u, *, tq=512):
    t, w = ps[0].shape

    def body(p0, p1, p2, g0, g1, g2, y_ref):
        y_ref[...] = (jax.nn.sigmoid(g0[...]) * p0[...] + jax.nn.sigmoid(g1[...]) * p1[...]
                      + jax.nn.sigmoid(g2[...]) * p2[...]).astype(BF16)

    blk = pl.BlockSpec((tq, w), lambda i: (i, 0))
    return pl.pallas_call(
        body, name="merge_fwd", grid=(t // tq,), in_specs=[blk] * 3 + _gate_specs(tq), out_specs=blk,
        out_shape=jax.ShapeDtypeStruct((t, w), BF16),
        compiler_params=_cparams(("parallel",)),
    )(*ps, u, u, u)


def _merge_bwd_call(ps, u, dy, *, tq=256):
    t, w = dy.shape

    def body(p0, p1, p2, g0, g1, g2, dy_ref, dg_ref, dp0, dp1, dp2):
        dyv = dy_ref[...]
        for j, (p, g, dp) in enumerate(((p0, g0, dp0), (p1, g1, dp1), (p2, g2, dp2))):
            sig = jax.nn.sigmoid(g[...])
            dp[...] = (dyv * sig).astype(BF16)
            dg_ref[:, j * w:(j + 1) * w] = (dyv * p[...] * sig * (1.0 - sig)).astype(BF16)

    blk = pl.BlockSpec((tq, w), lambda i: (i, 0))
    small = jax.ShapeDtypeStruct((t, w), BF16)
    return pl.pallas_call(
        body, name="merge_bwd", grid=(t // tq,), in_specs=[blk] * 3 + _gate_specs(tq) + [blk],
        out_specs=[pl.BlockSpec((tq, 3 * w), lambda i: (i, 0))] + [blk] * 3,
        out_shape=[_du_shape(t)] + [small] * 3,
        compiler_params=_cparams(("parallel",)),
    )(*ps, u, u, u, dy)


Q_SCALE = HEAD_DIM ** -0.5
A_PARTS = ((COL["a_q"], A_W, True, Q_SCALE), (COL["a_k"], A_W, True, 1.0), (COL["a_v"], A_W, False, 1.0))
B_PARTS = ((COL["b_q"], A_W, True, Q_SCALE), (COL["b_k"], B_KVW, True, 1.0), (COL["b_v"], B_KVW, False, 1.0))
BRANCHES = ("w_branch_a", "w_branch_b", "w_branch_c")


def _layer_fwd(x, tabs, p, wb, gather_src=None):
    h = _rms_fwd_call(x, p["norm_mix"], name="rms_mix_fwd", out_dtype=BF16)
    u = _mm(h, wb["w_in"], bias=p["b_in"], tn=IN_TN, name="in_proj_fwd")
    qkv_a = _rope_gather_call(u, tabs, A_PARTS, name="rope_a_fwd")
    a_runs = [_attn_fwd(cfg, qkv_a, None) for cfg in ATTN_A_CFGS]
    os_, lses = tuple(r[0] for r in a_runs), tuple(r[1] for r in a_runs)
    ya = _combine_fwd_call(os_, lses)
    qkv_b = _rope_gather_call(u, tabs, B_PARTS, name="rope_b_fwd")
    yb, _, b_saved = _attn_fwd(ATTN_B_CFG, qkv_b, p["sinks"])
    zc, qk, v = _conv_prep_fwd_call(u, p["conv_w"])
    uu, ww, qd, kd, attn, tmat, gc, beta, *gathered = _delta_prep_call(qk, v, u, p["a_log"], p["dt_bias"], gather_src)
    o, vn, st = _delta_scan_call(uu, ww, qd, kd, attn, gc)
    yc = _gated_norm_fwd_call(o, u, p["c_norm"])
    ys = (ya, yb, yc)
    ps = tuple(_mm(y, wb[n], name="branch_fwd") for y, n in zip(ys, BRANCHES))
    merged = _merge_fwd_call(ps, u)
    x1 = _mm(merged, wb["w_out"], add=x, name="out_proj_fwd")
    h2 = _rms_fwd_call(x1, p["norm_ffn"], name="rms_ffn_fwd", out_dtype=BF16)
    pre, act = _mm(h2, wb["w_ff1"], relu2_out=True, name="ffn_up")
    x2 = _mm(act, wb["w_ff2"], add=x1, name="ffn_down")
    saved = dict(x=x, h=h, u=u, a_saved=[r[2] for r in a_runs], os_=os_, lses=lses, b_saved=b_saved,
                 zc=zc, qk=qk, v=v, delta=(tmat, uu, ww, gc, beta, qd, kd, attn, vn, st), o=o, ys=ys, ps=ps,
                 merged=merged, x1=x1, h2=h2, pre=pre, act=act)
    return x2, saved, (gathered if gathered else None)


def _layer_bwd(s, dx2, tabs, p, wb):
    g = {}
    t = dx2.shape[0]
    dpre = _mm(dx2, wb["w_ff2"], tb=True, mul_drelu2=s["pre"], out_dtype=BF16, name="ffn_dpre")
    g["w_ff2"] = _mm(s["act"], dx2, ta=True, tk=1024, name="ffn_dw2")
    g["w_ff1"] = _mm(s["h2"], dpre, ta=True, tk=1024, name="ffn_dw1")
    dh2 = _mm(dpre, wb["w_ff1"], tb=True, name="ffn_dh")
    dx1, g["norm_ffn"] = _rms_bwd_call(s["x1"], p["norm_ffn"], dh2, add=dx2, name="rms_ffn_bwd")
    dmerged = _mm(dx1, wb["w_out"], tb=True, name="out_proj_da")
    g["w_out"] = _mm(s["merged"], dx1, ta=True, tk=1024, name="out_proj_dw")
    du, *dps = _merge_bwd_call(s["ps"], s["u"], dmerged)
    dys = []
    for y, dp, n in zip(s["ys"], dps, BRANCHES):
        dys.append(_mm(dp, wb[n], tb=True, name="branch_da"))
        g[n] = _mm(y, dp, ta=True, tk=1024, name="branch_dw")
    dya, dyb, dyc = dys
    tmat, uu, ww, gc, beta, qd, kd, attn, vn, st = s["delta"]
    du, do, g["c_norm"] = _gated_norm_bwd_call(s["o"], s["u"], p["c_norm"], dyc, du)
    ddu, ddw, dqd, dkd, dattn, dgl = _delta_scan_bwd_call(do, ww, qd, kd, attn, gc, vn, st)
    du, dqk, dv, dpar = _delta_prep_bwd_call(s["qk"], s["v"], s["u"], p["a_log"], p["dt_bias"], tmat, uu, ww, gc,
                                             beta, ddu, ddw, dqd, dkd, dattn, dgl, du)
    g["a_log"], g["dt_bias"] = dpar[0:1], dpar[1:2]
    dzc = _conv_prep_dz_call(s["zc"], dqk, dv)
    du, dconv = _conv_bwd_call(s["u"], dzc, p["conv_w"], du)
    g["conv_w"] = dconv[:C_CONV]
    no_dlse = jnp.zeros((t, LANES), F32)
    dq, dk, dv_b, dsink = _attn_bwd(ATTN_B_CFG, s["b_saved"], p["sinks"], dyb, no_dlse)
    g["sinks"] = dsink[0, :p["sinks"].shape[0]]
    du = _rope_scatter_call(du, t, [([dq], A_W, True, Q_SCALE)], COL["b_q"], tabs, name="rope_bq_bwd")
    du = _rope_scatter_call(du, t, [([dk], B_KVW, True, 1.0), ([dv_b], B_KVW, False, 1.0)], COL["b_k"], tabs,
                            name="rope_bkv_bwd")
    *dos, dl0, dl1, dl2 = _combine_bwd_call(s["os_"], s["lses"], dya)
    grads_a = [_attn_bwd(cfg, sv, None, do_c, dl)[:3]
               for cfg, sv, do_c, dl in zip(ATTN_A_CFGS, s["a_saved"], dos, (dl0, dl1, dl2))]
    dqs, dks, dvs = zip(*grads_a)
    du = _rope_scatter_call(du, t, [(list(dqs), A_W, True, Q_SCALE), (list(dks), A_W, True, 1.0),
                                    (list(dvs), A_W, False, 1.0)],
                            COL["a_q"], tabs, name="rope_a_bwd")
    dh = _mm(du, wb["w_in"], tb=True, tk=IN_TN, name="in_proj_da")
    g["w_in"], g["b_in"] = _mm(s["h"], du, ta=True, b_colsum=True, tn=IN_TN, tk=1024, name="in_proj_dw")
    dx, g["norm_mix"] = _rms_bwd_call(s["x"], p["norm_mix"], dh, add=dx1, name="rms_mix_bwd")
    return dx, g


def _local_step(x, params, first_gathered, payload_of_layer, weights_of_layer, tabs, tgt):
    saves, gathered = [], first_gathered
    for layer in range(DEPTH):
        wb, conv_w = weights_of_layer(layer, gathered)
        p = {n: w[layer] for n, w in params.items() if n != "norm_final"}
        p["conv_w"] = conv_w
        nxt = payload_of_layer(layer + 1) if layer + 1 < DEPTH else None
        x, s, gathered = _layer_fwd(x, tabs, p, wb, nxt)
        saves.append((s, p, wb))
    loss, dx, dfinal = _loss_call(x, params["norm_final"], tgt)
    per_layer = []
    for s, p, wb in reversed(saves):
        dx, g = _layer_bwd(s, dx, tabs, p, wb)
        per_layer.append(g)
    per_layer.reverse()
    grads = {n: jnp.stack([g[n] for g in per_layer]) for n in per_layer[0]}
    grads["norm_final"] = dfinal
    return loss, dx, grads


def _in_cols_to_kernel(w):
    lead = w.shape[:-1]
    parts, pos = [], 0
    for _, start, width, ref_start in IN_LAYOUT:
        if start > pos:
            parts.append(jnp.zeros(lead + (start - pos,), w.dtype))
        parts.append(w[..., ref_start:ref_start + width])
        pos = start + width
    parts.append(jnp.zeros(lead + (D_IN_PAD - pos,), w.dtype))
    return jnp.concatenate(parts, axis=-1)


def _in_cols_to_reference(w):
    by_ref = sorted(IN_LAYOUT, key=lambda e: e[3])
    return jnp.concatenate([w[..., start:start + width] for _, start, width, _ in by_ref], axis=-1)


W_IN_SHARD = 8464 // N_DEV


def _w_in_from_shards(blocks):
    lead = blocks.shape[1:-1]
    parts, pos = [], 0
    for _, start, width, ref_start in IN_LAYOUT:
        if start > pos:
            parts.append(jnp.zeros(lead + (start - pos,), blocks.dtype))
        col = ref_start
        while col < ref_start + width:
            d, l = divmod(col, W_IN_SHARD)
            n = min(W_IN_SHARD - l, ref_start + width - col)
            parts.append(blocks[d, ..., l:l + n])
            col += n
        pos = start + width
    parts.append(jnp.zeros(lead + (D_IN_PAD - pos,), blocks.dtype))
    return jnp.concatenate(parts, axis=-1)


def _w_in_to_shards(g):
    by_ref = sorted(IN_LAYOUT, key=lambda e: e[3])
    blocks = []
    for d in range(N_DEV):
        lo, hi = d * W_IN_SHARD, (d + 1) * W_IN_SHARD
        parts = []
        for _, start, width, ref_start in by_ref:
            a, b = max(lo, ref_start), min(hi, ref_start + width)
            if a < b:
                parts.append(g[..., start + a - ref_start:start + b - ref_start])
        blocks.append(jnp.concatenate(parts, axis=-1))
    return jnp.stack(blocks)


def _pad_lanes(v):
    return jnp.pad(v, ((0, 0), (0, LANES - v.shape[1])))[:, None, :]


BIG = (("w_in", 2), ("conv_w", 2), ("w_branch_a", 2), ("w_branch_b", 2), ("w_branch_c", 1), ("w_out", 1),
       ("w_ff1", 2), ("w_ff2", 1))
SMALL = ("norm_mix", "b_in", "a_log", "dt_bias", "sinks", "c_norm", "norm_ffn", "norm_final")
WEIGHTS = ("norm_mix", "w_in", "b_in", "conv_w", "a_log", "dt_bias", "sinks", "c_norm", "w_branch_a",
           "w_branch_b", "w_branch_c", "w_out", "norm_ffn", "w_ff1", "w_ff2", "norm_final")
MATMUL_WEIGHTS = ("w_in", "w_branch_a", "w_branch_b", "w_branch_c", "w_out", "w_ff1", "w_ff2")
PACK_ROWS = 1024
ROW_ALIGN = 16


def _seg_rows(n):
    return -(-n // (LANES * ROW_ALIGN)) * ROW_ALIGN


def _pack(arrays, lead=0):
    parts = []
    for a in arrays:
        lead_shape = a.shape[:lead]
        n = math.prod(a.shape[lead:])
        rows = _seg_rows(n)
        if rows * LANES != n:
            a = jnp.pad(a.reshape(lead_shape + (n,)), [(0, 0)] * lead + [(0, rows * LANES - n)])
        parts.append(a.reshape(lead_shape + (rows, LANES)))
    total = sum(p.shape[lead] for p in parts)
    padded = -(-total // PACK_ROWS) * PACK_ROWS
    if padded > total:
        parts.append(jnp.zeros(parts[0].shape[:lead] + (padded - total, LANES), parts[0].dtype))
    return jnp.concatenate(parts, axis=lead)


def _unpack(buf, shapes):
    lead = buf.shape[:-2]
    out, pos = [], 0
    for shp in shapes:
        n = math.prod(shp)
        rows = _seg_rows(n)
        seg = buf[..., pos:pos + rows, :]
        if rows * LANES != n:
            seg = seg.reshape(lead + (rows * LANES,))[..., :n]
        out.append(seg.reshape(lead + tuple(shp)))
        pos += rows
    return out


def _shards_to_full(blocks, axis):
    moved = jnp.moveaxis(blocks, 0, axis)
    shp = list(blocks.shape[1:])
    shp[axis] = shp[axis] * N_DEV
    return moved.reshape(shp)


def _full_to_shards(full, axis):
    shp = list(full.shape)
    shp[axis:axis + 1] = [N_DEV, shp[axis] // N_DEV]
    return jnp.moveaxis(full.reshape(shp), axis, 0)


def _my_place():
    return lax.axis_index("x"), lax.axis_index("y"), lax.axis_index("c")


def _slot(x, y, c):
    return 4 * x + 2 * y + c


GATHER_COPIES = 7


def _gather_plan(x_ref, out_ref, send_sems, recv_sems, local_sem, base):
    x, y, c = _my_place()
    me, sibling = (x, y, c), (x, y, 1 - c)
    chips = [(1 - x, y), (x, 1 - y), (1 - x, 1 - y)]

    def copy(k, blk, to, src=None):
        dst = out_ref.at[_slot(*blk)]
        return pltpu.make_async_remote_copy(
            src_ref=dst if src is None else src, dst_ref=dst,
            send_sem=send_sems.at[base + k], recv_sem=recv_sems.at[base + k], device_id=to, device_id_type=MESH_ID)

    def own():
        mine = pltpu.make_async_copy(x_ref, out_ref.at[_slot(*me)], local_sem)
        return mine, [copy(0, me, sibling, src=x_ref)] + [copy(1 + j, me, (*chip, c), src=x_ref)
                                                          for j, chip in enumerate(chips)]

    return copy, own, me, sibling, chips, c


def _gather_plans(srcs, outs, send_sems, recv_sems, local_sems):
    return [_gather_plan(x_ref, out_ref, send_sems, recv_sems, local_sems.at[i], GATHER_COPIES * i)
            for i, (x_ref, out_ref) in enumerate(zip(srcs, outs))]


def _gather_start(srcs, outs, *sems):
    for _, own, *_ in _gather_plans(srcs, outs, *sems):
        mine, first = own()
        mine.start()
        for cp in first:
            cp.start()


def _gather_finish(srcs, outs, *sems):
    plans = _gather_plans(srcs, outs, *sems)
    passed_all = []
    for copy, own, me, sibling, chips, c in plans:
        passed = [copy(4 + j, (*chip, c), sibling) for j, chip in enumerate(chips)]
        for j, chip in enumerate(chips):
            copy(1 + j, (*chip, c), me).wait_recv()
            passed[j].start()
        passed_all.append(passed)
    for (copy, own, me, sibling, chips, c), passed in zip(plans, passed_all):
        copy(0, sibling, me).wait_recv()
        for j, chip in enumerate(chips):
            copy(4 + j, (*chip, 1 - c), me).wait_recv()
        mine, first = own()
        for cp in first + passed:
            cp.wait_send()
        mine.wait()


def _gather_sems(n):
    return [pltpu.SemaphoreType.DMA((GATHER_COPIES * n,)), pltpu.SemaphoreType.DMA((GATHER_COPIES * n,)),
            pltpu.SemaphoreType.DMA((n,))]


def _gathered_shapes(blocks):
    return [jax.ShapeDtypeStruct((N_DEV,) + b.shape, b.dtype) for b in blocks]


def _all_gather(blocks, *, name):
    n = len(blocks)

    def body(*refs):
        srcs, outs, sems = refs[:n], refs[n:2 * n], refs[2 * n:]
        _gather_start(srcs, outs, *sems)
        _gather_finish(srcs, outs, *sems)

    return pl.pallas_call(
        body, name=name, out_shape=_gathered_shapes(blocks),
        in_specs=[HBM_SPEC] * n, out_specs=[HBM_SPEC] * n,
        scratch_shapes=_gather_sems(n),
    )(*blocks)


N_CHIP = N_DEV // 2


def _swap_with_sibling(blocks, *, name):
    n = len(blocks)

    def body(*refs):
        srcs, outs, send_sems, recv_sems = refs[:n], refs[n:2 * n], refs[2 * n], refs[2 * n + 1]
        x, y, c = _my_place()
        copies = [pltpu.make_async_remote_copy(src_ref=g_ref, dst_ref=out_ref, send_sem=send_sems.at[i],
                                               recv_sem=recv_sems.at[i], device_id=(x, y, 1 - c),
                                               device_id_type=MESH_ID)
                  for i, (g_ref, out_ref) in enumerate(zip(srcs, outs))]
        for cp in copies:
            cp.start()
        for cp in copies:
            cp.wait_recv()
        for cp in copies:
            cp.wait_send()

    return pl.pallas_call(
        body, name=name,
        out_shape=[jax.ShapeDtypeStruct(b.shape, b.dtype) for b in blocks],
        in_specs=[HBM_SPEC] * n, out_specs=[HBM_SPEC] * n,
        scratch_shapes=[pltpu.SemaphoreType.DMA((n,)), pltpu.SemaphoreType.DMA((n,))],
    )(*blocks)


def _chip_all_to_all(blocks, *, name):
    n = len(blocks)
    peers = N_CHIP - 1

    def body(*refs):
        srcs, outs = refs[:n], refs[n:2 * n]
        send_sems, recv_sems, local_sems = refs[2 * n:]
        x, y, c = _my_place()
        mine_slot = 2 * x + y
        locals_, copies = [], []
        for i, (g_ref, out_ref) in enumerate(zip(srcs, outs)):
            locals_.append(pltpu.make_async_copy(g_ref.at[mine_slot], out_ref.at[mine_slot], local_sems.at[i]))
            for k in range(1, N_CHIP):
                px, py = x ^ (k >> 1), y ^ (k & 1)
                copies.append(pltpu.make_async_remote_copy(
                    src_ref=g_ref.at[2 * px + py], dst_ref=out_ref.at[mine_slot],
                    send_sem=send_sems.at[peers * i + k - 1], recv_sem=recv_sems.at[peers * i + k - 1],
                    device_id=(px, py, c), device_id_type=MESH_ID))
        for cp in locals_ + copies:
            cp.start()
        for cp in copies:
            cp.wait_recv()
        for cp in copies:
            cp.wait_send()
        for cp in locals_:
            cp.wait()

    return pl.pallas_call(
        body, name=name,
        out_shape=[jax.ShapeDtypeStruct(b.shape, b.dtype) for b in blocks],
        in_specs=[HBM_SPEC] * n, out_specs=[HBM_SPEC] * n,
        scratch_shapes=[pltpu.SemaphoreType.DMA((peers * n,)), pltpu.SemaphoreType.DMA((peers * n,)),
                        pltpu.SemaphoreType.DMA((n,))],
    )(*blocks)


def _block_rows(rows, cols):
    tr = max(8, min(rows, PACK_ROWS * LANES // (-(-cols // LANES) * LANES) // 8 * 8))
    while rows % tr:
        tr -= 8
    return tr


def _add_bf16_call(a, b, *, name):
    n, rows, cols = a.shape
    tr = _block_rows(rows, cols)

    def body(a_ref, b_ref, o_ref):
        o_ref[...] = (a_ref[...].astype(F32) + b_ref[...].astype(F32)).astype(BF16)

    blk = pl.BlockSpec((n, tr, cols), lambda i: (0, i, 0))
    return pl.pallas_call(
        body, name=name, grid=(rows // tr,), in_specs=[blk, blk], out_specs=blk,
        out_shape=jax.ShapeDtypeStruct(a.shape, BF16),
        compiler_params=_cparams(("parallel",)),
    )(a, b)


def _adamw_call(parts, w, m, v, *, name):
    rows, cols = w.shape
    tr = _block_rows(rows, cols)
    n_parts = parts.shape[0]

    def body(p_ref, w_ref, m_ref, v_ref, g_ref, d_ref, nm_ref, nv_ref):
        g = p_ref[0].astype(F32)
        for s in range(1, n_parts):
            g = g + p_ref[s].astype(F32)
        nm = ADAM_B1 * m_ref[...] + (1.0 - ADAM_B1) * g
        nv = ADAM_B2 * v_ref[...] + (1.0 - ADAM_B2) * jnp.square(g)
        m_hat = nm / (1.0 - ADAM_B1 ** ADAM_STEP)
        v_hat = nv / (1.0 - ADAM_B2 ** ADAM_STEP)
        g_ref[...] = g
        nm_ref[...] = nm
        nv_ref[...] = nv
        d_ref[...] = -ADAM_LR * (m_hat / (jnp.sqrt(v_hat) + ADAM_EPS) + ADAM_WD * w_ref[...])

    blk = pl.BlockSpec((tr, cols), lambda i: (i, 0))
    shape = jax.ShapeDtypeStruct((rows, cols), F32)
    return pl.pallas_call(
        body, name=name, grid=(rows // tr,),
        in_specs=[pl.BlockSpec((n_parts, tr, cols), lambda i: (0, i, 0)), blk, blk, blk],
        out_specs=[blk] * 4, out_shape=[shape] * 4,
        compiler_params=_cparams(("parallel",)),
    )(parts, w, m, v)


def _kernel_params(full):
    return {
        "norm_mix": full["norm_mix"][:, None, :],
        "b_in": _in_cols_to_kernel(full["b_in"])[:, None, :],
        "a_log": _pad_lanes(full["a_log"]),
        "dt_bias": _pad_lanes(full["dt_bias"]),
        "sinks": full["sinks"],
        "c_norm": full["c_norm"][:, None, :],
        "norm_ffn": full["norm_ffn"][:, None, :],
        "norm_final": full["norm_final"][None, :],
    }


def _reference_grads(g):
    return {
        "norm_mix": g["norm_mix"][:, 0, :],
        "b_in": _in_cols_to_reference(g["b_in"][:, 0, :]),
        "conv_w": g["conv_w"],
        "a_log": g["a_log"][:, 0, :C_V_HEADS],
        "dt_bias": g["dt_bias"][:, 0, :C_V_HEADS],
        "sinks": g["sinks"],
        "c_norm": g["c_norm"][:, 0, :],
        "w_branch_a": g["w_branch_a"], "w_branch_b": g["w_branch_b"], "w_branch_c": g["w_branch_c"],
        "w_out": g["w_out"],
        "norm_ffn": g["norm_ffn"][:, 0, :],
        "w_ff1": g["w_ff1"], "w_ff2": g["w_ff2"],
        "norm_final": g["norm_final"][0],
    }


def kernel(x, positions, norm_mix, w_in, b_in, conv_w, a_log, dt_bias, sinks, c_norm, w_branch_a, w_branch_b, w_branch_c, w_out, norm_ffn, w_ff1, w_ff2, norm_final, loss_target, m_norm_mix, m_w_in, m_b_in, m_conv_w, m_a_log, m_dt_bias, m_sinks, m_c_norm, m_w_branch_a, m_w_branch_b, m_w_branch_c, m_w_out, m_norm_ffn, m_w_ff1, m_w_ff2, m_norm_final, v_norm_mix, v_w_in, v_b_in, v_conv_w, v_a_log, v_dt_bias, v_sinks, v_c_norm, v_w_branch_a, v_w_branch_b, v_w_branch_c, v_w_out, v_norm_ffn, v_w_ff1, v_w_ff2, v_norm_final):
    env = dict(locals())
    weights = {n: env[n] for n in WEIGHTS}
    moments_m = {n: env["m_" + n] for n in WEIGHTS}
    moments_v = {n: env["v_" + n] for n in WEIGHTS}

    axis_of = {n: axis - 1 for n, axis in BIG}

    packed_names = [n for n in MATMUL_WEIGHTS if n != "w_in"]

    def payload_of_layer(layer):
        cw = weights["conv_w"][layer]
        c1 = cw.astype(BF16)
        c2 = (cw - c1.astype(F32)).astype(BF16)
        c3 = (cw - c1.astype(F32) - c2.astype(F32)).astype(BF16)
        return [weights["w_in"][layer].astype(BF16),
                _pack([weights[n][layer].astype(BF16) for n in packed_names] + [c1, c2, c3])]

    def weights_of_layer(layer, gathered):
        w_in_blocks, packed = gathered
        shapes = [weights[n].shape[1:] for n in packed_names] + [weights["conv_w"].shape[1:]] * 3
        blocks = _unpack(packed, shapes)
        wb = {n: _shards_to_full(blk, axis_of[n]) for n, blk in zip(packed_names, blocks)}
        wb["w_in"] = _w_in_from_shards(w_in_blocks)
        return wb, _shards_to_full(sum(b.astype(F32) for b in blocks[-3:]), axis_of["conv_w"])

    tabs = rope_tables(positions[0])
    first = _all_gather(payload_of_layer(0), name="gather_weights")
    loss, dx, dparams = _local_step(x[0], _kernel_params({n: weights[n] for n in SMALL}), first,
                                    payload_of_layer, weights_of_layer, tabs, loss_target[0])
    grads = _reference_grads(dparams)
    loss = lax.psum(loss, ("x", "y", "c"))

    core = lax.axis_index("c")
    rest = [(n, axis) for n, axis in BIG if n != "w_in"]
    w_in_rows = DEPTH * D_MODEL

    def by_core(shards, which):
        sh = shards.reshape((N_CHIP, 2) + shards.shape[1:])
        return lax.dynamic_index_in_dim(sh, which, axis=1, keepdims=False).astype(BF16)

    def halves(which):
        w_in_half = by_core(_w_in_to_shards(dparams["w_in"]), which).reshape(N_CHIP, w_in_rows, W_IN_SHARD)
        return [w_in_half, _pack([by_core(_full_to_shards(grads[n], axis), which) for n, axis in rest], lead=1)]

    from_sibling = _swap_with_sibling(halves(1 - core), name="scatter_grads_d2d")
    chip_sums = [_add_bf16_call(keep, got, name="scatter_grads_add")
                 for keep, got in zip(halves(core), from_sibling)]
    w_in_parts, rest_parts = _chip_all_to_all(chip_sums, name="scatter_grads_ici")
    small_parts, = _all_gather([_pack([grads[n] for n in SMALL])], name="gather_small_grads")

    out = {}
    results = _adamw_call(w_in_parts, *[d["w_in"].reshape(w_in_rows, W_IN_SHARD)
                                        for d in (weights, moments_m, moments_v)], name="adamw_w_in")
    for kind, buf in zip(("grad", "delta", "new_m", "new_v"), results):
        out[kind, "w_in"] = buf.reshape(weights["w_in"].shape)
    for names, parts in (([n for n, _ in rest], rest_parts), (list(SMALL), small_parts)):
        shapes = [weights[n].shape for n in names]
        packed = [_pack([d[n] for n in names]) for d in (weights, moments_m, moments_v)]
        results = _adamw_call(parts, *packed, name="adamw_" + names[0])
        for kind, buf in zip(("grad", "delta", "new_m", "new_v"), results):
            for n, arr in zip(names, _unpack(buf, shapes)):
                out[kind, n] = arr
    return (loss, dx[None], *[out[kind, n] for kind in ("grad", "delta", "new_m", "new_v") for n in WEIGHTS])
```

```python
import functools
import math

import jax
import jax.numpy as jnp
from jax import lax
from jax.experimental import pallas as pl
from jax.experimental.pallas import tpu as pltpu

F32 = jnp.float32
BF16 = jnp.bfloat16

N_DEV = 8
D_MODEL = 1024
DEPTH = 2
HEAD_DIM = 64
ROT_DIM = 16
ROPE_THETA = 500000.0
BLK = 128
NEG_INF = -1e30
EPS = 1e-6
A_CONFIGS = ((128, 1), (512, 4), (2048, 16))
B_GROUP = 4
C_QK_HEADS = 4
C_V_HEADS = 8
C_DK = 128
C_CONV = 4
CHUNK = 64
ADAM_LR = 0.001
ADAM_B1 = 0.9
ADAM_B2 = 0.999
ADAM_EPS = 1e-08
ADAM_WD = 0.01
ADAM_STEP = 10

IN_LAYOUT = (
    ("gate_a", 0, 1024, 5392), ("gate_b", 1024, 1024, 6416), ("gate_c", 2048, 1024, 7440),
    ("a_q", 3072, 512, 0), ("a_k", 3584, 512, 512), ("a_v", 4096, 512, 1024), ("b_q", 4608, 512, 1536),
    ("c_z", 5120, 1024, 4352), ("c_qkv", 6144, 2048, 2304),
    ("b_k", 8192, 128, 2048), ("b_v", 8320, 128, 2176), ("c_ab", 8448, 16, 5376),
)
COL = {name: start for name, start, _, _ in IN_LAYOUT}
D_IN_PAD = 8704
IN_TN = D_IN_PAD // 4
LANES = 128
VMEM_LIMIT = 56 * 1024 * 1024


def _cparams(sem=None):
    return pltpu.CompilerParams(dimension_semantics=sem, vmem_limit_bytes=VMEM_LIMIT)


def _relu2(t):
    return jnp.square(jnp.maximum(t, 0.0))


def _mm(a, b, *, ta=False, tb=False, bias=None, a_fn=None, mul_drelu2=None, add=None,
        out_dtype=F32, relu2_out=False, b_colsum=False, gather_src=None, tm=1024, tn=1024, tk=2048, name):
    if ta:
        kdim, m = a.shape
    else:
        m, kdim = a.shape
    n = b.shape[0] if tb else b.shape[1]
    tm, tn, tk = min(tm, m), min(tn, n), min(tk, kdim)
    assert m % tm == 0 and n % tn == 0 and kdim % tk == 0, (a.shape, b.shape, tm, tn, tk)
    nk = kdim // tk
    assert not b_colsum or (m == tm and not tb and nk > 1)
    dims = (((0 if ta else 1,), (1 if tb else 0,)), ((), ()))
    extras = [e for e in (bias, mul_drelu2, add) if e is not None]
    ng = len(gather_src) if gather_src is not None else 0
    grid = (m // tm, n // tn, nk)

    def body(*refs):
        if ng:
            n_in = 2 + len(extras)
            n_out = 1 + int(relu2_out) + int(b_colsum)
            n_scr = int(nk > 1) + int(b_colsum)
            gather_refs = (refs[n_in:n_in + ng], refs[n_in + ng + n_out:n_in + 2 * ng + n_out],
                           *refs[n_in + 2 * ng + n_out + n_scr:])
            refs = refs[:n_in] + refs[n_in + ng:n_in + ng + n_out] + refs[n_in + 2 * ng + n_out:]
            at_first = functools.reduce(jnp.logical_and, [pl.program_id(d) == 0 for d in range(3)])
            at_last = functools.reduce(jnp.logical_and, [pl.program_id(d) == grid[d] - 1 for d in range(3)])

            @pl.when(at_first)
            def _():
                _gather_start(*gather_refs)
        a_ref, b_ref = refs[0], refs[1]
        pos = 2
        bias_ref = pre_ref = add_ref = None
        if bias is not None:
            bias_ref = refs[pos]; pos += 1
        if mul_drelu2 is not None:
            pre_ref = refs[pos]; pos += 1
        if add is not None:
            add_ref = refs[pos]; pos += 1
        o_ref = refs[pos]
        pos += 1
        r_ref = None
        if relu2_out:
            r_ref = refs[pos]; pos += 1
        cs_ref = None
        if b_colsum:
            cs_ref = refs[pos]; pos += 1
        acc_ref = refs[pos] if nk > 1 else None
        cs_acc = refs[pos + 1] if b_colsum else None

        av = a_ref[...]
        if a_fn is not None:
            av = a_fn(av)
        bv = b_ref[...]
        part = lax.dot_general(av.astype(BF16), bv.astype(BF16), dims,
                               preferred_element_type=F32)
        if b_colsum:
            cs_part = jnp.sum(bv.astype(F32).reshape(tk // 8, 8, tn), axis=0)

        def finish(acc):
            if bias_ref is not None:
                acc = acc + bias_ref[...]
            if pre_ref is not None:
                acc = acc * (2.0 * jnp.maximum(pre_ref[...], 0.0))
            if add_ref is not None:
                acc = acc + add_ref[...]
            o_ref[...] = acc.astype(out_dtype)
            if r_ref is not None:
                r_ref[...] = _relu2(acc).astype(BF16)

        if nk == 1:
            finish(part)
        else:
            k = pl.program_id(2)

            @pl.when(k == 0)
            def _():
                acc_ref[...] = part
                if b_colsum:
                    cs_acc[...] = cs_part

            @pl.when(k > 0)
            def _():
                acc_ref[...] += part
                if b_colsum:
                    cs_acc[...] += cs_part

            @pl.when(k == nk - 1)
            def _():
                finish(acc_ref[...])
                if b_colsum:
                    cs_ref[...] = jnp.sum(cs_acc[...], axis=0, keepdims=True)
        if ng:
            @pl.when(at_last)
            def _():
                _gather_finish(*gather_refs)

    a_spec = (pl.BlockSpec((tk, tm), lambda i, j, k: (k, i)) if ta
              else pl.BlockSpec((tm, tk), lambda i, j, k: (i, k)))
    b_spec = (pl.BlockSpec((tn, tk), lambda i, j, k: (j, k)) if tb
              else pl.BlockSpec((tk, tn), lambda i, j, k: (k, j)))
    in_specs = [a_spec, b_spec]
    if bias is not None:
        in_specs.append(pl.BlockSpec((1, tn), lambda i, j, k: (0, j)))
    for _ in extras[(1 if bias is not None else 0):]:
        in_specs.append(pl.BlockSpec((tm, tn), lambda i, j, k: (i, j)))
    o_spec = pl.BlockSpec((tm, tn), lambda i, j, k: (i, j))
    out_specs, out_shape = [o_spec], [jax.ShapeDtypeStruct((m, n), out_dtype)]
    scratch = [pltpu.VMEM((tm, tn), F32)] if nk > 1 else []
    if relu2_out:
        out_specs.append(o_spec)
        out_shape.append(jax.ShapeDtypeStruct((m, n), BF16))
    if b_colsum:
        out_specs.append(pl.BlockSpec((1, tn), lambda i, j, k: (0, j)))
        out_shape.append(jax.ShapeDtypeStruct((1, n), F32))
        scratch.append(pltpu.VMEM((8, tn), F32))
    if ng:
        in_specs += [HBM_SPEC] * ng
        out_specs += [HBM_SPEC] * ng
        out_shape += _gathered_shapes(gather_src)
        scratch += _gather_sems(ng)
    single = len(out_specs) == 1
    outs = pl.pallas_call(
        body, name=name,
        grid=grid,
        in_specs=in_specs,
        out_specs=out_specs[0] if single else out_specs,
        out_shape=out_shape[0] if single else out_shape,
        scratch_shapes=scratch,
        compiler_params=_cparams(("arbitrary",) * 3 if ng else ("parallel", "parallel", "arbitrary")),
    )(a, b, *extras, *(gather_src or []))
    if not ng:
        return outs
    n_out = len(outs) - ng
    return (outs[0] if n_out == 1 else tuple(outs[:n_out])), list(outs[n_out:])


def _rms_fwd_call(x, g, *, name, out_dtype=F32, tq=512):
    t, d = x.shape

    def body(x_ref, g_ref, y_ref):
        xv = x_ref[...]
        r = lax.rsqrt(jnp.mean(xv * xv, axis=-1, keepdims=True) + EPS)
        y_ref[...] = (xv * r * g_ref[...]).astype(out_dtype)

    return pl.pallas_call(
        body, name=name, grid=(t // tq,),
        in_specs=[pl.BlockSpec((tq, d), lambda i: (i, 0)), pl.BlockSpec((1, d), lambda i: (0, 0))],
        out_specs=pl.BlockSpec((tq, d), lambda i: (i, 0)),
        out_shape=jax.ShapeDtypeStruct((t, d), out_dtype),
        compiler_params=_cparams(("parallel",)),
    )(x, g)


def _rms_bwd_call(x, g, dy, *, name, add=None, tq=512):
    t, d = x.shape
    nt = t // tq

    def body(*refs):
        if add is None:
            x_ref, g_ref, dy_ref, dx_ref, dg_ref, acc_ref = refs
        else:
            x_ref, g_ref, dy_ref, add_ref, dx_ref, dg_ref, acc_ref = refs
        i = pl.program_id(0)
        xv = x_ref[...]
        r = lax.rsqrt(jnp.mean(xv * xv, axis=-1, keepdims=True) + EPS)
        xh = xv * r
        dyv = dy_ref[...]
        dxh = dyv * g_ref[...]
        dx = r * (dxh - xh * jnp.mean(dxh * xh, axis=-1, keepdims=True))
        dx_ref[...] = dx if add is None else dx + add_ref[...]
        part = jnp.sum((dyv * xh).reshape(tq // 8, 8, d), axis=0)

        @pl.when(i == 0)
        def _():
            acc_ref[...] = part

        @pl.when(i > 0)
        def _():
            acc_ref[...] += part

        @pl.when(i == nt - 1)
        def _():
            dg_ref[...] = jnp.sum(acc_ref[...], axis=0, keepdims=True)

    blk = pl.BlockSpec((tq, d), lambda i: (i, 0))
    row = pl.BlockSpec((1, d), lambda i: (0, 0))
    extra = [] if add is None else [add]
    return pl.pallas_call(
        body, name=name, grid=(nt,),
        in_specs=[blk, row, blk] + [blk] * len(extra),
        out_specs=[blk, row],
        out_shape=[jax.ShapeDtypeStruct((t, d), F32), jax.ShapeDtypeStruct((1, d), F32)],
        scratch_shapes=[pltpu.VMEM((8, d), F32)],
        compiler_params=_cparams(("arbitrary",)),
    )(x, g, dy, *extra)


def _loss_call(x, g, tgt, *, tq=512):
    t, d = x.shape
    nt = t // tq

    def body(x_ref, g_ref, t_ref, loss_ref, dx_ref, dg_ref, acc_ref, sq_ref):
        i = pl.program_id(0)
        xv = x_ref[...]
        r = lax.rsqrt(jnp.mean(xv * xv, axis=-1, keepdims=True) + EPS)
        xh = xv * r
        gv = g_ref[...]
        err = xh * gv - t_ref[...]
        dyv = err * (1.0 / d)
        dxh = dyv * gv
        dx_ref[...] = r * (dxh - xh * jnp.mean(dxh * xh, axis=-1, keepdims=True))
        part = jnp.sum((dyv * xh).reshape(tq // 8, 8, d), axis=0)
        sq = jnp.sum((err * err).reshape(tq // 8, 8, d), axis=0)

        @pl.when(i == 0)
        def _():
            acc_ref[...] = part
            sq_ref[...] = sq

        @pl.when(i > 0)
        def _():
            acc_ref[...] += part
            sq_ref[...] += sq

        @pl.when(i == nt - 1)
        def _():
            dg_ref[...] = jnp.sum(acc_ref[...], axis=0, keepdims=True)
            tot = jnp.sum(jnp.sum(sq_ref[...], axis=0, keepdims=True), axis=1, keepdims=True)
            loss_ref[...] = jnp.broadcast_to(tot * (0.5 / d), (8, LANES))

    blk = pl.BlockSpec((tq, d), lambda i: (i, 0))
    row = pl.BlockSpec((1, d), lambda i: (0, 0))
    loss, dx, dg = pl.pallas_call(
        body, name="loss", grid=(nt,),
        in_specs=[blk, row, blk],
        out_specs=[pl.BlockSpec((8, LANES), lambda i: (0, 0)), blk, row],
        out_shape=[jax.ShapeDtypeStruct((8, LANES), F32), jax.ShapeDtypeStruct((t, d), F32),
                   jax.ShapeDtypeStruct((1, d), F32)],
        scratch_shapes=[pltpu.VMEM((8, d), F32), pltpu.VMEM((8, d), F32)],
        compiler_params=_cparams(("arbitrary",)),
    )(x, g, tgt)
    return loss[0, 0], dx, dg


MESH_ID = pl.DeviceIdType.MESH
HBM_SPEC = pl.BlockSpec(memory_space=pl.ANY)


def rope_tables(positions):
    half = ROT_DIM // 2
    inv_freq = jnp.power(ROPE_THETA, -jnp.arange(0, ROT_DIM, 2, dtype=F32) / ROT_DIM)
    in_head = jnp.arange(LANES) % HEAD_DIM
    rot = in_head < ROT_DIM
    freq = jnp.where(rot, inv_freq[in_head % half], 0.0)
    ang = positions.astype(F32)[:, None] * freq[None, :]
    cos, sin = jnp.cos(ang), jnp.sin(ang)
    b = jnp.where(jnp.logical_and(rot, in_head >= half)[None, :], sin, 0.0)
    c = jnp.where((in_head < half)[None, :], -sin, 0.0)
    return cos, b, c


def _rope_chunk(xs, a, b, c, transpose):
    half = ROT_DIM // 2
    if transpose:
        return xs * a + pltpu.roll(xs * b, LANES - half, 1) + pltpu.roll(xs * c, half, 1)
    return xs * a + pltpu.roll(xs, half, 1) * b + pltpu.roll(xs, LANES - half, 1) * c


def _rope_gather_call(u, tabs, parts, *, name, tq=512):
    t = u.shape[0]
    total = sum(w for _, w, _, _ in parts)
    assert all(start % w == 0 for start, w, _, _ in parts)

    def body(a_ref, b_ref, c_ref, *refs):
        o_ref = refs[-1]
        a, b, c = a_ref[...], b_ref[...], c_ref[...]
        off = 0
        for x_ref, (_, w, roped, scale) in zip(refs[:-1], parts):
            for j in range(w // LANES):
                xs = x_ref[:, j * LANES:(j + 1) * LANES]
                val = _rope_chunk(xs, a, b, c, False) if roped else xs
                o_ref[:, off + j * LANES:off + (j + 1) * LANES] = (val * scale if scale != 1.0 else val).astype(BF16)
            off += w

    tab_spec = pl.BlockSpec((tq, LANES), lambda i: (i, 0))
    return pl.pallas_call(
        body, name=name, grid=(t // tq,),
        in_specs=[tab_spec] * 3 + [pl.BlockSpec((tq, w), lambda i, cb=start // w: (i, cb)) for start, w, _, _ in parts],
        out_specs=pl.BlockSpec((tq, total), lambda i: (i, 0)),
        out_shape=jax.ShapeDtypeStruct((t, total), BF16),
        compiler_params=_cparams(("parallel",)),
    )(*tabs, *([u] * len(parts)))


def _du_operands(du_buf, n_inputs):
    if du_buf is None:
        return [], [], {}
    return [du_buf], [HBM_SPEC], {n_inputs: 0}


def _du_shape(t):
    return jax.ShapeDtypeStruct((t, D_IN_PAD), BF16)


def _rope_scatter_call(du_buf, t, pieces, col, tabs, *, name, tq=512):
    total = sum(w for _, w, _, _ in pieces)
    assert col % total == 0
    arrays = [a for arrs, _, _, _ in pieces for a in arrs]
    extra, extra_specs, aliases = _du_operands(du_buf, 3 + len(arrays))

    def body(a_ref, b_ref, c_ref, *refs):
        o_ref = refs[len(arrays) + len(extra)]
        a, b, c = a_ref[...], b_ref[...], c_ref[...]
        k = off = 0
        for arrs, w, roped, scale in pieces:
            mine = refs[k:k + len(arrs)]
            k += len(arrs)
            for j in range(w // LANES):
                cs = slice(j * LANES, (j + 1) * LANES)
                xs = mine[0][:, cs].astype(F32)
                for r in mine[1:]:
                    xs = xs + r[:, cs].astype(F32)
                if scale != 1.0:
                    xs = xs * scale
                val = _rope_chunk(xs, a, b, c, True) if roped else xs
                o_ref[:, off + j * LANES:off + (j + 1) * LANES] = val.astype(BF16)
            off += w

    tab_spec = pl.BlockSpec((tq, LANES), lambda i: (i, 0))
    in_specs = [tab_spec] * 3 + [pl.BlockSpec((tq, w), lambda i: (i, 0)) for arrs, w, _, _ in pieces for _ in arrs]
    return pl.pallas_call(
        body, name=name, grid=(t // tq,),
        in_specs=in_specs + extra_specs,
        out_specs=pl.BlockSpec((tq, total), lambda i: (i, col // total)),
        out_shape=_du_shape(t), input_output_aliases=aliases,
        compiler_params=_cparams(("parallel",)),
    )(*tabs, *arrays, *extra)


def _band_masks(first_block, max_dist):
    qi = lax.broadcasted_iota(jnp.int32, (BLK, BLK), 0)
    kj = lax.broadcasted_iota(jnp.int32, (BLK, BLK), 1)
    valid_prev = jnp.logical_and(kj >= qi + (BLK - max_dist), jnp.logical_not(first_block))
    valid_cur = kj <= qi
    return valid_prev, valid_cur


_NN = (((1,), (0,)), ((), ()))
_NT = (((1,), (1,)), ((), ()))
_TN = (((0,), (0,)), ((), ()))


HEAD_STAGE = 8


def _attn_row_maps(nb):
    def cur(i):
        return jnp.minimum(i, nb - 1)

    def prev(i):
        return jnp.maximum(jnp.minimum(i, nb - 1) - 1, 0)

    return cur, prev


def _dil_view(a, dil):
    t, w = a.shape
    return a.reshape(t // dil, dil * w)


def _dil_spec(w, dil, rows, seg=None, off=0):
    seg = w if seg is None else seg
    assert off % w == 0 and (dil == 1 or seg % w == 0)
    return pl.BlockSpec((BLK, w), lambda r, i: (rows(i), (r * seg + off) // w))


def _dil_shape(l, dil, w, dtype=F32):
    return jax.ShapeDtypeStruct((l, dil * w), dtype)


def _attn_fwd_call(qkv2, sink, *, dil, group, max_dist, seg, offs, qw, kw, name):
    l = qkv2.shape[0]
    nh = qw // HEAD_DIM
    nb = l // BLK
    use_sink = sink is not None

    def body(*refs):
        if use_sink:
            sink_ref, refs = refs[0], refs[1:]
        q_ref, kp_ref, kc_ref, vp_ref, vc_ref, o_ref, lse_ref = refs
        valid_prev, valid_cur = _band_masks(pl.program_id(1) == 0, max_dist)
        lane = lax.broadcasted_iota(jnp.int32, (BLK, LANES), 1)
        lse_tile = jnp.zeros((BLK, LANES), F32)

        def dot(a, b, dims=_NN):
            return lax.dot_general(a, b, dims, preferred_element_type=F32)

        for g0 in range(0, nh, HEAD_STAGE):
            heads = list(range(g0, min(g0 + HEAD_STAGE, nh)))
            kv = {}
            for kh in sorted({h // group for h in heads}):
                ks = slice(kh * HEAD_DIM, (kh + 1) * HEAD_DIM)
                kv[kh] = tuple(ref[:, ks].astype(BF16) for ref in (kp_ref, kc_ref, vp_ref, vc_ref))
            qs = [q_ref[:, h * HEAD_DIM:(h + 1) * HEAD_DIM].astype(BF16) for h in heads]
            sps = [jnp.where(valid_prev, dot(qh, kv[h // group][0], _NT), NEG_INF) for h, qh in zip(heads, qs)]
            scs = [jnp.where(valid_cur, dot(qh, kv[h // group][1], _NT), NEG_INF) for h, qh in zip(heads, qs)]
            ms = [jnp.maximum(jnp.max(sp, axis=1, keepdims=True), jnp.max(sc, axis=1, keepdims=True))
                  for sp, sc in zip(sps, scs)]
            if use_sink:
                ms = [jnp.maximum(m, sink_ref[h]) for h, m in zip(heads, ms)]
            pps = [jnp.exp(sp - m) for sp, m in zip(sps, ms)]
            pcs = [jnp.exp(sc - m) for sc, m in zip(scs, ms)]
            dens = [jnp.sum(pp, axis=1, keepdims=True) + jnp.sum(pc, axis=1, keepdims=True)
                    for pp, pc in zip(pps, pcs)]
            if use_sink:
                dens = [den + jnp.exp(sink_ref[h] - m) for h, den, m in zip(heads, dens, ms)]
            outs = [dot(pp.astype(BF16), kv[h // group][2]) + dot(pc.astype(BF16), kv[h // group][3])
                    for h, pp, pc in zip(heads, pps, pcs)]
            for h, o, den, m in zip(heads, outs, dens, ms):
                o_ref[:, h * HEAD_DIM:(h + 1) * HEAD_DIM] = o / den
                lse_tile = jnp.where(lane == h, m + jnp.log(den), lse_tile)
        lse_ref[...] = lse_tile

    cur, prev = _attn_row_maps(nb)
    o_spec, lse_spec = _dil_spec(qw, dil, cur), _dil_spec(LANES, dil, cur)
    in_specs = [_dil_spec(qw, dil, cur, seg, offs[0]),
                _dil_spec(kw, dil, prev, seg, offs[1]), _dil_spec(kw, dil, cur, seg, offs[1]),
                _dil_spec(kw, dil, prev, seg, offs[2]), _dil_spec(kw, dil, cur, seg, offs[2])]
    args = [qkv2] * 5
    if use_sink:
        in_specs = [pl.BlockSpec(memory_space=pltpu.SMEM)] + in_specs
        args = [sink] + args
    return pl.pallas_call(
        body, name=name, grid=(dil, nb),
        in_specs=in_specs,
        out_specs=[o_spec, lse_spec],
        out_shape=[_dil_shape(l, dil, qw), _dil_shape(l, dil, LANES)],
        compiler_params=_cparams(("parallel", "parallel")),
    )(*args)


def _attn_bwd_call(qkv2, sink, o2, lse2, do2, dlse2, *, dil, group, max_dist, seg, offs, qw, kw, name):
    l = qkv2.shape[0]
    nh = qw // HEAD_DIM
    nb = l // BLK
    use_sink = sink is not None

    def body(*refs):
        if use_sink:
            sink_ref, refs = refs[0], refs[1:]
        (q_ref, kp_ref, kc_ref, vp_ref, vc_ref, o_ref, lse_ref, do_ref, dlse_ref,
         dq_ref, dk_ref, dv_ref, dsink_ref, ck_ref, cv_ref) = refs
        step = pl.program_id(1)

        @pl.when(jnp.logical_and(pl.program_id(0) == 0, step == 0))
        def _():
            dsink_ref[...] = jnp.zeros_like(dsink_ref)

        @pl.when(step == 0)
        def _():
            ck_ref[...] = jnp.zeros_like(ck_ref)
            cv_ref[...] = jnp.zeros_like(cv_ref)

        def dot(a, b, dims=_NN):
            return lax.dot_general(a, b, dims, preferred_element_type=F32)

        @pl.when(step < nb)
        def _():
            valid_prev, valid_cur = _band_masks(step == 0, max_dist)
            row = lax.broadcasted_iota(jnp.int32, (8, LANES), 0)
            lanes8 = lax.broadcasted_iota(jnp.int32, (8, LANES), 1)
            ds_tile = jnp.zeros((8, LANES), F32)
            for g0 in range(0, nh, HEAD_STAGE):
                heads = list(range(g0, min(g0 + HEAD_STAGE, nh)))
                hss = [slice(h * HEAD_DIM, (h + 1) * HEAD_DIM) for h in heads]
                kv = {}
                for kh in sorted({h // group for h in heads}):
                    ks = slice(kh * HEAD_DIM, (kh + 1) * HEAD_DIM)
                    kv[kh] = tuple(ref[:, ks].astype(BF16) for ref in (kp_ref, kc_ref, vp_ref, vc_ref))
                qs = [q_ref[:, hs].astype(BF16) for hs in hss]
                dos = [do_ref[:, hs] for hs in hss]
                dobs = [d.astype(BF16) for d in dos]
                lses = [lse_ref[:, h:h + 1] for h in heads]
                sps = [dot(qh, kv[h // group][0], _NT) for h, qh in zip(heads, qs)]
                scs = [dot(qh, kv[h // group][1], _NT) for h, qh in zip(heads, qs)]
                dpps = [dot(dob, kv[h // group][2], _NT) for h, dob in zip(heads, dobs)]
                dpcs = [dot(dob, kv[h // group][3], _NT) for h, dob in zip(heads, dobs)]
                pps = [jnp.where(valid_prev, jnp.exp(jnp.where(valid_prev, sp, NEG_INF) - ls), 0.0)
                       for sp, ls in zip(sps, lses)]
                pcs = [jnp.where(valid_cur, jnp.exp(jnp.where(valid_cur, sc, NEG_INF) - ls), 0.0)
                       for sc, ls in zip(scs, lses)]
                deltas = [jnp.sum(d * o_ref[:, hs], axis=1, keepdims=True) for d, hs in zip(dos, hss)]
                corrs = [dlse_ref[:, h:h + 1] - dl for h, dl in zip(heads, deltas)]
                dsps = [(pp * (dp + c)).astype(BF16) for pp, dp, c in zip(pps, dpps, corrs)]
                dscs = [(pc * (dp + c)).astype(BF16) for pc, dp, c in zip(pcs, dpcs, corrs)]
                for h, hs, dsp, dsc in zip(heads, hss, dsps, dscs):
                    dq = dot(dsp, kv[h // group][0]) + dot(dsc, kv[h // group][1])
                    dq_ref[:, hs] = dq.astype(BF16)
                parts = [(dot(dsc, qh, _TN), dot(dsp, qh, _TN),
                          dot(pc.astype(BF16), dob, _TN), dot(pp.astype(BF16), dob, _TN))
                         for dsc, dsp, qh, pc, pp, dob in zip(dscs, dsps, qs, pcs, pps, dobs)]
                for kh in kv:
                    ks = slice(kh * HEAD_DIM, (kh + 1) * HEAD_DIM)
                    mine = [p for h, p in zip(heads, parts) if h // group == kh]
                    dkc, dkp, dvc, dvp = (sum(p[j] for p in mine[1:]) + mine[0][j] for j in range(4))
                    dk_ref[:, ks] = (ck_ref[:, ks] + dkp).astype(BF16)
                    dv_ref[:, ks] = (cv_ref[:, ks] + dvp).astype(BF16)
                    ck_ref[:, ks] = dkc
                    cv_ref[:, ks] = dvc
                if use_sink:
                    for h, ls, dl in zip(heads, lses, deltas):
                        val = -jnp.sum(jnp.exp(sink_ref[h] - ls) * dl, axis=0, keepdims=True)
                        ds_tile = jnp.where(jnp.logical_and(row == 0, lanes8 == h), val, ds_tile)
            if use_sink:
                dsink_ref[...] += ds_tile

        @pl.when(step == nb)
        def _():
            dk_ref[...] = ck_ref[...].astype(BF16)
            dv_ref[...] = cv_ref[...].astype(BF16)

    cur, prev = _attn_row_maps(nb)
    q_spec, lse_spec = _dil_spec(qw, dil, cur), _dil_spec(LANES, dil, cur)
    lag_spec = _dil_spec(kw, dil, lambda i: jnp.maximum(i - 1, 0))
    in_specs = [_dil_spec(qw, dil, cur, seg, offs[0]),
                _dil_spec(kw, dil, prev, seg, offs[1]), _dil_spec(kw, dil, cur, seg, offs[1]),
                _dil_spec(kw, dil, prev, seg, offs[2]), _dil_spec(kw, dil, cur, seg, offs[2]),
                q_spec, lse_spec, q_spec, lse_spec]
    args = [qkv2] * 5 + [o2, lse2, do2, dlse2]
    if use_sink:
        in_specs = [pl.BlockSpec(memory_space=pltpu.SMEM)] + in_specs
        args = [sink] + args
    kv_shape = _dil_shape(l, dil, kw, BF16)
    return pl.pallas_call(
        body, name=name, grid=(dil, nb + 1),
        in_specs=in_specs,
        out_specs=[q_spec, lag_spec, lag_spec, pl.BlockSpec((8, LANES), lambda r, i: (0, 0))],
        out_shape=[_dil_shape(l, dil, qw, BF16), kv_shape, kv_shape,
                   jax.ShapeDtypeStruct((8, LANES), F32)],
        scratch_shapes=[pltpu.VMEM((BLK, kw), F32), pltpu.VMEM((BLK, kw), F32)],
        compiler_params=_cparams(("arbitrary", "arbitrary")),
    )(*args)


def _attn_config(tag, dil, group, max_dist, seg, offs, qw, kw):
    return dict(name=tag, dil=dil, group=group, max_dist=max_dist, seg=seg, offs=offs, qw=qw, kw=kw)


A_W = 8 * HEAD_DIM
ATTN_A_CFGS = tuple(_attn_config("attn_a%d" % dil, dil, 1, window // dil, 3 * A_W, (0, A_W, 2 * A_W), A_W, A_W)
                    for window, dil in A_CONFIGS)
B_KVW = 2 * HEAD_DIM
ATTN_B_CFG = _attn_config("attn_b", 1, B_GROUP, BLK - 1, A_W + 2 * B_KVW, (0, A_W, A_W + B_KVW), A_W, B_KVW)


def _attn_fwd(cfg, qkv, sink):
    t = qkv.shape[0]
    kw = {k: v for k, v in cfg.items() if k != "name"}
    qkv2 = _dil_view(qkv, cfg["dil"])
    o2, lse2 = _attn_fwd_call(qkv2, sink, name=cfg["name"] + "_fwd", **kw)
    return o2.reshape(t, cfg["qw"]), lse2.reshape(t, LANES), (qkv2, o2, lse2)


def _attn_bwd(cfg, saved, sink, do, dlse):
    qkv2, o2, lse2 = saved
    t = do.shape[0]
    kw = {k: v for k, v in cfg.items() if k != "name"}
    dq2, dk2, dv2, dsink = _attn_bwd_call(qkv2, sink, o2, lse2, _dil_view(do, cfg["dil"]),
                                          _dil_view(dlse, cfg["dil"]), name=cfg["name"] + "_bwd", **kw)
    return dq2.reshape(t, cfg["qw"]), dk2.reshape(t, cfg["kw"]), dv2.reshape(t, cfg["kw"]), dsink


def _head_expand():
    r = lax.broadcasted_iota(jnp.int32, (LANES, 8 * HEAD_DIM), 0)
    c = lax.broadcasted_iota(jnp.int32, (LANES, 8 * HEAD_DIM), 1)
    return (c // HEAD_DIM == r).astype(F32)


def _combine_weights(l0, l1, l2):
    m = jnp.maximum(jnp.maximum(l0, l1), l2)
    e0, e1, e2 = jnp.exp(l0 - m), jnp.exp(l1 - m), jnp.exp(l2 - m)
    inv = 1.0 / (e0 + e1 + e2)
    return e0 * inv, e1 * inv, e2 * inv


def _combine_fwd_call(os_, lses, *, tq=512):
    t, w = os_[0].shape

    def body(o0, o1, o2, l0, l1, l2, y_ref):
        ws = _combine_weights(l0[...], l1[...], l2[...])
        e = _head_expand()
        y = jnp.zeros((tq, w), F32)
        for o_ref, wt in zip((o0, o1, o2), ws):
            y = y + _dot_mask(e, wt, mask_left=False) * o_ref[...]
        y_ref[...] = y

    o_spec = pl.BlockSpec((tq, w), lambda i: (i, 0))
    l_spec = pl.BlockSpec((tq, LANES), lambda i: (i, 0))
    return pl.pallas_call(
        body, name="combine_fwd", grid=(t // tq,),
        in_specs=[o_spec] * 3 + [l_spec] * 3, out_specs=o_spec,
        out_shape=jax.ShapeDtypeStruct((t, w), F32),
        compiler_params=_cparams(("parallel",)),
    )(*os_, *lses)


def _combine_bwd_call(os_, lses, dy, *, tq=512):
    t, w = dy.shape

    def body(o0, o1, o2, l0, l1, l2, dy_ref, do0, do1, do2, dl0, dl1, dl2):
        ws = _combine_weights(l0[...], l1[...], l2[...])
        e = _head_expand()
        dyv = dy_ref[...]
        dws = []
        for o_ref, do_ref, wt in zip((o0, o1, o2), (do0, do1, do2), ws):
            do_ref[...] = _dot_mask(e, wt, mask_left=False) * dyv
            dws.append(_dot_mask(e, dyv * o_ref[...], _NT, mask_left=False))
        mean = ws[0] * dws[0] + ws[1] * dws[1] + ws[2] * dws[2]
        for dl_ref, wt, dw in zip((dl0, dl1, dl2), ws, dws):
            dl_ref[...] = wt * (dw - mean)

    o_spec = pl.BlockSpec((tq, w), lambda i: (i, 0))
    l_spec = pl.BlockSpec((tq, LANES), lambda i: (i, 0))
    o_shape = jax.ShapeDtypeStruct((t, w), F32)
    l_shape = jax.ShapeDtypeStruct((t, LANES), F32)
    return pl.pallas_call(
        body, name="combine_bwd", grid=(t // tq,),
        in_specs=[o_spec] * 3 + [l_spec] * 3 + [o_spec], out_specs=[o_spec] * 3 + [l_spec] * 3,
        out_shape=[o_shape] * 3 + [l_shape] * 3,
        compiler_params=_cparams(("parallel",)),
    )(*os_, *lses, dy)


C_QKW = C_QK_HEADS * C_DK
C_CONV_W = 2 * C_QKW + C_V_HEADS * C_DK
HALO = 8


def _silu_parts(z):
    sig = jax.nn.sigmoid(z)
    return z * sig, sig * (1.0 + z * (1.0 - sig))


def _conv_window_specs(tq, t):
    c = C_CONV_W
    cb = COL["c_qkv"] // c
    blk = pl.BlockSpec((tq, c), lambda i: (i, cb))
    before = pl.BlockSpec((HALO, c), lambda i: (jnp.maximum(i * (tq // HALO) - 1, 0), cb))
    return c, cb, blk, before


def _conv_prep_fwd_call(u, w, *, tq=512):
    t = u.shape[0]
    c, _, x_spec, halo_spec = _conv_window_specs(tq, t)
    nqk = 2 * C_QK_HEADS

    def body(x_ref, halo_ref, w_ref, z_ref, qk_ref, v_ref):
        i = pl.program_id(0)
        halo = jnp.where(i == 0, 0.0, halo_ref[...])
        xc = jnp.concatenate([halo, x_ref[...]], axis=0)
        wv = w_ref[...]
        z = xc[HALO - 3:HALO - 3 + tq] * wv[0:1]
        for j in range(1, C_CONV):
            z = z + xc[HALO - 3 + j:HALO - 3 + j + tq] * wv[j:j + 1]
        z_ref[...] = z
        act, _ = _silu_parts(z)
        for h in range(nqk):
            a = act[:, h * C_DK:(h + 1) * C_DK]
            qk_ref[:, h * C_DK:(h + 1) * C_DK] = a * lax.rsqrt(jnp.sum(a * a, axis=1, keepdims=True) + EPS)
        v_ref[...] = act[:, nqk * C_DK:]

    return pl.pallas_call(
        body, name="conv_prep_fwd", grid=(t // tq,),
        in_specs=[x_spec, halo_spec, pl.BlockSpec((C_CONV, c), lambda i: (0, 0))],
        out_specs=[pl.BlockSpec((tq, c), lambda i: (i, 0)),
                   pl.BlockSpec((tq, 2 * C_QKW), lambda i: (i, 0)),
                   pl.BlockSpec((tq, c - 2 * C_QKW), lambda i: (i, 0))],
        out_shape=[jax.ShapeDtypeStruct((t, c), F32), jax.ShapeDtypeStruct((t, 2 * C_QKW), F32),
                   jax.ShapeDtypeStruct((t, c - 2 * C_QKW), F32)],
        compiler_params=_cparams(("parallel",)),
    )(u, u, w)


def _conv_prep_dz_call(z, dqk, dv, *, tq=512):
    t, c = z.shape
    nqk = 2 * C_QK_HEADS

    def body(z_ref, dqk_ref, dv_ref, dz_ref):
        zv = z_ref[...]
        act, dact = _silu_parts(zv)
        for h in range(nqk):
            hs = slice(h * C_DK, (h + 1) * C_DK)
            a = act[:, hs]
            r = lax.rsqrt(jnp.sum(a * a, axis=1, keepdims=True) + EPS)
            nrm = a * r
            dn = dqk_ref[:, hs]
            da = r * (dn - nrm * jnp.sum(dn * nrm, axis=1, keepdims=True))
            dz_ref[:, hs] = da * dact[:, hs]
        dz_ref[:, nqk * C_DK:] = dv_ref[...] * dact[:, nqk * C_DK:]

    return pl.pallas_call(
        body, name="conv_prep_dz", grid=(t // tq,),
        in_specs=[pl.BlockSpec((tq, c), lambda i: (i, 0)),
                  pl.BlockSpec((tq, 2 * C_QKW), lambda i: (i, 0)),
                  pl.BlockSpec((tq, c - 2 * C_QKW), lambda i: (i, 0))],
        out_specs=pl.BlockSpec((tq, c), lambda i: (i, 0)),
        out_shape=jax.ShapeDtypeStruct((t, c), F32),
        compiler_params=_cparams(("parallel",)),
    )(z, dqk, dv)


def _conv_bwd_call(u, dz, w, du_buf, *, tq=512):
    t = u.shape[0]
    nt = t // tq
    c, cb, x_spec, halo_spec = _conv_window_specs(tq, t)
    extra, extra_specs, aliases = _du_operands(du_buf, 5)

    def body(x_ref, xh_ref, dz_ref, dzh_ref, w_ref, *refs):
        dx_ref, dw_ref = refs[len(extra):]
        i = pl.program_id(0)
        xc = jnp.concatenate([jnp.where(i == 0, 0.0, xh_ref[...]), x_ref[...]], axis=0)
        dzv = dz_ref[...]
        dzc = jnp.concatenate([dzv, jnp.where(i == nt - 1, 0.0, dzh_ref[...])], axis=0)
        wv = w_ref[...]
        dx = dzv * wv[3:4]
        for s in range(1, C_CONV):
            dx = dx + dzc[s:s + tq] * wv[3 - s:4 - s]
        dx_ref[...] = dx.astype(BF16)
        row = lax.broadcasted_iota(jnp.int32, (8, c), 0)
        dw = jnp.zeros((8, c), F32)
        for j in range(C_CONV):
            prod = dzv * xc[HALO - 3 + j:HALO - 3 + j + tq]
            col = jnp.sum(jnp.sum(prod.reshape(tq // 8, 8, c), axis=0), axis=0, keepdims=True)
            dw = jnp.where(row == j, col, dw)

        @pl.when(i == 0)
        def _():
            dw_ref[...] = dw

        @pl.when(i > 0)
        def _():
            dw_ref[...] += dw

    blk = pl.BlockSpec((tq, c), lambda i: (i, 0))
    after = pl.BlockSpec((HALO, c), lambda i: (jnp.minimum((i + 1) * (tq // HALO), t // HALO - 1), 0))
    return pl.pallas_call(
        body, name="conv_bwd", grid=(nt,),
        in_specs=[x_spec, halo_spec, blk, after, pl.BlockSpec((C_CONV, c), lambda i: (0, 0))] + extra_specs,
        out_specs=[pl.BlockSpec((tq, c), lambda i: (i, cb)), pl.BlockSpec((8, c), lambda i: (0, 0))],
        out_shape=[_du_shape(t), jax.ShapeDtypeStruct((8, c), F32)],
        input_output_aliases=aliases,
        compiler_params=_cparams(("arbitrary",)),
    )(u, u, dz, dz, w, *extra)


C_VW = C_V_HEADS * C_DK


def _softplus(x):
    return jnp.maximum(x, 0.0) + jnp.log(1.0 + jnp.exp(-jnp.abs(x)))


def _tri_masks():
    r = lax.broadcasted_iota(jnp.int32, (CHUNK, CHUNK), 0)
    c = lax.broadcasted_iota(jnp.int32, (CHUNK, CHUNK), 1)
    return r >= c, r > c


def _split_bf16(a):
    hi = a.astype(BF16)
    return hi, (a - hi.astype(F32)).astype(BF16)


def _dot_hi(a, b, dims=None):
    dims = _NN if dims is None else dims
    ah, al = _split_bf16(a)
    bh, bl = _split_bf16(b)

    def d(x, y):
        return lax.dot_general(x, y, dims, preferred_element_type=F32)

    return d(ah, bh) + (d(ah, bl) + d(al, bh))


def _dot_mask(mask, b, dims=None, mask_left=True):
    dims = _NN if dims is None else dims
    mb = mask.astype(BF16)
    b1 = b.astype(BF16)
    rest = b - b1.astype(F32)
    b2 = rest.astype(BF16)
    b3 = (rest - b2.astype(F32)).astype(BF16)
    out = None
    for p in (b1, b2, b3):
        term = (lax.dot_general(mb, p, dims, preferred_element_type=F32) if mask_left
                else lax.dot_general(p, mb, dims, preferred_element_type=F32))
        out = term if out is None else out + term
    return out


def _unit_lower_inverses(mats):
    r = lax.broadcasted_iota(jnp.int32, (CHUNK, CHUNK), 0)
    c = lax.broadcasted_iota(jnp.int32, (CHUNK, CHUNK), 1)
    eye = (r == c).astype(F32)
    xs = [eye - a for a in mats]
    ps = [_dot_hi(a, a) for a in mats]
    steps = int(math.log2(CHUNK)) - 1
    for s in range(steps):
        xs = [x + _dot_hi(x, p) for x, p in zip(xs, ps)]
        if s < steps - 1:
            ps = [_dot_hi(p, p) for p in ps]
    return xs


def _gate_tiles(cab, alog, dtb):
    pre = cab + dtb
    g = -jnp.exp(alog) * _softplus(pre)
    beta = jax.nn.sigmoid(pltpu.roll(cab, LANES - C_V_HEADS, 1))
    return g, beta, pre


def _chunk_common(kk, qk, gc, gct, beta, h, tri, strict):
    gcol, grow, bcol = gc[:, h:h + 1], gct[h:h + 1, :], beta[:, h:h + 1]
    decay = jnp.where(tri, jnp.exp(jnp.where(tri, gcol - grow, 0.0)), 0.0)
    kkd = jnp.where(strict, kk * decay, 0.0)
    attn = jnp.where(tri, qk * decay, 0.0)
    glast = gc[CHUNK - 1:CHUNK, h:h + 1]
    return gcol, bcol, decay, kkd, attn, glast


def _cab_spec():
    return pl.BlockSpec((CHUNK, LANES), lambda n: (n, COL["c_ab"] // LANES))


def _delta_prep_call(qk, v, u, alog, dtb, gather_src=None):
    t = qk.shape[0]
    nc = t // CHUNK
    scale = C_DK ** -0.5
    riding = gather_src is not None
    ng = len(gather_src) if riding else 0

    def body(q_ref, k_ref, v_ref, cab_ref, alog_ref, dtb_ref, *refs):
        if riding:
            gather_refs = (refs[:ng], refs[ng + 8:2 * ng + 8]) + tuple(refs[2 * ng + 8:])
            refs = refs[ng:ng + 8]

            @pl.when(pl.program_id(0) == 0)
            def _():
                _gather_start(*gather_refs)
        u_ref, w_ref, qd_ref, kd_ref, attn_ref, tmat_ref, gc_ref, beta_ref = refs
        tri, strict = _tri_masks()
        g, beta, _ = _gate_tiles(cab_ref[...], alog_ref[...], dtb_ref[...])
        gc = _dot_mask(tri, g)
        gct = gc.T
        gc_ref[...] = gc
        beta_ref[...] = beta
        mats, rhs = [], []
        for j in range(C_QK_HEADS):
            js = slice(j * C_DK, (j + 1) * C_DK)
            kf, qf = k_ref[:, js], q_ref[:, js] * scale
            kb, qb = kf.astype(BF16), qf.astype(BF16)
            kk = lax.dot_general(kb, kb, _NT, preferred_element_type=F32)
            qk = lax.dot_general(qb, kb, _NT, preferred_element_type=F32)
            for h in (2 * j, 2 * j + 1):
                hs = slice(h * C_DK, (h + 1) * C_DK)
                gcol, bcol, decay, kkd, attn, glast = _chunk_common(kk, qk, gc, gct, beta, h, tri, strict)
                gexp = jnp.exp(gcol)
                mats.append(kkd * bcol)
                rhs.append(jnp.concatenate([v_ref[:, hs] * bcol, kf * (bcol * gexp)], axis=1))
                qd_ref[:, hs] = (qf * gexp).astype(BF16)
                kd_ref[:, hs] = (kf * jnp.exp(glast - gcol)).astype(BF16)
                attn_ref[:, h * CHUNK:(h + 1) * CHUNK] = attn.astype(BF16)
        for h, (tmat, r) in enumerate(zip(_unit_lower_inverses(mats), rhs)):
            hs = slice(h * C_DK, (h + 1) * C_DK)
            uw = _dot_hi(tmat, r)
            u_ref[:, hs] = uw[:, :C_DK]
            w_ref[:, hs] = uw[:, C_DK:]
            tmat_ref[:, h * CHUNK:(h + 1) * CHUNK] = tmat
        if riding:
            @pl.when(pl.program_id(0) == nc - 1)
            def _():
                _gather_finish(*gather_refs)

    def blk(w):
        return pl.BlockSpec((CHUNK, w), lambda n: (n, 0))

    row = pl.BlockSpec((1, LANES), lambda n: (0, 0))
    big = jax.ShapeDtypeStruct((t, C_VW), F32)
    sq = jax.ShapeDtypeStruct((t, C_V_HEADS * CHUNK), F32)
    tile = jax.ShapeDtypeStruct((t, LANES), F32)
    half = jax.ShapeDtypeStruct((t, C_VW), BF16)
    in_specs = [blk(C_QKW), pl.BlockSpec((CHUNK, C_QKW), lambda n: (n, 1)), blk(C_VW), _cab_spec(), row, row]
    out_specs = [blk(C_VW)] * 4 + [blk(C_V_HEADS * CHUNK)] * 2 + [blk(LANES)] * 2
    out_shape = [big, big, half, half, jax.ShapeDtypeStruct(sq.shape, BF16), sq] + [tile] * 2
    args = [qk, qk, v, u, alog, dtb]
    if riding:
        in_specs += [HBM_SPEC] * ng
        out_specs += [HBM_SPEC] * ng
        out_shape += _gathered_shapes(gather_src)
        args += list(gather_src)
    return pl.pallas_call(
        body, name="delta_prep_gather" if riding else "delta_prep", grid=(nc,),
        in_specs=in_specs, out_specs=out_specs, out_shape=out_shape,
        scratch_shapes=_gather_sems(ng) if riding else [],
        compiler_params=_cparams(("arbitrary",) if riding else ("parallel",)),
    )(*args)


SCAN_SUB = 4


def _delta_scan_call(u, w, qd, kd, attn, gc):
    t = u.shape[0]
    nc = t // CHUNK
    rows = SCAN_SUB * CHUNK

    def body(u_ref, w_ref, qd_ref, kd_ref, attn_ref, gc_ref, o_ref, vn_ref, st_ref, s_ref):
        @pl.when(pl.program_id(0) == 0)
        def _():
            s_ref[...] = jnp.zeros_like(s_ref)

        hss = [slice(h * C_DK, (h + 1) * C_DK) for h in range(C_V_HEADS)]
        states = [s_ref[hs, :] for hs in hss]
        for c in range(SCAN_SUB):
            rs = slice(c * CHUNK, (c + 1) * CHUNK)
            for hs, s in zip(hss, states):
                st_ref[c, hs, :] = s
            sbs = [s.astype(BF16) for s in states]
            vns = [u_ref[rs, hs] - jnp.dot(w_ref[rs, hs].astype(BF16), sb, preferred_element_type=F32)
                   for hs, sb in zip(hss, sbs)]
            qss = [jnp.dot(qd_ref[rs, hs].astype(BF16), sb, preferred_element_type=F32) for hs, sb in zip(hss, sbs)]
            vnbs = [vn.astype(BF16) for vn in vns]
            for h, hs in enumerate(hss):
                vn_ref[rs, hs] = vnbs[h]
                o_ref[rs, hs] = qss[h] + jnp.dot(attn_ref[rs, h * CHUNK:(h + 1) * CHUNK].astype(BF16), vnbs[h],
                                                 preferred_element_type=F32)
            last = (c + 1) * CHUNK - 1
            states = [states[h] * jnp.exp(gc_ref[last:last + 1, h:h + 1])
                      + lax.dot_general(kd_ref[rs, hs].astype(BF16), vnbs[h], _TN, preferred_element_type=F32)
                      for h, hs in enumerate(hss)]
        for hs, s in zip(hss, states):
            s_ref[hs, :] = s

    def blk(wd):
        return pl.BlockSpec((rows, wd), lambda n: (n, 0))

    big = jax.ShapeDtypeStruct((t, C_VW), F32)
    return pl.pallas_call(
        body, name="delta_scan", grid=(nc // SCAN_SUB,),
        in_specs=[blk(C_VW)] * 4 + [blk(C_V_HEADS * CHUNK), blk(LANES)],
        out_specs=[blk(C_VW), blk(C_VW), pl.BlockSpec((SCAN_SUB, C_VW, C_DK), lambda n: (n, 0, 0))],
        out_shape=[big, jax.ShapeDtypeStruct((t, C_VW), BF16), jax.ShapeDtypeStruct((nc, C_VW, C_DK), F32)],
        scratch_shapes=[pltpu.VMEM((C_VW, C_DK), F32)],
        compiler_params=_cparams(("arbitrary",)),
    )(u, w, qd, kd, attn, gc)


def _delta_scan_bwd_call(do, w, qd, kd, attn, gc, vn, st):
    t = do.shape[0]
    nc = t // CHUNK
    rows = SCAN_SUB * CHUNK
    steps = nc // SCAN_SUB

    def body(do_ref, w_ref, qd_ref, kd_ref, attn_ref, gc_ref, vn_ref, st_ref,
             du_ref, dw_ref, dqd_ref, dkd_ref, dattn_ref, dgl_ref, ds_ref):
        @pl.when(pl.program_id(0) == 0)
        def _():
            ds_ref[...] = jnp.zeros_like(ds_ref)

        tri, _ = _tri_masks()
        row = lax.broadcasted_iota(jnp.int32, (8, LANES), 0)
        lane = lax.broadcasted_iota(jnp.int32, (8, LANES), 1)
        hss = [slice(h * C_DK, (h + 1) * C_DK) for h in range(C_V_HEADS)]
        css = [slice(h * CHUNK, (h + 1) * CHUNK) for h in range(C_V_HEADS)]

        def dg(a, b, dims):
            return lax.dot_general(a, b, dims, preferred_element_type=F32)

        dsps = [ds_ref[hs, :] for hs in hss]
        for c in reversed(range(SCAN_SUB)):
            rs = slice(c * CHUNK, (c + 1) * CHUNK)
            dgl = jnp.zeros((8, LANES), F32)
            ss = [st_ref[c, hs, :] for hs in hss]
            sbs = [s.astype(BF16) for s in ss]
            dspbs = [d.astype(BF16) for d in dsps]
            dobs = [do_ref[rs, hs].astype(BF16) for hs in hss]
            vnbs = [vn_ref[rs, hs].astype(BF16) for hs in hss]
            dvns = [dg(attn_ref[rs, cs].astype(BF16), dob, _TN) + dg(kd_ref[rs, hs].astype(BF16), dspb, _NN)
                    for hs, cs, dob, dspb in zip(hss, css, dobs, dspbs)]
            for h, hs in enumerate(hss):
                dqd_ref[rs, hs] = dg(dobs[h], sbs[h], _NT)
                dkd_ref[rs, hs] = dg(vnbs[h], dspbs[h], _NT)
                dattn_ref[rs, css[h]] = jnp.where(tri, dg(dobs[h], vnbs[h], _NT), 0.0)
            dvnbs = [d.astype(BF16) for d in dvns]
            for h, hs in enumerate(hss):
                du_ref[rs, hs] = dvns[h]
                dw_ref[rs, hs] = -dg(dvnbs[h], sbs[h], _NT)
                tot = jnp.sum(jnp.sum(dsps[h] * ss[h], axis=0, keepdims=True), axis=1, keepdims=True)
                dgl = jnp.where(jnp.logical_and(row == 0, lane == h), tot, dgl)
            dgl_ref[c * 8:(c + 1) * 8, :] = dgl
            last = (c + 1) * CHUNK - 1
            dsps = [dg(qd_ref[rs, hs].astype(BF16), dobs[h], _TN) + jnp.exp(gc_ref[last:last + 1, h:h + 1]) * dsps[h]
                    - dg(w_ref[rs, hs].astype(BF16), dvnbs[h], _TN) for h, hs in enumerate(hss)]
        for hs, d in zip(hss, dsps):
            ds_ref[hs, :] = d

    def blk(wd):
        return pl.BlockSpec((rows, wd), lambda n: (steps - 1 - n, 0))

    big = jax.ShapeDtypeStruct((t, C_VW), F32)
    return pl.pallas_call(
        body, name="delta_scan_bwd", grid=(steps,),
        in_specs=[blk(C_VW)] * 4 + [blk(C_V_HEADS * CHUNK), blk(LANES), blk(C_VW),
                                    pl.BlockSpec((SCAN_SUB, C_VW, C_DK), lambda n: (steps - 1 - n, 0, 0))],
        out_specs=[blk(C_VW)] * 4 + [blk(C_V_HEADS * CHUNK),
                                     pl.BlockSpec((SCAN_SUB * 8, LANES), lambda n: (steps - 1 - n, 0))],
        out_shape=[big] * 4 + [jax.ShapeDtypeStruct((t, C_V_HEADS * CHUNK), F32),
                               jax.ShapeDtypeStruct((nc * 8, LANES), F32)],
        scratch_shapes=[pltpu.VMEM((C_VW, C_DK), F32)],
        compiler_params=_cparams(("arbitrary",)),
    )(do, w, qd, kd, attn, gc, vn, st)


PREP_SUB = 2


def _delta_prep_bwd_call(qk, v, proj, alog, dtb, tmat, u, w, gc, beta, du, dw, dqd, dkd, dattn, dgl, du_buf):
    t = qk.shape[0]
    extra, extra_specs, aliases = _du_operands(du_buf, 17)
    nc = t // CHUNK
    rows = PREP_SUB * CHUNK
    scale = C_DK ** -0.5

    def body(q_ref, k_ref, v_ref, cab_ref, alog_ref, dtb_ref, tmat_ref, u_ref, w_ref, gc_ref, beta_ref,
             du_ref, dw_ref, dqd_ref, dkd_ref, dattn_ref, dgl_ref, *outs):
        dcab_ref, dqk_ref, dv_ref, dpar_ref = outs[len(extra):]
        tri, strict = _tri_masks()
        ones = jnp.ones((CHUNK, LANES), F32)
        lane = lax.broadcasted_iota(jnp.int32, (CHUNK, LANES), 1)
        rowi = lax.broadcasted_iota(jnp.int32, (CHUNK, 1), 0)
        subs = range(PREP_SUB)
        rss = [slice(c * CHUNK, (c + 1) * CHUNK) for c in subs]
        betas = [beta_ref[rs, :] for rs in rss]

        def dot(x, y, dims=_NN):
            return lax.dot_general(x, y, dims, preferred_element_type=F32)

        heads = []
        for c, rs in zip(subs, rss):
            gc = gc_ref[rs, :]
            gct = gc.T
            for j in range(C_QK_HEADS):
                js = slice(j * C_DK, (j + 1) * C_DK)
                kf, qf = k_ref[rs, js], q_ref[rs, js] * scale
                kb, qb = kf.astype(BF16), qf.astype(BF16)
                kk = dot(kb, kb, _NT)
                qk = dot(qb, kb, _NT)
                for h in (2 * j, 2 * j + 1):
                    heads.append((c, rs, h, kf, qf, kb, qb) + _chunk_common(kk, qk, gc, gct, betas[c], h, tri, strict))

        def cols(h):
            return slice(h * C_DK, (h + 1) * C_DK)

        def sq(h):
            return slice(h * CHUNK, (h + 1) * CHUNK)

        dvks = [_dot_hi(tmat_ref[hd[1], sq(hd[2])],
                        jnp.concatenate([du_ref[hd[1], cols(hd[2])], dw_ref[hd[1], cols(hd[2])]], axis=1), _TN)
                for hd in heads]
        das = [-jnp.where(strict, _dot_hi(dvk, jnp.concatenate([u_ref[hd[1], cols(hd[2])], w_ref[hd[1], cols(hd[2])]],
                                                               axis=1), _NT), 0.0)
               for hd, dvk in zip(heads, dvks)]
        pre = []
        for (c, rs, h, kf, qf, kb, qb, gcol, bcol, decay, kkd, attn, glast), da in zip(heads, das):
            dattn_h = dattn_ref[rs, sq(h)]
            pre.append(((da * decay * bcol).astype(BF16), (dattn_h * decay).astype(BF16),
                        da * kkd * bcol + dattn_h * attn))
        mms = [(dot(dkk, hd[5]), dot(dkk, hd[5], _TN), dot(dqk, hd[6], _TN), dot(dqk, hd[5]),
                _dot_mask(ones, e, _TN, mask_left=False))
               for hd, (dkk, dqk, e) in zip(heads, pre)]
        dq_parts, dk_parts = {}, {}
        dgc_tiles = [jnp.zeros((CHUNK, LANES), F32) for _ in subs]
        db_tiles = [jnp.zeros((CHUNK, LANES), F32) for _ in subs]
        for (c, rs, h, kf, qf, kb, qb, gcol, bcol, decay, kkd, attn, glast), dvk, da, (_, _, e), mm in zip(
                heads, dvks, das, pre, mms):
            hs = cols(h)
            gexp = jnp.exp(gcol)
            fdec = jnp.exp(glast - gcol)
            dvb, dkb = dvk[:, :C_DK], dvk[:, C_DK:]
            dgc = jnp.sum(e, axis=1, keepdims=True) - mm[4][:, :1]
            dk_parts[c, h] = mm[0] + mm[1] + mm[2] + dkb * (bcol * gexp) + dkd_ref[rs, hs] * fdec
            dq_parts[c, h] = mm[3] + dqd_ref[rs, hs] * gexp
            dv_ref[rs, hs] = dvb * bcol
            s_kb = jnp.sum(dkb * kf, axis=1, keepdims=True)
            db = (jnp.sum(da * kkd, axis=1, keepdims=True) + jnp.sum(dvb * v_ref[rs, hs], axis=1, keepdims=True)
                  + s_kb * gexp)
            rho = jnp.sum(dkd_ref[rs, hs] * kf, axis=1, keepdims=True) * fdec
            dgc = (dgc + s_kb * bcol * gexp + jnp.sum(dqd_ref[rs, hs] * qf, axis=1, keepdims=True) * gexp - rho)
            last = jnp.sum(rho, axis=0, keepdims=True) + dgl_ref[c * 8:c * 8 + 1, h:h + 1] * jnp.exp(glast)
            dgc = dgc + jnp.where(rowi == CHUNK - 1, last, 0.0)
            dgc_tiles[c] = jnp.where(lane == h, dgc, dgc_tiles[c])
            db_tiles[c] = jnp.where(lane == h, db, db_tiles[c])
        alog = alog_ref[...]
        row8 = lax.broadcasted_iota(jnp.int32, (8, LANES), 0)
        par = jnp.zeros((8, LANES), F32)
        for c, rs in zip(subs, rss):
            for j in range(C_QK_HEADS):
                dqk_ref[rs, j * C_DK:(j + 1) * C_DK] = (dq_parts[c, 2 * j] + dq_parts[c, 2 * j + 1]) * scale
                dqk_ref[rs, C_QKW + j * C_DK:C_QKW + (j + 1) * C_DK] = dk_parts[c, 2 * j] + dk_parts[c, 2 * j + 1]
            dg = _dot_mask(jnp.logical_not(strict), dgc_tiles[c])
            g, _, gate_pre = _gate_tiles(cab_ref[rs, :], alog, dtb_ref[...])
            dca = dg * (-jnp.exp(alog)) * jax.nn.sigmoid(gate_pre)
            beta = betas[c]
            dcab_ref[rs, :LANES] = (dca + pltpu.roll(db_tiles[c] * beta * (1.0 - beta), C_V_HEADS, 1)).astype(BF16)
            dcab_ref[rs, LANES:] = jnp.zeros((CHUNK, D_IN_PAD - COL["c_ab"] - LANES), BF16)
            par = par + jnp.where(row8 == 0, jnp.sum(dg * g, axis=0, keepdims=True),
                                  jnp.where(row8 == 1, jnp.sum(dca, axis=0, keepdims=True), 0.0))

        @pl.when(pl.program_id(0) == 0)
        def _():
            dpar_ref[...] = par

        @pl.when(pl.program_id(0) > 0)
        def _():
            dpar_ref[...] += par

    def blk(wd):
        return pl.BlockSpec((rows, wd), lambda n: (n, 0))

    row = pl.BlockSpec((1, LANES), lambda n: (0, 0))
    sqs = blk(C_V_HEADS * CHUNK)
    tail = D_IN_PAD - COL["c_ab"]
    assert COL["c_ab"] % tail == 0 and nc % PREP_SUB == 0
    return pl.pallas_call(
        body, name="delta_prep_bwd", grid=(nc // PREP_SUB,),
        in_specs=[blk(C_QKW), pl.BlockSpec((rows, C_QKW), lambda n: (n, 1)), blk(C_VW),
                  pl.BlockSpec((rows, LANES), lambda n: (n, COL["c_ab"] // LANES)), row, row, sqs,
                  blk(C_VW), blk(C_VW),
                  blk(LANES), blk(LANES), blk(C_VW), blk(C_VW), blk(C_VW), blk(C_VW), sqs,
                  pl.BlockSpec((PREP_SUB * 8, LANES), lambda n: (n, 0))] + extra_specs,
        out_specs=[pl.BlockSpec((rows, tail), lambda n: (n, COL["c_ab"] // tail)),
                   blk(2 * C_QKW), blk(C_VW), pl.BlockSpec((8, LANES), lambda n: (0, 0))],
        out_shape=[_du_shape(t), jax.ShapeDtypeStruct((t, 2 * C_QKW), F32),
                   jax.ShapeDtypeStruct((t, C_VW), F32), jax.ShapeDtypeStruct((8, LANES), F32)],
        input_output_aliases=aliases,
        compiler_params=_cparams(("arbitrary",)),
    )(qk, qk, v, proj, alog, dtb, tmat, u, w, gc, beta, du, dw, dqd, dkd, dattn, dgl, *extra)


def _z_spec(tq):
    return pl.BlockSpec((tq, C_VW), lambda i: (i, COL["c_z"] // C_VW))


def _gated_norm_fwd_call(o, u, gain, *, tq=512):
    t, w = o.shape

    def body(o_ref, z_ref, g_ref, y_ref):
        act, _ = _silu_parts(z_ref[...])
        gv = g_ref[...]
        for h in range(C_V_HEADS):
            hs = slice(h * C_DK, (h + 1) * C_DK)
            ov = o_ref[:, hs]
            r = lax.rsqrt(jnp.mean(ov * ov, axis=1, keepdims=True) + EPS)
            y_ref[:, hs] = ov * r * gv * act[:, hs]

    blk = pl.BlockSpec((tq, w), lambda i: (i, 0))
    return pl.pallas_call(
        body, name="gated_norm_fwd", grid=(t // tq,),
        in_specs=[blk, _z_spec(tq), pl.BlockSpec((1, C_DK), lambda i: (0, 0))], out_specs=blk,
        out_shape=jax.ShapeDtypeStruct((t, w), F32),
        compiler_params=_cparams(("parallel",)),
    )(o, u, gain)


def _gated_norm_bwd_call(o, u, gain, dy, du_buf, *, tq=512):
    t, w = o.shape
    nt = t // tq
    extra, extra_specs, aliases = _du_operands(du_buf, 4)

    def body(o_ref, z_ref, g_ref, dy_ref, *refs):
        dz_ref, do_ref, dg_ref, acc_ref = refs[len(extra):]
        i = pl.program_id(0)
        act, dact = _silu_parts(z_ref[...])
        gv = g_ref[...]
        part = jnp.zeros((8, C_DK), F32)
        for h in range(C_V_HEADS):
            hs = slice(h * C_DK, (h + 1) * C_DK)
            ov = o_ref[:, hs]
            r = lax.rsqrt(jnp.mean(ov * ov, axis=1, keepdims=True) + EPS)
            xh = ov * r
            dyv = dy_ref[:, hs]
            dn = dyv * act[:, hs]
            dz_ref[:, hs] = (dyv * xh * gv * dact[:, hs]).astype(BF16)
            dxh = dn * gv
            do_ref[:, hs] = r * (dxh - xh * jnp.mean(dxh * xh, axis=1, keepdims=True))
            part = part + jnp.sum((dn * xh).reshape(tq // 8, 8, C_DK), axis=0)

        @pl.when(i == 0)
        def _():
            acc_ref[...] = part

        @pl.when(i > 0)
        def _():
            acc_ref[...] += part

        @pl.when(i == nt - 1)
        def _():
            dg_ref[...] = jnp.sum(acc_ref[...], axis=0, keepdims=True)

    blk = pl.BlockSpec((tq, w), lambda i: (i, 0))
    grow = pl.BlockSpec((1, C_DK), lambda i: (0, 0))
    return pl.pallas_call(
        body, name="gated_norm_bwd", grid=(nt,),
        in_specs=[blk, _z_spec(tq), grow, blk] + extra_specs, out_specs=[_z_spec(tq), blk, grow],
        out_shape=[_du_shape(t), jax.ShapeDtypeStruct((t, w), F32), jax.ShapeDtypeStruct((1, C_DK), F32)],
        scratch_shapes=[pltpu.VMEM((8, C_DK), F32)],
        input_output_aliases=aliases,
        compiler_params=_cparams(("arbitrary",)),
    )(o, u, gain, dy, *extra)


def _gate_specs(tq):
    return [pl.BlockSpec((tq, D_MODEL), lambda i, j=j: (i, j)) for j in range(3)]


def _merge_fwd_call(ps, u, *, tq=512):
    t, w = ps[0].shape

    def body(p0, p1, p2, g0, g1, g2, y_ref):
        y_ref[...] = (jax.nn.sigmoid(g0[...]) * p0[...] + jax.nn.sigmoid(g1[...]) * p1[...]
                      + jax.nn.sigmoid(g2[...]) * p2[...]).astype(BF16)

    blk = pl.BlockSpec((tq, w), lambda i: (i, 0))
    return pl.pallas_call(
        body, name="merge_fwd", grid=(t // tq,), in_specs=[blk] * 3 + _gate_specs(tq), out_specs=blk,
        out_shape=jax.ShapeDtypeStruct((t, w), BF16),
        compiler_params=_cparams(("parallel",)),
    )(*ps, u, u, u)


def _merge_bwd_call(ps, u, dy, *, tq=256):
    t, w = dy.shape

    def body(p0, p1, p2, g0, g1, g2, dy_ref, dg_ref, dp0, dp1, dp2):
        dyv = dy_ref[...]
        for j, (p, g, dp) in enumerate(((p0, g0, dp0), (p1, g1, dp1), (p2, g2, dp2))):
            sig = jax.nn.sigmoid(g[...])
            dp[...] = (dyv * sig).astype(BF16)
            dg_ref[:, j * w:(j + 1) * w] = (dyv * p[...] * sig * (1.0 - sig)).astype(BF16)

    blk = pl.BlockSpec((tq, w), lambda i: (i, 0))
    small = jax.ShapeDtypeStruct((t, w), BF16)
    return pl.pallas_call(
        body, name="merge_bwd", grid=(t // tq,), in_specs=[blk] * 3 + _gate_specs(tq) + [blk],
        out_specs=[pl.BlockSpec((tq, 3 * w), lambda i: (i, 0))] + [blk] * 3,
        out_shape=[_du_shape(t)] + [small] * 3,
        compiler_params=_cparams(("parallel",)),
    )(*ps, u, u, u, dy)


Q_SCALE = HEAD_DIM ** -0.5
A_PARTS = ((COL["a_q"], A_W, True, Q_SCALE), (COL["a_k"], A_W, True, 1.0), (COL["a_v"], A_W, False, 1.0))
B_PARTS = ((COL["b_q"], A_W, True, Q_SCALE), (COL["b_k"], B_KVW, True, 1.0), (COL["b_v"], B_KVW, False, 1.0))
BRANCHES = ("w_branch_a", "w_branch_b", "w_branch_c")


def _layer_fwd(x, tabs, p, w_in_b, rest, rest_of, gather_src=None):
    h = _rms_fwd_call(x, p["norm_mix"], name="rms_mix_fwd", out_dtype=BF16)
    if rest[0] == "ride":
        u, (packed,) = _mm(h, w_in_b, bias=p["b_in"], tn=IN_TN, gather_src=[rest[1]], name="in_proj_fwd_gather")
    else:
        u, packed = _mm(h, w_in_b, bias=p["b_in"], tn=IN_TN, name="in_proj_fwd"), rest[1]
    wb, conv_w = rest_of(packed)
    wb = dict(wb, w_in=w_in_b)
    p = dict(p, conv_w=conv_w)
    qkv_a = _rope_gather_call(u, tabs, A_PARTS, name="rope_a_fwd")
    a_runs = [_attn_fwd(cfg, qkv_a, None) for cfg in ATTN_A_CFGS]
    os_, lses = tuple(r[0] for r in a_runs), tuple(r[1] for r in a_runs)
    ya = _combine_fwd_call(os_, lses)
    qkv_b = _rope_gather_call(u, tabs, B_PARTS, name="rope_b_fwd")
    yb, _, b_saved = _attn_fwd(ATTN_B_CFG, qkv_b, p["sinks"])
    zc, qk, v = _conv_prep_fwd_call(u, p["conv_w"])
    uu, ww, qd, kd, attn, tmat, gc, beta, *gathered = _delta_prep_call(qk, v, u, p["a_log"], p["dt_bias"], gather_src)
    o, vn, st = _delta_scan_call(uu, ww, qd, kd, attn, gc)
    yc = _gated_norm_fwd_call(o, u, p["c_norm"])
    ys = (ya, yb, yc)
    ps = tuple(_mm(y, wb[n], name="branch_fwd") for y, n in zip(ys, BRANCHES))
    merged = _merge_fwd_call(ps, u)
    x1 = _mm(merged, wb["w_out"], add=x, name="out_proj_fwd")
    h2 = _rms_fwd_call(x1, p["norm_ffn"], name="rms_ffn_fwd", out_dtype=BF16)
    pre, act = _mm(h2, wb["w_ff1"], relu2_out=True, name="ffn_up")
    x2 = _mm(act, wb["w_ff2"], add=x1, name="ffn_down")
    saved = dict(x=x, h=h, u=u, a_saved=[r[2] for r in a_runs], os_=os_, lses=lses, b_saved=b_saved,
                 zc=zc, qk=qk, v=v, delta=(tmat, uu, ww, gc, beta, qd, kd, attn, vn, st), o=o, ys=ys, ps=ps,
                 merged=merged, x1=x1, h2=h2, pre=pre, act=act, p=p, wb=wb)
    return x2, saved, (gathered if gathered else None)


def _layer_bwd(s, dx2, tabs):
    g, p, wb = {}, s["p"], s["wb"]
    t = dx2.shape[0]
    dpre = _mm(dx2, wb["w_ff2"], tb=True, mul_drelu2=s["pre"], out_dtype=BF16, name="ffn_dpre")
    g["w_ff2"] = _mm(s["act"], dx2, ta=True, tk=1024, name="ffn_dw2")
    g["w_ff1"] = _mm(s["h2"], dpre, ta=True, tk=1024, name="ffn_dw1")
    dh2 = _mm(dpre, wb["w_ff1"], tb=True, name="ffn_dh")
    dx1, g["norm_ffn"] = _rms_bwd_call(s["x1"], p["norm_ffn"], dh2, add=dx2, name="rms_ffn_bwd")
    dmerged = _mm(dx1, wb["w_out"], tb=True, name="out_proj_da")
    g["w_out"] = _mm(s["merged"], dx1, ta=True, tk=1024, name="out_proj_dw")
    du, *dps = _merge_bwd_call(s["ps"], s["u"], dmerged)
    dys = []
    for y, dp, n in zip(s["ys"], dps, BRANCHES):
        dys.append(_mm(dp, wb[n], tb=True, name="branch_da"))
        g[n] = _mm(y, dp, ta=True, tk=1024, name="branch_dw")
    dya, dyb, dyc = dys
    tmat, uu, ww, gc, beta, qd, kd, attn, vn, st = s["delta"]
    du, do, g["c_norm"] = _gated_norm_bwd_call(s["o"], s["u"], p["c_norm"], dyc, du)
    ddu, ddw, dqd, dkd, dattn, dgl = _delta_scan_bwd_call(do, ww, qd, kd, attn, gc, vn, st)
    du, dqk, dv, dpar = _delta_prep_bwd_call(s["qk"], s["v"], s["u"], p["a_log"], p["dt_bias"], tmat, uu, ww, gc,
                                             beta, ddu, ddw, dqd, dkd, dattn, dgl, du)
    g["a_log"], g["dt_bias"] = dpar[0:1], dpar[1:2]
    dzc = _conv_prep_dz_call(s["zc"], dqk, dv)
    du, dconv = _conv_bwd_call(s["u"], dzc, p["conv_w"], du)
    g["conv_w"] = dconv[:C_CONV]
    no_dlse = jnp.zeros((t, LANES), F32)
    dq, dk, dv_b, dsink = _attn_bwd(ATTN_B_CFG, s["b_saved"], p["sinks"], dyb, no_dlse)
    g["sinks"] = dsink[0, :p["sinks"].shape[0]]
    du = _rope_scatter_call(du, t, [([dq], A_W, True, Q_SCALE)], COL["b_q"], tabs, name="rope_bq_bwd")
    du = _rope_scatter_call(du, t, [([dk], B_KVW, True, 1.0), ([dv_b], B_KVW, False, 1.0)], COL["b_k"], tabs,
                            name="rope_bkv_bwd")
    *dos, dl0, dl1, dl2 = _combine_bwd_call(s["os_"], s["lses"], dya)
    grads_a = [_attn_bwd(cfg, sv, None, do_c, dl)[:3]
               for cfg, sv, do_c, dl in zip(ATTN_A_CFGS, s["a_saved"], dos, (dl0, dl1, dl2))]
    dqs, dks, dvs = zip(*grads_a)
    du = _rope_scatter_call(du, t, [(list(dqs), A_W, True, Q_SCALE), (list(dks), A_W, True, 1.0),
                                    (list(dvs), A_W, False, 1.0)],
                            COL["a_q"], tabs, name="rope_a_bwd")
    dh = _mm(du, wb["w_in"], tb=True, tk=IN_TN, name="in_proj_da")
    g["w_in"], g["b_in"] = _mm(s["h"], du, ta=True, b_colsum=True, tn=IN_TN, tk=1024, name="in_proj_dw")
    dx, g["norm_mix"] = _rms_bwd_call(s["x"], p["norm_mix"], dh, add=dx1, name="rms_mix_bwd")
    return dx, g


def _local_step(x, params, w_in_first, rest_first, payload_of_layer, w_in_of, rest_of, tabs, tgt):
    saves = []
    w_in_blocks, rest = w_in_first, rest_first
    for layer in range(DEPTH):
        p = {n: w[layer] for n, w in params.items() if n != "norm_final"}
        nxt = payload_of_layer(layer + 1) if layer + 1 < DEPTH else None
        x, s, gathered = _layer_fwd(x, tabs, p, w_in_of(w_in_blocks), rest, rest_of, nxt)
        saves.append(s)
        if gathered is not None:
            w_in_blocks, rest = gathered[0], ("ready", gathered[1])
    loss, dx, dfinal = _loss_call(x, params["norm_final"], tgt)
    per_layer = []
    for s in reversed(saves):
        dx, g = _layer_bwd(s, dx, tabs)
        per_layer.append(g)
    per_layer.reverse()
    grads = {n: jnp.stack([g[n] for g in per_layer]) for n in per_layer[0]}
    grads["norm_final"] = dfinal
    return loss, dx, grads


def _in_cols_to_kernel(w):
    lead = w.shape[:-1]
    parts, pos = [], 0
    for _, start, width, ref_start in IN_LAYOUT:
        if start > pos:
            parts.append(jnp.zeros(lead + (start - pos,), w.dtype))
        parts.append(w[..., ref_start:ref_start + width])
        pos = start + width
    parts.append(jnp.zeros(lead + (D_IN_PAD - pos,), w.dtype))
    return jnp.concatenate(parts, axis=-1)


def _in_cols_to_reference(w):
    by_ref = sorted(IN_LAYOUT, key=lambda e: e[3])
    return jnp.concatenate([w[..., start:start + width] for _, start, width, _ in by_ref], axis=-1)


W_IN_SHARD = 8464 // N_DEV


def _w_in_from_shards(blocks):
    lead = blocks.shape[1:-1]
    parts, pos = [], 0
    for _, start, width, ref_start in IN_LAYOUT:
        if start > pos:
            parts.append(jnp.zeros(lead + (start - pos,), blocks.dtype))
        col = ref_start
        while col < ref_start + width:
            d, l = divmod(col, W_IN_SHARD)
            n = min(W_IN_SHARD - l, ref_start + width - col)
            parts.append(blocks[d, ..., l:l + n])
            col += n
        pos = start + width
    parts.append(jnp.zeros(lead + (D_IN_PAD - pos,), blocks.dtype))
    return jnp.concatenate(parts, axis=-1)


def _w_in_to_shards(g):
    by_ref = sorted(IN_LAYOUT, key=lambda e: e[3])
    blocks = []
    for d in range(N_DEV):
        lo, hi = d * W_IN_SHARD, (d + 1) * W_IN_SHARD
        parts = []
        for _, start, width, ref_start in by_ref:
            a, b = max(lo, ref_start), min(hi, ref_start + width)
            if a < b:
                parts.append(g[..., start + a - ref_start:start + b - ref_start])
        blocks.append(jnp.concatenate(parts, axis=-1))
    return jnp.stack(blocks)


def _pad_lanes(v):
    return jnp.pad(v, ((0, 0), (0, LANES - v.shape[1])))[:, None, :]


BIG = (("w_in", 2), ("conv_w", 2), ("w_branch_a", 2), ("w_branch_b", 2), ("w_branch_c", 1), ("w_out", 1),
       ("w_ff1", 2), ("w_ff2", 1))
SMALL = ("norm_mix", "b_in", "a_log", "dt_bias", "sinks", "c_norm", "norm_ffn", "norm_final")
WEIGHTS = ("norm_mix", "w_in", "b_in", "conv_w", "a_log", "dt_bias", "sinks", "c_norm", "w_branch_a",
           "w_branch_b", "w_branch_c", "w_out", "norm_ffn", "w_ff1", "w_ff2", "norm_final")
MATMUL_WEIGHTS = ("w_in", "w_branch_a", "w_branch_b", "w_branch_c", "w_out", "w_ff1", "w_ff2")
PACK_ROWS = 1024
ROW_ALIGN = 16


def _seg_rows(n):
    return -(-n // (LANES * ROW_ALIGN)) * ROW_ALIGN


def _pack(arrays, lead=0):
    parts = []
    for a in arrays:
        lead_shape = a.shape[:lead]
        n = math.prod(a.shape[lead:])
        rows = _seg_rows(n)
        if rows * LANES != n:
            a = jnp.pad(a.reshape(lead_shape + (n,)), [(0, 0)] * lead + [(0, rows * LANES - n)])
        parts.append(a.reshape(lead_shape + (rows, LANES)))
    total = sum(p.shape[lead] for p in parts)
    padded = -(-total // PACK_ROWS) * PACK_ROWS
    if padded > total:
        parts.append(jnp.zeros(parts[0].shape[:lead] + (padded - total, LANES), parts[0].dtype))
    return jnp.concatenate(parts, axis=lead)


def _unpack(buf, shapes):
    lead = buf.shape[:-2]
    out, pos = [], 0
    for shp in shapes:
        n = math.prod(shp)
        rows = _seg_rows(n)
        seg = buf[..., pos:pos + rows, :]
        if rows * LANES != n:
            seg = seg.reshape(lead + (rows * LANES,))[..., :n]
        out.append(seg.reshape(lead + tuple(shp)))
        pos += rows
    return out


def _shards_to_full(blocks, axis):
    moved = jnp.moveaxis(blocks, 0, axis)
    shp = list(blocks.shape[1:])
    shp[axis] = shp[axis] * N_DEV
    return moved.reshape(shp)


def _full_to_shards(full, axis):
    shp = list(full.shape)
    shp[axis:axis + 1] = [N_DEV, shp[axis] // N_DEV]
    return jnp.moveaxis(full.reshape(shp), axis, 0)


def _my_place():
    return lax.axis_index("x"), lax.axis_index("y"), lax.axis_index("c")


def _slot(x, y, c):
    return 4 * x + 2 * y + c


GATHER_COPIES = 7


def _gather_plan(x_ref, out_ref, send_sems, recv_sems, local_sem, base):
    x, y, c = _my_place()
    me, sibling = (x, y, c), (x, y, 1 - c)
    chips = [(1 - x, y), (x, 1 - y), (1 - x, 1 - y)]

    def copy(k, blk, to, src=None):
        dst = out_ref.at[_slot(*blk)]
        return pltpu.make_async_remote_copy(
            src_ref=dst if src is None else src, dst_ref=dst,
            send_sem=send_sems.at[base + k], recv_sem=recv_sems.at[base + k], device_id=to, device_id_type=MESH_ID)

    def own():
        mine = pltpu.make_async_copy(x_ref, out_ref.at[_slot(*me)], local_sem)
        return mine, [copy(0, me, sibling, src=x_ref)] + [copy(1 + j, me, (*chip, c), src=x_ref)
                                                          for j, chip in enumerate(chips)]

    return copy, own, me, sibling, chips, c


def _gather_plans(srcs, outs, send_sems, recv_sems, local_sems):
    return [_gather_plan(x_ref, out_ref, send_sems, recv_sems, local_sems.at[i], GATHER_COPIES * i)
            for i, (x_ref, out_ref) in enumerate(zip(srcs, outs))]


def _gather_start(srcs, outs, *sems):
    for _, own, *_ in _gather_plans(srcs, outs, *sems):
        mine, first = own()
        mine.start()
        for cp in first:
            cp.start()


def _gather_finish(srcs, outs, *sems):
    plans = _gather_plans(srcs, outs, *sems)
    passed_all = []
    for copy, own, me, sibling, chips, c in plans:
        passed = [copy(4 + j, (*chip, c), sibling) for j, chip in enumerate(chips)]
        for j, chip in enumerate(chips):
            copy(1 + j, (*chip, c), me).wait_recv()
            passed[j].start()
        passed_all.append(passed)
    for (copy, own, me, sibling, chips, c), passed in zip(plans, passed_all):
        copy(0, sibling, me).wait_recv()
        for j, chip in enumerate(chips):
            copy(4 + j, (*chip, 1 - c), me).wait_recv()
        mine, first = own()
        for cp in first + passed:
            cp.wait_send()
        mine.wait()


def _gather_sems(n):
    return [pltpu.SemaphoreType.DMA((GATHER_COPIES * n,)), pltpu.SemaphoreType.DMA((GATHER_COPIES * n,)),
            pltpu.SemaphoreType.DMA((n,))]


def _gathered_shapes(blocks):
    return [jax.ShapeDtypeStruct((N_DEV,) + b.shape, b.dtype) for b in blocks]


def _all_gather(blocks, *, name):
    n = len(blocks)

    def body(*refs):
        srcs, outs, sems = refs[:n], refs[n:2 * n], refs[2 * n:]
        _gather_start(srcs, outs, *sems)
        _gather_finish(srcs, outs, *sems)

    return pl.pallas_call(
        body, name=name, out_shape=_gathered_shapes(blocks),
        in_specs=[HBM_SPEC] * n, out_specs=[HBM_SPEC] * n,
        scratch_shapes=_gather_sems(n),
    )(*blocks)


N_CHIP = N_DEV // 2


def _swap_with_sibling(blocks, *, name):
    n = len(blocks)

    def body(*refs):
        srcs, outs, send_sems, recv_sems = refs[:n], refs[n:2 * n], refs[2 * n], refs[2 * n + 1]
        x, y, c = _my_place()
        copies = [pltpu.make_async_remote_copy(src_ref=g_ref, dst_ref=out_ref, send_sem=send_sems.at[i],
                                               recv_sem=recv_sems.at[i], device_id=(x, y, 1 - c),
                                               device_id_type=MESH_ID)
                  for i, (g_ref, out_ref) in enumerate(zip(srcs, outs))]
        for cp in copies:
            cp.start()
        for cp in copies:
            cp.wait_recv()
        for cp in copies:
            cp.wait_send()

    return pl.pallas_call(
        body, name=name,
        out_shape=[jax.ShapeDtypeStruct(b.shape, b.dtype) for b in blocks],
        in_specs=[HBM_SPEC] * n, out_specs=[HBM_SPEC] * n,
        scratch_shapes=[pltpu.SemaphoreType.DMA((n,)), pltpu.SemaphoreType.DMA((n,))],
    )(*blocks)


def _chip_all_to_all(blocks, *, name):
    n = len(blocks)
    peers = N_CHIP - 1

    def body(*refs):
        srcs, outs = refs[:n], refs[n:2 * n]
        send_sems, recv_sems, local_sems = refs[2 * n:]
        x, y, c = _my_place()
        mine_slot = 2 * x + y
        locals_, copies = [], []
        for i, (g_ref, out_ref) in enumerate(zip(srcs, outs)):
            locals_.append(pltpu.make_async_copy(g_ref.at[mine_slot], out_ref.at[mine_slot], local_sems.at[i]))
            for k in range(1, N_CHIP):
                px, py = x ^ (k >> 1), y ^ (k & 1)
                copies.append(pltpu.make_async_remote_copy(
                    src_ref=g_ref.at[2 * px + py], dst_ref=out_ref.at[mine_slot],
                    send_sem=send_sems.at[peers * i + k - 1], recv_sem=recv_sems.at[peers * i + k - 1],
                    device_id=(px, py, c), device_id_type=MESH_ID))
        for cp in locals_ + copies:
            cp.start()
        for cp in copies:
            cp.wait_recv()
        for cp in copies:
            cp.wait_send()
        for cp in locals_:
            cp.wait()

    return pl.pallas_call(
        body, name=name,
        out_shape=[jax.ShapeDtypeStruct(b.shape, b.dtype) for b in blocks],
        in_specs=[HBM_SPEC] * n, out_specs=[HBM_SPEC] * n,
        scratch_shapes=[pltpu.SemaphoreType.DMA((peers * n,)), pltpu.SemaphoreType.DMA((peers * n,)),
                        pltpu.SemaphoreType.DMA((n,))],
    )(*blocks)


def _block_rows(rows, cols):
    tr = max(8, min(rows, PACK_ROWS * LANES // (-(-cols // LANES) * LANES) // 8 * 8))
    while rows % tr:
        tr -= 8
    return tr


def _add_bf16_call(a, b, *, name):
    n, rows, cols = a.shape
    tr = _block_rows(rows, cols)

    def body(a_ref, b_ref, o_ref):
        o_ref[...] = (a_ref[...].astype(F32) + b_ref[...].astype(F32)).astype(BF16)

    blk = pl.BlockSpec((n, tr, cols), lambda i: (0, i, 0))
    return pl.pallas_call(
        body, name=name, grid=(rows // tr,), in_specs=[blk, blk], out_specs=blk,
        out_shape=jax.ShapeDtypeStruct(a.shape, BF16),
        compiler_params=_cparams(("parallel",)),
    )(a, b)


def _adamw_call(parts, w, m, v, *, name):
    rows, cols = w.shape
    tr = _block_rows(rows, cols)
    n_parts = parts.shape[0]

    def body(p_ref, w_ref, m_ref, v_ref, g_ref, d_ref, nm_ref, nv_ref):
        g = p_ref[0].astype(F32)
        for s in range(1, n_parts):
            g = g + p_ref[s].astype(F32)
        nm = ADAM_B1 * m_ref[...] + (1.0 - ADAM_B1) * g
        nv = ADAM_B2 * v_ref[...] + (1.0 - ADAM_B2) * jnp.square(g)
        m_hat = nm / (1.0 - ADAM_B1 ** ADAM_STEP)
        v_hat = nv / (1.0 - ADAM_B2 ** ADAM_STEP)
        g_ref[...] = g
        nm_ref[...] = nm
        nv_ref[...] = nv
        d_ref[...] = -ADAM_LR * (m_hat / (jnp.sqrt(v_hat) + ADAM_EPS) + ADAM_WD * w_ref[...])

    blk = pl.BlockSpec((tr, cols), lambda i: (i, 0))
    shape = jax.ShapeDtypeStruct((rows, cols), F32)
    return pl.pallas_call(
        body, name=name, grid=(rows // tr,),
        in_specs=[pl.BlockSpec((n_parts, tr, cols), lambda i: (0, i, 0)), blk, blk, blk],
        out_specs=[blk] * 4, out_shape=[shape] * 4,
        compiler_params=_cparams(("parallel",)),
    )(parts, w, m, v)


def _kernel_params(full):
    return {
        "norm_mix": full["norm_mix"][:, None, :],
        "b_in": _in_cols_to_kernel(full["b_in"])[:, None, :],
        "a_log": _pad_lanes(full["a_log"]),
        "dt_bias": _pad_lanes(full["dt_bias"]),
        "sinks": full["sinks"],
        "c_norm": full["c_norm"][:, None, :],
        "norm_ffn": full["norm_ffn"][:, None, :],
        "norm_final": full["norm_final"][None, :],
    }


def _reference_grads(g):
    return {
        "norm_mix": g["norm_mix"][:, 0, :],
        "b_in": _in_cols_to_reference(g["b_in"][:, 0, :]),
        "conv_w": g["conv_w"],
        "a_log": g["a_log"][:, 0, :C_V_HEADS],
        "dt_bias": g["dt_bias"][:, 0, :C_V_HEADS],
        "sinks": g["sinks"],
        "c_norm": g["c_norm"][:, 0, :],
        "w_branch_a": g["w_branch_a"], "w_branch_b": g["w_branch_b"], "w_branch_c": g["w_branch_c"],
        "w_out": g["w_out"],
        "norm_ffn": g["norm_ffn"][:, 0, :],
        "w_ff1": g["w_ff1"], "w_ff2": g["w_ff2"],
        "norm_final": g["norm_final"][0],
    }


def kernel(x, positions, norm_mix, w_in, b_in, conv_w, a_log, dt_bias, sinks, c_norm, w_branch_a, w_branch_b, w_branch_c, w_out, norm_ffn, w_ff1, w_ff2, norm_final, loss_target, m_norm_mix, m_w_in, m_b_in, m_conv_w, m_a_log, m_dt_bias, m_sinks, m_c_norm, m_w_branch_a, m_w_branch_b, m_w_branch_c, m_w_out, m_norm_ffn, m_w_ff1, m_w_ff2, m_norm_final, v_norm_mix, v_w_in, v_b_in, v_conv_w, v_a_log, v_dt_bias, v_sinks, v_c_norm, v_w_branch_a, v_w_branch_b, v_w_branch_c, v_w_out, v_norm_ffn, v_w_ff1, v_w_ff2, v_norm_final):
    env = dict(locals())
    weights = {n: env[n] for n in WEIGHTS}
    moments_m = {n: env["m_" + n] for n in WEIGHTS}
    moments_v = {n: env["v_" + n] for n in WEIGHTS}

    axis_of = {n: axis - 1 for n, axis in BIG}

    packed_names = [n for n in MATMUL_WEIGHTS if n != "w_in"]

    def payload_of_layer(layer):
        cw = weights["conv_w"][layer]
        c1 = cw.astype(BF16)
        c2 = (cw - c1.astype(F32)).astype(BF16)
        c3 = (cw - c1.astype(F32) - c2.astype(F32)).astype(BF16)
        return [weights["w_in"][layer].astype(BF16),
                _pack([weights[n][layer].astype(BF16) for n in packed_names] + [c1, c2, c3])]

    def rest_of(packed):
        shapes = [weights[n].shape[1:] for n in packed_names] + [weights["conv_w"].shape[1:]] * 3
        blocks = _unpack(packed, shapes)
        wb = {n: _shards_to_full(blk, axis_of[n]) for n, blk in zip(packed_names, blocks)}
        return wb, _shards_to_full(sum(b.astype(F32) for b in blocks[-3:]), axis_of["conv_w"])

    tabs = rope_tables(positions[0])
    w_in_first, = _all_gather(payload_of_layer(0)[:1], name="gather_weights")
    loss, dx, dparams = _local_step(x[0], _kernel_params({n: weights[n] for n in SMALL}), w_in_first,
                                    ("ride", payload_of_layer(0)[1]), payload_of_layer, _w_in_from_shards,
                                    rest_of, tabs, loss_target[0])
    grads = _reference_grads(dparams)
    loss = lax.psum(loss, ("x", "y", "c"))

    core = lax.axis_index("c")
    rest = [(n, axis) for n, axis in BIG if n != "w_in"]
    w_in_rows = DEPTH * D_MODEL

    def by_core(shards, which):
        sh = shards.reshape((N_CHIP, 2) + shards.shape[1:])
        return lax.dynamic_index_in_dim(sh, which, axis=1, keepdims=False).astype(BF16)

    def halves(which):
        w_in_half = by_core(_w_in_to_shards(dparams["w_in"]), which).reshape(N_CHIP, w_in_rows, W_IN_SHARD)
        return [w_in_half, _pack([by_core(_full_to_shards(grads[n], axis), which) for n, axis in rest], lead=1)]

    from_sibling = _swap_with_sibling(halves(1 - core), name="scatter_grads_d2d")
    chip_sums = [_add_bf16_call(keep, got, name="scatter_grads_add")
                 for keep, got in zip(halves(core), from_sibling)]
    w_in_parts, rest_parts = _chip_all_to_all(chip_sums, name="scatter_grads_ici")
    small_parts, = _all_gather([_pack([grads[n] for n in SMALL])], name="gather_small_grads")

    out = {}
    results = _adamw_call(w_in_parts, *[d["w_in"].reshape(w_in_rows, W_IN_SHARD)
                                        for d in (weights, moments_m, moments_v)], name="adamw_w_in")
    for kind, buf in zip(("grad", "delta", "new_m", "new_v"), results):
        out[kind, "w_in"] = buf.reshape(weights["w_in"].shape)
    for names, parts in (([n for n, _ in rest], rest_parts), (list(SMALL), small_parts)):
        shapes = [weights[n].shape for n in names]
        packed = [_pack([d[n] for n in names]) for d in (weights, moments_m, moments_v)]
        results = _adamw_call(parts, *packed, name="adamw_" + names[0])
        for kind, buf in zip(("grad", "delta", "new_m", "new_v"), results):
            for n, arr in zip(names, _unpack(buf, shapes)):
                out[kind, n] = arr
    return (loss, dx[None], *[out[kind, n] for kind in ("grad", "delta", "new_m", "new_v") for n in WEIGHTS])
```

```python
import functools
import math

import jax
import jax.numpy as jnp
from jax import lax
from jax.experimental import pallas as pl
from jax.experimental.pallas import tpu as pltpu

F32 = jnp.float32
BF16 = jnp.bfloat16

N_DEV = 8
D_MODEL = 1024
DEPTH = 2
HEAD_DIM = 64
ROT_DIM = 16
ROPE_THETA = 500000.0
BLK = 128
NEG_INF = -1e30
EPS = 1e-6
A_CONFIGS = ((128, 1), (512, 4), (2048, 16))
B_GROUP = 4
C_QK_HEADS = 4
C_V_HEADS = 8
C_DK = 128
C_CONV = 4
CHUNK = 64
ADAM_LR = 0.001
ADAM_B1 = 0.9
ADAM_B2 = 0.999
ADAM_EPS = 1e-08
ADAM_WD = 0.01
ADAM_STEP = 10

IN_LAYOUT = (
    ("gate_a", 0, 1024, 5392), ("gate_b", 1024, 1024, 6416), ("gate_c", 2048, 1024, 7440),
    ("a_q", 3072, 512, 0), ("a_k", 3584, 512, 512), ("a_v", 4096, 512, 1024), ("b_q", 4608, 512, 1536),
    ("c_z", 5120, 1024, 4352), ("c_qkv", 6144, 2048, 2304),
    ("b_k", 8192, 128, 2048), ("b_v", 8320, 128, 2176), ("c_ab", 8448, 16, 5376),
)
COL = {name: start for name, start, _, _ in IN_LAYOUT}
D_IN_PAD = 8704
IN_TN = D_IN_PAD // 4
LANES = 128
VMEM_LIMIT = 56 * 1024 * 1024


def _cparams(sem=None):
    return pltpu.CompilerParams(dimension_semantics=sem, vmem_limit_bytes=VMEM_LIMIT)


def _relu2(t):
    return jnp.square(jnp.maximum(t, 0.0))


def _mm(a, b, *, ta=False, tb=False, bias=None, a_fn=None, mul_drelu2=None, add=None,
        out_dtype=F32, relu2_out=False, b_colsum=False, gather_src=None, tm=1024, tn=1024, tk=2048, name):
    if ta:
        kdim, m = a.shape
    else:
        m, kdim = a.shape
    n = b.shape[0] if tb else b.shape[1]
    tm, tn, tk = min(tm, m), min(tn, n), min(tk, kdim)
    assert m % tm == 0 and n % tn == 0 and kdim % tk == 0, (a.shape, b.shape, tm, tn, tk)
    nk = kdim // tk
    assert not b_colsum or (m == tm and not tb and nk > 1)
    dims = (((0 if ta else 1,), (1 if tb else 0,)), ((), ()))
    extras = [e for e in (bias, mul_drelu2, add) if e is not None]
    ng = len(gather_src) if gather_src is not None else 0
    grid = (m // tm, n // tn, nk)

    def body(*refs):
        if ng:
            n_in = 2 + len(extras)
            n_out = 1 + int(relu2_out) + int(b_colsum)
            n_scr = int(nk > 1) + int(b_colsum)
            gather_refs = (refs[n_in:n_in + ng], refs[n_in + ng + n_out:n_in + 2 * ng + n_out],
                           *refs[n_in + 2 * ng + n_out + n_scr:])
            refs = refs[:n_in] + refs[n_in + ng:n_in + ng + n_out] + refs[n_in + 2 * ng + n_out:]
            at_first = functools.reduce(jnp.logical_and, [pl.program_id(d) == 0 for d in range(3)])
            at_last = functools.reduce(jnp.logical_and, [pl.program_id(d) == grid[d] - 1 for d in range(3)])

            @pl.when(at_first)
            def _():
                _gather_start(*gather_refs)
        a_ref, b_ref = refs[0], refs[1]
        pos = 2
        bias_ref = pre_ref = add_ref = None
        if bias is not None:
            bias_ref = refs[pos]; pos += 1
        if mul_drelu2 is not None:
            pre_ref = refs[pos]; pos += 1
        if add is not None:
            add_ref = refs[pos]; pos += 1
        o_ref = refs[pos]
        pos += 1
        r_ref = None
        if relu2_out:
            r_ref = refs[pos]; pos += 1
        cs_ref = None
        if b_colsum:
            cs_ref = refs[pos]; pos += 1
        acc_ref = refs[pos] if nk > 1 else None
        cs_acc = refs[pos + 1] if b_colsum else None

        av = a_ref[...]
        if a_fn is not None:
            av = a_fn(av)
        bv = b_ref[...]
        part = lax.dot_general(av.astype(BF16), bv.astype(BF16), dims,
                               preferred_element_type=F32)
        if b_colsum:
            cs_part = jnp.sum(bv.astype(F32).reshape(tk // 8, 8, tn), axis=0)

        def finish(acc):
            if bias_ref is not None:
                acc = acc + bias_ref[...]
            if pre_ref is not None:
                acc = acc * (2.0 * jnp.maximum(pre_ref[...], 0.0))
            if add_ref is not None:
                acc = acc + add_ref[...]
            o_ref[...] = acc.astype(out_dtype)
            if r_ref is not None:
                r_ref[...] = _relu2(acc).astype(BF16)

        if nk == 1:
            finish(part)
        else:
            k = pl.program_id(2)

            @pl.when(k == 0)
            def _():
                acc_ref[...] = part
                if b_colsum:
                    cs_acc[...] = cs_part

            @pl.when(k > 0)
            def _():
                acc_ref[...] += part
                if b_colsum:
                    cs_acc[...] += cs_part

            @pl.when(k == nk - 1)
            def _():
                finish(acc_ref[...])
                if b_colsum:
                    cs_ref[...] = jnp.sum(cs_acc[...], axis=0, keepdims=True)
        if ng:
            @pl.when(at_last)
            def _():
                _gather_finish(*gather_refs)

    a_spec = (pl.BlockSpec((tk, tm), lambda i, j, k: (k, i)) if ta
              else pl.BlockSpec((tm, tk), lambda i, j, k: (i, k)))
    b_spec = (pl.BlockSpec((tn, tk), lambda i, j, k: (j, k)) if tb
              else pl.BlockSpec((tk, tn), lambda i, j, k: (k, j)))
    in_specs = [a_spec, b_spec]
    if bias is not None:
        in_specs.append(pl.BlockSpec((1, tn), lambda i, j, k: (0, j)))
    for _ in extras[(1 if bias is not None else 0):]:
        in_specs.append(pl.BlockSpec((tm, tn), lambda i, j, k: (i, j)))
    o_spec = pl.BlockSpec((tm, tn), lambda i, j, k: (i, j))
    out_specs, out_shape = [o_spec], [jax.ShapeDtypeStruct((m, n), out_dtype)]
    scratch = [pltpu.VMEM((tm, tn), F32)] if nk > 1 else []
    if relu2_out:
        out_specs.append(o_spec)
        out_shape.append(jax.ShapeDtypeStruct((m, n), BF16))
    if b_colsum:
        out_specs.append(pl.BlockSpec((1, tn), lambda i, j, k: (0, j)))
        out_shape.append(jax.ShapeDtypeStruct((1, n), F32))
        scratch.append(pltpu.VMEM((8, tn), F32))
    if ng:
        in_specs += [HBM_SPEC] * ng
        out_specs += [HBM_SPEC] * ng
        out_shape += _gathered_shapes(gather_src)
        scratch += _gather_sems(ng)
    single = len(out_specs) == 1
    outs = pl.pallas_call(
        body, name=name,
        grid=grid,
        in_specs=in_specs,
        out_specs=out_specs[0] if single else out_specs,
        out_shape=out_shape[0] if single else out_shape,
        scratch_shapes=scratch,
        compiler_params=_cparams(("arbitrary",) * 3 if ng else ("parallel", "parallel", "arbitrary")),
    )(a, b, *extras, *(gather_src or []))
    if not ng:
        return outs
    n_out = len(outs) - ng
    return (outs[0] if n_out == 1 else tuple(outs[:n_out])), list(outs[n_out:])


def _rms_fwd_call(x, g, *, name, out_dtype=F32, tq=512):
    t, d = x.shape

    def body(x_ref, g_ref, y_ref):
        xv = x_ref[...]
        r = lax.rsqrt(jnp.mean(xv * xv, axis=-1, keepdims=True) + EPS)
        y_ref[...] = (xv * r * g_ref[...]).astype(out_dtype)

    return pl.pallas_call(
        body, name=name, grid=(t // tq,),
        in_specs=[pl.BlockSpec((tq, d), lambda i: (i, 0)), pl.BlockSpec((1, d), lambda i: (0, 0))],
        out_specs=pl.BlockSpec((tq, d), lambda i: (i, 0)),
        out_shape=jax.ShapeDtypeStruct((t, d), out_dtype),
        compiler_params=_cparams(("parallel",)),
    )(x, g)


def _rms_bwd_call(x, g, dy, *, name, add=None, tq=512):
    t, d = x.shape
    nt = t // tq

    def body(*refs):
        if add is None:
            x_ref, g_ref, dy_ref, dx_ref, dg_ref, acc_ref = refs
        else:
            x_ref, g_ref, dy_ref, add_ref, dx_ref, dg_ref, acc_ref = refs
        i = pl.program_id(0)
        xv = x_ref[...]
        r = lax.rsqrt(jnp.mean(xv * xv, axis=-1, keepdims=True) + EPS)
        xh = xv * r
        dyv = dy_ref[...]
        dxh = dyv * g_ref[...]
        dx = r * (dxh - xh * jnp.mean(dxh * xh, axis=-1, keepdims=True))
        dx_ref[...] = dx if add is None else dx + add_ref[...]
        part = jnp.sum((dyv * xh).reshape(tq // 8, 8, d), axis=0)

        @pl.when(i == 0)
        def _():
            acc_ref[...] = part

        @pl.when(i > 0)
        def _():
            acc_ref[...] += part

        @pl.when(i == nt - 1)
        def _():
            dg_ref[...] = jnp.sum(acc_ref[...], axis=0, keepdims=True)

    blk = pl.BlockSpec((tq, d), lambda i: (i, 0))
    row = pl.BlockSpec((1, d), lambda i: (0, 0))
    extra = [] if add is None else [add]
    return pl.pallas_call(
        body, name=name, grid=(nt,),
        in_specs=[blk, row, blk] + [blk] * len(extra),
        out_specs=[blk, row],
        out_shape=[jax.ShapeDtypeStruct((t, d), F32), jax.ShapeDtypeStruct((1, d), F32)],
        scratch_shapes=[pltpu.VMEM((8, d), F32)],
        compiler_params=_cparams(("arbitrary",)),
    )(x, g, dy, *extra)


def _loss_call(x, g, tgt, *, tq=512):
    t, d = x.shape
    nt = t // tq

    def body(x_ref, g_ref, t_ref, loss_ref, dx_ref, dg_ref, acc_ref, sq_ref):
        i = pl.program_id(0)
        xv = x_ref[...]
        r = lax.rsqrt(jnp.mean(xv * xv, axis=-1, keepdims=True) + EPS)
        xh = xv * r
        gv = g_ref[...]
        err = xh * gv - t_ref[...]
        dyv = err * (1.0 / d)
        dxh = dyv * gv
        dx_ref[...] = r * (dxh - xh * jnp.mean(dxh * xh, axis=-1, keepdims=True))
        part = jnp.sum((dyv * xh).reshape(tq // 8, 8, d), axis=0)
        sq = jnp.sum((err * err).reshape(tq // 8, 8, d), axis=0)

        @pl.when(i == 0)
        def _():
            acc_ref[...] = part
            sq_ref[...] = sq

        @pl.when(i > 0)
        def _():
            acc_ref[...] += part
            sq_ref[...] += sq

        @pl.when(i == nt - 1)
        def _():
            dg_ref[...] = jnp.sum(acc_ref[...], axis=0, keepdims=True)
            tot = jnp.sum(jnp.sum(sq_ref[...], axis=0, keepdims=True), axis=1, keepdims=True)
            loss_ref[...] = jnp.broadcast_to(tot * (0.5 / d), (8, LANES))

    blk = pl.BlockSpec((tq, d), lambda i: (i, 0))
    row = pl.BlockSpec((1, d), lambda i: (0, 0))
    loss, dx, dg = pl.pallas_call(
        body, name="loss", grid=(nt,),
        in_specs=[blk, row, blk],
        out_specs=[pl.BlockSpec((8, LANES), lambda i: (0, 0)), blk, row],
        out_shape=[jax.ShapeDtypeStruct((8, LANES), F32), jax.ShapeDtypeStruct((t, d), F32),
                   jax.ShapeDtypeStruct((1, d), F32)],
        scratch_shapes=[pltpu.VMEM((8, d), F32), pltpu.VMEM((8, d), F32)],
        compiler_params=_cparams(("arbitrary",)),
    )(x, g, tgt)
    return loss[0, 0], dx, dg


MESH_ID = pl.DeviceIdType.MESH
HBM_SPEC = pl.BlockSpec(memory_space=pl.ANY)


def rope_tables(positions):
    half = ROT_DIM // 2
    inv_freq = jnp.power(ROPE_THETA, -jnp.arange(0, ROT_DIM, 2, dtype=F32) / ROT_DIM)
    in_head = jnp.arange(LANES) % HEAD_DIM
    rot = in_head < ROT_DIM
    freq = jnp.where(rot, inv_freq[in_head % half], 0.0)
    ang = positions.astype(F32)[:, None] * freq[None, :]
    cos, sin = jnp.cos(ang), jnp.sin(ang)
    b = jnp.where(jnp.logical_and(rot, in_head >= half)[None, :], sin, 0.0)
    c = jnp.where((in_head < half)[None, :], -sin, 0.0)
    return cos, b, c


def _rope_chunk(xs, a, b, c, transpose):
    half = ROT_DIM // 2
    if transpose:
        return xs * a + pltpu.roll(xs * b, LANES - half, 1) + pltpu.roll(xs * c, half, 1)
    return xs * a + pltpu.roll(xs, half, 1) * b + pltpu.roll(xs, LANES - half, 1) * c


def _rope_gather_call(u, tabs, parts, *, name, dils=(1,), tq=512):
    t = u.shape[0]
    total = sum(w for _, w, _, _ in parts)
    groups = total // LANES
    assert all(start % w == 0 for start, w, _, _ in parts)
    staged = any(d > 1 for d in dils)

    def body(a_ref, b_ref, c_ref, *refs):
        x_refs, o_refs = refs[:len(parts)], refs[len(parts):len(parts) + len(dils)]
        stage_refs = refs[len(parts) + len(dils):]
        a, b, c = a_ref[...], b_ref[...], c_ref[...]
        g = 0
        for x_ref, (_, w, roped, scale) in zip(x_refs, parts):
            for j in range(w // LANES):
                xs = x_ref[:, j * LANES:(j + 1) * LANES]
                val = _rope_chunk(xs, a, b, c, False) if roped else xs
                val = val * scale if scale != 1.0 else val
                if staged:
                    stage_refs[g][...] = val
                for o_ref, d in zip(o_refs, dils):
                    if d == 1:
                        o_ref[:, g * LANES:(g + 1) * LANES] = val.astype(BF16)
                        continue
                    for r in range(d):
                        rows = stage_refs[g][pl.ds(r, tq // d, stride=d), :]
                        o_ref[:, r * total + g * LANES:r * total + (g + 1) * LANES] = rows.astype(BF16)
                g += 1

    tab_spec = pl.BlockSpec((tq, LANES), lambda i: (i, 0))
    return pl.pallas_call(
        body, name=name, grid=(t // tq,),
        in_specs=[tab_spec] * 3 + [pl.BlockSpec((tq, w), lambda i, cb=start // w: (i, cb)) for start, w, _, _ in parts],
        out_specs=[pl.BlockSpec((tq // d, d * total), lambda i: (i, 0)) for d in dils],
        out_shape=[jax.ShapeDtypeStruct((t // d, d * total), BF16) for d in dils],
        scratch_shapes=[pltpu.VMEM((tq, LANES), F32)] * groups if staged else [],
        compiler_params=_cparams(("parallel",)),
    )(*tabs, *([u] * len(parts)))


def _du_operands(du_buf, n_inputs):
    if du_buf is None:
        return [], [], {}
    return [du_buf], [HBM_SPEC], {n_inputs: 0}


def _du_shape(t):
    return jax.ShapeDtypeStruct((t, D_IN_PAD), BF16)


def _rope_scatter_call(du_buf, t, pieces, col, tabs, *, name, tq=512):
    total = sum(w for _, w, _, _ in pieces)
    assert col % total == 0
    arrays = [a for arrs, _, _, _ in pieces for a in arrs]
    extra, extra_specs, aliases = _du_operands(du_buf, 3 + len(arrays))

    def body(a_ref, b_ref, c_ref, *refs):
        o_ref = refs[len(arrays) + len(extra)]
        a, b, c = a_ref[...], b_ref[...], c_ref[...]
        k = off = 0
        for arrs, w, roped, scale in pieces:
            mine = refs[k:k + len(arrs)]
            k += len(arrs)
            for j in range(w // LANES):
                cs = slice(j * LANES, (j + 1) * LANES)
                xs = mine[0][:, cs].astype(F32)
                for r in mine[1:]:
                    xs = xs + r[:, cs].astype(F32)
                if scale != 1.0:
                    xs = xs * scale
                val = _rope_chunk(xs, a, b, c, True) if roped else xs
                o_ref[:, off + j * LANES:off + (j + 1) * LANES] = val.astype(BF16)
            off += w

    tab_spec = pl.BlockSpec((tq, LANES), lambda i: (i, 0))
    in_specs = [tab_spec] * 3 + [pl.BlockSpec((tq, w), lambda i: (i, 0)) for arrs, w, _, _ in pieces for _ in arrs]
    return pl.pallas_call(
        body, name=name, grid=(t // tq,),
        in_specs=in_specs + extra_specs,
        out_specs=pl.BlockSpec((tq, total), lambda i: (i, col // total)),
        out_shape=_du_shape(t), input_output_aliases=aliases,
        compiler_params=_cparams(("parallel",)),
    )(*tabs, *arrays, *extra)


def _band_masks(first_block, max_dist):
    qi = lax.broadcasted_iota(jnp.int32, (BLK, BLK), 0)
    kj = lax.broadcasted_iota(jnp.int32, (BLK, BLK), 1)
    valid_prev = jnp.logical_and(kj >= qi + (BLK - max_dist), jnp.logical_not(first_block))
    valid_cur = kj <= qi
    return valid_prev, valid_cur


_NN = (((1,), (0,)), ((), ()))
_NT = (((1,), (1,)), ((), ()))
_TN = (((0,), (0,)), ((), ()))


HEAD_STAGE = 8


def _attn_row_maps(nb):
    def cur(i):
        return jnp.minimum(i, nb - 1)

    def prev(i):
        return jnp.maximum(jnp.minimum(i, nb - 1) - 1, 0)

    return cur, prev


def _dil_view(a, dil):
    t, w = a.shape
    return a.reshape(t // dil, dil * w)


def _dil_spec(w, dil, rows, seg=None, off=0):
    seg = w if seg is None else seg
    assert off % w == 0 and (dil == 1 or seg % w == 0)
    return pl.BlockSpec((BLK, w), lambda r, i: (rows(i), (r * seg + off) // w))


def _dil_shape(l, dil, w, dtype=F32):
    return jax.ShapeDtypeStruct((l, dil * w), dtype)


def _attn_fwd_call(qkv2, sink, *, dil, group, max_dist, seg, offs, qw, kw, name):
    l = qkv2.shape[0]
    nh = qw // HEAD_DIM
    nb = l // BLK
    use_sink = sink is not None

    def body(*refs):
        if use_sink:
            sink_ref, refs = refs[0], refs[1:]
        q_ref, kp_ref, kc_ref, vp_ref, vc_ref, o_ref, lse_ref = refs
        valid_prev, valid_cur = _band_masks(pl.program_id(1) == 0, max_dist)
        lane = lax.broadcasted_iota(jnp.int32, (BLK, LANES), 1)
        lse_tile = jnp.zeros((BLK, LANES), F32)

        def dot(a, b, dims=_NN):
            return lax.dot_general(a, b, dims, preferred_element_type=F32)

        for g0 in range(0, nh, HEAD_STAGE):
            heads = list(range(g0, min(g0 + HEAD_STAGE, nh)))
            kv = {}
            for kh in sorted({h // group for h in heads}):
                ks = slice(kh * HEAD_DIM, (kh + 1) * HEAD_DIM)
                kv[kh] = tuple(ref[:, ks].astype(BF16) for ref in (kp_ref, kc_ref, vp_ref, vc_ref))
            qs = [q_ref[:, h * HEAD_DIM:(h + 1) * HEAD_DIM].astype(BF16) for h in heads]
            sps = [jnp.where(valid_prev, dot(qh, kv[h // group][0], _NT), NEG_INF) for h, qh in zip(heads, qs)]
            scs = [jnp.where(valid_cur, dot(qh, kv[h // group][1], _NT), NEG_INF) for h, qh in zip(heads, qs)]
            ms = [jnp.maximum(jnp.max(sp, axis=1, keepdims=True), jnp.max(sc, axis=1, keepdims=True))
                  for sp, sc in zip(sps, scs)]
            if use_sink:
                ms = [jnp.maximum(m, sink_ref[h]) for h, m in zip(heads, ms)]
            pps = [jnp.exp(sp - m) for sp, m in zip(sps, ms)]
            pcs = [jnp.exp(sc - m) for sc, m in zip(scs, ms)]
            dens = [jnp.sum(pp, axis=1, keepdims=True) + jnp.sum(pc, axis=1, keepdims=True)
                    for pp, pc in zip(pps, pcs)]
            if use_sink:
                dens = [den + jnp.exp(sink_ref[h] - m) for h, den, m in zip(heads, dens, ms)]
            outs = [dot(pp.astype(BF16), kv[h // group][2]) + dot(pc.astype(BF16), kv[h // group][3])
                    for h, pp, pc in zip(heads, pps, pcs)]
            for h, o, den, m in zip(heads, outs, dens, ms):
                o_ref[:, h * HEAD_DIM:(h + 1) * HEAD_DIM] = o / den
                lse_tile = jnp.where(lane == h, m + jnp.log(den), lse_tile)
        lse_ref[...] = lse_tile

    cur, prev = _attn_row_maps(nb)
    o_spec, lse_spec = _dil_spec(qw, dil, cur), _dil_spec(LANES, dil, cur)
    in_specs = [_dil_spec(qw, dil, cur, seg, offs[0]),
                _dil_spec(kw, dil, prev, seg, offs[1]), _dil_spec(kw, dil, cur, seg, offs[1]),
                _dil_spec(kw, dil, prev, seg, offs[2]), _dil_spec(kw, dil, cur, seg, offs[2])]
    args = [qkv2] * 5
    if use_sink:
        in_specs = [pl.BlockSpec(memory_space=pltpu.SMEM)] + in_specs
        args = [sink] + args
    return pl.pallas_call(
        body, name=name, grid=(dil, nb),
        in_specs=in_specs,
        out_specs=[o_spec, lse_spec],
        out_shape=[_dil_shape(l, dil, qw), _dil_shape(l, dil, LANES)],
        compiler_params=_cparams(("parallel", "parallel")),
    )(*args)


def _attn_bwd_call(qkv2, sink, o2, lse2, do2, dlse2, *, dil, group, max_dist, seg, offs, qw, kw, name):
    l = qkv2.shape[0]
    nh = qw // HEAD_DIM
    nb = l // BLK
    use_sink = sink is not None

    def body(*refs):
        if use_sink:
            sink_ref, refs = refs[0], refs[1:]
        (q_ref, kp_ref, kc_ref, vp_ref, vc_ref, o_ref, lse_ref, do_ref, dlse_ref,
         dq_ref, dk_ref, dv_ref, dsink_ref, ck_ref, cv_ref) = refs
        step = pl.program_id(1)

        @pl.when(jnp.logical_and(pl.program_id(0) == 0, step == 0))
        def _():
            dsink_ref[...] = jnp.zeros_like(dsink_ref)

        @pl.when(step == 0)
        def _():
            ck_ref[...] = jnp.zeros_like(ck_ref)
            cv_ref[...] = jnp.zeros_like(cv_ref)

        def dot(a, b, dims=_NN):
            return lax.dot_general(a, b, dims, preferred_element_type=F32)

        @pl.when(step < nb)
        def _():
            valid_prev, valid_cur = _band_masks(step == 0, max_dist)
            row = lax.broadcasted_iota(jnp.int32, (8, LANES), 0)
            lanes8 = lax.broadcasted_iota(jnp.int32, (8, LANES), 1)
            ds_tile = jnp.zeros((8, LANES), F32)
            for g0 in range(0, nh, HEAD_STAGE):
                heads = list(range(g0, min(g0 + HEAD_STAGE, nh)))
                hss = [slice(h * HEAD_DIM, (h + 1) * HEAD_DIM) for h in heads]
                kv = {}
                for kh in sorted({h // group for h in heads}):
                    ks = slice(kh * HEAD_DIM, (kh + 1) * HEAD_DIM)
                    kv[kh] = tuple(ref[:, ks].astype(BF16) for ref in (kp_ref, kc_ref, vp_ref, vc_ref))
                qs = [q_ref[:, hs].astype(BF16) for hs in hss]
                dos = [do_ref[:, hs] for hs in hss]
                dobs = [d.astype(BF16) for d in dos]
                lses = [lse_ref[:, h:h + 1] for h in heads]
                sps = [dot(qh, kv[h // group][0], _NT) for h, qh in zip(heads, qs)]
                scs = [dot(qh, kv[h // group][1], _NT) for h, qh in zip(heads, qs)]
                dpps = [dot(dob, kv[h // group][2], _NT) for h, dob in zip(heads, dobs)]
                dpcs = [dot(dob, kv[h // group][3], _NT) for h, dob in zip(heads, dobs)]
                pps = [jnp.where(valid_prev, jnp.exp(jnp.where(valid_prev, sp, NEG_INF) - ls), 0.0)
                       for sp, ls in zip(sps, lses)]
                pcs = [jnp.where(valid_cur, jnp.exp(jnp.where(valid_cur, sc, NEG_INF) - ls), 0.0)
                       for sc, ls in zip(scs, lses)]
                deltas = [jnp.sum(d * o_ref[:, hs], axis=1, keepdims=True) for d, hs in zip(dos, hss)]
                corrs = [dlse_ref[:, h:h + 1] - dl for h, dl in zip(heads, deltas)]
                dsps = [(pp * (dp + c)).astype(BF16) for pp, dp, c in zip(pps, dpps, corrs)]
                dscs = [(pc * (dp + c)).astype(BF16) for pc, dp, c in zip(pcs, dpcs, corrs)]
                for h, hs, dsp, dsc in zip(heads, hss, dsps, dscs):
                    dq = dot(dsp, kv[h // group][0]) + dot(dsc, kv[h // group][1])
                    dq_ref[:, hs] = dq.astype(BF16)
                parts = [(dot(dsc, qh, _TN), dot(dsp, qh, _TN),
                          dot(pc.astype(BF16), dob, _TN), dot(pp.astype(BF16), dob, _TN))
                         for dsc, dsp, qh, pc, pp, dob in zip(dscs, dsps, qs, pcs, pps, dobs)]
                for kh in kv:
                    ks = slice(kh * HEAD_DIM, (kh + 1) * HEAD_DIM)
                    mine = [p for h, p in zip(heads, parts) if h // group == kh]
                    dkc, dkp, dvc, dvp = (sum(p[j] for p in mine[1:]) + mine[0][j] for j in range(4))
                    dk_ref[:, ks] = (ck_ref[:, ks] + dkp).astype(BF16)
                    dv_ref[:, ks] = (cv_ref[:, ks] + dvp).astype(BF16)
                    ck_ref[:, ks] = dkc
                    cv_ref[:, ks] = dvc
                if use_sink:
                    for h, ls, dl in zip(heads, lses, deltas):
                        val = -jnp.sum(jnp.exp(sink_ref[h] - ls) * dl, axis=0, keepdims=True)
                        ds_tile = jnp.where(jnp.logical_and(row == 0, lanes8 == h), val, ds_tile)
            if use_sink:
                dsink_ref[...] += ds_tile

        @pl.when(step == nb)
        def _():
            dk_ref[...] = ck_ref[...].astype(BF16)
            dv_ref[...] = cv_ref[...].astype(BF16)

    cur, prev = _attn_row_maps(nb)
    q_spec, lse_spec = _dil_spec(qw, dil, cur), _dil_spec(LANES, dil, cur)
    lag_spec = _dil_spec(kw, dil, lambda i: jnp.maximum(i - 1, 0))
    in_specs = [_dil_spec(qw, dil, cur, seg, offs[0]),
                _dil_spec(kw, dil, prev, seg, offs[1]), _dil_spec(kw, dil, cur, seg, offs[1]),
                _dil_spec(kw, dil, prev, seg, offs[2]), _dil_spec(kw, dil, cur, seg, offs[2]),
                q_spec, lse_spec, q_spec, lse_spec]
    args = [qkv2] * 5 + [o2, lse2, do2, dlse2]
    if use_sink:
        in_specs = [pl.BlockSpec(memory_space=pltpu.SMEM)] + in_specs
        args = [sink] + args
    kv_shape = _dil_shape(l, dil, kw, BF16)
    return pl.pallas_call(
        body, name=name, grid=(dil, nb + 1),
        in_specs=in_specs,
        out_specs=[q_spec, lag_spec, lag_spec, pl.BlockSpec((8, LANES), lambda r, i: (0, 0))],
        out_shape=[_dil_shape(l, dil, qw, BF16), kv_shape, kv_shape,
                   jax.ShapeDtypeStruct((8, LANES), F32)],
        scratch_shapes=[pltpu.VMEM((BLK, kw), F32), pltpu.VMEM((BLK, kw), F32)],
        compiler_params=_cparams(("arbitrary", "arbitrary")),
    )(*args)


def _attn_config(tag, dil, group, max_dist, seg, offs, qw, kw):
    return dict(name=tag, dil=dil, group=group, max_dist=max_dist, seg=seg, offs=offs, qw=qw, kw=kw)


A_W = 8 * HEAD_DIM
ATTN_A_CFGS = tuple(_attn_config("attn_a%d" % dil, dil, 1, window // dil, 3 * A_W, (0, A_W, 2 * A_W), A_W, A_W)
                    for window, dil in A_CONFIGS)
B_KVW = 2 * HEAD_DIM
ATTN_B_CFG = _attn_config("attn_b", 1, B_GROUP, BLK - 1, A_W + 2 * B_KVW, (0, A_W, A_W + B_KVW), A_W, B_KVW)


def _attn_fwd(cfg, qkv2, sink):
    t = qkv2.shape[0] * cfg["dil"]
    kw = {k: v for k, v in cfg.items() if k != "name"}
    o2, lse2 = _attn_fwd_call(qkv2, sink, name=cfg["name"] + "_fwd", **kw)
    return o2.reshape(t, cfg["qw"]), lse2.reshape(t, LANES), (qkv2, o2, lse2)


def _attn_bwd(cfg, saved, sink, do, dlse):
    qkv2, o2, lse2 = saved
    t = do.shape[0]
    kw = {k: v for k, v in cfg.items() if k != "name"}
    dq2, dk2, dv2, dsink = _attn_bwd_call(qkv2, sink, o2, lse2, _dil_view(do, cfg["dil"]),
                                          _dil_view(dlse, cfg["dil"]), name=cfg["name"] + "_bwd", **kw)
    return dq2.reshape(t, cfg["qw"]), dk2.reshape(t, cfg["kw"]), dv2.reshape(t, cfg["kw"]), dsink


def _head_expand():
    r = lax.broadcasted_iota(jnp.int32, (LANES, 8 * HEAD_DIM), 0)
    c = lax.broadcasted_iota(jnp.int32, (LANES, 8 * HEAD_DIM), 1)
    return (c // HEAD_DIM == r).astype(F32)


def _combine_weights(l0, l1, l2):
    m = jnp.maximum(jnp.maximum(l0, l1), l2)
    e0, e1, e2 = jnp.exp(l0 - m), jnp.exp(l1 - m), jnp.exp(l2 - m)
    inv = 1.0 / (e0 + e1 + e2)
    return e0 * inv, e1 * inv, e2 * inv


def _combine_fwd_call(os_, lses, *, tq=512):
    t, w = os_[0].shape

    def body(o0, o1, o2, l0, l1, l2, y_ref):
        ws = _combine_weights(l0[...], l1[...], l2[...])
        e = _head_expand()
        y = jnp.zeros((tq, w), F32)
        for o_ref, wt in zip((o0, o1, o2), ws):
            y = y + _dot_mask(e, wt, mask_left=False) * o_ref[...]
        y_ref[...] = y

    o_spec = pl.BlockSpec((tq, w), lambda i: (i, 0))
    l_spec = pl.BlockSpec((tq, LANES), lambda i: (i, 0))
    return pl.pallas_call(
        body, name="combine_fwd", grid=(t // tq,),
        in_specs=[o_spec] * 3 + [l_spec] * 3, out_specs=o_spec,
        out_shape=jax.ShapeDtypeStruct((t, w), F32),
        compiler_params=_cparams(("parallel",)),
    )(*os_, *lses)


def _combine_bwd_call(os_, lses, dy, *, tq=512):
    t, w = dy.shape

    def body(o0, o1, o2, l0, l1, l2, dy_ref, do0, do1, do2, dl0, dl1, dl2):
        ws = _combine_weights(l0[...], l1[...], l2[...])
        e = _head_expand()
        dyv = dy_ref[...]
        dws = []
        for o_ref, do_ref, wt in zip((o0, o1, o2), (do0, do1, do2), ws):
            do_ref[...] = _dot_mask(e, wt, mask_left=False) * dyv
            dws.append(_dot_mask(e, dyv * o_ref[...], _NT, mask_left=False))
        mean = ws[0] * dws[0] + ws[1] * dws[1] + ws[2] * dws[2]
        for dl_ref, wt, dw in zip((dl0, dl1, dl2), ws, dws):
            dl_ref[...] = wt * (dw - mean)

    o_spec = pl.BlockSpec((tq, w), lambda i: (i, 0))
    l_spec = pl.BlockSpec((tq, LANES), lambda i: (i, 0))
    o_shape = jax.ShapeDtypeStruct((t, w), F32)
    l_shape = jax.ShapeDtypeStruct((t, LANES), F32)
    return pl.pallas_call(
        body, name="combine_bwd", grid=(t // tq,),
        in_specs=[o_spec] * 3 + [l_spec] * 3 + [o_spec], out_specs=[o_spec] * 3 + [l_spec] * 3,
        out_shape=[o_shape] * 3 + [l_shape] * 3,
        compiler_params=_cparams(("parallel",)),
    )(*os_, *lses, dy)


C_QKW = C_QK_HEADS * C_DK
C_CONV_W = 2 * C_QKW + C_V_HEADS * C_DK
HALO = 8


def _silu_parts(z):
    sig = jax.nn.sigmoid(z)
    return z * sig, sig * (1.0 + z * (1.0 - sig))


def _conv_window_specs(tq, t):
    c = C_CONV_W
    cb = COL["c_qkv"] // c
    blk = pl.BlockSpec((tq, c), lambda i: (i, cb))
    before = pl.BlockSpec((HALO, c), lambda i: (jnp.maximum(i * (tq // HALO) - 1, 0), cb))
    return c, cb, blk, before


def _conv_prep_fwd_call(u, w, *, tq=512):
    t = u.shape[0]
    c, _, x_spec, halo_spec = _conv_window_specs(tq, t)
    nqk = 2 * C_QK_HEADS

    def body(x_ref, halo_ref, w_ref, z_ref, qk_ref, v_ref):
        i = pl.program_id(0)
        halo = jnp.where(i == 0, 0.0, halo_ref[...])
        xc = jnp.concatenate([halo, x_ref[...]], axis=0)
        wv = w_ref[...]
        z = xc[HALO - 3:HALO - 3 + tq] * wv[0:1]
        for j in range(1, C_CONV):
            z = z + xc[HALO - 3 + j:HALO - 3 + j + tq] * wv[j:j + 1]
        z_ref[...] = z
        act, _ = _silu_parts(z)
        for h in range(nqk):
            a = act[:, h * C_DK:(h + 1) * C_DK]
            qk_ref[:, h * C_DK:(h + 1) * C_DK] = a * lax.rsqrt(jnp.sum(a * a, axis=1, keepdims=True) + EPS)
        v_ref[...] = act[:, nqk * C_DK:]

    return pl.pallas_call(
        body, name="conv_prep_fwd", grid=(t // tq,),
        in_specs=[x_spec, halo_spec, pl.BlockSpec((C_CONV, c), lambda i: (0, 0))],
        out_specs=[pl.BlockSpec((tq, c), lambda i: (i, 0)),
                   pl.BlockSpec((tq, 2 * C_QKW), lambda i: (i, 0)),
                   pl.BlockSpec((tq, c - 2 * C_QKW), lambda i: (i, 0))],
        out_shape=[jax.ShapeDtypeStruct((t, c), F32), jax.ShapeDtypeStruct((t, 2 * C_QKW), F32),
                   jax.ShapeDtypeStruct((t, c - 2 * C_QKW), F32)],
        compiler_params=_cparams(("parallel",)),
    )(u, u, w)


def _conv_prep_dz_call(z, dqk, dv, *, tq=512):
    t, c = z.shape
    nqk = 2 * C_QK_HEADS

    def body(z_ref, dqk_ref, dv_ref, dz_ref):
        zv = z_ref[...]
        act, dact = _silu_parts(zv)
        for h in range(nqk):
            hs = slice(h * C_DK, (h + 1) * C_DK)
            a = act[:, hs]
            r = lax.rsqrt(jnp.sum(a * a, axis=1, keepdims=True) + EPS)
            nrm = a * r
            dn = dqk_ref[:, hs]
            da = r * (dn - nrm * jnp.sum(dn * nrm, axis=1, keepdims=True))
            dz_ref[:, hs] = da * dact[:, hs]
        dz_ref[:, nqk * C_DK:] = dv_ref[...] * dact[:, nqk * C_DK:]

    return pl.pallas_call(
        body, name="conv_prep_dz", grid=(t // tq,),
        in_specs=[pl.BlockSpec((tq, c), lambda i: (i, 0)),
                  pl.BlockSpec((tq, 2 * C_QKW), lambda i: (i, 0)),
                  pl.BlockSpec((tq, c - 2 * C_QKW), lambda i: (i, 0))],
        out_specs=pl.BlockSpec((tq, c), lambda i: (i, 0)),
        out_shape=jax.ShapeDtypeStruct((t, c), F32),
        compiler_params=_cparams(("parallel",)),
    )(z, dqk, dv)


def _conv_bwd_call(u, dz, w, du_buf, *, tq=512):
    t = u.shape[0]
    nt = t // tq
    c, cb, x_spec, halo_spec = _conv_window_specs(tq, t)
    extra, extra_specs, aliases = _du_operands(du_buf, 5)

    def body(x_ref, xh_ref, dz_ref, dzh_ref, w_ref, *refs):
        dx_ref, dw_ref = refs[len(extra):]
        i = pl.program_id(0)
        xc = jnp.concatenate([jnp.where(i == 0, 0.0, xh_ref[...]), x_ref[...]], axis=0)
        dzv = dz_ref[...]
        dzc = jnp.concatenate([dzv, jnp.where(i == nt - 1, 0.0, dzh_ref[...])], axis=0)
        wv = w_ref[...]
        dx = dzv * wv[3:4]
        for s in range(1, C_CONV):
            dx = dx + dzc[s:s + tq] * wv[3 - s:4 - s]
        dx_ref[...] = dx.astype(BF16)
        row = lax.broadcasted_iota(jnp.int32, (8, c), 0)
        dw = jnp.zeros((8, c), F32)
        for j in range(C_CONV):
            prod = dzv * xc[HALO - 3 + j:HALO - 3 + j + tq]
            col = jnp.sum(jnp.sum(prod.reshape(tq // 8, 8, c), axis=0), axis=0, keepdims=True)
            dw = jnp.where(row == j, col, dw)

        @pl.when(i == 0)
        def _():
            dw_ref[...] = dw

        @pl.when(i > 0)
        def _():
            dw_ref[...] += dw

    blk = pl.BlockSpec((tq, c), lambda i: (i, 0))
    after = pl.BlockSpec((HALO, c), lambda i: (jnp.minimum((i + 1) * (tq // HALO), t // HALO - 1), 0))
    return pl.pallas_call(
        body, name="conv_bwd", grid=(nt,),
        in_specs=[x_spec, halo_spec, blk, after, pl.BlockSpec((C_CONV, c), lambda i: (0, 0))] + extra_specs,
        out_specs=[pl.BlockSpec((tq, c), lambda i: (i, cb)), pl.BlockSpec((8, c), lambda i: (0, 0))],
        out_shape=[_du_shape(t), jax.ShapeDtypeStruct((8, c), F32)],
        input_output_aliases=aliases,
        compiler_params=_cparams(("arbitrary",)),
    )(u, u, dz, dz, w, *extra)


C_VW = C_V_HEADS * C_DK


def _softplus(x):
    return jnp.maximum(x, 0.0) + jnp.log(1.0 + jnp.exp(-jnp.abs(x)))


def _tri_masks():
    r = lax.broadcasted_iota(jnp.int32, (CHUNK, CHUNK), 0)
    c = lax.broadcasted_iota(jnp.int32, (CHUNK, CHUNK), 1)
    return r >= c, r > c


def _split_bf16(a):
    hi = a.astype(BF16)
    return hi, (a - hi.astype(F32)).astype(BF16)


def _dot_hi(a, b, dims=None):
    dims = _NN if dims is None else dims
    ah, al = _split_bf16(a)
    bh, bl = _split_bf16(b)

    def d(x, y):
        return lax.dot_general(x, y, dims, preferred_element_type=F32)

    return d(ah, bh) + (d(ah, bl) + d(al, bh))


def _dot_mask(mask, b, dims=None, mask_left=True):
    dims = _NN if dims is None else dims
    mb = mask.astype(BF16)
    b1 = b.astype(BF16)
    rest = b - b1.astype(F32)
    b2 = rest.astype(BF16)
    b3 = (rest - b2.astype(F32)).astype(BF16)
    out = None
    for p in (b1, b2, b3):
        term = (lax.dot_general(mb, p, dims, preferred_element_type=F32) if mask_left
                else lax.dot_general(p, mb, dims, preferred_element_type=F32))
        out = term if out is None else out + term
    return out


def _unit_lower_inverses(mats):
    r = lax.broadcasted_iota(jnp.int32, (CHUNK, CHUNK), 0)
    c = lax.broadcasted_iota(jnp.int32, (CHUNK, CHUNK), 1)
    eye = (r == c).astype(F32)
    xs = [eye - a for a in mats]
    ps = [_dot_hi(a, a) for a in mats]
    steps = int(math.log2(CHUNK)) - 1
    for s in range(steps):
        xs = [x + _dot_hi(x, p) for x, p in zip(xs, ps)]
        if s < steps - 1:
            ps = [_dot_hi(p, p) for p in ps]
    return xs


def _gate_tiles(cab, alog, dtb):
    pre = cab + dtb
    g = -jnp.exp(alog) * _softplus(pre)
    beta = jax.nn.sigmoid(pltpu.roll(cab, LANES - C_V_HEADS, 1))
    return g, beta, pre


def _chunk_common(kk, qk, gc, gct, beta, h, tri, strict):
    gcol, grow, bcol = gc[:, h:h + 1], gct[h:h + 1, :], beta[:, h:h + 1]
    decay = jnp.where(tri, jnp.exp(jnp.where(tri, gcol - grow, 0.0)), 0.0)
    kkd = jnp.where(strict, kk * decay, 0.0)
    attn = jnp.where(tri, qk * decay, 0.0)
    glast = gc[CHUNK - 1:CHUNK, h:h + 1]
    return gcol, bcol, decay, kkd, attn, glast


def _cab_spec():
    return pl.BlockSpec((CHUNK, LANES), lambda n: (n, COL["c_ab"] // LANES))


def _delta_prep_call(qk, v, u, alog, dtb, gather_src=None):
    t = qk.shape[0]
    nc = t // CHUNK
    scale = C_DK ** -0.5
    riding = gather_src is not None
    ng = len(gather_src) if riding else 0

    def body(q_ref, k_ref, v_ref, cab_ref, alog_ref, dtb_ref, *refs):
        if riding:
            gather_refs = (refs[:ng], refs[ng + 8:2 * ng + 8]) + tuple(refs[2 * ng + 8:])
            refs = refs[ng:ng + 8]

            @pl.when(pl.program_id(0) == 0)
            def _():
                _gather_start(*gather_refs)
        u_ref, w_ref, qd_ref, kd_ref, attn_ref, tmat_ref, gc_ref, beta_ref = refs
        tri, strict = _tri_masks()
        g, beta, _ = _gate_tiles(cab_ref[...], alog_ref[...], dtb_ref[...])
        gc = _dot_mask(tri, g)
        gct = gc.T
        gc_ref[...] = gc
        beta_ref[...] = beta
        mats, rhs = [], []
        for j in range(C_QK_HEADS):
            js = slice(j * C_DK, (j + 1) * C_DK)
            kf, qf = k_ref[:, js], q_ref[:, js] * scale
            kb, qb = kf.astype(BF16), qf.astype(BF16)
            kk = lax.dot_general(kb, kb, _NT, preferred_element_type=F32)
            qk = lax.dot_general(qb, kb, _NT, preferred_element_type=F32)
            for h in (2 * j, 2 * j + 1):
                hs = slice(h * C_DK, (h + 1) * C_DK)
                gcol, bcol, decay, kkd, attn, glast = _chunk_common(kk, qk, gc, gct, beta, h, tri, strict)
                gexp = jnp.exp(gcol)
                mats.append(kkd * bcol)
                rhs.append(jnp.concatenate([v_ref[:, hs] * bcol, kf * (bcol * gexp)], axis=1))
                qd_ref[:, hs] = (qf * gexp).astype(BF16)
                kd_ref[:, hs] = (kf * jnp.exp(glast - gcol)).astype(BF16)
                attn_ref[:, h * CHUNK:(h + 1) * CHUNK] = attn.astype(BF16)
        for h, (tmat, r) in enumerate(zip(_unit_lower_inverses(mats), rhs)):
            hs = slice(h * C_DK, (h + 1) * C_DK)
            uw = _dot_hi(tmat, r)
            u_ref[:, hs] = uw[:, :C_DK]
            w_ref[:, hs] = uw[:, C_DK:]
            tmat_ref[:, h * CHUNK:(h + 1) * CHUNK] = tmat
        if riding:
            @pl.when(pl.program_id(0) == nc - 1)
            def _():
                _gather_finish(*gather_refs)

    def blk(w):
        return pl.BlockSpec((CHUNK, w), lambda n: (n, 0))

    row = pl.BlockSpec((1, LANES), lambda n: (0, 0))
    big = jax.ShapeDtypeStruct((t, C_VW), F32)
    sq = jax.ShapeDtypeStruct((t, C_V_HEADS * CHUNK), F32)
    tile = jax.ShapeDtypeStruct((t, LANES), F32)
    half = jax.ShapeDtypeStruct((t, C_VW), BF16)
    in_specs = [blk(C_QKW), pl.BlockSpec((CHUNK, C_QKW), lambda n: (n, 1)), blk(C_VW), _cab_spec(), row, row]
    out_specs = [blk(C_VW)] * 4 + [blk(C_V_HEADS * CHUNK)] * 2 + [blk(LANES)] * 2
    out_shape = [big, big, half, half, jax.ShapeDtypeStruct(sq.shape, BF16), sq] + [tile] * 2
    args = [qk, qk, v, u, alog, dtb]
    if riding:
        in_specs += [HBM_SPEC] * ng
        out_specs += [HBM_SPEC] * ng
        out_shape += _gathered_shapes(gather_src)
        args += list(gather_src)
    return pl.pallas_call(
        body, name="delta_prep_gather" if riding else "delta_prep", grid=(nc,),
        in_specs=in_specs, out_specs=out_specs, out_shape=out_shape,
        scratch_shapes=_gather_sems(ng) if riding else [],
        compiler_params=_cparams(("arbitrary",) if riding else ("parallel",)),
    )(*args)


SCAN_SUB = 4


def _delta_scan_call(u, w, qd, kd, attn, gc):
    t = u.shape[0]
    nc = t // CHUNK
    rows = SCAN_SUB * CHUNK

    def body(u_ref, w_ref, qd_ref, kd_ref, attn_ref, gc_ref, o_ref, vn_ref, st_ref, s_ref):
        @pl.when(pl.program_id(0) == 0)
        def _():
            s_ref[...] = jnp.zeros_like(s_ref)

        hss = [slice(h * C_DK, (h + 1) * C_DK) for h in range(C_V_HEADS)]
        states = [s_ref[hs, :] for hs in hss]
        for c in range(SCAN_SUB):
            rs = slice(c * CHUNK, (c + 1) * CHUNK)
            for hs, s in zip(hss, states):
                st_ref[c, hs, :] = s
            sbs = [s.astype(BF16) for s in states]
            vns = [u_ref[rs, hs] - jnp.dot(w_ref[rs, hs].astype(BF16), sb, preferred_element_type=F32)
                   for hs, sb in zip(hss, sbs)]
            qss = [jnp.dot(qd_ref[rs, hs].astype(BF16), sb, preferred_element_type=F32) for hs, sb in zip(hss, sbs)]
            vnbs = [vn.astype(BF16) for vn in vns]
            for h, hs in enumerate(hss):
                vn_ref[rs, hs] = vnbs[h]
                o_ref[rs, hs] = qss[h] + jnp.dot(attn_ref[rs, h * CHUNK:(h + 1) * CHUNK].astype(BF16), vnbs[h],
                                                 preferred_element_type=F32)
            last = (c + 1) * CHUNK - 1
            states = [states[h] * jnp.exp(gc_ref[last:last + 1, h:h + 1])
                      + lax.dot_general(kd_ref[rs, hs].astype(BF16), vnbs[h], _TN, preferred_element_type=F32)
                      for h, hs in enumerate(hss)]
        for hs, s in zip(hss, states):
            s_ref[hs, :] = s

    def blk(wd):
        return pl.BlockSpec((rows, wd), lambda n: (n, 0))

    big = jax.ShapeDtypeStruct((t, C_VW), F32)
    return pl.pallas_call(
        body, name="delta_scan", grid=(nc // SCAN_SUB,),
        in_specs=[blk(C_VW)] * 4 + [blk(C_V_HEADS * CHUNK), blk(LANES)],
        out_specs=[blk(C_VW), blk(C_VW), pl.BlockSpec((SCAN_SUB, C_VW, C_DK), lambda n: (n, 0, 0))],
        out_shape=[big, jax.ShapeDtypeStruct((t, C_VW), BF16), jax.ShapeDtypeStruct((nc, C_VW, C_DK), F32)],
        scratch_shapes=[pltpu.VMEM((C_VW, C_DK), F32)],
        compiler_params=_cparams(("arbitrary",)),
    )(u, w, qd, kd, attn, gc)


def _delta_scan_bwd_call(do, w, qd, kd, attn, gc, vn, st):
    t = do.shape[0]
    nc = t // CHUNK
    rows = SCAN_SUB * CHUNK
    steps = nc // SCAN_SUB

    def body(do_ref, w_ref, qd_ref, kd_ref, attn_ref, gc_ref, vn_ref, st_ref,
             du_ref, dw_ref, dqd_ref, dkd_ref, dattn_ref, dgl_ref, ds_ref):
        @pl.when(pl.program_id(0) == 0)
        def _():
            ds_ref[...] = jnp.zeros_like(ds_ref)

        tri, _ = _tri_masks()
        row = lax.broadcasted_iota(jnp.int32, (8, LANES), 0)
        lane = lax.broadcasted_iota(jnp.int32, (8, LANES), 1)
        hss = [slice(h * C_DK, (h + 1) * C_DK) for h in range(C_V_HEADS)]
        css = [slice(h * CHUNK, (h + 1) * CHUNK) for h in range(C_V_HEADS)]

        def dg(a, b, dims):
            return lax.dot_general(a, b, dims, preferred_element_type=F32)

        dsps = [ds_ref[hs, :] for hs in hss]
        for c in reversed(range(SCAN_SUB)):
            rs = slice(c * CHUNK, (c + 1) * CHUNK)
            dgl = jnp.zeros((8, LANES), F32)
            ss = [st_ref[c, hs, :] for hs in hss]
            sbs = [s.astype(BF16) for s in ss]
            dspbs = [d.astype(BF16) for d in dsps]
            dobs = [do_ref[rs, hs].astype(BF16) for hs in hss]
            vnbs = [vn_ref[rs, hs].astype(BF16) for hs in hss]
            dvns = [dg(attn_ref[rs, cs].astype(BF16), dob, _TN) + dg(kd_ref[rs, hs].astype(BF16), dspb, _NN)
                    for hs, cs, dob, dspb in zip(hss, css, dobs, dspbs)]
            for h, hs in enumerate(hss):
                dqd_ref[rs, hs] = dg(dobs[h], sbs[h], _NT)
                dkd_ref[rs, hs] = dg(vnbs[h], dspbs[h], _NT)
                dattn_ref[rs, css[h]] = jnp.where(tri, dg(dobs[h], vnbs[h], _NT), 0.0)
            dvnbs = [d.astype(BF16) for d in dvns]
            for h, hs in enumerate(hss):
                du_ref[rs, hs] = dvns[h]
                dw_ref[rs, hs] = -dg(dvnbs[h], sbs[h], _NT)
                tot = jnp.sum(jnp.sum(dsps[h] * ss[h], axis=0, keepdims=True), axis=1, keepdims=True)
                dgl = jnp.where(jnp.logical_and(row == 0, lane == h), tot, dgl)
            dgl_ref[c * 8:(c + 1) * 8, :] = dgl
            last = (c + 1) * CHUNK - 1
            dsps = [dg(qd_ref[rs, hs].astype(BF16), dobs[h], _TN) + jnp.exp(gc_ref[last:last + 1, h:h + 1]) * dsps[h]
                    - dg(w_ref[rs, hs].astype(BF16), dvnbs[h], _TN) for h, hs in enumerate(hss)]
        for hs, d in zip(hss, dsps):
            ds_ref[hs, :] = d

    def blk(wd):
        return pl.BlockSpec((rows, wd), lambda n: (steps - 1 - n, 0))

    big = jax.ShapeDtypeStruct((t, C_VW), F32)
    return pl.pallas_call(
        body, name="delta_scan_bwd", grid=(steps,),
        in_specs=[blk(C_VW)] * 4 + [blk(C_V_HEADS * CHUNK), blk(LANES), blk(C_VW),
                                    pl.BlockSpec((SCAN_SUB, C_VW, C_DK), lambda n: (steps - 1 - n, 0, 0))],
        out_specs=[blk(C_VW)] * 4 + [blk(C_V_HEADS * CHUNK),
                                     pl.BlockSpec((SCAN_SUB * 8, LANES), lambda n: (steps - 1 - n, 0))],
        out_shape=[big] * 4 + [jax.ShapeDtypeStruct((t, C_V_HEADS * CHUNK), F32),
                               jax.ShapeDtypeStruct((nc * 8, LANES), F32)],
        scratch_shapes=[pltpu.VMEM((C_VW, C_DK), F32)],
        compiler_params=_cparams(("arbitrary",)),
    )(do, w, qd, kd, attn, gc, vn, st)


PREP_SUB = 2


def _delta_prep_bwd_call(qk, v, proj, alog, dtb, tmat, u, w, gc, beta, du, dw, dqd, dkd, dattn, dgl, du_buf):
    t = qk.shape[0]
    extra, extra_specs, aliases = _du_operands(du_buf, 17)
    nc = t // CHUNK
    rows = PREP_SUB * CHUNK
    scale = C_DK ** -0.5

    def body(q_ref, k_ref, v_ref, cab_ref, alog_ref, dtb_ref, tmat_ref, u_ref, w_ref, gc_ref, beta_ref,
             du_ref, dw_ref, dqd_ref, dkd_ref, dattn_ref, dgl_ref, *outs):
        dcab_ref, dqk_ref, dv_ref, dpar_ref = outs[len(extra):]
        tri, strict = _tri_masks()
        ones = jnp.ones((CHUNK, LANES), F32)
        lane = lax.broadcasted_iota(jnp.int32, (CHUNK, LANES), 1)
        rowi = lax.broadcasted_iota(jnp.int32, (CHUNK, 1), 0)
        subs = range(PREP_SUB)
        rss = [slice(c * CHUNK, (c + 1) * CHUNK) for c in subs]
        betas = [beta_ref[rs, :] for rs in rss]

        def dot(x, y, dims=_NN):
            return lax.dot_general(x, y, dims, preferred_element_type=F32)

        heads = []
        for c, rs in zip(subs, rss):
            gc = gc_ref[rs, :]
            gct = gc.T
            for j in range(C_QK_HEADS):
                js = slice(j * C_DK, (j + 1) * C_DK)
                kf, qf = k_ref[rs, js], q_ref[rs, js] * scale
                kb, qb = kf.astype(BF16), qf.astype(BF16)
                kk = dot(kb, kb, _NT)
                qk = dot(qb, kb, _NT)
                for h in (2 * j, 2 * j + 1):
                    heads.append((c, rs, h, kf, qf, kb, qb) + _chunk_common(kk, qk, gc, gct, betas[c], h, tri, strict))

        def cols(h):
            return slice(h * C_DK, (h + 1) * C_DK)

        def sq(h):
            return slice(h * CHUNK, (h + 1) * CHUNK)

        dvks = [_dot_hi(tmat_ref[hd[1], sq(hd[2])],
                        jnp.concatenate([du_ref[hd[1], cols(hd[2])], dw_ref[hd[1], cols(hd[2])]], axis=1), _TN)
                for hd in heads]
        das = [-jnp.where(strict, _dot_hi(dvk, jnp.concatenate([u_ref[hd[1], cols(hd[2])], w_ref[hd[1], cols(hd[2])]],
                                                               axis=1), _NT), 0.0)
               for hd, dvk in zip(heads, dvks)]
        pre = []
        for (c, rs, h, kf, qf, kb, qb, gcol, bcol, decay, kkd, attn, glast), da in zip(heads, das):
            dattn_h = dattn_ref[rs, sq(h)]
            pre.append(((da * decay * bcol).astype(BF16), (dattn_h * decay).astype(BF16),
                        da * kkd * bcol + dattn_h * attn))
        mms = [(dot(dkk, hd[5]), dot(dkk, hd[5], _TN), dot(dqk, hd[6], _TN), dot(dqk, hd[5]),
                _dot_mask(ones, e, _TN, mask_left=False))
               for hd, (dkk, dqk, e) in zip(heads, pre)]
        dq_parts, dk_parts = {}, {}
        dgc_tiles = [jnp.zeros((CHUNK, LANES), F32) for _ in subs]
        db_tiles = [jnp.zeros((CHUNK, LANES), F32) for _ in subs]
        for (c, rs, h, kf, qf, kb, qb, gcol, bcol, decay, kkd, attn, glast), dvk, da, (_, _, e), mm in zip(
                heads, dvks, das, pre, mms):
            hs = cols(h)
            gexp = jnp.exp(gcol)
            fdec = jnp.exp(glast - gcol)
            dvb, dkb = dvk[:, :C_DK], dvk[:, C_DK:]
            dgc = jnp.sum(e, axis=1, keepdims=True) - mm[4][:, :1]
            dk_parts[c, h] = mm[0] + mm[1] + mm[2] + dkb * (bcol * gexp) + dkd_ref[rs, hs] * fdec
            dq_parts[c, h] = mm[3] + dqd_ref[rs, hs] * gexp
            dv_ref[rs, hs] = dvb * bcol
            s_kb = jnp.sum(dkb * kf, axis=1, keepdims=True)
            db = (jnp.sum(da * kkd, axis=1, keepdims=True) + jnp.sum(dvb * v_ref[rs, hs], axis=1, keepdims=True)
                  + s_kb * gexp)
            rho = jnp.sum(dkd_ref[rs, hs] * kf, axis=1, keepdims=True) * fdec
            dgc = (dgc + s_kb * bcol * gexp + jnp.sum(dqd_ref[rs, hs] * qf, axis=1, keepdims=True) * gexp - rho)
            last = jnp.sum(rho, axis=0, keepdims=True) + dgl_ref[c * 8:c * 8 + 1, h:h + 1] * jnp.exp(glast)
            dgc = dgc + jnp.where(rowi == CHUNK - 1, last, 0.0)
            dgc_tiles[c] = jnp.where(lane == h, dgc, dgc_tiles[c])
            db_tiles[c] = jnp.where(lane == h, db, db_tiles[c])
        alog = alog_ref[...]
        row8 = lax.broadcasted_iota(jnp.int32, (8, LANES), 0)
        par = jnp.zeros((8, LANES), F32)
        for c, rs in zip(subs, rss):
            for j in range(C_QK_HEADS):
                dqk_ref[rs, j * C_DK:(j + 1) * C_DK] = (dq_parts[c, 2 * j] + dq_parts[c, 2 * j + 1]) * scale
                dqk_ref[rs, C_QKW + j * C_DK:C_QKW + (j + 1) * C_DK] = dk_parts[c, 2 * j] + dk_parts[c, 2 * j + 1]
            dg = _dot_mask(jnp.logical_not(strict), dgc_tiles[c])
            g, _, gate_pre = _gate_tiles(cab_ref[rs, :], alog, dtb_ref[...])
            dca = dg * (-jnp.exp(alog)) * jax.nn.sigmoid(gate_pre)
            beta = betas[c]
            dcab_ref[rs, :LANES] = (dca + pltpu.roll(db_tiles[c] * beta * (1.0 - beta), C_V_HEADS, 1)).astype(BF16)
            dcab_ref[rs, LANES:] = jnp.zeros((CHUNK, D_IN_PAD - COL["c_ab"] - LANES), BF16)
            par = par + jnp.where(row8 == 0, jnp.sum(dg * g, axis=0, keepdims=True),
                                  jnp.where(row8 == 1, jnp.sum(dca, axis=0, keepdims=True), 0.0))

        @pl.when(pl.program_id(0) == 0)
        def _():
            dpar_ref[...] = par

        @pl.when(pl.program_id(0) > 0)
        def _():
            dpar_ref[...] += par

    def blk(wd):
        return pl.BlockSpec((rows, wd), lambda n: (n, 0))

    row = pl.BlockSpec((1, LANES), lambda n: (0, 0))
    sqs = blk(C_V_HEADS * CHUNK)
    tail = D_IN_PAD - COL["c_ab"]
    assert COL["c_ab"] % tail == 0 and nc % PREP_SUB == 0
    return pl.pallas_call(
        body, name="delta_prep_bwd", grid=(nc // PREP_SUB,),
        in_specs=[blk(C_QKW), pl.BlockSpec((rows, C_QKW), lambda n: (n, 1)), blk(C_VW),
                  pl.BlockSpec((rows, LANES), lambda n: (n, COL["c_ab"] // LANES)), row, row, sqs,
                  blk(C_VW), blk(C_VW),
                  blk(LANES), blk(LANES), blk(C_VW), blk(C_VW), blk(C_VW), blk(C_VW), sqs,
                  pl.BlockSpec((PREP_SUB * 8, LANES), lambda n: (n, 0))] + extra_specs,
        out_specs=[pl.BlockSpec((rows, tail), lambda n: (n, COL["c_ab"] // tail)),
                   blk(2 * C_QKW), blk(C_VW), pl.BlockSpec((8, LANES), lambda n: (0, 0))],
        out_shape=[_du_shape(t), jax.ShapeDtypeStruct((t, 2 * C_QKW), F32),
                   jax.ShapeDtypeStruct((t, C_VW), F32), jax.ShapeDtypeStruct((8, LANES), F32)],
        input_output_aliases=aliases,
        compiler_params=_cparams(("arbitrary",)),
    )(qk, qk, v, proj, alog, dtb, tmat, u, w, gc, beta, du, dw, dqd, dkd, dattn, dgl, *extra)


def _z_spec(tq):
    return pl.BlockSpec((tq, C_VW), lambda i: (i, COL["c_z"] // C_VW))


def _gated_norm_fwd_call(o, u, gain, *, tq=512):
    t, w = o.shape

    def body(o_ref, z_ref, g_ref, y_ref):
        act, _ = _silu_parts(z_ref[...])
        gv = g_ref[...]
        for h in range(C_V_HEADS):
            hs = slice(h * C_DK, (h + 1) * C_DK)
            ov = o_ref[:, hs]
            r = lax.rsqrt(jnp.mean(ov * ov, axis=1, keepdims=True) + EPS)
            y_ref[:, hs] = ov * r * gv * act[:, hs]

    blk = pl.BlockSpec((tq, w), lambda i: (i, 0))
    return pl.pallas_call(
        body, name="gated_norm_fwd", grid=(t // tq,),
        in_specs=[blk, _z_spec(tq), pl.BlockSpec((1, C_DK), lambda i: (0, 0))], out_specs=blk,
        out_shape=jax.ShapeDtypeStruct((t, w), F32),
        compiler_params=_cparams(("parallel",)),
    )(o, u, gain)


def _gated_norm_bwd_call(o, u, gain, dy, du_buf, *, tq=512):
    t, w = o.shape
    nt = t // tq
    extra, extra_specs, aliases = _du_operands(du_buf, 4)

    def body(o_ref, z_ref, g_ref, dy_ref, *refs):
        dz_ref, do_ref, dg_ref, acc_ref = refs[len(extra):]
        i = pl.program_id(0)
        act, dact = _silu_parts(z_ref[...])
        gv = g_ref[...]
        part = jnp.zeros((8, C_DK), F32)
        for h in range(C_V_HEADS):
            hs = slice(h * C_DK, (h + 1) * C_DK)
            ov = o_ref[:, hs]
            r = lax.rsqrt(jnp.mean(ov * ov, axis=1, keepdims=True) + EPS)
            xh = ov * r
            dyv = dy_ref[:, hs]
            dn = dyv * act[:, hs]
            dz_ref[:, hs] = (dyv * xh * gv * dact[:, hs]).astype(BF16)
            dxh = dn * gv
            do_ref[:, hs] = r * (dxh - xh * jnp.mean(dxh * xh, axis=1, keepdims=True))
            part = part + jnp.sum((dn * xh).reshape(tq // 8, 8, C_DK), axis=0)

        @pl.when(i == 0)
        def _():
            acc_ref[...] = part

        @pl.when(i > 0)
        def _():
            acc_ref[...] += part

        @pl.when(i == nt - 1)
        def _():
            dg_ref[...] = jnp.sum(acc_ref[...], axis=0, keepdims=True)

    blk = pl.BlockSpec((tq, w), lambda i: (i, 0))
    grow = pl.BlockSpec((1, C_DK), lambda i: (0, 0))
    return pl.pallas_call(
        body, name="gated_norm_bwd", grid=(nt,),
        in_specs=[blk, _z_spec(tq), grow, blk] + extra_specs, out_specs=[_z_spec(tq), blk, grow],
        out_shape=[_du_shape(t), jax.ShapeDtypeStruct((t, w), F32), jax.ShapeDtypeStruct((1, C_DK), F32)],
        scratch_shapes=[pltpu.VMEM((8, C_DK), F32)],
        input_output_aliases=aliases,
        compiler_params=_cparams(("arbitrary",)),
    )(o, u, gain, dy, *extra)


def _gate_specs(tq):
    return [pl.BlockSpec((tq, D_MODEL), lambda i, j=j: (i, j)) for j in range(3)]


def _merge_fwd_call(ps, u, *, tq=512):
    t, w = ps[0].shape

    def body(p0, p1, p2, g0, g1, g2, y_ref):
        y_ref[...] = (jax.nn.sigmoid(g0[...]) * p0[...] + jax.nn.sigmoid(g1[...]) * p1[...]
                      + jax.nn.sigmoid(g2[...]) * p2[...]).astype(BF16)

    blk = pl.BlockSpec((tq, w), lambda i: (i, 0))
    return pl.pallas_call(
        body, name="merge_fwd", grid=(t // tq,), in_specs=[blk] * 3 + _gate_specs(tq), out_specs=blk,
        out_shape=jax.ShapeDtypeStruct((t, w), BF16),
        compiler_params=_cparams(("parallel",)),
    )(*ps, u, u, u)


def _merge_bwd_call(ps, u, dy, *, tq=256):
    t, w = dy.shape

    def body(p0, p1, p2, g0, g1, g2, dy_ref, dg_ref, dp0, dp1, dp2):
        dyv = dy_ref[...]
        for j, (p, g, dp) in enumerate(((p0, g0, dp0), (p1, g1, dp1), (p2, g2, dp2))):
            sig = jax.nn.sigmoid(g[...])
            dp[...] = (dyv * sig).astype(BF16)
            dg_ref[:, j * w:(j + 1) * w] = (dyv * p[...] * sig * (1.0 - sig)).astype(BF16)

    blk = pl.BlockSpec((tq, w), lambda i: (i, 0))
    small = jax.ShapeDtypeStruct((t, w), BF16)
    return pl.pallas_call(
        body, name="merge_bwd", grid=(t // tq,), in_specs=[blk] * 3 + _gate_specs(tq) + [blk],
        out_specs=[pl.BlockSpec((tq, 3 * w), lambda i: (i, 0))] + [blk] * 3,
        out_shape=[_du_shape(t)] + [small] * 3,
        compiler_params=_cparams(("parallel",)),
    )(*ps, u, u, u, dy)


Q_SCALE = HEAD_DIM ** -0.5
A_PARTS = ((COL["a_q"], A_W, True, Q_SCALE), (COL["a_k"], A_W, True, 1.0), (COL["a_v"], A_W, False, 1.0))
B_PARTS = ((COL["b_q"], A_W, True, Q_SCALE), (COL["b_k"], B_KVW, True, 1.0), (COL["b_v"], B_KVW, False, 1.0))
BRANCHES = ("w_branch_a", "w_branch_b", "w_branch_c")


def _layer_fwd(x, tabs, p, w_in_b, rest, rest_of, gather_src=None):
    h = _rms_fwd_call(x, p["norm_mix"], name="rms_mix_fwd", out_dtype=BF16)
    if rest[0] == "ride":
        u, (packed,) = _mm(h, w_in_b, bias=p["b_in"], tn=IN_TN, gather_src=[rest[1]], name="in_proj_fwd_gather")
    else:
        u, packed = _mm(h, w_in_b, bias=p["b_in"], tn=IN_TN, name="in_proj_fwd"), rest[1]
    wb, conv_w = rest_of(packed)
    wb = dict(wb, w_in=w_in_b)
    p = dict(p, conv_w=conv_w)
    qkv_a = _rope_gather_call(u, tabs, A_PARTS, dils=[cfg["dil"] for cfg in ATTN_A_CFGS], name="rope_a_fwd")
    a_runs = [_attn_fwd(cfg, qkv2, None) for cfg, qkv2 in zip(ATTN_A_CFGS, qkv_a)]
    os_, lses = tuple(r[0] for r in a_runs), tuple(r[1] for r in a_runs)
    ya = _combine_fwd_call(os_, lses)
    qkv_b, = _rope_gather_call(u, tabs, B_PARTS, name="rope_b_fwd")
    yb, _, b_saved = _attn_fwd(ATTN_B_CFG, qkv_b, p["sinks"])
    zc, qk, v = _conv_prep_fwd_call(u, p["conv_w"])
    uu, ww, qd, kd, attn, tmat, gc, beta, *gathered = _delta_prep_call(qk, v, u, p["a_log"], p["dt_bias"], gather_src)
    o, vn, st = _delta_scan_call(uu, ww, qd, kd, attn, gc)
    yc = _gated_norm_fwd_call(o, u, p["c_norm"])
    ys = (ya, yb, yc)
    ps = tuple(_mm(y, wb[n], name="branch_fwd") for y, n in zip(ys, BRANCHES))
    merged = _merge_fwd_call(ps, u)
    x1 = _mm(merged, wb["w_out"], add=x, name="out_proj_fwd")
    h2 = _rms_fwd_call(x1, p["norm_ffn"], name="rms_ffn_fwd", out_dtype=BF16)
    pre, act = _mm(h2, wb["w_ff1"], relu2_out=True, name="ffn_up")
    x2 = _mm(act, wb["w_ff2"], add=x1, name="ffn_down")
    saved = dict(x=x, h=h, u=u, a_saved=[r[2] for r in a_runs], os_=os_, lses=lses, b_saved=b_saved,
                 zc=zc, qk=qk, v=v, delta=(tmat, uu, ww, gc, beta, qd, kd, attn, vn, st), o=o, ys=ys, ps=ps,
                 merged=merged, x1=x1, h2=h2, pre=pre, act=act, p=p, wb=wb)
    return x2, saved, (gathered if gathered else None)


def _layer_bwd(s, dx2, tabs):
    g, p, wb = {}, s["p"], s["wb"]
    t = dx2.shape[0]
    dpre = _mm(dx2, wb["w_ff2"], tb=True, mul_drelu2=s["pre"], out_dtype=BF16, name="ffn_dpre")
    g["w_ff2"] = _mm(s["act"], dx2, ta=True, tk=1024, name="ffn_dw2")
    g["w_ff1"] = _mm(s["h2"], dpre, ta=True, tk=1024, name="ffn_dw1")
    dh2 = _mm(dpre, wb["w_ff1"], tb=True, name="ffn_dh")
    dx1, g["norm_ffn"] = _rms_bwd_call(s["x1"], p["norm_ffn"], dh2, add=dx2, name="rms_ffn_bwd")
    dmerged = _mm(dx1, wb["w_out"], tb=True, name="out_proj_da")
    g["w_out"] = _mm(s["merged"], dx1, ta=True, tk=1024, name="out_proj_dw")
    du, *dps = _merge_bwd_call(s["ps"], s["u"], dmerged)
    dys = []
    for y, dp, n in zip(s["ys"], dps, BRANCHES):
        dys.append(_mm(dp, wb[n], tb=True, name="branch_da"))
        g[n] = _mm(y, dp, ta=True, tk=1024, name="branch_dw")
    dya, dyb, dyc = dys
    tmat, uu, ww, gc, beta, qd, kd, attn, vn, st = s["delta"]
    du, do, g["c_norm"] = _gated_norm_bwd_call(s["o"], s["u"], p["c_norm"], dyc, du)
    ddu, ddw, dqd, dkd, dattn, dgl = _delta_scan_bwd_call(do, ww, qd, kd, attn, gc, vn, st)
    du, dqk, dv, dpar = _delta_prep_bwd_call(s["qk"], s["v"], s["u"], p["a_log"], p["dt_bias"], tmat, uu, ww, gc,
                                             beta, ddu, ddw, dqd, dkd, dattn, dgl, du)
    g["a_log"], g["dt_bias"] = dpar[0:1], dpar[1:2]
    dzc = _conv_prep_dz_call(s["zc"], dqk, dv)
    du, dconv = _conv_bwd_call(s["u"], dzc, p["conv_w"], du)
    g["conv_w"] = dconv[:C_CONV]
    no_dlse = jnp.zeros((t, LANES), F32)
    dq, dk, dv_b, dsink = _attn_bwd(ATTN_B_CFG, s["b_saved"], p["sinks"], dyb, no_dlse)
    g["sinks"] = dsink[0, :p["sinks"].shape[0]]
    du = _rope_scatter_call(du, t, [([dq], A_W, True, Q_SCALE)], COL["b_q"], tabs, name="rope_bq_bwd")
    du = _rope_scatter_call(du, t, [([dk], B_KVW, True, 1.0), ([dv_b], B_KVW, False, 1.0)], COL["b_k"], tabs,
                            name="rope_bkv_bwd")
    *dos, dl0, dl1, dl2 = _combine_bwd_call(s["os_"], s["lses"], dya)
    grads_a = [_attn_bwd(cfg, sv, None, do_c, dl)[:3]
               for cfg, sv, do_c, dl in zip(ATTN_A_CFGS, s["a_saved"], dos, (dl0, dl1, dl2))]
    dqs, dks, dvs = zip(*grads_a)
    du = _rope_scatter_call(du, t, [(list(dqs), A_W, True, Q_SCALE), (list(dks), A_W, True, 1.0),
                                    (list(dvs), A_W, False, 1.0)],
                            COL["a_q"], tabs, name="rope_a_bwd")
    dh = _mm(du, wb["w_in"], tb=True, tk=IN_TN, name="in_proj_da")
    g["w_in"], g["b_in"] = _mm(s["h"], du, ta=True, b_colsum=True, tn=IN_TN, tk=1024, name="in_proj_dw")
    dx, g["norm_mix"] = _rms_bwd_call(s["x"], p["norm_mix"], dh, add=dx1, name="rms_mix_bwd")
    return dx, g


def _local_step(x, params, w_in_first, rest_first, payload_of_layer, w_in_of, rest_of, tabs, tgt):
    saves = []
    w_in_blocks, rest = w_in_first, rest_first
    for layer in range(DEPTH):
        p = {n: w[layer] for n, w in params.items() if n != "norm_final"}
        nxt = payload_of_layer(layer + 1) if layer + 1 < DEPTH else None
        x, s, gathered = _layer_fwd(x, tabs, p, w_in_of(w_in_blocks), rest, rest_of, nxt)
        saves.append(s)
        if gathered is not None:
            w_in_blocks, rest = gathered[0], ("ready", gathered[1])
    loss, dx, dfinal = _loss_call(x, params["norm_final"], tgt)
    per_layer = []
    for s in reversed(saves):
        dx, g = _layer_bwd(s, dx, tabs)
        per_layer.append(g)
    per_layer.reverse()
    grads = {n: jnp.stack([g[n] for g in per_layer]) for n in per_layer[0]}
    grads["norm_final"] = dfinal
    return loss, dx, grads


def _in_cols_to_kernel(w):
    lead = w.shape[:-1]
    parts, pos = [], 0
    for _, start, width, ref_start in IN_LAYOUT:
        if start > pos:
            parts.append(jnp.zeros(lead + (start - pos,), w.dtype))
        parts.append(w[..., ref_start:ref_start + width])
        pos = start + width
    parts.append(jnp.zeros(lead + (D_IN_PAD - pos,), w.dtype))
    return jnp.concatenate(parts, axis=-1)


def _in_cols_to_reference(w):
    by_ref = sorted(IN_LAYOUT, key=lambda e: e[3])
    return jnp.concatenate([w[..., start:start + width] for _, start, width, _ in by_ref], axis=-1)


W_IN_SHARD = 8464 // N_DEV


def _w_in_from_shards(blocks):
    lead = blocks.shape[1:-1]
    parts, pos = [], 0
    for _, start, width, ref_start in IN_LAYOUT:
        if start > pos:
            parts.append(jnp.zeros(lead + (start - pos,), blocks.dtype))
        col = ref_start
        while col < ref_start + width:
            d, l = divmod(col, W_IN_SHARD)
            n = min(W_IN_SHARD - l, ref_start + width - col)
            parts.append(blocks[d, ..., l:l + n])
            col += n
        pos = start + width
    parts.append(jnp.zeros(lead + (D_IN_PAD - pos,), blocks.dtype))
    return jnp.concatenate(parts, axis=-1)


def _w_in_to_shards(g):
    by_ref = sorted(IN_LAYOUT, key=lambda e: e[3])
    blocks = []
    for d in range(N_DEV):
        lo, hi = d * W_IN_SHARD, (d + 1) * W_IN_SHARD
        parts = []
        for _, start, width, ref_start in by_ref:
            a, b = max(lo, ref_start), min(hi, ref_start + width)
            if a < b:
                parts.append(g[..., start + a - ref_start:start + b - ref_start])
        blocks.append(jnp.concatenate(parts, axis=-1))
    return jnp.stack(blocks)


def _pad_lanes(v):
    return jnp.pad(v, ((0, 0), (0, LANES - v.shape[1])))[:, None, :]


BIG = (("w_in", 2), ("conv_w", 2), ("w_branch_a", 2), ("w_branch_b", 2), ("w_branch_c", 1), ("w_out", 1),
       ("w_ff1", 2), ("w_ff2", 1))
SMALL = ("norm_mix", "b_in", "a_log", "dt_bias", "sinks", "c_norm", "norm_ffn", "norm_final")
WEIGHTS = ("norm_mix", "w_in", "b_in", "conv_w", "a_log", "dt_bias", "sinks", "c_norm", "w_branch_a",
           "w_branch_b", "w_branch_c", "w_out", "norm_ffn", "w_ff1", "w_ff2", "norm_final")
MATMUL_WEIGHTS = ("w_in", "w_branch_a", "w_branch_b", "w_branch_c", "w_out", "w_ff1", "w_ff2")
PACK_ROWS = 1024
ROW_ALIGN = 16


def _seg_rows(n):
    return -(-n // (LANES * ROW_ALIGN)) * ROW_ALIGN


def _pack(arrays, lead=0):
    parts = []
    for a in arrays:
        lead_shape = a.shape[:lead]
        n = math.prod(a.shape[lead:])
        rows = _seg_rows(n)
        if rows * LANES != n:
            a = jnp.pad(a.reshape(lead_shape + (n,)), [(0, 0)] * lead + [(0, rows * LANES - n)])
        parts.append(a.reshape(lead_shape + (rows, LANES)))
    total = sum(p.shape[lead] for p in parts)
    padded = -(-total // PACK_ROWS) * PACK_ROWS
    if padded > total:
        parts.append(jnp.zeros(parts[0].shape[:lead] + (padded - total, LANES), parts[0].dtype))
    return jnp.concatenate(parts, axis=lead)


def _unpack(buf, shapes):
    lead = buf.shape[:-2]
    out, pos = [], 0
    for shp in shapes:
        n = math.prod(shp)
        rows = _seg_rows(n)
        seg = buf[..., pos:pos + rows, :]
        if rows * LANES != n:
            seg = seg.reshape(lead + (rows * LANES,))[..., :n]
        out.append(seg.reshape(lead + tuple(shp)))
        pos += rows
    return out


def _shards_to_full(blocks, axis):
    moved = jnp.moveaxis(blocks, 0, axis)
    shp = list(blocks.shape[1:])
    shp[axis] = shp[axis] * N_DEV
    return moved.reshape(shp)


def _full_to_shards(full, axis):
    shp = list(full.shape)
    shp[axis:axis + 1] = [N_DEV, shp[axis] // N_DEV]
    return jnp.moveaxis(full.reshape(shp), axis, 0)


def _my_place():
    return lax.axis_index("x"), lax.axis_index("y"), lax.axis_index("c")


def _slot(x, y, c):
    return 4 * x + 2 * y + c


GATHER_COPIES = 7


def _gather_plan(x_ref, out_ref, send_sems, recv_sems, local_sem, base):
    x, y, c = _my_place()
    me, sibling = (x, y, c), (x, y, 1 - c)
    chips = [(1 - x, y), (x, 1 - y), (1 - x, 1 - y)]

    def copy(k, blk, to, src=None):
        dst = out_ref.at[_slot(*blk)]
        return pltpu.make_async_remote_copy(
            src_ref=dst if src is None else src, dst_ref=dst,
            send_sem=send_sems.at[base + k], recv_sem=recv_sems.at[base + k], device_id=to, device_id_type=MESH_ID)

    def own():
        mine = pltpu.make_async_copy(x_ref, out_ref.at[_slot(*me)], local_sem)
        return mine, [copy(0, me, sibling, src=x_ref)] + [copy(1 + j, me, (*chip, c), src=x_ref)
                                                          for j, chip in enumerate(chips)]

    return copy, own, me, sibling, chips, c


def _gather_plans(srcs, outs, send_sems, recv_sems, local_sems):
    return [_gather_plan(x_ref, out_ref, send_sems, recv_sems, local_sems.at[i], GATHER_COPIES * i)
            for i, (x_ref, out_ref) in enumerate(zip(srcs, outs))]


def _gather_start(srcs, outs, *sems):
    for _, own, *_ in _gather_plans(srcs, outs, *sems):
        mine, first = own()
        mine.start()
        for cp in first:
            cp.start()


def _gather_finish(srcs, outs, *sems):
    plans = _gather_plans(srcs, outs, *sems)
    passed_all = []
    for copy, own, me, sibling, chips, c in plans:
        passed = [copy(4 + j, (*chip, c), sibling) for j, chip in enumerate(chips)]
        for j, chip in enumerate(chips):
            copy(1 + j, (*chip, c), me).wait_recv()
            passed[j].start()
        passed_all.append(passed)
    for (copy, own, me, sibling, chips, c), passed in zip(plans, passed_all):
        copy(0, sibling, me).wait_recv()
        for j, chip in enumerate(chips):
            copy(4 + j, (*chip, 1 - c), me).wait_recv()
        mine, first = own()
        for cp in first + passed:
            cp.wait_send()
        mine.wait()


def _gather_sems(n):
    return [pltpu.SemaphoreType.DMA((GATHER_COPIES * n,)), pltpu.SemaphoreType.DMA((GATHER_COPIES * n,)),
            pltpu.SemaphoreType.DMA((n,))]


def _gathered_shapes(blocks):
    return [jax.ShapeDtypeStruct((N_DEV,) + b.shape, b.dtype) for b in blocks]


def _all_gather(blocks, *, name):
    n = len(blocks)

    def body(*refs):
        srcs, outs, sems = refs[:n], refs[n:2 * n], refs[2 * n:]
        _gather_start(srcs, outs, *sems)
        _gather_finish(srcs, outs, *sems)

    return pl.pallas_call(
        body, name=name, out_shape=_gathered_shapes(blocks),
        in_specs=[HBM_SPEC] * n, out_specs=[HBM_SPEC] * n,
        scratch_shapes=_gather_sems(n),
    )(*blocks)


N_CHIP = N_DEV // 2


def _swap_with_sibling(blocks, *, name):
    n = len(blocks)

    def body(*refs):
        srcs, outs, send_sems, recv_sems = refs[:n], refs[n:2 * n], refs[2 * n], refs[2 * n + 1]
        x, y, c = _my_place()
        copies = [pltpu.make_async_remote_copy(src_ref=g_ref, dst_ref=out_ref, send_sem=send_sems.at[i],
                                               recv_sem=recv_sems.at[i], device_id=(x, y, 1 - c),
                                               device_id_type=MESH_ID)
                  for i, (g_ref, out_ref) in enumerate(zip(srcs, outs))]
        for cp in copies:
            cp.start()
        for cp in copies:
            cp.wait_recv()
        for cp in copies:
            cp.wait_send()

    return pl.pallas_call(
        body, name=name,
        out_shape=[jax.ShapeDtypeStruct(b.shape, b.dtype) for b in blocks],
        in_specs=[HBM_SPEC] * n, out_specs=[HBM_SPEC] * n,
        scratch_shapes=[pltpu.SemaphoreType.DMA((n,)), pltpu.SemaphoreType.DMA((n,))],
    )(*blocks)


def _chip_all_to_all(blocks, *, name):
    n = len(blocks)
    peers = N_CHIP - 1

    def body(*refs):
        srcs, outs = refs[:n], refs[n:2 * n]
        send_sems, recv_sems, local_sems = refs[2 * n:]
        x, y, c = _my_place()
        mine_slot = 2 * x + y
        locals_, copies = [], []
        for i, (g_ref, out_ref) in enumerate(zip(srcs, outs)):
            locals_.append(pltpu.make_async_copy(g_ref.at[mine_slot], out_ref.at[mine_slot], local_sems.at[i]))
            for k in range(1, N_CHIP):
                px, py = x ^ (k >> 1), y ^ (k & 1)
                copies.append(pltpu.make_async_remote_copy(
                    src_ref=g_ref.at[2 * px + py], dst_ref=out_ref.at[mine_slot],
                    send_sem=send_sems.at[peers * i + k - 1], recv_sem=recv_sems.at[peers * i + k - 1],
                    device_id=(px, py, c), device_id_type=MESH_ID))
        for cp in locals_ + copies:
            cp.start()
        for cp in copies:
            cp.wait_recv()
        for cp in copies:
            cp.wait_send()
        for cp in locals_:
            cp.wait()

    return pl.pallas_call(
        body, name=name,
        out_shape=[jax.ShapeDtypeStruct(b.shape, b.dtype) for b in blocks],
        in_specs=[HBM_SPEC] * n, out_specs=[HBM_SPEC] * n,
        scratch_shapes=[pltpu.SemaphoreType.DMA((peers * n,)), pltpu.SemaphoreType.DMA((peers * n,)),
                        pltpu.SemaphoreType.DMA((n,))],
    )(*blocks)


def _block_rows(rows, cols):
    tr = max(8, min(rows, PACK_ROWS * LANES // (-(-cols // LANES) * LANES) // 8 * 8))
    while rows % tr:
        tr -= 8
    return tr


def _add_bf16_call(a, b, *, name):
    n, rows, cols = a.shape
    tr = _block_rows(rows, cols)

    def body(a_ref, b_ref, o_ref):
        o_ref[...] = (a_ref[...].astype(F32) + b_ref[...].astype(F32)).astype(BF16)

    blk = pl.BlockSpec((n, tr, cols), lambda i: (0, i, 0))
    return pl.pallas_call(
        body, name=name, grid=(rows // tr,), in_specs=[blk, blk], out_specs=blk,
        out_shape=jax.ShapeDtypeStruct(a.shape, BF16),
        compiler_params=_cparams(("parallel",)),
    )(a, b)


def _adamw_call(parts, w, m, v, *, name):
    rows, cols = w.shape
    tr = _block_rows(rows, cols)
    n_parts = parts.shape[0]

    def body(p_ref, w_ref, m_ref, v_ref, g_ref, d_ref, nm_ref, nv_ref):
        g = p_ref[0].astype(F32)
        for s in range(1, n_parts):
            g = g + p_ref[s].astype(F32)
        nm = ADAM_B1 * m_ref[...] + (1.0 - ADAM_B1) * g
        nv = ADAM_B2 * v_ref[...] + (1.0 - ADAM_B2) * jnp.square(g)
        m_hat = nm / (1.0 - ADAM_B1 ** ADAM_STEP)
        v_hat = nv / (1.0 - ADAM_B2 ** ADAM_STEP)
        g_ref[...] = g
        nm_ref[...] = nm
        nv_ref[...] = nv
        d_ref[...] = -ADAM_LR * (m_hat / (jnp.sqrt(v_hat) + ADAM_EPS) + ADAM_WD * w_ref[...])

    blk = pl.BlockSpec((tr, cols), lambda i: (i, 0))
    shape = jax.ShapeDtypeStruct((rows, cols), F32)
    return pl.pallas_call(
        body, name=name, grid=(rows // tr,),
        in_specs=[pl.BlockSpec((n_parts, tr, cols), lambda i: (0, i, 0)), blk, blk, blk],
        out_specs=[blk] * 4, out_shape=[shape] * 4,
        compiler_params=_cparams(("parallel",)),
    )(parts, w, m, v)


def _kernel_params(full):
    return {
        "norm_mix": full["norm_mix"][:, None, :],
        "b_in": _in_cols_to_kernel(full["b_in"])[:, None, :],
        "a_log": _pad_lanes(full["a_log"]),
        "dt_bias": _pad_lanes(full["dt_bias"]),
        "sinks": full["sinks"],
        "c_norm": full["c_norm"][:, None, :],
        "norm_ffn": full["norm_ffn"][:, None, :],
        "norm_final": full["norm_final"][None, :],
    }


def _reference_grads(g):
    return {
        "norm_mix": g["norm_mix"][:, 0, :],
        "b_in": _in_cols_to_reference(g["b_in"][:, 0, :]),
        "conv_w": g["conv_w"],
        "a_log": g["a_log"][:, 0, :C_V_HEADS],
        "dt_bias": g["dt_bias"][:, 0, :C_V_HEADS],
        "sinks": g["sinks"],
        "c_norm": g["c_norm"][:, 0, :],
        "w_branch_a": g["w_branch_a"], "w_branch_b": g["w_branch_b"], "w_branch_c": g["w_branch_c"],
        "w_out": g["w_out"],
        "norm_ffn": g["norm_ffn"][:, 0, :],
        "w_ff1": g["w_ff1"], "w_ff2": g["w_ff2"],
        "norm_final": g["norm_final"][0],
    }


def kernel(x, positions, norm_mix, w_in, b_in, conv_w, a_log, dt_bias, sinks, c_norm, w_branch_a, w_branch_b, w_branch_c, w_out, norm_ffn, w_ff1, w_ff2, norm_final, loss_target, m_norm_mix, m_w_in, m_b_in, m_conv_w, m_a_log, m_dt_bias, m_sinks, m_c_norm, m_w_branch_a, m_w_branch_b, m_w_branch_c, m_w_out, m_norm_ffn, m_w_ff1, m_w_ff2, m_norm_final, v_norm_mix, v_w_in, v_b_in, v_conv_w, v_a_log, v_dt_bias, v_sinks, v_c_norm, v_w_branch_a, v_w_branch_b, v_w_branch_c, v_w_out, v_norm_ffn, v_w_ff1, v_w_ff2, v_norm_final):
    env = dict(locals())
    weights = {n: env[n] for n in WEIGHTS}
    moments_m = {n: env["m_" + n] for n in WEIGHTS}
    moments_v = {n: env["v_" + n] for n in WEIGHTS}

    axis_of = {n: axis - 1 for n, axis in BIG}

    packed_names = [n for n in MATMUL_WEIGHTS if n != "w_in"]

    def payload_of_layer(layer):
        cw = weights["conv_w"][layer]
        c1 = cw.astype(BF16)
        c2 = (cw - c1.astype(F32)).astype(BF16)
        c3 = (cw - c1.astype(F32) - c2.astype(F32)).astype(BF16)
        return [weights["w_in"][layer].astype(BF16),
                _pack([weights[n][layer].astype(BF16) for n in packed_names] + [c1, c2, c3])]

    def rest_of(packed):
        shapes = [weights[n].shape[1:] for n in packed_names] + [weights["conv_w"].shape[1:]] * 3
        blocks = _unpack(packed, shapes)
        wb = {n: _shards_to_full(blk, axis_of[n]) for n, blk in zip(packed_names, blocks)}
        return wb, _shards_to_full(sum(b.astype(F32) for b in blocks[-3:]), axis_of["conv_w"])

    tabs = rope_tables(positions[0])
    w_in_first, = _all_gather(payload_of_layer(0)[:1], name="gather_weights")
    loss, dx, dparams = _local_step(x[0], _kernel_params({n: weights[n] for n in SMALL}), w_in_first,
                                    ("ride", payload_of_layer(0)[1]), payload_of_layer, _w_in_from_shards,
                                    rest_of, tabs, loss_target[0])
    grads = _reference_grads(dparams)
    loss = lax.psum(loss, ("x", "y", "c"))

    core = lax.axis_index("c")
    rest = [(n, axis) for n, axis in BIG if n != "w_in"]
    w_in_rows = DEPTH * D_MODEL

    def by_core(shards, which):
        sh = shards.reshape((N_CHIP, 2) + shards.shape[1:])
        return lax.dynamic_index_in_dim(sh, which, axis=1, keepdims=False).astype(BF16)

    def halves(which):
        w_in_half = by_core(_w_in_to_shards(dparams["w_in"]), which).reshape(N_CHIP, w_in_rows, W_IN_SHARD)
        return [w_in_half, _pack([by_core(_full_to_shards(grads[n], axis), which) for n, axis in rest], lead=1)]

    from_sibling = _swap_with_sibling(halves(1 - core), name="scatter_grads_d2d")
    chip_sums = [_add_bf16_call(keep, got, name="scatter_grads_add")
                 for keep, got in zip(halves(core), from_sibling)]
    w_in_parts, rest_parts = _chip_all_to_all(chip_sums, name="scatter_grads_ici")
    small_parts, = _all_gather([_pack([grads[n] for n in SMALL])], name="gather_small_grads")

    out = {}
    results = _adamw_call(w_in_parts, *[d["w_in"].reshape(w_in_rows, W_IN_SHARD)
                                        for d in (weights, moments_m, moments_v)], name="adamw_w_in")
    for kind, buf in zip(("grad", "delta", "new_m", "new_v"), results):
        out[kind, "w_in"] = buf.reshape(weights["w_in"].shape)
    for names, parts in (([n for n, _ in rest], rest_parts), (list(SMALL), small_parts)):
        shapes = [weights[n].shape for n in names]
        packed = [_pack([d[n] for n in names]) for d in (weights, moments_m, moments_v)]
        results = _adamw_call(parts, *packed, name="adamw_" + names[0])
        for kind, buf in zip(("grad", "delta", "new_m", "new_v"), results):
            for n, arr in zip(names, _unpack(buf, shapes)):
                out[kind, n] = arr
    return (loss, dx[None], *[out[kind, n] for kind in ("grad", "delta", "new_m", "new_v") for n in WEIGHTS])
```

```python
import functools
import math

import jax
import jax.numpy as jnp
from jax import lax
from jax.experimental import pallas as pl
from jax.experimental.pallas import tpu as pltpu

F32 = jnp.float32
BF16 = jnp.bfloat16

N_DEV = 8
D_MODEL = 1024
DEPTH = 2
HEAD_DIM = 64
ROT_DIM = 16
ROPE_THETA = 500000.0
BLK = 128
NEG_INF = -1e30
EPS = 1e-6
A_CONFIGS = ((128, 1), (512, 4), (2048, 16))
B_GROUP = 4
C_QK_HEADS = 4
C_V_HEADS = 8
C_DK = 128
C_CONV = 4
CHUNK = 64
ADAM_LR = 0.001
ADAM_B1 = 0.9
ADAM_B2 = 0.999
ADAM_EPS = 1e-08
ADAM_WD = 0.01
ADAM_STEP = 10

IN_LAYOUT = (
    ("gate_a", 0, 1024, 5392), ("gate_b", 1024, 1024, 6416), ("gate_c", 2048, 1024, 7440),
    ("a_q", 3072, 512, 0), ("a_k", 3584, 512, 512), ("a_v", 4096, 512, 1024), ("b_q", 4608, 512, 1536),
    ("c_z", 5120, 1024, 4352), ("c_qkv", 6144, 2048, 2304),
    ("b_k", 8192, 128, 2048), ("b_v", 8320, 128, 2176), ("c_ab", 8448, 16, 5376),
)
COL = {name: start for name, start, _, _ in IN_LAYOUT}
D_IN_PAD = 8704
IN_TN = D_IN_PAD // 4
LANES = 128
VMEM_LIMIT = 56 * 1024 * 1024


def _cparams(sem=None):
    return pltpu.CompilerParams(dimension_semantics=sem, vmem_limit_bytes=VMEM_LIMIT)


def _relu2(t):
    return jnp.square(jnp.maximum(t, 0.0))


def _mm(a, b, *, ta=False, tb=False, bias=None, a_fn=None, mul_drelu2=None, add=None,
        out_dtype=F32, relu2_out=False, b_colsum=False, gather_src=None, tm=1024, tn=1024, tk=2048, name):
    if ta:
        kdim, m = a.shape
    else:
        m, kdim = a.shape
    n = b.shape[0] if tb else b.shape[1]
    tm, tn, tk = min(tm, m), min(tn, n), min(tk, kdim)
    assert m % tm == 0 and n % tn == 0 and kdim % tk == 0, (a.shape, b.shape, tm, tn, tk)
    nk = kdim // tk
    assert not b_colsum or (m == tm and not tb and nk > 1)
    dims = (((0 if ta else 1,), (1 if tb else 0,)), ((), ()))
    extras = [e for e in (bias, mul_drelu2, add) if e is not None]
    ng = len(gather_src) if gather_src is not None else 0
    grid = (m // tm, n // tn, nk)

    def body(*refs):
        if ng:
            n_in = 2 + len(extras)
            n_out = 1 + int(relu2_out) + int(b_colsum)
            n_scr = int(nk > 1) + int(b_colsum)
            gather_refs = (refs[n_in:n_in + ng], refs[n_in + ng + n_out:n_in + 2 * ng + n_out],
                           *refs[n_in + 2 * ng + n_out + n_scr:])
            refs = refs[:n_in] + refs[n_in + ng:n_in + ng + n_out] + refs[n_in + 2 * ng + n_out:]
            at_first = functools.reduce(jnp.logical_and, [pl.program_id(d) == 0 for d in range(3)])
            at_last = functools.reduce(jnp.logical_and, [pl.program_id(d) == grid[d] - 1 for d in range(3)])

            @pl.when(at_first)
            def _():
                _gather_start(*gather_refs)
        a_ref, b_ref = refs[0], refs[1]
        pos = 2
        bias_ref = pre_ref = add_ref = None
        if bias is not None:
            bias_ref = refs[pos]; pos += 1
        if mul_drelu2 is not None:
            pre_ref = refs[pos]; pos += 1
        if add is not None:
            add_ref = refs[pos]; pos += 1
        o_ref = refs[pos]
        pos += 1
        r_ref = None
        if relu2_out:
            r_ref = refs[pos]; pos += 1
        cs_ref = None
        if b_colsum:
            cs_ref = refs[pos]; pos += 1
        acc_ref = refs[pos] if nk > 1 else None
        cs_acc = refs[pos + 1] if b_colsum else None

        av = a_ref[...]
        if a_fn is not None:
            av = a_fn(av)
        bv = b_ref[...]
        part = lax.dot_general(av.astype(BF16), bv.astype(BF16), dims,
                               preferred_element_type=F32)
        if b_colsum:
            cs_part = jnp.sum(bv.astype(F32).reshape(tk // 8, 8, tn), axis=0)

        def finish(acc):
            if bias_ref is not None:
                acc = acc + bias_ref[...]
            if pre_ref is not None:
                acc = acc * (2.0 * jnp.maximum(pre_ref[...], 0.0))
            if add_ref is not None:
                acc = acc + add_ref[...]
            o_ref[...] = acc.astype(out_dtype)
            if r_ref is not None:
                r_ref[...] = _relu2(acc).astype(BF16)

        if nk == 1:
            finish(part)
        else:
            k = pl.program_id(2)

            @pl.when(k == 0)
            def _():
                acc_ref[...] = part
                if b_colsum:
                    cs_acc[...] = cs_part

            @pl.when(k > 0)
            def _():
                acc_ref[...] += part
                if b_colsum:
                    cs_acc[...] += cs_part

            @pl.when(k == nk - 1)
            def _():
                finish(acc_ref[...])
                if b_colsum:
                    cs_ref[...] = jnp.sum(cs_acc[...], axis=0, keepdims=True)
        if ng:
            @pl.when(at_last)
            def _():
                _gather_finish(*gather_refs)

    a_spec = (pl.BlockSpec((tk, tm), lambda i, j, k: (k, i)) if ta
              else pl.BlockSpec((tm, tk), lambda i, j, k: (i, k)))
    b_spec = (pl.BlockSpec((tn, tk), lambda i, j, k: (j, k)) if tb
              else pl.BlockSpec((tk, tn), lambda i, j, k: (k, j)))
    in_specs = [a_spec, b_spec]
    if bias is not None:
        in_specs.append(pl.BlockSpec((1, tn), lambda i, j, k: (0, j)))
    for _ in extras[(1 if bias is not None else 0):]:
        in_specs.append(pl.BlockSpec((tm, tn), lambda i, j, k: (i, j)))
    o_spec = pl.BlockSpec((tm, tn), lambda i, j, k: (i, j))
    out_specs, out_shape = [o_spec], [jax.ShapeDtypeStruct((m, n), out_dtype)]
    scratch = [pltpu.VMEM((tm, tn), F32)] if nk > 1 else []
    if relu2_out:
        out_specs.append(o_spec)
        out_shape.append(jax.ShapeDtypeStruct((m, n), BF16))
    if b_colsum:
        out_specs.append(pl.BlockSpec((1, tn), lambda i, j, k: (0, j)))
        out_shape.append(jax.ShapeDtypeStruct((1, n), F32))
        scratch.append(pltpu.VMEM((8, tn), F32))
    if ng:
        in_specs += [HBM_SPEC] * ng
        out_specs += [HBM_SPEC] * ng
        out_shape += _gathered_shapes(gather_src)
        scratch += _gather_sems(ng)
    single = len(out_specs) == 1
    outs = pl.pallas_call(
        body, name=name,
        grid=grid,
        in_specs=in_specs,
        out_specs=out_specs[0] if single else out_specs,
        out_shape=out_shape[0] if single else out_shape,
        scratch_shapes=scratch,
        compiler_params=_cparams(("arbitrary",) * 3 if ng else ("parallel", "parallel", "arbitrary")),
    )(a, b, *extras, *(gather_src or []))
    if not ng:
        return outs
    n_out = len(outs) - ng
    return (outs[0] if n_out == 1 else tuple(outs[:n_out])), list(outs[n_out:])


def _rms_fwd_call(x, g, *, name, out_dtype=F32, tq=512):
    t, d = x.shape

    def body(x_ref, g_ref, y_ref):
        xv = x_ref[...]
        r = lax.rsqrt(jnp.mean(xv * xv, axis=-1, keepdims=True) + EPS)
        y_ref[...] = (xv * r * g_ref[...]).astype(out_dtype)

    return pl.pallas_call(
        body, name=name, grid=(t // tq,),
        in_specs=[pl.BlockSpec((tq, d), lambda i: (i, 0)), pl.BlockSpec((1, d), lambda i: (0, 0))],
        out_specs=pl.BlockSpec((tq, d), lambda i: (i, 0)),
        out_shape=jax.ShapeDtypeStruct((t, d), out_dtype),
        compiler_params=_cparams(("parallel",)),
    )(x, g)


def _rms_bwd_call(x, g, dy, *, name, add=None, tq=512):
    t, d = x.shape
    nt = t // tq

    def body(*refs):
        if add is None:
            x_ref, g_ref, dy_ref, dx_ref, dg_ref, acc_ref = refs
        else:
            x_ref, g_ref, dy_ref, add_ref, dx_ref, dg_ref, acc_ref = refs
        i = pl.program_id(0)
        xv = x_ref[...]
        r = lax.rsqrt(jnp.mean(xv * xv, axis=-1, keepdims=True) + EPS)
        xh = xv * r
        dyv = dy_ref[...]
        dxh = dyv * g_ref[...]
        dx = r * (dxh - xh * jnp.mean(dxh * xh, axis=-1, keepdims=True))
        dx_ref[...] = dx if add is None else dx + add_ref[...]
        part = jnp.sum((dyv * xh).reshape(tq // 8, 8, d), axis=0)

        @pl.when(i == 0)
        def _():
            acc_ref[...] = part

        @pl.when(i > 0)
        def _():
            acc_ref[...] += part

        @pl.when(i == nt - 1)
        def _():
            dg_ref[...] = jnp.sum(acc_ref[...], axis=0, keepdims=True)

    blk = pl.BlockSpec((tq, d), lambda i: (i, 0))
    row = pl.BlockSpec((1, d), lambda i: (0, 0))
    extra = [] if add is None else [add]
    return pl.pallas_call(
        body, name=name, grid=(nt,),
        in_specs=[blk, row, blk] + [blk] * len(extra),
        out_specs=[blk, row],
        out_shape=[jax.ShapeDtypeStruct((t, d), F32), jax.ShapeDtypeStruct((1, d), F32)],
        scratch_shapes=[pltpu.VMEM((8, d), F32)],
        compiler_params=_cparams(("arbitrary",)),
    )(x, g, dy, *extra)


def _loss_call(x, g, tgt, *, tq=512):
    t, d = x.shape
    nt = t // tq

    def body(x_ref, g_ref, t_ref, loss_ref, dx_ref, dg_ref, acc_ref, sq_ref):
        i = pl.program_id(0)
        xv = x_ref[...]
        r = lax.rsqrt(jnp.mean(xv * xv, axis=-1, keepdims=True) + EPS)
        xh = xv * r
        gv = g_ref[...]
        err = xh * gv - t_ref[...]
        dyv = err * (1.0 / d)
        dxh = dyv * gv
        dx_ref[...] = r * (dxh - xh * jnp.mean(dxh * xh, axis=-1, keepdims=True))
        part = jnp.sum((dyv * xh).reshape(tq // 8, 8, d), axis=0)
        sq = jnp.sum((err * err).reshape(tq // 8, 8, d), axis=0)

        @pl.when(i == 0)
        def _():
            acc_ref[...] = part
            sq_ref[...] = sq

        @pl.when(i > 0)
        def _():
            acc_ref[...] += part
            sq_ref[...] += sq

        @pl.when(i == nt - 1)
        def _():
            dg_ref[...] = jnp.sum(acc_ref[...], axis=0, keepdims=True)
            tot = jnp.sum(jnp.sum(sq_ref[...], axis=0, keepdims=True), axis=1, keepdims=True)
            loss_ref[...] = jnp.broadcast_to(tot * (0.5 / d), (8, LANES))

    blk = pl.BlockSpec((tq, d), lambda i: (i, 0))
    row = pl.BlockSpec((1, d), lambda i: (0, 0))
    loss, dx, dg = pl.pallas_call(
        body, name="loss", grid=(nt,),
        in_specs=[blk, row, blk],
        out_specs=[pl.BlockSpec((8, LANES), lambda i: (0, 0)), blk, row],
        out_shape=[jax.ShapeDtypeStruct((8, LANES), F32), jax.ShapeDtypeStruct((t, d), F32),
                   jax.ShapeDtypeStruct((1, d), F32)],
        scratch_shapes=[pltpu.VMEM((8, d), F32), pltpu.VMEM((8, d), F32)],
        compiler_params=_cparams(("arbitrary",)),
    )(x, g, tgt)
    return loss[0, 0], dx, dg


MESH_ID = pl.DeviceIdType.MESH
HBM_SPEC = pl.BlockSpec(memory_space=pl.ANY)


def rope_tables(positions):
    half = ROT_DIM // 2
    inv_freq = jnp.power(ROPE_THETA, -jnp.arange(0, ROT_DIM, 2, dtype=F32) / ROT_DIM)
    in_head = jnp.arange(LANES) % HEAD_DIM
    rot = in_head < ROT_DIM
    freq = jnp.where(rot, inv_freq[in_head % half], 0.0)
    ang = positions.astype(F32)[:, None] * freq[None, :]
    cos, sin = jnp.cos(ang), jnp.sin(ang)
    b = jnp.where(jnp.logical_and(rot, in_head >= half)[None, :], sin, 0.0)
    c = jnp.where((in_head < half)[None, :], -sin, 0.0)
    return cos, b, c


def _rope_chunk(xs, a, b, c, transpose):
    half = ROT_DIM // 2
    if transpose:
        return xs * a + pltpu.roll(xs * b, LANES - half, 1) + pltpu.roll(xs * c, half, 1)
    return xs * a + pltpu.roll(xs, half, 1) * b + pltpu.roll(xs, LANES - half, 1) * c


def _dilated_spec(tq, w, d):
    return pl.BlockSpec((tq // d, d * w), lambda i: (i, 0))


def _load_dilated(ref, stage_ref, d, w, j):
    cs = slice(j * LANES, (j + 1) * LANES)
    if d == 1:
        return ref[:, cs].astype(F32)
    for r in range(d):
        stage_ref[pl.ds(r, ref.shape[0], stride=d), :] = ref[:, r * w + j * LANES:r * w + (j + 1) * LANES].astype(F32)
    return stage_ref[...]


def _store_dilated(ref, stage_ref, d, w, j, val):
    if d == 1:
        ref[:, j * LANES:(j + 1) * LANES] = val.astype(ref.dtype)
        return
    stage_ref[...] = val
    for r in range(d):
        rows = stage_ref[pl.ds(r, ref.shape[0], stride=d), :]
        ref[:, r * w + j * LANES:r * w + (j + 1) * LANES] = rows.astype(ref.dtype)


def _stage_buffers(tq, n):
    return [pltpu.VMEM((tq, LANES), F32)] * n


def _rope_gather_call(u, tabs, parts, *, name, dils=(1,), tq=512):
    t = u.shape[0]
    total = sum(w for _, w, _, _ in parts)
    assert all(start % w == 0 for start, w, _, _ in parts)
    n_stage = (total // LANES) * sum(d > 1 for d in dils)

    def body(a_ref, b_ref, c_ref, *refs):
        x_refs, o_refs = refs[:len(parts)], refs[len(parts):len(parts) + len(dils)]
        stages = iter(refs[len(parts) + len(dils):])
        a, b, c = a_ref[...], b_ref[...], c_ref[...]
        g = 0
        for x_ref, (_, w, roped, scale) in zip(x_refs, parts):
            for j in range(w // LANES):
                xs = x_ref[:, j * LANES:(j + 1) * LANES]
                val = _rope_chunk(xs, a, b, c, False) if roped else xs
                val = val * scale if scale != 1.0 else val
                for o_ref, d in zip(o_refs, dils):
                    _store_dilated(o_ref, next(stages) if d > 1 else None, d, total, g, val)
                g += 1

    tab_spec = pl.BlockSpec((tq, LANES), lambda i: (i, 0))
    return pl.pallas_call(
        body, name=name, grid=(t // tq,),
        in_specs=[tab_spec] * 3 + [pl.BlockSpec((tq, w), lambda i, cb=start // w: (i, cb)) for start, w, _, _ in parts],
        out_specs=[_dilated_spec(tq, total, d) for d in dils],
        out_shape=[jax.ShapeDtypeStruct((t // d, d * total), BF16) for d in dils],
        scratch_shapes=_stage_buffers(tq, n_stage),
        compiler_params=_cparams(("parallel",)),
    )(*tabs, *([u] * len(parts)))


def _du_operands(du_buf, n_inputs):
    if du_buf is None:
        return [], [], {}
    return [du_buf], [HBM_SPEC], {n_inputs: 0}


def _du_shape(t):
    return jax.ShapeDtypeStruct((t, D_IN_PAD), BF16)


def _rope_scatter_call(du_buf, t, pieces, col, tabs, *, name, dils=(1,), tq=512):
    total = sum(w for _, w, _, _ in pieces)
    assert col % total == 0 and all(len(arrs) == len(dils) for arrs, _, _, _ in pieces)
    arrays = [a for arrs, _, _, _ in pieces for a in arrs]
    extra, extra_specs, aliases = _du_operands(du_buf, 3 + len(arrays))
    n_stage = (total // LANES) * sum(d > 1 for d in dils)

    def body(a_ref, b_ref, c_ref, *refs):
        o_ref = refs[len(arrays) + len(extra)]
        stages = iter(refs[len(arrays) + len(extra) + 1:])
        a, b, c = a_ref[...], b_ref[...], c_ref[...]
        k = off = 0
        for arrs, w, roped, scale in pieces:
            mine = refs[k:k + len(arrs)]
            k += len(arrs)
            for j in range(w // LANES):
                xs = None
                for r, d in zip(mine, dils):
                    part = _load_dilated(r, next(stages) if d > 1 else None, d, w, j)
                    xs = part if xs is None else xs + part
                if scale != 1.0:
                    xs = xs * scale
                val = _rope_chunk(xs, a, b, c, True) if roped else xs
                o_ref[:, off + j * LANES:off + (j + 1) * LANES] = val.astype(BF16)
            off += w

    tab_spec = pl.BlockSpec((tq, LANES), lambda i: (i, 0))
    in_specs = [tab_spec] * 3 + [_dilated_spec(tq, w, d) for _, w, _, _ in pieces for d in dils]
    return pl.pallas_call(
        body, name=name, grid=(t // tq,),
        in_specs=in_specs + extra_specs,
        out_specs=pl.BlockSpec((tq, total), lambda i: (i, col // total)),
        out_shape=_du_shape(t), input_output_aliases=aliases,
        scratch_shapes=_stage_buffers(tq, n_stage),
        compiler_params=_cparams(("parallel",)),
    )(*tabs, *arrays, *extra)


def _band_masks(first_block, max_dist):
    qi = lax.broadcasted_iota(jnp.int32, (BLK, BLK), 0)
    kj = lax.broadcasted_iota(jnp.int32, (BLK, BLK), 1)
    valid_prev = jnp.logical_and(kj >= qi + (BLK - max_dist), jnp.logical_not(first_block))
    valid_cur = kj <= qi
    return valid_prev, valid_cur


_NN = (((1,), (0,)), ((), ()))
_NT = (((1,), (1,)), ((), ()))
_TN = (((0,), (0,)), ((), ()))


HEAD_STAGE = 8


def _attn_row_maps(nb):
    def cur(i):
        return jnp.minimum(i, nb - 1)

    def prev(i):
        return jnp.maximum(jnp.minimum(i, nb - 1) - 1, 0)

    return cur, prev


def _dil_spec(w, dil, rows, seg=None, off=0):
    seg = w if seg is None else seg
    assert off % w == 0 and (dil == 1 or seg % w == 0)
    return pl.BlockSpec((BLK, w), lambda r, i: (rows(i), (r * seg + off) // w))


def _dil_shape(l, dil, w, dtype=F32):
    return jax.ShapeDtypeStruct((l, dil * w), dtype)


def _attn_fwd_call(qkv2, sink, *, dil, group, max_dist, seg, offs, qw, kw, name):
    l = qkv2.shape[0]
    nh = qw // HEAD_DIM
    nb = l // BLK
    use_sink = sink is not None

    def body(*refs):
        if use_sink:
            sink_ref, refs = refs[0], refs[1:]
        q_ref, kp_ref, kc_ref, vp_ref, vc_ref, o_ref, lse_ref = refs
        valid_prev, valid_cur = _band_masks(pl.program_id(1) == 0, max_dist)
        lane = lax.broadcasted_iota(jnp.int32, (BLK, LANES), 1)
        lse_tile = jnp.zeros((BLK, LANES), F32)

        def dot(a, b, dims=_NN):
            return lax.dot_general(a, b, dims, preferred_element_type=F32)

        for g0 in range(0, nh, HEAD_STAGE):
            heads = list(range(g0, min(g0 + HEAD_STAGE, nh)))
            kv = {}
            for kh in sorted({h // group for h in heads}):
                ks = slice(kh * HEAD_DIM, (kh + 1) * HEAD_DIM)
                kv[kh] = tuple(ref[:, ks].astype(BF16) for ref in (kp_ref, kc_ref, vp_ref, vc_ref))
            qs = [q_ref[:, h * HEAD_DIM:(h + 1) * HEAD_DIM].astype(BF16) for h in heads]
            sps = [jnp.where(valid_prev, dot(qh, kv[h // group][0], _NT), NEG_INF) for h, qh in zip(heads, qs)]
            scs = [jnp.where(valid_cur, dot(qh, kv[h // group][1], _NT), NEG_INF) for h, qh in zip(heads, qs)]
            ms = [jnp.maximum(jnp.max(sp, axis=1, keepdims=True), jnp.max(sc, axis=1, keepdims=True))
                  for sp, sc in zip(sps, scs)]
            if use_sink:
                ms = [jnp.maximum(m, sink_ref[h]) for h, m in zip(heads, ms)]
            pps = [jnp.exp(sp - m) for sp, m in zip(sps, ms)]
            pcs = [jnp.exp(sc - m) for sc, m in zip(scs, ms)]
            dens = [jnp.sum(pp, axis=1, keepdims=True) + jnp.sum(pc, axis=1, keepdims=True)
                    for pp, pc in zip(pps, pcs)]
            if use_sink:
                dens = [den + jnp.exp(sink_ref[h] - m) for h, den, m in zip(heads, dens, ms)]
            outs = [dot(pp.astype(BF16), kv[h // group][2]) + dot(pc.astype(BF16), kv[h // group][3])
                    for h, pp, pc in zip(heads, pps, pcs)]
            for h, o, den, m in zip(heads, outs, dens, ms):
                o_ref[:, h * HEAD_DIM:(h + 1) * HEAD_DIM] = o / den
                lse_tile = jnp.where(lane == h, m + jnp.log(den), lse_tile)
        lse_ref[...] = lse_tile

    cur, prev = _attn_row_maps(nb)
    o_spec, lse_spec = _dil_spec(qw, dil, cur), _dil_spec(LANES, dil, cur)
    in_specs = [_dil_spec(qw, dil, cur, seg, offs[0]),
                _dil_spec(kw, dil, prev, seg, offs[1]), _dil_spec(kw, dil, cur, seg, offs[1]),
                _dil_spec(kw, dil, prev, seg, offs[2]), _dil_spec(kw, dil, cur, seg, offs[2])]
    args = [qkv2] * 5
    if use_sink:
        in_specs = [pl.BlockSpec(memory_space=pltpu.SMEM)] + in_specs
        args = [sink] + args
    return pl.pallas_call(
        body, name=name, grid=(dil, nb),
        in_specs=in_specs,
        out_specs=[o_spec, lse_spec],
        out_shape=[_dil_shape(l, dil, qw), _dil_shape(l, dil, LANES)],
        compiler_params=_cparams(("parallel", "parallel")),
    )(*args)


def _attn_bwd_call(qkv2, sink, o2, lse2, do2, dlse2, *, dil, group, max_dist, seg, offs, qw, kw, name):
    l = qkv2.shape[0]
    nh = qw // HEAD_DIM
    nb = l // BLK
    use_sink = sink is not None

    def body(*refs):
        if use_sink:
            sink_ref, refs = refs[0], refs[1:]
        (q_ref, kp_ref, kc_ref, vp_ref, vc_ref, o_ref, lse_ref, do_ref, dlse_ref,
         dq_ref, dk_ref, dv_ref, dsink_ref, ck_ref, cv_ref) = refs
        step = pl.program_id(1)

        @pl.when(jnp.logical_and(pl.program_id(0) == 0, step == 0))
        def _():
            dsink_ref[...] = jnp.zeros_like(dsink_ref)

        @pl.when(step == 0)
        def _():
            ck_ref[...] = jnp.zeros_like(ck_ref)
            cv_ref[...] = jnp.zeros_like(cv_ref)

        def dot(a, b, dims=_NN):
            return lax.dot_general(a, b, dims, preferred_element_type=F32)

        @pl.when(step < nb)
        def _():
            valid_prev, valid_cur = _band_masks(step == 0, max_dist)
            row = lax.broadcasted_iota(jnp.int32, (8, LANES), 0)
            lanes8 = lax.broadcasted_iota(jnp.int32, (8, LANES), 1)
            ds_tile = jnp.zeros((8, LANES), F32)
            for g0 in range(0, nh, HEAD_STAGE):
                heads = list(range(g0, min(g0 + HEAD_STAGE, nh)))
                hss = [slice(h * HEAD_DIM, (h + 1) * HEAD_DIM) for h in heads]
                kv = {}
                for kh in sorted({h // group for h in heads}):
                    ks = slice(kh * HEAD_DIM, (kh + 1) * HEAD_DIM)
                    kv[kh] = tuple(ref[:, ks].astype(BF16) for ref in (kp_ref, kc_ref, vp_ref, vc_ref))
                qs = [q_ref[:, hs].astype(BF16) for hs in hss]
                dos = [do_ref[:, hs] for hs in hss]
                dobs = [d.astype(BF16) for d in dos]
                lses = [lse_ref[:, h:h + 1] for h in heads]
                sps = [dot(qh, kv[h // group][0], _NT) for h, qh in zip(heads, qs)]
                scs = [dot(qh, kv[h // group][1], _NT) for h, qh in zip(heads, qs)]
                dpps = [dot(dob, kv[h // group][2], _NT) for h, dob in zip(heads, dobs)]
                dpcs = [dot(dob, kv[h // group][3], _NT) for h, dob in zip(heads, dobs)]
                pps = [jnp.where(valid_prev, jnp.exp(jnp.where(valid_prev, sp, NEG_INF) - ls), 0.0)
                       for sp, ls in zip(sps, lses)]
                pcs = [jnp.where(valid_cur, jnp.exp(jnp.where(valid_cur, sc, NEG_INF) - ls), 0.0)
                       for sc, ls in zip(scs, lses)]
                deltas = [jnp.sum(d * o_ref[:, hs], axis=1, keepdims=True) for d, hs in zip(dos, hss)]
                corrs = [dlse_ref[:, h:h + 1] - dl for h, dl in zip(heads, deltas)]
                dsps = [(pp * (dp + c)).astype(BF16) for pp, dp, c in zip(pps, dpps, corrs)]
                dscs = [(pc * (dp + c)).astype(BF16) for pc, dp, c in zip(pcs, dpcs, corrs)]
                for h, hs, dsp, dsc in zip(heads, hss, dsps, dscs):
                    dq = dot(dsp, kv[h // group][0]) + dot(dsc, kv[h // group][1])
                    dq_ref[:, hs] = dq.astype(BF16)
                parts = [(dot(dsc, qh, _TN), dot(dsp, qh, _TN),
                          dot(pc.astype(BF16), dob, _TN), dot(pp.astype(BF16), dob, _TN))
                         for dsc, dsp, qh, pc, pp, dob in zip(dscs, dsps, qs, pcs, pps, dobs)]
                for kh in kv:
                    ks = slice(kh * HEAD_DIM, (kh + 1) * HEAD_DIM)
                    mine = [p for h, p in zip(heads, parts) if h // group == kh]
                    dkc, dkp, dvc, dvp = (sum(p[j] for p in mine[1:]) + mine[0][j] for j in range(4))
                    dk_ref[:, ks] = (ck_ref[:, ks] + dkp).astype(BF16)
                    dv_ref[:, ks] = (cv_ref[:, ks] + dvp).astype(BF16)
                    ck_ref[:, ks] = dkc
                    cv_ref[:, ks] = dvc
                if use_sink:
                    for h, ls, dl in zip(heads, lses, deltas):
                        val = -jnp.sum(jnp.exp(sink_ref[h] - ls) * dl, axis=0, keepdims=True)
                        ds_tile = jnp.where(jnp.logical_and(row == 0, lanes8 == h), val, ds_tile)
            if use_sink:
                dsink_ref[...] += ds_tile

        @pl.when(step == nb)
        def _():
            dk_ref[...] = ck_ref[...].astype(BF16)
            dv_ref[...] = cv_ref[...].astype(BF16)

    cur, prev = _attn_row_maps(nb)
    q_spec, lse_spec = _dil_spec(qw, dil, cur), _dil_spec(LANES, dil, cur)
    lag_spec = _dil_spec(kw, dil, lambda i: jnp.maximum(i - 1, 0))
    in_specs = [_dil_spec(qw, dil, cur, seg, offs[0]),
                _dil_spec(kw, dil, prev, seg, offs[1]), _dil_spec(kw, dil, cur, seg, offs[1]),
                _dil_spec(kw, dil, prev, seg, offs[2]), _dil_spec(kw, dil, cur, seg, offs[2]),
                q_spec, lse_spec, q_spec, lse_spec]
    args = [qkv2] * 5 + [o2, lse2, do2, dlse2]
    if use_sink:
        in_specs = [pl.BlockSpec(memory_space=pltpu.SMEM)] + in_specs
        args = [sink] + args
    kv_shape = _dil_shape(l, dil, kw, BF16)
    return pl.pallas_call(
        body, name=name, grid=(dil, nb + 1),
        in_specs=in_specs,
        out_specs=[q_spec, lag_spec, lag_spec, pl.BlockSpec((8, LANES), lambda r, i: (0, 0))],
        out_shape=[_dil_shape(l, dil, qw, BF16), kv_shape, kv_shape,
                   jax.ShapeDtypeStruct((8, LANES), F32)],
        scratch_shapes=[pltpu.VMEM((BLK, kw), F32), pltpu.VMEM((BLK, kw), F32)],
        compiler_params=_cparams(("arbitrary", "arbitrary")),
    )(*args)


def _attn_config(tag, dil, group, max_dist, seg, offs, qw, kw):
    return dict(name=tag, dil=dil, group=group, max_dist=max_dist, seg=seg, offs=offs, qw=qw, kw=kw)


A_W = 8 * HEAD_DIM
ATTN_A_CFGS = tuple(_attn_config("attn_a%d" % dil, dil, 1, window // dil, 3 * A_W, (0, A_W, 2 * A_W), A_W, A_W)
                    for window, dil in A_CONFIGS)
B_KVW = 2 * HEAD_DIM
A_DILS = tuple(cfg["dil"] for cfg in ATTN_A_CFGS)
ATTN_B_CFG = _attn_config("attn_b", 1, B_GROUP, BLK - 1, A_W + 2 * B_KVW, (0, A_W, A_W + B_KVW), A_W, B_KVW)


def _attn_fwd(cfg, qkv2, sink):
    kw = {k: v for k, v in cfg.items() if k != "name"}
    return _attn_fwd_call(qkv2, sink, name=cfg["name"] + "_fwd", **kw)


def _attn_bwd(cfg, qkv2, o2, lse2, sink, do2, dlse2):
    kw = {k: v for k, v in cfg.items() if k != "name"}
    return _attn_bwd_call(qkv2, sink, o2, lse2, do2, dlse2, name=cfg["name"] + "_bwd", **kw)


def _head_expand():
    r = lax.broadcasted_iota(jnp.int32, (LANES, 8 * HEAD_DIM), 0)
    c = lax.broadcasted_iota(jnp.int32, (LANES, 8 * HEAD_DIM), 1)
    return (c // HEAD_DIM == r).astype(F32)


def _combine_weights(l0, l1, l2):
    m = jnp.maximum(jnp.maximum(l0, l1), l2)
    e0, e1, e2 = jnp.exp(l0 - m), jnp.exp(l1 - m), jnp.exp(l2 - m)
    inv = 1.0 / (e0 + e1 + e2)
    return e0 * inv, e1 * inv, e2 * inv


def _combine_fwd_call(os_, lses, dils, *, tq=512):
    w = os_[0].shape[1] // dils[0]
    t = os_[0].shape[0] * dils[0]
    groups = w // LANES
    n_stage = (groups + 1) * sum(d > 1 for d in dils)

    def body(*refs):
        o_refs, l_refs, y_ref = refs[:3], refs[3:6], refs[6]
        stages = iter(refs[7:])
        ws = _combine_weights(*[_load_dilated(l, next(stages) if d > 1 else None, d, LANES, 0)
                                for l, d in zip(l_refs, dils)])
        e = _head_expand()
        wide = [_dot_mask(e, wt, mask_left=False) for wt in ws]
        for j in range(groups):
            cs = slice(j * LANES, (j + 1) * LANES)
            y = None
            for o_ref, d, wd in zip(o_refs, dils, wide):
                term = wd[:, cs] * _load_dilated(o_ref, next(stages) if d > 1 else None, d, w, j)
                y = term if y is None else y + term
            y_ref[:, cs] = y

    return pl.pallas_call(
        body, name="combine_fwd", grid=(t // tq,),
        in_specs=[_dilated_spec(tq, w, d) for d in dils] + [_dilated_spec(tq, LANES, d) for d in dils],
        out_specs=pl.BlockSpec((tq, w), lambda i: (i, 0)),
        out_shape=jax.ShapeDtypeStruct((t, w), F32),
        scratch_shapes=_stage_buffers(tq, n_stage),
        compiler_params=_cparams(("parallel",)),
    )(*os_, *lses)


def _combine_bwd_call(os_, lses, dy, dils, *, tq=512):
    t, w = dy.shape
    groups = w // LANES
    n_stage = 2 * (groups + 1) * sum(d > 1 for d in dils)

    def body(*refs):
        o_refs, l_refs, dy_ref, do_refs, dl_refs = refs[:3], refs[3:6], refs[6], refs[7:10], refs[10:13]
        stages = iter(refs[13:])

        def stage(d):
            return next(stages) if d > 1 else None

        ws = _combine_weights(*[_load_dilated(l, stage(d), d, LANES, 0) for l, d in zip(l_refs, dils)])
        e = _head_expand()
        dyv = dy_ref[...]
        dws = []
        for o_ref, do_ref, d, wt in zip(o_refs, do_refs, dils, ws):
            do = _dot_mask(e, wt, mask_left=False) * dyv
            for j in range(groups):
                _store_dilated(do_ref, stage(d), d, w, j, do[:, j * LANES:(j + 1) * LANES])
            ov = jnp.concatenate([_load_dilated(o_ref, stage(d), d, w, j) for j in range(groups)], axis=1)
            dws.append(_dot_mask(e, dyv * ov, _NT, mask_left=False))
        mean = ws[0] * dws[0] + ws[1] * dws[1] + ws[2] * dws[2]
        for dl_ref, d, wt, dw in zip(dl_refs, dils, ws, dws):
            _store_dilated(dl_ref, stage(d), d, LANES, 0, wt * (dw - mean))

    o_specs = [_dilated_spec(tq, w, d) for d in dils]
    l_specs = [_dilated_spec(tq, LANES, d) for d in dils]
    return pl.pallas_call(
        body, name="combine_bwd", grid=(t // tq,),
        in_specs=o_specs + l_specs + [pl.BlockSpec((tq, w), lambda i: (i, 0))], out_specs=o_specs + l_specs,
        out_shape=[jax.ShapeDtypeStruct((t // d, d * w), F32) for d in dils]
        + [jax.ShapeDtypeStruct((t // d, d * LANES), F32) for d in dils],
        scratch_shapes=_stage_buffers(tq, n_stage),
        compiler_params=_cparams(("parallel",)),
    )(*os_, *lses, dy)


C_QKW = C_QK_HEADS * C_DK
C_CONV_W = 2 * C_QKW + C_V_HEADS * C_DK
HALO = 8


def _silu_parts(z):
    sig = jax.nn.sigmoid(z)
    return z * sig, sig * (1.0 + z * (1.0 - sig))


def _conv_window_specs(tq, t):
    c = C_CONV_W
    cb = COL["c_qkv"] // c
    blk = pl.BlockSpec((tq, c), lambda i: (i, cb))
    before = pl.BlockSpec((HALO, c), lambda i: (jnp.maximum(i * (tq // HALO) - 1, 0), cb))
    return c, cb, blk, before


def _conv_prep_fwd_call(u, w, *, tq=512):
    t = u.shape[0]
    c, _, x_spec, halo_spec = _conv_window_specs(tq, t)
    nqk = 2 * C_QK_HEADS

    def body(x_ref, halo_ref, w_ref, z_ref, qk_ref, v_ref):
        i = pl.program_id(0)
        halo = jnp.where(i == 0, 0.0, halo_ref[...])
        xc = jnp.concatenate([halo, x_ref[...]], axis=0)
        wv = w_ref[...]
        z = xc[HALO - 3:HALO - 3 + tq] * wv[0:1]
        for j in range(1, C_CONV):
            z = z + xc[HALO - 3 + j:HALO - 3 + j + tq] * wv[j:j + 1]
        z_ref[...] = z
        act, _ = _silu_parts(z)
        for h in range(nqk):
            a = act[:, h * C_DK:(h + 1) * C_DK]
            qk_ref[:, h * C_DK:(h + 1) * C_DK] = a * lax.rsqrt(jnp.sum(a * a, axis=1, keepdims=True) + EPS)
        v_ref[...] = act[:, nqk * C_DK:]

    return pl.pallas_call(
        body, name="conv_prep_fwd", grid=(t // tq,),
        in_specs=[x_spec, halo_spec, pl.BlockSpec((C_CONV, c), lambda i: (0, 0))],
        out_specs=[pl.BlockSpec((tq, c), lambda i: (i, 0)),
                   pl.BlockSpec((tq, 2 * C_QKW), lambda i: (i, 0)),
                   pl.BlockSpec((tq, c - 2 * C_QKW), lambda i: (i, 0))],
        out_shape=[jax.ShapeDtypeStruct((t, c), F32), jax.ShapeDtypeStruct((t, 2 * C_QKW), F32),
                   jax.ShapeDtypeStruct((t, c - 2 * C_QKW), F32)],
        compiler_params=_cparams(("parallel",)),
    )(u, u, w)


def _conv_prep_dz_call(z, dqk, dv, *, tq=512):
    t, c = z.shape
    nqk = 2 * C_QK_HEADS

    def body(z_ref, dqk_ref, dv_ref, dz_ref):
        zv = z_ref[...]
        act, dact = _silu_parts(zv)
        for h in range(nqk):
            hs = slice(h * C_DK, (h + 1) * C_DK)
            a = act[:, hs]
            r = lax.rsqrt(jnp.sum(a * a, axis=1, keepdims=True) + EPS)
            nrm = a * r
            dn = dqk_ref[:, hs]
            da = r * (dn - nrm * jnp.sum(dn * nrm, axis=1, keepdims=True))
            dz_ref[:, hs] = da * dact[:, hs]
        dz_ref[:, nqk * C_DK:] = dv_ref[...] * dact[:, nqk * C_DK:]

    return pl.pallas_call(
        body, name="conv_prep_dz", grid=(t // tq,),
        in_specs=[pl.BlockSpec((tq, c), lambda i: (i, 0)),
                  pl.BlockSpec((tq, 2 * C_QKW), lambda i: (i, 0)),
                  pl.BlockSpec((tq, c - 2 * C_QKW), lambda i: (i, 0))],
        out_specs=pl.BlockSpec((tq, c), lambda i: (i, 0)),
        out_shape=jax.ShapeDtypeStruct((t, c), F32),
        compiler_params=_cparams(("parallel",)),
    )(z, dqk, dv)


def _conv_bwd_call(u, dz, w, du_buf, *, tq=512):
    t = u.shape[0]
    nt = t // tq
    c, cb, x_spec, halo_spec = _conv_window_specs(tq, t)
    extra, extra_specs, aliases = _du_operands(du_buf, 5)

    def body(x_ref, xh_ref, dz_ref, dzh_ref, w_ref, *refs):
        dx_ref, dw_ref = refs[len(extra):]
        i = pl.program_id(0)
        xc = jnp.concatenate([jnp.where(i == 0, 0.0, xh_ref[...]), x_ref[...]], axis=0)
        dzv = dz_ref[...]
        dzc = jnp.concatenate([dzv, jnp.where(i == nt - 1, 0.0, dzh_ref[...])], axis=0)
        wv = w_ref[...]
        dx = dzv * wv[3:4]
        for s in range(1, C_CONV):
            dx = dx + dzc[s:s + tq] * wv[3 - s:4 - s]
        dx_ref[...] = dx.astype(BF16)
        row = lax.broadcasted_iota(jnp.int32, (8, c), 0)
        dw = jnp.zeros((8, c), F32)
        for j in range(C_CONV):
            prod = dzv * xc[HALO - 3 + j:HALO - 3 + j + tq]
            col = jnp.sum(jnp.sum(prod.reshape(tq // 8, 8, c), axis=0), axis=0, keepdims=True)
            dw = jnp.where(row == j, col, dw)

        @pl.when(i == 0)
        def _():
            dw_ref[...] = dw

        @pl.when(i > 0)
        def _():
            dw_ref[...] += dw

    blk = pl.BlockSpec((tq, c), lambda i: (i, 0))
    after = pl.BlockSpec((HALO, c), lambda i: (jnp.minimum((i + 1) * (tq // HALO), t // HALO - 1), 0))
    return pl.pallas_call(
        body, name="conv_bwd", grid=(nt,),
        in_specs=[x_spec, halo_spec, blk, after, pl.BlockSpec((C_CONV, c), lambda i: (0, 0))] + extra_specs,
        out_specs=[pl.BlockSpec((tq, c), lambda i: (i, cb)), pl.BlockSpec((8, c), lambda i: (0, 0))],
        out_shape=[_du_shape(t), jax.ShapeDtypeStruct((8, c), F32)],
        input_output_aliases=aliases,
        compiler_params=_cparams(("arbitrary",)),
    )(u, u, dz, dz, w, *extra)


C_VW = C_V_HEADS * C_DK


def _softplus(x):
    return jnp.maximum(x, 0.0) + jnp.log(1.0 + jnp.exp(-jnp.abs(x)))


def _tri_masks():
    r = lax.broadcasted_iota(jnp.int32, (CHUNK, CHUNK), 0)
    c = lax.broadcasted_iota(jnp.int32, (CHUNK, CHUNK), 1)
    return r >= c, r > c


def _split_bf16(a):
    hi = a.astype(BF16)
    return hi, (a - hi.astype(F32)).astype(BF16)


def _dot_hi(a, b, dims=None):
    dims = _NN if dims is None else dims
    ah, al = _split_bf16(a)
    bh, bl = _split_bf16(b)

    def d(x, y):
        return lax.dot_general(x, y, dims, preferred_element_type=F32)

    return d(ah, bh) + (d(ah, bl) + d(al, bh))


def _dot_mask(mask, b, dims=None, mask_left=True):
    dims = _NN if dims is None else dims
    mb = mask.astype(BF16)
    b1 = b.astype(BF16)
    rest = b - b1.astype(F32)
    b2 = rest.astype(BF16)
    b3 = (rest - b2.astype(F32)).astype(BF16)
    out = None
    for p in (b1, b2, b3):
        term = (lax.dot_general(mb, p, dims, preferred_element_type=F32) if mask_left
                else lax.dot_general(p, mb, dims, preferred_element_type=F32))
        out = term if out is None else out + term
    return out


def _unit_lower_inverses(mats):
    r = lax.broadcasted_iota(jnp.int32, (CHUNK, CHUNK), 0)
    c = lax.broadcasted_iota(jnp.int32, (CHUNK, CHUNK), 1)
    eye = (r == c).astype(F32)
    xs = [eye - a for a in mats]
    ps = [_dot_hi(a, a) for a in mats]
    steps = int(math.log2(CHUNK)) - 1
    for s in range(steps):
        xs = [x + _dot_hi(x, p) for x, p in zip(xs, ps)]
        if s < steps - 1:
            ps = [_dot_hi(p, p) for p in ps]
    return xs


def _gate_tiles(cab, alog, dtb):
    pre = cab + dtb
    g = -jnp.exp(alog) * _softplus(pre)
    beta = jax.nn.sigmoid(pltpu.roll(cab, LANES - C_V_HEADS, 1))
    return g, beta, pre


def _chunk_common(kk, qk, gc, gct, beta, h, tri, strict):
    gcol, grow, bcol = gc[:, h:h + 1], gct[h:h + 1, :], beta[:, h:h + 1]
    decay = jnp.where(tri, jnp.exp(jnp.where(tri, gcol - grow, 0.0)), 0.0)
    kkd = jnp.where(strict, kk * decay, 0.0)
    attn = jnp.where(tri, qk * decay, 0.0)
    glast = gc[CHUNK - 1:CHUNK, h:h + 1]
    return gcol, bcol, decay, kkd, attn, glast


def _cab_spec():
    return pl.BlockSpec((CHUNK, LANES), lambda n: (n, COL["c_ab"] // LANES))


def _delta_prep_call(qk, v, u, alog, dtb, gather_src=None):
    t = qk.shape[0]
    nc = t // CHUNK
    scale = C_DK ** -0.5
    riding = gather_src is not None
    ng = len(gather_src) if riding else 0

    def body(q_ref, k_ref, v_ref, cab_ref, alog_ref, dtb_ref, *refs):
        if riding:
            gather_refs = (refs[:ng], refs[ng + 8:2 * ng + 8]) + tuple(refs[2 * ng + 8:])
            refs = refs[ng:ng + 8]

            @pl.when(pl.program_id(0) == 0)
            def _():
                _gather_start(*gather_refs)
        u_ref, w_ref, qd_ref, kd_ref, attn_ref, tmat_ref, gc_ref, beta_ref = refs
        tri, strict = _tri_masks()
        g, beta, _ = _gate_tiles(cab_ref[...], alog_ref[...], dtb_ref[...])
        gc = _dot_mask(tri, g)
        gct = gc.T
        gc_ref[...] = gc
        beta_ref[...] = beta
        mats, rhs = [], []
        for j in range(C_QK_HEADS):
            js = slice(j * C_DK, (j + 1) * C_DK)
            kf, qf = k_ref[:, js], q_ref[:, js] * scale
            kb, qb = kf.astype(BF16), qf.astype(BF16)
            kk = lax.dot_general(kb, kb, _NT, preferred_element_type=F32)
            qk = lax.dot_general(qb, kb, _NT, preferred_element_type=F32)
            for h in (2 * j, 2 * j + 1):
                hs = slice(h * C_DK, (h + 1) * C_DK)
                gcol, bcol, decay, kkd, attn, glast = _chunk_common(kk, qk, gc, gct, beta, h, tri, strict)
                gexp = jnp.exp(gcol)
                mats.append(kkd * bcol)
                rhs.append(jnp.concatenate([v_ref[:, hs] * bcol, kf * (bcol * gexp)], axis=1))
                qd_ref[:, hs] = (qf * gexp).astype(BF16)
                kd_ref[:, hs] = (kf * jnp.exp(glast - gcol)).astype(BF16)
                attn_ref[:, h * CHUNK:(h + 1) * CHUNK] = attn.astype(BF16)
        for h, (tmat, r) in enumerate(zip(_unit_lower_inverses(mats), rhs)):
            hs = slice(h * C_DK, (h + 1) * C_DK)
            uw = _dot_hi(tmat, r)
            u_ref[:, hs] = uw[:, :C_DK]
            w_ref[:, hs] = uw[:, C_DK:]
            tmat_ref[:, h * CHUNK:(h + 1) * CHUNK] = tmat
        if riding:
            @pl.when(pl.program_id(0) == nc - 1)
            def _():
                _gather_finish(*gather_refs)

    def blk(w):
        return pl.BlockSpec((CHUNK, w), lambda n: (n, 0))

    row = pl.BlockSpec((1, LANES), lambda n: (0, 0))
    big = jax.ShapeDtypeStruct((t, C_VW), F32)
    sq = jax.ShapeDtypeStruct((t, C_V_HEADS * CHUNK), F32)
    tile = jax.ShapeDtypeStruct((t, LANES), F32)
    half = jax.ShapeDtypeStruct((t, C_VW), BF16)
    in_specs = [blk(C_QKW), pl.BlockSpec((CHUNK, C_QKW), lambda n: (n, 1)), blk(C_VW), _cab_spec(), row, row]
    out_specs = [blk(C_VW)] * 4 + [blk(C_V_HEADS * CHUNK)] * 2 + [blk(LANES)] * 2
    out_shape = [big, big, half, half, jax.ShapeDtypeStruct(sq.shape, BF16), sq] + [tile] * 2
    args = [qk, qk, v, u, alog, dtb]
    if riding:
        in_specs += [HBM_SPEC] * ng
        out_specs += [HBM_SPEC] * ng
        out_shape += _gathered_shapes(gather_src)
        args += list(gather_src)
    return pl.pallas_call(
        body, name="delta_prep_gather" if riding else "delta_prep", grid=(nc,),
        in_specs=in_specs, out_specs=out_specs, out_shape=out_shape,
        scratch_shapes=_gather_sems(ng) if riding else [],
        compiler_params=_cparams(("arbitrary",) if riding else ("parallel",)),
    )(*args)


SCAN_SUB = 4


def _delta_scan_call(u, w, qd, kd, attn, gc):
    t = u.shape[0]
    nc = t // CHUNK
    rows = SCAN_SUB * CHUNK

    def body(u_ref, w_ref, qd_ref, kd_ref, attn_ref, gc_ref, o_ref, vn_ref, st_ref, s_ref):
        @pl.when(pl.program_id(0) == 0)
        def _():
            s_ref[...] = jnp.zeros_like(s_ref)

        hss = [slice(h * C_DK, (h + 1) * C_DK) for h in range(C_V_HEADS)]
        states = [s_ref[hs, :] for hs in hss]
        for c in range(SCAN_SUB):
            rs = slice(c * CHUNK, (c + 1) * CHUNK)
            for hs, s in zip(hss, states):
                st_ref[c, hs, :] = s
            sbs = [s.astype(BF16) for s in states]
            vns = [u_ref[rs, hs] - jnp.dot(w_ref[rs, hs].astype(BF16), sb, preferred_element_type=F32)
                   for hs, sb in zip(hss, sbs)]
            qss = [jnp.dot(qd_ref[rs, hs].astype(BF16), sb, preferred_element_type=F32) for hs, sb in zip(hss, sbs)]
            vnbs = [vn.astype(BF16) for vn in vns]
            for h, hs in enumerate(hss):
                vn_ref[rs, hs] = vnbs[h]
                o_ref[rs, hs] = qss[h] + jnp.dot(attn_ref[rs, h * CHUNK:(h + 1) * CHUNK].astype(BF16), vnbs[h],
                                                 preferred_element_type=F32)
            last = (c + 1) * CHUNK - 1
            states = [states[h] * jnp.exp(gc_ref[last:last + 1, h:h + 1])
                      + lax.dot_general(kd_ref[rs, hs].astype(BF16), vnbs[h], _TN, preferred_element_type=F32)
                      for h, hs in enumerate(hss)]
        for hs, s in zip(hss, states):
            s_ref[hs, :] = s

    def blk(wd):
        return pl.BlockSpec((rows, wd), lambda n: (n, 0))

    big = jax.ShapeDtypeStruct((t, C_VW), F32)
    return pl.pallas_call(
        body, name="delta_scan", grid=(nc // SCAN_SUB,),
        in_specs=[blk(C_VW)] * 4 + [blk(C_V_HEADS * CHUNK), blk(LANES)],
        out_specs=[blk(C_VW), blk(C_VW), pl.BlockSpec((SCAN_SUB, C_VW, C_DK), lambda n: (n, 0, 0))],
        out_shape=[big, jax.ShapeDtypeStruct((t, C_VW), BF16), jax.ShapeDtypeStruct((nc, C_VW, C_DK), F32)],
        scratch_shapes=[pltpu.VMEM((C_VW, C_DK), F32)],
        compiler_params=_cparams(("arbitrary",)),
    )(u, w, qd, kd, attn, gc)


def _delta_scan_bwd_call(do, w, qd, kd, attn, gc, vn, st):
    t = do.shape[0]
    nc = t // CHUNK
    rows = SCAN_SUB * CHUNK
    steps = nc // SCAN_SUB

    def body(do_ref, w_ref, qd_ref, kd_ref, attn_ref, gc_ref, vn_ref, st_ref,
             du_ref, dw_ref, dqd_ref, dkd_ref, dattn_ref, dgl_ref, ds_ref):
        @pl.when(pl.program_id(0) == 0)
        def _():
            ds_ref[...] = jnp.zeros_like(ds_ref)

        tri, _ = _tri_masks()
        row = lax.broadcasted_iota(jnp.int32, (8, LANES), 0)
        lane = lax.broadcasted_iota(jnp.int32, (8, LANES), 1)
        hss = [slice(h * C_DK, (h + 1) * C_DK) for h in range(C_V_HEADS)]
        css = [slice(h * CHUNK, (h + 1) * CHUNK) for h in range(C_V_HEADS)]

        def dg(a, b, dims):
            return lax.dot_general(a, b, dims, preferred_element_type=F32)

        dsps = [ds_ref[hs, :] for hs in hss]
        for c in reversed(range(SCAN_SUB)):
            rs = slice(c * CHUNK, (c + 1) * CHUNK)
            dgl = jnp.zeros((8, LANES), F32)
            ss = [st_ref[c, hs, :] for hs in hss]
            sbs = [s.astype(BF16) for s in ss]
            dspbs = [d.astype(BF16) for d in dsps]
            dobs = [do_ref[rs, hs].astype(BF16) for hs in hss]
            vnbs = [vn_ref[rs, hs].astype(BF16) for hs in hss]
            dvns = [dg(attn_ref[rs, cs].astype(BF16), dob, _TN) + dg(kd_ref[rs, hs].astype(BF16), dspb, _NN)
                    for hs, cs, dob, dspb in zip(hss, css, dobs, dspbs)]
            for h, hs in enumerate(hss):
                dqd_ref[rs, hs] = dg(dobs[h], sbs[h], _NT)
                dkd_ref[rs, hs] = dg(vnbs[h], dspbs[h], _NT)
                dattn_ref[rs, css[h]] = jnp.where(tri, dg(dobs[h], vnbs[h], _NT), 0.0)
            dvnbs = [d.astype(BF16) for d in dvns]
            for h, hs in enumerate(hss):
                du_ref[rs, hs] = dvns[h]
                dw_ref[rs, hs] = -dg(dvnbs[h], sbs[h], _NT)
                tot = jnp.sum(jnp.sum(dsps[h] * ss[h], axis=0, keepdims=True), axis=1, keepdims=True)
                dgl = jnp.where(jnp.logical_and(row == 0, lane == h), tot, dgl)
            dgl_ref[c * 8:(c + 1) * 8, :] = dgl
            last = (c + 1) * CHUNK - 1
            dsps = [dg(qd_ref[rs, hs].astype(BF16), dobs[h], _TN) + jnp.exp(gc_ref[last:last + 1, h:h + 1]) * dsps[h]
                    - dg(w_ref[rs, hs].astype(BF16), dvnbs[h], _TN) for h, hs in enumerate(hss)]
        for hs, d in zip(hss, dsps):
            ds_ref[hs, :] = d

    def blk(wd):
        return pl.BlockSpec((rows, wd), lambda n: (steps - 1 - n, 0))

    big = jax.ShapeDtypeStruct((t, C_VW), F32)
    return pl.pallas_call(
        body, name="delta_scan_bwd", grid=(steps,),
        in_specs=[blk(C_VW)] * 4 + [blk(C_V_HEADS * CHUNK), blk(LANES), blk(C_VW),
                                    pl.BlockSpec((SCAN_SUB, C_VW, C_DK), lambda n: (steps - 1 - n, 0, 0))],
        out_specs=[blk(C_VW)] * 4 + [blk(C_V_HEADS * CHUNK),
                                     pl.BlockSpec((SCAN_SUB * 8, LANES), lambda n: (steps - 1 - n, 0))],
        out_shape=[big] * 4 + [jax.ShapeDtypeStruct((t, C_V_HEADS * CHUNK), F32),
                               jax.ShapeDtypeStruct((nc * 8, LANES), F32)],
        scratch_shapes=[pltpu.VMEM((C_VW, C_DK), F32)],
        compiler_params=_cparams(("arbitrary",)),
    )(do, w, qd, kd, attn, gc, vn, st)


PREP_SUB = 2


def _delta_prep_bwd_call(qk, v, proj, alog, dtb, tmat, u, w, gc, beta, du, dw, dqd, dkd, dattn, dgl, du_buf):
    t = qk.shape[0]
    extra, extra_specs, aliases = _du_operands(du_buf, 17)
    nc = t // CHUNK
    rows = PREP_SUB * CHUNK
    scale = C_DK ** -0.5

    def body(q_ref, k_ref, v_ref, cab_ref, alog_ref, dtb_ref, tmat_ref, u_ref, w_ref, gc_ref, beta_ref,
             du_ref, dw_ref, dqd_ref, dkd_ref, dattn_ref, dgl_ref, *outs):
        dcab_ref, dqk_ref, dv_ref, dpar_ref = outs[len(extra):]
        tri, strict = _tri_masks()
        ones = jnp.ones((CHUNK, LANES), F32)
        lane = lax.broadcasted_iota(jnp.int32, (CHUNK, LANES), 1)
        rowi = lax.broadcasted_iota(jnp.int32, (CHUNK, 1), 0)
        subs = range(PREP_SUB)
        rss = [slice(c * CHUNK, (c + 1) * CHUNK) for c in subs]
        betas = [beta_ref[rs, :] for rs in rss]

        def dot(x, y, dims=_NN):
            return lax.dot_general(x, y, dims, preferred_element_type=F32)

        heads = []
        for c, rs in zip(subs, rss):
            gc = gc_ref[rs, :]
            gct = gc.T
            for j in range(C_QK_HEADS):
                js = slice(j * C_DK, (j + 1) * C_DK)
                kf, qf = k_ref[rs, js], q_ref[rs, js] * scale
                kb, qb = kf.astype(BF16), qf.astype(BF16)
                kk = dot(kb, kb, _NT)
                qk = dot(qb, kb, _NT)
                for h in (2 * j, 2 * j + 1):
                    heads.append((c, rs, h, kf, qf, kb, qb) + _chunk_common(kk, qk, gc, gct, betas[c], h, tri, strict))

        def cols(h):
            return slice(h * C_DK, (h + 1) * C_DK)

        def sq(h):
            return slice(h * CHUNK, (h + 1) * CHUNK)

        dvks = [_dot_hi(tmat_ref[hd[1], sq(hd[2])],
                        jnp.concatenate([du_ref[hd[1], cols(hd[2])], dw_ref[hd[1], cols(hd[2])]], axis=1), _TN)
                for hd in heads]
        das = [-jnp.where(strict, _dot_hi(dvk, jnp.concatenate([u_ref[hd[1], cols(hd[2])], w_ref[hd[1], cols(hd[2])]],
                                                               axis=1), _NT), 0.0)
               for hd, dvk in zip(heads, dvks)]
        pre = []
        for (c, rs, h, kf, qf, kb, qb, gcol, bcol, decay, kkd, attn, glast), da in zip(heads, das):
            dattn_h = dattn_ref[rs, sq(h)]
            pre.append(((da * decay * bcol).astype(BF16), (dattn_h * decay).astype(BF16),
                        da * kkd * bcol + dattn_h * attn))
        mms = [(dot(dkk, hd[5]), dot(dkk, hd[5], _TN), dot(dqk, hd[6], _TN), dot(dqk, hd[5]),
                _dot_mask(ones, e, _TN, mask_left=False))
               for hd, (dkk, dqk, e) in zip(heads, pre)]
        dq_parts, dk_parts = {}, {}
        dgc_tiles = [jnp.zeros((CHUNK, LANES), F32) for _ in subs]
        db_tiles = [jnp.zeros((CHUNK, LANES), F32) for _ in subs]
        for (c, rs, h, kf, qf, kb, qb, gcol, bcol, decay, kkd, attn, glast), dvk, da, (_, _, e), mm in zip(
                heads, dvks, das, pre, mms):
            hs = cols(h)
            gexp = jnp.exp(gcol)
            fdec = jnp.exp(glast - gcol)
            dvb, dkb = dvk[:, :C_DK], dvk[:, C_DK:]
            dgc = jnp.sum(e, axis=1, keepdims=True) - mm[4][:, :1]
            dk_parts[c, h] = mm[0] + mm[1] + mm[2] + dkb * (bcol * gexp) + dkd_ref[rs, hs] * fdec
            dq_parts[c, h] = mm[3] + dqd_ref[rs, hs] * gexp
            dv_ref[rs, hs] = dvb * bcol
            s_kb = jnp.sum(dkb * kf, axis=1, keepdims=True)
            db = (jnp.sum(da * kkd, axis=1, keepdims=True) + jnp.sum(dvb * v_ref[rs, hs], axis=1, keepdims=True)
                  + s_kb * gexp)
            rho = jnp.sum(dkd_ref[rs, hs] * kf, axis=1, keepdims=True) * fdec
            dgc = (dgc + s_kb * bcol * gexp + jnp.sum(dqd_ref[rs, hs] * qf, axis=1, keepdims=True) * gexp - rho)
            last = jnp.sum(rho, axis=0, keepdims=True) + dgl_ref[c * 8:c * 8 + 1, h:h + 1] * jnp.exp(glast)
            dgc = dgc + jnp.where(rowi == CHUNK - 1, last, 0.0)
            dgc_tiles[c] = jnp.where(lane == h, dgc, dgc_tiles[c])
            db_tiles[c] = jnp.where(lane == h, db, db_tiles[c])
        alog = alog_ref[...]
        row8 = lax.broadcasted_iota(jnp.int32, (8, LANES), 0)
        par = jnp.zeros((8, LANES), F32)
        for c, rs in zip(subs, rss):
            for j in range(C_QK_HEADS):
                dqk_ref[rs, j * C_DK:(j + 1) * C_DK] = (dq_parts[c, 2 * j] + dq_parts[c, 2 * j + 1]) * scale
                dqk_ref[rs, C_QKW + j * C_DK:C_QKW + (j + 1) * C_DK] = dk_parts[c, 2 * j] + dk_parts[c, 2 * j + 1]
            dg = _dot_mask(jnp.logical_not(strict), dgc_tiles[c])
            g, _, gate_pre = _gate_tiles(cab_ref[rs, :], alog, dtb_ref[...])
            dca = dg * (-jnp.exp(alog)) * jax.nn.sigmoid(gate_pre)
            beta = betas[c]
            dcab_ref[rs, :LANES] = (dca + pltpu.roll(db_tiles[c] * beta * (1.0 - beta), C_V_HEADS, 1)).astype(BF16)
            dcab_ref[rs, LANES:] = jnp.zeros((CHUNK, D_IN_PAD - COL["c_ab"] - LANES), BF16)
            par = par + jnp.where(row8 == 0, jnp.sum(dg * g, axis=0, keepdims=True),
                                  jnp.where(row8 == 1, jnp.sum(dca, axis=0, keepdims=True), 0.0))

        @pl.when(pl.program_id(0) == 0)
        def _():
            dpar_ref[...] = par

        @pl.when(pl.program_id(0) > 0)
        def _():
            dpar_ref[...] += par

    def blk(wd):
        return pl.BlockSpec((rows, wd), lambda n: (n, 0))

    row = pl.BlockSpec((1, LANES), lambda n: (0, 0))
    sqs = blk(C_V_HEADS * CHUNK)
    tail = D_IN_PAD - COL["c_ab"]
    assert COL["c_ab"] % tail == 0 and nc % PREP_SUB == 0
    return pl.pallas_call(
        body, name="delta_prep_bwd", grid=(nc // PREP_SUB,),
        in_specs=[blk(C_QKW), pl.BlockSpec((rows, C_QKW), lambda n: (n, 1)), blk(C_VW),
                  pl.BlockSpec((rows, LANES), lambda n: (n, COL["c_ab"] // LANES)), row, row, sqs,
                  blk(C_VW), blk(C_VW),
                  blk(LANES), blk(LANES), blk(C_VW), blk(C_VW), blk(C_VW), blk(C_VW), sqs,
                  pl.BlockSpec((PREP_SUB * 8, LANES), lambda n: (n, 0))] + extra_specs,
        out_specs=[pl.BlockSpec((rows, tail), lambda n: (n, COL["c_ab"] // tail)),
                   blk(2 * C_QKW), blk(C_VW), pl.BlockSpec((8, LANES), lambda n: (0, 0))],
        out_shape=[_du_shape(t), jax.ShapeDtypeStruct((t, 2 * C_QKW), F32),
                   jax.ShapeDtypeStruct((t, C_VW), F32), jax.ShapeDtypeStruct((8, LANES), F32)],
        input_output_aliases=aliases,
        compiler_params=_cparams(("arbitrary",)),
    )(qk, qk, v, proj, alog, dtb, tmat, u, w, gc, beta, du, dw, dqd, dkd, dattn, dgl, *extra)


def _z_spec(tq):
    return pl.BlockSpec((tq, C_VW), lambda i: (i, COL["c_z"] // C_VW))


def _gated_norm_fwd_call(o, u, gain, *, tq=512):
    t, w = o.shape

    def body(o_ref, z_ref, g_ref, y_ref):
        act, _ = _silu_parts(z_ref[...])
        gv = g_ref[...]
        for h in range(C_V_HEADS):
            hs = slice(h * C_DK, (h + 1) * C_DK)
            ov = o_ref[:, hs]
            r = lax.rsqrt(jnp.mean(ov * ov, axis=1, keepdims=True) + EPS)
            y_ref[:, hs] = ov * r * gv * act[:, hs]

    blk = pl.BlockSpec((tq, w), lambda i: (i, 0))
    return pl.pallas_call(
        body, name="gated_norm_fwd", grid=(t // tq,),
        in_specs=[blk, _z_spec(tq), pl.BlockSpec((1, C_DK), lambda i: (0, 0))], out_specs=blk,
        out_shape=jax.ShapeDtypeStruct((t, w), F32),
        compiler_params=_cparams(("parallel",)),
    )(o, u, gain)


def _gated_norm_bwd_call(o, u, gain, dy, du_buf, *, tq=512):
    t, w = o.shape
    nt = t // tq
    extra, extra_specs, aliases = _du_operands(du_buf, 4)

    def body(o_ref, z_ref, g_ref, dy_ref, *refs):
        dz_ref, do_ref, dg_ref, acc_ref = refs[len(extra):]
        i = pl.program_id(0)
        act, dact = _silu_parts(z_ref[...])
        gv = g_ref[...]
        part = jnp.zeros((8, C_DK), F32)
        for h in range(C_V_HEADS):
            hs = slice(h * C_DK, (h + 1) * C_DK)
            ov = o_ref[:, hs]
            r = lax.rsqrt(jnp.mean(ov * ov, axis=1, keepdims=True) + EPS)
            xh = ov * r
            dyv = dy_ref[:, hs]
            dn = dyv * act[:, hs]
            dz_ref[:, hs] = (dyv * xh * gv * dact[:, hs]).astype(BF16)
            dxh = dn * gv
            do_ref[:, hs] = r * (dxh - xh * jnp.mean(dxh * xh, axis=1, keepdims=True))
            part = part + jnp.sum((dn * xh).reshape(tq // 8, 8, C_DK), axis=0)

        @pl.when(i == 0)
        def _():
            acc_ref[...] = part

        @pl.when(i > 0)
        def _():
            acc_ref[...] += part

        @pl.when(i == nt - 1)
        def _():
            dg_ref[...] = jnp.sum(acc_ref[...], axis=0, keepdims=True)

    blk = pl.BlockSpec((tq, w), lambda i: (i, 0))
    grow = pl.BlockSpec((1, C_DK), lambda i: (0, 0))
    return pl.pallas_call(
        body, name="gated_norm_bwd", grid=(nt,),
        in_specs=[blk, _z_spec(tq), grow, blk] + extra_specs, out_specs=[_z_spec(tq), blk, grow],
        out_shape=[_du_shape(t), jax.ShapeDtypeStruct((t, w), F32), jax.ShapeDtypeStruct((1, C_DK), F32)],
        scratch_shapes=[pltpu.VMEM((8, C_DK), F32)],
        input_output_aliases=aliases,
        compiler_params=_cparams(("arbitrary",)),
    )(o, u, gain, dy, *extra)


def _gate_specs(tq):
    return [pl.BlockSpec((tq, D_MODEL), lambda i, j=j: (i, j)) for j in range(3)]


def _merge_fwd_call(ps, u, *, tq=512):
    t, w = ps[0].shape

    def body(p0, p1, p2, g0, g1, g2, y_ref):
        y_ref[...] = (jax.nn.sigmoid(g0[...]) * p0[...] + jax.nn.sigmoid(g1[...]) * p1[...]
                      + jax.nn.sigmoid(g2[...]) * p2[...]).astype(BF16)

    blk = pl.BlockSpec((tq, w), lambda i: (i, 0))
    return pl.pallas_call(
        body, name="merge_fwd", grid=(t // tq,), in_specs=[blk] * 3 + _gate_specs(tq), out_specs=blk,
        out_shape=jax.ShapeDtypeStruct((t, w), BF16),
        compiler_params=_cparams(("parallel",)),
    )(*ps, u, u, u)


def _merge_bwd_call(ps, u, dy, *, tq=256):
    t, w = dy.shape

    def body(p0, p1, p2, g0, g1, g2, dy_ref, dg_ref, dp0, dp1, dp2):
        dyv = dy_ref[...]
        for j, (p, g, dp) in enumerate(((p0, g0, dp0), (p1, g1, dp1), (p2, g2, dp2))):
            sig = jax.nn.sigmoid(g[...])
            dp[...] = (dyv * sig).astype(BF16)
            dg_ref[:, j * w:(j + 1) * w] = (dyv * p[...] * sig * (1.0 - sig)).astype(BF16)

    blk = pl.BlockSpec((tq, w), lambda i: (i, 0))
    small = jax.ShapeDtypeStruct((t, w), BF16)
    return pl.pallas_call(
        body, name="merge_bwd", grid=(t // tq,), in_specs=[blk] * 3 + _gate_specs(tq) + [blk],
        out_specs=[pl.BlockSpec((tq, 3 * w), lambda i: (i, 0))] + [blk] * 3,
        out_shape=[_du_shape(t)] + [small] * 3,
        compiler_params=_cparams(("parallel",)),
    )(*ps, u, u, u, dy)


Q_SCALE = HEAD_DIM ** -0.5
A_PARTS = ((COL["a_q"], A_W, True, Q_SCALE), (COL["a_k"], A_W, True, 1.0), (COL["a_v"], A_W, False, 1.0))
B_PARTS = ((COL["b_q"], A_W, True, Q_SCALE), (COL["b_k"], B_KVW, True, 1.0), (COL["b_v"], B_KVW, False, 1.0))
BRANCHES = ("w_branch_a", "w_branch_b", "w_branch_c")


def _layer_fwd(x, tabs, p, w_in_b, rest, rest_of, gather_src=None):
    h = _rms_fwd_call(x, p["norm_mix"], name="rms_mix_fwd", out_dtype=BF16)
    if rest[0] == "ride":
        u, (packed,) = _mm(h, w_in_b, bias=p["b_in"], tn=IN_TN, gather_src=[rest[1]], name="in_proj_fwd_gather")
    else:
        u, packed = _mm(h, w_in_b, bias=p["b_in"], tn=IN_TN, name="in_proj_fwd"), rest[1]
    wb, conv_w = rest_of(packed)
    wb = dict(wb, w_in=w_in_b)
    p = dict(p, conv_w=conv_w)
    qkv_a = _rope_gather_call(u, tabs, A_PARTS, dils=A_DILS, name="rope_a_fwd")
    os_, lses = zip(*[_attn_fwd(cfg, qkv2, None) for cfg, qkv2 in zip(ATTN_A_CFGS, qkv_a)])
    ya = _combine_fwd_call(os_, lses, A_DILS)
    qkv_b, = _rope_gather_call(u, tabs, B_PARTS, name="rope_b_fwd")
    yb, lse_b = _attn_fwd(ATTN_B_CFG, qkv_b, p["sinks"])
    zc, qk, v = _conv_prep_fwd_call(u, p["conv_w"])
    uu, ww, qd, kd, attn, tmat, gc, beta, *gathered = _delta_prep_call(qk, v, u, p["a_log"], p["dt_bias"], gather_src)
    o, vn, st = _delta_scan_call(uu, ww, qd, kd, attn, gc)
    yc = _gated_norm_fwd_call(o, u, p["c_norm"])
    ys = (ya, yb, yc)
    ps = tuple(_mm(y, wb[n], name="branch_fwd") for y, n in zip(ys, BRANCHES))
    merged = _merge_fwd_call(ps, u)
    x1 = _mm(merged, wb["w_out"], add=x, name="out_proj_fwd")
    h2 = _rms_fwd_call(x1, p["norm_ffn"], name="rms_ffn_fwd", out_dtype=BF16)
    pre, act = _mm(h2, wb["w_ff1"], relu2_out=True, name="ffn_up")
    x2 = _mm(act, wb["w_ff2"], add=x1, name="ffn_down")
    saved = dict(x=x, h=h, u=u, qkv_a=qkv_a, os_=os_, lses=lses, b_saved=(qkv_b, yb, lse_b),
                 zc=zc, qk=qk, v=v, delta=(tmat, uu, ww, gc, beta, qd, kd, attn, vn, st), o=o, ys=ys, ps=ps,
                 merged=merged, x1=x1, h2=h2, pre=pre, act=act, p=p, wb=wb)
    return x2, saved, (gathered if gathered else None)


def _layer_bwd(s, dx2, tabs):
    g, p, wb = {}, s["p"], s["wb"]
    t = dx2.shape[0]
    dpre = _mm(dx2, wb["w_ff2"], tb=True, mul_drelu2=s["pre"], out_dtype=BF16, name="ffn_dpre")
    g["w_ff2"] = _mm(s["act"], dx2, ta=True, tk=1024, name="ffn_dw2")
    g["w_ff1"] = _mm(s["h2"], dpre, ta=True, tk=1024, name="ffn_dw1")
    dh2 = _mm(dpre, wb["w_ff1"], tb=True, name="ffn_dh")
    dx1, g["norm_ffn"] = _rms_bwd_call(s["x1"], p["norm_ffn"], dh2, add=dx2, name="rms_ffn_bwd")
    dmerged = _mm(dx1, wb["w_out"], tb=True, name="out_proj_da")
    g["w_out"] = _mm(s["merged"], dx1, ta=True, tk=1024, name="out_proj_dw")
    du, *dps = _merge_bwd_call(s["ps"], s["u"], dmerged)
    dys = []
    for y, dp, n in zip(s["ys"], dps, BRANCHES):
        dys.append(_mm(dp, wb[n], tb=True, name="branch_da"))
        g[n] = _mm(y, dp, ta=True, tk=1024, name="branch_dw")
    dya, dyb, dyc = dys
    tmat, uu, ww, gc, beta, qd, kd, attn, vn, st = s["delta"]
    du, do, g["c_norm"] = _gated_norm_bwd_call(s["o"], s["u"], p["c_norm"], dyc, du)
    ddu, ddw, dqd, dkd, dattn, dgl = _delta_scan_bwd_call(do, ww, qd, kd, attn, gc, vn, st)
    du, dqk, dv, dpar = _delta_prep_bwd_call(s["qk"], s["v"], s["u"], p["a_log"], p["dt_bias"], tmat, uu, ww, gc,
                                             beta, ddu, ddw, dqd, dkd, dattn, dgl, du)
    g["a_log"], g["dt_bias"] = dpar[0:1], dpar[1:2]
    dzc = _conv_prep_dz_call(s["zc"], dqk, dv)
    du, dconv = _conv_bwd_call(s["u"], dzc, p["conv_w"], du)
    g["conv_w"] = dconv[:C_CONV]
    no_dlse = jnp.zeros((t, LANES), F32)
    dq, dk, dv_b, dsink = _attn_bwd(ATTN_B_CFG, *s["b_saved"], p["sinks"], dyb, no_dlse)
    g["sinks"] = dsink[0, :p["sinks"].shape[0]]
    du = _rope_scatter_call(du, t, [([dq], A_W, True, Q_SCALE)], COL["b_q"], tabs, name="rope_bq_bwd")
    du = _rope_scatter_call(du, t, [([dk], B_KVW, True, 1.0), ([dv_b], B_KVW, False, 1.0)], COL["b_k"], tabs,
                            name="rope_bkv_bwd")
    *dos, dl0, dl1, dl2 = _combine_bwd_call(s["os_"], s["lses"], dya, A_DILS)
    grads_a = [_attn_bwd(cfg, qkv2, o2, lse2, None, do2, dl2_)[:3]
               for cfg, qkv2, o2, lse2, do2, dl2_ in zip(ATTN_A_CFGS, s["qkv_a"], s["os_"], s["lses"], dos,
                                                         (dl0, dl1, dl2))]
    dqs, dks, dvs = zip(*grads_a)
    du = _rope_scatter_call(du, t, [(list(dqs), A_W, True, Q_SCALE), (list(dks), A_W, True, 1.0),
                                    (list(dvs), A_W, False, 1.0)],
                            COL["a_q"], tabs, dils=A_DILS, name="rope_a_bwd")
    dh = _mm(du, wb["w_in"], tb=True, tk=IN_TN, name="in_proj_da")
    g["w_in"], g["b_in"] = _mm(s["h"], du, ta=True, b_colsum=True, tn=IN_TN, tk=1024, name="in_proj_dw")
    dx, g["norm_mix"] = _rms_bwd_call(s["x"], p["norm_mix"], dh, add=dx1, name="rms_mix_bwd")
    return dx, g


def _local_step(x, params, w_in_first, rest_first, payload_of_layer, w_in_of, rest_of, tabs, tgt):
    saves = []
    w_in_blocks, rest = w_in_first, rest_first
    for layer in range(DEPTH):
        p = {n: w[layer] for n, w in params.items() if n != "norm_final"}
        nxt = payload_of_layer(layer + 1) if layer + 1 < DEPTH else None
        x, s, gathered = _layer_fwd(x, tabs, p, w_in_of(w_in_blocks), rest, rest_of, nxt)
        saves.append(s)
        if gathered is not None:
            w_in_blocks, rest = gathered[0], ("ready", gathered[1])
    loss, dx, dfinal = _loss_call(x, params["norm_final"], tgt)
    per_layer = []
    for s in reversed(saves):
        dx, g = _layer_bwd(s, dx, tabs)
        per_layer.append(g)
    per_layer.reverse()
    grads = {n: jnp.stack([g[n] for g in per_layer]) for n in per_layer[0]}
    grads["norm_final"] = dfinal
    return loss, dx, grads


def _in_cols_to_kernel(w):
    lead = w.shape[:-1]
    parts, pos = [], 0
    for _, start, width, ref_start in IN_LAYOUT:
        if start > pos:
            parts.append(jnp.zeros(lead + (start - pos,), w.dtype))
        parts.append(w[..., ref_start:ref_start + width])
        pos = start + width
    parts.append(jnp.zeros(lead + (D_IN_PAD - pos,), w.dtype))
    return jnp.concatenate(parts, axis=-1)


def _in_cols_to_reference(w):
    by_ref = sorted(IN_LAYOUT, key=lambda e: e[3])
    return jnp.concatenate([w[..., start:start + width] for _, start, width, _ in by_ref], axis=-1)


W_IN_SHARD = 8464 // N_DEV


def _w_in_from_shards(blocks):
    lead = blocks.shape[1:-1]
    parts, pos = [], 0
    for _, start, width, ref_start in IN_LAYOUT:
        if start > pos:
            parts.append(jnp.zeros(lead + (start - pos,), blocks.dtype))
        col = ref_start
        while col < ref_start + width:
            d, l = divmod(col, W_IN_SHARD)
            n = min(W_IN_SHARD - l, ref_start + width - col)
            parts.append(blocks[d, ..., l:l + n])
            col += n
        pos = start + width
    parts.append(jnp.zeros(lead + (D_IN_PAD - pos,), blocks.dtype))
    return jnp.concatenate(parts, axis=-1)


def _w_in_to_shards(g):
    by_ref = sorted(IN_LAYOUT, key=lambda e: e[3])
    blocks = []
    for d in range(N_DEV):
        lo, hi = d * W_IN_SHARD, (d + 1) * W_IN_SHARD
        parts = []
        for _, start, width, ref_start in by_ref:
            a, b = max(lo, ref_start), min(hi, ref_start + width)
            if a < b:
                parts.append(g[..., start + a - ref_start:start + b - ref_start])
        blocks.append(jnp.concatenate(parts, axis=-1))
    return jnp.stack(blocks)


def _pad_lanes(v):
    return jnp.pad(v, ((0, 0), (0, LANES - v.shape[1])))[:, None, :]


BIG = (("w_in", 2), ("conv_w", 2), ("w_branch_a", 2), ("w_branch_b", 2), ("w_branch_c", 1), ("w_out", 1),
       ("w_ff1", 2), ("w_ff2", 1))
SMALL = ("norm_mix", "b_in", "a_log", "dt_bias", "sinks", "c_norm", "norm_ffn", "norm_final")
WEIGHTS = ("norm_mix", "w_in", "b_in", "conv_w", "a_log", "dt_bias", "sinks", "c_norm", "w_branch_a",
           "w_branch_b", "w_branch_c", "w_out", "norm_ffn", "w_ff1", "w_ff2", "norm_final")
MATMUL_WEIGHTS = ("w_in", "w_branch_a", "w_branch_b", "w_branch_c", "w_out", "w_ff1", "w_ff2")
PACK_ROWS = 1024
ROW_ALIGN = 16


def _seg_rows(n):
    return -(-n // (LANES * ROW_ALIGN)) * ROW_ALIGN


def _pack(arrays, lead=0):
    parts = []
    for a in arrays:
        lead_shape = a.shape[:lead]
        n = math.prod(a.shape[lead:])
        rows = _seg_rows(n)
        if rows * LANES != n:
            a = jnp.pad(a.reshape(lead_shape + (n,)), [(0, 0)] * lead + [(0, rows * LANES - n)])
        parts.append(a.reshape(lead_shape + (rows, LANES)))
    total = sum(p.shape[lead] for p in parts)
    padded = -(-total // PACK_ROWS) * PACK_ROWS
    if padded > total:
        parts.append(jnp.zeros(parts[0].shape[:lead] + (padded - total, LANES), parts[0].dtype))
    return jnp.concatenate(parts, axis=lead)


def _unpack(buf, shapes):
    lead = buf.shape[:-2]
    out, pos = [], 0
    for shp in shapes:
        n = math.prod(shp)
        rows = _seg_rows(n)
        seg = buf[..., pos:pos + rows, :]
        if rows * LANES != n:
            seg = seg.reshape(lead + (rows * LANES,))[..., :n]
        out.append(seg.reshape(lead + tuple(shp)))
        pos += rows
    return out


def _shards_to_full(blocks, axis):
    moved = jnp.moveaxis(blocks, 0, axis)
    shp = list(blocks.shape[1:])
    shp[axis] = shp[axis] * N_DEV
    return moved.reshape(shp)


def _full_to_shards(full, axis):
    shp = list(full.shape)
    shp[axis:axis + 1] = [N_DEV, shp[axis] // N_DEV]
    return jnp.moveaxis(full.reshape(shp), axis, 0)


def _my_place():
    return lax.axis_index("x"), lax.axis_index("y"), lax.axis_index("c")


def _slot(x, y, c):
    return 4 * x + 2 * y + c


GATHER_COPIES = 7


def _gather_plan(x_ref, out_ref, send_sems, recv_sems, local_sem, base):
    x, y, c = _my_place()
    me, sibling = (x, y, c), (x, y, 1 - c)
    chips = [(1 - x, y), (x, 1 - y), (1 - x, 1 - y)]

    def copy(k, blk, to, src=None):
        dst = out_ref.at[_slot(*blk)]
        return pltpu.make_async_remote_copy(
            src_ref=dst if src is None else src, dst_ref=dst,
            send_sem=send_sems.at[base + k], recv_sem=recv_sems.at[base + k], device_id=to, device_id_type=MESH_ID)

    def own():
        mine = pltpu.make_async_copy(x_ref, out_ref.at[_slot(*me)], local_sem)
        return mine, [copy(0, me, sibling, src=x_ref)] + [copy(1 + j, me, (*chip, c), src=x_ref)
                                                          for j, chip in enumerate(chips)]

    return copy, own, me, sibling, chips, c


def _gather_plans(srcs, outs, send_sems, recv_sems, local_sems):
    return [_gather_plan(x_ref, out_ref, send_sems, recv_sems, local_sems.at[i], GATHER_COPIES * i)
            for i, (x_ref, out_ref) in enumerate(zip(srcs, outs))]


def _gather_start(srcs, outs, *sems):
    for _, own, *_ in _gather_plans(srcs, outs, *sems):
        mine, first = own()
        mine.start()
        for cp in first:
            cp.start()


def _gather_finish(srcs, outs, *sems):
    plans = _gather_plans(srcs, outs, *sems)
    passed_all = []
    for copy, own, me, sibling, chips, c in plans:
        passed = [copy(4 + j, (*chip, c), sibling) for j, chip in enumerate(chips)]
        for j, chip in enumerate(chips):
            copy(1 + j, (*chip, c), me).wait_recv()
            passed[j].start()
        passed_all.append(passed)
    for (copy, own, me, sibling, chips, c), passed in zip(plans, passed_all):
        copy(0, sibling, me).wait_recv()
        for j, chip in enumerate(chips):
            copy(4 + j, (*chip, 1 - c), me).wait_recv()
        mine, first = own()
        for cp in first + passed:
            cp.wait_send()
        mine.wait()


def _gather_sems(n):
    return [pltpu.SemaphoreType.DMA((GATHER_COPIES * n,)), pltpu.SemaphoreType.DMA((GATHER_COPIES * n,)),
            pltpu.SemaphoreType.DMA((n,))]


def _gathered_shapes(blocks):
    return [jax.ShapeDtypeStruct((N_DEV,) + b.shape, b.dtype) for b in blocks]


def _all_gather(blocks, *, name):
    n = len(blocks)

    def body(*refs):
        srcs, outs, sems = refs[:n], refs[n:2 * n], refs[2 * n:]
        _gather_start(srcs, outs, *sems)
        _gather_finish(srcs, outs, *sems)

    return pl.pallas_call(
        body, name=name, out_shape=_gathered_shapes(blocks),
        in_specs=[HBM_SPEC] * n, out_specs=[HBM_SPEC] * n,
        scratch_shapes=_gather_sems(n),
    )(*blocks)


N_CHIP = N_DEV // 2


def _swap_with_sibling(blocks, *, name):
    n = len(blocks)

    def body(*refs):
        srcs, outs, send_sems, recv_sems = refs[:n], refs[n:2 * n], refs[2 * n], refs[2 * n + 1]
        x, y, c = _my_place()
        copies = [pltpu.make_async_remote_copy(src_ref=g_ref, dst_ref=out_ref, send_sem=send_sems.at[i],
                                               recv_sem=recv_sems.at[i], device_id=(x, y, 1 - c),
                                               device_id_type=MESH_ID)
                  for i, (g_ref, out_ref) in enumerate(zip(srcs, outs))]
        for cp in copies:
            cp.start()
        for cp in copies:
            cp.wait_recv()
        for cp in copies:
            cp.wait_send()

    return pl.pallas_call(
        body, name=name,
        out_shape=[jax.ShapeDtypeStruct(b.shape, b.dtype) for b in blocks],
        in_specs=[HBM_SPEC] * n, out_specs=[HBM_SPEC] * n,
        scratch_shapes=[pltpu.SemaphoreType.DMA((n,)), pltpu.SemaphoreType.DMA((n,))],
    )(*blocks)


def _chip_all_to_all(blocks, *, name):
    n = len(blocks)
    peers = N_CHIP - 1

    def body(*refs):
        srcs, outs = refs[:n], refs[n:2 * n]
        send_sems, recv_sems, local_sems = refs[2 * n:]
        x, y, c = _my_place()
        mine_slot = 2 * x + y
        locals_, copies = [], []
        for i, (g_ref, out_ref) in enumerate(zip(srcs, outs)):
            locals_.append(pltpu.make_async_copy(g_ref.at[mine_slot], out_ref.at[mine_slot], local_sems.at[i]))
            for k in range(1, N_CHIP):
                px, py = x ^ (k >> 1), y ^ (k & 1)
                copies.append(pltpu.make_async_remote_copy(
                    src_ref=g_ref.at[2 * px + py], dst_ref=out_ref.at[mine_slot],
                    send_sem=send_sems.at[peers * i + k - 1], recv_sem=recv_sems.at[peers * i + k - 1],
                    device_id=(px, py, c), device_id_type=MESH_ID))
        for cp in locals_ + copies:
            cp.start()
        for cp in copies:
            cp.wait_recv()
        for cp in copies:
            cp.wait_send()
        for cp in locals_:
            cp.wait()

    return pl.pallas_call(
        body, name=name,
        out_shape=[jax.ShapeDtypeStruct(b.shape, b.dtype) for b in blocks],
        in_specs=[HBM_SPEC] * n, out_specs=[HBM_SPEC] * n,
        scratch_shapes=[pltpu.SemaphoreType.DMA((peers * n,)), pltpu.SemaphoreType.DMA((peers * n,)),
                        pltpu.SemaphoreType.DMA((n,))],
    )(*blocks)


def _block_rows(rows, cols):
    tr = max(8, min(rows, PACK_ROWS * LANES // (-(-cols // LANES) * LANES) // 8 * 8))
    while rows % tr:
        tr -= 8
    return tr


def _add_bf16_call(a, b, *, name):
    n, rows, cols = a.shape
    tr = _block_rows(rows, cols)

    def body(a_ref, b_ref, o_ref):
        o_ref[...] = (a_ref[...].astype(F32) + b_ref[...].astype(F32)).astype(BF16)

    blk = pl.BlockSpec((n, tr, cols), lambda i: (0, i, 0))
    return pl.pallas_call(
        body, name=name, grid=(rows // tr,), in_specs=[blk, blk], out_specs=blk,
        out_shape=jax.ShapeDtypeStruct(a.shape, BF16),
        compiler_params=_cparams(("parallel",)),
    )(a, b)


def _adamw_call(parts, w, m, v, *, name):
    rows, cols = w.shape
    tr = _block_rows(rows, cols)
    n_parts = parts.shape[0]

    def body(p_ref, w_ref, m_ref, v_ref, g_ref, d_ref, nm_ref, nv_ref):
        g = p_ref[0].astype(F32)
        for s in range(1, n_parts):
            g = g + p_ref[s].astype(F32)
        nm = ADAM_B1 * m_ref[...] + (1.0 - ADAM_B1) * g
        nv = ADAM_B2 * v_ref[...] + (1.0 - ADAM_B2) * jnp.square(g)
        m_hat = nm / (1.0 - ADAM_B1 ** ADAM_STEP)
        v_hat = nv / (1.0 - ADAM_B2 ** ADAM_STEP)
        g_ref[...] = g
        nm_ref[...] = nm
        nv_ref[...] = nv
        d_ref[...] = -ADAM_LR * (m_hat / (jnp.sqrt(v_hat) + ADAM_EPS) + ADAM_WD * w_ref[...])

    blk = pl.BlockSpec((tr, cols), lambda i: (i, 0))
    shape = jax.ShapeDtypeStruct((rows, cols), F32)
    return pl.pallas_call(
        body, name=name, grid=(rows // tr,),
        in_specs=[pl.BlockSpec((n_parts, tr, cols), lambda i: (0, i, 0)), blk, blk, blk],
        out_specs=[blk] * 4, out_shape=[shape] * 4,
        compiler_params=_cparams(("parallel",)),
    )(parts, w, m, v)


def _kernel_params(full):
    return {
        "norm_mix": full["norm_mix"][:, None, :],
        "b_in": _in_cols_to_kernel(full["b_in"])[:, None, :],
        "a_log": _pad_lanes(full["a_log"]),
        "dt_bias": _pad_lanes(full["dt_bias"]),
        "sinks": full["sinks"],
        "c_norm": full["c_norm"][:, None, :],
        "norm_ffn": full["norm_ffn"][:, None, :],
        "norm_final": full["norm_final"][None, :],
    }


def _reference_grads(g):
    return {
        "norm_mix": g["norm_mix"][:, 0, :],
        "b_in": _in_cols_to_reference(g["b_in"][:, 0, :]),
        "conv_w": g["conv_w"],
        "a_log": g["a_log"][:, 0, :C_V_HEADS],
        "dt_bias": g["dt_bias"][:, 0, :C_V_HEADS],
        "sinks": g["sinks"],
        "c_norm": g["c_norm"][:, 0, :],
        "w_branch_a": g["w_branch_a"], "w_branch_b": g["w_branch_b"], "w_branch_c": g["w_branch_c"],
        "w_out": g["w_out"],
        "norm_ffn": g["norm_ffn"][:, 0, :],
        "w_ff1": g["w_ff1"], "w_ff2": g["w_ff2"],
        "norm_final": g["norm_final"][0],
    }


def kernel(x, positions, norm_mix, w_in, b_in, conv_w, a_log, dt_bias, sinks, c_norm, w_branch_a, w_branch_b, w_branch_c, w_out, norm_ffn, w_ff1, w_ff2, norm_final, loss_target, m_norm_mix, m_w_in, m_b_in, m_conv_w, m_a_log, m_dt_bias, m_sinks, m_c_norm, m_w_branch_a, m_w_branch_b, m_w_branch_c, m_w_out, m_norm_ffn, m_w_ff1, m_w_ff2, m_norm_final, v_norm_mix, v_w_in, v_b_in, v_conv_w, v_a_log, v_dt_bias, v_sinks, v_c_norm, v_w_branch_a, v_w_branch_b, v_w_branch_c, v_w_out, v_norm_ffn, v_w_ff1, v_w_ff2, v_norm_final):
    env = dict(locals())
    weights = {n: env[n] for n in WEIGHTS}
    moments_m = {n: env["m_" + n] for n in WEIGHTS}
    moments_v = {n: env["v_" + n] for n in WEIGHTS}

    axis_of = {n: axis - 1 for n, axis in BIG}

    packed_names = [n for n in MATMUL_WEIGHTS if n != "w_in"]

    def payload_of_layer(layer):
        cw = weights["conv_w"][layer]
        c1 = cw.astype(BF16)
        c2 = (cw - c1.astype(F32)).astype(BF16)
        c3 = (cw - c1.astype(F32) - c2.astype(F32)).astype(BF16)
        return [weights["w_in"][layer].astype(BF16),
                _pack([weights[n][layer].astype(BF16) for n in packed_names] + [c1, c2, c3])]

    def rest_of(packed):
        shapes = [weights[n].shape[1:] for n in packed_names] + [weights["conv_w"].shape[1:]] * 3
        blocks = _unpack(packed, shapes)
        wb = {n: _shards_to_full(blk, axis_of[n]) for n, blk in zip(packed_names, blocks)}
        return wb, _shards_to_full(sum(b.astype(F32) for b in blocks[-3:]), axis_of["conv_w"])

    tabs = rope_tables(positions[0])
    w_in_first, = _all_gather(payload_of_layer(0)[:1], name="gather_weights")
    loss, dx, dparams = _local_step(x[0], _kernel_params({n: weights[n] for n in SMALL}), w_in_first,
                                    ("ride", payload_of_layer(0)[1]), payload_of_layer, _w_in_from_shards,
                                    rest_of, tabs, loss_target[0])
    grads = _reference_grads(dparams)
    loss = lax.psum(loss, ("x", "y", "c"))

    core = lax.axis_index("c")
    rest = [(n, axis) for n, axis in BIG if n != "w_in"]
    w_in_rows = DEPTH * D_MODEL

    def by_core(shards, which):
        sh = shards.reshape((N_CHIP, 2) + shards.shape[1:])
        return lax.dynamic_index_in_dim(sh, which, axis=1, keepdims=False).astype(BF16)

    def halves(which):
        w_in_half = by_core(_w_in_to_shards(dparams["w_in"]), which).reshape(N_CHIP, w_in_rows, W_IN_SHARD)
        return [w_in_half, _pack([by_core(_full_to_shards(grads[n], axis), which) for n, axis in rest], lead=1)]

    from_sibling = _swap_with_sibling(halves(1 - core), name="scatter_grads_d2d")
    chip_sums = [_add_bf16_call(keep, got, name="scatter_grads_add")
                 for keep, got in zip(halves(core), from_sibling)]
    w_in_parts, rest_parts = _chip_all_to_all(chip_sums, name="scatter_grads_ici")
    small_parts, = _all_gather([_pack([grads[n] for n in SMALL])], name="gather_small_grads")

    out = {}
    results = _adamw_call(w_in_parts, *[d["w_in"].reshape(w_in_rows, W_IN_SHARD)
                                        for d in (weights, moments_m, moments_v)], name="adamw_w_in")
    for kind, buf in zip(("grad", "delta", "new_m", "new_v"), results):
        out[kind, "w_in"] = buf.reshape(weights["w_in"].shape)
    for names, parts in (([n for n, _ in rest], rest_parts), (list(SMALL), small_parts)):
        shapes = [weights[n].shape for n in names]
        packed = [_pack([d[n] for n in names]) for d in (weights, moments_m, moments_v)]
        results = _adamw_call(parts, *packed, name="adamw_" + names[0])
        for kind, buf in zip(("grad", "delta", "new_m", "new_v"), results):
            for n, arr in zip(names, _unpack(buf, shapes)):
                out[kind, n] = arr
    return (loss, dx[None], *[out[kind, n] for kind in ("grad", "delta", "new_m", "new_v") for n in WEIGHTS])
```

```python
import functools
import math

import jax
import jax.numpy as jnp
from jax import lax
from jax.experimental import pallas as pl
from jax.experimental.pallas import tpu as pltpu

F32 = jnp.float32
BF16 = jnp.bfloat16

N_DEV = 8
D_MODEL = 1024
DEPTH = 2
HEAD_DIM = 64
ROT_DIM = 16
ROPE_THETA = 500000.0
BLK = 128
NEG_INF = -1e30
EPS = 1e-6
A_CONFIGS = ((128, 1), (512, 4), (2048, 16))
B_GROUP = 4
C_QK_HEADS = 4
C_V_HEADS = 8
C_DK = 128
C_CONV = 4
CHUNK = 64
ADAM_LR = 0.001
ADAM_B1 = 0.9
ADAM_B2 = 0.999
ADAM_EPS = 1e-08
ADAM_WD = 0.01
ADAM_STEP = 10

IN_LAYOUT = (
    ("gate_a", 0, 1024, 5392), ("gate_b", 1024, 1024, 6416), ("gate_c", 2048, 1024, 7440),
    ("a_q", 3072, 512, 0), ("a_k", 3584, 512, 512), ("a_v", 4096, 512, 1024), ("b_q", 4608, 512, 1536),
    ("c_z", 5120, 1024, 4352), ("c_qkv", 6144, 2048, 2304),
    ("b_k", 8192, 128, 2048), ("b_v", 8320, 128, 2176), ("c_ab", 8448, 16, 5376),
)
COL = {name: start for name, start, _, _ in IN_LAYOUT}
D_IN_PAD = 8704
IN_TN = D_IN_PAD // 4
LANES = 128
VMEM_LIMIT = 56 * 1024 * 1024


def _cparams(sem=None):
    return pltpu.CompilerParams(dimension_semantics=sem, vmem_limit_bytes=VMEM_LIMIT)


def _relu2(t):
    return jnp.square(jnp.maximum(t, 0.0))


def _mm(a, b, *, ta=False, tb=False, bias=None, a_fn=None, mul_drelu2=None, add=None,
        out_dtype=F32, relu2_out=False, b_colsum=False, gather_src=None, tm=1024, tn=1024, tk=2048, name):
    if ta:
        kdim, m = a.shape
    else:
        m, kdim = a.shape
    n = b.shape[0] if tb else b.shape[1]
    tm, tn, tk = min(tm, m), min(tn, n), min(tk, kdim)
    assert m % tm == 0 and n % tn == 0 and kdim % tk == 0, (a.shape, b.shape, tm, tn, tk)
    nk = kdim // tk
    assert not b_colsum or (m == tm and not tb and nk > 1)
    dims = (((0 if ta else 1,), (1 if tb else 0,)), ((), ()))
    extras = [e for e in (bias, mul_drelu2, add) if e is not None]
    ng = len(gather_src) if gather_src is not None else 0
    grid = (m // tm, n // tn, nk)

    def body(*refs):
        if ng:
            n_in = 2 + len(extras)
            n_out = 1 + int(relu2_out) + int(b_colsum)
            n_scr = int(nk > 1) + int(b_colsum)
            gather_refs = (refs[n_in:n_in + ng], refs[n_in + ng + n_out:n_in + 2 * ng + n_out],
                           *refs[n_in + 2 * ng + n_out + n_scr:])
            refs = refs[:n_in] + refs[n_in + ng:n_in + ng + n_out] + refs[n_in + 2 * ng + n_out:]
            at_first = functools.reduce(jnp.logical_and, [pl.program_id(d) == 0 for d in range(3)])
            at_last = functools.reduce(jnp.logical_and, [pl.program_id(d) == grid[d] - 1 for d in range(3)])

            @pl.when(at_first)
            def _():
                _gather_start(*gather_refs)
        a_ref, b_ref = refs[0], refs[1]
        pos = 2
        bias_ref = pre_ref = add_ref = None
        if bias is not None:
            bias_ref = refs[pos]; pos += 1
        if mul_drelu2 is not None:
            pre_ref = refs[pos]; pos += 1
        if add is not None:
            add_ref = refs[pos]; pos += 1
        o_ref = refs[pos]
        pos += 1
        r_ref = None
        if relu2_out:
            r_ref = refs[pos]; pos += 1
        cs_ref = None
        if b_colsum:
            cs_ref = refs[pos]; pos += 1
        acc_ref = refs[pos] if nk > 1 else None
        cs_acc = refs[pos + 1] if b_colsum else None

        av = a_ref[...]
        if a_fn is not None:
            av = a_fn(av)
        bv = b_ref[...]
        part = lax.dot_general(av.astype(BF16), bv.astype(BF16), dims,
                               preferred_element_type=F32)
        if b_colsum:
            cs_part = jnp.sum(bv.astype(F32).reshape(tk // 8, 8, tn), axis=0)

        def finish(acc):
            if bias_ref is not None:
                acc = acc + bias_ref[...]
            if pre_ref is not None:
                acc = acc * (2.0 * jnp.maximum(pre_ref[...], 0.0))
            if add_ref is not None:
                acc = acc + add_ref[...]
            o_ref[...] = acc.astype(out_dtype)
            if r_ref is not None:
                r_ref[...] = _relu2(acc).astype(BF16)

        if nk == 1:
            finish(part)
        else:
            k = pl.program_id(2)

            @pl.when(k == 0)
            def _():
                acc_ref[...] = part
                if b_colsum:
                    cs_acc[...] = cs_part

            @pl.when(k > 0)
            def _():
                acc_ref[...] += part
                if b_colsum:
                    cs_acc[...] += cs_part

            @pl.when(k == nk - 1)
            def _():
                finish(acc_ref[...])
                if b_colsum:
                    cs_ref[...] = jnp.sum(cs_acc[...], axis=0, keepdims=True)
        if ng:
            @pl.when(at_last)
            def _():
                _gather_finish(*gather_refs)

    a_spec = (pl.BlockSpec((tk, tm), lambda i, j, k: (k, i)) if ta
              else pl.BlockSpec((tm, tk), lambda i, j, k: (i, k)))
    b_spec = (pl.BlockSpec((tn, tk), lambda i, j, k: (j, k)) if tb
              else pl.BlockSpec((tk, tn), lambda i, j, k: (k, j)))
    in_specs = [a_spec, b_spec]
    if bias is not None:
        in_specs.append(pl.BlockSpec((1, tn), lambda i, j, k: (0, j)))
    for _ in extras[(1 if bias is not None else 0):]:
        in_specs.append(pl.BlockSpec((tm, tn), lambda i, j, k: (i, j)))
    o_spec = pl.BlockSpec((tm, tn), lambda i, j, k: (i, j))
    out_specs, out_shape = [o_spec], [jax.ShapeDtypeStruct((m, n), out_dtype)]
    scratch = [pltpu.VMEM((tm, tn), F32)] if nk > 1 else []
    if relu2_out:
        out_specs.append(o_spec)
        out_shape.append(jax.ShapeDtypeStruct((m, n), BF16))
    if b_colsum:
        out_specs.append(pl.BlockSpec((1, tn), lambda i, j, k: (0, j)))
        out_shape.append(jax.ShapeDtypeStruct((1, n), F32))
        scratch.append(pltpu.VMEM((8, tn), F32))
    if ng:
        in_specs += [HBM_SPEC] * ng
        out_specs += [HBM_SPEC] * ng
        out_shape += _gathered_shapes(gather_src)
        scratch += _gather_sems(ng)
    single = len(out_specs) == 1
    outs = pl.pallas_call(
        body, name=name,
        grid=grid,
        in_specs=in_specs,
        out_specs=out_specs[0] if single else out_specs,
        out_shape=out_shape[0] if single else out_shape,
        scratch_shapes=scratch,
        compiler_params=_cparams(("arbitrary",) * 3 if ng else ("parallel", "parallel", "arbitrary")),
    )(a, b, *extras, *(gather_src or []))
    if not ng:
        return outs
    n_out = len(outs) - ng
    return (outs[0] if n_out == 1 else tuple(outs[:n_out])), list(outs[n_out:])


def _rms_fwd_call(x, g, *, name, out_dtype=F32, tq=512):
    t, d = x.shape

    def body(x_ref, g_ref, y_ref):
        xv = x_ref[...]
        r = lax.rsqrt(jnp.mean(xv * xv, axis=-1, keepdims=True) + EPS)
        y_ref[...] = (xv * r * g_ref[...]).astype(out_dtype)

    return pl.pallas_call(
        body, name=name, grid=(t // tq,),
        in_specs=[pl.BlockSpec((tq, d), lambda i: (i, 0)), pl.BlockSpec((1, d), lambda i: (0, 0))],
        out_specs=pl.BlockSpec((tq, d), lambda i: (i, 0)),
        out_shape=jax.ShapeDtypeStruct((t, d), out_dtype),
        compiler_params=_cparams(("parallel",)),
    )(x, g)


def _rms_bwd_call(x, g, dy, *, name, add=None, tq=512):
    t, d = x.shape
    nt = t // tq

    def body(*refs):
        if add is None:
            x_ref, g_ref, dy_ref, dx_ref, dg_ref, acc_ref = refs
        else:
            x_ref, g_ref, dy_ref, add_ref, dx_ref, dg_ref, acc_ref = refs
        i = pl.program_id(0)
        xv = x_ref[...]
        r = lax.rsqrt(jnp.mean(xv * xv, axis=-1, keepdims=True) + EPS)
        xh = xv * r
        dyv = dy_ref[...]
        dxh = dyv * g_ref[...]
        dx = r * (dxh - xh * jnp.mean(dxh * xh, axis=-1, keepdims=True))
        dx_ref[...] = dx if add is None else dx + add_ref[...]
        part = jnp.sum((dyv * xh).reshape(tq // 8, 8, d), axis=0)

        @pl.when(i == 0)
        def _():
            acc_ref[...] = part

        @pl.when(i > 0)
        def _():
            acc_ref[...] += part

        @pl.when(i == nt - 1)
        def _():
            dg_ref[...] = jnp.sum(acc_ref[...], axis=0, keepdims=True)

    blk = pl.BlockSpec((tq, d), lambda i: (i, 0))
    row = pl.BlockSpec((1, d), lambda i: (0, 0))
    extra = [] if add is None else [add]
    return pl.pallas_call(
        body, name=name, grid=(nt,),
        in_specs=[blk, row, blk] + [blk] * len(extra),
        out_specs=[blk, row],
        out_shape=[jax.ShapeDtypeStruct((t, d), F32), jax.ShapeDtypeStruct((1, d), F32)],
        scratch_shapes=[pltpu.VMEM((8, d), F32)],
        compiler_params=_cparams(("arbitrary",)),
    )(x, g, dy, *extra)


def _loss_call(x, g, tgt, *, tq=512):
    t, d = x.shape
    nt = t // tq

    def body(x_ref, g_ref, t_ref, loss_ref, dx_ref, dg_ref, acc_ref, sq_ref):
        i = pl.program_id(0)
        xv = x_ref[...]
        r = lax.rsqrt(jnp.mean(xv * xv, axis=-1, keepdims=True) + EPS)
        xh = xv * r
        gv = g_ref[...]
        err = xh * gv - t_ref[...]
        dyv = err * (1.0 / d)
        dxh = dyv * gv
        dx_ref[...] = r * (dxh - xh * jnp.mean(dxh * xh, axis=-1, keepdims=True))
        part = jnp.sum((dyv * xh).reshape(tq // 8, 8, d), axis=0)
        sq = jnp.sum((err * err).reshape(tq // 8, 8, d), axis=0)

        @pl.when(i == 0)
        def _():
            acc_ref[...] = part
            sq_ref[...] = sq

        @pl.when(i > 0)
        def _():
            acc_ref[...] += part
            sq_ref[...] += sq

        @pl.when(i == nt - 1)
        def _():
            dg_ref[...] = jnp.sum(acc_ref[...], axis=0, keepdims=True)
            tot = jnp.sum(jnp.sum(sq_ref[...], axis=0, keepdims=True), axis=1, keepdims=True)
            loss_ref[...] = jnp.broadcast_to(tot * (0.5 / d), (8, LANES))

    blk = pl.BlockSpec((tq, d), lambda i: (i, 0))
    row = pl.BlockSpec((1, d), lambda i: (0, 0))
    loss, dx, dg = pl.pallas_call(
        body, name="loss", grid=(nt,),
        in_specs=[blk, row, blk],
        out_specs=[pl.BlockSpec((8, LANES), lambda i: (0, 0)), blk, row],
        out_shape=[jax.ShapeDtypeStruct((8, LANES), F32), jax.ShapeDtypeStruct((t, d), F32),
                   jax.ShapeDtypeStruct((1, d), F32)],
        scratch_shapes=[pltpu.VMEM((8, d), F32), pltpu.VMEM((8, d), F32)],
        compiler_params=_cparams(("arbitrary",)),
    )(x, g, tgt)
    return loss[0, 0], dx, dg


MESH_ID = pl.DeviceIdType.MESH
HBM_SPEC = pl.BlockSpec(memory_space=pl.ANY)


def rope_tables(positions):
    half = ROT_DIM // 2
    inv_freq = jnp.power(ROPE_THETA, -jnp.arange(0, ROT_DIM, 2, dtype=F32) / ROT_DIM)
    in_head = jnp.arange(LANES) % HEAD_DIM
    rot = in_head < ROT_DIM
    freq = jnp.where(rot, inv_freq[in_head % half], 0.0)
    ang = positions.astype(F32)[:, None] * freq[None, :]
    cos, sin = jnp.cos(ang), jnp.sin(ang)
    b = jnp.where(jnp.logical_and(rot, in_head >= half)[None, :], sin, 0.0)
    c = jnp.where((in_head < half)[None, :], -sin, 0.0)
    return cos, b, c


def _rope_chunk(xs, a, b, c, transpose):
    half = ROT_DIM // 2
    if transpose:
        return xs * a + pltpu.roll(xs * b, LANES - half, 1) + pltpu.roll(xs * c, half, 1)
    return xs * a + pltpu.roll(xs, half, 1) * b + pltpu.roll(xs, LANES - half, 1) * c


def _dilated_spec(tq, w, d):
    return pl.BlockSpec((tq // d, d * w), lambda i: (i, 0))


def _load_dilated(ref, stage_ref, d, w, j):
    cs = slice(j * LANES, (j + 1) * LANES)
    if d == 1:
        return ref[:, cs].astype(F32)
    for r in range(d):
        stage_ref[pl.ds(r, ref.shape[0], stride=d), :] = ref[:, r * w + j * LANES:r * w + (j + 1) * LANES].astype(F32)
    return stage_ref[...]


def _store_dilated(ref, stage_ref, d, w, j, val):
    if d == 1:
        ref[:, j * LANES:(j + 1) * LANES] = val.astype(ref.dtype)
        return
    stage_ref[...] = val
    for r in range(d):
        rows = stage_ref[pl.ds(r, ref.shape[0], stride=d), :]
        ref[:, r * w + j * LANES:r * w + (j + 1) * LANES] = rows.astype(ref.dtype)


def _stage_buffers(tq, n):
    return [pltpu.VMEM((tq, LANES), F32)] * n


def _rope_gather_call(u, tabs, parts, *, name, dils=(1,), tq=512):
    t = u.shape[0]
    total = sum(w for _, w, _, _ in parts)
    assert all(start % w == 0 for start, w, _, _ in parts)
    n_stage = (total // LANES) * sum(d > 1 for d in dils)

    def body(a_ref, b_ref, c_ref, *refs):
        x_refs, o_refs = refs[:len(parts)], refs[len(parts):len(parts) + len(dils)]
        stages = iter(refs[len(parts) + len(dils):])
        a, b, c = a_ref[...], b_ref[...], c_ref[...]
        g = 0
        for x_ref, (_, w, roped, scale) in zip(x_refs, parts):
            for j in range(w // LANES):
                xs = x_ref[:, j * LANES:(j + 1) * LANES]
                val = _rope_chunk(xs, a, b, c, False) if roped else xs
                val = val * scale if scale != 1.0 else val
                for o_ref, d in zip(o_refs, dils):
                    _store_dilated(o_ref, next(stages) if d > 1 else None, d, total, g, val)
                g += 1

    tab_spec = pl.BlockSpec((tq, LANES), lambda i: (i, 0))
    return pl.pallas_call(
        body, name=name, grid=(t // tq,),
        in_specs=[tab_spec] * 3 + [pl.BlockSpec((tq, w), lambda i, cb=start // w: (i, cb)) for start, w, _, _ in parts],
        out_specs=[_dilated_spec(tq, total, d) for d in dils],
        out_shape=[jax.ShapeDtypeStruct((t // d, d * total), BF16) for d in dils],
        scratch_shapes=_stage_buffers(tq, n_stage),
        compiler_params=_cparams(("parallel",)),
    )(*tabs, *([u] * len(parts)))


def _du_operands(du_buf, n_inputs):
    if du_buf is None:
        return [], [], {}
    return [du_buf], [HBM_SPEC], {n_inputs: 0}


def _du_shape(t):
    return jax.ShapeDtypeStruct((t, D_IN_PAD), BF16)


def _rope_scatter_call(du_buf, t, pieces, col, tabs, *, name, dils=(1,), tq=512):
    total = sum(w for _, w, _, _ in pieces)
    assert col % total == 0 and all(len(arrs) == len(dils) for arrs, _, _, _ in pieces)
    arrays = [a for arrs, _, _, _ in pieces for a in arrs]
    extra, extra_specs, aliases = _du_operands(du_buf, 3 + len(arrays))
    n_stage = (total // LANES) * sum(d > 1 for d in dils)

    def body(a_ref, b_ref, c_ref, *refs):
        o_ref = refs[len(arrays) + len(extra)]
        stages = iter(refs[len(arrays) + len(extra) + 1:])
        a, b, c = a_ref[...], b_ref[...], c_ref[...]
        k = off = 0
        for arrs, w, roped, scale in pieces:
            mine = refs[k:k + len(arrs)]
            k += len(arrs)
            for j in range(w // LANES):
                xs = None
                for r, d in zip(mine, dils):
                    part = _load_dilated(r, next(stages) if d > 1 else None, d, w, j)
                    xs = part if xs is None else xs + part
                if scale != 1.0:
                    xs = xs * scale
                val = _rope_chunk(xs, a, b, c, True) if roped else xs
                o_ref[:, off + j * LANES:off + (j + 1) * LANES] = val.astype(BF16)
            off += w

    tab_spec = pl.BlockSpec((tq, LANES), lambda i: (i, 0))
    in_specs = [tab_spec] * 3 + [_dilated_spec(tq, w, d) for _, w, _, _ in pieces for d in dils]
    return pl.pallas_call(
        body, name=name, grid=(t // tq,),
        in_specs=in_specs + extra_specs,
        out_specs=pl.BlockSpec((tq, total), lambda i: (i, col // total)),
        out_shape=_du_shape(t), input_output_aliases=aliases,
        scratch_shapes=_stage_buffers(tq, n_stage),
        compiler_params=_cparams(("parallel",)),
    )(*tabs, *arrays, *extra)


def _band_masks(first_block, max_dist):
    qi = lax.broadcasted_iota(jnp.int32, (BLK, BLK), 0)
    kj = lax.broadcasted_iota(jnp.int32, (BLK, BLK), 1)
    valid_prev = jnp.logical_and(kj >= qi + (BLK - max_dist), jnp.logical_not(first_block))
    valid_cur = kj <= qi
    return valid_prev, valid_cur


_NN = (((1,), (0,)), ((), ()))
_NT = (((1,), (1,)), ((), ()))
_TN = (((0,), (0,)), ((), ()))


HEAD_STAGE = 8


def _attn_row_maps(nb):
    def cur(i):
        return jnp.minimum(i, nb - 1)

    def prev(i):
        return jnp.maximum(jnp.minimum(i, nb - 1) - 1, 0)

    return cur, prev


def _dil_spec(w, dil, rows, seg=None, off=0):
    seg = w if seg is None else seg
    assert off % w == 0 and (dil == 1 or seg % w == 0)
    return pl.BlockSpec((BLK, w), lambda r, i: (rows(i), (r * seg + off) // w))


def _dil_shape(l, dil, w, dtype=F32):
    return jax.ShapeDtypeStruct((l, dil * w), dtype)


def _attn_fwd_call(qkv2, sink, *, dil, group, max_dist, seg, offs, qw, kw, name):
    l = qkv2.shape[0]
    nh = qw // HEAD_DIM
    nb = l // BLK
    use_sink = sink is not None

    def body(*refs):
        if use_sink:
            sink_ref, refs = refs[0], refs[1:]
        q_ref, kp_ref, kc_ref, vp_ref, vc_ref, o_ref, lse_ref = refs
        valid_prev, valid_cur = _band_masks(pl.program_id(1) == 0, max_dist)
        lane = lax.broadcasted_iota(jnp.int32, (BLK, LANES), 1)
        lse_tile = jnp.zeros((BLK, LANES), F32)

        def dot(a, b, dims=_NN):
            return lax.dot_general(a, b, dims, preferred_element_type=F32)

        for g0 in range(0, nh, HEAD_STAGE):
            heads = list(range(g0, min(g0 + HEAD_STAGE, nh)))
            kv = {}
            for kh in sorted({h // group for h in heads}):
                ks = slice(kh * HEAD_DIM, (kh + 1) * HEAD_DIM)
                kv[kh] = tuple(ref[:, ks].astype(BF16) for ref in (kp_ref, kc_ref, vp_ref, vc_ref))
            qs = [q_ref[:, h * HEAD_DIM:(h + 1) * HEAD_DIM].astype(BF16) for h in heads]
            sps = [jnp.where(valid_prev, dot(qh, kv[h // group][0], _NT), NEG_INF) for h, qh in zip(heads, qs)]
            scs = [jnp.where(valid_cur, dot(qh, kv[h // group][1], _NT), NEG_INF) for h, qh in zip(heads, qs)]
            ms = [jnp.maximum(jnp.max(sp, axis=1, keepdims=True), jnp.max(sc, axis=1, keepdims=True))
                  for sp, sc in zip(sps, scs)]
            if use_sink:
                ms = [jnp.maximum(m, sink_ref[h]) for h, m in zip(heads, ms)]
            pps = [jnp.exp(sp - m) for sp, m in zip(sps, ms)]
            pcs = [jnp.exp(sc - m) for sc, m in zip(scs, ms)]
            dens = [jnp.sum(pp, axis=1, keepdims=True) + jnp.sum(pc, axis=1, keepdims=True)
                    for pp, pc in zip(pps, pcs)]
            if use_sink:
                dens = [den + jnp.exp(sink_ref[h] - m) for h, den, m in zip(heads, dens, ms)]
            outs = [dot(pp.astype(BF16), kv[h // group][2]) + dot(pc.astype(BF16), kv[h // group][3])
                    for h, pp, pc in zip(heads, pps, pcs)]
            for h, o, den, m in zip(heads, outs, dens, ms):
                o_ref[:, h * HEAD_DIM:(h + 1) * HEAD_DIM] = o / den
                lse_tile = jnp.where(lane == h, m + jnp.log(den), lse_tile)
        lse_ref[...] = lse_tile

    cur, prev = _attn_row_maps(nb)
    o_spec, lse_spec = _dil_spec(qw, dil, cur), _dil_spec(LANES, dil, cur)
    in_specs = [_dil_spec(qw, dil, cur, seg, offs[0]),
                _dil_spec(kw, dil, prev, seg, offs[1]), _dil_spec(kw, dil, cur, seg, offs[1]),
                _dil_spec(kw, dil, prev, seg, offs[2]), _dil_spec(kw, dil, cur, seg, offs[2])]
    args = [qkv2] * 5
    if use_sink:
        in_specs = [pl.BlockSpec(memory_space=pltpu.SMEM)] + in_specs
        args = [sink] + args
    return pl.pallas_call(
        body, name=name, grid=(dil, nb),
        in_specs=in_specs,
        out_specs=[o_spec, lse_spec],
        out_shape=[_dil_shape(l, dil, qw), _dil_shape(l, dil, LANES)],
        compiler_params=_cparams(("parallel", "parallel")),
    )(*args)


def _attn_bwd_call(qkv2, sink, o2, lse2, do2, dlse2, *, dil, group, max_dist, seg, offs, qw, kw, name):
    l = qkv2.shape[0]
    nh = qw // HEAD_DIM
    nb = l // BLK
    use_sink = sink is not None

    def body(*refs):
        if use_sink:
            sink_ref, refs = refs[0], refs[1:]
        (q_ref, kp_ref, kc_ref, vp_ref, vc_ref, o_ref, lse_ref, do_ref, dlse_ref,
         dq_ref, dk_ref, dv_ref, dsink_ref, ck_ref, cv_ref) = refs
        step = pl.program_id(1)

        @pl.when(jnp.logical_and(pl.program_id(0) == 0, step == 0))
        def _():
            dsink_ref[...] = jnp.zeros_like(dsink_ref)

        @pl.when(step == 0)
        def _():
            ck_ref[...] = jnp.zeros_like(ck_ref)
            cv_ref[...] = jnp.zeros_like(cv_ref)

        def dot(a, b, dims=_NN):
            return lax.dot_general(a, b, dims, preferred_element_type=F32)

        @pl.when(step < nb)
        def _():
            valid_prev, valid_cur = _band_masks(step == 0, max_dist)
            row = lax.broadcasted_iota(jnp.int32, (8, LANES), 0)
            lanes8 = lax.broadcasted_iota(jnp.int32, (8, LANES), 1)
            ds_tile = jnp.zeros((8, LANES), F32)
            for g0 in range(0, nh, HEAD_STAGE):
                heads = list(range(g0, min(g0 + HEAD_STAGE, nh)))
                hss = [slice(h * HEAD_DIM, (h + 1) * HEAD_DIM) for h in heads]
                kv = {}
                for kh in sorted({h // group for h in heads}):
                    ks = slice(kh * HEAD_DIM, (kh + 1) * HEAD_DIM)
                    kv[kh] = tuple(ref[:, ks].astype(BF16) for ref in (kp_ref, kc_ref, vp_ref, vc_ref))
                qs = [q_ref[:, hs].astype(BF16) for hs in hss]
                dos = [do_ref[:, hs] for hs in hss]
                dobs = [d.astype(BF16) for d in dos]
                lses = [lse_ref[:, h:h + 1] for h in heads]
                sps = [dot(qh, kv[h // group][0], _NT) for h, qh in zip(heads, qs)]
                scs = [dot(qh, kv[h // group][1], _NT) for h, qh in zip(heads, qs)]
                dpps = [dot(dob, kv[h // group][2], _NT) for h, dob in zip(heads, dobs)]
                dpcs = [dot(dob, kv[h // group][3], _NT) for h, dob in zip(heads, dobs)]
                pps = [jnp.where(valid_prev, jnp.exp(jnp.where(valid_prev, sp, NEG_INF) - ls), 0.0)
                       for sp, ls in zip(sps, lses)]
                pcs = [jnp.where(valid_cur, jnp.exp(jnp.where(valid_cur, sc, NEG_INF) - ls), 0.0)
                       for sc, ls in zip(scs, lses)]
                deltas = [jnp.sum(d * o_ref[:, hs], axis=1, keepdims=True) for d, hs in zip(dos, hss)]
                corrs = [dlse_ref[:, h:h + 1] - dl for h, dl in zip(heads, deltas)]
                dsps = [(pp * (dp + c)).astype(BF16) for pp, dp, c in zip(pps, dpps, corrs)]
                dscs = [(pc * (dp + c)).astype(BF16) for pc, dp, c in zip(pcs, dpcs, corrs)]
                for h, hs, dsp, dsc in zip(heads, hss, dsps, dscs):
                    dq = dot(dsp, kv[h // group][0]) + dot(dsc, kv[h // group][1])
                    dq_ref[:, hs] = dq.astype(BF16)
                parts = [(dot(dsc, qh, _TN), dot(dsp, qh, _TN),
                          dot(pc.astype(BF16), dob, _TN), dot(pp.astype(BF16), dob, _TN))
                         for dsc, dsp, qh, pc, pp, dob in zip(dscs, dsps, qs, pcs, pps, dobs)]
                for kh in kv:
                    ks = slice(kh * HEAD_DIM, (kh + 1) * HEAD_DIM)
                    mine = [p for h, p in zip(heads, parts) if h // group == kh]
                    dkc, dkp, dvc, dvp = (sum(p[j] for p in mine[1:]) + mine[0][j] for j in range(4))
                    dk_ref[:, ks] = (ck_ref[:, ks] + dkp).astype(BF16)
                    dv_ref[:, ks] = (cv_ref[:, ks] + dvp).astype(BF16)
                    ck_ref[:, ks] = dkc
                    cv_ref[:, ks] = dvc
                if use_sink:
                    for h, ls, dl in zip(heads, lses, deltas):
                        val = -jnp.sum(jnp.exp(sink_ref[h] - ls) * dl, axis=0, keepdims=True)
                        ds_tile = jnp.where(jnp.logical_and(row == 0, lanes8 == h), val, ds_tile)
            if use_sink:
                dsink_ref[...] += ds_tile

        @pl.when(step == nb)
        def _():
            dk_ref[...] = ck_ref[...].astype(BF16)
            dv_ref[...] = cv_ref[...].astype(BF16)

    cur, prev = _attn_row_maps(nb)
    q_spec, lse_spec = _dil_spec(qw, dil, cur), _dil_spec(LANES, dil, cur)
    lag_spec = _dil_spec(kw, dil, lambda i: jnp.maximum(i - 1, 0))
    in_specs = [_dil_spec(qw, dil, cur, seg, offs[0]),
                _dil_spec(kw, dil, prev, seg, offs[1]), _dil_spec(kw, dil, cur, seg, offs[1]),
                _dil_spec(kw, dil, prev, seg, offs[2]), _dil_spec(kw, dil, cur, seg, offs[2]),
                q_spec, lse_spec, q_spec, lse_spec]
    args = [qkv2] * 5 + [o2, lse2, do2, dlse2]
    if use_sink:
        in_specs = [pl.BlockSpec(memory_space=pltpu.SMEM)] + in_specs
        args = [sink] + args
    kv_shape = _dil_shape(l, dil, kw, BF16)
    return pl.pallas_call(
        body, name=name, grid=(dil, nb + 1),
        in_specs=in_specs,
        out_specs=[q_spec, lag_spec, lag_spec, pl.BlockSpec((8, LANES), lambda r, i: (0, 0))],
        out_shape=[_dil_shape(l, dil, qw, BF16), kv_shape, kv_shape,
                   jax.ShapeDtypeStruct((8, LANES), F32)],
        scratch_shapes=[pltpu.VMEM((BLK, kw), F32), pltpu.VMEM((BLK, kw), F32)],
        compiler_params=_cparams(("arbitrary", "arbitrary")),
    )(*args)


def _attn_config(tag, dil, group, max_dist, seg, offs, qw, kw):
    return dict(name=tag, dil=dil, group=group, max_dist=max_dist, seg=seg, offs=offs, qw=qw, kw=kw)


A_W = 8 * HEAD_DIM
ATTN_A_CFGS = tuple(_attn_config("attn_a%d" % dil, dil, 1, window // dil, 3 * A_W, (0, A_W, 2 * A_W), A_W, A_W)
                    for window, dil in A_CONFIGS)
B_KVW = 2 * HEAD_DIM
A_DILS = tuple(cfg["dil"] for cfg in ATTN_A_CFGS)
ATTN_B_CFG = _attn_config("attn_b", 1, B_GROUP, BLK - 1, A_W + 2 * B_KVW, (0, A_W, A_W + B_KVW), A_W, B_KVW)


def _attn_fwd(cfg, qkv2, sink):
    kw = {k: v for k, v in cfg.items() if k != "name"}
    return _attn_fwd_call(qkv2, sink, name=cfg["name"] + "_fwd", **kw)


def _attn_bwd(cfg, qkv2, o2, lse2, sink, do2, dlse2):
    kw = {k: v for k, v in cfg.items() if k != "name"}
    return _attn_bwd_call(qkv2, sink, o2, lse2, do2, dlse2, name=cfg["name"] + "_bwd", **kw)


def _head_expand():
    r = lax.broadcasted_iota(jnp.int32, (LANES, 8 * HEAD_DIM), 0)
    c = lax.broadcasted_iota(jnp.int32, (LANES, 8 * HEAD_DIM), 1)
    return (c // HEAD_DIM == r).astype(F32)


def _combine_weights(l0, l1, l2):
    m = jnp.maximum(jnp.maximum(l0, l1), l2)
    e0, e1, e2 = jnp.exp(l0 - m), jnp.exp(l1 - m), jnp.exp(l2 - m)
    inv = 1.0 / (e0 + e1 + e2)
    return e0 * inv, e1 * inv, e2 * inv


def _combine_fwd_call(os_, lses, dils, *, tq=512):
    w = os_[0].shape[1] // dils[0]
    t = os_[0].shape[0] * dils[0]
    groups = w // LANES
    n_stage = (groups + 1) * sum(d > 1 for d in dils)

    def body(*refs):
        o_refs, l_refs, y_ref = refs[:3], refs[3:6], refs[6]
        stages = iter(refs[7:])
        ws = _combine_weights(*[_load_dilated(l, next(stages) if d > 1 else None, d, LANES, 0)
                                for l, d in zip(l_refs, dils)])
        e = _head_expand()
        wide = [_dot_mask(e, wt, mask_left=False) for wt in ws]
        for j in range(groups):
            cs = slice(j * LANES, (j + 1) * LANES)
            y = None
            for o_ref, d, wd in zip(o_refs, dils, wide):
                term = wd[:, cs] * _load_dilated(o_ref, next(stages) if d > 1 else None, d, w, j)
                y = term if y is None else y + term
            y_ref[:, cs] = y

    return pl.pallas_call(
        body, name="combine_fwd", grid=(t // tq,),
        in_specs=[_dilated_spec(tq, w, d) for d in dils] + [_dilated_spec(tq, LANES, d) for d in dils],
        out_specs=pl.BlockSpec((tq, w), lambda i: (i, 0)),
        out_shape=jax.ShapeDtypeStruct((t, w), F32),
        scratch_shapes=_stage_buffers(tq, n_stage),
        compiler_params=_cparams(("parallel",)),
    )(*os_, *lses)


def _combine_bwd_call(os_, lses, dy, dils, *, tq=512):
    t, w = dy.shape
    groups = w // LANES
    n_stage = 2 * (groups + 1) * sum(d > 1 for d in dils)

    def body(*refs):
        o_refs, l_refs, dy_ref, do_refs, dl_refs = refs[:3], refs[3:6], refs[6], refs[7:10], refs[10:13]
        stages = iter(refs[13:])

        def stage(d):
            return next(stages) if d > 1 else None

        ws = _combine_weights(*[_load_dilated(l, stage(d), d, LANES, 0) for l, d in zip(l_refs, dils)])
        e = _head_expand()
        dyv = dy_ref[...]
        dws = []
        for o_ref, do_ref, d, wt in zip(o_refs, do_refs, dils, ws):
            do = _dot_mask(e, wt, mask_left=False) * dyv
            for j in range(groups):
                _store_dilated(do_ref, stage(d), d, w, j, do[:, j * LANES:(j + 1) * LANES])
            ov = jnp.concatenate([_load_dilated(o_ref, stage(d), d, w, j) for j in range(groups)], axis=1)
            dws.append(_dot_mask(e, dyv * ov, _NT, mask_left=False))
        mean = ws[0] * dws[0] + ws[1] * dws[1] + ws[2] * dws[2]
        for dl_ref, d, wt, dw in zip(dl_refs, dils, ws, dws):
            _store_dilated(dl_ref, stage(d), d, LANES, 0, wt * (dw - mean))

    o_specs = [_dilated_spec(tq, w, d) for d in dils]
    l_specs = [_dilated_spec(tq, LANES, d) for d in dils]
    return pl.pallas_call(
        body, name="combine_bwd", grid=(t // tq,),
        in_specs=o_specs + l_specs + [pl.BlockSpec((tq, w), lambda i: (i, 0))], out_specs=o_specs + l_specs,
        out_shape=[jax.ShapeDtypeStruct((t // d, d * w), F32) for d in dils]
        + [jax.ShapeDtypeStruct((t // d, d * LANES), F32) for d in dils],
        scratch_shapes=_stage_buffers(tq, n_stage),
        compiler_params=_cparams(("parallel",)),
    )(*os_, *lses, dy)


C_QKW = C_QK_HEADS * C_DK
C_CONV_W = 2 * C_QKW + C_V_HEADS * C_DK
HALO = 8


def _silu_parts(z):
    sig = jax.nn.sigmoid(z)
    return z * sig, sig * (1.0 + z * (1.0 - sig))


def _conv_window_specs(tq, t):
    c = C_CONV_W
    cb = COL["c_qkv"] // c
    blk = pl.BlockSpec((tq, c), lambda i: (i, cb))
    before = pl.BlockSpec((HALO, c), lambda i: (jnp.maximum(i * (tq // HALO) - 1, 0), cb))
    return c, cb, blk, before


def _conv_prep_fwd_call(u, w, *, tq=512):
    t = u.shape[0]
    c, _, x_spec, halo_spec = _conv_window_specs(tq, t)
    nqk = 2 * C_QK_HEADS

    def body(x_ref, halo_ref, w_ref, z_ref, qk_ref, v_ref):
        i = pl.program_id(0)
        halo = jnp.where(i == 0, 0.0, halo_ref[...])
        xc = jnp.concatenate([halo, x_ref[...]], axis=0)
        wv = w_ref[...]
        z = xc[HALO - 3:HALO - 3 + tq] * wv[0:1]
        for j in range(1, C_CONV):
            z = z + xc[HALO - 3 + j:HALO - 3 + j + tq] * wv[j:j + 1]
        z_ref[...] = z
        act, _ = _silu_parts(z)
        for h in range(nqk):
            a = act[:, h * C_DK:(h + 1) * C_DK]
            qk_ref[:, h * C_DK:(h + 1) * C_DK] = a * lax.rsqrt(jnp.sum(a * a, axis=1, keepdims=True) + EPS)
        v_ref[...] = act[:, nqk * C_DK:]

    return pl.pallas_call(
        body, name="conv_prep_fwd", grid=(t // tq,),
        in_specs=[x_spec, halo_spec, pl.BlockSpec((C_CONV, c), lambda i: (0, 0))],
        out_specs=[pl.BlockSpec((tq, c), lambda i: (i, 0)),
                   pl.BlockSpec((tq, 2 * C_QKW), lambda i: (i, 0)),
                   pl.BlockSpec((tq, c - 2 * C_QKW), lambda i: (i, 0))],
        out_shape=[jax.ShapeDtypeStruct((t, c), F32), jax.ShapeDtypeStruct((t, 2 * C_QKW), F32),
                   jax.ShapeDtypeStruct((t, c - 2 * C_QKW), F32)],
        compiler_params=_cparams(("parallel",)),
    )(u, u, w)


def _conv_prep_dz_call(z, dqk, dv, *, tq=512):
    t, c = z.shape
    nqk = 2 * C_QK_HEADS

    def body(z_ref, dqk_ref, dv_ref, dz_ref):
        zv = z_ref[...]
        act, dact = _silu_parts(zv)
        for h in range(nqk):
            hs = slice(h * C_DK, (h + 1) * C_DK)
            a = act[:, hs]
            r = lax.rsqrt(jnp.sum(a * a, axis=1, keepdims=True) + EPS)
            nrm = a * r
            dn = dqk_ref[:, hs]
            da = r * (dn - nrm * jnp.sum(dn * nrm, axis=1, keepdims=True))
            dz_ref[:, hs] = da * dact[:, hs]
        dz_ref[:, nqk * C_DK:] = dv_ref[...] * dact[:, nqk * C_DK:]

    return pl.pallas_call(
        body, name="conv_prep_dz", grid=(t // tq,),
        in_specs=[pl.BlockSpec((tq, c), lambda i: (i, 0)),
                  pl.BlockSpec((tq, 2 * C_QKW), lambda i: (i, 0)),
                  pl.BlockSpec((tq, c - 2 * C_QKW), lambda i: (i, 0))],
        out_specs=pl.BlockSpec((tq, c), lambda i: (i, 0)),
        out_shape=jax.ShapeDtypeStruct((t, c), F32),
        compiler_params=_cparams(("parallel",)),
    )(z, dqk, dv)


def _conv_bwd_call(u, dz, w, du_buf, *, tq=512):
    t = u.shape[0]
    nt = t // tq
    c, cb, x_spec, halo_spec = _conv_window_specs(tq, t)
    extra, extra_specs, aliases = _du_operands(du_buf, 5)

    def body(x_ref, xh_ref, dz_ref, dzh_ref, w_ref, *refs):
        dx_ref, dw_ref = refs[len(extra):]
        i = pl.program_id(0)
        xc = jnp.concatenate([jnp.where(i == 0, 0.0, xh_ref[...]), x_ref[...]], axis=0)
        dzv = dz_ref[...]
        dzc = jnp.concatenate([dzv, jnp.where(i == nt - 1, 0.0, dzh_ref[...])], axis=0)
        wv = w_ref[...]
        dx = dzv * wv[3:4]
        for s in range(1, C_CONV):
            dx = dx + dzc[s:s + tq] * wv[3 - s:4 - s]
        dx_ref[...] = dx.astype(BF16)
        row = lax.broadcasted_iota(jnp.int32, (8, c), 0)
        dw = jnp.zeros((8, c), F32)
        for j in range(C_CONV):
            prod = dzv * xc[HALO - 3 + j:HALO - 3 + j + tq]
            col = jnp.sum(jnp.sum(prod.reshape(tq // 8, 8, c), axis=0), axis=0, keepdims=True)
            dw = jnp.where(row == j, col, dw)

        @pl.when(i == 0)
        def _():
            dw_ref[...] = dw

        @pl.when(i > 0)
        def _():
            dw_ref[...] += dw

    blk = pl.BlockSpec((tq, c), lambda i: (i, 0))
    after = pl.BlockSpec((HALO, c), lambda i: (jnp.minimum((i + 1) * (tq // HALO), t // HALO - 1), 0))
    return pl.pallas_call(
        body, name="conv_bwd", grid=(nt,),
        in_specs=[x_spec, halo_spec, blk, after, pl.BlockSpec((C_CONV, c), lambda i: (0, 0))] + extra_specs,
        out_specs=[pl.BlockSpec((tq, c), lambda i: (i, cb)), pl.BlockSpec((8, c), lambda i: (0, 0))],
        out_shape=[_du_shape(t), jax.ShapeDtypeStruct((8, c), F32)],
        input_output_aliases=aliases,
        compiler_params=_cparams(("arbitrary",)),
    )(u, u, dz, dz, w, *extra)


C_VW = C_V_HEADS * C_DK


def _softplus(x):
    return jnp.maximum(x, 0.0) + jnp.log(1.0 + jnp.exp(-jnp.abs(x)))


def _tri_masks():
    r = lax.broadcasted_iota(jnp.int32, (CHUNK, CHUNK), 0)
    c = lax.broadcasted_iota(jnp.int32, (CHUNK, CHUNK), 1)
    return r >= c, r > c


def _split_bf16(a):
    hi = a.astype(BF16)
    return hi, (a - hi.astype(F32)).astype(BF16)


def _dot_hi(a, b, dims=None):
    dims = _NN if dims is None else dims
    ah, al = _split_bf16(a)
    bh, bl = _split_bf16(b)

    def d(x, y):
        return lax.dot_general(x, y, dims, preferred_element_type=F32)

    return d(ah, bh) + (d(ah, bl) + d(al, bh))


def _dot_mask(mask, b, dims=None, mask_left=True):
    dims = _NN if dims is None else dims
    mb = mask.astype(BF16)
    b1 = b.astype(BF16)
    rest = b - b1.astype(F32)
    b2 = rest.astype(BF16)
    b3 = (rest - b2.astype(F32)).astype(BF16)
    out = None
    for p in (b1, b2, b3):
        term = (lax.dot_general(mb, p, dims, preferred_element_type=F32) if mask_left
                else lax.dot_general(p, mb, dims, preferred_element_type=F32))
        out = term if out is None else out + term
    return out


def _unit_lower_inverses(mats):
    r = lax.broadcasted_iota(jnp.int32, (CHUNK, CHUNK), 0)
    c = lax.broadcasted_iota(jnp.int32, (CHUNK, CHUNK), 1)
    eye = (r == c).astype(F32)
    xs = [eye - a for a in mats]
    ps = [_dot_hi(a, a) for a in mats]
    steps = int(math.log2(CHUNK)) - 1
    for s in range(steps):
        xs = [x + _dot_hi(x, p) for x, p in zip(xs, ps)]
        if s < steps - 1:
            ps = [_dot_hi(p, p) for p in ps]
    return xs


def _gate_tiles(cab, alog, dtb):
    pre = cab + dtb
    g = -jnp.exp(alog) * _softplus(pre)
    beta = jax.nn.sigmoid(pltpu.roll(cab, LANES - C_V_HEADS, 1))
    return g, beta, pre


def _chunk_common(kk, qk, gc, gct, beta, h, tri, strict):
    gcol, grow, bcol = gc[:, h:h + 1], gct[h:h + 1, :], beta[:, h:h + 1]
    decay = jnp.where(tri, jnp.exp(jnp.where(tri, gcol - grow, 0.0)), 0.0)
    kkd = jnp.where(strict, kk * decay, 0.0)
    attn = jnp.where(tri, qk * decay, 0.0)
    glast = gc[CHUNK - 1:CHUNK, h:h + 1]
    return gcol, bcol, decay, kkd, attn, glast


def _cab_spec():
    return pl.BlockSpec((CHUNK, LANES), lambda n: (n, COL["c_ab"] // LANES))


def _delta_prep_call(qk, v, u, alog, dtb, gather_src=None):
    t = qk.shape[0]
    nc = t // CHUNK
    scale = C_DK ** -0.5
    riding = gather_src is not None
    ng = len(gather_src) if riding else 0

    def body(q_ref, k_ref, v_ref, cab_ref, alog_ref, dtb_ref, *refs):
        if riding:
            gather_refs = (refs[:ng], refs[ng + 8:2 * ng + 8]) + tuple(refs[2 * ng + 8:])
            refs = refs[ng:ng + 8]

            @pl.when(pl.program_id(0) == 0)
            def _():
                _gather_start(*gather_refs)
        u_ref, w_ref, qd_ref, kd_ref, attn_ref, tmat_ref, gc_ref, beta_ref = refs
        tri, strict = _tri_masks()
        g, beta, _ = _gate_tiles(cab_ref[...], alog_ref[...], dtb_ref[...])
        gc = _dot_mask(tri, g)
        gct = gc.T
        gc_ref[...] = gc
        beta_ref[...] = beta
        mats, rhs = [], []
        for j in range(C_QK_HEADS):
            js = slice(j * C_DK, (j + 1) * C_DK)
            kf, qf = k_ref[:, js], q_ref[:, js] * scale
            kb, qb = kf.astype(BF16), qf.astype(BF16)
            kk = lax.dot_general(kb, kb, _NT, preferred_element_type=F32)
            qk = lax.dot_general(qb, kb, _NT, preferred_element_type=F32)
            for h in (2 * j, 2 * j + 1):
                hs = slice(h * C_DK, (h + 1) * C_DK)
                gcol, bcol, decay, kkd, attn, glast = _chunk_common(kk, qk, gc, gct, beta, h, tri, strict)
                gexp = jnp.exp(gcol)
                mats.append(kkd * bcol)
                rhs.append(jnp.concatenate([v_ref[:, hs] * bcol, kf * (bcol * gexp)], axis=1))
                qd_ref[:, hs] = (qf * gexp).astype(BF16)
                kd_ref[:, hs] = (kf * jnp.exp(glast - gcol)).astype(BF16)
                attn_ref[:, h * CHUNK:(h + 1) * CHUNK] = attn.astype(BF16)
        for h, (tmat, r) in enumerate(zip(_unit_lower_inverses(mats), rhs)):
            hs = slice(h * C_DK, (h + 1) * C_DK)
            uw = _dot_hi(tmat, r)
            u_ref[:, hs] = uw[:, :C_DK]
            w_ref[:, hs] = uw[:, C_DK:]
            tmat_ref[:, h * CHUNK:(h + 1) * CHUNK] = tmat
        if riding:
            @pl.when(pl.program_id(0) == nc - 1)
            def _():
                _gather_finish(*gather_refs)

    def blk(w):
        return pl.BlockSpec((CHUNK, w), lambda n: (n, 0))

    row = pl.BlockSpec((1, LANES), lambda n: (0, 0))
    big = jax.ShapeDtypeStruct((t, C_VW), F32)
    sq = jax.ShapeDtypeStruct((t, C_V_HEADS * CHUNK), F32)
    tile = jax.ShapeDtypeStruct((t, LANES), F32)
    half = jax.ShapeDtypeStruct((t, C_VW), BF16)
    in_specs = [blk(C_QKW), pl.BlockSpec((CHUNK, C_QKW), lambda n: (n, 1)), blk(C_VW), _cab_spec(), row, row]
    out_specs = [blk(C_VW)] * 4 + [blk(C_V_HEADS * CHUNK)] * 2 + [blk(LANES)] * 2
    out_shape = [big, big, half, half, jax.ShapeDtypeStruct(sq.shape, BF16), sq] + [tile] * 2
    args = [qk, qk, v, u, alog, dtb]
    if riding:
        in_specs += [HBM_SPEC] * ng
        out_specs += [HBM_SPEC] * ng
        out_shape += _gathered_shapes(gather_src)
        args += list(gather_src)
    return pl.pallas_call(
        body, name="delta_prep_gather" if riding else "delta_prep", grid=(nc,),
        in_specs=in_specs, out_specs=out_specs, out_shape=out_shape,
        scratch_shapes=_gather_sems(ng) if riding else [],
        compiler_params=_cparams(("arbitrary",) if riding else ("parallel",)),
    )(*args)


SCAN_SUB = 4


def _delta_scan_call(u, w, qd, kd, attn, gc):
    t = u.shape[0]
    nc = t // CHUNK
    rows = SCAN_SUB * CHUNK

    def body(u_ref, w_ref, qd_ref, kd_ref, attn_ref, gc_ref, o_ref, vn_ref, st_ref, s_ref):
        @pl.when(pl.program_id(0) == 0)
        def _():
            s_ref[...] = jnp.zeros_like(s_ref)

        hss = [slice(h * C_DK, (h + 1) * C_DK) for h in range(C_V_HEADS)]
        states = [s_ref[hs, :] for hs in hss]
        for c in range(SCAN_SUB):
            rs = slice(c * CHUNK, (c + 1) * CHUNK)
            for hs, s in zip(hss, states):
                st_ref[c, hs, :] = s
            sbs = [s.astype(BF16) for s in states]
            vns = [u_ref[rs, hs] - jnp.dot(w_ref[rs, hs].astype(BF16), sb, preferred_element_type=F32)
                   for hs, sb in zip(hss, sbs)]
            qss = [jnp.dot(qd_ref[rs, hs].astype(BF16), sb, preferred_element_type=F32) for hs, sb in zip(hss, sbs)]
            vnbs = [vn.astype(BF16) for vn in vns]
            for h, hs in enumerate(hss):
                vn_ref[rs, hs] = vnbs[h]
                o_ref[rs, hs] = qss[h] + jnp.dot(attn_ref[rs, h * CHUNK:(h + 1) * CHUNK].astype(BF16), vnbs[h],
                                                 preferred_element_type=F32)
            last = (c + 1) * CHUNK - 1
            states = [states[h] * jnp.exp(gc_ref[last:last + 1, h:h + 1])
                      + lax.dot_general(kd_ref[rs, hs].astype(BF16), vnbs[h], _TN, preferred_element_type=F32)
                      for h, hs in enumerate(hss)]
        for hs, s in zip(hss, states):
            s_ref[hs, :] = s

    def blk(wd):
        return pl.BlockSpec((rows, wd), lambda n: (n, 0))

    big = jax.ShapeDtypeStruct((t, C_VW), F32)
    return pl.pallas_call(
        body, name="delta_scan", grid=(nc // SCAN_SUB,),
        in_specs=[blk(C_VW)] * 4 + [blk(C_V_HEADS * CHUNK), blk(LANES)],
        out_specs=[blk(C_VW), blk(C_VW), pl.BlockSpec((SCAN_SUB, C_VW, C_DK), lambda n: (n, 0, 0))],
        out_shape=[big, jax.ShapeDtypeStruct((t, C_VW), BF16), jax.ShapeDtypeStruct((nc, C_VW, C_DK), F32)],
        scratch_shapes=[pltpu.VMEM((C_VW, C_DK), F32)],
        compiler_params=_cparams(("arbitrary",)),
    )(u, w, qd, kd, attn, gc)


def _delta_scan_bwd_call(do, w, qd, kd, attn, gc, vn, st):
    t = do.shape[0]
    nc = t // CHUNK
    rows = SCAN_SUB * CHUNK
    steps = nc // SCAN_SUB

    def body(do_ref, w_ref, qd_ref, kd_ref, attn_ref, gc_ref, vn_ref, st_ref,
             du_ref, dw_ref, dqd_ref, dkd_ref, dattn_ref, dgl_ref, ds_ref):
        @pl.when(pl.program_id(0) == 0)
        def _():
            ds_ref[...] = jnp.zeros_like(ds_ref)

        tri, _ = _tri_masks()
        row = lax.broadcasted_iota(jnp.int32, (8, LANES), 0)
        lane = lax.broadcasted_iota(jnp.int32, (8, LANES), 1)
        hss = [slice(h * C_DK, (h + 1) * C_DK) for h in range(C_V_HEADS)]
        css = [slice(h * CHUNK, (h + 1) * CHUNK) for h in range(C_V_HEADS)]

        def dg(a, b, dims):
            return lax.dot_general(a, b, dims, preferred_element_type=F32)

        dsps = [ds_ref[hs, :] for hs in hss]
        for c in reversed(range(SCAN_SUB)):
            rs = slice(c * CHUNK, (c + 1) * CHUNK)
            dgl = jnp.zeros((8, LANES), F32)
            ss = [st_ref[c, hs, :] for hs in hss]
            sbs = [s.astype(BF16) for s in ss]
            dspbs = [d.astype(BF16) for d in dsps]
            dobs = [do_ref[rs, hs].astype(BF16) for hs in hss]
            vnbs = [vn_ref[rs, hs].astype(BF16) for hs in hss]
            dvns = [dg(attn_ref[rs, cs].astype(BF16), dob, _TN) + dg(kd_ref[rs, hs].astype(BF16), dspb, _NN)
                    for hs, cs, dob, dspb in zip(hss, css, dobs, dspbs)]
            for h, hs in enumerate(hss):
                dqd_ref[rs, hs] = dg(dobs[h], sbs[h], _NT)
                dkd_ref[rs, hs] = dg(vnbs[h], dspbs[h], _NT)
                dattn_ref[rs, css[h]] = jnp.where(tri, dg(dobs[h], vnbs[h], _NT), 0.0)
            dvnbs = [d.astype(BF16) for d in dvns]
            for h, hs in enumerate(hss):
                du_ref[rs, hs] = dvns[h]
                dw_ref[rs, hs] = -dg(dvnbs[h], sbs[h], _NT)
                tot = jnp.sum(jnp.sum(dsps[h] * ss[h], axis=0, keepdims=True), axis=1, keepdims=True)
                dgl = jnp.where(jnp.logical_and(row == 0, lane == h), tot, dgl)
            dgl_ref[c * 8:(c + 1) * 8, :] = dgl
            last = (c + 1) * CHUNK - 1
            dsps = [dg(qd_ref[rs, hs].astype(BF16), dobs[h], _TN) + jnp.exp(gc_ref[last:last + 1, h:h + 1]) * dsps[h]
                    - dg(w_ref[rs, hs].astype(BF16), dvnbs[h], _TN) for h, hs in enumerate(hss)]
        for hs, d in zip(hss, dsps):
            ds_ref[hs, :] = d

    def blk(wd):
        return pl.BlockSpec((rows, wd), lambda n: (steps - 1 - n, 0))

    big = jax.ShapeDtypeStruct((t, C_VW), F32)
    return pl.pallas_call(
        body, name="delta_scan_bwd", grid=(steps,),
        in_specs=[blk(C_VW)] * 4 + [blk(C_V_HEADS * CHUNK), blk(LANES), blk(C_VW),
                                    pl.BlockSpec((SCAN_SUB, C_VW, C_DK), lambda n: (steps - 1 - n, 0, 0))],
        out_specs=[blk(C_VW)] * 4 + [blk(C_V_HEADS * CHUNK),
                                     pl.BlockSpec((SCAN_SUB * 8, LANES), lambda n: (steps - 1 - n, 0))],
        out_shape=[big] * 4 + [jax.ShapeDtypeStruct((t, C_V_HEADS * CHUNK), F32),
                               jax.ShapeDtypeStruct((nc * 8, LANES), F32)],
        scratch_shapes=[pltpu.VMEM((C_VW, C_DK), F32)],
        compiler_params=_cparams(("arbitrary",)),
    )(do, w, qd, kd, attn, gc, vn, st)


PREP_SUB = 2


def _delta_prep_bwd_call(qk, v, proj, alog, dtb, tmat, u, w, gc, beta, du, dw, dqd, dkd, dattn, dgl, du_buf):
    t = qk.shape[0]
    extra, extra_specs, aliases = _du_operands(du_buf, 17)
    nc = t // CHUNK
    rows = PREP_SUB * CHUNK
    scale = C_DK ** -0.5

    def body(q_ref, k_ref, v_ref, cab_ref, alog_ref, dtb_ref, tmat_ref, u_ref, w_ref, gc_ref, beta_ref,
             du_ref, dw_ref, dqd_ref, dkd_ref, dattn_ref, dgl_ref, *outs):
        dcab_ref, dqk_ref, dv_ref, dpar_ref = outs[len(extra):]
        tri, strict = _tri_masks()
        lane = lax.broadcasted_iota(jnp.int32, (CHUNK, LANES), 1)
        rowi = lax.broadcasted_iota(jnp.int32, (CHUNK, 1), 0)
        subs = range(PREP_SUB)
        rss = [slice(c * CHUNK, (c + 1) * CHUNK) for c in subs]
        betas = [beta_ref[rs, :] for rs in rss]

        def dot(x, y, dims=_NN):
            return lax.dot_general(x, y, dims, preferred_element_type=F32)

        heads = []
        for c, rs in zip(subs, rss):
            gc = gc_ref[rs, :]
            gct = gc.T
            for j in range(C_QK_HEADS):
                js = slice(j * C_DK, (j + 1) * C_DK)
                kf, qf = k_ref[rs, js], q_ref[rs, js] * scale
                kb, qb = kf.astype(BF16), qf.astype(BF16)
                kk = dot(kb, kb, _NT)
                qk = dot(qb, kb, _NT)
                for h in (2 * j, 2 * j + 1):
                    heads.append((c, rs, h, kf, qf, kb, qb) + _chunk_common(kk, qk, gc, gct, betas[c], h, tri, strict))

        def cols(h):
            return slice(h * C_DK, (h + 1) * C_DK)

        def sq(h):
            return slice(h * CHUNK, (h + 1) * CHUNK)

        dvks = [_dot_hi(tmat_ref[hd[1], sq(hd[2])],
                        jnp.concatenate([du_ref[hd[1], cols(hd[2])], dw_ref[hd[1], cols(hd[2])]], axis=1), _TN)
                for hd in heads]
        das = [-jnp.where(strict, _dot_hi(dvk, jnp.concatenate([u_ref[hd[1], cols(hd[2])], w_ref[hd[1], cols(hd[2])]],
                                                               axis=1), _NT), 0.0)
               for hd, dvk in zip(heads, dvks)]
        pre = []
        for (c, rs, h, kf, qf, kb, qb, gcol, bcol, decay, kkd, attn, glast), da in zip(heads, das):
            dattn_h = dattn_ref[rs, sq(h)]
            pre.append(((da * decay * bcol).astype(BF16), (dattn_h * decay).astype(BF16),
                        da * kkd * bcol + dattn_h * attn))
        mms = [(dot(dkk, hd[5]), dot(dkk, hd[5], _TN), dot(dqk, hd[6], _TN), dot(dqk, hd[5]))
               for hd, (dkk, dqk, e) in zip(heads, pre)]
        dq_parts, dk_parts = {}, {}
        dgc_tiles = [jnp.zeros((CHUNK, LANES), F32) for _ in subs]
        db_tiles = [jnp.zeros((CHUNK, LANES), F32) for _ in subs]
        later_tiles = [jnp.zeros((CHUNK, LANES), F32) for _ in subs]
        upper = jnp.logical_not(strict)
        for (c, rs, h, kf, qf, kb, qb, gcol, bcol, decay, kkd, attn, glast), dvk, da, (_, _, e), mm in zip(
                heads, dvks, das, pre, mms):
            hs = cols(h)
            gexp = jnp.exp(gcol)
            fdec = jnp.exp(glast - gcol)
            dvb, dkb = dvk[:, :C_DK], dvk[:, C_DK:]
            dgc = jnp.sum(e, axis=1, keepdims=True)
            later = jnp.sum(jnp.where(upper, jnp.sum(e, axis=0, keepdims=True), 0.0), axis=1, keepdims=True)
            later_tiles[c] = jnp.where(lane == h, later, later_tiles[c])
            dk_parts[c, h] = mm[0] + mm[1] + mm[2] + dkb * (bcol * gexp) + dkd_ref[rs, hs] * fdec
            dq_parts[c, h] = mm[3] + dqd_ref[rs, hs] * gexp
            dv_ref[rs, hs] = dvb * bcol
            s_kb = jnp.sum(dkb * kf, axis=1, keepdims=True)
            db = (jnp.sum(da * kkd, axis=1, keepdims=True) + jnp.sum(dvb * v_ref[rs, hs], axis=1, keepdims=True)
                  + s_kb * gexp)
            rho = jnp.sum(dkd_ref[rs, hs] * kf, axis=1, keepdims=True) * fdec
            dgc = (dgc + s_kb * bcol * gexp + jnp.sum(dqd_ref[rs, hs] * qf, axis=1, keepdims=True) * gexp - rho)
            last = jnp.sum(rho, axis=0, keepdims=True) + dgl_ref[c * 8:c * 8 + 1, h:h + 1] * jnp.exp(glast)
            dgc = dgc + jnp.where(rowi == CHUNK - 1, last, 0.0)
            dgc_tiles[c] = jnp.where(lane == h, dgc, dgc_tiles[c])
            db_tiles[c] = jnp.where(lane == h, db, db_tiles[c])
        alog = alog_ref[...]
        row8 = lax.broadcasted_iota(jnp.int32, (8, LANES), 0)
        par = jnp.zeros((8, LANES), F32)
        for c, rs in zip(subs, rss):
            for j in range(C_QK_HEADS):
                dqk_ref[rs, j * C_DK:(j + 1) * C_DK] = (dq_parts[c, 2 * j] + dq_parts[c, 2 * j + 1]) * scale
                dqk_ref[rs, C_QKW + j * C_DK:C_QKW + (j + 1) * C_DK] = dk_parts[c, 2 * j] + dk_parts[c, 2 * j + 1]
            dg = _dot_mask(upper, dgc_tiles[c]) - later_tiles[c]
            g, _, gate_pre = _gate_tiles(cab_ref[rs, :], alog, dtb_ref[...])
            dca = dg * (-jnp.exp(alog)) * jax.nn.sigmoid(gate_pre)
            beta = betas[c]
            dcab_ref[rs, :LANES] = (dca + pltpu.roll(db_tiles[c] * beta * (1.0 - beta), C_V_HEADS, 1)).astype(BF16)
            dcab_ref[rs, LANES:] = jnp.zeros((CHUNK, D_IN_PAD - COL["c_ab"] - LANES), BF16)
            par = par + jnp.where(row8 == 0, jnp.sum(dg * g, axis=0, keepdims=True),
                                  jnp.where(row8 == 1, jnp.sum(dca, axis=0, keepdims=True), 0.0))

        @pl.when(pl.program_id(0) == 0)
        def _():
            dpar_ref[...] = par

        @pl.when(pl.program_id(0) > 0)
        def _():
            dpar_ref[...] += par

    def blk(wd):
        return pl.BlockSpec((rows, wd), lambda n: (n, 0))

    row = pl.BlockSpec((1, LANES), lambda n: (0, 0))
    sqs = blk(C_V_HEADS * CHUNK)
    tail = D_IN_PAD - COL["c_ab"]
    assert COL["c_ab"] % tail == 0 and nc % PREP_SUB == 0
    return pl.pallas_call(
        body, name="delta_prep_bwd", grid=(nc // PREP_SUB,),
        in_specs=[blk(C_QKW), pl.BlockSpec((rows, C_QKW), lambda n: (n, 1)), blk(C_VW),
                  pl.BlockSpec((rows, LANES), lambda n: (n, COL["c_ab"] // LANES)), row, row, sqs,
                  blk(C_VW), blk(C_VW),
                  blk(LANES), blk(LANES), blk(C_VW), blk(C_VW), blk(C_VW), blk(C_VW), sqs,
                  pl.BlockSpec((PREP_SUB * 8, LANES), lambda n: (n, 0))] + extra_specs,
        out_specs=[pl.BlockSpec((rows, tail), lambda n: (n, COL["c_ab"] // tail)),
                   blk(2 * C_QKW), blk(C_VW), pl.BlockSpec((8, LANES), lambda n: (0, 0))],
        out_shape=[_du_shape(t), jax.ShapeDtypeStruct((t, 2 * C_QKW), F32),
                   jax.ShapeDtypeStruct((t, C_VW), F32), jax.ShapeDtypeStruct((8, LANES), F32)],
        input_output_aliases=aliases,
        compiler_params=_cparams(("arbitrary",)),
    )(qk, qk, v, proj, alog, dtb, tmat, u, w, gc, beta, du, dw, dqd, dkd, dattn, dgl, *extra)


def _z_spec(tq):
    return pl.BlockSpec((tq, C_VW), lambda i: (i, COL["c_z"] // C_VW))


def _gated_norm_fwd_call(o, u, gain, *, tq=512):
    t, w = o.shape

    def body(o_ref, z_ref, g_ref, y_ref):
        act, _ = _silu_parts(z_ref[...])
        gv = g_ref[...]
        for h in range(C_V_HEADS):
            hs = slice(h * C_DK, (h + 1) * C_DK)
            ov = o_ref[:, hs]
            r = lax.rsqrt(jnp.mean(ov * ov, axis=1, keepdims=True) + EPS)
            y_ref[:, hs] = ov * r * gv * act[:, hs]

    blk = pl.BlockSpec((tq, w), lambda i: (i, 0))
    return pl.pallas_call(
        body, name="gated_norm_fwd", grid=(t // tq,),
        in_specs=[blk, _z_spec(tq), pl.BlockSpec((1, C_DK), lambda i: (0, 0))], out_specs=blk,
        out_shape=jax.ShapeDtypeStruct((t, w), F32),
        compiler_params=_cparams(("parallel",)),
    )(o, u, gain)


def _gated_norm_bwd_call(o, u, gain, dy, du_buf, *, tq=512):
    t, w = o.shape
    nt = t // tq
    extra, extra_specs, aliases = _du_operands(du_buf, 4)

    def body(o_ref, z_ref, g_ref, dy_ref, *refs):
        dz_ref, do_ref, dg_ref, acc_ref = refs[len(extra):]
        i = pl.program_id(0)
        act, dact = _silu_parts(z_ref[...])
        gv = g_ref[...]
        part = jnp.zeros((8, C_DK), F32)
        for h in range(C_V_HEADS):
            hs = slice(h * C_DK, (h + 1) * C_DK)
            ov = o_ref[:, hs]
            r = lax.rsqrt(jnp.mean(ov * ov, axis=1, keepdims=True) + EPS)
            xh = ov * r
            dyv = dy_ref[:, hs]
            dn = dyv * act[:, hs]
            dz_ref[:, hs] = (dyv * xh * gv * dact[:, hs]).astype(BF16)
            dxh = dn * gv
            do_ref[:, hs] = r * (dxh - xh * jnp.mean(dxh * xh, axis=1, keepdims=True))
            part = part + jnp.sum((dn * xh).reshape(tq // 8, 8, C_DK), axis=0)

        @pl.when(i == 0)
        def _():
            acc_ref[...] = part

        @pl.when(i > 0)
        def _():
            acc_ref[...] += part

        @pl.when(i == nt - 1)
        def _():
            dg_ref[...] = jnp.sum(acc_ref[...], axis=0, keepdims=True)

    blk = pl.BlockSpec((tq, w), lambda i: (i, 0))
    grow = pl.BlockSpec((1, C_DK), lambda i: (0, 0))
    return pl.pallas_call(
        body, name="gated_norm_bwd", grid=(nt,),
        in_specs=[blk, _z_spec(tq), grow, blk] + extra_specs, out_specs=[_z_spec(tq), blk, grow],
        out_shape=[_du_shape(t), jax.ShapeDtypeStruct((t, w), F32), jax.ShapeDtypeStruct((1, C_DK), F32)],
        scratch_shapes=[pltpu.VMEM((8, C_DK), F32)],
        input_output_aliases=aliases,
        compiler_params=_cparams(("arbitrary",)),
    )(o, u, gain, dy, *extra)


def _gate_specs(tq):
    return [pl.BlockSpec((tq, D_MODEL), lambda i, j=j: (i, j)) for j in range(3)]


def _merge_fwd_call(ps, u, *, tq=512):
    t, w = ps[0].shape

    def body(p0, p1, p2, g0, g1, g2, y_ref):
        y_ref[...] = (jax.nn.sigmoid(g0[...]) * p0[...] + jax.nn.sigmoid(g1[...]) * p1[...]
                      + jax.nn.sigmoid(g2[...]) * p2[...]).astype(BF16)

    blk = pl.BlockSpec((tq, w), lambda i: (i, 0))
    return pl.pallas_call(
        body, name="merge_fwd", grid=(t // tq,), in_specs=[blk] * 3 + _gate_specs(tq), out_specs=blk,
        out_shape=jax.ShapeDtypeStruct((t, w), BF16),
        compiler_params=_cparams(("parallel",)),
    )(*ps, u, u, u)


def _merge_bwd_call(ps, u, dy, *, tq=256):
    t, w = dy.shape

    def body(p0, p1, p2, g0, g1, g2, dy_ref, dg_ref, dp0, dp1, dp2):
        dyv = dy_ref[...]
        for j, (p, g, dp) in enumerate(((p0, g0, dp0), (p1, g1, dp1), (p2, g2, dp2))):
            sig = jax.nn.sigmoid(g[...])
            dp[...] = (dyv * sig).astype(BF16)
            dg_ref[:, j * w:(j + 1) * w] = (dyv * p[...] * sig * (1.0 - sig)).astype(BF16)

    blk = pl.BlockSpec((tq, w), lambda i: (i, 0))
    small = jax.ShapeDtypeStruct((t, w), BF16)
    return pl.pallas_call(
        body, name="merge_bwd", grid=(t // tq,), in_specs=[blk] * 3 + _gate_specs(tq) + [blk],
        out_specs=[pl.BlockSpec((tq, 3 * w), lambda i: (i, 0))] + [blk] * 3,
        out_shape=[_du_shape(t)] + [small] * 3,
        compiler_params=_cparams(("parallel",)),
    )(*ps, u, u, u, dy)


Q_SCALE = HEAD_DIM ** -0.5
A_PARTS = ((COL["a_q"], A_W, True, Q_SCALE), (COL["a_k"], A_W, True, 1.0), (COL["a_v"], A_W, False, 1.0))
B_PARTS = ((COL["b_q"], A_W, True, Q_SCALE), (COL["b_k"], B_KVW, True, 1.0), (COL["b_v"], B_KVW, False, 1.0))
BRANCHES = ("w_branch_a", "w_branch_b", "w_branch_c")


def _layer_fwd(x, tabs, p, w_in_b, rest, rest_of, gather_src=None):
    h = _rms_fwd_call(x, p["norm_mix"], name="rms_mix_fwd", out_dtype=BF16)
    if rest[0] == "ride":
        u, (packed,) = _mm(h, w_in_b, bias=p["b_in"], tn=IN_TN, gather_src=[rest[1]], name="in_proj_fwd_gather")
    else:
        u, packed = _mm(h, w_in_b, bias=p["b_in"], tn=IN_TN, name="in_proj_fwd"), rest[1]
    wb, conv_w = rest_of(packed)
    wb = dict(wb, w_in=w_in_b)
    p = dict(p, conv_w=conv_w)
    qkv_a = _rope_gather_call(u, tabs, A_PARTS, dils=A_DILS, name="rope_a_fwd")
    os_, lses = zip(*[_attn_fwd(cfg, qkv2, None) for cfg, qkv2 in zip(ATTN_A_CFGS, qkv_a)])
    ya = _combine_fwd_call(os_, lses, A_DILS)
    qkv_b, = _rope_gather_call(u, tabs, B_PARTS, name="rope_b_fwd")
    yb, lse_b = _attn_fwd(ATTN_B_CFG, qkv_b, p["sinks"])
    zc, qk, v = _conv_prep_fwd_call(u, p["conv_w"])
    uu, ww, qd, kd, attn, tmat, gc, beta, *gathered = _delta_prep_call(qk, v, u, p["a_log"], p["dt_bias"], gather_src)
    o, vn, st = _delta_scan_call(uu, ww, qd, kd, attn, gc)
    yc = _gated_norm_fwd_call(o, u, p["c_norm"])
    ys = (ya, yb, yc)
    ps = tuple(_mm(y, wb[n], name="branch_fwd") for y, n in zip(ys, BRANCHES))
    merged = _merge_fwd_call(ps, u)
    x1 = _mm(merged, wb["w_out"], add=x, name="out_proj_fwd")
    h2 = _rms_fwd_call(x1, p["norm_ffn"], name="rms_ffn_fwd", out_dtype=BF16)
    pre, act = _mm(h2, wb["w_ff1"], relu2_out=True, name="ffn_up")
    x2 = _mm(act, wb["w_ff2"], add=x1, name="ffn_down")
    saved = dict(x=x, h=h, u=u, qkv_a=qkv_a, os_=os_, lses=lses, b_saved=(qkv_b, yb, lse_b),
                 zc=zc, qk=qk, v=v, delta=(tmat, uu, ww, gc, beta, qd, kd, attn, vn, st), o=o, ys=ys, ps=ps,
                 merged=merged, x1=x1, h2=h2, pre=pre, act=act, p=p, wb=wb)
    return x2, saved, (gathered if gathered else None)


def _layer_bwd(s, dx2, tabs):
    g, p, wb = {}, s["p"], s["wb"]
    t = dx2.shape[0]
    dpre = _mm(dx2, wb["w_ff2"], tb=True, mul_drelu2=s["pre"], out_dtype=BF16, name="ffn_dpre")
    g["w_ff2"] = _mm(s["act"], dx2, ta=True, tk=1024, name="ffn_dw2")
    g["w_ff1"] = _mm(s["h2"], dpre, ta=True, tk=1024, name="ffn_dw1")
    dh2 = _mm(dpre, wb["w_ff1"], tb=True, name="ffn_dh")
    dx1, g["norm_ffn"] = _rms_bwd_call(s["x1"], p["norm_ffn"], dh2, add=dx2, name="rms_ffn_bwd")
    dmerged = _mm(dx1, wb["w_out"], tb=True, name="out_proj_da")
    g["w_out"] = _mm(s["merged"], dx1, ta=True, tk=1024, name="out_proj_dw")
    du, *dps = _merge_bwd_call(s["ps"], s["u"], dmerged)
    dys = []
    for y, dp, n in zip(s["ys"], dps, BRANCHES):
        dys.append(_mm(dp, wb[n], tb=True, name="branch_da"))
        g[n] = _mm(y, dp, ta=True, tk=1024, name="branch_dw")
    dya, dyb, dyc = dys
    tmat, uu, ww, gc, beta, qd, kd, attn, vn, st = s["delta"]
    du, do, g["c_norm"] = _gated_norm_bwd_call(s["o"], s["u"], p["c_norm"], dyc, du)
    ddu, ddw, dqd, dkd, dattn, dgl = _delta_scan_bwd_call(do, ww, qd, kd, attn, gc, vn, st)
    du, dqk, dv, dpar = _delta_prep_bwd_call(s["qk"], s["v"], s["u"], p["a_log"], p["dt_bias"], tmat, uu, ww, gc,
                                             beta, ddu, ddw, dqd, dkd, dattn, dgl, du)
    g["a_log"], g["dt_bias"] = dpar[0:1], dpar[1:2]
    dzc = _conv_prep_dz_call(s["zc"], dqk, dv)
    du, dconv = _conv_bwd_call(s["u"], dzc, p["conv_w"], du)
    g["conv_w"] = dconv[:C_CONV]
    no_dlse = jnp.zeros((t, LANES), F32)
    dq, dk, dv_b, dsink = _attn_bwd(ATTN_B_CFG, *s["b_saved"], p["sinks"], dyb, no_dlse)
    g["sinks"] = dsink[0, :p["sinks"].shape[0]]
    du = _rope_scatter_call(du, t, [([dq], A_W, True, Q_SCALE)], COL["b_q"], tabs, name="rope_bq_bwd")
    du = _rope_scatter_call(du, t, [([dk], B_KVW, True, 1.0), ([dv_b], B_KVW, False, 1.0)], COL["b_k"], tabs,
                            name="rope_bkv_bwd")
    *dos, dl0, dl1, dl2 = _combine_bwd_call(s["os_"], s["lses"], dya, A_DILS)
    grads_a = [_attn_bwd(cfg, qkv2, o2, lse2, None, do2, dl2_)[:3]
               for cfg, qkv2, o2, lse2, do2, dl2_ in zip(ATTN_A_CFGS, s["qkv_a"], s["os_"], s["lses"], dos,
                                                         (dl0, dl1, dl2))]
    dqs, dks, dvs = zip(*grads_a)
    du = _rope_scatter_call(du, t, [(list(dqs), A_W, True, Q_SCALE), (list(dks), A_W, True, 1.0),
                                    (list(dvs), A_W, False, 1.0)],
                            COL["a_q"], tabs, dils=A_DILS, name="rope_a_bwd")
    dh = _mm(du, wb["w_in"], tb=True, tk=IN_TN, name="in_proj_da")
    g["w_in"], g["b_in"] = _mm(s["h"], du, ta=True, b_colsum=True, tn=IN_TN, tk=1024, name="in_proj_dw")
    dx, g["norm_mix"] = _rms_bwd_call(s["x"], p["norm_mix"], dh, add=dx1, name="rms_mix_bwd")
    return dx, g


def _local_step(x, params, w_in_first, rest_first, payload_of_layer, w_in_of, rest_of, tabs, tgt):
    saves = []
    w_in_blocks, rest = w_in_first, rest_first
    for layer in range(DEPTH):
        p = {n: w[layer] for n, w in params.items() if n != "norm_final"}
        nxt = payload_of_layer(layer + 1) if layer + 1 < DEPTH else None
        x, s, gathered = _layer_fwd(x, tabs, p, w_in_of(w_in_blocks), rest, rest_of, nxt)
        saves.append(s)
        if gathered is not None:
            w_in_blocks, rest = gathered[0], ("ready", gathered[1])
    loss, dx, dfinal = _loss_call(x, params["norm_final"], tgt)
    per_layer = []
    for s in reversed(saves):
        dx, g = _layer_bwd(s, dx, tabs)
        per_layer.append(g)
    per_layer.reverse()
    grads = {n: jnp.stack([g[n] for g in per_layer]) for n in per_layer[0]}
    grads["norm_final"] = dfinal
    return loss, dx, grads


def _in_cols_to_kernel(w):
    lead = w.shape[:-1]
    parts, pos = [], 0
    for _, start, width, ref_start in IN_LAYOUT:
        if start > pos:
            parts.append(jnp.zeros(lead + (start - pos,), w.dtype))
        parts.append(w[..., ref_start:ref_start + width])
        pos = start + width
    parts.append(jnp.zeros(lead + (D_IN_PAD - pos,), w.dtype))
    return jnp.concatenate(parts, axis=-1)


def _in_cols_to_reference(w):
    by_ref = sorted(IN_LAYOUT, key=lambda e: e[3])
    return jnp.concatenate([w[..., start:start + width] for _, start, width, _ in by_ref], axis=-1)


W_IN_SHARD = 8464 // N_DEV


def _w_in_from_shards(blocks):
    lead = blocks.shape[1:-1]
    parts, pos = [], 0
    for _, start, width, ref_start in IN_LAYOUT:
        if start > pos:
            parts.append(jnp.zeros(lead + (start - pos,), blocks.dtype))
        col = ref_start
        while col < ref_start + width:
            d, l = divmod(col, W_IN_SHARD)
            n = min(W_IN_SHARD - l, ref_start + width - col)
            parts.append(blocks[d, ..., l:l + n])
            col += n
        pos = start + width
    parts.append(jnp.zeros(lead + (D_IN_PAD - pos,), blocks.dtype))
    return jnp.concatenate(parts, axis=-1)


def _w_in_to_shards(g):
    by_ref = sorted(IN_LAYOUT, key=lambda e: e[3])
    blocks = []
    for d in range(N_DEV):
        lo, hi = d * W_IN_SHARD, (d + 1) * W_IN_SHARD
        parts = []
        for _, start, width, ref_start in by_ref:
            a, b = max(lo, ref_start), min(hi, ref_start + width)
            if a < b:
                parts.append(g[..., start + a - ref_start:start + b - ref_start])
        blocks.append(jnp.concatenate(parts, axis=-1))
    return jnp.stack(blocks)


def _pad_lanes(v):
    return jnp.pad(v, ((0, 0), (0, LANES - v.shape[1])))[:, None, :]


BIG = (("w_in", 2), ("conv_w", 2), ("w_branch_a", 2), ("w_branch_b", 2), ("w_branch_c", 1), ("w_out", 1),
       ("w_ff1", 2), ("w_ff2", 1))
SMALL = ("norm_mix", "b_in", "a_log", "dt_bias", "sinks", "c_norm", "norm_ffn", "norm_final")
WEIGHTS = ("norm_mix", "w_in", "b_in", "conv_w", "a_log", "dt_bias", "sinks", "c_norm", "w_branch_a",
           "w_branch_b", "w_branch_c", "w_out", "norm_ffn", "w_ff1", "w_ff2", "norm_final")
MATMUL_WEIGHTS = ("w_in", "w_branch_a", "w_branch_b", "w_branch_c", "w_out", "w_ff1", "w_ff2")
PACK_ROWS = 1024
ROW_ALIGN = 16


def _seg_rows(n):
    return -(-n // (LANES * ROW_ALIGN)) * ROW_ALIGN


def _pack(arrays, lead=0):
    parts = []
    for a in arrays:
        lead_shape = a.shape[:lead]
        n = math.prod(a.shape[lead:])
        rows = _seg_rows(n)
        if rows * LANES != n:
            a = jnp.pad(a.reshape(lead_shape + (n,)), [(0, 0)] * lead + [(0, rows * LANES - n)])
        parts.append(a.reshape(lead_shape + (rows, LANES)))
    total = sum(p.shape[lead] for p in parts)
    padded = -(-total // PACK_ROWS) * PACK_ROWS
    if padded > total:
        parts.append(jnp.zeros(parts[0].shape[:lead] + (padded - total, LANES), parts[0].dtype))
    return jnp.concatenate(parts, axis=lead)


def _unpack(buf, shapes):
    lead = buf.shape[:-2]
    out, pos = [], 0
    for shp in shapes:
        n = math.prod(shp)
        rows = _seg_rows(n)
        seg = buf[..., pos:pos + rows, :]
        if rows * LANES != n:
            seg = seg.reshape(lead + (rows * LANES,))[..., :n]
        out.append(seg.reshape(lead + tuple(shp)))
        pos += rows
    return out


def _shards_to_full(blocks, axis):
    moved = jnp.moveaxis(blocks, 0, axis)
    shp = list(blocks.shape[1:])
    shp[axis] = shp[axis] * N_DEV
    return moved.reshape(shp)


def _full_to_shards(full, axis):
    shp = list(full.shape)
    shp[axis:axis + 1] = [N_DEV, shp[axis] // N_DEV]
    return jnp.moveaxis(full.reshape(shp), axis, 0)


def _my_place():
    return lax.axis_index("x"), lax.axis_index("y"), lax.axis_index("c")


def _slot(x, y, c):
    return 4 * x + 2 * y + c


GATHER_COPIES = 7


def _gather_plan(x_ref, out_ref, send_sems, recv_sems, local_sem, base):
    x, y, c = _my_place()
    me, sibling = (x, y, c), (x, y, 1 - c)
    chips = [(1 - x, y), (x, 1 - y), (1 - x, 1 - y)]

    def copy(k, blk, to, src=None):
        dst = out_ref.at[_slot(*blk)]
        return pltpu.make_async_remote_copy(
            src_ref=dst if src is None else src, dst_ref=dst,
            send_sem=send_sems.at[base + k], recv_sem=recv_sems.at[base + k], device_id=to, device_id_type=MESH_ID)

    def own():
        mine = pltpu.make_async_copy(x_ref, out_ref.at[_slot(*me)], local_sem)
        return mine, [copy(0, me, sibling, src=x_ref)] + [copy(1 + j, me, (*chip, c), src=x_ref)
                                                          for j, chip in enumerate(chips)]

    return copy, own, me, sibling, chips, c


def _gather_plans(srcs, outs, send_sems, recv_sems, local_sems):
    return [_gather_plan(x_ref, out_ref, send_sems, recv_sems, local_sems.at[i], GATHER_COPIES * i)
            for i, (x_ref, out_ref) in enumerate(zip(srcs, outs))]


def _gather_start(srcs, outs, *sems):
    for _, own, *_ in _gather_plans(srcs, outs, *sems):
        mine, first = own()
        mine.start()
        for cp in first:
            cp.start()


def _gather_finish(srcs, outs, *sems):
    plans = _gather_plans(srcs, outs, *sems)
    passed_all = []
    for copy, own, me, sibling, chips, c in plans:
        passed = [copy(4 + j, (*chip, c), sibling) for j, chip in enumerate(chips)]
        for j, chip in enumerate(chips):
            copy(1 + j, (*chip, c), me).wait_recv()
            passed[j].start()
        passed_all.append(passed)
    for (copy, own, me, sibling, chips, c), passed in zip(plans, passed_all):
        copy(0, sibling, me).wait_recv()
        for j, chip in enumerate(chips):
            copy(4 + j, (*chip, 1 - c), me).wait_recv()
        mine, first = own()
        for cp in first + passed:
            cp.wait_send()
        mine.wait()


def _gather_sems(n):
    return [pltpu.SemaphoreType.DMA((GATHER_COPIES * n,)), pltpu.SemaphoreType.DMA((GATHER_COPIES * n,)),
            pltpu.SemaphoreType.DMA((n,))]


def _gathered_shapes(blocks):
    return [jax.ShapeDtypeStruct((N_DEV,) + b.shape, b.dtype) for b in blocks]


def _all_gather(blocks, *, name):
    n = len(blocks)

    def body(*refs):
        srcs, outs, sems = refs[:n], refs[n:2 * n], refs[2 * n:]
        _gather_start(srcs, outs, *sems)
        _gather_finish(srcs, outs, *sems)

    return pl.pallas_call(
        body, name=name, out_shape=_gathered_shapes(blocks),
        in_specs=[HBM_SPEC] * n, out_specs=[HBM_SPEC] * n,
        scratch_shapes=_gather_sems(n),
    )(*blocks)


N_CHIP = N_DEV // 2


def _swap_with_sibling(blocks, *, name):
    n = len(blocks)

    def body(*refs):
        srcs, outs, send_sems, recv_sems = refs[:n], refs[n:2 * n], refs[2 * n], refs[2 * n + 1]
        x, y, c = _my_place()
        copies = [pltpu.make_async_remote_copy(src_ref=g_ref, dst_ref=out_ref, send_sem=send_sems.at[i],
                                               recv_sem=recv_sems.at[i], device_id=(x, y, 1 - c),
                                               device_id_type=MESH_ID)
                  for i, (g_ref, out_ref) in enumerate(zip(srcs, outs))]
        for cp in copies:
            cp.start()
        for cp in copies:
            cp.wait_recv()
        for cp in copies:
            cp.wait_send()

    return pl.pallas_call(
        body, name=name,
        out_shape=[jax.ShapeDtypeStruct(b.shape, b.dtype) for b in blocks],
        in_specs=[HBM_SPEC] * n, out_specs=[HBM_SPEC] * n,
        scratch_shapes=[pltpu.SemaphoreType.DMA((n,)), pltpu.SemaphoreType.DMA((n,))],
    )(*blocks)


def _chip_all_to_all(blocks, *, name):
    n = len(blocks)
    peers = N_CHIP - 1

    def body(*refs):
        srcs, outs = refs[:n], refs[n:2 * n]
        send_sems, recv_sems, local_sems = refs[2 * n:]
        x, y, c = _my_place()
        mine_slot = 2 * x + y
        locals_, copies = [], []
        for i, (g_ref, out_ref) in enumerate(zip(srcs, outs)):
            locals_.append(pltpu.make_async_copy(g_ref.at[mine_slot], out_ref.at[mine_slot], local_sems.at[i]))
            for k in range(1, N_CHIP):
                px, py = x ^ (k >> 1), y ^ (k & 1)
                copies.append(pltpu.make_async_remote_copy(
                    src_ref=g_ref.at[2 * px + py], dst_ref=out_ref.at[mine_slot],
                    send_sem=send_sems.at[peers * i + k - 1], recv_sem=recv_sems.at[peers * i + k - 1],
                    device_id=(px, py, c), device_id_type=MESH_ID))
        for cp in locals_ + copies:
            cp.start()
        for cp in copies:
            cp.wait_recv()
        for cp in copies:
            cp.wait_send()
        for cp in locals_:
            cp.wait()

    return pl.pallas_call(
        body, name=name,
        out_shape=[jax.ShapeDtypeStruct(b.shape, b.dtype) for b in blocks],
        in_specs=[HBM_SPEC] * n, out_specs=[HBM_SPEC] * n,
        scratch_shapes=[pltpu.SemaphoreType.DMA((peers * n,)), pltpu.SemaphoreType.DMA((peers * n,)),
                        pltpu.SemaphoreType.DMA((n,))],
    )(*blocks)


def _block_rows(rows, cols):
    tr = max(8, min(rows, PACK_ROWS * LANES // (-(-cols // LANES) * LANES) // 8 * 8))
    while rows % tr:
        tr -= 8
    return tr


def _add_bf16_call(a, b, *, name):
    n, rows, cols = a.shape
    tr = _block_rows(rows, cols)

    def body(a_ref, b_ref, o_ref):
        o_ref[...] = (a_ref[...].astype(F32) + b_ref[...].astype(F32)).astype(BF16)

    blk = pl.BlockSpec((n, tr, cols), lambda i: (0, i, 0))
    return pl.pallas_call(
        body, name=name, grid=(rows // tr,), in_specs=[blk, blk], out_specs=blk,
        out_shape=jax.ShapeDtypeStruct(a.shape, BF16),
        compiler_params=_cparams(("parallel",)),
    )(a, b)


def _adamw_call(parts, w, m, v, *, name):
    rows, cols = w.shape
    tr = _block_rows(rows, cols)
    n_parts = parts.shape[0]

    def body(p_ref, w_ref, m_ref, v_ref, g_ref, d_ref, nm_ref, nv_ref):
        g = p_ref[0].astype(F32)
        for s in range(1, n_parts):
            g = g + p_ref[s].astype(F32)
        nm = ADAM_B1 * m_ref[...] + (1.0 - ADAM_B1) * g
        nv = ADAM_B2 * v_ref[...] + (1.0 - ADAM_B2) * jnp.square(g)
        m_hat = nm / (1.0 - ADAM_B1 ** ADAM_STEP)
        v_hat = nv / (1.0 - ADAM_B2 ** ADAM_STEP)
        g_ref[...] = g
        nm_ref[...] = nm
        nv_ref[...] = nv
        d_ref[...] = -ADAM_LR * (m_hat / (jnp.sqrt(v_hat) + ADAM_EPS) + ADAM_WD * w_ref[...])

    blk = pl.BlockSpec((tr, cols), lambda i: (i, 0))
    shape = jax.ShapeDtypeStruct((rows, cols), F32)
    return pl.pallas_call(
        body, name=name, grid=(rows // tr,),
        in_specs=[pl.BlockSpec((n_parts, tr, cols), lambda i: (0, i, 0)), blk, blk, blk],
        out_specs=[blk] * 4, out_shape=[shape] * 4,
        compiler_params=_cparams(("parallel",)),
    )(parts, w, m, v)


def _kernel_params(full):
    return {
        "norm_mix": full["norm_mix"][:, None, :],
        "b_in": _in_cols_to_kernel(full["b_in"])[:, None, :],
        "a_log": _pad_lanes(full["a_log"]),
        "dt_bias": _pad_lanes(full["dt_bias"]),
        "sinks": full["sinks"],
        "c_norm": full["c_norm"][:, None, :],
        "norm_ffn": full["norm_ffn"][:, None, :],
        "norm_final": full["norm_final"][None, :],
    }


def _reference_grads(g):
    return {
        "norm_mix": g["norm_mix"][:, 0, :],
        "b_in": _in_cols_to_reference(g["b_in"][:, 0, :]),
        "conv_w": g["conv_w"],
        "a_log": g["a_log"][:, 0, :C_V_HEADS],
        "dt_bias": g["dt_bias"][:, 0, :C_V_HEADS],
        "sinks": g["sinks"],
        "c_norm": g["c_norm"][:, 0, :],
        "w_branch_a": g["w_branch_a"], "w_branch_b": g["w_branch_b"], "w_branch_c": g["w_branch_c"],
        "w_out": g["w_out"],
        "norm_ffn": g["norm_ffn"][:, 0, :],
        "w_ff1": g["w_ff1"], "w_ff2": g["w_ff2"],
        "norm_final": g["norm_final"][0],
    }


def kernel(x, positions, norm_mix, w_in, b_in, conv_w, a_log, dt_bias, sinks, c_norm, w_branch_a, w_branch_b, w_branch_c, w_out, norm_ffn, w_ff1, w_ff2, norm_final, loss_target, m_norm_mix, m_w_in, m_b_in, m_conv_w, m_a_log, m_dt_bias, m_sinks, m_c_norm, m_w_branch_a, m_w_branch_b, m_w_branch_c, m_w_out, m_norm_ffn, m_w_ff1, m_w_ff2, m_norm_final, v_norm_mix, v_w_in, v_b_in, v_conv_w, v_a_log, v_dt_bias, v_sinks, v_c_norm, v_w_branch_a, v_w_branch_b, v_w_branch_c, v_w_out, v_norm_ffn, v_w_ff1, v_w_ff2, v_norm_final):
    env = dict(locals())
    weights = {n: env[n] for n in WEIGHTS}
    moments_m = {n: env["m_" + n] for n in WEIGHTS}
    moments_v = {n: env["v_" + n] for n in WEIGHTS}

    axis_of = {n: axis - 1 for n, axis in BIG}

    packed_names = [n for n in MATMUL_WEIGHTS if n != "w_in"]

    def payload_of_layer(layer):
        cw = weights["conv_w"][layer]
        c1 = cw.astype(BF16)
        c2 = (cw - c1.astype(F32)).astype(BF16)
        c3 = (cw - c1.astype(F32) - c2.astype(F32)).astype(BF16)
        return [weights["w_in"][layer].astype(BF16),
                _pack([weights[n][layer].astype(BF16) for n in packed_names] + [c1, c2, c3])]

    def rest_of(packed):
        shapes = [weights[n].shape[1:] for n in packed_names] + [weights["conv_w"].shape[1:]] * 3
        blocks = _unpack(packed, shapes)
        wb = {n: _shards_to_full(blk, axis_of[n]) for n, blk in zip(packed_names, blocks)}
        return wb, _shards_to_full(sum(b.astype(F32) for b in blocks[-3:]), axis_of["conv_w"])

    tabs = rope_tables(positions[0])
    w_in_first, = _all_gather(payload_of_layer(0)[:1], name="gather_weights")
    loss, dx, dparams = _local_step(x[0], _kernel_params({n: weights[n] for n in SMALL}), w_in_first,
                                    ("ride", payload_of_layer(0)[1]), payload_of_layer, _w_in_from_shards,
                                    rest_of, tabs, loss_target[0])
    grads = _reference_grads(dparams)
    loss = lax.psum(loss, ("x", "y", "c"))

    core = lax.axis_index("c")
    rest = [(n, axis) for n, axis in BIG if n != "w_in"]
    w_in_rows = DEPTH * D_MODEL

    def by_core(shards, which):
        sh = shards.reshape((N_CHIP, 2) + shards.shape[1:])
        return lax.dynamic_index_in_dim(sh, which, axis=1, keepdims=False).astype(BF16)

    def halves(which):
        w_in_half = by_core(_w_in_to_shards(dparams["w_in"]), which).reshape(N_CHIP, w_in_rows, W_IN_SHARD)
        return [w_in_half, _pack([by_core(_full_to_shards(grads[n], axis), which) for n, axis in rest], lead=1)]

    from_sibling = _swap_with_sibling(halves(1 - core), name="scatter_grads_d2d")
    chip_sums = [_add_bf16_call(keep, got, name="scatter_grads_add")
                 for keep, got in zip(halves(core), from_sibling)]
    w_in_parts, rest_parts = _chip_all_to_all(chip_sums, name="scatter_grads_ici")
    small_parts, = _all_gather([_pack([grads[n] for n in SMALL])], name="gather_small_grads")

    out = {}
    results = _adamw_call(w_in_parts, *[d["w_in"].reshape(w_in_rows, W_IN_SHARD)
                                        for d in (weights, moments_m, moments_v)], name="adamw_w_in")
    for kind, buf in zip(("grad", "delta", "new_m", "new_v"), results):
        out[kind, "w_in"] = buf.reshape(weights["w_in"].shape)
    for names, parts in (([n for n, _ in rest], rest_parts), (list(SMALL), small_parts)):
        shapes = [weights[n].shape for n in names]
        packed = [_pack([d[n] for n in names]) for d in (weights, moments_m, moments_v)]
        results = _adamw_call(parts, *packed, name="adamw_" + names[0])
        for kind, buf in zip(("grad", "delta", "new_m", "new_v"), results):
            for n, arr in zip(names, _unpack(buf, shapes)):
                out[kind, n] = arr
    return (loss, dx[None], *[out[kind, n] for kind in ("grad", "delta", "new_m", "new_v") for n in WEIGHTS])
```

```python
import functools
import math

import jax
import jax.numpy as jnp
from jax import lax
from jax.experimental import pallas as pl
from jax.experimental.pallas import tpu as pltpu

F32 = jnp.float32
BF16 = jnp.bfloat16

N_DEV = 8
D_MODEL = 1024
DEPTH = 2
HEAD_DIM = 64
ROT_DIM = 16
ROPE_THETA = 500000.0
BLK = 128
NEG_INF = -1e30
EPS = 1e-6
A_CONFIGS = ((128, 1), (512, 4), (2048, 16))
B_GROUP = 4
C_QK_HEADS = 4
C_V_HEADS = 8
C_DK = 128
C_CONV = 4
CHUNK = 64
ADAM_LR = 0.001
ADAM_B1 = 0.9
ADAM_B2 = 0.999
ADAM_EPS = 1e-08
ADAM_WD = 0.01
ADAM_STEP = 10

IN_LAYOUT = (
    ("gate_a", 0, 1024, 5392), ("gate_b", 1024, 1024, 6416), ("gate_c", 2048, 1024, 7440),
    ("a_q", 3072, 512, 0), ("a_k", 3584, 512, 512), ("a_v", 4096, 512, 1024), ("b_q", 4608, 512, 1536),
    ("c_z", 5120, 1024, 4352), ("c_qkv", 6144, 2048, 2304),
    ("b_k", 8192, 128, 2048), ("b_v", 8320, 128, 2176), ("c_ab", 8448, 16, 5376),
)
COL = {name: start for name, start, _, _ in IN_LAYOUT}
D_IN_PAD = 8704
IN_TN = D_IN_PAD // 4
LANES = 128
VMEM_LIMIT = 56 * 1024 * 1024


def _cparams(sem=None):
    return pltpu.CompilerParams(dimension_semantics=sem, vmem_limit_bytes=VMEM_LIMIT)


def _relu2(t):
    return jnp.square(jnp.maximum(t, 0.0))


def _mm(a, b, *, ta=False, tb=False, bias=None, a_fn=None, mul_drelu2=None, add=None,
        out_dtype=F32, relu2_out=False, b_colsum=False, gather_src=None, tm=1024, tn=1024, tk=2048, name):
    if ta:
        kdim, m = a.shape
    else:
        m, kdim = a.shape
    n = b.shape[0] if tb else b.shape[1]
    tm, tn, tk = min(tm, m), min(tn, n), min(tk, kdim)
    assert m % tm == 0 and n % tn == 0 and kdim % tk == 0, (a.shape, b.shape, tm, tn, tk)
    nk = kdim // tk
    assert not b_colsum or (m == tm and not tb and nk > 1)
    dims = (((0 if ta else 1,), (1 if tb else 0,)), ((), ()))
    extras = [e for e in (bias, mul_drelu2, add) if e is not None]
    ng = len(gather_src) if gather_src is not None else 0
    grid = (m // tm, n // tn, nk)

    def body(*refs):
        if ng:
            n_in = 2 + len(extras)
            n_out = 1 + int(relu2_out) + int(b_colsum)
            n_scr = int(nk > 1) + int(b_colsum)
            gather_refs = (refs[n_in:n_in + ng], refs[n_in + ng + n_out:n_in + 2 * ng + n_out],
                           *refs[n_in + 2 * ng + n_out + n_scr:])
            refs = refs[:n_in] + refs[n_in + ng:n_in + ng + n_out] + refs[n_in + 2 * ng + n_out:]
            at_first = functools.reduce(jnp.logical_and, [pl.program_id(d) == 0 for d in range(3)])
            at_last = functools.reduce(jnp.logical_and, [pl.program_id(d) == grid[d] - 1 for d in range(3)])

            @pl.when(at_first)
            def _():
                _gather_start(*gather_refs)
        a_ref, b_ref = refs[0], refs[1]
        pos = 2
        bias_ref = pre_ref = add_ref = None
        if bias is not None:
            bias_ref = refs[pos]; pos += 1
        if mul_drelu2 is not None:
            pre_ref = refs[pos]; pos += 1
        if add is not None:
            add_ref = refs[pos]; pos += 1
        o_ref = refs[pos]
        pos += 1
        r_ref = None
        if relu2_out:
            r_ref = refs[pos]; pos += 1
        cs_ref = None
        if b_colsum:
            cs_ref = refs[pos]; pos += 1
        acc_ref = refs[pos] if nk > 1 else None
        cs_acc = refs[pos + 1] if b_colsum else None

        av = a_ref[...]
        if a_fn is not None:
            av = a_fn(av)
        bv = b_ref[...]
        part = lax.dot_general(av.astype(BF16), bv.astype(BF16), dims,
                               preferred_element_type=F32)
        if b_colsum:
            cs_part = jnp.sum(bv.astype(F32).reshape(tk // 8, 8, tn), axis=0)

        def finish(acc):
            if bias_ref is not None:
                acc = acc + bias_ref[...]
            if pre_ref is not None:
                acc = acc * (2.0 * jnp.maximum(pre_ref[...], 0.0))
            if add_ref is not None:
                acc = acc + add_ref[...]
            o_ref[...] = acc.astype(out_dtype)
            if r_ref is not None:
                r_ref[...] = _relu2(acc).astype(BF16)

        if nk == 1:
            finish(part)
        else:
            k = pl.program_id(2)

            @pl.when(k == 0)
            def _():
                acc_ref[...] = part
                if b_colsum:
                    cs_acc[...] = cs_part

            @pl.when(k > 0)
            def _():
                acc_ref[...] += part
                if b_colsum:
                    cs_acc[...] += cs_part

            @pl.when(k == nk - 1)
            def _():
                finish(acc_ref[...])
                if b_colsum:
                    cs_ref[...] = jnp.sum(cs_acc[...], axis=0, keepdims=True)
        if ng:
            @pl.when(at_last)
            def _():
                _gather_finish(*gather_refs)

    a_spec = (pl.BlockSpec((tk, tm), lambda i, j, k: (k, i)) if ta
              else pl.BlockSpec((tm, tk), lambda i, j, k: (i, k)))
    b_spec = (pl.BlockSpec((tn, tk), lambda i, j, k: (j, k)) if tb
              else pl.BlockSpec((tk, tn), lambda i, j, k: (k, j)))
    in_specs = [a_spec, b_spec]
    if bias is not None:
        in_specs.append(pl.BlockSpec((1, tn), lambda i, j, k: (0, j)))
    for _ in extras[(1 if bias is not None else 0):]:
        in_specs.append(pl.BlockSpec((tm, tn), lambda i, j, k: (i, j)))
    o_spec = pl.BlockSpec((tm, tn), lambda i, j, k: (i, j))
    out_specs, out_shape = [o_spec], [jax.ShapeDtypeStruct((m, n), out_dtype)]
    scratch = [pltpu.VMEM((tm, tn), F32)] if nk > 1 else []
    if relu2_out:
        out_specs.append(o_spec)
        out_shape.append(jax.ShapeDtypeStruct((m, n), BF16))
    if b_colsum:
        out_specs.append(pl.BlockSpec((1, tn), lambda i, j, k: (0, j)))
        out_shape.append(jax.ShapeDtypeStruct((1, n), F32))
        scratch.append(pltpu.VMEM((8, tn), F32))
    if ng:
        in_specs += [HBM_SPEC] * ng
        out_specs += [HBM_SPEC] * ng
        out_shape += _gathered_shapes(gather_src)
        scratch += _gather_sems(ng)
    single = len(out_specs) == 1
    outs = pl.pallas_call(
        body, name=name,
        grid=grid,
        in_specs=in_specs,
        out_specs=out_specs[0] if single else out_specs,
        out_shape=out_shape[0] if single else out_shape,
        scratch_shapes=scratch,
        compiler_params=_cparams(("arbitrary",) * 3 if ng else ("parallel", "parallel", "arbitrary")),
    )(a, b, *extras, *(gather_src or []))
    if not ng:
        return outs
    n_out = len(outs) - ng
    return (outs[0] if n_out == 1 else tuple(outs[:n_out])), list(outs[n_out:])


def _rms_fwd_call(x, g, *, name, out_dtype=F32, tq=512):
    t, d = x.shape

    def body(x_ref, g_ref, y_ref):
        xv = x_ref[...]
        r = lax.rsqrt(jnp.mean(xv * xv, axis=-1, keepdims=True) + EPS)
        y_ref[...] = (xv * r * g_ref[...]).astype(out_dtype)

    return pl.pallas_call(
        body, name=name, grid=(t // tq,),
        in_specs=[pl.BlockSpec((tq, d), lambda i: (i, 0)), pl.BlockSpec((1, d), lambda i: (0, 0))],
        out_specs=pl.BlockSpec((tq, d), lambda i: (i, 0)),
        out_shape=jax.ShapeDtypeStruct((t, d), out_dtype),
        compiler_params=_cparams(("parallel",)),
    )(x, g)


def _rms_bwd_call(x, g, dy, *, name, add=None, tq=512):
    t, d = x.shape
    nt = t // tq

    def body(*refs):
        if add is None:
            x_ref, g_ref, dy_ref, dx_ref, dg_ref, acc_ref = refs
        else:
            x_ref, g_ref, dy_ref, add_ref, dx_ref, dg_ref, acc_ref = refs
        i = pl.program_id(0)
        xv = x_ref[...]
        r = lax.rsqrt(jnp.mean(xv * xv, axis=-1, keepdims=True) + EPS)
        xh = xv * r
        dyv = dy_ref[...]
        dxh = dyv * g_ref[...]
        dx = r * (dxh - xh * jnp.mean(dxh * xh, axis=-1, keepdims=True))
        dx_ref[...] = dx if add is None else dx + add_ref[...]
        part = jnp.sum((dyv * xh).reshape(tq // 8, 8, d), axis=0)

        @pl.when(i == 0)
        def _():
            acc_ref[...] = part

        @pl.when(i > 0)
        def _():
            acc_ref[...] += part

        @pl.when(i == nt - 1)
        def _():
            dg_ref[...] = jnp.sum(acc_ref[...], axis=0, keepdims=True)

    blk = pl.BlockSpec((tq, d), lambda i: (i, 0))
    row = pl.BlockSpec((1, d), lambda i: (0, 0))
    extra = [] if add is None else [add]
    return pl.pallas_call(
        body, name=name, grid=(nt,),
        in_specs=[blk, row, blk] + [blk] * len(extra),
        out_specs=[blk, row],
        out_shape=[jax.ShapeDtypeStruct((t, d), F32), jax.ShapeDtypeStruct((1, d), F32)],
        scratch_shapes=[pltpu.VMEM((8, d), F32)],
        compiler_params=_cparams(("arbitrary",)),
    )(x, g, dy, *extra)


def _loss_call(x, g, tgt, *, tq=512):
    t, d = x.shape
    nt = t // tq

    def body(x_ref, g_ref, t_ref, loss_ref, dx_ref, dg_ref, acc_ref, sq_ref):
        i = pl.program_id(0)
        xv = x_ref[...]
        r = lax.rsqrt(jnp.mean(xv * xv, axis=-1, keepdims=True) + EPS)
        xh = xv * r
        gv = g_ref[...]
        err = xh * gv - t_ref[...]
        dyv = err * (1.0 / d)
        dxh = dyv * gv
        dx_ref[...] = r * (dxh - xh * jnp.mean(dxh * xh, axis=-1, keepdims=True))
        part = jnp.sum((dyv * xh).reshape(tq // 8, 8, d), axis=0)
        sq = jnp.sum((err * err).reshape(tq // 8, 8, d), axis=0)

        @pl.when(i == 0)
        def _():
            acc_ref[...] = part
            sq_ref[...] = sq

        @pl.when(i > 0)
        def _():
            acc_ref[...] += part
            sq_ref[...] += sq

        @pl.when(i == nt - 1)
        def _():
            dg_ref[...] = jnp.sum(acc_ref[...], axis=0, keepdims=True)
            tot = jnp.sum(jnp.sum(sq_ref[...], axis=0, keepdims=True), axis=1, keepdims=True)
            loss_ref[...] = jnp.broadcast_to(tot * (0.5 / d), (8, LANES))

    blk = pl.BlockSpec((tq, d), lambda i: (i, 0))
    row = pl.BlockSpec((1, d), lambda i: (0, 0))
    loss, dx, dg = pl.pallas_call(
        body, name="loss", grid=(nt,),
        in_specs=[blk, row, blk],
        out_specs=[pl.BlockSpec((8, LANES), lambda i: (0, 0)), blk, row],
        out_shape=[jax.ShapeDtypeStruct((8, LANES), F32), jax.ShapeDtypeStruct((t, d), F32),
                   jax.ShapeDtypeStruct((1, d), F32)],
        scratch_shapes=[pltpu.VMEM((8, d), F32), pltpu.VMEM((8, d), F32)],
        compiler_params=_cparams(("arbitrary",)),
    )(x, g, tgt)
    return loss[0, 0], dx, dg


MESH_ID = pl.DeviceIdType.MESH
HBM_SPEC = pl.BlockSpec(memory_space=pl.ANY)


def rope_tables(positions):
    half = ROT_DIM // 2
    inv_freq = jnp.power(ROPE_THETA, -jnp.arange(0, ROT_DIM, 2, dtype=F32) / ROT_DIM)
    in_head = jnp.arange(LANES) % HEAD_DIM
    rot = in_head < ROT_DIM
    freq = jnp.where(rot, inv_freq[in_head % half], 0.0)
    ang = positions.astype(F32)[:, None] * freq[None, :]
    cos, sin = jnp.cos(ang), jnp.sin(ang)
    b = jnp.where(jnp.logical_and(rot, in_head >= half)[None, :], sin, 0.0)
    c = jnp.where((in_head < half)[None, :], -sin, 0.0)
    return cos, b, c


def _rope_chunk(xs, a, b, c, transpose):
    half = ROT_DIM // 2
    if transpose:
        return xs * a + pltpu.roll(xs * b, LANES - half, 1) + pltpu.roll(xs * c, half, 1)
    return xs * a + pltpu.roll(xs, half, 1) * b + pltpu.roll(xs, LANES - half, 1) * c


def _dilated_spec(tq, w, d):
    return pl.BlockSpec((tq // d, d * w), lambda i: (i, 0))


def _load_dilated(ref, stage_ref, d, w, j):
    cs = slice(j * LANES, (j + 1) * LANES)
    if d == 1:
        return ref[:, cs].astype(F32)
    for r in range(d):
        stage_ref[pl.ds(r, ref.shape[0], stride=d), :] = ref[:, r * w + j * LANES:r * w + (j + 1) * LANES].astype(F32)
    return stage_ref[...]


def _store_dilated(ref, stage_ref, d, w, j, val):
    if d == 1:
        ref[:, j * LANES:(j + 1) * LANES] = val.astype(ref.dtype)
        return
    stage_ref[...] = val
    for r in range(d):
        rows = stage_ref[pl.ds(r, ref.shape[0], stride=d), :]
        ref[:, r * w + j * LANES:r * w + (j + 1) * LANES] = rows.astype(ref.dtype)


def _stage_buffers(tq, n):
    return [pltpu.VMEM((tq, LANES), F32)] * n


def _rope_gather_call(u, tabs, parts, *, name, dils=(1,), tq=512):
    t = u.shape[0]
    total = sum(w for _, w, _, _ in parts)
    assert all(start % w == 0 for start, w, _, _ in parts)
    n_stage = (total // LANES) * sum(d > 1 for d in dils)

    def body(a_ref, b_ref, c_ref, *refs):
        x_refs, o_refs = refs[:len(parts)], refs[len(parts):len(parts) + len(dils)]
        stages = iter(refs[len(parts) + len(dils):])
        a, b, c = a_ref[...], b_ref[...], c_ref[...]
        g = 0
        for x_ref, (_, w, roped, scale) in zip(x_refs, parts):
            for j in range(w // LANES):
                xs = x_ref[:, j * LANES:(j + 1) * LANES]
                val = _rope_chunk(xs, a, b, c, False) if roped else xs
                val = val * scale if scale != 1.0 else val
                for o_ref, d in zip(o_refs, dils):
                    _store_dilated(o_ref, next(stages) if d > 1 else None, d, total, g, val)
                g += 1

    tab_spec = pl.BlockSpec((tq, LANES), lambda i: (i, 0))
    return pl.pallas_call(
        body, name=name, grid=(t // tq,),
        in_specs=[tab_spec] * 3 + [pl.BlockSpec((tq, w), lambda i, cb=start // w: (i, cb)) for start, w, _, _ in parts],
        out_specs=[_dilated_spec(tq, total, d) for d in dils],
        out_shape=[jax.ShapeDtypeStruct((t // d, d * total), BF16) for d in dils],
        scratch_shapes=_stage_buffers(tq, n_stage),
        compiler_params=_cparams(("parallel",)),
    )(*tabs, *([u] * len(parts)))


def _du_operands(du_buf, n_inputs):
    if du_buf is None:
        return [], [], {}
    return [du_buf], [HBM_SPEC], {n_inputs: 0}


def _du_shape(t):
    return jax.ShapeDtypeStruct((t, D_IN_PAD), BF16)


def _rope_scatter_call(du_buf, t, pieces, col, tabs, *, name, dils=(1,), tq=512):
    total = sum(w for _, w, _, _ in pieces)
    assert col % total == 0 and all(len(arrs) == len(dils) for arrs, _, _, _ in pieces)
    arrays = [a for arrs, _, _, _ in pieces for a in arrs]
    extra, extra_specs, aliases = _du_operands(du_buf, 3 + len(arrays))
    n_stage = (total // LANES) * sum(d > 1 for d in dils)

    def body(a_ref, b_ref, c_ref, *refs):
        o_ref = refs[len(arrays) + len(extra)]
        stages = iter(refs[len(arrays) + len(extra) + 1:])
        a, b, c = a_ref[...], b_ref[...], c_ref[...]
        k = off = 0
        for arrs, w, roped, scale in pieces:
            mine = refs[k:k + len(arrs)]
            k += len(arrs)
            for j in range(w // LANES):
                xs = None
                for r, d in zip(mine, dils):
                    part = _load_dilated(r, next(stages) if d > 1 else None, d, w, j)
                    xs = part if xs is None else xs + part
                if scale != 1.0:
                    xs = xs * scale
                val = _rope_chunk(xs, a, b, c, True) if roped else xs
                o_ref[:, off + j * LANES:off + (j + 1) * LANES] = val.astype(BF16)
            off += w

    tab_spec = pl.BlockSpec((tq, LANES), lambda i: (i, 0))
    in_specs = [tab_spec] * 3 + [_dilated_spec(tq, w, d) for _, w, _, _ in pieces for d in dils]
    return pl.pallas_call(
        body, name=name, grid=(t // tq,),
        in_specs=in_specs + extra_specs,
        out_specs=pl.BlockSpec((tq, total), lambda i: (i, col // total)),
        out_shape=_du_shape(t), input_output_aliases=aliases,
        scratch_shapes=_stage_buffers(tq, n_stage),
        compiler_params=_cparams(("parallel",)),
    )(*tabs, *arrays, *extra)


def _band_masks(first_block, max_dist):
    qi = lax.broadcasted_iota(jnp.int32, (BLK, BLK), 0)
    kj = lax.broadcasted_iota(jnp.int32, (BLK, BLK), 1)
    valid_prev = jnp.logical_and(kj >= qi + (BLK - max_dist), jnp.logical_not(first_block))
    valid_cur = kj <= qi
    return valid_prev, valid_cur


_NN = (((1,), (0,)), ((), ()))
_NT = (((1,), (1,)), ((), ()))
_TN = (((0,), (0,)), ((), ()))


HEAD_STAGE = 8


def _attn_row_maps(nb):
    def cur(i):
        return jnp.minimum(i, nb - 1)

    def prev(i):
        return jnp.maximum(jnp.minimum(i, nb - 1) - 1, 0)

    return cur, prev


def _dil_spec(w, dil, rows, seg=None, off=0):
    seg = w if seg is None else seg
    assert off % w == 0 and (dil == 1 or seg % w == 0)
    return pl.BlockSpec((BLK, w), lambda r, i: (rows(i), (r * seg + off) // w))


def _dil_shape(l, dil, w, dtype=F32):
    return jax.ShapeDtypeStruct((l, dil * w), dtype)


def _attn_fwd_call(qkv2, sink, *, dil, group, max_dist, seg, offs, qw, kw, name):
    l = qkv2.shape[0]
    nh = qw // HEAD_DIM
    nb = l // BLK
    use_sink = sink is not None

    def body(*refs):
        if use_sink:
            sink_ref, refs = refs[0], refs[1:]
        q_ref, kp_ref, kc_ref, vp_ref, vc_ref, o_ref, lse_ref = refs
        valid_prev, valid_cur = _band_masks(pl.program_id(1) == 0, max_dist)
        lane = lax.broadcasted_iota(jnp.int32, (BLK, LANES), 1)
        lse_tile = jnp.zeros((BLK, LANES), F32)

        def dot(a, b, dims=_NN):
            return lax.dot_general(a, b, dims, preferred_element_type=F32)

        for g0 in range(0, nh, HEAD_STAGE):
            heads = list(range(g0, min(g0 + HEAD_STAGE, nh)))
            kv = {}
            for kh in sorted({h // group for h in heads}):
                ks = slice(kh * HEAD_DIM, (kh + 1) * HEAD_DIM)
                kv[kh] = tuple(ref[:, ks].astype(BF16) for ref in (kp_ref, kc_ref, vp_ref, vc_ref))
            qs = [q_ref[:, h * HEAD_DIM:(h + 1) * HEAD_DIM].astype(BF16) for h in heads]
            sps = [jnp.where(valid_prev, dot(qh, kv[h // group][0], _NT), NEG_INF) for h, qh in zip(heads, qs)]
            scs = [jnp.where(valid_cur, dot(qh, kv[h // group][1], _NT), NEG_INF) for h, qh in zip(heads, qs)]
            ms = [jnp.maximum(jnp.max(sp, axis=1, keepdims=True), jnp.max(sc, axis=1, keepdims=True))
                  for sp, sc in zip(sps, scs)]
            if use_sink:
                ms = [jnp.maximum(m, sink_ref[h]) for h, m in zip(heads, ms)]
            pps = [jnp.exp(sp - m) for sp, m in zip(sps, ms)]
            pcs = [jnp.exp(sc - m) for sc, m in zip(scs, ms)]
            dens = [jnp.sum(pp, axis=1, keepdims=True) + jnp.sum(pc, axis=1, keepdims=True)
                    for pp, pc in zip(pps, pcs)]
            if use_sink:
                dens = [den + jnp.exp(sink_ref[h] - m) for h, den, m in zip(heads, dens, ms)]
            outs = [dot(pp.astype(BF16), kv[h // group][2]) + dot(pc.astype(BF16), kv[h // group][3])
                    for h, pp, pc in zip(heads, pps, pcs)]
            for h, o, den, m in zip(heads, outs, dens, ms):
                o_ref[:, h * HEAD_DIM:(h + 1) * HEAD_DIM] = o / den
                lse_tile = jnp.where(lane == h, m + jnp.log(den), lse_tile)
        lse_ref[...] = lse_tile

    cur, prev = _attn_row_maps(nb)
    o_spec, lse_spec = _dil_spec(qw, dil, cur), _dil_spec(LANES, dil, cur)
    in_specs = [_dil_spec(qw, dil, cur, seg, offs[0]),
                _dil_spec(kw, dil, prev, seg, offs[1]), _dil_spec(kw, dil, cur, seg, offs[1]),
                _dil_spec(kw, dil, prev, seg, offs[2]), _dil_spec(kw, dil, cur, seg, offs[2])]
    args = [qkv2] * 5
    if use_sink:
        in_specs = [pl.BlockSpec(memory_space=pltpu.SMEM)] + in_specs
        args = [sink] + args
    return pl.pallas_call(
        body, name=name, grid=(dil, nb),
        in_specs=in_specs,
        out_specs=[o_spec, lse_spec],
        out_shape=[_dil_shape(l, dil, qw), _dil_shape(l, dil, LANES)],
        compiler_params=_cparams(("parallel", "parallel")),
    )(*args)


def _attn_bwd_call(qkv2, sink, o2, lse2, do2, dlse2, *, dil, group, max_dist, seg, offs, qw, kw, name):
    l = qkv2.shape[0]
    nh = qw // HEAD_DIM
    nb = l // BLK
    use_sink = sink is not None

    def body(*refs):
        if use_sink:
            sink_ref, refs = refs[0], refs[1:]
        (q_ref, kp_ref, kc_ref, vp_ref, vc_ref, o_ref, lse_ref, do_ref, dlse_ref,
         dq_ref, dk_ref, dv_ref, dsink_ref, ck_ref, cv_ref) = refs
        step = pl.program_id(1)

        @pl.when(jnp.logical_and(pl.program_id(0) == 0, step == 0))
        def _():
            dsink_ref[...] = jnp.zeros_like(dsink_ref)

        @pl.when(step == 0)
        def _():
            ck_ref[...] = jnp.zeros_like(ck_ref)
            cv_ref[...] = jnp.zeros_like(cv_ref)

        def dot(a, b, dims=_NN):
            return lax.dot_general(a, b, dims, preferred_element_type=F32)

        @pl.when(step < nb)
        def _():
            valid_prev, valid_cur = _band_masks(step == 0, max_dist)
            row = lax.broadcasted_iota(jnp.int32, (8, LANES), 0)
            lanes8 = lax.broadcasted_iota(jnp.int32, (8, LANES), 1)
            ds_tile = jnp.zeros((8, LANES), F32)
            for g0 in range(0, nh, HEAD_STAGE):
                heads = list(range(g0, min(g0 + HEAD_STAGE, nh)))
                hss = [slice(h * HEAD_DIM, (h + 1) * HEAD_DIM) for h in heads]
                kv = {}
                for kh in sorted({h // group for h in heads}):
                    ks = slice(kh * HEAD_DIM, (kh + 1) * HEAD_DIM)
                    kv[kh] = tuple(ref[:, ks].astype(BF16) for ref in (kp_ref, kc_ref, vp_ref, vc_ref))
                qs = [q_ref[:, hs].astype(BF16) for hs in hss]
                dos = [do_ref[:, hs] for hs in hss]
                dobs = [d.astype(BF16) for d in dos]
                lses = [lse_ref[:, h:h + 1] for h in heads]
                sps = [dot(qh, kv[h // group][0], _NT) for h, qh in zip(heads, qs)]
                scs = [dot(qh, kv[h // group][1], _NT) for h, qh in zip(heads, qs)]
                dpps = [dot(dob, kv[h // group][2], _NT) for h, dob in zip(heads, dobs)]
                dpcs = [dot(dob, kv[h // group][3], _NT) for h, dob in zip(heads, dobs)]
                pps = [jnp.where(valid_prev, jnp.exp(jnp.where(valid_prev, sp, NEG_INF) - ls), 0.0)
                       for sp, ls in zip(sps, lses)]
                pcs = [jnp.where(valid_cur, jnp.exp(jnp.where(valid_cur, sc, NEG_INF) - ls), 0.0)
                       for sc, ls in zip(scs, lses)]
                deltas = [jnp.sum(d * o_ref[:, hs], axis=1, keepdims=True) for d, hs in zip(dos, hss)]
                corrs = [dlse_ref[:, h:h + 1] - dl for h, dl in zip(heads, deltas)]
                dsps = [(pp * (dp + c)).astype(BF16) for pp, dp, c in zip(pps, dpps, corrs)]
                dscs = [(pc * (dp + c)).astype(BF16) for pc, dp, c in zip(pcs, dpcs, corrs)]
                for h, hs, dsp, dsc in zip(heads, hss, dsps, dscs):
                    dq = dot(dsp, kv[h // group][0]) + dot(dsc, kv[h // group][1])
                    dq_ref[:, hs] = dq.astype(BF16)
                parts = [(dot(dsc, qh, _TN), dot(dsp, qh, _TN),
                          dot(pc.astype(BF16), dob, _TN), dot(pp.astype(BF16), dob, _TN))
                         for dsc, dsp, qh, pc, pp, dob in zip(dscs, dsps, qs, pcs, pps, dobs)]
                for kh in kv:
                    ks = slice(kh * HEAD_DIM, (kh + 1) * HEAD_DIM)
                    mine = [p for h, p in zip(heads, parts) if h // group == kh]
                    dkc, dkp, dvc, dvp = (sum(p[j] for p in mine[1:]) + mine[0][j] for j in range(4))
                    dk_ref[:, ks] = (ck_ref[:, ks] + dkp).astype(BF16)
                    dv_ref[:, ks] = (cv_ref[:, ks] + dvp).astype(BF16)
                    ck_ref[:, ks] = dkc
                    cv_ref[:, ks] = dvc
                if use_sink:
                    for h, ls, dl in zip(heads, lses, deltas):
                        val = -jnp.sum(jnp.exp(sink_ref[h] - ls) * dl, axis=0, keepdims=True)
                        ds_tile = jnp.where(jnp.logical_and(row == 0, lanes8 == h), val, ds_tile)
            if use_sink:
                dsink_ref[...] += ds_tile

        @pl.when(step == nb)
        def _():
            dk_ref[...] = ck_ref[...].astype(BF16)
            dv_ref[...] = cv_ref[...].astype(BF16)

    cur, prev = _attn_row_maps(nb)
    q_spec, lse_spec = _dil_spec(qw, dil, cur), _dil_spec(LANES, dil, cur)
    lag_spec = _dil_spec(kw, dil, lambda i: jnp.maximum(i - 1, 0))
    in_specs = [_dil_spec(qw, dil, cur, seg, offs[0]),
                _dil_spec(kw, dil, prev, seg, offs[1]), _dil_spec(kw, dil, cur, seg, offs[1]),
                _dil_spec(kw, dil, prev, seg, offs[2]), _dil_spec(kw, dil, cur, seg, offs[2]),
                q_spec, lse_spec, q_spec, lse_spec]
    args = [qkv2] * 5 + [o2, lse2, do2, dlse2]
    if use_sink:
        in_specs = [pl.BlockSpec(memory_space=pltpu.SMEM)] + in_specs
        args = [sink] + args
    kv_shape = _dil_shape(l, dil, kw, BF16)
    return pl.pallas_call(
        body, name=name, grid=(dil, nb + 1),
        in_specs=in_specs,
        out_specs=[q_spec, lag_spec, lag_spec, pl.BlockSpec((8, LANES), lambda r, i: (0, 0))],
        out_shape=[_dil_shape(l, dil, qw, BF16), kv_shape, kv_shape,
                   jax.ShapeDtypeStruct((8, LANES), F32)],
        scratch_shapes=[pltpu.VMEM((BLK, kw), F32), pltpu.VMEM((BLK, kw), F32)],
        compiler_params=_cparams(("arbitrary", "arbitrary")),
    )(*args)


def _attn_config(tag, dil, group, max_dist, seg, offs, qw, kw):
    return dict(name=tag, dil=dil, group=group, max_dist=max_dist, seg=seg, offs=offs, qw=qw, kw=kw)


A_W = 8 * HEAD_DIM
ATTN_A_CFGS = tuple(_attn_config("attn_a%d" % dil, dil, 1, window // dil, 3 * A_W, (0, A_W, 2 * A_W), A_W, A_W)
                    for window, dil in A_CONFIGS)
B_KVW = 2 * HEAD_DIM
A_DILS = tuple(cfg["dil"] for cfg in ATTN_A_CFGS)
ATTN_B_CFG = _attn_config("attn_b", 1, B_GROUP, BLK - 1, A_W + 2 * B_KVW, (0, A_W, A_W + B_KVW), A_W, B_KVW)


def _attn_fwd(cfg, qkv2, sink):
    kw = {k: v for k, v in cfg.items() if k != "name"}
    return _attn_fwd_call(qkv2, sink, name=cfg["name"] + "_fwd", **kw)


def _attn_bwd(cfg, qkv2, o2, lse2, sink, do2, dlse2):
    kw = {k: v for k, v in cfg.items() if k != "name"}
    return _attn_bwd_call(qkv2, sink, o2, lse2, do2, dlse2, name=cfg["name"] + "_bwd", **kw)


def _head_expand():
    r = lax.broadcasted_iota(jnp.int32, (LANES, 8 * HEAD_DIM), 0)
    c = lax.broadcasted_iota(jnp.int32, (LANES, 8 * HEAD_DIM), 1)
    return (c // HEAD_DIM == r).astype(F32)


def _combine_weights(l0, l1, l2):
    m = jnp.maximum(jnp.maximum(l0, l1), l2)
    e0, e1, e2 = jnp.exp(l0 - m), jnp.exp(l1 - m), jnp.exp(l2 - m)
    inv = 1.0 / (e0 + e1 + e2)
    return e0 * inv, e1 * inv, e2 * inv


def _combine_fwd_call(os_, lses, dils, *, tq=512):
    w = os_[0].shape[1] // dils[0]
    t = os_[0].shape[0] * dils[0]
    groups = w // LANES
    n_stage = (groups + 1) * sum(d > 1 for d in dils)

    def body(*refs):
        o_refs, l_refs, y_ref = refs[:3], refs[3:6], refs[6]
        stages = iter(refs[7:])
        ws = _combine_weights(*[_load_dilated(l, next(stages) if d > 1 else None, d, LANES, 0)
                                for l, d in zip(l_refs, dils)])
        e = _head_expand()
        wide = [_dot_mask(e, wt, mask_left=False) for wt in ws]
        for j in range(groups):
            cs = slice(j * LANES, (j + 1) * LANES)
            y = None
            for o_ref, d, wd in zip(o_refs, dils, wide):
                term = wd[:, cs] * _load_dilated(o_ref, next(stages) if d > 1 else None, d, w, j)
                y = term if y is None else y + term
            y_ref[:, cs] = y

    return pl.pallas_call(
        body, name="combine_fwd", grid=(t // tq,),
        in_specs=[_dilated_spec(tq, w, d) for d in dils] + [_dilated_spec(tq, LANES, d) for d in dils],
        out_specs=pl.BlockSpec((tq, w), lambda i: (i, 0)),
        out_shape=jax.ShapeDtypeStruct((t, w), F32),
        scratch_shapes=_stage_buffers(tq, n_stage),
        compiler_params=_cparams(("parallel",)),
    )(*os_, *lses)


def _combine_bwd_call(os_, lses, dy, dils, *, tq=512):
    t, w = dy.shape
    groups = w // LANES
    n_stage = 2 * (groups + 1) * sum(d > 1 for d in dils)

    def body(*refs):
        o_refs, l_refs, dy_ref, do_refs, dl_refs = refs[:3], refs[3:6], refs[6], refs[7:10], refs[10:13]
        stages = iter(refs[13:])

        def stage(d):
            return next(stages) if d > 1 else None

        ws = _combine_weights(*[_load_dilated(l, stage(d), d, LANES, 0) for l, d in zip(l_refs, dils)])
        e = _head_expand()
        dyv = dy_ref[...]
        dws = []
        for o_ref, do_ref, d, wt in zip(o_refs, do_refs, dils, ws):
            do = _dot_mask(e, wt, mask_left=False) * dyv
            for j in range(groups):
                _store_dilated(do_ref, stage(d), d, w, j, do[:, j * LANES:(j + 1) * LANES])
            ov = jnp.concatenate([_load_dilated(o_ref, stage(d), d, w, j) for j in range(groups)], axis=1)
            dws.append(_dot_mask(e, dyv * ov, _NT, mask_left=False))
        mean = ws[0] * dws[0] + ws[1] * dws[1] + ws[2] * dws[2]
        for dl_ref, d, wt, dw in zip(dl_refs, dils, ws, dws):
            _store_dilated(dl_ref, stage(d), d, LANES, 0, wt * (dw - mean))

    o_specs = [_dilated_spec(tq, w, d) for d in dils]
    l_specs = [_dilated_spec(tq, LANES, d) for d in dils]
    return pl.pallas_call(
        body, name="combine_bwd", grid=(t // tq,),
        in_specs=o_specs + l_specs + [pl.BlockSpec((tq, w), lambda i: (i, 0))], out_specs=o_specs + l_specs,
        out_shape=[jax.ShapeDtypeStruct((t // d, d * w), F32) for d in dils]
        + [jax.ShapeDtypeStruct((t // d, d * LANES), F32) for d in dils],
        scratch_shapes=_stage_buffers(tq, n_stage),
        compiler_params=_cparams(("parallel",)),
    )(*os_, *lses, dy)


C_QKW = C_QK_HEADS * C_DK
C_CONV_W = 2 * C_QKW + C_V_HEADS * C_DK
HALO = 8


def _silu_parts(z):
    sig = jax.nn.sigmoid(z)
    return z * sig, sig * (1.0 + z * (1.0 - sig))


def _conv_window_specs(tq, t):
    c = C_CONV_W
    cb = COL["c_qkv"] // c
    blk = pl.BlockSpec((tq, c), lambda i: (i, cb))
    before = pl.BlockSpec((HALO, c), lambda i: (jnp.maximum(i * (tq // HALO) - 1, 0), cb))
    return c, cb, blk, before


def _conv_prep_fwd_call(u, w, *, tq=512):
    t = u.shape[0]
    c, _, x_spec, halo_spec = _conv_window_specs(tq, t)
    nqk = 2 * C_QK_HEADS

    def body(x_ref, halo_ref, w_ref, z_ref, qk_ref, v_ref):
        i = pl.program_id(0)
        halo = jnp.where(i == 0, 0.0, halo_ref[...])
        xc = jnp.concatenate([halo, x_ref[...]], axis=0)
        wv = w_ref[...]
        z = xc[HALO - 3:HALO - 3 + tq] * wv[0:1]
        for j in range(1, C_CONV):
            z = z + xc[HALO - 3 + j:HALO - 3 + j + tq] * wv[j:j + 1]
        z_ref[...] = z
        act, _ = _silu_parts(z)
        for h in range(nqk):
            a = act[:, h * C_DK:(h + 1) * C_DK]
            qk_ref[:, h * C_DK:(h + 1) * C_DK] = a * lax.rsqrt(jnp.sum(a * a, axis=1, keepdims=True) + EPS)
        v_ref[...] = act[:, nqk * C_DK:]

    return pl.pallas_call(
        body, name="conv_prep_fwd", grid=(t // tq,),
        in_specs=[x_spec, halo_spec, pl.BlockSpec((C_CONV, c), lambda i: (0, 0))],
        out_specs=[pl.BlockSpec((tq, c), lambda i: (i, 0)),
                   pl.BlockSpec((tq, 2 * C_QKW), lambda i: (i, 0)),
                   pl.BlockSpec((tq, c - 2 * C_QKW), lambda i: (i, 0))],
        out_shape=[jax.ShapeDtypeStruct((t, c), F32), jax.ShapeDtypeStruct((t, 2 * C_QKW), F32),
                   jax.ShapeDtypeStruct((t, c - 2 * C_QKW), F32)],
        compiler_params=_cparams(("parallel",)),
    )(u, u, w)


def _conv_prep_dz_call(z, dqk, dv, *, tq=512):
    t, c = z.shape
    nqk = 2 * C_QK_HEADS

    def body(z_ref, dqk_ref, dv_ref, dz_ref):
        zv = z_ref[...]
        act, dact = _silu_parts(zv)
        for h in range(nqk):
            hs = slice(h * C_DK, (h + 1) * C_DK)
            a = act[:, hs]
            r = lax.rsqrt(jnp.sum(a * a, axis=1, keepdims=True) + EPS)
            nrm = a * r
            dn = dqk_ref[:, hs]
            da = r * (dn - nrm * jnp.sum(dn * nrm, axis=1, keepdims=True))
            dz_ref[:, hs] = da * dact[:, hs]
        dz_ref[:, nqk * C_DK:] = dv_ref[...] * dact[:, nqk * C_DK:]

    return pl.pallas_call(
        body, name="conv_prep_dz", grid=(t // tq,),
        in_specs=[pl.BlockSpec((tq, c), lambda i: (i, 0)),
                  pl.BlockSpec((tq, 2 * C_QKW), lambda i: (i, 0)),
                  pl.BlockSpec((tq, c - 2 * C_QKW), lambda i: (i, 0))],
        out_specs=pl.BlockSpec((tq, c), lambda i: (i, 0)),
        out_shape=jax.ShapeDtypeStruct((t, c), F32),
        compiler_params=_cparams(("parallel",)),
    )(z, dqk, dv)


def _conv_bwd_call(u, dz, w, du_buf, *, tq=512):
    t = u.shape[0]
    nt = t // tq
    c, cb, x_spec, halo_spec = _conv_window_specs(tq, t)
    extra, extra_specs, aliases = _du_operands(du_buf, 5)

    def body(x_ref, xh_ref, dz_ref, dzh_ref, w_ref, *refs):
        dx_ref, dw_ref = refs[len(extra):]
        i = pl.program_id(0)
        xc = jnp.concatenate([jnp.where(i == 0, 0.0, xh_ref[...]), x_ref[...]], axis=0)
        dzv = dz_ref[...]
        dzc = jnp.concatenate([dzv, jnp.where(i == nt - 1, 0.0, dzh_ref[...])], axis=0)
        wv = w_ref[...]
        dx = dzv * wv[3:4]
        for s in range(1, C_CONV):
            dx = dx + dzc[s:s + tq] * wv[3 - s:4 - s]
        dx_ref[...] = dx.astype(BF16)
        row = lax.broadcasted_iota(jnp.int32, (8, c), 0)
        dw = jnp.zeros((8, c), F32)
        for j in range(C_CONV):
            prod = dzv * xc[HALO - 3 + j:HALO - 3 + j + tq]
            col = jnp.sum(jnp.sum(prod.reshape(tq // 8, 8, c), axis=0), axis=0, keepdims=True)
            dw = jnp.where(row == j, col, dw)

        @pl.when(i == 0)
        def _():
            dw_ref[...] = dw

        @pl.when(i > 0)
        def _():
            dw_ref[...] += dw

    blk = pl.BlockSpec((tq, c), lambda i: (i, 0))
    after = pl.BlockSpec((HALO, c), lambda i: (jnp.minimum((i + 1) * (tq // HALO), t // HALO - 1), 0))
    return pl.pallas_call(
        body, name="conv_bwd", grid=(nt,),
        in_specs=[x_spec, halo_spec, blk, after, pl.BlockSpec((C_CONV, c), lambda i: (0, 0))] + extra_specs,
        out_specs=[pl.BlockSpec((tq, c), lambda i: (i, cb)), pl.BlockSpec((8, c), lambda i: (0, 0))],
        out_shape=[_du_shape(t), jax.ShapeDtypeStruct((8, c), F32)],
        input_output_aliases=aliases,
        compiler_params=_cparams(("arbitrary",)),
    )(u, u, dz, dz, w, *extra)


C_VW = C_V_HEADS * C_DK


def _softplus(x):
    return jnp.maximum(x, 0.0) + jnp.log(1.0 + jnp.exp(-jnp.abs(x)))


def _tri_masks():
    r = lax.broadcasted_iota(jnp.int32, (CHUNK, CHUNK), 0)
    c = lax.broadcasted_iota(jnp.int32, (CHUNK, CHUNK), 1)
    return r >= c, r > c


def _split_bf16(a):
    hi = a.astype(BF16)
    return hi, (a - hi.astype(F32)).astype(BF16)


def _dot_hi(a, b, dims=None):
    dims = _NN if dims is None else dims
    ah, al = _split_bf16(a)
    bh, bl = _split_bf16(b)

    def d(x, y):
        return lax.dot_general(x, y, dims, preferred_element_type=F32)

    return d(ah, bh) + (d(ah, bl) + d(al, bh))


def _dot_mask(mask, b, dims=None, mask_left=True):
    dims = _NN if dims is None else dims
    mb = mask.astype(BF16)
    b1 = b.astype(BF16)
    rest = b - b1.astype(F32)
    b2 = rest.astype(BF16)
    b3 = (rest - b2.astype(F32)).astype(BF16)
    out = None
    for p in (b1, b2, b3):
        term = (lax.dot_general(mb, p, dims, preferred_element_type=F32) if mask_left
                else lax.dot_general(p, mb, dims, preferred_element_type=F32))
        out = term if out is None else out + term
    return out


def _unit_lower_inverses(mats):
    r = lax.broadcasted_iota(jnp.int32, (CHUNK, CHUNK), 0)
    c = lax.broadcasted_iota(jnp.int32, (CHUNK, CHUNK), 1)
    eye = (r == c).astype(F32)
    xs = [eye - a for a in mats]
    ps = [_dot_hi(a, a) for a in mats]
    steps = int(math.log2(CHUNK)) - 1
    for s in range(steps):
        xs = [x + _dot_hi(x, p) for x, p in zip(xs, ps)]
        if s < steps - 1:
            ps = [_dot_hi(p, p) for p in ps]
    return xs


def _gate_tiles(cab, alog, dtb):
    pre = cab + dtb
    g = -jnp.exp(alog) * _softplus(pre)
    beta = jax.nn.sigmoid(pltpu.roll(cab, LANES - C_V_HEADS, 1))
    return g, beta, pre


def _chunk_common(kk, qk, gc, gct, beta, h, tri, strict):
    gcol, grow, bcol = gc[:, h:h + 1], gct[h:h + 1, :], beta[:, h:h + 1]
    decay = jnp.where(tri, jnp.exp(jnp.where(tri, gcol - grow, 0.0)), 0.0)
    kkd = jnp.where(strict, kk * decay, 0.0)
    attn = jnp.where(tri, qk * decay, 0.0)
    glast = gc[CHUNK - 1:CHUNK, h:h + 1]
    return gcol, bcol, decay, kkd, attn, glast


def _cab_spec():
    return pl.BlockSpec((CHUNK, LANES), lambda n: (n, COL["c_ab"] // LANES))


def _delta_prep_call(qk, v, u, alog, dtb, gather_src=None):
    t = qk.shape[0]
    nc = t // CHUNK
    scale = C_DK ** -0.5
    riding = gather_src is not None
    ng = len(gather_src) if riding else 0

    def body(q_ref, k_ref, v_ref, cab_ref, alog_ref, dtb_ref, *refs):
        if riding:
            gather_refs = (refs[:ng], refs[ng + 8:2 * ng + 8]) + tuple(refs[2 * ng + 8:])
            refs = refs[ng:ng + 8]

            @pl.when(pl.program_id(0) == 0)
            def _():
                _gather_start(*gather_refs)
        u_ref, w_ref, qd_ref, kd_ref, attn_ref, tmat_ref, gc_ref, beta_ref = refs
        tri, strict = _tri_masks()
        g, beta, _ = _gate_tiles(cab_ref[...], alog_ref[...], dtb_ref[...])
        gc = _dot_mask(tri, g)
        gct = gc.T
        gc_ref[...] = gc
        beta_ref[...] = beta
        mats, rhs = [], []
        for j in range(C_QK_HEADS):
            js = slice(j * C_DK, (j + 1) * C_DK)
            kf, qf = k_ref[:, js], q_ref[:, js] * scale
            kb, qb = kf.astype(BF16), qf.astype(BF16)
            kk = lax.dot_general(kb, kb, _NT, preferred_element_type=F32)
            qk = lax.dot_general(qb, kb, _NT, preferred_element_type=F32)
            for h in (2 * j, 2 * j + 1):
                hs = slice(h * C_DK, (h + 1) * C_DK)
                gcol, bcol, decay, kkd, attn, glast = _chunk_common(kk, qk, gc, gct, beta, h, tri, strict)
                gexp = jnp.exp(gcol)
                mats.append(kkd * bcol)
                rhs.append(jnp.concatenate([v_ref[:, hs] * bcol, kf * (bcol * gexp)], axis=1))
                qd_ref[:, hs] = (qf * gexp).astype(BF16)
                kd_ref[:, hs] = (kf * jnp.exp(glast - gcol)).astype(BF16)
                attn_ref[:, h * CHUNK:(h + 1) * CHUNK] = attn.astype(BF16)
        for h, (tmat, r) in enumerate(zip(_unit_lower_inverses(mats), rhs)):
            hs = slice(h * C_DK, (h + 1) * C_DK)
            uw = _dot_hi(tmat, r)
            u_ref[:, hs] = uw[:, :C_DK]
            w_ref[:, hs] = uw[:, C_DK:]
            tmat_ref[:, h * CHUNK:(h + 1) * CHUNK] = tmat.T
        if riding:
            @pl.when(pl.program_id(0) == nc - 1)
            def _():
                _gather_finish(*gather_refs)

    def blk(w):
        return pl.BlockSpec((CHUNK, w), lambda n: (n, 0))

    row = pl.BlockSpec((1, LANES), lambda n: (0, 0))
    big = jax.ShapeDtypeStruct((t, C_VW), F32)
    sq = jax.ShapeDtypeStruct((t, C_V_HEADS * CHUNK), F32)
    tile = jax.ShapeDtypeStruct((t, LANES), F32)
    half = jax.ShapeDtypeStruct((t, C_VW), BF16)
    in_specs = [blk(C_QKW), pl.BlockSpec((CHUNK, C_QKW), lambda n: (n, 1)), blk(C_VW), _cab_spec(), row, row]
    out_specs = [blk(C_VW)] * 4 + [blk(C_V_HEADS * CHUNK)] * 2 + [blk(LANES)] * 2
    out_shape = [big, big, half, half, jax.ShapeDtypeStruct(sq.shape, BF16), sq] + [tile] * 2
    args = [qk, qk, v, u, alog, dtb]
    if riding:
        in_specs += [HBM_SPEC] * ng
        out_specs += [HBM_SPEC] * ng
        out_shape += _gathered_shapes(gather_src)
        args += list(gather_src)
    return pl.pallas_call(
        body, name="delta_prep_gather" if riding else "delta_prep", grid=(nc,),
        in_specs=in_specs, out_specs=out_specs, out_shape=out_shape,
        scratch_shapes=_gather_sems(ng) if riding else [],
        compiler_params=_cparams(("arbitrary",) if riding else ("parallel",)),
    )(*args)


SCAN_SUB = 4


def _delta_scan_call(u, w, qd, kd, attn, gc):
    t = u.shape[0]
    nc = t // CHUNK
    rows = SCAN_SUB * CHUNK

    def body(u_ref, w_ref, qd_ref, kd_ref, attn_ref, gc_ref, o_ref, vn_ref, st_ref, s_ref):
        @pl.when(pl.program_id(0) == 0)
        def _():
            s_ref[...] = jnp.zeros_like(s_ref)

        hss = [slice(h * C_DK, (h + 1) * C_DK) for h in range(C_V_HEADS)]
        states = [s_ref[hs, :] for hs in hss]
        for c in range(SCAN_SUB):
            rs = slice(c * CHUNK, (c + 1) * CHUNK)
            for hs, s in zip(hss, states):
                st_ref[c, hs, :] = s
            sbs = [s.astype(BF16) for s in states]
            vns = [u_ref[rs, hs] - jnp.dot(w_ref[rs, hs].astype(BF16), sb, preferred_element_type=F32)
                   for hs, sb in zip(hss, sbs)]
            qss = [jnp.dot(qd_ref[rs, hs].astype(BF16), sb, preferred_element_type=F32) for hs, sb in zip(hss, sbs)]
            vnbs = [vn.astype(BF16) for vn in vns]
            for h, hs in enumerate(hss):
                vn_ref[rs, hs] = vnbs[h]
                o_ref[rs, hs] = qss[h] + jnp.dot(attn_ref[rs, h * CHUNK:(h + 1) * CHUNK].astype(BF16), vnbs[h],
                                                 preferred_element_type=F32)
            last = (c + 1) * CHUNK - 1
            states = [states[h] * jnp.exp(gc_ref[last:last + 1, h:h + 1])
                      + lax.dot_general(kd_ref[rs, hs].astype(BF16), vnbs[h], _TN, preferred_element_type=F32)
                      for h, hs in enumerate(hss)]
        for hs, s in zip(hss, states):
            s_ref[hs, :] = s

    def blk(wd):
        return pl.BlockSpec((rows, wd), lambda n: (n, 0))

    big = jax.ShapeDtypeStruct((t, C_VW), F32)
    return pl.pallas_call(
        body, name="delta_scan", grid=(nc // SCAN_SUB,),
        in_specs=[blk(C_VW)] * 4 + [blk(C_V_HEADS * CHUNK), blk(LANES)],
        out_specs=[blk(C_VW), blk(C_VW), pl.BlockSpec((SCAN_SUB, C_VW, C_DK), lambda n: (n, 0, 0))],
        out_shape=[big, jax.ShapeDtypeStruct((t, C_VW), BF16), jax.ShapeDtypeStruct((nc, C_VW, C_DK), F32)],
        scratch_shapes=[pltpu.VMEM((C_VW, C_DK), F32)],
        compiler_params=_cparams(("arbitrary",)),
    )(u, w, qd, kd, attn, gc)


def _delta_scan_bwd_call(do, w, qd, kd, attn, gc, vn, st):
    t = do.shape[0]
    nc = t // CHUNK
    rows = SCAN_SUB * CHUNK
    steps = nc // SCAN_SUB

    def body(do_ref, w_ref, qd_ref, kd_ref, attn_ref, gc_ref, vn_ref, st_ref,
             du_ref, dw_ref, dqd_ref, dkd_ref, dattn_ref, dgl_ref, ds_ref):
        @pl.when(pl.program_id(0) == 0)
        def _():
            ds_ref[...] = jnp.zeros_like(ds_ref)

        tri, _ = _tri_masks()
        row = lax.broadcasted_iota(jnp.int32, (8, LANES), 0)
        lane = lax.broadcasted_iota(jnp.int32, (8, LANES), 1)
        hss = [slice(h * C_DK, (h + 1) * C_DK) for h in range(C_V_HEADS)]
        css = [slice(h * CHUNK, (h + 1) * CHUNK) for h in range(C_V_HEADS)]

        def dg(a, b, dims):
            return lax.dot_general(a, b, dims, preferred_element_type=F32)

        dsps = [ds_ref[hs, :] for hs in hss]
        for c in reversed(range(SCAN_SUB)):
            rs = slice(c * CHUNK, (c + 1) * CHUNK)
            dgl = jnp.zeros((8, LANES), F32)
            ss = [st_ref[c, hs, :] for hs in hss]
            sbs = [s.astype(BF16) for s in ss]
            dspbs = [d.astype(BF16) for d in dsps]
            dobs = [do_ref[rs, hs].astype(BF16) for hs in hss]
            vnbs = [vn_ref[rs, hs].astype(BF16) for hs in hss]
            dvns = [dg(attn_ref[rs, cs].astype(BF16), dob, _TN) + dg(kd_ref[rs, hs].astype(BF16), dspb, _NN)
                    for hs, cs, dob, dspb in zip(hss, css, dobs, dspbs)]
            for h, hs in enumerate(hss):
                dqd_ref[rs, hs] = dg(dobs[h], sbs[h], _NT)
                dkd_ref[rs, hs] = dg(vnbs[h], dspbs[h], _NT)
                dattn_ref[rs, css[h]] = jnp.where(tri, dg(dobs[h], vnbs[h], _NT), 0.0)
            dvnbs = [d.astype(BF16) for d in dvns]
            for h, hs in enumerate(hss):
                du_ref[rs, hs] = dvns[h]
                dw_ref[rs, hs] = -dg(dvnbs[h], sbs[h], _NT)
                tot = jnp.sum(jnp.sum(dsps[h] * ss[h], axis=0, keepdims=True), axis=1, keepdims=True)
                dgl = jnp.where(jnp.logical_and(row == 0, lane == h), tot, dgl)
            dgl_ref[c * 8:(c + 1) * 8, :] = dgl
            last = (c + 1) * CHUNK - 1
            dsps = [dg(qd_ref[rs, hs].astype(BF16), dobs[h], _TN) + jnp.exp(gc_ref[last:last + 1, h:h + 1]) * dsps[h]
                    - dg(w_ref[rs, hs].astype(BF16), dvnbs[h], _TN) for h, hs in enumerate(hss)]
        for hs, d in zip(hss, dsps):
            ds_ref[hs, :] = d

    def blk(wd):
        return pl.BlockSpec((rows, wd), lambda n: (steps - 1 - n, 0))

    big = jax.ShapeDtypeStruct((t, C_VW), F32)
    return pl.pallas_call(
        body, name="delta_scan_bwd", grid=(steps,),
        in_specs=[blk(C_VW)] * 4 + [blk(C_V_HEADS * CHUNK), blk(LANES), blk(C_VW),
                                    pl.BlockSpec((SCAN_SUB, C_VW, C_DK), lambda n: (steps - 1 - n, 0, 0))],
        out_specs=[blk(C_VW)] * 4 + [blk(C_V_HEADS * CHUNK),
                                     pl.BlockSpec((SCAN_SUB * 8, LANES), lambda n: (steps - 1 - n, 0))],
        out_shape=[big] * 4 + [jax.ShapeDtypeStruct((t, C_V_HEADS * CHUNK), F32),
                               jax.ShapeDtypeStruct((nc * 8, LANES), F32)],
        scratch_shapes=[pltpu.VMEM((C_VW, C_DK), F32)],
        compiler_params=_cparams(("arbitrary",)),
    )(do, w, qd, kd, attn, gc, vn, st)


PREP_SUB = 2


def _delta_prep_bwd_call(qk, v, proj, alog, dtb, tmat, u, w, gc, beta, du, dw, dqd, dkd, dattn, dgl, du_buf):
    t = qk.shape[0]
    extra, extra_specs, aliases = _du_operands(du_buf, 17)
    nc = t // CHUNK
    rows = PREP_SUB * CHUNK
    scale = C_DK ** -0.5

    def body(q_ref, k_ref, v_ref, cab_ref, alog_ref, dtb_ref, tmat_ref, u_ref, w_ref, gc_ref, beta_ref,
             du_ref, dw_ref, dqd_ref, dkd_ref, dattn_ref, dgl_ref, *outs):
        dcab_ref, dqk_ref, dv_ref, dpar_ref = outs[len(extra):]
        tri, strict = _tri_masks()
        lane = lax.broadcasted_iota(jnp.int32, (CHUNK, LANES), 1)
        rowi = lax.broadcasted_iota(jnp.int32, (CHUNK, 1), 0)
        subs = range(PREP_SUB)
        rss = [slice(c * CHUNK, (c + 1) * CHUNK) for c in subs]
        betas = [beta_ref[rs, :] for rs in rss]

        def dot(x, y, dims=_NN):
            return lax.dot_general(x, y, dims, preferred_element_type=F32)

        heads = []
        for c, rs in zip(subs, rss):
            gc = gc_ref[rs, :]
            gct = gc.T
            for j in range(C_QK_HEADS):
                js = slice(j * C_DK, (j + 1) * C_DK)
                kf, qf = k_ref[rs, js], q_ref[rs, js] * scale
                kb, qb = kf.astype(BF16), qf.astype(BF16)
                kk = dot(kb, kb, _NT)
                qk = dot(qb, kb, _NT)
                for h in (2 * j, 2 * j + 1):
                    heads.append((c, rs, h, kf, qf, kb, qb) + _chunk_common(kk, qk, gc, gct, betas[c], h, tri, strict))

        def cols(h):
            return slice(h * C_DK, (h + 1) * C_DK)

        def sq(h):
            return slice(h * CHUNK, (h + 1) * CHUNK)

        dvks = [_dot_hi(tmat_ref[hd[1], sq(hd[2])],
                        jnp.concatenate([du_ref[hd[1], cols(hd[2])], dw_ref[hd[1], cols(hd[2])]], axis=1))
                for hd in heads]
        das = [-jnp.where(strict, _dot_hi(dvk, jnp.concatenate([u_ref[hd[1], cols(hd[2])], w_ref[hd[1], cols(hd[2])]],
                                                               axis=1), _NT), 0.0)
               for hd, dvk in zip(heads, dvks)]
        pre = []
        for (c, rs, h, kf, qf, kb, qb, gcol, bcol, decay, kkd, attn, glast), da in zip(heads, das):
            dattn_h = dattn_ref[rs, sq(h)]
            pre.append(((da * decay * bcol).astype(BF16), (dattn_h * decay).astype(BF16),
                        da * kkd * bcol + dattn_h * attn))
        mms = [(dot(dkk, hd[5]), dot(dkk, hd[5], _TN), dot(dqk, hd[6], _TN), dot(dqk, hd[5]))
               for hd, (dkk, dqk, e) in zip(heads, pre)]
        dq_parts, dk_parts = {}, {}
        dgc_tiles = [jnp.zeros((CHUNK, LANES), F32) for _ in subs]
        db_tiles = [jnp.zeros((CHUNK, LANES), F32) for _ in subs]
        later_tiles = [jnp.zeros((CHUNK, LANES), F32) for _ in subs]
        upper = jnp.logical_not(strict)
        for (c, rs, h, kf, qf, kb, qb, gcol, bcol, decay, kkd, attn, glast), dvk, da, (_, _, e), mm in zip(
                heads, dvks, das, pre, mms):
            hs = cols(h)
            gexp = jnp.exp(gcol)
            fdec = jnp.exp(glast - gcol)
            dvb, dkb = dvk[:, :C_DK], dvk[:, C_DK:]
            dgc = jnp.sum(e, axis=1, keepdims=True)
            later = jnp.sum(jnp.where(upper, jnp.sum(e, axis=0, keepdims=True), 0.0), axis=1, keepdims=True)
            later_tiles[c] = jnp.where(lane == h, later, later_tiles[c])
            dk_parts[c, h] = mm[0] + mm[1] + mm[2] + dkb * (bcol * gexp) + dkd_ref[rs, hs] * fdec
            dq_parts[c, h] = mm[3] + dqd_ref[rs, hs] * gexp
            dv_ref[rs, hs] = dvb * bcol
            s_kb = jnp.sum(dkb * kf, axis=1, keepdims=True)
            db = (jnp.sum(da * kkd, axis=1, keepdims=True) + jnp.sum(dvb * v_ref[rs, hs], axis=1, keepdims=True)
                  + s_kb * gexp)
            rho = jnp.sum(dkd_ref[rs, hs] * kf, axis=1, keepdims=True) * fdec
            dgc = (dgc + s_kb * bcol * gexp + jnp.sum(dqd_ref[rs, hs] * qf, axis=1, keepdims=True) * gexp - rho)
            last = jnp.sum(rho, axis=0, keepdims=True) + dgl_ref[c * 8:c * 8 + 1, h:h + 1] * jnp.exp(glast)
            dgc = dgc + jnp.where(rowi == CHUNK - 1, last, 0.0)
            dgc_tiles[c] = jnp.where(lane == h, dgc, dgc_tiles[c])
            db_tiles[c] = jnp.where(lane == h, db, db_tiles[c])
        alog = alog_ref[...]
        row8 = lax.broadcasted_iota(jnp.int32, (8, LANES), 0)
        par = jnp.zeros((8, LANES), F32)
        for c, rs in zip(subs, rss):
            for j in range(C_QK_HEADS):
                dqk_ref[rs, j * C_DK:(j + 1) * C_DK] = (dq_parts[c, 2 * j] + dq_parts[c, 2 * j + 1]) * scale
                dqk_ref[rs, C_QKW + j * C_DK:C_QKW + (j + 1) * C_DK] = dk_parts[c, 2 * j] + dk_parts[c, 2 * j + 1]
            dg = _dot_mask(upper, dgc_tiles[c]) - later_tiles[c]
            g, _, gate_pre = _gate_tiles(cab_ref[rs, :], alog, dtb_ref[...])
            dca = dg * (-jnp.exp(alog)) * jax.nn.sigmoid(gate_pre)
            beta = betas[c]
            dcab_ref[rs, :LANES] = (dca + pltpu.roll(db_tiles[c] * beta * (1.0 - beta), C_V_HEADS, 1)).astype(BF16)
            dcab_ref[rs, LANES:] = jnp.zeros((CHUNK, D_IN_PAD - COL["c_ab"] - LANES), BF16)
            par = par + jnp.where(row8 == 0, jnp.sum(dg * g, axis=0, keepdims=True),
                                  jnp.where(row8 == 1, jnp.sum(dca, axis=0, keepdims=True), 0.0))

        @pl.when(pl.program_id(0) == 0)
        def _():
            dpar_ref[...] = par

        @pl.when(pl.program_id(0) > 0)
        def _():
            dpar_ref[...] += par

    def blk(wd):
        return pl.BlockSpec((rows, wd), lambda n: (n, 0))

    row = pl.BlockSpec((1, LANES), lambda n: (0, 0))
    sqs = blk(C_V_HEADS * CHUNK)
    tail = D_IN_PAD - COL["c_ab"]
    assert COL["c_ab"] % tail == 0 and nc % PREP_SUB == 0
    return pl.pallas_call(
        body, name="delta_prep_bwd", grid=(nc // PREP_SUB,),
        in_specs=[blk(C_QKW), pl.BlockSpec((rows, C_QKW), lambda n: (n, 1)), blk(C_VW),
                  pl.BlockSpec((rows, LANES), lambda n: (n, COL["c_ab"] // LANES)), row, row, sqs,
                  blk(C_VW), blk(C_VW),
                  blk(LANES), blk(LANES), blk(C_VW), blk(C_VW), blk(C_VW), blk(C_VW), sqs,
                  pl.BlockSpec((PREP_SUB * 8, LANES), lambda n: (n, 0))] + extra_specs,
        out_specs=[pl.BlockSpec((rows, tail), lambda n: (n, COL["c_ab"] // tail)),
                   blk(2 * C_QKW), blk(C_VW), pl.BlockSpec((8, LANES), lambda n: (0, 0))],
        out_shape=[_du_shape(t), jax.ShapeDtypeStruct((t, 2 * C_QKW), F32),
                   jax.ShapeDtypeStruct((t, C_VW), F32), jax.ShapeDtypeStruct((8, LANES), F32)],
        input_output_aliases=aliases,
        compiler_params=_cparams(("arbitrary",)),
    )(qk, qk, v, proj, alog, dtb, tmat, u, w, gc, beta, du, dw, dqd, dkd, dattn, dgl, *extra)


def _z_spec(tq):
    return pl.BlockSpec((tq, C_VW), lambda i: (i, COL["c_z"] // C_VW))


def _gated_norm_fwd_call(o, u, gain, *, tq=512):
    t, w = o.shape

    def body(o_ref, z_ref, g_ref, y_ref):
        act, _ = _silu_parts(z_ref[...])
        gv = g_ref[...]
        for h in range(C_V_HEADS):
            hs = slice(h * C_DK, (h + 1) * C_DK)
            ov = o_ref[:, hs]
            r = lax.rsqrt(jnp.mean(ov * ov, axis=1, keepdims=True) + EPS)
            y_ref[:, hs] = ov * r * gv * act[:, hs]

    blk = pl.BlockSpec((tq, w), lambda i: (i, 0))
    return pl.pallas_call(
        body, name="gated_norm_fwd", grid=(t // tq,),
        in_specs=[blk, _z_spec(tq), pl.BlockSpec((1, C_DK), lambda i: (0, 0))], out_specs=blk,
        out_shape=jax.ShapeDtypeStruct((t, w), F32),
        compiler_params=_cparams(("parallel",)),
    )(o, u, gain)


def _gated_norm_bwd_call(o, u, gain, dy, du_buf, *, tq=512):
    t, w = o.shape
    nt = t // tq
    extra, extra_specs, aliases = _du_operands(du_buf, 4)

    def body(o_ref, z_ref, g_ref, dy_ref, *refs):
        dz_ref, do_ref, dg_ref, acc_ref = refs[len(extra):]
        i = pl.program_id(0)
        act, dact = _silu_parts(z_ref[...])
        gv = g_ref[...]
        part = jnp.zeros((8, C_DK), F32)
        for h in range(C_V_HEADS):
            hs = slice(h * C_DK, (h + 1) * C_DK)
            ov = o_ref[:, hs]
            r = lax.rsqrt(jnp.mean(ov * ov, axis=1, keepdims=True) + EPS)
            xh = ov * r
            dyv = dy_ref[:, hs]
            dn = dyv * act[:, hs]
            dz_ref[:, hs] = (dyv * xh * gv * dact[:, hs]).astype(BF16)
            dxh = dn * gv
            do_ref[:, hs] = r * (dxh - xh * jnp.mean(dxh * xh, axis=1, keepdims=True))
            part = part + jnp.sum((dn * xh).reshape(tq // 8, 8, C_DK), axis=0)

        @pl.when(i == 0)
        def _():
            acc_ref[...] = part

        @pl.when(i > 0)
        def _():
            acc_ref[...] += part

        @pl.when(i == nt - 1)
        def _():
            dg_ref[...] = jnp.sum(acc_ref[...], axis=0, keepdims=True)

    blk = pl.BlockSpec((tq, w), lambda i: (i, 0))
    grow = pl.BlockSpec((1, C_DK), lambda i: (0, 0))
    return pl.pallas_call(
        body, name="gated_norm_bwd", grid=(nt,),
        in_specs=[blk, _z_spec(tq), grow, blk] + extra_specs, out_specs=[_z_spec(tq), blk, grow],
        out_shape=[_du_shape(t), jax.ShapeDtypeStruct((t, w), F32), jax.ShapeDtypeStruct((1, C_DK), F32)],
        scratch_shapes=[pltpu.VMEM((8, C_DK), F32)],
        input_output_aliases=aliases,
        compiler_params=_cparams(("arbitrary",)),
    )(o, u, gain, dy, *extra)


def _gate_specs(tq):
    return [pl.BlockSpec((tq, D_MODEL), lambda i, j=j: (i, j)) for j in range(3)]


def _merge_fwd_call(ps, u, *, tq=512):
    t, w = ps[0].shape

    def body(p0, p1, p2, g0, g1, g2, y_ref):
        y_ref[...] = (jax.nn.sigmoid(g0[...]) * p0[...] + jax.nn.sigmoid(g1[...]) * p1[...]
                      + jax.nn.sigmoid(g2[...]) * p2[...]).astype(BF16)

    blk = pl.BlockSpec((tq, w), lambda i: (i, 0))
    return pl.pallas_call(
        body, name="merge_fwd", grid=(t // tq,), in_specs=[blk] * 3 + _gate_specs(tq), out_specs=blk,
        out_shape=jax.ShapeDtypeStruct((t, w), BF16),
        compiler_params=_cparams(("parallel",)),
    )(*ps, u, u, u)


def _merge_bwd_call(ps, u, dy, *, tq=256):
    t, w = dy.shape

    def body(p0, p1, p2, g0, g1, g2, dy_ref, dg_ref, dp0, dp1, dp2):
        dyv = dy_ref[...]
        for j, (p, g, dp) in enumerate(((p0, g0, dp0), (p1, g1, dp1), (p2, g2, dp2))):
            sig = jax.nn.sigmoid(g[...])
            dp[...] = (dyv * sig).astype(BF16)
            dg_ref[:, j * w:(j + 1) * w] = (dyv * p[...] * sig * (1.0 - sig)).astype(BF16)

    blk = pl.BlockSpec((tq, w), lambda i: (i, 0))
    small = jax.ShapeDtypeStruct((t, w), BF16)
    return pl.pallas_call(
        body, name="merge_bwd", grid=(t // tq,), in_specs=[blk] * 3 + _gate_specs(tq) + [blk],
        out_specs=[pl.BlockSpec((tq, 3 * w), lambda i: (i, 0))] + [blk] * 3,
        out_shape=[_du_shape(t)] + [small] * 3,
        compiler_params=_cparams(("parallel",)),
    )(*ps, u, u, u, dy)


Q_SCALE = HEAD_DIM ** -0.5
A_PARTS = ((COL["a_q"], A_W, True, Q_SCALE), (COL["a_k"], A_W, True, 1.0), (COL["a_v"], A_W, False, 1.0))
B_PARTS = ((COL["b_q"], A_W, True, Q_SCALE), (COL["b_k"], B_KVW, True, 1.0), (COL["b_v"], B_KVW, False, 1.0))
BRANCHES = ("w_branch_a", "w_branch_b", "w_branch_c")


def _layer_fwd(x, tabs, p, w_in_b, rest, rest_of, gather_src=None):
    h = _rms_fwd_call(x, p["norm_mix"], name="rms_mix_fwd", out_dtype=BF16)
    if rest[0] == "ride":
        u, (packed,) = _mm(h, w_in_b, bias=p["b_in"], tn=IN_TN, gather_src=[rest[1]], name="in_proj_fwd_gather")
    else:
        u, packed = _mm(h, w_in_b, bias=p["b_in"], tn=IN_TN, name="in_proj_fwd"), rest[1]
    wb, conv_w = rest_of(packed)
    wb = dict(wb, w_in=w_in_b)
    p = dict(p, conv_w=conv_w)
    qkv_a = _rope_gather_call(u, tabs, A_PARTS, dils=A_DILS, name="rope_a_fwd")
    os_, lses = zip(*[_attn_fwd(cfg, qkv2, None) for cfg, qkv2 in zip(ATTN_A_CFGS, qkv_a)])
    ya = _combine_fwd_call(os_, lses, A_DILS)
    qkv_b, = _rope_gather_call(u, tabs, B_PARTS, name="rope_b_fwd")
    yb, lse_b = _attn_fwd(ATTN_B_CFG, qkv_b, p["sinks"])
    zc, qk, v = _conv_prep_fwd_call(u, p["conv_w"])
    uu, ww, qd, kd, attn, tmat, gc, beta, *gathered = _delta_prep_call(qk, v, u, p["a_log"], p["dt_bias"], gather_src)
    o, vn, st = _delta_scan_call(uu, ww, qd, kd, attn, gc)
    yc = _gated_norm_fwd_call(o, u, p["c_norm"])
    ys = (ya, yb, yc)
    ps = tuple(_mm(y, wb[n], name="branch_fwd") for y, n in zip(ys, BRANCHES))
    merged = _merge_fwd_call(ps, u)
    x1 = _mm(merged, wb["w_out"], add=x, name="out_proj_fwd")
    h2 = _rms_fwd_call(x1, p["norm_ffn"], name="rms_ffn_fwd", out_dtype=BF16)
    pre, act = _mm(h2, wb["w_ff1"], relu2_out=True, name="ffn_up")
    x2 = _mm(act, wb["w_ff2"], add=x1, name="ffn_down")
    saved = dict(x=x, h=h, u=u, qkv_a=qkv_a, os_=os_, lses=lses, b_saved=(qkv_b, yb, lse_b),
                 zc=zc, qk=qk, v=v, delta=(tmat, uu, ww, gc, beta, qd, kd, attn, vn, st), o=o, ys=ys, ps=ps,
                 merged=merged, x1=x1, h2=h2, pre=pre, act=act, p=p, wb=wb)
    return x2, saved, (gathered if gathered else None)


def _layer_bwd(s, dx2, tabs):
    g, p, wb = {}, s["p"], s["wb"]
    t = dx2.shape[0]
    dpre = _mm(dx2, wb["w_ff2"], tb=True, mul_drelu2=s["pre"], out_dtype=BF16, name="ffn_dpre")
    g["w_ff2"] = _mm(s["act"], dx2, ta=True, tk=1024, name="ffn_dw2")
    g["w_ff1"] = _mm(s["h2"], dpre, ta=True, tk=1024, name="ffn_dw1")
    dh2 = _mm(dpre, wb["w_ff1"], tb=True, name="ffn_dh")
    dx1, g["norm_ffn"] = _rms_bwd_call(s["x1"], p["norm_ffn"], dh2, add=dx2, name="rms_ffn_bwd")
    dmerged = _mm(dx1, wb["w_out"], tb=True, name="out_proj_da")
    g["w_out"] = _mm(s["merged"], dx1, ta=True, tk=1024, name="out_proj_dw")
    du, *dps = _merge_bwd_call(s["ps"], s["u"], dmerged)
    dys = []
    for y, dp, n in zip(s["ys"], dps, BRANCHES):
        dys.append(_mm(dp, wb[n], tb=True, name="branch_da"))
        g[n] = _mm(y, dp, ta=True, tk=1024, name="branch_dw")
    dya, dyb, dyc = dys
    tmat, uu, ww, gc, beta, qd, kd, attn, vn, st = s["delta"]
    du, do, g["c_norm"] = _gated_norm_bwd_call(s["o"], s["u"], p["c_norm"], dyc, du)
    ddu, ddw, dqd, dkd, dattn, dgl = _delta_scan_bwd_call(do, ww, qd, kd, attn, gc, vn, st)
    du, dqk, dv, dpar = _delta_prep_bwd_call(s["qk"], s["v"], s["u"], p["a_log"], p["dt_bias"], tmat, uu, ww, gc,
                                             beta, ddu, ddw, dqd, dkd, dattn, dgl, du)
    g["a_log"], g["dt_bias"] = dpar[0:1], dpar[1:2]
    dzc = _conv_prep_dz_call(s["zc"], dqk, dv)
    du, dconv = _conv_bwd_call(s["u"], dzc, p["conv_w"], du)
    g["conv_w"] = dconv[:C_CONV]
    no_dlse = jnp.zeros((t, LANES), F32)
    dq, dk, dv_b, dsink = _attn_bwd(ATTN_B_CFG, *s["b_saved"], p["sinks"], dyb, no_dlse)
    g["sinks"] = dsink[0, :p["sinks"].shape[0]]
    du = _rope_scatter_call(du, t, [([dq], A_W, True, Q_SCALE)], COL["b_q"], tabs, name="rope_bq_bwd")
    du = _rope_scatter_call(du, t, [([dk], B_KVW, True, 1.0), ([dv_b], B_KVW, False, 1.0)], COL["b_k"], tabs,
                            name="rope_bkv_bwd")
    *dos, dl0, dl1, dl2 = _combine_bwd_call(s["os_"], s["lses"], dya, A_DILS)
    grads_a = [_attn_bwd(cfg, qkv2, o2, lse2, None, do2, dl2_)[:3]
               for cfg, qkv2, o2, lse2, do2, dl2_ in zip(ATTN_A_CFGS, s["qkv_a"], s["os_"], s["lses"], dos,
                                                         (dl0, dl1, dl2))]
    dqs, dks, dvs = zip(*grads_a)
    du = _rope_scatter_call(du, t, [(list(dqs), A_W, True, Q_SCALE), (list(dks), A_W, True, 1.0),
                                    (list(dvs), A_W, False, 1.0)],
                            COL["a_q"], tabs, dils=A_DILS, name="rope_a_bwd")
    dh = _mm(du, wb["w_in"], tb=True, tk=IN_TN, name="in_proj_da")
    g["w_in"], g["b_in"] = _mm(s["h"], du, ta=True, b_colsum=True, tn=IN_TN, tk=1024, name="in_proj_dw")
    dx, g["norm_mix"] = _rms_bwd_call(s["x"], p["norm_mix"], dh, add=dx1, name="rms_mix_bwd")
    return dx, g


def _local_step(x, params, w_in_first, rest_first, payload_of_layer, w_in_of, rest_of, tabs, tgt):
    saves = []
    w_in_blocks, rest = w_in_first, rest_first
    for layer in range(DEPTH):
        p = {n: w[layer] for n, w in params.items() if n != "norm_final"}
        nxt = payload_of_layer(layer + 1) if layer + 1 < DEPTH else None
        x, s, gathered = _layer_fwd(x, tabs, p, w_in_of(w_in_blocks), rest, rest_of, nxt)
        saves.append(s)
        if gathered is not None:
            w_in_blocks, rest = gathered[0], ("ready", gathered[1])
    loss, dx, dfinal = _loss_call(x, params["norm_final"], tgt)
    per_layer = []
    for s in reversed(saves):
        dx, g = _layer_bwd(s, dx, tabs)
        per_layer.append(g)
    per_layer.reverse()
    grads = {n: jnp.stack([g[n] for g in per_layer]) for n in per_layer[0]}
    grads["norm_final"] = dfinal
    return loss, dx, grads


def _in_cols_to_kernel(w):
    lead = w.shape[:-1]
    parts, pos = [], 0
    for _, start, width, ref_start in IN_LAYOUT:
        if start > pos:
            parts.append(jnp.zeros(lead + (start - pos,), w.dtype))
        parts.append(w[..., ref_start:ref_start + width])
        pos = start + width
    parts.append(jnp.zeros(lead + (D_IN_PAD - pos,), w.dtype))
    return jnp.concatenate(parts, axis=-1)


def _in_cols_to_reference(w):
    by_ref = sorted(IN_LAYOUT, key=lambda e: e[3])
    return jnp.concatenate([w[..., start:start + width] for _, start, width, _ in by_ref], axis=-1)


W_IN_SHARD = 8464 // N_DEV


def _w_in_from_shards(blocks):
    lead = blocks.shape[1:-1]
    parts, pos = [], 0
    for _, start, width, ref_start in IN_LAYOUT:
        if start > pos:
            parts.append(jnp.zeros(lead + (start - pos,), blocks.dtype))
        col = ref_start
        while col < ref_start + width:
            d, l = divmod(col, W_IN_SHARD)
            n = min(W_IN_SHARD - l, ref_start + width - col)
            parts.append(blocks[d, ..., l:l + n])
            col += n
        pos = start + width
    parts.append(jnp.zeros(lead + (D_IN_PAD - pos,), blocks.dtype))
    return jnp.concatenate(parts, axis=-1)


def _w_in_to_shards(g):
    by_ref = sorted(IN_LAYOUT, key=lambda e: e[3])
    blocks = []
    for d in range(N_DEV):
        lo, hi = d * W_IN_SHARD, (d + 1) * W_IN_SHARD
        parts = []
        for _, start, width, ref_start in by_ref:
            a, b = max(lo, ref_start), min(hi, ref_start + width)
            if a < b:
                parts.append(g[..., start + a - ref_start:start + b - ref_start])
        blocks.append(jnp.concatenate(parts, axis=-1))
    return jnp.stack(blocks)


def _pad_lanes(v):
    return jnp.pad(v, ((0, 0), (0, LANES - v.shape[1])))[:, None, :]


BIG = (("w_in", 2), ("conv_w", 2), ("w_branch_a", 2), ("w_branch_b", 2), ("w_branch_c", 1), ("w_out", 1),
       ("w_ff1", 2), ("w_ff2", 1))
SMALL = ("norm_mix", "b_in", "a_log", "dt_bias", "sinks", "c_norm", "norm_ffn", "norm_final")
WEIGHTS = ("norm_mix", "w_in", "b_in", "conv_w", "a_log", "dt_bias", "sinks", "c_norm", "w_branch_a",
           "w_branch_b", "w_branch_c", "w_out", "norm_ffn", "w_ff1", "w_ff2", "norm_final")
MATMUL_WEIGHTS = ("w_in", "w_branch_a", "w_branch_b", "w_branch_c", "w_out", "w_ff1", "w_ff2")
PACK_ROWS = 1024
ROW_ALIGN = 16


def _seg_rows(n):
    return -(-n // (LANES * ROW_ALIGN)) * ROW_ALIGN


def _pack(arrays, lead=0):
    parts = []
    for a in arrays:
        lead_shape = a.shape[:lead]
        n = math.prod(a.shape[lead:])
        rows = _seg_rows(n)
        if rows * LANES != n:
            a = jnp.pad(a.reshape(lead_shape + (n,)), [(0, 0)] * lead + [(0, rows * LANES - n)])
        parts.append(a.reshape(lead_shape + (rows, LANES)))
    total = sum(p.shape[lead] for p in parts)
    padded = -(-total // PACK_ROWS) * PACK_ROWS
    if padded > total:
        parts.append(jnp.zeros(parts[0].shape[:lead] + (padded - total, LANES), parts[0].dtype))
    return jnp.concatenate(parts, axis=lead)


def _unpack(buf, shapes):
    lead = buf.shape[:-2]
    out, pos = [], 0
    for shp in shapes:
        n = math.prod(shp)
        rows = _seg_rows(n)
        seg = buf[..., pos:pos + rows, :]
        if rows * LANES != n:
            seg = seg.reshape(lead + (rows * LANES,))[..., :n]
        out.append(seg.reshape(lead + tuple(shp)))
        pos += rows
    return out


def _shards_to_full(blocks, axis):
    moved = jnp.moveaxis(blocks, 0, axis)
    shp = list(blocks.shape[1:])
    shp[axis] = shp[axis] * N_DEV
    return moved.reshape(shp)


def _full_to_shards(full, axis):
    shp = list(full.shape)
    shp[axis:axis + 1] = [N_DEV, shp[axis] // N_DEV]
    return jnp.moveaxis(full.reshape(shp), axis, 0)


def _my_place():
    return lax.axis_index("x"), lax.axis_index("y"), lax.axis_index("c")


def _slot(x, y, c):
    return 4 * x + 2 * y + c


GATHER_COPIES = 7


def _gather_plan(x_ref, out_ref, send_sems, recv_sems, local_sem, base):
    x, y, c = _my_place()
    me, sibling = (x, y, c), (x, y, 1 - c)
    chips = [(1 - x, y), (x, 1 - y), (1 - x, 1 - y)]

    def copy(k, blk, to, src=None):
        dst = out_ref.at[_slot(*blk)]
        return pltpu.make_async_remote_copy(
            src_ref=dst if src is None else src, dst_ref=dst,
            send_sem=send_sems.at[base + k], recv_sem=recv_sems.at[base + k], device_id=to, device_id_type=MESH_ID)

    def own():
        mine = pltpu.make_async_copy(x_ref, out_ref.at[_slot(*me)], local_sem)
        return mine, [copy(0, me, sibling, src=x_ref)] + [copy(1 + j, me, (*chip, c), src=x_ref)
                                                          for j, chip in enumerate(chips)]

    return copy, own, me, sibling, chips, c


def _gather_plans(srcs, outs, send_sems, recv_sems, local_sems):
    return [_gather_plan(x_ref, out_ref, send_sems, recv_sems, local_sems.at[i], GATHER_COPIES * i)
            for i, (x_ref, out_ref) in enumerate(zip(srcs, outs))]


def _gather_start(srcs, outs, *sems):
    for _, own, *_ in _gather_plans(srcs, outs, *sems):
        mine, first = own()
        mine.start()
        for cp in first:
            cp.start()


def _gather_finish(srcs, outs, *sems):
    plans = _gather_plans(srcs, outs, *sems)
    passed_all = []
    for copy, own, me, sibling, chips, c in plans:
        passed = [copy(4 + j, (*chip, c), sibling) for j, chip in enumerate(chips)]
        for j, chip in enumerate(chips):
            copy(1 + j, (*chip, c), me).wait_recv()
            passed[j].start()
        passed_all.append(passed)
    for (copy, own, me, sibling, chips, c), passed in zip(plans, passed_all):
        copy(0, sibling, me).wait_recv()
        for j, chip in enumerate(chips):
            copy(4 + j, (*chip, 1 - c), me).wait_recv()
        mine, first = own()
        for cp in first + passed:
            cp.wait_send()
        mine.wait()


def _gather_sems(n):
    return [pltpu.SemaphoreType.DMA((GATHER_COPIES * n,)), pltpu.SemaphoreType.DMA((GATHER_COPIES * n,)),
            pltpu.SemaphoreType.DMA((n,))]


def _gathered_shapes(blocks):
    return [jax.ShapeDtypeStruct((N_DEV,) + b.shape, b.dtype) for b in blocks]


def _all_gather(blocks, *, name):
    n = len(blocks)

    def body(*refs):
        srcs, outs, sems = refs[:n], refs[n:2 * n], refs[2 * n:]
        _gather_start(srcs, outs, *sems)
        _gather_finish(srcs, outs, *sems)

    return pl.pallas_call(
        body, name=name, out_shape=_gathered_shapes(blocks),
        in_specs=[HBM_SPEC] * n, out_specs=[HBM_SPEC] * n,
        scratch_shapes=_gather_sems(n),
    )(*blocks)


N_CHIP = N_DEV // 2


def _swap_with_sibling(blocks, *, name):
    n = len(blocks)

    def body(*refs):
        srcs, outs, send_sems, recv_sems = refs[:n], refs[n:2 * n], refs[2 * n], refs[2 * n + 1]
        x, y, c = _my_place()
        copies = [pltpu.make_async_remote_copy(src_ref=g_ref, dst_ref=out_ref, send_sem=send_sems.at[i],
                                               recv_sem=recv_sems.at[i], device_id=(x, y, 1 - c),
                                               device_id_type=MESH_ID)
                  for i, (g_ref, out_ref) in enumerate(zip(srcs, outs))]
        for cp in copies:
            cp.start()
        for cp in copies:
            cp.wait_recv()
        for cp in copies:
            cp.wait_send()

    return pl.pallas_call(
        body, name=name,
        out_shape=[jax.ShapeDtypeStruct(b.shape, b.dtype) for b in blocks],
        in_specs=[HBM_SPEC] * n, out_specs=[HBM_SPEC] * n,
        scratch_shapes=[pltpu.SemaphoreType.DMA((n,)), pltpu.SemaphoreType.DMA((n,))],
    )(*blocks)


def _chip_all_to_all(blocks, *, name):
    n = len(blocks)
    peers = N_CHIP - 1

    def body(*refs):
        srcs, outs = refs[:n], refs[n:2 * n]
        send_sems, recv_sems, local_sems = refs[2 * n:]
        x, y, c = _my_place()
        mine_slot = 2 * x + y
        locals_, copies = [], []
        for i, (g_ref, out_ref) in enumerate(zip(srcs, outs)):
            locals_.append(pltpu.make_async_copy(g_ref.at[mine_slot], out_ref.at[mine_slot], local_sems.at[i]))
            for k in range(1, N_CHIP):
                px, py = x ^ (k >> 1), y ^ (k & 1)
                copies.append(pltpu.make_async_remote_copy(
                    src_ref=g_ref.at[2 * px + py], dst_ref=out_ref.at[mine_slot],
                    send_sem=send_sems.at[peers * i + k - 1], recv_sem=recv_sems.at[peers * i + k - 1],
                    device_id=(px, py, c), device_id_type=MESH_ID))
        for cp in locals_ + copies:
            cp.start()
        for cp in copies:
            cp.wait_recv()
        for cp in copies:
            cp.wait_send()
        for cp in locals_:
            cp.wait()

    return pl.pallas_call(
        body, name=name,
        out_shape=[jax.ShapeDtypeStruct(b.shape, b.dtype) for b in blocks],
        in_specs=[HBM_SPEC] * n, out_specs=[HBM_SPEC] * n,
        scratch_shapes=[pltpu.SemaphoreType.DMA((peers * n,)), pltpu.SemaphoreType.DMA((peers * n,)),
                        pltpu.SemaphoreType.DMA((n,))],
    )(*blocks)


def _block_rows(rows, cols):
    tr = max(8, min(rows, PACK_ROWS * LANES // (-(-cols // LANES) * LANES) // 8 * 8))
    while rows % tr:
        tr -= 8
    return tr


def _add_bf16_call(a, b, *, name):
    n, rows, cols = a.shape
    tr = _block_rows(rows, cols)

    def body(a_ref, b_ref, o_ref):
        o_ref[...] = (a_ref[...].astype(F32) + b_ref[...].astype(F32)).astype(BF16)

    blk = pl.BlockSpec((n, tr, cols), lambda i: (0, i, 0))
    return pl.pallas_call(
        body, name=name, grid=(rows // tr,), in_specs=[blk, blk], out_specs=blk,
        out_shape=jax.ShapeDtypeStruct(a.shape, BF16),
        compiler_params=_cparams(("parallel",)),
    )(a, b)


def _adamw_call(parts, w, m, v, *, name):
    rows, cols = w.shape
    tr = _block_rows(rows, cols)
    n_parts = parts.shape[0]

    def body(p_ref, w_ref, m_ref, v_ref, g_ref, d_ref, nm_ref, nv_ref):
        g = p_ref[0].astype(F32)
        for s in range(1, n_parts):
            g = g + p_ref[s].astype(F32)
        nm = ADAM_B1 * m_ref[...] + (1.0 - ADAM_B1) * g
        nv = ADAM_B2 * v_ref[...] + (1.0 - ADAM_B2) * jnp.square(g)
        m_hat = nm / (1.0 - ADAM_B1 ** ADAM_STEP)
        v_hat = nv / (1.0 - ADAM_B2 ** ADAM_STEP)
        g_ref[...] = g
        nm_ref[...] = nm
        nv_ref[...] = nv
        d_ref[...] = -ADAM_LR * (m_hat / (jnp.sqrt(v_hat) + ADAM_EPS) + ADAM_WD * w_ref[...])

    blk = pl.BlockSpec((tr, cols), lambda i: (i, 0))
    shape = jax.ShapeDtypeStruct((rows, cols), F32)
    return pl.pallas_call(
        body, name=name, grid=(rows // tr,),
        in_specs=[pl.BlockSpec((n_parts, tr, cols), lambda i: (0, i, 0)), blk, blk, blk],
        out_specs=[blk] * 4, out_shape=[shape] * 4,
        compiler_params=_cparams(("parallel",)),
    )(parts, w, m, v)


def _kernel_params(full):
    return {
        "norm_mix": full["norm_mix"][:, None, :],
        "b_in": _in_cols_to_kernel(full["b_in"])[:, None, :],
        "a_log": _pad_lanes(full["a_log"]),
        "dt_bias": _pad_lanes(full["dt_bias"]),
        "sinks": full["sinks"],
        "c_norm": full["c_norm"][:, None, :],
        "norm_ffn": full["norm_ffn"][:, None, :],
        "norm_final": full["norm_final"][None, :],
    }


def _reference_grads(g):
    return {
        "norm_mix": g["norm_mix"][:, 0, :],
        "b_in": _in_cols_to_reference(g["b_in"][:, 0, :]),
        "conv_w": g["conv_w"],
        "a_log": g["a_log"][:, 0, :C_V_HEADS],
        "dt_bias": g["dt_bias"][:, 0, :C_V_HEADS],
        "sinks": g["sinks"],
        "c_norm": g["c_norm"][:, 0, :],
        "w_branch_a": g["w_branch_a"], "w_branch_b": g["w_branch_b"], "w_branch_c": g["w_branch_c"],
        "w_out": g["w_out"],
        "norm_ffn": g["norm_ffn"][:, 0, :],
        "w_ff1": g["w_ff1"], "w_ff2": g["w_ff2"],
        "norm_final": g["norm_final"][0],
    }


def kernel(x, positions, norm_mix, w_in, b_in, conv_w, a_log, dt_bias, sinks, c_norm, w_branch_a, w_branch_b, w_branch_c, w_out, norm_ffn, w_ff1, w_ff2, norm_final, loss_target, m_norm_mix, m_w_in, m_b_in, m_conv_w, m_a_log, m_dt_bias, m_sinks, m_c_norm, m_w_branch_a, m_w_branch_b, m_w_branch_c, m_w_out, m_norm_ffn, m_w_ff1, m_w_ff2, m_norm_final, v_norm_mix, v_w_in, v_b_in, v_conv_w, v_a_log, v_dt_bias, v_sinks, v_c_norm, v_w_branch_a, v_w_branch_b, v_w_branch_c, v_w_out, v_norm_ffn, v_w_ff1, v_w_ff2, v_norm_final):
    env = dict(locals())
    weights = {n: env[n] for n in WEIGHTS}
    moments_m = {n: env["m_" + n] for n in WEIGHTS}
    moments_v = {n: env["v_" + n] for n in WEIGHTS}

    axis_of = {n: axis - 1 for n, axis in BIG}

    packed_names = [n for n in MATMUL_WEIGHTS if n != "w_in"]

    def payload_of_layer(layer):
        cw = weights["conv_w"][layer]
        c1 = cw.astype(BF16)
        c2 = (cw - c1.astype(F32)).astype(BF16)
        c3 = (cw - c1.astype(F32) - c2.astype(F32)).astype(BF16)
        return [weights["w_in"][layer].astype(BF16),
                _pack([weights[n][layer].astype(BF16) for n in packed_names] + [c1, c2, c3])]

    def rest_of(packed):
        shapes = [weights[n].shape[1:] for n in packed_names] + [weights["conv_w"].shape[1:]] * 3
        blocks = _unpack(packed, shapes)
        wb = {n: _shards_to_full(blk, axis_of[n]) for n, blk in zip(packed_names, blocks)}
        return wb, _shards_to_full(sum(b.astype(F32) for b in blocks[-3:]), axis_of["conv_w"])

    tabs = rope_tables(positions[0])
    w_in_first, = _all_gather(payload_of_layer(0)[:1], name="gather_weights")
    loss, dx, dparams = _local_step(x[0], _kernel_params({n: weights[n] for n in SMALL}), w_in_first,
                                    ("ride", payload_of_layer(0)[1]), payload_of_layer, _w_in_from_shards,
                                    rest_of, tabs, loss_target[0])
    grads = _reference_grads(dparams)
    loss = lax.psum(loss, ("x", "y", "c"))

    core = lax.axis_index("c")
    rest = [(n, axis) for n, axis in BIG if n != "w_in"]
    w_in_rows = DEPTH * D_MODEL

    def by_core(shards, which):
        sh = shards.reshape((N_CHIP, 2) + shards.shape[1:])
        return lax.dynamic_index_in_dim(sh, which, axis=1, keepdims=False).astype(BF16)

    def halves(which):
        w_in_half = by_core(_w_in_to_shards(dparams["w_in"]), which).reshape(N_CHIP, w_in_rows, W_IN_SHARD)
        return [w_in_half, _pack([by_core(_full_to_shards(grads[n], axis), which) for n, axis in rest], lead=1)]

    from_sibling = _swap_with_sibling(halves(1 - core), name="scatter_grads_d2d")
    chip_sums = [_add_bf16_call(keep, got, name="scatter_grads_add")
                 for keep, got in zip(halves(core), from_sibling)]
    w_in_parts, rest_parts = _chip_all_to_all(chip_sums, name="scatter_grads_ici")
    small_parts, = _all_gather([_pack([grads[n] for n in SMALL])], name="gather_small_grads")

    out = {}
    results = _adamw_call(w_in_parts, *[d["w_in"].reshape(w_in_rows, W_IN_SHARD)
                                        for d in (weights, moments_m, moments_v)], name="adamw_w_in")
    for kind, buf in zip(("grad", "delta", "new_m", "new_v"), results):
        out[kind, "w_in"] = buf.reshape(weights["w_in"].shape)
    for names, parts in (([n for n, _ in rest], rest_parts), (list(SMALL), small_parts)):
        shapes = [weights[n].shape for n in names]
        packed = [_pack([d[n] for n in names]) for d in (weights, moments_m, moments_v)]
        results = _adamw_call(parts, *packed, name="adamw_" + names[0])
        for kind, buf in zip(("grad", "delta", "new_m", "new_v"), results):
            for n, arr in zip(names, _unpack(buf, shapes)):
                out[kind, n] = arr
    return (loss, dx[None], *[out[kind, n] for kind in ("grad", "delta", "new_m", "new_v") for n in WEIGHTS])
```

```python
import functools
import math

import jax
import jax.numpy as jnp
from jax import lax
from jax.experimental import pallas as pl
from jax.experimental.pallas import tpu as pltpu

F32 = jnp.float32
BF16 = jnp.bfloat16

N_DEV = 8
D_MODEL = 1024
DEPTH = 2
HEAD_DIM = 64
ROT_DIM = 16
ROPE_THETA = 500000.0
BLK = 128
NEG_INF = -1e30
EPS = 1e-6
A_CONFIGS = ((128, 1), (512, 4), (2048, 16))
B_GROUP = 4
C_QK_HEADS = 4
C_V_HEADS = 8
C_DK = 128
C_CONV = 4
CHUNK = 64
ADAM_LR = 0.001
ADAM_B1 = 0.9
ADAM_B2 = 0.999
ADAM_EPS = 1e-08
ADAM_WD = 0.01
ADAM_STEP = 10

IN_LAYOUT = (
    ("gate_a", 0, 1024, 5392), ("gate_b", 1024, 1024, 6416), ("gate_c", 2048, 1024, 7440),
    ("a_q", 3072, 512, 0), ("a_k", 3584, 512, 512), ("a_v", 4096, 512, 1024), ("b_q", 4608, 512, 1536),
    ("c_z", 5120, 1024, 4352), ("c_qkv", 6144, 2048, 2304),
    ("b_k", 8192, 128, 2048), ("b_v", 8320, 128, 2176), ("c_ab", 8448, 16, 5376),
)
COL = {name: start for name, start, _, _ in IN_LAYOUT}
D_IN_PAD = 8704
IN_TN = D_IN_PAD // 4
LANES = 128
VMEM_LIMIT = 56 * 1024 * 1024


def _cparams(sem=None):
    return pltpu.CompilerParams(dimension_semantics=sem, vmem_limit_bytes=VMEM_LIMIT)


def _relu2(t):
    return jnp.square(jnp.maximum(t, 0.0))


def _mm(a, b, *, ta=False, tb=False, bias=None, a_fn=None, mul_drelu2=None, add=None,
        out_dtype=F32, relu2_out=False, b_colsum=False, gather_src=None, tm=1024, tn=1024, tk=2048, name):
    if ta:
        kdim, m = a.shape
    else:
        m, kdim = a.shape
    n = b.shape[0] if tb else b.shape[1]
    tm, tn, tk = min(tm, m), min(tn, n), min(tk, kdim)
    assert m % tm == 0 and n % tn == 0 and kdim % tk == 0, (a.shape, b.shape, tm, tn, tk)
    nk = kdim // tk
    assert not b_colsum or (m == tm and not tb and nk > 1)
    dims = (((0 if ta else 1,), (1 if tb else 0,)), ((), ()))
    extras = [e for e in (bias, mul_drelu2, add) if e is not None]
    ng = len(gather_src) if gather_src is not None else 0
    grid = (m // tm, n // tn, nk)

    def body(*refs):
        if ng:
            n_in = 2 + len(extras)
            n_out = 1 + int(relu2_out) + int(b_colsum)
            n_scr = int(nk > 1) + int(b_colsum)
            gather_refs = (refs[n_in:n_in + ng], refs[n_in + ng + n_out:n_in + 2 * ng + n_out],
                           *refs[n_in + 2 * ng + n_out + n_scr:])
            refs = refs[:n_in] + refs[n_in + ng:n_in + ng + n_out] + refs[n_in + 2 * ng + n_out:]
            at_first = functools.reduce(jnp.logical_and, [pl.program_id(d) == 0 for d in range(3)])
            at_last = functools.reduce(jnp.logical_and, [pl.program_id(d) == grid[d] - 1 for d in range(3)])

            @pl.when(at_first)
            def _():
                _gather_start(*gather_refs)
        a_ref, b_ref = refs[0], refs[1]
        pos = 2
        bias_ref = pre_ref = add_ref = None
        if bias is not None:
            bias_ref = refs[pos]; pos += 1
        if mul_drelu2 is not None:
            pre_ref = refs[pos]; pos += 1
        if add is not None:
            add_ref = refs[pos]; pos += 1
        o_ref = refs[pos]
        pos += 1
        r_ref = None
        if relu2_out:
            r_ref = refs[pos]; pos += 1
        cs_ref = None
        if b_colsum:
            cs_ref = refs[pos]; pos += 1
        acc_ref = refs[pos] if nk > 1 else None
        cs_acc = refs[pos + 1] if b_colsum else None

        av = a_ref[...]
        if a_fn is not None:
            av = a_fn(av)
        bv = b_ref[...]
        part = lax.dot_general(av.astype(BF16), bv.astype(BF16), dims,
                               preferred_element_type=F32)
        if b_colsum:
            cs_part = jnp.sum(bv.astype(F32).reshape(tk // 8, 8, tn), axis=0)

        def finish(acc):
            if bias_ref is not None:
                acc = acc + bias_ref[...]
            if pre_ref is not None:
                acc = acc * (2.0 * jnp.maximum(pre_ref[...], 0.0))
            if add_ref is not None:
                acc = acc + add_ref[...]
            o_ref[...] = acc.astype(out_dtype)
            if r_ref is not None:
                r_ref[...] = _relu2(acc).astype(BF16)

        if nk == 1:
            finish(part)
        else:
            k = pl.program_id(2)

            @pl.when(k == 0)
            def _():
                acc_ref[...] = part
                if b_colsum:
                    cs_acc[...] = cs_part

            @pl.when(k > 0)
            def _():
                acc_ref[...] += part
                if b_colsum:
                    cs_acc[...] += cs_part

            @pl.when(k == nk - 1)
            def _():
                finish(acc_ref[...])
                if b_colsum:
                    cs_ref[...] = jnp.sum(cs_acc[...], axis=0, keepdims=True)
        if ng:
            @pl.when(at_last)
            def _():
                _gather_finish(*gather_refs)

    a_spec = (pl.BlockSpec((tk, tm), lambda i, j, k: (k, i)) if ta
              else pl.BlockSpec((tm, tk), lambda i, j, k: (i, k)))
    b_spec = (pl.BlockSpec((tn, tk), lambda i, j, k: (j, k)) if tb
              else pl.BlockSpec((tk, tn), lambda i, j, k: (k, j)))
    in_specs = [a_spec, b_spec]
    if bias is not None:
        in_specs.append(pl.BlockSpec((1, tn), lambda i, j, k: (0, j)))
    for _ in extras[(1 if bias is not None else 0):]:
        in_specs.append(pl.BlockSpec((tm, tn), lambda i, j, k: (i, j)))
    o_spec = pl.BlockSpec((tm, tn), lambda i, j, k: (i, j))
    out_specs, out_shape = [o_spec], [jax.ShapeDtypeStruct((m, n), out_dtype)]
    scratch = [pltpu.VMEM((tm, tn), F32)] if nk > 1 else []
    if relu2_out:
        out_specs.append(o_spec)
        out_shape.append(jax.ShapeDtypeStruct((m, n), BF16))
    if b_colsum:
        out_specs.append(pl.BlockSpec((1, tn), lambda i, j, k: (0, j)))
        out_shape.append(jax.ShapeDtypeStruct((1, n), F32))
        scratch.append(pltpu.VMEM((8, tn), F32))
    if ng:
        in_specs += [HBM_SPEC] * ng
        out_specs += [HBM_SPEC] * ng
        out_shape += _gathered_shapes(gather_src)
        scratch += _gather_sems(ng)
    single = len(out_specs) == 1
    outs = pl.pallas_call(
        body, name=name,
        grid=grid,
        in_specs=in_specs,
        out_specs=out_specs[0] if single else out_specs,
        out_shape=out_shape[0] if single else out_shape,
        scratch_shapes=scratch,
        compiler_params=_cparams(("arbitrary",) * 3 if ng else ("parallel", "parallel", "arbitrary")),
    )(a, b, *extras, *(gather_src or []))
    if not ng:
        return outs
    n_out = len(outs) - ng
    return (outs[0] if n_out == 1 else tuple(outs[:n_out])), list(outs[n_out:])


def _rms_fwd_call(x, g, *, name, out_dtype=F32, tq=512):
    t, d = x.shape

    def body(x_ref, g_ref, y_ref):
        xv = x_ref[...]
        r = lax.rsqrt(jnp.mean(xv * xv, axis=-1, keepdims=True) + EPS)
        y_ref[...] = (xv * r * g_ref[...]).astype(out_dtype)

    return pl.pallas_call(
        body, name=name, grid=(t // tq,),
        in_specs=[pl.BlockSpec((tq, d), lambda i: (i, 0)), pl.BlockSpec((1, d), lambda i: (0, 0))],
        out_specs=pl.BlockSpec((tq, d), lambda i: (i, 0)),
        out_shape=jax.ShapeDtypeStruct((t, d), out_dtype),
        compiler_params=_cparams(("parallel",)),
    )(x, g)


def _rms_bwd_call(x, g, dy, *, name, add=None, tq=512):
    t, d = x.shape
    nt = t // tq

    def body(*refs):
        if add is None:
            x_ref, g_ref, dy_ref, dx_ref, dg_ref, acc_ref = refs
        else:
            x_ref, g_ref, dy_ref, add_ref, dx_ref, dg_ref, acc_ref = refs
        i = pl.program_id(0)
        xv = x_ref[...]
        r = lax.rsqrt(jnp.mean(xv * xv, axis=-1, keepdims=True) + EPS)
        xh = xv * r
        dyv = dy_ref[...]
        dxh = dyv * g_ref[...]
        dx = r * (dxh - xh * jnp.mean(dxh * xh, axis=-1, keepdims=True))
        dx_ref[...] = dx if add is None else dx + add_ref[...]
        part = jnp.sum((dyv * xh).reshape(tq // 8, 8, d), axis=0)

        @pl.when(i == 0)
        def _():
            acc_ref[...] = part

        @pl.when(i > 0)
        def _():
            acc_ref[...] += part

        @pl.when(i == nt - 1)
        def _():
            dg_ref[...] = jnp.sum(acc_ref[...], axis=0, keepdims=True)

    blk = pl.BlockSpec((tq, d), lambda i: (i, 0))
    row = pl.BlockSpec((1, d), lambda i: (0, 0))
    extra = [] if add is None else [add]
    return pl.pallas_call(
        body, name=name, grid=(nt,),
        in_specs=[blk, row, blk] + [blk] * len(extra),
        out_specs=[blk, row],
        out_shape=[jax.ShapeDtypeStruct((t, d), F32), jax.ShapeDtypeStruct((1, d), F32)],
        scratch_shapes=[pltpu.VMEM((8, d), F32)],
        compiler_params=_cparams(("arbitrary",)),
    )(x, g, dy, *extra)


def _loss_call(x, g, tgt, *, tq=512):
    t, d = x.shape
    nt = t // tq

    def body(x_ref, g_ref, t_ref, loss_ref, dx_ref, dg_ref, acc_ref, sq_ref):
        i = pl.program_id(0)
        xv = x_ref[...]
        r = lax.rsqrt(jnp.mean(xv * xv, axis=-1, keepdims=True) + EPS)
        xh = xv * r
        gv = g_ref[...]
        err = xh * gv - t_ref[...]
        dyv = err * (1.0 / d)
        dxh = dyv * gv
        dx_ref[...] = r * (dxh - xh * jnp.mean(dxh * xh, axis=-1, keepdims=True))
        part = jnp.sum((dyv * xh).reshape(tq // 8, 8, d), axis=0)
        sq = jnp.sum((err * err).reshape(tq // 8, 8, d), axis=0)

        @pl.when(i == 0)
        def _():
            acc_ref[...] = part
            sq_ref[...] = sq

        @pl.when(i > 0)
        def _():
            acc_ref[...] += part
            sq_ref[...] += sq

        @pl.when(i == nt - 1)
        def _():
            dg_ref[...] = jnp.sum(acc_ref[...], axis=0, keepdims=True)
            tot = jnp.sum(jnp.sum(sq_ref[...], axis=0, keepdims=True), axis=1, keepdims=True)
            loss_ref[...] = jnp.broadcast_to(tot * (0.5 / d), (8, LANES))

    blk = pl.BlockSpec((tq, d), lambda i: (i, 0))
    row = pl.BlockSpec((1, d), lambda i: (0, 0))
    loss, dx, dg = pl.pallas_call(
        body, name="loss", grid=(nt,),
        in_specs=[blk, row, blk],
        out_specs=[pl.BlockSpec((8, LANES), lambda i: (0, 0)), blk, row],
        out_shape=[jax.ShapeDtypeStruct((8, LANES), F32), jax.ShapeDtypeStruct((t, d), F32),
                   jax.ShapeDtypeStruct((1, d), F32)],
        scratch_shapes=[pltpu.VMEM((8, d), F32), pltpu.VMEM((8, d), F32)],
        compiler_params=_cparams(("arbitrary",)),
    )(x, g, tgt)
    return loss[0, 0], dx, dg


MESH_ID = pl.DeviceIdType.MESH
HBM_SPEC = pl.BlockSpec(memory_space=pl.ANY)


def rope_tables(positions):
    half = ROT_DIM // 2
    inv_freq = jnp.power(ROPE_THETA, -jnp.arange(0, ROT_DIM, 2, dtype=F32) / ROT_DIM)
    in_head = jnp.arange(LANES) % HEAD_DIM
    rot = in_head < ROT_DIM
    freq = jnp.where(rot, inv_freq[in_head % half], 0.0)
    ang = positions.astype(F32)[:, None] * freq[None, :]
    cos, sin = jnp.cos(ang), jnp.sin(ang)
    b = jnp.where(jnp.logical_and(rot, in_head >= half)[None, :], sin, 0.0)
    c = jnp.where((in_head < half)[None, :], -sin, 0.0)
    return cos, b, c


def _rope_chunk(xs, a, b, c, transpose):
    half = ROT_DIM // 2
    if transpose:
        return xs * a + pltpu.roll(xs * b, LANES - half, 1) + pltpu.roll(xs * c, half, 1)
    return xs * a + pltpu.roll(xs, half, 1) * b + pltpu.roll(xs, LANES - half, 1) * c


def _dilated_spec(tq, w, d):
    return pl.BlockSpec((tq // d, d * w), lambda i: (i, 0))


def _load_dilated(ref, stage_ref, d, w, j):
    cs = slice(j * LANES, (j + 1) * LANES)
    if d == 1:
        return ref[:, cs].astype(F32)
    for r in range(d):
        stage_ref[pl.ds(r, ref.shape[0], stride=d), :] = ref[:, r * w + j * LANES:r * w + (j + 1) * LANES].astype(F32)
    return stage_ref[...]


def _store_dilated(ref, stage_ref, d, w, j, val):
    if d == 1:
        ref[:, j * LANES:(j + 1) * LANES] = val.astype(ref.dtype)
        return
    stage_ref[...] = val
    for r in range(d):
        rows = stage_ref[pl.ds(r, ref.shape[0], stride=d), :]
        ref[:, r * w + j * LANES:r * w + (j + 1) * LANES] = rows.astype(ref.dtype)


def _stage_buffers(tq, n):
    return [pltpu.VMEM((tq, LANES), F32)] * n


def _rope_gather_call(u, tabs, parts, *, name, dils=(1,), tq=512):
    t = u.shape[0]
    total = sum(w for _, w, _, _ in parts)
    assert all(start % w == 0 for start, w, _, _ in parts)
    n_stage = (total // LANES) * sum(d > 1 for d in dils)

    def body(a_ref, b_ref, c_ref, *refs):
        x_refs, o_refs = refs[:len(parts)], refs[len(parts):len(parts) + len(dils)]
        stages = iter(refs[len(parts) + len(dils):])
        a, b, c = a_ref[...], b_ref[...], c_ref[...]
        g = 0
        for x_ref, (_, w, roped, scale) in zip(x_refs, parts):
            for j in range(w // LANES):
                xs = x_ref[:, j * LANES:(j + 1) * LANES]
                val = _rope_chunk(xs, a, b, c, False) if roped else xs
                val = val * scale if scale != 1.0 else val
                for o_ref, d in zip(o_refs, dils):
                    _store_dilated(o_ref, next(stages) if d > 1 else None, d, total, g, val)
                g += 1

    tab_spec = pl.BlockSpec((tq, LANES), lambda i: (i, 0))
    return pl.pallas_call(
        body, name=name, grid=(t // tq,),
        in_specs=[tab_spec] * 3 + [pl.BlockSpec((tq, w), lambda i, cb=start // w: (i, cb)) for start, w, _, _ in parts],
        out_specs=[_dilated_spec(tq, total, d) for d in dils],
        out_shape=[jax.ShapeDtypeStruct((t // d, d * total), BF16) for d in dils],
        scratch_shapes=_stage_buffers(tq, n_stage),
        compiler_params=_cparams(("parallel",)),
    )(*tabs, *([u] * len(parts)))


def _du_operands(du_buf, n_inputs):
    if du_buf is None:
        return [], [], {}
    return [du_buf], [HBM_SPEC], {n_inputs: 0}


def _du_shape(t):
    return jax.ShapeDtypeStruct((t, D_IN_PAD), BF16)


def _rope_scatter_call(du_buf, t, pieces, col, tabs, *, name, dils=(1,), tq=512):
    total = sum(w for _, w, _, _ in pieces)
    assert col % total == 0 and all(len(arrs) == len(dils) for arrs, _, _, _ in pieces)
    arrays = [a for arrs, _, _, _ in pieces for a in arrs]
    extra, extra_specs, aliases = _du_operands(du_buf, 3 + len(arrays))
    n_stage = (total // LANES) * sum(d > 1 for d in dils)

    def body(a_ref, b_ref, c_ref, *refs):
        o_ref = refs[len(arrays) + len(extra)]
        stages = iter(refs[len(arrays) + len(extra) + 1:])
        a, b, c = a_ref[...], b_ref[...], c_ref[...]
        k = off = 0
        for arrs, w, roped, scale in pieces:
            mine = refs[k:k + len(arrs)]
            k += len(arrs)
            for j in range(w // LANES):
                xs = None
                for r, d in zip(mine, dils):
                    part = _load_dilated(r, next(stages) if d > 1 else None, d, w, j)
                    xs = part if xs is None else xs + part
                if scale != 1.0:
                    xs = xs * scale
                val = _rope_chunk(xs, a, b, c, True) if roped else xs
                o_ref[:, off + j * LANES:off + (j + 1) * LANES] = val.astype(BF16)
            off += w

    tab_spec = pl.BlockSpec((tq, LANES), lambda i: (i, 0))
    in_specs = [tab_spec] * 3 + [_dilated_spec(tq, w, d) for _, w, _, _ in pieces for d in dils]
    return pl.pallas_call(
        body, name=name, grid=(t // tq,),
        in_specs=in_specs + extra_specs,
        out_specs=pl.BlockSpec((tq, total), lambda i: (i, col // total)),
        out_shape=_du_shape(t), input_output_aliases=aliases,
        scratch_shapes=_stage_buffers(tq, n_stage),
        compiler_params=_cparams(("parallel",)),
    )(*tabs, *arrays, *extra)


def _band_masks(first_block, max_dist):
    qi = lax.broadcasted_iota(jnp.int32, (BLK, BLK), 0)
    kj = lax.broadcasted_iota(jnp.int32, (BLK, BLK), 1)
    valid_prev = jnp.logical_and(kj >= qi + (BLK - max_dist), jnp.logical_not(first_block))
    valid_cur = kj <= qi
    return valid_prev, valid_cur


_NN = (((1,), (0,)), ((), ()))
_NT = (((1,), (1,)), ((), ()))
_TN = (((0,), (0,)), ((), ()))


HEAD_STAGE = 8


def _attn_row_maps(nb):
    def cur(i):
        return jnp.minimum(i, nb - 1)

    def prev(i):
        return jnp.maximum(jnp.minimum(i, nb - 1) - 1, 0)

    return cur, prev


def _dil_spec(w, dil, rows, seg=None, off=0):
    seg = w if seg is None else seg
    assert off % w == 0 and (dil == 1 or seg % w == 0)
    return pl.BlockSpec((BLK, w), lambda r, i: (rows(i), (r * seg + off) // w))


def _dil_shape(l, dil, w, dtype=F32):
    return jax.ShapeDtypeStruct((l, dil * w), dtype)


def _attn_fwd_call(qkv2, sink, *, dil, group, max_dist, seg, offs, qw, kw, name):
    l = qkv2.shape[0]
    nh = qw // HEAD_DIM
    nb = l // BLK
    use_sink = sink is not None

    def body(*refs):
        if use_sink:
            sink_ref, refs = refs[0], refs[1:]
        q_ref, kp_ref, kc_ref, vp_ref, vc_ref, o_ref, lse_ref = refs
        valid_prev, valid_cur = _band_masks(pl.program_id(1) == 0, max_dist)
        lane = lax.broadcasted_iota(jnp.int32, (BLK, LANES), 1)
        lse_tile = jnp.zeros((BLK, LANES), F32)

        def dot(a, b, dims=_NN):
            return lax.dot_general(a, b, dims, preferred_element_type=F32)

        for g0 in range(0, nh, HEAD_STAGE):
            heads = list(range(g0, min(g0 + HEAD_STAGE, nh)))
            kv = {}
            for kh in sorted({h // group for h in heads}):
                ks = slice(kh * HEAD_DIM, (kh + 1) * HEAD_DIM)
                kv[kh] = tuple(ref[:, ks].astype(BF16) for ref in (kp_ref, kc_ref, vp_ref, vc_ref))
            qs = [q_ref[:, h * HEAD_DIM:(h + 1) * HEAD_DIM].astype(BF16) for h in heads]
            sps = [jnp.where(valid_prev, dot(qh, kv[h // group][0], _NT), NEG_INF) for h, qh in zip(heads, qs)]
            scs = [jnp.where(valid_cur, dot(qh, kv[h // group][1], _NT), NEG_INF) for h, qh in zip(heads, qs)]
            ms = [jnp.maximum(jnp.max(sp, axis=1, keepdims=True), jnp.max(sc, axis=1, keepdims=True))
                  for sp, sc in zip(sps, scs)]
            if use_sink:
                ms = [jnp.maximum(m, sink_ref[h]) for h, m in zip(heads, ms)]
            pps = [jnp.exp(sp - m) for sp, m in zip(sps, ms)]
            pcs = [jnp.exp(sc - m) for sc, m in zip(scs, ms)]
            dens = [jnp.sum(pp, axis=1, keepdims=True) + jnp.sum(pc, axis=1, keepdims=True)
                    for pp, pc in zip(pps, pcs)]
            if use_sink:
                dens = [den + jnp.exp(sink_ref[h] - m) for h, den, m in zip(heads, dens, ms)]
            outs = [dot(pp.astype(BF16), kv[h // group][2]) + dot(pc.astype(BF16), kv[h // group][3])
                    for h, pp, pc in zip(heads, pps, pcs)]
            for h, o, den, m in zip(heads, outs, dens, ms):
                o_ref[:, h * HEAD_DIM:(h + 1) * HEAD_DIM] = o / den
                lse_tile = jnp.where(lane == h, m + jnp.log(den), lse_tile)
        lse_ref[...] = lse_tile

    cur, prev = _attn_row_maps(nb)
    o_spec, lse_spec = _dil_spec(qw, dil, cur), _dil_spec(LANES, dil, cur)
    in_specs = [_dil_spec(qw, dil, cur, seg, offs[0]),
                _dil_spec(kw, dil, prev, seg, offs[1]), _dil_spec(kw, dil, cur, seg, offs[1]),
                _dil_spec(kw, dil, prev, seg, offs[2]), _dil_spec(kw, dil, cur, seg, offs[2])]
    args = [qkv2] * 5
    if use_sink:
        in_specs = [pl.BlockSpec(memory_space=pltpu.SMEM)] + in_specs
        args = [sink] + args
    return pl.pallas_call(
        body, name=name, grid=(dil, nb),
        in_specs=in_specs,
        out_specs=[o_spec, lse_spec],
        out_shape=[_dil_shape(l, dil, qw), _dil_shape(l, dil, LANES)],
        compiler_params=_cparams(("parallel", "parallel")),
    )(*args)


def _attn_bwd_call(qkv2, sink, o2, lse2, do2, dlse2, *, dil, group, max_dist, seg, offs, qw, kw, name):
    l = qkv2.shape[0]
    nh = qw // HEAD_DIM
    nb = l // BLK
    use_sink = sink is not None

    def body(*refs):
        if use_sink:
            sink_ref, refs = refs[0], refs[1:]
        (q_ref, kp_ref, kc_ref, vp_ref, vc_ref, o_ref, lse_ref, do_ref, dlse_ref,
         dq_ref, dk_ref, dv_ref, dsink_ref, ck_ref, cv_ref) = refs
        step = pl.program_id(1)

        @pl.when(jnp.logical_and(pl.program_id(0) == 0, step == 0))
        def _():
            dsink_ref[...] = jnp.zeros_like(dsink_ref)

        @pl.when(step == 0)
        def _():
            ck_ref[...] = jnp.zeros_like(ck_ref)
            cv_ref[...] = jnp.zeros_like(cv_ref)

        def dot(a, b, dims=_NN):
            return lax.dot_general(a, b, dims, preferred_element_type=F32)

        @pl.when(step < nb)
        def _():
            valid_prev, valid_cur = _band_masks(step == 0, max_dist)
            row = lax.broadcasted_iota(jnp.int32, (8, LANES), 0)
            lanes8 = lax.broadcasted_iota(jnp.int32, (8, LANES), 1)
            ds_tile = jnp.zeros((8, LANES), F32)
            for g0 in range(0, nh, HEAD_STAGE):
                heads = list(range(g0, min(g0 + HEAD_STAGE, nh)))
                hss = [slice(h * HEAD_DIM, (h + 1) * HEAD_DIM) for h in heads]
                kv = {}
                for kh in sorted({h // group for h in heads}):
                    ks = slice(kh * HEAD_DIM, (kh + 1) * HEAD_DIM)
                    kv[kh] = tuple(ref[:, ks].astype(BF16) for ref in (kp_ref, kc_ref, vp_ref, vc_ref))
                qs = [q_ref[:, hs].astype(BF16) for hs in hss]
                dos = [do_ref[:, hs] for hs in hss]
                dobs = [d.astype(BF16) for d in dos]
                lses = [lse_ref[:, h:h + 1] for h in heads]
                sps = [dot(qh, kv[h // group][0], _NT) for h, qh in zip(heads, qs)]
                scs = [dot(qh, kv[h // group][1], _NT) for h, qh in zip(heads, qs)]
                dpps = [dot(dob, kv[h // group][2], _NT) for h, dob in zip(heads, dobs)]
                dpcs = [dot(dob, kv[h // group][3], _NT) for h, dob in zip(heads, dobs)]
                pps = [jnp.where(valid_prev, jnp.exp(jnp.where(valid_prev, sp, NEG_INF) - ls), 0.0)
                       for sp, ls in zip(sps, lses)]
                pcs = [jnp.where(valid_cur, jnp.exp(jnp.where(valid_cur, sc, NEG_INF) - ls), 0.0)
                       for sc, ls in zip(scs, lses)]
                deltas = [jnp.sum(d * o_ref[:, hs], axis=1, keepdims=True) for d, hs in zip(dos, hss)]
                corrs = [dlse_ref[:, h:h + 1] - dl for h, dl in zip(heads, deltas)]
                dsps = [(pp * (dp + c)).astype(BF16) for pp, dp, c in zip(pps, dpps, corrs)]
                dscs = [(pc * (dp + c)).astype(BF16) for pc, dp, c in zip(pcs, dpcs, corrs)]
                for h, hs, dsp, dsc in zip(heads, hss, dsps, dscs):
                    dq = dot(dsp, kv[h // group][0]) + dot(dsc, kv[h // group][1])
                    dq_ref[:, hs] = dq.astype(BF16)
                parts = [(dot(dsc, qh, _TN), dot(dsp, qh, _TN),
                          dot(pc.astype(BF16), dob, _TN), dot(pp.astype(BF16), dob, _TN))
                         for dsc, dsp, qh, pc, pp, dob in zip(dscs, dsps, qs, pcs, pps, dobs)]
                for kh in kv:
                    ks = slice(kh * HEAD_DIM, (kh + 1) * HEAD_DIM)
                    mine = [p for h, p in zip(heads, parts) if h // group == kh]
                    dkc, dkp, dvc, dvp = (sum(p[j] for p in mine[1:]) + mine[0][j] for j in range(4))
                    dk_ref[:, ks] = (ck_ref[:, ks] + dkp).astype(BF16)
                    dv_ref[:, ks] = (cv_ref[:, ks] + dvp).astype(BF16)
                    ck_ref[:, ks] = dkc
                    cv_ref[:, ks] = dvc
                if use_sink:
                    for h, ls, dl in zip(heads, lses, deltas):
                        val = -jnp.sum(jnp.exp(sink_ref[h] - ls) * dl, axis=0, keepdims=True)
                        ds_tile = jnp.where(jnp.logical_and(row == 0, lanes8 == h), val, ds_tile)
            if use_sink:
                dsink_ref[...] += ds_tile

        @pl.when(step == nb)
        def _():
            dk_ref[...] = ck_ref[...].astype(BF16)
            dv_ref[...] = cv_ref[...].astype(BF16)

    cur, prev = _attn_row_maps(nb)
    q_spec, lse_spec = _dil_spec(qw, dil, cur), _dil_spec(LANES, dil, cur)
    lag_spec = _dil_spec(kw, dil, lambda i: jnp.maximum(i - 1, 0))
    in_specs = [_dil_spec(qw, dil, cur, seg, offs[0]),
                _dil_spec(kw, dil, prev, seg, offs[1]), _dil_spec(kw, dil, cur, seg, offs[1]),
                _dil_spec(kw, dil, prev, seg, offs[2]), _dil_spec(kw, dil, cur, seg, offs[2]),
                q_spec, lse_spec, q_spec, lse_spec]
    args = [qkv2] * 5 + [o2, lse2, do2, dlse2]
    if use_sink:
        in_specs = [pl.BlockSpec(memory_space=pltpu.SMEM)] + in_specs
        args = [sink] + args
    kv_shape = _dil_shape(l, dil, kw, BF16)
    return pl.pallas_call(
        body, name=name, grid=(dil, nb + 1),
        in_specs=in_specs,
        out_specs=[q_spec, lag_spec, lag_spec, pl.BlockSpec((8, LANES), lambda r, i: (0, 0))],
        out_shape=[_dil_shape(l, dil, qw, BF16), kv_shape, kv_shape,
                   jax.ShapeDtypeStruct((8, LANES), F32)],
        scratch_shapes=[pltpu.VMEM((BLK, kw), F32), pltpu.VMEM((BLK, kw), F32)],
        compiler_params=_cparams(("arbitrary", "arbitrary")),
    )(*args)


def _attn_config(tag, dil, group, max_dist, seg, offs, qw, kw):
    return dict(name=tag, dil=dil, group=group, max_dist=max_dist, seg=seg, offs=offs, qw=qw, kw=kw)


A_W = 8 * HEAD_DIM
ATTN_A_CFGS = tuple(_attn_config("attn_a%d" % dil, dil, 1, window // dil, 3 * A_W, (0, A_W, 2 * A_W), A_W, A_W)
                    for window, dil in A_CONFIGS)
B_KVW = 2 * HEAD_DIM
A_DILS = tuple(cfg["dil"] for cfg in ATTN_A_CFGS)
ATTN_B_CFG = _attn_config("attn_b", 1, B_GROUP, BLK - 1, A_W + 2 * B_KVW, (0, A_W, A_W + B_KVW), A_W, B_KVW)


def _attn_fwd(cfg, qkv2, sink):
    kw = {k: v for k, v in cfg.items() if k != "name"}
    return _attn_fwd_call(qkv2, sink, name=cfg["name"] + "_fwd", **kw)


def _attn_bwd(cfg, qkv2, o2, lse2, sink, do2, dlse2):
    kw = {k: v for k, v in cfg.items() if k != "name"}
    return _attn_bwd_call(qkv2, sink, o2, lse2, do2, dlse2, name=cfg["name"] + "_bwd", **kw)


def _head_expand():
    r = lax.broadcasted_iota(jnp.int32, (LANES, 8 * HEAD_DIM), 0)
    c = lax.broadcasted_iota(jnp.int32, (LANES, 8 * HEAD_DIM), 1)
    return (c // HEAD_DIM == r).astype(F32)


def _combine_weights(l0, l1, l2):
    m = jnp.maximum(jnp.maximum(l0, l1), l2)
    e0, e1, e2 = jnp.exp(l0 - m), jnp.exp(l1 - m), jnp.exp(l2 - m)
    inv = 1.0 / (e0 + e1 + e2)
    return e0 * inv, e1 * inv, e2 * inv


def _combine_fwd_call(os_, lses, dils, *, tq=512):
    w = os_[0].shape[1] // dils[0]
    t = os_[0].shape[0] * dils[0]
    groups = w // LANES
    n_stage = (groups + 1) * sum(d > 1 for d in dils)

    def body(*refs):
        o_refs, l_refs, y_ref = refs[:3], refs[3:6], refs[6]
        stages = iter(refs[7:])
        ws = _combine_weights(*[_load_dilated(l, next(stages) if d > 1 else None, d, LANES, 0)
                                for l, d in zip(l_refs, dils)])
        e = _head_expand()
        wide = [_dot_mask(e, wt, mask_left=False) for wt in ws]
        for j in range(groups):
            cs = slice(j * LANES, (j + 1) * LANES)
            y = None
            for o_ref, d, wd in zip(o_refs, dils, wide):
                term = wd[:, cs] * _load_dilated(o_ref, next(stages) if d > 1 else None, d, w, j)
                y = term if y is None else y + term
            y_ref[:, cs] = y

    return pl.pallas_call(
        body, name="combine_fwd", grid=(t // tq,),
        in_specs=[_dilated_spec(tq, w, d) for d in dils] + [_dilated_spec(tq, LANES, d) for d in dils],
        out_specs=pl.BlockSpec((tq, w), lambda i: (i, 0)),
        out_shape=jax.ShapeDtypeStruct((t, w), F32),
        scratch_shapes=_stage_buffers(tq, n_stage),
        compiler_params=_cparams(("parallel",)),
    )(*os_, *lses)


def _combine_bwd_call(os_, lses, dy, dils, *, tq=512):
    t, w = dy.shape
    groups = w // LANES
    n_stage = 2 * (groups + 1) * sum(d > 1 for d in dils)

    def body(*refs):
        o_refs, l_refs, dy_ref, do_refs, dl_refs = refs[:3], refs[3:6], refs[6], refs[7:10], refs[10:13]
        stages = iter(refs[13:])

        def stage(d):
            return next(stages) if d > 1 else None

        ws = _combine_weights(*[_load_dilated(l, stage(d), d, LANES, 0) for l, d in zip(l_refs, dils)])
        e = _head_expand()
        dyv = dy_ref[...]
        dws = []
        for o_ref, do_ref, d, wt in zip(o_refs, do_refs, dils, ws):
            do = _dot_mask(e, wt, mask_left=False) * dyv
            for j in range(groups):
                _store_dilated(do_ref, stage(d), d, w, j, do[:, j * LANES:(j + 1) * LANES])
            ov = jnp.concatenate([_load_dilated(o_ref, stage(d), d, w, j) for j in range(groups)], axis=1)
            dws.append(_dot_mask(e, dyv * ov, _NT, mask_left=False))
        mean = ws[0] * dws[0] + ws[1] * dws[1] + ws[2] * dws[2]
        for dl_ref, d, wt, dw in zip(dl_refs, dils, ws, dws):
            _store_dilated(dl_ref, stage(d), d, LANES, 0, wt * (dw - mean))

    o_specs = [_dilated_spec(tq, w, d) for d in dils]
    l_specs = [_dilated_spec(tq, LANES, d) for d in dils]
    return pl.pallas_call(
        body, name="combine_bwd", grid=(t // tq,),
        in_specs=o_specs + l_specs + [pl.BlockSpec((tq, w), lambda i: (i, 0))], out_specs=o_specs + l_specs,
        out_shape=[jax.ShapeDtypeStruct((t // d, d * w), F32) for d in dils]
        + [jax.ShapeDtypeStruct((t // d, d * LANES), F32) for d in dils],
        scratch_shapes=_stage_buffers(tq, n_stage),
        compiler_params=_cparams(("parallel",)),
    )(*os_, *lses, dy)


C_QKW = C_QK_HEADS * C_DK
C_CONV_W = 2 * C_QKW + C_V_HEADS * C_DK
HALO = 8


def _silu_parts(z):
    sig = jax.nn.sigmoid(z)
    return z * sig, sig * (1.0 + z * (1.0 - sig))


def _conv_window_specs(tq, t):
    c = C_CONV_W
    cb = COL["c_qkv"] // c
    blk = pl.BlockSpec((tq, c), lambda i: (i, cb))
    before = pl.BlockSpec((HALO, c), lambda i: (jnp.maximum(i * (tq // HALO) - 1, 0), cb))
    return c, cb, blk, before


def _conv_prep_fwd_call(u, w, *, tq=512):
    t = u.shape[0]
    c, _, x_spec, halo_spec = _conv_window_specs(tq, t)
    nqk = 2 * C_QK_HEADS

    def body(x_ref, halo_ref, w_ref, z_ref, qk_ref, v_ref):
        i = pl.program_id(0)
        halo = jnp.where(i == 0, 0.0, halo_ref[...])
        xc = jnp.concatenate([halo, x_ref[...]], axis=0)
        wv = w_ref[...]
        z = xc[HALO - 3:HALO - 3 + tq] * wv[0:1]
        for j in range(1, C_CONV):
            z = z + xc[HALO - 3 + j:HALO - 3 + j + tq] * wv[j:j + 1]
        z_ref[...] = z
        act, _ = _silu_parts(z)
        for h in range(nqk):
            a = act[:, h * C_DK:(h + 1) * C_DK]
            qk_ref[:, h * C_DK:(h + 1) * C_DK] = a * lax.rsqrt(jnp.sum(a * a, axis=1, keepdims=True) + EPS)
        v_ref[...] = act[:, nqk * C_DK:]

    return pl.pallas_call(
        body, name="conv_prep_fwd", grid=(t // tq,),
        in_specs=[x_spec, halo_spec, pl.BlockSpec((C_CONV, c), lambda i: (0, 0))],
        out_specs=[pl.BlockSpec((tq, c), lambda i: (i, 0)),
                   pl.BlockSpec((tq, 2 * C_QKW), lambda i: (i, 0)),
                   pl.BlockSpec((tq, c - 2 * C_QKW), lambda i: (i, 0))],
        out_shape=[jax.ShapeDtypeStruct((t, c), F32), jax.ShapeDtypeStruct((t, 2 * C_QKW), F32),
                   jax.ShapeDtypeStruct((t, c - 2 * C_QKW), F32)],
        compiler_params=_cparams(("parallel",)),
    )(u, u, w)


def _conv_prep_dz_call(z, dqk, dv, *, tq=512):
    t, c = z.shape
    nqk = 2 * C_QK_HEADS

    def body(z_ref, dqk_ref, dv_ref, dz_ref):
        zv = z_ref[...]
        act, dact = _silu_parts(zv)
        for h in range(nqk):
            hs = slice(h * C_DK, (h + 1) * C_DK)
            a = act[:, hs]
            r = lax.rsqrt(jnp.sum(a * a, axis=1, keepdims=True) + EPS)
            nrm = a * r
            dn = dqk_ref[:, hs]
            da = r * (dn - nrm * jnp.sum(dn * nrm, axis=1, keepdims=True))
            dz_ref[:, hs] = da * dact[:, hs]
        dz_ref[:, nqk * C_DK:] = dv_ref[...] * dact[:, nqk * C_DK:]

    return pl.pallas_call(
        body, name="conv_prep_dz", grid=(t // tq,),
        in_specs=[pl.BlockSpec((tq, c), lambda i: (i, 0)),
                  pl.BlockSpec((tq, 2 * C_QKW), lambda i: (i, 0)),
                  pl.BlockSpec((tq, c - 2 * C_QKW), lambda i: (i, 0))],
        out_specs=pl.BlockSpec((tq, c), lambda i: (i, 0)),
        out_shape=jax.ShapeDtypeStruct((t, c), F32),
        compiler_params=_cparams(("parallel",)),
    )(z, dqk, dv)


def _conv_bwd_call(u, dz, w, du_buf, *, tq=512):
    t = u.shape[0]
    nt = t // tq
    c, cb, x_spec, halo_spec = _conv_window_specs(tq, t)
    extra, extra_specs, aliases = _du_operands(du_buf, 5)

    def body(x_ref, xh_ref, dz_ref, dzh_ref, w_ref, *refs):
        dx_ref, dw_ref = refs[len(extra):]
        i = pl.program_id(0)
        xc = jnp.concatenate([jnp.where(i == 0, 0.0, xh_ref[...]), x_ref[...]], axis=0)
        dzv = dz_ref[...]
        dzc = jnp.concatenate([dzv, jnp.where(i == nt - 1, 0.0, dzh_ref[...])], axis=0)
        wv = w_ref[...]
        dx = dzv * wv[3:4]
        for s in range(1, C_CONV):
            dx = dx + dzc[s:s + tq] * wv[3 - s:4 - s]
        dx_ref[...] = dx.astype(BF16)
        row = lax.broadcasted_iota(jnp.int32, (8, c), 0)
        dw = jnp.zeros((8, c), F32)
        for j in range(C_CONV):
            prod = dzv * xc[HALO - 3 + j:HALO - 3 + j + tq]
            col = jnp.sum(jnp.sum(prod.reshape(tq // 8, 8, c), axis=0), axis=0, keepdims=True)
            dw = jnp.where(row == j, col, dw)

        @pl.when(i == 0)
        def _():
            dw_ref[...] = dw

        @pl.when(i > 0)
        def _():
            dw_ref[...] += dw

    blk = pl.BlockSpec((tq, c), lambda i: (i, 0))
    after = pl.BlockSpec((HALO, c), lambda i: (jnp.minimum((i + 1) * (tq // HALO), t // HALO - 1), 0))
    return pl.pallas_call(
        body, name="conv_bwd", grid=(nt,),
        in_specs=[x_spec, halo_spec, blk, after, pl.BlockSpec((C_CONV, c), lambda i: (0, 0))] + extra_specs,
        out_specs=[pl.BlockSpec((tq, c), lambda i: (i, cb)), pl.BlockSpec((8, c), lambda i: (0, 0))],
        out_shape=[_du_shape(t), jax.ShapeDtypeStruct((8, c), F32)],
        input_output_aliases=aliases,
        compiler_params=_cparams(("arbitrary",)),
    )(u, u, dz, dz, w, *extra)


C_VW = C_V_HEADS * C_DK


def _softplus(x):
    return jnp.maximum(x, 0.0) + jnp.log(1.0 + jnp.exp(-jnp.abs(x)))


def _tri_masks():
    r = lax.broadcasted_iota(jnp.int32, (CHUNK, CHUNK), 0)
    c = lax.broadcasted_iota(jnp.int32, (CHUNK, CHUNK), 1)
    return r >= c, r > c


def _split_bf16(a):
    hi = a.astype(BF16)
    return hi, (a - hi.astype(F32)).astype(BF16)


def _dot_hi(a, b, dims=None):
    dims = _NN if dims is None else dims
    ah, al = _split_bf16(a)
    bh, bl = _split_bf16(b)

    def d(x, y):
        return lax.dot_general(x, y, dims, preferred_element_type=F32)

    return d(ah, bh) + (d(ah, bl) + d(al, bh))


def _dot_mask(mask, b, dims=None, mask_left=True):
    dims = _NN if dims is None else dims
    mb = mask.astype(BF16)
    b1 = b.astype(BF16)
    rest = b - b1.astype(F32)
    b2 = rest.astype(BF16)
    b3 = (rest - b2.astype(F32)).astype(BF16)
    out = None
    for p in (b1, b2, b3):
        term = (lax.dot_general(mb, p, dims, preferred_element_type=F32) if mask_left
                else lax.dot_general(p, mb, dims, preferred_element_type=F32))
        out = term if out is None else out + term
    return out


def _unit_lower_inverses(mats):
    r = lax.broadcasted_iota(jnp.int32, (CHUNK, CHUNK), 0)
    c = lax.broadcasted_iota(jnp.int32, (CHUNK, CHUNK), 1)
    eye = (r == c).astype(F32)
    xs = [eye - a for a in mats]
    ps = [_dot_hi(a, a) for a in mats]
    steps = int(math.log2(CHUNK)) - 1
    for s in range(steps):
        xs = [x + _dot_hi(x, p) for x, p in zip(xs, ps)]
        if s < steps - 1:
            ps = [_dot_hi(p, p) for p in ps]
    return xs


def _gate_tiles(cab, alog, dtb):
    pre = cab + dtb
    g = -jnp.exp(alog) * _softplus(pre)
    beta = jax.nn.sigmoid(pltpu.roll(cab, LANES - C_V_HEADS, 1))
    return g, beta, pre


def _chunk_common(kk, qk, gc, gct, beta, h, tri, strict):
    gcol, grow, bcol = gc[:, h:h + 1], gct[h:h + 1, :], beta[:, h:h + 1]
    decay = jnp.where(tri, jnp.exp(jnp.where(tri, gcol - grow, 0.0)), 0.0)
    kkd = jnp.where(strict, kk * decay, 0.0)
    attn = jnp.where(tri, qk * decay, 0.0)
    glast = gc[CHUNK - 1:CHUNK, h:h + 1]
    return gcol, bcol, decay, kkd, attn, glast


def _cab_spec():
    return pl.BlockSpec((CHUNK, LANES), lambda n: (n, COL["c_ab"] // LANES))


def _delta_prep_call(qk, v, u, alog, dtb, gather_src=None):
    t = qk.shape[0]
    nc = t // CHUNK
    scale = C_DK ** -0.5
    riding = gather_src is not None
    ng = len(gather_src) if riding else 0

    def body(q_ref, k_ref, v_ref, cab_ref, alog_ref, dtb_ref, *refs):
        if riding:
            gather_refs = (refs[:ng], refs[ng + 8:2 * ng + 8]) + tuple(refs[2 * ng + 8:])
            refs = refs[ng:ng + 8]

            @pl.when(pl.program_id(0) == 0)
            def _():
                _gather_start(*gather_refs)
        u_ref, w_ref, qd_ref, kd_ref, attn_ref, tmat_ref, gc_ref, beta_ref = refs
        tri, strict = _tri_masks()
        g, beta, _ = _gate_tiles(cab_ref[...], alog_ref[...], dtb_ref[...])
        gc = _dot_mask(tri, g)
        gct = gc.T
        gc_ref[...] = gc
        beta_ref[...] = beta
        mats, rhs = [], []
        for j in range(C_QK_HEADS):
            js = slice(j * C_DK, (j + 1) * C_DK)
            kf, qf = k_ref[:, js], q_ref[:, js] * scale
            kb, qb = kf.astype(BF16), qf.astype(BF16)
            kk = lax.dot_general(kb, kb, _NT, preferred_element_type=F32)
            qk = lax.dot_general(qb, kb, _NT, preferred_element_type=F32)
            for h in (2 * j, 2 * j + 1):
                hs = slice(h * C_DK, (h + 1) * C_DK)
                gcol, bcol, decay, kkd, attn, glast = _chunk_common(kk, qk, gc, gct, beta, h, tri, strict)
                gexp = jnp.exp(gcol)
                mats.append(kkd * bcol)
                rhs.append(jnp.concatenate([v_ref[:, hs] * bcol, kf * (bcol * gexp)], axis=1))
                qd_ref[:, hs] = (qf * gexp).astype(BF16)
                kd_ref[:, hs] = (kf * jnp.exp(glast - gcol)).astype(BF16)
                attn_ref[:, h * CHUNK:(h + 1) * CHUNK] = attn.astype(BF16)
        for h, (tmat, r) in enumerate(zip(_unit_lower_inverses(mats), rhs)):
            hs = slice(h * C_DK, (h + 1) * C_DK)
            uw = _dot_hi(tmat, r)
            u_ref[:, hs] = uw[:, :C_DK]
            w_ref[:, hs] = uw[:, C_DK:]
            tmat_ref[:, h * CHUNK:(h + 1) * CHUNK] = tmat.T
        if riding:
            @pl.when(pl.program_id(0) == nc - 1)
            def _():
                _gather_finish(*gather_refs)

    def blk(w):
        return pl.BlockSpec((CHUNK, w), lambda n: (n, 0))

    row = pl.BlockSpec((1, LANES), lambda n: (0, 0))
    big = jax.ShapeDtypeStruct((t, C_VW), F32)
    sq = jax.ShapeDtypeStruct((t, C_V_HEADS * CHUNK), F32)
    tile = jax.ShapeDtypeStruct((t, LANES), F32)
    half = jax.ShapeDtypeStruct((t, C_VW), BF16)
    in_specs = [blk(C_QKW), pl.BlockSpec((CHUNK, C_QKW), lambda n: (n, 1)), blk(C_VW), _cab_spec(), row, row]
    out_specs = [blk(C_VW)] * 4 + [blk(C_V_HEADS * CHUNK)] * 2 + [blk(LANES)] * 2
    out_shape = [big, big, half, half, jax.ShapeDtypeStruct(sq.shape, BF16), sq] + [tile] * 2
    args = [qk, qk, v, u, alog, dtb]
    if riding:
        in_specs += [HBM_SPEC] * ng
        out_specs += [HBM_SPEC] * ng
        out_shape += _gathered_shapes(gather_src)
        args += list(gather_src)
    return pl.pallas_call(
        body, name="delta_prep_gather" if riding else "delta_prep", grid=(nc,),
        in_specs=in_specs, out_specs=out_specs, out_shape=out_shape,
        scratch_shapes=_gather_sems(ng) if riding else [],
        compiler_params=_cparams(("arbitrary",) if riding else ("parallel",)),
    )(*args)


SCAN_SUB = 4


def _delta_scan_call(u, w, qd, kd, attn, gc):
    t = u.shape[0]
    nc = t // CHUNK
    rows = SCAN_SUB * CHUNK

    def body(u_ref, w_ref, qd_ref, kd_ref, attn_ref, gc_ref, o_ref, vn_ref, st_ref, s_ref):
        @pl.when(pl.program_id(0) == 0)
        def _():
            s_ref[...] = jnp.zeros_like(s_ref)

        hss = [slice(h * C_DK, (h + 1) * C_DK) for h in range(C_V_HEADS)]
        states = [s_ref[hs, :] for hs in hss]
        for c in range(SCAN_SUB):
            rs = slice(c * CHUNK, (c + 1) * CHUNK)
            for hs, s in zip(hss, states):
                st_ref[c, hs, :] = s
            sbs = [s.astype(BF16) for s in states]
            vns = [u_ref[rs, hs] - jnp.dot(w_ref[rs, hs].astype(BF16), sb, preferred_element_type=F32)
                   for hs, sb in zip(hss, sbs)]
            qss = [jnp.dot(qd_ref[rs, hs].astype(BF16), sb, preferred_element_type=F32) for hs, sb in zip(hss, sbs)]
            vnbs = [vn.astype(BF16) for vn in vns]
            for h, hs in enumerate(hss):
                vn_ref[rs, hs] = vnbs[h]
                o_ref[rs, hs] = qss[h] + jnp.dot(attn_ref[rs, h * CHUNK:(h + 1) * CHUNK].astype(BF16), vnbs[h],
                                                 preferred_element_type=F32)
            last = (c + 1) * CHUNK - 1
            states = [states[h] * jnp.exp(gc_ref[last:last + 1, h:h + 1])
                      + lax.dot_general(kd_ref[rs, hs].astype(BF16), vnbs[h], _TN, preferred_element_type=F32)
                      for h, hs in enumerate(hss)]
        for hs, s in zip(hss, states):
            s_ref[hs, :] = s

    def blk(wd):
        return pl.BlockSpec((rows, wd), lambda n: (n, 0))

    big = jax.ShapeDtypeStruct((t, C_VW), F32)
    return pl.pallas_call(
        body, name="delta_scan", grid=(nc // SCAN_SUB,),
        in_specs=[blk(C_VW)] * 4 + [blk(C_V_HEADS * CHUNK), blk(LANES)],
        out_specs=[blk(C_VW), blk(C_VW), pl.BlockSpec((SCAN_SUB, C_VW, C_DK), lambda n: (n, 0, 0))],
        out_shape=[big, jax.ShapeDtypeStruct((t, C_VW), BF16), jax.ShapeDtypeStruct((nc, C_VW, C_DK), F32)],
        scratch_shapes=[pltpu.VMEM((C_VW, C_DK), F32)],
        compiler_params=_cparams(("arbitrary",)),
    )(u, w, qd, kd, attn, gc)


def _delta_scan_bwd_call(do, w, qd, kd, attn, gc, vn, st):
    t = do.shape[0]
    nc = t // CHUNK
    rows = SCAN_SUB * CHUNK
    steps = nc // SCAN_SUB

    def body(do_ref, w_ref, qd_ref, kd_ref, attn_ref, gc_ref, vn_ref, st_ref,
             du_ref, dw_ref, dqd_ref, dkd_ref, dattn_ref, dgl_ref, ds_ref):
        @pl.when(pl.program_id(0) == 0)
        def _():
            ds_ref[...] = jnp.zeros_like(ds_ref)

        tri, _ = _tri_masks()
        row = lax.broadcasted_iota(jnp.int32, (8, LANES), 0)
        lane = lax.broadcasted_iota(jnp.int32, (8, LANES), 1)
        hss = [slice(h * C_DK, (h + 1) * C_DK) for h in range(C_V_HEADS)]
        css = [slice(h * CHUNK, (h + 1) * CHUNK) for h in range(C_V_HEADS)]

        def dg(a, b, dims):
            return lax.dot_general(a, b, dims, preferred_element_type=F32)

        dsps = [ds_ref[hs, :] for hs in hss]
        for c in reversed(range(SCAN_SUB)):
            rs = slice(c * CHUNK, (c + 1) * CHUNK)
            dgl = jnp.zeros((8, LANES), F32)
            ss = [st_ref[c, hs, :] for hs in hss]
            sbs = [s.astype(BF16) for s in ss]
            dspbs = [d.astype(BF16) for d in dsps]
            dobs = [do_ref[rs, hs].astype(BF16) for hs in hss]
            vnbs = [vn_ref[rs, hs].astype(BF16) for hs in hss]
            dvns = [dg(attn_ref[rs, cs].astype(BF16), dob, _TN) + dg(kd_ref[rs, hs].astype(BF16), dspb, _NN)
                    for hs, cs, dob, dspb in zip(hss, css, dobs, dspbs)]
            for h, hs in enumerate(hss):
                dqd_ref[rs, hs] = dg(dobs[h], sbs[h], _NT)
                dkd_ref[rs, hs] = dg(vnbs[h], dspbs[h], _NT)
                dattn_ref[rs, css[h]] = jnp.where(tri, dg(dobs[h], vnbs[h], _NT), 0.0)
            dvnbs = [d.astype(BF16) for d in dvns]
            for h, hs in enumerate(hss):
                du_ref[rs, hs] = dvns[h]
                dw_ref[rs, hs] = -dg(dvnbs[h], sbs[h], _NT)
                tot = jnp.sum(jnp.sum(dsps[h] * ss[h], axis=0, keepdims=True), axis=1, keepdims=True)
                dgl = jnp.where(jnp.logical_and(row == 0, lane == h), tot, dgl)
            dgl_ref[c * 8:(c + 1) * 8, :] = dgl
            last = (c + 1) * CHUNK - 1
            dsps = [dg(qd_ref[rs, hs].astype(BF16), dobs[h], _TN) + jnp.exp(gc_ref[last:last + 1, h:h + 1]) * dsps[h]
                    - dg(w_ref[rs, hs].astype(BF16), dvnbs[h], _TN) for h, hs in enumerate(hss)]
        for hs, d in zip(hss, dsps):
            ds_ref[hs, :] = d

    def blk(wd):
        return pl.BlockSpec((rows, wd), lambda n: (steps - 1 - n, 0))

    big = jax.ShapeDtypeStruct((t, C_VW), F32)
    return pl.pallas_call(
        body, name="delta_scan_bwd", grid=(steps,),
        in_specs=[blk(C_VW)] * 4 + [blk(C_V_HEADS * CHUNK), blk(LANES), blk(C_VW),
                                    pl.BlockSpec((SCAN_SUB, C_VW, C_DK), lambda n: (steps - 1 - n, 0, 0))],
        out_specs=[blk(C_VW)] * 4 + [blk(C_V_HEADS * CHUNK),
                                     pl.BlockSpec((SCAN_SUB * 8, LANES), lambda n: (steps - 1 - n, 0))],
        out_shape=[big] * 4 + [jax.ShapeDtypeStruct((t, C_V_HEADS * CHUNK), F32),
                               jax.ShapeDtypeStruct((nc * 8, LANES), F32)],
        scratch_shapes=[pltpu.VMEM((C_VW, C_DK), F32)],
        compiler_params=_cparams(("arbitrary",)),
    )(do, w, qd, kd, attn, gc, vn, st)


PREP_SUB = 4


def _delta_prep_bwd_call(qk, v, proj, alog, dtb, tmat, u, w, gc, beta, du, dw, dqd, dkd, dattn, dgl, du_buf):
    t = qk.shape[0]
    extra, extra_specs, aliases = _du_operands(du_buf, 17)
    nc = t // CHUNK
    rows = PREP_SUB * CHUNK
    scale = C_DK ** -0.5

    def body(q_ref, k_ref, v_ref, cab_ref, alog_ref, dtb_ref, tmat_ref, u_ref, w_ref, gc_ref, beta_ref,
             du_ref, dw_ref, dqd_ref, dkd_ref, dattn_ref, dgl_ref, *outs):
        dcab_ref, dqk_ref, dv_ref, dpar_ref = outs[len(extra):]
        tri, strict = _tri_masks()
        lane = lax.broadcasted_iota(jnp.int32, (CHUNK, LANES), 1)
        rowi = lax.broadcasted_iota(jnp.int32, (CHUNK, 1), 0)
        subs = range(PREP_SUB)
        rss = [slice(c * CHUNK, (c + 1) * CHUNK) for c in subs]
        betas = [beta_ref[rs, :] for rs in rss]

        def dot(x, y, dims=_NN):
            return lax.dot_general(x, y, dims, preferred_element_type=F32)

        heads = []
        for c, rs in zip(subs, rss):
            gc = gc_ref[rs, :]
            gct = gc.T
            for j in range(C_QK_HEADS):
                js = slice(j * C_DK, (j + 1) * C_DK)
                kf, qf = k_ref[rs, js], q_ref[rs, js] * scale
                kb, qb = kf.astype(BF16), qf.astype(BF16)
                kk = dot(kb, kb, _NT)
                qk = dot(qb, kb, _NT)
                for h in (2 * j, 2 * j + 1):
                    heads.append((c, rs, h, kf, qf, kb, qb) + _chunk_common(kk, qk, gc, gct, betas[c], h, tri, strict))

        def cols(h):
            return slice(h * C_DK, (h + 1) * C_DK)

        def sq(h):
            return slice(h * CHUNK, (h + 1) * CHUNK)

        dvks = [_dot_hi(tmat_ref[hd[1], sq(hd[2])],
                        jnp.concatenate([du_ref[hd[1], cols(hd[2])], dw_ref[hd[1], cols(hd[2])]], axis=1))
                for hd in heads]
        das = [-jnp.where(strict, _dot_hi(dvk, jnp.concatenate([u_ref[hd[1], cols(hd[2])], w_ref[hd[1], cols(hd[2])]],
                                                               axis=1), _NT), 0.0)
               for hd, dvk in zip(heads, dvks)]
        pre = []
        for (c, rs, h, kf, qf, kb, qb, gcol, bcol, decay, kkd, attn, glast), da in zip(heads, das):
            dattn_h = dattn_ref[rs, sq(h)]
            pre.append(((da * decay * bcol).astype(BF16), (dattn_h * decay).astype(BF16),
                        da * kkd * bcol + dattn_h * attn))
        mms = [(dot(dkk, hd[5]), dot(dkk, hd[5], _TN), dot(dqk, hd[6], _TN), dot(dqk, hd[5]))
               for hd, (dkk, dqk, e) in zip(heads, pre)]
        dq_parts, dk_parts = {}, {}
        dgc_tiles = [jnp.zeros((CHUNK, LANES), F32) for _ in subs]
        db_tiles = [jnp.zeros((CHUNK, LANES), F32) for _ in subs]
        later_tiles = [jnp.zeros((CHUNK, LANES), F32) for _ in subs]
        upper = jnp.logical_not(strict)
        for (c, rs, h, kf, qf, kb, qb, gcol, bcol, decay, kkd, attn, glast), dvk, da, (_, _, e), mm in zip(
                heads, dvks, das, pre, mms):
            hs = cols(h)
            gexp = jnp.exp(gcol)
            fdec = jnp.exp(glast - gcol)
            dvb, dkb = dvk[:, :C_DK], dvk[:, C_DK:]
            dgc = jnp.sum(e, axis=1, keepdims=True)
            later = jnp.sum(jnp.where(upper, jnp.sum(e, axis=0, keepdims=True), 0.0), axis=1, keepdims=True)
            later_tiles[c] = jnp.where(lane == h, later, later_tiles[c])
            dk_parts[c, h] = mm[0] + mm[1] + mm[2] + dkb * (bcol * gexp) + dkd_ref[rs, hs] * fdec
            dq_parts[c, h] = mm[3] + dqd_ref[rs, hs] * gexp
            dv_ref[rs, hs] = dvb * bcol
            s_kb = jnp.sum(dkb * kf, axis=1, keepdims=True)
            db = (jnp.sum(da * kkd, axis=1, keepdims=True) + jnp.sum(dvb * v_ref[rs, hs], axis=1, keepdims=True)
                  + s_kb * gexp)
            rho = jnp.sum(dkd_ref[rs, hs] * kf, axis=1, keepdims=True) * fdec
            dgc = (dgc + s_kb * bcol * gexp + jnp.sum(dqd_ref[rs, hs] * qf, axis=1, keepdims=True) * gexp - rho)
            last = jnp.sum(rho, axis=0, keepdims=True) + dgl_ref[c * 8:c * 8 + 1, h:h + 1] * jnp.exp(glast)
            dgc = dgc + jnp.where(rowi == CHUNK - 1, last, 0.0)
            dgc_tiles[c] = jnp.where(lane == h, dgc, dgc_tiles[c])
            db_tiles[c] = jnp.where(lane == h, db, db_tiles[c])
        alog = alog_ref[...]
        row8 = lax.broadcasted_iota(jnp.int32, (8, LANES), 0)
        par = jnp.zeros((8, LANES), F32)
        for c, rs in zip(subs, rss):
            for j in range(C_QK_HEADS):
                dqk_ref[rs, j * C_DK:(j + 1) * C_DK] = (dq_parts[c, 2 * j] + dq_parts[c, 2 * j + 1]) * scale
                dqk_ref[rs, C_QKW + j * C_DK:C_QKW + (j + 1) * C_DK] = dk_parts[c, 2 * j] + dk_parts[c, 2 * j + 1]
            dg = _dot_mask(upper, dgc_tiles[c]) - later_tiles[c]
            g, _, gate_pre = _gate_tiles(cab_ref[rs, :], alog, dtb_ref[...])
            dca = dg * (-jnp.exp(alog)) * jax.nn.sigmoid(gate_pre)
            beta = betas[c]
            dcab_ref[rs, :LANES] = (dca + pltpu.roll(db_tiles[c] * beta * (1.0 - beta), C_V_HEADS, 1)).astype(BF16)
            dcab_ref[rs, LANES:] = jnp.zeros((CHUNK, D_IN_PAD - COL["c_ab"] - LANES), BF16)
            par = par + jnp.where(row8 == 0, jnp.sum(dg * g, axis=0, keepdims=True),
                                  jnp.where(row8 == 1, jnp.sum(dca, axis=0, keepdims=True), 0.0))

        @pl.when(pl.program_id(0) == 0)
        def _():
            dpar_ref[...] = par

        @pl.when(pl.program_id(0) > 0)
        def _():
            dpar_ref[...] += par

    def blk(wd):
        return pl.BlockSpec((rows, wd), lambda n: (n, 0))

    row = pl.BlockSpec((1, LANES), lambda n: (0, 0))
    sqs = blk(C_V_HEADS * CHUNK)
    tail = D_IN_PAD - COL["c_ab"]
    assert COL["c_ab"] % tail == 0 and nc % PREP_SUB == 0
    return pl.pallas_call(
        body, name="delta_prep_bwd", grid=(nc // PREP_SUB,),
        in_specs=[blk(C_QKW), pl.BlockSpec((rows, C_QKW), lambda n: (n, 1)), blk(C_VW),
                  pl.BlockSpec((rows, LANES), lambda n: (n, COL["c_ab"] // LANES)), row, row, sqs,
                  blk(C_VW), blk(C_VW),
                  blk(LANES), blk(LANES), blk(C_VW), blk(C_VW), blk(C_VW), blk(C_VW), sqs,
                  pl.BlockSpec((PREP_SUB * 8, LANES), lambda n: (n, 0))] + extra_specs,
        out_specs=[pl.BlockSpec((rows, tail), lambda n: (n, COL["c_ab"] // tail)),
                   blk(2 * C_QKW), blk(C_VW), pl.BlockSpec((8, LANES), lambda n: (0, 0))],
        out_shape=[_du_shape(t), jax.ShapeDtypeStruct((t, 2 * C_QKW), F32),
                   jax.ShapeDtypeStruct((t, C_VW), F32), jax.ShapeDtypeStruct((8, LANES), F32)],
        input_output_aliases=aliases,
        compiler_params=_cparams(("arbitrary",)),
    )(qk, qk, v, proj, alog, dtb, tmat, u, w, gc, beta, du, dw, dqd, dkd, dattn, dgl, *extra)


def _z_spec(tq):
    return pl.BlockSpec((tq, C_VW), lambda i: (i, COL["c_z"] // C_VW))


def _gated_norm_fwd_call(o, u, gain, *, tq=512):
    t, w = o.shape

    def body(o_ref, z_ref, g_ref, y_ref):
        act, _ = _silu_parts(z_ref[...])
        gv = g_ref[...]
        for h in range(C_V_HEADS):
            hs = slice(h * C_DK, (h + 1) * C_DK)
            ov = o_ref[:, hs]
            r = lax.rsqrt(jnp.mean(ov * ov, axis=1, keepdims=True) + EPS)
            y_ref[:, hs] = ov * r * gv * act[:, hs]

    blk = pl.BlockSpec((tq, w), lambda i: (i, 0))
    return pl.pallas_call(
        body, name="gated_norm_fwd", grid=(t // tq,),
        in_specs=[blk, _z_spec(tq), pl.BlockSpec((1, C_DK), lambda i: (0, 0))], out_specs=blk,
        out_shape=jax.ShapeDtypeStruct((t, w), F32),
        compiler_params=_cparams(("parallel",)),
    )(o, u, gain)


def _gated_norm_bwd_call(o, u, gain, dy, du_buf, *, tq=512):
    t, w = o.shape
    nt = t // tq
    extra, extra_specs, aliases = _du_operands(du_buf, 4)

    def body(o_ref, z_ref, g_ref, dy_ref, *refs):
        dz_ref, do_ref, dg_ref, acc_ref = refs[len(extra):]
        i = pl.program_id(0)
        act, dact = _silu_parts(z_ref[...])
        gv = g_ref[...]
        part = jnp.zeros((8, C_DK), F32)
        for h in range(C_V_HEADS):
            hs = slice(h * C_DK, (h + 1) * C_DK)
            ov = o_ref[:, hs]
            r = lax.rsqrt(jnp.mean(ov * ov, axis=1, keepdims=True) + EPS)
            xh = ov * r
            dyv = dy_ref[:, hs]
            dn = dyv * act[:, hs]
            dz_ref[:, hs] = (dyv * xh * gv * dact[:, hs]).astype(BF16)
            dxh = dn * gv
            do_ref[:, hs] = r * (dxh - xh * jnp.mean(dxh * xh, axis=1, keepdims=True))
            part = part + jnp.sum((dn * xh).reshape(tq // 8, 8, C_DK), axis=0)

        @pl.when(i == 0)
        def _():
            acc_ref[...] = part

        @pl.when(i > 0)
        def _():
            acc_ref[...] += part

        @pl.when(i == nt - 1)
        def _():
            dg_ref[...] = jnp.sum(acc_ref[...], axis=0, keepdims=True)

    blk = pl.BlockSpec((tq, w), lambda i: (i, 0))
    grow = pl.BlockSpec((1, C_DK), lambda i: (0, 0))
    return pl.pallas_call(
        body, name="gated_norm_bwd", grid=(nt,),
        in_specs=[blk, _z_spec(tq), grow, blk] + extra_specs, out_specs=[_z_spec(tq), blk, grow],
        out_shape=[_du_shape(t), jax.ShapeDtypeStruct((t, w), F32), jax.ShapeDtypeStruct((1, C_DK), F32)],
        scratch_shapes=[pltpu.VMEM((8, C_DK), F32)],
        input_output_aliases=aliases,
        compiler_params=_cparams(("arbitrary",)),
    )(o, u, gain, dy, *extra)


def _gate_specs(tq):
    return [pl.BlockSpec((tq, D_MODEL), lambda i, j=j: (i, j)) for j in range(3)]


def _merge_fwd_call(ps, u, *, tq=512):
    t, w = ps[0].shape

    def body(p0, p1, p2, g0, g1, g2, y_ref):
        y_ref[...] = (jax.nn.sigmoid(g0[...]) * p0[...] + jax.nn.sigmoid(g1[...]) * p1[...]
                      + jax.nn.sigmoid(g2[...]) * p2[...]).astype(BF16)

    blk = pl.BlockSpec((tq, w), lambda i: (i, 0))
    return pl.pallas_call(
        body, name="merge_fwd", grid=(t // tq,), in_specs=[blk] * 3 + _gate_specs(tq), out_specs=blk,
        out_shape=jax.ShapeDtypeStruct((t, w), BF16),
        compiler_params=_cparams(("parallel",)),
    )(*ps, u, u, u)


def _merge_bwd_call(ps, u, dy, *, tq=256):
    t, w = dy.shape

    def body(p0, p1, p2, g0, g1, g2, dy_ref, dg_ref, dp0, dp1, dp2):
        dyv = dy_ref[...]
        for j, (p, g, dp) in enumerate(((p0, g0, dp0), (p1, g1, dp1), (p2, g2, dp2))):
            sig = jax.nn.sigmoid(g[...])
            dp[...] = (dyv * sig).astype(BF16)
            dg_ref[:, j * w:(j + 1) * w] = (dyv * p[...] * sig * (1.0 - sig)).astype(BF16)

    blk = pl.BlockSpec((tq, w), lambda i: (i, 0))
    small = jax.ShapeDtypeStruct((t, w), BF16)
    return pl.pallas_call(
        body, name="merge_bwd", grid=(t // tq,), in_specs=[blk] * 3 + _gate_specs(tq) + [blk],
        out_specs=[pl.BlockSpec((tq, 3 * w), lambda i: (i, 0))] + [blk] * 3,
        out_shape=[_du_shape(t)] + [small] * 3,
        compiler_params=_cparams(("parallel",)),
    )(*ps, u, u, u, dy)


Q_SCALE = HEAD_DIM ** -0.5
A_PARTS = ((COL["a_q"], A_W, True, Q_SCALE), (COL["a_k"], A_W, True, 1.0), (COL["a_v"], A_W, False, 1.0))
B_PARTS = ((COL["b_q"], A_W, True, Q_SCALE), (COL["b_k"], B_KVW, True, 1.0), (COL["b_v"], B_KVW, False, 1.0))
BRANCHES = ("w_branch_a", "w_branch_b", "w_branch_c")


def _layer_fwd(x, tabs, p, w_in_b, rest, rest_of, gather_src=None):
    h = _rms_fwd_call(x, p["norm_mix"], name="rms_mix_fwd", out_dtype=BF16)
    if rest[0] == "ride":
        u, (packed,) = _mm(h, w_in_b, bias=p["b_in"], tn=IN_TN, gather_src=[rest[1]], name="in_proj_fwd_gather")
    else:
        u, packed = _mm(h, w_in_b, bias=p["b_in"], tn=IN_TN, name="in_proj_fwd"), rest[1]
    wb, conv_w = rest_of(packed)
    wb = dict(wb, w_in=w_in_b)
    p = dict(p, conv_w=conv_w)
    qkv_a = _rope_gather_call(u, tabs, A_PARTS, dils=A_DILS, name="rope_a_fwd")
    os_, lses = zip(*[_attn_fwd(cfg, qkv2, None) for cfg, qkv2 in zip(ATTN_A_CFGS, qkv_a)])
    ya = _combine_fwd_call(os_, lses, A_DILS)
    qkv_b, = _rope_gather_call(u, tabs, B_PARTS, name="rope_b_fwd")
    yb, lse_b = _attn_fwd(ATTN_B_CFG, qkv_b, p["sinks"])
    zc, qk, v = _conv_prep_fwd_call(u, p["conv_w"])
    uu, ww, qd, kd, attn, tmat, gc, beta, *gathered = _delta_prep_call(qk, v, u, p["a_log"], p["dt_bias"], gather_src)
    o, vn, st = _delta_scan_call(uu, ww, qd, kd, attn, gc)
    yc = _gated_norm_fwd_call(o, u, p["c_norm"])
    ys = (ya, yb, yc)
    ps = tuple(_mm(y, wb[n], name="branch_fwd") for y, n in zip(ys, BRANCHES))
    merged = _merge_fwd_call(ps, u)
    x1 = _mm(merged, wb["w_out"], add=x, name="out_proj_fwd")
    h2 = _rms_fwd_call(x1, p["norm_ffn"], name="rms_ffn_fwd", out_dtype=BF16)
    pre, act = _mm(h2, wb["w_ff1"], relu2_out=True, name="ffn_up")
    x2 = _mm(act, wb["w_ff2"], add=x1, name="ffn_down")
    saved = dict(x=x, h=h, u=u, qkv_a=qkv_a, os_=os_, lses=lses, b_saved=(qkv_b, yb, lse_b),
                 zc=zc, qk=qk, v=v, delta=(tmat, uu, ww, gc, beta, qd, kd, attn, vn, st), o=o, ys=ys, ps=ps,
                 merged=merged, x1=x1, h2=h2, pre=pre, act=act, p=p, wb=wb)
    return x2, saved, (gathered if gathered else None)


def _layer_bwd(s, dx2, tabs):
    g, p, wb = {}, s["p"], s["wb"]
    t = dx2.shape[0]
    dpre = _mm(dx2, wb["w_ff2"], tb=True, mul_drelu2=s["pre"], out_dtype=BF16, name="ffn_dpre")
    g["w_ff2"] = _mm(s["act"], dx2, ta=True, tk=1024, name="ffn_dw2")
    g["w_ff1"] = _mm(s["h2"], dpre, ta=True, tk=1024, name="ffn_dw1")
    dh2 = _mm(dpre, wb["w_ff1"], tb=True, name="ffn_dh")
    dx1, g["norm_ffn"] = _rms_bwd_call(s["x1"], p["norm_ffn"], dh2, add=dx2, name="rms_ffn_bwd")
    dmerged = _mm(dx1, wb["w_out"], tb=True, name="out_proj_da")
    g["w_out"] = _mm(s["merged"], dx1, ta=True, tk=1024, name="out_proj_dw")
    du, *dps = _merge_bwd_call(s["ps"], s["u"], dmerged)
    dys = []
    for y, dp, n in zip(s["ys"], dps, BRANCHES):
        dys.append(_mm(dp, wb[n], tb=True, name="branch_da"))
        g[n] = _mm(y, dp, ta=True, tk=1024, name="branch_dw")
    dya, dyb, dyc = dys
    tmat, uu, ww, gc, beta, qd, kd, attn, vn, st = s["delta"]
    du, do, g["c_norm"] = _gated_norm_bwd_call(s["o"], s["u"], p["c_norm"], dyc, du)
    ddu, ddw, dqd, dkd, dattn, dgl = _delta_scan_bwd_call(do, ww, qd, kd, attn, gc, vn, st)
    du, dqk, dv, dpar = _delta_prep_bwd_call(s["qk"], s["v"], s["u"], p["a_log"], p["dt_bias"], tmat, uu, ww, gc,
                                             beta, ddu, ddw, dqd, dkd, dattn, dgl, du)
    g["a_log"], g["dt_bias"] = dpar[0:1], dpar[1:2]
    dzc = _conv_prep_dz_call(s["zc"], dqk, dv)
    du, dconv = _conv_bwd_call(s["u"], dzc, p["conv_w"], du)
    g["conv_w"] = dconv[:C_CONV]
    no_dlse = jnp.zeros((t, LANES), F32)
    dq, dk, dv_b, dsink = _attn_bwd(ATTN_B_CFG, *s["b_saved"], p["sinks"], dyb, no_dlse)
    g["sinks"] = dsink[0, :p["sinks"].shape[0]]
    du = _rope_scatter_call(du, t, [([dq], A_W, True, Q_SCALE)], COL["b_q"], tabs, name="rope_bq_bwd")
    du = _rope_scatter_call(du, t, [([dk], B_KVW, True, 1.0), ([dv_b], B_KVW, False, 1.0)], COL["b_k"], tabs,
                            name="rope_bkv_bwd")
    *dos, dl0, dl1, dl2 = _combine_bwd_call(s["os_"], s["lses"], dya, A_DILS)
    grads_a = [_attn_bwd(cfg, qkv2, o2, lse2, None, do2, dl2_)[:3]
               for cfg, qkv2, o2, lse2, do2, dl2_ in zip(ATTN_A_CFGS, s["qkv_a"], s["os_"], s["lses"], dos,
                                                         (dl0, dl1, dl2))]
    dqs, dks, dvs = zip(*grads_a)
    du = _rope_scatter_call(du, t, [(list(dqs), A_W, True, Q_SCALE), (list(dks), A_W, True, 1.0),
                                    (list(dvs), A_W, False, 1.0)],
                            COL["a_q"], tabs, dils=A_DILS, name="rope_a_bwd")
    dh = _mm(du, wb["w_in"], tb=True, tk=IN_TN, name="in_proj_da")
    g["w_in"], g["b_in"] = _mm(s["h"], du, ta=True, b_colsum=True, tn=IN_TN, tk=1024, name="in_proj_dw")
    dx, g["norm_mix"] = _rms_bwd_call(s["x"], p["norm_mix"], dh, add=dx1, name="rms_mix_bwd")
    return dx, g


def _local_step(x, params, w_in_first, rest_first, payload_of_layer, w_in_of, rest_of, tabs, tgt):
    saves = []
    w_in_blocks, rest = w_in_first, rest_first
    for layer in range(DEPTH):
        p = {n: w[layer] for n, w in params.items() if n != "norm_final"}
        nxt = payload_of_layer(layer + 1) if layer + 1 < DEPTH else None
        x, s, gathered = _layer_fwd(x, tabs, p, w_in_of(w_in_blocks), rest, rest_of, nxt)
        saves.append(s)
        if gathered is not None:
            w_in_blocks, rest = gathered[0], ("ready", gathered[1])
    loss, dx, dfinal = _loss_call(x, params["norm_final"], tgt)
    per_layer = []
    for s in reversed(saves):
        dx, g = _layer_bwd(s, dx, tabs)
        per_layer.append(g)
    per_layer.reverse()
    grads = {n: jnp.stack([g[n] for g in per_layer]) for n in per_layer[0]}
    grads["norm_final"] = dfinal
    return loss, dx, grads


def _in_cols_to_kernel(w):
    lead = w.shape[:-1]
    parts, pos = [], 0
    for _, start, width, ref_start in IN_LAYOUT:
        if start > pos:
            parts.append(jnp.zeros(lead + (start - pos,), w.dtype))
        parts.append(w[..., ref_start:ref_start + width])
        pos = start + width
    parts.append(jnp.zeros(lead + (D_IN_PAD - pos,), w.dtype))
    return jnp.concatenate(parts, axis=-1)


def _in_cols_to_reference(w):
    by_ref = sorted(IN_LAYOUT, key=lambda e: e[3])
    return jnp.concatenate([w[..., start:start + width] for _, start, width, _ in by_ref], axis=-1)


W_IN_SHARD = 8464 // N_DEV


def _w_in_from_shards(blocks):
    lead = blocks.shape[1:-1]
    parts, pos = [], 0
    for _, start, width, ref_start in IN_LAYOUT:
        if start > pos:
            parts.append(jnp.zeros(lead + (start - pos,), blocks.dtype))
        col = ref_start
        while col < ref_start + width:
            d, l = divmod(col, W_IN_SHARD)
            n = min(W_IN_SHARD - l, ref_start + width - col)
            parts.append(blocks[d, ..., l:l + n])
            col += n
        pos = start + width
    parts.append(jnp.zeros(lead + (D_IN_PAD - pos,), blocks.dtype))
    return jnp.concatenate(parts, axis=-1)


def _w_in_to_shards(g):
    by_ref = sorted(IN_LAYOUT, key=lambda e: e[3])
    blocks = []
    for d in range(N_DEV):
        lo, hi = d * W_IN_SHARD, (d + 1) * W_IN_SHARD
        parts = []
        for _, start, width, ref_start in by_ref:
            a, b = max(lo, ref_start), min(hi, ref_start + width)
            if a < b:
                parts.append(g[..., start + a - ref_start:start + b - ref_start])
        blocks.append(jnp.concatenate(parts, axis=-1))
    return jnp.stack(blocks)


def _pad_lanes(v):
    return jnp.pad(v, ((0, 0), (0, LANES - v.shape[1])))[:, None, :]


BIG = (("w_in", 2), ("conv_w", 2), ("w_branch_a", 2), ("w_branch_b", 2), ("w_branch_c", 1), ("w_out", 1),
       ("w_ff1", 2), ("w_ff2", 1))
SMALL = ("norm_mix", "b_in", "a_log", "dt_bias", "sinks", "c_norm", "norm_ffn", "norm_final")
WEIGHTS = ("norm_mix", "w_in", "b_in", "conv_w", "a_log", "dt_bias", "sinks", "c_norm", "w_branch_a",
           "w_branch_b", "w_branch_c", "w_out", "norm_ffn", "w_ff1", "w_ff2", "norm_final")
MATMUL_WEIGHTS = ("w_in", "w_branch_a", "w_branch_b", "w_branch_c", "w_out", "w_ff1", "w_ff2")
PACK_ROWS = 1024
ROW_ALIGN = 16


def _seg_rows(n):
    return -(-n // (LANES * ROW_ALIGN)) * ROW_ALIGN


def _pack(arrays, lead=0):
    parts = []
    for a in arrays:
        lead_shape = a.shape[:lead]
        n = math.prod(a.shape[lead:])
        rows = _seg_rows(n)
        if rows * LANES != n:
            a = jnp.pad(a.reshape(lead_shape + (n,)), [(0, 0)] * lead + [(0, rows * LANES - n)])
        parts.append(a.reshape(lead_shape + (rows, LANES)))
    total = sum(p.shape[lead] for p in parts)
    padded = -(-total // PACK_ROWS) * PACK_ROWS
    if padded > total:
        parts.append(jnp.zeros(parts[0].shape[:lead] + (padded - total, LANES), parts[0].dtype))
    return jnp.concatenate(parts, axis=lead)


def _unpack(buf, shapes):
    lead = buf.shape[:-2]
    out, pos = [], 0
    for shp in shapes:
        n = math.prod(shp)
        rows = _seg_rows(n)
        seg = buf[..., pos:pos + rows, :]
        if rows * LANES != n:
            seg = seg.reshape(lead + (rows * LANES,))[..., :n]
        out.append(seg.reshape(lead + tuple(shp)))
        pos += rows
    return out


def _shards_to_full(blocks, axis):
    moved = jnp.moveaxis(blocks, 0, axis)
    shp = list(blocks.shape[1:])
    shp[axis] = shp[axis] * N_DEV
    return moved.reshape(shp)


def _full_to_shards(full, axis):
    shp = list(full.shape)
    shp[axis:axis + 1] = [N_DEV, shp[axis] // N_DEV]
    return jnp.moveaxis(full.reshape(shp), axis, 0)


def _my_place():
    return lax.axis_index("x"), lax.axis_index("y"), lax.axis_index("c")


def _slot(x, y, c):
    return 4 * x + 2 * y + c


GATHER_COPIES = 7


def _gather_plan(x_ref, out_ref, send_sems, recv_sems, local_sem, base):
    x, y, c = _my_place()
    me, sibling = (x, y, c), (x, y, 1 - c)
    chips = [(1 - x, y), (x, 1 - y), (1 - x, 1 - y)]

    def copy(k, blk, to, src=None):
        dst = out_ref.at[_slot(*blk)]
        return pltpu.make_async_remote_copy(
            src_ref=dst if src is None else src, dst_ref=dst,
            send_sem=send_sems.at[base + k], recv_sem=recv_sems.at[base + k], device_id=to, device_id_type=MESH_ID)

    def own():
        mine = pltpu.make_async_copy(x_ref, out_ref.at[_slot(*me)], local_sem)
        return mine, [copy(0, me, sibling, src=x_ref)] + [copy(1 + j, me, (*chip, c), src=x_ref)
                                                          for j, chip in enumerate(chips)]

    return copy, own, me, sibling, chips, c


def _gather_plans(srcs, outs, send_sems, recv_sems, local_sems):
    return [_gather_plan(x_ref, out_ref, send_sems, recv_sems, local_sems.at[i], GATHER_COPIES * i)
            for i, (x_ref, out_ref) in enumerate(zip(srcs, outs))]


def _gather_start(srcs, outs, *sems):
    for _, own, *_ in _gather_plans(srcs, outs, *sems):
        mine, first = own()
        mine.start()
        for cp in first:
            cp.start()


def _gather_finish(srcs, outs, *sems):
    plans = _gather_plans(srcs, outs, *sems)
    passed_all = []
    for copy, own, me, sibling, chips, c in plans:
        passed = [copy(4 + j, (*chip, c), sibling) for j, chip in enumerate(chips)]
        for j, chip in enumerate(chips):
            copy(1 + j, (*chip, c), me).wait_recv()
            passed[j].start()
        passed_all.append(passed)
    for (copy, own, me, sibling, chips, c), passed in zip(plans, passed_all):
        copy(0, sibling, me).wait_recv()
        for j, chip in enumerate(chips):
            copy(4 + j, (*chip, 1 - c), me).wait_recv()
        mine, first = own()
        for cp in first + passed:
            cp.wait_send()
        mine.wait()


def _gather_sems(n):
    return [pltpu.SemaphoreType.DMA((GATHER_COPIES * n,)), pltpu.SemaphoreType.DMA((GATHER_COPIES * n,)),
            pltpu.SemaphoreType.DMA((n,))]


def _gathered_shapes(blocks):
    return [jax.ShapeDtypeStruct((N_DEV,) + b.shape, b.dtype) for b in blocks]


def _all_gather(blocks, *, name):
    n = len(blocks)

    def body(*refs):
        srcs, outs, sems = refs[:n], refs[n:2 * n], refs[2 * n:]
        _gather_start(srcs, outs, *sems)
        _gather_finish(srcs, outs, *sems)

    return pl.pallas_call(
        body, name=name, out_shape=_gathered_shapes(blocks),
        in_specs=[HBM_SPEC] * n, out_specs=[HBM_SPEC] * n,
        scratch_shapes=_gather_sems(n),
    )(*blocks)


N_CHIP = N_DEV // 2


def _swap_with_sibling(blocks, *, name):
    n = len(blocks)

    def body(*refs):
        srcs, outs, send_sems, recv_sems = refs[:n], refs[n:2 * n], refs[2 * n], refs[2 * n + 1]
        x, y, c = _my_place()
        copies = [pltpu.make_async_remote_copy(src_ref=g_ref, dst_ref=out_ref, send_sem=send_sems.at[i],
                                               recv_sem=recv_sems.at[i], device_id=(x, y, 1 - c),
                                               device_id_type=MESH_ID)
                  for i, (g_ref, out_ref) in enumerate(zip(srcs, outs))]
        for cp in copies:
            cp.start()
        for cp in copies:
            cp.wait_recv()
        for cp in copies:
            cp.wait_send()

    return pl.pallas_call(
        body, name=name,
        out_shape=[jax.ShapeDtypeStruct(b.shape, b.dtype) for b in blocks],
        in_specs=[HBM_SPEC] * n, out_specs=[HBM_SPEC] * n,
        scratch_shapes=[pltpu.SemaphoreType.DMA((n,)), pltpu.SemaphoreType.DMA((n,))],
    )(*blocks)


def _chip_all_to_all(blocks, *, name):
    n = len(blocks)
    peers = N_CHIP - 1

    def body(*refs):
        srcs, outs = refs[:n], refs[n:2 * n]
        send_sems, recv_sems, local_sems = refs[2 * n:]
        x, y, c = _my_place()
        mine_slot = 2 * x + y
        locals_, copies = [], []
        for i, (g_ref, out_ref) in enumerate(zip(srcs, outs)):
            locals_.append(pltpu.make_async_copy(g_ref.at[mine_slot], out_ref.at[mine_slot], local_sems.at[i]))
            for k in range(1, N_CHIP):
                px, py = x ^ (k >> 1), y ^ (k & 1)
                copies.append(pltpu.make_async_remote_copy(
                    src_ref=g_ref.at[2 * px + py], dst_ref=out_ref.at[mine_slot],
                    send_sem=send_sems.at[peers * i + k - 1], recv_sem=recv_sems.at[peers * i + k - 1],
                    device_id=(px, py, c), device_id_type=MESH_ID))
        for cp in locals_ + copies:
            cp.start()
        for cp in copies:
            cp.wait_recv()
        for cp in copies:
            cp.wait_send()
        for cp in locals_:
            cp.wait()

    return pl.pallas_call(
        body, name=name,
        out_shape=[jax.ShapeDtypeStruct(b.shape, b.dtype) for b in blocks],
        in_specs=[HBM_SPEC] * n, out_specs=[HBM_SPEC] * n,
        scratch_shapes=[pltpu.SemaphoreType.DMA((peers * n,)), pltpu.SemaphoreType.DMA((peers * n,)),
                        pltpu.SemaphoreType.DMA((n,))],
    )(*blocks)


def _block_rows(rows, cols):
    tr = max(8, min(rows, PACK_ROWS * LANES // (-(-cols // LANES) * LANES) // 8 * 8))
    while rows % tr:
        tr -= 8
    return tr


def _add_bf16_call(a, b, *, name):
    n, rows, cols = a.shape
    tr = _block_rows(rows, cols)

    def body(a_ref, b_ref, o_ref):
        o_ref[...] = (a_ref[...].astype(F32) + b_ref[...].astype(F32)).astype(BF16)

    blk = pl.BlockSpec((n, tr, cols), lambda i: (0, i, 0))
    return pl.pallas_call(
        body, name=name, grid=(rows // tr,), in_specs=[blk, blk], out_specs=blk,
        out_shape=jax.ShapeDtypeStruct(a.shape, BF16),
        compiler_params=_cparams(("parallel",)),
    )(a, b)


def _adamw_call(parts, w, m, v, *, name):
    rows, cols = w.shape
    tr = _block_rows(rows, cols)
    n_parts = parts.shape[0]

    def body(p_ref, w_ref, m_ref, v_ref, g_ref, d_ref, nm_ref, nv_ref):
        g = p_ref[0].astype(F32)
        for s in range(1, n_parts):
            g = g + p_ref[s].astype(F32)
        nm = ADAM_B1 * m_ref[...] + (1.0 - ADAM_B1) * g
        nv = ADAM_B2 * v_ref[...] + (1.0 - ADAM_B2) * jnp.square(g)
        m_hat = nm / (1.0 - ADAM_B1 ** ADAM_STEP)
        v_hat = nv / (1.0 - ADAM_B2 ** ADAM_STEP)
        g_ref[...] = g
        nm_ref[...] = nm
        nv_ref[...] = nv
        d_ref[...] = -ADAM_LR * (m_hat / (jnp.sqrt(v_hat) + ADAM_EPS) + ADAM_WD * w_ref[...])

    blk = pl.BlockSpec((tr, cols), lambda i: (i, 0))
    shape = jax.ShapeDtypeStruct((rows, cols), F32)
    return pl.pallas_call(
        body, name=name, grid=(rows // tr,),
        in_specs=[pl.BlockSpec((n_parts, tr, cols), lambda i: (0, i, 0)), blk, blk, blk],
        out_specs=[blk] * 4, out_shape=[shape] * 4,
        compiler_params=_cparams(("parallel",)),
    )(parts, w, m, v)


def _kernel_params(full):
    return {
        "norm_mix": full["norm_mix"][:, None, :],
        "b_in": _in_cols_to_kernel(full["b_in"])[:, None, :],
        "a_log": _pad_lanes(full["a_log"]),
        "dt_bias": _pad_lanes(full["dt_bias"]),
        "sinks": full["sinks"],
        "c_norm": full["c_norm"][:, None, :],
        "norm_ffn": full["norm_ffn"][:, None, :],
        "norm_final": full["norm_final"][None, :],
    }


def _reference_grads(g):
    return {
        "norm_mix": g["norm_mix"][:, 0, :],
        "b_in": _in_cols_to_reference(g["b_in"][:, 0, :]),
        "conv_w": g["conv_w"],
        "a_log": g["a_log"][:, 0, :C_V_HEADS],
        "dt_bias": g["dt_bias"][:, 0, :C_V_HEADS],
        "sinks": g["sinks"],
        "c_norm": g["c_norm"][:, 0, :],
        "w_branch_a": g["w_branch_a"], "w_branch_b": g["w_branch_b"], "w_branch_c": g["w_branch_c"],
        "w_out": g["w_out"],
        "norm_ffn": g["norm_ffn"][:, 0, :],
        "w_ff1": g["w_ff1"], "w_ff2": g["w_ff2"],
        "norm_final": g["norm_final"][0],
    }


def kernel(x, positions, norm_mix, w_in, b_in, conv_w, a_log, dt_bias, sinks, c_norm, w_branch_a, w_branch_b, w_branch_c, w_out, norm_ffn, w_ff1, w_ff2, norm_final, loss_target, m_norm_mix, m_w_in, m_b_in, m_conv_w, m_a_log, m_dt_bias, m_sinks, m_c_norm, m_w_branch_a, m_w_branch_b, m_w_branch_c, m_w_out, m_norm_ffn, m_w_ff1, m_w_ff2, m_norm_final, v_norm_mix, v_w_in, v_b_in, v_conv_w, v_a_log, v_dt_bias, v_sinks, v_c_norm, v_w_branch_a, v_w_branch_b, v_w_branch_c, v_w_out, v_norm_ffn, v_w_ff1, v_w_ff2, v_norm_final):
    env = dict(locals())
    weights = {n: env[n] for n in WEIGHTS}
    moments_m = {n: env["m_" + n] for n in WEIGHTS}
    moments_v = {n: env["v_" + n] for n in WEIGHTS}

    axis_of = {n: axis - 1 for n, axis in BIG}

    packed_names = [n for n in MATMUL_WEIGHTS if n != "w_in"]

    def payload_of_layer(layer):
        cw = weights["conv_w"][layer]
        c1 = cw.astype(BF16)
        c2 = (cw - c1.astype(F32)).astype(BF16)
        c3 = (cw - c1.astype(F32) - c2.astype(F32)).astype(BF16)
        return [weights["w_in"][layer].astype(BF16),
                _pack([weights[n][layer].astype(BF16) for n in packed_names] + [c1, c2, c3])]

    def rest_of(packed):
        shapes = [weights[n].shape[1:] for n in packed_names] + [weights["conv_w"].shape[1:]] * 3
        blocks = _unpack(packed, shapes)
        wb = {n: _shards_to_full(blk, axis_of[n]) for n, blk in zip(packed_names, blocks)}
        return wb, _shards_to_full(sum(b.astype(F32) for b in blocks[-3:]), axis_of["conv_w"])

    tabs = rope_tables(positions[0])
    w_in_first, = _all_gather(payload_of_layer(0)[:1], name="gather_weights")
    loss, dx, dparams = _local_step(x[0], _kernel_params({n: weights[n] for n in SMALL}), w_in_first,
                                    ("ride", payload_of_layer(0)[1]), payload_of_layer, _w_in_from_shards,
                                    rest_of, tabs, loss_target[0])
    grads = _reference_grads(dparams)
    loss = lax.psum(loss, ("x", "y", "c"))

    core = lax.axis_index("c")
    rest = [(n, axis) for n, axis in BIG if n != "w_in"]
    w_in_rows = DEPTH * D_MODEL

    def by_core(shards, which):
        sh = shards.reshape((N_CHIP, 2) + shards.shape[1:])
        return lax.dynamic_index_in_dim(sh, which, axis=1, keepdims=False).astype(BF16)

    def halves(which):
        w_in_half = by_core(_w_in_to_shards(dparams["w_in"]), which).reshape(N_CHIP, w_in_rows, W_IN_SHARD)
        return [w_in_half, _pack([by_core(_full_to_shards(grads[n], axis), which) for n, axis in rest], lead=1)]

    from_sibling = _swap_with_sibling(halves(1 - core), name="scatter_grads_d2d")
    chip_sums = [_add_bf16_call(keep, got, name="scatter_grads_add")
                 for keep, got in zip(halves(core), from_sibling)]
    w_in_parts, rest_parts = _chip_all_to_all(chip_sums, name="scatter_grads_ici")
    small_parts, = _all_gather([_pack([grads[n] for n in SMALL])], name="gather_small_grads")

    out = {}
    results = _adamw_call(w_in_parts, *[d["w_in"].reshape(w_in_rows, W_IN_SHARD)
                                        for d in (weights, moments_m, moments_v)], name="adamw_w_in")
    for kind, buf in zip(("grad", "delta", "new_m", "new_v"), results):
        out[kind, "w_in"] = buf.reshape(weights["w_in"].shape)
    for names, parts in (([n for n, _ in rest], rest_parts), (list(SMALL), small_parts)):
        shapes = [weights[n].shape for n in names]
        packed = [_pack([d[n] for n in names]) for d in (weights, moments_m, moments_v)]
        results = _adamw_call(parts, *packed, name="adamw_" + names[0])
        for kind, buf in zip(("grad", "delta", "new_m", "new_v"), results):
            for n, arr in zip(names, _unpack(buf, shapes)):
                out[kind, n] = arr
    return (loss, dx[None], *[out[kind, n] for kind in ("grad", "delta", "new_m", "new_v") for n in WEIGHTS])
```

```python
import functools
import math

import jax
import jax.numpy as jnp
from jax import lax
from jax.experimental import pallas as pl
from jax.experimental.pallas import tpu as pltpu

F32 = jnp.float32
BF16 = jnp.bfloat16

N_DEV = 8
D_MODEL = 1024
DEPTH = 2
HEAD_DIM = 64
ROT_DIM = 16
ROPE_THETA = 500000.0
BLK = 128
NEG_INF = -1e30
EPS = 1e-6
A_CONFIGS = ((128, 1), (512, 4), (2048, 16))
B_GROUP = 4
C_QK_HEADS = 4
C_V_HEADS = 8
C_DK = 128
C_CONV = 4
CHUNK = 64
ADAM_LR = 0.001
ADAM_B1 = 0.9
ADAM_B2 = 0.999
ADAM_EPS = 1e-08
ADAM_WD = 0.01
ADAM_STEP = 10

IN_LAYOUT = (
    ("gate_a", 0, 1024, 5392), ("gate_b", 1024, 1024, 6416), ("gate_c", 2048, 1024, 7440),
    ("a_q", 3072, 512, 0), ("a_k", 3584, 512, 512), ("a_v", 4096, 512, 1024), ("b_q", 4608, 512, 1536),
    ("c_z", 5120, 1024, 4352), ("c_qkv", 6144, 2048, 2304),
    ("b_k", 8192, 128, 2048), ("b_v", 8320, 128, 2176), ("c_ab", 8448, 16, 5376),
)
COL = {name: start for name, start, _, _ in IN_LAYOUT}
D_IN_PAD = 8704
IN_TN = D_IN_PAD // 4
LANES = 128
VMEM_LIMIT = 56 * 1024 * 1024


def _cparams(sem=None):
    return pltpu.CompilerParams(dimension_semantics=sem, vmem_limit_bytes=VMEM_LIMIT)


def _relu2(t):
    return jnp.square(jnp.maximum(t, 0.0))


def _mm(a, b, *, ta=False, tb=False, bias=None, a_fn=None, mul_drelu2=None, add=None,
        out_dtype=F32, relu2_out=False, b_colsum=False, gather_src=None, tm=1024, tn=1024, tk=2048, name):
    if ta:
        kdim, m = a.shape
    else:
        m, kdim = a.shape
    n = b.shape[0] if tb else b.shape[1]
    tm, tn, tk = min(tm, m), min(tn, n), min(tk, kdim)
    assert m % tm == 0 and n % tn == 0 and kdim % tk == 0, (a.shape, b.shape, tm, tn, tk)
    nk = kdim // tk
    assert not b_colsum or (m == tm and not tb and nk > 1)
    dims = (((0 if ta else 1,), (1 if tb else 0,)), ((), ()))
    extras = [e for e in (bias, mul_drelu2, add) if e is not None]
    ng = len(gather_src) if gather_src is not None else 0
    grid = (m // tm, n // tn, nk)

    def body(*refs):
        if ng:
            n_in = 2 + len(extras)
            n_out = 1 + int(relu2_out) + int(b_colsum)
            n_scr = int(nk > 1) + int(b_colsum)
            gather_refs = (refs[n_in:n_in + ng], refs[n_in + ng + n_out:n_in + 2 * ng + n_out],
                           *refs[n_in + 2 * ng + n_out + n_scr:])
            refs = refs[:n_in] + refs[n_in + ng:n_in + ng + n_out] + refs[n_in + 2 * ng + n_out:]
            at_first = functools.reduce(jnp.logical_and, [pl.program_id(d) == 0 for d in range(3)])
            at_last = functools.reduce(jnp.logical_and, [pl.program_id(d) == grid[d] - 1 for d in range(3)])

            @pl.when(at_first)
            def _():
                _gather_start(*gather_refs)
        a_ref, b_ref = refs[0], refs[1]
        pos = 2
        bias_ref = pre_ref = add_ref = None
        if bias is not None:
            bias_ref = refs[pos]; pos += 1
        if mul_drelu2 is not None:
            pre_ref = refs[pos]; pos += 1
        if add is not None:
            add_ref = refs[pos]; pos += 1
        o_ref = refs[pos]
        pos += 1
        r_ref = None
        if relu2_out:
            r_ref = refs[pos]; pos += 1
        cs_ref = None
        if b_colsum:
            cs_ref = refs[pos]; pos += 1
        acc_ref = refs[pos] if nk > 1 else None
        cs_acc = refs[pos + 1] if b_colsum else None

        av = a_ref[...]
        if a_fn is not None:
            av = a_fn(av)
        bv = b_ref[...]
        part = lax.dot_general(av.astype(BF16), bv.astype(BF16), dims,
                               preferred_element_type=F32)
        if b_colsum:
            cs_part = jnp.sum(bv.astype(F32).reshape(tk // 8, 8, tn), axis=0)

        def finish(acc):
            if bias_ref is not None:
                acc = acc + bias_ref[...]
            if pre_ref is not None:
                acc = acc * (2.0 * jnp.maximum(pre_ref[...], 0.0))
            if add_ref is not None:
                acc = acc + add_ref[...]
            o_ref[...] = acc.astype(out_dtype)
            if r_ref is not None:
                r_ref[...] = _relu2(acc).astype(BF16)

        if nk == 1:
            finish(part)
        else:
            k = pl.program_id(2)

            @pl.when(k == 0)
            def _():
                acc_ref[...] = part
                if b_colsum:
                    cs_acc[...] = cs_part

            @pl.when(k > 0)
            def _():
                acc_ref[...] += part
                if b_colsum:
                    cs_acc[...] += cs_part

            @pl.when(k == nk - 1)
            def _():
                finish(acc_ref[...])
                if b_colsum:
                    cs_ref[...] = jnp.sum(cs_acc[...], axis=0, keepdims=True)
        if ng:
            @pl.when(at_last)
            def _():
                _gather_finish(*gather_refs)

    a_spec = (pl.BlockSpec((tk, tm), lambda i, j, k: (k, i)) if ta
              else pl.BlockSpec((tm, tk), lambda i, j, k: (i, k)))
    b_spec = (pl.BlockSpec((tn, tk), lambda i, j, k: (j, k)) if tb
              else pl.BlockSpec((tk, tn), lambda i, j, k: (k, j)))
    in_specs = [a_spec, b_spec]
    if bias is not None:
        in_specs.append(pl.BlockSpec((1, tn), lambda i, j, k: (0, j)))
    for _ in extras[(1 if bias is not None else 0):]:
        in_specs.append(pl.BlockSpec((tm, tn), lambda i, j, k: (i, j)))
    o_spec = pl.BlockSpec((tm, tn), lambda i, j, k: (i, j))
    out_specs, out_shape = [o_spec], [jax.ShapeDtypeStruct((m, n), out_dtype)]
    scratch = [pltpu.VMEM((tm, tn), F32)] if nk > 1 else []
    if relu2_out:
        out_specs.append(o_spec)
        out_shape.append(jax.ShapeDtypeStruct((m, n), BF16))
    if b_colsum:
        out_specs.append(pl.BlockSpec((1, tn), lambda i, j, k: (0, j)))
        out_shape.append(jax.ShapeDtypeStruct((1, n), F32))
        scratch.append(pltpu.VMEM((8, tn), F32))
    if ng:
        in_specs += [HBM_SPEC] * ng
        out_specs += [HBM_SPEC] * ng
        out_shape += _gathered_shapes(gather_src)
        scratch += _gather_sems(ng)
    single = len(out_specs) == 1
    outs = pl.pallas_call(
        body, name=name,
        grid=grid,
        in_specs=in_specs,
        out_specs=out_specs[0] if single else out_specs,
        out_shape=out_shape[0] if single else out_shape,
        scratch_shapes=scratch,
        compiler_params=_cparams(("arbitrary",) * 3 if ng else ("parallel", "parallel", "arbitrary")),
    )(a, b, *extras, *(gather_src or []))
    if not ng:
        return outs
    n_out = len(outs) - ng
    return (outs[0] if n_out == 1 else tuple(outs[:n_out])), list(outs[n_out:])


def _rms_fwd_call(x, g, *, name, out_dtype=F32, tq=512):
    t, d = x.shape

    def body(x_ref, g_ref, y_ref):
        xv = x_ref[...]
        r = lax.rsqrt(jnp.mean(xv * xv, axis=-1, keepdims=True) + EPS)
        y_ref[...] = (xv * r * g_ref[...]).astype(out_dtype)

    return pl.pallas_call(
        body, name=name, grid=(t // tq,),
        in_specs=[pl.BlockSpec((tq, d), lambda i: (i, 0)), pl.BlockSpec((1, d), lambda i: (0, 0))],
        out_specs=pl.BlockSpec((tq, d), lambda i: (i, 0)),
        out_shape=jax.ShapeDtypeStruct((t, d), out_dtype),
        compiler_params=_cparams(("parallel",)),
    )(x, g)


def _rms_bwd_call(x, g, dy, *, name, add=None, tq=512):
    t, d = x.shape
    nt = t // tq

    def body(*refs):
        if add is None:
            x_ref, g_ref, dy_ref, dx_ref, dg_ref, acc_ref = refs
        else:
            x_ref, g_ref, dy_ref, add_ref, dx_ref, dg_ref, acc_ref = refs
        i = pl.program_id(0)
        xv = x_ref[...]
        r = lax.rsqrt(jnp.mean(xv * xv, axis=-1, keepdims=True) + EPS)
        xh = xv * r
        dyv = dy_ref[...]
        dxh = dyv * g_ref[...]
        dx = r * (dxh - xh * jnp.mean(dxh * xh, axis=-1, keepdims=True))
        dx_ref[...] = dx if add is None else dx + add_ref[...]
        part = jnp.sum((dyv * xh).reshape(tq // 8, 8, d), axis=0)

        @pl.when(i == 0)
        def _():
            acc_ref[...] = part

        @pl.when(i > 0)
        def _():
            acc_ref[...] += part

        @pl.when(i == nt - 1)
        def _():
            dg_ref[...] = jnp.sum(acc_ref[...], axis=0, keepdims=True)

    blk = pl.BlockSpec((tq, d), lambda i: (i, 0))
    row = pl.BlockSpec((1, d), lambda i: (0, 0))
    extra = [] if add is None else [add]
    return pl.pallas_call(
        body, name=name, grid=(nt,),
        in_specs=[blk, row, blk] + [blk] * len(extra),
        out_specs=[blk, row],
        out_shape=[jax.ShapeDtypeStruct((t, d), F32), jax.ShapeDtypeStruct((1, d), F32)],
        scratch_shapes=[pltpu.VMEM((8, d), F32)],
        compiler_params=_cparams(("arbitrary",)),
    )(x, g, dy, *extra)


def _loss_call(x, g, tgt, *, tq=512):
    t, d = x.shape
    nt = t // tq

    def body(x_ref, g_ref, t_ref, loss_ref, dx_ref, dg_ref, acc_ref, sq_ref):
        i = pl.program_id(0)
        xv = x_ref[...]
        r = lax.rsqrt(jnp.mean(xv * xv, axis=-1, keepdims=True) + EPS)
        xh = xv * r
        gv = g_ref[...]
        err = xh * gv - t_ref[...]
        dyv = err * (1.0 / d)
        dxh = dyv * gv
        dx_ref[...] = r * (dxh - xh * jnp.mean(dxh * xh, axis=-1, keepdims=True))
        part = jnp.sum((dyv * xh).reshape(tq // 8, 8, d), axis=0)
        sq = jnp.sum((err * err).reshape(tq // 8, 8, d), axis=0)

        @pl.when(i == 0)
        def _():
            acc_ref[...] = part
            sq_ref[...] = sq

        @pl.when(i > 0)
        def _():
            acc_ref[...] += part
            sq_ref[...] += sq

        @pl.when(i == nt - 1)
        def _():
            dg_ref[...] = jnp.sum(acc_ref[...], axis=0, keepdims=True)
            tot = jnp.sum(jnp.sum(sq_ref[...], axis=0, keepdims=True), axis=1, keepdims=True)
            loss_ref[...] = jnp.broadcast_to(tot * (0.5 / d), (8, LANES))

    blk = pl.BlockSpec((tq, d), lambda i: (i, 0))
    row = pl.BlockSpec((1, d), lambda i: (0, 0))
    loss, dx, dg = pl.pallas_call(
        body, name="loss", grid=(nt,),
        in_specs=[blk, row, blk],
        out_specs=[pl.BlockSpec((8, LANES), lambda i: (0, 0)), blk, row],
        out_shape=[jax.ShapeDtypeStruct((8, LANES), F32), jax.ShapeDtypeStruct((t, d), F32),
                   jax.ShapeDtypeStruct((1, d), F32)],
        scratch_shapes=[pltpu.VMEM((8, d), F32), pltpu.VMEM((8, d), F32)],
        compiler_params=_cparams(("arbitrary",)),
    )(x, g, tgt)
    return loss[0, 0], dx, dg


MESH_ID = pl.DeviceIdType.MESH
HBM_SPEC = pl.BlockSpec(memory_space=pl.ANY)


def rope_tables(positions):
    half = ROT_DIM // 2
    inv_freq = jnp.power(ROPE_THETA, -jnp.arange(0, ROT_DIM, 2, dtype=F32) / ROT_DIM)
    in_head = jnp.arange(LANES) % HEAD_DIM
    rot = in_head < ROT_DIM
    freq = jnp.where(rot, inv_freq[in_head % half], 0.0)
    ang = positions.astype(F32)[:, None] * freq[None, :]
    cos, sin = jnp.cos(ang), jnp.sin(ang)
    b = jnp.where(jnp.logical_and(rot, in_head >= half)[None, :], sin, 0.0)
    c = jnp.where((in_head < half)[None, :], -sin, 0.0)
    return cos, b, c


def _rope_chunk(xs, a, b, c, transpose):
    half = ROT_DIM // 2
    if transpose:
        return xs * a + pltpu.roll(xs * b, LANES - half, 1) + pltpu.roll(xs * c, half, 1)
    return xs * a + pltpu.roll(xs, half, 1) * b + pltpu.roll(xs, LANES - half, 1) * c


def _dilated_spec(tq, w, d):
    return pl.BlockSpec((tq // d, d * w), lambda i: (i, 0))


def _load_dilated(ref, stage_ref, d, w, j):
    cs = slice(j * LANES, (j + 1) * LANES)
    if d == 1:
        return ref[:, cs].astype(F32)
    for r in range(d):
        stage_ref[pl.ds(r, ref.shape[0], stride=d), :] = ref[:, r * w + j * LANES:r * w + (j + 1) * LANES].astype(F32)
    return stage_ref[...]


def _store_dilated(ref, stage_ref, d, w, j, val):
    if d == 1:
        ref[:, j * LANES:(j + 1) * LANES] = val.astype(ref.dtype)
        return
    stage_ref[...] = val
    for r in range(d):
        rows = stage_ref[pl.ds(r, ref.shape[0], stride=d), :]
        ref[:, r * w + j * LANES:r * w + (j + 1) * LANES] = rows.astype(ref.dtype)


def _stage_buffers(tq, n):
    return [pltpu.VMEM((tq, LANES), F32)] * n


def _rope_gather_call(u, tabs, parts, *, name, dils=(1,), tq=512):
    t = u.shape[0]
    total = sum(w for _, w, _, _ in parts)
    assert all(start % w == 0 for start, w, _, _ in parts)
    n_stage = (total // LANES) * sum(d > 1 for d in dils)

    def body(a_ref, b_ref, c_ref, *refs):
        x_refs, o_refs = refs[:len(parts)], refs[len(parts):len(parts) + len(dils)]
        stages = iter(refs[len(parts) + len(dils):])
        a, b, c = a_ref[...], b_ref[...], c_ref[...]
        g = 0
        for x_ref, (_, w, roped, scale) in zip(x_refs, parts):
            for j in range(w // LANES):
                xs = x_ref[:, j * LANES:(j + 1) * LANES]
                val = _rope_chunk(xs, a, b, c, False) if roped else xs
                val = val * scale if scale != 1.0 else val
                for o_ref, d in zip(o_refs, dils):
                    _store_dilated(o_ref, next(stages) if d > 1 else None, d, total, g, val)
                g += 1

    tab_spec = pl.BlockSpec((tq, LANES), lambda i: (i, 0))
    return pl.pallas_call(
        body, name=name, grid=(t // tq,),
        in_specs=[tab_spec] * 3 + [pl.BlockSpec((tq, w), lambda i, cb=start // w: (i, cb)) for start, w, _, _ in parts],
        out_specs=[_dilated_spec(tq, total, d) for d in dils],
        out_shape=[jax.ShapeDtypeStruct((t // d, d * total), BF16) for d in dils],
        scratch_shapes=_stage_buffers(tq, n_stage),
        compiler_params=_cparams(("parallel",)),
    )(*tabs, *([u] * len(parts)))


def _du_operands(du_buf, n_inputs):
    if du_buf is None:
        return [], [], {}
    return [du_buf], [HBM_SPEC], {n_inputs: 0}


def _du_shape(t):
    return jax.ShapeDtypeStruct((t, D_IN_PAD), BF16)


def _rope_scatter_call(du_buf, t, pieces, col, tabs, *, name, dils=(1,), tq=512):
    total = sum(w for _, w, _, _ in pieces)
    assert col % total == 0 and all(len(arrs) == len(dils) for arrs, _, _, _ in pieces)
    arrays = [a for arrs, _, _, _ in pieces for a in arrs]
    extra, extra_specs, aliases = _du_operands(du_buf, 3 + len(arrays))
    n_stage = (total // LANES) * sum(d > 1 for d in dils)

    def body(a_ref, b_ref, c_ref, *refs):
        o_ref = refs[len(arrays) + len(extra)]
        stages = iter(refs[len(arrays) + len(extra) + 1:])
        a, b, c = a_ref[...], b_ref[...], c_ref[...]
        k = off = 0
        for arrs, w, roped, scale in pieces:
            mine = refs[k:k + len(arrs)]
            k += len(arrs)
            for j in range(w // LANES):
                xs = None
                for r, d in zip(mine, dils):
                    part = _load_dilated(r, next(stages) if d > 1 else None, d, w, j)
                    xs = part if xs is None else xs + part
                if scale != 1.0:
                    xs = xs * scale
                val = _rope_chunk(xs, a, b, c, True) if roped else xs
                o_ref[:, off + j * LANES:off + (j + 1) * LANES] = val.astype(BF16)
            off += w

    tab_spec = pl.BlockSpec((tq, LANES), lambda i: (i, 0))
    in_specs = [tab_spec] * 3 + [_dilated_spec(tq, w, d) for _, w, _, _ in pieces for d in dils]
    return pl.pallas_call(
        body, name=name, grid=(t // tq,),
        in_specs=in_specs + extra_specs,
        out_specs=pl.BlockSpec((tq, total), lambda i: (i, col // total)),
        out_shape=_du_shape(t), input_output_aliases=aliases,
        scratch_shapes=_stage_buffers(tq, n_stage),
        compiler_params=_cparams(("parallel",)),
    )(*tabs, *arrays, *extra)


def _band_masks(first_block, max_dist):
    qi = lax.broadcasted_iota(jnp.int32, (BLK, BLK), 0)
    kj = lax.broadcasted_iota(jnp.int32, (BLK, BLK), 1)
    valid_prev = jnp.logical_and(kj >= qi + (BLK - max_dist), jnp.logical_not(first_block))
    valid_cur = kj <= qi
    return valid_prev, valid_cur


_NN = (((1,), (0,)), ((), ()))
_NT = (((1,), (1,)), ((), ()))
_TN = (((0,), (0,)), ((), ()))


HEAD_STAGE = 8


def _attn_row_maps(nb):
    def cur(i):
        return jnp.minimum(i, nb - 1)

    def prev(i):
        return jnp.maximum(jnp.minimum(i, nb - 1) - 1, 0)

    return cur, prev


def _dil_spec(w, dil, rows, seg=None, off=0):
    seg = w if seg is None else seg
    assert off % w == 0 and (dil == 1 or seg % w == 0)
    return pl.BlockSpec((BLK, w), lambda r, i: (rows(i), (r * seg + off) // w))


def _dil_shape(l, dil, w, dtype=F32):
    return jax.ShapeDtypeStruct((l, dil * w), dtype)


def _attn_fwd_call(qkv2, sink, *, dil, group, max_dist, seg, offs, qw, kw, name):
    l = qkv2.shape[0]
    nh = qw // HEAD_DIM
    nb = l // BLK
    use_sink = sink is not None

    def body(*refs):
        if use_sink:
            sink_ref, refs = refs[0], refs[1:]
        q_ref, kp_ref, kc_ref, vp_ref, vc_ref, o_ref, lse_ref = refs
        valid_prev, valid_cur = _band_masks(pl.program_id(1) == 0, max_dist)
        lane = lax.broadcasted_iota(jnp.int32, (BLK, LANES), 1)
        lse_tile = jnp.zeros((BLK, LANES), F32)

        def dot(a, b, dims=_NN):
            return lax.dot_general(a, b, dims, preferred_element_type=F32)

        for g0 in range(0, nh, HEAD_STAGE):
            heads = list(range(g0, min(g0 + HEAD_STAGE, nh)))
            kv = {}
            for kh in sorted({h // group for h in heads}):
                ks = slice(kh * HEAD_DIM, (kh + 1) * HEAD_DIM)
                kv[kh] = tuple(ref[:, ks].astype(BF16) for ref in (kp_ref, kc_ref, vp_ref, vc_ref))
            qs = [q_ref[:, h * HEAD_DIM:(h + 1) * HEAD_DIM].astype(BF16) for h in heads]
            sps = [jnp.where(valid_prev, dot(qh, kv[h // group][0], _NT), NEG_INF) for h, qh in zip(heads, qs)]
            scs = [jnp.where(valid_cur, dot(qh, kv[h // group][1], _NT), NEG_INF) for h, qh in zip(heads, qs)]
            ms = [jnp.maximum(jnp.max(sp, axis=1, keepdims=True), jnp.max(sc, axis=1, keepdims=True))
                  for sp, sc in zip(sps, scs)]
            if use_sink:
                ms = [jnp.maximum(m, sink_ref[h]) for h, m in zip(heads, ms)]
            pps = [jnp.exp(sp - m) for sp, m in zip(sps, ms)]
            pcs = [jnp.exp(sc - m) for sc, m in zip(scs, ms)]
            dens = [jnp.sum(pp, axis=1, keepdims=True) + jnp.sum(pc, axis=1, keepdims=True)
                    for pp, pc in zip(pps, pcs)]
            if use_sink:
                dens = [den + jnp.exp(sink_ref[h] - m) for h, den, m in zip(heads, dens, ms)]
            outs = [dot(pp.astype(BF16), kv[h // group][2]) + dot(pc.astype(BF16), kv[h // group][3])
                    for h, pp, pc in zip(heads, pps, pcs)]
            for h, o, den, m in zip(heads, outs, dens, ms):
                o_ref[:, h * HEAD_DIM:(h + 1) * HEAD_DIM] = o / den
                lse_tile = jnp.where(lane == h, m + jnp.log(den), lse_tile)
        lse_ref[...] = lse_tile

    cur, prev = _attn_row_maps(nb)
    o_spec, lse_spec = _dil_spec(qw, dil, cur), _dil_spec(LANES, dil, cur)
    in_specs = [_dil_spec(qw, dil, cur, seg, offs[0]),
                _dil_spec(kw, dil, prev, seg, offs[1]), _dil_spec(kw, dil, cur, seg, offs[1]),
                _dil_spec(kw, dil, prev, seg, offs[2]), _dil_spec(kw, dil, cur, seg, offs[2])]
    args = [qkv2] * 5
    if use_sink:
        in_specs = [pl.BlockSpec(memory_space=pltpu.SMEM)] + in_specs
        args = [sink] + args
    return pl.pallas_call(
        body, name=name, grid=(dil, nb),
        in_specs=in_specs,
        out_specs=[o_spec, lse_spec],
        out_shape=[_dil_shape(l, dil, qw), _dil_shape(l, dil, LANES)],
        compiler_params=_cparams(("parallel", "parallel")),
    )(*args)


def _attn_bwd_call(qkv2, sink, o2, lse2, do2, dlse2, *, dil, group, max_dist, seg, offs, qw, kw, name):
    l = qkv2.shape[0]
    nh = qw // HEAD_DIM
    nb = l // BLK
    use_sink = sink is not None

    def body(*refs):
        if use_sink:
            sink_ref, refs = refs[0], refs[1:]
        (q_ref, kp_ref, kc_ref, vp_ref, vc_ref, o_ref, lse_ref, do_ref, dlse_ref,
         dq_ref, dk_ref, dv_ref, dsink_ref, ck_ref, cv_ref) = refs
        step = pl.program_id(1)

        @pl.when(jnp.logical_and(pl.program_id(0) == 0, step == 0))
        def _():
            dsink_ref[...] = jnp.zeros_like(dsink_ref)

        @pl.when(step == 0)
        def _():
            ck_ref[...] = jnp.zeros_like(ck_ref)
            cv_ref[...] = jnp.zeros_like(cv_ref)

        def dot(a, b, dims=_NN):
            return lax.dot_general(a, b, dims, preferred_element_type=F32)

        @pl.when(step < nb)
        def _():
            valid_prev, valid_cur = _band_masks(step == 0, max_dist)
            row = lax.broadcasted_iota(jnp.int32, (8, LANES), 0)
            lanes8 = lax.broadcasted_iota(jnp.int32, (8, LANES), 1)
            ds_tile = jnp.zeros((8, LANES), F32)
            for g0 in range(0, nh, HEAD_STAGE):
                heads = list(range(g0, min(g0 + HEAD_STAGE, nh)))
                hss = [slice(h * HEAD_DIM, (h + 1) * HEAD_DIM) for h in heads]
                kv = {}
                for kh in sorted({h // group for h in heads}):
                    ks = slice(kh * HEAD_DIM, (kh + 1) * HEAD_DIM)
                    kv[kh] = tuple(ref[:, ks].astype(BF16) for ref in (kp_ref, kc_ref, vp_ref, vc_ref))
                qs = [q_ref[:, hs].astype(BF16) for hs in hss]
                dos = [do_ref[:, hs] for hs in hss]
                dobs = [d.astype(BF16) for d in dos]
                lses = [lse_ref[:, h:h + 1] for h in heads]
                sps = [dot(qh, kv[h // group][0], _NT) for h, qh in zip(heads, qs)]
                scs = [dot(qh, kv[h // group][1], _NT) for h, qh in zip(heads, qs)]
                dpps = [dot(dob, kv[h // group][2], _NT) for h, dob in zip(heads, dobs)]
                dpcs = [dot(dob, kv[h // group][3], _NT) for h, dob in zip(heads, dobs)]
                pps = [jnp.where(valid_prev, jnp.exp(jnp.where(valid_prev, sp, NEG_INF) - ls), 0.0)
                       for sp, ls in zip(sps, lses)]
                pcs = [jnp.where(valid_cur, jnp.exp(jnp.where(valid_cur, sc, NEG_INF) - ls), 0.0)
                       for sc, ls in zip(scs, lses)]
                deltas = [jnp.sum(d * o_ref[:, hs], axis=1, keepdims=True) for d, hs in zip(dos, hss)]
                corrs = [dlse_ref[:, h:h + 1] - dl for h, dl in zip(heads, deltas)]
                dsps = [(pp * (dp + c)).astype(BF16) for pp, dp, c in zip(pps, dpps, corrs)]
                dscs = [(pc * (dp + c)).astype(BF16) for pc, dp, c in zip(pcs, dpcs, corrs)]
                for h, hs, dsp, dsc in zip(heads, hss, dsps, dscs):
                    dq = dot(dsp, kv[h // group][0]) + dot(dsc, kv[h // group][1])
                    dq_ref[:, hs] = dq.astype(BF16)
                parts = [(dot(dsc, qh, _TN), dot(dsp, qh, _TN),
                          dot(pc.astype(BF16), dob, _TN), dot(pp.astype(BF16), dob, _TN))
                         for dsc, dsp, qh, pc, pp, dob in zip(dscs, dsps, qs, pcs, pps, dobs)]
                for kh in kv:
                    ks = slice(kh * HEAD_DIM, (kh + 1) * HEAD_DIM)
                    mine = [p for h, p in zip(heads, parts) if h // group == kh]
                    dkc, dkp, dvc, dvp = (sum(p[j] for p in mine[1:]) + mine[0][j] for j in range(4))
                    dk_ref[:, ks] = (ck_ref[:, ks] + dkp).astype(BF16)
                    dv_ref[:, ks] = (cv_ref[:, ks] + dvp).astype(BF16)
                    ck_ref[:, ks] = dkc
                    cv_ref[:, ks] = dvc
                if use_sink:
                    for h, ls, dl in zip(heads, lses, deltas):
                        val = -jnp.sum(jnp.exp(sink_ref[h] - ls) * dl, axis=0, keepdims=True)
                        ds_tile = jnp.where(jnp.logical_and(row == 0, lanes8 == h), val, ds_tile)
            if use_sink:
                dsink_ref[...] += ds_tile

        @pl.when(step == nb)
        def _():
            dk_ref[...] = ck_ref[...].astype(BF16)
            dv_ref[...] = cv_ref[...].astype(BF16)

    cur, prev = _attn_row_maps(nb)
    q_spec, lse_spec = _dil_spec(qw, dil, cur), _dil_spec(LANES, dil, cur)
    lag_spec = _dil_spec(kw, dil, lambda i: jnp.maximum(i - 1, 0))
    in_specs = [_dil_spec(qw, dil, cur, seg, offs[0]),
                _dil_spec(kw, dil, prev, seg, offs[1]), _dil_spec(kw, dil, cur, seg, offs[1]),
                _dil_spec(kw, dil, prev, seg, offs[2]), _dil_spec(kw, dil, cur, seg, offs[2]),
                q_spec, lse_spec, q_spec, lse_spec]
    args = [qkv2] * 5 + [o2, lse2, do2, dlse2]
    if use_sink:
        in_specs = [pl.BlockSpec(memory_space=pltpu.SMEM)] + in_specs
        args = [sink] + args
    kv_shape = _dil_shape(l, dil, kw, BF16)
    return pl.pallas_call(
        body, name=name, grid=(dil, nb + 1),
        in_specs=in_specs,
        out_specs=[q_spec, lag_spec, lag_spec, pl.BlockSpec((8, LANES), lambda r, i: (0, 0))],
        out_shape=[_dil_shape(l, dil, qw, BF16), kv_shape, kv_shape,
                   jax.ShapeDtypeStruct((8, LANES), F32)],
        scratch_shapes=[pltpu.VMEM((BLK, kw), F32), pltpu.VMEM((BLK, kw), F32)],
        compiler_params=_cparams(("arbitrary", "arbitrary")),
    )(*args)


def _attn_config(tag, dil, group, max_dist, seg, offs, qw, kw):
    return dict(name=tag, dil=dil, group=group, max_dist=max_dist, seg=seg, offs=offs, qw=qw, kw=kw)


A_W = 8 * HEAD_DIM
ATTN_A_CFGS = tuple(_attn_config("attn_a%d" % dil, dil, 1, window // dil, 3 * A_W, (0, A_W, 2 * A_W), A_W, A_W)
                    for window, dil in A_CONFIGS)
B_KVW = 2 * HEAD_DIM
A_DILS = tuple(cfg["dil"] for cfg in ATTN_A_CFGS)
ATTN_B_CFG = _attn_config("attn_b", 1, B_GROUP, BLK - 1, A_W + 2 * B_KVW, (0, A_W, A_W + B_KVW), A_W, B_KVW)


def _attn_fwd(cfg, qkv2, sink):
    kw = {k: v for k, v in cfg.items() if k != "name"}
    return _attn_fwd_call(qkv2, sink, name=cfg["name"] + "_fwd", **kw)


def _attn_bwd(cfg, qkv2, o2, lse2, sink, do2, dlse2):
    kw = {k: v for k, v in cfg.items() if k != "name"}
    return _attn_bwd_call(qkv2, sink, o2, lse2, do2, dlse2, name=cfg["name"] + "_bwd", **kw)


def _head_expand():
    r = lax.broadcasted_iota(jnp.int32, (LANES, 8 * HEAD_DIM), 0)
    c = lax.broadcasted_iota(jnp.int32, (LANES, 8 * HEAD_DIM), 1)
    return (c // HEAD_DIM == r).astype(F32)


def _combine_weights(l0, l1, l2):
    m = jnp.maximum(jnp.maximum(l0, l1), l2)
    e0, e1, e2 = jnp.exp(l0 - m), jnp.exp(l1 - m), jnp.exp(l2 - m)
    inv = 1.0 / (e0 + e1 + e2)
    return e0 * inv, e1 * inv, e2 * inv


def _combine_fwd_call(os_, lses, dils, *, tq=512):
    w = os_[0].shape[1] // dils[0]
    t = os_[0].shape[0] * dils[0]
    groups = w // LANES
    n_stage = (groups + 1) * sum(d > 1 for d in dils)

    def body(*refs):
        o_refs, l_refs, y_ref = refs[:3], refs[3:6], refs[6]
        stages = iter(refs[7:])
        ws = _combine_weights(*[_load_dilated(l, next(stages) if d > 1 else None, d, LANES, 0)
                                for l, d in zip(l_refs, dils)])
        e = _head_expand()
        wide = [_dot_mask(e, wt, mask_left=False) for wt in ws]
        for j in range(groups):
            cs = slice(j * LANES, (j + 1) * LANES)
            y = None
            for o_ref, d, wd in zip(o_refs, dils, wide):
                term = wd[:, cs] * _load_dilated(o_ref, next(stages) if d > 1 else None, d, w, j)
                y = term if y is None else y + term
            y_ref[:, cs] = y

    return pl.pallas_call(
        body, name="combine_fwd", grid=(t // tq,),
        in_specs=[_dilated_spec(tq, w, d) for d in dils] + [_dilated_spec(tq, LANES, d) for d in dils],
        out_specs=pl.BlockSpec((tq, w), lambda i: (i, 0)),
        out_shape=jax.ShapeDtypeStruct((t, w), F32),
        scratch_shapes=_stage_buffers(tq, n_stage),
        compiler_params=_cparams(("parallel",)),
    )(*os_, *lses)


def _combine_bwd_call(os_, lses, dy, dils, *, tq=512):
    t, w = dy.shape
    groups = w // LANES
    n_stage = 2 * (groups + 1) * sum(d > 1 for d in dils)

    def body(*refs):
        o_refs, l_refs, dy_ref, do_refs, dl_refs = refs[:3], refs[3:6], refs[6], refs[7:10], refs[10:13]
        stages = iter(refs[13:])

        def stage(d):
            return next(stages) if d > 1 else None

        ws = _combine_weights(*[_load_dilated(l, stage(d), d, LANES, 0) for l, d in zip(l_refs, dils)])
        e = _head_expand()
        dyv = dy_ref[...]
        dws = []
        for o_ref, do_ref, d, wt in zip(o_refs, do_refs, dils, ws):
            do = _dot_mask(e, wt, mask_left=False) * dyv
            for j in range(groups):
                _store_dilated(do_ref, stage(d), d, w, j, do[:, j * LANES:(j + 1) * LANES])
            ov = jnp.concatenate([_load_dilated(o_ref, stage(d), d, w, j) for j in range(groups)], axis=1)
            dws.append(_dot_mask(e, dyv * ov, _NT, mask_left=False))
        mean = ws[0] * dws[0] + ws[1] * dws[1] + ws[2] * dws[2]
        for dl_ref, d, wt, dw in zip(dl_refs, dils, ws, dws):
            _store_dilated(dl_ref, stage(d), d, LANES, 0, wt * (dw - mean))

    o_specs = [_dilated_spec(tq, w, d) for d in dils]
    l_specs = [_dilated_spec(tq, LANES, d) for d in dils]
    return pl.pallas_call(
        body, name="combine_bwd", grid=(t // tq,),
        in_specs=o_specs + l_specs + [pl.BlockSpec((tq, w), lambda i: (i, 0))], out_specs=o_specs + l_specs,
        out_shape=[jax.ShapeDtypeStruct((t // d, d * w), F32) for d in dils]
        + [jax.ShapeDtypeStruct((t // d, d * LANES), F32) for d in dils],
        scratch_shapes=_stage_buffers(tq, n_stage),
        compiler_params=_cparams(("parallel",)),
    )(*os_, *lses, dy)


C_QKW = C_QK_HEADS * C_DK
C_CONV_W = 2 * C_QKW + C_V_HEADS * C_DK
HALO = 8


def _silu_parts(z):
    sig = jax.nn.sigmoid(z)
    return z * sig, sig * (1.0 + z * (1.0 - sig))


def _conv_window_specs(tq, t):
    c = C_CONV_W
    cb = COL["c_qkv"] // c
    blk = pl.BlockSpec((tq, c), lambda i: (i, cb))
    before = pl.BlockSpec((HALO, c), lambda i: (jnp.maximum(i * (tq // HALO) - 1, 0), cb))
    return c, cb, blk, before


def _conv_prep_fwd_call(u, w, *, tq=512):
    t = u.shape[0]
    c, _, x_spec, halo_spec = _conv_window_specs(tq, t)
    nqk = 2 * C_QK_HEADS

    def body(x_ref, halo_ref, w_ref, z_ref, qk_ref, v_ref):
        i = pl.program_id(0)
        halo = jnp.where(i == 0, 0.0, halo_ref[...])
        xc = jnp.concatenate([halo, x_ref[...]], axis=0)
        wv = w_ref[...]
        z = xc[HALO - 3:HALO - 3 + tq] * wv[0:1]
        for j in range(1, C_CONV):
            z = z + xc[HALO - 3 + j:HALO - 3 + j + tq] * wv[j:j + 1]
        z_ref[...] = z
        act, _ = _silu_parts(z)
        for h in range(nqk):
            a = act[:, h * C_DK:(h + 1) * C_DK]
            qk_ref[:, h * C_DK:(h + 1) * C_DK] = a * lax.rsqrt(jnp.sum(a * a, axis=1, keepdims=True) + EPS)
        v_ref[...] = act[:, nqk * C_DK:]

    return pl.pallas_call(
        body, name="conv_prep_fwd", grid=(t // tq,),
        in_specs=[x_spec, halo_spec, pl.BlockSpec((C_CONV, c), lambda i: (0, 0))],
        out_specs=[pl.BlockSpec((tq, c), lambda i: (i, 0)),
                   pl.BlockSpec((tq, 2 * C_QKW), lambda i: (i, 0)),
                   pl.BlockSpec((tq, c - 2 * C_QKW), lambda i: (i, 0))],
        out_shape=[jax.ShapeDtypeStruct((t, c), F32), jax.ShapeDtypeStruct((t, 2 * C_QKW), F32),
                   jax.ShapeDtypeStruct((t, c - 2 * C_QKW), F32)],
        compiler_params=_cparams(("parallel",)),
    )(u, u, w)


def _conv_prep_dz_call(z, dqk, dv, *, tq=512):
    t, c = z.shape
    nqk = 2 * C_QK_HEADS

    def body(z_ref, dqk_ref, dv_ref, dz_ref):
        zv = z_ref[...]
        act, dact = _silu_parts(zv)
        for h in range(nqk):
            hs = slice(h * C_DK, (h + 1) * C_DK)
            a = act[:, hs]
            r = lax.rsqrt(jnp.sum(a * a, axis=1, keepdims=True) + EPS)
            nrm = a * r
            dn = dqk_ref[:, hs]
            da = r * (dn - nrm * jnp.sum(dn * nrm, axis=1, keepdims=True))
            dz_ref[:, hs] = da * dact[:, hs]
        dz_ref[:, nqk * C_DK:] = dv_ref[...] * dact[:, nqk * C_DK:]

    return pl.pallas_call(
        body, name="conv_prep_dz", grid=(t // tq,),
        in_specs=[pl.BlockSpec((tq, c), lambda i: (i, 0)),
                  pl.BlockSpec((tq, 2 * C_QKW), lambda i: (i, 0)),
                  pl.BlockSpec((tq, c - 2 * C_QKW), lambda i: (i, 0))],
        out_specs=pl.BlockSpec((tq, c), lambda i: (i, 0)),
        out_shape=jax.ShapeDtypeStruct((t, c), F32),
        compiler_params=_cparams(("parallel",)),
    )(z, dqk, dv)


def _conv_bwd_call(u, dz, w, du_buf, *, tq=512):
    t = u.shape[0]
    nt = t // tq
    c, cb, x_spec, halo_spec = _conv_window_specs(tq, t)
    extra, extra_specs, aliases = _du_operands(du_buf, 5)

    def body(x_ref, xh_ref, dz_ref, dzh_ref, w_ref, *refs):
        dx_ref, dw_ref = refs[len(extra):]
        i = pl.program_id(0)
        xc = jnp.concatenate([jnp.where(i == 0, 0.0, xh_ref[...]), x_ref[...]], axis=0)
        dzv = dz_ref[...]
        dzc = jnp.concatenate([dzv, jnp.where(i == nt - 1, 0.0, dzh_ref[...])], axis=0)
        wv = w_ref[...]
        dx = dzv * wv[3:4]
        for s in range(1, C_CONV):
            dx = dx + dzc[s:s + tq] * wv[3 - s:4 - s]
        dx_ref[...] = dx.astype(BF16)
        row = lax.broadcasted_iota(jnp.int32, (8, c), 0)
        dw = jnp.zeros((8, c), F32)
        for j in range(C_CONV):
            prod = dzv * xc[HALO - 3 + j:HALO - 3 + j + tq]
            col = jnp.sum(jnp.sum(prod.reshape(tq // 8, 8, c), axis=0), axis=0, keepdims=True)
            dw = jnp.where(row == j, col, dw)

        @pl.when(i == 0)
        def _():
            dw_ref[...] = dw

        @pl.when(i > 0)
        def _():
            dw_ref[...] += dw

    blk = pl.BlockSpec((tq, c), lambda i: (i, 0))
    after = pl.BlockSpec((HALO, c), lambda i: (jnp.minimum((i + 1) * (tq // HALO), t // HALO - 1), 0))
    return pl.pallas_call(
        body, name="conv_bwd", grid=(nt,),
        in_specs=[x_spec, halo_spec, blk, after, pl.BlockSpec((C_CONV, c), lambda i: (0, 0))] + extra_specs,
        out_specs=[pl.BlockSpec((tq, c), lambda i: (i, cb)), pl.BlockSpec((8, c), lambda i: (0, 0))],
        out_shape=[_du_shape(t), jax.ShapeDtypeStruct((8, c), F32)],
        input_output_aliases=aliases,
        compiler_params=_cparams(("arbitrary",)),
    )(u, u, dz, dz, w, *extra)


C_VW = C_V_HEADS * C_DK


def _softplus(x):
    return jnp.maximum(x, 0.0) + jnp.log(1.0 + jnp.exp(-jnp.abs(x)))


def _tri_masks():
    r = lax.broadcasted_iota(jnp.int32, (CHUNK, CHUNK), 0)
    c = lax.broadcasted_iota(jnp.int32, (CHUNK, CHUNK), 1)
    return r >= c, r > c


def _split_bf16(a):
    hi = a.astype(BF16)
    return hi, (a - hi.astype(F32)).astype(BF16)


def _dot_hi(a, b, dims=None):
    dims = _NN if dims is None else dims
    ah, al = _split_bf16(a)
    bh, bl = _split_bf16(b)

    def d(x, y):
        return lax.dot_general(x, y, dims, preferred_element_type=F32)

    return d(ah, bh) + (d(ah, bl) + d(al, bh))


def _dot_mask(mask, b, dims=None, mask_left=True):
    dims = _NN if dims is None else dims
    mb = mask.astype(BF16)
    b1 = b.astype(BF16)
    rest = b - b1.astype(F32)
    b2 = rest.astype(BF16)
    b3 = (rest - b2.astype(F32)).astype(BF16)
    out = None
    for p in (b1, b2, b3):
        term = (lax.dot_general(mb, p, dims, preferred_element_type=F32) if mask_left
                else lax.dot_general(p, mb, dims, preferred_element_type=F32))
        out = term if out is None else out + term
    return out


def _unit_lower_inverses(mats):
    r = lax.broadcasted_iota(jnp.int32, (CHUNK, CHUNK), 0)
    c = lax.broadcasted_iota(jnp.int32, (CHUNK, CHUNK), 1)
    eye = (r == c).astype(F32)
    xs = [eye - a for a in mats]
    ps = [_dot_hi(a, a) for a in mats]
    steps = int(math.log2(CHUNK)) - 1
    for s in range(steps):
        xs = [x + _dot_hi(x, p) for x, p in zip(xs, ps)]
        if s < steps - 1:
            ps = [_dot_hi(p, p) for p in ps]
    return xs


def _gate_tiles(cab, alog, dtb):
    pre = cab + dtb
    g = -jnp.exp(alog) * _softplus(pre)
    beta = jax.nn.sigmoid(pltpu.roll(cab, LANES - C_V_HEADS, 1))
    return g, beta, pre


def _chunk_common(kk, qk, gc, gct, beta, h, tri, strict):
    gcol, grow, bcol = gc[:, h:h + 1], gct[h:h + 1, :], beta[:, h:h + 1]
    decay = jnp.where(tri, jnp.exp(jnp.where(tri, gcol - grow, 0.0)), 0.0)
    kkd = jnp.where(strict, kk * decay, 0.0)
    attn = jnp.where(tri, qk * decay, 0.0)
    glast = gc[CHUNK - 1:CHUNK, h:h + 1]
    return gcol, bcol, decay, kkd, attn, glast


def _cab_spec():
    return pl.BlockSpec((CHUNK, LANES), lambda n: (n, COL["c_ab"] // LANES))


def _delta_prep_call(qk, v, u, alog, dtb, gather_src=None):
    t = qk.shape[0]
    nc = t // CHUNK
    scale = C_DK ** -0.5
    riding = gather_src is not None
    ng = len(gather_src) if riding else 0

    def body(q_ref, k_ref, v_ref, cab_ref, alog_ref, dtb_ref, *refs):
        if riding:
            gather_refs = (refs[:ng], refs[ng + 8:2 * ng + 8]) + tuple(refs[2 * ng + 8:])
            refs = refs[ng:ng + 8]

            @pl.when(pl.program_id(0) == 0)
            def _():
                _gather_start(*gather_refs)
        u_ref, w_ref, qd_ref, kd_ref, attn_ref, tmat_ref, gc_ref, beta_ref = refs
        tri, strict = _tri_masks()
        g, beta, _ = _gate_tiles(cab_ref[...], alog_ref[...], dtb_ref[...])
        gc = _dot_mask(tri, g)
        gct = gc.T
        gc_ref[...] = gc
        beta_ref[...] = beta
        mats, rhs = [], []
        for j in range(C_QK_HEADS):
            js = slice(j * C_DK, (j + 1) * C_DK)
            kf, qf = k_ref[:, js], q_ref[:, js] * scale
            kb, qb = kf.astype(BF16), qf.astype(BF16)
            kk = lax.dot_general(kb, kb, _NT, preferred_element_type=F32)
            qk = lax.dot_general(qb, kb, _NT, preferred_element_type=F32)
            for h in (2 * j, 2 * j + 1):
                hs = slice(h * C_DK, (h + 1) * C_DK)
                gcol, bcol, decay, kkd, attn, glast = _chunk_common(kk, qk, gc, gct, beta, h, tri, strict)
                gexp = jnp.exp(gcol)
                mats.append(kkd * bcol)
                rhs.append(jnp.concatenate([v_ref[:, hs] * bcol, kf * (bcol * gexp)], axis=1))
                qd_ref[:, hs] = (qf * gexp).astype(BF16)
                kd_ref[:, hs] = (kf * jnp.exp(glast - gcol)).astype(BF16)
                attn_ref[:, h * CHUNK:(h + 1) * CHUNK] = attn.astype(BF16)
        for h, (tmat, r) in enumerate(zip(_unit_lower_inverses(mats), rhs)):
            hs = slice(h * C_DK, (h + 1) * C_DK)
            uw = _dot_hi(tmat, r)
            u_ref[:, hs] = uw[:, :C_DK]
            w_ref[:, hs] = uw[:, C_DK:]
            tmat_ref[:, h * CHUNK:(h + 1) * CHUNK] = tmat.T
        if riding:
            @pl.when(pl.program_id(0) == nc - 1)
            def _():
                _gather_finish(*gather_refs)

    def blk(w):
        return pl.BlockSpec((CHUNK, w), lambda n: (n, 0))

    row = pl.BlockSpec((1, LANES), lambda n: (0, 0))
    big = jax.ShapeDtypeStruct((t, C_VW), F32)
    sq = jax.ShapeDtypeStruct((t, C_V_HEADS * CHUNK), F32)
    tile = jax.ShapeDtypeStruct((t, LANES), F32)
    half = jax.ShapeDtypeStruct((t, C_VW), BF16)
    in_specs = [blk(C_QKW), pl.BlockSpec((CHUNK, C_QKW), lambda n: (n, 1)), blk(C_VW), _cab_spec(), row, row]
    out_specs = [blk(C_VW)] * 4 + [blk(C_V_HEADS * CHUNK)] * 2 + [blk(LANES)] * 2
    out_shape = [big, big, half, half, jax.ShapeDtypeStruct(sq.shape, BF16), sq] + [tile] * 2
    args = [qk, qk, v, u, alog, dtb]
    if riding:
        in_specs += [HBM_SPEC] * ng
        out_specs += [HBM_SPEC] * ng
        out_shape += _gathered_shapes(gather_src)
        args += list(gather_src)
    return pl.pallas_call(
        body, name="delta_prep_gather" if riding else "delta_prep", grid=(nc,),
        in_specs=in_specs, out_specs=out_specs, out_shape=out_shape,
        scratch_shapes=_gather_sems(ng) if riding else [],
        compiler_params=_cparams(("arbitrary",) if riding else ("parallel",)),
    )(*args)


SCAN_SUB = 4


def _delta_scan_call(u, w, qd, kd, attn, gc):
    t = u.shape[0]
    nc = t // CHUNK
    rows = SCAN_SUB * CHUNK

    def body(u_ref, w_ref, qd_ref, kd_ref, attn_ref, gc_ref, o_ref, vn_ref, st_ref, s_ref):
        @pl.when(pl.program_id(0) == 0)
        def _():
            s_ref[...] = jnp.zeros_like(s_ref)

        hss = [slice(h * C_DK, (h + 1) * C_DK) for h in range(C_V_HEADS)]
        states = [s_ref[hs, :] for hs in hss]
        for c in range(SCAN_SUB):
            rs = slice(c * CHUNK, (c + 1) * CHUNK)
            for hs, s in zip(hss, states):
                st_ref[c, hs, :] = s
            sbs = [s.astype(BF16) for s in states]
            vns = [u_ref[rs, hs] - jnp.dot(w_ref[rs, hs].astype(BF16), sb, preferred_element_type=F32)
                   for hs, sb in zip(hss, sbs)]
            qss = [jnp.dot(qd_ref[rs, hs].astype(BF16), sb, preferred_element_type=F32) for hs, sb in zip(hss, sbs)]
            vnbs = [vn.astype(BF16) for vn in vns]
            for h, hs in enumerate(hss):
                vn_ref[rs, hs] = vnbs[h]
                o_ref[rs, hs] = qss[h] + jnp.dot(attn_ref[rs, h * CHUNK:(h + 1) * CHUNK].astype(BF16), vnbs[h],
                                                 preferred_element_type=F32)
            last = (c + 1) * CHUNK - 1
            states = [states[h] * jnp.exp(gc_ref[last:last + 1, h:h + 1])
                      + lax.dot_general(kd_ref[rs, hs].astype(BF16), vnbs[h], _TN, preferred_element_type=F32)
                      for h, hs in enumerate(hss)]
        for hs, s in zip(hss, states):
            s_ref[hs, :] = s

    def blk(wd):
        return pl.BlockSpec((rows, wd), lambda n: (n, 0))

    big = jax.ShapeDtypeStruct((t, C_VW), F32)
    return pl.pallas_call(
        body, name="delta_scan", grid=(nc // SCAN_SUB,),
        in_specs=[blk(C_VW)] * 4 + [blk(C_V_HEADS * CHUNK), blk(LANES)],
        out_specs=[blk(C_VW), blk(C_VW), pl.BlockSpec((SCAN_SUB, C_VW, C_DK), lambda n: (n, 0, 0))],
        out_shape=[big, jax.ShapeDtypeStruct((t, C_VW), BF16), jax.ShapeDtypeStruct((nc, C_VW, C_DK), F32)],
        scratch_shapes=[pltpu.VMEM((C_VW, C_DK), F32)],
        compiler_params=_cparams(("arbitrary",)),
    )(u, w, qd, kd, attn, gc)


def _delta_scan_bwd_call(do, w, qd, kd, attn, gc, vn, st):
    t = do.shape[0]
    nc = t // CHUNK
    rows = SCAN_SUB * CHUNK
    steps = nc // SCAN_SUB

    def body(do_ref, w_ref, qd_ref, kd_ref, attn_ref, gc_ref, vn_ref, st_ref,
             du_ref, dw_ref, dqd_ref, dkd_ref, dattn_ref, dgl_ref, ds_ref):
        @pl.when(pl.program_id(0) == 0)
        def _():
            ds_ref[...] = jnp.zeros_like(ds_ref)

        tri, _ = _tri_masks()
        row = lax.broadcasted_iota(jnp.int32, (8, LANES), 0)
        lane = lax.broadcasted_iota(jnp.int32, (8, LANES), 1)
        hss = [slice(h * C_DK, (h + 1) * C_DK) for h in range(C_V_HEADS)]
        css = [slice(h * CHUNK, (h + 1) * CHUNK) for h in range(C_V_HEADS)]

        def dg(a, b, dims):
            return lax.dot_general(a, b, dims, preferred_element_type=F32)

        dsps = [ds_ref[hs, :] for hs in hss]
        for c in reversed(range(SCAN_SUB)):
            rs = slice(c * CHUNK, (c + 1) * CHUNK)
            dgl = jnp.zeros((8, LANES), F32)
            ss = [st_ref[c, hs, :] for hs in hss]
            sbs = [s.astype(BF16) for s in ss]
            dspbs = [d.astype(BF16) for d in dsps]
            dobs = [do_ref[rs, hs].astype(BF16) for hs in hss]
            vnbs = [vn_ref[rs, hs].astype(BF16) for hs in hss]
            dvns = [dg(attn_ref[rs, cs].astype(BF16), dob, _TN) + dg(kd_ref[rs, hs].astype(BF16), dspb, _NN)
                    for hs, cs, dob, dspb in zip(hss, css, dobs, dspbs)]
            for h, hs in enumerate(hss):
                dqd_ref[rs, hs] = dg(dobs[h], sbs[h], _NT)
                dkd_ref[rs, hs] = dg(vnbs[h], dspbs[h], _NT)
                dattn_ref[rs, css[h]] = jnp.where(tri, dg(dobs[h], vnbs[h], _NT), 0.0)
            dvnbs = [d.astype(BF16) for d in dvns]
            for h, hs in enumerate(hss):
                du_ref[rs, hs] = dvns[h]
                dw_ref[rs, hs] = -dg(dvnbs[h], sbs[h], _NT)
                tot = jnp.sum(jnp.sum(dsps[h] * ss[h], axis=0, keepdims=True), axis=1, keepdims=True)
                dgl = jnp.where(jnp.logical_and(row == 0, lane == h), tot, dgl)
            dgl_ref[c * 8:(c + 1) * 8, :] = dgl
            last = (c + 1) * CHUNK - 1
            dsps = [dg(qd_ref[rs, hs].astype(BF16), dobs[h], _TN) + jnp.exp(gc_ref[last:last + 1, h:h + 1]) * dsps[h]
                    - dg(w_ref[rs, hs].astype(BF16), dvnbs[h], _TN) for h, hs in enumerate(hss)]
        for hs, d in zip(hss, dsps):
            ds_ref[hs, :] = d

    def blk(wd):
        return pl.BlockSpec((rows, wd), lambda n: (steps - 1 - n, 0))

    big = jax.ShapeDtypeStruct((t, C_VW), F32)
    return pl.pallas_call(
        body, name="delta_scan_bwd", grid=(steps,),
        in_specs=[blk(C_VW)] * 4 + [blk(C_V_HEADS * CHUNK), blk(LANES), blk(C_VW),
                                    pl.BlockSpec((SCAN_SUB, C_VW, C_DK), lambda n: (steps - 1 - n, 0, 0))],
        out_specs=[blk(C_VW)] * 4 + [blk(C_V_HEADS * CHUNK),
                                     pl.BlockSpec((SCAN_SUB * 8, LANES), lambda n: (steps - 1 - n, 0))],
        out_shape=[big] * 4 + [jax.ShapeDtypeStruct((t, C_V_HEADS * CHUNK), F32),
                               jax.ShapeDtypeStruct((nc * 8, LANES), F32)],
        scratch_shapes=[pltpu.VMEM((C_VW, C_DK), F32)],
        compiler_params=_cparams(("arbitrary",)),
    )(do, w, qd, kd, attn, gc, vn, st)


PREP_SUB = 4


def _delta_prep_bwd_call(qk, v, proj, alog, dtb, tmat, u, w, gc, beta, du, dw, dqd, dkd, dattn, dgl, du_buf):
    t = qk.shape[0]
    extra, extra_specs, aliases = _du_operands(du_buf, 17)
    nc = t // CHUNK
    rows = PREP_SUB * CHUNK
    scale = C_DK ** -0.5

    def body(q_ref, k_ref, v_ref, cab_ref, alog_ref, dtb_ref, tmat_ref, u_ref, w_ref, gc_ref, beta_ref,
             du_ref, dw_ref, dqd_ref, dkd_ref, dattn_ref, dgl_ref, *outs):
        dcab_ref, dqk_ref, dv_ref, dpar_ref = outs[len(extra):]
        tri, strict = _tri_masks()
        lane = lax.broadcasted_iota(jnp.int32, (CHUNK, LANES), 1)
        rowi = lax.broadcasted_iota(jnp.int32, (CHUNK, 1), 0)
        subs = range(PREP_SUB)
        rss = [slice(c * CHUNK, (c + 1) * CHUNK) for c in subs]
        betas = [beta_ref[rs, :] for rs in rss]

        def dot(x, y, dims=_NN):
            return lax.dot_general(x, y, dims, preferred_element_type=F32)

        heads = []
        for c, rs in zip(subs, rss):
            gc = gc_ref[rs, :]
            gct = gc.T
            for j in range(C_QK_HEADS):
                js = slice(j * C_DK, (j + 1) * C_DK)
                kf, qf = k_ref[rs, js], q_ref[rs, js] * scale
                kb, qb = kf.astype(BF16), qf.astype(BF16)
                kk = dot(kb, kb, _NT)
                qk = dot(qb, kb, _NT)
                for h in (2 * j, 2 * j + 1):
                    heads.append((c, rs, h, kf, qf, kb, qb) + _chunk_common(kk, qk, gc, gct, betas[c], h, tri, strict))

        def cols(h):
            return slice(h * C_DK, (h + 1) * C_DK)

        def sq(h):
            return slice(h * CHUNK, (h + 1) * CHUNK)

        dvks = [_dot_hi(tmat_ref[hd[1], sq(hd[2])],
                        jnp.concatenate([du_ref[hd[1], cols(hd[2])], dw_ref[hd[1], cols(hd[2])]], axis=1))
                for hd in heads]
        das = [-jnp.where(strict, _dot_hi(dvk, jnp.concatenate([u_ref[hd[1], cols(hd[2])], w_ref[hd[1], cols(hd[2])]],
                                                               axis=1), _NT), 0.0)
               for hd, dvk in zip(heads, dvks)]
        pre = []
        for (c, rs, h, kf, qf, kb, qb, gcol, bcol, decay, kkd, attn, glast), da in zip(heads, das):
            dattn_h = dattn_ref[rs, sq(h)]
            pre.append(((da * decay * bcol).astype(BF16), (dattn_h * decay).astype(BF16),
                        da * kkd * bcol + dattn_h * attn))
        mms = [(dot(dkk, hd[5]), dot(dkk, hd[5], _TN), dot(dqk, hd[6], _TN), dot(dqk, hd[5]))
               for hd, (dkk, dqk, e) in zip(heads, pre)]
        dq_parts, dk_parts = {}, {}
        dgc_tiles = [jnp.zeros((CHUNK, LANES), F32) for _ in subs]
        db_tiles = [jnp.zeros((CHUNK, LANES), F32) for _ in subs]
        later_tiles = [jnp.zeros((CHUNK, LANES), F32) for _ in subs]
        upper = jnp.logical_not(strict)
        for (c, rs, h, kf, qf, kb, qb, gcol, bcol, decay, kkd, attn, glast), dvk, da, (_, _, e), mm in zip(
                heads, dvks, das, pre, mms):
            hs = cols(h)
            gexp = jnp.exp(gcol)
            fdec = jnp.exp(glast - gcol)
            dvb, dkb = dvk[:, :C_DK], dvk[:, C_DK:]
            dgc = jnp.sum(e, axis=1, keepdims=True)
            later = jnp.sum(jnp.where(upper, jnp.sum(e, axis=0, keepdims=True), 0.0), axis=1, keepdims=True)
            later_tiles[c] = jnp.where(lane == h, later, later_tiles[c])
            dk_parts[c, h] = mm[0] + mm[1] + mm[2] + dkb * (bcol * gexp) + dkd_ref[rs, hs] * fdec
            dq_parts[c, h] = mm[3] + dqd_ref[rs, hs] * gexp
            dv_ref[rs, hs] = dvb * bcol
            s_kb = jnp.sum(dkb * kf, axis=1, keepdims=True)
            db = (jnp.sum(da * kkd, axis=1, keepdims=True) + jnp.sum(dvb * v_ref[rs, hs], axis=1, keepdims=True)
                  + s_kb * gexp)
            rho = jnp.sum(dkd_ref[rs, hs] * kf, axis=1, keepdims=True) * fdec
            dgc = (dgc + s_kb * bcol * gexp + jnp.sum(dqd_ref[rs, hs] * qf, axis=1, keepdims=True) * gexp - rho)
            last = jnp.sum(rho, axis=0, keepdims=True) + dgl_ref[c * 8:c * 8 + 1, h:h + 1] * jnp.exp(glast)
            dgc = dgc + jnp.where(rowi == CHUNK - 1, last, 0.0)
            dgc_tiles[c] = jnp.where(lane == h, dgc, dgc_tiles[c])
            db_tiles[c] = jnp.where(lane == h, db, db_tiles[c])
        alog = alog_ref[...]
        row8 = lax.broadcasted_iota(jnp.int32, (8, LANES), 0)
        par = jnp.zeros((8, LANES), F32)
        for c, rs in zip(subs, rss):
            for j in range(C_QK_HEADS):
                dqk_ref[rs, j * C_DK:(j + 1) * C_DK] = (dq_parts[c, 2 * j] + dq_parts[c, 2 * j + 1]) * scale
                dqk_ref[rs, C_QKW + j * C_DK:C_QKW + (j + 1) * C_DK] = dk_parts[c, 2 * j] + dk_parts[c, 2 * j + 1]
            dg = _dot_mask(upper, dgc_tiles[c]) - later_tiles[c]
            g, _, gate_pre = _gate_tiles(cab_ref[rs, :], alog, dtb_ref[...])
            dca = dg * (-jnp.exp(alog)) * jax.nn.sigmoid(gate_pre)
            beta = betas[c]
            dcab_ref[rs, :LANES] = (dca + pltpu.roll(db_tiles[c] * beta * (1.0 - beta), C_V_HEADS, 1)).astype(BF16)
            dcab_ref[rs, LANES:] = jnp.zeros((CHUNK, D_IN_PAD - COL["c_ab"] - LANES), BF16)
            par = par + jnp.where(row8 == 0, jnp.sum(dg * g, axis=0, keepdims=True),
                                  jnp.where(row8 == 1, jnp.sum(dca, axis=0, keepdims=True), 0.0))

        @pl.when(pl.program_id(0) == 0)
        def _():
            dpar_ref[...] = par

        @pl.when(pl.program_id(0) > 0)
        def _():
            dpar_ref[...] += par

    def blk(wd):
        return pl.BlockSpec((rows, wd), lambda n: (n, 0))

    row = pl.BlockSpec((1, LANES), lambda n: (0, 0))
    sqs = blk(C_V_HEADS * CHUNK)
    tail = D_IN_PAD - COL["c_ab"]
    assert COL["c_ab"] % tail == 0 and nc % PREP_SUB == 0
    return pl.pallas_call(
        body, name="delta_prep_bwd", grid=(nc // PREP_SUB,),
        in_specs=[blk(C_QKW), pl.BlockSpec((rows, C_QKW), lambda n: (n, 1)), blk(C_VW),
                  pl.BlockSpec((rows, LANES), lambda n: (n, COL["c_ab"] // LANES)), row, row, sqs,
                  blk(C_VW), blk(C_VW),
                  blk(LANES), blk(LANES), blk(C_VW), blk(C_VW), blk(C_VW), blk(C_VW), sqs,
                  pl.BlockSpec((PREP_SUB * 8, LANES), lambda n: (n, 0))] + extra_specs,
        out_specs=[pl.BlockSpec((rows, tail), lambda n: (n, COL["c_ab"] // tail)),
                   blk(2 * C_QKW), blk(C_VW), pl.BlockSpec((8, LANES), lambda n: (0, 0))],
        out_shape=[_du_shape(t), jax.ShapeDtypeStruct((t, 2 * C_QKW), F32),
                   jax.ShapeDtypeStruct((t, C_VW), F32), jax.ShapeDtypeStruct((8, LANES), F32)],
        input_output_aliases=aliases,
        compiler_params=_cparams(("arbitrary",)),
    )(qk, qk, v, proj, alog, dtb, tmat, u, w, gc, beta, du, dw, dqd, dkd, dattn, dgl, *extra)


def _z_spec(tq):
    return pl.BlockSpec((tq, C_VW), lambda i: (i, COL["c_z"] // C_VW))


def _gated_norm_fwd_call(o, u, gain, *, tq=512):
    t, w = o.shape

    def body(o_ref, z_ref, g_ref, y_ref):
        act, _ = _silu_parts(z_ref[...])
        gv = g_ref[...]
        for h in range(C_V_HEADS):
            hs = slice(h * C_DK, (h + 1) * C_DK)
            ov = o_ref[:, hs]
            r = lax.rsqrt(jnp.mean(ov * ov, axis=1, keepdims=True) + EPS)
            y_ref[:, hs] = ov * r * gv * act[:, hs]

    blk = pl.BlockSpec((tq, w), lambda i: (i, 0))
    return pl.pallas_call(
        body, name="gated_norm_fwd", grid=(t // tq,),
        in_specs=[blk, _z_spec(tq), pl.BlockSpec((1, C_DK), lambda i: (0, 0))], out_specs=blk,
        out_shape=jax.ShapeDtypeStruct((t, w), F32),
        compiler_params=_cparams(("parallel",)),
    )(o, u, gain)


def _gated_norm_bwd_call(o, u, gain, dy, du_buf, *, tq=512):
    t, w = o.shape
    nt = t // tq
    extra, extra_specs, aliases = _du_operands(du_buf, 4)

    def body(o_ref, z_ref, g_ref, dy_ref, *refs):
        dz_ref, do_ref, dg_ref, acc_ref = refs[len(extra):]
        i = pl.program_id(0)
        act, dact = _silu_parts(z_ref[...])
        gv = g_ref[...]
        part = jnp.zeros((8, C_DK), F32)
        for h in range(C_V_HEADS):
            hs = slice(h * C_DK, (h + 1) * C_DK)
            ov = o_ref[:, hs]
            r = lax.rsqrt(jnp.mean(ov * ov, axis=1, keepdims=True) + EPS)
            xh = ov * r
            dyv = dy_ref[:, hs]
            dn = dyv * act[:, hs]
            dz_ref[:, hs] = (dyv * xh * gv * dact[:, hs]).astype(BF16)
            dxh = dn * gv
            do_ref[:, hs] = r * (dxh - xh * jnp.mean(dxh * xh, axis=1, keepdims=True))
            part = part + jnp.sum((dn * xh).reshape(tq // 8, 8, C_DK), axis=0)

        @pl.when(i == 0)
        def _():
            acc_ref[...] = part

        @pl.when(i > 0)
        def _():
            acc_ref[...] += part

        @pl.when(i == nt - 1)
        def _():
            dg_ref[...] = jnp.sum(acc_ref[...], axis=0, keepdims=True)

    blk = pl.BlockSpec((tq, w), lambda i: (i, 0))
    grow = pl.BlockSpec((1, C_DK), lambda i: (0, 0))
    return pl.pallas_call(
        body, name="gated_norm_bwd", grid=(nt,),
        in_specs=[blk, _z_spec(tq), grow, blk] + extra_specs, out_specs=[_z_spec(tq), blk, grow],
        out_shape=[_du_shape(t), jax.ShapeDtypeStruct((t, w), F32), jax.ShapeDtypeStruct((1, C_DK), F32)],
        scratch_shapes=[pltpu.VMEM((8, C_DK), F32)],
        input_output_aliases=aliases,
        compiler_params=_cparams(("arbitrary",)),
    )(o, u, gain, dy, *extra)


def _gate_specs(tq):
    return [pl.BlockSpec((tq, D_MODEL), lambda i, j=j: (i, j)) for j in range(3)]


def _merge_fwd_call(ps, u, *, tq=512):
    t, w = ps[0].shape

    def body(p0, p1, p2, g0, g1, g2, y_ref):
        y_ref[...] = (jax.nn.sigmoid(g0[...]) * p0[...] + jax.nn.sigmoid(g1[...]) * p1[...]
                      + jax.nn.sigmoid(g2[...]) * p2[...]).astype(BF16)

    blk = pl.BlockSpec((tq, w), lambda i: (i, 0))
    return pl.pallas_call(
        body, name="merge_fwd", grid=(t // tq,), in_specs=[blk] * 3 + _gate_specs(tq), out_specs=blk,
        out_shape=jax.ShapeDtypeStruct((t, w), BF16),
        compiler_params=_cparams(("parallel",)),
    )(*ps, u, u, u)


def _merge_bwd_call(ps, u, dy, *, tq=256):
    t, w = dy.shape

    def body(p0, p1, p2, g0, g1, g2, dy_ref, dg_ref, dp0, dp1, dp2):
        dyv = dy_ref[...]
        for j, (p, g, dp) in enumerate(((p0, g0, dp0), (p1, g1, dp1), (p2, g2, dp2))):
            sig = jax.nn.sigmoid(g[...])
            dp[...] = (dyv * sig).astype(BF16)
            dg_ref[:, j * w:(j + 1) * w] = (dyv * p[...] * sig * (1.0 - sig)).astype(BF16)

    blk = pl.BlockSpec((tq, w), lambda i: (i, 0))
    small = jax.ShapeDtypeStruct((t, w), BF16)
    return pl.pallas_call(
        body, name="merge_bwd", grid=(t // tq,), in_specs=[blk] * 3 + _gate_specs(tq) + [blk],
        out_specs=[pl.BlockSpec((tq, 3 * w), lambda i: (i, 0))] + [blk] * 3,
        out_shape=[_du_shape(t)] + [small] * 3,
        compiler_params=_cparams(("parallel",)),
    )(*ps, u, u, u, dy)


Q_SCALE = HEAD_DIM ** -0.5
A_PARTS = ((COL["a_q"], A_W, True, Q_SCALE), (COL["a_k"], A_W, True, 1.0), (COL["a_v"], A_W, False, 1.0))
B_PARTS = ((COL["b_q"], A_W, True, Q_SCALE), (COL["b_k"], B_KVW, True, 1.0), (COL["b_v"], B_KVW, False, 1.0))
BRANCHES = ("w_branch_a", "w_branch_b", "w_branch_c")


def _layer_fwd(x, tabs, p, w_in_b, rest, rest_of, gather_src=None):
    h = _rms_fwd_call(x, p["norm_mix"], name="rms_mix_fwd", out_dtype=BF16)
    if rest[0] == "ride":
        u, (packed,) = _mm(h, w_in_b, bias=p["b_in"], tn=IN_TN, gather_src=[rest[1]], name="in_proj_fwd_gather")
    else:
        u, packed = _mm(h, w_in_b, bias=p["b_in"], tn=IN_TN, name="in_proj_fwd"), rest[1]
    wb, conv_w = rest_of(packed)
    wb = dict(wb, w_in=w_in_b)
    p = dict(p, conv_w=conv_w)
    qkv_a = _rope_gather_call(u, tabs, A_PARTS, dils=A_DILS, name="rope_a_fwd")
    os_, lses = zip(*[_attn_fwd(cfg, qkv2, None) for cfg, qkv2 in zip(ATTN_A_CFGS, qkv_a)])
    ya = _combine_fwd_call(os_, lses, A_DILS)
    qkv_b, = _rope_gather_call(u, tabs, B_PARTS, name="rope_b_fwd")
    yb, lse_b = _attn_fwd(ATTN_B_CFG, qkv_b, p["sinks"])
    zc, qk, v = _conv_prep_fwd_call(u, p["conv_w"])
    uu, ww, qd, kd, attn, tmat, gc, beta, *gathered = _delta_prep_call(qk, v, u, p["a_log"], p["dt_bias"], gather_src)
    o, vn, st = _delta_scan_call(uu, ww, qd, kd, attn, gc)
    yc = _gated_norm_fwd_call(o, u, p["c_norm"])
    ys = (ya, yb, yc)
    ps = tuple(_mm(y, wb[n], name="branch_fwd") for y, n in zip(ys, BRANCHES))
    merged = _merge_fwd_call(ps, u)
    x1 = _mm(merged, wb["w_out"], add=x, name="out_proj_fwd")
    h2 = _rms_fwd_call(x1, p["norm_ffn"], name="rms_ffn_fwd", out_dtype=BF16)
    pre, act = _mm(h2, wb["w_ff1"], relu2_out=True, name="ffn_up")
    x2 = _mm(act, wb["w_ff2"], add=x1, name="ffn_down")
    saved = dict(x=x, h=h, u=u, qkv_a=qkv_a, os_=os_, lses=lses, b_saved=(qkv_b, yb, lse_b),
                 zc=zc, qk=qk, v=v, delta=(tmat, uu, ww, gc, beta, qd, kd, attn, vn, st), o=o, ys=ys, ps=ps,
                 merged=merged, x1=x1, h2=h2, pre=pre, act=act, p=p, wb=wb)
    return x2, saved, (gathered if gathered else None)


def _layer_bwd(s, dx2, tabs):
    g, p, wb = {}, s["p"], s["wb"]
    t = dx2.shape[0]
    dpre = _mm(dx2, wb["w_ff2"], tb=True, mul_drelu2=s["pre"], out_dtype=BF16, name="ffn_dpre")
    g["w_ff2"] = _mm(s["act"], dx2, ta=True, tk=1024, name="ffn_dw2")
    g["w_ff1"] = _mm(s["h2"], dpre, ta=True, tk=1024, name="ffn_dw1")
    dh2 = _mm(dpre, wb["w_ff1"], tb=True, name="ffn_dh")
    dx1, g["norm_ffn"] = _rms_bwd_call(s["x1"], p["norm_ffn"], dh2, add=dx2, name="rms_ffn_bwd")
    dmerged = _mm(dx1, wb["w_out"], tb=True, name="out_proj_da")
    g["w_out"] = _mm(s["merged"], dx1, ta=True, tk=1024, name="out_proj_dw")
    du, *dps = _merge_bwd_call(s["ps"], s["u"], dmerged)
    dys = []
    for y, dp, n in zip(s["ys"], dps, BRANCHES):
        dys.append(_mm(dp, wb[n], tb=True, name="branch_da"))
        g[n] = _mm(y, dp, ta=True, tk=1024, name="branch_dw")
    dya, dyb, dyc = dys
    tmat, uu, ww, gc, beta, qd, kd, attn, vn, st = s["delta"]
    du, do, g["c_norm"] = _gated_norm_bwd_call(s["o"], s["u"], p["c_norm"], dyc, du)
    ddu, ddw, dqd, dkd, dattn, dgl = _delta_scan_bwd_call(do, ww, qd, kd, attn, gc, vn, st)
    du, dqk, dv, dpar = _delta_prep_bwd_call(s["qk"], s["v"], s["u"], p["a_log"], p["dt_bias"], tmat, uu, ww, gc,
                                             beta, ddu, ddw, dqd, dkd, dattn, dgl, du)
    g["a_log"], g["dt_bias"] = dpar[0:1], dpar[1:2]
    dzc = _conv_prep_dz_call(s["zc"], dqk, dv)
    du, dconv = _conv_bwd_call(s["u"], dzc, p["conv_w"], du)
    g["conv_w"] = dconv[:C_CONV]
    no_dlse = jnp.zeros((t, LANES), F32)
    dq, dk, dv_b, dsink = _attn_bwd(ATTN_B_CFG, *s["b_saved"], p["sinks"], dyb, no_dlse)
    g["sinks"] = dsink[0, :p["sinks"].shape[0]]
    du = _rope_scatter_call(du, t, [([dq], A_W, True, Q_SCALE)], COL["b_q"], tabs, name="rope_bq_bwd")
    du = _rope_scatter_call(du, t, [([dk], B_KVW, True, 1.0), ([dv_b], B_KVW, False, 1.0)], COL["b_k"], tabs,
                            name="rope_bkv_bwd")
    *dos, dl0, dl1, dl2 = _combine_bwd_call(s["os_"], s["lses"], dya, A_DILS)
    grads_a = [_attn_bwd(cfg, qkv2, o2, lse2, None, do2, dl2_)[:3]
               for cfg, qkv2, o2, lse2, do2, dl2_ in zip(ATTN_A_CFGS, s["qkv_a"], s["os_"], s["lses"], dos,
                                                         (dl0, dl1, dl2))]
    dqs, dks, dvs = zip(*grads_a)
    du = _rope_scatter_call(du, t, [(list(dqs), A_W, True, Q_SCALE), (list(dks), A_W, True, 1.0),
                                    (list(dvs), A_W, False, 1.0)],
                            COL["a_q"], tabs, dils=A_DILS, name="rope_a_bwd")
    dh = _mm(du, wb["w_in"], tb=True, tk=IN_TN, name="in_proj_da")
    g["w_in"], g["b_in"] = _mm(s["h"], du, ta=True, b_colsum=True, tn=IN_TN, tk=1024, name="in_proj_dw")
    dx, g["norm_mix"] = _rms_bwd_call(s["x"], p["norm_mix"], dh, add=dx1, name="rms_mix_bwd")
    return dx, g


def _local_step(x, params, w_in_first, rest_first, payload_of_layer, w_in_of, rest_of, tabs, tgt):
    saves = []
    w_in_blocks, rest = w_in_first, rest_first
    for layer in range(DEPTH):
        p = {n: w[layer] for n, w in params.items() if n != "norm_final"}
        nxt = payload_of_layer(layer + 1) if layer + 1 < DEPTH else None
        x, s, gathered = _layer_fwd(x, tabs, p, w_in_of(w_in_blocks), rest, rest_of, nxt)
        saves.append(s)
        if gathered is not None:
            w_in_blocks, rest = gathered[0], ("ready", gathered[1])
    loss, dx, dfinal = _loss_call(x, params["norm_final"], tgt)
    per_layer = []
    for s in reversed(saves):
        dx, g = _layer_bwd(s, dx, tabs)
        per_layer.append(g)
    per_layer.reverse()
    grads = {n: jnp.stack([g[n] for g in per_layer]) for n in per_layer[0]}
    grads["norm_final"] = dfinal
    return loss, dx, grads


def _in_cols_to_kernel(w):
    lead = w.shape[:-1]
    parts, pos = [], 0
    for _, start, width, ref_start in IN_LAYOUT:
        if start > pos:
            parts.append(jnp.zeros(lead + (start - pos,), w.dtype))
        parts.append(w[..., ref_start:ref_start + width])
        pos = start + width
    parts.append(jnp.zeros(lead + (D_IN_PAD - pos,), w.dtype))
    return jnp.concatenate(parts, axis=-1)


def _in_cols_to_reference(w):
    by_ref = sorted(IN_LAYOUT, key=lambda e: e[3])
    return jnp.concatenate([w[..., start:start + width] for _, start, width, _ in by_ref], axis=-1)


W_IN_SHARD = 8464 // N_DEV


def _w_in_from_shards(blocks):
    lead = blocks.shape[1:-1]
    parts, pos = [], 0
    for _, start, width, ref_start in IN_LAYOUT:
        if start > pos:
            parts.append(jnp.zeros(lead + (start - pos,), blocks.dtype))
        col = ref_start
        while col < ref_start + width:
            d, l = divmod(col, W_IN_SHARD)
            n = min(W_IN_SHARD - l, ref_start + width - col)
            parts.append(blocks[d, ..., l:l + n])
            col += n
        pos = start + width
    parts.append(jnp.zeros(lead + (D_IN_PAD - pos,), blocks.dtype))
    return jnp.concatenate(parts, axis=-1)


def _w_in_to_shards(g):
    by_ref = sorted(IN_LAYOUT, key=lambda e: e[3])
    blocks = []
    for d in range(N_DEV):
        lo, hi = d * W_IN_SHARD, (d + 1) * W_IN_SHARD
        parts = []
        for _, start, width, ref_start in by_ref:
            a, b = max(lo, ref_start), min(hi, ref_start + width)
            if a < b:
                parts.append(g[..., start + a - ref_start:start + b - ref_start])
        blocks.append(jnp.concatenate(parts, axis=-1))
    return jnp.stack(blocks)


def _pad_lanes(v):
    return jnp.pad(v, ((0, 0), (0, LANES - v.shape[1])))[:, None, :]


BIG = (("w_in", 2), ("conv_w", 2), ("w_branch_a", 2), ("w_branch_b", 2), ("w_branch_c", 1), ("w_out", 1),
       ("w_ff1", 2), ("w_ff2", 1))
SMALL = ("norm_mix", "b_in", "a_log", "dt_bias", "sinks", "c_norm", "norm_ffn", "norm_final")
WEIGHTS = ("norm_mix", "w_in", "b_in", "conv_w", "a_log", "dt_bias", "sinks", "c_norm", "w_branch_a",
           "w_branch_b", "w_branch_c", "w_out", "norm_ffn", "w_ff1", "w_ff2", "norm_final")
MATMUL_WEIGHTS = ("w_in", "w_branch_a", "w_branch_b", "w_branch_c", "w_out", "w_ff1", "w_ff2")
PACK_ROWS = 1024
ROW_ALIGN = 16


def _seg_rows(n):
    return -(-n // (LANES * ROW_ALIGN)) * ROW_ALIGN


def _pack(arrays, lead=0):
    parts = []
    for a in arrays:
        lead_shape = a.shape[:lead]
        n = math.prod(a.shape[lead:])
        rows = _seg_rows(n)
        if rows * LANES != n:
            a = jnp.pad(a.reshape(lead_shape + (n,)), [(0, 0)] * lead + [(0, rows * LANES - n)])
        parts.append(a.reshape(lead_shape + (rows, LANES)))
    total = sum(p.shape[lead] for p in parts)
    padded = -(-total // PACK_ROWS) * PACK_ROWS
    if padded > total:
        parts.append(jnp.zeros(parts[0].shape[:lead] + (padded - total, LANES), parts[0].dtype))
    return jnp.concatenate(parts, axis=lead)


def _unpack(buf, shapes):
    lead = buf.shape[:-2]
    out, pos = [], 0
    for shp in shapes:
        n = math.prod(shp)
        rows = _seg_rows(n)
        seg = buf[..., pos:pos + rows, :]
        if rows * LANES != n:
            seg = seg.reshape(lead + (rows * LANES,))[..., :n]
        out.append(seg.reshape(lead + tuple(shp)))
        pos += rows
    return out


def _shards_to_full(blocks, axis):
    moved = jnp.moveaxis(blocks, 0, axis)
    shp = list(blocks.shape[1:])
    shp[axis] = shp[axis] * N_DEV
    return moved.reshape(shp)


def _full_to_shards(full, axis):
    shp = list(full.shape)
    shp[axis:axis + 1] = [N_DEV, shp[axis] // N_DEV]
    return jnp.moveaxis(full.reshape(shp), axis, 0)


def _my_place():
    return lax.axis_index("x"), lax.axis_index("y"), lax.axis_index("c")


def _slot(x, y, c):
    return 4 * x + 2 * y + c


GATHER_COPIES = 7


def _gather_plan(x_ref, out_ref, send_sems, recv_sems, local_sem, base):
    x, y, c = _my_place()
    me, sibling = (x, y, c), (x, y, 1 - c)
    chips = [(1 - x, y), (x, 1 - y), (1 - x, 1 - y)]

    def copy(k, blk, to, src=None):
        dst = out_ref.at[_slot(*blk)]
        return pltpu.make_async_remote_copy(
            src_ref=dst if src is None else src, dst_ref=dst,
            send_sem=send_sems.at[base + k], recv_sem=recv_sems.at[base + k], device_id=to, device_id_type=MESH_ID)

    def own():
        mine = pltpu.make_async_copy(x_ref, out_ref.at[_slot(*me)], local_sem)
        return mine, [copy(0, me, sibling, src=x_ref)] + [copy(1 + j, me, (*chip, c), src=x_ref)
                                                          for j, chip in enumerate(chips)]

    return copy, own, me, sibling, chips, c


def _gather_plans(srcs, outs, send_sems, recv_sems, local_sems):
    return [_gather_plan(x_ref, out_ref, send_sems, recv_sems, local_sems.at[i], GATHER_COPIES * i)
            for i, (x_ref, out_ref) in enumerate(zip(srcs, outs))]


def _gather_start(srcs, outs, *sems):
    for _, own, *_ in _gather_plans(srcs, outs, *sems):
        mine, first = own()
        mine.start()
        for cp in first:
            cp.start()


def _gather_finish(srcs, outs, *sems):
    plans = _gather_plans(srcs, outs, *sems)
    passed_all = []
    for copy, own, me, sibling, chips, c in plans:
        passed = [copy(4 + j, (*chip, c), sibling) for j, chip in enumerate(chips)]
        for j, chip in enumerate(chips):
            copy(1 + j, (*chip, c), me).wait_recv()
            passed[j].start()
        passed_all.append(passed)
    for (copy, own, me, sibling, chips, c), passed in zip(plans, passed_all):
        copy(0, sibling, me).wait_recv()
        for j, chip in enumerate(chips):
            copy(4 + j, (*chip, 1 - c), me).wait_recv()
        mine, first = own()
        for cp in first + passed:
            cp.wait_send()
        mine.wait()


def _gather_sems(n):
    return [pltpu.SemaphoreType.DMA((GATHER_COPIES * n,)), pltpu.SemaphoreType.DMA((GATHER_COPIES * n,)),
            pltpu.SemaphoreType.DMA((n,))]


def _gathered_shapes(blocks):
    return [jax.ShapeDtypeStruct((N_DEV,) + b.shape, b.dtype) for b in blocks]


def _all_gather(blocks, *, name):
    n = len(blocks)

    def body(*refs):
        srcs, outs, sems = refs[:n], refs[n:2 * n], refs[2 * n:]
        _gather_start(srcs, outs, *sems)
        _gather_finish(srcs, outs, *sems)

    return pl.pallas_call(
        body, name=name, out_shape=_gathered_shapes(blocks),
        in_specs=[HBM_SPEC] * n, out_specs=[HBM_SPEC] * n,
        scratch_shapes=_gather_sems(n),
    )(*blocks)


N_CHIP = N_DEV // 2


def _swap_with_sibling(blocks, *, name):
    n = len(blocks)

    def body(*refs):
        srcs, outs, send_sems, recv_sems = refs[:n], refs[n:2 * n], refs[2 * n], refs[2 * n + 1]
        x, y, c = _my_place()
        copies = [pltpu.make_async_remote_copy(src_ref=g_ref, dst_ref=out_ref, send_sem=send_sems.at[i],
                                               recv_sem=recv_sems.at[i], device_id=(x, y, 1 - c),
                                               device_id_type=MESH_ID)
                  for i, (g_ref, out_ref) in enumerate(zip(srcs, outs))]
        for cp in copies:
            cp.start()
        for cp in copies:
            cp.wait_recv()
        for cp in copies:
            cp.wait_send()

    return pl.pallas_call(
        body, name=name,
        out_shape=[jax.ShapeDtypeStruct(b.shape, b.dtype) for b in blocks],
        in_specs=[HBM_SPEC] * n, out_specs=[HBM_SPEC] * n,
        scratch_shapes=[pltpu.SemaphoreType.DMA((n,)), pltpu.SemaphoreType.DMA((n,))],
    )(*blocks)


def _chip_all_to_all(blocks, *, name):
    n = len(blocks)
    peers = N_CHIP - 1

    def body(*refs):
        srcs, outs = refs[:n], refs[n:2 * n]
        send_sems, recv_sems, local_sems = refs[2 * n:]
        x, y, c = _my_place()
        mine_slot = 2 * x + y
        locals_, copies = [], []
        for i, (g_ref, out_ref) in enumerate(zip(srcs, outs)):
            locals_.append(pltpu.make_async_copy(g_ref.at[mine_slot], out_ref.at[mine_slot], local_sems.at[i]))
            for k in range(1, N_CHIP):
                px, py = x ^ (k >> 1), y ^ (k & 1)
                copies.append(pltpu.make_async_remote_copy(
                    src_ref=g_ref.at[2 * px + py], dst_ref=out_ref.at[mine_slot],
                    send_sem=send_sems.at[peers * i + k - 1], recv_sem=recv_sems.at[peers * i + k - 1],
                    device_id=(px, py, c), device_id_type=MESH_ID))
        for cp in locals_ + copies:
            cp.start()
        for cp in copies:
            cp.wait_recv()
        for cp in copies:
            cp.wait_send()
        for cp in locals_:
            cp.wait()

    return pl.pallas_call(
        body, name=name,
        out_shape=[jax.ShapeDtypeStruct(b.shape, b.dtype) for b in blocks],
        in_specs=[HBM_SPEC] * n, out_specs=[HBM_SPEC] * n,
        scratch_shapes=[pltpu.SemaphoreType.DMA((peers * n,)), pltpu.SemaphoreType.DMA((peers * n,)),
                        pltpu.SemaphoreType.DMA((n,))],
    )(*blocks)


def _block_rows(rows, cols):
    tr = max(8, min(rows, PACK_ROWS * LANES // (-(-cols // LANES) * LANES) // 8 * 8))
    while rows % tr:
        tr -= 8
    return tr


def _add_bf16_call(a, b, *, name, gather_src=()):
    n, rows, cols = a.shape
    tr = _block_rows(rows, cols)
    steps = rows // tr
    ng = len(gather_src)

    def body(a_ref, b_ref, *refs):
        o_ref = refs[ng]
        if ng:
            gather_refs = (refs[:ng], refs[ng + 1:2 * ng + 1]) + tuple(refs[2 * ng + 1:])

            @pl.when(pl.program_id(0) == 0)
            def _():
                _gather_start(*gather_refs)
        o_ref[...] = (a_ref[...].astype(F32) + b_ref[...].astype(F32)).astype(BF16)
        if ng:
            @pl.when(pl.program_id(0) == steps - 1)
            def _():
                _gather_finish(*gather_refs)

    blk = pl.BlockSpec((n, tr, cols), lambda i: (0, i, 0))
    return pl.pallas_call(
        body, name=name, grid=(steps,), in_specs=[blk, blk] + [HBM_SPEC] * ng, out_specs=[blk] + [HBM_SPEC] * ng,
        out_shape=[jax.ShapeDtypeStruct(a.shape, BF16)] + _gathered_shapes(gather_src),
        scratch_shapes=_gather_sems(ng) if ng else [],
        compiler_params=_cparams(("arbitrary",) if ng else ("parallel",)),
    )(a, b, *gather_src)


def _adamw_call(parts, w, m, v, *, name):
    rows, cols = w.shape
    tr = _block_rows(rows, cols)
    n_parts = parts.shape[0]

    def body(p_ref, w_ref, m_ref, v_ref, g_ref, d_ref, nm_ref, nv_ref):
        g = p_ref[0].astype(F32)
        for s in range(1, n_parts):
            g = g + p_ref[s].astype(F32)
        nm = ADAM_B1 * m_ref[...] + (1.0 - ADAM_B1) * g
        nv = ADAM_B2 * v_ref[...] + (1.0 - ADAM_B2) * jnp.square(g)
        m_hat = nm / (1.0 - ADAM_B1 ** ADAM_STEP)
        v_hat = nv / (1.0 - ADAM_B2 ** ADAM_STEP)
        g_ref[...] = g
        nm_ref[...] = nm
        nv_ref[...] = nv
        d_ref[...] = -ADAM_LR * (m_hat / (jnp.sqrt(v_hat) + ADAM_EPS) + ADAM_WD * w_ref[...])

    blk = pl.BlockSpec((tr, cols), lambda i: (i, 0))
    shape = jax.ShapeDtypeStruct((rows, cols), F32)
    return pl.pallas_call(
        body, name=name, grid=(rows // tr,),
        in_specs=[pl.BlockSpec((n_parts, tr, cols), lambda i: (0, i, 0)), blk, blk, blk],
        out_specs=[blk] * 4, out_shape=[shape] * 4,
        compiler_params=_cparams(("parallel",)),
    )(parts, w, m, v)


def _kernel_params(full):
    return {
        "norm_mix": full["norm_mix"][:, None, :],
        "b_in": _in_cols_to_kernel(full["b_in"])[:, None, :],
        "a_log": _pad_lanes(full["a_log"]),
        "dt_bias": _pad_lanes(full["dt_bias"]),
        "sinks": full["sinks"],
        "c_norm": full["c_norm"][:, None, :],
        "norm_ffn": full["norm_ffn"][:, None, :],
        "norm_final": full["norm_final"][None, :],
    }


def _reference_grads(g):
    return {
        "norm_mix": g["norm_mix"][:, 0, :],
        "b_in": _in_cols_to_reference(g["b_in"][:, 0, :]),
        "conv_w": g["conv_w"],
        "a_log": g["a_log"][:, 0, :C_V_HEADS],
        "dt_bias": g["dt_bias"][:, 0, :C_V_HEADS],
        "sinks": g["sinks"],
        "c_norm": g["c_norm"][:, 0, :],
        "w_branch_a": g["w_branch_a"], "w_branch_b": g["w_branch_b"], "w_branch_c": g["w_branch_c"],
        "w_out": g["w_out"],
        "norm_ffn": g["norm_ffn"][:, 0, :],
        "w_ff1": g["w_ff1"], "w_ff2": g["w_ff2"],
        "norm_final": g["norm_final"][0],
    }


def kernel(x, positions, norm_mix, w_in, b_in, conv_w, a_log, dt_bias, sinks, c_norm, w_branch_a, w_branch_b, w_branch_c, w_out, norm_ffn, w_ff1, w_ff2, norm_final, loss_target, m_norm_mix, m_w_in, m_b_in, m_conv_w, m_a_log, m_dt_bias, m_sinks, m_c_norm, m_w_branch_a, m_w_branch_b, m_w_branch_c, m_w_out, m_norm_ffn, m_w_ff1, m_w_ff2, m_norm_final, v_norm_mix, v_w_in, v_b_in, v_conv_w, v_a_log, v_dt_bias, v_sinks, v_c_norm, v_w_branch_a, v_w_branch_b, v_w_branch_c, v_w_out, v_norm_ffn, v_w_ff1, v_w_ff2, v_norm_final):
    env = dict(locals())
    weights = {n: env[n] for n in WEIGHTS}
    moments_m = {n: env["m_" + n] for n in WEIGHTS}
    moments_v = {n: env["v_" + n] for n in WEIGHTS}

    axis_of = {n: axis - 1 for n, axis in BIG}

    packed_names = [n for n in MATMUL_WEIGHTS if n != "w_in"]

    def payload_of_layer(layer):
        cw = weights["conv_w"][layer]
        c1 = cw.astype(BF16)
        c2 = (cw - c1.astype(F32)).astype(BF16)
        c3 = (cw - c1.astype(F32) - c2.astype(F32)).astype(BF16)
        return [weights["w_in"][layer].astype(BF16),
                _pack([weights[n][layer].astype(BF16) for n in packed_names] + [c1, c2, c3])]

    def rest_of(packed):
        shapes = [weights[n].shape[1:] for n in packed_names] + [weights["conv_w"].shape[1:]] * 3
        blocks = _unpack(packed, shapes)
        wb = {n: _shards_to_full(blk, axis_of[n]) for n, blk in zip(packed_names, blocks)}
        return wb, _shards_to_full(sum(b.astype(F32) for b in blocks[-3:]), axis_of["conv_w"])

    tabs = rope_tables(positions[0])
    w_in_first, = _all_gather(payload_of_layer(0)[:1], name="gather_weights")
    loss, dx, dparams = _local_step(x[0], _kernel_params({n: weights[n] for n in SMALL}), w_in_first,
                                    ("ride", payload_of_layer(0)[1]), payload_of_layer, _w_in_from_shards,
                                    rest_of, tabs, loss_target[0])
    grads = _reference_grads(dparams)
    loss = lax.psum(loss, ("x", "y", "c"))

    core = lax.axis_index("c")
    rest = [(n, axis) for n, axis in BIG if n != "w_in"]
    w_in_rows = DEPTH * D_MODEL

    def by_core(shards, which):
        sh = shards.reshape((N_CHIP, 2) + shards.shape[1:])
        return lax.dynamic_index_in_dim(sh, which, axis=1, keepdims=False).astype(BF16)

    def halves(which):
        w_in_half = by_core(_w_in_to_shards(dparams["w_in"]), which).reshape(N_CHIP, w_in_rows, W_IN_SHARD)
        return [w_in_half, _pack([by_core(_full_to_shards(grads[n], axis), which) for n, axis in rest], lead=1)]

    from_sibling = _swap_with_sibling(halves(1 - core), name="scatter_grads_d2d")
    keep_w_in, keep_rest = halves(core)
    w_in_sum, small_parts = _add_bf16_call(keep_w_in, from_sibling[0], gather_src=[_pack([grads[n] for n in SMALL])],
                                           name="scatter_grads_add_gather")
    rest_sum, = _add_bf16_call(keep_rest, from_sibling[1], name="scatter_grads_add")
    w_in_parts, rest_parts = _chip_all_to_all([w_in_sum, rest_sum], name="scatter_grads_ici")

    out = {}
    results = _adamw_call(w_in_parts, *[d["w_in"].reshape(w_in_rows, W_IN_SHARD)
                                        for d in (weights, moments_m, moments_v)], name="adamw_w_in")
    for kind, buf in zip(("grad", "delta", "new_m", "new_v"), results):
        out[kind, "w_in"] = buf.reshape(weights["w_in"].shape)
    for names, parts in (([n for n, _ in rest], rest_parts), (list(SMALL), small_parts)):
        shapes = [weights[n].shape for n in names]
        packed = [_pack([d[n] for n in names]) for d in (weights, moments_m, moments_v)]
        results = _adamw_call(parts, *packed, name="adamw_" + names[0])
        for kind, buf in zip(("grad", "delta", "new_m", "new_v"), results):
            for n, arr in zip(names, _unpack(buf, shapes)):
                out[kind, n] = arr
    return (loss, dx[None], *[out[kind, n] for kind in ("grad", "delta", "new_m", "new_v") for n in WEIGHTS])
```

```python
import functools
import math

import jax
import jax.numpy as jnp
from jax import lax
from jax.experimental import pallas as pl
from jax.experimental.pallas import tpu as pltpu

F32 = jnp.float32
BF16 = jnp.bfloat16

N_DEV = 8
D_MODEL = 1024
DEPTH = 2
HEAD_DIM = 64
ROT_DIM = 16
ROPE_THETA = 500000.0
BLK = 128
NEG_INF = -1e30
EPS = 1e-6
A_CONFIGS = ((128, 1), (512, 4), (2048, 16))
B_GROUP = 4
C_QK_HEADS = 4
C_V_HEADS = 8
C_DK = 128
C_CONV = 4
CHUNK = 64
ADAM_LR = 0.001
ADAM_B1 = 0.9
ADAM_B2 = 0.999
ADAM_EPS = 1e-08
ADAM_WD = 0.01
ADAM_STEP = 10

IN_LAYOUT = (
    ("gate_a", 0, 1024, 5392), ("gate_b", 1024, 1024, 6416), ("gate_c", 2048, 1024, 7440),
    ("a_q", 3072, 512, 0), ("a_k", 3584, 512, 512), ("a_v", 4096, 512, 1024), ("b_q", 4608, 512, 1536),
    ("c_z", 5120, 1024, 4352), ("c_qkv", 6144, 2048, 2304),
    ("b_k", 8192, 128, 2048), ("b_v", 8320, 128, 2176), ("c_ab", 8448, 16, 5376),
)
COL = {name: start for name, start, _, _ in IN_LAYOUT}
D_IN_PAD = 8704
IN_TN = D_IN_PAD // 4
LANES = 128
VMEM_LIMIT = 56 * 1024 * 1024


def _cparams(sem=None):
    return pltpu.CompilerParams(dimension_semantics=sem, vmem_limit_bytes=VMEM_LIMIT)


def _relu2(t):
    return jnp.square(jnp.maximum(t, 0.0))


def _mm(a, b, *, ta=False, tb=False, bias=None, a_fn=None, mul_drelu2=None, add=None,
        out_dtype=F32, relu2_out=False, b_colsum=False, gather_src=None, tm=1024, tn=1024, tk=2048, name):
    if ta:
        kdim, m = a.shape
    else:
        m, kdim = a.shape
    n = b.shape[0] if tb else b.shape[1]
    tm, tn, tk = min(tm, m), min(tn, n), min(tk, kdim)
    assert m % tm == 0 and n % tn == 0 and kdim % tk == 0, (a.shape, b.shape, tm, tn, tk)
    nk = kdim // tk
    assert not b_colsum or (m == tm and not tb and nk > 1)
    dims = (((0 if ta else 1,), (1 if tb else 0,)), ((), ()))
    extras = [e for e in (bias, mul_drelu2, add) if e is not None]
    ng = len(gather_src) if gather_src is not None else 0
    grid = (m // tm, n // tn, nk)

    def body(*refs):
        if ng:
            n_in = 2 + len(extras)
            n_out = 1 + int(relu2_out) + int(b_colsum)
            n_scr = int(nk > 1) + int(b_colsum)
            gather_refs = (refs[n_in:n_in + ng], refs[n_in + ng + n_out:n_in + 2 * ng + n_out],
                           *refs[n_in + 2 * ng + n_out + n_scr:])
            refs = refs[:n_in] + refs[n_in + ng:n_in + ng + n_out] + refs[n_in + 2 * ng + n_out:]
            at_first = functools.reduce(jnp.logical_and, [pl.program_id(d) == 0 for d in range(3)])
            at_last = functools.reduce(jnp.logical_and, [pl.program_id(d) == grid[d] - 1 for d in range(3)])

            @pl.when(at_first)
            def _():
                _gather_start(*gather_refs)
        a_ref, b_ref = refs[0], refs[1]
        pos = 2
        bias_ref = pre_ref = add_ref = None
        if bias is not None:
            bias_ref = refs[pos]; pos += 1
        if mul_drelu2 is not None:
            pre_ref = refs[pos]; pos += 1
        if add is not None:
            add_ref = refs[pos]; pos += 1
        o_ref = refs[pos]
        pos += 1
        r_ref = None
        if relu2_out:
            r_ref = refs[pos]; pos += 1
        cs_ref = None
        if b_colsum:
            cs_ref = refs[pos]; pos += 1
        acc_ref = refs[pos] if nk > 1 else None
        cs_acc = refs[pos + 1] if b_colsum else None

        av = a_ref[...]
        if a_fn is not None:
            av = a_fn(av)
        bv = b_ref[...]
        part = lax.dot_general(av.astype(BF16), bv.astype(BF16), dims,
                               preferred_element_type=F32)
        if b_colsum:
            cs_part = jnp.sum(bv.astype(F32).reshape(tk // 8, 8, tn), axis=0)

        def finish(acc):
            if bias_ref is not None:
                acc = acc + bias_ref[...]
            if pre_ref is not None:
                acc = acc * (2.0 * jnp.maximum(pre_ref[...], 0.0))
            if add_ref is not None:
                acc = acc + add_ref[...]
            o_ref[...] = acc.astype(out_dtype)
            if r_ref is not None:
                r_ref[...] = _relu2(acc).astype(BF16)

        if nk == 1:
            finish(part)
        else:
            k = pl.program_id(2)

            @pl.when(k == 0)
            def _():
                acc_ref[...] = part
                if b_colsum:
                    cs_acc[...] = cs_part

            @pl.when(k > 0)
            def _():
                acc_ref[...] += part
                if b_colsum:
                    cs_acc[...] += cs_part

            @pl.when(k == nk - 1)
            def _():
                finish(acc_ref[...])
                if b_colsum:
                    cs_ref[...] = jnp.sum(cs_acc[...], axis=0, keepdims=True)
        if ng:
            @pl.when(at_last)
            def _():
                _gather_finish(*gather_refs)

    a_spec = (pl.BlockSpec((tk, tm), lambda i, j, k: (k, i)) if ta
              else pl.BlockSpec((tm, tk), lambda i, j, k: (i, k)))
    b_spec = (pl.BlockSpec((tn, tk), lambda i, j, k: (j, k)) if tb
              else pl.BlockSpec((tk, tn), lambda i, j, k: (k, j)))
    in_specs = [a_spec, b_spec]
    if bias is not None:
        in_specs.append(pl.BlockSpec((1, tn), lambda i, j, k: (0, j)))
    for _ in extras[(1 if bias is not None else 0):]:
        in_specs.append(pl.BlockSpec((tm, tn), lambda i, j, k: (i, j)))
    o_spec = pl.BlockSpec((tm, tn), lambda i, j, k: (i, j))
    out_specs, out_shape = [o_spec], [jax.ShapeDtypeStruct((m, n), out_dtype)]
    scratch = [pltpu.VMEM((tm, tn), F32)] if nk > 1 else []
    if relu2_out:
        out_specs.append(o_spec)
        out_shape.append(jax.ShapeDtypeStruct((m, n), BF16))
    if b_colsum:
        out_specs.append(pl.BlockSpec((1, tn), lambda i, j, k: (0, j)))
        out_shape.append(jax.ShapeDtypeStruct((1, n), F32))
        scratch.append(pltpu.VMEM((8, tn), F32))
    if ng:
        in_specs += [HBM_SPEC] * ng
        out_specs += [HBM_SPEC] * ng
        out_shape += _gathered_shapes(gather_src)
        scratch += _gather_sems(ng)
    single = len(out_specs) == 1
    outs = pl.pallas_call(
        body, name=name,
        grid=grid,
        in_specs=in_specs,
        out_specs=out_specs[0] if single else out_specs,
        out_shape=out_shape[0] if single else out_shape,
        scratch_shapes=scratch,
        compiler_params=_cparams(("arbitrary",) * 3 if ng else ("parallel", "parallel", "arbitrary")),
    )(a, b, *extras, *(gather_src or []))
    if not ng:
        return outs
    n_out = len(outs) - ng
    return (outs[0] if n_out == 1 else tuple(outs[:n_out])), list(outs[n_out:])


def _rms_fwd_call(x, g, *, name, out_dtype=F32, tq=512):
    t, d = x.shape

    def body(x_ref, g_ref, y_ref):
        xv = x_ref[...]
        r = lax.rsqrt(jnp.mean(xv * xv, axis=-1, keepdims=True) + EPS)
        y_ref[...] = (xv * r * g_ref[...]).astype(out_dtype)

    return pl.pallas_call(
        body, name=name, grid=(t // tq,),
        in_specs=[pl.BlockSpec((tq, d), lambda i: (i, 0)), pl.BlockSpec((1, d), lambda i: (0, 0))],
        out_specs=pl.BlockSpec((tq, d), lambda i: (i, 0)),
        out_shape=jax.ShapeDtypeStruct((t, d), out_dtype),
        compiler_params=_cparams(("parallel",)),
    )(x, g)


def _rms_bwd_call(x, g, dy, *, name, add=None, tq=512):
    t, d = x.shape
    nt = t // tq

    def body(*refs):
        if add is None:
            x_ref, g_ref, dy_ref, dx_ref, dg_ref, acc_ref = refs
        else:
            x_ref, g_ref, dy_ref, add_ref, dx_ref, dg_ref, acc_ref = refs
        i = pl.program_id(0)
        xv = x_ref[...]
        r = lax.rsqrt(jnp.mean(xv * xv, axis=-1, keepdims=True) + EPS)
        xh = xv * r
        dyv = dy_ref[...]
        dxh = dyv * g_ref[...]
        dx = r * (dxh - xh * jnp.mean(dxh * xh, axis=-1, keepdims=True))
        dx_ref[...] = dx if add is None else dx + add_ref[...]
        part = jnp.sum((dyv * xh).reshape(tq // 8, 8, d), axis=0)

        @pl.when(i == 0)
        def _():
            acc_ref[...] = part

        @pl.when(i > 0)
        def _():
            acc_ref[...] += part

        @pl.when(i == nt - 1)
        def _():
            dg_ref[...] = jnp.sum(acc_ref[...], axis=0, keepdims=True)

    blk = pl.BlockSpec((tq, d), lambda i: (i, 0))
    row = pl.BlockSpec((1, d), lambda i: (0, 0))
    extra = [] if add is None else [add]
    return pl.pallas_call(
        body, name=name, grid=(nt,),
        in_specs=[blk, row, blk] + [blk] * len(extra),
        out_specs=[blk, row],
        out_shape=[jax.ShapeDtypeStruct((t, d), F32), jax.ShapeDtypeStruct((1, d), F32)],
        scratch_shapes=[pltpu.VMEM((8, d), F32)],
        compiler_params=_cparams(("arbitrary",)),
    )(x, g, dy, *extra)


def _loss_call(x, g, tgt, *, tq=512):
    t, d = x.shape
    nt = t // tq

    def body(x_ref, g_ref, t_ref, loss_ref, dx_ref, dg_ref, acc_ref, sq_ref):
        i = pl.program_id(0)
        xv = x_ref[...]
        r = lax.rsqrt(jnp.mean(xv * xv, axis=-1, keepdims=True) + EPS)
        xh = xv * r
        gv = g_ref[...]
        err = xh * gv - t_ref[...]
        dyv = err * (1.0 / d)
        dxh = dyv * gv
        dx_ref[...] = r * (dxh - xh * jnp.mean(dxh * xh, axis=-1, keepdims=True))
        part = jnp.sum((dyv * xh).reshape(tq // 8, 8, d), axis=0)
        sq = jnp.sum((err * err).reshape(tq // 8, 8, d), axis=0)

        @pl.when(i == 0)
        def _():
            acc_ref[...] = part
            sq_ref[...] = sq

        @pl.when(i > 0)
        def _():
            acc_ref[...] += part
            sq_ref[...] += sq

        @pl.when(i == nt - 1)
        def _():
            dg_ref[...] = jnp.sum(acc_ref[...], axis=0, keepdims=True)
            tot = jnp.sum(jnp.sum(sq_ref[...], axis=0, keepdims=True), axis=1, keepdims=True)
            loss_ref[...] = jnp.broadcast_to(tot * (0.5 / d), (8, LANES))

    blk = pl.BlockSpec((tq, d), lambda i: (i, 0))
    row = pl.BlockSpec((1, d), lambda i: (0, 0))
    loss, dx, dg = pl.pallas_call(
        body, name="loss", grid=(nt,),
        in_specs=[blk, row, blk],
        out_specs=[pl.BlockSpec((8, LANES), lambda i: (0, 0)), blk, row],
        out_shape=[jax.ShapeDtypeStruct((8, LANES), F32), jax.ShapeDtypeStruct((t, d), F32),
                   jax.ShapeDtypeStruct((1, d), F32)],
        scratch_shapes=[pltpu.VMEM((8, d), F32), pltpu.VMEM((8, d), F32)],
        compiler_params=_cparams(("arbitrary",)),
    )(x, g, tgt)
    return loss[0, 0], dx, dg


MESH_ID = pl.DeviceIdType.MESH
HBM_SPEC = pl.BlockSpec(memory_space=pl.ANY)


def rope_tables(positions):
    half = ROT_DIM // 2
    inv_freq = jnp.power(ROPE_THETA, -jnp.arange(0, ROT_DIM, 2, dtype=F32) / ROT_DIM)
    in_head = jnp.arange(LANES) % HEAD_DIM
    rot = in_head < ROT_DIM
    freq = jnp.where(rot, inv_freq[in_head % half], 0.0)
    ang = positions.astype(F32)[:, None] * freq[None, :]
    cos, sin = jnp.cos(ang), jnp.sin(ang)
    b = jnp.where(jnp.logical_and(rot, in_head >= half)[None, :], sin, 0.0)
    c = jnp.where((in_head < half)[None, :], -sin, 0.0)
    return cos, b, c


def _rope_chunk(xs, a, b, c, transpose):
    half = ROT_DIM // 2
    if transpose:
        return xs * a + pltpu.roll(xs * b, LANES - half, 1) + pltpu.roll(xs * c, half, 1)
    return xs * a + pltpu.roll(xs, half, 1) * b + pltpu.roll(xs, LANES - half, 1) * c


def _dilated_spec(tq, w, d):
    return pl.BlockSpec((tq // d, d * w), lambda i: (i, 0))


def _load_dilated(ref, stage_ref, d, w, j):
    cs = slice(j * LANES, (j + 1) * LANES)
    if d == 1:
        return ref[:, cs].astype(F32)
    for r in range(d):
        stage_ref[pl.ds(r, ref.shape[0], stride=d), :] = ref[:, r * w + j * LANES:r * w + (j + 1) * LANES].astype(F32)
    return stage_ref[...]


def _store_dilated(ref, stage_ref, d, w, j, val):
    if d == 1:
        ref[:, j * LANES:(j + 1) * LANES] = val.astype(ref.dtype)
        return
    stage_ref[...] = val
    for r in range(d):
        rows = stage_ref[pl.ds(r, ref.shape[0], stride=d), :]
        ref[:, r * w + j * LANES:r * w + (j + 1) * LANES] = rows.astype(ref.dtype)


def _stage_buffers(tq, n):
    return [pltpu.VMEM((tq, LANES), F32)] * n


def _rope_gather_call(u, tabs, parts, *, name, dils=(1,), tq=512):
    t = u.shape[0]
    total = sum(w for _, w, _, _ in parts)
    assert all(start % w == 0 for start, w, _, _ in parts)
    n_stage = (total // LANES) * sum(d > 1 for d in dils)

    def body(a_ref, b_ref, c_ref, *refs):
        x_refs, o_refs = refs[:len(parts)], refs[len(parts):len(parts) + len(dils)]
        stages = iter(refs[len(parts) + len(dils):])
        a, b, c = a_ref[...], b_ref[...], c_ref[...]
        g = 0
        for x_ref, (_, w, roped, scale) in zip(x_refs, parts):
            for j in range(w // LANES):
                xs = x_ref[:, j * LANES:(j + 1) * LANES]
                val = _rope_chunk(xs, a, b, c, False) if roped else xs
                val = val * scale if scale != 1.0 else val
                for o_ref, d in zip(o_refs, dils):
                    _store_dilated(o_ref, next(stages) if d > 1 else None, d, total, g, val)
                g += 1

    tab_spec = pl.BlockSpec((tq, LANES), lambda i: (i, 0))
    return pl.pallas_call(
        body, name=name, grid=(t // tq,),
        in_specs=[tab_spec] * 3 + [pl.BlockSpec((tq, w), lambda i, cb=start // w: (i, cb)) for start, w, _, _ in parts],
        out_specs=[_dilated_spec(tq, total, d) for d in dils],
        out_shape=[jax.ShapeDtypeStruct((t // d, d * total), BF16) for d in dils],
        scratch_shapes=_stage_buffers(tq, n_stage),
        compiler_params=_cparams(("parallel",)),
    )(*tabs, *([u] * len(parts)))


def _du_operands(du_buf, n_inputs):
    if du_buf is None:
        return [], [], {}
    return [du_buf], [HBM_SPEC], {n_inputs: 0}


def _du_shape(t):
    return jax.ShapeDtypeStruct((t, D_IN_PAD), BF16)


def _rope_scatter_call(du_buf, t, pieces, col, tabs, *, name, dils=(1,), tq=512):
    total = sum(w for _, w, _, _ in pieces)
    assert col % total == 0 and all(len(arrs) == len(dils) for arrs, _, _, _ in pieces)
    arrays = [a for arrs, _, _, _ in pieces for a in arrs]
    extra, extra_specs, aliases = _du_operands(du_buf, 3 + len(arrays))
    n_stage = (total // LANES) * sum(d > 1 for d in dils)

    def body(a_ref, b_ref, c_ref, *refs):
        o_ref = refs[len(arrays) + len(extra)]
        stages = iter(refs[len(arrays) + len(extra) + 1:])
        a, b, c = a_ref[...], b_ref[...], c_ref[...]
        k = off = 0
        for arrs, w, roped, scale in pieces:
            mine = refs[k:k + len(arrs)]
            k += len(arrs)
            for j in range(w // LANES):
                xs = None
                for r, d in zip(mine, dils):
                    part = _load_dilated(r, next(stages) if d > 1 else None, d, w, j)
                    xs = part if xs is None else xs + part
                if scale != 1.0:
                    xs = xs * scale
                val = _rope_chunk(xs, a, b, c, True) if roped else xs
                o_ref[:, off + j * LANES:off + (j + 1) * LANES] = val.astype(BF16)
            off += w

    tab_spec = pl.BlockSpec((tq, LANES), lambda i: (i, 0))
    in_specs = [tab_spec] * 3 + [_dilated_spec(tq, w, d) for _, w, _, _ in pieces for d in dils]
    return pl.pallas_call(
        body, name=name, grid=(t // tq,),
        in_specs=in_specs + extra_specs,
        out_specs=pl.BlockSpec((tq, total), lambda i: (i, col // total)),
        out_shape=_du_shape(t), input_output_aliases=aliases,
        scratch_shapes=_stage_buffers(tq, n_stage),
        compiler_params=_cparams(("parallel",)),
    )(*tabs, *arrays, *extra)


def _band_masks(first_block, max_dist):
    qi = lax.broadcasted_iota(jnp.int32, (BLK, BLK), 0)
    kj = lax.broadcasted_iota(jnp.int32, (BLK, BLK), 1)
    valid_prev = jnp.logical_and(kj >= qi + (BLK - max_dist), jnp.logical_not(first_block))
    valid_cur = kj <= qi
    return valid_prev, valid_cur


_NN = (((1,), (0,)), ((), ()))
_NT = (((1,), (1,)), ((), ()))
_TN = (((0,), (0,)), ((), ()))


HEAD_STAGE = 8


def _attn_row_maps(nb):
    def cur(i):
        return jnp.minimum(i, nb - 1)

    def prev(i):
        return jnp.maximum(jnp.minimum(i, nb - 1) - 1, 0)

    return cur, prev


def _dil_spec(w, dil, rows, seg=None, off=0):
    seg = w if seg is None else seg
    assert off % w == 0 and (dil == 1 or seg % w == 0)
    return pl.BlockSpec((BLK, w), lambda r, i: (rows(i), (r * seg + off) // w))


def _dil_shape(l, dil, w, dtype=F32):
    return jax.ShapeDtypeStruct((l, dil * w), dtype)


def _attn_fwd_call(qkv2, sink, *, dil, group, max_dist, seg, offs, qw, kw, name):
    l = qkv2.shape[0]
    nh = qw // HEAD_DIM
    nb = l // BLK
    use_sink = sink is not None

    def body(*refs):
        if use_sink:
            sink_ref, refs = refs[0], refs[1:]
        q_ref, kp_ref, kc_ref, vp_ref, vc_ref, o_ref, lse_ref = refs
        valid_prev, valid_cur = _band_masks(pl.program_id(1) == 0, max_dist)
        lane = lax.broadcasted_iota(jnp.int32, (BLK, LANES), 1)
        lse_tile = jnp.zeros((BLK, LANES), F32)

        def dot(a, b, dims=_NN):
            return lax.dot_general(a, b, dims, preferred_element_type=F32)

        for g0 in range(0, nh, HEAD_STAGE):
            heads = list(range(g0, min(g0 + HEAD_STAGE, nh)))
            kv = {}
            for kh in sorted({h // group for h in heads}):
                ks = slice(kh * HEAD_DIM, (kh + 1) * HEAD_DIM)
                kv[kh] = tuple(ref[:, ks].astype(BF16) for ref in (kp_ref, kc_ref, vp_ref, vc_ref))
            qs = [q_ref[:, h * HEAD_DIM:(h + 1) * HEAD_DIM].astype(BF16) for h in heads]
            sps = [jnp.where(valid_prev, dot(qh, kv[h // group][0], _NT), NEG_INF) for h, qh in zip(heads, qs)]
            scs = [jnp.where(valid_cur, dot(qh, kv[h // group][1], _NT), NEG_INF) for h, qh in zip(heads, qs)]
            ms = [jnp.maximum(jnp.max(sp, axis=1, keepdims=True), jnp.max(sc, axis=1, keepdims=True))
                  for sp, sc in zip(sps, scs)]
            if use_sink:
                ms = [jnp.maximum(m, sink_ref[h]) for h, m in zip(heads, ms)]
            pps = [jnp.exp(sp - m) for sp, m in zip(sps, ms)]
            pcs = [jnp.exp(sc - m) for sc, m in zip(scs, ms)]
            dens = [jnp.sum(pp, axis=1, keepdims=True) + jnp.sum(pc, axis=1, keepdims=True)
                    for pp, pc in zip(pps, pcs)]
            if use_sink:
                dens = [den + jnp.exp(sink_ref[h] - m) for h, den, m in zip(heads, dens, ms)]
            outs = [dot(pp.astype(BF16), kv[h // group][2]) + dot(pc.astype(BF16), kv[h // group][3])
                    for h, pp, pc in zip(heads, pps, pcs)]
            for h, o, den, m in zip(heads, outs, dens, ms):
                o_ref[:, h * HEAD_DIM:(h + 1) * HEAD_DIM] = o / den
                lse_tile = jnp.where(lane == h, m + jnp.log(den), lse_tile)
        lse_ref[...] = lse_tile

    cur, prev = _attn_row_maps(nb)
    o_spec, lse_spec = _dil_spec(qw, dil, cur), _dil_spec(LANES, dil, cur)
    in_specs = [_dil_spec(qw, dil, cur, seg, offs[0]),
                _dil_spec(kw, dil, prev, seg, offs[1]), _dil_spec(kw, dil, cur, seg, offs[1]),
                _dil_spec(kw, dil, prev, seg, offs[2]), _dil_spec(kw, dil, cur, seg, offs[2])]
    args = [qkv2] * 5
    if use_sink:
        in_specs = [pl.BlockSpec(memory_space=pltpu.SMEM)] + in_specs
        args = [sink] + args
    return pl.pallas_call(
        body, name=name, grid=(dil, nb),
        in_specs=in_specs,
        out_specs=[o_spec, lse_spec],
        out_shape=[_dil_shape(l, dil, qw), _dil_shape(l, dil, LANES)],
        compiler_params=_cparams(("parallel", "parallel")),
    )(*args)


def _attn_bwd_call(qkv2, sink, o2, lse2, do2, dlse2, *, dil, group, max_dist, seg, offs, qw, kw, name):
    l = qkv2.shape[0]
    nh = qw // HEAD_DIM
    nb = l // BLK
    use_sink = sink is not None

    def body(*refs):
        if use_sink:
            sink_ref, refs = refs[0], refs[1:]
        (q_ref, kp_ref, kc_ref, vp_ref, vc_ref, o_ref, lse_ref, do_ref, dlse_ref,
         dq_ref, dk_ref, dv_ref, dsink_ref, ck_ref, cv_ref) = refs
        step = pl.program_id(1)

        @pl.when(jnp.logical_and(pl.program_id(0) == 0, step == 0))
        def _():
            dsink_ref[...] = jnp.zeros_like(dsink_ref)

        @pl.when(step == 0)
        def _():
            ck_ref[...] = jnp.zeros_like(ck_ref)
            cv_ref[...] = jnp.zeros_like(cv_ref)

        def dot(a, b, dims=_NN):
            return lax.dot_general(a, b, dims, preferred_element_type=F32)

        @pl.when(step < nb)
        def _():
            valid_prev, valid_cur = _band_masks(step == 0, max_dist)
            row = lax.broadcasted_iota(jnp.int32, (8, LANES), 0)
            lanes8 = lax.broadcasted_iota(jnp.int32, (8, LANES), 1)
            ds_tile = jnp.zeros((8, LANES), F32)
            for g0 in range(0, nh, HEAD_STAGE):
                heads = list(range(g0, min(g0 + HEAD_STAGE, nh)))
                hss = [slice(h * HEAD_DIM, (h + 1) * HEAD_DIM) for h in heads]
                kv = {}
                for kh in sorted({h // group for h in heads}):
                    ks = slice(kh * HEAD_DIM, (kh + 1) * HEAD_DIM)
                    kv[kh] = tuple(ref[:, ks].astype(BF16) for ref in (kp_ref, kc_ref, vp_ref, vc_ref))
                qs = [q_ref[:, hs].astype(BF16) for hs in hss]
                dos = [do_ref[:, hs] for hs in hss]
                dobs = [d.astype(BF16) for d in dos]
                lses = [lse_ref[:, h:h + 1] for h in heads]
                sps = [dot(qh, kv[h // group][0], _NT) for h, qh in zip(heads, qs)]
                scs = [dot(qh, kv[h // group][1], _NT) for h, qh in zip(heads, qs)]
                dpps = [dot(dob, kv[h // group][2], _NT) for h, dob in zip(heads, dobs)]
                dpcs = [dot(dob, kv[h // group][3], _NT) for h, dob in zip(heads, dobs)]
                pps = [jnp.where(valid_prev, jnp.exp(jnp.where(valid_prev, sp, NEG_INF) - ls), 0.0)
                       for sp, ls in zip(sps, lses)]
                pcs = [jnp.where(valid_cur, jnp.exp(jnp.where(valid_cur, sc, NEG_INF) - ls), 0.0)
                       for sc, ls in zip(scs, lses)]
                deltas = [jnp.sum(d * o_ref[:, hs], axis=1, keepdims=True) for d, hs in zip(dos, hss)]
                corrs = [dlse_ref[:, h:h + 1] - dl for h, dl in zip(heads, deltas)]
                dsps = [(pp * (dp + c)).astype(BF16) for pp, dp, c in zip(pps, dpps, corrs)]
                dscs = [(pc * (dp + c)).astype(BF16) for pc, dp, c in zip(pcs, dpcs, corrs)]
                for h, hs, dsp, dsc in zip(heads, hss, dsps, dscs):
                    dq = dot(dsp, kv[h // group][0]) + dot(dsc, kv[h // group][1])
                    dq_ref[:, hs] = dq.astype(BF16)
                parts = [(dot(dsc, qh, _TN), dot(dsp, qh, _TN),
                          dot(pc.astype(BF16), dob, _TN), dot(pp.astype(BF16), dob, _TN))
                         for dsc, dsp, qh, pc, pp, dob in zip(dscs, dsps, qs, pcs, pps, dobs)]
                for kh in kv:
                    ks = slice(kh * HEAD_DIM, (kh + 1) * HEAD_DIM)
                    mine = [p for h, p in zip(heads, parts) if h // group == kh]
                    dkc, dkp, dvc, dvp = (sum(p[j] for p in mine[1:]) + mine[0][j] for j in range(4))
                    dk_ref[:, ks] = (ck_ref[:, ks] + dkp).astype(BF16)
                    dv_ref[:, ks] = (cv_ref[:, ks] + dvp).astype(BF16)
                    ck_ref[:, ks] = dkc
                    cv_ref[:, ks] = dvc
                if use_sink:
                    for h, ls, dl in zip(heads, lses, deltas):
                        val = -jnp.sum(jnp.exp(sink_ref[h] - ls) * dl, axis=0, keepdims=True)
                        ds_tile = jnp.where(jnp.logical_and(row == 0, lanes8 == h), val, ds_tile)
            if use_sink:
                dsink_ref[...] += ds_tile

        @pl.when(step == nb)
        def _():
            dk_ref[...] = ck_ref[...].astype(BF16)
            dv_ref[...] = cv_ref[...].astype(BF16)

    cur, prev = _attn_row_maps(nb)
    q_spec, lse_spec = _dil_spec(qw, dil, cur), _dil_spec(LANES, dil, cur)
    lag_spec = _dil_spec(kw, dil, lambda i: jnp.maximum(i - 1, 0))
    in_specs = [_dil_spec(qw, dil, cur, seg, offs[0]),
                _dil_spec(kw, dil, prev, seg, offs[1]), _dil_spec(kw, dil, cur, seg, offs[1]),
                _dil_spec(kw, dil, prev, seg, offs[2]), _dil_spec(kw, dil, cur, seg, offs[2]),
                q_spec, lse_spec, q_spec, lse_spec]
    args = [qkv2] * 5 + [o2, lse2, do2, dlse2]
    if use_sink:
        in_specs = [pl.BlockSpec(memory_space=pltpu.SMEM)] + in_specs
        args = [sink] + args
    kv_shape = _dil_shape(l, dil, kw, BF16)
    return pl.pallas_call(
        body, name=name, grid=(dil, nb + 1),
        in_specs=in_specs,
        out_specs=[q_spec, lag_spec, lag_spec, pl.BlockSpec((8, LANES), lambda r, i: (0, 0))],
        out_shape=[_dil_shape(l, dil, qw, BF16), kv_shape, kv_shape,
                   jax.ShapeDtypeStruct((8, LANES), F32)],
        scratch_shapes=[pltpu.VMEM((BLK, kw), F32), pltpu.VMEM((BLK, kw), F32)],
        compiler_params=_cparams(("arbitrary", "arbitrary")),
    )(*args)


def _attn_config(tag, dil, group, max_dist, seg, offs, qw, kw):
    return dict(name=tag, dil=dil, group=group, max_dist=max_dist, seg=seg, offs=offs, qw=qw, kw=kw)


A_W = 8 * HEAD_DIM
ATTN_A_CFGS = tuple(_attn_config("attn_a%d" % dil, dil, 1, window // dil, 3 * A_W, (0, A_W, 2 * A_W), A_W, A_W)
                    for window, dil in A_CONFIGS)
B_KVW = 2 * HEAD_DIM
A_DILS = tuple(cfg["dil"] for cfg in ATTN_A_CFGS)
ATTN_B_CFG = _attn_config("attn_b", 1, B_GROUP, BLK - 1, A_W + 2 * B_KVW, (0, A_W, A_W + B_KVW), A_W, B_KVW)


def _attn_fwd(cfg, qkv2, sink):
    kw = {k: v for k, v in cfg.items() if k != "name"}
    return _attn_fwd_call(qkv2, sink, name=cfg["name"] + "_fwd", **kw)


def _attn_bwd(cfg, qkv2, o2, lse2, sink, do2, dlse2):
    kw = {k: v for k, v in cfg.items() if k != "name"}
    return _attn_bwd_call(qkv2, sink, o2, lse2, do2, dlse2, name=cfg["name"] + "_bwd", **kw)


def _head_expand():
    r = lax.broadcasted_iota(jnp.int32, (LANES, 8 * HEAD_DIM), 0)
    c = lax.broadcasted_iota(jnp.int32, (LANES, 8 * HEAD_DIM), 1)
    return (c // HEAD_DIM == r).astype(F32)


def _combine_weights(l0, l1, l2):
    m = jnp.maximum(jnp.maximum(l0, l1), l2)
    e0, e1, e2 = jnp.exp(l0 - m), jnp.exp(l1 - m), jnp.exp(l2 - m)
    inv = 1.0 / (e0 + e1 + e2)
    return e0 * inv, e1 * inv, e2 * inv


def _combine_fwd_call(os_, lses, dils, *, tq=512):
    w = os_[0].shape[1] // dils[0]
    t = os_[0].shape[0] * dils[0]
    groups = w // LANES
    n_stage = (groups + 1) * sum(d > 1 for d in dils)

    def body(*refs):
        o_refs, l_refs, y_ref = refs[:3], refs[3:6], refs[6]
        stages = iter(refs[7:])
        ws = _combine_weights(*[_load_dilated(l, next(stages) if d > 1 else None, d, LANES, 0)
                                for l, d in zip(l_refs, dils)])
        e = _head_expand()
        wide = [_dot_mask(e, wt, mask_left=False) for wt in ws]
        for j in range(groups):
            cs = slice(j * LANES, (j + 1) * LANES)
            y = None
            for o_ref, d, wd in zip(o_refs, dils, wide):
                term = wd[:, cs] * _load_dilated(o_ref, next(stages) if d > 1 else None, d, w, j)
                y = term if y is None else y + term
            y_ref[:, cs] = y

    return pl.pallas_call(
        body, name="combine_fwd", grid=(t // tq,),
        in_specs=[_dilated_spec(tq, w, d) for d in dils] + [_dilated_spec(tq, LANES, d) for d in dils],
        out_specs=pl.BlockSpec((tq, w), lambda i: (i, 0)),
        out_shape=jax.ShapeDtypeStruct((t, w), F32),
        scratch_shapes=_stage_buffers(tq, n_stage),
        compiler_params=_cparams(("parallel",)),
    )(*os_, *lses)


def _combine_bwd_call(os_, lses, dy, dils, *, tq=512):
    t, w = dy.shape
    groups = w // LANES
    n_stage = 2 * (groups + 1) * sum(d > 1 for d in dils)

    def body(*refs):
        o_refs, l_refs, dy_ref, do_refs, dl_refs = refs[:3], refs[3:6], refs[6], refs[7:10], refs[10:13]
        stages = iter(refs[13:])

        def stage(d):
            return next(stages) if d > 1 else None

        ws = _combine_weights(*[_load_dilated(l, stage(d), d, LANES, 0) for l, d in zip(l_refs, dils)])
        e = _head_expand()
        dyv = dy_ref[...]
        dws = []
        for o_ref, do_ref, d, wt in zip(o_refs, do_refs, dils, ws):
            do = _dot_mask(e, wt, mask_left=False) * dyv
            for j in range(groups):
                _store_dilated(do_ref, stage(d), d, w, j, do[:, j * LANES:(j + 1) * LANES])
            ov = jnp.concatenate([_load_dilated(o_ref, stage(d), d, w, j) for j in range(groups)], axis=1)
            dws.append(_dot_mask(e, dyv * ov, _NT, mask_left=False))
        mean = ws[0] * dws[0] + ws[1] * dws[1] + ws[2] * dws[2]
        for dl_ref, d, wt, dw in zip(dl_refs, dils, ws, dws):
            _store_dilated(dl_ref, stage(d), d, LANES, 0, wt * (dw - mean))

    o_specs = [_dilated_spec(tq, w, d) for d in dils]
    l_specs = [_dilated_spec(tq, LANES, d) for d in dils]
    return pl.pallas_call(
        body, name="combine_bwd", grid=(t // tq,),
        in_specs=o_specs + l_specs + [pl.BlockSpec((tq, w), lambda i: (i, 0))], out_specs=o_specs + l_specs,
        out_shape=[jax.ShapeDtypeStruct((t // d, d * w), F32) for d in dils]
        + [jax.ShapeDtypeStruct((t // d, d * LANES), F32) for d in dils],
        scratch_shapes=_stage_buffers(tq, n_stage),
        compiler_params=_cparams(("parallel",)),
    )(*os_, *lses, dy)


C_QKW = C_QK_HEADS * C_DK
C_CONV_W = 2 * C_QKW + C_V_HEADS * C_DK
HALO = 8


def _silu_parts(z):
    sig = jax.nn.sigmoid(z)
    return z * sig, sig * (1.0 + z * (1.0 - sig))


def _conv_window_specs(tq, t):
    c = C_CONV_W
    cb = COL["c_qkv"] // c
    blk = pl.BlockSpec((tq, c), lambda i: (i, cb))
    before = pl.BlockSpec((HALO, c), lambda i: (jnp.maximum(i * (tq // HALO) - 1, 0), cb))
    return c, cb, blk, before


def _conv_prep_fwd_call(u, w, *, tq=512):
    t = u.shape[0]
    c, _, x_spec, halo_spec = _conv_window_specs(tq, t)
    nqk = 2 * C_QK_HEADS

    def body(x_ref, halo_ref, w_ref, z_ref, qk_ref, v_ref):
        i = pl.program_id(0)
        halo = jnp.where(i == 0, 0.0, halo_ref[...])
        xc = jnp.concatenate([halo, x_ref[...]], axis=0)
        wv = w_ref[...]
        z = xc[HALO - 3:HALO - 3 + tq] * wv[0:1]
        for j in range(1, C_CONV):
            z = z + xc[HALO - 3 + j:HALO - 3 + j + tq] * wv[j:j + 1]
        z_ref[...] = z
        act, _ = _silu_parts(z)
        for h in range(nqk):
            a = act[:, h * C_DK:(h + 1) * C_DK]
            qk_ref[:, h * C_DK:(h + 1) * C_DK] = a * lax.rsqrt(jnp.sum(a * a, axis=1, keepdims=True) + EPS)
        v_ref[...] = act[:, nqk * C_DK:]

    return pl.pallas_call(
        body, name="conv_prep_fwd", grid=(t // tq,),
        in_specs=[x_spec, halo_spec, pl.BlockSpec((C_CONV, c), lambda i: (0, 0))],
        out_specs=[pl.BlockSpec((tq, c), lambda i: (i, 0)),
                   pl.BlockSpec((tq, 2 * C_QKW), lambda i: (i, 0)),
                   pl.BlockSpec((tq, c - 2 * C_QKW), lambda i: (i, 0))],
        out_shape=[jax.ShapeDtypeStruct((t, c), F32), jax.ShapeDtypeStruct((t, 2 * C_QKW), F32),
                   jax.ShapeDtypeStruct((t, c - 2 * C_QKW), F32)],
        compiler_params=_cparams(("parallel",)),
    )(u, u, w)


def _conv_prep_dz_call(z, dqk, dv, *, tq=512):
    t, c = z.shape
    nqk = 2 * C_QK_HEADS

    def body(z_ref, dqk_ref, dv_ref, dz_ref):
        zv = z_ref[...]
        act, dact = _silu_parts(zv)
        for h in range(nqk):
            hs = slice(h * C_DK, (h + 1) * C_DK)
            a = act[:, hs]
            r = lax.rsqrt(jnp.sum(a * a, axis=1, keepdims=True) + EPS)
            nrm = a * r
            dn = dqk_ref[:, hs]
            da = r * (dn - nrm * jnp.sum(dn * nrm, axis=1, keepdims=True))
            dz_ref[:, hs] = da * dact[:, hs]
        dz_ref[:, nqk * C_DK:] = dv_ref[...] * dact[:, nqk * C_DK:]

    return pl.pallas_call(
        body, name="conv_prep_dz", grid=(t // tq,),
        in_specs=[pl.BlockSpec((tq, c), lambda i: (i, 0)),
                  pl.BlockSpec((tq, 2 * C_QKW), lambda i: (i, 0)),
                  pl.BlockSpec((tq, c - 2 * C_QKW), lambda i: (i, 0))],
        out_specs=pl.BlockSpec((tq, c), lambda i: (i, 0)),
        out_shape=jax.ShapeDtypeStruct((t, c), F32),
        compiler_params=_cparams(("parallel",)),
    )(z, dqk, dv)


def _conv_bwd_call(u, dz, w, du_buf, *, tq=512):
    t = u.shape[0]
    nt = t // tq
    c, cb, x_spec, halo_spec = _conv_window_specs(tq, t)
    extra, extra_specs, aliases = _du_operands(du_buf, 5)

    def body(x_ref, xh_ref, dz_ref, dzh_ref, w_ref, *refs):
        dx_ref, dw_ref = refs[len(extra):]
        i = pl.program_id(0)
        xc = jnp.concatenate([jnp.where(i == 0, 0.0, xh_ref[...]), x_ref[...]], axis=0)
        dzv = dz_ref[...]
        dzc = jnp.concatenate([dzv, jnp.where(i == nt - 1, 0.0, dzh_ref[...])], axis=0)
        wv = w_ref[...]
        dx = dzv * wv[3:4]
        for s in range(1, C_CONV):
            dx = dx + dzc[s:s + tq] * wv[3 - s:4 - s]
        dx_ref[...] = dx.astype(BF16)
        row = lax.broadcasted_iota(jnp.int32, (8, c), 0)
        dw = jnp.zeros((8, c), F32)
        for j in range(C_CONV):
            prod = dzv * xc[HALO - 3 + j:HALO - 3 + j + tq]
            col = jnp.sum(jnp.sum(prod.reshape(tq // 8, 8, c), axis=0), axis=0, keepdims=True)
            dw = jnp.where(row == j, col, dw)

        @pl.when(i == 0)
        def _():
            dw_ref[...] = dw

        @pl.when(i > 0)
        def _():
            dw_ref[...] += dw

    blk = pl.BlockSpec((tq, c), lambda i: (i, 0))
    after = pl.BlockSpec((HALO, c), lambda i: (jnp.minimum((i + 1) * (tq // HALO), t // HALO - 1), 0))
    return pl.pallas_call(
        body, name="conv_bwd", grid=(nt,),
        in_specs=[x_spec, halo_spec, blk, after, pl.BlockSpec((C_CONV, c), lambda i: (0, 0))] + extra_specs,
        out_specs=[pl.BlockSpec((tq, c), lambda i: (i, cb)), pl.BlockSpec((8, c), lambda i: (0, 0))],
        out_shape=[_du_shape(t), jax.ShapeDtypeStruct((8, c), F32)],
        input_output_aliases=aliases,
        compiler_params=_cparams(("arbitrary",)),
    )(u, u, dz, dz, w, *extra)


C_VW = C_V_HEADS * C_DK


def _softplus(x):
    return jnp.maximum(x, 0.0) + jnp.log(1.0 + jnp.exp(-jnp.abs(x)))


def _tri_masks():
    r = lax.broadcasted_iota(jnp.int32, (CHUNK, CHUNK), 0)
    c = lax.broadcasted_iota(jnp.int32, (CHUNK, CHUNK), 1)
    return r >= c, r > c


def _split_bf16(a):
    hi = a.astype(BF16)
    return hi, (a - hi.astype(F32)).astype(BF16)


def _dot_hi(a, b, dims=None):
    dims = _NN if dims is None else dims
    ah, al = _split_bf16(a)
    bh, bl = _split_bf16(b)

    def d(x, y):
        return lax.dot_general(x, y, dims, preferred_element_type=F32)

    return d(ah, bh) + (d(ah, bl) + d(al, bh))


def _dot_mask(mask, b, dims=None, mask_left=True):
    dims = _NN if dims is None else dims
    mb = mask.astype(BF16)
    b1 = b.astype(BF16)
    rest = b - b1.astype(F32)
    b2 = rest.astype(BF16)
    b3 = (rest - b2.astype(F32)).astype(BF16)
    out = None
    for p in (b1, b2, b3):
        term = (lax.dot_general(mb, p, dims, preferred_element_type=F32) if mask_left
                else lax.dot_general(p, mb, dims, preferred_element_type=F32))
        out = term if out is None else out + term
    return out


def _unit_lower_inverses(mats):
    r = lax.broadcasted_iota(jnp.int32, (CHUNK, CHUNK), 0)
    c = lax.broadcasted_iota(jnp.int32, (CHUNK, CHUNK), 1)
    eye = (r == c).astype(F32)
    xs = [eye - a for a in mats]
    ps = [_dot_hi(a, a) for a in mats]
    steps = int(math.log2(CHUNK)) - 1
    for s in range(steps):
        xs = [x + _dot_hi(x, p) for x, p in zip(xs, ps)]
        if s < steps - 1:
            ps = [_dot_hi(p, p) for p in ps]
    return xs


def _gate_tiles(cab, alog, dtb):
    pre = cab + dtb
    g = -jnp.exp(alog) * _softplus(pre)
    beta = jax.nn.sigmoid(pltpu.roll(cab, LANES - C_V_HEADS, 1))
    return g, beta, pre


def _chunk_common(kk, qk, gc, gct, beta, h, tri, strict):
    gcol, grow, bcol = gc[:, h:h + 1], gct[h:h + 1, :], beta[:, h:h + 1]
    decay = jnp.where(tri, jnp.exp(jnp.where(tri, gcol - grow, 0.0)), 0.0)
    kkd = jnp.where(strict, kk * decay, 0.0)
    attn = jnp.where(tri, qk * decay, 0.0)
    glast = gc[CHUNK - 1:CHUNK, h:h + 1]
    return gcol, bcol, decay, kkd, attn, glast


def _cab_spec():
    return pl.BlockSpec((CHUNK, LANES), lambda n: (n, COL["c_ab"] // LANES))


def _delta_prep_call(qk, v, u, alog, dtb, gather_src=None):
    t = qk.shape[0]
    nc = t // CHUNK
    scale = C_DK ** -0.5
    riding = gather_src is not None
    ng = len(gather_src) if riding else 0

    def body(q_ref, k_ref, v_ref, cab_ref, alog_ref, dtb_ref, *refs):
        if riding:
            gather_refs = (refs[:ng], refs[ng + 8:2 * ng + 8]) + tuple(refs[2 * ng + 8:])
            refs = refs[ng:ng + 8]

            @pl.when(pl.program_id(0) == 0)
            def _():
                _gather_start(*gather_refs)
        u_ref, w_ref, qd_ref, kd_ref, attn_ref, tmat_ref, gc_ref, beta_ref = refs
        tri, strict = _tri_masks()
        g, beta, _ = _gate_tiles(cab_ref[...], alog_ref[...], dtb_ref[...])
        gc = _dot_mask(tri, g)
        gct = gc.T
        gc_ref[...] = gc
        beta_ref[...] = beta
        mats, rhs = [], []
        for j in range(C_QK_HEADS):
            js = slice(j * C_DK, (j + 1) * C_DK)
            kf, qf = k_ref[:, js], q_ref[:, js] * scale
            kb, qb = kf.astype(BF16), qf.astype(BF16)
            kk = lax.dot_general(kb, kb, _NT, preferred_element_type=F32)
            qk = lax.dot_general(qb, kb, _NT, preferred_element_type=F32)
            for h in (2 * j, 2 * j + 1):
                hs = slice(h * C_DK, (h + 1) * C_DK)
                gcol, bcol, decay, kkd, attn, glast = _chunk_common(kk, qk, gc, gct, beta, h, tri, strict)
                gexp = jnp.exp(gcol)
                mats.append(kkd * bcol)
                rhs.append(jnp.concatenate([v_ref[:, hs] * bcol, kf * (bcol * gexp)], axis=1))
                qd_ref[:, hs] = (qf * gexp).astype(BF16)
                kd_ref[:, hs] = (kf * jnp.exp(glast - gcol)).astype(BF16)
                attn_ref[:, h * CHUNK:(h + 1) * CHUNK] = attn.astype(BF16)
        for h, (tmat, r) in enumerate(zip(_unit_lower_inverses(mats), rhs)):
            hs = slice(h * C_DK, (h + 1) * C_DK)
            uw = _dot_hi(tmat, r)
            u_ref[:, hs] = uw[:, :C_DK]
            w_ref[:, hs] = uw[:, C_DK:]
            tmat_ref[:, h * CHUNK:(h + 1) * CHUNK] = tmat.T
        if riding:
            @pl.when(pl.program_id(0) == nc - 1)
            def _():
                _gather_finish(*gather_refs)

    def blk(w):
        return pl.BlockSpec((CHUNK, w), lambda n: (n, 0))

    row = pl.BlockSpec((1, LANES), lambda n: (0, 0))
    big = jax.ShapeDtypeStruct((t, C_VW), F32)
    sq = jax.ShapeDtypeStruct((t, C_V_HEADS * CHUNK), F32)
    tile = jax.ShapeDtypeStruct((t, LANES), F32)
    half = jax.ShapeDtypeStruct((t, C_VW), BF16)
    in_specs = [blk(C_QKW), pl.BlockSpec((CHUNK, C_QKW), lambda n: (n, 1)), blk(C_VW), _cab_spec(), row, row]
    out_specs = [blk(C_VW)] * 4 + [blk(C_V_HEADS * CHUNK)] * 2 + [blk(LANES)] * 2
    out_shape = [big, big, half, half, jax.ShapeDtypeStruct(sq.shape, BF16), sq] + [tile] * 2
    args = [qk, qk, v, u, alog, dtb]
    if riding:
        in_specs += [HBM_SPEC] * ng
        out_specs += [HBM_SPEC] * ng
        out_shape += _gathered_shapes(gather_src)
        args += list(gather_src)
    return pl.pallas_call(
        body, name="delta_prep_gather" if riding else "delta_prep", grid=(nc,),
        in_specs=in_specs, out_specs=out_specs, out_shape=out_shape,
        scratch_shapes=_gather_sems(ng) if riding else [],
        compiler_params=_cparams(("arbitrary",) if riding else ("parallel",)),
    )(*args)


SCAN_SUB = 4


def _delta_scan_call(u, w, qd, kd, attn, gc):
    t = u.shape[0]
    nc = t // CHUNK
    rows = SCAN_SUB * CHUNK

    def body(u_ref, w_ref, qd_ref, kd_ref, attn_ref, gc_ref, o_ref, vn_ref, st_ref, s_ref):
        @pl.when(pl.program_id(0) == 0)
        def _():
            s_ref[...] = jnp.zeros_like(s_ref)

        hss = [slice(h * C_DK, (h + 1) * C_DK) for h in range(C_V_HEADS)]
        states = [s_ref[hs, :] for hs in hss]
        for c in range(SCAN_SUB):
            rs = slice(c * CHUNK, (c + 1) * CHUNK)
            for hs, s in zip(hss, states):
                st_ref[c, hs, :] = s
            sbs = [s.astype(BF16) for s in states]
            vns = [u_ref[rs, hs] - jnp.dot(w_ref[rs, hs].astype(BF16), sb, preferred_element_type=F32)
                   for hs, sb in zip(hss, sbs)]
            qss = [jnp.dot(qd_ref[rs, hs].astype(BF16), sb, preferred_element_type=F32) for hs, sb in zip(hss, sbs)]
            vnbs = [vn.astype(BF16) for vn in vns]
            for h, hs in enumerate(hss):
                vn_ref[rs, hs] = vnbs[h]
                o_ref[rs, hs] = qss[h] + jnp.dot(attn_ref[rs, h * CHUNK:(h + 1) * CHUNK].astype(BF16), vnbs[h],
                                                 preferred_element_type=F32)
            last = (c + 1) * CHUNK - 1
            states = [states[h] * jnp.exp(gc_ref[last:last + 1, h:h + 1])
                      + lax.dot_general(kd_ref[rs, hs].astype(BF16), vnbs[h], _TN, preferred_element_type=F32)
                      for h, hs in enumerate(hss)]
        for hs, s in zip(hss, states):
            s_ref[hs, :] = s

    def blk(wd):
        return pl.BlockSpec((rows, wd), lambda n: (n, 0))

    big = jax.ShapeDtypeStruct((t, C_VW), F32)
    return pl.pallas_call(
        body, name="delta_scan", grid=(nc // SCAN_SUB,),
        in_specs=[blk(C_VW)] * 4 + [blk(C_V_HEADS * CHUNK), blk(LANES)],
        out_specs=[blk(C_VW), blk(C_VW), pl.BlockSpec((SCAN_SUB, C_VW, C_DK), lambda n: (n, 0, 0))],
        out_shape=[big, jax.ShapeDtypeStruct((t, C_VW), BF16), jax.ShapeDtypeStruct((nc, C_VW, C_DK), F32)],
        scratch_shapes=[pltpu.VMEM((C_VW, C_DK), F32)],
        compiler_params=_cparams(("arbitrary",)),
    )(u, w, qd, kd, attn, gc)


def _delta_scan_bwd_call(do, w, qd, kd, attn, gc, vn, st):
    t = do.shape[0]
    nc = t // CHUNK
    rows = SCAN_SUB * CHUNK
    steps = nc // SCAN_SUB

    def body(do_ref, w_ref, qd_ref, kd_ref, attn_ref, gc_ref, vn_ref, st_ref,
             du_ref, dw_ref, dqd_ref, dkd_ref, dattn_ref, dgl_ref, ds_ref):
        @pl.when(pl.program_id(0) == 0)
        def _():
            ds_ref[...] = jnp.zeros_like(ds_ref)

        tri, _ = _tri_masks()
        row = lax.broadcasted_iota(jnp.int32, (8, LANES), 0)
        lane = lax.broadcasted_iota(jnp.int32, (8, LANES), 1)
        hss = [slice(h * C_DK, (h + 1) * C_DK) for h in range(C_V_HEADS)]
        css = [slice(h * CHUNK, (h + 1) * CHUNK) for h in range(C_V_HEADS)]

        def dg(a, b, dims):
            return lax.dot_general(a, b, dims, preferred_element_type=F32)

        dsps = [ds_ref[hs, :] for hs in hss]
        for c in reversed(range(SCAN_SUB)):
            rs = slice(c * CHUNK, (c + 1) * CHUNK)
            dgl = jnp.zeros((8, LANES), F32)
            ss = [st_ref[c, hs, :] for hs in hss]
            sbs = [s.astype(BF16) for s in ss]
            dspbs = [d.astype(BF16) for d in dsps]
            dobs = [do_ref[rs, hs].astype(BF16) for hs in hss]
            vnbs = [vn_ref[rs, hs].astype(BF16) for hs in hss]
            dvns = [dg(attn_ref[rs, cs].astype(BF16), dob, _TN) + dg(kd_ref[rs, hs].astype(BF16), dspb, _NN)
                    for hs, cs, dob, dspb in zip(hss, css, dobs, dspbs)]
            for h, hs in enumerate(hss):
                dqd_ref[rs, hs] = dg(dobs[h], sbs[h], _NT)
                dkd_ref[rs, hs] = dg(vnbs[h], dspbs[h], _NT)
                dattn_ref[rs, css[h]] = jnp.where(tri, dg(dobs[h], vnbs[h], _NT), 0.0)
            dvnbs = [d.astype(BF16) for d in dvns]
            for h, hs in enumerate(hss):
                du_ref[rs, hs] = dvns[h]
                dw_ref[rs, hs] = -dg(dvnbs[h], sbs[h], _NT)
                tot = jnp.sum(jnp.sum(dsps[h] * ss[h], axis=0, keepdims=True), axis=1, keepdims=True)
                dgl = jnp.where(jnp.logical_and(row == 0, lane == h), tot, dgl)
            dgl_ref[c * 8:(c + 1) * 8, :] = dgl
            last = (c + 1) * CHUNK - 1
            dsps = [dg(qd_ref[rs, hs].astype(BF16), dobs[h], _TN) + jnp.exp(gc_ref[last:last + 1, h:h + 1]) * dsps[h]
                    - dg(w_ref[rs, hs].astype(BF16), dvnbs[h], _TN) for h, hs in enumerate(hss)]
        for hs, d in zip(hss, dsps):
            ds_ref[hs, :] = d

    def blk(wd):
        return pl.BlockSpec((rows, wd), lambda n: (steps - 1 - n, 0))

    big = jax.ShapeDtypeStruct((t, C_VW), F32)
    return pl.pallas_call(
        body, name="delta_scan_bwd", grid=(steps,),
        in_specs=[blk(C_VW)] * 4 + [blk(C_V_HEADS * CHUNK), blk(LANES), blk(C_VW),
                                    pl.BlockSpec((SCAN_SUB, C_VW, C_DK), lambda n: (steps - 1 - n, 0, 0))],
        out_specs=[blk(C_VW)] * 4 + [blk(C_V_HEADS * CHUNK),
                                     pl.BlockSpec((SCAN_SUB * 8, LANES), lambda n: (steps - 1 - n, 0))],
        out_shape=[big] * 4 + [jax.ShapeDtypeStruct((t, C_V_HEADS * CHUNK), F32),
                               jax.ShapeDtypeStruct((nc * 8, LANES), F32)],
        scratch_shapes=[pltpu.VMEM((C_VW, C_DK), F32)],
        compiler_params=_cparams(("arbitrary",)),
    )(do, w, qd, kd, attn, gc, vn, st)


PREP_SUB = 4


def _delta_prep_bwd_call(qk, v, proj, alog, dtb, tmat, u, w, gc, beta, du, dw, dqd, dkd, dattn, dgl, du_buf):
    t = qk.shape[0]
    extra, extra_specs, aliases = _du_operands(du_buf, 17)
    nc = t // CHUNK
    rows = PREP_SUB * CHUNK
    scale = C_DK ** -0.5

    def body(q_ref, k_ref, v_ref, cab_ref, alog_ref, dtb_ref, tmat_ref, u_ref, w_ref, gc_ref, beta_ref,
             du_ref, dw_ref, dqd_ref, dkd_ref, dattn_ref, dgl_ref, *outs):
        dcab_ref, dqk_ref, dv_ref, dpar_ref = outs[len(extra):]
        tri, strict = _tri_masks()
        lane = lax.broadcasted_iota(jnp.int32, (CHUNK, LANES), 1)
        rowi = lax.broadcasted_iota(jnp.int32, (CHUNK, 1), 0)
        subs = range(PREP_SUB)
        rss = [slice(c * CHUNK, (c + 1) * CHUNK) for c in subs]
        betas = [beta_ref[rs, :] for rs in rss]

        def dot(x, y, dims=_NN):
            return lax.dot_general(x, y, dims, preferred_element_type=F32)

        heads = []
        for c, rs in zip(subs, rss):
            gc = gc_ref[rs, :]
            gct = gc.T
            for j in range(C_QK_HEADS):
                js = slice(j * C_DK, (j + 1) * C_DK)
                kf, qf = k_ref[rs, js], q_ref[rs, js] * scale
                kb, qb = kf.astype(BF16), qf.astype(BF16)
                kk = dot(kb, kb, _NT)
                qk = dot(qb, kb, _NT)
                for h in (2 * j, 2 * j + 1):
                    heads.append((c, rs, h, kf, qf, kb, qb) + _chunk_common(kk, qk, gc, gct, betas[c], h, tri, strict))

        def cols(h):
            return slice(h * C_DK, (h + 1) * C_DK)

        def sq(h):
            return slice(h * CHUNK, (h + 1) * CHUNK)

        dvks = [_dot_hi(tmat_ref[hd[1], sq(hd[2])],
                        jnp.concatenate([du_ref[hd[1], cols(hd[2])], dw_ref[hd[1], cols(hd[2])]], axis=1))
                for hd in heads]
        das = [-jnp.where(strict, _dot_hi(dvk, jnp.concatenate([u_ref[hd[1], cols(hd[2])], w_ref[hd[1], cols(hd[2])]],
                                                               axis=1), _NT), 0.0)
               for hd, dvk in zip(heads, dvks)]
        pre = []
        for (c, rs, h, kf, qf, kb, qb, gcol, bcol, decay, kkd, attn, glast), da in zip(heads, das):
            dattn_h = dattn_ref[rs, sq(h)]
            pre.append(((da * decay * bcol).astype(BF16), (dattn_h * decay).astype(BF16),
                        da * kkd * bcol + dattn_h * attn))
        mms = [(dot(dkk, hd[5]), dot(dkk, hd[5], _TN), dot(dqk, hd[6], _TN), dot(dqk, hd[5]))
               for hd, (dkk, dqk, e) in zip(heads, pre)]
        dq_parts, dk_parts = {}, {}
        dgc_tiles = [jnp.zeros((CHUNK, LANES), F32) for _ in subs]
        db_tiles = [jnp.zeros((CHUNK, LANES), F32) for _ in subs]
        later_tiles = [jnp.zeros((CHUNK, LANES), F32) for _ in subs]
        upper = jnp.logical_not(strict)
        for (c, rs, h, kf, qf, kb, qb, gcol, bcol, decay, kkd, attn, glast), dvk, da, (_, _, e), mm in zip(
                heads, dvks, das, pre, mms):
            hs = cols(h)
            gexp = jnp.exp(gcol)
            fdec = jnp.exp(glast - gcol)
            dvb, dkb = dvk[:, :C_DK], dvk[:, C_DK:]
            dgc = jnp.sum(e, axis=1, keepdims=True)
            later = jnp.sum(jnp.where(upper, jnp.sum(e, axis=0, keepdims=True), 0.0), axis=1, keepdims=True)
            later_tiles[c] = jnp.where(lane == h, later, later_tiles[c])
            dk_parts[c, h] = mm[0] + mm[1] + mm[2] + dkb * (bcol * gexp) + dkd_ref[rs, hs] * fdec
            dq_parts[c, h] = mm[3] + dqd_ref[rs, hs] * gexp
            dv_ref[rs, hs] = dvb * bcol
            s_kb = jnp.sum(dkb * kf, axis=1, keepdims=True)
            db = (jnp.sum(da * kkd, axis=1, keepdims=True) + jnp.sum(dvb * v_ref[rs, hs], axis=1, keepdims=True)
                  + s_kb * gexp)
            rho = jnp.sum(dkd_ref[rs, hs] * kf, axis=1, keepdims=True) * fdec
            dgc = (dgc + s_kb * bcol * gexp + jnp.sum(dqd_ref[rs, hs] * qf, axis=1, keepdims=True) * gexp - rho)
            last = jnp.sum(rho, axis=0, keepdims=True) + dgl_ref[c * 8:c * 8 + 1, h:h + 1] * jnp.exp(glast)
            dgc = dgc + jnp.where(rowi == CHUNK - 1, last, 0.0)
            dgc_tiles[c] = jnp.where(lane == h, dgc, dgc_tiles[c])
            db_tiles[c] = jnp.where(lane == h, db, db_tiles[c])
        alog = alog_ref[...]
        row8 = lax.broadcasted_iota(jnp.int32, (8, LANES), 0)
        par = jnp.zeros((8, LANES), F32)
        for c, rs in zip(subs, rss):
            for j in range(C_QK_HEADS):
                dqk_ref[rs, j * C_DK:(j + 1) * C_DK] = (dq_parts[c, 2 * j] + dq_parts[c, 2 * j + 1]) * scale
                dqk_ref[rs, C_QKW + j * C_DK:C_QKW + (j + 1) * C_DK] = dk_parts[c, 2 * j] + dk_parts[c, 2 * j + 1]
            dg = _dot_mask(upper, dgc_tiles[c]) - later_tiles[c]
            g, _, gate_pre = _gate_tiles(cab_ref[rs, :], alog, dtb_ref[...])
            dca = dg * (-jnp.exp(alog)) * jax.nn.sigmoid(gate_pre)
            beta = betas[c]
            dcab_ref[rs, :LANES] = (dca + pltpu.roll(db_tiles[c] * beta * (1.0 - beta), C_V_HEADS, 1)).astype(BF16)
            dcab_ref[rs, LANES:] = jnp.zeros((CHUNK, D_IN_PAD - COL["c_ab"] - LANES), BF16)
            par = par + jnp.where(row8 == 0, jnp.sum(dg * g, axis=0, keepdims=True),
                                  jnp.where(row8 == 1, jnp.sum(dca, axis=0, keepdims=True), 0.0))

        @pl.when(pl.program_id(0) == 0)
        def _():
            dpar_ref[...] = par

        @pl.when(pl.program_id(0) > 0)
        def _():
            dpar_ref[...] += par

    def blk(wd):
        return pl.BlockSpec((rows, wd), lambda n: (n, 0))

    row = pl.BlockSpec((1, LANES), lambda n: (0, 0))
    sqs = blk(C_V_HEADS * CHUNK)
    tail = D_IN_PAD - COL["c_ab"]
    assert COL["c_ab"] % tail == 0 and nc % PREP_SUB == 0
    return pl.pallas_call(
        body, name="delta_prep_bwd", grid=(nc // PREP_SUB,),
        in_specs=[blk(C_QKW), pl.BlockSpec((rows, C_QKW), lambda n: (n, 1)), blk(C_VW),
                  pl.BlockSpec((rows, LANES), lambda n: (n, COL["c_ab"] // LANES)), row, row, sqs,
                  blk(C_VW), blk(C_VW),
                  blk(LANES), blk(LANES), blk(C_VW), blk(C_VW), blk(C_VW), blk(C_VW), sqs,
                  pl.BlockSpec((PREP_SUB * 8, LANES), lambda n: (n, 0))] + extra_specs,
        out_specs=[pl.BlockSpec((rows, tail), lambda n: (n, COL["c_ab"] // tail)),
                   blk(2 * C_QKW), blk(C_VW), pl.BlockSpec((8, LANES), lambda n: (0, 0))],
        out_shape=[_du_shape(t), jax.ShapeDtypeStruct((t, 2 * C_QKW), F32),
                   jax.ShapeDtypeStruct((t, C_VW), F32), jax.ShapeDtypeStruct((8, LANES), F32)],
        input_output_aliases=aliases,
        compiler_params=_cparams(("arbitrary",)),
    )(qk, qk, v, proj, alog, dtb, tmat, u, w, gc, beta, du, dw, dqd, dkd, dattn, dgl, *extra)


def _z_spec(tq):
    return pl.BlockSpec((tq, C_VW), lambda i: (i, COL["c_z"] // C_VW))


def _gated_norm_fwd_call(o, u, gain, *, tq=512):
    t, w = o.shape

    def body(o_ref, z_ref, g_ref, y_ref):
        act, _ = _silu_parts(z_ref[...])
        gv = g_ref[...]
        for h in range(C_V_HEADS):
            hs = slice(h * C_DK, (h + 1) * C_DK)
            ov = o_ref[:, hs]
            r = lax.rsqrt(jnp.mean(ov * ov, axis=1, keepdims=True) + EPS)
            y_ref[:, hs] = ov * r * gv * act[:, hs]

    blk = pl.BlockSpec((tq, w), lambda i: (i, 0))
    return pl.pallas_call(
        body, name="gated_norm_fwd", grid=(t // tq,),
        in_specs=[blk, _z_spec(tq), pl.BlockSpec((1, C_DK), lambda i: (0, 0))], out_specs=blk,
        out_shape=jax.ShapeDtypeStruct((t, w), F32),
        compiler_params=_cparams(("parallel",)),
    )(o, u, gain)


def _gated_norm_bwd_call(o, u, gain, dy, du_buf, *, tq=512):
    t, w = o.shape
    nt = t // tq
    extra, extra_specs, aliases = _du_operands(du_buf, 4)

    def body(o_ref, z_ref, g_ref, dy_ref, *refs):
        dz_ref, do_ref, dg_ref, acc_ref = refs[len(extra):]
        i = pl.program_id(0)
        act, dact = _silu_parts(z_ref[...])
        gv = g_ref[...]
        part = jnp.zeros((8, C_DK), F32)
        for h in range(C_V_HEADS):
            hs = slice(h * C_DK, (h + 1) * C_DK)
            ov = o_ref[:, hs]
            r = lax.rsqrt(jnp.mean(ov * ov, axis=1, keepdims=True) + EPS)
            xh = ov * r
            dyv = dy_ref[:, hs]
            dn = dyv * act[:, hs]
            dz_ref[:, hs] = (dyv * xh * gv * dact[:, hs]).astype(BF16)
            dxh = dn * gv
            do_ref[:, hs] = r * (dxh - xh * jnp.mean(dxh * xh, axis=1, keepdims=True))
            part = part + jnp.sum((dn * xh).reshape(tq // 8, 8, C_DK), axis=0)

        @pl.when(i == 0)
        def _():
            acc_ref[...] = part

        @pl.when(i > 0)
        def _():
            acc_ref[...] += part

        @pl.when(i == nt - 1)
        def _():
            dg_ref[...] = jnp.sum(acc_ref[...], axis=0, keepdims=True)

    blk = pl.BlockSpec((tq, w), lambda i: (i, 0))
    grow = pl.BlockSpec((1, C_DK), lambda i: (0, 0))
    return pl.pallas_call(
        body, name="gated_norm_bwd", grid=(nt,),
        in_specs=[blk, _z_spec(tq), grow, blk] + extra_specs, out_specs=[_z_spec(tq), blk, grow],
        out_shape=[_du_shape(t), jax.ShapeDtypeStruct((t, w), F32), jax.ShapeDtypeStruct((1, C_DK), F32)],
        scratch_shapes=[pltpu.VMEM((8, C_DK), F32)],
        input_output_aliases=aliases,
        compiler_params=_cparams(("arbitrary",)),
    )(o, u, gain, dy, *extra)


def _gate_specs(tq):
    return [pl.BlockSpec((tq, D_MODEL), lambda i, j=j: (i, j)) for j in range(3)]


def _merge_fwd_call(ps, u, *, tq=512):
    t, w = ps[0].shape

    def body(p0, p1, p2, g0, g1, g2, y_ref):
        y_ref[...] = (jax.nn.sigmoid(g0[...]) * p0[...] + jax.nn.sigmoid(g1[...]) * p1[...]
                      + jax.nn.sigmoid(g2[...]) * p2[...]).astype(BF16)

    blk = pl.BlockSpec((tq, w), lambda i: (i, 0))
    return pl.pallas_call(
        body, name="merge_fwd", grid=(t // tq,), in_specs=[blk] * 3 + _gate_specs(tq), out_specs=blk,
        out_shape=jax.ShapeDtypeStruct((t, w), BF16),
        compiler_params=_cparams(("parallel",)),
    )(*ps, u, u, u)


def _merge_bwd_call(ps, u, dy, *, tq=512):
    t, w = dy.shape

    def body(p0, p1, p2, g0, g1, g2, dy_ref, dg_ref, dp0, dp1, dp2):
        dyv = dy_ref[...]
        for j, (p, g, dp) in enumerate(((p0, g0, dp0), (p1, g1, dp1), (p2, g2, dp2))):
            sig = jax.nn.sigmoid(g[...])
            dp[...] = (dyv * sig).astype(BF16)
            dg_ref[:, j * w:(j + 1) * w] = (dyv * p[...] * sig * (1.0 - sig)).astype(BF16)

    blk = pl.BlockSpec((tq, w), lambda i: (i, 0))
    small = jax.ShapeDtypeStruct((t, w), BF16)
    return pl.pallas_call(
        body, name="merge_bwd", grid=(t // tq,), in_specs=[blk] * 3 + _gate_specs(tq) + [blk],
        out_specs=[pl.BlockSpec((tq, 3 * w), lambda i: (i, 0))] + [blk] * 3,
        out_shape=[_du_shape(t)] + [small] * 3,
        compiler_params=_cparams(("parallel",)),
    )(*ps, u, u, u, dy)


Q_SCALE = HEAD_DIM ** -0.5
A_PARTS = ((COL["a_q"], A_W, True, Q_SCALE), (COL["a_k"], A_W, True, 1.0), (COL["a_v"], A_W, False, 1.0))
B_PARTS = ((COL["b_q"], A_W, True, Q_SCALE), (COL["b_k"], B_KVW, True, 1.0), (COL["b_v"], B_KVW, False, 1.0))
BRANCHES = ("w_branch_a", "w_branch_b", "w_branch_c")


def _layer_fwd(x, tabs, p, w_in_b, rest, rest_of, gather_src=None):
    h = _rms_fwd_call(x, p["norm_mix"], name="rms_mix_fwd", out_dtype=BF16)
    if rest[0] == "ride":
        u, (packed,) = _mm(h, w_in_b, bias=p["b_in"], tn=IN_TN, gather_src=[rest[1]], name="in_proj_fwd_gather")
    else:
        u, packed = _mm(h, w_in_b, bias=p["b_in"], tn=IN_TN, name="in_proj_fwd"), rest[1]
    wb, conv_w = rest_of(packed)
    wb = dict(wb, w_in=w_in_b)
    p = dict(p, conv_w=conv_w)
    qkv_a = _rope_gather_call(u, tabs, A_PARTS, dils=A_DILS, name="rope_a_fwd")
    os_, lses = zip(*[_attn_fwd(cfg, qkv2, None) for cfg, qkv2 in zip(ATTN_A_CFGS, qkv_a)])
    ya = _combine_fwd_call(os_, lses, A_DILS)
    qkv_b, = _rope_gather_call(u, tabs, B_PARTS, name="rope_b_fwd")
    yb, lse_b = _attn_fwd(ATTN_B_CFG, qkv_b, p["sinks"])
    zc, qk, v = _conv_prep_fwd_call(u, p["conv_w"])
    uu, ww, qd, kd, attn, tmat, gc, beta, *gathered = _delta_prep_call(qk, v, u, p["a_log"], p["dt_bias"], gather_src)
    o, vn, st = _delta_scan_call(uu, ww, qd, kd, attn, gc)
    yc = _gated_norm_fwd_call(o, u, p["c_norm"])
    ys = (ya, yb, yc)
    ps = tuple(_mm(y, wb[n], name="branch_fwd") for y, n in zip(ys, BRANCHES))
    merged = _merge_fwd_call(ps, u)
    x1 = _mm(merged, wb["w_out"], add=x, name="out_proj_fwd")
    h2 = _rms_fwd_call(x1, p["norm_ffn"], name="rms_ffn_fwd", out_dtype=BF16)
    pre, act = _mm(h2, wb["w_ff1"], relu2_out=True, name="ffn_up")
    x2 = _mm(act, wb["w_ff2"], add=x1, name="ffn_down")
    saved = dict(x=x, h=h, u=u, qkv_a=qkv_a, os_=os_, lses=lses, b_saved=(qkv_b, yb, lse_b),
                 zc=zc, qk=qk, v=v, delta=(tmat, uu, ww, gc, beta, qd, kd, attn, vn, st), o=o, ys=ys, ps=ps,
                 merged=merged, x1=x1, h2=h2, pre=pre, act=act, p=p, wb=wb)
    return x2, saved, (gathered if gathered else None)


def _layer_bwd(s, dx2, tabs):
    g, p, wb = {}, s["p"], s["wb"]
    t = dx2.shape[0]
    dpre = _mm(dx2, wb["w_ff2"], tb=True, mul_drelu2=s["pre"], out_dtype=BF16, name="ffn_dpre")
    g["w_ff2"] = _mm(s["act"], dx2, ta=True, tk=1024, name="ffn_dw2")
    g["w_ff1"] = _mm(s["h2"], dpre, ta=True, tk=1024, name="ffn_dw1")
    dh2 = _mm(dpre, wb["w_ff1"], tb=True, name="ffn_dh")
    dx1, g["norm_ffn"] = _rms_bwd_call(s["x1"], p["norm_ffn"], dh2, add=dx2, name="rms_ffn_bwd")
    dmerged = _mm(dx1, wb["w_out"], tb=True, name="out_proj_da")
    g["w_out"] = _mm(s["merged"], dx1, ta=True, tk=1024, name="out_proj_dw")
    du, *dps = _merge_bwd_call(s["ps"], s["u"], dmerged)
    dys = []
    for y, dp, n in zip(s["ys"], dps, BRANCHES):
        dys.append(_mm(dp, wb[n], tb=True, name="branch_da"))
        g[n] = _mm(y, dp, ta=True, tk=1024, name="branch_dw")
    dya, dyb, dyc = dys
    tmat, uu, ww, gc, beta, qd, kd, attn, vn, st = s["delta"]
    du, do, g["c_norm"] = _gated_norm_bwd_call(s["o"], s["u"], p["c_norm"], dyc, du)
    ddu, ddw, dqd, dkd, dattn, dgl = _delta_scan_bwd_call(do, ww, qd, kd, attn, gc, vn, st)
    du, dqk, dv, dpar = _delta_prep_bwd_call(s["qk"], s["v"], s["u"], p["a_log"], p["dt_bias"], tmat, uu, ww, gc,
                                             beta, ddu, ddw, dqd, dkd, dattn, dgl, du)
    g["a_log"], g["dt_bias"] = dpar[0:1], dpar[1:2]
    dzc = _conv_prep_dz_call(s["zc"], dqk, dv)
    du, dconv = _conv_bwd_call(s["u"], dzc, p["conv_w"], du)
    g["conv_w"] = dconv[:C_CONV]
    no_dlse = jnp.zeros((t, LANES), F32)
    dq, dk, dv_b, dsink = _attn_bwd(ATTN_B_CFG, *s["b_saved"], p["sinks"], dyb, no_dlse)
    g["sinks"] = dsink[0, :p["sinks"].shape[0]]
    du = _rope_scatter_call(du, t, [([dq], A_W, True, Q_SCALE)], COL["b_q"], tabs, name="rope_bq_bwd")
    du = _rope_scatter_call(du, t, [([dk], B_KVW, True, 1.0), ([dv_b], B_KVW, False, 1.0)], COL["b_k"], tabs,
                            name="rope_bkv_bwd")
    *dos, dl0, dl1, dl2 = _combine_bwd_call(s["os_"], s["lses"], dya, A_DILS)
    grads_a = [_attn_bwd(cfg, qkv2, o2, lse2, None, do2, dl2_)[:3]
               for cfg, qkv2, o2, lse2, do2, dl2_ in zip(ATTN_A_CFGS, s["qkv_a"], s["os_"], s["lses"], dos,
                                                         (dl0, dl1, dl2))]
    dqs, dks, dvs = zip(*grads_a)
    du = _rope_scatter_call(du, t, [(list(dqs), A_W, True, Q_SCALE), (list(dks), A_W, True, 1.0),
                                    (list(dvs), A_W, False, 1.0)],
                            COL["a_q"], tabs, dils=A_DILS, name="rope_a_bwd")
    dh = _mm(du, wb["w_in"], tb=True, tk=IN_TN, name="in_proj_da")
    g["w_in"], g["b_in"] = _mm(s["h"], du, ta=True, b_colsum=True, tn=IN_TN, tk=1024, name="in_proj_dw")
    dx, g["norm_mix"] = _rms_bwd_call(s["x"], p["norm_mix"], dh, add=dx1, name="rms_mix_bwd")
    return dx, g


def _local_step(x, params, w_in_first, rest_first, payload_of_layer, w_in_of, rest_of, tabs, tgt):
    saves = []
    w_in_blocks, rest = w_in_first, rest_first
    for layer in range(DEPTH):
        p = {n: w[layer] for n, w in params.items() if n != "norm_final"}
        nxt = payload_of_layer(layer + 1) if layer + 1 < DEPTH else None
        x, s, gathered = _layer_fwd(x, tabs, p, w_in_of(w_in_blocks), rest, rest_of, nxt)
        saves.append(s)
        if gathered is not None:
            w_in_blocks, rest = gathered[0], ("ready", gathered[1])
    loss, dx, dfinal = _loss_call(x, params["norm_final"], tgt)
    per_layer = []
    for s in reversed(saves):
        dx, g = _layer_bwd(s, dx, tabs)
        per_layer.append(g)
    per_layer.reverse()
    grads = {n: jnp.stack([g[n] for g in per_layer]) for n in per_layer[0]}
    grads["norm_final"] = dfinal
    return loss, dx, grads


def _in_cols_to_kernel(w):
    lead = w.shape[:-1]
    parts, pos = [], 0
    for _, start, width, ref_start in IN_LAYOUT:
        if start > pos:
            parts.append(jnp.zeros(lead + (start - pos,), w.dtype))
        parts.append(w[..., ref_start:ref_start + width])
        pos = start + width
    parts.append(jnp.zeros(lead + (D_IN_PAD - pos,), w.dtype))
    return jnp.concatenate(parts, axis=-1)


def _in_cols_to_reference(w):
    by_ref = sorted(IN_LAYOUT, key=lambda e: e[3])
    return jnp.concatenate([w[..., start:start + width] for _, start, width, _ in by_ref], axis=-1)


W_IN_SHARD = 8464 // N_DEV


def _w_in_from_shards(blocks):
    lead = blocks.shape[1:-1]
    parts, pos = [], 0
    for _, start, width, ref_start in IN_LAYOUT:
        if start > pos:
            parts.append(jnp.zeros(lead + (start - pos,), blocks.dtype))
        col = ref_start
        while col < ref_start + width:
            d, l = divmod(col, W_IN_SHARD)
            n = min(W_IN_SHARD - l, ref_start + width - col)
            parts.append(blocks[d, ..., l:l + n])
            col += n
        pos = start + width
    parts.append(jnp.zeros(lead + (D_IN_PAD - pos,), blocks.dtype))
    return jnp.concatenate(parts, axis=-1)


def _w_in_to_shards(g):
    by_ref = sorted(IN_LAYOUT, key=lambda e: e[3])
    blocks = []
    for d in range(N_DEV):
        lo, hi = d * W_IN_SHARD, (d + 1) * W_IN_SHARD
        parts = []
        for _, start, width, ref_start in by_ref:
            a, b = max(lo, ref_start), min(hi, ref_start + width)
            if a < b:
                parts.append(g[..., start + a - ref_start:start + b - ref_start])
        blocks.append(jnp.concatenate(parts, axis=-1))
    return jnp.stack(blocks)


def _pad_lanes(v):
    return jnp.pad(v, ((0, 0), (0, LANES - v.shape[1])))[:, None, :]


BIG = (("w_in", 2), ("conv_w", 2), ("w_branch_a", 2), ("w_branch_b", 2), ("w_branch_c", 1), ("w_out", 1),
       ("w_ff1", 2), ("w_ff2", 1))
SMALL = ("norm_mix", "b_in", "a_log", "dt_bias", "sinks", "c_norm", "norm_ffn", "norm_final")
WEIGHTS = ("norm_mix", "w_in", "b_in", "conv_w", "a_log", "dt_bias", "sinks", "c_norm", "w_branch_a",
           "w_branch_b", "w_branch_c", "w_out", "norm_ffn", "w_ff1", "w_ff2", "norm_final")
MATMUL_WEIGHTS = ("w_in", "w_branch_a", "w_branch_b", "w_branch_c", "w_out", "w_ff1", "w_ff2")
PACK_ROWS = 1024
ROW_ALIGN = 16


def _seg_rows(n):
    return -(-n // (LANES * ROW_ALIGN)) * ROW_ALIGN


def _pack(arrays, lead=0):
    parts = []
    for a in arrays:
        lead_shape = a.shape[:lead]
        n = math.prod(a.shape[lead:])
        rows = _seg_rows(n)
        if rows * LANES != n:
            a = jnp.pad(a.reshape(lead_shape + (n,)), [(0, 0)] * lead + [(0, rows * LANES - n)])
        parts.append(a.reshape(lead_shape + (rows, LANES)))
    total = sum(p.shape[lead] for p in parts)
    padded = -(-total // PACK_ROWS) * PACK_ROWS
    if padded > total:
        parts.append(jnp.zeros(parts[0].shape[:lead] + (padded - total, LANES), parts[0].dtype))
    return jnp.concatenate(parts, axis=lead)


def _unpack(buf, shapes):
    lead = buf.shape[:-2]
    out, pos = [], 0
    for shp in shapes:
        n = math.prod(shp)
        rows = _seg_rows(n)
        seg = buf[..., pos:pos + rows, :]
        if rows * LANES != n:
            seg = seg.reshape(lead + (rows * LANES,))[..., :n]
        out.append(seg.reshape(lead + tuple(shp)))
        pos += rows
    return out


def _shards_to_full(blocks, axis):
    moved = jnp.moveaxis(blocks, 0, axis)
    shp = list(blocks.shape[1:])
    shp[axis] = shp[axis] * N_DEV
    return moved.reshape(shp)


def _full_to_shards(full, axis):
    shp = list(full.shape)
    shp[axis:axis + 1] = [N_DEV, shp[axis] // N_DEV]
    return jnp.moveaxis(full.reshape(shp), axis, 0)


def _my_place():
    return lax.axis_index("x"), lax.axis_index("y"), lax.axis_index("c")


def _slot(x, y, c):
    return 4 * x + 2 * y + c


GATHER_COPIES = 7


def _gather_plan(x_ref, out_ref, send_sems, recv_sems, local_sem, base):
    x, y, c = _my_place()
    me, sibling = (x, y, c), (x, y, 1 - c)
    chips = [(1 - x, y), (x, 1 - y), (1 - x, 1 - y)]

    def copy(k, blk, to, src=None):
        dst = out_ref.at[_slot(*blk)]
        return pltpu.make_async_remote_copy(
            src_ref=dst if src is None else src, dst_ref=dst,
            send_sem=send_sems.at[base + k], recv_sem=recv_sems.at[base + k], device_id=to, device_id_type=MESH_ID)

    def own():
        mine = pltpu.make_async_copy(x_ref, out_ref.at[_slot(*me)], local_sem)
        return mine, [copy(0, me, sibling, src=x_ref)] + [copy(1 + j, me, (*chip, c), src=x_ref)
                                                          for j, chip in enumerate(chips)]

    return copy, own, me, sibling, chips, c


def _gather_plans(srcs, outs, send_sems, recv_sems, local_sems):
    return [_gather_plan(x_ref, out_ref, send_sems, recv_sems, local_sems.at[i], GATHER_COPIES * i)
            for i, (x_ref, out_ref) in enumerate(zip(srcs, outs))]


def _gather_start(srcs, outs, *sems):
    for _, own, *_ in _gather_plans(srcs, outs, *sems):
        mine, first = own()
        mine.start()
        for cp in first:
            cp.start()


def _gather_finish(srcs, outs, *sems):
    plans = _gather_plans(srcs, outs, *sems)
    passed_all = []
    for copy, own, me, sibling, chips, c in plans:
        passed = [copy(4 + j, (*chip, c), sibling) for j, chip in enumerate(chips)]
        for j, chip in enumerate(chips):
            copy(1 + j, (*chip, c), me).wait_recv()
            passed[j].start()
        passed_all.append(passed)
    for (copy, own, me, sibling, chips, c), passed in zip(plans, passed_all):
        copy(0, sibling, me).wait_recv()
        for j, chip in enumerate(chips):
            copy(4 + j, (*chip, 1 - c), me).wait_recv()
        mine, first = own()
        for cp in first + passed:
            cp.wait_send()
        mine.wait()


def _gather_sems(n):
    return [pltpu.SemaphoreType.DMA((GATHER_COPIES * n,)), pltpu.SemaphoreType.DMA((GATHER_COPIES * n,)),
            pltpu.SemaphoreType.DMA((n,))]


def _gathered_shapes(blocks):
    return [jax.ShapeDtypeStruct((N_DEV,) + b.shape, b.dtype) for b in blocks]


def _all_gather(blocks, *, name):
    n = len(blocks)

    def body(*refs):
        srcs, outs, sems = refs[:n], refs[n:2 * n], refs[2 * n:]
        _gather_start(srcs, outs, *sems)
        _gather_finish(srcs, outs, *sems)

    return pl.pallas_call(
        body, name=name, out_shape=_gathered_shapes(blocks),
        in_specs=[HBM_SPEC] * n, out_specs=[HBM_SPEC] * n,
        scratch_shapes=_gather_sems(n),
    )(*blocks)


N_CHIP = N_DEV // 2


def _swap_with_sibling(blocks, *, name):
    n = len(blocks)

    def body(*refs):
        srcs, outs, send_sems, recv_sems = refs[:n], refs[n:2 * n], refs[2 * n], refs[2 * n + 1]
        x, y, c = _my_place()
        copies = [pltpu.make_async_remote_copy(src_ref=g_ref, dst_ref=out_ref, send_sem=send_sems.at[i],
                                               recv_sem=recv_sems.at[i], device_id=(x, y, 1 - c),
                                               device_id_type=MESH_ID)
                  for i, (g_ref, out_ref) in enumerate(zip(srcs, outs))]
        for cp in copies:
            cp.start()
        for cp in copies:
            cp.wait_recv()
        for cp in copies:
            cp.wait_send()

    return pl.pallas_call(
        body, name=name,
        out_shape=[jax.ShapeDtypeStruct(b.shape, b.dtype) for b in blocks],
        in_specs=[HBM_SPEC] * n, out_specs=[HBM_SPEC] * n,
        scratch_shapes=[pltpu.SemaphoreType.DMA((n,)), pltpu.SemaphoreType.DMA((n,))],
    )(*blocks)


def _chip_all_to_all(blocks, *, name):
    n = len(blocks)
    peers = N_CHIP - 1

    def body(*refs):
        srcs, outs = refs[:n], refs[n:2 * n]
        send_sems, recv_sems, local_sems = refs[2 * n:]
        x, y, c = _my_place()
        mine_slot = 2 * x + y
        locals_, copies = [], []
        for i, (g_ref, out_ref) in enumerate(zip(srcs, outs)):
            locals_.append(pltpu.make_async_copy(g_ref.at[mine_slot], out_ref.at[mine_slot], local_sems.at[i]))
            for k in range(1, N_CHIP):
                px, py = x ^ (k >> 1), y ^ (k & 1)
                copies.append(pltpu.make_async_remote_copy(
                    src_ref=g_ref.at[2 * px + py], dst_ref=out_ref.at[mine_slot],
                    send_sem=send_sems.at[peers * i + k - 1], recv_sem=recv_sems.at[peers * i + k - 1],
                    device_id=(px, py, c), device_id_type=MESH_ID))
        for cp in locals_ + copies:
            cp.start()
        for cp in copies:
            cp.wait_recv()
        for cp in copies:
            cp.wait_send()
        for cp in locals_:
            cp.wait()

    return pl.pallas_call(
        body, name=name,
        out_shape=[jax.ShapeDtypeStruct(b.shape, b.dtype) for b in blocks],
        in_specs=[HBM_SPEC] * n, out_specs=[HBM_SPEC] * n,
        scratch_shapes=[pltpu.SemaphoreType.DMA((peers * n,)), pltpu.SemaphoreType.DMA((peers * n,)),
                        pltpu.SemaphoreType.DMA((n,))],
    )(*blocks)


def _block_rows(rows, cols):
    tr = max(8, min(rows, PACK_ROWS * LANES // (-(-cols // LANES) * LANES) // 8 * 8))
    while rows % tr:
        tr -= 8
    return tr


def _add_bf16_call(a, b, *, name, gather_src=()):
    n, rows, cols = a.shape
    tr = _block_rows(rows, cols)
    steps = rows // tr
    ng = len(gather_src)

    def body(a_ref, b_ref, *refs):
        o_ref = refs[ng]
        if ng:
            gather_refs = (refs[:ng], refs[ng + 1:2 * ng + 1]) + tuple(refs[2 * ng + 1:])

            @pl.when(pl.program_id(0) == 0)
            def _():
                _gather_start(*gather_refs)
        o_ref[...] = (a_ref[...].astype(F32) + b_ref[...].astype(F32)).astype(BF16)
        if ng:
            @pl.when(pl.program_id(0) == steps - 1)
            def _():
                _gather_finish(*gather_refs)

    blk = pl.BlockSpec((n, tr, cols), lambda i: (0, i, 0))
    return pl.pallas_call(
        body, name=name, grid=(steps,), in_specs=[blk, blk] + [HBM_SPEC] * ng, out_specs=[blk] + [HBM_SPEC] * ng,
        out_shape=[jax.ShapeDtypeStruct(a.shape, BF16)] + _gathered_shapes(gather_src),
        scratch_shapes=_gather_sems(ng) if ng else [],
        compiler_params=_cparams(("arbitrary",) if ng else ("parallel",)),
    )(a, b, *gather_src)


def _adamw_call(parts, w, m, v, *, name):
    rows, cols = w.shape
    tr = _block_rows(rows, cols)
    n_parts = parts.shape[0]

    def body(p_ref, w_ref, m_ref, v_ref, g_ref, d_ref, nm_ref, nv_ref):
        g = p_ref[0].astype(F32)
        for s in range(1, n_parts):
            g = g + p_ref[s].astype(F32)
        nm = ADAM_B1 * m_ref[...] + (1.0 - ADAM_B1) * g
        nv = ADAM_B2 * v_ref[...] + (1.0 - ADAM_B2) * jnp.square(g)
        m_hat = nm / (1.0 - ADAM_B1 ** ADAM_STEP)
        v_hat = nv / (1.0 - ADAM_B2 ** ADAM_STEP)
        g_ref[...] = g
        nm_ref[...] = nm
        nv_ref[...] = nv
        d_ref[...] = -ADAM_LR * (m_hat / (jnp.sqrt(v_hat) + ADAM_EPS) + ADAM_WD * w_ref[...])

    blk = pl.BlockSpec((tr, cols), lambda i: (i, 0))
    shape = jax.ShapeDtypeStruct((rows, cols), F32)
    return pl.pallas_call(
        body, name=name, grid=(rows // tr,),
        in_specs=[pl.BlockSpec((n_parts, tr, cols), lambda i: (0, i, 0)), blk, blk, blk],
        out_specs=[blk] * 4, out_shape=[shape] * 4,
        compiler_params=_cparams(("parallel",)),
    )(parts, w, m, v)


def _kernel_params(full):
    return {
        "norm_mix": full["norm_mix"][:, None, :],
        "b_in": _in_cols_to_kernel(full["b_in"])[:, None, :],
        "a_log": _pad_lanes(full["a_log"]),
        "dt_bias": _pad_lanes(full["dt_bias"]),
        "sinks": full["sinks"],
        "c_norm": full["c_norm"][:, None, :],
        "norm_ffn": full["norm_ffn"][:, None, :],
        "norm_final": full["norm_final"][None, :],
    }


def _reference_grads(g):
    return {
        "norm_mix": g["norm_mix"][:, 0, :],
        "b_in": _in_cols_to_reference(g["b_in"][:, 0, :]),
        "conv_w": g["conv_w"],
        "a_log": g["a_log"][:, 0, :C_V_HEADS],
        "dt_bias": g["dt_bias"][:, 0, :C_V_HEADS],
        "sinks": g["sinks"],
        "c_norm": g["c_norm"][:, 0, :],
        "w_branch_a": g["w_branch_a"], "w_branch_b": g["w_branch_b"], "w_branch_c": g["w_branch_c"],
        "w_out": g["w_out"],
        "norm_ffn": g["norm_ffn"][:, 0, :],
        "w_ff1": g["w_ff1"], "w_ff2": g["w_ff2"],
        "norm_final": g["norm_final"][0],
    }


def kernel(x, positions, norm_mix, w_in, b_in, conv_w, a_log, dt_bias, sinks, c_norm, w_branch_a, w_branch_b, w_branch_c, w_out, norm_ffn, w_ff1, w_ff2, norm_final, loss_target, m_norm_mix, m_w_in, m_b_in, m_conv_w, m_a_log, m_dt_bias, m_sinks, m_c_norm, m_w_branch_a, m_w_branch_b, m_w_branch_c, m_w_out, m_norm_ffn, m_w_ff1, m_w_ff2, m_norm_final, v_norm_mix, v_w_in, v_b_in, v_conv_w, v_a_log, v_dt_bias, v_sinks, v_c_norm, v_w_branch_a, v_w_branch_b, v_w_branch_c, v_w_out, v_norm_ffn, v_w_ff1, v_w_ff2, v_norm_final):
    env = dict(locals())
    weights = {n: env[n] for n in WEIGHTS}
    moments_m = {n: env["m_" + n] for n in WEIGHTS}
    moments_v = {n: env["v_" + n] for n in WEIGHTS}

    axis_of = {n: axis - 1 for n, axis in BIG}

    packed_names = [n for n in MATMUL_WEIGHTS if n != "w_in"]

    def payload_of_layer(layer):
        cw = weights["conv_w"][layer]
        c1 = cw.astype(BF16)
        c2 = (cw - c1.astype(F32)).astype(BF16)
        c3 = (cw - c1.astype(F32) - c2.astype(F32)).astype(BF16)
        return [weights["w_in"][layer].astype(BF16),
                _pack([weights[n][layer].astype(BF16) for n in packed_names] + [c1, c2, c3])]

    def rest_of(packed):
        shapes = [weights[n].shape[1:] for n in packed_names] + [weights["conv_w"].shape[1:]] * 3
        blocks = _unpack(packed, shapes)
        wb = {n: _shards_to_full(blk, axis_of[n]) for n, blk in zip(packed_names, blocks)}
        return wb, _shards_to_full(sum(b.astype(F32) for b in blocks[-3:]), axis_of["conv_w"])

    tabs = rope_tables(positions[0])
    w_in_first, = _all_gather(payload_of_layer(0)[:1], name="gather_weights")
    loss, dx, dparams = _local_step(x[0], _kernel_params({n: weights[n] for n in SMALL}), w_in_first,
                                    ("ride", payload_of_layer(0)[1]), payload_of_layer, _w_in_from_shards,
                                    rest_of, tabs, loss_target[0])
    grads = _reference_grads(dparams)
    loss = lax.psum(loss, ("x", "y", "c"))

    core = lax.axis_index("c")
    rest = [(n, axis) for n, axis in BIG if n != "w_in"]
    w_in_rows = DEPTH * D_MODEL

    def by_core(shards, which):
        sh = shards.reshape((N_CHIP, 2) + shards.shape[1:])
        return lax.dynamic_index_in_dim(sh, which, axis=1, keepdims=False).astype(BF16)

    def halves(which):
        w_in_half = by_core(_w_in_to_shards(dparams["w_in"]), which).reshape(N_CHIP, w_in_rows, W_IN_SHARD)
        return [w_in_half, _pack([by_core(_full_to_shards(grads[n], axis), which) for n, axis in rest], lead=1)]

    from_sibling = _swap_with_sibling(halves(1 - core), name="scatter_grads_d2d")
    keep_w_in, keep_rest = halves(core)
    w_in_sum, small_parts = _add_bf16_call(keep_w_in, from_sibling[0], gather_src=[_pack([grads[n] for n in SMALL])],
                                           name="scatter_grads_add_gather")
    rest_sum, = _add_bf16_call(keep_rest, from_sibling[1], name="scatter_grads_add")
    w_in_parts, rest_parts = _chip_all_to_all([w_in_sum, rest_sum], name="scatter_grads_ici")

    out = {}
    results = _adamw_call(w_in_parts, *[d["w_in"].reshape(w_in_rows, W_IN_SHARD)
                                        for d in (weights, moments_m, moments_v)], name="adamw_w_in")
    for kind, buf in zip(("grad", "delta", "new_m", "new_v"), results):
        out[kind, "w_in"] = buf.reshape(weights["w_in"].shape)
    for names, parts in (([n for n, _ in rest], rest_parts), (list(SMALL), small_parts)):
        shapes = [weights[n].shape for n in names]
        packed = [_pack([d[n] for n in names]) for d in (weights, moments_m, moments_v)]
        results = _adamw_call(parts, *packed, name="adamw_" + names[0])
        for kind, buf in zip(("grad", "delta", "new_m", "new_v"), results):
            for n, arr in zip(names, _unpack(buf, shapes)):
                out[kind, n] = arr
    return (loss, dx[None], *[out[kind, n] for kind in ("grad", "delta", "new_m", "new_v") for n in WEIGHTS])
```
